```python
import math
import jax, jax.numpy as jnp
from jax import lax
import numpy as np

D_MODEL = 1024
BATCH = 16
SEQ = 2048
DEPTH = 1

CHUNK = 64
RET_HEADS = 4
RET_KEY_DIM = 128
RET_VAL_DIM = 256
RET_QK = RET_HEADS * RET_KEY_DIM
RET_V = RET_HEADS * RET_VAL_DIM
ATT_HEADS = 8
ATT_HEAD_DIM = 64
ATT_W = ATT_HEADS * ATT_HEAD_DIM
BAND_CHUNKS = 8
BAND = (BAND_CHUNKS + 1) * CHUNK
MAX_REL = 256
N_REL = CHUNK + MAX_REL
D_FF = -(-8 * D_MODEL // (3 * 256)) * 256
ROPE_BASE = 10000.0
EPS = 1e-6
NEG_INF = -1e30
IN_SIZES = (RET_QK, RET_QK, RET_V, RET_V, ATT_W, ATT_W, ATT_W, 2 * D_MODEL)
IN_SPLITS = tuple(int(s) for s in np.cumsum(IN_SIZES)[:-1])
N_IN = int(sum(IN_SIZES))

kernel_name = "hybrid_retention_chunkattn_gated_block"


def rmsnorm(x, g):
    xf = x.astype(jnp.float32)
    y = xf * lax.rsqrt(jnp.mean(xf * xf, axis=-1, keepdims=True) + EPS)
    return (y * g.astype(jnp.float32)).astype(x.dtype)


def rotary(x, pos):
    d = x.shape[-1]
    freqs = ROPE_BASE ** (-jnp.arange(0, d, 2, dtype=jnp.float32) / d)
    ang = pos[:, None] * freqs[None, :]
    cos = jnp.cos(ang)[None, :, None, :].astype(x.dtype)
    sin = jnp.sin(ang)[None, :, None, :].astype(x.dtype)
    x1, x2 = x[..., : d // 2], x[..., d // 2:]
    return jnp.concatenate([x1 * cos - x2 * sin, x1 * sin + x2 * cos], axis=-1)


def retention(q, k, v):
    B, S, H, dk = q.shape
    dv = v.shape[-1]
    nc = S // CHUNK
    dt = q.dtype
    log_g = jnp.log(1.0 - 2.0 ** (-5.0 - jnp.arange(H, dtype=jnp.float32)))
    p = jnp.arange(CHUNK, dtype=jnp.float32)
    intra = jnp.exp(log_g[:, None, None] * jnp.abs(p[:, None] - p[None, :])).astype(dt)
    q_dec = jnp.exp(log_g[None, :] * (p[:, None] + 1.0)).astype(dt)[None, :, :, None]
    k_dec = jnp.exp(log_g[None, :] * (CHUNK - 1.0 - p[:, None])).astype(dt)[None, :, :, None]
    chunk_dec = jnp.exp(log_g * CHUNK).astype(dt)[None, :, None, None]
    k = k * jnp.asarray(RET_KEY_DIM ** -0.5, dt)

    def to_chunks(t):
        return t.reshape(B, nc, CHUNK, H, t.shape[-1]).transpose(1, 0, 2, 3, 4)

    def step(state, inp):
        qi, ki, vi = inp
        s = jnp.einsum('bnhd,bmhd->bhnm', qi, ki) * intra[None]
        o = jnp.einsum('bhnm,bmhe->bnhe', s, vi)
        o = o + jnp.einsum('bnhd,bhde->bnhe', qi * q_dec, state)
        new_state = state * chunk_dec + jnp.einsum('bmhd,bmhe->bhde', ki * k_dec, vi)
        return new_state, o

    state0 = jnp.zeros((B, H, dk, dv), dt)
    _, o = lax.scan(step, state0, (to_chunks(q), to_chunks(k), to_chunks(v)))
    return o.transpose(1, 0, 2, 3, 4).reshape(B, S, H, dv)


def head_groupnorm(o):
    of = o.astype(jnp.float32)
    mu = jnp.mean(of, axis=-1, keepdims=True)
    var = jnp.mean(jnp.square(of - mu), axis=-1, keepdims=True)
    return ((of - mu) * lax.rsqrt(var + EPS)).astype(o.dtype)


def chunk_band_attention(q, k, v, rel_bias):
    B, S, H, dh = q.shape
    nc = S // CHUNK
    pad = BAND_CHUNKS * CHUNK
    kp = jnp.pad(k, ((0, 0), (pad, 0), (0, 0), (0, 0)))
    vp = jnp.pad(v, ((0, 0), (pad, 0), (0, 0), (0, 0)))
    n = jnp.arange(CHUNK)
    j = jnp.arange(BAND)
    rel = (pad + n[:, None]) - j[None, :]
    idx = jnp.clip(rel, -(CHUNK - 1), MAX_REL) + (CHUNK - 1)
    bias = rel_bias.astype(jnp.float32)[:, idx]
    scale = ATT_HEAD_DIM ** -0.5
    qc = q.reshape(B, nc, CHUNK, H, dh).transpose(1, 0, 2, 3, 4)

    def one_chunk(args):
        i, qi = args
        ki = lax.dynamic_slice_in_dim(kp, i * CHUNK, BAND, axis=1)
        vi = lax.dynamic_slice_in_dim(vp, i * CHUNK, BAND, axis=1)
        s = jnp.einsum('bnhd,bmhd->bhnm', qi, ki).astype(jnp.float32) * scale + bias[None]
        valid = j >= (BAND_CHUNKS - i) * CHUNK
        s = jnp.where(valid[None, None, None, :], s, NEG_INF)
        pr = jax.nn.softmax(s, axis=-1).astype(vi.dtype)
        return jnp.einsum('bhnm,bmhd->bnhd', pr, vi)

    o = lax.map(one_chunk, (jnp.arange(nc), qc))
    return o.transpose(1, 0, 2, 3, 4).reshape(B, S, H * dh)


def _fwd_setup_inputs(seed: int = 0) -> dict:
    key = jax.random.key(seed)
    ks = jax.random.split(key, 16)
    f32 = jnp.float32

    def w(k, shape, fan_in):
        return jax.random.normal(k, shape, f32) * (fan_in ** -0.5)

    return {
        "x": jax.random.normal(ks[0], (BATCH, SEQ, D_MODEL), f32),
        "norm_mix": 1.0 + 0.05 * jax.random.normal(ks[1], (DEPTH, D_MODEL), f32),
        "w_in": w(ks[2], (DEPTH, D_MODEL, N_IN), D_MODEL),
        "b_gate": 0.02 * jax.random.normal(ks[3], (DEPTH, 2 * D_MODEL), f32),
        "rel_bias": 0.1 * jax.random.normal(ks[4], (DEPTH, ATT_HEADS, N_REL), f32),
        "w_ret_out": w(ks[5], (DEPTH, RET_V, D_MODEL), RET_V),
        "w_att_out": w(ks[6], (DEPTH, ATT_W, D_MODEL), ATT_W),
        "w_out": w(ks[7], (DEPTH, D_MODEL, D_MODEL), D_MODEL),
        "norm_ffn": 1.0 + 0.05 * jax.random.normal(ks[8], (DEPTH, D_MODEL), f32),
        "w_ffn_gate": w(ks[9], (DEPTH, D_MODEL, D_FF), D_MODEL),
        "w_ffn_up": w(ks[10], (DEPTH, D_MODEL, D_FF), D_MODEL),
        "w_ffn_down": w(ks[11], (DEPTH, D_FF, D_MODEL), D_FF),
        "norm_final": 1.0 + 0.05 * jax.random.normal(ks[12], (D_MODEL,), f32),
    }


def _fwd_reference(x, norm_mix, w_in, b_gate, rel_bias, w_ret_out, w_att_out, w_out,
              norm_ffn, w_ffn_gate, w_ffn_up, w_ffn_down, norm_final):
    B, S, _ = x.shape
    pos = jnp.arange(S, dtype=jnp.float32)
    h = x
    for l in range(DEPTH):
        xn = rmsnorm(h, norm_mix[l])
        proj = xn @ w_in[l]
        rq, rk, rv, rg, aq, ak, av, gl = jnp.split(proj, IN_SPLITS, axis=-1)
        rq = rotary(rq.reshape(B, S, RET_HEADS, RET_KEY_DIM), pos)
        rk = rotary(rk.reshape(B, S, RET_HEADS, RET_KEY_DIM), pos)
        rv = rv.reshape(B, S, RET_HEADS, RET_VAL_DIM)
        ro = head_groupnorm(retention(rq, rk, rv)).reshape(B, S, RET_V)
        y_ret = (jax.nn.silu(rg) * ro) @ w_ret_out[l]
        ao = chunk_band_attention(aq.reshape(B, S, ATT_HEADS, ATT_HEAD_DIM),
                                  ak.reshape(B, S, ATT_HEADS, ATT_HEAD_DIM),
                                  av.reshape(B, S, ATT_HEADS, ATT_HEAD_DIM),
                                  rel_bias[l])
        y_att = ao @ w_att_out[l]
        gates = jax.nn.sigmoid(gl + b_gate[l])
        g_ret, g_att = gates[..., :D_MODEL], gates[..., D_MODEL:]
        h = h + (g_ret * y_ret + g_att * y_att) @ w_out[l]
        hn = rmsnorm(h, norm_ffn[l])
        h = h + (jax.nn.silu(hn @ w_ffn_gate[l]) * (hn @ w_ffn_up[l])) @ w_ffn_down[l]
    return rmsnorm(h, norm_final)


import jax as _jax
import jax.numpy as _jnp

TWIN_FORMAT = 'train_step'
FWD_PARAMS = ['x', 'norm_mix', 'w_in', 'b_gate', 'rel_bias', 'w_ret_out', 'w_att_out', 'w_out', 'norm_ffn', 'w_ffn_gate', 'w_ffn_up', 'w_ffn_down', 'norm_final']
TWIN_WEIGHTS = ['norm_mix', 'w_in', 'b_gate', 'rel_bias', 'w_ret_out', 'w_att_out', 'w_out', 'norm_ffn', 'w_ffn_gate', 'w_ffn_up', 'w_ffn_down', 'norm_final']
TWIN_DIFF_INPUT = 'x'
TWIN_INPUTS = ['x', 'norm_mix', 'w_in', 'b_gate', 'rel_bias', 'w_ret_out', 'w_att_out', 'w_out', 'norm_ffn', 'w_ffn_gate', 'w_ffn_up', 'w_ffn_down', 'norm_final', 'loss_target', 'm_norm_mix', 'm_w_in', 'm_b_gate', 'm_rel_bias', 'm_w_ret_out', 'm_w_att_out', 'm_w_out', 'm_norm_ffn', 'm_w_ffn_gate', 'm_w_ffn_up', 'm_w_ffn_down', 'm_norm_final', 'v_norm_mix', 'v_w_in', 'v_b_gate', 'v_rel_bias', 'v_w_ret_out', 'v_w_att_out', 'v_w_out', 'v_norm_ffn', 'v_w_ffn_gate', 'v_w_ffn_up', 'v_w_ffn_down', 'v_norm_final']
TWIN_OUTPUTS = ['loss', 'grad_x', 'grad_norm_mix', 'grad_w_in', 'grad_b_gate', 'grad_rel_bias', 'grad_w_ret_out', 'grad_w_att_out', 'grad_w_out', 'grad_norm_ffn', 'grad_w_ffn_gate', 'grad_w_ffn_up', 'grad_w_ffn_down', 'grad_norm_final', 'delta_norm_mix', 'delta_w_in', 'delta_b_gate', 'delta_rel_bias', 'delta_w_ret_out', 'delta_w_att_out', 'delta_w_out', 'delta_norm_ffn', 'delta_w_ffn_gate', 'delta_w_ffn_up', 'delta_w_ffn_down', 'delta_norm_final', 'new_m_norm_mix', 'new_m_w_in', 'new_m_b_gate', 'new_m_rel_bias', 'new_m_w_ret_out', 'new_m_w_att_out', 'new_m_w_out', 'new_m_norm_ffn', 'new_m_w_ffn_gate', 'new_m_w_ffn_up', 'new_m_w_ffn_down', 'new_m_norm_final', 'new_v_norm_mix', 'new_v_w_in', 'new_v_b_gate', 'new_v_rel_bias', 'new_v_w_ret_out', 'new_v_w_att_out', 'new_v_w_out', 'new_v_norm_ffn', 'new_v_w_ffn_gate', 'new_v_w_ffn_up', 'new_v_w_ffn_down', 'new_v_norm_final']
TWIN_LEAF_KINDS = {'loss': 'loss', 'grad_x': 'grad_x', 'grad_norm_mix': 'grad_w', 'grad_w_in': 'grad_w', 'grad_b_gate': 'grad_w', 'grad_rel_bias': 'grad_w', 'grad_w_ret_out': 'grad_w', 'grad_w_att_out': 'grad_w', 'grad_w_out': 'grad_w', 'grad_norm_ffn': 'grad_w', 'grad_w_ffn_gate': 'grad_w', 'grad_w_ffn_up': 'grad_w', 'grad_w_ffn_down': 'grad_w', 'grad_norm_final': 'grad_w', 'delta_norm_mix': 'delta_w', 'delta_w_in': 'delta_w', 'delta_b_gate': 'delta_w', 'delta_rel_bias': 'delta_w', 'delta_w_ret_out': 'delta_w', 'delta_w_att_out': 'delta_w', 'delta_w_out': 'delta_w', 'delta_norm_ffn': 'delta_w', 'delta_w_ffn_gate': 'delta_w', 'delta_w_ffn_up': 'delta_w', 'delta_w_ffn_down': 'delta_w', 'delta_norm_final': 'delta_w', 'new_m_norm_mix': 'new_m', 'new_m_w_in': 'new_m', 'new_m_b_gate': 'new_m', 'new_m_rel_bias': 'new_m', 'new_m_w_ret_out': 'new_m', 'new_m_w_att_out': 'new_m', 'new_m_w_out': 'new_m', 'new_m_norm_ffn': 'new_m', 'new_m_w_ffn_gate': 'new_m', 'new_m_w_ffn_up': 'new_m', 'new_m_w_ffn_down': 'new_m', 'new_m_norm_final': 'new_m', 'new_v_norm_mix': 'new_v', 'new_v_w_in': 'new_v', 'new_v_b_gate': 'new_v', 'new_v_rel_bias': 'new_v', 'new_v_w_ret_out': 'new_v', 'new_v_w_att_out': 'new_v', 'new_v_w_out': 'new_v', 'new_v_norm_ffn': 'new_v', 'new_v_w_ffn_gate': 'new_v', 'new_v_w_ffn_up': 'new_v', 'new_v_w_ffn_down': 'new_v', 'new_v_norm_final': 'new_v'}


def _forward(args):
    return _fwd_reference(*[args[k] for k in FWD_PARAMS])


def _output_shape():
    out = _jax.eval_shape(lambda: _forward(_fwd_setup_inputs(0)))
    return out.shape, out.dtype

N_MICROBATCH = 1
ADAM_LR = 0.001
ADAM_B1 = 0.9
ADAM_B2 = 0.999
ADAM_EPS = 1e-08
ADAM_WD = 0.01
ADAM_STEP = 10
PER_EXAMPLE_BATCH_AXIS = {'x': 0, 'loss_target': 0}
SHARED_INPUTS = []
_WEIGHT_DTYPES = {'norm_mix': _jnp.float32, 'w_in': _jnp.float32, 'b_gate': _jnp.float32, 'rel_bias': _jnp.float32, 'w_ret_out': _jnp.float32, 'w_att_out': _jnp.float32, 'w_out': _jnp.float32, 'norm_ffn': _jnp.float32, 'w_ffn_gate': _jnp.float32, 'w_ffn_up': _jnp.float32, 'w_ffn_down': _jnp.float32, 'norm_final': _jnp.float32}
MOMENT_SCALE = {'norm_mix': 1.338363e-01, 'w_in': 5.104822e-02, 'b_gate': 1.761824e-02, 'rel_bias': 9.052288e-03, 'w_ret_out': 6.210010e-02, 'w_att_out': 1.518506e-02, 'w_out': 6.319589e-02, 'norm_ffn': 1.274182e-01, 'w_ffn_gate': 5.491259e-02, 'w_ffn_up': 5.341878e-02, 'w_ffn_down': 8.902008e-02, 'norm_final': 3.213040e+01}


def _to_microbatches(a, axis):
    t = _jnp.moveaxis(a, axis, 0)
    t = t.reshape((N_MICROBATCH, t.shape[0] // N_MICROBATCH) + t.shape[1:])
    return _jnp.moveaxis(t, 1, axis + 1)


def setup_inputs(seed: int = 0) -> dict:
    inp = _fwd_setup_inputs(seed)
    key = _jax.random.fold_in(_jax.random.key(seed), 7919)
    shape, _ = _output_shape()
    out = dict(inp)
    out["loss_target"] = _jax.random.normal(_jax.random.fold_in(key, 0), shape, _jnp.float32)
    for i, name in enumerate(TWIN_WEIGHTS):
        w = inp[name].astype(_jnp.float32)
        if MOMENT_SCALE is None:
            s = _jnp.sqrt(_jnp.mean(_jnp.square(w)) + 1e-30)
        else:
            s = MOMENT_SCALE[name]
        km, kv = _jax.random.split(_jax.random.fold_in(key, i + 1))
        out[name] = w
        out["m_" + name] = s * _jax.random.normal(km, w.shape, _jnp.float32)
        out["v_" + name] = (s * s) * _jax.random.uniform(kv, w.shape, _jnp.float32, 0.5, 1.5)
    if N_MICROBATCH > 1:
        for name, axis in PER_EXAMPLE_BATCH_AXIS.items():
            out[name] = _to_microbatches(out[name], axis)
    return {'x': out['x'], 'norm_mix': out['norm_mix'], 'w_in': out['w_in'], 'b_gate': out['b_gate'], 'rel_bias': out['rel_bias'], 'w_ret_out': out['w_ret_out'], 'w_att_out': out['w_att_out'], 'w_out': out['w_out'], 'norm_ffn': out['norm_ffn'], 'w_ffn_gate': out['w_ffn_gate'], 'w_ffn_up': out['w_ffn_up'], 'w_ffn_down': out['w_ffn_down'], 'norm_final': out['norm_final'], 'loss_target': out['loss_target'], 'm_norm_mix': out['m_norm_mix'], 'm_w_in': out['m_w_in'], 'm_b_gate': out['m_b_gate'], 'm_rel_bias': out['m_rel_bias'], 'm_w_ret_out': out['m_w_ret_out'], 'm_w_att_out': out['m_w_att_out'], 'm_w_out': out['m_w_out'], 'm_norm_ffn': out['m_norm_ffn'], 'm_w_ffn_gate': out['m_w_ffn_gate'], 'm_w_ffn_up': out['m_w_ffn_up'], 'm_w_ffn_down': out['m_w_ffn_down'], 'm_norm_final': out['m_norm_final'], 'v_norm_mix': out['v_norm_mix'], 'v_w_in': out['v_w_in'], 'v_b_gate': out['v_b_gate'], 'v_rel_bias': out['v_rel_bias'], 'v_w_ret_out': out['v_w_ret_out'], 'v_w_att_out': out['v_w_att_out'], 'v_w_out': out['v_w_out'], 'v_norm_ffn': out['v_norm_ffn'], 'v_w_ffn_gate': out['v_w_ffn_gate'], 'v_w_ffn_up': out['v_w_ffn_up'], 'v_w_ffn_down': out['v_w_ffn_down'], 'v_norm_final': out['v_norm_final']}


def _loss(weights, diff, rest, loss_target):
    with _jax.named_scope("forward"):
        args = {**rest, TWIN_DIFF_INPUT: diff, **{k: w.astype(_WEIGHT_DTYPES[k]) for k, w in weights.items()}}
        y = _forward(args)
    with _jax.named_scope("loss_head"):
        err = _jnp.square(y.astype(_jnp.float32) - loss_target)
        return 0.5 * _jnp.sum(_jnp.mean(err, axis=-1)) if err.ndim else 0.5 * err


def _adamw(w, g, m, v):
    m = ADAM_B1 * m + (1.0 - ADAM_B1) * g
    v = ADAM_B2 * v + (1.0 - ADAM_B2) * _jnp.square(g)
    m_hat = m / (1.0 - ADAM_B1 ** ADAM_STEP)
    v_hat = v / (1.0 - ADAM_B2 ** ADAM_STEP)
    delta = -ADAM_LR * (m_hat / (_jnp.sqrt(v_hat) + ADAM_EPS) + ADAM_WD * w)
    return delta, m, v


def reference(x, norm_mix, w_in, b_gate, rel_bias, w_ret_out, w_att_out, w_out, norm_ffn, w_ffn_gate, w_ffn_up, w_ffn_down, norm_final, loss_target, m_norm_mix, m_w_in, m_b_gate, m_rel_bias, m_w_ret_out, m_w_att_out, m_w_out, m_norm_ffn, m_w_ffn_gate, m_w_ffn_up, m_w_ffn_down, m_norm_final, v_norm_mix, v_w_in, v_b_gate, v_rel_bias, v_w_ret_out, v_w_att_out, v_w_out, v_norm_ffn, v_w_ffn_gate, v_w_ffn_up, v_w_ffn_down, v_norm_final):
    given = dict(x=x, norm_mix=norm_mix, w_in=w_in, b_gate=b_gate, rel_bias=rel_bias, w_ret_out=w_ret_out, w_att_out=w_att_out, w_out=w_out, norm_ffn=norm_ffn, w_ffn_gate=w_ffn_gate, w_ffn_up=w_ffn_up, w_ffn_down=w_ffn_down, norm_final=norm_final, loss_target=loss_target, m_norm_mix=m_norm_mix, m_w_in=m_w_in, m_b_gate=m_b_gate, m_rel_bias=m_rel_bias, m_w_ret_out=m_w_ret_out, m_w_att_out=m_w_att_out, m_w_out=m_w_out, m_norm_ffn=m_norm_ffn, m_w_ffn_gate=m_w_ffn_gate, m_w_ffn_up=m_w_ffn_up, m_w_ffn_down=m_w_ffn_down, m_norm_final=m_norm_final, v_norm_mix=v_norm_mix, v_w_in=v_w_in, v_b_gate=v_b_gate, v_rel_bias=v_rel_bias, v_w_ret_out=v_w_ret_out, v_w_att_out=v_w_att_out, v_w_out=v_w_out, v_norm_ffn=v_norm_ffn, v_w_ffn_gate=v_w_ffn_gate, v_w_ffn_up=v_w_ffn_up, v_w_ffn_down=v_w_ffn_down, v_norm_final=v_norm_final)
    weights = {n: given[n] for n in TWIN_WEIGHTS}
    shared = {n: given[n] for n in SHARED_INPUTS}
    per_example = {n: given[n] for n in ['x']}
    grad_fn = _jax.value_and_grad(_loss, argnums=(0, 1))

    def one_microbatch(ex, loss_target):
        ex = dict(ex)
        diff = ex.pop(TWIN_DIFF_INPUT)
        return grad_fn(weights, diff, {**shared, **ex}, loss_target)

    if N_MICROBATCH == 1:
        loss, (grad_w, grad_x) = one_microbatch(per_example, given["loss_target"])
    else:
        def body(carry, xs):
            loss_sum, grad_sum = carry
            l_k, (gw_k, gx_k) = one_microbatch(xs[0], xs[1])
            with _jax.named_scope("update"):
                return (loss_sum + l_k, _jax.tree.map(_jnp.add, grad_sum, gw_k)), gx_k

        init = (_jnp.zeros((), _jnp.float32), _jax.tree.map(_jnp.zeros_like, weights))
        (loss, grad_w), grad_x = _jax.lax.scan(body, init, (per_example, given["loss_target"]))
    with _jax.named_scope("update"):
        delta_w, new_m, new_v = {}, {}, {}
        for n in TWIN_WEIGHTS:
            delta_w[n], new_m[n], new_v[n] = _adamw(weights[n], grad_w[n], given["m_" + n], given["v_" + n])
    return (loss, grad_x, *[grad_w[n] for n in TWIN_WEIGHTS], *[delta_w[n] for n in TWIN_WEIGHTS],
            *[new_m[n] for n in TWIN_WEIGHTS], *[new_v[n] for n in TWIN_WEIGHTS])
```

```python
import numpy as np
import jax
import jax.numpy as jnp
from jax import lax
from jax.experimental import pallas as pl
from jax.experimental.pallas import tpu as pltpu

F32 = jnp.float32
BF16 = jnp.bfloat16
MESH = pl.DeviceIdType.MESH

D_MODEL = 1024
CHUNK = 64
RET_HEADS = 4
RET_KEY_DIM = 128
RET_VAL_DIM = 256
ATT_HEADS = 8
BAND_CHUNKS = 8
MAX_REL = 256
N_REL = CHUNK + MAX_REL
D_FF = 2816
N_IN = 6656
ROPE_BASE = 10000.0
EPS = 1e-6
NEG_INF = -1e30
C_RQ, C_RK, C_RV, C_RG, C_AQ, C_AK, C_AV, C_GL = 0, 512, 1024, 2048, 3072, 3584, 4096, 4608

ADAM_LR = 0.001
ADAM_B1 = 0.9
ADAM_B2 = 0.999
ADAM_EPS = 1e-08
ADAM_WD = 0.01
ADAM_STEP = 10

N_DEV = 8
LANES = 128
RET_TILE = 256
ATT_Q = 256
ATT_PAD = BAND_CHUNKS * CHUNK
ATT_WIN = ATT_PAD + ATT_Q
BIAS_LEN = 1024
VMEM_LIMIT = 48 * 1024 * 1024


def _params(*sem):
    return pltpu.CompilerParams(dimension_semantics=sem, vmem_limit_bytes=VMEM_LIMIT)


def _dot(a, b):
    return lax.dot_general(a, b, (((1,), (0,)), ((), ())), preferred_element_type=F32)


def _dot_nt(a, b):
    return lax.dot_general(a, b, (((1,), (1,)), ((), ())), preferred_element_type=F32)


def _dot_tn(a, b):
    return lax.dot_general(a, b, (((0,), (0,)), ((), ())), preferred_element_type=F32)


def _sigmoid(x):
    return 1.0 / (1.0 + jnp.exp(-x))


def _rms_bwd(x, g, dy):
    r = lax.rsqrt(jnp.mean(x * x, axis=-1, keepdims=True) + EPS)
    u = dy * g
    dx = r * u - x * (r * r * r) * jnp.mean(u * x, axis=-1, keepdims=True)
    return dx, dy * x * r


def _mm(a, b, *, ta=False, tb=False, out_dtype, tm, tn, tk, name):
    m, k = (a.shape[1], a.shape[0]) if ta else a.shape
    n = b.shape[0] if tb else b.shape[1]
    assert k == (b.shape[1] if tb else b.shape[0])
    tm, tn, tk = min(tm, m), min(tn, n), min(tk, k)
    assert m % tm == 0 and n % tn == 0 and k % tk == 0, (name, m, n, k)
    nk = k // tk
    dims = (((0 if ta else 1,), (1 if tb else 0,)), ((), ()))

    def body(a_ref, b_ref, o_ref, acc_ref):
        kk = pl.program_id(2)

        @pl.when(kk == 0)
        def _():
            acc_ref[...] = jnp.zeros_like(acc_ref)

        acc_ref[...] += lax.dot_general(a_ref[...].astype(BF16), b_ref[...].astype(BF16), dims,
                                        preferred_element_type=F32)

        @pl.when(kk == nk - 1)
        def _():
            o_ref[...] = acc_ref[...].astype(o_ref.dtype)

    a_spec = (pl.BlockSpec((tk, tm), lambda i, j, kk: (kk, i)) if ta
              else pl.BlockSpec((tm, tk), lambda i, j, kk: (i, kk)))
    b_spec = (pl.BlockSpec((tn, tk), lambda i, j, kk: (j, kk)) if tb
              else pl.BlockSpec((tk, tn), lambda i, j, kk: (kk, j)))
    return pl.pallas_call(
        body, name=name, grid=(m // tm, n // tn, nk),
        in_specs=[a_spec, b_spec],
        out_specs=pl.BlockSpec((tm, tn), lambda i, j, kk: (i, j)),
        out_shape=jax.ShapeDtypeStruct((m, n), out_dtype),
        scratch_shapes=[pltpu.VMEM((tm, tn), F32)],
        compiler_params=_params("parallel", "parallel", "arbitrary"),
    )(a, b)


def _rms_fwd(x2, g):
    t = x2.shape[0]
    tm = min(512, t)

    def body(x_ref, g_ref, o_ref):
        x = x_ref[...]
        r = lax.rsqrt(jnp.mean(x * x, axis=-1, keepdims=True) + EPS)
        o_ref[...] = (x * r * g_ref[...]).astype(o_ref.dtype)

    return pl.pallas_call(
        body, name="rms_in_fwd", grid=(t // tm,),
        in_specs=[pl.BlockSpec((tm, D_MODEL), lambda i: (i, 0)), pl.BlockSpec((1, D_MODEL), lambda i: (0, 0))],
        out_specs=pl.BlockSpec((tm, D_MODEL), lambda i: (i, 0)),
        out_shape=jax.ShapeDtypeStruct((t, D_MODEL), BF16),
        compiler_params=_params("parallel"),
    )(x2, g)


def _decay_tiles(lg):
    row = lax.broadcasted_iota(jnp.int32, (RET_TILE, RET_TILE), 0)
    col = lax.broadcasted_iota(jnp.int32, (RET_TILE, RET_TILE), 1)
    diff = (row - col).astype(F32)
    within = jnp.exp(lg * jnp.abs(diff))
    e_qk = jnp.exp(lg * diff)
    d_qk = jnp.where((col >> 6) <= (row >> 6), within, 0.0)
    return e_qk, d_qk, row, col, diff, within


def _ret_fwd(proj, cs, sn, lg_arr, batch, seq):
    t = batch * seq
    nt = seq // RET_TILE

    def body(q_ref, k_ref, v_ref, rg_ref, cs_ref, sn_ref, lg_ref, gro_ref, o_ref, qr_ref, kr_ref):
        lg = lg_ref[:, 0:1]
        cs_t, sn_t = cs_ref[...], sn_ref[...]
        q = q_ref[...].astype(F32)
        k = k_ref[...].astype(F32)
        qr_ref[...] = (q * cs_t + pltpu.roll(q, 64, 1) * sn_t).astype(BF16)
        kr_ref[...] = ((k * cs_t + pltpu.roll(k, 64, 1) * sn_t) * (RET_KEY_DIM ** -0.5)).astype(BF16)
        e_qk, d_qk = _decay_tiles(lg)[:2]
        for i in range(nt):
            rows = slice(i * RET_TILE, (i + 1) * RET_TILE)
            qi = qr_ref[rows, :]

            def pair(j, acc, i=i, qi=qi):
                off = pl.multiple_of(j * RET_TILE, RET_TILE)
                s = _dot_nt(qi, kr_ref[pl.ds(off, RET_TILE), :])
                far = jnp.exp(lg * (RET_TILE * (i - j)).astype(F32))
                p = (s * (e_qk * far)).astype(BF16)
                return acc + _dot(p, v_ref[pl.ds(off, RET_TILE), :])

            acc = lax.fori_loop(0, i, pair, jnp.zeros((RET_TILE, RET_VAL_DIM), F32))
            s = _dot_nt(qi, kr_ref[rows, :])
            acc = acc + _dot((s * d_qk).astype(BF16), v_ref[rows, :])
            o_ref[rows, :] = acc
            xc = acc - jnp.mean(acc, axis=-1, keepdims=True)
            nrm = xc * lax.rsqrt(jnp.mean(xc * xc, axis=-1, keepdims=True) + EPS)
            rg = rg_ref[rows, :].astype(F32)
            gro_ref[rows, :] = (rg * _sigmoid(rg) * nrm).astype(BF16)

    def col(base, width):
        return lambda b, h: (b, base // width + h)

    return pl.pallas_call(
        body, name="ret_fwd", grid=(batch, RET_HEADS),
        in_specs=[pl.BlockSpec((seq, RET_KEY_DIM), col(C_RQ, RET_KEY_DIM)),
                  pl.BlockSpec((seq, RET_KEY_DIM), col(C_RK, RET_KEY_DIM)),
                  pl.BlockSpec((seq, RET_VAL_DIM), col(C_RV, RET_VAL_DIM)),
                  pl.BlockSpec((seq, RET_VAL_DIM), col(C_RG, RET_VAL_DIM)),
                  pl.BlockSpec((seq, RET_KEY_DIM), lambda b, h: (0, 0)),
                  pl.BlockSpec((seq, RET_KEY_DIM), lambda b, h: (0, 0)),
                  pl.BlockSpec((None, 1, LANES), lambda b, h: (h, 0, 0))],
        out_specs=[pl.BlockSpec((seq, RET_VAL_DIM), lambda b, h: (b, h)),
                   pl.BlockSpec((seq, RET_VAL_DIM), lambda b, h: (b, h)),
                   pl.BlockSpec((seq, RET_KEY_DIM), lambda b, h: (b, h)),
                   pl.BlockSpec((seq, RET_KEY_DIM), lambda b, h: (b, h))],
        out_shape=[jax.ShapeDtypeStruct((t, RET_HEADS * RET_VAL_DIM), BF16),
                   jax.ShapeDtypeStruct((t, RET_HEADS * RET_VAL_DIM), F32),
                   jax.ShapeDtypeStruct((t, RET_HEADS * RET_KEY_DIM), BF16),
                   jax.ShapeDtypeStruct((t, RET_HEADS * RET_KEY_DIM), BF16)],
        compiler_params=_params("parallel", "parallel"),
    )(proj, proj, proj, proj, cs, sn, lg_arr)


def _att_bias(w_ref, bias_ref):
    n_i = lax.broadcasted_iota(jnp.int32, (ATT_Q, BIAS_LEN), 0)
    qc = lax.broadcasted_iota(jnp.int32, (ATT_Q, ATT_WIN), 0) >> 6
    kc = lax.broadcasted_iota(jnp.int32, (ATT_Q, ATT_WIN), 1) >> 6
    dc = qc + BAND_CHUNKS - kc
    band = (dc >= 0) & (dc <= BAND_CHUNKS)
    for e in range(2):
        xw = jnp.broadcast_to(w_ref[e:e + 1, :], (ATT_Q, BIAS_LEN))
        for bit in range(8):
            xw = jnp.where(((n_i >> bit) & 1) == 1, pltpu.roll(xw, 1 << bit, 1), xw)
        bias_ref[e] = jnp.where(band, xw[:, BIAS_LEN - ATT_WIN:], NEG_INF)


def _att_specs(batch, seq):
    ni = seq // ATT_Q
    q_spec = pl.BlockSpec((ATT_Q, LANES), lambda hp, b, i: (b * ni + i, C_AQ // LANES + hp))
    kv_spec = pl.BlockSpec((seq + ATT_PAD, LANES), lambda hp, b, i: (b, hp))
    w_spec = pl.BlockSpec((None, 2, BIAS_LEN), lambda hp, b, i: (hp, 0, 0))
    return ni, q_spec, kv_spec, w_spec


def _att_scores(q2, k2, bias, sel, start_ok):
    qm = jnp.where(sel, q2, jnp.zeros_like(q2))
    s = _dot_nt(qm, k2) * (1.0 / 8.0) + bias
    s = jnp.where(start_ok, s, NEG_INF)
    p = jnp.exp(s - jnp.max(s, axis=-1, keepdims=True))
    return p / jnp.sum(p, axis=-1, keepdims=True)


def _att_fwd(proj, kpad, vpad, wvec, batch, seq):
    ni, q_spec, kv_spec, w_spec = _att_specs(batch, seq)

    def body(q_ref, k_ref, v_ref, w_ref, o_ref, bias_ref):
        b, i = pl.program_id(1), pl.program_id(2)

        @pl.when((b == 0) & (i == 0))
        def _():
            _att_bias(w_ref, bias_ref)

        win = pl.ds(pl.multiple_of(i * ATT_Q, ATT_Q), ATT_WIN)
        k2, v2, q2 = k_ref[win, :], v_ref[win, :], q_ref[...]
        start_ok = lax.broadcasted_iota(jnp.int32, (ATT_Q, ATT_WIN), 1) + (i * ATT_Q - ATT_PAD) >= 0
        lo = lax.broadcasted_iota(jnp.int32, (1, LANES), 1) < 64
        out = jnp.zeros((ATT_Q, LANES), F32)
        for e in range(2):
            sel = lo if e == 0 else jnp.logical_not(lo)
            p = _att_scores(q2, k2, bias_ref[e], sel, start_ok)
            out = out + _dot(p.astype(BF16), jnp.where(sel, v2, jnp.zeros_like(v2)))
        o_ref[...] = out.astype(BF16)

    return pl.pallas_call(
        body, name="att_fwd", grid=(ATT_HEADS // 2, batch, ni),
        in_specs=[q_spec, kv_spec, kv_spec, w_spec],
        out_specs=pl.BlockSpec((ATT_Q, LANES), lambda hp, b, i: (b * ni + i, hp)),
        out_shape=jax.ShapeDtypeStruct((batch * seq, ATT_HEADS * 64), BF16),
        scratch_shapes=[pltpu.VMEM((2, ATT_Q, ATT_WIN), F32)],
        compiler_params=_params("arbitrary", "arbitrary", "arbitrary"),
    )(proj, kpad, vpad, wvec)


def _mix_fwd(gro, ao, proj, b_gate, w_ret, w_att_t):
    t = gro.shape[0]
    tm, tn = min(256, t), 512

    def body(gro_ref, ao_ref, glr_ref, gla_ref, br_ref, ba_ref, wr_ref, wa_ref, z_ref, yr_ref, ya_ref):
        yr = _dot(gro_ref[...], wr_ref[...])
        ya = _dot_nt(ao_ref[...], wa_ref[...])
        gr = _sigmoid(glr_ref[...].astype(F32) + br_ref[...])
        ga = _sigmoid(gla_ref[...].astype(F32) + ba_ref[...])
        z_ref[...] = (gr * yr + ga * ya).astype(BF16)
        yr_ref[...] = yr.astype(BF16)
        ya_ref[...] = ya.astype(BF16)

    nb = D_MODEL // tn
    out = pl.BlockSpec((tm, tn), lambda i, j: (i, j))
    return pl.pallas_call(
        body, name="mix_fwd", grid=(t // tm, nb),
        in_specs=[pl.BlockSpec((tm, D_MODEL), lambda i, j: (i, 0)),
                  pl.BlockSpec((tm, 512), lambda i, j: (i, 0)),
                  pl.BlockSpec((tm, tn), lambda i, j: (i, C_GL // tn + j)),
                  pl.BlockSpec((tm, tn), lambda i, j: (i, C_GL // tn + nb + j)),
                  pl.BlockSpec((1, tn), lambda i, j: (0, j)),
                  pl.BlockSpec((1, tn), lambda i, j: (0, nb + j)),
                  pl.BlockSpec((D_MODEL, tn), lambda i, j: (0, j)),
                  pl.BlockSpec((tn, 512), lambda i, j: (j, 0))],
        out_specs=[out, out, out],
        out_shape=[jax.ShapeDtypeStruct((t, D_MODEL), BF16)] * 3,
        compiler_params=_params("parallel", "parallel"),
    )(gro, ao, proj, proj, b_gate, b_gate, w_ret, w_att_t)


def _out_fwd(z, x2, w_out, g2):
    t = z.shape[0]
    tm = min(256, t)

    def body(z_ref, x_ref, w_ref, g_ref, h_ref, hn_ref):
        h = x_ref[...] + _dot(z_ref[...], w_ref[...])
        h_ref[...] = h
        r = lax.rsqrt(jnp.mean(h * h, axis=-1, keepdims=True) + EPS)
        hn_ref[...] = (h * r * g_ref[...]).astype(BF16)

    row = pl.BlockSpec((tm, D_MODEL), lambda i: (i, 0))
    return pl.pallas_call(
        body, name="out_fwd", grid=(t // tm,),
        in_specs=[row, row, pl.BlockSpec((D_MODEL, D_MODEL), lambda i: (0, 0)),
                  pl.BlockSpec((1, D_MODEL), lambda i: (0, 0))],
        out_specs=[row, row],
        out_shape=[jax.ShapeDtypeStruct((t, D_MODEL), F32), jax.ShapeDtypeStruct((t, D_MODEL), BF16)],
        compiler_params=_params("parallel"),
    )(z, x2, w_out, g2)


def _ffn_up(hn, wg_t, wu_t):
    t = hn.shape[0]
    tm, tn = min(512, t), 256

    def body(h_ref, wg_ref, wu_ref, g_ref, u_ref, a_ref):
        g = _dot_nt(h_ref[...], wg_ref[...])
        u = _dot_nt(h_ref[...], wu_ref[...])
        g_ref[...] = g.astype(BF16)
        u_ref[...] = u.astype(BF16)
        a_ref[...] = (g * _sigmoid(g) * u).astype(BF16)

    w_spec = pl.BlockSpec((tn, D_MODEL), lambda i, j: (j, 0))
    out = pl.BlockSpec((tm, tn), lambda i, j: (i, j))
    return pl.pallas_call(
        body, name="ffn_up", grid=(t // tm, D_FF // tn),
        in_specs=[pl.BlockSpec((tm, D_MODEL), lambda i, j: (i, 0)), w_spec, w_spec],
        out_specs=[out, out, out],
        out_shape=[jax.ShapeDtypeStruct((t, D_FF), BF16)] * 3,
        compiler_params=_params("parallel", "parallel"),
    )(hn, wg_t, wu_t)


def _ffn_down_loss(a, h1, tgt, w_down, g3):
    t = a.shape[0]
    tm = min(256, t)

    def body(a_ref, h_ref, t_ref, w_ref, g_ref, dh_ref, dhb_ref, loss_ref, dg_ref):
        @pl.when(pl.program_id(0) == 0)
        def _():
            loss_ref[...] = jnp.zeros_like(loss_ref)
            dg_ref[...] = jnp.zeros_like(dg_ref)

        g = g_ref[...]
        h2 = h_ref[...] + _dot(a_ref[...], w_ref[...])
        r = lax.rsqrt(jnp.mean(h2 * h2, axis=-1, keepdims=True) + EPS)
        err = h2 * r * g - t_ref[...]
        loss_ref[...] += jnp.sum(err * err) * (0.5 / D_MODEL)
        dy = err * (1.0 / D_MODEL)
        dh, dg_rows = _rms_bwd(h2, g, dy)
        dg_ref[...] += jnp.sum(dg_rows, axis=0, keepdims=True)
        dh_ref[...] = dh
        dhb_ref[...] = dh.astype(BF16)

    row = pl.BlockSpec((tm, D_MODEL), lambda i: (i, 0))
    vec = pl.BlockSpec((1, D_MODEL), lambda i: (0, 0))
    return pl.pallas_call(
        body, name="ffn_down_loss", grid=(t // tm,),
        in_specs=[pl.BlockSpec((tm, D_FF), lambda i: (i, 0)), row, row,
                  pl.BlockSpec((D_FF, D_MODEL), lambda i: (0, 0)), vec],
        out_specs=[row, row, pl.BlockSpec((1, LANES), lambda i: (0, 0)), vec],
        out_shape=[jax.ShapeDtypeStruct((t, D_MODEL), F32), jax.ShapeDtypeStruct((t, D_MODEL), BF16),
                   jax.ShapeDtypeStruct((1, LANES), F32), jax.ShapeDtypeStruct((1, D_MODEL), F32)],
        compiler_params=_params("arbitrary"),
    )(a, h1, tgt, w_down, g3)


def _ffn_bwd_act(dh2b, w_down, g_act, u_act):
    t = dh2b.shape[0]
    tm, tn = min(512, t), 256

    def body(d_ref, w_ref, g_ref, u_ref, dg_ref, du_ref):
        da = _dot_nt(d_ref[...], w_ref[...])
        g = g_ref[...].astype(F32)
        u = u_ref[...].astype(F32)
        sg = _sigmoid(g)
        dg_ref[...] = (da * u * sg * (1.0 + g * (1.0 - sg))).astype(BF16)
        du_ref[...] = (da * g * sg).astype(BF16)

    blk = pl.BlockSpec((tm, tn), lambda i, j: (i, j))
    return pl.pallas_call(
        body, name="ffn_bwd_act", grid=(t // tm, D_FF // tn),
        in_specs=[pl.BlockSpec((tm, D_MODEL), lambda i, j: (i, 0)),
                  pl.BlockSpec((tn, D_MODEL), lambda i, j: (j, 0)), blk, blk],
        out_specs=[blk, blk],
        out_shape=[jax.ShapeDtypeStruct((t, D_FF), BF16)] * 2,
        compiler_params=_params("parallel", "parallel"),
    )(dh2b, w_down, g_act, u_act)


def _ffn_bwd_in(dg, du, wg_t, wu_t, h1, dh2, g2):
    t = dg.shape[0]
    tm, tk = min(512, t), D_FF // 2
    nk = D_FF // tk

    def body(dg_ref, du_ref, wg_ref, wu_ref, h_ref, d2_ref, g_ref, dh_ref, dhb_ref, gn_ref, acc_ref):
        i, kk = pl.program_id(0), pl.program_id(1)

        @pl.when((i == 0) & (kk == 0))
        def _():
            gn_ref[...] = jnp.zeros_like(gn_ref)

        @pl.when(kk == 0)
        def _():
            acc_ref[...] = jnp.zeros_like(acc_ref)

        acc_ref[...] += _dot(dg_ref[...], wg_ref[...]) + _dot(du_ref[...], wu_ref[...])

        @pl.when(kk == nk - 1)
        def _():
            dx, dg_rows = _rms_bwd(h_ref[...], g_ref[...], acc_ref[...])
            dh = d2_ref[...] + dx
            dh_ref[...] = dh
            dhb_ref[...] = dh.astype(BF16)
            gn_ref[...] += jnp.sum(dg_rows, axis=0, keepdims=True)

    act = pl.BlockSpec((tm, tk), lambda i, kk: (i, kk))
    wsp = pl.BlockSpec((tk, D_MODEL), lambda i, kk: (kk, 0))
    row = pl.BlockSpec((tm, D_MODEL), lambda i, kk: (i, 0))
    vec = pl.BlockSpec((1, D_MODEL), lambda i, kk: (0, 0))
    return pl.pallas_call(
        body, name="ffn_bwd_in", grid=(t // tm, nk),
        in_specs=[act, act, wsp, wsp, row, row, vec],
        out_specs=[row, row, vec],
        out_shape=[jax.ShapeDtypeStruct((t, D_MODEL), F32), jax.ShapeDtypeStruct((t, D_MODEL), BF16),
                   jax.ShapeDtypeStruct((1, D_MODEL), F32)],
        scratch_shapes=[pltpu.VMEM((tm, D_MODEL), F32)],
        compiler_params=_params("arbitrary", "arbitrary"),
    )(dg, du, wg_t, wu_t, h1, dh2, g2)


def _mix_bwd(dh1b, w_out, proj, b_gate, y_ret, y_att):
    t = dh1b.shape[0]
    tm, tn = min(256, t), 512
    nb = D_MODEL // tn

    def body(d_ref, w_ref, glr_ref, gla_ref, br_ref, ba_ref, yr_ref, ya_ref,
             dyr_ref, dya_ref, dglr_ref, dgla_ref, dbr_ref, dba_ref):
        @pl.when(pl.program_id(1) == 0)
        def _():
            dbr_ref[...] = jnp.zeros_like(dbr_ref)
            dba_ref[...] = jnp.zeros_like(dba_ref)

        dz = _dot_nt(d_ref[...], w_ref[...])
        gr = _sigmoid(glr_ref[...].astype(F32) + br_ref[...])
        ga = _sigmoid(gla_ref[...].astype(F32) + ba_ref[...])
        dyr_ref[...] = (dz * gr).astype(BF16)
        dya_ref[...] = (dz * ga).astype(BF16)
        dglr = dz * yr_ref[...].astype(F32) * gr * (1.0 - gr)
        dgla = dz * ya_ref[...].astype(F32) * ga * (1.0 - ga)
        dglr_ref[...] = dglr.astype(BF16)
        dgla_ref[...] = dgla.astype(BF16)
        dbr_ref[...] += jnp.sum(dglr, axis=0, keepdims=True)
        dba_ref[...] += jnp.sum(dgla, axis=0, keepdims=True)

    blk = pl.BlockSpec((tm, tn), lambda j, i: (i, j))
    vec = pl.BlockSpec((1, tn), lambda j, i: (0, j))
    return pl.pallas_call(
        body, name="mix_bwd", grid=(nb, t // tm),
        in_specs=[pl.BlockSpec((tm, D_MODEL), lambda j, i: (i, 0)),
                  pl.BlockSpec((tn, D_MODEL), lambda j, i: (j, 0)),
                  pl.BlockSpec((tm, tn), lambda j, i: (i, C_GL // tn + j)),
                  pl.BlockSpec((tm, tn), lambda j, i: (i, C_GL // tn + nb + j)),
                  vec, pl.BlockSpec((1, tn), lambda j, i: (0, nb + j)), blk, blk],
        out_specs=[blk, blk, blk, blk, vec, vec],
        out_shape=[jax.ShapeDtypeStruct((t, D_MODEL), BF16)] * 4 + [jax.ShapeDtypeStruct((1, D_MODEL), F32)] * 2,
        compiler_params=_params("arbitrary", "arbitrary"),
    )(dh1b, w_out, proj, proj, b_gate, b_gate, y_ret, y_att)


def _ret_bwd(dgro, proj, o_ret, qr, kr, cs, sn, lg_arr, batch, seq):
    t = batch * seq
    nt = seq // RET_TILE

    def body(dgro_ref, rg_ref, o_ref, qr_ref, kr_ref, v_ref, cs_ref, sn_ref, lg_ref,
             dq_ref, dk_ref, dv_ref, drg_ref, do_ref):
        lg = lg_ref[:, 0:1]
        e_qk, d_qk, row, col, diff, within = _decay_tiles(lg)
        e_kq = jnp.exp(-lg * diff)
        d_kq = jnp.where((row >> 6) <= (col >> 6), within, 0.0)

        for i in range(nt):
            rows = slice(i * RET_TILE, (i + 1) * RET_TILE)
            o = o_ref[rows, :]
            xc = o - jnp.mean(o, axis=-1, keepdims=True)
            rs = lax.rsqrt(jnp.mean(xc * xc, axis=-1, keepdims=True) + EPS)
            nrm = xc * rs
            rg = rg_ref[rows, :].astype(F32)
            sg = _sigmoid(rg)
            dg = dgro_ref[rows, :].astype(F32)
            drg_ref[rows, :] = (dg * nrm * sg * (1.0 + rg * (1.0 - sg))).astype(BF16)
            dn = dg * rg * sg
            do = rs * (dn - jnp.mean(dn, axis=-1, keepdims=True)
                       - nrm * jnp.mean(dn * nrm, axis=-1, keepdims=True))
            do_ref[rows, :] = do.astype(BF16)

        for i in range(nt):
            rows = slice(i * RET_TILE, (i + 1) * RET_TILE)
            doi = do_ref[rows, :]

            def pair_q(j, acc, i=i, doi=doi):
                off = pl.multiple_of(j * RET_TILE, RET_TILE)
                dp = _dot_nt(doi, v_ref[pl.ds(off, RET_TILE), :])
                far = jnp.exp(lg * (RET_TILE * (i - j)).astype(F32))
                return acc + _dot((dp * (e_qk * far)).astype(BF16), kr_ref[pl.ds(off, RET_TILE), :])

            acc = lax.fori_loop(0, i, pair_q, jnp.zeros((RET_TILE, RET_KEY_DIM), F32))
            dp = _dot_nt(doi, v_ref[rows, :])
            acc = acc + _dot((dp * d_qk).astype(BF16), kr_ref[rows, :])
            dq_ref[rows, :] = (acc * cs_ref[rows, :] - pltpu.roll(acc, 64, 1) * sn_ref[rows, :]).astype(BF16)

        for j in range(nt):
            rows = slice(j * RET_TILE, (j + 1) * RET_TILE)
            kj, vj = kr_ref[rows, :], v_ref[rows, :]

            def pair_k(i, carry, j=j, kj=kj, vj=vj):
                dk, dv = carry
                off = pl.multiple_of(i * RET_TILE, RET_TILE)
                qi, doi = qr_ref[pl.ds(off, RET_TILE), :], do_ref[pl.ds(off, RET_TILE), :]
                dec = e_kq * jnp.exp(lg * (RET_TILE * (i - j)).astype(F32))
                dk = dk + _dot((_dot_nt(vj, doi) * dec).astype(BF16), qi)
                dv = dv + _dot((_dot_nt(kj, qi) * dec).astype(BF16), doi)
                return dk, dv

            dk, dv = lax.fori_loop(j + 1, nt, pair_k, (jnp.zeros((RET_TILE, RET_KEY_DIM), F32),
                                                        jnp.zeros((RET_TILE, RET_VAL_DIM), F32)))
            qi, doi = qr_ref[rows, :], do_ref[rows, :]
            dk = dk + _dot((_dot_nt(vj, doi) * d_kq).astype(BF16), qi)
            dv = dv + _dot((_dot_nt(kj, qi) * d_kq).astype(BF16), doi)
            dk = (dk * cs_ref[rows, :] - pltpu.roll(dk, 64, 1) * sn_ref[rows, :]) * (RET_KEY_DIM ** -0.5)
            dk_ref[rows, :] = dk.astype(BF16)
            dv_ref[rows, :] = dv.astype(BF16)

    key = pl.BlockSpec((seq, RET_KEY_DIM), lambda b, h: (b, h))
    val = pl.BlockSpec((seq, RET_VAL_DIM), lambda b, h: (b, h))
    tab = pl.BlockSpec((seq, RET_KEY_DIM), lambda b, h: (0, 0))
    return pl.pallas_call(
        body, name="ret_bwd", grid=(batch, RET_HEADS),
        in_specs=[val, pl.BlockSpec((seq, RET_VAL_DIM), lambda b, h: (b, C_RG // RET_VAL_DIM + h)), val, key, key,
                  pl.BlockSpec((seq, RET_VAL_DIM), lambda b, h: (b, C_RV // RET_VAL_DIM + h)), tab, tab,
                  pl.BlockSpec((None, 1, LANES), lambda b, h: (h, 0, 0))],
        out_specs=[key, key, val, val],
        out_shape=[jax.ShapeDtypeStruct((t, RET_HEADS * RET_KEY_DIM), BF16)] * 2
                  + [jax.ShapeDtypeStruct((t, RET_HEADS * RET_VAL_DIM), BF16)] * 2,
        scratch_shapes=[pltpu.VMEM((seq, RET_VAL_DIM), BF16)],
        compiler_params=_params("parallel", "parallel"),
    )(dgro, proj, o_ret, qr, kr, proj, cs, sn, lg_arr)


def _att_bwd(proj, kpad, vpad, wvec, dao, batch, seq):
    ni, q_spec, kv_spec, w_spec = _att_specs(batch, seq)
    t = batch * seq

    def body(q_ref, k_ref, v_ref, w_ref, do_ref, dq_ref, dk_ref, dv_ref, dw_ref,
             bias_ref, dbias_ref, dk_acc, dv_acc):
        b, i = pl.program_id(1), pl.program_id(2)

        @pl.when((b == 0) & (i == 0))
        def _():
            _att_bias(w_ref, bias_ref)
            dbias_ref[...] = jnp.zeros_like(dbias_ref)

        @pl.when(i == 0)
        def _():
            dk_acc[...] = jnp.zeros_like(dk_acc)
            dv_acc[...] = jnp.zeros_like(dv_acc)

        win = pl.ds(pl.multiple_of(i * ATT_Q, ATT_Q), ATT_WIN)
        k2, v2, q2, do2 = k_ref[win, :], v_ref[win, :], q_ref[...], do_ref[...]
        start_ok = lax.broadcasted_iota(jnp.int32, (ATT_Q, ATT_WIN), 1) + (i * ATT_Q - ATT_PAD) >= 0
        lo = lax.broadcasted_iota(jnp.int32, (1, LANES), 1) < 64
        dq = jnp.zeros((ATT_Q, LANES), F32)
        dk = jnp.zeros((ATT_WIN, LANES), F32)
        dv = jnp.zeros((ATT_WIN, LANES), F32)
        for e in range(2):
            sel = lo if e == 0 else jnp.logical_not(lo)
            p = _att_scores(q2, k2, bias_ref[e], sel, start_ok)
            dom = jnp.where(sel, do2, jnp.zeros_like(do2))
            dp = _dot_nt(dom, v2)
            ds = p * (dp - jnp.sum(dp * p, axis=-1, keepdims=True))
            dbias_ref[e] += ds
            dsb = (ds * (1.0 / 8.0)).astype(BF16)
            dq = dq + _dot(dsb, jnp.where(sel, k2, jnp.zeros_like(k2)))
            dk = dk + _dot_tn(dsb, jnp.where(sel, q2, jnp.zeros_like(q2)))
            dv = dv + _dot_tn(p.astype(BF16), dom)
        dq_ref[...] = dq.astype(BF16)
        dk_acc[win, :] += dk
        dv_acc[win, :] += dv

        @pl.when(i == ni - 1)
        def _():
            dk_ref[...] = dk_acc[ATT_PAD:, :].astype(BF16)
            dv_ref[...] = dv_acc[ATT_PAD:, :].astype(BF16)

        @pl.when((b == batch - 1) & (i == ni - 1))
        def _():
            n_i = lax.broadcasted_iota(jnp.int32, (ATT_Q, BIAS_LEN), 0)
            for e in range(2):
                xw = jnp.concatenate([jnp.zeros((ATT_Q, BIAS_LEN - ATT_WIN), F32), dbias_ref[e]], axis=1)
                for bit in range(8):
                    xw = jnp.where(((n_i >> bit) & 1) == 1, pltpu.roll(xw, BIAS_LEN - (1 << bit), 1), xw)
                dw_ref[e:e + 1, :] = jnp.sum(xw, axis=0, keepdims=True)

    seq_blk = pl.BlockSpec((seq, LANES), lambda hp, b, i: (b, hp))
    q_out = pl.BlockSpec((ATT_Q, LANES), lambda hp, b, i: (b * ni + i, hp))
    return pl.pallas_call(
        body, name="att_bwd", grid=(ATT_HEADS // 2, batch, ni),
        in_specs=[q_spec, kv_spec, kv_spec, w_spec, q_out],
        out_specs=[q_out, seq_blk, seq_blk, w_spec],
        out_shape=[jax.ShapeDtypeStruct((t, 512), BF16)] * 3
                  + [jax.ShapeDtypeStruct((ATT_HEADS // 2, 2, BIAS_LEN), F32)],
        scratch_shapes=[pltpu.VMEM((2, ATT_Q, ATT_WIN), F32), pltpu.VMEM((2, ATT_Q, ATT_WIN), F32),
                        pltpu.VMEM((seq + ATT_PAD, LANES), F32), pltpu.VMEM((seq + ATT_PAD, LANES), F32)],
        compiler_params=_params("arbitrary", "arbitrary", "arbitrary"),
    )(proj, kpad, vpad, wvec, dao)


def _rms_in_bwd(x2, dxn, dh1, g1):
    t = x2.shape[0]
    tm = min(512, t)

    def body(x_ref, d_ref, h_ref, g_ref, dx_ref, dg_ref):
        @pl.when(pl.program_id(0) == 0)
        def _():
            dg_ref[...] = jnp.zeros_like(dg_ref)

        dx, dg_rows = _rms_bwd(x_ref[...], g_ref[...], d_ref[...])
        dx_ref[...] = h_ref[...] + dx
        dg_ref[...] += jnp.sum(dg_rows, axis=0, keepdims=True)

    row = pl.BlockSpec((tm, D_MODEL), lambda i: (i, 0))
    vec = pl.BlockSpec((1, D_MODEL), lambda i: (0, 0))
    return pl.pallas_call(
        body, name="rms_in_bwd", grid=(t // tm,),
        in_specs=[row, row, row, vec], out_specs=[row, vec],
        out_shape=[jax.ShapeDtypeStruct((t, D_MODEL), F32), jax.ShapeDtypeStruct((1, D_MODEL), F32)],
        compiler_params=_params("arbitrary"),
    )(x2, dxn, dh1, g1)


def _pack_small(dg1, dbr, dba, dg2, dg3, dw):
    def body(a_ref, b_ref, c_ref, d_ref, e_ref, w_ref, o_ref):
        o_ref[...] = jnp.zeros_like(o_ref)
        for r, ref in enumerate((a_ref, b_ref, c_ref, d_ref, e_ref)):
            o_ref[r:r + 1, :] = ref[...]
        for hp in range(ATT_HEADS // 2):
            o_ref[8 + 2 * hp:10 + 2 * hp, :] = w_ref[hp]

    return pl.pallas_call(body, name="pack_small",
                          out_shape=jax.ShapeDtypeStruct((16, D_MODEL), F32))(dg1, dbr, dba, dg2, dg3, dw)


def _rotary_tables(seq):
    freqs = ROPE_BASE ** (-jnp.arange(0, RET_KEY_DIM, 2, dtype=F32) / RET_KEY_DIM)
    ang = jnp.arange(seq, dtype=F32)[:, None] * freqs[None, :]
    cos, sin = jnp.cos(ang), jnp.sin(ang)
    return jnp.concatenate([cos, cos], axis=1), jnp.concatenate([-sin, sin], axis=1)


def _bias_rows(rel_bias):
    n_far = BIAS_LEN - ATT_Q - MAX_REL + 1
    n_near = BIAS_LEN - n_far - (N_REL - 2)
    w = jnp.concatenate([jnp.broadcast_to(rel_bias[:, N_REL - 1:], (ATT_HEADS, n_far)),
                         rel_bias[:, 1:N_REL - 1][:, ::-1],
                         jnp.broadcast_to(rel_bias[:, :1], (ATT_HEADS, n_near))], axis=1)
    return w.reshape(ATT_HEADS // 2, 2, BIAS_LEN)


def _bias_rows_bwd(dw):
    n_far = BIAS_LEN - ATT_Q - MAX_REL + 1
    mid = dw[:, n_far:n_far + N_REL - 2][:, ::-1]
    return jnp.concatenate([jnp.sum(dw[:, n_far + N_REL - 2:], axis=1, keepdims=True), mid,
                            jnp.sum(dw[:, :n_far], axis=1, keepdims=True)], axis=1)


def _pad_keys(a, batch, seq):
    a = a.reshape(batch, seq, a.shape[-1])
    return jnp.pad(a, ((0, 0), (ATT_PAD, 0), (0, 0))).reshape(batch * (seq + ATT_PAD), a.shape[-1])


def _local_step(x, tgt, norm_mix, b_gate, norm_ffn, norm_final, rel_bias,
                w_in_t, w_gate_t, w_up_t, w_down, w_ret, w_out, w_att_t):
    batch, seq, _ = x.shape
    t = batch * seq
    x2, tgt2 = x.reshape(t, D_MODEL), tgt.reshape(t, D_MODEL)
    g3 = norm_final.reshape(1, D_MODEL)
    cs, sn = _rotary_tables(seq)
    lg = np.log(1.0 - 2.0 ** (-5.0 - np.arange(RET_HEADS, dtype=np.float32))).astype(np.float32)
    lg_arr = jnp.asarray(np.broadcast_to(lg[:, None, None], (RET_HEADS, 1, LANES)))
    wvec = _bias_rows(rel_bias)

    xn = _rms_fwd(x2, norm_mix)
    proj = _mm(xn, w_in_t, tb=True, out_dtype=BF16, tm=512, tn=512, tk=1024, name="proj")
    gro, o_ret, qr, kr = _ret_fwd(proj, cs, sn, lg_arr, batch, seq)
    kpad = _pad_keys(proj[:, C_AK:C_AV], batch, seq)
    vpad = _pad_keys(proj[:, C_AV:C_GL], batch, seq)
    ao = _att_fwd(proj, kpad, vpad, wvec, batch, seq)
    z, y_ret, y_att = _mix_fwd(gro, ao, proj, b_gate, w_ret, w_att_t)
    h1, hn = _out_fwd(z, x2, w_out, norm_ffn)
    g_act, u_act, a_act = _ffn_up(hn, w_gate_t, w_up_t)
    dh2, dh2b, loss, dg3 = _ffn_down_loss(a_act, h1, tgt2, w_down, g3)

    d_gact, d_uact = _ffn_bwd_act(dh2b, w_down, g_act, u_act)
    dh1, dh1b, dg2 = _ffn_bwd_in(d_gact, d_uact, w_gate_t, w_up_t, h1, dh2, norm_ffn)
    dyr, dya, dglr, dgla, dbr, dba = _mix_bwd(dh1b, w_out, proj, b_gate, y_ret, y_att)
    dgro = _mm(dyr, w_ret, tb=True, out_dtype=BF16, tm=512, tn=512, tk=1024, name="dgro")
    dao = _mm(dya, w_att_t, out_dtype=BF16, tm=512, tn=512, tk=1024, name="dao")
    drq, drk, drv, drg = _ret_bwd(dgro, proj, o_ret, qr, kr, cs, sn, lg_arr, batch, seq)
    daq, dak, dav, dw = _att_bwd(proj, kpad, vpad, wvec, dao, batch, seq)
    dproj = jnp.concatenate([drq, drk, drv, drg, daq, dak, dav, dglr, dgla], axis=1)
    dxn = _mm(dproj, w_in_t, out_dtype=F32, tm=512, tn=512, tk=512, name="dxn")
    dx, dg1 = _rms_in_bwd(x2, dxn, dh1, norm_mix)

    wg = dict(out_dtype=F32, tn=512, tk=1024, ta=True)
    grads = dict(
        w_in_t=_mm(dproj, xn, tm=512, name="dw_in", **wg),
        w_gate_t=_mm(d_gact, hn, tm=256, name="dw_gate", **wg),
        w_up_t=_mm(d_uact, hn, tm=256, name="dw_up", **wg),
        w_down=_mm(a_act, dh2b, tm=256, name="dw_down", **wg),
        w_ret=_mm(gro, dyr, tm=512, name="dw_ret", **wg),
        w_out=_mm(z, dh1b, tm=512, name="dw_out", **wg),
        w_att_t=_mm(dya, ao, tm=512, name="dw_att", **wg),
    )
    small = _pack_small(dg1, dbr, dba, dg2, dg3, dw)
    return loss[0, 0], dx.reshape(batch, seq, D_MODEL), grads, small


def _place():
    return lax.axis_index("x"), lax.axis_index("y"), lax.axis_index("c")


def _peer(k):
    x, y, c = _place()
    return ((1 - x) if k & 4 else x, (1 - y) if k & 2 else y, (1 - c) if k & 1 else c)


def _index(place):
    return 4 * place[0] + 2 * place[1] + place[2]


def _rows(ref, block, nrows):
    align = 16 if ref.dtype == BF16 else 8
    return ref.at[pl.ds(pl.multiple_of(block * nrows, align), nrows)]


def _exchange(arrays, scatter, name):
    n = len(arrays)

    def body(*refs):
        ins, outs = refs[:n], refs[n:2 * n]
        send_sems, recv_sems, local_sems = refs[2 * n:]
        me = _index(_place())

        def src(w, to):
            return _rows(ins[w], to, ins[w].shape[0] // N_DEV) if scatter[w] else ins[w]

        def dst(w, origin):
            return outs[w].at[origin] if scatter[w] else _rows(outs[w], origin, ins[w].shape[0])

        def remote(w, k, to, origin):
            return pltpu.make_async_remote_copy(src_ref=src(w, to), dst_ref=dst(w, origin),
                                                send_sem=send_sems.at[w, k - 1], recv_sem=recv_sems.at[w, k - 1],
                                                device_id=_peer(k), device_id_type=MESH)

        started = []
        for w in range(n):
            own = pltpu.make_async_copy(src(w, me), dst(w, me), local_sems.at[w])
            own.start()
            started.append(own)
        for w in range(n):
            for k in range(1, N_DEV):
                cp = remote(w, k, _index(_peer(k)), me)
                cp.start()
                started.append(cp)
        for w in range(n):
            for k in range(1, N_DEV):
                remote(w, k, me, _index(_peer(k))).wait_recv()
        for cp in started[n:]:
            cp.wait_send()
        for own in started[:n]:
            own.wait()

    def result(a, sc):
        if sc:
            return jax.ShapeDtypeStruct((N_DEV, a.shape[0] // N_DEV) + a.shape[1:], a.dtype)
        return jax.ShapeDtypeStruct((N_DEV * a.shape[0],) + a.shape[1:], a.dtype)

    hbm = pl.BlockSpec(memory_space=pltpu.HBM)
    return pl.pallas_call(
        body, name=name,
        in_specs=[hbm] * n, out_specs=[hbm] * n,
        out_shape=[result(a, sc) for a, sc in zip(arrays, scatter)],
        scratch_shapes=[pltpu.SemaphoreType.DMA((n, N_DEV - 1)), pltpu.SemaphoreType.DMA((n, N_DEV - 1)),
                        pltpu.SemaphoreType.DMA((n,))],
    )(*arrays)


def _sum_slots(slots, name):
    _, r, c = slots.shape
    tr = max(d for d in range(8, r + 1, 8) if r % d == 0 and (d * c <= 128 * 1024 or d == 8))

    def body(s_ref, o_ref):
        acc = s_ref[0]
        for s in range(1, N_DEV):
            acc = acc + s_ref[s]
        o_ref[...] = acc

    return pl.pallas_call(
        body, name=name, grid=(r // tr,),
        in_specs=[pl.BlockSpec((N_DEV, tr, c), lambda i: (0, i, 0))],
        out_specs=pl.BlockSpec((tr, c), lambda i: (i, 0)),
        out_shape=jax.ShapeDtypeStruct((r, c), F32),
        compiler_params=_params("parallel"),
    )(slots)


def _adamw_math(w, g, m, v):
    m = ADAM_B1 * m + (1.0 - ADAM_B1) * g
    v = ADAM_B2 * v + (1.0 - ADAM_B2) * (g * g)
    m_hat = m / (1.0 - ADAM_B1 ** ADAM_STEP)
    v_hat = v / (1.0 - ADAM_B2 ** ADAM_STEP)
    return -ADAM_LR * (m_hat / (jnp.sqrt(v_hat) + ADAM_EPS) + ADAM_WD * w), m, v


def _adamw(w, g, m, v, name):
    r, c = w.shape
    tr = r
    while tr * c > 128 * 1024 and tr % 16 == 0:
        tr //= 2

    def body(w_ref, g_ref, m_ref, v_ref, d_ref, nm_ref, nv_ref):
        d_ref[...], nm_ref[...], nv_ref[...] = _adamw_math(w_ref[...], g_ref[...], m_ref[...], v_ref[...])

    blk = pl.BlockSpec((tr, c), lambda i: (i, 0))
    return pl.pallas_call(
        body, name=name, grid=(r // tr,),
        in_specs=[blk] * 4, out_specs=[blk] * 3,
        out_shape=[jax.ShapeDtypeStruct((r, c), F32)] * 3,
        compiler_params=_params("parallel"),
    )(w, g, m, v)


def _adamw_small(ws, gs, ms, vs):
    n = len(ws)

    def body(*refs):
        for i in range(n):
            w_ref, g_ref, m_ref, v_ref = (refs[j * n + i] for j in range(4))
            d_ref, nm_ref, nv_ref = (refs[(4 + j) * n + i] for j in range(3))
            d_ref[...], nm_ref[...], nv_ref[...] = _adamw_math(w_ref[...], g_ref[...], m_ref[...], v_ref[...])

    shapes = [jax.ShapeDtypeStruct(w.shape, F32) for w in ws]
    outs = pl.pallas_call(body, name="adamw_small", out_shape=shapes * 3)(*ws, *gs, *ms, *vs)
    return outs[:n], outs[n:2 * n], outs[2 * n:]


def kernel(x, norm_mix, w_in, b_gate, rel_bias, w_ret_out, w_att_out, w_out, norm_ffn, w_ffn_gate, w_ffn_up, w_ffn_down, norm_final, loss_target, m_norm_mix, m_w_in, m_b_gate, m_rel_bias, m_w_ret_out, m_w_att_out, m_w_out, m_norm_ffn, m_w_ffn_gate, m_w_ffn_up, m_w_ffn_down, m_norm_final, v_norm_mix, v_w_in, v_b_gate, v_rel_bias, v_w_ret_out, v_w_att_out, v_w_out, v_norm_ffn, v_w_ffn_gate, v_w_ffn_up, v_w_ffn_down, v_norm_final):
    me = _index(_place())
    n_rb = rel_bias.shape[-1]

    names = ("w_in_t", "w_gate_t", "w_up_t", "w_down", "w_ret", "w_out", "w_att_t")
    shards = [w_in[0].T, w_ffn_gate[0].T, w_ffn_up[0].T, w_ffn_down[0], w_ret_out[0], w_out[0], w_att_out[0].T]
    shards = [s.astype(BF16) for s in shards]
    rb_shard = jnp.pad(rel_bias[0], ((0, 0), (0, LANES - n_rb)))
    *full, rb_full = _exchange(shards + [rb_shard], [False] * 8, "gather_weights")
    rb_full = rb_full.reshape(N_DEV, ATT_HEADS, LANES)[:, :, :n_rb]
    rb_full = jnp.transpose(rb_full, (1, 0, 2)).reshape(ATT_HEADS, N_DEV * n_rb)

    loss, dx, grads, small = _local_step(x, loss_target, norm_mix[0:1], b_gate[0:1], norm_ffn[0:1], norm_final,
                                         rb_full, *full)
    loss = lax.psum(loss, ("x", "y", "c"))

    *slots, small_slots = _exchange([grads[nm] for nm in names] + [small], [True] * 7 + [False], "scatter_grads")
    summed = {nm: _sum_slots(s, "sum_" + nm) for nm, s in zip(names, slots)}
    small_sum = _sum_slots(small_slots.reshape(N_DEV, 16, D_MODEL), "sum_small")

    g = dict(
        w_in=summed["w_in_t"].T, w_ffn_gate=summed["w_gate_t"].T, w_ffn_up=summed["w_up_t"].T,
        w_ffn_down=summed["w_down"], w_ret_out=summed["w_ret"], w_out=summed["w_out"], w_att_out=summed["w_att_t"].T,
        norm_mix=small_sum[0:1], b_gate=jnp.concatenate([small_sum[1:2], small_sum[2:3]], axis=1),
        norm_ffn=small_sum[3:4], norm_final=small_sum[4:5],
        rel_bias=lax.dynamic_slice_in_dim(_bias_rows_bwd(small_sum[8:16]), me * n_rb, n_rb, axis=1),
    )
    w = dict(norm_mix=norm_mix, w_in=w_in, b_gate=b_gate, rel_bias=rel_bias, w_ret_out=w_ret_out, w_att_out=w_att_out,
             w_out=w_out, norm_ffn=norm_ffn, w_ffn_gate=w_ffn_gate, w_ffn_up=w_ffn_up, w_ffn_down=w_ffn_down,
             norm_final=norm_final)
    m = dict(norm_mix=m_norm_mix, w_in=m_w_in, b_gate=m_b_gate, rel_bias=m_rel_bias, w_ret_out=m_w_ret_out,
             w_att_out=m_w_att_out, w_out=m_w_out, norm_ffn=m_norm_ffn, w_ffn_gate=m_w_ffn_gate, w_ffn_up=m_w_ffn_up,
             w_ffn_down=m_w_ffn_down, norm_final=m_norm_final)
    v = dict(norm_mix=v_norm_mix, w_in=v_w_in, b_gate=v_b_gate, rel_bias=v_rel_bias, w_ret_out=v_w_ret_out,
             w_att_out=v_w_att_out, w_out=v_w_out, norm_ffn=v_norm_ffn, w_ffn_gate=v_w_ffn_gate, w_ffn_up=v_w_ffn_up,
             w_ffn_down=v_w_ffn_down, norm_final=v_norm_final)
    order = ("norm_mix", "w_in", "b_gate", "rel_bias", "w_ret_out", "w_att_out", "w_out", "norm_ffn",
             "w_ffn_gate", "w_ffn_up", "w_ffn_down", "norm_final")
    small_names = ("norm_mix", "b_gate", "rel_bias", "norm_ffn", "norm_final")

    def flat(a):
        return a[0] if a.ndim == 3 else a.reshape(-1, a.shape[-1])

    grad, delta, new_m, new_v = {}, {}, {}, {}
    for nm in order:
        if nm not in small_names:
            d, nmom, nvar = _adamw(flat(w[nm]), g[nm], flat(m[nm]), flat(v[nm]), "adamw_" + nm)
            grad[nm], delta[nm], new_m[nm], new_v[nm] = (a.reshape(w[nm].shape) for a in (g[nm], d, nmom, nvar))
    ds, nms, nvs = _adamw_small([flat(w[nm]) for nm in small_names], [g[nm] for nm in small_names],
                                [flat(m[nm]) for nm in small_names], [flat(v[nm]) for nm in small_names])
    for i, nm in enumerate(small_names):
        grad[nm], delta[nm], new_m[nm], new_v[nm] = (a.reshape(w[nm].shape) for a in (g[nm], ds[i], nms[i], nvs[i]))

    return (loss, dx, *[grad[nm] for nm in order], *[delta[nm] for nm in order],
            *[new_m[nm] for nm in order], *[new_v[nm] for nm in order])
```

```python
import numpy as np
import jax
import jax.numpy as jnp
from jax import lax
from jax.experimental import pallas as pl
from jax.experimental.pallas import tpu as pltpu

F32 = jnp.float32
BF16 = jnp.bfloat16
MESH = pl.DeviceIdType.MESH

D_MODEL = 1024
CHUNK = 64
RET_HEADS = 4
RET_KEY_DIM = 128
RET_VAL_DIM = 256
ATT_HEADS = 8
BAND_CHUNKS = 8
MAX_REL = 256
N_REL = CHUNK + MAX_REL
D_FF = 2816
N_IN = 6656
ROPE_BASE = 10000.0
EPS = 1e-6
NEG_INF = -1e30
C_RQ, C_RK, C_RV, C_RG, C_AQ, C_AK, C_AV, C_GL = 0, 512, 1024, 2048, 3072, 3584, 4096, 4608

ADAM_LR = 0.001
ADAM_B1 = 0.9
ADAM_B2 = 0.999
ADAM_EPS = 1e-08
ADAM_WD = 0.01
ADAM_STEP = 10

N_DEV = 8
LANES = 128
RET_TILE = 256
ATT_Q = 256
ATT_PAD = BAND_CHUNKS * CHUNK
ATT_WIN = ATT_PAD + ATT_Q
BIAS_LEN = 1024
VMEM_LIMIT = 48 * 1024 * 1024


def _params(*sem):
    return pltpu.CompilerParams(dimension_semantics=sem, vmem_limit_bytes=VMEM_LIMIT)


def _dot(a, b):
    return lax.dot_general(a, b, (((1,), (0,)), ((), ())), preferred_element_type=F32)


def _dot_nt(a, b):
    return lax.dot_general(a, b, (((1,), (1,)), ((), ())), preferred_element_type=F32)


def _dot_tn(a, b):
    return lax.dot_general(a, b, (((0,), (0,)), ((), ())), preferred_element_type=F32)


def _sigmoid(x):
    return 1.0 / (1.0 + jnp.exp(-x))


def _rms_bwd(x, g, dy):
    r = lax.rsqrt(jnp.mean(x * x, axis=-1, keepdims=True) + EPS)
    u = dy * g
    dx = r * u - x * (r * r * r) * jnp.mean(u * x, axis=-1, keepdims=True)
    return dx, dy * x * r


def _place():
    return lax.axis_index("x"), lax.axis_index("y"), lax.axis_index("c")


def _peer(k):
    x, y, c = _place()
    return ((1 - x) if k & 4 else x, (1 - y) if k & 2 else y, (1 - c) if k & 1 else c)


def _index(place):
    return 4 * place[0] + 2 * place[1] + place[2]


def _rows(ref, block, nrows):
    align = 16 if ref.dtype == BF16 else 8
    return ref.at[pl.ds(pl.multiple_of(block * nrows, align), nrows)]


class _Exchange:
    def __init__(self, arrays, scatter):
        self.arrays, self.scatter, self.n = list(arrays), scatter, len(arrays)

    def out_shape(self):
        if self.scatter:
            return [jax.ShapeDtypeStruct((N_DEV, a.shape[0] // N_DEV) + a.shape[1:], a.dtype) for a in self.arrays]
        return [jax.ShapeDtypeStruct((N_DEV * a.shape[0],) + a.shape[1:], a.dtype) for a in self.arrays]

    def scratch(self):
        return [pltpu.SemaphoreType.DMA((self.n, N_DEV - 1)), pltpu.SemaphoreType.DMA((self.n, N_DEV - 1)),
                pltpu.SemaphoreType.DMA((self.n,))]

    def _copies(self, ins, outs, sems):
        send_sems, recv_sems, local_sems = sems
        me = _index(_place())

        def src(w, to):
            return _rows(ins[w], to, ins[w].shape[0] // N_DEV) if self.scatter else ins[w]

        def dst(w, origin):
            return outs[w].at[origin] if self.scatter else _rows(outs[w], origin, ins[w].shape[0])

        def remote(w, k, to, origin):
            return pltpu.make_async_remote_copy(src_ref=src(w, to), dst_ref=dst(w, origin),
                                                send_sem=send_sems.at[w, k - 1], recv_sem=recv_sems.at[w, k - 1],
                                                device_id=_peer(k), device_id_type=MESH)

        pairs = [(w, k) for w in range(self.n) for k in range(1, N_DEV)]
        own = [pltpu.make_async_copy(src(w, me), dst(w, me), local_sems.at[w]) for w in range(self.n)]
        sent = [remote(w, k, _index(_peer(k)), me) for w, k in pairs]
        arriving = [remote(w, k, me, _index(_peer(k))) for w, k in pairs]
        return own, sent, arriving

    def start(self, ins, outs, sems):
        own, sent, _ = self._copies(ins, outs, sems)
        for cp in own + sent:
            cp.start()

    def wait(self, ins, outs, sems):
        own, sent, arriving = self._copies(ins, outs, sems)
        for cp in arriving:
            cp.wait_recv()
        for cp in sent:
            cp.wait_send()
        for cp in own:
            cp.wait()


def _call(body, *, name, grid, in_specs, out_specs, out_shape, scratch=(), semantics, args, exchange=None):
    if exchange is None:
        return pl.pallas_call(body, name=name, grid=grid, in_specs=in_specs, out_specs=out_specs, out_shape=out_shape,
                              scratch_shapes=list(scratch), compiler_params=_params(*semantics))(*args), None
    n_in, n_out, n_scr, nx = len(in_specs), len(out_specs), len(scratch), exchange.n

    def full_body(*refs):
        ins, refs = refs[:n_in], refs[n_in:]
        x_in, refs = refs[:nx], refs[nx:]
        outs, refs = refs[:n_out], refs[n_out:]
        x_out, refs = refs[:nx], refs[nx:]
        scr, sems = refs[:n_scr], refs[n_scr:]
        first, last = True, True
        for axis, size in enumerate(grid):
            first = jnp.logical_and(first, pl.program_id(axis) == 0)
            last = jnp.logical_and(last, pl.program_id(axis) == size - 1)
        if grid:
            pl.when(first)(lambda: exchange.start(x_in, x_out, sems))
        else:
            exchange.start(x_in, x_out, sems)
        body(*ins, *outs, *scr)
        if grid:
            pl.when(last)(lambda: exchange.wait(x_in, x_out, sems))
        else:
            exchange.wait(x_in, x_out, sems)

    hbm = pl.BlockSpec(memory_space=pltpu.HBM)
    res = pl.pallas_call(
        full_body, name=name, grid=grid,
        in_specs=list(in_specs) + [hbm] * nx, out_specs=list(out_specs) + [hbm] * nx,
        out_shape=list(out_shape) + exchange.out_shape(),
        scratch_shapes=list(scratch) + exchange.scratch(),
        compiler_params=_params(*(["arbitrary"] * len(grid))),
    )(*args, *exchange.arrays)
    return res[:n_out], res[n_out:]


def _exchange(arrays, scatter, name):
    return _call(lambda: None, name=name, grid=(), in_specs=[], out_specs=[], out_shape=[], semantics=(),
                 args=(), exchange=_Exchange(arrays, scatter))[1]


def _mm(a, b, *, ta=False, tb=False, out_dtype, tm, tn, tk, name, exchange=None):
    m, k = (a.shape[1], a.shape[0]) if ta else a.shape
    n = b.shape[0] if tb else b.shape[1]
    assert k == (b.shape[1] if tb else b.shape[0])
    tm, tn, tk = min(tm, m), min(tn, n), min(tk, k)
    assert m % tm == 0 and n % tn == 0 and k % tk == 0, (name, m, n, k)
    nk = k // tk
    dims = (((0 if ta else 1,), (1 if tb else 0,)), ((), ()))

    def body(a_ref, b_ref, o_ref, acc_ref):
        kk = pl.program_id(2)

        @pl.when(kk == 0)
        def _():
            acc_ref[...] = jnp.zeros_like(acc_ref)

        acc_ref[...] += lax.dot_general(a_ref[...].astype(BF16), b_ref[...].astype(BF16), dims,
                                        preferred_element_type=F32)

        @pl.when(kk == nk - 1)
        def _():
            o_ref[...] = acc_ref[...].astype(o_ref.dtype)

    a_spec = (pl.BlockSpec((tk, tm), lambda i, j, kk: (kk, i)) if ta
              else pl.BlockSpec((tm, tk), lambda i, j, kk: (i, kk)))
    b_spec = (pl.BlockSpec((tn, tk), lambda i, j, kk: (j, kk)) if tb
              else pl.BlockSpec((tk, tn), lambda i, j, kk: (kk, j)))
    (out,), moved = _call(
        body, name=name, grid=(m // tm, n // tn, nk),
        in_specs=[a_spec, b_spec],
        out_specs=[pl.BlockSpec((tm, tn), lambda i, j, kk: (i, j))],
        out_shape=[jax.ShapeDtypeStruct((m, n), out_dtype)],
        scratch=[pltpu.VMEM((tm, tn), F32)],
        semantics=("parallel", "parallel", "arbitrary"), args=(a, b), exchange=exchange)
    return out if exchange is None else (out, moved)


def _rms_fwd(x2, g):
    t = x2.shape[0]
    tm = min(512, t)

    def body(x_ref, g_ref, o_ref):
        x = x_ref[...]
        r = lax.rsqrt(jnp.mean(x * x, axis=-1, keepdims=True) + EPS)
        o_ref[...] = (x * r * g_ref[...]).astype(o_ref.dtype)

    return pl.pallas_call(
        body, name="rms_in_fwd", grid=(t // tm,),
        in_specs=[pl.BlockSpec((tm, D_MODEL), lambda i: (i, 0)), pl.BlockSpec((1, D_MODEL), lambda i: (0, 0))],
        out_specs=pl.BlockSpec((tm, D_MODEL), lambda i: (i, 0)),
        out_shape=jax.ShapeDtypeStruct((t, D_MODEL), BF16),
        compiler_params=_params("parallel"),
    )(x2, g)


def _decay_tiles(lg):
    row = lax.broadcasted_iota(jnp.int32, (RET_TILE, RET_TILE), 0)
    col = lax.broadcasted_iota(jnp.int32, (RET_TILE, RET_TILE), 1)
    diff = (row - col).astype(F32)
    within = jnp.exp(lg * jnp.abs(diff))
    e_qk = jnp.exp(lg * diff)
    d_qk = jnp.where((col >> 6) <= (row >> 6), within, 0.0)
    return e_qk, d_qk, row, col, diff, within


def _ret_fwd(proj, cs, sn, lg_arr, batch, seq, exchange):
    t = batch * seq
    nt = seq // RET_TILE

    def body(q_ref, k_ref, v_ref, rg_ref, cs_ref, sn_ref, lg_ref, gro_ref, o_ref, qr_ref, kr_ref):
        lg = lg_ref[:, 0:1]
        cs_t, sn_t = cs_ref[...], sn_ref[...]
        q = q_ref[...].astype(F32)
        k = k_ref[...].astype(F32)
        qr_ref[...] = (q * cs_t + pltpu.roll(q, 64, 1) * sn_t).astype(BF16)
        kr_ref[...] = ((k * cs_t + pltpu.roll(k, 64, 1) * sn_t) * (RET_KEY_DIM ** -0.5)).astype(BF16)
        e_qk, d_qk = _decay_tiles(lg)[:2]
        for i in range(nt):
            rows = slice(i * RET_TILE, (i + 1) * RET_TILE)
            qi = qr_ref[rows, :]

            def pair(j, acc, i=i, qi=qi):
                off = pl.multiple_of(j * RET_TILE, RET_TILE)
                s = _dot_nt(qi, kr_ref[pl.ds(off, RET_TILE), :])
                far = jnp.exp(lg * (RET_TILE * (i - j)).astype(F32))
                p = (s * (e_qk * far)).astype(BF16)
                return acc + _dot(p, v_ref[pl.ds(off, RET_TILE), :])

            acc = lax.fori_loop(0, i, pair, jnp.zeros((RET_TILE, RET_VAL_DIM), F32))
            s = _dot_nt(qi, kr_ref[rows, :])
            acc = acc + _dot((s * d_qk).astype(BF16), v_ref[rows, :])
            o_ref[rows, :] = acc
            xc = acc - jnp.mean(acc, axis=-1, keepdims=True)
            nrm = xc * lax.rsqrt(jnp.mean(xc * xc, axis=-1, keepdims=True) + EPS)
            rg = rg_ref[rows, :].astype(F32)
            gro_ref[rows, :] = (rg * _sigmoid(rg) * nrm).astype(BF16)

    def col(base, width):
        return lambda b, h: (b, base // width + h)

    return _call(
        body, name="ret_fwd", grid=(batch, RET_HEADS),
        in_specs=[pl.BlockSpec((seq, RET_KEY_DIM), col(C_RQ, RET_KEY_DIM)),
                  pl.BlockSpec((seq, RET_KEY_DIM), col(C_RK, RET_KEY_DIM)),
                  pl.BlockSpec((seq, RET_VAL_DIM), col(C_RV, RET_VAL_DIM)),
                  pl.BlockSpec((seq, RET_VAL_DIM), col(C_RG, RET_VAL_DIM)),
                  pl.BlockSpec((seq, RET_KEY_DIM), lambda b, h: (0, 0)),
                  pl.BlockSpec((seq, RET_KEY_DIM), lambda b, h: (0, 0)),
                  pl.BlockSpec((None, 1, LANES), lambda b, h: (h, 0, 0))],
        out_specs=[pl.BlockSpec((seq, RET_VAL_DIM), lambda b, h: (b, h)),
                   pl.BlockSpec((seq, RET_VAL_DIM), lambda b, h: (b, h)),
                   pl.BlockSpec((seq, RET_KEY_DIM), lambda b, h: (b, h)),
                   pl.BlockSpec((seq, RET_KEY_DIM), lambda b, h: (b, h))],
        out_shape=[jax.ShapeDtypeStruct((t, RET_HEADS * RET_VAL_DIM), BF16),
                   jax.ShapeDtypeStruct((t, RET_HEADS * RET_VAL_DIM), F32),
                   jax.ShapeDtypeStruct((t, RET_HEADS * RET_KEY_DIM), BF16),
                   jax.ShapeDtypeStruct((t, RET_HEADS * RET_KEY_DIM), BF16)],
        semantics=("parallel", "parallel"), args=(proj, proj, proj, proj, cs, sn, lg_arr), exchange=exchange)


def _att_bias(w_ref, bias_ref):
    n_i = lax.broadcasted_iota(jnp.int32, (ATT_Q, BIAS_LEN), 0)
    qc = lax.broadcasted_iota(jnp.int32, (ATT_Q, ATT_WIN), 0) >> 6
    kc = lax.broadcasted_iota(jnp.int32, (ATT_Q, ATT_WIN), 1) >> 6
    dc = qc + BAND_CHUNKS - kc
    band = (dc >= 0) & (dc <= BAND_CHUNKS)
    for e in range(2):
        xw = jnp.broadcast_to(w_ref[e:e + 1, :], (ATT_Q, BIAS_LEN))
        for bit in range(8):
            xw = jnp.where(((n_i >> bit) & 1) == 1, pltpu.roll(xw, 1 << bit, 1), xw)
        bias_ref[e] = jnp.where(band, xw[:, BIAS_LEN - ATT_WIN:], NEG_INF)


def _att_specs(batch, seq):
    ni = seq // ATT_Q
    q_spec = pl.BlockSpec((ATT_Q, LANES), lambda hp, b, i: (b * ni + i, C_AQ // LANES + hp))
    kv_spec = pl.BlockSpec((seq + ATT_PAD, LANES), lambda hp, b, i: (b, hp))
    w_spec = pl.BlockSpec((None, 2, BIAS_LEN), lambda hp, b, i: (hp, 0, 0))
    return ni, q_spec, kv_spec, w_spec


def _att_scores(q2, k2, bias, sel, start_ok):
    qm = jnp.where(sel, q2, jnp.zeros_like(q2))
    s = _dot_nt(qm, k2) * (1.0 / 8.0) + bias
    s = jnp.where(start_ok, s, NEG_INF)
    p = jnp.exp(s - jnp.max(s, axis=-1, keepdims=True))
    return p / jnp.sum(p, axis=-1, keepdims=True)


def _att_fwd(proj, kpad, vpad, wvec, batch, seq):
    ni, q_spec, kv_spec, w_spec = _att_specs(batch, seq)

    def body(q_ref, k_ref, v_ref, w_ref, o_ref, bias_ref):
        b, i = pl.program_id(1), pl.program_id(2)

        @pl.when((b == 0) & (i == 0))
        def _():
            _att_bias(w_ref, bias_ref)

        win = pl.ds(pl.multiple_of(i * ATT_Q, ATT_Q), ATT_WIN)
        k2, v2, q2 = k_ref[win, :], v_ref[win, :], q_ref[...]
        start_ok = lax.broadcasted_iota(jnp.int32, (ATT_Q, ATT_WIN), 1) + (i * ATT_Q - ATT_PAD) >= 0
        lo = lax.broadcasted_iota(jnp.int32, (1, LANES), 1) < 64
        out = jnp.zeros((ATT_Q, LANES), F32)
        for e in range(2):
            sel = lo if e == 0 else jnp.logical_not(lo)
            p = _att_scores(q2, k2, bias_ref[e], sel, start_ok)
            out = out + _dot(p.astype(BF16), jnp.where(sel, v2, jnp.zeros_like(v2)))
        o_ref[...] = out.astype(BF16)

    return pl.pallas_call(
        body, name="att_fwd", grid=(ATT_HEADS // 2, batch, ni),
        in_specs=[q_spec, kv_spec, kv_spec, w_spec],
        out_specs=pl.BlockSpec((ATT_Q, LANES), lambda hp, b, i: (b * ni + i, hp)),
        out_shape=jax.ShapeDtypeStruct((batch * seq, ATT_HEADS * 64), BF16),
        scratch_shapes=[pltpu.VMEM((2, ATT_Q, ATT_WIN), F32)],
        compiler_params=_params("arbitrary", "arbitrary", "arbitrary"),
    )(proj, kpad, vpad, wvec)


def _mix_fwd(gro, ao, proj, b_gate, w_ret, w_att_t):
    t = gro.shape[0]
    tm, tn = min(256, t), 512

    def body(gro_ref, ao_ref, glr_ref, gla_ref, br_ref, ba_ref, wr_ref, wa_ref, z_ref, yr_ref, ya_ref):
        yr = _dot(gro_ref[...], wr_ref[...])
        ya = _dot_nt(ao_ref[...], wa_ref[...])
        gr = _sigmoid(glr_ref[...].astype(F32) + br_ref[...])
        ga = _sigmoid(gla_ref[...].astype(F32) + ba_ref[...])
        z_ref[...] = (gr * yr + ga * ya).astype(BF16)
        yr_ref[...] = yr.astype(BF16)
        ya_ref[...] = ya.astype(BF16)

    nb = D_MODEL // tn
    out = pl.BlockSpec((tm, tn), lambda i, j: (i, j))
    return pl.pallas_call(
        body, name="mix_fwd", grid=(t // tm, nb),
        in_specs=[pl.BlockSpec((tm, D_MODEL), lambda i, j: (i, 0)),
                  pl.BlockSpec((tm, 512), lambda i, j: (i, 0)),
                  pl.BlockSpec((tm, tn), lambda i, j: (i, C_GL // tn + j)),
                  pl.BlockSpec((tm, tn), lambda i, j: (i, C_GL // tn + nb + j)),
                  pl.BlockSpec((1, tn), lambda i, j: (0, j)),
                  pl.BlockSpec((1, tn), lambda i, j: (0, nb + j)),
                  pl.BlockSpec((D_MODEL, tn), lambda i, j: (0, j)),
                  pl.BlockSpec((tn, 512), lambda i, j: (j, 0))],
        out_specs=[out, out, out],
        out_shape=[jax.ShapeDtypeStruct((t, D_MODEL), BF16)] * 3,
        compiler_params=_params("parallel", "parallel"),
    )(gro, ao, proj, proj, b_gate, b_gate, w_ret, w_att_t)


def _out_fwd(z, x2, w_out, g2):
    t = z.shape[0]
    tm = min(256, t)

    def body(z_ref, x_ref, w_ref, g_ref, h_ref, hn_ref):
        h = x_ref[...] + _dot(z_ref[...], w_ref[...])
        h_ref[...] = h
        r = lax.rsqrt(jnp.mean(h * h, axis=-1, keepdims=True) + EPS)
        hn_ref[...] = (h * r * g_ref[...]).astype(BF16)

    row = pl.BlockSpec((tm, D_MODEL), lambda i: (i, 0))
    return pl.pallas_call(
        body, name="out_fwd", grid=(t // tm,),
        in_specs=[row, row, pl.BlockSpec((D_MODEL, D_MODEL), lambda i: (0, 0)),
                  pl.BlockSpec((1, D_MODEL), lambda i: (0, 0))],
        out_specs=[row, row],
        out_shape=[jax.ShapeDtypeStruct((t, D_MODEL), F32), jax.ShapeDtypeStruct((t, D_MODEL), BF16)],
        compiler_params=_params("parallel"),
    )(z, x2, w_out, g2)


def _ffn_up(hn, wg_t, wu_t):
    t = hn.shape[0]
    tm, tn = min(512, t), 256

    def body(h_ref, wg_ref, wu_ref, g_ref, u_ref, a_ref):
        g = _dot_nt(h_ref[...], wg_ref[...])
        u = _dot_nt(h_ref[...], wu_ref[...])
        g_ref[...] = g.astype(BF16)
        u_ref[...] = u.astype(BF16)
        a_ref[...] = (g * _sigmoid(g) * u).astype(BF16)

    w_spec = pl.BlockSpec((tn, D_MODEL), lambda i, j: (j, 0))
    out = pl.BlockSpec((tm, tn), lambda i, j: (i, j))
    return pl.pallas_call(
        body, name="ffn_up", grid=(t // tm, D_FF // tn),
        in_specs=[pl.BlockSpec((tm, D_MODEL), lambda i, j: (i, 0)), w_spec, w_spec],
        out_specs=[out, out, out],
        out_shape=[jax.ShapeDtypeStruct((t, D_FF), BF16)] * 3,
        compiler_params=_params("parallel", "parallel"),
    )(hn, wg_t, wu_t)


def _ffn_down_loss(a, h1, tgt, w_down, g3):
    t = a.shape[0]
    tm = min(256, t)

    def body(a_ref, h_ref, t_ref, w_ref, g_ref, dh_ref, dhb_ref, loss_ref, dg_ref):
        @pl.when(pl.program_id(0) == 0)
        def _():
            loss_ref[...] = jnp.zeros_like(loss_ref)
            dg_ref[...] = jnp.zeros_like(dg_ref)

        g = g_ref[...]
        h2 = h_ref[...] + _dot(a_ref[...], w_ref[...])
        r = lax.rsqrt(jnp.mean(h2 * h2, axis=-1, keepdims=True) + EPS)
        err = h2 * r * g - t_ref[...]
        loss_ref[...] += jnp.sum(err * err) * (0.5 / D_MODEL)
        dy = err * (1.0 / D_MODEL)
        dh, dg_rows = _rms_bwd(h2, g, dy)
        dg_ref[...] += jnp.sum(dg_rows, axis=0, keepdims=True)
        dh_ref[...] = dh
        dhb_ref[...] = dh.astype(BF16)

    row = pl.BlockSpec((tm, D_MODEL), lambda i: (i, 0))
    vec = pl.BlockSpec((1, D_MODEL), lambda i: (0, 0))
    return pl.pallas_call(
        body, name="ffn_down_loss", grid=(t // tm,),
        in_specs=[pl.BlockSpec((tm, D_FF), lambda i: (i, 0)), row, row,
                  pl.BlockSpec((D_FF, D_MODEL), lambda i: (0, 0)), vec],
        out_specs=[row, row, pl.BlockSpec((1, LANES), lambda i: (0, 0)), vec],
        out_shape=[jax.ShapeDtypeStruct((t, D_MODEL), F32), jax.ShapeDtypeStruct((t, D_MODEL), BF16),
                   jax.ShapeDtypeStruct((1, LANES), F32), jax.ShapeDtypeStruct((1, D_MODEL), F32)],
        compiler_params=_params("arbitrary"),
    )(a, h1, tgt, w_down, g3)


def _ffn_bwd_act(dh2b, w_down, g_act, u_act, exchange):
    t = dh2b.shape[0]
    tm, tn = min(512, t), 256

    def body(d_ref, w_ref, g_ref, u_ref, dg_ref, du_ref):
        da = _dot_nt(d_ref[...], w_ref[...])
        g = g_ref[...].astype(F32)
        u = u_ref[...].astype(F32)
        sg = _sigmoid(g)
        dg_ref[...] = (da * u * sg * (1.0 + g * (1.0 - sg))).astype(BF16)
        du_ref[...] = (da * g * sg).astype(BF16)

    blk = pl.BlockSpec((tm, tn), lambda i, j: (i, j))
    return _call(
        body, name="ffn_bwd_act", grid=(t // tm, D_FF // tn),
        in_specs=[pl.BlockSpec((tm, D_MODEL), lambda i, j: (i, 0)),
                  pl.BlockSpec((tn, D_MODEL), lambda i, j: (j, 0)), blk, blk],
        out_specs=[blk, blk],
        out_shape=[jax.ShapeDtypeStruct((t, D_FF), BF16)] * 2,
        semantics=("parallel", "parallel"), args=(dh2b, w_down, g_act, u_act), exchange=exchange)


def _ffn_bwd_in(dg, du, wg_t, wu_t, h1, dh2, g2, exchange):
    t = dg.shape[0]
    tm, tk = min(512, t), D_FF // 2
    nk = D_FF // tk

    def body(dg_ref, du_ref, wg_ref, wu_ref, h_ref, d2_ref, g_ref, dh_ref, dhb_ref, gn_ref, acc_ref):
        i, kk = pl.program_id(0), pl.program_id(1)

        @pl.when((i == 0) & (kk == 0))
        def _():
            gn_ref[...] = jnp.zeros_like(gn_ref)

        @pl.when(kk == 0)
        def _():
            acc_ref[...] = jnp.zeros_like(acc_ref)

        acc_ref[...] += _dot(dg_ref[...], wg_ref[...]) + _dot(du_ref[...], wu_ref[...])

        @pl.when(kk == nk - 1)
        def _():
            dx, dg_rows = _rms_bwd(h_ref[...], g_ref[...], acc_ref[...])
            dh = d2_ref[...] + dx
            dh_ref[...] = dh
            dhb_ref[...] = dh.astype(BF16)
            gn_ref[...] += jnp.sum(dg_rows, axis=0, keepdims=True)

    act = pl.BlockSpec((tm, tk), lambda i, kk: (i, kk))
    wsp = pl.BlockSpec((tk, D_MODEL), lambda i, kk: (kk, 0))
    row = pl.BlockSpec((tm, D_MODEL), lambda i, kk: (i, 0))
    vec = pl.BlockSpec((1, D_MODEL), lambda i, kk: (0, 0))
    return _call(
        body, name="ffn_bwd_in", grid=(t // tm, nk),
        in_specs=[act, act, wsp, wsp, row, row, vec],
        out_specs=[row, row, vec],
        out_shape=[jax.ShapeDtypeStruct((t, D_MODEL), F32), jax.ShapeDtypeStruct((t, D_MODEL), BF16),
                   jax.ShapeDtypeStruct((1, D_MODEL), F32)],
        scratch=[pltpu.VMEM((tm, D_MODEL), F32)],
        semantics=("arbitrary", "arbitrary"), args=(dg, du, wg_t, wu_t, h1, dh2, g2), exchange=exchange)


def _mix_bwd(dh1b, w_out, proj, b_gate, y_ret, y_att, exchange):
    t = dh1b.shape[0]
    tm, tn = min(256, t), 512
    nb = D_MODEL // tn

    def body(d_ref, w_ref, glr_ref, gla_ref, br_ref, ba_ref, yr_ref, ya_ref,
             dyr_ref, dya_ref, dglr_ref, dgla_ref, dbr_ref, dba_ref):
        @pl.when(pl.program_id(1) == 0)
        def _():
            dbr_ref[...] = jnp.zeros_like(dbr_ref)
            dba_ref[...] = jnp.zeros_like(dba_ref)

        dz = _dot_nt(d_ref[...], w_ref[...])
        gr = _sigmoid(glr_ref[...].astype(F32) + br_ref[...])
        ga = _sigmoid(gla_ref[...].astype(F32) + ba_ref[...])
        dyr_ref[...] = (dz * gr).astype(BF16)
        dya_ref[...] = (dz * ga).astype(BF16)
        dglr = dz * yr_ref[...].astype(F32) * gr * (1.0 - gr)
        dgla = dz * ya_ref[...].astype(F32) * ga * (1.0 - ga)
        dglr_ref[...] = dglr.astype(BF16)
        dgla_ref[...] = dgla.astype(BF16)
        dbr_ref[...] += jnp.sum(dglr, axis=0, keepdims=True)
        dba_ref[...] += jnp.sum(dgla, axis=0, keepdims=True)

    blk = pl.BlockSpec((tm, tn), lambda j, i: (i, j))
    vec = pl.BlockSpec((1, tn), lambda j, i: (0, j))
    return _call(
        body, name="mix_bwd", grid=(nb, t // tm),
        in_specs=[pl.BlockSpec((tm, D_MODEL), lambda j, i: (i, 0)),
                  pl.BlockSpec((tn, D_MODEL), lambda j, i: (j, 0)),
                  pl.BlockSpec((tm, tn), lambda j, i: (i, C_GL // tn + j)),
                  pl.BlockSpec((tm, tn), lambda j, i: (i, C_GL // tn + nb + j)),
                  vec, pl.BlockSpec((1, tn), lambda j, i: (0, nb + j)), blk, blk],
        out_specs=[blk, blk, blk, blk, vec, vec],
        out_shape=[jax.ShapeDtypeStruct((t, D_MODEL), BF16)] * 4 + [jax.ShapeDtypeStruct((1, D_MODEL), F32)] * 2,
        semantics=("arbitrary", "arbitrary"), args=(dh1b, w_out, proj, proj, b_gate, b_gate, y_ret, y_att),
        exchange=exchange)


def _ret_bwd(dgro, proj, o_ret, qr, kr, cs, sn, lg_arr, batch, seq, exchange):
    t = batch * seq
    nt = seq // RET_TILE

    def body(dgro_ref, rg_ref, o_ref, qr_ref, kr_ref, v_ref, cs_ref, sn_ref, lg_ref,
             dq_ref, dk_ref, dv_ref, drg_ref, do_ref):
        lg = lg_ref[:, 0:1]
        e_qk, d_qk, row, col, diff, within = _decay_tiles(lg)
        e_kq = jnp.exp(-lg * diff)
        d_kq = jnp.where((row >> 6) <= (col >> 6), within, 0.0)

        for i in range(nt):
            rows = slice(i * RET_TILE, (i + 1) * RET_TILE)
            o = o_ref[rows, :]
            xc = o - jnp.mean(o, axis=-1, keepdims=True)
            rs = lax.rsqrt(jnp.mean(xc * xc, axis=-1, keepdims=True) + EPS)
            nrm = xc * rs
            rg = rg_ref[rows, :].astype(F32)
            sg = _sigmoid(rg)
            dg = dgro_ref[rows, :].astype(F32)
            drg_ref[rows, :] = (dg * nrm * sg * (1.0 + rg * (1.0 - sg))).astype(BF16)
            dn = dg * rg * sg
            do = rs * (dn - jnp.mean(dn, axis=-1, keepdims=True)
                       - nrm * jnp.mean(dn * nrm, axis=-1, keepdims=True))
            do_ref[rows, :] = do.astype(BF16)

        for i in range(nt):
            rows = slice(i * RET_TILE, (i + 1) * RET_TILE)
            doi = do_ref[rows, :]

            def pair_q(j, acc, i=i, doi=doi):
                off = pl.multiple_of(j * RET_TILE, RET_TILE)
                dp = _dot_nt(doi, v_ref[pl.ds(off, RET_TILE), :])
                far = jnp.exp(lg * (RET_TILE * (i - j)).astype(F32))
                return acc + _dot((dp * (e_qk * far)).astype(BF16), kr_ref[pl.ds(off, RET_TILE), :])

            acc = lax.fori_loop(0, i, pair_q, jnp.zeros((RET_TILE, RET_KEY_DIM), F32))
            dp = _dot_nt(doi, v_ref[rows, :])
            acc = acc + _dot((dp * d_qk).astype(BF16), kr_ref[rows, :])
            dq_ref[rows, :] = (acc * cs_ref[rows, :] - pltpu.roll(acc, 64, 1) * sn_ref[rows, :]).astype(BF16)

        for j in range(nt):
            rows = slice(j * RET_TILE, (j + 1) * RET_TILE)
            kj, vj = kr_ref[rows, :], v_ref[rows, :]

            def pair_k(i, carry, j=j, kj=kj, vj=vj):
                dk, dv = carry
                off = pl.multiple_of(i * RET_TILE, RET_TILE)
                qi, doi = qr_ref[pl.ds(off, RET_TILE), :], do_ref[pl.ds(off, RET_TILE), :]
                dec = e_kq * jnp.exp(lg * (RET_TILE * (i - j)).astype(F32))
                dk = dk + _dot((_dot_nt(vj, doi) * dec).astype(BF16), qi)
                dv = dv + _dot((_dot_nt(kj, qi) * dec).astype(BF16), doi)
                return dk, dv

            dk, dv = lax.fori_loop(j + 1, nt, pair_k, (jnp.zeros((RET_TILE, RET_KEY_DIM), F32),
                                                        jnp.zeros((RET_TILE, RET_VAL_DIM), F32)))
            qi, doi = qr_ref[rows, :], do_ref[rows, :]
            dk = dk + _dot((_dot_nt(vj, doi) * d_kq).astype(BF16), qi)
            dv = dv + _dot((_dot_nt(kj, qi) * d_kq).astype(BF16), doi)
            dk = (dk * cs_ref[rows, :] - pltpu.roll(dk, 64, 1) * sn_ref[rows, :]) * (RET_KEY_DIM ** -0.5)
            dk_ref[rows, :] = dk.astype(BF16)
            dv_ref[rows, :] = dv.astype(BF16)

    key = pl.BlockSpec((seq, RET_KEY_DIM), lambda b, h: (b, h))
    val = pl.BlockSpec((seq, RET_VAL_DIM), lambda b, h: (b, h))
    tab = pl.BlockSpec((seq, RET_KEY_DIM), lambda b, h: (0, 0))
    return _call(
        body, name="ret_bwd", grid=(batch, RET_HEADS),
        in_specs=[val, pl.BlockSpec((seq, RET_VAL_DIM), lambda b, h: (b, C_RG // RET_VAL_DIM + h)), val, key, key,
                  pl.BlockSpec((seq, RET_VAL_DIM), lambda b, h: (b, C_RV // RET_VAL_DIM + h)), tab, tab,
                  pl.BlockSpec((None, 1, LANES), lambda b, h: (h, 0, 0))],
        out_specs=[key, key, val, val],
        out_shape=[jax.ShapeDtypeStruct((t, RET_HEADS * RET_KEY_DIM), BF16)] * 2
                  + [jax.ShapeDtypeStruct((t, RET_HEADS * RET_VAL_DIM), BF16)] * 2,
        scratch=[pltpu.VMEM((seq, RET_VAL_DIM), BF16)],
        semantics=("parallel", "parallel"), args=(dgro, proj, o_ret, qr, kr, proj, cs, sn, lg_arr),
        exchange=exchange)


def _att_bwd(proj, kpad, vpad, wvec, dao, batch, seq):
    ni, q_spec, kv_spec, w_spec = _att_specs(batch, seq)
    t = batch * seq

    def body(q_ref, k_ref, v_ref, w_ref, do_ref, dq_ref, dk_ref, dv_ref, dw_ref,
             bias_ref, dbias_ref, dk_acc, dv_acc):
        b, i = pl.program_id(1), pl.program_id(2)

        @pl.when((b == 0) & (i == 0))
        def _():
            _att_bias(w_ref, bias_ref)
            dbias_ref[...] = jnp.zeros_like(dbias_ref)

        @pl.when(i == 0)
        def _():
            dk_acc[...] = jnp.zeros_like(dk_acc)
            dv_acc[...] = jnp.zeros_like(dv_acc)

        win = pl.ds(pl.multiple_of(i * ATT_Q, ATT_Q), ATT_WIN)
        k2, v2, q2, do2 = k_ref[win, :], v_ref[win, :], q_ref[...], do_ref[...]
        start_ok = lax.broadcasted_iota(jnp.int32, (ATT_Q, ATT_WIN), 1) + (i * ATT_Q - ATT_PAD) >= 0
        lo = lax.broadcasted_iota(jnp.int32, (1, LANES), 1) < 64
        dq = jnp.zeros((ATT_Q, LANES), F32)
        dk = jnp.zeros((ATT_WIN, LANES), F32)
        dv = jnp.zeros((ATT_WIN, LANES), F32)
        for e in range(2):
            sel = lo if e == 0 else jnp.logical_not(lo)
            p = _att_scores(q2, k2, bias_ref[e], sel, start_ok)
            dom = jnp.where(sel, do2, jnp.zeros_like(do2))
            dp = _dot_nt(dom, v2)
            ds = p * (dp - jnp.sum(dp * p, axis=-1, keepdims=True))
            dbias_ref[e] += ds
            dsb = (ds * (1.0 / 8.0)).astype(BF16)
            dq = dq + _dot(dsb, jnp.where(sel, k2, jnp.zeros_like(k2)))
            dk = dk + _dot_tn(dsb, jnp.where(sel, q2, jnp.zeros_like(q2)))
            dv = dv + _dot_tn(p.astype(BF16), dom)
        dq_ref[...] = dq.astype(BF16)
        dk_acc[win, :] += dk
        dv_acc[win, :] += dv

        @pl.when(i == ni - 1)
        def _():
            dk_ref[...] = dk_acc[ATT_PAD:, :].astype(BF16)
            dv_ref[...] = dv_acc[ATT_PAD:, :].astype(BF16)

        @pl.when((b == batch - 1) & (i == ni - 1))
        def _():
            n_i = lax.broadcasted_iota(jnp.int32, (ATT_Q, BIAS_LEN), 0)
            for e in range(2):
                xw = jnp.concatenate([jnp.zeros((ATT_Q, BIAS_LEN - ATT_WIN), F32), dbias_ref[e]], axis=1)
                for bit in range(8):
                    xw = jnp.where(((n_i >> bit) & 1) == 1, pltpu.roll(xw, BIAS_LEN - (1 << bit), 1), xw)
                dw_ref[e:e + 1, :] = jnp.sum(xw, axis=0, keepdims=True)

    seq_blk = pl.BlockSpec((seq, LANES), lambda hp, b, i: (b, hp))
    q_out = pl.BlockSpec((ATT_Q, LANES), lambda hp, b, i: (b * ni + i, hp))
    return pl.pallas_call(
        body, name="att_bwd", grid=(ATT_HEADS // 2, batch, ni),
        in_specs=[q_spec, kv_spec, kv_spec, w_spec, q_out],
        out_specs=[q_out, seq_blk, seq_blk, w_spec],
        out_shape=[jax.ShapeDtypeStruct((t, 512), BF16)] * 3
                  + [jax.ShapeDtypeStruct((ATT_HEADS // 2, 2, BIAS_LEN), F32)],
        scratch_shapes=[pltpu.VMEM((2, ATT_Q, ATT_WIN), F32), pltpu.VMEM((2, ATT_Q, ATT_WIN), F32),
                        pltpu.VMEM((seq + ATT_PAD, LANES), F32), pltpu.VMEM((seq + ATT_PAD, LANES), F32)],
        compiler_params=_params("arbitrary", "arbitrary", "arbitrary"),
    )(proj, kpad, vpad, wvec, dao)


def _rms_in_bwd(x2, dxn, dh1, g1):
    t = x2.shape[0]
    tm = min(512, t)

    def body(x_ref, d_ref, h_ref, g_ref, dx_ref, dg_ref):
        @pl.when(pl.program_id(0) == 0)
        def _():
            dg_ref[...] = jnp.zeros_like(dg_ref)

        dx, dg_rows = _rms_bwd(x_ref[...], g_ref[...], d_ref[...])
        dx_ref[...] = h_ref[...] + dx
        dg_ref[...] += jnp.sum(dg_rows, axis=0, keepdims=True)

    row = pl.BlockSpec((tm, D_MODEL), lambda i: (i, 0))
    vec = pl.BlockSpec((1, D_MODEL), lambda i: (0, 0))
    return pl.pallas_call(
        body, name="rms_in_bwd", grid=(t // tm,),
        in_specs=[row, row, row, vec], out_specs=[row, vec],
        out_shape=[jax.ShapeDtypeStruct((t, D_MODEL), F32), jax.ShapeDtypeStruct((1, D_MODEL), F32)],
        compiler_params=_params("arbitrary"),
    )(x2, dxn, dh1, g1)


def _pack_small(dg1, dbr, dba, dg2, dg3, dw):
    def body(a_ref, b_ref, c_ref, d_ref, e_ref, w_ref, o_ref):
        o_ref[...] = jnp.zeros_like(o_ref)
        for r, ref in enumerate((a_ref, b_ref, c_ref, d_ref, e_ref)):
            o_ref[r:r + 1, :] = ref[...]
        for hp in range(ATT_HEADS // 2):
            o_ref[8 + 2 * hp:10 + 2 * hp, :] = w_ref[hp]

    return pl.pallas_call(body, name="pack_small",
                          out_shape=jax.ShapeDtypeStruct((16, D_MODEL), F32))(dg1, dbr, dba, dg2, dg3, dw)


def _rotary_tables(seq):
    freqs = ROPE_BASE ** (-jnp.arange(0, RET_KEY_DIM, 2, dtype=F32) / RET_KEY_DIM)
    ang = jnp.arange(seq, dtype=F32)[:, None] * freqs[None, :]
    cos, sin = jnp.cos(ang), jnp.sin(ang)
    return jnp.concatenate([cos, cos], axis=1), jnp.concatenate([-sin, sin], axis=1)


def _bias_rows(rel_bias):
    n_far = BIAS_LEN - ATT_Q - MAX_REL + 1
    n_near = BIAS_LEN - n_far - (N_REL - 2)
    w = jnp.concatenate([jnp.broadcast_to(rel_bias[:, N_REL - 1:], (ATT_HEADS, n_far)),
                         rel_bias[:, 1:N_REL - 1][:, ::-1],
                         jnp.broadcast_to(rel_bias[:, :1], (ATT_HEADS, n_near))], axis=1)
    return w.reshape(ATT_HEADS // 2, 2, BIAS_LEN)


def _bias_rows_bwd(dw):
    n_far = BIAS_LEN - ATT_Q - MAX_REL + 1
    mid = dw[:, n_far:n_far + N_REL - 2][:, ::-1]
    return jnp.concatenate([jnp.sum(dw[:, n_far + N_REL - 2:], axis=1, keepdims=True), mid,
                            jnp.sum(dw[:, :n_far], axis=1, keepdims=True)], axis=1)


def _pad_keys(a, batch, seq):
    a = a.reshape(batch, seq, a.shape[-1])
    return jnp.pad(a, ((0, 0), (ATT_PAD, 0), (0, 0))).reshape(batch * (seq + ATT_PAD), a.shape[-1])


def _step(x, tgt, norm_mix, b_gate, norm_ffn, norm_final, rel_bias_shard, shard):
    batch, seq, _ = x.shape
    t = batch * seq
    n_rb = rel_bias_shard.shape[-1]
    x2, tgt2 = x.reshape(t, D_MODEL), tgt.reshape(t, D_MODEL)
    g3 = norm_final.reshape(1, D_MODEL)
    cs, sn = _rotary_tables(seq)
    lg = np.log(1.0 - 2.0 ** (-5.0 - np.arange(RET_HEADS, dtype=np.float32))).astype(np.float32)
    lg_arr = jnp.asarray(np.broadcast_to(lg[:, None, None], (RET_HEADS, 1, LANES)))

    def gather(*names):
        return _Exchange([shard[nm] for nm in names], scatter=False)

    def scatter(*grads):
        return _Exchange(grads, scatter=True)

    rb_pad = jnp.pad(rel_bias_shard, ((0, 0), (0, LANES - n_rb)))
    w_in_t, rb_full = _exchange([shard["w_in_t"], rb_pad], False, "gather_w_in")
    rb_full = rb_full.reshape(N_DEV, ATT_HEADS, LANES)[:, :, :n_rb]
    wvec = _bias_rows(jnp.transpose(rb_full, (1, 0, 2)).reshape(ATT_HEADS, N_DEV * n_rb))

    xn = _rms_fwd(x2, norm_mix)
    proj, (w_ret, w_att_t, w_out, w_gate_t) = _mm(
        xn, w_in_t, tb=True, out_dtype=BF16, tm=512, tn=512, tk=1024, name="proj",
        exchange=gather("w_ret", "w_att_t", "w_out", "w_gate_t"))
    (gro, o_ret, qr, kr), (w_up_t, w_down) = _ret_fwd(proj, cs, sn, lg_arr, batch, seq, gather("w_up_t", "w_down"))
    kpad = _pad_keys(proj[:, C_AK:C_AV], batch, seq)
    vpad = _pad_keys(proj[:, C_AV:C_GL], batch, seq)
    ao = _att_fwd(proj, kpad, vpad, wvec, batch, seq)
    z, y_ret, y_att = _mix_fwd(gro, ao, proj, b_gate, w_ret, w_att_t)
    h1, hn = _out_fwd(z, x2, w_out, norm_ffn)
    g_act, u_act, a_act = _ffn_up(hn, w_gate_t, w_up_t)
    dh2, dh2b, loss, dg3 = _ffn_down_loss(a_act, h1, tgt2, w_down, g3)

    wg = dict(out_dtype=BF16, tn=512, tk=1024, ta=True)
    slots = {}
    dw_down = _mm(a_act, dh2b, tm=256, name="dw_down", **wg)
    (d_gact, d_uact), (slots["w_down"],) = _ffn_bwd_act(dh2b, w_down, g_act, u_act, scatter(dw_down))
    dw_gate = _mm(d_gact, hn, tm=256, name="dw_gate", **wg)
    dw_up, (slots["w_gate_t"],) = _mm(d_uact, hn, tm=256, name="dw_up", exchange=scatter(dw_gate), **wg)
    (dh1, dh1b, dg2), (slots["w_up_t"],) = _ffn_bwd_in(d_gact, d_uact, w_gate_t, w_up_t, h1, dh2, norm_ffn,
                                                     scatter(dw_up))
    dw_out = _mm(z, dh1b, tm=512, name="dw_out", **wg)
    (dyr, dya, dglr, dgla, dbr, dba), (slots["w_out"],) = _mix_bwd(dh1b, w_out, proj, b_gate, y_ret, y_att,
                                                                scatter(dw_out))
    dgro = _mm(dyr, w_ret, tb=True, out_dtype=BF16, tm=512, tn=512, tk=1024, name="dgro")
    dao = _mm(dya, w_att_t, out_dtype=BF16, tm=512, tn=512, tk=1024, name="dao")
    dw_ret = _mm(gro, dyr, tm=512, name="dw_ret", **wg)
    dw_att = _mm(dya, ao, tm=512, name="dw_att", **wg)
    (drq, drk, drv, drg), (slots["w_ret"], slots["w_att_t"]) = _ret_bwd(
        dgro, proj, o_ret, qr, kr, cs, sn, lg_arr, batch, seq, scatter(dw_ret, dw_att))
    daq, dak, dav, dw = _att_bwd(proj, kpad, vpad, wvec, dao, batch, seq)
    dproj = jnp.concatenate([drq, drk, drv, drg, daq, dak, dav, dglr, dgla], axis=1)
    dw_in = _mm(dproj, xn, tm=512, name="dw_in", **wg)
    dxn, (slots["w_in_t"],) = _mm(dproj, w_in_t, out_dtype=F32, tm=512, tn=512, tk=512, name="dxn",
                                  exchange=scatter(dw_in))
    dx, dg1 = _rms_in_bwd(x2, dxn, dh1, norm_mix)
    small = _pack_small(dg1, dbr, dba, dg2, dg3, dw)
    (small_slots,) = _exchange([small], False, "gather_small")
    return loss[0, 0], dx.reshape(batch, seq, D_MODEL), slots, small_slots.reshape(N_DEV, 16, D_MODEL)


def _sum_slots(slots, name):
    _, r, c = slots.shape
    tr = max(d for d in range(16, r + 1, 16) if r % d == 0 and (d * c <= 256 * 1024 or d == 16))

    def body(s_ref, o_ref):
        acc = s_ref[0].astype(F32)
        for s in range(1, N_DEV):
            acc = acc + s_ref[s].astype(F32)
        o_ref[...] = acc

    return pl.pallas_call(
        body, name=name, grid=(r // tr,),
        in_specs=[pl.BlockSpec((N_DEV, tr, c), lambda i: (0, i, 0))],
        out_specs=pl.BlockSpec((tr, c), lambda i: (i, 0)),
        out_shape=jax.ShapeDtypeStruct((r, c), F32),
        compiler_params=_params("parallel"),
    )(slots)


def _adamw_math(w, g, m, v):
    m = ADAM_B1 * m + (1.0 - ADAM_B1) * g
    v = ADAM_B2 * v + (1.0 - ADAM_B2) * (g * g)
    m_hat = m / (1.0 - ADAM_B1 ** ADAM_STEP)
    v_hat = v / (1.0 - ADAM_B2 ** ADAM_STEP)
    return -ADAM_LR * (m_hat / (jnp.sqrt(v_hat) + ADAM_EPS) + ADAM_WD * w), m, v


def _adamw(w, g, m, v, name):
    r, c = w.shape
    tr = r
    while tr * c > 128 * 1024 and tr % 16 == 0:
        tr //= 2

    def body(w_ref, g_ref, m_ref, v_ref, d_ref, nm_ref, nv_ref):
        d_ref[...], nm_ref[...], nv_ref[...] = _adamw_math(w_ref[...], g_ref[...], m_ref[...], v_ref[...])

    blk = pl.BlockSpec((tr, c), lambda i: (i, 0))
    return pl.pallas_call(
        body, name=name, grid=(r // tr,),
        in_specs=[blk] * 4, out_specs=[blk] * 3,
        out_shape=[jax.ShapeDtypeStruct((r, c), F32)] * 3,
        compiler_params=_params("parallel"),
    )(w, g, m, v)


def _adamw_small(ws, gs, ms, vs):
    n = len(ws)

    def body(*refs):
        for i in range(n):
            w_ref, g_ref, m_ref, v_ref = (refs[j * n + i] for j in range(4))
            d_ref, nm_ref, nv_ref = (refs[(4 + j) * n + i] for j in range(3))
            d_ref[...], nm_ref[...], nv_ref[...] = _adamw_math(w_ref[...], g_ref[...], m_ref[...], v_ref[...])

    shapes = [jax.ShapeDtypeStruct(w.shape, F32) for w in ws]
    outs = pl.pallas_call(body, name="adamw_small", out_shape=shapes * 3)(*ws, *gs, *ms, *vs)
    return outs[:n], outs[n:2 * n], outs[2 * n:]


def kernel(x, norm_mix, w_in, b_gate, rel_bias, w_ret_out, w_att_out, w_out, norm_ffn, w_ffn_gate, w_ffn_up, w_ffn_down, norm_final, loss_target, m_norm_mix, m_w_in, m_b_gate, m_rel_bias, m_w_ret_out, m_w_att_out, m_w_out, m_norm_ffn, m_w_ffn_gate, m_w_ffn_up, m_w_ffn_down, m_norm_final, v_norm_mix, v_w_in, v_b_gate, v_rel_bias, v_w_ret_out, v_w_att_out, v_w_out, v_norm_ffn, v_w_ffn_gate, v_w_ffn_up, v_w_ffn_down, v_norm_final):
    me = _index(_place())
    n_rb = rel_bias.shape[-1]

    shard = dict(w_in_t=w_in[0].T, w_gate_t=w_ffn_gate[0].T, w_up_t=w_ffn_up[0].T, w_down=w_ffn_down[0],
                 w_ret=w_ret_out[0], w_out=w_out[0], w_att_t=w_att_out[0].T)
    shard = {nm: s.astype(BF16) for nm, s in shard.items()}
    loss, dx, slots, small_slots = _step(x, loss_target, norm_mix, b_gate, norm_ffn, norm_final, rel_bias[0], shard)
    loss = lax.psum(loss, ("x", "y", "c"))
    summed = {nm: _sum_slots(s, "sum_" + nm) for nm, s in slots.items()}
    small_sum = _sum_slots(small_slots, "sum_small")

    g = dict(
        w_in=summed["w_in_t"].T, w_ffn_gate=summed["w_gate_t"].T, w_ffn_up=summed["w_up_t"].T,
        w_ffn_down=summed["w_down"], w_ret_out=summed["w_ret"], w_out=summed["w_out"], w_att_out=summed["w_att_t"].T,
        norm_mix=small_sum[0:1], b_gate=jnp.concatenate([small_sum[1:2], small_sum[2:3]], axis=1),
        norm_ffn=small_sum[3:4], norm_final=small_sum[4:5],
        rel_bias=lax.dynamic_slice_in_dim(_bias_rows_bwd(small_sum[8:16]), me * n_rb, n_rb, axis=1),
    )
    w = dict(norm_mix=norm_mix, w_in=w_in, b_gate=b_gate, rel_bias=rel_bias, w_ret_out=w_ret_out, w_att_out=w_att_out,
             w_out=w_out, norm_ffn=norm_ffn, w_ffn_gate=w_ffn_gate, w_ffn_up=w_ffn_up, w_ffn_down=w_ffn_down,
             norm_final=norm_final)
    m = dict(norm_mix=m_norm_mix, w_in=m_w_in, b_gate=m_b_gate, rel_bias=m_rel_bias, w_ret_out=m_w_ret_out,
             w_att_out=m_w_att_out, w_out=m_w_out, norm_ffn=m_norm_ffn, w_ffn_gate=m_w_ffn_gate, w_ffn_up=m_w_ffn_up,
             w_ffn_down=m_w_ffn_down, norm_final=m_norm_final)
    v = dict(norm_mix=v_norm_mix, w_in=v_w_in, b_gate=v_b_gate, rel_bias=v_rel_bias, w_ret_out=v_w_ret_out,
             w_att_out=v_w_att_out, w_out=v_w_out, norm_ffn=v_norm_ffn, w_ffn_gate=v_w_ffn_gate, w_ffn_up=v_w_ffn_up,
             w_ffn_down=v_w_ffn_down, norm_final=v_norm_final)
    order = ("norm_mix", "w_in", "b_gate", "rel_bias", "w_ret_out", "w_att_out", "w_out", "norm_ffn",
             "w_ffn_gate", "w_ffn_up", "w_ffn_down", "norm_final")
    small_names = ("norm_mix", "b_gate", "rel_bias", "norm_ffn", "norm_final")

    def flat(a):
        return a[0] if a.ndim == 3 else a.reshape(-1, a.shape[-1])

    grad, delta, new_m, new_v = {}, {}, {}, {}
    for nm in order:
        if nm not in small_names:
            d, nmom, nvar = _adamw(flat(w[nm]), g[nm], flat(m[nm]), flat(v[nm]), "adamw_" + nm)
            grad[nm], delta[nm], new_m[nm], new_v[nm] = (a.reshape(w[nm].shape) for a in (g[nm], d, nmom, nvar))
    ds, nms, nvs = _adamw_small([flat(w[nm]) for nm in small_names], [g[nm] for nm in small_names],
                                [flat(m[nm]) for nm in small_names], [flat(v[nm]) for nm in small_names])
    for i, nm in enumerate(small_names):
        grad[nm], delta[nm], new_m[nm], new_v[nm] = (a.reshape(w[nm].shape) for a in (g[nm], ds[i], nms[i], nvs[i]))

    return (loss, dx, *[grad[nm] for nm in order], *[delta[nm] for nm in order],
            *[new_m[nm] for nm in order], *[new_v[nm] for nm in order])
```

```python
import numpy as np
import jax
import jax.numpy as jnp
from jax import lax
from jax.experimental import pallas as pl
from jax.experimental.pallas import tpu as pltpu

F32 = jnp.float32
BF16 = jnp.bfloat16
MESH = pl.DeviceIdType.MESH

D_MODEL = 1024
CHUNK = 64
RET_HEADS = 4
RET_KEY_DIM = 128
RET_VAL_DIM = 256
ATT_HEADS = 8
BAND_CHUNKS = 8
MAX_REL = 256
N_REL = CHUNK + MAX_REL
D_FF = 2816
N_IN = 6656
ROPE_BASE = 10000.0
EPS = 1e-6
NEG_INF = -1e30
C_RQ, C_RK, C_RV, C_RG, C_AQ, C_AK, C_AV, C_GL = 0, 512, 1024, 2048, 3072, 3584, 4096, 4608

ADAM_LR = 0.001
ADAM_B1 = 0.9
ADAM_B2 = 0.999
ADAM_EPS = 1e-08
ADAM_WD = 0.01
ADAM_STEP = 10

N_DEV = 8
LANES = 128
RET_TILE = 256
ATT_Q = 256
ATT_PAD = BAND_CHUNKS * CHUNK
ATT_WIN = ATT_PAD + ATT_Q
BIAS_LEN = 1024
VMEM_LIMIT = 48 * 1024 * 1024


def _params(*sem):
    return pltpu.CompilerParams(dimension_semantics=sem, vmem_limit_bytes=VMEM_LIMIT)


def _dot(a, b):
    return lax.dot_general(a, b, (((1,), (0,)), ((), ())), preferred_element_type=F32)


def _dot_nt(a, b):
    return lax.dot_general(a, b, (((1,), (1,)), ((), ())), preferred_element_type=F32)


def _dot_tn(a, b):
    return lax.dot_general(a, b, (((0,), (0,)), ((), ())), preferred_element_type=F32)


def _sigmoid(x):
    return 1.0 / (1.0 + jnp.exp(-x))


def _rms_bwd(x, g, dy):
    r = lax.rsqrt(jnp.mean(x * x, axis=-1, keepdims=True) + EPS)
    u = dy * g
    dx = r * u - x * (r * r * r) * jnp.mean(u * x, axis=-1, keepdims=True)
    return dx, dy * x * r


def _place():
    return lax.axis_index("x"), lax.axis_index("y"), lax.axis_index("c")


def _peer(k):
    x, y, c = _place()
    return ((1 - x) if k & 4 else x, (1 - y) if k & 2 else y, (1 - c) if k & 1 else c)


def _index(place):
    return 4 * place[0] + 2 * place[1] + place[2]


def _rows(ref, block, nrows):
    align = 16 if ref.dtype == BF16 else 8
    return ref.at[pl.ds(pl.multiple_of(block * nrows, align), nrows)]


class _Exchange:
    def __init__(self, arrays, scatter):
        self.arrays, self.scatter, self.n = list(arrays), scatter, len(arrays)

    def out_shape(self):
        if self.scatter:
            return [jax.ShapeDtypeStruct((N_DEV, a.shape[0] // N_DEV) + a.shape[1:], a.dtype) for a in self.arrays]
        return [jax.ShapeDtypeStruct((N_DEV * a.shape[0],) + a.shape[1:], a.dtype) for a in self.arrays]

    def scratch(self):
        return [pltpu.SemaphoreType.DMA((self.n, N_DEV - 1)), pltpu.SemaphoreType.DMA((self.n, N_DEV - 1)),
                pltpu.SemaphoreType.DMA((self.n,))]

    def _copies(self, ins, outs, sems):
        send_sems, recv_sems, local_sems = sems
        me = _index(_place())

        def src(w, to):
            return _rows(ins[w], to, ins[w].shape[0] // N_DEV) if self.scatter else ins[w]

        def dst(w, origin):
            return outs[w].at[origin] if self.scatter else _rows(outs[w], origin, ins[w].shape[0])

        def remote(w, k, to, origin):
            return pltpu.make_async_remote_copy(src_ref=src(w, to), dst_ref=dst(w, origin),
                                                send_sem=send_sems.at[w, k - 1], recv_sem=recv_sems.at[w, k - 1],
                                                device_id=_peer(k), device_id_type=MESH)

        pairs = [(w, k) for w in range(self.n) for k in range(1, N_DEV)]
        own = lambda: [pltpu.make_async_copy(src(w, me), dst(w, me), local_sems.at[w]) for w in range(self.n)]
        sent = lambda: [remote(w, k, _index(_peer(k)), me) for w, k in pairs]
        arriving = lambda: [remote(w, k, me, _index(_peer(k))) for w, k in pairs]
        return own, sent, arriving

    def start(self, ins, outs, sems):
        own, sent, _ = self._copies(ins, outs, sems)
        for cp in own() + sent():
            cp.start()

    def wait(self, ins, outs, sems):
        own, sent, arriving = self._copies(ins, outs, sems)
        for cp in arriving():
            cp.wait_recv()
        for cp in sent():
            cp.wait_send()
        for cp in own():
            cp.wait()


def _call(body, *, name, grid, in_specs, out_specs, out_shape, scratch=(), semantics, args, exchange=None):
    if exchange is None:
        return pl.pallas_call(body, name=name, grid=grid, in_specs=in_specs, out_specs=out_specs, out_shape=out_shape,
                              scratch_shapes=list(scratch), compiler_params=_params(*semantics))(*args), None
    n_in, n_out, n_scr, nx = len(in_specs), len(out_specs), len(scratch), exchange.n

    def full_body(*refs):
        ins, refs = refs[:n_in], refs[n_in:]
        x_in, refs = refs[:nx], refs[nx:]
        outs, refs = refs[:n_out], refs[n_out:]
        x_out, refs = refs[:nx], refs[nx:]
        scr, sems = refs[:n_scr], refs[n_scr:]
        first, last = True, True
        for axis, size in enumerate(grid):
            first = jnp.logical_and(first, pl.program_id(axis) == 0)
            last = jnp.logical_and(last, pl.program_id(axis) == size - 1)
        if grid:
            pl.when(first)(lambda: exchange.start(x_in, x_out, sems))
        else:
            exchange.start(x_in, x_out, sems)
        body(*ins, *outs, *scr)
        if grid:
            pl.when(last)(lambda: exchange.wait(x_in, x_out, sems))
        else:
            exchange.wait(x_in, x_out, sems)

    hbm = pl.BlockSpec(memory_space=pltpu.HBM)
    res = pl.pallas_call(
        full_body, name=name, grid=grid,
        in_specs=list(in_specs) + [hbm] * nx, out_specs=list(out_specs) + [hbm] * nx,
        out_shape=list(out_shape) + exchange.out_shape(),
        scratch_shapes=list(scratch) + exchange.scratch(),
        compiler_params=_params(*(["arbitrary"] * len(grid))),
    )(*args, *exchange.arrays)
    return res[:n_out], res[n_out:]


def _exchange(arrays, scatter, name):
    return _call(lambda: None, name=name, grid=(), in_specs=[], out_specs=[], out_shape=[], semantics=(),
                 args=(), exchange=_Exchange(arrays, scatter))[1]


def _mm(a, b, *, ta=False, tb=False, out_dtype, tm, tn, tk, name, exchange=None):
    m, k = (a.shape[1], a.shape[0]) if ta else a.shape
    n = b.shape[0] if tb else b.shape[1]
    assert k == (b.shape[1] if tb else b.shape[0])
    tm, tn, tk = min(tm, m), min(tn, n), min(tk, k)
    assert m % tm == 0 and n % tn == 0 and k % tk == 0, (name, m, n, k)
    nk = k // tk
    dims = (((0 if ta else 1,), (1 if tb else 0,)), ((), ()))

    def body(a_ref, b_ref, o_ref, *acc):
        prod = lax.dot_general(a_ref[...].astype(BF16), b_ref[...].astype(BF16), dims, preferred_element_type=F32)
        if nk == 1:
            o_ref[...] = prod.astype(o_ref.dtype)
            return
        acc_ref, kk = acc[0], pl.program_id(2)

        @pl.when(kk == 0)
        def _():
            acc_ref[...] = prod

        @pl.when((kk > 0) & (kk < nk - 1))
        def _():
            acc_ref[...] += prod

        @pl.when(kk == nk - 1)
        def _():
            o_ref[...] = (acc_ref[...] + prod).astype(o_ref.dtype)

    a_spec = (pl.BlockSpec((tk, tm), lambda i, j, kk: (kk, i)) if ta
              else pl.BlockSpec((tm, tk), lambda i, j, kk: (i, kk)))
    b_spec = (pl.BlockSpec((tn, tk), lambda i, j, kk: (j, kk)) if tb
              else pl.BlockSpec((tk, tn), lambda i, j, kk: (kk, j)))
    (out,), moved = _call(
        body, name=name, grid=(m // tm, n // tn, nk),
        in_specs=[a_spec, b_spec],
        out_specs=[pl.BlockSpec((tm, tn), lambda i, j, kk: (i, j))],
        out_shape=[jax.ShapeDtypeStruct((m, n), out_dtype)],
        scratch=[pltpu.VMEM((tm, tn), F32)] if nk > 1 else [],
        semantics=("parallel", "parallel", "arbitrary"), args=(a, b), exchange=exchange)
    return out if exchange is None else (out, moved)


def _rms_fwd(x2, g):
    t = x2.shape[0]
    tm = min(512, t)

    def body(x_ref, g_ref, o_ref):
        x = x_ref[...]
        r = lax.rsqrt(jnp.mean(x * x, axis=-1, keepdims=True) + EPS)
        o_ref[...] = (x * r * g_ref[...]).astype(o_ref.dtype)

    return pl.pallas_call(
        body, name="rms_in_fwd", grid=(t // tm,),
        in_specs=[pl.BlockSpec((tm, D_MODEL), lambda i: (i, 0)), pl.BlockSpec((1, D_MODEL), lambda i: (0, 0))],
        out_specs=pl.BlockSpec((tm, D_MODEL), lambda i: (i, 0)),
        out_shape=jax.ShapeDtypeStruct((t, D_MODEL), BF16),
        compiler_params=_params("parallel"),
    )(x2, g)


def _decay_tiles(lg):
    row = lax.broadcasted_iota(jnp.int32, (RET_TILE, RET_TILE), 0)
    col = lax.broadcasted_iota(jnp.int32, (RET_TILE, RET_TILE), 1)
    diff = (row - col).astype(F32)
    within = jnp.exp(lg * jnp.abs(diff))
    e_qk = jnp.exp(lg * diff)
    d_qk = jnp.where((col >> 6) <= (row >> 6), within, 0.0)
    return e_qk, d_qk, row, col, diff, within


def _ret_fwd(proj, cs, sn, lg_arr, batch, seq, exchange):
    t = batch * seq
    nt = seq // RET_TILE

    def body(q_ref, k_ref, v_ref, rg_ref, cs_ref, sn_ref, lg_ref, gro_ref, o_ref, qr_ref, kr_ref):
        lg = lg_ref[:, 0:1]
        cs_t, sn_t = cs_ref[...], sn_ref[...]
        q = q_ref[...].astype(F32)
        k = k_ref[...].astype(F32)
        qr_ref[...] = (q * cs_t + pltpu.roll(q, 64, 1) * sn_t).astype(BF16)
        kr_ref[...] = ((k * cs_t + pltpu.roll(k, 64, 1) * sn_t) * (RET_KEY_DIM ** -0.5)).astype(BF16)
        e_qk, d_qk = _decay_tiles(lg)[:2]
        for i in range(nt):
            rows = slice(i * RET_TILE, (i + 1) * RET_TILE)
            qi = qr_ref[rows, :]

            def pair(j, acc, i=i, qi=qi):
                off = pl.multiple_of(j * RET_TILE, RET_TILE)
                s = _dot_nt(qi, kr_ref[pl.ds(off, RET_TILE), :])
                far = jnp.exp(lg * (RET_TILE * (i - j)).astype(F32))
                p = (s * (e_qk * far)).astype(BF16)
                return acc + _dot(p, v_ref[pl.ds(off, RET_TILE), :])

            acc = lax.fori_loop(0, i, pair, jnp.zeros((RET_TILE, RET_VAL_DIM), F32))
            s = _dot_nt(qi, kr_ref[rows, :])
            acc = acc + _dot((s * d_qk).astype(BF16), v_ref[rows, :])
            o_ref[rows, :] = acc
            xc = acc - jnp.mean(acc, axis=-1, keepdims=True)
            nrm = xc * lax.rsqrt(jnp.mean(xc * xc, axis=-1, keepdims=True) + EPS)
            rg = rg_ref[rows, :].astype(F32)
            gro_ref[rows, :] = (rg * _sigmoid(rg) * nrm).astype(BF16)

    def col(base, width):
        return lambda b, h: (b, base // width + h)

    return _call(
        body, name="ret_fwd", grid=(batch, RET_HEADS),
        in_specs=[pl.BlockSpec((seq, RET_KEY_DIM), col(C_RQ, RET_KEY_DIM)),
                  pl.BlockSpec((seq, RET_KEY_DIM), col(C_RK, RET_KEY_DIM)),
                  pl.BlockSpec((seq, RET_VAL_DIM), col(C_RV, RET_VAL_DIM)),
                  pl.BlockSpec((seq, RET_VAL_DIM), col(C_RG, RET_VAL_DIM)),
                  pl.BlockSpec((seq, RET_KEY_DIM), lambda b, h: (0, 0)),
                  pl.BlockSpec((seq, RET_KEY_DIM), lambda b, h: (0, 0)),
                  pl.BlockSpec((None, 1, LANES), lambda b, h: (h, 0, 0))],
        out_specs=[pl.BlockSpec((seq, RET_VAL_DIM), lambda b, h: (b, h)),
                   pl.BlockSpec((seq, RET_VAL_DIM), lambda b, h: (b, h)),
                   pl.BlockSpec((seq, RET_KEY_DIM), lambda b, h: (b, h)),
                   pl.BlockSpec((seq, RET_KEY_DIM), lambda b, h: (b, h))],
        out_shape=[jax.ShapeDtypeStruct((t, RET_HEADS * RET_VAL_DIM), BF16),
                   jax.ShapeDtypeStruct((t, RET_HEADS * RET_VAL_DIM), F32),
                   jax.ShapeDtypeStruct((t, RET_HEADS * RET_KEY_DIM), BF16),
                   jax.ShapeDtypeStruct((t, RET_HEADS * RET_KEY_DIM), BF16)],
        semantics=("parallel", "parallel"), args=(proj, proj, proj, proj, cs, sn, lg_arr), exchange=exchange)


def _att_bias(w_ref, bias_ref):
    n_i = lax.broadcasted_iota(jnp.int32, (ATT_Q, BIAS_LEN), 0)
    qc = lax.broadcasted_iota(jnp.int32, (ATT_Q, ATT_WIN), 0) >> 6
    kc = lax.broadcasted_iota(jnp.int32, (ATT_Q, ATT_WIN), 1) >> 6
    dc = qc + BAND_CHUNKS - kc
    band = (dc >= 0) & (dc <= BAND_CHUNKS)
    for e in range(2):
        xw = jnp.broadcast_to(w_ref[e:e + 1, :], (ATT_Q, BIAS_LEN))
        for bit in range(8):
            xw = jnp.where(((n_i >> bit) & 1) == 1, pltpu.roll(xw, 1 << bit, 1), xw)
        bias_ref[e] = jnp.where(band, xw[:, BIAS_LEN - ATT_WIN:], NEG_INF)


def _att_specs(batch, seq):
    ni = seq // ATT_Q
    q_spec = pl.BlockSpec((ATT_Q, LANES), lambda hp, b, i: (b * ni + i, C_AQ // LANES + hp))
    kv_spec = pl.BlockSpec((seq + ATT_PAD, LANES), lambda hp, b, i: (b, hp))
    w_spec = pl.BlockSpec((None, 2, BIAS_LEN), lambda hp, b, i: (hp, 0, 0))
    return ni, q_spec, kv_spec, w_spec


def _att_scores(q2, k2, bias, sel, start_ok):
    qm = jnp.where(sel, q2, jnp.zeros_like(q2))
    s = _dot_nt(qm, k2) * (1.0 / 8.0) + bias
    s = jnp.where(start_ok, s, NEG_INF)
    p = jnp.exp(s - jnp.max(s, axis=-1, keepdims=True))
    return p / jnp.sum(p, axis=-1, keepdims=True)


def _att_fwd(proj, kpad, vpad, wvec, batch, seq):
    ni, q_spec, kv_spec, w_spec = _att_specs(batch, seq)

    def body(q_ref, k_ref, v_ref, w_ref, o_ref, bias_ref):
        b, i = pl.program_id(1), pl.program_id(2)

        @pl.when((b == 0) & (i == 0))
        def _():
            _att_bias(w_ref, bias_ref)

        win = pl.ds(pl.multiple_of(i * ATT_Q, ATT_Q), ATT_WIN)
        k2, v2, q2 = k_ref[win, :], v_ref[win, :], q_ref[...]
        start_ok = lax.broadcasted_iota(jnp.int32, (ATT_Q, ATT_WIN), 1) + (i * ATT_Q - ATT_PAD) >= 0
        lo = lax.broadcasted_iota(jnp.int32, (1, LANES), 1) < 64
        out = jnp.zeros((ATT_Q, LANES), F32)
        for e in range(2):
            sel = lo if e == 0 else jnp.logical_not(lo)
            p = _att_scores(q2, k2, bias_ref[e], sel, start_ok)
            out = out + _dot(p.astype(BF16), jnp.where(sel, v2, jnp.zeros_like(v2)))
        o_ref[...] = out.astype(BF16)

    return pl.pallas_call(
        body, name="att_fwd", grid=(ATT_HEADS // 2, batch, ni),
        in_specs=[q_spec, kv_spec, kv_spec, w_spec],
        out_specs=pl.BlockSpec((ATT_Q, LANES), lambda hp, b, i: (b * ni + i, hp)),
        out_shape=jax.ShapeDtypeStruct((batch * seq, ATT_HEADS * 64), BF16),
        scratch_shapes=[pltpu.VMEM((2, ATT_Q, ATT_WIN), F32)],
        compiler_params=_params("arbitrary", "arbitrary", "arbitrary"),
    )(proj, kpad, vpad, wvec)


def _mix_fwd(gro, ao, proj, b_gate, w_ret, w_att_t):
    t = gro.shape[0]
    tm, tn = min(256, t), 512

    def body(gro_ref, ao_ref, glr_ref, gla_ref, br_ref, ba_ref, wr_ref, wa_ref, z_ref, yr_ref, ya_ref):
        yr = _dot(gro_ref[...], wr_ref[...])
        ya = _dot_nt(ao_ref[...], wa_ref[...])
        gr = _sigmoid(glr_ref[...].astype(F32) + br_ref[...])
        ga = _sigmoid(gla_ref[...].astype(F32) + ba_ref[...])
        z_ref[...] = (gr * yr + ga * ya).astype(BF16)
        yr_ref[...] = yr.astype(BF16)
        ya_ref[...] = ya.astype(BF16)

    nb = D_MODEL // tn
    out = pl.BlockSpec((tm, tn), lambda i, j: (i, j))
    return pl.pallas_call(
        body, name="mix_fwd", grid=(t // tm, nb),
        in_specs=[pl.BlockSpec((tm, D_MODEL), lambda i, j: (i, 0)),
                  pl.BlockSpec((tm, 512), lambda i, j: (i, 0)),
                  pl.BlockSpec((tm, tn), lambda i, j: (i, C_GL // tn + j)),
                  pl.BlockSpec((tm, tn), lambda i, j: (i, C_GL // tn + nb + j)),
                  pl.BlockSpec((1, tn), lambda i, j: (0, j)),
                  pl.BlockSpec((1, tn), lambda i, j: (0, nb + j)),
                  pl.BlockSpec((D_MODEL, tn), lambda i, j: (0, j)),
                  pl.BlockSpec((tn, 512), lambda i, j: (j, 0))],
        out_specs=[out, out, out],
        out_shape=[jax.ShapeDtypeStruct((t, D_MODEL), BF16)] * 3,
        compiler_params=_params("parallel", "parallel"),
    )(gro, ao, proj, proj, b_gate, b_gate, w_ret, w_att_t)


def _out_fwd(z, x2, w_out, g2):
    t = z.shape[0]
    tm = min(256, t)

    def body(z_ref, x_ref, w_ref, g_ref, h_ref, hn_ref):
        h = x_ref[...] + _dot(z_ref[...], w_ref[...])
        h_ref[...] = h
        r = lax.rsqrt(jnp.mean(h * h, axis=-1, keepdims=True) + EPS)
        hn_ref[...] = (h * r * g_ref[...]).astype(BF16)

    row = pl.BlockSpec((tm, D_MODEL), lambda i: (i, 0))
    return pl.pallas_call(
        body, name="out_fwd", grid=(t // tm,),
        in_specs=[row, row, pl.BlockSpec((D_MODEL, D_MODEL), lambda i: (0, 0)),
                  pl.BlockSpec((1, D_MODEL), lambda i: (0, 0))],
        out_specs=[row, row],
        out_shape=[jax.ShapeDtypeStruct((t, D_MODEL), F32), jax.ShapeDtypeStruct((t, D_MODEL), BF16)],
        compiler_params=_params("parallel"),
    )(z, x2, w_out, g2)


def _ffn_up(hn, wg_t, wu_t):
    t = hn.shape[0]
    tm, tn = min(512, t), D_FF // 2

    def body(h_ref, wg_ref, wu_ref, g_ref, u_ref, a_ref):
        g = _dot_nt(h_ref[...], wg_ref[...])
        u = _dot_nt(h_ref[...], wu_ref[...])
        g_ref[...] = g.astype(BF16)
        u_ref[...] = u.astype(BF16)
        a_ref[...] = (g * _sigmoid(g) * u).astype(BF16)

    w_spec = pl.BlockSpec((tn, D_MODEL), lambda j, i: (j, 0))
    out = pl.BlockSpec((tm, tn), lambda j, i: (i, j))
    return pl.pallas_call(
        body, name="ffn_up", grid=(D_FF // tn, t // tm),
        in_specs=[pl.BlockSpec((tm, D_MODEL), lambda j, i: (i, 0)), w_spec, w_spec],
        out_specs=[out, out, out],
        out_shape=[jax.ShapeDtypeStruct((t, D_FF), BF16)] * 3,
        compiler_params=_params("parallel", "parallel"),
    )(hn, wg_t, wu_t)


def _ffn_down_loss(a, h1, tgt, w_down, g3):
    t = a.shape[0]
    tm = min(512, t)

    def body(a_ref, h_ref, t_ref, w_ref, g_ref, dh_ref, dhb_ref, loss_ref, dg_ref):
        @pl.when(pl.program_id(0) == 0)
        def _():
            loss_ref[...] = jnp.zeros_like(loss_ref)
            dg_ref[...] = jnp.zeros_like(dg_ref)

        g = g_ref[...]
        h2 = h_ref[...] + _dot(a_ref[...], w_ref[...])
        r = lax.rsqrt(jnp.mean(h2 * h2, axis=-1, keepdims=True) + EPS)
        err = h2 * r * g - t_ref[...]
        loss_ref[...] += jnp.sum(err * err) * (0.5 / D_MODEL)
        dy = err * (1.0 / D_MODEL)
        dh, dg_rows = _rms_bwd(h2, g, dy)
        dg_ref[...] += jnp.sum(dg_rows, axis=0, keepdims=True)
        dh_ref[...] = dh
        dhb_ref[...] = dh.astype(BF16)

    row = pl.BlockSpec((tm, D_MODEL), lambda i: (i, 0))
    vec = pl.BlockSpec((1, D_MODEL), lambda i: (0, 0))
    return pl.pallas_call(
        body, name="ffn_down_loss", grid=(t // tm,),
        in_specs=[pl.BlockSpec((tm, D_FF), lambda i: (i, 0)), row, row,
                  pl.BlockSpec((D_FF, D_MODEL), lambda i: (0, 0)), vec],
        out_specs=[row, row, pl.BlockSpec((1, LANES), lambda i: (0, 0)), vec],
        out_shape=[jax.ShapeDtypeStruct((t, D_MODEL), F32), jax.ShapeDtypeStruct((t, D_MODEL), BF16),
                   jax.ShapeDtypeStruct((1, LANES), F32), jax.ShapeDtypeStruct((1, D_MODEL), F32)],
        compiler_params=_params("arbitrary"),
    )(a, h1, tgt, w_down, g3)


def _ffn_bwd_act(dh2b, w_down, g_act, u_act, exchange):
    t = dh2b.shape[0]
    tm, tn = min(512, t), D_FF // 2

    def body(d_ref, w_ref, g_ref, u_ref, dg_ref, du_ref):
        da = _dot_nt(d_ref[...], w_ref[...])
        g = g_ref[...].astype(F32)
        u = u_ref[...].astype(F32)
        sg = _sigmoid(g)
        dg_ref[...] = (da * u * sg * (1.0 + g * (1.0 - sg))).astype(BF16)
        du_ref[...] = (da * g * sg).astype(BF16)

    blk = pl.BlockSpec((tm, tn), lambda j, i: (i, j))
    return _call(
        body, name="ffn_bwd_act", grid=(D_FF // tn, t // tm),
        in_specs=[pl.BlockSpec((tm, D_MODEL), lambda j, i: (i, 0)),
                  pl.BlockSpec((tn, D_MODEL), lambda j, i: (j, 0)), blk, blk],
        out_specs=[blk, blk],
        out_shape=[jax.ShapeDtypeStruct((t, D_FF), BF16)] * 2,
        semantics=("parallel", "parallel"), args=(dh2b, w_down, g_act, u_act), exchange=exchange)


def _ffn_bwd_in(dg, du, wg_t, wu_t, h1, dh2, g2, exchange):
    t = dg.shape[0]
    tm, tk = min(512, t), D_FF // 2
    nk = D_FF // tk

    def body(dg_ref, du_ref, wg_ref, wu_ref, h_ref, d2_ref, g_ref, dh_ref, dhb_ref, gn_ref, acc_ref):
        i, kk = pl.program_id(0), pl.program_id(1)

        @pl.when((i == 0) & (kk == 0))
        def _():
            gn_ref[...] = jnp.zeros_like(gn_ref)

        @pl.when(kk == 0)
        def _():
            acc_ref[...] = jnp.zeros_like(acc_ref)

        acc_ref[...] += _dot(dg_ref[...], wg_ref[...]) + _dot(du_ref[...], wu_ref[...])

        @pl.when(kk == nk - 1)
        def _():
            dx, dg_rows = _rms_bwd(h_ref[...], g_ref[...], acc_ref[...])
            dh = d2_ref[...] + dx
            dh_ref[...] = dh
            dhb_ref[...] = dh.astype(BF16)
            gn_ref[...] += jnp.sum(dg_rows, axis=0, keepdims=True)

    act = pl.BlockSpec((tm, tk), lambda i, kk: (i, kk))
    wsp = pl.BlockSpec((tk, D_MODEL), lambda i, kk: (kk, 0))
    row = pl.BlockSpec((tm, D_MODEL), lambda i, kk: (i, 0))
    vec = pl.BlockSpec((1, D_MODEL), lambda i, kk: (0, 0))
    return _call(
        body, name="ffn_bwd_in", grid=(t // tm, nk),
        in_specs=[act, act, wsp, wsp, row, row, vec],
        out_specs=[row, row, vec],
        out_shape=[jax.ShapeDtypeStruct((t, D_MODEL), F32), jax.ShapeDtypeStruct((t, D_MODEL), BF16),
                   jax.ShapeDtypeStruct((1, D_MODEL), F32)],
        scratch=[pltpu.VMEM((tm, D_MODEL), F32)],
        semantics=("arbitrary", "arbitrary"), args=(dg, du, wg_t, wu_t, h1, dh2, g2), exchange=exchange)


def _mix_bwd(dh1b, w_out, proj, b_gate, y_ret, y_att, exchange):
    t = dh1b.shape[0]
    tm, tn = min(256, t), 512
    nb = D_MODEL // tn

    def body(d_ref, w_ref, glr_ref, gla_ref, br_ref, ba_ref, yr_ref, ya_ref,
             dyr_ref, dya_ref, dglr_ref, dgla_ref, dbr_ref, dba_ref):
        @pl.when(pl.program_id(1) == 0)
        def _():
            dbr_ref[...] = jnp.zeros_like(dbr_ref)
            dba_ref[...] = jnp.zeros_like(dba_ref)

        dz = _dot_nt(d_ref[...], w_ref[...])
        gr = _sigmoid(glr_ref[...].astype(F32) + br_ref[...])
        ga = _sigmoid(gla_ref[...].astype(F32) + ba_ref[...])
        dyr_ref[...] = (dz * gr).astype(BF16)
        dya_ref[...] = (dz * ga).astype(BF16)
        dglr = dz * yr_ref[...].astype(F32) * gr * (1.0 - gr)
        dgla = dz * ya_ref[...].astype(F32) * ga * (1.0 - ga)
        dglr_ref[...] = dglr.astype(BF16)
        dgla_ref[...] = dgla.astype(BF16)
        dbr_ref[...] += jnp.sum(dglr, axis=0, keepdims=True)
        dba_ref[...] += jnp.sum(dgla, axis=0, keepdims=True)

    blk = pl.BlockSpec((tm, tn), lambda j, i: (i, j))
    vec = pl.BlockSpec((1, tn), lambda j, i: (0, j))
    return _call(
        body, name="mix_bwd", grid=(nb, t // tm),
        in_specs=[pl.BlockSpec((tm, D_MODEL), lambda j, i: (i, 0)),
                  pl.BlockSpec((tn, D_MODEL), lambda j, i: (j, 0)),
                  pl.BlockSpec((tm, tn), lambda j, i: (i, C_GL // tn + j)),
                  pl.BlockSpec((tm, tn), lambda j, i: (i, C_GL // tn + nb + j)),
                  vec, pl.BlockSpec((1, tn), lambda j, i: (0, nb + j)), blk, blk],
        out_specs=[blk, blk, blk, blk, vec, vec],
        out_shape=[jax.ShapeDtypeStruct((t, D_MODEL), BF16)] * 4 + [jax.ShapeDtypeStruct((1, D_MODEL), F32)] * 2,
        semantics=("arbitrary", "arbitrary"), args=(dh1b, w_out, proj, proj, b_gate, b_gate, y_ret, y_att),
        exchange=exchange)


def _ret_bwd(dgro, proj, o_ret, qr, kr, cs, sn, lg_arr, batch, seq, exchange):
    t = batch * seq
    nt = seq // RET_TILE

    def body(dgro_ref, rg_ref, o_ref, qr_ref, kr_ref, v_ref, cs_ref, sn_ref, lg_ref,
             dq_ref, dk_ref, dv_ref, drg_ref, do_ref):
        lg = lg_ref[:, 0:1]
        e_qk, d_qk, row, col, diff, within = _decay_tiles(lg)
        e_kq = jnp.exp(-lg * diff)
        d_kq = jnp.where((row >> 6) <= (col >> 6), within, 0.0)

        for i in range(nt):
            rows = slice(i * RET_TILE, (i + 1) * RET_TILE)
            o = o_ref[rows, :]
            xc = o - jnp.mean(o, axis=-1, keepdims=True)
            rs = lax.rsqrt(jnp.mean(xc * xc, axis=-1, keepdims=True) + EPS)
            nrm = xc * rs
            rg = rg_ref[rows, :].astype(F32)
            sg = _sigmoid(rg)
            dg = dgro_ref[rows, :].astype(F32)
            drg_ref[rows, :] = (dg * nrm * sg * (1.0 + rg * (1.0 - sg))).astype(BF16)
            dn = dg * rg * sg
            do = rs * (dn - jnp.mean(dn, axis=-1, keepdims=True)
                       - nrm * jnp.mean(dn * nrm, axis=-1, keepdims=True))
            do_ref[rows, :] = do.astype(BF16)

        for i in range(nt):
            rows = slice(i * RET_TILE, (i + 1) * RET_TILE)
            doi = do_ref[rows, :]

            def pair_q(j, acc, i=i, doi=doi):
                off = pl.multiple_of(j * RET_TILE, RET_TILE)
                dp = _dot_nt(doi, v_ref[pl.ds(off, RET_TILE), :])
                far = jnp.exp(lg * (RET_TILE * (i - j)).astype(F32))
                return acc + _dot((dp * (e_qk * far)).astype(BF16), kr_ref[pl.ds(off, RET_TILE), :])

            acc = lax.fori_loop(0, i, pair_q, jnp.zeros((RET_TILE, RET_KEY_DIM), F32))
            dp = _dot_nt(doi, v_ref[rows, :])
            acc = acc + _dot((dp * d_qk).astype(BF16), kr_ref[rows, :])
            dq_ref[rows, :] = (acc * cs_ref[rows, :] - pltpu.roll(acc, 64, 1) * sn_ref[rows, :]).astype(BF16)

        for j in range(nt):
            rows = slice(j * RET_TILE, (j + 1) * RET_TILE)
            kj, vj = kr_ref[rows, :], v_ref[rows, :]

            def pair_k(i, carry, j=j, kj=kj, vj=vj):
                dk, dv = carry
                off = pl.multiple_of(i * RET_TILE, RET_TILE)
                qi, doi = qr_ref[pl.ds(off, RET_TILE), :], do_ref[pl.ds(off, RET_TILE), :]
                dec = e_kq * jnp.exp(lg * (RET_TILE * (i - j)).astype(F32))
                dk = dk + _dot((_dot_nt(vj, doi) * dec).astype(BF16), qi)
                dv = dv + _dot((_dot_nt(kj, qi) * dec).astype(BF16), doi)
                return dk, dv

            dk, dv = lax.fori_loop(j + 1, nt, pair_k, (jnp.zeros((RET_TILE, RET_KEY_DIM), F32),
                                                        jnp.zeros((RET_TILE, RET_VAL_DIM), F32)))
            qi, doi = qr_ref[rows, :], do_ref[rows, :]
            dk = dk + _dot((_dot_nt(vj, doi) * d_kq).astype(BF16), qi)
            dv = dv + _dot((_dot_nt(kj, qi) * d_kq).astype(BF16), doi)
            dk = (dk * cs_ref[rows, :] - pltpu.roll(dk, 64, 1) * sn_ref[rows, :]) * (RET_KEY_DIM ** -0.5)
            dk_ref[rows, :] = dk.astype(BF16)
            dv_ref[rows, :] = dv.astype(BF16)

    key = pl.BlockSpec((seq, RET_KEY_DIM), lambda b, h: (b, h))
    val = pl.BlockSpec((seq, RET_VAL_DIM), lambda b, h: (b, h))
    tab = pl.BlockSpec((seq, RET_KEY_DIM), lambda b, h: (0, 0))
    return _call(
        body, name="ret_bwd", grid=(batch, RET_HEADS),
        in_specs=[val, pl.BlockSpec((seq, RET_VAL_DIM), lambda b, h: (b, C_RG // RET_VAL_DIM + h)), val, key, key,
                  pl.BlockSpec((seq, RET_VAL_DIM), lambda b, h: (b, C_RV // RET_VAL_DIM + h)), tab, tab,
                  pl.BlockSpec((None, 1, LANES), lambda b, h: (h, 0, 0))],
        out_specs=[key, key, val, val],
        out_shape=[jax.ShapeDtypeStruct((t, RET_HEADS * RET_KEY_DIM), BF16)] * 2
                  + [jax.ShapeDtypeStruct((t, RET_HEADS * RET_VAL_DIM), BF16)] * 2,
        scratch=[pltpu.VMEM((seq, RET_VAL_DIM), BF16)],
        semantics=("parallel", "parallel"), args=(dgro, proj, o_ret, qr, kr, proj, cs, sn, lg_arr),
        exchange=exchange)


def _att_bwd(proj, kpad, vpad, wvec, dao, batch, seq):
    ni, q_spec, kv_spec, w_spec = _att_specs(batch, seq)
    t = batch * seq

    def body(q_ref, k_ref, v_ref, w_ref, do_ref, dq_ref, dk_ref, dv_ref, dw_ref,
             bias_ref, dbias_ref, dk_acc, dv_acc):
        b, i = pl.program_id(1), pl.program_id(2)

        @pl.when((b == 0) & (i == 0))
        def _():
            _att_bias(w_ref, bias_ref)
            dbias_ref[...] = jnp.zeros_like(dbias_ref)

        @pl.when(i == 0)
        def _():
            dk_acc[...] = jnp.zeros_like(dk_acc)
            dv_acc[...] = jnp.zeros_like(dv_acc)

        win = pl.ds(pl.multiple_of(i * ATT_Q, ATT_Q), ATT_WIN)
        k2, v2, q2, do2 = k_ref[win, :], v_ref[win, :], q_ref[...], do_ref[...]
        start_ok = lax.broadcasted_iota(jnp.int32, (ATT_Q, ATT_WIN), 1) + (i * ATT_Q - ATT_PAD) >= 0
        lo = lax.broadcasted_iota(jnp.int32, (1, LANES), 1) < 64
        dq = jnp.zeros((ATT_Q, LANES), F32)
        dk = jnp.zeros((ATT_WIN, LANES), F32)
        dv = jnp.zeros((ATT_WIN, LANES), F32)
        for e in range(2):
            sel = lo if e == 0 else jnp.logical_not(lo)
            p = _att_scores(q2, k2, bias_ref[e], sel, start_ok)
            dom = jnp.where(sel, do2, jnp.zeros_like(do2))
            dp = _dot_nt(dom, v2)
            ds = p * (dp - jnp.sum(dp * p, axis=-1, keepdims=True))
            dbias_ref[e] += ds
            dsb = (ds * (1.0 / 8.0)).astype(BF16)
            dq = dq + _dot(dsb, jnp.where(sel, k2, jnp.zeros_like(k2)))
            dk = dk + _dot_tn(dsb, jnp.where(sel, q2, jnp.zeros_like(q2)))
            dv = dv + _dot_tn(p.astype(BF16), dom)
        dq_ref[...] = dq.astype(BF16)
        dk_acc[win, :] += dk
        dv_acc[win, :] += dv

        @pl.when(i == ni - 1)
        def _():
            dk_ref[...] = dk_acc[ATT_PAD:, :].astype(BF16)
            dv_ref[...] = dv_acc[ATT_PAD:, :].astype(BF16)

        @pl.when((b == batch - 1) & (i == ni - 1))
        def _():
            n_i = lax.broadcasted_iota(jnp.int32, (ATT_Q, BIAS_LEN), 0)
            for e in range(2):
                xw = jnp.concatenate([jnp.zeros((ATT_Q, BIAS_LEN - ATT_WIN), F32), dbias_ref[e]], axis=1)
                for bit in range(8):
                    xw = jnp.where(((n_i >> bit) & 1) == 1, pltpu.roll(xw, BIAS_LEN - (1 << bit), 1), xw)
                dw_ref[e:e + 1, :] = jnp.sum(xw, axis=0, keepdims=True)

    seq_blk = pl.BlockSpec((seq, LANES), lambda hp, b, i: (b, hp))
    q_out = pl.BlockSpec((ATT_Q, LANES), lambda hp, b, i: (b * ni + i, hp))
    return pl.pallas_call(
        body, name="att_bwd", grid=(ATT_HEADS // 2, batch, ni),
        in_specs=[q_spec, kv_spec, kv_spec, w_spec, q_out],
        out_specs=[q_out, seq_blk, seq_blk, w_spec],
        out_shape=[jax.ShapeDtypeStruct((t, 512), BF16)] * 3
                  + [jax.ShapeDtypeStruct((ATT_HEADS // 2, 2, BIAS_LEN), F32)],
        scratch_shapes=[pltpu.VMEM((2, ATT_Q, ATT_WIN), F32), pltpu.VMEM((2, ATT_Q, ATT_WIN), F32),
                        pltpu.VMEM((seq + ATT_PAD, LANES), F32), pltpu.VMEM((seq + ATT_PAD, LANES), F32)],
        compiler_params=_params("arbitrary", "arbitrary", "arbitrary"),
    )(proj, kpad, vpad, wvec, dao)


def _rms_in_bwd(x2, dxn, dh1, g1):
    t = x2.shape[0]
    tm = min(512, t)

    def body(x_ref, d_ref, h_ref, g_ref, dx_ref, dg_ref):
        @pl.when(pl.program_id(0) == 0)
        def _():
            dg_ref[...] = jnp.zeros_like(dg_ref)

        dx, dg_rows = _rms_bwd(x_ref[...], g_ref[...], d_ref[...])
        dx_ref[...] = h_ref[...] + dx
        dg_ref[...] += jnp.sum(dg_rows, axis=0, keepdims=True)

    row = pl.BlockSpec((tm, D_MODEL), lambda i: (i, 0))
    vec = pl.BlockSpec((1, D_MODEL), lambda i: (0, 0))
    return pl.pallas_call(
        body, name="rms_in_bwd", grid=(t // tm,),
        in_specs=[row, row, row, vec], out_specs=[row, vec],
        out_shape=[jax.ShapeDtypeStruct((t, D_MODEL), F32), jax.ShapeDtypeStruct((1, D_MODEL), F32)],
        compiler_params=_params("arbitrary"),
    )(x2, dxn, dh1, g1)


def _pack_small(dg1, dbr, dba, dg2, dg3, dw):
    def body(a_ref, b_ref, c_ref, d_ref, e_ref, w_ref, o_ref):
        o_ref[...] = jnp.zeros_like(o_ref)
        for r, ref in enumerate((a_ref, b_ref, c_ref, d_ref, e_ref)):
            o_ref[r:r + 1, :] = ref[...]
        for hp in range(ATT_HEADS // 2):
            o_ref[8 + 2 * hp:10 + 2 * hp, :] = w_ref[hp]

    return pl.pallas_call(body, name="pack_small",
                          out_shape=jax.ShapeDtypeStruct((16, D_MODEL), F32))(dg1, dbr, dba, dg2, dg3, dw)


def _rotary_tables(seq):
    freqs = ROPE_BASE ** (-jnp.arange(0, RET_KEY_DIM, 2, dtype=F32) / RET_KEY_DIM)
    ang = jnp.arange(seq, dtype=F32)[:, None] * freqs[None, :]
    cos, sin = jnp.cos(ang), jnp.sin(ang)
    return jnp.concatenate([cos, cos], axis=1), jnp.concatenate([-sin, sin], axis=1)


def _bias_rows(rel_bias):
    n_far = BIAS_LEN - ATT_Q - MAX_REL + 1
    n_near = BIAS_LEN - n_far - (N_REL - 2)
    w = jnp.concatenate([jnp.broadcast_to(rel_bias[:, N_REL - 1:], (ATT_HEADS, n_far)),
                         rel_bias[:, 1:N_REL - 1][:, ::-1],
                         jnp.broadcast_to(rel_bias[:, :1], (ATT_HEADS, n_near))], axis=1)
    return w.reshape(ATT_HEADS // 2, 2, BIAS_LEN)


def _bias_rows_bwd(dw):
    n_far = BIAS_LEN - ATT_Q - MAX_REL + 1
    mid = dw[:, n_far:n_far + N_REL - 2][:, ::-1]
    return jnp.concatenate([jnp.sum(dw[:, n_far + N_REL - 2:], axis=1, keepdims=True), mid,
                            jnp.sum(dw[:, :n_far], axis=1, keepdims=True)], axis=1)


def _pad_keys(a, batch, seq):
    a = a.reshape(batch, seq, a.shape[-1])
    return jnp.pad(a, ((0, 0), (ATT_PAD, 0), (0, 0))).reshape(batch * (seq + ATT_PAD), a.shape[-1])


def _step(x, tgt, norm_mix, b_gate, norm_ffn, norm_final, rel_bias_shard, shard):
    batch, seq, _ = x.shape
    t = batch * seq
    n_rb = rel_bias_shard.shape[-1]
    x2, tgt2 = x.reshape(t, D_MODEL), tgt.reshape(t, D_MODEL)
    g3 = norm_final.reshape(1, D_MODEL)
    cs, sn = _rotary_tables(seq)
    lg = np.log(1.0 - 2.0 ** (-5.0 - np.arange(RET_HEADS, dtype=np.float32))).astype(np.float32)
    lg_arr = jnp.asarray(np.broadcast_to(lg[:, None, None], (RET_HEADS, 1, LANES)))

    def gather(*names):
        return _Exchange([shard[nm] for nm in names], scatter=False)

    def scatter(*grads):
        return _Exchange(grads, scatter=True)

    rb_pad = jnp.pad(rel_bias_shard, ((0, 0), (0, LANES - n_rb)))
    w_in_t, rb_full = _exchange([shard["w_in_t"], rb_pad], False, "gather_w_in")
    rb_full = rb_full.reshape(N_DEV, ATT_HEADS, LANES)[:, :, :n_rb]
    wvec = _bias_rows(jnp.transpose(rb_full, (1, 0, 2)).reshape(ATT_HEADS, N_DEV * n_rb))

    xn = _rms_fwd(x2, norm_mix)
    proj, (w_ret, w_att_t, w_out, w_gate_t) = _mm(
        xn, w_in_t, tb=True, out_dtype=BF16, tm=1024, tn=1664, tk=1024, name="proj",
        exchange=gather("w_ret", "w_att_t", "w_out", "w_gate_t"))
    (gro, o_ret, qr, kr), (w_up_t, w_down) = _ret_fwd(proj, cs, sn, lg_arr, batch, seq, gather("w_up_t", "w_down"))
    kpad = _pad_keys(proj[:, C_AK:C_AV], batch, seq)
    vpad = _pad_keys(proj[:, C_AV:C_GL], batch, seq)
    ao = _att_fwd(proj, kpad, vpad, wvec, batch, seq)
    z, y_ret, y_att = _mix_fwd(gro, ao, proj, b_gate, w_ret, w_att_t)
    h1, hn = _out_fwd(z, x2, w_out, norm_ffn)
    g_act, u_act, a_act = _ffn_up(hn, w_gate_t, w_up_t)
    dh2, dh2b, loss, dg3 = _ffn_down_loss(a_act, h1, tgt2, w_down, g3)

    wg = dict(out_dtype=BF16, tn=1024, ta=True)
    slots = {}
    dw_down = _mm(a_act, dh2b, tm=1408, tk=1024, name="dw_down", **wg)
    (d_gact, d_uact), (slots["w_down"],) = _ffn_bwd_act(dh2b, w_down, g_act, u_act, scatter(dw_down))
    dw_gate = _mm(d_gact, hn, tm=1408, tk=1024, name="dw_gate", **wg)
    dw_up, (slots["w_gate_t"],) = _mm(d_uact, hn, tm=1408, tk=1024, name="dw_up", exchange=scatter(dw_gate), **wg)
    (dh1, dh1b, dg2), (slots["w_up_t"],) = _ffn_bwd_in(d_gact, d_uact, w_gate_t, w_up_t, h1, dh2, norm_ffn,
                                                     scatter(dw_up))
    dw_out = _mm(z, dh1b, tm=1024, tk=2048, name="dw_out", **wg)
    (dyr, dya, dglr, dgla, dbr, dba), (slots["w_out"],) = _mix_bwd(dh1b, w_out, proj, b_gate, y_ret, y_att,
                                                                scatter(dw_out))
    dgro = _mm(dyr, w_ret, tb=True, out_dtype=BF16, tm=1024, tn=1024, tk=1024, name="dgro")
    dao = _mm(dya, w_att_t, out_dtype=BF16, tm=1024, tn=512, tk=1024, name="dao")
    dw_ret = _mm(gro, dyr, tm=1024, tk=2048, name="dw_ret", **wg)
    dw_att = _mm(dya, ao, tm=1024, tk=2048, name="dw_att", **wg)
    (drq, drk, drv, drg), (slots["w_ret"], slots["w_att_t"]) = _ret_bwd(
        dgro, proj, o_ret, qr, kr, cs, sn, lg_arr, batch, seq, scatter(dw_ret, dw_att))
    daq, dak, dav, dw = _att_bwd(proj, kpad, vpad, wvec, dao, batch, seq)
    dproj = jnp.concatenate([drq, drk, drv, drg, daq, dak, dav, dglr, dgla], axis=1)
    dw_in = _mm(dproj, xn, tm=512, tk=2048, name="dw_in", **wg)
    dxn, (slots["w_in_t"],) = _mm(dproj, w_in_t, out_dtype=F32, tm=1024, tn=1024, tk=1664, name="dxn",
                                  exchange=scatter(dw_in))
    dx, dg1 = _rms_in_bwd(x2, dxn, dh1, norm_mix)
    small = _pack_small(dg1, dbr, dba, dg2, dg3, dw)
    (small_slots,) = _exchange([small], False, "gather_small")
    return loss[0, 0], dx.reshape(batch, seq, D_MODEL), slots, small_slots.reshape(N_DEV, 16, D_MODEL)


def _sum_slots(slots, name):
    _, r, c = slots.shape
    tr = max(d for d in range(16, r + 1, 16) if r % d == 0 and (d * c <= 256 * 1024 or d == 16))

    def body(s_ref, o_ref):
        acc = s_ref[0].astype(F32)
        for s in range(1, N_DEV):
            acc = acc + s_ref[s].astype(F32)
        o_ref[...] = acc

    return pl.pallas_call(
        body, name=name, grid=(r // tr,),
        in_specs=[pl.BlockSpec((N_DEV, tr, c), lambda i: (0, i, 0))],
        out_specs=pl.BlockSpec((tr, c), lambda i: (i, 0)),
        out_shape=jax.ShapeDtypeStruct((r, c), F32),
        compiler_params=_params("parallel"),
    )(slots)


def _adamw_math(w, g, m, v):
    m = ADAM_B1 * m + (1.0 - ADAM_B1) * g
    v = ADAM_B2 * v + (1.0 - ADAM_B2) * (g * g)
    m_hat = m / (1.0 - ADAM_B1 ** ADAM_STEP)
    v_hat = v / (1.0 - ADAM_B2 ** ADAM_STEP)
    return -ADAM_LR * (m_hat / (jnp.sqrt(v_hat) + ADAM_EPS) + ADAM_WD * w), m, v


def _adamw(w, g, m, v, name):
    r, c = w.shape
    tr = r
    while tr * c > 128 * 1024 and tr % 16 == 0:
        tr //= 2

    def body(w_ref, g_ref, m_ref, v_ref, d_ref, nm_ref, nv_ref):
        d_ref[...], nm_ref[...], nv_ref[...] = _adamw_math(w_ref[...], g_ref[...], m_ref[...], v_ref[...])

    blk = pl.BlockSpec((tr, c), lambda i: (i, 0))
    return pl.pallas_call(
        body, name=name, grid=(r // tr,),
        in_specs=[blk] * 4, out_specs=[blk] * 3,
        out_shape=[jax.ShapeDtypeStruct((r, c), F32)] * 3,
        compiler_params=_params("parallel"),
    )(w, g, m, v)


def _adamw_small(ws, gs, ms, vs):
    n = len(ws)

    def body(*refs):
        for i in range(n):
            w_ref, g_ref, m_ref, v_ref = (refs[j * n + i] for j in range(4))
            d_ref, nm_ref, nv_ref = (refs[(4 + j) * n + i] for j in range(3))
            d_ref[...], nm_ref[...], nv_ref[...] = _adamw_math(w_ref[...], g_ref[...], m_ref[...], v_ref[...])

    shapes = [jax.ShapeDtypeStruct(w.shape, F32) for w in ws]
    outs = pl.pallas_call(body, name="adamw_small", out_shape=shapes * 3)(*ws, *gs, *ms, *vs)
    return outs[:n], outs[n:2 * n], outs[2 * n:]


def kernel(x, norm_mix, w_in, b_gate, rel_bias, w_ret_out, w_att_out, w_out, norm_ffn, w_ffn_gate, w_ffn_up, w_ffn_down, norm_final, loss_target, m_norm_mix, m_w_in, m_b_gate, m_rel_bias, m_w_ret_out, m_w_att_out, m_w_out, m_norm_ffn, m_w_ffn_gate, m_w_ffn_up, m_w_ffn_down, m_norm_final, v_norm_mix, v_w_in, v_b_gate, v_rel_bias, v_w_ret_out, v_w_att_out, v_w_out, v_norm_ffn, v_w_ffn_gate, v_w_ffn_up, v_w_ffn_down, v_norm_final):
    me = _index(_place())
    n_rb = rel_bias.shape[-1]

    shard = dict(w_in_t=w_in[0].T, w_gate_t=w_ffn_gate[0].T, w_up_t=w_ffn_up[0].T, w_down=w_ffn_down[0],
                 w_ret=w_ret_out[0], w_out=w_out[0], w_att_t=w_att_out[0].T)
    shard = {nm: s.astype(BF16) for nm, s in shard.items()}
    loss, dx, slots, small_slots = _step(x, loss_target, norm_mix, b_gate, norm_ffn, norm_final, rel_bias[0], shard)
    loss = lax.psum(loss, ("x", "y", "c"))
    summed = {nm: _sum_slots(s, "sum_" + nm) for nm, s in slots.items()}
    small_sum = _sum_slots(small_slots, "sum_small")

    g = dict(
        w_in=summed["w_in_t"].T, w_ffn_gate=summed["w_gate_t"].T, w_ffn_up=summed["w_up_t"].T,
        w_ffn_down=summed["w_down"], w_ret_out=summed["w_ret"], w_out=summed["w_out"], w_att_out=summed["w_att_t"].T,
        norm_mix=small_sum[0:1], b_gate=jnp.concatenate([small_sum[1:2], small_sum[2:3]], axis=1),
        norm_ffn=small_sum[3:4], norm_final=small_sum[4:5],
        rel_bias=lax.dynamic_slice_in_dim(_bias_rows_bwd(small_sum[8:16]), me * n_rb, n_rb, axis=1),
    )
    w = dict(norm_mix=norm_mix, w_in=w_in, b_gate=b_gate, rel_bias=rel_bias, w_ret_out=w_ret_out, w_att_out=w_att_out,
             w_out=w_out, norm_ffn=norm_ffn, w_ffn_gate=w_ffn_gate, w_ffn_up=w_ffn_up, w_ffn_down=w_ffn_down,
             norm_final=norm_final)
    m = dict(norm_mix=m_norm_mix, w_in=m_w_in, b_gate=m_b_gate, rel_bias=m_rel_bias, w_ret_out=m_w_ret_out,
             w_att_out=m_w_att_out, w_out=m_w_out, norm_ffn=m_norm_ffn, w_ffn_gate=m_w_ffn_gate, w_ffn_up=m_w_ffn_up,
             w_ffn_down=m_w_ffn_down, norm_final=m_norm_final)
    v = dict(norm_mix=v_norm_mix, w_in=v_w_in, b_gate=v_b_gate, rel_bias=v_rel_bias, w_ret_out=v_w_ret_out,
             w_att_out=v_w_att_out, w_out=v_w_out, norm_ffn=v_norm_ffn, w_ffn_gate=v_w_ffn_gate, w_ffn_up=v_w_ffn_up,
             w_ffn_down=v_w_ffn_down, norm_final=v_norm_final)
    order = ("norm_mix", "w_in", "b_gate", "rel_bias", "w_ret_out", "w_att_out", "w_out", "norm_ffn",
             "w_ffn_gate", "w_ffn_up", "w_ffn_down", "norm_final")
    small_names = ("norm_mix", "b_gate", "rel_bias", "norm_ffn", "norm_final")

    def flat(a):
        return a[0] if a.ndim == 3 else a.reshape(-1, a.shape[-1])

    grad, delta, new_m, new_v = {}, {}, {}, {}
    for nm in order:
        if nm not in small_names:
            d, nmom, nvar = _adamw(flat(w[nm]), g[nm], flat(m[nm]), flat(v[nm]), "adamw_" + nm)
            grad[nm], delta[nm], new_m[nm], new_v[nm] = (a.reshape(w[nm].shape) for a in (g[nm], d, nmom, nvar))
    ds, nms, nvs = _adamw_small([flat(w[nm]) for nm in small_names], [g[nm] for nm in small_names],
                                [flat(m[nm]) for nm in small_names], [flat(v[nm]) for nm in small_names])
    for i, nm in enumerate(small_names):
        grad[nm], delta[nm], new_m[nm], new_v[nm] = (a.reshape(w[nm].shape) for a in (g[nm], ds[i], nms[i], nvs[i]))

    return (loss, dx, *[grad[nm] for nm in order], *[delta[nm] for nm in order],
            *[new_m[nm] for nm in order], *[new_v[nm] for nm in order])
```

```python
import numpy as np
import jax
import jax.numpy as jnp
from jax import lax
from jax.experimental import pallas as pl
from jax.experimental.pallas import tpu as pltpu

F32 = jnp.float32
BF16 = jnp.bfloat16
MESH = pl.DeviceIdType.MESH

D_MODEL = 1024
CHUNK = 64
RET_HEADS = 4
RET_KEY_DIM = 128
RET_VAL_DIM = 256
ATT_HEADS = 8
BAND_CHUNKS = 8
MAX_REL = 256
N_REL = CHUNK + MAX_REL
D_FF = 2816
N_IN = 6656
ROPE_BASE = 10000.0
EPS = 1e-6
NEG_INF = -1e30
C_RQ, C_RK, C_RV, C_RG, C_AQ, C_AK, C_AV, C_GL = 0, 512, 1024, 2048, 3072, 3584, 4096, 4608

ADAM_LR = 0.001
ADAM_B1 = 0.9
ADAM_B2 = 0.999
ADAM_EPS = 1e-08
ADAM_WD = 0.01
ADAM_STEP = 10

N_DEV = 8
LANES = 128
RET_TILE = 256
ATT_Q = 256
ATT_PAD = BAND_CHUNKS * CHUNK
ATT_WIN = ATT_PAD + ATT_Q
BIAS_LEN = 1024
VMEM_LIMIT = 48 * 1024 * 1024


def _params(*sem):
    return pltpu.CompilerParams(dimension_semantics=sem, vmem_limit_bytes=VMEM_LIMIT)


def _dot(a, b):
    return lax.dot_general(a, b, (((1,), (0,)), ((), ())), preferred_element_type=F32)


def _dot_nt(a, b):
    return lax.dot_general(a, b, (((1,), (1,)), ((), ())), preferred_element_type=F32)


def _dot_tn(a, b):
    return lax.dot_general(a, b, (((0,), (0,)), ((), ())), preferred_element_type=F32)


def _sigmoid(x):
    return 1.0 / (1.0 + jnp.exp(-x))


def _rms_bwd(x, g, dy):
    r = lax.rsqrt(jnp.mean(x * x, axis=-1, keepdims=True) + EPS)
    u = dy * g
    dx = r * u - x * (r * r * r) * jnp.mean(u * x, axis=-1, keepdims=True)
    return dx, dy * x * r


def _place():
    return lax.axis_index("x"), lax.axis_index("y"), lax.axis_index("c")


def _peer(k):
    x, y, c = _place()
    return ((1 - x) if k & 4 else x, (1 - y) if k & 2 else y, (1 - c) if k & 1 else c)


def _index(place):
    return 4 * place[0] + 2 * place[1] + place[2]


def _rows(ref, block, nrows):
    align = 16 if ref.dtype == BF16 else 8
    return ref.at[pl.ds(pl.multiple_of(block * nrows, align), nrows)]


class _Exchange:
    def __init__(self, arrays, scatter):
        self.arrays, self.scatter, self.n = list(arrays), scatter, len(arrays)

    def out_shape(self):
        if self.scatter:
            return [jax.ShapeDtypeStruct((N_DEV, a.shape[0] // N_DEV) + a.shape[1:], a.dtype) for a in self.arrays]
        return [jax.ShapeDtypeStruct((N_DEV * a.shape[0],) + a.shape[1:], a.dtype) for a in self.arrays]

    def scratch(self):
        return [pltpu.SemaphoreType.DMA((self.n, N_DEV - 1)), pltpu.SemaphoreType.DMA((self.n, N_DEV - 1)),
                pltpu.SemaphoreType.DMA((self.n,))]

    def _copies(self, ins, outs, sems):
        send_sems, recv_sems, local_sems = sems
        me = _index(_place())

        def src(w, to):
            return _rows(ins[w], to, ins[w].shape[0] // N_DEV) if self.scatter else ins[w]

        def dst(w, origin):
            return outs[w].at[origin] if self.scatter else _rows(outs[w], origin, ins[w].shape[0])

        def remote(w, k, to, origin):
            return pltpu.make_async_remote_copy(src_ref=src(w, to), dst_ref=dst(w, origin),
                                                send_sem=send_sems.at[w, k - 1], recv_sem=recv_sems.at[w, k - 1],
                                                device_id=_peer(k), device_id_type=MESH)

        pairs = [(w, k) for w in range(self.n) for k in range(1, N_DEV)]
        own = lambda: [pltpu.make_async_copy(src(w, me), dst(w, me), local_sems.at[w]) for w in range(self.n)]
        sent = lambda: [remote(w, k, _index(_peer(k)), me) for w, k in pairs]
        arriving = lambda: [remote(w, k, me, _index(_peer(k))) for w, k in pairs]
        return own, sent, arriving

    def start(self, ins, outs, sems):
        own, sent, _ = self._copies(ins, outs, sems)
        for cp in own() + sent():
            cp.start()

    def wait(self, ins, outs, sems):
        own, sent, arriving = self._copies(ins, outs, sems)
        for cp in arriving():
            cp.wait_recv()
        for cp in sent():
            cp.wait_send()
        for cp in own():
            cp.wait()


def _call(body, *, name, grid, in_specs, out_specs, out_shape, scratch=(), semantics, args, exchange=None):
    if exchange is None:
        return pl.pallas_call(body, name=name, grid=grid, in_specs=in_specs, out_specs=out_specs, out_shape=out_shape,
                              scratch_shapes=list(scratch), compiler_params=_params(*semantics))(*args), None
    n_in, n_out, n_scr, nx = len(in_specs), len(out_specs), len(scratch), exchange.n

    def full_body(*refs):
        ins, refs = refs[:n_in], refs[n_in:]
        x_in, refs = refs[:nx], refs[nx:]
        outs, refs = refs[:n_out], refs[n_out:]
        x_out, refs = refs[:nx], refs[nx:]
        scr, sems = refs[:n_scr], refs[n_scr:]
        first, last = True, True
        for axis, size in enumerate(grid):
            first = jnp.logical_and(first, pl.program_id(axis) == 0)
            last = jnp.logical_and(last, pl.program_id(axis) == size - 1)
        if grid:
            pl.when(first)(lambda: exchange.start(x_in, x_out, sems))
        else:
            exchange.start(x_in, x_out, sems)
        body(*ins, *outs, *scr)
        if grid:
            pl.when(last)(lambda: exchange.wait(x_in, x_out, sems))
        else:
            exchange.wait(x_in, x_out, sems)

    hbm = pl.BlockSpec(memory_space=pltpu.HBM)
    res = pl.pallas_call(
        full_body, name=name, grid=grid,
        in_specs=list(in_specs) + [hbm] * nx, out_specs=list(out_specs) + [hbm] * nx,
        out_shape=list(out_shape) + exchange.out_shape(),
        scratch_shapes=list(scratch) + exchange.scratch(),
        compiler_params=_params(*(["arbitrary"] * len(grid))),
    )(*args, *exchange.arrays)
    return res[:n_out], res[n_out:]


def _exchange(arrays, scatter, name):
    return _call(lambda: None, name=name, grid=(), in_specs=[], out_specs=[], out_shape=[], semantics=(),
                 args=(), exchange=_Exchange(arrays, scatter))[1]


def _mm(a, b, *, ta=False, tb=False, out_dtype, tm, tn, tk, name, exchange=None):
    m, k = (a.shape[1], a.shape[0]) if ta else a.shape
    n = b.shape[0] if tb else b.shape[1]
    assert k == (b.shape[1] if tb else b.shape[0])
    tm, tn, tk = min(tm, m), min(tn, n), min(tk, k)
    assert m % tm == 0 and n % tn == 0 and k % tk == 0, (name, m, n, k)
    nk = k // tk
    dims = (((0 if ta else 1,), (1 if tb else 0,)), ((), ()))

    def body(a_ref, b_ref, o_ref, *acc):
        prod = lax.dot_general(a_ref[...].astype(BF16), b_ref[...].astype(BF16), dims, preferred_element_type=F32)
        if nk == 1:
            o_ref[...] = prod.astype(o_ref.dtype)
            return
        acc_ref, kk = acc[0], pl.program_id(2)

        @pl.when(kk == 0)
        def _():
            acc_ref[...] = prod

        @pl.when((kk > 0) & (kk < nk - 1))
        def _():
            acc_ref[...] += prod

        @pl.when(kk == nk - 1)
        def _():
            o_ref[...] = (acc_ref[...] + prod).astype(o_ref.dtype)

    a_spec = (pl.BlockSpec((tk, tm), lambda i, j, kk: (kk, i)) if ta
              else pl.BlockSpec((tm, tk), lambda i, j, kk: (i, kk)))
    b_spec = (pl.BlockSpec((tn, tk), lambda i, j, kk: (j, kk)) if tb
              else pl.BlockSpec((tk, tn), lambda i, j, kk: (kk, j)))
    (out,), moved = _call(
        body, name=name, grid=(m // tm, n // tn, nk),
        in_specs=[a_spec, b_spec],
        out_specs=[pl.BlockSpec((tm, tn), lambda i, j, kk: (i, j))],
        out_shape=[jax.ShapeDtypeStruct((m, n), out_dtype)],
        scratch=[pltpu.VMEM((tm, tn), F32)] if nk > 1 else [],
        semantics=("parallel", "parallel", "arbitrary"), args=(a, b), exchange=exchange)
    return out if exchange is None else (out, moved)


def _rms_fwd(x2, g):
    t = x2.shape[0]
    tm = min(512, t)

    def body(x_ref, g_ref, o_ref):
        x = x_ref[...]
        r = lax.rsqrt(jnp.mean(x * x, axis=-1, keepdims=True) + EPS)
        o_ref[...] = (x * r * g_ref[...]).astype(o_ref.dtype)

    return pl.pallas_call(
        body, name="rms_in_fwd", grid=(t // tm,),
        in_specs=[pl.BlockSpec((tm, D_MODEL), lambda i: (i, 0)), pl.BlockSpec((1, D_MODEL), lambda i: (0, 0))],
        out_specs=pl.BlockSpec((tm, D_MODEL), lambda i: (i, 0)),
        out_shape=jax.ShapeDtypeStruct((t, D_MODEL), BF16),
        compiler_params=_params("parallel"),
    )(x2, g)


def _decay(lg):
    row = lax.broadcasted_iota(jnp.int32, (RET_TILE, RET_TILE), 0)
    col = lax.broadcasted_iota(jnp.int32, (RET_TILE, RET_TILE), 1)
    within = jnp.exp(lg * jnp.abs(row - col).astype(F32))
    inside = jnp.where((col >> 6) <= (row >> 6), within, 0.0)
    pos = lax.broadcasted_iota(jnp.int32, (RET_TILE, 1), 0).astype(F32)
    q_dec = jnp.exp(lg * (pos + 1.0))
    k_dec = jnp.exp(lg * (RET_TILE - 1.0 - pos))
    tile_dec = jnp.exp(lg * float(RET_TILE))
    return inside, q_dec, k_dec, tile_dec


def _scaled(a_bf16, dec):
    return (a_bf16.astype(F32) * dec).astype(BF16)


def _ret_fwd(proj, cs, sn, lg_arr, batch, seq, exchange):
    t = batch * seq
    nt = seq // RET_TILE

    def body(q_ref, k_ref, v_ref, rg_ref, cs_ref, sn_ref, lg_ref, gro_ref, o_ref, qr_ref, kr_ref):
        lg = lg_ref[:, 0:1]
        cs_t, sn_t = cs_ref[...], sn_ref[...]
        q = q_ref[...].astype(F32)
        k = k_ref[...].astype(F32)
        qr_ref[...] = (q * cs_t + pltpu.roll(q, 64, 1) * sn_t).astype(BF16)
        kr_ref[...] = ((k * cs_t + pltpu.roll(k, 64, 1) * sn_t) * (RET_KEY_DIM ** -0.5)).astype(BF16)
        inside, q_dec, k_dec, tile_dec = _decay(lg)
        state = jnp.zeros((RET_KEY_DIM, RET_VAL_DIM), F32)
        for i in range(nt):
            rows = slice(i * RET_TILE, (i + 1) * RET_TILE)
            qi, ki, vi = qr_ref[rows, :], kr_ref[rows, :], v_ref[rows, :]
            acc = _dot((_dot_nt(qi, ki) * inside).astype(BF16), vi)
            if i > 0:
                acc = acc + _dot(_scaled(qi, q_dec), state.astype(BF16))
            if i < nt - 1:
                state = state * tile_dec + _dot_tn(_scaled(ki, k_dec), vi)
            o_ref[rows, :] = acc
            xc = acc - jnp.mean(acc, axis=-1, keepdims=True)
            nrm = xc * lax.rsqrt(jnp.mean(xc * xc, axis=-1, keepdims=True) + EPS)
            rg = rg_ref[rows, :].astype(F32)
            gro_ref[rows, :] = (rg * _sigmoid(rg) * nrm).astype(BF16)

    def col(base, width):
        return lambda b, h: (b, base // width + h)

    return _call(
        body, name="ret_fwd", grid=(batch, RET_HEADS),
        in_specs=[pl.BlockSpec((seq, RET_KEY_DIM), col(C_RQ, RET_KEY_DIM)),
                  pl.BlockSpec((seq, RET_KEY_DIM), col(C_RK, RET_KEY_DIM)),
                  pl.BlockSpec((seq, RET_VAL_DIM), col(C_RV, RET_VAL_DIM)),
                  pl.BlockSpec((seq, RET_VAL_DIM), col(C_RG, RET_VAL_DIM)),
                  pl.BlockSpec((seq, RET_KEY_DIM), lambda b, h: (0, 0)),
                  pl.BlockSpec((seq, RET_KEY_DIM), lambda b, h: (0, 0)),
                  pl.BlockSpec((None, 1, LANES), lambda b, h: (h, 0, 0))],
        out_specs=[pl.BlockSpec((seq, RET_VAL_DIM), lambda b, h: (b, h)),
                   pl.BlockSpec((seq, RET_VAL_DIM), lambda b, h: (b, h)),
                   pl.BlockSpec((seq, RET_KEY_DIM), lambda b, h: (b, h)),
                   pl.BlockSpec((seq, RET_KEY_DIM), lambda b, h: (b, h))],
        out_shape=[jax.ShapeDtypeStruct((t, RET_HEADS * RET_VAL_DIM), BF16),
                   jax.ShapeDtypeStruct((t, RET_HEADS * RET_VAL_DIM), F32),
                   jax.ShapeDtypeStruct((t, RET_HEADS * RET_KEY_DIM), BF16),
                   jax.ShapeDtypeStruct((t, RET_HEADS * RET_KEY_DIM), BF16)],
        semantics=("parallel", "parallel"), args=(proj, proj, proj, proj, cs, sn, lg_arr), exchange=exchange)


def _att_bias(w_ref, bias_ref):
    n_i = lax.broadcasted_iota(jnp.int32, (ATT_Q, BIAS_LEN), 0)
    qc = lax.broadcasted_iota(jnp.int32, (ATT_Q, ATT_WIN), 0) >> 6
    kc = lax.broadcasted_iota(jnp.int32, (ATT_Q, ATT_WIN), 1) >> 6
    dc = qc + BAND_CHUNKS - kc
    band = (dc >= 0) & (dc <= BAND_CHUNKS)
    for e in range(2):
        xw = jnp.broadcast_to(w_ref[e:e + 1, :], (ATT_Q, BIAS_LEN))
        for bit in range(8):
            xw = jnp.where(((n_i >> bit) & 1) == 1, pltpu.roll(xw, 1 << bit, 1), xw)
        bias_ref[e] = jnp.where(band, xw[:, BIAS_LEN - ATT_WIN:], NEG_INF)


def _att_specs(batch, seq):
    ni = seq // ATT_Q
    q_spec = pl.BlockSpec((ATT_Q, LANES), lambda hp, b, i: (b * ni + i, C_AQ // LANES + hp))
    kv_spec = pl.BlockSpec((seq + ATT_PAD, LANES), lambda hp, b, i: (b, hp))
    w_spec = pl.BlockSpec((None, 2, BIAS_LEN), lambda hp, b, i: (hp, 0, 0))
    return ni, q_spec, kv_spec, w_spec


def _att_scores(q2, k2, bias, sel, start_ok):
    qm = jnp.where(sel, q2, jnp.zeros_like(q2))
    s = _dot_nt(qm, k2) * (1.0 / 8.0) + bias
    s = jnp.where(start_ok, s, NEG_INF)
    p = jnp.exp(s - jnp.max(s, axis=-1, keepdims=True))
    return p / jnp.sum(p, axis=-1, keepdims=True)


def _att_fwd(proj, kpad, vpad, wvec, batch, seq):
    ni, q_spec, kv_spec, w_spec = _att_specs(batch, seq)

    def body(q_ref, k_ref, v_ref, w_ref, o_ref, bias_ref):
        b, i = pl.program_id(1), pl.program_id(2)

        @pl.when((b == 0) & (i == 0))
        def _():
            _att_bias(w_ref, bias_ref)

        win = pl.ds(pl.multiple_of(i * ATT_Q, ATT_Q), ATT_WIN)
        k2, v2, q2 = k_ref[win, :], v_ref[win, :], q_ref[...]
        start_ok = lax.broadcasted_iota(jnp.int32, (ATT_Q, ATT_WIN), 1) + (i * ATT_Q - ATT_PAD) >= 0
        lo = lax.broadcasted_iota(jnp.int32, (1, LANES), 1) < 64
        out = jnp.zeros((ATT_Q, LANES), F32)
        for e in range(2):
            sel = lo if e == 0 else jnp.logical_not(lo)
            p = _att_scores(q2, k2, bias_ref[e], sel, start_ok)
            out = out + _dot(p.astype(BF16), jnp.where(sel, v2, jnp.zeros_like(v2)))
        o_ref[...] = out.astype(BF16)

    return pl.pallas_call(
        body, name="att_fwd", grid=(ATT_HEADS // 2, batch, ni),
        in_specs=[q_spec, kv_spec, kv_spec, w_spec],
        out_specs=pl.BlockSpec((ATT_Q, LANES), lambda hp, b, i: (b * ni + i, hp)),
        out_shape=jax.ShapeDtypeStruct((batch * seq, ATT_HEADS * 64), BF16),
        scratch_shapes=[pltpu.VMEM((2, ATT_Q, ATT_WIN), F32)],
        compiler_params=_params("arbitrary", "arbitrary", "arbitrary"),
    )(proj, kpad, vpad, wvec)


def _mix_fwd(gro, ao, proj, b_gate, w_ret, w_att_t):
    t = gro.shape[0]
    tm, tn = min(256, t), 512

    def body(gro_ref, ao_ref, glr_ref, gla_ref, br_ref, ba_ref, wr_ref, wa_ref, z_ref, yr_ref, ya_ref):
        yr = _dot(gro_ref[...], wr_ref[...])
        ya = _dot_nt(ao_ref[...], wa_ref[...])
        gr = _sigmoid(glr_ref[...].astype(F32) + br_ref[...])
        ga = _sigmoid(gla_ref[...].astype(F32) + ba_ref[...])
        z_ref[...] = (gr * yr + ga * ya).astype(BF16)
        yr_ref[...] = yr.astype(BF16)
        ya_ref[...] = ya.astype(BF16)

    nb = D_MODEL // tn
    out = pl.BlockSpec((tm, tn), lambda i, j: (i, j))
    return pl.pallas_call(
        body, name="mix_fwd", grid=(t // tm, nb),
        in_specs=[pl.BlockSpec((tm, D_MODEL), lambda i, j: (i, 0)),
                  pl.BlockSpec((tm, 512), lambda i, j: (i, 0)),
                  pl.BlockSpec((tm, tn), lambda i, j: (i, C_GL // tn + j)),
                  pl.BlockSpec((tm, tn), lambda i, j: (i, C_GL // tn + nb + j)),
                  pl.BlockSpec((1, tn), lambda i, j: (0, j)),
                  pl.BlockSpec((1, tn), lambda i, j: (0, nb + j)),
                  pl.BlockSpec((D_MODEL, tn), lambda i, j: (0, j)),
                  pl.BlockSpec((tn, 512), lambda i, j: (j, 0))],
        out_specs=[out, out, out],
        out_shape=[jax.ShapeDtypeStruct((t, D_MODEL), BF16)] * 3,
        compiler_params=_params("parallel", "parallel"),
    )(gro, ao, proj, proj, b_gate, b_gate, w_ret, w_att_t)


def _out_fwd(z, x2, w_out, g2):
    t = z.shape[0]
    tm = min(256, t)

    def body(z_ref, x_ref, w_ref, g_ref, h_ref, hn_ref):
        h = x_ref[...] + _dot(z_ref[...], w_ref[...])
        h_ref[...] = h
        r = lax.rsqrt(jnp.mean(h * h, axis=-1, keepdims=True) + EPS)
        hn_ref[...] = (h * r * g_ref[...]).astype(BF16)

    row = pl.BlockSpec((tm, D_MODEL), lambda i: (i, 0))
    return pl.pallas_call(
        body, name="out_fwd", grid=(t // tm,),
        in_specs=[row, row, pl.BlockSpec((D_MODEL, D_MODEL), lambda i: (0, 0)),
                  pl.BlockSpec((1, D_MODEL), lambda i: (0, 0))],
        out_specs=[row, row],
        out_shape=[jax.ShapeDtypeStruct((t, D_MODEL), F32), jax.ShapeDtypeStruct((t, D_MODEL), BF16)],
        compiler_params=_params("parallel"),
    )(z, x2, w_out, g2)


def _ffn_up(hn, wg_t, wu_t):
    t = hn.shape[0]
    tm, tn = min(512, t), D_FF // 2

    def body(h_ref, wg_ref, wu_ref, g_ref, u_ref, a_ref):
        g = _dot_nt(h_ref[...], wg_ref[...])
        u = _dot_nt(h_ref[...], wu_ref[...])
        g_ref[...] = g.astype(BF16)
        u_ref[...] = u.astype(BF16)
        a_ref[...] = (g * _sigmoid(g) * u).astype(BF16)

    w_spec = pl.BlockSpec((tn, D_MODEL), lambda j, i: (j, 0))
    out = pl.BlockSpec((tm, tn), lambda j, i: (i, j))
    return pl.pallas_call(
        body, name="ffn_up", grid=(D_FF // tn, t // tm),
        in_specs=[pl.BlockSpec((tm, D_MODEL), lambda j, i: (i, 0)), w_spec, w_spec],
        out_specs=[out, out, out],
        out_shape=[jax.ShapeDtypeStruct((t, D_FF), BF16)] * 3,
        compiler_params=_params("parallel", "parallel"),
    )(hn, wg_t, wu_t)


def _ffn_down_loss(a, h1, tgt, w_down, g3):
    t = a.shape[0]
    tm = min(512, t)

    def body(a_ref, h_ref, t_ref, w_ref, g_ref, dh_ref, dhb_ref, loss_ref, dg_ref):
        @pl.when(pl.program_id(0) == 0)
        def _():
            loss_ref[...] = jnp.zeros_like(loss_ref)
            dg_ref[...] = jnp.zeros_like(dg_ref)

        g = g_ref[...]
        h2 = h_ref[...] + _dot(a_ref[...], w_ref[...])
        r = lax.rsqrt(jnp.mean(h2 * h2, axis=-1, keepdims=True) + EPS)
        err = h2 * r * g - t_ref[...]
        loss_ref[...] += jnp.sum(err * err) * (0.5 / D_MODEL)
        dy = err * (1.0 / D_MODEL)
        dh, dg_rows = _rms_bwd(h2, g, dy)
        dg_ref[...] += jnp.sum(dg_rows, axis=0, keepdims=True)
        dh_ref[...] = dh
        dhb_ref[...] = dh.astype(BF16)

    row = pl.BlockSpec((tm, D_MODEL), lambda i: (i, 0))
    vec = pl.BlockSpec((1, D_MODEL), lambda i: (0, 0))
    return pl.pallas_call(
        body, name="ffn_down_loss", grid=(t // tm,),
        in_specs=[pl.BlockSpec((tm, D_FF), lambda i: (i, 0)), row, row,
                  pl.BlockSpec((D_FF, D_MODEL), lambda i: (0, 0)), vec],
        out_specs=[row, row, pl.BlockSpec((1, LANES), lambda i: (0, 0)), vec],
        out_shape=[jax.ShapeDtypeStruct((t, D_MODEL), F32), jax.ShapeDtypeStruct((t, D_MODEL), BF16),
                   jax.ShapeDtypeStruct((1, LANES), F32), jax.ShapeDtypeStruct((1, D_MODEL), F32)],
        compiler_params=_params("arbitrary"),
    )(a, h1, tgt, w_down, g3)


def _ffn_bwd_act(dh2b, w_down, g_act, u_act, exchange):
    t = dh2b.shape[0]
    tm, tn = min(512, t), D_FF // 2

    def body(d_ref, w_ref, g_ref, u_ref, dg_ref, du_ref):
        da = _dot_nt(d_ref[...], w_ref[...])
        g = g_ref[...].astype(F32)
        u = u_ref[...].astype(F32)
        sg = _sigmoid(g)
        dg_ref[...] = (da * u * sg * (1.0 + g * (1.0 - sg))).astype(BF16)
        du_ref[...] = (da * g * sg).astype(BF16)

    blk = pl.BlockSpec((tm, tn), lambda j, i: (i, j))
    return _call(
        body, name="ffn_bwd_act", grid=(D_FF // tn, t // tm),
        in_specs=[pl.BlockSpec((tm, D_MODEL), lambda j, i: (i, 0)),
                  pl.BlockSpec((tn, D_MODEL), lambda j, i: (j, 0)), blk, blk],
        out_specs=[blk, blk],
        out_shape=[jax.ShapeDtypeStruct((t, D_FF), BF16)] * 2,
        semantics=("parallel", "parallel"), args=(dh2b, w_down, g_act, u_act), exchange=exchange)


def _ffn_bwd_in(dg, du, wg_t, wu_t, h1, dh2, g2, exchange):
    t = dg.shape[0]
    tm, tk = min(512, t), D_FF // 2
    nk = D_FF // tk

    def body(dg_ref, du_ref, wg_ref, wu_ref, h_ref, d2_ref, g_ref, dh_ref, dhb_ref, gn_ref, acc_ref):
        i, kk = pl.program_id(0), pl.program_id(1)

        @pl.when((i == 0) & (kk == 0))
        def _():
            gn_ref[...] = jnp.zeros_like(gn_ref)

        @pl.when(kk == 0)
        def _():
            acc_ref[...] = jnp.zeros_like(acc_ref)

        acc_ref[...] += _dot(dg_ref[...], wg_ref[...]) + _dot(du_ref[...], wu_ref[...])

        @pl.when(kk == nk - 1)
        def _():
            dx, dg_rows = _rms_bwd(h_ref[...], g_ref[...], acc_ref[...])
            dh = d2_ref[...] + dx
            dh_ref[...] = dh
            dhb_ref[...] = dh.astype(BF16)
            gn_ref[...] += jnp.sum(dg_rows, axis=0, keepdims=True)

    act = pl.BlockSpec((tm, tk), lambda i, kk: (i, kk))
    wsp = pl.BlockSpec((tk, D_MODEL), lambda i, kk: (kk, 0))
    row = pl.BlockSpec((tm, D_MODEL), lambda i, kk: (i, 0))
    vec = pl.BlockSpec((1, D_MODEL), lambda i, kk: (0, 0))
    return _call(
        body, name="ffn_bwd_in", grid=(t // tm, nk),
        in_specs=[act, act, wsp, wsp, row, row, vec],
        out_specs=[row, row, vec],
        out_shape=[jax.ShapeDtypeStruct((t, D_MODEL), F32), jax.ShapeDtypeStruct((t, D_MODEL), BF16),
                   jax.ShapeDtypeStruct((1, D_MODEL), F32)],
        scratch=[pltpu.VMEM((tm, D_MODEL), F32)],
        semantics=("arbitrary", "arbitrary"), args=(dg, du, wg_t, wu_t, h1, dh2, g2), exchange=exchange)


def _mix_bwd(dh1b, w_out, proj, b_gate, y_ret, y_att, exchange):
    t = dh1b.shape[0]
    tm, tn = min(256, t), 512
    nb = D_MODEL // tn

    def body(d_ref, w_ref, glr_ref, gla_ref, br_ref, ba_ref, yr_ref, ya_ref,
             dyr_ref, dya_ref, dglr_ref, dgla_ref, dbr_ref, dba_ref):
        @pl.when(pl.program_id(1) == 0)
        def _():
            dbr_ref[...] = jnp.zeros_like(dbr_ref)
            dba_ref[...] = jnp.zeros_like(dba_ref)

        dz = _dot_nt(d_ref[...], w_ref[...])
        gr = _sigmoid(glr_ref[...].astype(F32) + br_ref[...])
        ga = _sigmoid(gla_ref[...].astype(F32) + ba_ref[...])
        dyr_ref[...] = (dz * gr).astype(BF16)
        dya_ref[...] = (dz * ga).astype(BF16)
        dglr = dz * yr_ref[...].astype(F32) * gr * (1.0 - gr)
        dgla = dz * ya_ref[...].astype(F32) * ga * (1.0 - ga)
        dglr_ref[...] = dglr.astype(BF16)
        dgla_ref[...] = dgla.astype(BF16)
        dbr_ref[...] += jnp.sum(dglr, axis=0, keepdims=True)
        dba_ref[...] += jnp.sum(dgla, axis=0, keepdims=True)

    blk = pl.BlockSpec((tm, tn), lambda j, i: (i, j))
    vec = pl.BlockSpec((1, tn), lambda j, i: (0, j))
    return _call(
        body, name="mix_bwd", grid=(nb, t // tm),
        in_specs=[pl.BlockSpec((tm, D_MODEL), lambda j, i: (i, 0)),
                  pl.BlockSpec((tn, D_MODEL), lambda j, i: (j, 0)),
                  pl.BlockSpec((tm, tn), lambda j, i: (i, C_GL // tn + j)),
                  pl.BlockSpec((tm, tn), lambda j, i: (i, C_GL // tn + nb + j)),
                  vec, pl.BlockSpec((1, tn), lambda j, i: (0, nb + j)), blk, blk],
        out_specs=[blk, blk, blk, blk, vec, vec],
        out_shape=[jax.ShapeDtypeStruct((t, D_MODEL), BF16)] * 4 + [jax.ShapeDtypeStruct((1, D_MODEL), F32)] * 2,
        semantics=("arbitrary", "arbitrary"), args=(dh1b, w_out, proj, proj, b_gate, b_gate, y_ret, y_att),
        exchange=exchange)


def _ret_bwd(dgro, proj, o_ret, qr, kr, cs, sn, lg_arr, batch, seq, exchange):
    t = batch * seq
    nt = seq // RET_TILE

    def body(dgro_ref, rg_ref, o_ref, qr_ref, kr_ref, v_ref, cs_ref, sn_ref, lg_ref,
             dq_ref, dk_ref, dv_ref, drg_ref, do_ref, st_ref):
        lg = lg_ref[:, 0:1]
        inside, q_dec, k_dec, tile_dec = _decay(lg)

        state = jnp.zeros((RET_KEY_DIM, RET_VAL_DIM), F32)
        for i in range(nt - 1):
            rows = slice(i * RET_TILE, (i + 1) * RET_TILE)
            state = state * tile_dec + _dot_tn(_scaled(kr_ref[rows, :], k_dec), v_ref[rows, :])
            st_ref[i + 1] = state.astype(BF16)

        for i in range(nt):
            rows = slice(i * RET_TILE, (i + 1) * RET_TILE)
            o = o_ref[rows, :]
            xc = o - jnp.mean(o, axis=-1, keepdims=True)
            rs = lax.rsqrt(jnp.mean(xc * xc, axis=-1, keepdims=True) + EPS)
            nrm = xc * rs
            rg = rg_ref[rows, :].astype(F32)
            sg = _sigmoid(rg)
            dg = dgro_ref[rows, :].astype(F32)
            drg_ref[rows, :] = (dg * nrm * sg * (1.0 + rg * (1.0 - sg))).astype(BF16)
            dn = dg * rg * sg
            do = rs * (dn - jnp.mean(dn, axis=-1, keepdims=True)
                       - nrm * jnp.mean(dn * nrm, axis=-1, keepdims=True))
            do_ref[rows, :] = do.astype(BF16)

        dstate = jnp.zeros((RET_KEY_DIM, RET_VAL_DIM), F32)
        for i in reversed(range(nt)):
            rows = slice(i * RET_TILE, (i + 1) * RET_TILE)
            qi, ki, vi, doi = qr_ref[rows, :], kr_ref[rows, :], v_ref[rows, :], do_ref[rows, :]
            p = (_dot_nt(qi, ki) * inside).astype(BF16)
            dp = (_dot_nt(doi, vi) * inside).astype(BF16)
            dq = _dot(dp, ki)
            dk = _dot_tn(dp, qi)
            dv = _dot_tn(p, doi)
            if i > 0:
                dq = dq + _dot_nt(doi, st_ref[i]) * q_dec
            if i < nt - 1:
                dsb = dstate.astype(BF16)
                dk = dk + _dot_nt(vi, dsb) * k_dec
                dv = dv + _dot(_scaled(ki, k_dec), dsb)
            if i > 0:
                dstate = dstate * tile_dec + _dot_tn(_scaled(qi, q_dec), doi)
            dq_ref[rows, :] = (dq * cs_ref[rows, :] - pltpu.roll(dq, 64, 1) * sn_ref[rows, :]).astype(BF16)
            dk = (dk * cs_ref[rows, :] - pltpu.roll(dk, 64, 1) * sn_ref[rows, :]) * (RET_KEY_DIM ** -0.5)
            dk_ref[rows, :] = dk.astype(BF16)
            dv_ref[rows, :] = dv.astype(BF16)

    key = pl.BlockSpec((seq, RET_KEY_DIM), lambda b, h: (b, h))
    val = pl.BlockSpec((seq, RET_VAL_DIM), lambda b, h: (b, h))
    tab = pl.BlockSpec((seq, RET_KEY_DIM), lambda b, h: (0, 0))
    return _call(
        body, name="ret_bwd", grid=(batch, RET_HEADS),
        in_specs=[val, pl.BlockSpec((seq, RET_VAL_DIM), lambda b, h: (b, C_RG // RET_VAL_DIM + h)), val, key, key,
                  pl.BlockSpec((seq, RET_VAL_DIM), lambda b, h: (b, C_RV // RET_VAL_DIM + h)), tab, tab,
                  pl.BlockSpec((None, 1, LANES), lambda b, h: (h, 0, 0))],
        out_specs=[key, key, val, val],
        out_shape=[jax.ShapeDtypeStruct((t, RET_HEADS * RET_KEY_DIM), BF16)] * 2
                  + [jax.ShapeDtypeStruct((t, RET_HEADS * RET_VAL_DIM), BF16)] * 2,
        scratch=[pltpu.VMEM((seq, RET_VAL_DIM), BF16), pltpu.VMEM((nt, RET_KEY_DIM, RET_VAL_DIM), BF16)],
        semantics=("parallel", "parallel"), args=(dgro, proj, o_ret, qr, kr, proj, cs, sn, lg_arr),
        exchange=exchange)


def _att_bwd(proj, kpad, vpad, wvec, dao, batch, seq):
    ni, q_spec, kv_spec, w_spec = _att_specs(batch, seq)
    t = batch * seq

    def body(q_ref, k_ref, v_ref, w_ref, do_ref, dq_ref, dk_ref, dv_ref, dw_ref,
             bias_ref, dbias_ref, dk_acc, dv_acc):
        b, i = pl.program_id(1), pl.program_id(2)

        @pl.when((b == 0) & (i == 0))
        def _():
            _att_bias(w_ref, bias_ref)
            dbias_ref[...] = jnp.zeros_like(dbias_ref)

        @pl.when(i == 0)
        def _():
            dk_acc[...] = jnp.zeros_like(dk_acc)
            dv_acc[...] = jnp.zeros_like(dv_acc)

        win = pl.ds(pl.multiple_of(i * ATT_Q, ATT_Q), ATT_WIN)
        k2, v2, q2, do2 = k_ref[win, :], v_ref[win, :], q_ref[...], do_ref[...]
        start_ok = lax.broadcasted_iota(jnp.int32, (ATT_Q, ATT_WIN), 1) + (i * ATT_Q - ATT_PAD) >= 0
        lo = lax.broadcasted_iota(jnp.int32, (1, LANES), 1) < 64
        dq = jnp.zeros((ATT_Q, LANES), F32)
        dk = jnp.zeros((ATT_WIN, LANES), F32)
        dv = jnp.zeros((ATT_WIN, LANES), F32)
        for e in range(2):
            sel = lo if e == 0 else jnp.logical_not(lo)
            p = _att_scores(q2, k2, bias_ref[e], sel, start_ok)
            dom = jnp.where(sel, do2, jnp.zeros_like(do2))
            dp = _dot_nt(dom, v2)
            ds = p * (dp - jnp.sum(dp * p, axis=-1, keepdims=True))
            dbias_ref[e] += ds
            dsb = (ds * (1.0 / 8.0)).astype(BF16)
            dq = dq + _dot(dsb, jnp.where(sel, k2, jnp.zeros_like(k2)))
            dk = dk + _dot_tn(dsb, jnp.where(sel, q2, jnp.zeros_like(q2)))
            dv = dv + _dot_tn(p.astype(BF16), dom)
        dq_ref[...] = dq.astype(BF16)
        dk_acc[win, :] += dk
        dv_acc[win, :] += dv

        @pl.when(i == ni - 1)
        def _():
            dk_ref[...] = dk_acc[ATT_PAD:, :].astype(BF16)
            dv_ref[...] = dv_acc[ATT_PAD:, :].astype(BF16)

        @pl.when((b == batch - 1) & (i == ni - 1))
        def _():
            n_i = lax.broadcasted_iota(jnp.int32, (ATT_Q, BIAS_LEN), 0)
            for e in range(2):
                xw = jnp.concatenate([jnp.zeros((ATT_Q, BIAS_LEN - ATT_WIN), F32), dbias_ref[e]], axis=1)
                for bit in range(8):
                    xw = jnp.where(((n_i >> bit) & 1) == 1, pltpu.roll(xw, BIAS_LEN - (1 << bit), 1), xw)
                dw_ref[e:e + 1, :] = jnp.sum(xw, axis=0, keepdims=True)

    seq_blk = pl.BlockSpec((seq, LANES), lambda hp, b, i: (b, hp))
    q_out = pl.BlockSpec((ATT_Q, LANES), lambda hp, b, i: (b * ni + i, hp))
    return pl.pallas_call(
        body, name="att_bwd", grid=(ATT_HEADS // 2, batch, ni),
        in_specs=[q_spec, kv_spec, kv_spec, w_spec, q_out],
        out_specs=[q_out, seq_blk, seq_blk, w_spec],
        out_shape=[jax.ShapeDtypeStruct((t, 512), BF16)] * 3
                  + [jax.ShapeDtypeStruct((ATT_HEADS // 2, 2, BIAS_LEN), F32)],
        scratch_shapes=[pltpu.VMEM((2, ATT_Q, ATT_WIN), F32), pltpu.VMEM((2, ATT_Q, ATT_WIN), F32),
                        pltpu.VMEM((seq + ATT_PAD, LANES), F32), pltpu.VMEM((seq + ATT_PAD, LANES), F32)],
        compiler_params=_params("arbitrary", "arbitrary", "arbitrary"),
    )(proj, kpad, vpad, wvec, dao)


def _rms_in_bwd(x2, dxn, dh1, g1):
    t = x2.shape[0]
    tm = min(512, t)

    def body(x_ref, d_ref, h_ref, g_ref, dx_ref, dg_ref):
        @pl.when(pl.program_id(0) == 0)
        def _():
            dg_ref[...] = jnp.zeros_like(dg_ref)

        dx, dg_rows = _rms_bwd(x_ref[...], g_ref[...], d_ref[...])
        dx_ref[...] = h_ref[...] + dx
        dg_ref[...] += jnp.sum(dg_rows, axis=0, keepdims=True)

    row = pl.BlockSpec((tm, D_MODEL), lambda i: (i, 0))
    vec = pl.BlockSpec((1, D_MODEL), lambda i: (0, 0))
    return pl.pallas_call(
        body, name="rms_in_bwd", grid=(t // tm,),
        in_specs=[row, row, row, vec], out_specs=[row, vec],
        out_shape=[jax.ShapeDtypeStruct((t, D_MODEL), F32), jax.ShapeDtypeStruct((1, D_MODEL), F32)],
        compiler_params=_params("arbitrary"),
    )(x2, dxn, dh1, g1)


def _pack_small(dg1, dbr, dba, dg2, dg3, dw):
    def body(a_ref, b_ref, c_ref, d_ref, e_ref, w_ref, o_ref):
        o_ref[...] = jnp.zeros_like(o_ref)
        for r, ref in enumerate((a_ref, b_ref, c_ref, d_ref, e_ref)):
            o_ref[r:r + 1, :] = ref[...]
        for hp in range(ATT_HEADS // 2):
            o_ref[8 + 2 * hp:10 + 2 * hp, :] = w_ref[hp]

    return pl.pallas_call(body, name="pack_small",
                          out_shape=jax.ShapeDtypeStruct((16, D_MODEL), F32))(dg1, dbr, dba, dg2, dg3, dw)


def _rotary_tables(seq):
    freqs = ROPE_BASE ** (-jnp.arange(0, RET_KEY_DIM, 2, dtype=F32) / RET_KEY_DIM)
    ang = jnp.arange(seq, dtype=F32)[:, None] * freqs[None, :]
    cos, sin = jnp.cos(ang), jnp.sin(ang)
    return jnp.concatenate([cos, cos], axis=1), jnp.concatenate([-sin, sin], axis=1)


def _bias_rows(rel_bias):
    n_far = BIAS_LEN - ATT_Q - MAX_REL + 1
    n_near = BIAS_LEN - n_far - (N_REL - 2)
    w = jnp.concatenate([jnp.broadcast_to(rel_bias[:, N_REL - 1:], (ATT_HEADS, n_far)),
                         rel_bias[:, 1:N_REL - 1][:, ::-1],
                         jnp.broadcast_to(rel_bias[:, :1], (ATT_HEADS, n_near))], axis=1)
    return w.reshape(ATT_HEADS // 2, 2, BIAS_LEN)


def _bias_rows_bwd(dw):
    n_far = BIAS_LEN - ATT_Q - MAX_REL + 1
    mid = dw[:, n_far:n_far + N_REL - 2][:, ::-1]
    return jnp.concatenate([jnp.sum(dw[:, n_far + N_REL - 2:], axis=1, keepdims=True), mid,
                            jnp.sum(dw[:, :n_far], axis=1, keepdims=True)], axis=1)


def _pad_keys(a, batch, seq):
    a = a.reshape(batch, seq, a.shape[-1])
    return jnp.pad(a, ((0, 0), (ATT_PAD, 0), (0, 0))).reshape(batch * (seq + ATT_PAD), a.shape[-1])


def _step(x, tgt, norm_mix, b_gate, norm_ffn, norm_final, rel_bias_shard, shard):
    batch, seq, _ = x.shape
    t = batch * seq
    n_rb = rel_bias_shard.shape[-1]
    x2, tgt2 = x.reshape(t, D_MODEL), tgt.reshape(t, D_MODEL)
    g3 = norm_final.reshape(1, D_MODEL)
    cs, sn = _rotary_tables(seq)
    lg = np.log(1.0 - 2.0 ** (-5.0 - np.arange(RET_HEADS, dtype=np.float32))).astype(np.float32)
    lg_arr = jnp.asarray(np.broadcast_to(lg[:, None, None], (RET_HEADS, 1, LANES)))

    def gather(*names):
        return _Exchange([shard[nm] for nm in names], scatter=False)

    def scatter(*grads):
        return _Exchange(grads, scatter=True)

    rb_pad = jnp.pad(rel_bias_shard, ((0, 0), (0, LANES - n_rb)))
    w_in_t, rb_full = _exchange([shard["w_in_t"], rb_pad], False, "gather_w_in")
    rb_full = rb_full.reshape(N_DEV, ATT_HEADS, LANES)[:, :, :n_rb]
    wvec = _bias_rows(jnp.transpose(rb_full, (1, 0, 2)).reshape(ATT_HEADS, N_DEV * n_rb))

    xn = _rms_fwd(x2, norm_mix)
    proj, (w_ret, w_att_t, w_out, w_gate_t) = _mm(
        xn, w_in_t, tb=True, out_dtype=BF16, tm=1024, tn=1664, tk=1024, name="proj",
        exchange=gather("w_ret", "w_att_t", "w_out", "w_gate_t"))
    (gro, o_ret, qr, kr), (w_up_t, w_down) = _ret_fwd(proj, cs, sn, lg_arr, batch, seq, gather("w_up_t", "w_down"))
    kpad = _pad_keys(proj[:, C_AK:C_AV], batch, seq)
    vpad = _pad_keys(proj[:, C_AV:C_GL], batch, seq)
    ao = _att_fwd(proj, kpad, vpad, wvec, batch, seq)
    z, y_ret, y_att = _mix_fwd(gro, ao, proj, b_gate, w_ret, w_att_t)
    h1, hn = _out_fwd(z, x2, w_out, norm_ffn)
    g_act, u_act, a_act = _ffn_up(hn, w_gate_t, w_up_t)
    dh2, dh2b, loss, dg3 = _ffn_down_loss(a_act, h1, tgt2, w_down, g3)

    wg = dict(out_dtype=BF16, tn=1024, ta=True)
    slots = {}
    dw_down = _mm(a_act, dh2b, tm=1408, tk=1024, name="dw_down", **wg)
    (d_gact, d_uact), (slots["w_down"],) = _ffn_bwd_act(dh2b, w_down, g_act, u_act, scatter(dw_down))
    dw_gate = _mm(d_gact, hn, tm=1408, tk=1024, name="dw_gate", **wg)
    dw_up, (slots["w_gate_t"],) = _mm(d_uact, hn, tm=1408, tk=1024, name="dw_up", exchange=scatter(dw_gate), **wg)
    (dh1, dh1b, dg2), (slots["w_up_t"],) = _ffn_bwd_in(d_gact, d_uact, w_gate_t, w_up_t, h1, dh2, norm_ffn,
                                                     scatter(dw_up))
    dw_out = _mm(z, dh1b, tm=1024, tk=2048, name="dw_out", **wg)
    (dyr, dya, dglr, dgla, dbr, dba), (slots["w_out"],) = _mix_bwd(dh1b, w_out, proj, b_gate, y_ret, y_att,
                                                                scatter(dw_out))
    dgro = _mm(dyr, w_ret, tb=True, out_dtype=BF16, tm=1024, tn=1024, tk=1024, name="dgro")
    dao = _mm(dya, w_att_t, out_dtype=BF16, tm=1024, tn=512, tk=1024, name="dao")
    dw_ret = _mm(gro, dyr, tm=1024, tk=2048, name="dw_ret", **wg)
    dw_att = _mm(dya, ao, tm=1024, tk=2048, name="dw_att", **wg)
    (drq, drk, drv, drg), (slots["w_ret"], slots["w_att_t"]) = _ret_bwd(
        dgro, proj, o_ret, qr, kr, cs, sn, lg_arr, batch, seq, scatter(dw_ret, dw_att))
    daq, dak, dav, dw = _att_bwd(proj, kpad, vpad, wvec, dao, batch, seq)
    dproj = jnp.concatenate([drq, drk, drv, drg, daq, dak, dav, dglr, dgla], axis=1)
    dw_in = _mm(dproj, xn, tm=512, tk=2048, name="dw_in", **wg)
    dxn, (slots["w_in_t"],) = _mm(dproj, w_in_t, out_dtype=F32, tm=1024, tn=1024, tk=1664, name="dxn",
                                  exchange=scatter(dw_in))
    dx, dg1 = _rms_in_bwd(x2, dxn, dh1, norm_mix)
    small = _pack_small(dg1, dbr, dba, dg2, dg3, dw)
    (small_slots,) = _exchange([small], False, "gather_small")
    return loss[0, 0], dx.reshape(batch, seq, D_MODEL), slots, small_slots.reshape(N_DEV, 16, D_MODEL)


def _sum_slots(slots, name):
    _, r, c = slots.shape
    tr = max(d for d in range(16, r + 1, 16) if r % d == 0 and (d * c <= 256 * 1024 or d == 16))

    def body(s_ref, o_ref):
        acc = s_ref[0].astype(F32)
        for s in range(1, N_DEV):
            acc = acc + s_ref[s].astype(F32)
        o_ref[...] = acc

    return pl.pallas_call(
        body, name=name, grid=(r // tr,),
        in_specs=[pl.BlockSpec((N_DEV, tr, c), lambda i: (0, i, 0))],
        out_specs=pl.BlockSpec((tr, c), lambda i: (i, 0)),
        out_shape=jax.ShapeDtypeStruct((r, c), F32),
        compiler_params=_params("parallel"),
    )(slots)


def _adamw_math(w, g, m, v):
    m = ADAM_B1 * m + (1.0 - ADAM_B1) * g
    v = ADAM_B2 * v + (1.0 - ADAM_B2) * (g * g)
    m_hat = m / (1.0 - ADAM_B1 ** ADAM_STEP)
    v_hat = v / (1.0 - ADAM_B2 ** ADAM_STEP)
    return -ADAM_LR * (m_hat / (jnp.sqrt(v_hat) + ADAM_EPS) + ADAM_WD * w), m, v


def _adamw(w, g, m, v, name):
    r, c = w.shape
    tr = r
    while tr * c > 128 * 1024 and tr % 16 == 0:
        tr //= 2

    def body(w_ref, g_ref, m_ref, v_ref, d_ref, nm_ref, nv_ref):
        d_ref[...], nm_ref[...], nv_ref[...] = _adamw_math(w_ref[...], g_ref[...], m_ref[...], v_ref[...])

    blk = pl.BlockSpec((tr, c), lambda i: (i, 0))
    return pl.pallas_call(
        body, name=name, grid=(r // tr,),
        in_specs=[blk] * 4, out_specs=[blk] * 3,
        out_shape=[jax.ShapeDtypeStruct((r, c), F32)] * 3,
        compiler_params=_params("parallel"),
    )(w, g, m, v)


def _adamw_small(ws, gs, ms, vs):
    n = len(ws)

    def body(*refs):
        for i in range(n):
            w_ref, g_ref, m_ref, v_ref = (refs[j * n + i] for j in range(4))
            d_ref, nm_ref, nv_ref = (refs[(4 + j) * n + i] for j in range(3))
            d_ref[...], nm_ref[...], nv_ref[...] = _adamw_math(w_ref[...], g_ref[...], m_ref[...], v_ref[...])

    shapes = [jax.ShapeDtypeStruct(w.shape, F32) for w in ws]
    outs = pl.pallas_call(body, name="adamw_small", out_shape=shapes * 3)(*ws, *gs, *ms, *vs)
    return outs[:n], outs[n:2 * n], outs[2 * n:]


def kernel(x, norm_mix, w_in, b_gate, rel_bias, w_ret_out, w_att_out, w_out, norm_ffn, w_ffn_gate, w_ffn_up, w_ffn_down, norm_final, loss_target, m_norm_mix, m_w_in, m_b_gate, m_rel_bias, m_w_ret_out, m_w_att_out, m_w_out, m_norm_ffn, m_w_ffn_gate, m_w_ffn_up, m_w_ffn_down, m_norm_final, v_norm_mix, v_w_in, v_b_gate, v_rel_bias, v_w_ret_out, v_w_att_out, v_w_out, v_norm_ffn, v_w_ffn_gate, v_w_ffn_up, v_w_ffn_down, v_norm_final):
    me = _index(_place())
    n_rb = rel_bias.shape[-1]

    shard = dict(w_in_t=w_in[0].T, w_gate_t=w_ffn_gate[0].T, w_up_t=w_ffn_up[0].T, w_down=w_ffn_down[0],
                 w_ret=w_ret_out[0], w_out=w_out[0], w_att_t=w_att_out[0].T)
    shard = {nm: s.astype(BF16) for nm, s in shard.items()}
    loss, dx, slots, small_slots = _step(x, loss_target, norm_mix, b_gate, norm_ffn, norm_final, rel_bias[0], shard)
    loss = lax.psum(loss, ("x", "y", "c"))
    summed = {nm: _sum_slots(s, "sum_" + nm) for nm, s in slots.items()}
    small_sum = _sum_slots(small_slots, "sum_small")

    g = dict(
        w_in=summed["w_in_t"].T, w_ffn_gate=summed["w_gate_t"].T, w_ffn_up=summed["w_up_t"].T,
        w_ffn_down=summed["w_down"], w_ret_out=summed["w_ret"], w_out=summed["w_out"], w_att_out=summed["w_att_t"].T,
        norm_mix=small_sum[0:1], b_gate=jnp.concatenate([small_sum[1:2], small_sum[2:3]], axis=1),
        norm_ffn=small_sum[3:4], norm_final=small_sum[4:5],
        rel_bias=lax.dynamic_slice_in_dim(_bias_rows_bwd(small_sum[8:16]), me * n_rb, n_rb, axis=1),
    )
    w = dict(norm_mix=norm_mix, w_in=w_in, b_gate=b_gate, rel_bias=rel_bias, w_ret_out=w_ret_out, w_att_out=w_att_out,
             w_out=w_out, norm_ffn=norm_ffn, w_ffn_gate=w_ffn_gate, w_ffn_up=w_ffn_up, w_ffn_down=w_ffn_down,
             norm_final=norm_final)
    m = dict(norm_mix=m_norm_mix, w_in=m_w_in, b_gate=m_b_gate, rel_bias=m_rel_bias, w_ret_out=m_w_ret_out,
             w_att_out=m_w_att_out, w_out=m_w_out, norm_ffn=m_norm_ffn, w_ffn_gate=m_w_ffn_gate, w_ffn_up=m_w_ffn_up,
             w_ffn_down=m_w_ffn_down, norm_final=m_norm_final)
    v = dict(norm_mix=v_norm_mix, w_in=v_w_in, b_gate=v_b_gate, rel_bias=v_rel_bias, w_ret_out=v_w_ret_out,
             w_att_out=v_w_att_out, w_out=v_w_out, norm_ffn=v_norm_ffn, w_ffn_gate=v_w_ffn_gate, w_ffn_up=v_w_ffn_up,
             w_ffn_down=v_w_ffn_down, norm_final=v_norm_final)
    order = ("norm_mix", "w_in", "b_gate", "rel_bias", "w_ret_out", "w_att_out", "w_out", "norm_ffn",
             "w_ffn_gate", "w_ffn_up", "w_ffn_down", "norm_final")
    small_names = ("norm_mix", "b_gate", "rel_bias", "norm_ffn", "norm_final")

    def flat(a):
        return a[0] if a.ndim == 3 else a.reshape(-1, a.shape[-1])

    grad, delta, new_m, new_v = {}, {}, {}, {}
    for nm in order:
        if nm not in small_names:
            d, nmom, nvar = _adamw(flat(w[nm]), g[nm], flat(m[nm]), flat(v[nm]), "adamw_" + nm)
            grad[nm], delta[nm], new_m[nm], new_v[nm] = (a.reshape(w[nm].shape) for a in (g[nm], d, nmom, nvar))
    ds, nms, nvs = _adamw_small([flat(w[nm]) for nm in small_names], [g[nm] for nm in small_names],
                                [flat(m[nm]) for nm in small_names], [flat(v[nm]) for nm in small_names])
    for i, nm in enumerate(small_names):
        grad[nm], delta[nm], new_m[nm], new_v[nm] = (a.reshape(w[nm].shape) for a in (g[nm], ds[i], nms[i], nvs[i]))

    return (loss, dx, *[grad[nm] for nm in order], *[delta[nm] for nm in order],
            *[new_m[nm] for nm in order], *[new_v[nm] for nm in order])
```

```python
import numpy as np
import jax
import jax.numpy as jnp
from jax import lax
from jax.experimental import pallas as pl
from jax.experimental.pallas import tpu as pltpu

F32 = jnp.float32
BF16 = jnp.bfloat16
MESH = pl.DeviceIdType.MESH

D_MODEL = 1024
CHUNK = 64
RET_HEADS = 4
RET_KEY_DIM = 128
RET_VAL_DIM = 256
ATT_HEADS = 8
BAND_CHUNKS = 8
MAX_REL = 256
N_REL = CHUNK + MAX_REL
D_FF = 2816
N_IN = 6656
ROPE_BASE = 10000.0
EPS = 1e-6
NEG_INF = -1e30
C_RQ, C_RK, C_RV, C_RG, C_AQ, C_AK, C_AV, C_GL = 0, 512, 1024, 2048, 3072, 3584, 4096, 4608

ADAM_LR = 0.001
ADAM_B1 = 0.9
ADAM_B2 = 0.999
ADAM_EPS = 1e-08
ADAM_WD = 0.01
ADAM_STEP = 10

N_DEV = 8
LANES = 128
RET_TILE = 256
ATT_Q = 256
ATT_PAD = BAND_CHUNKS * CHUNK
ATT_WIN = ATT_PAD + ATT_Q
BIAS_LEN = 1024
VMEM_LIMIT = 48 * 1024 * 1024


def _params(*sem):
    return pltpu.CompilerParams(dimension_semantics=sem, vmem_limit_bytes=VMEM_LIMIT)


def _dot(a, b):
    return lax.dot_general(a, b, (((1,), (0,)), ((), ())), preferred_element_type=F32)


def _dot_nt(a, b):
    return lax.dot_general(a, b, (((1,), (1,)), ((), ())), preferred_element_type=F32)


def _dot_tn(a, b):
    return lax.dot_general(a, b, (((0,), (0,)), ((), ())), preferred_element_type=F32)


def _sigmoid(x):
    return 1.0 / (1.0 + jnp.exp(-x))


def _rms_bwd(x, g, dy):
    r = lax.rsqrt(jnp.mean(x * x, axis=-1, keepdims=True) + EPS)
    u = dy * g
    dx = r * u - x * (r * r * r) * jnp.mean(u * x, axis=-1, keepdims=True)
    return dx, dy * x * r


def _place():
    return lax.axis_index("x"), lax.axis_index("y"), lax.axis_index("c")


def _peer(k):
    x, y, c = _place()
    return ((1 - x) if k & 4 else x, (1 - y) if k & 2 else y, (1 - c) if k & 1 else c)


def _index(place):
    return 4 * place[0] + 2 * place[1] + place[2]


def _rows(ref, block, nrows):
    align = 16 if ref.dtype == BF16 else 8
    return ref.at[pl.ds(pl.multiple_of(block * nrows, align), nrows)]


class _Exchange:
    def __init__(self, arrays, scatter):
        self.arrays, self.scatter, self.n = list(arrays), scatter, len(arrays)

    def out_shape(self):
        if self.scatter:
            return [jax.ShapeDtypeStruct((N_DEV, a.shape[0] // N_DEV) + a.shape[1:], a.dtype) for a in self.arrays]
        return [jax.ShapeDtypeStruct((N_DEV * a.shape[0],) + a.shape[1:], a.dtype) for a in self.arrays]

    def scratch(self):
        return [pltpu.SemaphoreType.DMA((self.n, N_DEV - 1)), pltpu.SemaphoreType.DMA((self.n, N_DEV - 1)),
                pltpu.SemaphoreType.DMA((self.n,))]

    def _copies(self, ins, outs, sems):
        send_sems, recv_sems, local_sems = sems
        me = _index(_place())

        def src(w, to):
            return _rows(ins[w], to, ins[w].shape[0] // N_DEV) if self.scatter else ins[w]

        def dst(w, origin):
            return outs[w].at[origin] if self.scatter else _rows(outs[w], origin, ins[w].shape[0])

        def remote(w, k, to, origin):
            return pltpu.make_async_remote_copy(src_ref=src(w, to), dst_ref=dst(w, origin),
                                                send_sem=send_sems.at[w, k - 1], recv_sem=recv_sems.at[w, k - 1],
                                                device_id=_peer(k), device_id_type=MESH)

        pairs = [(w, k) for w in range(self.n) for k in range(1, N_DEV)]
        own = lambda: [pltpu.make_async_copy(src(w, me), dst(w, me), local_sems.at[w]) for w in range(self.n)]
        sent = lambda: [remote(w, k, _index(_peer(k)), me) for w, k in pairs]
        arriving = lambda: [remote(w, k, me, _index(_peer(k))) for w, k in pairs]
        return own, sent, arriving

    def start(self, ins, outs, sems):
        own, sent, _ = self._copies(ins, outs, sems)
        for cp in own() + sent():
            cp.start()

    def wait(self, ins, outs, sems):
        own, sent, arriving = self._copies(ins, outs, sems)
        for cp in arriving():
            cp.wait_recv()
        for cp in sent():
            cp.wait_send()
        for cp in own():
            cp.wait()


class _ChipGather:
    def __init__(self, arrays):
        self.arrays, self.n = list(arrays), len(arrays)

    def out_shape(self):
        return [jax.ShapeDtypeStruct((N_DEV * a.shape[0],) + a.shape[1:], a.dtype) for a in self.arrays]

    def scratch(self):
        return [pltpu.SemaphoreType.DMA((self.n, N_DEV - 1)), pltpu.SemaphoreType.DMA((self.n, N_DEV - 1)),
                pltpu.SemaphoreType.DMA((self.n,))]

    def _parts(self, ins, outs, sems):
        send_sems, recv_sems, local_sems = sems
        x, y, c = _place()
        me, sibling = (x, y, c), (x, y, 1 - c)
        chips = [(1 - x, y), (x, 1 - y), (1 - x, 1 - y)]

        def rows(w, place):
            return _rows(outs[w], _index(place), ins[w].shape[0])

        def copy(w, k, block, to, own=False):
            return pltpu.make_async_remote_copy(src_ref=ins[w] if own else rows(w, block), dst_ref=rows(w, block),
                                                send_sem=send_sems.at[w, k], recv_sem=recv_sems.at[w, k],
                                                device_id=to, device_id_type=MESH)

        def local(w):
            return pltpu.make_async_copy(ins[w], rows(w, me), local_sems.at[w])

        return me, sibling, chips, c, copy, local

    def start(self, ins, outs, sems):
        me, sibling, chips, c, copy, local = self._parts(ins, outs, sems)
        for w in range(self.n):
            local(w).start()
            copy(w, 0, me, sibling, own=True).start()
            for j, chip in enumerate(chips):
                copy(w, 1 + j, me, (*chip, c), own=True).start()

    def wait(self, ins, outs, sems):
        me, sibling, chips, c, copy, local = self._parts(ins, outs, sems)
        for w in range(self.n):
            for j, chip in enumerate(chips):
                copy(w, 1 + j, (*chip, c), me).wait_recv()
                copy(w, 4 + j, (*chip, c), sibling).start()
        for w in range(self.n):
            copy(w, 0, sibling, me).wait_recv()
            for j, chip in enumerate(chips):
                copy(w, 4 + j, (*chip, 1 - c), me).wait_recv()
            copy(w, 0, me, sibling, own=True).wait_send()
            for j, chip in enumerate(chips):
                copy(w, 1 + j, me, (*chip, c), own=True).wait_send()
                copy(w, 4 + j, (*chip, c), sibling).wait_send()
            local(w).wait()


def _call(body, *, name, grid, in_specs, out_specs, out_shape, scratch=(), semantics, args, exchange=None):
    if exchange is None:
        return pl.pallas_call(body, name=name, grid=grid, in_specs=in_specs, out_specs=out_specs, out_shape=out_shape,
                              scratch_shapes=list(scratch), compiler_params=_params(*semantics))(*args), None
    n_in, n_out, n_scr, nx = len(in_specs), len(out_specs), len(scratch), exchange.n

    def full_body(*refs):
        ins, refs = refs[:n_in], refs[n_in:]
        x_in, refs = refs[:nx], refs[nx:]
        outs, refs = refs[:n_out], refs[n_out:]
        x_out, refs = refs[:nx], refs[nx:]
        scr, sems = refs[:n_scr], refs[n_scr:]
        first, last = True, True
        for axis, size in enumerate(grid):
            first = jnp.logical_and(first, pl.program_id(axis) == 0)
            last = jnp.logical_and(last, pl.program_id(axis) == size - 1)
        if grid:
            pl.when(first)(lambda: exchange.start(x_in, x_out, sems))
        else:
            exchange.start(x_in, x_out, sems)
        body(*ins, *outs, *scr)
        if grid:
            pl.when(last)(lambda: exchange.wait(x_in, x_out, sems))
        else:
            exchange.wait(x_in, x_out, sems)

    hbm = pl.BlockSpec(memory_space=pltpu.HBM)
    res = pl.pallas_call(
        full_body, name=name, grid=grid,
        in_specs=list(in_specs) + [hbm] * nx, out_specs=list(out_specs) + [hbm] * nx,
        out_shape=list(out_shape) + exchange.out_shape(),
        scratch_shapes=list(scratch) + exchange.scratch(),
        compiler_params=_params(*(["arbitrary"] * len(grid))),
    )(*args, *exchange.arrays)
    return res[:n_out], res[n_out:]


def _alone(exchange, name):
    return _call(lambda: None, name=name, grid=(), in_specs=[], out_specs=[], out_shape=[], semantics=(),
                 args=(), exchange=exchange)[1]


def _mm(a, b, *, ta=False, tb=False, out_dtype, tm, tn, tk, name, exchange=None):
    m, k = (a.shape[1], a.shape[0]) if ta else a.shape
    n = b.shape[0] if tb else b.shape[1]
    assert k == (b.shape[1] if tb else b.shape[0])
    tm, tn, tk = min(tm, m), min(tn, n), min(tk, k)
    assert m % tm == 0 and n % tn == 0 and k % tk == 0, (name, m, n, k)
    nk = k // tk
    dims = (((0 if ta else 1,), (1 if tb else 0,)), ((), ()))

    def body(a_ref, b_ref, o_ref, *acc):
        prod = lax.dot_general(a_ref[...].astype(BF16), b_ref[...].astype(BF16), dims, preferred_element_type=F32)
        if nk == 1:
            o_ref[...] = prod.astype(o_ref.dtype)
            return
        acc_ref, kk = acc[0], pl.program_id(2)

        @pl.when(kk == 0)
        def _():
            acc_ref[...] = prod

        @pl.when((kk > 0) & (kk < nk - 1))
        def _():
            acc_ref[...] += prod

        @pl.when(kk == nk - 1)
        def _():
            o_ref[...] = (acc_ref[...] + prod).astype(o_ref.dtype)

    a_spec = (pl.BlockSpec((tk, tm), lambda i, j, kk: (kk, i)) if ta
              else pl.BlockSpec((tm, tk), lambda i, j, kk: (i, kk)))
    b_spec = (pl.BlockSpec((tn, tk), lambda i, j, kk: (j, kk)) if tb
              else pl.BlockSpec((tk, tn), lambda i, j, kk: (kk, j)))
    (out,), moved = _call(
        body, name=name, grid=(m // tm, n // tn, nk),
        in_specs=[a_spec, b_spec],
        out_specs=[pl.BlockSpec((tm, tn), lambda i, j, kk: (i, j))],
        out_shape=[jax.ShapeDtypeStruct((m, n), out_dtype)],
        scratch=[pltpu.VMEM((tm, tn), F32)] if nk > 1 else [],
        semantics=("parallel", "parallel", "arbitrary"), args=(a, b), exchange=exchange)
    return out if exchange is None else (out, moved)


def _rms_fwd(x2, g):
    t = x2.shape[0]
    tm = min(512, t)

    def body(x_ref, g_ref, o_ref):
        x = x_ref[...]
        r = lax.rsqrt(jnp.mean(x * x, axis=-1, keepdims=True) + EPS)
        o_ref[...] = (x * r * g_ref[...]).astype(o_ref.dtype)

    return pl.pallas_call(
        body, name="rms_in_fwd", grid=(t // tm,),
        in_specs=[pl.BlockSpec((tm, D_MODEL), lambda i: (i, 0)), pl.BlockSpec((1, D_MODEL), lambda i: (0, 0))],
        out_specs=pl.BlockSpec((tm, D_MODEL), lambda i: (i, 0)),
        out_shape=jax.ShapeDtypeStruct((t, D_MODEL), BF16),
        compiler_params=_params("parallel"),
    )(x2, g)


def _decay(lg):
    row = lax.broadcasted_iota(jnp.int32, (RET_TILE, RET_TILE), 0)
    col = lax.broadcasted_iota(jnp.int32, (RET_TILE, RET_TILE), 1)
    within = jnp.exp(lg * jnp.abs(row - col).astype(F32))
    inside = jnp.where((col >> 6) <= (row >> 6), within, 0.0)
    pos = lax.broadcasted_iota(jnp.int32, (RET_TILE, 1), 0).astype(F32)
    q_dec = jnp.exp(lg * (pos + 1.0))
    k_dec = jnp.exp(lg * (RET_TILE - 1.0 - pos))
    tile_dec = jnp.exp(lg * float(RET_TILE))
    return inside, q_dec, k_dec, tile_dec


def _scaled(a_bf16, dec):
    return (a_bf16.astype(F32) * dec).astype(BF16)


def _ret_fwd(proj, cs, sn, lg_arr, batch, seq, exchange):
    t = batch * seq
    nt = seq // RET_TILE

    def body(q_ref, k_ref, v_ref, rg_ref, cs_ref, sn_ref, lg_ref, gro_ref, o_ref, qr_ref, kr_ref):
        lg = lg_ref[:, 0:1]
        cs_t, sn_t = cs_ref[...], sn_ref[...]
        q = q_ref[...].astype(F32)
        k = k_ref[...].astype(F32)
        qr_ref[...] = (q * cs_t + pltpu.roll(q, 64, 1) * sn_t).astype(BF16)
        kr_ref[...] = ((k * cs_t + pltpu.roll(k, 64, 1) * sn_t) * (RET_KEY_DIM ** -0.5)).astype(BF16)
        inside, q_dec, k_dec, tile_dec = _decay(lg)
        state = jnp.zeros((RET_KEY_DIM, RET_VAL_DIM), F32)
        for i in range(nt):
            rows = slice(i * RET_TILE, (i + 1) * RET_TILE)
            qi, ki, vi = qr_ref[rows, :], kr_ref[rows, :], v_ref[rows, :]
            acc = _dot((_dot_nt(qi, ki) * inside).astype(BF16), vi)
            if i > 0:
                acc = acc + _dot(_scaled(qi, q_dec), state.astype(BF16))
            if i < nt - 1:
                state = state * tile_dec + _dot_tn(_scaled(ki, k_dec), vi)
            o_ref[rows, :] = acc
            xc = acc - jnp.mean(acc, axis=-1, keepdims=True)
            nrm = xc * lax.rsqrt(jnp.mean(xc * xc, axis=-1, keepdims=True) + EPS)
            rg = rg_ref[rows, :].astype(F32)
            gro_ref[rows, :] = (rg * _sigmoid(rg) * nrm).astype(BF16)

    def col(base, width):
        return lambda b, h: (b, base // width + h)

    return _call(
        body, name="ret_fwd", grid=(batch, RET_HEADS),
        in_specs=[pl.BlockSpec((seq, RET_KEY_DIM), col(C_RQ, RET_KEY_DIM)),
                  pl.BlockSpec((seq, RET_KEY_DIM), col(C_RK, RET_KEY_DIM)),
                  pl.BlockSpec((seq, RET_VAL_DIM), col(C_RV, RET_VAL_DIM)),
                  pl.BlockSpec((seq, RET_VAL_DIM), col(C_RG, RET_VAL_DIM)),
                  pl.BlockSpec((seq, RET_KEY_DIM), lambda b, h: (0, 0)),
                  pl.BlockSpec((seq, RET_KEY_DIM), lambda b, h: (0, 0)),
                  pl.BlockSpec((None, 1, LANES), lambda b, h: (h, 0, 0))],
        out_specs=[pl.BlockSpec((seq, RET_VAL_DIM), lambda b, h: (b, h)),
                   pl.BlockSpec((seq, RET_VAL_DIM), lambda b, h: (b, h)),
                   pl.BlockSpec((seq, RET_KEY_DIM), lambda b, h: (b, h)),
                   pl.BlockSpec((seq, RET_KEY_DIM), lambda b, h: (b, h))],
        out_shape=[jax.ShapeDtypeStruct((t, RET_HEADS * RET_VAL_DIM), BF16),
                   jax.ShapeDtypeStruct((t, RET_HEADS * RET_VAL_DIM), F32),
                   jax.ShapeDtypeStruct((t, RET_HEADS * RET_KEY_DIM), BF16),
                   jax.ShapeDtypeStruct((t, RET_HEADS * RET_KEY_DIM), BF16)],
        semantics=("parallel", "parallel"), args=(proj, proj, proj, proj, cs, sn, lg_arr), exchange=exchange)


def _att_bias(w_ref, bias_ref):
    n_i = lax.broadcasted_iota(jnp.int32, (ATT_Q, BIAS_LEN), 0)
    qc = lax.broadcasted_iota(jnp.int32, (ATT_Q, ATT_WIN), 0) >> 6
    kc = lax.broadcasted_iota(jnp.int32, (ATT_Q, ATT_WIN), 1) >> 6
    dc = qc + BAND_CHUNKS - kc
    band = (dc >= 0) & (dc <= BAND_CHUNKS)
    for e in range(2):
        xw = jnp.broadcast_to(w_ref[e:e + 1, :], (ATT_Q, BIAS_LEN))
        for bit in range(8):
            xw = jnp.where(((n_i >> bit) & 1) == 1, pltpu.roll(xw, 1 << bit, 1), xw)
        bias_ref[e] = jnp.where(band, xw[:, BIAS_LEN - ATT_WIN:], NEG_INF)


def _att_specs(batch, seq):
    ni = seq // ATT_Q
    q_spec = pl.BlockSpec((ATT_Q, LANES), lambda hp, b, i: (b * ni + i, C_AQ // LANES + hp))
    kv_spec = pl.BlockSpec((seq + ATT_PAD, LANES), lambda hp, b, i: (b, hp))
    w_spec = pl.BlockSpec((None, 2, BIAS_LEN), lambda hp, b, i: (hp, 0, 0))
    return ni, q_spec, kv_spec, w_spec


def _att_scores(q2, k2, bias, sel, start_ok):
    qm = jnp.where(sel, q2, jnp.zeros_like(q2))
    s = _dot_nt(qm, k2) * (1.0 / 8.0) + bias
    s = jnp.where(start_ok, s, NEG_INF)
    p = jnp.exp(s - jnp.max(s, axis=-1, keepdims=True))
    return p / jnp.sum(p, axis=-1, keepdims=True)


def _att_fwd(proj, kpad, vpad, wvec, batch, seq, exchange):
    ni, q_spec, kv_spec, w_spec = _att_specs(batch, seq)

    def body(q_ref, k_ref, v_ref, w_ref, o_ref, bias_ref):
        b, i = pl.program_id(1), pl.program_id(2)

        @pl.when((b == 0) & (i == 0))
        def _():
            _att_bias(w_ref, bias_ref)

        win = pl.ds(pl.multiple_of(i * ATT_Q, ATT_Q), ATT_WIN)
        k2, v2, q2 = k_ref[win, :], v_ref[win, :], q_ref[...]
        start_ok = lax.broadcasted_iota(jnp.int32, (ATT_Q, ATT_WIN), 1) + (i * ATT_Q - ATT_PAD) >= 0
        lo = lax.broadcasted_iota(jnp.int32, (1, LANES), 1) < 64
        out = jnp.zeros((ATT_Q, LANES), F32)
        for e in range(2):
            sel = lo if e == 0 else jnp.logical_not(lo)
            p = _att_scores(q2, k2, bias_ref[e], sel, start_ok)
            out = out + _dot(p.astype(BF16), jnp.where(sel, v2, jnp.zeros_like(v2)))
        o_ref[...] = out.astype(BF16)

    return _call(
        body, name="att_fwd", grid=(ATT_HEADS // 2, batch, ni),
        in_specs=[q_spec, kv_spec, kv_spec, w_spec],
        out_specs=[pl.BlockSpec((ATT_Q, LANES), lambda hp, b, i: (b * ni + i, hp))],
        out_shape=[jax.ShapeDtypeStruct((batch * seq, ATT_HEADS * 64), BF16)],
        scratch=[pltpu.VMEM((2, ATT_Q, ATT_WIN), F32)],
        semantics=("arbitrary", "arbitrary", "arbitrary"), args=(proj, kpad, vpad, wvec), exchange=exchange)


def _mix_fwd(gro, ao, proj, b_gate, w_ret, w_att_t):
    t = gro.shape[0]
    tm, tn = min(256, t), 512

    def body(gro_ref, ao_ref, glr_ref, gla_ref, br_ref, ba_ref, wr_ref, wa_ref, z_ref, yr_ref, ya_ref):
        yr = _dot(gro_ref[...], wr_ref[...])
        ya = _dot_nt(ao_ref[...], wa_ref[...])
        gr = _sigmoid(glr_ref[...].astype(F32) + br_ref[...])
        ga = _sigmoid(gla_ref[...].astype(F32) + ba_ref[...])
        z_ref[...] = (gr * yr + ga * ya).astype(BF16)
        yr_ref[...] = yr.astype(BF16)
        ya_ref[...] = ya.astype(BF16)

    nb = D_MODEL // tn
    out = pl.BlockSpec((tm, tn), lambda i, j: (i, j))
    return pl.pallas_call(
        body, name="mix_fwd", grid=(t // tm, nb),
        in_specs=[pl.BlockSpec((tm, D_MODEL), lambda i, j: (i, 0)),
                  pl.BlockSpec((tm, 512), lambda i, j: (i, 0)),
                  pl.BlockSpec((tm, tn), lambda i, j: (i, C_GL // tn + j)),
                  pl.BlockSpec((tm, tn), lambda i, j: (i, C_GL // tn + nb + j)),
                  pl.BlockSpec((1, tn), lambda i, j: (0, j)),
                  pl.BlockSpec((1, tn), lambda i, j: (0, nb + j)),
                  pl.BlockSpec((D_MODEL, tn), lambda i, j: (0, j)),
                  pl.BlockSpec((tn, 512), lambda i, j: (j, 0))],
        out_specs=[out, out, out],
        out_shape=[jax.ShapeDtypeStruct((t, D_MODEL), BF16)] * 3,
        compiler_params=_params("parallel", "parallel"),
    )(gro, ao, proj, proj, b_gate, b_gate, w_ret, w_att_t)


def _out_fwd(z, x2, w_out, g2):
    t = z.shape[0]
    tm = min(256, t)

    def body(z_ref, x_ref, w_ref, g_ref, h_ref, hn_ref):
        h = x_ref[...] + _dot(z_ref[...], w_ref[...])
        h_ref[...] = h
        r = lax.rsqrt(jnp.mean(h * h, axis=-1, keepdims=True) + EPS)
        hn_ref[...] = (h * r * g_ref[...]).astype(BF16)

    row = pl.BlockSpec((tm, D_MODEL), lambda i: (i, 0))
    return pl.pallas_call(
        body, name="out_fwd", grid=(t // tm,),
        in_specs=[row, row, pl.BlockSpec((D_MODEL, D_MODEL), lambda i: (0, 0)),
                  pl.BlockSpec((1, D_MODEL), lambda i: (0, 0))],
        out_specs=[row, row],
        out_shape=[jax.ShapeDtypeStruct((t, D_MODEL), F32), jax.ShapeDtypeStruct((t, D_MODEL), BF16)],
        compiler_params=_params("parallel"),
    )(z, x2, w_out, g2)


def _ffn_up(hn, wg_t, wu_t):
    t = hn.shape[0]
    tm, tn = min(512, t), D_FF // 2

    def body(h_ref, wg_ref, wu_ref, g_ref, u_ref, a_ref):
        g = _dot_nt(h_ref[...], wg_ref[...])
        u = _dot_nt(h_ref[...], wu_ref[...])
        g_ref[...] = g.astype(BF16)
        u_ref[...] = u.astype(BF16)
        a_ref[...] = (g * _sigmoid(g) * u).astype(BF16)

    w_spec = pl.BlockSpec((tn, D_MODEL), lambda j, i: (j, 0))
    out = pl.BlockSpec((tm, tn), lambda j, i: (i, j))
    return pl.pallas_call(
        body, name="ffn_up", grid=(D_FF // tn, t // tm),
        in_specs=[pl.BlockSpec((tm, D_MODEL), lambda j, i: (i, 0)), w_spec, w_spec],
        out_specs=[out, out, out],
        out_shape=[jax.ShapeDtypeStruct((t, D_FF), BF16)] * 3,
        compiler_params=_params("parallel", "parallel"),
    )(hn, wg_t, wu_t)


def _ffn_down_loss(a, h1, tgt, w_down, g3):
    t = a.shape[0]
    tm = min(512, t)

    def body(a_ref, h_ref, t_ref, w_ref, g_ref, dh_ref, dhb_ref, loss_ref, dg_ref):
        @pl.when(pl.program_id(0) == 0)
        def _():
            loss_ref[...] = jnp.zeros_like(loss_ref)
            dg_ref[...] = jnp.zeros_like(dg_ref)

        g = g_ref[...]
        h2 = h_ref[...] + _dot(a_ref[...], w_ref[...])
        r = lax.rsqrt(jnp.mean(h2 * h2, axis=-1, keepdims=True) + EPS)
        err = h2 * r * g - t_ref[...]
        loss_ref[...] += jnp.sum(err * err) * (0.5 / D_MODEL)
        dy = err * (1.0 / D_MODEL)
        dh, dg_rows = _rms_bwd(h2, g, dy)
        dg_ref[...] += jnp.sum(dg_rows, axis=0, keepdims=True)
        dh_ref[...] = dh
        dhb_ref[...] = dh.astype(BF16)

    row = pl.BlockSpec((tm, D_MODEL), lambda i: (i, 0))
    vec = pl.BlockSpec((1, D_MODEL), lambda i: (0, 0))
    return pl.pallas_call(
        body, name="ffn_down_loss", grid=(t // tm,),
        in_specs=[pl.BlockSpec((tm, D_FF), lambda i: (i, 0)), row, row,
                  pl.BlockSpec((D_FF, D_MODEL), lambda i: (0, 0)), vec],
        out_specs=[row, row, pl.BlockSpec((1, LANES), lambda i: (0, 0)), vec],
        out_shape=[jax.ShapeDtypeStruct((t, D_MODEL), F32), jax.ShapeDtypeStruct((t, D_MODEL), BF16),
                   jax.ShapeDtypeStruct((1, LANES), F32), jax.ShapeDtypeStruct((1, D_MODEL), F32)],
        compiler_params=_params("arbitrary"),
    )(a, h1, tgt, w_down, g3)


def _ffn_bwd_act(dh2b, w_down, g_act, u_act, exchange):
    t = dh2b.shape[0]
    tm, tn = min(512, t), D_FF // 2

    def body(d_ref, w_ref, g_ref, u_ref, dg_ref, du_ref):
        da = _dot_nt(d_ref[...], w_ref[...])
        g = g_ref[...].astype(F32)
        u = u_ref[...].astype(F32)
        sg = _sigmoid(g)
        dg_ref[...] = (da * u * sg * (1.0 + g * (1.0 - sg))).astype(BF16)
        du_ref[...] = (da * g * sg).astype(BF16)

    blk = pl.BlockSpec((tm, tn), lambda j, i: (i, j))
    return _call(
        body, name="ffn_bwd_act", grid=(D_FF // tn, t // tm),
        in_specs=[pl.BlockSpec((tm, D_MODEL), lambda j, i: (i, 0)),
                  pl.BlockSpec((tn, D_MODEL), lambda j, i: (j, 0)), blk, blk],
        out_specs=[blk, blk],
        out_shape=[jax.ShapeDtypeStruct((t, D_FF), BF16)] * 2,
        semantics=("parallel", "parallel"), args=(dh2b, w_down, g_act, u_act), exchange=exchange)


def _ffn_bwd_in(dg, du, wg_t, wu_t, h1, dh2, g2, exchange):
    t = dg.shape[0]
    tm, tk = min(512, t), D_FF // 2
    nk = D_FF // tk

    def body(dg_ref, du_ref, wg_ref, wu_ref, h_ref, d2_ref, g_ref, dh_ref, dhb_ref, gn_ref, acc_ref):
        i, kk = pl.program_id(0), pl.program_id(1)

        @pl.when((i == 0) & (kk == 0))
        def _():
            gn_ref[...] = jnp.zeros_like(gn_ref)

        @pl.when(kk == 0)
        def _():
            acc_ref[...] = jnp.zeros_like(acc_ref)

        acc_ref[...] += _dot(dg_ref[...], wg_ref[...]) + _dot(du_ref[...], wu_ref[...])

        @pl.when(kk == nk - 1)
        def _():
            dx, dg_rows = _rms_bwd(h_ref[...], g_ref[...], acc_ref[...])
            dh = d2_ref[...] + dx
            dh_ref[...] = dh
            dhb_ref[...] = dh.astype(BF16)
            gn_ref[...] += jnp.sum(dg_rows, axis=0, keepdims=True)

    act = pl.BlockSpec((tm, tk), lambda i, kk: (i, kk))
    wsp = pl.BlockSpec((tk, D_MODEL), lambda i, kk: (kk, 0))
    row = pl.BlockSpec((tm, D_MODEL), lambda i, kk: (i, 0))
    vec = pl.BlockSpec((1, D_MODEL), lambda i, kk: (0, 0))
    return _call(
        body, name="ffn_bwd_in", grid=(t // tm, nk),
        in_specs=[act, act, wsp, wsp, row, row, vec],
        out_specs=[row, row, vec],
        out_shape=[jax.ShapeDtypeStruct((t, D_MODEL), F32), jax.ShapeDtypeStruct((t, D_MODEL), BF16),
                   jax.ShapeDtypeStruct((1, D_MODEL), F32)],
        scratch=[pltpu.VMEM((tm, D_MODEL), F32)],
        semantics=("arbitrary", "arbitrary"), args=(dg, du, wg_t, wu_t, h1, dh2, g2), exchange=exchange)


def _mix_bwd(dh1b, w_out, proj, b_gate, y_ret, y_att, exchange):
    t = dh1b.shape[0]
    tm, tn = min(256, t), 512
    nb = D_MODEL // tn

    def body(d_ref, w_ref, glr_ref, gla_ref, br_ref, ba_ref, yr_ref, ya_ref,
             dyr_ref, dya_ref, dglr_ref, dgla_ref, dbr_ref, dba_ref):
        @pl.when(pl.program_id(1) == 0)
        def _():
            dbr_ref[...] = jnp.zeros_like(dbr_ref)
            dba_ref[...] = jnp.zeros_like(dba_ref)

        dz = _dot_nt(d_ref[...], w_ref[...])
        gr = _sigmoid(glr_ref[...].astype(F32) + br_ref[...])
        ga = _sigmoid(gla_ref[...].astype(F32) + ba_ref[...])
        dyr_ref[...] = (dz * gr).astype(BF16)
        dya_ref[...] = (dz * ga).astype(BF16)
        dglr = dz * yr_ref[...].astype(F32) * gr * (1.0 - gr)
        dgla = dz * ya_ref[...].astype(F32) * ga * (1.0 - ga)
        dglr_ref[...] = dglr.astype(BF16)
        dgla_ref[...] = dgla.astype(BF16)
        dbr_ref[...] += jnp.sum(dglr, axis=0, keepdims=True)
        dba_ref[...] += jnp.sum(dgla, axis=0, keepdims=True)

    blk = pl.BlockSpec((tm, tn), lambda j, i: (i, j))
    vec = pl.BlockSpec((1, tn), lambda j, i: (0, j))
    return _call(
        body, name="mix_bwd", grid=(nb, t // tm),
        in_specs=[pl.BlockSpec((tm, D_MODEL), lambda j, i: (i, 0)),
                  pl.BlockSpec((tn, D_MODEL), lambda j, i: (j, 0)),
                  pl.BlockSpec((tm, tn), lambda j, i: (i, C_GL // tn + j)),
                  pl.BlockSpec((tm, tn), lambda j, i: (i, C_GL // tn + nb + j)),
                  vec, pl.BlockSpec((1, tn), lambda j, i: (0, nb + j)), blk, blk],
        out_specs=[blk, blk, blk, blk, vec, vec],
        out_shape=[jax.ShapeDtypeStruct((t, D_MODEL), BF16)] * 4 + [jax.ShapeDtypeStruct((1, D_MODEL), F32)] * 2,
        semantics=("arbitrary", "arbitrary"), args=(dh1b, w_out, proj, proj, b_gate, b_gate, y_ret, y_att),
        exchange=exchange)


def _ret_bwd(dgro, proj, o_ret, qr, kr, cs, sn, lg_arr, batch, seq, exchange):
    t = batch * seq
    nt = seq // RET_TILE

    def body(dgro_ref, rg_ref, o_ref, qr_ref, kr_ref, v_ref, cs_ref, sn_ref, lg_ref,
             dq_ref, dk_ref, dv_ref, drg_ref, do_ref, st_ref):
        lg = lg_ref[:, 0:1]
        inside, q_dec, k_dec, tile_dec = _decay(lg)

        state = jnp.zeros((RET_KEY_DIM, RET_VAL_DIM), F32)
        for i in range(nt - 1):
            rows = slice(i * RET_TILE, (i + 1) * RET_TILE)
            state = state * tile_dec + _dot_tn(_scaled(kr_ref[rows, :], k_dec), v_ref[rows, :])
            st_ref[i + 1] = state.astype(BF16)

        for i in range(nt):
            rows = slice(i * RET_TILE, (i + 1) * RET_TILE)
            o = o_ref[rows, :]
            xc = o - jnp.mean(o, axis=-1, keepdims=True)
            rs = lax.rsqrt(jnp.mean(xc * xc, axis=-1, keepdims=True) + EPS)
            nrm = xc * rs
            rg = rg_ref[rows, :].astype(F32)
            sg = _sigmoid(rg)
            dg = dgro_ref[rows, :].astype(F32)
            drg_ref[rows, :] = (dg * nrm * sg * (1.0 + rg * (1.0 - sg))).astype(BF16)
            dn = dg * rg * sg
            do = rs * (dn - jnp.mean(dn, axis=-1, keepdims=True)
                       - nrm * jnp.mean(dn * nrm, axis=-1, keepdims=True))
            do_ref[rows, :] = do.astype(BF16)

        dstate = jnp.zeros((RET_KEY_DIM, RET_VAL_DIM), F32)
        for i in reversed(range(nt)):
            rows = slice(i * RET_TILE, (i + 1) * RET_TILE)
            qi, ki, vi, doi = qr_ref[rows, :], kr_ref[rows, :], v_ref[rows, :], do_ref[rows, :]
            p = (_dot_nt(qi, ki) * inside).astype(BF16)
            dp = (_dot_nt(doi, vi) * inside).astype(BF16)
            dq = _dot(dp, ki)
            dk = _dot_tn(dp, qi)
            dv = _dot_tn(p, doi)
            if i > 0:
                dq = dq + _dot_nt(doi, st_ref[i]) * q_dec
            if i < nt - 1:
                dsb = dstate.astype(BF16)
                dk = dk + _dot_nt(vi, dsb) * k_dec
                dv = dv + _dot(_scaled(ki, k_dec), dsb)
            if i > 0:
                dstate = dstate * tile_dec + _dot_tn(_scaled(qi, q_dec), doi)
            dq_ref[rows, :] = (dq * cs_ref[rows, :] - pltpu.roll(dq, 64, 1) * sn_ref[rows, :]).astype(BF16)
            dk = (dk * cs_ref[rows, :] - pltpu.roll(dk, 64, 1) * sn_ref[rows, :]) * (RET_KEY_DIM ** -0.5)
            dk_ref[rows, :] = dk.astype(BF16)
            dv_ref[rows, :] = dv.astype(BF16)

    key = pl.BlockSpec((seq, RET_KEY_DIM), lambda b, h: (b, h))
    val = pl.BlockSpec((seq, RET_VAL_DIM), lambda b, h: (b, h))
    tab = pl.BlockSpec((seq, RET_KEY_DIM), lambda b, h: (0, 0))
    return _call(
        body, name="ret_bwd", grid=(batch, RET_HEADS),
        in_specs=[val, pl.BlockSpec((seq, RET_VAL_DIM), lambda b, h: (b, C_RG // RET_VAL_DIM + h)), val, key, key,
                  pl.BlockSpec((seq, RET_VAL_DIM), lambda b, h: (b, C_RV // RET_VAL_DIM + h)), tab, tab,
                  pl.BlockSpec((None, 1, LANES), lambda b, h: (h, 0, 0))],
        out_specs=[key, key, val, val],
        out_shape=[jax.ShapeDtypeStruct((t, RET_HEADS * RET_KEY_DIM), BF16)] * 2
                  + [jax.ShapeDtypeStruct((t, RET_HEADS * RET_VAL_DIM), BF16)] * 2,
        scratch=[pltpu.VMEM((seq, RET_VAL_DIM), BF16), pltpu.VMEM((nt, RET_KEY_DIM, RET_VAL_DIM), BF16)],
        semantics=("parallel", "parallel"), args=(dgro, proj, o_ret, qr, kr, proj, cs, sn, lg_arr),
        exchange=exchange)


def _att_bwd(proj, kpad, vpad, wvec, dao, batch, seq, exchange):
    ni, q_spec, kv_spec, w_spec = _att_specs(batch, seq)
    t = batch * seq

    def body(q_ref, k_ref, v_ref, w_ref, do_ref, dq_ref, dk_ref, dv_ref, dw_ref,
             bias_ref, dbias_ref, dk_acc, dv_acc):
        b, i = pl.program_id(1), pl.program_id(2)

        @pl.when((b == 0) & (i == 0))
        def _():
            _att_bias(w_ref, bias_ref)
            dbias_ref[...] = jnp.zeros_like(dbias_ref)

        @pl.when(i == 0)
        def _():
            dk_acc[...] = jnp.zeros_like(dk_acc)
            dv_acc[...] = jnp.zeros_like(dv_acc)

        win = pl.ds(pl.multiple_of(i * ATT_Q, ATT_Q), ATT_WIN)
        k2, v2, q2, do2 = k_ref[win, :], v_ref[win, :], q_ref[...], do_ref[...]
        start_ok = lax.broadcasted_iota(jnp.int32, (ATT_Q, ATT_WIN), 1) + (i * ATT_Q - ATT_PAD) >= 0
        lo = lax.broadcasted_iota(jnp.int32, (1, LANES), 1) < 64
        dq = jnp.zeros((ATT_Q, LANES), F32)
        dk = jnp.zeros((ATT_WIN, LANES), F32)
        dv = jnp.zeros((ATT_WIN, LANES), F32)
        for e in range(2):
            sel = lo if e == 0 else jnp.logical_not(lo)
            p = _att_scores(q2, k2, bias_ref[e], sel, start_ok)
            dom = jnp.where(sel, do2, jnp.zeros_like(do2))
            dp = _dot_nt(dom, v2)
            ds = p * (dp - jnp.sum(dp * p, axis=-1, keepdims=True))
            dbias_ref[e] += ds
            dsb = (ds * (1.0 / 8.0)).astype(BF16)
            dq = dq + _dot(dsb, jnp.where(sel, k2, jnp.zeros_like(k2)))
            dk = dk + _dot_tn(dsb, jnp.where(sel, q2, jnp.zeros_like(q2)))
            dv = dv + _dot_tn(p.astype(BF16), dom)
        dq_ref[...] = dq.astype(BF16)
        dk_acc[win, :] += dk
        dv_acc[win, :] += dv

        @pl.when(i == ni - 1)
        def _():
            dk_ref[...] = dk_acc[ATT_PAD:, :].astype(BF16)
            dv_ref[...] = dv_acc[ATT_PAD:, :].astype(BF16)

        @pl.when((b == batch - 1) & (i == ni - 1))
        def _():
            n_i = lax.broadcasted_iota(jnp.int32, (ATT_Q, BIAS_LEN), 0)
            for e in range(2):
                xw = jnp.concatenate([jnp.zeros((ATT_Q, BIAS_LEN - ATT_WIN), F32), dbias_ref[e]], axis=1)
                for bit in range(8):
                    xw = jnp.where(((n_i >> bit) & 1) == 1, pltpu.roll(xw, BIAS_LEN - (1 << bit), 1), xw)
                dw_ref[e:e + 1, :] = jnp.sum(xw, axis=0, keepdims=True)

    seq_blk = pl.BlockSpec((seq, LANES), lambda hp, b, i: (b, hp))
    q_out = pl.BlockSpec((ATT_Q, LANES), lambda hp, b, i: (b * ni + i, hp))
    return _call(
        body, name="att_bwd", grid=(ATT_HEADS // 2, batch, ni),
        in_specs=[q_spec, kv_spec, kv_spec, w_spec, q_out],
        out_specs=[q_out, seq_blk, seq_blk, w_spec],
        out_shape=[jax.ShapeDtypeStruct((t, 512), BF16)] * 3
                  + [jax.ShapeDtypeStruct((ATT_HEADS // 2, 2, BIAS_LEN), F32)],
        scratch=[pltpu.VMEM((2, ATT_Q, ATT_WIN), F32), pltpu.VMEM((2, ATT_Q, ATT_WIN), F32),
                 pltpu.VMEM((seq + ATT_PAD, LANES), F32), pltpu.VMEM((seq + ATT_PAD, LANES), F32)],
        semantics=("arbitrary", "arbitrary", "arbitrary"), args=(proj, kpad, vpad, wvec, dao), exchange=exchange)


def _rms_in_bwd(x2, dxn, dh1, g1):
    t = x2.shape[0]
    tm = min(512, t)

    def body(x_ref, d_ref, h_ref, g_ref, dx_ref, dg_ref):
        @pl.when(pl.program_id(0) == 0)
        def _():
            dg_ref[...] = jnp.zeros_like(dg_ref)

        dx, dg_rows = _rms_bwd(x_ref[...], g_ref[...], d_ref[...])
        dx_ref[...] = h_ref[...] + dx
        dg_ref[...] += jnp.sum(dg_rows, axis=0, keepdims=True)

    row = pl.BlockSpec((tm, D_MODEL), lambda i: (i, 0))
    vec = pl.BlockSpec((1, D_MODEL), lambda i: (0, 0))
    return pl.pallas_call(
        body, name="rms_in_bwd", grid=(t // tm,),
        in_specs=[row, row, row, vec], out_specs=[row, vec],
        out_shape=[jax.ShapeDtypeStruct((t, D_MODEL), F32), jax.ShapeDtypeStruct((1, D_MODEL), F32)],
        compiler_params=_params("arbitrary"),
    )(x2, dxn, dh1, g1)


def _pack_small(dg1, dbr, dba, dg2, dg3, dw):
    def body(a_ref, b_ref, c_ref, d_ref, e_ref, w_ref, o_ref):
        o_ref[...] = jnp.zeros_like(o_ref)
        for r, ref in enumerate((a_ref, b_ref, c_ref, d_ref, e_ref)):
            o_ref[r:r + 1, :] = ref[...]
        for hp in range(ATT_HEADS // 2):
            o_ref[8 + 2 * hp:10 + 2 * hp, :] = w_ref[hp]

    return pl.pallas_call(body, name="pack_small",
                          out_shape=jax.ShapeDtypeStruct((16, D_MODEL), F32))(dg1, dbr, dba, dg2, dg3, dw)


def _rotary_tables(seq):
    freqs = ROPE_BASE ** (-jnp.arange(0, RET_KEY_DIM, 2, dtype=F32) / RET_KEY_DIM)
    ang = jnp.arange(seq, dtype=F32)[:, None] * freqs[None, :]
    cos, sin = jnp.cos(ang), jnp.sin(ang)
    return jnp.concatenate([cos, cos], axis=1), jnp.concatenate([-sin, sin], axis=1)


def _bias_rows(rel_bias):
    n_far = BIAS_LEN - ATT_Q - MAX_REL + 1
    n_near = BIAS_LEN - n_far - (N_REL - 2)
    w = jnp.concatenate([jnp.broadcast_to(rel_bias[:, N_REL - 1:], (ATT_HEADS, n_far)),
                         rel_bias[:, 1:N_REL - 1][:, ::-1],
                         jnp.broadcast_to(rel_bias[:, :1], (ATT_HEADS, n_near))], axis=1)
    return w.reshape(ATT_HEADS // 2, 2, BIAS_LEN)


def _bias_rows_bwd(dw):
    n_far = BIAS_LEN - ATT_Q - MAX_REL + 1
    mid = dw[:, n_far:n_far + N_REL - 2][:, ::-1]
    return jnp.concatenate([jnp.sum(dw[:, n_far + N_REL - 2:], axis=1, keepdims=True), mid,
                            jnp.sum(dw[:, :n_far], axis=1, keepdims=True)], axis=1)


def _pad_keys(a, batch, seq):
    a = a.reshape(batch, seq, a.shape[-1])
    return jnp.pad(a, ((0, 0), (ATT_PAD, 0), (0, 0))).reshape(batch * (seq + ATT_PAD), a.shape[-1])


def _step(x, tgt, norm_mix, b_gate, norm_ffn, norm_final, rel_bias_shard, shard):
    batch, seq, _ = x.shape
    t = batch * seq
    n_rb = rel_bias_shard.shape[-1]
    x2, tgt2 = x.reshape(t, D_MODEL), tgt.reshape(t, D_MODEL)
    g3 = norm_final.reshape(1, D_MODEL)
    cs, sn = _rotary_tables(seq)
    lg = np.log(1.0 - 2.0 ** (-5.0 - np.arange(RET_HEADS, dtype=np.float32))).astype(np.float32)
    lg_arr = jnp.asarray(np.broadcast_to(lg[:, None, None], (RET_HEADS, 1, LANES)))

    def gather(*names):
        return _ChipGather([shard[nm] for nm in names])

    def scatter(*grads):
        return _Exchange(grads, scatter=True)

    rb_pad = jnp.pad(rel_bias_shard, ((0, 0), (0, LANES - n_rb)))
    w_in_t, rb_full = _alone(_ChipGather([shard["w_in_t"], rb_pad]), "gather_w_in")
    rb_full = rb_full.reshape(N_DEV, ATT_HEADS, LANES)[:, :, :n_rb]
    wvec = _bias_rows(jnp.transpose(rb_full, (1, 0, 2)).reshape(ATT_HEADS, N_DEV * n_rb))

    xn = _rms_fwd(x2, norm_mix)
    proj, (w_ret, w_att_t, w_out, w_gate_t) = _mm(
        xn, w_in_t, tb=True, out_dtype=BF16, tm=1024, tn=1664, tk=1024, name="proj",
        exchange=gather("w_ret", "w_att_t", "w_out", "w_gate_t"))
    (gro, o_ret, qr, kr), (w_up_t,) = _ret_fwd(proj, cs, sn, lg_arr, batch, seq, gather("w_up_t"))
    kpad = _pad_keys(proj[:, C_AK:C_AV], batch, seq)
    vpad = _pad_keys(proj[:, C_AV:C_GL], batch, seq)
    (ao,), (w_down,) = _att_fwd(proj, kpad, vpad, wvec, batch, seq, gather("w_down"))
    z, y_ret, y_att = _mix_fwd(gro, ao, proj, b_gate, w_ret, w_att_t)
    h1, hn = _out_fwd(z, x2, w_out, norm_ffn)
    g_act, u_act, a_act = _ffn_up(hn, w_gate_t, w_up_t)
    dh2, dh2b, loss, dg3 = _ffn_down_loss(a_act, h1, tgt2, w_down, g3)

    wg = dict(out_dtype=BF16, tn=1024, ta=True)
    slots = {}
    dw_down = _mm(a_act, dh2b, tm=1408, tk=1024, name="dw_down", **wg)
    (d_gact, d_uact), (slots["w_down"],) = _ffn_bwd_act(dh2b, w_down, g_act, u_act, scatter(dw_down))
    dw_gate = _mm(d_gact, hn, tm=1408, tk=1024, name="dw_gate", **wg)
    dw_up = _mm(d_uact, hn, tm=1408, tk=1024, name="dw_up", **wg)
    (dh1, dh1b, dg2), (slots["w_gate_t"],) = _ffn_bwd_in(d_gact, d_uact, w_gate_t, w_up_t, h1, dh2, norm_ffn,
                                                       scatter(dw_gate))
    dw_out = _mm(z, dh1b, tm=1024, tk=2048, name="dw_out", **wg)
    (dyr, dya, dglr, dgla, dbr, dba), _ = _mix_bwd(dh1b, w_out, proj, b_gate, y_ret, y_att, None)
    dgro = _mm(dyr, w_ret, tb=True, out_dtype=BF16, tm=1024, tn=1024, tk=1024, name="dgro")
    dao = _mm(dya, w_att_t, out_dtype=BF16, tm=1024, tn=512, tk=1024, name="dao")
    dw_ret = _mm(gro, dyr, tm=1024, tk=2048, name="dw_ret", **wg)
    dw_att = _mm(dya, ao, tm=1024, tk=2048, name="dw_att", **wg)
    (drq, drk, drv, drg), (slots["w_up_t"],) = _ret_bwd(
        dgro, proj, o_ret, qr, kr, cs, sn, lg_arr, batch, seq, scatter(dw_up))
    (daq, dak, dav, dw), (slots["w_out"], slots["w_ret"], slots["w_att_t"]) = _att_bwd(
        proj, kpad, vpad, wvec, dao, batch, seq, scatter(dw_out, dw_ret, dw_att))
    dproj = jnp.concatenate([drq, drk, drv, drg, daq, dak, dav, dglr, dgla], axis=1)
    dw_in = _mm(dproj, xn, tm=512, tk=2048, name="dw_in", **wg)
    dxn, (slots["w_in_t"],) = _mm(dproj, w_in_t, out_dtype=F32, tm=1024, tn=1024, tk=1664, name="dxn",
                                  exchange=scatter(dw_in))
    dx, dg1 = _rms_in_bwd(x2, dxn, dh1, norm_mix)
    small = _pack_small(dg1, dbr, dba, dg2, dg3, dw)
    (small_slots,) = _alone(_ChipGather([small]), "gather_small")
    return loss[0, 0], dx.reshape(batch, seq, D_MODEL), slots, small_slots.reshape(N_DEV, 16, D_MODEL)


def _sum_slots(slots, name):
    _, r, c = slots.shape
    tr = max(d for d in range(16, r + 1, 16) if r % d == 0 and (d * c <= 256 * 1024 or d == 16))

    def body(s_ref, o_ref):
        acc = s_ref[0].astype(F32)
        for s in range(1, N_DEV):
            acc = acc + s_ref[s].astype(F32)
        o_ref[...] = acc

    return pl.pallas_call(
        body, name=name, grid=(r // tr,),
        in_specs=[pl.BlockSpec((N_DEV, tr, c), lambda i: (0, i, 0))],
        out_specs=pl.BlockSpec((tr, c), lambda i: (i, 0)),
        out_shape=jax.ShapeDtypeStruct((r, c), F32),
        compiler_params=_params("parallel"),
    )(slots)


def _adamw_math(w, g, m, v):
    m = ADAM_B1 * m + (1.0 - ADAM_B1) * g
    v = ADAM_B2 * v + (1.0 - ADAM_B2) * (g * g)
    m_hat = m / (1.0 - ADAM_B1 ** ADAM_STEP)
    v_hat = v / (1.0 - ADAM_B2 ** ADAM_STEP)
    return -ADAM_LR * (m_hat / (jnp.sqrt(v_hat) + ADAM_EPS) + ADAM_WD * w), m, v


def _adamw(w, g, m, v, name):
    r, c = w.shape
    tr = r
    while tr * c > 128 * 1024 and tr % 16 == 0:
        tr //= 2

    def body(w_ref, g_ref, m_ref, v_ref, d_ref, nm_ref, nv_ref):
        d_ref[...], nm_ref[...], nv_ref[...] = _adamw_math(w_ref[...], g_ref[...], m_ref[...], v_ref[...])

    blk = pl.BlockSpec((tr, c), lambda i: (i, 0))
    return pl.pallas_call(
        body, name=name, grid=(r // tr,),
        in_specs=[blk] * 4, out_specs=[blk] * 3,
        out_shape=[jax.ShapeDtypeStruct((r, c), F32)] * 3,
        compiler_params=_params("parallel"),
    )(w, g, m, v)


def _adamw_small(ws, gs, ms, vs):
    n = len(ws)

    def body(*refs):
        for i in range(n):
            w_ref, g_ref, m_ref, v_ref = (refs[j * n + i] for j in range(4))
            d_ref, nm_ref, nv_ref = (refs[(4 + j) * n + i] for j in range(3))
            d_ref[...], nm_ref[...], nv_ref[...] = _adamw_math(w_ref[...], g_ref[...], m_ref[...], v_ref[...])

    shapes = [jax.ShapeDtypeStruct(w.shape, F32) for w in ws]
    outs = pl.pallas_call(body, name="adamw_small", out_shape=shapes * 3)(*ws, *gs, *ms, *vs)
    return outs[:n], outs[n:2 * n], outs[2 * n:]


def kernel(x, norm_mix, w_in, b_gate, rel_bias, w_ret_out, w_att_out, w_out, norm_ffn, w_ffn_gate, w_ffn_up, w_ffn_down, norm_final, loss_target, m_norm_mix, m_w_in, m_b_gate, m_rel_bias, m_w_ret_out, m_w_att_out, m_w_out, m_norm_ffn, m_w_ffn_gate, m_w_ffn_up, m_w_ffn_down, m_norm_final, v_norm_mix, v_w_in, v_b_gate, v_rel_bias, v_w_ret_out, v_w_att_out, v_w_out, v_norm_ffn, v_w_ffn_gate, v_w_ffn_up, v_w_ffn_down, v_norm_final):
    me = _index(_place())
    n_rb = rel_bias.shape[-1]

    shard = dict(w_in_t=w_in[0].T, w_gate_t=w_ffn_gate[0].T, w_up_t=w_ffn_up[0].T, w_down=w_ffn_down[0],
                 w_ret=w_ret_out[0], w_out=w_out[0], w_att_t=w_att_out[0].T)
    shard = {nm: s.astype(BF16) for nm, s in shard.items()}
    loss, dx, slots, small_slots = _step(x, loss_target, norm_mix, b_gate, norm_ffn, norm_final, rel_bias[0], shard)
    loss = lax.psum(loss, ("x", "y", "c"))
    summed = {nm: _sum_slots(s, "sum_" + nm) for nm, s in slots.items()}
    small_sum = _sum_slots(small_slots, "sum_small")

    g = dict(
        w_in=summed["w_in_t"].T, w_ffn_gate=summed["w_gate_t"].T, w_ffn_up=summed["w_up_t"].T,
        w_ffn_down=summed["w_down"], w_ret_out=summed["w_ret"], w_out=summed["w_out"], w_att_out=summed["w_att_t"].T,
        norm_mix=small_sum[0:1], b_gate=jnp.concatenate([small_sum[1:2], small_sum[2:3]], axis=1),
        norm_ffn=small_sum[3:4], norm_final=small_sum[4:5],
        rel_bias=lax.dynamic_slice_in_dim(_bias_rows_bwd(small_sum[8:16]), me * n_rb, n_rb, axis=1),
    )
    w = dict(norm_mix=norm_mix, w_in=w_in, b_gate=b_gate, rel_bias=rel_bias, w_ret_out=w_ret_out, w_att_out=w_att_out,
             w_out=w_out, norm_ffn=norm_ffn, w_ffn_gate=w_ffn_gate, w_ffn_up=w_ffn_up, w_ffn_down=w_ffn_down,
             norm_final=norm_final)
    m = dict(norm_mix=m_norm_mix, w_in=m_w_in, b_gate=m_b_gate, rel_bias=m_rel_bias, w_ret_out=m_w_ret_out,
             w_att_out=m_w_att_out, w_out=m_w_out, norm_ffn=m_norm_ffn, w_ffn_gate=m_w_ffn_gate, w_ffn_up=m_w_ffn_up,
             w_ffn_down=m_w_ffn_down, norm_final=m_norm_final)
    v = dict(norm_mix=v_norm_mix, w_in=v_w_in, b_gate=v_b_gate, rel_bias=v_rel_bias, w_ret_out=v_w_ret_out,
             w_att_out=v_w_att_out, w_out=v_w_out, norm_ffn=v_norm_ffn, w_ffn_gate=v_w_ffn_gate, w_ffn_up=v_w_ffn_up,
             w_ffn_down=v_w_ffn_down, norm_final=v_norm_final)
    order = ("norm_mix", "w_in", "b_gate", "rel_bias", "w_ret_out", "w_att_out", "w_out", "norm_ffn",
             "w_ffn_gate", "w_ffn_up", "w_ffn_down", "norm_final")
    small_names = ("norm_mix", "b_gate", "rel_bias", "norm_ffn", "norm_final")

    def flat(a):
        return a[0] if a.ndim == 3 else a.reshape(-1, a.shape[-1])

    grad, delta, new_m, new_v = {}, {}, {}, {}
    for nm in order:
        if nm not in small_names:
            d, nmom, nvar = _adamw(flat(w[nm]), g[nm], flat(m[nm]), flat(v[nm]), "adamw_" + nm)
            grad[nm], delta[nm], new_m[nm], new_v[nm] = (a.reshape(w[nm].shape) for a in (g[nm], d, nmom, nvar))
    ds, nms, nvs = _adamw_small([flat(w[nm]) for nm in small_names], [g[nm] for nm in small_names],
                                [flat(m[nm]) for nm in small_names], [flat(v[nm]) for nm in small_names])
    for i, nm in enumerate(small_names):
        grad[nm], delta[nm], new_m[nm], new_v[nm] = (a.reshape(w[nm].shape) for a in (g[nm], ds[i], nms[i], nvs[i]))

    return (loss, dx, *[grad[nm] for nm in order], *[delta[nm] for nm in order],
            *[new_m[nm] for nm in order], *[new_v[nm] for nm in order])
```

```python
import numpy as np
import jax
import jax.numpy as jnp
from jax import lax
from jax.experimental import pallas as pl
from jax.experimental.pallas import tpu as pltpu

F32 = jnp.float32
BF16 = jnp.bfloat16
MESH = pl.DeviceIdType.MESH

D_MODEL = 1024
CHUNK = 64
RET_HEADS = 4
RET_KEY_DIM = 128
RET_VAL_DIM = 256
ATT_HEADS = 8
BAND_CHUNKS = 8
MAX_REL = 256
N_REL = CHUNK + MAX_REL
D_FF = 2816
N_IN = 6656
ROPE_BASE = 10000.0
EPS = 1e-6
NEG_INF = -1e30
C_RQ, C_RK, C_RV, C_RG, C_AQ, C_AK, C_AV, C_GL = 0, 512, 1024, 2048, 3072, 3584, 4096, 4608

ADAM_LR = 0.001
ADAM_B1 = 0.9
ADAM_B2 = 0.999
ADAM_EPS = 1e-08
ADAM_WD = 0.01
ADAM_STEP = 10

N_DEV = 8
LANES = 128
RET_TILE = 256
ATT_Q = 256
ATT_PAD = BAND_CHUNKS * CHUNK
ATT_WIN = ATT_PAD + ATT_Q
BIAS_LEN = 1024
VMEM_LIMIT = 48 * 1024 * 1024


def _params(*sem):
    return pltpu.CompilerParams(dimension_semantics=sem, vmem_limit_bytes=VMEM_LIMIT)


def _dot(a, b):
    return lax.dot_general(a, b, (((1,), (0,)), ((), ())), preferred_element_type=F32)


def _dot_nt(a, b):
    return lax.dot_general(a, b, (((1,), (1,)), ((), ())), preferred_element_type=F32)


def _dot_tn(a, b):
    return lax.dot_general(a, b, (((0,), (0,)), ((), ())), preferred_element_type=F32)


def _sigmoid(x):
    return 1.0 / (1.0 + jnp.exp(-x))


def _rms_bwd(x, g, dy):
    r = lax.rsqrt(jnp.mean(x * x, axis=-1, keepdims=True) + EPS)
    u = dy * g
    dx = r * u - x * (r * r * r) * jnp.mean(u * x, axis=-1, keepdims=True)
    return dx, dy * x * r


def _place():
    return lax.axis_index("x"), lax.axis_index("y"), lax.axis_index("c")


def _peer(k):
    x, y, c = _place()
    return ((1 - x) if k & 4 else x, (1 - y) if k & 2 else y, (1 - c) if k & 1 else c)


def _index(place):
    return 4 * place[0] + 2 * place[1] + place[2]


def _rows(ref, block, nrows):
    align = 16 if ref.dtype == BF16 else 8
    return ref.at[pl.ds(pl.multiple_of(block * nrows, align), nrows)]


class _Exchange:
    def __init__(self, arrays, scatter):
        self.arrays, self.scatter, self.n = list(arrays), scatter, len(arrays)

    def out_shape(self):
        if self.scatter:
            return [jax.ShapeDtypeStruct((N_DEV, a.shape[0] // N_DEV) + a.shape[1:], a.dtype) for a in self.arrays]
        return [jax.ShapeDtypeStruct((N_DEV * a.shape[0],) + a.shape[1:], a.dtype) for a in self.arrays]

    def scratch(self):
        return [pltpu.SemaphoreType.DMA((self.n, N_DEV - 1)), pltpu.SemaphoreType.DMA((self.n, N_DEV - 1)),
                pltpu.SemaphoreType.DMA((self.n,))]

    def _copies(self, ins, outs, sems):
        send_sems, recv_sems, local_sems = sems
        me = _index(_place())

        def src(w, to):
            return _rows(ins[w], to, ins[w].shape[0] // N_DEV) if self.scatter else ins[w]

        def dst(w, origin):
            return outs[w].at[origin] if self.scatter else _rows(outs[w], origin, ins[w].shape[0])

        def remote(w, k, to, origin):
            return pltpu.make_async_remote_copy(src_ref=src(w, to), dst_ref=dst(w, origin),
                                                send_sem=send_sems.at[w, k - 1], recv_sem=recv_sems.at[w, k - 1],
                                                device_id=_peer(k), device_id_type=MESH)

        pairs = [(w, k) for w in range(self.n) for k in range(1, N_DEV)]
        own = lambda: [pltpu.make_async_copy(src(w, me), dst(w, me), local_sems.at[w]) for w in range(self.n)]
        sent = lambda: [remote(w, k, _index(_peer(k)), me) for w, k in pairs]
        arriving = lambda: [remote(w, k, me, _index(_peer(k))) for w, k in pairs]
        return own, sent, arriving

    def start(self, ins, outs, sems):
        own, sent, _ = self._copies(ins, outs, sems)
        for cp in own() + sent():
            cp.start()

    def wait(self, ins, outs, sems):
        own, sent, arriving = self._copies(ins, outs, sems)
        for cp in arriving():
            cp.wait_recv()
        for cp in sent():
            cp.wait_send()
        for cp in own():
            cp.wait()


class _ChipGather:
    def __init__(self, arrays):
        self.arrays, self.n = list(arrays), len(arrays)

    def out_shape(self):
        return [jax.ShapeDtypeStruct((N_DEV * a.shape[0],) + a.shape[1:], a.dtype) for a in self.arrays]

    def scratch(self):
        return [pltpu.SemaphoreType.DMA((self.n, N_DEV - 1)), pltpu.SemaphoreType.DMA((self.n, N_DEV - 1)),
                pltpu.SemaphoreType.DMA((self.n,))]

    def _parts(self, ins, outs, sems):
        send_sems, recv_sems, local_sems = sems
        x, y, c = _place()
        me, sibling = (x, y, c), (x, y, 1 - c)
        chips = [(1 - x, y), (x, 1 - y), (1 - x, 1 - y)]

        def rows(w, place):
            return _rows(outs[w], _index(place), ins[w].shape[0])

        def copy(w, k, block, to, own=False):
            return pltpu.make_async_remote_copy(src_ref=ins[w] if own else rows(w, block), dst_ref=rows(w, block),
                                                send_sem=send_sems.at[w, k], recv_sem=recv_sems.at[w, k],
                                                device_id=to, device_id_type=MESH)

        def local(w):
            return pltpu.make_async_copy(ins[w], rows(w, me), local_sems.at[w])

        return me, sibling, chips, c, copy, local

    def start(self, ins, outs, sems):
        me, sibling, chips, c, copy, local = self._parts(ins, outs, sems)
        for w in range(self.n):
            local(w).start()
            copy(w, 0, me, sibling, own=True).start()
            for j, chip in enumerate(chips):
                copy(w, 1 + j, me, (*chip, c), own=True).start()

    def wait(self, ins, outs, sems):
        me, sibling, chips, c, copy, local = self._parts(ins, outs, sems)
        for w in range(self.n):
            for j, chip in enumerate(chips):
                copy(w, 1 + j, (*chip, c), me).wait_recv()
                copy(w, 4 + j, (*chip, c), sibling).start()
        for w in range(self.n):
            copy(w, 0, sibling, me).wait_recv()
            for j, chip in enumerate(chips):
                copy(w, 4 + j, (*chip, 1 - c), me).wait_recv()
            copy(w, 0, me, sibling, own=True).wait_send()
            for j, chip in enumerate(chips):
                copy(w, 1 + j, me, (*chip, c), own=True).wait_send()
                copy(w, 4 + j, (*chip, c), sibling).wait_send()
            local(w).wait()


def _call(body, *, name, grid, in_specs, out_specs, out_shape, scratch=(), semantics, args, exchange=None):
    if exchange is None:
        return pl.pallas_call(body, name=name, grid=grid, in_specs=in_specs, out_specs=out_specs, out_shape=out_shape,
                              scratch_shapes=list(scratch), compiler_params=_params(*semantics))(*args), None
    n_in, n_out, n_scr, nx = len(in_specs), len(out_specs), len(scratch), exchange.n

    def full_body(*refs):
        ins, refs = refs[:n_in], refs[n_in:]
        x_in, refs = refs[:nx], refs[nx:]
        outs, refs = refs[:n_out], refs[n_out:]
        x_out, refs = refs[:nx], refs[nx:]
        scr, sems = refs[:n_scr], refs[n_scr:]
        first, last = True, True
        for axis, size in enumerate(grid):
            first = jnp.logical_and(first, pl.program_id(axis) == 0)
            last = jnp.logical_and(last, pl.program_id(axis) == size - 1)
        if grid:
            pl.when(first)(lambda: exchange.start(x_in, x_out, sems))
        else:
            exchange.start(x_in, x_out, sems)
        body(*ins, *outs, *scr)
        if grid:
            pl.when(last)(lambda: exchange.wait(x_in, x_out, sems))
        else:
            exchange.wait(x_in, x_out, sems)

    hbm = pl.BlockSpec(memory_space=pltpu.HBM)
    res = pl.pallas_call(
        full_body, name=name, grid=grid,
        in_specs=list(in_specs) + [hbm] * nx, out_specs=list(out_specs) + [hbm] * nx,
        out_shape=list(out_shape) + exchange.out_shape(),
        scratch_shapes=list(scratch) + exchange.scratch(),
        compiler_params=_params(*(["arbitrary"] * len(grid))),
    )(*args, *exchange.arrays)
    return res[:n_out], res[n_out:]


def _alone(exchange, name):
    return _call(lambda: None, name=name, grid=(), in_specs=[], out_specs=[], out_shape=[], semantics=(),
                 args=(), exchange=exchange)[1]


def _mm(a, b, *, ta=False, tb=False, out_dtype, tm, tn, tk, name, exchange=None):
    m, k = (a.shape[1], a.shape[0]) if ta else a.shape
    n = b.shape[0] if tb else b.shape[1]
    assert k == (b.shape[1] if tb else b.shape[0])
    tm, tn, tk = min(tm, m), min(tn, n), min(tk, k)
    assert m % tm == 0 and n % tn == 0 and k % tk == 0, (name, m, n, k)
    nk = k // tk
    dims = (((0 if ta else 1,), (1 if tb else 0,)), ((), ()))

    def body(a_ref, b_ref, o_ref, *acc):
        prod = lax.dot_general(a_ref[...].astype(BF16), b_ref[...].astype(BF16), dims, preferred_element_type=F32)
        if nk == 1:
            o_ref[...] = prod.astype(o_ref.dtype)
            return
        acc_ref, kk = acc[0], pl.program_id(2)

        @pl.when(kk == 0)
        def _():
            acc_ref[...] = prod

        @pl.when((kk > 0) & (kk < nk - 1))
        def _():
            acc_ref[...] += prod

        @pl.when(kk == nk - 1)
        def _():
            o_ref[...] = (acc_ref[...] + prod).astype(o_ref.dtype)

    a_spec = (pl.BlockSpec((tk, tm), lambda i, j, kk: (kk, i)) if ta
              else pl.BlockSpec((tm, tk), lambda i, j, kk: (i, kk)))
    b_spec = (pl.BlockSpec((tn, tk), lambda i, j, kk: (j, kk)) if tb
              else pl.BlockSpec((tk, tn), lambda i, j, kk: (kk, j)))
    (out,), moved = _call(
        body, name=name, grid=(m // tm, n // tn, nk),
        in_specs=[a_spec, b_spec],
        out_specs=[pl.BlockSpec((tm, tn), lambda i, j, kk: (i, j))],
        out_shape=[jax.ShapeDtypeStruct((m, n), out_dtype)],
        scratch=[pltpu.VMEM((tm, tn), F32)] if nk > 1 else [],
        semantics=("parallel", "parallel", "arbitrary"), args=(a, b), exchange=exchange)
    return out if exchange is None else (out, moved)


def _mm_pieces(pieces, b, *, ta, out_dtype, tm, tn, tk, name, exchange=None):
    rows, n = pieces[0].shape[0], b.shape[1]
    step = tm if ta else tk
    assert all(p.shape[0] == rows and p.shape[1] % step == 0 for p in pieces), name
    edges = [int(e) for e in np.cumsum([0] + [p.shape[1] // step for p in pieces])]
    total = edges[-1] * step
    m, k = (total, rows) if ta else (rows, total)
    assert b.shape[0] == k and m % tm == 0 and n % tn == 0 and k % tk == 0, name
    nk, npieces = k // tk, len(pieces)
    dims = (((0 if ta else 1,), (0,)), ((), ()))

    def body(*refs):
        a_refs, (b_ref, o_ref, acc_ref) = refs[:npieces], refs[npieces:]
        kk = pl.program_id(2)
        pos = pl.program_id(0) if ta else kk

        @pl.when(kk == 0)
        def _():
            acc_ref[...] = jnp.zeros_like(acc_ref)

        for p, a_ref in enumerate(a_refs):
            @pl.when((pos >= edges[p]) & (pos < edges[p + 1]))
            def _(a_ref=a_ref):
                acc_ref[...] += lax.dot_general(a_ref[...], b_ref[...], dims, preferred_element_type=F32)

        @pl.when(kk == nk - 1)
        def _():
            o_ref[...] = acc_ref[...].astype(o_ref.dtype)

    def a_spec(p):
        lo, last = edges[p], edges[p + 1] - edges[p] - 1
        if ta:
            return pl.BlockSpec((tk, tm), lambda i, j, kk: (kk, jnp.clip(i - lo, 0, last)))
        return pl.BlockSpec((tm, tk), lambda i, j, kk: (i, jnp.clip(kk - lo, 0, last)))

    (out,), moved = _call(
        body, name=name, grid=(m // tm, n // tn, nk),
        in_specs=[a_spec(p) for p in range(npieces)] + [pl.BlockSpec((tk, tn), lambda i, j, kk: (kk, j))],
        out_specs=[pl.BlockSpec((tm, tn), lambda i, j, kk: (i, j))],
        out_shape=[jax.ShapeDtypeStruct((m, n), out_dtype)],
        scratch=[pltpu.VMEM((tm, tn), F32)],
        semantics=("parallel", "parallel", "arbitrary"), args=(*pieces, b), exchange=exchange)
    return out if exchange is None else (out, moved)


def _rms_fwd(x2, g):
    t = x2.shape[0]
    tm = min(512, t)

    def body(x_ref, g_ref, o_ref):
        x = x_ref[...]
        r = lax.rsqrt(jnp.mean(x * x, axis=-1, keepdims=True) + EPS)
        o_ref[...] = (x * r * g_ref[...]).astype(o_ref.dtype)

    return pl.pallas_call(
        body, name="rms_in_fwd", grid=(t // tm,),
        in_specs=[pl.BlockSpec((tm, D_MODEL), lambda i: (i, 0)), pl.BlockSpec((1, D_MODEL), lambda i: (0, 0))],
        out_specs=pl.BlockSpec((tm, D_MODEL), lambda i: (i, 0)),
        out_shape=jax.ShapeDtypeStruct((t, D_MODEL), BF16),
        compiler_params=_params("parallel"),
    )(x2, g)


def _decay(lg):
    row = lax.broadcasted_iota(jnp.int32, (RET_TILE, RET_TILE), 0)
    col = lax.broadcasted_iota(jnp.int32, (RET_TILE, RET_TILE), 1)
    within = jnp.exp(lg * jnp.abs(row - col).astype(F32))
    inside = jnp.where((col >> 6) <= (row >> 6), within, 0.0)
    pos = lax.broadcasted_iota(jnp.int32, (RET_TILE, 1), 0).astype(F32)
    q_dec = jnp.exp(lg * (pos + 1.0))
    k_dec = jnp.exp(lg * (RET_TILE - 1.0 - pos))
    tile_dec = jnp.exp(lg * float(RET_TILE))
    return inside, q_dec, k_dec, tile_dec


def _scaled(a_bf16, dec):
    return (a_bf16.astype(F32) * dec).astype(BF16)


def _ret_fwd(proj, cs, sn, lg_arr, batch, seq, exchange):
    t = batch * seq
    nt = seq // RET_TILE

    def body(q_ref, k_ref, v_ref, rg_ref, cs_ref, sn_ref, lg_ref, gro_ref, o_ref, qr_ref, kr_ref):
        lg = lg_ref[:, 0:1]
        cs_t, sn_t = cs_ref[...], sn_ref[...]
        q = q_ref[...].astype(F32)
        k = k_ref[...].astype(F32)
        qr_ref[...] = (q * cs_t + pltpu.roll(q, 64, 1) * sn_t).astype(BF16)
        kr_ref[...] = ((k * cs_t + pltpu.roll(k, 64, 1) * sn_t) * (RET_KEY_DIM ** -0.5)).astype(BF16)
        inside, q_dec, k_dec, tile_dec = _decay(lg)
        state = jnp.zeros((RET_KEY_DIM, RET_VAL_DIM), F32)
        for i in range(nt):
            rows = slice(i * RET_TILE, (i + 1) * RET_TILE)
            qi, ki, vi = qr_ref[rows, :], kr_ref[rows, :], v_ref[rows, :]
            acc = _dot((_dot_nt(qi, ki) * inside).astype(BF16), vi)
            if i > 0:
                acc = acc + _dot(_scaled(qi, q_dec), state.astype(BF16))
            if i < nt - 1:
                state = state * tile_dec + _dot_tn(_scaled(ki, k_dec), vi)
            o_ref[rows, :] = acc
            xc = acc - jnp.mean(acc, axis=-1, keepdims=True)
            nrm = xc * lax.rsqrt(jnp.mean(xc * xc, axis=-1, keepdims=True) + EPS)
            rg = rg_ref[rows, :].astype(F32)
            gro_ref[rows, :] = (rg * _sigmoid(rg) * nrm).astype(BF16)

    def col(base, width):
        return lambda b, h: (b, base // width + h)

    return _call(
        body, name="ret_fwd", grid=(batch, RET_HEADS),
        in_specs=[pl.BlockSpec((seq, RET_KEY_DIM), col(C_RQ, RET_KEY_DIM)),
                  pl.BlockSpec((seq, RET_KEY_DIM), col(C_RK, RET_KEY_DIM)),
                  pl.BlockSpec((seq, RET_VAL_DIM), col(C_RV, RET_VAL_DIM)),
                  pl.BlockSpec((seq, RET_VAL_DIM), col(C_RG, RET_VAL_DIM)),
                  pl.BlockSpec((seq, RET_KEY_DIM), lambda b, h: (0, 0)),
                  pl.BlockSpec((seq, RET_KEY_DIM), lambda b, h: (0, 0)),
                  pl.BlockSpec((None, 1, LANES), lambda b, h: (h, 0, 0))],
        out_specs=[pl.BlockSpec((seq, RET_VAL_DIM), lambda b, h: (b, h)),
                   pl.BlockSpec((seq, RET_VAL_DIM), lambda b, h: (b, h)),
                   pl.BlockSpec((seq, RET_KEY_DIM), lambda b, h: (b, h)),
                   pl.BlockSpec((seq, RET_KEY_DIM), lambda b, h: (b, h))],
        out_shape=[jax.ShapeDtypeStruct((t, RET_HEADS * RET_VAL_DIM), BF16),
                   jax.ShapeDtypeStruct((t, RET_HEADS * RET_VAL_DIM), F32),
                   jax.ShapeDtypeStruct((t, RET_HEADS * RET_KEY_DIM), BF16),
                   jax.ShapeDtypeStruct((t, RET_HEADS * RET_KEY_DIM), BF16)],
        semantics=("parallel", "parallel"), args=(proj, proj, proj, proj, cs, sn, lg_arr), exchange=exchange)


def _att_bias(w_ref, bias_ref):
    n_i = lax.broadcasted_iota(jnp.int32, (ATT_Q, BIAS_LEN), 0)
    qc = lax.broadcasted_iota(jnp.int32, (ATT_Q, ATT_WIN), 0) >> 6
    kc = lax.broadcasted_iota(jnp.int32, (ATT_Q, ATT_WIN), 1) >> 6
    dc = qc + BAND_CHUNKS - kc
    band = (dc >= 0) & (dc <= BAND_CHUNKS)
    for e in range(2):
        xw = jnp.broadcast_to(w_ref[e:e + 1, :], (ATT_Q, BIAS_LEN))
        for bit in range(8):
            xw = jnp.where(((n_i >> bit) & 1) == 1, pltpu.roll(xw, 1 << bit, 1), xw)
        bias_ref[e] = jnp.where(band, xw[:, BIAS_LEN - ATT_WIN:], NEG_INF)


def _att_specs(batch, seq):
    ni = seq // ATT_Q
    q_spec = pl.BlockSpec((ATT_Q, LANES), lambda hp, b, i: (b * ni + i, C_AQ // LANES + hp))
    k_spec = pl.BlockSpec((seq, LANES), lambda hp, b, i: (b, C_AK // LANES + hp))
    v_spec = pl.BlockSpec((seq, LANES), lambda hp, b, i: (b, C_AV // LANES + hp))
    w_spec = pl.BlockSpec((None, 2, BIAS_LEN), lambda hp, b, i: (hp, 0, 0))
    pad = pltpu.VMEM((seq + ATT_PAD, LANES), BF16)
    return ni, q_spec, k_spec, v_spec, w_spec, pad


def _att_pad(src_ref, pad_ref):
    pad_ref[:ATT_PAD, :] = jnp.zeros((ATT_PAD, LANES), BF16)
    pad_ref[ATT_PAD:, :] = src_ref[...]


def _att_exp(q2, k2, bias, sel, i):
    qm = jnp.where(sel, q2, jnp.zeros_like(q2)) * 0.125
    s = _dot_nt(qm, k2) + bias

    def before_start(s):
        key = lax.broadcasted_iota(jnp.int32, (ATT_Q, ATT_WIN), 1) + (i * ATT_Q - ATT_PAD)
        return jnp.where(key >= 0, s, NEG_INF)

    s = lax.cond(i * ATT_Q < ATT_PAD, before_start, lambda s: s, s)
    e = jnp.exp(s - jnp.max(s, axis=-1, keepdims=True))
    return qm, e, 1.0 / jnp.sum(e, axis=-1, keepdims=True)


def _att_fwd(proj, wvec, batch, seq, exchange):
    ni, q_spec, k_spec, v_spec, w_spec, pad = _att_specs(batch, seq)

    def body(q_ref, k_ref, v_ref, w_ref, o_ref, bias_ref, kp_ref, vp_ref):
        b, i = pl.program_id(1), pl.program_id(2)

        @pl.when((b == 0) & (i == 0))
        def _():
            _att_bias(w_ref, bias_ref)

        @pl.when(i == 0)
        def _():
            _att_pad(k_ref, kp_ref)
            _att_pad(v_ref, vp_ref)

        win = pl.ds(pl.multiple_of(i * ATT_Q, ATT_Q), ATT_WIN)
        k2, v2, q2 = kp_ref[win, :], vp_ref[win, :], q_ref[...]
        lo = lax.broadcasted_iota(jnp.int32, (1, LANES), 1) < 64
        out = jnp.zeros((ATT_Q, LANES), F32)
        for e in range(2):
            sel = lo if e == 0 else jnp.logical_not(lo)
            _, ex, rsum = _att_exp(q2, k2, bias_ref[e], sel, i)
            out = out + _dot(ex.astype(BF16), jnp.where(sel, v2, jnp.zeros_like(v2))) * rsum
        o_ref[...] = out.astype(BF16)

    return _call(
        body, name="att_fwd", grid=(ATT_HEADS // 2, batch, ni),
        in_specs=[q_spec, k_spec, v_spec, w_spec],
        out_specs=[pl.BlockSpec((ATT_Q, LANES), lambda hp, b, i: (b * ni + i, hp))],
        out_shape=[jax.ShapeDtypeStruct((batch * seq, ATT_HEADS * 64), BF16)],
        scratch=[pltpu.VMEM((2, ATT_Q, ATT_WIN), F32), pad, pad],
        semantics=("arbitrary", "arbitrary", "arbitrary"), args=(proj, proj, proj, wvec), exchange=exchange)


def _mix_fwd(gro, ao, proj, b_gate, w_ret, w_att_t):
    t = gro.shape[0]
    tm, tn = min(256, t), 512

    def body(gro_ref, ao_ref, glr_ref, gla_ref, br_ref, ba_ref, wr_ref, wa_ref, z_ref, yr_ref, ya_ref):
        yr = _dot(gro_ref[...], wr_ref[...])
        ya = _dot_nt(ao_ref[...], wa_ref[...])
        gr = _sigmoid(glr_ref[...].astype(F32) + br_ref[...])
        ga = _sigmoid(gla_ref[...].astype(F32) + ba_ref[...])
        z_ref[...] = (gr * yr + ga * ya).astype(BF16)
        yr_ref[...] = yr.astype(BF16)
        ya_ref[...] = ya.astype(BF16)

    nb = D_MODEL // tn
    out = pl.BlockSpec((tm, tn), lambda i, j: (i, j))
    return pl.pallas_call(
        body, name="mix_fwd", grid=(t // tm, nb),
        in_specs=[pl.BlockSpec((tm, D_MODEL), lambda i, j: (i, 0)),
                  pl.BlockSpec((tm, 512), lambda i, j: (i, 0)),
                  pl.BlockSpec((tm, tn), lambda i, j: (i, C_GL // tn + j)),
                  pl.BlockSpec((tm, tn), lambda i, j: (i, C_GL // tn + nb + j)),
                  pl.BlockSpec((1, tn), lambda i, j: (0, j)),
                  pl.BlockSpec((1, tn), lambda i, j: (0, nb + j)),
                  pl.BlockSpec((D_MODEL, tn), lambda i, j: (0, j)),
                  pl.BlockSpec((tn, 512), lambda i, j: (j, 0))],
        out_specs=[out, out, out],
        out_shape=[jax.ShapeDtypeStruct((t, D_MODEL), BF16)] * 3,
        compiler_params=_params("parallel", "parallel"),
    )(gro, ao, proj, proj, b_gate, b_gate, w_ret, w_att_t)


def _out_fwd(z, x2, w_out, g2):
    t = z.shape[0]
    tm = min(256, t)

    def body(z_ref, x_ref, w_ref, g_ref, h_ref, hn_ref):
        h = x_ref[...] + _dot(z_ref[...], w_ref[...])
        h_ref[...] = h
        r = lax.rsqrt(jnp.mean(h * h, axis=-1, keepdims=True) + EPS)
        hn_ref[...] = (h * r * g_ref[...]).astype(BF16)

    row = pl.BlockSpec((tm, D_MODEL), lambda i: (i, 0))
    return pl.pallas_call(
        body, name="out_fwd", grid=(t // tm,),
        in_specs=[row, row, pl.BlockSpec((D_MODEL, D_MODEL), lambda i: (0, 0)),
                  pl.BlockSpec((1, D_MODEL), lambda i: (0, 0))],
        out_specs=[row, row],
        out_shape=[jax.ShapeDtypeStruct((t, D_MODEL), F32), jax.ShapeDtypeStruct((t, D_MODEL), BF16)],
        compiler_params=_params("parallel"),
    )(z, x2, w_out, g2)


def _ffn_up(hn, wg_t, wu_t):
    t = hn.shape[0]
    tm, tn = min(512, t), D_FF // 2

    def body(h_ref, wg_ref, wu_ref, g_ref, u_ref, a_ref):
        g = _dot_nt(h_ref[...], wg_ref[...])
        u = _dot_nt(h_ref[...], wu_ref[...])
        g_ref[...] = g.astype(BF16)
        u_ref[...] = u.astype(BF16)
        a_ref[...] = (g * _sigmoid(g) * u).astype(BF16)

    w_spec = pl.BlockSpec((tn, D_MODEL), lambda j, i: (j, 0))
    out = pl.BlockSpec((tm, tn), lambda j, i: (i, j))
    return pl.pallas_call(
        body, name="ffn_up", grid=(D_FF // tn, t // tm),
        in_specs=[pl.BlockSpec((tm, D_MODEL), lambda j, i: (i, 0)), w_spec, w_spec],
        out_specs=[out, out, out],
        out_shape=[jax.ShapeDtypeStruct((t, D_FF), BF16)] * 3,
        compiler_params=_params("parallel", "parallel"),
    )(hn, wg_t, wu_t)


def _ffn_down_loss(a, h1, tgt, w_down, g3):
    t = a.shape[0]
    tm = min(512, t)

    def body(a_ref, h_ref, t_ref, w_ref, g_ref, dh_ref, dhb_ref, loss_ref, dg_ref):
        @pl.when(pl.program_id(0) == 0)
        def _():
            loss_ref[...] = jnp.zeros_like(loss_ref)
            dg_ref[...] = jnp.zeros_like(dg_ref)

        g = g_ref[...]
        h2 = h_ref[...] + _dot(a_ref[...], w_ref[...])
        r = lax.rsqrt(jnp.mean(h2 * h2, axis=-1, keepdims=True) + EPS)
        err = h2 * r * g - t_ref[...]
        loss_ref[...] += jnp.sum(err * err) * (0.5 / D_MODEL)
        dy = err * (1.0 / D_MODEL)
        dh, dg_rows = _rms_bwd(h2, g, dy)
        dg_ref[...] += jnp.sum(dg_rows, axis=0, keepdims=True)
        dh_ref[...] = dh
        dhb_ref[...] = dh.astype(BF16)

    row = pl.BlockSpec((tm, D_MODEL), lambda i: (i, 0))
    vec = pl.BlockSpec((1, D_MODEL), lambda i: (0, 0))
    return pl.pallas_call(
        body, name="ffn_down_loss", grid=(t // tm,),
        in_specs=[pl.BlockSpec((tm, D_FF), lambda i: (i, 0)), row, row,
                  pl.BlockSpec((D_FF, D_MODEL), lambda i: (0, 0)), vec],
        out_specs=[row, row, pl.BlockSpec((1, LANES), lambda i: (0, 0)), vec],
        out_shape=[jax.ShapeDtypeStruct((t, D_MODEL), F32), jax.ShapeDtypeStruct((t, D_MODEL), BF16),
                   jax.ShapeDtypeStruct((1, LANES), F32), jax.ShapeDtypeStruct((1, D_MODEL), F32)],
        compiler_params=_params("arbitrary"),
    )(a, h1, tgt, w_down, g3)


def _ffn_bwd_act(dh2b, w_down, g_act, u_act, exchange):
    t = dh2b.shape[0]
    tm, tn = min(512, t), D_FF // 2

    def body(d_ref, w_ref, g_ref, u_ref, dg_ref, du_ref):
        da = _dot_nt(d_ref[...], w_ref[...])
        g = g_ref[...].astype(F32)
        u = u_ref[...].astype(F32)
        sg = _sigmoid(g)
        dg_ref[...] = (da * u * sg * (1.0 + g * (1.0 - sg))).astype(BF16)
        du_ref[...] = (da * g * sg).astype(BF16)

    blk = pl.BlockSpec((tm, tn), lambda j, i: (i, j))
    return _call(
        body, name="ffn_bwd_act", grid=(D_FF // tn, t // tm),
        in_specs=[pl.BlockSpec((tm, D_MODEL), lambda j, i: (i, 0)),
                  pl.BlockSpec((tn, D_MODEL), lambda j, i: (j, 0)), blk, blk],
        out_specs=[blk, blk],
        out_shape=[jax.ShapeDtypeStruct((t, D_FF), BF16)] * 2,
        semantics=("parallel", "parallel"), args=(dh2b, w_down, g_act, u_act), exchange=exchange)


def _ffn_bwd_in(dg, du, wg_t, wu_t, h1, dh2, g2, exchange):
    t = dg.shape[0]
    tm, tk = min(512, t), D_FF // 2
    nk = D_FF // tk

    def body(dg_ref, du_ref, wg_ref, wu_ref, h_ref, d2_ref, g_ref, dh_ref, dhb_ref, gn_ref, acc_ref):
        i, kk = pl.program_id(0), pl.program_id(1)

        @pl.when((i == 0) & (kk == 0))
        def _():
            gn_ref[...] = jnp.zeros_like(gn_ref)

        @pl.when(kk == 0)
        def _():
            acc_ref[...] = jnp.zeros_like(acc_ref)

        acc_ref[...] += _dot(dg_ref[...], wg_ref[...]) + _dot(du_ref[...], wu_ref[...])

        @pl.when(kk == nk - 1)
        def _():
            dx, dg_rows = _rms_bwd(h_ref[...], g_ref[...], acc_ref[...])
            dh = d2_ref[...] + dx
            dh_ref[...] = dh
            dhb_ref[...] = dh.astype(BF16)
            gn_ref[...] += jnp.sum(dg_rows, axis=0, keepdims=True)

    act = pl.BlockSpec((tm, tk), lambda i, kk: (i, kk))
    wsp = pl.BlockSpec((tk, D_MODEL), lambda i, kk: (kk, 0))
    row = pl.BlockSpec((tm, D_MODEL), lambda i, kk: (i, 0))
    vec = pl.BlockSpec((1, D_MODEL), lambda i, kk: (0, 0))
    return _call(
        body, name="ffn_bwd_in", grid=(t // tm, nk),
        in_specs=[act, act, wsp, wsp, row, row, vec],
        out_specs=[row, row, vec],
        out_shape=[jax.ShapeDtypeStruct((t, D_MODEL), F32), jax.ShapeDtypeStruct((t, D_MODEL), BF16),
                   jax.ShapeDtypeStruct((1, D_MODEL), F32)],
        scratch=[pltpu.VMEM((tm, D_MODEL), F32)],
        semantics=("arbitrary", "arbitrary"), args=(dg, du, wg_t, wu_t, h1, dh2, g2), exchange=exchange)


def _mix_bwd(dh1b, w_out, proj, b_gate, y_ret, y_att, exchange):
    t = dh1b.shape[0]
    tm, tn = min(256, t), 512
    nb = D_MODEL // tn

    def body(d_ref, w_ref, glr_ref, gla_ref, br_ref, ba_ref, yr_ref, ya_ref,
             dyr_ref, dya_ref, dglr_ref, dgla_ref, dbr_ref, dba_ref):
        @pl.when(pl.program_id(1) == 0)
        def _():
            dbr_ref[...] = jnp.zeros_like(dbr_ref)
            dba_ref[...] = jnp.zeros_like(dba_ref)

        dz = _dot_nt(d_ref[...], w_ref[...])
        gr = _sigmoid(glr_ref[...].astype(F32) + br_ref[...])
        ga = _sigmoid(gla_ref[...].astype(F32) + ba_ref[...])
        dyr_ref[...] = (dz * gr).astype(BF16)
        dya_ref[...] = (dz * ga).astype(BF16)
        dglr = dz * yr_ref[...].astype(F32) * gr * (1.0 - gr)
        dgla = dz * ya_ref[...].astype(F32) * ga * (1.0 - ga)
        dglr_ref[...] = dglr.astype(BF16)
        dgla_ref[...] = dgla.astype(BF16)
        dbr_ref[...] += jnp.sum(dglr, axis=0, keepdims=True)
        dba_ref[...] += jnp.sum(dgla, axis=0, keepdims=True)

    blk = pl.BlockSpec((tm, tn), lambda j, i: (i, j))
    vec = pl.BlockSpec((1, tn), lambda j, i: (0, j))
    return _call(
        body, name="mix_bwd", grid=(nb, t // tm),
        in_specs=[pl.BlockSpec((tm, D_MODEL), lambda j, i: (i, 0)),
                  pl.BlockSpec((tn, D_MODEL), lambda j, i: (j, 0)),
                  pl.BlockSpec((tm, tn), lambda j, i: (i, C_GL // tn + j)),
                  pl.BlockSpec((tm, tn), lambda j, i: (i, C_GL // tn + nb + j)),
                  vec, pl.BlockSpec((1, tn), lambda j, i: (0, nb + j)), blk, blk],
        out_specs=[blk, blk, blk, blk, vec, vec],
        out_shape=[jax.ShapeDtypeStruct((t, D_MODEL), BF16)] * 4 + [jax.ShapeDtypeStruct((1, D_MODEL), F32)] * 2,
        semantics=("arbitrary", "arbitrary"), args=(dh1b, w_out, proj, proj, b_gate, b_gate, y_ret, y_att),
        exchange=exchange)


def _ret_bwd(dgro, proj, o_ret, qr, kr, cs, sn, lg_arr, batch, seq, exchange):
    t = batch * seq
    nt = seq // RET_TILE

    def body(dgro_ref, rg_ref, o_ref, qr_ref, kr_ref, v_ref, cs_ref, sn_ref, lg_ref,
             dq_ref, dk_ref, dv_ref, drg_ref, do_ref, st_ref):
        lg = lg_ref[:, 0:1]
        inside, q_dec, k_dec, tile_dec = _decay(lg)

        state = jnp.zeros((RET_KEY_DIM, RET_VAL_DIM), F32)
        for i in range(nt - 1):
            rows = slice(i * RET_TILE, (i + 1) * RET_TILE)
            state = state * tile_dec + _dot_tn(_scaled(kr_ref[rows, :], k_dec), v_ref[rows, :])
            st_ref[i + 1] = state.astype(BF16)

        for i in range(nt):
            rows = slice(i * RET_TILE, (i + 1) * RET_TILE)
            o = o_ref[rows, :]
            xc = o - jnp.mean(o, axis=-1, keepdims=True)
            rs = lax.rsqrt(jnp.mean(xc * xc, axis=-1, keepdims=True) + EPS)
            nrm = xc * rs
            rg = rg_ref[rows, :].astype(F32)
            sg = _sigmoid(rg)
            dg = dgro_ref[rows, :].astype(F32)
            drg_ref[rows, :] = (dg * nrm * sg * (1.0 + rg * (1.0 - sg))).astype(BF16)
            dn = dg * rg * sg
            do = rs * (dn - jnp.mean(dn, axis=-1, keepdims=True)
                       - nrm * jnp.mean(dn * nrm, axis=-1, keepdims=True))
            do_ref[rows, :] = do.astype(BF16)

        dstate = jnp.zeros((RET_KEY_DIM, RET_VAL_DIM), F32)
        for i in reversed(range(nt)):
            rows = slice(i * RET_TILE, (i + 1) * RET_TILE)
            qi, ki, vi, doi = qr_ref[rows, :], kr_ref[rows, :], v_ref[rows, :], do_ref[rows, :]
            p = (_dot_nt(qi, ki) * inside).astype(BF16)
            dp = (_dot_nt(doi, vi) * inside).astype(BF16)
            dq = _dot(dp, ki)
            dk = _dot_tn(dp, qi)
            dv = _dot_tn(p, doi)
            if i > 0:
                dq = dq + _dot_nt(doi, st_ref[i]) * q_dec
            if i < nt - 1:
                dsb = dstate.astype(BF16)
                dk = dk + _dot_nt(vi, dsb) * k_dec
                dv = dv + _dot(_scaled(ki, k_dec), dsb)
            if i > 0:
                dstate = dstate * tile_dec + _dot_tn(_scaled(qi, q_dec), doi)
            dq_ref[rows, :] = (dq * cs_ref[rows, :] - pltpu.roll(dq, 64, 1) * sn_ref[rows, :]).astype(BF16)
            dk = (dk * cs_ref[rows, :] - pltpu.roll(dk, 64, 1) * sn_ref[rows, :]) * (RET_KEY_DIM ** -0.5)
            dk_ref[rows, :] = dk.astype(BF16)
            dv_ref[rows, :] = dv.astype(BF16)

    key = pl.BlockSpec((seq, RET_KEY_DIM), lambda b, h: (b, h))
    val = pl.BlockSpec((seq, RET_VAL_DIM), lambda b, h: (b, h))
    tab = pl.BlockSpec((seq, RET_KEY_DIM), lambda b, h: (0, 0))
    return _call(
        body, name="ret_bwd", grid=(batch, RET_HEADS),
        in_specs=[val, pl.BlockSpec((seq, RET_VAL_DIM), lambda b, h: (b, C_RG // RET_VAL_DIM + h)), val, key, key,
                  pl.BlockSpec((seq, RET_VAL_DIM), lambda b, h: (b, C_RV // RET_VAL_DIM + h)), tab, tab,
                  pl.BlockSpec((None, 1, LANES), lambda b, h: (h, 0, 0))],
        out_specs=[key, key, val, val],
        out_shape=[jax.ShapeDtypeStruct((t, RET_HEADS * RET_KEY_DIM), BF16)] * 2
                  + [jax.ShapeDtypeStruct((t, RET_HEADS * RET_VAL_DIM), BF16)] * 2,
        scratch=[pltpu.VMEM((seq, RET_VAL_DIM), BF16), pltpu.VMEM((nt, RET_KEY_DIM, RET_VAL_DIM), BF16)],
        semantics=("parallel", "parallel"), args=(dgro, proj, o_ret, qr, kr, proj, cs, sn, lg_arr),
        exchange=exchange)


def _att_bwd(proj, wvec, dao, batch, seq, exchange):
    ni, q_spec, k_spec, v_spec, w_spec, pad = _att_specs(batch, seq)
    t = batch * seq

    def body(q_ref, k_ref, v_ref, w_ref, do_ref, dq_ref, dk_ref, dv_ref, dw_ref,
             bias_ref, dbias_ref, dk_acc, dv_acc, kp_ref, vp_ref):
        b, i = pl.program_id(1), pl.program_id(2)

        @pl.when((b == 0) & (i == 0))
        def _():
            _att_bias(w_ref, bias_ref)
            dbias_ref[...] = jnp.zeros_like(dbias_ref)

        @pl.when(i == 0)
        def _():
            _att_pad(k_ref, kp_ref)
            _att_pad(v_ref, vp_ref)
            dk_acc[...] = jnp.zeros_like(dk_acc)
            dv_acc[...] = jnp.zeros_like(dv_acc)

        win = pl.ds(pl.multiple_of(i * ATT_Q, ATT_Q), ATT_WIN)
        k2, v2, q2, do2 = kp_ref[win, :], vp_ref[win, :], q_ref[...], do_ref[...]
        lo = lax.broadcasted_iota(jnp.int32, (1, LANES), 1) < 64
        dq = jnp.zeros((ATT_Q, LANES), F32)
        dk = jnp.zeros((ATT_WIN, LANES), F32)
        dv = jnp.zeros((ATT_WIN, LANES), F32)
        for e in range(2):
            sel = lo if e == 0 else jnp.logical_not(lo)
            qm, ex, rsum = _att_exp(q2, k2, bias_ref[e], sel, i)
            dom = jnp.where(sel, do2, jnp.zeros_like(do2))
            dp = _dot_nt(dom, v2)
            mean = jnp.sum(dp * ex, axis=-1, keepdims=True) * rsum
            ds = ex * ((dp - mean) * rsum)
            dbias_ref[e] += ds
            dsb = ds.astype(BF16)
            dq = dq + _dot(dsb, jnp.where(sel, k2, jnp.zeros_like(k2)))
            dk = dk + _dot_tn(dsb, qm)
            dv = dv + _dot_tn(ex.astype(BF16), (dom.astype(F32) * rsum).astype(BF16))
        dq_ref[...] = (dq * 0.125).astype(BF16)
        dk_acc[win, :] += dk
        dv_acc[win, :] += dv

        @pl.when(i == ni - 1)
        def _():
            dk_ref[...] = dk_acc[ATT_PAD:, :].astype(BF16)
            dv_ref[...] = dv_acc[ATT_PAD:, :].astype(BF16)

        @pl.when((b == batch - 1) & (i == ni - 1))
        def _():
            n_i = lax.broadcasted_iota(jnp.int32, (ATT_Q, BIAS_LEN), 0)
            for e in range(2):
                xw = jnp.concatenate([jnp.zeros((ATT_Q, BIAS_LEN - ATT_WIN), F32), dbias_ref[e]], axis=1)
                for bit in range(8):
                    xw = jnp.where(((n_i >> bit) & 1) == 1, pltpu.roll(xw, BIAS_LEN - (1 << bit), 1), xw)
                dw_ref[e:e + 1, :] = jnp.sum(xw, axis=0, keepdims=True)

    seq_blk = pl.BlockSpec((seq, LANES), lambda hp, b, i: (b, hp))
    q_out = pl.BlockSpec((ATT_Q, LANES), lambda hp, b, i: (b * ni + i, hp))
    return _call(
        body, name="att_bwd", grid=(ATT_HEADS // 2, batch, ni),
        in_specs=[q_spec, k_spec, v_spec, w_spec, q_out],
        out_specs=[q_out, seq_blk, seq_blk, w_spec],
        out_shape=[jax.ShapeDtypeStruct((t, 512), BF16)] * 3
                  + [jax.ShapeDtypeStruct((ATT_HEADS // 2, 2, BIAS_LEN), F32)],
        scratch=[pltpu.VMEM((2, ATT_Q, ATT_WIN), F32), pltpu.VMEM((2, ATT_Q, ATT_WIN), F32),
                 pltpu.VMEM((seq + ATT_PAD, LANES), F32), pltpu.VMEM((seq + ATT_PAD, LANES), F32), pad, pad],
        semantics=("arbitrary", "arbitrary", "arbitrary"), args=(proj, proj, proj, wvec, dao), exchange=exchange)


def _rms_in_bwd(x2, dxn, dh1, g1):
    t = x2.shape[0]
    tm = min(512, t)

    def body(x_ref, d_ref, h_ref, g_ref, dx_ref, dg_ref):
        @pl.when(pl.program_id(0) == 0)
        def _():
            dg_ref[...] = jnp.zeros_like(dg_ref)

        dx, dg_rows = _rms_bwd(x_ref[...], g_ref[...], d_ref[...])
        dx_ref[...] = h_ref[...] + dx
        dg_ref[...] += jnp.sum(dg_rows, axis=0, keepdims=True)

    row = pl.BlockSpec((tm, D_MODEL), lambda i: (i, 0))
    vec = pl.BlockSpec((1, D_MODEL), lambda i: (0, 0))
    return pl.pallas_call(
        body, name="rms_in_bwd", grid=(t // tm,),
        in_specs=[row, row, row, vec], out_specs=[row, vec],
        out_shape=[jax.ShapeDtypeStruct((t, D_MODEL), F32), jax.ShapeDtypeStruct((1, D_MODEL), F32)],
        compiler_params=_params("arbitrary"),
    )(x2, dxn, dh1, g1)


def _pack_small(dg1, dbr, dba, dg2, dg3, dw):
    def body(a_ref, b_ref, c_ref, d_ref, e_ref, w_ref, o_ref):
        o_ref[...] = jnp.zeros_like(o_ref)
        for r, ref in enumerate((a_ref, b_ref, c_ref, d_ref, e_ref)):
            o_ref[r:r + 1, :] = ref[...]
        for hp in range(ATT_HEADS // 2):
            o_ref[8 + 2 * hp:10 + 2 * hp, :] = w_ref[hp]

    return pl.pallas_call(body, name="pack_small",
                          out_shape=jax.ShapeDtypeStruct((16, D_MODEL), F32))(dg1, dbr, dba, dg2, dg3, dw)


def _rotary_tables(seq):
    freqs = ROPE_BASE ** (-jnp.arange(0, RET_KEY_DIM, 2, dtype=F32) / RET_KEY_DIM)
    ang = jnp.arange(seq, dtype=F32)[:, None] * freqs[None, :]
    cos, sin = jnp.cos(ang), jnp.sin(ang)
    return jnp.concatenate([cos, cos], axis=1), jnp.concatenate([-sin, sin], axis=1)


def _bias_rows(rel_bias):
    n_far = BIAS_LEN - ATT_Q - MAX_REL + 1
    n_near = BIAS_LEN - n_far - (N_REL - 2)
    w = jnp.concatenate([jnp.broadcast_to(rel_bias[:, N_REL - 1:], (ATT_HEADS, n_far)),
                         rel_bias[:, 1:N_REL - 1][:, ::-1],
                         jnp.broadcast_to(rel_bias[:, :1], (ATT_HEADS, n_near))], axis=1)
    return w.reshape(ATT_HEADS // 2, 2, BIAS_LEN)


def _bias_rows_bwd(dw):
    n_far = BIAS_LEN - ATT_Q - MAX_REL + 1
    mid = dw[:, n_far:n_far + N_REL - 2][:, ::-1]
    return jnp.concatenate([jnp.sum(dw[:, n_far + N_REL - 2:], axis=1, keepdims=True), mid,
                            jnp.sum(dw[:, :n_far], axis=1, keepdims=True)], axis=1)


def _step(x, tgt, norm_mix, b_gate, norm_ffn, norm_final, rel_bias_shard, shard):
    batch, seq, _ = x.shape
    t = batch * seq
    n_rb = rel_bias_shard.shape[-1]
    x2, tgt2 = x.reshape(t, D_MODEL), tgt.reshape(t, D_MODEL)
    g3 = norm_final.reshape(1, D_MODEL)
    cs, sn = _rotary_tables(seq)
    lg = np.log(1.0 - 2.0 ** (-5.0 - np.arange(RET_HEADS, dtype=np.float32))).astype(np.float32)
    lg_arr = jnp.asarray(np.broadcast_to(lg[:, None, None], (RET_HEADS, 1, LANES)))

    def gather(*names):
        return _ChipGather([shard[nm] for nm in names])

    def scatter(*grads):
        return _Exchange(grads, scatter=True)

    rb_pad = jnp.pad(rel_bias_shard, ((0, 0), (0, LANES - n_rb)))
    w_in_t, rb_full = _alone(_ChipGather([shard["w_in_t"], rb_pad]), "gather_w_in")
    rb_full = rb_full.reshape(N_DEV, ATT_HEADS, LANES)[:, :, :n_rb]
    wvec = _bias_rows(jnp.transpose(rb_full, (1, 0, 2)).reshape(ATT_HEADS, N_DEV * n_rb))

    xn = _rms_fwd(x2, norm_mix)
    proj, (w_ret, w_att_t, w_out, w_gate_t) = _mm(
        xn, w_in_t, tb=True, out_dtype=BF16, tm=1024, tn=1664, tk=1024, name="proj",
        exchange=gather("w_ret", "w_att_t", "w_out", "w_gate_t"))
    (gro, o_ret, qr, kr), (w_up_t,) = _ret_fwd(proj, cs, sn, lg_arr, batch, seq, gather("w_up_t"))
    (ao,), (w_down,) = _att_fwd(proj, wvec, batch, seq, gather("w_down"))
    z, y_ret, y_att = _mix_fwd(gro, ao, proj, b_gate, w_ret, w_att_t)
    h1, hn = _out_fwd(z, x2, w_out, norm_ffn)
    g_act, u_act, a_act = _ffn_up(hn, w_gate_t, w_up_t)
    dh2, dh2b, loss, dg3 = _ffn_down_loss(a_act, h1, tgt2, w_down, g3)

    wg = dict(out_dtype=BF16, tn=1024, ta=True)
    slots = {}
    dw_down = _mm(a_act, dh2b, tm=1408, tk=1024, name="dw_down", **wg)
    (d_gact, d_uact), (slots["w_down"],) = _ffn_bwd_act(dh2b, w_down, g_act, u_act, scatter(dw_down))
    dw_gate = _mm(d_gact, hn, tm=1408, tk=1024, name="dw_gate", **wg)
    dw_up = _mm(d_uact, hn, tm=1408, tk=1024, name="dw_up", **wg)
    (dh1, dh1b, dg2), (slots["w_gate_t"],) = _ffn_bwd_in(d_gact, d_uact, w_gate_t, w_up_t, h1, dh2, norm_ffn,
                                                       scatter(dw_gate))
    dw_out = _mm(z, dh1b, tm=1024, tk=2048, name="dw_out", **wg)
    (dyr, dya, dglr, dgla, dbr, dba), _ = _mix_bwd(dh1b, w_out, proj, b_gate, y_ret, y_att, None)
    dgro = _mm(dyr, w_ret, tb=True, out_dtype=BF16, tm=1024, tn=1024, tk=1024, name="dgro")
    dao = _mm(dya, w_att_t, out_dtype=BF16, tm=1024, tn=512, tk=1024, name="dao")
    dw_ret = _mm(gro, dyr, tm=1024, tk=2048, name="dw_ret", **wg)
    dw_att = _mm(dya, ao, tm=1024, tk=2048, name="dw_att", **wg)
    (drq, drk, drv, drg), (slots["w_up_t"],) = _ret_bwd(
        dgro, proj, o_ret, qr, kr, cs, sn, lg_arr, batch, seq, scatter(dw_up))
    (daq, dak, dav, dw), (slots["w_out"], slots["w_ret"], slots["w_att_t"]) = _att_bwd(
        proj, wvec, dao, batch, seq, scatter(dw_out, dw_ret, dw_att))
    dproj = [drq, drk, drv, drg, daq, dak, dav, dglr, dgla]
    dw_in = _mm_pieces(dproj, xn, ta=True, out_dtype=BF16, tm=512, tn=1024, tk=1024, name="dw_in")
    dxn, (slots["w_in_t"],) = _mm_pieces(dproj, w_in_t, ta=False, out_dtype=F32, tm=1024, tn=1024, tk=512, name="dxn",
                                  exchange=scatter(dw_in))
    dx, dg1 = _rms_in_bwd(x2, dxn, dh1, norm_mix)
    small = _pack_small(dg1, dbr, dba, dg2, dg3, dw)
    (small_slots,) = _alone(_ChipGather([small]), "gather_small")
    return loss[0, 0], dx.reshape(batch, seq, D_MODEL), slots, small_slots.reshape(N_DEV, 16, D_MODEL)


def _sum_slots(slots, name):
    _, r, c = slots.shape
    tr = max(d for d in range(16, r + 1, 16) if r % d == 0 and (d * c <= 256 * 1024 or d == 16))

    def body(s_ref, o_ref):
        acc = s_ref[0].astype(F32)
        for s in range(1, N_DEV):
            acc = acc + s_ref[s].astype(F32)
        o_ref[...] = acc

    return pl.pallas_call(
        body, name=name, grid=(r // tr,),
        in_specs=[pl.BlockSpec((N_DEV, tr, c), lambda i: (0, i, 0))],
        out_specs=pl.BlockSpec((tr, c), lambda i: (i, 0)),
        out_shape=jax.ShapeDtypeStruct((r, c), F32),
        compiler_params=_params("parallel"),
    )(slots)


def _adamw_math(w, g, m, v):
    m = ADAM_B1 * m + (1.0 - ADAM_B1) * g
    v = ADAM_B2 * v + (1.0 - ADAM_B2) * (g * g)
    m_hat = m / (1.0 - ADAM_B1 ** ADAM_STEP)
    v_hat = v / (1.0 - ADAM_B2 ** ADAM_STEP)
    return -ADAM_LR * (m_hat / (jnp.sqrt(v_hat) + ADAM_EPS) + ADAM_WD * w), m, v


def _adamw(w, g, m, v, name):
    r, c = w.shape
    tr = r
    while tr * c > 128 * 1024 and tr % 16 == 0:
        tr //= 2

    def body(w_ref, g_ref, m_ref, v_ref, d_ref, nm_ref, nv_ref):
        d_ref[...], nm_ref[...], nv_ref[...] = _adamw_math(w_ref[...], g_ref[...], m_ref[...], v_ref[...])

    blk = pl.BlockSpec((tr, c), lambda i: (i, 0))
    return pl.pallas_call(
        body, name=name, grid=(r // tr,),
        in_specs=[blk] * 4, out_specs=[blk] * 3,
        out_shape=[jax.ShapeDtypeStruct((r, c), F32)] * 3,
        compiler_params=_params("parallel"),
    )(w, g, m, v)


def _adamw_small(ws, gs, ms, vs):
    n = len(ws)

    def body(*refs):
        for i in range(n):
            w_ref, g_ref, m_ref, v_ref = (refs[j * n + i] for j in range(4))
            d_ref, nm_ref, nv_ref = (refs[(4 + j) * n + i] for j in range(3))
            d_ref[...], nm_ref[...], nv_ref[...] = _adamw_math(w_ref[...], g_ref[...], m_ref[...], v_ref[...])

    shapes = [jax.ShapeDtypeStruct(w.shape, F32) for w in ws]
    outs = pl.pallas_call(body, name="adamw_small", out_shape=shapes * 3)(*ws, *gs, *ms, *vs)
    return outs[:n], outs[n:2 * n], outs[2 * n:]


def kernel(x, norm_mix, w_in, b_gate, rel_bias, w_ret_out, w_att_out, w_out, norm_ffn, w_ffn_gate, w_ffn_up, w_ffn_down, norm_final, loss_target, m_norm_mix, m_w_in, m_b_gate, m_rel_bias, m_w_ret_out, m_w_att_out, m_w_out, m_norm_ffn, m_w_ffn_gate, m_w_ffn_up, m_w_ffn_down, m_norm_final, v_norm_mix, v_w_in, v_b_gate, v_rel_bias, v_w_ret_out, v_w_att_out, v_w_out, v_norm_ffn, v_w_ffn_gate, v_w_ffn_up, v_w_ffn_down, v_norm_final):
    me = _index(_place())
    n_rb = rel_bias.shape[-1]

    shard = dict(w_in_t=w_in[0].T, w_gate_t=w_ffn_gate[0].T, w_up_t=w_ffn_up[0].T, w_down=w_ffn_down[0],
                 w_ret=w_ret_out[0], w_out=w_out[0], w_att_t=w_att_out[0].T)
    shard = {nm: s.astype(BF16) for nm, s in shard.items()}
    loss, dx, slots, small_slots = _step(x, loss_target, norm_mix, b_gate, norm_ffn, norm_final, rel_bias[0], shard)
    loss = lax.psum(loss, ("x", "y", "c"))
    summed = {nm: _sum_slots(s, "sum_" + nm) for nm, s in slots.items()}
    small_sum = _sum_slots(small_slots, "sum_small")

    g = dict(
        w_in=summed["w_in_t"].T, w_ffn_gate=summed["w_gate_t"].T, w_ffn_up=summed["w_up_t"].T,
        w_ffn_down=summed["w_down"], w_ret_out=summed["w_ret"], w_out=summed["w_out"], w_att_out=summed["w_att_t"].T,
        norm_mix=small_sum[0:1], b_gate=jnp.concatenate([small_sum[1:2], small_sum[2:3]], axis=1),
        norm_ffn=small_sum[3:4], norm_final=small_sum[4:5],
        rel_bias=lax.dynamic_slice_in_dim(_bias_rows_bwd(small_sum[8:16]), me * n_rb, n_rb, axis=1),
    )
    w = dict(norm_mix=norm_mix, w_in=w_in, b_gate=b_gate, rel_bias=rel_bias, w_ret_out=w_ret_out, w_att_out=w_att_out,
             w_out=w_out, norm_ffn=norm_ffn, w_ffn_gate=w_ffn_gate, w_ffn_up=w_ffn_up, w_ffn_down=w_ffn_down,
             norm_final=norm_final)
    m = dict(norm_mix=m_norm_mix, w_in=m_w_in, b_gate=m_b_gate, rel_bias=m_rel_bias, w_ret_out=m_w_ret_out,
             w_att_out=m_w_att_out, w_out=m_w_out, norm_ffn=m_norm_ffn, w_ffn_gate=m_w_ffn_gate, w_ffn_up=m_w_ffn_up,
             w_ffn_down=m_w_ffn_down, norm_final=m_norm_final)
    v = dict(norm_mix=v_norm_mix, w_in=v_w_in, b_gate=v_b_gate, rel_bias=v_rel_bias, w_ret_out=v_w_ret_out,
             w_att_out=v_w_att_out, w_out=v_w_out, norm_ffn=v_norm_ffn, w_ffn_gate=v_w_ffn_gate, w_ffn_up=v_w_ffn_up,
             w_ffn_down=v_w_ffn_down, norm_final=v_norm_final)
    order = ("norm_mix", "w_in", "b_gate", "rel_bias", "w_ret_out", "w_att_out", "w_out", "norm_ffn",
             "w_ffn_gate", "w_ffn_up", "w_ffn_down", "norm_final")
    small_names = ("norm_mix", "b_gate", "rel_bias", "norm_ffn", "norm_final")

    def flat(a):
        return a[0] if a.ndim == 3 else a.reshape(-1, a.shape[-1])

    grad, delta, new_m, new_v = {}, {}, {}, {}
    for nm in order:
        if nm not in small_names:
            d, nmom, nvar = _adamw(flat(w[nm]), g[nm], flat(m[nm]), flat(v[nm]), "adamw_" + nm)
            grad[nm], delta[nm], new_m[nm], new_v[nm] = (a.reshape(w[nm].shape) for a in (g[nm], d, nmom, nvar))
    ds, nms, nvs = _adamw_small([flat(w[nm]) for nm in small_names], [g[nm] for nm in small_names],
                                [flat(m[nm]) for nm in small_names], [flat(v[nm]) for nm in small_names])
    for i, nm in enumerate(small_names):
        grad[nm], delta[nm], new_m[nm], new_v[nm] = (a.reshape(w[nm].shape) for a in (g[nm], ds[i], nms[i], nvs[i]))

    return (loss, dx, *[grad[nm] for nm in order], *[delta[nm] for nm in order],
            *[new_m[nm] for nm in order], *[new_v[nm] for nm in order])
```

```python
import numpy as np
import jax
import jax.numpy as jnp
from jax import lax
from jax.experimental import pallas as pl
from jax.experimental.pallas import tpu as pltpu

F32 = jnp.float32
BF16 = jnp.bfloat16
MESH = pl.DeviceIdType.MESH

D_MODEL = 1024
CHUNK = 64
RET_HEADS = 4
RET_KEY_DIM = 128
RET_VAL_DIM = 256
ATT_HEADS = 8
BAND_CHUNKS = 8
MAX_REL = 256
N_REL = CHUNK + MAX_REL
D_FF = 2816
N_IN = 6656
ROPE_BASE = 10000.0
EPS = 1e-6
NEG_INF = -1e30
C_RQ, C_RK, C_RV, C_RG, C_AQ, C_AK, C_AV, C_GL = 0, 512, 1024, 2048, 3072, 3584, 4096, 4608

ADAM_LR = 0.001
ADAM_B1 = 0.9
ADAM_B2 = 0.999
ADAM_EPS = 1e-08
ADAM_WD = 0.01
ADAM_STEP = 10

N_DEV = 8
LANES = 128
RET_TILE = 256
ATT_Q = 256
ATT_PAD = BAND_CHUNKS * CHUNK
ATT_WIN = ATT_PAD + ATT_Q
ATT_STARTS = ATT_PAD // ATT_Q
ATT_ROWS = 32
BIAS_LEN = 1024
VMEM_LIMIT = 48 * 1024 * 1024


def _params(*sem):
    return pltpu.CompilerParams(dimension_semantics=sem, vmem_limit_bytes=VMEM_LIMIT)


def _dot(a, b):
    return lax.dot_general(a, b, (((1,), (0,)), ((), ())), preferred_element_type=F32)


def _dot_nt(a, b):
    return lax.dot_general(a, b, (((1,), (1,)), ((), ())), preferred_element_type=F32)


def _dot_tn(a, b):
    return lax.dot_general(a, b, (((0,), (0,)), ((), ())), preferred_element_type=F32)


def _sigmoid(x):
    return 1.0 / (1.0 + jnp.exp(-x))


def _rms_bwd(x, g, dy):
    r = lax.rsqrt(jnp.mean(x * x, axis=-1, keepdims=True) + EPS)
    u = dy * g
    dx = r * u - x * (r * r * r) * jnp.mean(u * x, axis=-1, keepdims=True)
    return dx, dy * x * r


def _place():
    return lax.axis_index("x"), lax.axis_index("y"), lax.axis_index("c")


def _peer(k):
    x, y, c = _place()
    return ((1 - x) if k & 4 else x, (1 - y) if k & 2 else y, (1 - c) if k & 1 else c)


def _index(place):
    return 4 * place[0] + 2 * place[1] + place[2]


def _rows(ref, block, nrows):
    align = 16 if ref.dtype == BF16 else 8
    return ref.at[pl.ds(pl.multiple_of(block * nrows, align), nrows)]


class _Exchange:
    def __init__(self, arrays, scatter):
        self.arrays, self.scatter, self.n = list(arrays), scatter, len(arrays)

    def out_shape(self):
        if self.scatter:
            return [jax.ShapeDtypeStruct((N_DEV, a.shape[0] // N_DEV) + a.shape[1:], a.dtype) for a in self.arrays]
        return [jax.ShapeDtypeStruct((N_DEV * a.shape[0],) + a.shape[1:], a.dtype) for a in self.arrays]

    def scratch(self):
        return [pltpu.SemaphoreType.DMA((self.n, N_DEV - 1)), pltpu.SemaphoreType.DMA((self.n, N_DEV - 1)),
                pltpu.SemaphoreType.DMA((self.n,))]

    def _copies(self, ins, outs, sems):
        send_sems, recv_sems, local_sems = sems
        me = _index(_place())

        def src(w, to):
            return _rows(ins[w], to, ins[w].shape[0] // N_DEV) if self.scatter else ins[w]

        def dst(w, origin):
            return outs[w].at[origin] if self.scatter else _rows(outs[w], origin, ins[w].shape[0])

        def remote(w, k, to, origin):
            return pltpu.make_async_remote_copy(src_ref=src(w, to), dst_ref=dst(w, origin),
                                                send_sem=send_sems.at[w, k - 1], recv_sem=recv_sems.at[w, k - 1],
                                                device_id=_peer(k), device_id_type=MESH)

        pairs = [(w, k) for w in range(self.n) for k in range(1, N_DEV)]
        own = lambda: [pltpu.make_async_copy(src(w, me), dst(w, me), local_sems.at[w]) for w in range(self.n)]
        sent = lambda: [remote(w, k, _index(_peer(k)), me) for w, k in pairs]
        arriving = lambda: [remote(w, k, me, _index(_peer(k))) for w, k in pairs]
        return own, sent, arriving

    def start(self, ins, outs, sems):
        own, sent, _ = self._copies(ins, outs, sems)
        for cp in own() + sent():
            cp.start()

    def wait(self, ins, outs, sems):
        own, sent, arriving = self._copies(ins, outs, sems)
        for cp in arriving():
            cp.wait_recv()
        for cp in sent():
            cp.wait_send()
        for cp in own():
            cp.wait()


class _ChipGather:
    def __init__(self, arrays):
        self.arrays, self.n = list(arrays), len(arrays)

    def out_shape(self):
        return [jax.ShapeDtypeStruct((N_DEV * a.shape[0],) + a.shape[1:], a.dtype) for a in self.arrays]

    def scratch(self):
        return [pltpu.SemaphoreType.DMA((self.n, N_DEV - 1)), pltpu.SemaphoreType.DMA((self.n, N_DEV - 1)),
                pltpu.SemaphoreType.DMA((self.n,))]

    def _parts(self, ins, outs, sems):
        send_sems, recv_sems, local_sems = sems
        x, y, c = _place()
        me, sibling = (x, y, c), (x, y, 1 - c)
        chips = [(1 - x, y), (x, 1 - y), (1 - x, 1 - y)]

        def rows(w, place):
            return _rows(outs[w], _index(place), ins[w].shape[0])

        def copy(w, k, block, to, own=False):
            return pltpu.make_async_remote_copy(src_ref=ins[w] if own else rows(w, block), dst_ref=rows(w, block),
                                                send_sem=send_sems.at[w, k], recv_sem=recv_sems.at[w, k],
                                                device_id=to, device_id_type=MESH)

        def local(w):
            return pltpu.make_async_copy(ins[w], rows(w, me), local_sems.at[w])

        return me, sibling, chips, c, copy, local

    def start(self, ins, outs, sems):
        me, sibling, chips, c, copy, local = self._parts(ins, outs, sems)
        for w in range(self.n):
            local(w).start()
            copy(w, 0, me, sibling, own=True).start()
            for j, chip in enumerate(chips):
                copy(w, 1 + j, me, (*chip, c), own=True).start()

    def wait(self, ins, outs, sems):
        me, sibling, chips, c, copy, local = self._parts(ins, outs, sems)
        for w in range(self.n):
            for j, chip in enumerate(chips):
                copy(w, 1 + j, (*chip, c), me).wait_recv()
                copy(w, 4 + j, (*chip, c), sibling).start()
        for w in range(self.n):
            copy(w, 0, sibling, me).wait_recv()
            for j, chip in enumerate(chips):
                copy(w, 4 + j, (*chip, 1 - c), me).wait_recv()
            copy(w, 0, me, sibling, own=True).wait_send()
            for j, chip in enumerate(chips):
                copy(w, 1 + j, me, (*chip, c), own=True).wait_send()
                copy(w, 4 + j, (*chip, c), sibling).wait_send()
            local(w).wait()


def _call(body, *, name, grid, in_specs, out_specs, out_shape, scratch=(), semantics, args, exchange=None):
    if exchange is None:
        return pl.pallas_call(body, name=name, grid=grid, in_specs=in_specs, out_specs=out_specs, out_shape=out_shape,
                              scratch_shapes=list(scratch), compiler_params=_params(*semantics))(*args), None
    n_in, n_out, n_scr, nx = len(in_specs), len(out_specs), len(scratch), exchange.n

    def full_body(*refs):
        ins, refs = refs[:n_in], refs[n_in:]
        x_in, refs = refs[:nx], refs[nx:]
        outs, refs = refs[:n_out], refs[n_out:]
        x_out, refs = refs[:nx], refs[nx:]
        scr, sems = refs[:n_scr], refs[n_scr:]
        first, last = True, True
        for axis, size in enumerate(grid):
            first = jnp.logical_and(first, pl.program_id(axis) == 0)
            last = jnp.logical_and(last, pl.program_id(axis) == size - 1)
        if grid:
            pl.when(first)(lambda: exchange.start(x_in, x_out, sems))
        else:
            exchange.start(x_in, x_out, sems)
        body(*ins, *outs, *scr)
        if grid:
            pl.when(last)(lambda: exchange.wait(x_in, x_out, sems))
        else:
            exchange.wait(x_in, x_out, sems)

    hbm = pl.BlockSpec(memory_space=pltpu.HBM)
    res = pl.pallas_call(
        full_body, name=name, grid=grid,
        in_specs=list(in_specs) + [hbm] * nx, out_specs=list(out_specs) + [hbm] * nx,
        out_shape=list(out_shape) + exchange.out_shape(),
        scratch_shapes=list(scratch) + exchange.scratch(),
        compiler_params=_params(*(["arbitrary"] * len(grid))),
    )(*args, *exchange.arrays)
    return res[:n_out], res[n_out:]


def _alone(exchange, name):
    return _call(lambda: None, name=name, grid=(), in_specs=[], out_specs=[], out_shape=[], semantics=(),
                 args=(), exchange=exchange)[1]


def _mm(a, b, *, ta=False, tb=False, out_dtype, tm, tn, tk, name, exchange=None):
    m, k = (a.shape[1], a.shape[0]) if ta else a.shape
    n = b.shape[0] if tb else b.shape[1]
    assert k == (b.shape[1] if tb else b.shape[0])
    tm, tn, tk = min(tm, m), min(tn, n), min(tk, k)
    assert m % tm == 0 and n % tn == 0 and k % tk == 0, (name, m, n, k)
    nk = k // tk
    dims = (((0 if ta else 1,), (1 if tb else 0,)), ((), ()))

    def body(a_ref, b_ref, o_ref, *acc):
        prod = lax.dot_general(a_ref[...].astype(BF16), b_ref[...].astype(BF16), dims, preferred_element_type=F32)
        if nk == 1:
            o_ref[...] = prod.astype(o_ref.dtype)
            return
        acc_ref, kk = acc[0], pl.program_id(2)

        @pl.when(kk == 0)
        def _():
            acc_ref[...] = prod

        @pl.when((kk > 0) & (kk < nk - 1))
        def _():
            acc_ref[...] += prod

        @pl.when(kk == nk - 1)
        def _():
            o_ref[...] = (acc_ref[...] + prod).astype(o_ref.dtype)

    a_spec = (pl.BlockSpec((tk, tm), lambda i, j, kk: (kk, i)) if ta
              else pl.BlockSpec((tm, tk), lambda i, j, kk: (i, kk)))
    b_spec = (pl.BlockSpec((tn, tk), lambda i, j, kk: (j, kk)) if tb
              else pl.BlockSpec((tk, tn), lambda i, j, kk: (kk, j)))
    (out,), moved = _call(
        body, name=name, grid=(m // tm, n // tn, nk),
        in_specs=[a_spec, b_spec],
        out_specs=[pl.BlockSpec((tm, tn), lambda i, j, kk: (i, j))],
        out_shape=[jax.ShapeDtypeStruct((m, n), out_dtype)],
        scratch=[pltpu.VMEM((tm, tn), F32)] if nk > 1 else [],
        semantics=("parallel", "parallel", "arbitrary"), args=(a, b), exchange=exchange)
    return out if exchange is None else (out, moved)


def _mm_pieces(pieces, b, *, ta, out_dtype, tm, tn, tk, name, exchange=None):
    rows, n = pieces[0].shape[0], b.shape[1]
    step = tm if ta else tk
    assert all(p.shape[0] == rows and p.shape[1] % step == 0 for p in pieces), name
    edges = [int(e) for e in np.cumsum([0] + [p.shape[1] // step for p in pieces])]
    total = edges[-1] * step
    m, k = (total, rows) if ta else (rows, total)
    assert b.shape[0] == k and m % tm == 0 and n % tn == 0 and k % tk == 0, name
    nk, npieces = k // tk, len(pieces)
    dims = (((0 if ta else 1,), (0,)), ((), ()))

    def body(*refs):
        a_refs, (b_ref, o_ref, acc_ref) = refs[:npieces], refs[npieces:]
        kk = pl.program_id(2)
        pos = pl.program_id(0) if ta else kk

        @pl.when(kk == 0)
        def _():
            acc_ref[...] = jnp.zeros_like(acc_ref)

        for p, a_ref in enumerate(a_refs):
            @pl.when((pos >= edges[p]) & (pos < edges[p + 1]))
            def _(a_ref=a_ref):
                acc_ref[...] += lax.dot_general(a_ref[...], b_ref[...], dims, preferred_element_type=F32)

        @pl.when(kk == nk - 1)
        def _():
            o_ref[...] = acc_ref[...].astype(o_ref.dtype)

    def a_spec(p):
        lo, last = edges[p], edges[p + 1] - edges[p] - 1
        if ta:
            def index(i, j, kk):
                inside = (i >= lo) & (i <= lo + last)
                return jnp.where(inside, kk, 0), jnp.clip(i - lo, 0, last)
            return pl.BlockSpec((tk, tm), index)
        return pl.BlockSpec((tm, tk), lambda i, j, kk: (i, jnp.clip(kk - lo, 0, last)))

    (out,), moved = _call(
        body, name=name, grid=(m // tm, n // tn, nk),
        in_specs=[a_spec(p) for p in range(npieces)] + [pl.BlockSpec((tk, tn), lambda i, j, kk: (kk, j))],
        out_specs=[pl.BlockSpec((tm, tn), lambda i, j, kk: (i, j))],
        out_shape=[jax.ShapeDtypeStruct((m, n), out_dtype)],
        scratch=[pltpu.VMEM((tm, tn), F32)],
        semantics=("parallel", "parallel", "arbitrary"), args=(*pieces, b), exchange=exchange)
    return out if exchange is None else (out, moved)


def _rms_fwd(x2, g):
    t = x2.shape[0]
    tm = min(512, t)

    def body(x_ref, g_ref, o_ref):
        x = x_ref[...]
        r = lax.rsqrt(jnp.mean(x * x, axis=-1, keepdims=True) + EPS)
        o_ref[...] = (x * r * g_ref[...]).astype(o_ref.dtype)

    return pl.pallas_call(
        body, name="rms_in_fwd", grid=(t // tm,),
        in_specs=[pl.BlockSpec((tm, D_MODEL), lambda i: (i, 0)), pl.BlockSpec((1, D_MODEL), lambda i: (0, 0))],
        out_specs=pl.BlockSpec((tm, D_MODEL), lambda i: (i, 0)),
        out_shape=jax.ShapeDtypeStruct((t, D_MODEL), BF16),
        compiler_params=_params("parallel"),
    )(x2, g)


def _decay(lg):
    row = lax.broadcasted_iota(jnp.int32, (RET_TILE, RET_TILE), 0)
    col = lax.broadcasted_iota(jnp.int32, (RET_TILE, RET_TILE), 1)
    within = jnp.exp(lg * jnp.abs(row - col).astype(F32))
    inside = jnp.where((col >> 6) <= (row >> 6), within, 0.0)
    pos = lax.broadcasted_iota(jnp.int32, (RET_TILE, 1), 0).astype(F32)
    q_dec = jnp.exp(lg * (pos + 1.0))
    k_dec = jnp.exp(lg * (RET_TILE - 1.0 - pos))
    tile_dec = jnp.exp(lg * float(RET_TILE))
    return inside, q_dec, k_dec, tile_dec


def _scaled(a_bf16, dec):
    return (a_bf16.astype(F32) * dec).astype(BF16)


def _ret_fwd(proj, cs, sn, lg_arr, batch, seq, exchange):
    t = batch * seq
    nt = seq // RET_TILE

    def body(q_ref, k_ref, v_ref, rg_ref, cs_ref, sn_ref, lg_ref, gro_ref, o_ref, qr_ref, kr_ref):
        lg = lg_ref[:, 0:1]
        cs_t, sn_t = cs_ref[...], sn_ref[...]
        q = q_ref[...].astype(F32)
        k = k_ref[...].astype(F32)
        qr_ref[...] = (q * cs_t + pltpu.roll(q, 64, 1) * sn_t).astype(BF16)
        kr_ref[...] = ((k * cs_t + pltpu.roll(k, 64, 1) * sn_t) * (RET_KEY_DIM ** -0.5)).astype(BF16)
        inside, q_dec, k_dec, tile_dec = _decay(lg)
        state = jnp.zeros((RET_KEY_DIM, RET_VAL_DIM), F32)
        for i in range(nt):
            rows = slice(i * RET_TILE, (i + 1) * RET_TILE)
            qi, ki, vi = qr_ref[rows, :], kr_ref[rows, :], v_ref[rows, :]
            acc = _dot((_dot_nt(qi, ki) * inside).astype(BF16), vi)
            if i > 0:
                acc = acc + _dot(_scaled(qi, q_dec), state.astype(BF16))
            if i < nt - 1:
                state = state * tile_dec + _dot_tn(_scaled(ki, k_dec), vi)
            o_ref[rows, :] = acc
            xc = acc - jnp.mean(acc, axis=-1, keepdims=True)
            nrm = xc * lax.rsqrt(jnp.mean(xc * xc, axis=-1, keepdims=True) + EPS)
            rg = rg_ref[rows, :].astype(F32)
            gro_ref[rows, :] = (rg * _sigmoid(rg) * nrm).astype(BF16)

    def col(base, width):
        return lambda b, h: (b, base // width + h)

    return _call(
        body, name="ret_fwd", grid=(batch, RET_HEADS),
        in_specs=[pl.BlockSpec((seq, RET_KEY_DIM), col(C_RQ, RET_KEY_DIM)),
                  pl.BlockSpec((seq, RET_KEY_DIM), col(C_RK, RET_KEY_DIM)),
                  pl.BlockSpec((seq, RET_VAL_DIM), col(C_RV, RET_VAL_DIM)),
                  pl.BlockSpec((seq, RET_VAL_DIM), col(C_RG, RET_VAL_DIM)),
                  pl.BlockSpec((seq, RET_KEY_DIM), lambda b, h: (0, 0)),
                  pl.BlockSpec((seq, RET_KEY_DIM), lambda b, h: (0, 0)),
                  pl.BlockSpec((None, 1, LANES), lambda b, h: (h, 0, 0))],
        out_specs=[pl.BlockSpec((seq, RET_VAL_DIM), lambda b, h: (b, h)),
                   pl.BlockSpec((seq, RET_VAL_DIM), lambda b, h: (b, h)),
                   pl.BlockSpec((seq, RET_KEY_DIM), lambda b, h: (b, h)),
                   pl.BlockSpec((seq, RET_KEY_DIM), lambda b, h: (b, h))],
        out_shape=[jax.ShapeDtypeStruct((t, RET_HEADS * RET_VAL_DIM), BF16),
                   jax.ShapeDtypeStruct((t, RET_HEADS * RET_VAL_DIM), F32),
                   jax.ShapeDtypeStruct((t, RET_HEADS * RET_KEY_DIM), BF16),
                   jax.ShapeDtypeStruct((t, RET_HEADS * RET_KEY_DIM), BF16)],
        semantics=("parallel", "parallel"), args=(proj, proj, proj, proj, cs, sn, lg_arr), exchange=exchange)


def _att_bias(w_ref, bias_ref):
    n_i = lax.broadcasted_iota(jnp.int32, (ATT_Q, BIAS_LEN), 0)
    qc = lax.broadcasted_iota(jnp.int32, (ATT_Q, ATT_WIN), 0) >> 6
    kc = lax.broadcasted_iota(jnp.int32, (ATT_Q, ATT_WIN), 1) >> 6
    dc = qc + BAND_CHUNKS - kc
    band = (dc >= 0) & (dc <= BAND_CHUNKS)
    key = lax.broadcasted_iota(jnp.int32, (ATT_Q, ATT_WIN), 1)
    for e in range(2):
        xw = jnp.broadcast_to(w_ref[e:e + 1, :], (ATT_Q, BIAS_LEN))
        for bit in range(8):
            xw = jnp.where(((n_i >> bit) & 1) == 1, pltpu.roll(xw, 1 << bit, 1), xw)
        bias = jnp.where(band, xw[:, BIAS_LEN - ATT_WIN:], NEG_INF)
        for first in range(ATT_STARTS):
            bias_ref[first, e] = jnp.where(key + (first * ATT_Q - ATT_PAD) >= 0, bias, NEG_INF)
        bias_ref[ATT_STARTS, e] = bias


def _att_specs(batch, seq):
    ni = seq // ATT_Q
    q_spec = pl.BlockSpec((ATT_Q, LANES), lambda hp, b, i: (b * ni + i, C_AQ // LANES + hp))
    k_spec = pl.BlockSpec((seq, LANES), lambda hp, b, i: (b, C_AK // LANES + hp))
    v_spec = pl.BlockSpec((seq, LANES), lambda hp, b, i: (b, C_AV // LANES + hp))
    w_spec = pl.BlockSpec((None, 2, BIAS_LEN), lambda hp, b, i: (hp, 0, 0))
    pad = pltpu.VMEM((seq + ATT_PAD, LANES), BF16)
    return ni, q_spec, k_spec, v_spec, w_spec, pad


def _att_pad(src_ref, pad_ref):
    pad_ref[:ATT_PAD, :] = jnp.zeros((ATT_PAD, LANES), BF16)
    pad_ref[ATT_PAD:, :] = src_ref[...]


def _att_head(q2, sel):
    return jnp.where(sel, q2, jnp.zeros_like(q2)) * 0.125


def _att_softmax_rows(s_ref, bias_ref, rows):
    s = s_ref[rows, :] + bias_ref[rows, :]
    ex = jnp.exp(s - jnp.max(s, axis=-1, keepdims=True))
    return ex, 1.0 / jnp.sum(ex, axis=-1, keepdims=True)


def _att_fwd(proj, wvec, batch, seq, exchange):
    ni, q_spec, k_spec, v_spec, w_spec, pad = _att_specs(batch, seq)

    def body(q_ref, k_ref, v_ref, w_ref, o_ref, bias_ref, kp_ref, vp_ref, s_ref, e_ref):
        b, i = pl.program_id(1), pl.program_id(2)

        @pl.when((b == 0) & (i == 0))
        def _():
            _att_bias(w_ref, bias_ref)

        @pl.when(i == 0)
        def _():
            _att_pad(k_ref, kp_ref)
            _att_pad(v_ref, vp_ref)

        win = pl.ds(pl.multiple_of(i * ATT_Q, ATT_Q), ATT_WIN)
        k2, v2, q2 = kp_ref[win, :], vp_ref[win, :], q_ref[...]
        lo = lax.broadcasted_iota(jnp.int32, (1, LANES), 1) < 64
        start = jnp.minimum(i, ATT_STARTS)
        out = jnp.zeros((ATT_Q, LANES), F32)
        for e in range(2):
            sel = lo if e == 0 else jnp.logical_not(lo)
            s_ref[e] = _dot_nt(_att_head(q2, sel), k2)
            rsum = []
            for c in range(ATT_Q // ATT_ROWS):
                rows = slice(c * ATT_ROWS, (c + 1) * ATT_ROWS)
                ex, r = _att_softmax_rows(s_ref.at[e], bias_ref.at[start, e], rows)
                e_ref[e, rows, :] = ex.astype(BF16)
                rsum.append(r)
            out = out + _dot(e_ref[e], jnp.where(sel, v2, jnp.zeros_like(v2))) * jnp.concatenate(rsum, axis=0)
        o_ref[...] = out.astype(BF16)

    return _call(
        body, name="att_fwd", grid=(ATT_HEADS // 2, batch, ni),
        in_specs=[q_spec, k_spec, v_spec, w_spec],
        out_specs=[pl.BlockSpec((ATT_Q, LANES), lambda hp, b, i: (b * ni + i, hp))],
        out_shape=[jax.ShapeDtypeStruct((batch * seq, ATT_HEADS * 64), BF16)],
        scratch=[pltpu.VMEM((ATT_STARTS + 1, 2, ATT_Q, ATT_WIN), F32), pad, pad,
                 pltpu.VMEM((2, ATT_Q, ATT_WIN), F32), pltpu.VMEM((2, ATT_Q, ATT_WIN), BF16)],
        semantics=("arbitrary", "arbitrary", "arbitrary"), args=(proj, proj, proj, wvec), exchange=exchange)


def _mix_fwd(gro, ao, proj, b_gate, w_ret, w_att_t):
    t = gro.shape[0]
    tm, tn = min(256, t), 512

    def body(gro_ref, ao_ref, glr_ref, gla_ref, br_ref, ba_ref, wr_ref, wa_ref, z_ref, yr_ref, ya_ref):
        yr = _dot(gro_ref[...], wr_ref[...])
        ya = _dot_nt(ao_ref[...], wa_ref[...])
        gr = _sigmoid(glr_ref[...].astype(F32) + br_ref[...])
        ga = _sigmoid(gla_ref[...].astype(F32) + ba_ref[...])
        z_ref[...] = (gr * yr + ga * ya).astype(BF16)
        yr_ref[...] = yr.astype(BF16)
        ya_ref[...] = ya.astype(BF16)

    nb = D_MODEL // tn
    out = pl.BlockSpec((tm, tn), lambda i, j: (i, j))
    return pl.pallas_call(
        body, name="mix_fwd", grid=(t // tm, nb),
        in_specs=[pl.BlockSpec((tm, D_MODEL), lambda i, j: (i, 0)),
                  pl.BlockSpec((tm, 512), lambda i, j: (i, 0)),
                  pl.BlockSpec((tm, tn), lambda i, j: (i, C_GL // tn + j)),
                  pl.BlockSpec((tm, tn), lambda i, j: (i, C_GL // tn + nb + j)),
                  pl.BlockSpec((1, tn), lambda i, j: (0, j)),
                  pl.BlockSpec((1, tn), lambda i, j: (0, nb + j)),
                  pl.BlockSpec((D_MODEL, tn), lambda i, j: (0, j)),
                  pl.BlockSpec((tn, 512), lambda i, j: (j, 0))],
        out_specs=[out, out, out],
        out_shape=[jax.ShapeDtypeStruct((t, D_MODEL), BF16)] * 3,
        compiler_params=_params("parallel", "parallel"),
    )(gro, ao, proj, proj, b_gate, b_gate, w_ret, w_att_t)


def _out_fwd(z, x2, w_out, g2):
    t = z.shape[0]
    tm = min(256, t)

    def body(z_ref, x_ref, w_ref, g_ref, h_ref, hn_ref):
        h = x_ref[...] + _dot(z_ref[...], w_ref[...])
        h_ref[...] = h
        r = lax.rsqrt(jnp.mean(h * h, axis=-1, keepdims=True) + EPS)
        hn_ref[...] = (h * r * g_ref[...]).astype(BF16)

    row = pl.BlockSpec((tm, D_MODEL), lambda i: (i, 0))
    return pl.pallas_call(
        body, name="out_fwd", grid=(t // tm,),
        in_specs=[row, row, pl.BlockSpec((D_MODEL, D_MODEL), lambda i: (0, 0)),
                  pl.BlockSpec((1, D_MODEL), lambda i: (0, 0))],
        out_specs=[row, row],
        out_shape=[jax.ShapeDtypeStruct((t, D_MODEL), F32), jax.ShapeDtypeStruct((t, D_MODEL), BF16)],
        compiler_params=_params("parallel"),
    )(z, x2, w_out, g2)


def _ffn_up(hn, wg_t, wu_t):
    t = hn.shape[0]
    tm, tn = min(512, t), D_FF // 2

    def body(h_ref, wg_ref, wu_ref, g_ref, u_ref, a_ref):
        g = _dot_nt(h_ref[...], wg_ref[...])
        u = _dot_nt(h_ref[...], wu_ref[...])
        g_ref[...] = g.astype(BF16)
        u_ref[...] = u.astype(BF16)
        a_ref[...] = (g * _sigmoid(g) * u).astype(BF16)

    w_spec = pl.BlockSpec((tn, D_MODEL), lambda j, i: (j, 0))
    out = pl.BlockSpec((tm, tn), lambda j, i: (i, j))
    return pl.pallas_call(
        body, name="ffn_up", grid=(D_FF // tn, t // tm),
        in_specs=[pl.BlockSpec((tm, D_MODEL), lambda j, i: (i, 0)), w_spec, w_spec],
        out_specs=[out, out, out],
        out_shape=[jax.ShapeDtypeStruct((t, D_FF), BF16)] * 3,
        compiler_params=_params("parallel", "parallel"),
    )(hn, wg_t, wu_t)


def _ffn_down_loss(a, h1, tgt, w_down, g3):
    t = a.shape[0]
    tm = min(512, t)

    def body(a_ref, h_ref, t_ref, w_ref, g_ref, dh_ref, dhb_ref, loss_ref, dg_ref):
        @pl.when(pl.program_id(0) == 0)
        def _():
            loss_ref[...] = jnp.zeros_like(loss_ref)
            dg_ref[...] = jnp.zeros_like(dg_ref)

        g = g_ref[...]
        h2 = h_ref[...] + _dot(a_ref[...], w_ref[...])
        r = lax.rsqrt(jnp.mean(h2 * h2, axis=-1, keepdims=True) + EPS)
        err = h2 * r * g - t_ref[...]
        loss_ref[...] += jnp.sum(err * err) * (0.5 / D_MODEL)
        dy = err * (1.0 / D_MODEL)
        dh, dg_rows = _rms_bwd(h2, g, dy)
        dg_ref[...] += jnp.sum(dg_rows, axis=0, keepdims=True)
        dh_ref[...] = dh
        dhb_ref[...] = dh.astype(BF16)

    row = pl.BlockSpec((tm, D_MODEL), lambda i: (i, 0))
    vec = pl.BlockSpec((1, D_MODEL), lambda i: (0, 0))
    return pl.pallas_call(
        body, name="ffn_down_loss", grid=(t // tm,),
        in_specs=[pl.BlockSpec((tm, D_FF), lambda i: (i, 0)), row, row,
                  pl.BlockSpec((D_FF, D_MODEL), lambda i: (0, 0)), vec],
        out_specs=[row, row, pl.BlockSpec((1, LANES), lambda i: (0, 0)), vec],
        out_shape=[jax.ShapeDtypeStruct((t, D_MODEL), F32), jax.ShapeDtypeStruct((t, D_MODEL), BF16),
                   jax.ShapeDtypeStruct((1, LANES), F32), jax.ShapeDtypeStruct((1, D_MODEL), F32)],
        compiler_params=_params("arbitrary"),
    )(a, h1, tgt, w_down, g3)


def _ffn_bwd_act(dh2b, w_down, g_act, u_act, exchange):
    t = dh2b.shape[0]
    tm, tn = min(512, t), D_FF // 2

    def body(d_ref, w_ref, g_ref, u_ref, dg_ref, du_ref):
        da = _dot_nt(d_ref[...], w_ref[...])
        g = g_ref[...].astype(F32)
        u = u_ref[...].astype(F32)
        sg = _sigmoid(g)
        dg_ref[...] = (da * u * sg * (1.0 + g * (1.0 - sg))).astype(BF16)
        du_ref[...] = (da * g * sg).astype(BF16)

    blk = pl.BlockSpec((tm, tn), lambda j, i: (i, j))
    return _call(
        body, name="ffn_bwd_act", grid=(D_FF // tn, t // tm),
        in_specs=[pl.BlockSpec((tm, D_MODEL), lambda j, i: (i, 0)),
                  pl.BlockSpec((tn, D_MODEL), lambda j, i: (j, 0)), blk, blk],
        out_specs=[blk, blk],
        out_shape=[jax.ShapeDtypeStruct((t, D_FF), BF16)] * 2,
        semantics=("parallel", "parallel"), args=(dh2b, w_down, g_act, u_act), exchange=exchange)


def _ffn_bwd_in(dg, du, wg_t, wu_t, h1, dh2, g2, exchange):
    t = dg.shape[0]
    tm, tk = min(512, t), D_FF // 2
    nk = D_FF // tk

    def body(dg_ref, du_ref, wg_ref, wu_ref, h_ref, d2_ref, g_ref, dh_ref, dhb_ref, gn_ref, acc_ref):
        i, kk = pl.program_id(0), pl.program_id(1)

        @pl.when((i == 0) & (kk == 0))
        def _():
            gn_ref[...] = jnp.zeros_like(gn_ref)

        @pl.when(kk == 0)
        def _():
            acc_ref[...] = jnp.zeros_like(acc_ref)

        acc_ref[...] += _dot(dg_ref[...], wg_ref[...]) + _dot(du_ref[...], wu_ref[...])

        @pl.when(kk == nk - 1)
        def _():
            dx, dg_rows = _rms_bwd(h_ref[...], g_ref[...], acc_ref[...])
            dh = d2_ref[...] + dx
            dh_ref[...] = dh
            dhb_ref[...] = dh.astype(BF16)
            gn_ref[...] += jnp.sum(dg_rows, axis=0, keepdims=True)

    act = pl.BlockSpec((tm, tk), lambda i, kk: (i, kk))
    wsp = pl.BlockSpec((tk, D_MODEL), lambda i, kk: (kk, 0))
    row = pl.BlockSpec((tm, D_MODEL), lambda i, kk: (i, 0))
    vec = pl.BlockSpec((1, D_MODEL), lambda i, kk: (0, 0))
    return _call(
        body, name="ffn_bwd_in", grid=(t // tm, nk),
        in_specs=[act, act, wsp, wsp, row, row, vec],
        out_specs=[row, row, vec],
        out_shape=[jax.ShapeDtypeStruct((t, D_MODEL), F32), jax.ShapeDtypeStruct((t, D_MODEL), BF16),
                   jax.ShapeDtypeStruct((1, D_MODEL), F32)],
        scratch=[pltpu.VMEM((tm, D_MODEL), F32)],
        semantics=("arbitrary", "arbitrary"), args=(dg, du, wg_t, wu_t, h1, dh2, g2), exchange=exchange)


def _mix_bwd(dh1b, w_out, proj, b_gate, y_ret, y_att, exchange):
    t = dh1b.shape[0]
    tm, tn = min(256, t), 512
    nb = D_MODEL // tn

    def body(d_ref, w_ref, glr_ref, gla_ref, br_ref, ba_ref, yr_ref, ya_ref,
             dyr_ref, dya_ref, dglr_ref, dgla_ref, dbr_ref, dba_ref):
        @pl.when(pl.program_id(1) == 0)
        def _():
            dbr_ref[...] = jnp.zeros_like(dbr_ref)
            dba_ref[...] = jnp.zeros_like(dba_ref)

        dz = _dot_nt(d_ref[...], w_ref[...])
        gr = _sigmoid(glr_ref[...].astype(F32) + br_ref[...])
        ga = _sigmoid(gla_ref[...].astype(F32) + ba_ref[...])
        dyr_ref[...] = (dz * gr).astype(BF16)
        dya_ref[...] = (dz * ga).astype(BF16)
        dglr = dz * yr_ref[...].astype(F32) * gr * (1.0 - gr)
        dgla = dz * ya_ref[...].astype(F32) * ga * (1.0 - ga)
        dglr_ref[...] = dglr.astype(BF16)
        dgla_ref[...] = dgla.astype(BF16)
        dbr_ref[...] += jnp.sum(dglr, axis=0, keepdims=True)
        dba_ref[...] += jnp.sum(dgla, axis=0, keepdims=True)

    blk = pl.BlockSpec((tm, tn), lambda j, i: (i, j))
    vec = pl.BlockSpec((1, tn), lambda j, i: (0, j))
    return _call(
        body, name="mix_bwd", grid=(nb, t // tm),
        in_specs=[pl.BlockSpec((tm, D_MODEL), lambda j, i: (i, 0)),
                  pl.BlockSpec((tn, D_MODEL), lambda j, i: (j, 0)),
                  pl.BlockSpec((tm, tn), lambda j, i: (i, C_GL // tn + j)),
                  pl.BlockSpec((tm, tn), lambda j, i: (i, C_GL // tn + nb + j)),
                  vec, pl.BlockSpec((1, tn), lambda j, i: (0, nb + j)), blk, blk],
        out_specs=[blk, blk, blk, blk, vec, vec],
        out_shape=[jax.ShapeDtypeStruct((t, D_MODEL), BF16)] * 4 + [jax.ShapeDtypeStruct((1, D_MODEL), F32)] * 2,
        semantics=("arbitrary", "arbitrary"), args=(dh1b, w_out, proj, proj, b_gate, b_gate, y_ret, y_att),
        exchange=exchange)


def _ret_bwd(dgro, proj, o_ret, qr, kr, cs, sn, lg_arr, batch, seq, exchange):
    t = batch * seq
    nt = seq // RET_TILE

    def body(dgro_ref, rg_ref, o_ref, qr_ref, kr_ref, v_ref, cs_ref, sn_ref, lg_ref,
             dq_ref, dk_ref, dv_ref, drg_ref, do_ref, st_ref):
        lg = lg_ref[:, 0:1]
        inside, q_dec, k_dec, tile_dec = _decay(lg)

        state = jnp.zeros((RET_KEY_DIM, RET_VAL_DIM), F32)
        for i in range(nt - 1):
            rows = slice(i * RET_TILE, (i + 1) * RET_TILE)
            state = state * tile_dec + _dot_tn(_scaled(kr_ref[rows, :], k_dec), v_ref[rows, :])
            st_ref[i + 1] = state.astype(BF16)

        for i in range(nt):
            rows = slice(i * RET_TILE, (i + 1) * RET_TILE)
            o = o_ref[rows, :]
            xc = o - jnp.mean(o, axis=-1, keepdims=True)
            rs = lax.rsqrt(jnp.mean(xc * xc, axis=-1, keepdims=True) + EPS)
            nrm = xc * rs
            rg = rg_ref[rows, :].astype(F32)
            sg = _sigmoid(rg)
            dg = dgro_ref[rows, :].astype(F32)
            drg_ref[rows, :] = (dg * nrm * sg * (1.0 + rg * (1.0 - sg))).astype(BF16)
            dn = dg * rg * sg
            do = rs * (dn - jnp.mean(dn, axis=-1, keepdims=True)
                       - nrm * jnp.mean(dn * nrm, axis=-1, keepdims=True))
            do_ref[rows, :] = do.astype(BF16)

        dstate = jnp.zeros((RET_KEY_DIM, RET_VAL_DIM), F32)
        for i in reversed(range(nt)):
            rows = slice(i * RET_TILE, (i + 1) * RET_TILE)
            qi, ki, vi, doi = qr_ref[rows, :], kr_ref[rows, :], v_ref[rows, :], do_ref[rows, :]
            p = (_dot_nt(qi, ki) * inside).astype(BF16)
            dp = (_dot_nt(doi, vi) * inside).astype(BF16)
            dq = _dot(dp, ki)
            dk = _dot_tn(dp, qi)
            dv = _dot_tn(p, doi)
            if i > 0:
                dq = dq + _dot_nt(doi, st_ref[i]) * q_dec
            if i < nt - 1:
                dsb = dstate.astype(BF16)
                dk = dk + _dot_nt(vi, dsb) * k_dec
                dv = dv + _dot(_scaled(ki, k_dec), dsb)
            if i > 0:
                dstate = dstate * tile_dec + _dot_tn(_scaled(qi, q_dec), doi)
            dq_ref[rows, :] = (dq * cs_ref[rows, :] - pltpu.roll(dq, 64, 1) * sn_ref[rows, :]).astype(BF16)
            dk = (dk * cs_ref[rows, :] - pltpu.roll(dk, 64, 1) * sn_ref[rows, :]) * (RET_KEY_DIM ** -0.5)
            dk_ref[rows, :] = dk.astype(BF16)
            dv_ref[rows, :] = dv.astype(BF16)

    key = pl.BlockSpec((seq, RET_KEY_DIM), lambda b, h: (b, h))
    val = pl.BlockSpec((seq, RET_VAL_DIM), lambda b, h: (b, h))
    tab = pl.BlockSpec((seq, RET_KEY_DIM), lambda b, h: (0, 0))
    return _call(
        body, name="ret_bwd", grid=(batch, RET_HEADS),
        in_specs=[val, pl.BlockSpec((seq, RET_VAL_DIM), lambda b, h: (b, C_RG // RET_VAL_DIM + h)), val, key, key,
                  pl.BlockSpec((seq, RET_VAL_DIM), lambda b, h: (b, C_RV // RET_VAL_DIM + h)), tab, tab,
                  pl.BlockSpec((None, 1, LANES), lambda b, h: (h, 0, 0))],
        out_specs=[key, key, val, val],
        out_shape=[jax.ShapeDtypeStruct((t, RET_HEADS * RET_KEY_DIM), BF16)] * 2
                  + [jax.ShapeDtypeStruct((t, RET_HEADS * RET_VAL_DIM), BF16)] * 2,
        scratch=[pltpu.VMEM((seq, RET_VAL_DIM), BF16), pltpu.VMEM((nt, RET_KEY_DIM, RET_VAL_DIM), BF16)],
        semantics=("parallel", "parallel"), args=(dgro, proj, o_ret, qr, kr, proj, cs, sn, lg_arr),
        exchange=exchange)


def _att_bwd(proj, wvec, dao, batch, seq, exchange):
    ni, q_spec, k_spec, v_spec, w_spec, pad = _att_specs(batch, seq)
    t = batch * seq

    def body(q_ref, k_ref, v_ref, w_ref, do_ref, dq_ref, dk_ref, dv_ref, dw_ref,
             bias_ref, dbias_ref, dk_acc, dv_acc, kp_ref, vp_ref, s_ref, dp_ref, e_ref, ds_ref):
        b, i = pl.program_id(1), pl.program_id(2)

        @pl.when((b == 0) & (i == 0))
        def _():
            _att_bias(w_ref, bias_ref)
            dbias_ref[...] = jnp.zeros_like(dbias_ref)

        @pl.when(i == 0)
        def _():
            _att_pad(k_ref, kp_ref)
            _att_pad(v_ref, vp_ref)
            dk_acc[...] = jnp.zeros_like(dk_acc)
            dv_acc[...] = jnp.zeros_like(dv_acc)

        win = pl.ds(pl.multiple_of(i * ATT_Q, ATT_Q), ATT_WIN)
        k2, v2, q2, do2 = kp_ref[win, :], vp_ref[win, :], q_ref[...], do_ref[...]
        lo = lax.broadcasted_iota(jnp.int32, (1, LANES), 1) < 64
        dq = jnp.zeros((ATT_Q, LANES), F32)
        dk = jnp.zeros((ATT_WIN, LANES), F32)
        dv = jnp.zeros((ATT_WIN, LANES), F32)
        start = jnp.minimum(i, ATT_STARTS)
        for e in range(2):
            sel = lo if e == 0 else jnp.logical_not(lo)
            qm = _att_head(q2, sel)
            dom = jnp.where(sel, do2, jnp.zeros_like(do2))
            s_ref[e] = _dot_nt(qm, k2)
            dp_ref[e] = _dot_nt(dom, v2)
            rsum = []
            for c in range(ATT_Q // ATT_ROWS):
                rows = slice(c * ATT_ROWS, (c + 1) * ATT_ROWS)
                ex, r = _att_softmax_rows(s_ref.at[e], bias_ref.at[start, e], rows)
                dp = dp_ref[e, rows, :]
                mean = jnp.sum(dp * ex, axis=-1, keepdims=True) * r
                ds = ex * ((dp - mean) * r)
                dbias_ref[e, rows, :] += ds
                ds_ref[e, rows, :] = ds.astype(BF16)
                e_ref[e, rows, :] = ex.astype(BF16)
                rsum.append(r)
            dq = dq + _dot(ds_ref[e], jnp.where(sel, k2, jnp.zeros_like(k2)))
            dk = dk + _dot_tn(ds_ref[e], qm)
            dv = dv + _dot_tn(e_ref[e], (dom.astype(F32) * jnp.concatenate(rsum, axis=0)).astype(BF16))
        dq_ref[...] = (dq * 0.125).astype(BF16)
        dk_acc[win, :] += dk
        dv_acc[win, :] += dv

        @pl.when(i == ni - 1)
        def _():
            dk_ref[...] = dk_acc[ATT_PAD:, :].astype(BF16)
            dv_ref[...] = dv_acc[ATT_PAD:, :].astype(BF16)

        @pl.when((b == batch - 1) & (i == ni - 1))
        def _():
            n_i = lax.broadcasted_iota(jnp.int32, (ATT_Q, BIAS_LEN), 0)
            for e in range(2):
                xw = jnp.concatenate([jnp.zeros((ATT_Q, BIAS_LEN - ATT_WIN), F32), dbias_ref[e]], axis=1)
                for bit in range(8):
                    xw = jnp.where(((n_i >> bit) & 1) == 1, pltpu.roll(xw, BIAS_LEN - (1 << bit), 1), xw)
                dw_ref[e:e + 1, :] = jnp.sum(xw, axis=0, keepdims=True)

    seq_blk = pl.BlockSpec((seq, LANES), lambda hp, b, i: (b, hp))
    q_out = pl.BlockSpec((ATT_Q, LANES), lambda hp, b, i: (b * ni + i, hp))
    return _call(
        body, name="att_bwd", grid=(ATT_HEADS // 2, batch, ni),
        in_specs=[q_spec, k_spec, v_spec, w_spec, q_out],
        out_specs=[q_out, seq_blk, seq_blk, w_spec],
        out_shape=[jax.ShapeDtypeStruct((t, 512), BF16)] * 3
                  + [jax.ShapeDtypeStruct((ATT_HEADS // 2, 2, BIAS_LEN), F32)],
        scratch=[pltpu.VMEM((ATT_STARTS + 1, 2, ATT_Q, ATT_WIN), F32), pltpu.VMEM((2, ATT_Q, ATT_WIN), F32),
                 pltpu.VMEM((seq + ATT_PAD, LANES), F32), pltpu.VMEM((seq + ATT_PAD, LANES), F32), pad, pad,
                 pltpu.VMEM((2, ATT_Q, ATT_WIN), F32), pltpu.VMEM((2, ATT_Q, ATT_WIN), F32),
                 pltpu.VMEM((2, ATT_Q, ATT_WIN), BF16), pltpu.VMEM((2, ATT_Q, ATT_WIN), BF16)],
        semantics=("arbitrary", "arbitrary", "arbitrary"), args=(proj, proj, proj, wvec, dao), exchange=exchange)


def _rms_in_bwd(x2, dxn, dh1, g1):
    t = x2.shape[0]
    tm = min(512, t)

    def body(x_ref, d_ref, h_ref, g_ref, dx_ref, dg_ref):
        @pl.when(pl.program_id(0) == 0)
        def _():
            dg_ref[...] = jnp.zeros_like(dg_ref)

        dx, dg_rows = _rms_bwd(x_ref[...], g_ref[...], d_ref[...])
        dx_ref[...] = h_ref[...] + dx
        dg_ref[...] += jnp.sum(dg_rows, axis=0, keepdims=True)

    row = pl.BlockSpec((tm, D_MODEL), lambda i: (i, 0))
    vec = pl.BlockSpec((1, D_MODEL), lambda i: (0, 0))
    return pl.pallas_call(
        body, name="rms_in_bwd", grid=(t // tm,),
        in_specs=[row, row, row, vec], out_specs=[row, vec],
        out_shape=[jax.ShapeDtypeStruct((t, D_MODEL), F32), jax.ShapeDtypeStruct((1, D_MODEL), F32)],
        compiler_params=_params("arbitrary"),
    )(x2, dxn, dh1, g1)


def _pack_small(dg1, dbr, dba, dg2, dg3, dw):
    def body(a_ref, b_ref, c_ref, d_ref, e_ref, w_ref, o_ref):
        o_ref[...] = jnp.zeros_like(o_ref)
        for r, ref in enumerate((a_ref, b_ref, c_ref, d_ref, e_ref)):
            o_ref[r:r + 1, :] = ref[...]
        for hp in range(ATT_HEADS // 2):
            o_ref[8 + 2 * hp:10 + 2 * hp, :] = w_ref[hp]

    return pl.pallas_call(body, name="pack_small",
                          out_shape=jax.ShapeDtypeStruct((16, D_MODEL), F32))(dg1, dbr, dba, dg2, dg3, dw)


def _rotary_tables(seq):
    freqs = ROPE_BASE ** (-jnp.arange(0, RET_KEY_DIM, 2, dtype=F32) / RET_KEY_DIM)
    ang = jnp.arange(seq, dtype=F32)[:, None] * freqs[None, :]
    cos, sin = jnp.cos(ang), jnp.sin(ang)
    return jnp.concatenate([cos, cos], axis=1), jnp.concatenate([-sin, sin], axis=1)


def _bias_rows(rel_bias):
    n_far = BIAS_LEN - ATT_Q - MAX_REL + 1
    n_near = BIAS_LEN - n_far - (N_REL - 2)
    w = jnp.concatenate([jnp.broadcast_to(rel_bias[:, N_REL - 1:], (ATT_HEADS, n_far)),
                         rel_bias[:, 1:N_REL - 1][:, ::-1],
                         jnp.broadcast_to(rel_bias[:, :1], (ATT_HEADS, n_near))], axis=1)
    return w.reshape(ATT_HEADS // 2, 2, BIAS_LEN)


def _bias_rows_bwd(dw):
    n_far = BIAS_LEN - ATT_Q - MAX_REL + 1
    mid = dw[:, n_far:n_far + N_REL - 2][:, ::-1]
    return jnp.concatenate([jnp.sum(dw[:, n_far + N_REL - 2:], axis=1, keepdims=True), mid,
                            jnp.sum(dw[:, :n_far], axis=1, keepdims=True)], axis=1)


def _step(x, tgt, norm_mix, b_gate, norm_ffn, norm_final, rel_bias_shard, shard):
    batch, seq, _ = x.shape
    t = batch * seq
    n_rb = rel_bias_shard.shape[-1]
    x2, tgt2 = x.reshape(t, D_MODEL), tgt.reshape(t, D_MODEL)
    g3 = norm_final.reshape(1, D_MODEL)
    cs, sn = _rotary_tables(seq)
    lg = np.log(1.0 - 2.0 ** (-5.0 - np.arange(RET_HEADS, dtype=np.float32))).astype(np.float32)
    lg_arr = jnp.asarray(np.broadcast_to(lg[:, None, None], (RET_HEADS, 1, LANES)))

    def gather(*names):
        return _ChipGather([shard[nm] for nm in names])

    def scatter(*grads):
        return _Exchange(grads, scatter=True)

    rb_pad = jnp.pad(rel_bias_shard, ((0, 0), (0, LANES - n_rb)))
    w_in_t, rb_full = _alone(_ChipGather([shard["w_in_t"], rb_pad]), "gather_w_in")
    rb_full = rb_full.reshape(N_DEV, ATT_HEADS, LANES)[:, :, :n_rb]
    wvec = _bias_rows(jnp.transpose(rb_full, (1, 0, 2)).reshape(ATT_HEADS, N_DEV * n_rb))

    xn = _rms_fwd(x2, norm_mix)
    proj, (w_ret, w_att_t, w_out, w_gate_t) = _mm(
        xn, w_in_t, tb=True, out_dtype=BF16, tm=1024, tn=1664, tk=1024, name="proj",
        exchange=gather("w_ret", "w_att_t", "w_out", "w_gate_t"))
    (gro, o_ret, qr, kr), (w_up_t,) = _ret_fwd(proj, cs, sn, lg_arr, batch, seq, gather("w_up_t"))
    (ao,), (w_down,) = _att_fwd(proj, wvec, batch, seq, gather("w_down"))
    z, y_ret, y_att = _mix_fwd(gro, ao, proj, b_gate, w_ret, w_att_t)
    h1, hn = _out_fwd(z, x2, w_out, norm_ffn)
    g_act, u_act, a_act = _ffn_up(hn, w_gate_t, w_up_t)
    dh2, dh2b, loss, dg3 = _ffn_down_loss(a_act, h1, tgt2, w_down, g3)

    wg = dict(out_dtype=BF16, tn=1024, ta=True)
    slots = {}
    dw_down = _mm(a_act, dh2b, tm=1408, tk=1024, name="dw_down", **wg)
    (d_gact, d_uact), (slots["w_down"],) = _ffn_bwd_act(dh2b, w_down, g_act, u_act, scatter(dw_down))
    dw_gate = _mm(d_gact, hn, tm=1408, tk=1024, name="dw_gate", **wg)
    dw_up = _mm(d_uact, hn, tm=1408, tk=1024, name="dw_up", **wg)
    (dh1, dh1b, dg2), (slots["w_gate_t"],) = _ffn_bwd_in(d_gact, d_uact, w_gate_t, w_up_t, h1, dh2, norm_ffn,
                                                       scatter(dw_gate))
    dw_out = _mm(z, dh1b, tm=1024, tk=2048, name="dw_out", **wg)
    (dyr, dya, dglr, dgla, dbr, dba), _ = _mix_bwd(dh1b, w_out, proj, b_gate, y_ret, y_att, None)
    dgro = _mm(dyr, w_ret, tb=True, out_dtype=BF16, tm=1024, tn=1024, tk=1024, name="dgro")
    dao = _mm(dya, w_att_t, out_dtype=BF16, tm=1024, tn=512, tk=1024, name="dao")
    dw_ret = _mm(gro, dyr, tm=1024, tk=2048, name="dw_ret", **wg)
    dw_att = _mm(dya, ao, tm=1024, tk=2048, name="dw_att", **wg)
    (drq, drk, drv, drg), (slots["w_up_t"],) = _ret_bwd(
        dgro, proj, o_ret, qr, kr, cs, sn, lg_arr, batch, seq, scatter(dw_up))
    (daq, dak, dav, dw), (slots["w_out"], slots["w_ret"], slots["w_att_t"]) = _att_bwd(
        proj, wvec, dao, batch, seq, scatter(dw_out, dw_ret, dw_att))
    dproj = [drq, drk, drv, drg, daq, dak, dav, dglr, dgla]
    dw_in = _mm_pieces(dproj, xn, ta=True, out_dtype=BF16, tm=512, tn=1024, tk=1024, name="dw_in")
    dxn, (slots["w_in_t"],) = _mm_pieces(dproj, w_in_t, ta=False, out_dtype=F32, tm=1024, tn=1024, tk=512, name="dxn",
                                  exchange=scatter(dw_in))
    dx, dg1 = _rms_in_bwd(x2, dxn, dh1, norm_mix)
    small = _pack_small(dg1, dbr, dba, dg2, dg3, dw)
    (small_slots,) = _alone(_ChipGather([small]), "gather_small")
    return loss[0, 0], dx.reshape(batch, seq, D_MODEL), slots, small_slots.reshape(N_DEV, 16, D_MODEL)


def _sum_slots(slots, name):
    _, r, c = slots.shape
    tr = max(d for d in range(16, r + 1, 16) if r % d == 0 and (d * c <= 256 * 1024 or d == 16))

    def body(s_ref, o_ref):
        acc = s_ref[0].astype(F32)
        for s in range(1, N_DEV):
            acc = acc + s_ref[s].astype(F32)
        o_ref[...] = acc

    return pl.pallas_call(
        body, name=name, grid=(r // tr,),
        in_specs=[pl.BlockSpec((N_DEV, tr, c), lambda i: (0, i, 0))],
        out_specs=pl.BlockSpec((tr, c), lambda i: (i, 0)),
        out_shape=jax.ShapeDtypeStruct((r, c), F32),
        compiler_params=_params("parallel"),
    )(slots)


def _adamw_math(w, g, m, v):
    m = ADAM_B1 * m + (1.0 - ADAM_B1) * g
    v = ADAM_B2 * v + (1.0 - ADAM_B2) * (g * g)
    m_hat = m / (1.0 - ADAM_B1 ** ADAM_STEP)
    v_hat = v / (1.0 - ADAM_B2 ** ADAM_STEP)
    return -ADAM_LR * (m_hat / (jnp.sqrt(v_hat) + ADAM_EPS) + ADAM_WD * w), m, v


def _adamw(w, g, m, v, name):
    r, c = w.shape
    tr = r
    while tr * c > 128 * 1024 and tr % 16 == 0:
        tr //= 2

    def body(w_ref, g_ref, m_ref, v_ref, d_ref, nm_ref, nv_ref):
        d_ref[...], nm_ref[...], nv_ref[...] = _adamw_math(w_ref[...], g_ref[...], m_ref[...], v_ref[...])

    blk = pl.BlockSpec((tr, c), lambda i: (i, 0))
    return pl.pallas_call(
        body, name=name, grid=(r // tr,),
        in_specs=[blk] * 4, out_specs=[blk] * 3,
        out_shape=[jax.ShapeDtypeStruct((r, c), F32)] * 3,
        compiler_params=_params("parallel"),
    )(w, g, m, v)


def _adamw_small(ws, gs, ms, vs):
    n = len(ws)

    def body(*refs):
        for i in range(n):
            w_ref, g_ref, m_ref, v_ref = (refs[j * n + i] for j in range(4))
            d_ref, nm_ref, nv_ref = (refs[(4 + j) * n + i] for j in range(3))
            d_ref[...], nm_ref[...], nv_ref[...] = _adamw_math(w_ref[...], g_ref[...], m_ref[...], v_ref[...])

    shapes = [jax.ShapeDtypeStruct(w.shape, F32) for w in ws]
    outs = pl.pallas_call(body, name="adamw_small", out_shape=shapes * 3)(*ws, *gs, *ms, *vs)
    return outs[:n], outs[n:2 * n], outs[2 * n:]


def kernel(x, norm_mix, w_in, b_gate, rel_bias, w_ret_out, w_att_out, w_out, norm_ffn, w_ffn_gate, w_ffn_up, w_ffn_down, norm_final, loss_target, m_norm_mix, m_w_in, m_b_gate, m_rel_bias, m_w_ret_out, m_w_att_out, m_w_out, m_norm_ffn, m_w_ffn_gate, m_w_ffn_up, m_w_ffn_down, m_norm_final, v_norm_mix, v_w_in, v_b_gate, v_rel_bias, v_w_ret_out, v_w_att_out, v_w_out, v_norm_ffn, v_w_ffn_gate, v_w_ffn_up, v_w_ffn_down, v_norm_final):
    me = _index(_place())
    n_rb = rel_bias.shape[-1]

    shard = dict(w_in_t=w_in[0].T, w_gate_t=w_ffn_gate[0].T, w_up_t=w_ffn_up[0].T, w_down=w_ffn_down[0],
                 w_ret=w_ret_out[0], w_out=w_out[0], w_att_t=w_att_out[0].T)
    shard = {nm: s.astype(BF16) for nm, s in shard.items()}
    loss, dx, slots, small_slots = _step(x, loss_target, norm_mix, b_gate, norm_ffn, norm_final, rel_bias[0], shard)
    loss = lax.psum(loss, ("x", "y", "c"))
    summed = {nm: _sum_slots(s, "sum_" + nm) for nm, s in slots.items()}
    small_sum = _sum_slots(small_slots, "sum_small")

    g = dict(
        w_in=summed["w_in_t"].T, w_ffn_gate=summed["w_gate_t"].T, w_ffn_up=summed["w_up_t"].T,
        w_ffn_down=summed["w_down"], w_ret_out=summed["w_ret"], w_out=summed["w_out"], w_att_out=summed["w_att_t"].T,
        norm_mix=small_sum[0:1], b_gate=jnp.concatenate([small_sum[1:2], small_sum[2:3]], axis=1),
        norm_ffn=small_sum[3:4], norm_final=small_sum[4:5],
        rel_bias=lax.dynamic_slice_in_dim(_bias_rows_bwd(small_sum[8:16]), me * n_rb, n_rb, axis=1),
    )
    w = dict(norm_mix=norm_mix, w_in=w_in, b_gate=b_gate, rel_bias=rel_bias, w_ret_out=w_ret_out, w_att_out=w_att_out,
             w_out=w_out, norm_ffn=norm_ffn, w_ffn_gate=w_ffn_gate, w_ffn_up=w_ffn_up, w_ffn_down=w_ffn_down,
             norm_final=norm_final)
    m = dict(norm_mix=m_norm_mix, w_in=m_w_in, b_gate=m_b_gate, rel_bias=m_rel_bias, w_ret_out=m_w_ret_out,
             w_att_out=m_w_att_out, w_out=m_w_out, norm_ffn=m_norm_ffn, w_ffn_gate=m_w_ffn_gate, w_ffn_up=m_w_ffn_up,
             w_ffn_down=m_w_ffn_down, norm_final=m_norm_final)
    v = dict(norm_mix=v_norm_mix, w_in=v_w_in, b_gate=v_b_gate, rel_bias=v_rel_bias, w_ret_out=v_w_ret_out,
             w_att_out=v_w_att_out, w_out=v_w_out, norm_ffn=v_norm_ffn, w_ffn_gate=v_w_ffn_gate, w_ffn_up=v_w_ffn_up,
             w_ffn_down=v_w_ffn_down, norm_final=v_norm_final)
    order = ("norm_mix", "w_in", "b_gate", "rel_bias", "w_ret_out", "w_att_out", "w_out", "norm_ffn",
             "w_ffn_gate", "w_ffn_up", "w_ffn_down", "norm_final")
    small_names = ("norm_mix", "b_gate", "rel_bias", "norm_ffn", "norm_final")

    def flat(a):
        return a[0] if a.ndim == 3 else a.reshape(-1, a.shape[-1])

    grad, delta, new_m, new_v = {}, {}, {}, {}
    for nm in order:
        if nm not in small_names:
            d, nmom, nvar = _adamw(flat(w[nm]), g[nm], flat(m[nm]), flat(v[nm]), "adamw_" + nm)
            grad[nm], delta[nm], new_m[nm], new_v[nm] = (a.reshape(w[nm].shape) for a in (g[nm], d, nmom, nvar))
    ds, nms, nvs = _adamw_small([flat(w[nm]) for nm in small_names], [g[nm] for nm in small_names],
                                [flat(m[nm]) for nm in small_names], [flat(v[nm]) for nm in small_names])
    for i, nm in enumerate(small_names):
        grad[nm], delta[nm], new_m[nm], new_v[nm] = (a.reshape(w[nm].shape) for a in (g[nm], ds[i], nms[i], nvs[i]))

    return (loss, dx, *[grad[nm] for nm in order], *[delta[nm] for nm in order],
            *[new_m[nm] for nm in order], *[new_v[nm] for nm in order])
```

```python
import numpy as np
import jax
import jax.numpy as jnp
from jax import lax
from jax.experimental import pallas as pl
from jax.experimental.pallas import tpu as pltpu

F32 = jnp.float32
BF16 = jnp.bfloat16
MESH = pl.DeviceIdType.MESH

D_MODEL = 1024
CHUNK = 64
RET_HEADS = 4
RET_KEY_DIM = 128
RET_VAL_DIM = 256
ATT_HEADS = 8
BAND_CHUNKS = 8
MAX_REL = 256
N_REL = CHUNK + MAX_REL
D_FF = 2816
N_IN = 6656
ROPE_BASE = 10000.0
EPS = 1e-6
NEG_INF = -1e30
C_RQ, C_RK, C_RV, C_RG, C_AQ, C_AK, C_AV, C_GL = 0, 512, 1024, 2048, 3072, 3584, 4096, 4608

ADAM_LR = 0.001
ADAM_B1 = 0.9
ADAM_B2 = 0.999
ADAM_EPS = 1e-08
ADAM_WD = 0.01
ADAM_STEP = 10

N_DEV = 8
LANES = 128
RET_TILE = 256
ATT_Q = 256
ATT_PAD = BAND_CHUNKS * CHUNK
ATT_WIN = ATT_PAD + ATT_Q
ATT_STARTS = ATT_PAD // ATT_Q
ATT_ROWS = 32
BIAS_LEN = 1024
VMEM_LIMIT = 48 * 1024 * 1024


def _params(*sem):
    return pltpu.CompilerParams(dimension_semantics=sem, vmem_limit_bytes=VMEM_LIMIT)


def _dot(a, b):
    return lax.dot_general(a, b, (((1,), (0,)), ((), ())), preferred_element_type=F32)


def _dot_nt(a, b):
    return lax.dot_general(a, b, (((1,), (1,)), ((), ())), preferred_element_type=F32)


def _dot_tn(a, b):
    return lax.dot_general(a, b, (((0,), (0,)), ((), ())), preferred_element_type=F32)


def _sigmoid(x):
    return 1.0 / (1.0 + jnp.exp(-x))


def _rms_bwd(x, g, dy):
    r = lax.rsqrt(jnp.mean(x * x, axis=-1, keepdims=True) + EPS)
    u = dy * g
    dx = r * u - x * (r * r * r) * jnp.mean(u * x, axis=-1, keepdims=True)
    return dx, dy * x * r


def _place():
    return lax.axis_index("x"), lax.axis_index("y"), lax.axis_index("c")


def _peer(k):
    x, y, c = _place()
    return ((1 - x) if k & 4 else x, (1 - y) if k & 2 else y, (1 - c) if k & 1 else c)


def _index(place):
    return 4 * place[0] + 2 * place[1] + place[2]


def _rows(ref, block, nrows):
    align = 16 if ref.dtype == BF16 else 8
    return ref.at[pl.ds(pl.multiple_of(block * nrows, align), nrows)]


class _Exchange:
    def __init__(self, arrays, scatter):
        self.arrays, self.scatter, self.n = list(arrays), scatter, len(arrays)

    def out_shape(self):
        if self.scatter:
            return [jax.ShapeDtypeStruct((N_DEV, a.shape[0] // N_DEV) + a.shape[1:], a.dtype) for a in self.arrays]
        return [jax.ShapeDtypeStruct((N_DEV * a.shape[0],) + a.shape[1:], a.dtype) for a in self.arrays]

    def scratch(self):
        return [pltpu.SemaphoreType.DMA((self.n, N_DEV - 1)), pltpu.SemaphoreType.DMA((self.n, N_DEV - 1)),
                pltpu.SemaphoreType.DMA((self.n,))]

    def _copies(self, ins, outs, sems):
        send_sems, recv_sems, local_sems = sems
        me = _index(_place())

        def src(w, to):
            return _rows(ins[w], to, ins[w].shape[0] // N_DEV) if self.scatter else ins[w]

        def dst(w, origin):
            return outs[w].at[origin] if self.scatter else _rows(outs[w], origin, ins[w].shape[0])

        def remote(w, k, to, origin):
            return pltpu.make_async_remote_copy(src_ref=src(w, to), dst_ref=dst(w, origin),
                                                send_sem=send_sems.at[w, k - 1], recv_sem=recv_sems.at[w, k - 1],
                                                device_id=_peer(k), device_id_type=MESH)

        pairs = [(w, k) for w in range(self.n) for k in range(1, N_DEV)]
        own = lambda: [pltpu.make_async_copy(src(w, me), dst(w, me), local_sems.at[w]) for w in range(self.n)]
        sent = lambda: [remote(w, k, _index(_peer(k)), me) for w, k in pairs]
        arriving = lambda: [remote(w, k, me, _index(_peer(k))) for w, k in pairs]
        return own, sent, arriving

    def start(self, ins, outs, sems):
        own, sent, _ = self._copies(ins, outs, sems)
        for cp in own() + sent():
            cp.start()

    def wait(self, ins, outs, sems):
        own, sent, arriving = self._copies(ins, outs, sems)
        for cp in arriving():
            cp.wait_recv()
        for cp in sent():
            cp.wait_send()
        for cp in own():
            cp.wait()


class _ChipGather:
    def __init__(self, arrays):
        self.arrays, self.n = list(arrays), len(arrays)

    def out_shape(self):
        return [jax.ShapeDtypeStruct((N_DEV * a.shape[0],) + a.shape[1:], a.dtype) for a in self.arrays]

    def scratch(self):
        return [pltpu.SemaphoreType.DMA((self.n, N_DEV - 1)), pltpu.SemaphoreType.DMA((self.n, N_DEV - 1)),
                pltpu.SemaphoreType.DMA((self.n,))]

    def _parts(self, ins, outs, sems):
        send_sems, recv_sems, local_sems = sems
        x, y, c = _place()
        me, sibling = (x, y, c), (x, y, 1 - c)
        chips = [(1 - x, y), (x, 1 - y), (1 - x, 1 - y)]

        def rows(w, place):
            return _rows(outs[w], _index(place), ins[w].shape[0])

        def copy(w, k, block, to, own=False):
            return pltpu.make_async_remote_copy(src_ref=ins[w] if own else rows(w, block), dst_ref=rows(w, block),
                                                send_sem=send_sems.at[w, k], recv_sem=recv_sems.at[w, k],
                                                device_id=to, device_id_type=MESH)

        def local(w):
            return pltpu.make_async_copy(ins[w], rows(w, me), local_sems.at[w])

        return me, sibling, chips, c, copy, local

    def start(self, ins, outs, sems):
        me, sibling, chips, c, copy, local = self._parts(ins, outs, sems)
        for w in range(self.n):
            local(w).start()
            copy(w, 0, me, sibling, own=True).start()
            for j, chip in enumerate(chips):
                copy(w, 1 + j, me, (*chip, c), own=True).start()

    def wait(self, ins, outs, sems):
        me, sibling, chips, c, copy, local = self._parts(ins, outs, sems)
        for w in range(self.n):
            for j, chip in enumerate(chips):
                copy(w, 1 + j, (*chip, c), me).wait_recv()
                copy(w, 4 + j, (*chip, c), sibling).start()
        for w in range(self.n):
            copy(w, 0, sibling, me).wait_recv()
            for j, chip in enumerate(chips):
                copy(w, 4 + j, (*chip, 1 - c), me).wait_recv()
            copy(w, 0, me, sibling, own=True).wait_send()
            for j, chip in enumerate(chips):
                copy(w, 1 + j, me, (*chip, c), own=True).wait_send()
                copy(w, 4 + j, (*chip, c), sibling).wait_send()
            local(w).wait()


def _call(body, *, name, grid, in_specs, out_specs, out_shape, scratch=(), semantics, args, exchange=None):
    if exchange is None:
        return pl.pallas_call(body, name=name, grid=grid, in_specs=in_specs, out_specs=out_specs, out_shape=out_shape,
                              scratch_shapes=list(scratch), compiler_params=_params(*semantics))(*args), None
    n_in, n_out, n_scr, nx = len(in_specs), len(out_specs), len(scratch), exchange.n

    def full_body(*refs):
        ins, refs = refs[:n_in], refs[n_in:]
        x_in, refs = refs[:nx], refs[nx:]
        outs, refs = refs[:n_out], refs[n_out:]
        x_out, refs = refs[:nx], refs[nx:]
        scr, sems = refs[:n_scr], refs[n_scr:]
        first, last = True, True
        for axis, size in enumerate(grid):
            first = jnp.logical_and(first, pl.program_id(axis) == 0)
            last = jnp.logical_and(last, pl.program_id(axis) == size - 1)
        if grid:
            pl.when(first)(lambda: exchange.start(x_in, x_out, sems))
        else:
            exchange.start(x_in, x_out, sems)
        body(*ins, *outs, *scr)
        if grid:
            pl.when(last)(lambda: exchange.wait(x_in, x_out, sems))
        else:
            exchange.wait(x_in, x_out, sems)

    hbm = pl.BlockSpec(memory_space=pltpu.HBM)
    res = pl.pallas_call(
        full_body, name=name, grid=grid,
        in_specs=list(in_specs) + [hbm] * nx, out_specs=list(out_specs) + [hbm] * nx,
        out_shape=list(out_shape) + exchange.out_shape(),
        scratch_shapes=list(scratch) + exchange.scratch(),
        compiler_params=_params(*(["arbitrary"] * len(grid))),
    )(*args, *exchange.arrays)
    return res[:n_out], res[n_out:]


def _alone(exchange, name):
    return _call(lambda: None, name=name, grid=(), in_specs=[], out_specs=[], out_shape=[], semantics=(),
                 args=(), exchange=exchange)[1]


def _mm(a, b, *, ta=False, tb=False, out_dtype, tm, tn, tk, name, exchange=None):
    m, k = (a.shape[1], a.shape[0]) if ta else a.shape
    n = b.shape[0] if tb else b.shape[1]
    assert k == (b.shape[1] if tb else b.shape[0])
    tm, tn, tk = min(tm, m), min(tn, n), min(tk, k)
    assert m % tm == 0 and n % tn == 0 and k % tk == 0, (name, m, n, k)
    nk = k // tk
    dims = (((0 if ta else 1,), (1 if tb else 0,)), ((), ()))

    def body(a_ref, b_ref, o_ref, *acc):
        prod = lax.dot_general(a_ref[...].astype(BF16), b_ref[...].astype(BF16), dims, preferred_element_type=F32)
        if nk == 1:
            o_ref[...] = prod.astype(o_ref.dtype)
            return
        acc_ref, kk = acc[0], pl.program_id(2)

        @pl.when(kk == 0)
        def _():
            acc_ref[...] = prod

        @pl.when((kk > 0) & (kk < nk - 1))
        def _():
            acc_ref[...] += prod

        @pl.when(kk == nk - 1)
        def _():
            o_ref[...] = (acc_ref[...] + prod).astype(o_ref.dtype)

    a_spec = (pl.BlockSpec((tk, tm), lambda i, j, kk: (kk, i)) if ta
              else pl.BlockSpec((tm, tk), lambda i, j, kk: (i, kk)))
    b_spec = (pl.BlockSpec((tn, tk), lambda i, j, kk: (j, kk)) if tb
              else pl.BlockSpec((tk, tn), lambda i, j, kk: (kk, j)))
    (out,), moved = _call(
        body, name=name, grid=(m // tm, n // tn, nk),
        in_specs=[a_spec, b_spec],
        out_specs=[pl.BlockSpec((tm, tn), lambda i, j, kk: (i, j))],
        out_shape=[jax.ShapeDtypeStruct((m, n), out_dtype)],
        scratch=[pltpu.VMEM((tm, tn), F32)] if nk > 1 else [],
        semantics=("parallel", "parallel", "arbitrary"), args=(a, b), exchange=exchange)
    return out if exchange is None else (out, moved)


def _mm_pieces(pieces, b, *, ta, out_dtype, tm, tn, tk, name, exchange=None):
    rows, n = pieces[0].shape[0], b.shape[1]
    step = tm if ta else tk
    assert all(p.shape[0] == rows and p.shape[1] % step == 0 for p in pieces), name
    edges = [int(e) for e in np.cumsum([0] + [p.shape[1] // step for p in pieces])]
    total = edges[-1] * step
    m, k = (total, rows) if ta else (rows, total)
    assert b.shape[0] == k and m % tm == 0 and n % tn == 0 and k % tk == 0, name
    nk, npieces = k // tk, len(pieces)
    dims = (((0 if ta else 1,), (0,)), ((), ()))

    def body(*refs):
        a_refs, (b_ref, o_ref, acc_ref) = refs[:npieces], refs[npieces:]
        kk = pl.program_id(2)
        pos = pl.program_id(0) if ta else kk

        @pl.when(kk == 0)
        def _():
            acc_ref[...] = jnp.zeros_like(acc_ref)

        for p, a_ref in enumerate(a_refs):
            @pl.when((pos >= edges[p]) & (pos < edges[p + 1]))
            def _(a_ref=a_ref):
                acc_ref[...] += lax.dot_general(a_ref[...], b_ref[...], dims, preferred_element_type=F32)

        @pl.when(kk == nk - 1)
        def _():
            o_ref[...] = acc_ref[...].astype(o_ref.dtype)

    def a_spec(p):
        lo, last = edges[p], edges[p + 1] - edges[p] - 1
        if ta:
            def index(i, j, kk):
                inside = (i >= lo) & (i <= lo + last)
                return jnp.where(inside, kk, 0), jnp.clip(i - lo, 0, last)
            return pl.BlockSpec((tk, tm), index)
        return pl.BlockSpec((tm, tk), lambda i, j, kk: (i, jnp.clip(kk - lo, 0, last)))

    (out,), moved = _call(
        body, name=name, grid=(m // tm, n // tn, nk),
        in_specs=[a_spec(p) for p in range(npieces)] + [pl.BlockSpec((tk, tn), lambda i, j, kk: (kk, j))],
        out_specs=[pl.BlockSpec((tm, tn), lambda i, j, kk: (i, j))],
        out_shape=[jax.ShapeDtypeStruct((m, n), out_dtype)],
        scratch=[pltpu.VMEM((tm, tn), F32)],
        semantics=("parallel", "parallel", "arbitrary"), args=(*pieces, b), exchange=exchange)
    return out if exchange is None else (out, moved)


def _rms_fwd(x2, g):
    t = x2.shape[0]
    tm = min(512, t)

    def body(x_ref, g_ref, o_ref):
        x = x_ref[...]
        r = lax.rsqrt(jnp.mean(x * x, axis=-1, keepdims=True) + EPS)
        o_ref[...] = (x * r * g_ref[...]).astype(o_ref.dtype)

    return pl.pallas_call(
        body, name="rms_in_fwd", grid=(t // tm,),
        in_specs=[pl.BlockSpec((tm, D_MODEL), lambda i: (i, 0)), pl.BlockSpec((1, D_MODEL), lambda i: (0, 0))],
        out_specs=pl.BlockSpec((tm, D_MODEL), lambda i: (i, 0)),
        out_shape=jax.ShapeDtypeStruct((t, D_MODEL), BF16),
        compiler_params=_params("parallel"),
    )(x2, g)


def _decay(lg):
    row = lax.broadcasted_iota(jnp.int32, (RET_TILE, RET_TILE), 0)
    col = lax.broadcasted_iota(jnp.int32, (RET_TILE, RET_TILE), 1)
    within = jnp.exp(lg * jnp.abs(row - col).astype(F32))
    inside = jnp.where((col >> 6) <= (row >> 6), within, 0.0)
    pos = lax.broadcasted_iota(jnp.int32, (RET_TILE, 1), 0).astype(F32)
    q_dec = jnp.exp(lg * (pos + 1.0))
    k_dec = jnp.exp(lg * (RET_TILE - 1.0 - pos))
    tile_dec = jnp.exp(lg * float(RET_TILE))
    return inside, q_dec, k_dec, tile_dec


def _scaled(a_bf16, dec):
    return (a_bf16.astype(F32) * dec).astype(BF16)


def _ret_fwd(proj, cs, sn, lg_arr, batch, seq, exchange):
    t = batch * seq
    nt = seq // RET_TILE

    def body(q_ref, k_ref, v_ref, rg_ref, cs_ref, sn_ref, lg_ref, gro_ref, o_ref, qr_ref, kr_ref):
        lg = lg_ref[:, 0:1]
        cs_t, sn_t = cs_ref[...], sn_ref[...]
        q = q_ref[...].astype(F32)
        k = k_ref[...].astype(F32)
        qr_ref[...] = (q * cs_t + pltpu.roll(q, 64, 1) * sn_t).astype(BF16)
        kr_ref[...] = ((k * cs_t + pltpu.roll(k, 64, 1) * sn_t) * (RET_KEY_DIM ** -0.5)).astype(BF16)
        inside, q_dec, k_dec, tile_dec = _decay(lg)
        state = jnp.zeros((RET_KEY_DIM, RET_VAL_DIM), F32)
        for i in range(nt):
            rows = slice(i * RET_TILE, (i + 1) * RET_TILE)
            qi, ki, vi = qr_ref[rows, :], kr_ref[rows, :], v_ref[rows, :]
            acc = _dot((_dot_nt(qi, ki) * inside).astype(BF16), vi)
            if i > 0:
                acc = acc + _dot(_scaled(qi, q_dec), state.astype(BF16))
            if i < nt - 1:
                state = state * tile_dec + _dot_tn(_scaled(ki, k_dec), vi)
            o_ref[rows, :] = acc
            xc = acc - jnp.mean(acc, axis=-1, keepdims=True)
            nrm = xc * lax.rsqrt(jnp.mean(xc * xc, axis=-1, keepdims=True) + EPS)
            rg = rg_ref[rows, :].astype(F32)
            gro_ref[rows, :] = (rg * _sigmoid(rg) * nrm).astype(BF16)

    def col(base, width):
        return lambda b, h: (b, base // width + h)

    return _call(
        body, name="ret_fwd", grid=(batch, RET_HEADS),
        in_specs=[pl.BlockSpec((seq, RET_KEY_DIM), col(C_RQ, RET_KEY_DIM)),
                  pl.BlockSpec((seq, RET_KEY_DIM), col(C_RK, RET_KEY_DIM)),
                  pl.BlockSpec((seq, RET_VAL_DIM), col(C_RV, RET_VAL_DIM)),
                  pl.BlockSpec((seq, RET_VAL_DIM), col(C_RG, RET_VAL_DIM)),
                  pl.BlockSpec((seq, RET_KEY_DIM), lambda b, h: (0, 0)),
                  pl.BlockSpec((seq, RET_KEY_DIM), lambda b, h: (0, 0)),
                  pl.BlockSpec((None, 1, LANES), lambda b, h: (h, 0, 0))],
        out_specs=[pl.BlockSpec((seq, RET_VAL_DIM), lambda b, h: (b, h)),
                   pl.BlockSpec((seq, RET_VAL_DIM), lambda b, h: (b, h)),
                   pl.BlockSpec((seq, RET_KEY_DIM), lambda b, h: (b, h)),
                   pl.BlockSpec((seq, RET_KEY_DIM), lambda b, h: (b, h))],
        out_shape=[jax.ShapeDtypeStruct((t, RET_HEADS * RET_VAL_DIM), BF16),
                   jax.ShapeDtypeStruct((t, RET_HEADS * RET_VAL_DIM), F32),
                   jax.ShapeDtypeStruct((t, RET_HEADS * RET_KEY_DIM), BF16),
                   jax.ShapeDtypeStruct((t, RET_HEADS * RET_KEY_DIM), BF16)],
        semantics=("parallel", "parallel"), args=(proj, proj, proj, proj, cs, sn, lg_arr), exchange=exchange)


def _att_bias(w_ref, bias_ref):
    n_i = lax.broadcasted_iota(jnp.int32, (ATT_Q, BIAS_LEN), 0)
    qc = lax.broadcasted_iota(jnp.int32, (ATT_Q, ATT_WIN), 0) >> 6
    kc = lax.broadcasted_iota(jnp.int32, (ATT_Q, ATT_WIN), 1) >> 6
    dc = qc + BAND_CHUNKS - kc
    band = (dc >= 0) & (dc <= BAND_CHUNKS)
    key = lax.broadcasted_iota(jnp.int32, (ATT_Q, ATT_WIN), 1)
    for e in range(2):
        xw = jnp.broadcast_to(w_ref[e:e + 1, :], (ATT_Q, BIAS_LEN))
        for bit in range(8):
            xw = jnp.where(((n_i >> bit) & 1) == 1, pltpu.roll(xw, 1 << bit, 1), xw)
        bias = jnp.where(band, xw[:, BIAS_LEN - ATT_WIN:], NEG_INF)
        for first in range(ATT_STARTS):
            bias_ref[first, e] = jnp.where(key + (first * ATT_Q - ATT_PAD) >= 0, bias, NEG_INF)
        bias_ref[ATT_STARTS, e] = bias


def _att_specs(batch, seq):
    ni = seq // ATT_Q
    q_spec = pl.BlockSpec((ATT_Q, LANES), lambda hp, b, i: (b * ni + i, C_AQ // LANES + hp))
    k_spec = pl.BlockSpec((seq, LANES), lambda hp, b, i: (b, C_AK // LANES + hp))
    v_spec = pl.BlockSpec((seq, LANES), lambda hp, b, i: (b, C_AV // LANES + hp))
    w_spec = pl.BlockSpec((None, 2, BIAS_LEN), lambda hp, b, i: (hp, 0, 0))
    pad = pltpu.VMEM((seq + ATT_PAD, LANES), BF16)
    return ni, q_spec, k_spec, v_spec, w_spec, pad


def _att_pad(src_ref, pad_ref):
    pad_ref[:ATT_PAD, :] = jnp.zeros((ATT_PAD, LANES), BF16)
    pad_ref[ATT_PAD:, :] = src_ref[...]


def _att_head(q2, sel):
    return jnp.where(sel, q2, jnp.zeros_like(q2)) * 0.125


def _att_softmax_rows(s_ref, bias_ref, rows):
    s = s_ref[rows, :] + bias_ref[rows, :]
    ex = jnp.exp(s - jnp.max(s, axis=-1, keepdims=True))
    return ex, 1.0 / jnp.sum(ex, axis=-1, keepdims=True)


def _att_fwd(proj, wvec, batch, seq, exchange):
    ni, q_spec, k_spec, v_spec, w_spec, pad = _att_specs(batch, seq)

    def body(q_ref, k_ref, v_ref, w_ref, o_ref, bias_ref, kp_ref, vp_ref, s_ref, e_ref):
        b, i = pl.program_id(1), pl.program_id(2)

        @pl.when((b == 0) & (i == 0))
        def _():
            _att_bias(w_ref, bias_ref)

        @pl.when(i == 0)
        def _():
            _att_pad(k_ref, kp_ref)
            _att_pad(v_ref, vp_ref)

        win = pl.ds(pl.multiple_of(i * ATT_Q, ATT_Q), ATT_WIN)
        k2, v2, q2 = kp_ref[win, :], vp_ref[win, :], q_ref[...]
        lo = lax.broadcasted_iota(jnp.int32, (1, LANES), 1) < 64
        start = jnp.minimum(i, ATT_STARTS)
        out = jnp.zeros((ATT_Q, LANES), F32)
        for e in range(2):
            sel = lo if e == 0 else jnp.logical_not(lo)
            s_ref[e] = _dot_nt(_att_head(q2, sel), k2)
            rsum = []
            for c in range(ATT_Q // ATT_ROWS):
                rows = slice(c * ATT_ROWS, (c + 1) * ATT_ROWS)
                ex, r = _att_softmax_rows(s_ref.at[e], bias_ref.at[start, e], rows)
                e_ref[e, rows, :] = ex.astype(BF16)
                rsum.append(r)
            out = out + _dot(e_ref[e], jnp.where(sel, v2, jnp.zeros_like(v2))) * jnp.concatenate(rsum, axis=0)
        o_ref[...] = out.astype(BF16)

    return _call(
        body, name="att_fwd", grid=(ATT_HEADS // 2, batch, ni),
        in_specs=[q_spec, k_spec, v_spec, w_spec],
        out_specs=[pl.BlockSpec((ATT_Q, LANES), lambda hp, b, i: (b * ni + i, hp))],
        out_shape=[jax.ShapeDtypeStruct((batch * seq, ATT_HEADS * 64), BF16)],
        scratch=[pltpu.VMEM((ATT_STARTS + 1, 2, ATT_Q, ATT_WIN), F32), pad, pad,
                 pltpu.VMEM((2, ATT_Q, ATT_WIN), F32), pltpu.VMEM((2, ATT_Q, ATT_WIN), BF16)],
        semantics=("arbitrary", "arbitrary", "arbitrary"), args=(proj, proj, proj, wvec), exchange=exchange)


def _mix_fwd(gro, ao, proj, b_gate, w_ret, w_att_t):
    t = gro.shape[0]
    tm, tn = min(256, t), 512

    def body(gro_ref, ao_ref, glr_ref, gla_ref, br_ref, ba_ref, wr_ref, wa_ref, z_ref, yr_ref, ya_ref):
        yr = _dot(gro_ref[...], wr_ref[...])
        ya = _dot_nt(ao_ref[...], wa_ref[...])
        gr = _sigmoid(glr_ref[...].astype(F32) + br_ref[...])
        ga = _sigmoid(gla_ref[...].astype(F32) + ba_ref[...])
        z_ref[...] = (gr * yr + ga * ya).astype(BF16)
        yr_ref[...] = yr.astype(BF16)
        ya_ref[...] = ya.astype(BF16)

    nb = D_MODEL // tn
    out = pl.BlockSpec((tm, tn), lambda i, j: (i, j))
    return pl.pallas_call(
        body, name="mix_fwd", grid=(t // tm, nb),
        in_specs=[pl.BlockSpec((tm, D_MODEL), lambda i, j: (i, 0)),
                  pl.BlockSpec((tm, 512), lambda i, j: (i, 0)),
                  pl.BlockSpec((tm, tn), lambda i, j: (i, C_GL // tn + j)),
                  pl.BlockSpec((tm, tn), lambda i, j: (i, C_GL // tn + nb + j)),
                  pl.BlockSpec((1, tn), lambda i, j: (0, j)),
                  pl.BlockSpec((1, tn), lambda i, j: (0, nb + j)),
                  pl.BlockSpec((D_MODEL, tn), lambda i, j: (0, j)),
                  pl.BlockSpec((tn, 512), lambda i, j: (j, 0))],
        out_specs=[out, out, out],
        out_shape=[jax.ShapeDtypeStruct((t, D_MODEL), BF16)] * 3,
        compiler_params=_params("parallel", "parallel"),
    )(gro, ao, proj, proj, b_gate, b_gate, w_ret, w_att_t)


def _out_fwd(z, x2, w_out, g2):
    t = z.shape[0]
    tm = min(256, t)

    def body(z_ref, x_ref, w_ref, g_ref, h_ref, hn_ref):
        h = x_ref[...] + _dot(z_ref[...], w_ref[...])
        h_ref[...] = h
        r = lax.rsqrt(jnp.mean(h * h, axis=-1, keepdims=True) + EPS)
        hn_ref[...] = (h * r * g_ref[...]).astype(BF16)

    row = pl.BlockSpec((tm, D_MODEL), lambda i: (i, 0))
    return pl.pallas_call(
        body, name="out_fwd", grid=(t // tm,),
        in_specs=[row, row, pl.BlockSpec((D_MODEL, D_MODEL), lambda i: (0, 0)),
                  pl.BlockSpec((1, D_MODEL), lambda i: (0, 0))],
        out_specs=[row, row],
        out_shape=[jax.ShapeDtypeStruct((t, D_MODEL), F32), jax.ShapeDtypeStruct((t, D_MODEL), BF16)],
        compiler_params=_params("parallel"),
    )(z, x2, w_out, g2)


def _ffn_up(hn, wg_t, wu_t):
    t = hn.shape[0]
    tm, tn = min(512, t), D_FF // 2

    def body(h_ref, wg_ref, wu_ref, g_ref, u_ref, a_ref):
        g = _dot_nt(h_ref[...], wg_ref[...])
        u = _dot_nt(h_ref[...], wu_ref[...])
        g_ref[...] = g.astype(BF16)
        u_ref[...] = u.astype(BF16)
        a_ref[...] = (g * _sigmoid(g) * u).astype(BF16)

    w_spec = pl.BlockSpec((tn, D_MODEL), lambda j, i: (j, 0))
    out = pl.BlockSpec((tm, tn), lambda j, i: (i, j))
    return pl.pallas_call(
        body, name="ffn_up", grid=(D_FF // tn, t // tm),
        in_specs=[pl.BlockSpec((tm, D_MODEL), lambda j, i: (i, 0)), w_spec, w_spec],
        out_specs=[out, out, out],
        out_shape=[jax.ShapeDtypeStruct((t, D_FF), BF16)] * 3,
        compiler_params=_params("parallel", "parallel"),
    )(hn, wg_t, wu_t)


def _ffn_down_loss(a, h1, tgt, w_down, g3):
    t = a.shape[0]
    tm = min(512, t)

    def body(a_ref, h_ref, t_ref, w_ref, g_ref, dh_ref, dhb_ref, loss_ref, dg_ref):
        @pl.when(pl.program_id(0) == 0)
        def _():
            loss_ref[...] = jnp.zeros_like(loss_ref)
            dg_ref[...] = jnp.zeros_like(dg_ref)

        g = g_ref[...]
        h2 = h_ref[...] + _dot(a_ref[...], w_ref[...])
        r = lax.rsqrt(jnp.mean(h2 * h2, axis=-1, keepdims=True) + EPS)
        err = h2 * r * g - t_ref[...]
        loss_ref[...] += jnp.sum(err * err) * (0.5 / D_MODEL)
        dy = err * (1.0 / D_MODEL)
        dh, dg_rows = _rms_bwd(h2, g, dy)
        dg_ref[...] += jnp.sum(dg_rows, axis=0, keepdims=True)
        dh_ref[...] = dh
        dhb_ref[...] = dh.astype(BF16)

    row = pl.BlockSpec((tm, D_MODEL), lambda i: (i, 0))
    vec = pl.BlockSpec((1, D_MODEL), lambda i: (0, 0))
    return pl.pallas_call(
        body, name="ffn_down_loss", grid=(t // tm,),
        in_specs=[pl.BlockSpec((tm, D_FF), lambda i: (i, 0)), row, row,
                  pl.BlockSpec((D_FF, D_MODEL), lambda i: (0, 0)), vec],
        out_specs=[row, row, pl.BlockSpec((1, LANES), lambda i: (0, 0)), vec],
        out_shape=[jax.ShapeDtypeStruct((t, D_MODEL), F32), jax.ShapeDtypeStruct((t, D_MODEL), BF16),
                   jax.ShapeDtypeStruct((1, LANES), F32), jax.ShapeDtypeStruct((1, D_MODEL), F32)],
        compiler_params=_params("arbitrary"),
    )(a, h1, tgt, w_down, g3)


def _ffn_bwd_act(dh2b, w_down, g_act, u_act, exchange):
    t = dh2b.shape[0]
    tm, tn = min(512, t), D_FF // 2

    def body(d_ref, w_ref, g_ref, u_ref, dg_ref, du_ref):
        da = _dot_nt(d_ref[...], w_ref[...])
        g = g_ref[...].astype(F32)
        u = u_ref[...].astype(F32)
        sg = _sigmoid(g)
        dg_ref[...] = (da * u * sg * (1.0 + g * (1.0 - sg))).astype(BF16)
        du_ref[...] = (da * g * sg).astype(BF16)

    blk = pl.BlockSpec((tm, tn), lambda j, i: (i, j))
    return _call(
        body, name="ffn_bwd_act", grid=(D_FF // tn, t // tm),
        in_specs=[pl.BlockSpec((tm, D_MODEL), lambda j, i: (i, 0)),
                  pl.BlockSpec((tn, D_MODEL), lambda j, i: (j, 0)), blk, blk],
        out_specs=[blk, blk],
        out_shape=[jax.ShapeDtypeStruct((t, D_FF), BF16)] * 2,
        semantics=("parallel", "parallel"), args=(dh2b, w_down, g_act, u_act), exchange=exchange)


def _ffn_bwd_in(dg, du, wg_t, wu_t, h1, dh2, g2, exchange):
    t = dg.shape[0]
    tm, tk = min(512, t), D_FF // 2
    nk = D_FF // tk

    def body(dg_ref, du_ref, wg_ref, wu_ref, h_ref, d2_ref, g_ref, dh_ref, dhb_ref, gn_ref, acc_ref):
        i, kk = pl.program_id(0), pl.program_id(1)

        @pl.when((i == 0) & (kk == 0))
        def _():
            gn_ref[...] = jnp.zeros_like(gn_ref)

        @pl.when(kk == 0)
        def _():
            acc_ref[...] = jnp.zeros_like(acc_ref)

        acc_ref[...] += _dot(dg_ref[...], wg_ref[...]) + _dot(du_ref[...], wu_ref[...])

        @pl.when(kk == nk - 1)
        def _():
            dx, dg_rows = _rms_bwd(h_ref[...], g_ref[...], acc_ref[...])
            dh = d2_ref[...] + dx
            dh_ref[...] = dh
            dhb_ref[...] = dh.astype(BF16)
            gn_ref[...] += jnp.sum(dg_rows, axis=0, keepdims=True)

    act = pl.BlockSpec((tm, tk), lambda i, kk: (i, kk))
    wsp = pl.BlockSpec((tk, D_MODEL), lambda i, kk: (kk, 0))
    row = pl.BlockSpec((tm, D_MODEL), lambda i, kk: (i, 0))
    vec = pl.BlockSpec((1, D_MODEL), lambda i, kk: (0, 0))
    return _call(
        body, name="ffn_bwd_in", grid=(t // tm, nk),
        in_specs=[act, act, wsp, wsp, row, row, vec],
        out_specs=[row, row, vec],
        out_shape=[jax.ShapeDtypeStruct((t, D_MODEL), F32), jax.ShapeDtypeStruct((t, D_MODEL), BF16),
                   jax.ShapeDtypeStruct((1, D_MODEL), F32)],
        scratch=[pltpu.VMEM((tm, D_MODEL), F32)],
        semantics=("arbitrary", "arbitrary"), args=(dg, du, wg_t, wu_t, h1, dh2, g2), exchange=exchange)


def _mix_bwd(dh1b, w_out, proj, b_gate, y_ret, y_att, exchange):
    t = dh1b.shape[0]
    tm, tn = min(256, t), 512
    nb = D_MODEL // tn

    def body(d_ref, w_ref, glr_ref, gla_ref, br_ref, ba_ref, yr_ref, ya_ref,
             dyr_ref, dya_ref, dglr_ref, dgla_ref, dbr_ref, dba_ref):
        @pl.when(pl.program_id(1) == 0)
        def _():
            dbr_ref[...] = jnp.zeros_like(dbr_ref)
            dba_ref[...] = jnp.zeros_like(dba_ref)

        dz = _dot_nt(d_ref[...], w_ref[...])
        gr = _sigmoid(glr_ref[...].astype(F32) + br_ref[...])
        ga = _sigmoid(gla_ref[...].astype(F32) + ba_ref[...])
        dyr_ref[...] = (dz * gr).astype(BF16)
        dya_ref[...] = (dz * ga).astype(BF16)
        dglr = dz * yr_ref[...].astype(F32) * gr * (1.0 - gr)
        dgla = dz * ya_ref[...].astype(F32) * ga * (1.0 - ga)
        dglr_ref[...] = dglr.astype(BF16)
        dgla_ref[...] = dgla.astype(BF16)
        dbr_ref[...] += jnp.sum(dglr, axis=0, keepdims=True)
        dba_ref[...] += jnp.sum(dgla, axis=0, keepdims=True)

    blk = pl.BlockSpec((tm, tn), lambda j, i: (i, j))
    vec = pl.BlockSpec((1, tn), lambda j, i: (0, j))
    return _call(
        body, name="mix_bwd", grid=(nb, t // tm),
        in_specs=[pl.BlockSpec((tm, D_MODEL), lambda j, i: (i, 0)),
                  pl.BlockSpec((tn, D_MODEL), lambda j, i: (j, 0)),
                  pl.BlockSpec((tm, tn), lambda j, i: (i, C_GL // tn + j)),
                  pl.BlockSpec((tm, tn), lambda j, i: (i, C_GL // tn + nb + j)),
                  vec, pl.BlockSpec((1, tn), lambda j, i: (0, nb + j)), blk, blk],
        out_specs=[blk, blk, blk, blk, vec, vec],
        out_shape=[jax.ShapeDtypeStruct((t, D_MODEL), BF16)] * 4 + [jax.ShapeDtypeStruct((1, D_MODEL), F32)] * 2,
        semantics=("arbitrary", "arbitrary"), args=(dh1b, w_out, proj, proj, b_gate, b_gate, y_ret, y_att),
        exchange=exchange)


def _ret_bwd(dgro, proj, o_ret, qr, kr, cs, sn, lg_arr, batch, seq, exchange):
    t = batch * seq
    nt = seq // RET_TILE

    def body(dgro_ref, rg_ref, o_ref, qr_ref, kr_ref, v_ref, cs_ref, sn_ref, lg_ref,
             dq_ref, dk_ref, dv_ref, drg_ref, do_ref, st_ref):
        lg = lg_ref[:, 0:1]
        inside, q_dec, k_dec, tile_dec = _decay(lg)

        state = jnp.zeros((RET_KEY_DIM, RET_VAL_DIM), F32)
        for i in range(nt - 1):
            rows = slice(i * RET_TILE, (i + 1) * RET_TILE)
            state = state * tile_dec + _dot_tn(_scaled(kr_ref[rows, :], k_dec), v_ref[rows, :])
            st_ref[i + 1] = state.astype(BF16)

        for i in range(nt):
            rows = slice(i * RET_TILE, (i + 1) * RET_TILE)
            o = o_ref[rows, :]
            xc = o - jnp.mean(o, axis=-1, keepdims=True)
            rs = lax.rsqrt(jnp.mean(xc * xc, axis=-1, keepdims=True) + EPS)
            nrm = xc * rs
            rg = rg_ref[rows, :].astype(F32)
            sg = _sigmoid(rg)
            dg = dgro_ref[rows, :].astype(F32)
            drg_ref[rows, :] = (dg * nrm * sg * (1.0 + rg * (1.0 - sg))).astype(BF16)
            dn = dg * rg * sg
            do = rs * (dn - jnp.mean(dn, axis=-1, keepdims=True)
                       - nrm * jnp.mean(dn * nrm, axis=-1, keepdims=True))
            do_ref[rows, :] = do.astype(BF16)

        dstate = jnp.zeros((RET_KEY_DIM, RET_VAL_DIM), F32)
        for i in reversed(range(nt)):
            rows = slice(i * RET_TILE, (i + 1) * RET_TILE)
            qi, ki, vi, doi = qr_ref[rows, :], kr_ref[rows, :], v_ref[rows, :], do_ref[rows, :]
            p = (_dot_nt(qi, ki) * inside).astype(BF16)
            dp = (_dot_nt(doi, vi) * inside).astype(BF16)
            dq = _dot(dp, ki)
            dk = _dot_tn(dp, qi)
            dv = _dot_tn(p, doi)
            if i > 0:
                dq = dq + _dot_nt(doi, st_ref[i]) * q_dec
            if i < nt - 1:
                dsb = dstate.astype(BF16)
                dk = dk + _dot_nt(vi, dsb) * k_dec
                dv = dv + _dot(_scaled(ki, k_dec), dsb)
            if i > 0:
                dstate = dstate * tile_dec + _dot_tn(_scaled(qi, q_dec), doi)
            dq_ref[rows, :] = (dq * cs_ref[rows, :] - pltpu.roll(dq, 64, 1) * sn_ref[rows, :]).astype(BF16)
            dk = (dk * cs_ref[rows, :] - pltpu.roll(dk, 64, 1) * sn_ref[rows, :]) * (RET_KEY_DIM ** -0.5)
            dk_ref[rows, :] = dk.astype(BF16)
            dv_ref[rows, :] = dv.astype(BF16)

    key = pl.BlockSpec((seq, RET_KEY_DIM), lambda b, h: (b, h))
    val = pl.BlockSpec((seq, RET_VAL_DIM), lambda b, h: (b, h))
    tab = pl.BlockSpec((seq, RET_KEY_DIM), lambda b, h: (0, 0))
    return _call(
        body, name="ret_bwd", grid=(batch, RET_HEADS),
        in_specs=[val, pl.BlockSpec((seq, RET_VAL_DIM), lambda b, h: (b, C_RG // RET_VAL_DIM + h)), val, key, key,
                  pl.BlockSpec((seq, RET_VAL_DIM), lambda b, h: (b, C_RV // RET_VAL_DIM + h)), tab, tab,
                  pl.BlockSpec((None, 1, LANES), lambda b, h: (h, 0, 0))],
        out_specs=[key, key, val, val],
        out_shape=[jax.ShapeDtypeStruct((t, RET_HEADS * RET_KEY_DIM), BF16)] * 2
                  + [jax.ShapeDtypeStruct((t, RET_HEADS * RET_VAL_DIM), BF16)] * 2,
        scratch=[pltpu.VMEM((seq, RET_VAL_DIM), BF16), pltpu.VMEM((nt, RET_KEY_DIM, RET_VAL_DIM), BF16)],
        semantics=("parallel", "parallel"), args=(dgro, proj, o_ret, qr, kr, proj, cs, sn, lg_arr),
        exchange=exchange)


def _att_bwd(proj, wvec, dao, batch, seq, exchange):
    ni, q_spec, k_spec, v_spec, w_spec, pad = _att_specs(batch, seq)
    t = batch * seq

    def body(q_ref, k_ref, v_ref, w_ref, do_ref, dq_ref, dk_ref, dv_ref, dw_ref,
             bias_ref, dbias_ref, dk_acc, dv_acc, kp_ref, vp_ref, s_ref, dp_ref, e_ref, ds_ref):
        b, i = pl.program_id(1), pl.program_id(2)

        @pl.when((b == 0) & (i == 0))
        def _():
            _att_bias(w_ref, bias_ref)
            dbias_ref[...] = jnp.zeros_like(dbias_ref)

        @pl.when(i == 0)
        def _():
            _att_pad(k_ref, kp_ref)
            _att_pad(v_ref, vp_ref)
            dk_acc[...] = jnp.zeros_like(dk_acc)
            dv_acc[...] = jnp.zeros_like(dv_acc)

        win = pl.ds(pl.multiple_of(i * ATT_Q, ATT_Q), ATT_WIN)
        k2, v2, q2, do2 = kp_ref[win, :], vp_ref[win, :], q_ref[...], do_ref[...]
        lo = lax.broadcasted_iota(jnp.int32, (1, LANES), 1) < 64
        dq = jnp.zeros((ATT_Q, LANES), F32)
        dk = jnp.zeros((ATT_WIN, LANES), F32)
        dv = jnp.zeros((ATT_WIN, LANES), F32)
        start = jnp.minimum(i, ATT_STARTS)
        for e in range(2):
            sel = lo if e == 0 else jnp.logical_not(lo)
            qm = _att_head(q2, sel)
            dom = jnp.where(sel, do2, jnp.zeros_like(do2))
            s_ref[e] = _dot_nt(qm, k2)
            dp_ref[e] = _dot_nt(dom, v2)
            rsum = []
            for c in range(ATT_Q // ATT_ROWS):
                rows = slice(c * ATT_ROWS, (c + 1) * ATT_ROWS)
                ex, r = _att_softmax_rows(s_ref.at[e], bias_ref.at[start, e], rows)
                dp = dp_ref[e, rows, :]
                mean = jnp.sum(dp * ex, axis=-1, keepdims=True) * r
                ds = ex * ((dp - mean) * r)
                dbias_ref[e, rows, :] += ds
                ds_ref[e, rows, :] = ds.astype(BF16)
                e_ref[e, rows, :] = ex.astype(BF16)
                rsum.append(r)
            dq = dq + _dot(ds_ref[e], jnp.where(sel, k2, jnp.zeros_like(k2)))
            dk = dk + _dot_tn(ds_ref[e], qm)
            dv = dv + _dot_tn(e_ref[e], (dom.astype(F32) * jnp.concatenate(rsum, axis=0)).astype(BF16))
        dq_ref[...] = (dq * 0.125).astype(BF16)
        dk_acc[win, :] += dk
        dv_acc[win, :] += dv

        @pl.when(i == ni - 1)
        def _():
            dk_ref[...] = dk_acc[ATT_PAD:, :].astype(BF16)
            dv_ref[...] = dv_acc[ATT_PAD:, :].astype(BF16)

        @pl.when((b == batch - 1) & (i == ni - 1))
        def _():
            n_i = lax.broadcasted_iota(jnp.int32, (ATT_Q, BIAS_LEN), 0)
            for e in range(2):
                xw = jnp.concatenate([jnp.zeros((ATT_Q, BIAS_LEN - ATT_WIN), F32), dbias_ref[e]], axis=1)
                for bit in range(8):
                    xw = jnp.where(((n_i >> bit) & 1) == 1, pltpu.roll(xw, BIAS_LEN - (1 << bit), 1), xw)
                dw_ref[e:e + 1, :] = jnp.sum(xw, axis=0, keepdims=True)

    seq_blk = pl.BlockSpec((seq, LANES), lambda hp, b, i: (b, hp))
    q_out = pl.BlockSpec((ATT_Q, LANES), lambda hp, b, i: (b * ni + i, hp))
    return _call(
        body, name="att_bwd", grid=(ATT_HEADS // 2, batch, ni),
        in_specs=[q_spec, k_spec, v_spec, w_spec, q_out],
        out_specs=[q_out, seq_blk, seq_blk, w_spec],
        out_shape=[jax.ShapeDtypeStruct((t, 512), BF16)] * 3
                  + [jax.ShapeDtypeStruct((ATT_HEADS // 2, 2, BIAS_LEN), F32)],
        scratch=[pltpu.VMEM((ATT_STARTS + 1, 2, ATT_Q, ATT_WIN), F32), pltpu.VMEM((2, ATT_Q, ATT_WIN), F32),
                 pltpu.VMEM((seq + ATT_PAD, LANES), F32), pltpu.VMEM((seq + ATT_PAD, LANES), F32), pad, pad,
                 pltpu.VMEM((2, ATT_Q, ATT_WIN), F32), pltpu.VMEM((2, ATT_Q, ATT_WIN), F32),
                 pltpu.VMEM((2, ATT_Q, ATT_WIN), BF16), pltpu.VMEM((2, ATT_Q, ATT_WIN), BF16)],
        semantics=("arbitrary", "arbitrary", "arbitrary"), args=(proj, proj, proj, wvec, dao), exchange=exchange)


def _rms_in_bwd(x2, dxn, dh1, g1):
    t = x2.shape[0]
    tm = min(512, t)

    def body(x_ref, d_ref, h_ref, g_ref, dx_ref, dg_ref):
        @pl.when(pl.program_id(0) == 0)
        def _():
            dg_ref[...] = jnp.zeros_like(dg_ref)

        dx, dg_rows = _rms_bwd(x_ref[...], g_ref[...], d_ref[...])
        dx_ref[...] = h_ref[...] + dx
        dg_ref[...] += jnp.sum(dg_rows, axis=0, keepdims=True)

    row = pl.BlockSpec((tm, D_MODEL), lambda i: (i, 0))
    vec = pl.BlockSpec((1, D_MODEL), lambda i: (0, 0))
    return pl.pallas_call(
        body, name="rms_in_bwd", grid=(t // tm,),
        in_specs=[row, row, row, vec], out_specs=[row, vec],
        out_shape=[jax.ShapeDtypeStruct((t, D_MODEL), F32), jax.ShapeDtypeStruct((1, D_MODEL), F32)],
        compiler_params=_params("arbitrary"),
    )(x2, dxn, dh1, g1)


def _pack_small(dg1, dbr, dba, dg2, dg3, dw):
    def body(a_ref, b_ref, c_ref, d_ref, e_ref, w_ref, o_ref):
        o_ref[...] = jnp.zeros_like(o_ref)
        for r, ref in enumerate((a_ref, b_ref, c_ref, d_ref, e_ref)):
            o_ref[r:r + 1, :] = ref[...]
        for hp in range(ATT_HEADS // 2):
            o_ref[8 + 2 * hp:10 + 2 * hp, :] = w_ref[hp]

    return pl.pallas_call(body, name="pack_small",
                          out_shape=jax.ShapeDtypeStruct((16, D_MODEL), F32))(dg1, dbr, dba, dg2, dg3, dw)


def _rotary_tables(seq):
    freqs = ROPE_BASE ** (-jnp.arange(0, RET_KEY_DIM, 2, dtype=F32) / RET_KEY_DIM)
    ang = jnp.arange(seq, dtype=F32)[:, None] * freqs[None, :]
    cos, sin = jnp.cos(ang), jnp.sin(ang)
    return jnp.concatenate([cos, cos], axis=1), jnp.concatenate([-sin, sin], axis=1)


def _bias_rows(rel_bias):
    n_far = BIAS_LEN - ATT_Q - MAX_REL + 1
    n_near = BIAS_LEN - n_far - (N_REL - 2)
    w = jnp.concatenate([jnp.broadcast_to(rel_bias[:, N_REL - 1:], (ATT_HEADS, n_far)),
                         rel_bias[:, 1:N_REL - 1][:, ::-1],
                         jnp.broadcast_to(rel_bias[:, :1], (ATT_HEADS, n_near))], axis=1)
    return w.reshape(ATT_HEADS // 2, 2, BIAS_LEN)


def _bias_rows_bwd(dw):
    n_far = BIAS_LEN - ATT_Q - MAX_REL + 1
    mid = dw[:, n_far:n_far + N_REL - 2][:, ::-1]
    return jnp.concatenate([jnp.sum(dw[:, n_far + N_REL - 2:], axis=1, keepdims=True), mid,
                            jnp.sum(dw[:, :n_far], axis=1, keepdims=True)], axis=1)


def _step(x, tgt, norm_mix, b_gate, norm_ffn, norm_final, rel_bias_shard, shard):
    batch, seq, _ = x.shape
    t = batch * seq
    n_rb = rel_bias_shard.shape[-1]
    x2, tgt2 = x.reshape(t, D_MODEL), tgt.reshape(t, D_MODEL)
    g3 = norm_final.reshape(1, D_MODEL)
    cs, sn = _rotary_tables(seq)
    lg = np.log(1.0 - 2.0 ** (-5.0 - np.arange(RET_HEADS, dtype=np.float32))).astype(np.float32)
    lg_arr = jnp.asarray(np.broadcast_to(lg[:, None, None], (RET_HEADS, 1, LANES)))

    def gather(*names):
        return _ChipGather([shard[nm] for nm in names])

    def scatter(*grads):
        return _Exchange(grads, scatter=True)

    rb_pad = jnp.pad(rel_bias_shard, ((0, 0), (0, LANES - n_rb)))
    w_in_t, rb_full = _alone(_ChipGather([shard["w_in_t"], rb_pad]), "gather_w_in")
    rb_full = rb_full.reshape(N_DEV, ATT_HEADS, LANES)[:, :, :n_rb]
    wvec = _bias_rows(jnp.transpose(rb_full, (1, 0, 2)).reshape(ATT_HEADS, N_DEV * n_rb))

    xn = _rms_fwd(x2, norm_mix)
    proj, (w_ret, w_att_t, w_out, w_gate_t) = _mm(
        xn, w_in_t, tb=True, out_dtype=BF16, tm=1024, tn=1664, tk=1024, name="proj",
        exchange=gather("w_ret", "w_att_t", "w_out", "w_gate_t"))
    (gro, o_ret, qr, kr), (w_up_t,) = _ret_fwd(proj, cs, sn, lg_arr, batch, seq, gather("w_up_t"))
    (ao,), (w_down,) = _att_fwd(proj, wvec, batch, seq, gather("w_down"))
    z, y_ret, y_att = _mix_fwd(gro, ao, proj, b_gate, w_ret, w_att_t)
    h1, hn = _out_fwd(z, x2, w_out, norm_ffn)
    g_act, u_act, a_act = _ffn_up(hn, w_gate_t, w_up_t)
    dh2, dh2b, loss, dg3 = _ffn_down_loss(a_act, h1, tgt2, w_down, g3)

    wg = dict(out_dtype=BF16, tn=1024, ta=True)
    slots = {}
    dw_down = _mm(a_act, dh2b, tm=1408, tk=1024, name="dw_down", **wg)
    (d_gact, d_uact), (slots["w_down"],) = _ffn_bwd_act(dh2b, w_down, g_act, u_act, scatter(dw_down))
    dw_gate = _mm(d_gact, hn, tm=1408, tk=1024, name="dw_gate", **wg)
    dw_up = _mm(d_uact, hn, tm=1408, tk=1024, name="dw_up", **wg)
    (dh1, dh1b, dg2), (slots["w_gate_t"],) = _ffn_bwd_in(d_gact, d_uact, w_gate_t, w_up_t, h1, dh2, norm_ffn,
                                                       scatter(dw_gate))
    dw_out = _mm(z, dh1b, tm=1024, tk=2048, name="dw_out", **wg)
    (dyr, dya, dglr, dgla, dbr, dba), _ = _mix_bwd(dh1b, w_out, proj, b_gate, y_ret, y_att, None)
    dgro = _mm(dyr, w_ret, tb=True, out_dtype=BF16, tm=1024, tn=1024, tk=1024, name="dgro")
    dao = _mm(dya, w_att_t, out_dtype=BF16, tm=1024, tn=512, tk=1024, name="dao")
    dw_ret = _mm(gro, dyr, tm=1024, tk=2048, name="dw_ret", **wg)
    dw_att = _mm(dya, ao, tm=1024, tk=2048, name="dw_att", **wg)
    (drq, drk, drv, drg), (slots["w_up_t"],) = _ret_bwd(
        dgro, proj, o_ret, qr, kr, cs, sn, lg_arr, batch, seq, scatter(dw_up))
    (daq, dak, dav, dw), (slots["w_out"], slots["w_ret"], slots["w_att_t"]) = _att_bwd(
        proj, wvec, dao, batch, seq, scatter(dw_out, dw_ret, dw_att))
    dproj = [drq, drk, drv, drg, daq, dak, dav, dglr, dgla]
    dw_in = _mm_pieces(dproj, xn, ta=True, out_dtype=BF16, tm=512, tn=1024, tk=1024, name="dw_in")
    dxn, (slots["w_in_t"],) = _mm_pieces(dproj, w_in_t, ta=False, out_dtype=F32, tm=1024, tn=1024, tk=512, name="dxn",
                                  exchange=scatter(dw_in))
    dx, dg1 = _rms_in_bwd(x2, dxn, dh1, norm_mix)
    small = _pack_small(dg1, dbr, dba, dg2, dg3, dw)
    (small_slots,) = _alone(_ChipGather([small]), "gather_small")
    return loss[0, 0], dx.reshape(batch, seq, D_MODEL), slots, small_slots.reshape(N_DEV, 16, D_MODEL)


def _sum_slots(slots, name):
    _, r, c = slots.shape
    tr = max(d for d in range(16, r + 1, 16) if r % d == 0 and (d * c <= 256 * 1024 or d == 16))

    def body(s_ref, o_ref):
        acc = s_ref[0].astype(F32)
        for s in range(1, N_DEV):
            acc = acc + s_ref[s].astype(F32)
        o_ref[...] = acc

    return pl.pallas_call(
        body, name=name, grid=(r // tr,),
        in_specs=[pl.BlockSpec((N_DEV, tr, c), lambda i: (0, i, 0))],
        out_specs=pl.BlockSpec((tr, c), lambda i: (i, 0)),
        out_shape=jax.ShapeDtypeStruct((r, c), F32),
        compiler_params=_params("parallel"),
    )(slots)


def _adamw_math(w, g, m, v):
    m = ADAM_B1 * m + (1.0 - ADAM_B1) * g
    v = ADAM_B2 * v + (1.0 - ADAM_B2) * (g * g)
    m_hat = m / (1.0 - ADAM_B1 ** ADAM_STEP)
    v_hat = v / (1.0 - ADAM_B2 ** ADAM_STEP)
    return -ADAM_LR * (m_hat / (jnp.sqrt(v_hat) + ADAM_EPS) + ADAM_WD * w), m, v


def _adamw(w, g, m, v, name):
    r, c = w.shape
    tr = r
    while tr * c > 128 * 1024 and tr % 16 == 0:
        tr //= 2

    def body(w_ref, g_ref, m_ref, v_ref, d_ref, nm_ref, nv_ref):
        d_ref[...], nm_ref[...], nv_ref[...] = _adamw_math(w_ref[...], g_ref[...], m_ref[...], v_ref[...])

    blk = pl.BlockSpec((tr, c), lambda i: (i, 0))
    return pl.pallas_call(
        body, name=name, grid=(r // tr,),
        in_specs=[blk] * 4, out_specs=[blk] * 3,
        out_shape=[jax.ShapeDtypeStruct((r, c), F32)] * 3,
        compiler_params=_params("parallel"),
    )(w, g, m, v)


def _adamw_small(ws, gs, ms, vs):
    n = len(ws)

    def body(*refs):
        for i in range(n):
            w_ref, g_ref, m_ref, v_ref = (refs[j * n + i] for j in range(4))
            d_ref, nm_ref, nv_ref = (refs[(4 + j) * n + i] for j in range(3))
            d_ref[...], nm_ref[...], nv_ref[...] = _adamw_math(w_ref[...], g_ref[...], m_ref[...], v_ref[...])

    shapes = [jax.ShapeDtypeStruct(w.shape, F32) for w in ws]
    outs = pl.pallas_call(body, name="adamw_small", out_shape=shapes * 3)(*ws, *gs, *ms, *vs)
    return outs[:n], outs[n:2 * n], outs[2 * n:]


def kernel(x, norm_mix, w_in, b_gate, rel_bias, w_ret_out, w_att_out, w_out, norm_ffn, w_ffn_gate, w_ffn_up, w_ffn_down, norm_final, loss_target, m_norm_mix, m_w_in, m_b_gate, m_rel_bias, m_w_ret_out, m_w_att_out, m_w_out, m_norm_ffn, m_w_ffn_gate, m_w_ffn_up, m_w_ffn_down, m_norm_final, v_norm_mix, v_w_in, v_b_gate, v_rel_bias, v_w_ret_out, v_w_att_out, v_w_out, v_norm_ffn, v_w_ffn_gate, v_w_ffn_up, v_w_ffn_down, v_norm_final):
    me = _index(_place())
    n_rb = rel_bias.shape[-1]

    shard = dict(w_in_t=w_in[0].T, w_gate_t=w_ffn_gate[0].T, w_up_t=w_ffn_up[0].T, w_down=w_ffn_down[0],
                 w_ret=w_ret_out[0], w_out=w_out[0], w_att_t=w_att_out[0].T)
    shard = {nm: s.astype(BF16) for nm, s in shard.items()}
    loss, dx, slots, small_slots = _step(x, loss_target, norm_mix, b_gate, norm_ffn, norm_final, rel_bias[0], shard)
    loss = lax.psum(loss, ("x", "y", "c"))
    summed = {nm: _sum_slots(s, "sum_" + nm) for nm, s in slots.items()}
    small_sum = _sum_slots(small_slots, "sum_small")

    transposed = dict(w_in="w_in_t", w_ffn_gate="w_gate_t", w_ffn_up="w_up_t", w_att_out="w_att_t")
    g = dict(
        w_ffn_down=summed["w_down"], w_ret_out=summed["w_ret"], w_out=summed["w_out"],
        norm_mix=small_sum[0:1], b_gate=jnp.concatenate([small_sum[1:2], small_sum[2:3]], axis=1),
        norm_ffn=small_sum[3:4], norm_final=small_sum[4:5],
        rel_bias=lax.dynamic_slice_in_dim(_bias_rows_bwd(small_sum[8:16]), me * n_rb, n_rb, axis=1),
    )
    w = dict(norm_mix=norm_mix, w_in=w_in, b_gate=b_gate, rel_bias=rel_bias, w_ret_out=w_ret_out, w_att_out=w_att_out,
             w_out=w_out, norm_ffn=norm_ffn, w_ffn_gate=w_ffn_gate, w_ffn_up=w_ffn_up, w_ffn_down=w_ffn_down,
             norm_final=norm_final)
    m = dict(norm_mix=m_norm_mix, w_in=m_w_in, b_gate=m_b_gate, rel_bias=m_rel_bias, w_ret_out=m_w_ret_out,
             w_att_out=m_w_att_out, w_out=m_w_out, norm_ffn=m_norm_ffn, w_ffn_gate=m_w_ffn_gate, w_ffn_up=m_w_ffn_up,
             w_ffn_down=m_w_ffn_down, norm_final=m_norm_final)
    v = dict(norm_mix=v_norm_mix, w_in=v_w_in, b_gate=v_b_gate, rel_bias=v_rel_bias, w_ret_out=v_w_ret_out,
             w_att_out=v_w_att_out, w_out=v_w_out, norm_ffn=v_norm_ffn, w_ffn_gate=v_w_ffn_gate, w_ffn_up=v_w_ffn_up,
             w_ffn_down=v_w_ffn_down, norm_final=v_norm_final)
    order = ("norm_mix", "w_in", "b_gate", "rel_bias", "w_ret_out", "w_att_out", "w_out", "norm_ffn",
             "w_ffn_gate", "w_ffn_up", "w_ffn_down", "norm_final")
    small_names = ("norm_mix", "b_gate", "rel_bias", "norm_ffn", "norm_final")

    def flat(a):
        return a[0] if a.ndim == 3 else a.reshape(-1, a.shape[-1])

    grad, delta, new_m, new_v = {}, {}, {}, {}
    for nm in order:
        if nm in transposed:
            g_t = summed[transposed[nm]]
            d, nmom, nvar = _adamw(w[nm][0].T, g_t, m[nm][0].T, v[nm][0].T, "adamw_" + nm)
            grad[nm], delta[nm], new_m[nm], new_v[nm] = (a.T[None] for a in (g_t, d, nmom, nvar))
        elif nm not in small_names:
            d, nmom, nvar = _adamw(flat(w[nm]), g[nm], flat(m[nm]), flat(v[nm]), "adamw_" + nm)
            grad[nm], delta[nm], new_m[nm], new_v[nm] = (a.reshape(w[nm].shape) for a in (g[nm], d, nmom, nvar))
    ds, nms, nvs = _adamw_small([flat(w[nm]) for nm in small_names], [g[nm] for nm in small_names],
                                [flat(m[nm]) for nm in small_names], [flat(v[nm]) for nm in small_names])
    for i, nm in enumerate(small_names):
        grad[nm], delta[nm], new_m[nm], new_v[nm] = (a.reshape(w[nm].shape) for a in (g[nm], ds[i], nms[i], nvs[i]))

    return (loss, dx, *[grad[nm] for nm in order], *[delta[nm] for nm in order],
            *[new_m[nm] for nm in order], *[new_v[nm] for nm in order])
```

```python
import numpy as np
import jax
import jax.numpy as jnp
from jax import lax
from jax.experimental import pallas as pl
from jax.experimental.pallas import tpu as pltpu

F32 = jnp.float32
BF16 = jnp.bfloat16
MESH = pl.DeviceIdType.MESH

D_MODEL = 1024
CHUNK = 64
RET_HEADS = 4
RET_KEY_DIM = 128
RET_VAL_DIM = 256
ATT_HEADS = 8
BAND_CHUNKS = 8
MAX_REL = 256
N_REL = CHUNK + MAX_REL
D_FF = 2816
N_IN = 6656
ROPE_BASE = 10000.0
EPS = 1e-6
NEG_INF = -1e30
C_RQ, C_RK, C_RV, C_RG, C_AQ, C_AK, C_AV, C_GL = 0, 512, 1024, 2048, 3072, 3584, 4096, 4608

ADAM_LR = 0.001
ADAM_B1 = 0.9
ADAM_B2 = 0.999
ADAM_EPS = 1e-08
ADAM_WD = 0.01
ADAM_STEP = 10

N_DEV = 8
LANES = 128
RET_TILE = 256
ATT_Q = 256
ATT_PAD = BAND_CHUNKS * CHUNK
ATT_WIN = ATT_PAD + ATT_Q
ATT_STARTS = ATT_PAD // ATT_Q
ATT_ROWS = 32
BIAS_LEN = 1024
VMEM_LIMIT = 48 * 1024 * 1024


def _params(*sem):
    return pltpu.CompilerParams(dimension_semantics=sem, vmem_limit_bytes=VMEM_LIMIT)


def _dot(a, b):
    return lax.dot_general(a, b, (((1,), (0,)), ((), ())), preferred_element_type=F32)


def _dot_nt(a, b):
    return lax.dot_general(a, b, (((1,), (1,)), ((), ())), preferred_element_type=F32)


def _dot_tn(a, b):
    return lax.dot_general(a, b, (((0,), (0,)), ((), ())), preferred_element_type=F32)


def _sigmoid(x):
    return 1.0 / (1.0 + jnp.exp(-x))


def _rms_bwd(x, g, dy):
    r = lax.rsqrt(jnp.mean(x * x, axis=-1, keepdims=True) + EPS)
    u = dy * g
    dx = r * u - x * (r * r * r) * jnp.mean(u * x, axis=-1, keepdims=True)
    return dx, dy * x * r


def _place():
    return lax.axis_index("x"), lax.axis_index("y"), lax.axis_index("c")


def _peer(k):
    x, y, c = _place()
    return ((1 - x) if k & 4 else x, (1 - y) if k & 2 else y, (1 - c) if k & 1 else c)


def _index(place):
    return 4 * place[0] + 2 * place[1] + place[2]


def _rows(ref, block, nrows):
    align = 16 if ref.dtype == BF16 else 8
    return ref.at[pl.ds(pl.multiple_of(block * nrows, align), nrows)]


class _Exchange:
    def __init__(self, arrays, scatter):
        self.arrays, self.scatter, self.n = list(arrays), scatter, len(arrays)

    def out_shape(self):
        if self.scatter:
            return [jax.ShapeDtypeStruct((N_DEV, a.shape[0] // N_DEV) + a.shape[1:], a.dtype) for a in self.arrays]
        return [jax.ShapeDtypeStruct((N_DEV * a.shape[0],) + a.shape[1:], a.dtype) for a in self.arrays]

    def scratch(self):
        return [pltpu.SemaphoreType.DMA((self.n, N_DEV - 1)), pltpu.SemaphoreType.DMA((self.n, N_DEV - 1)),
                pltpu.SemaphoreType.DMA((self.n,))]

    def _copies(self, ins, outs, sems):
        send_sems, recv_sems, local_sems = sems
        me = _index(_place())

        def src(w, to):
            return _rows(ins[w], to, ins[w].shape[0] // N_DEV) if self.scatter else ins[w]

        def dst(w, origin):
            return outs[w].at[origin] if self.scatter else _rows(outs[w], origin, ins[w].shape[0])

        def remote(w, k, to, origin):
            return pltpu.make_async_remote_copy(src_ref=src(w, to), dst_ref=dst(w, origin),
                                                send_sem=send_sems.at[w, k - 1], recv_sem=recv_sems.at[w, k - 1],
                                                device_id=_peer(k), device_id_type=MESH)

        pairs = [(w, k) for w in range(self.n) for k in range(1, N_DEV)]
        own = lambda: [pltpu.make_async_copy(src(w, me), dst(w, me), local_sems.at[w]) for w in range(self.n)]
        sent = lambda: [remote(w, k, _index(_peer(k)), me) for w, k in pairs]
        arriving = lambda: [remote(w, k, me, _index(_peer(k))) for w, k in pairs]
        return own, sent, arriving

    def start(self, ins, outs, sems):
        own, sent, _ = self._copies(ins, outs, sems)
        for cp in own() + sent():
            cp.start()

    def wait(self, ins, outs, sems):
        own, sent, arriving = self._copies(ins, outs, sems)
        for cp in arriving():
            cp.wait_recv()
        for cp in sent():
            cp.wait_send()
        for cp in own():
            cp.wait()


class _ChipGather:
    def __init__(self, arrays):
        self.arrays, self.n = list(arrays), len(arrays)

    def out_shape(self):
        return [jax.ShapeDtypeStruct((N_DEV * a.shape[0],) + a.shape[1:], a.dtype) for a in self.arrays]

    def scratch(self):
        return [pltpu.SemaphoreType.DMA((self.n, N_DEV - 1)), pltpu.SemaphoreType.DMA((self.n, N_DEV - 1)),
                pltpu.SemaphoreType.DMA((self.n,))]

    def _parts(self, ins, outs, sems):
        send_sems, recv_sems, local_sems = sems
        x, y, c = _place()
        me, sibling = (x, y, c), (x, y, 1 - c)
        chips = [(1 - x, y), (x, 1 - y), (1 - x, 1 - y)]

        def rows(w, place):
            return _rows(outs[w], _index(place), ins[w].shape[0])

        def copy(w, k, block, to, own=False):
            return pltpu.make_async_remote_copy(src_ref=ins[w] if own else rows(w, block), dst_ref=rows(w, block),
                                                send_sem=send_sems.at[w, k], recv_sem=recv_sems.at[w, k],
                                                device_id=to, device_id_type=MESH)

        def local(w):
            return pltpu.make_async_copy(ins[w], rows(w, me), local_sems.at[w])

        return me, sibling, chips, c, copy, local

    def start(self, ins, outs, sems):
        me, sibling, chips, c, copy, local = self._parts(ins, outs, sems)
        for w in range(self.n):
            local(w).start()
            copy(w, 0, me, sibling, own=True).start()
            for j, chip in enumerate(chips):
                copy(w, 1 + j, me, (*chip, c), own=True).start()

    def wait(self, ins, outs, sems):
        me, sibling, chips, c, copy, local = self._parts(ins, outs, sems)
        for w in range(self.n):
            for j, chip in enumerate(chips):
                copy(w, 1 + j, (*chip, c), me).wait_recv()
                copy(w, 4 + j, (*chip, c), sibling).start()
        for w in range(self.n):
            copy(w, 0, sibling, me).wait_recv()
            for j, chip in enumerate(chips):
                copy(w, 4 + j, (*chip, 1 - c), me).wait_recv()
            copy(w, 0, me, sibling, own=True).wait_send()
            for j, chip in enumerate(chips):
                copy(w, 1 + j, me, (*chip, c), own=True).wait_send()
                copy(w, 4 + j, (*chip, c), sibling).wait_send()
            local(w).wait()


def _call(body, *, name, grid, in_specs, out_specs, out_shape, scratch=(), semantics, args, exchange=None):
    if exchange is None:
        return pl.pallas_call(body, name=name, grid=grid, in_specs=in_specs, out_specs=out_specs, out_shape=out_shape,
                              scratch_shapes=list(scratch), compiler_params=_params(*semantics))(*args), None
    n_in, n_out, n_scr, nx = len(in_specs), len(out_specs), len(scratch), exchange.n

    def full_body(*refs):
        ins, refs = refs[:n_in], refs[n_in:]
        x_in, refs = refs[:nx], refs[nx:]
        outs, refs = refs[:n_out], refs[n_out:]
        x_out, refs = refs[:nx], refs[nx:]
        scr, sems = refs[:n_scr], refs[n_scr:]
        first, last = True, True
        for axis, size in enumerate(grid):
            first = jnp.logical_and(first, pl.program_id(axis) == 0)
            last = jnp.logical_and(last, pl.program_id(axis) == size - 1)
        if grid:
            pl.when(first)(lambda: exchange.start(x_in, x_out, sems))
        else:
            exchange.start(x_in, x_out, sems)
        body(*ins, *outs, *scr)
        if grid:
            pl.when(last)(lambda: exchange.wait(x_in, x_out, sems))
        else:
            exchange.wait(x_in, x_out, sems)

    hbm = pl.BlockSpec(memory_space=pltpu.HBM)
    res = pl.pallas_call(
        full_body, name=name, grid=grid,
        in_specs=list(in_specs) + [hbm] * nx, out_specs=list(out_specs) + [hbm] * nx,
        out_shape=list(out_shape) + exchange.out_shape(),
        scratch_shapes=list(scratch) + exchange.scratch(),
        compiler_params=_params(*(["arbitrary"] * len(grid))),
    )(*args, *exchange.arrays)
    return res[:n_out], res[n_out:]


def _alone(exchange, name):
    return _call(lambda: None, name=name, grid=(), in_specs=[], out_specs=[], out_shape=[], semantics=(),
                 args=(), exchange=exchange)[1]


def _mm(a, b, *, ta=False, tb=False, out_dtype, tm, tn, tk, name, exchange=None):
    m, k = (a.shape[1], a.shape[0]) if ta else a.shape
    n = b.shape[0] if tb else b.shape[1]
    assert k == (b.shape[1] if tb else b.shape[0])
    tm, tn, tk = min(tm, m), min(tn, n), min(tk, k)
    assert m % tm == 0 and n % tn == 0 and k % tk == 0, (name, m, n, k)
    nk = k // tk
    dims = (((0 if ta else 1,), (1 if tb else 0,)), ((), ()))

    def body(a_ref, b_ref, o_ref, *acc):
        prod = lax.dot_general(a_ref[...].astype(BF16), b_ref[...].astype(BF16), dims, preferred_element_type=F32)
        if nk == 1:
            o_ref[...] = prod.astype(o_ref.dtype)
            return
        acc_ref, kk = acc[0], pl.program_id(2)

        @pl.when(kk == 0)
        def _():
            acc_ref[...] = prod

        @pl.when((kk > 0) & (kk < nk - 1))
        def _():
            acc_ref[...] += prod

        @pl.when(kk == nk - 1)
        def _():
            o_ref[...] = (acc_ref[...] + prod).astype(o_ref.dtype)

    a_spec = (pl.BlockSpec((tk, tm), lambda i, j, kk: (kk, i)) if ta
              else pl.BlockSpec((tm, tk), lambda i, j, kk: (i, kk)))
    b_spec = (pl.BlockSpec((tn, tk), lambda i, j, kk: (j, kk)) if tb
              else pl.BlockSpec((tk, tn), lambda i, j, kk: (kk, j)))
    (out,), moved = _call(
        body, name=name, grid=(m // tm, n // tn, nk),
        in_specs=[a_spec, b_spec],
        out_specs=[pl.BlockSpec((tm, tn), lambda i, j, kk: (i, j))],
        out_shape=[jax.ShapeDtypeStruct((m, n), out_dtype)],
        scratch=[pltpu.VMEM((tm, tn), F32)] if nk > 1 else [],
        semantics=("parallel", "parallel", "arbitrary"), args=(a, b), exchange=exchange)
    return out if exchange is None else (out, moved)


def _mm_pieces(pieces, b, *, ta, out_dtype, tm, tn, tk, name, exchange=None):
    rows, n = pieces[0].shape[0], b.shape[1]
    step = tm if ta else tk
    assert all(p.shape[0] == rows and p.shape[1] % step == 0 for p in pieces), name
    edges = [int(e) for e in np.cumsum([0] + [p.shape[1] // step for p in pieces])]
    total = edges[-1] * step
    m, k = (total, rows) if ta else (rows, total)
    assert b.shape[0] == k and m % tm == 0 and n % tn == 0 and k % tk == 0, name
    nk, npieces = k // tk, len(pieces)
    dims = (((0 if ta else 1,), (0,)), ((), ()))

    def body(*refs):
        a_refs, (b_ref, o_ref, acc_ref) = refs[:npieces], refs[npieces:]
        kk = pl.program_id(2)
        pos = pl.program_id(0) if ta else kk

        @pl.when(kk == 0)
        def _():
            acc_ref[...] = jnp.zeros_like(acc_ref)

        for p, a_ref in enumerate(a_refs):
            @pl.when((pos >= edges[p]) & (pos < edges[p + 1]))
            def _(a_ref=a_ref):
                acc_ref[...] += lax.dot_general(a_ref[...], b_ref[...], dims, preferred_element_type=F32)

        @pl.when(kk == nk - 1)
        def _():
            o_ref[...] = acc_ref[...].astype(o_ref.dtype)

    def a_spec(p):
        lo, last = edges[p], edges[p + 1] - edges[p] - 1
        if ta:
            def index(i, j, kk):
                inside = (i >= lo) & (i <= lo + last)
                return jnp.where(inside, kk, 0), jnp.clip(i - lo, 0, last)
            return pl.BlockSpec((tk, tm), index)
        return pl.BlockSpec((tm, tk), lambda i, j, kk: (i, jnp.clip(kk - lo, 0, last)))

    (out,), moved = _call(
        body, name=name, grid=(m // tm, n // tn, nk),
        in_specs=[a_spec(p) for p in range(npieces)] + [pl.BlockSpec((tk, tn), lambda i, j, kk: (kk, j))],
        out_specs=[pl.BlockSpec((tm, tn), lambda i, j, kk: (i, j))],
        out_shape=[jax.ShapeDtypeStruct((m, n), out_dtype)],
        scratch=[pltpu.VMEM((tm, tn), F32)],
        semantics=("parallel", "parallel", "arbitrary"), args=(*pieces, b), exchange=exchange)
    return out if exchange is None else (out, moved)


def _rms_fwd(x2, g, exchange):
    t = x2.shape[0]
    tm = min(512, t)

    def body(x_ref, g_ref, o_ref):
        x = x_ref[...]
        r = lax.rsqrt(jnp.mean(x * x, axis=-1, keepdims=True) + EPS)
        o_ref[...] = (x * r * g_ref[...]).astype(o_ref.dtype)

    return _call(
        body, name="rms_in_fwd", grid=(t // tm,),
        in_specs=[pl.BlockSpec((tm, D_MODEL), lambda i: (i, 0)), pl.BlockSpec((1, D_MODEL), lambda i: (0, 0))],
        out_specs=[pl.BlockSpec((tm, D_MODEL), lambda i: (i, 0))],
        out_shape=[jax.ShapeDtypeStruct((t, D_MODEL), BF16)],
        semantics=("parallel",), args=(x2, g), exchange=exchange)


def _decay(lg):
    row = lax.broadcasted_iota(jnp.int32, (RET_TILE, RET_TILE), 0)
    col = lax.broadcasted_iota(jnp.int32, (RET_TILE, RET_TILE), 1)
    within = jnp.exp(lg * jnp.abs(row - col).astype(F32))
    inside = jnp.where((col >> 6) <= (row >> 6), within, 0.0)
    pos = lax.broadcasted_iota(jnp.int32, (RET_TILE, 1), 0).astype(F32)
    q_dec = jnp.exp(lg * (pos + 1.0))
    k_dec = jnp.exp(lg * (RET_TILE - 1.0 - pos))
    tile_dec = jnp.exp(lg * float(RET_TILE))
    return inside, q_dec, k_dec, tile_dec


def _scaled(a_bf16, dec):
    return (a_bf16.astype(F32) * dec).astype(BF16)


def _ret_fwd(proj, cs, sn, lg_arr, batch, seq, exchange):
    t = batch * seq
    nt = seq // RET_TILE

    def body(q_ref, k_ref, v_ref, rg_ref, cs_ref, sn_ref, lg_ref, gro_ref, o_ref, qr_ref, kr_ref):
        lg = lg_ref[:, 0:1]
        cs_t, sn_t = cs_ref[...], sn_ref[...]
        q = q_ref[...].astype(F32)
        k = k_ref[...].astype(F32)
        qr_ref[...] = (q * cs_t + pltpu.roll(q, 64, 1) * sn_t).astype(BF16)
        kr_ref[...] = ((k * cs_t + pltpu.roll(k, 64, 1) * sn_t) * (RET_KEY_DIM ** -0.5)).astype(BF16)
        inside, q_dec, k_dec, tile_dec = _decay(lg)
        state = jnp.zeros((RET_KEY_DIM, RET_VAL_DIM), F32)
        for i in range(nt):
            rows = slice(i * RET_TILE, (i + 1) * RET_TILE)
            qi, ki, vi = qr_ref[rows, :], kr_ref[rows, :], v_ref[rows, :]
            acc = _dot((_dot_nt(qi, ki) * inside).astype(BF16), vi)
            if i > 0:
                acc = acc + _dot(_scaled(qi, q_dec), state.astype(BF16))
            if i < nt - 1:
                state = state * tile_dec + _dot_tn(_scaled(ki, k_dec), vi)
            o_ref[rows, :] = acc
            xc = acc - jnp.mean(acc, axis=-1, keepdims=True)
            nrm = xc * lax.rsqrt(jnp.mean(xc * xc, axis=-1, keepdims=True) + EPS)
            rg = rg_ref[rows, :].astype(F32)
            gro_ref[rows, :] = (rg * _sigmoid(rg) * nrm).astype(BF16)

    def col(base, width):
        return lambda b, h: (b, base // width + h)

    return _call(
        body, name="ret_fwd", grid=(batch, RET_HEADS),
        in_specs=[pl.BlockSpec((seq, RET_KEY_DIM), col(C_RQ, RET_KEY_DIM)),
                  pl.BlockSpec((seq, RET_KEY_DIM), col(C_RK, RET_KEY_DIM)),
                  pl.BlockSpec((seq, RET_VAL_DIM), col(C_RV, RET_VAL_DIM)),
                  pl.BlockSpec((seq, RET_VAL_DIM), col(C_RG, RET_VAL_DIM)),
                  pl.BlockSpec((seq, RET_KEY_DIM), lambda b, h: (0, 0)),
                  pl.BlockSpec((seq, RET_KEY_DIM), lambda b, h: (0, 0)),
                  pl.BlockSpec((None, 1, LANES), lambda b, h: (h, 0, 0))],
        out_specs=[pl.BlockSpec((seq, RET_VAL_DIM), lambda b, h: (b, h)),
                   pl.BlockSpec((seq, RET_VAL_DIM), lambda b, h: (b, h)),
                   pl.BlockSpec((seq, RET_KEY_DIM), lambda b, h: (b, h)),
                   pl.BlockSpec((seq, RET_KEY_DIM), lambda b, h: (b, h))],
        out_shape=[jax.ShapeDtypeStruct((t, RET_HEADS * RET_VAL_DIM), BF16),
                   jax.ShapeDtypeStruct((t, RET_HEADS * RET_VAL_DIM), F32),
                   jax.ShapeDtypeStruct((t, RET_HEADS * RET_KEY_DIM), BF16),
                   jax.ShapeDtypeStruct((t, RET_HEADS * RET_KEY_DIM), BF16)],
        semantics=("parallel", "parallel"), args=(proj, proj, proj, proj, cs, sn, lg_arr), exchange=exchange)


def _att_bias(w_ref, bias_ref):
    n_i = lax.broadcasted_iota(jnp.int32, (ATT_Q, BIAS_LEN), 0)
    qc = lax.broadcasted_iota(jnp.int32, (ATT_Q, ATT_WIN), 0) >> 6
    kc = lax.broadcasted_iota(jnp.int32, (ATT_Q, ATT_WIN), 1) >> 6
    dc = qc + BAND_CHUNKS - kc
    band = (dc >= 0) & (dc <= BAND_CHUNKS)
    key = lax.broadcasted_iota(jnp.int32, (ATT_Q, ATT_WIN), 1)
    for e in range(2):
        xw = jnp.broadcast_to(w_ref[e:e + 1, :], (ATT_Q, BIAS_LEN))
        for bit in range(8):
            xw = jnp.where(((n_i >> bit) & 1) == 1, pltpu.roll(xw, 1 << bit, 1), xw)
        bias = jnp.where(band, xw[:, BIAS_LEN - ATT_WIN:], NEG_INF)
        for first in range(ATT_STARTS):
            bias_ref[first, e] = jnp.where(key + (first * ATT_Q - ATT_PAD) >= 0, bias, NEG_INF)
        bias_ref[ATT_STARTS, e] = bias


def _att_specs(batch, seq):
    ni = seq // ATT_Q
    q_spec = pl.BlockSpec((ATT_Q, LANES), lambda hp, b, i: (b * ni + i, C_AQ // LANES + hp))
    k_spec = pl.BlockSpec((seq, LANES), lambda hp, b, i: (b, C_AK // LANES + hp))
    v_spec = pl.BlockSpec((seq, LANES), lambda hp, b, i: (b, C_AV // LANES + hp))
    w_spec = pl.BlockSpec((None, 2, BIAS_LEN), lambda hp, b, i: (hp, 0, 0))
    b_spec = pl.BlockSpec((None, ATT_STARTS + 1, 2, ATT_Q, ATT_WIN), lambda hp, b, i: (hp, 0, 0, 0, 0))
    pad = pltpu.VMEM((seq + ATT_PAD, LANES), BF16)
    return ni, q_spec, k_spec, v_spec, w_spec, b_spec, pad


def _att_bias_tiles(wvec):
    return pl.pallas_call(
        _att_bias, name="att_bias", grid=(ATT_HEADS // 2,),
        in_specs=[pl.BlockSpec((None, 2, BIAS_LEN), lambda hp: (hp, 0, 0))],
        out_specs=pl.BlockSpec((None, ATT_STARTS + 1, 2, ATT_Q, ATT_WIN), lambda hp: (hp, 0, 0, 0, 0)),
        out_shape=jax.ShapeDtypeStruct((ATT_HEADS // 2, ATT_STARTS + 1, 2, ATT_Q, ATT_WIN), F32),
        compiler_params=_params("parallel"),
    )(wvec)


def _att_pad(src_ref, pad_ref):
    pad_ref[:ATT_PAD, :] = jnp.zeros((ATT_PAD, LANES), BF16)
    pad_ref[ATT_PAD:, :] = src_ref[...]


def _att_head(q2, sel):
    return jnp.where(sel, q2, jnp.zeros_like(q2)) * 0.125


def _att_softmax_rows(s_ref, bias_ref, rows):
    s = s_ref[rows, :] + bias_ref[rows, :]
    ex = jnp.exp(s - jnp.max(s, axis=-1, keepdims=True))
    return ex, 1.0 / jnp.sum(ex, axis=-1, keepdims=True)


def _att_fwd(proj, bias, batch, seq, exchange):
    ni, q_spec, k_spec, v_spec, _, b_spec, pad = _att_specs(batch, seq)

    def body(q_ref, k_ref, v_ref, bias_ref, o_ref, kp_ref, vp_ref, s_ref, e_ref):
        i = pl.program_id(2)

        @pl.when(i == 0)
        def _():
            _att_pad(k_ref, kp_ref)
            _att_pad(v_ref, vp_ref)

        win = pl.ds(pl.multiple_of(i * ATT_Q, ATT_Q), ATT_WIN)
        k2, v2, q2 = kp_ref[win, :], vp_ref[win, :], q_ref[...]
        lo = lax.broadcasted_iota(jnp.int32, (1, LANES), 1) < 64
        start = jnp.minimum(i, ATT_STARTS)
        out = jnp.zeros((ATT_Q, LANES), F32)
        for e in range(2):
            sel = lo if e == 0 else jnp.logical_not(lo)
            s_ref[e] = _dot_nt(_att_head(q2, sel), k2)
            rsum = []
            for c in range(ATT_Q // ATT_ROWS):
                rows = slice(c * ATT_ROWS, (c + 1) * ATT_ROWS)
                ex, r = _att_softmax_rows(s_ref.at[e], bias_ref.at[start, e], rows)
                e_ref[e, rows, :] = ex.astype(BF16)
                rsum.append(r)
            out = out + _dot(e_ref[e], jnp.where(sel, v2, jnp.zeros_like(v2))) * jnp.concatenate(rsum, axis=0)
        o_ref[...] = out.astype(BF16)

    return _call(
        body, name="att_fwd", grid=(ATT_HEADS // 2, batch, ni),
        in_specs=[q_spec, k_spec, v_spec, b_spec],
        out_specs=[pl.BlockSpec((ATT_Q, LANES), lambda hp, b, i: (b * ni + i, hp))],
        out_shape=[jax.ShapeDtypeStruct((batch * seq, ATT_HEADS * 64), BF16)],
        scratch=[pad, pad, pltpu.VMEM((2, ATT_Q, ATT_WIN), F32), pltpu.VMEM((2, ATT_Q, ATT_WIN), BF16)],
        semantics=("arbitrary", "arbitrary", "arbitrary"), args=(proj, proj, proj, bias), exchange=exchange)


def _mix_fwd(gro, ao, proj, b_gate, w_ret, w_att_t):
    t = gro.shape[0]
    tm, tn = min(256, t), 512

    def body(gro_ref, ao_ref, glr_ref, gla_ref, br_ref, ba_ref, wr_ref, wa_ref, z_ref, yr_ref, ya_ref):
        yr = _dot(gro_ref[...], wr_ref[...])
        ya = _dot_nt(ao_ref[...], wa_ref[...])
        gr = _sigmoid(glr_ref[...].astype(F32) + br_ref[...])
        ga = _sigmoid(gla_ref[...].astype(F32) + ba_ref[...])
        z_ref[...] = (gr * yr + ga * ya).astype(BF16)
        yr_ref[...] = yr.astype(BF16)
        ya_ref[...] = ya.astype(BF16)

    nb = D_MODEL // tn
    out = pl.BlockSpec((tm, tn), lambda i, j: (i, j))
    return pl.pallas_call(
        body, name="mix_fwd", grid=(t // tm, nb),
        in_specs=[pl.BlockSpec((tm, D_MODEL), lambda i, j: (i, 0)),
                  pl.BlockSpec((tm, 512), lambda i, j: (i, 0)),
                  pl.BlockSpec((tm, tn), lambda i, j: (i, C_GL // tn + j)),
                  pl.BlockSpec((tm, tn), lambda i, j: (i, C_GL // tn + nb + j)),
                  pl.BlockSpec((1, tn), lambda i, j: (0, j)),
                  pl.BlockSpec((1, tn), lambda i, j: (0, nb + j)),
                  pl.BlockSpec((D_MODEL, tn), lambda i, j: (0, j)),
                  pl.BlockSpec((tn, 512), lambda i, j: (j, 0))],
        out_specs=[out, out, out],
        out_shape=[jax.ShapeDtypeStruct((t, D_MODEL), BF16)] * 3,
        compiler_params=_params("parallel", "parallel"),
    )(gro, ao, proj, proj, b_gate, b_gate, w_ret, w_att_t)


def _out_fwd(z, x2, w_out, g2):
    t = z.shape[0]
    tm = min(256, t)

    def body(z_ref, x_ref, w_ref, g_ref, h_ref, hn_ref):
        h = x_ref[...] + _dot(z_ref[...], w_ref[...])
        h_ref[...] = h
        r = lax.rsqrt(jnp.mean(h * h, axis=-1, keepdims=True) + EPS)
        hn_ref[...] = (h * r * g_ref[...]).astype(BF16)

    row = pl.BlockSpec((tm, D_MODEL), lambda i: (i, 0))
    return pl.pallas_call(
        body, name="out_fwd", grid=(t // tm,),
        in_specs=[row, row, pl.BlockSpec((D_MODEL, D_MODEL), lambda i: (0, 0)),
                  pl.BlockSpec((1, D_MODEL), lambda i: (0, 0))],
        out_specs=[row, row],
        out_shape=[jax.ShapeDtypeStruct((t, D_MODEL), F32), jax.ShapeDtypeStruct((t, D_MODEL), BF16)],
        compiler_params=_params("parallel"),
    )(z, x2, w_out, g2)


def _ffn_up(hn, wg_t, wu_t):
    t = hn.shape[0]
    tm, tn = min(512, t), D_FF // 2

    def body(h_ref, wg_ref, wu_ref, g_ref, u_ref, a_ref):
        g = _dot_nt(h_ref[...], wg_ref[...])
        u = _dot_nt(h_ref[...], wu_ref[...])
        g_ref[...] = g.astype(BF16)
        u_ref[...] = u.astype(BF16)
        a_ref[...] = (g * _sigmoid(g) * u).astype(BF16)

    w_spec = pl.BlockSpec((tn, D_MODEL), lambda j, i: (j, 0))
    out = pl.BlockSpec((tm, tn), lambda j, i: (i, j))
    return pl.pallas_call(
        body, name="ffn_up", grid=(D_FF // tn, t // tm),
        in_specs=[pl.BlockSpec((tm, D_MODEL), lambda j, i: (i, 0)), w_spec, w_spec],
        out_specs=[out, out, out],
        out_shape=[jax.ShapeDtypeStruct((t, D_FF), BF16)] * 3,
        compiler_params=_params("parallel", "parallel"),
    )(hn, wg_t, wu_t)


def _ffn_down_loss(a, h1, tgt, w_down, g3):
    t = a.shape[0]
    tm = min(512, t)

    def body(a_ref, h_ref, t_ref, w_ref, g_ref, dh_ref, dhb_ref, loss_ref, dg_ref):
        @pl.when(pl.program_id(0) == 0)
        def _():
            loss_ref[...] = jnp.zeros_like(loss_ref)
            dg_ref[...] = jnp.zeros_like(dg_ref)

        g = g_ref[...]
        h2 = h_ref[...] + _dot(a_ref[...], w_ref[...])
        r = lax.rsqrt(jnp.mean(h2 * h2, axis=-1, keepdims=True) + EPS)
        err = h2 * r * g - t_ref[...]
        loss_ref[...] += jnp.sum(err * err) * (0.5 / D_MODEL)
        dy = err * (1.0 / D_MODEL)
        dh, dg_rows = _rms_bwd(h2, g, dy)
        dg_ref[...] += jnp.sum(dg_rows, axis=0, keepdims=True)
        dh_ref[...] = dh
        dhb_ref[...] = dh.astype(BF16)

    row = pl.BlockSpec((tm, D_MODEL), lambda i: (i, 0))
    vec = pl.BlockSpec((1, D_MODEL), lambda i: (0, 0))
    return pl.pallas_call(
        body, name="ffn_down_loss", grid=(t // tm,),
        in_specs=[pl.BlockSpec((tm, D_FF), lambda i: (i, 0)), row, row,
                  pl.BlockSpec((D_FF, D_MODEL), lambda i: (0, 0)), vec],
        out_specs=[row, row, pl.BlockSpec((1, LANES), lambda i: (0, 0)), vec],
        out_shape=[jax.ShapeDtypeStruct((t, D_MODEL), F32), jax.ShapeDtypeStruct((t, D_MODEL), BF16),
                   jax.ShapeDtypeStruct((1, LANES), F32), jax.ShapeDtypeStruct((1, D_MODEL), F32)],
        compiler_params=_params("arbitrary"),
    )(a, h1, tgt, w_down, g3)


def _ffn_bwd_act(dh2b, w_down, g_act, u_act, exchange):
    t = dh2b.shape[0]
    tm, tn = min(512, t), D_FF // 2

    def body(d_ref, w_ref, g_ref, u_ref, dg_ref, du_ref):
        da = _dot_nt(d_ref[...], w_ref[...])
        g = g_ref[...].astype(F32)
        u = u_ref[...].astype(F32)
        sg = _sigmoid(g)
        dg_ref[...] = (da * u * sg * (1.0 + g * (1.0 - sg))).astype(BF16)
        du_ref[...] = (da * g * sg).astype(BF16)

    blk = pl.BlockSpec((tm, tn), lambda j, i: (i, j))
    return _call(
        body, name="ffn_bwd_act", grid=(D_FF // tn, t // tm),
        in_specs=[pl.BlockSpec((tm, D_MODEL), lambda j, i: (i, 0)),
                  pl.BlockSpec((tn, D_MODEL), lambda j, i: (j, 0)), blk, blk],
        out_specs=[blk, blk],
        out_shape=[jax.ShapeDtypeStruct((t, D_FF), BF16)] * 2,
        semantics=("parallel", "parallel"), args=(dh2b, w_down, g_act, u_act), exchange=exchange)


def _ffn_bwd_in(dg, du, wg_t, wu_t, h1, dh2, g2, exchange):
    t = dg.shape[0]
    tm, tk = min(512, t), D_FF // 2
    nk = D_FF // tk

    def body(dg_ref, du_ref, wg_ref, wu_ref, h_ref, d2_ref, g_ref, dh_ref, dhb_ref, gn_ref, acc_ref):
        i, kk = pl.program_id(0), pl.program_id(1)

        @pl.when((i == 0) & (kk == 0))
        def _():
            gn_ref[...] = jnp.zeros_like(gn_ref)

        @pl.when(kk == 0)
        def _():
            acc_ref[...] = jnp.zeros_like(acc_ref)

        acc_ref[...] += _dot(dg_ref[...], wg_ref[...]) + _dot(du_ref[...], wu_ref[...])

        @pl.when(kk == nk - 1)
        def _():
            dx, dg_rows = _rms_bwd(h_ref[...], g_ref[...], acc_ref[...])
            dh = d2_ref[...] + dx
            dh_ref[...] = dh
            dhb_ref[...] = dh.astype(BF16)
            gn_ref[...] += jnp.sum(dg_rows, axis=0, keepdims=True)

    act = pl.BlockSpec((tm, tk), lambda i, kk: (i, kk))
    wsp = pl.BlockSpec((tk, D_MODEL), lambda i, kk: (kk, 0))
    row = pl.BlockSpec((tm, D_MODEL), lambda i, kk: (i, 0))
    vec = pl.BlockSpec((1, D_MODEL), lambda i, kk: (0, 0))
    return _call(
        body, name="ffn_bwd_in", grid=(t // tm, nk),
        in_specs=[act, act, wsp, wsp, row, row, vec],
        out_specs=[row, row, vec],
        out_shape=[jax.ShapeDtypeStruct((t, D_MODEL), F32), jax.ShapeDtypeStruct((t, D_MODEL), BF16),
                   jax.ShapeDtypeStruct((1, D_MODEL), F32)],
        scratch=[pltpu.VMEM((tm, D_MODEL), F32)],
        semantics=("arbitrary", "arbitrary"), args=(dg, du, wg_t, wu_t, h1, dh2, g2), exchange=exchange)


def _mix_bwd(dh1b, w_out, proj, b_gate, y_ret, y_att, exchange):
    t = dh1b.shape[0]
    tm, tn = min(256, t), 512
    nb = D_MODEL // tn

    def body(d_ref, w_ref, glr_ref, gla_ref, br_ref, ba_ref, yr_ref, ya_ref,
             dyr_ref, dya_ref, dglr_ref, dgla_ref, dbr_ref, dba_ref):
        @pl.when(pl.program_id(1) == 0)
        def _():
            dbr_ref[...] = jnp.zeros_like(dbr_ref)
            dba_ref[...] = jnp.zeros_like(dba_ref)

        dz = _dot_nt(d_ref[...], w_ref[...])
        gr = _sigmoid(glr_ref[...].astype(F32) + br_ref[...])
        ga = _sigmoid(gla_ref[...].astype(F32) + ba_ref[...])
        dyr_ref[...] = (dz * gr).astype(BF16)
        dya_ref[...] = (dz * ga).astype(BF16)
        dglr = dz * yr_ref[...].astype(F32) * gr * (1.0 - gr)
        dgla = dz * ya_ref[...].astype(F32) * ga * (1.0 - ga)
        dglr_ref[...] = dglr.astype(BF16)
        dgla_ref[...] = dgla.astype(BF16)
        dbr_ref[...] += jnp.sum(dglr, axis=0, keepdims=True)
        dba_ref[...] += jnp.sum(dgla, axis=0, keepdims=True)

    blk = pl.BlockSpec((tm, tn), lambda j, i: (i, j))
    vec = pl.BlockSpec((1, tn), lambda j, i: (0, j))
    return _call(
        body, name="mix_bwd", grid=(nb, t // tm),
        in_specs=[pl.BlockSpec((tm, D_MODEL), lambda j, i: (i, 0)),
                  pl.BlockSpec((tn, D_MODEL), lambda j, i: (j, 0)),
                  pl.BlockSpec((tm, tn), lambda j, i: (i, C_GL // tn + j)),
                  pl.BlockSpec((tm, tn), lambda j, i: (i, C_GL // tn + nb + j)),
                  vec, pl.BlockSpec((1, tn), lambda j, i: (0, nb + j)), blk, blk],
        out_specs=[blk, blk, blk, blk, vec, vec],
        out_shape=[jax.ShapeDtypeStruct((t, D_MODEL), BF16)] * 4 + [jax.ShapeDtypeStruct((1, D_MODEL), F32)] * 2,
        semantics=("arbitrary", "arbitrary"), args=(dh1b, w_out, proj, proj, b_gate, b_gate, y_ret, y_att),
        exchange=exchange)


def _ret_bwd(dgro, proj, o_ret, qr, kr, cs, sn, lg_arr, batch, seq, exchange):
    t = batch * seq
    nt = seq // RET_TILE

    def body(dgro_ref, rg_ref, o_ref, qr_ref, kr_ref, v_ref, cs_ref, sn_ref, lg_ref,
             dq_ref, dk_ref, dv_ref, drg_ref, do_ref, st_ref):
        lg = lg_ref[:, 0:1]
        inside, q_dec, k_dec, tile_dec = _decay(lg)

        state = jnp.zeros((RET_KEY_DIM, RET_VAL_DIM), F32)
        for i in range(nt - 1):
            rows = slice(i * RET_TILE, (i + 1) * RET_TILE)
            state = state * tile_dec + _dot_tn(_scaled(kr_ref[rows, :], k_dec), v_ref[rows, :])
            st_ref[i + 1] = state.astype(BF16)

        for i in range(nt):
            rows = slice(i * RET_TILE, (i + 1) * RET_TILE)
            o = o_ref[rows, :]
            xc = o - jnp.mean(o, axis=-1, keepdims=True)
            rs = lax.rsqrt(jnp.mean(xc * xc, axis=-1, keepdims=True) + EPS)
            nrm = xc * rs
            rg = rg_ref[rows, :].astype(F32)
            sg = _sigmoid(rg)
            dg = dgro_ref[rows, :].astype(F32)
            drg_ref[rows, :] = (dg * nrm * sg * (1.0 + rg * (1.0 - sg))).astype(BF16)
            dn = dg * rg * sg
            do = rs * (dn - jnp.mean(dn, axis=-1, keepdims=True)
                       - nrm * jnp.mean(dn * nrm, axis=-1, keepdims=True))
            do_ref[rows, :] = do.astype(BF16)

        dstate = jnp.zeros((RET_KEY_DIM, RET_VAL_DIM), F32)
        for i in reversed(range(nt)):
            rows = slice(i * RET_TILE, (i + 1) * RET_TILE)
            qi, ki, vi, doi = qr_ref[rows, :], kr_ref[rows, :], v_ref[rows, :], do_ref[rows, :]
            p = (_dot_nt(qi, ki) * inside).astype(BF16)
            dp = (_dot_nt(doi, vi) * inside).astype(BF16)
            dq = _dot(dp, ki)
            dk = _dot_tn(dp, qi)
            dv = _dot_tn(p, doi)
            if i > 0:
                dq = dq + _dot_nt(doi, st_ref[i]) * q_dec
            if i < nt - 1:
                dsb = dstate.astype(BF16)
                dk = dk + _dot_nt(vi, dsb) * k_dec
                dv = dv + _dot(_scaled(ki, k_dec), dsb)
            if i > 0:
                dstate = dstate * tile_dec + _dot_tn(_scaled(qi, q_dec), doi)
            dq_ref[rows, :] = (dq * cs_ref[rows, :] - pltpu.roll(dq, 64, 1) * sn_ref[rows, :]).astype(BF16)
            dk = (dk * cs_ref[rows, :] - pltpu.roll(dk, 64, 1) * sn_ref[rows, :]) * (RET_KEY_DIM ** -0.5)
            dk_ref[rows, :] = dk.astype(BF16)
            dv_ref[rows, :] = dv.astype(BF16)

    key = pl.BlockSpec((seq, RET_KEY_DIM), lambda b, h: (b, h))
    val = pl.BlockSpec((seq, RET_VAL_DIM), lambda b, h: (b, h))
    tab = pl.BlockSpec((seq, RET_KEY_DIM), lambda b, h: (0, 0))
    return _call(
        body, name="ret_bwd", grid=(batch, RET_HEADS),
        in_specs=[val, pl.BlockSpec((seq, RET_VAL_DIM), lambda b, h: (b, C_RG // RET_VAL_DIM + h)), val, key, key,
                  pl.BlockSpec((seq, RET_VAL_DIM), lambda b, h: (b, C_RV // RET_VAL_DIM + h)), tab, tab,
                  pl.BlockSpec((None, 1, LANES), lambda b, h: (h, 0, 0))],
        out_specs=[key, key, val, val],
        out_shape=[jax.ShapeDtypeStruct((t, RET_HEADS * RET_KEY_DIM), BF16)] * 2
                  + [jax.ShapeDtypeStruct((t, RET_HEADS * RET_VAL_DIM), BF16)] * 2,
        scratch=[pltpu.VMEM((seq, RET_VAL_DIM), BF16), pltpu.VMEM((nt, RET_KEY_DIM, RET_VAL_DIM), BF16)],
        semantics=("parallel", "parallel"), args=(dgro, proj, o_ret, qr, kr, proj, cs, sn, lg_arr),
        exchange=exchange)


def _att_bwd(proj, bias, dao, batch, seq, exchange):
    ni, q_spec, k_spec, v_spec, w_spec, b_spec, pad = _att_specs(batch, seq)
    t = batch * seq

    def body(q_ref, k_ref, v_ref, bias_ref, do_ref, dq_ref, dk_ref, dv_ref, dw_ref,
             dbias_ref, dk_acc, dv_acc, kp_ref, vp_ref, s_ref, dp_ref, e_ref, ds_ref):
        b, i = pl.program_id(1), pl.program_id(2)

        @pl.when((b == 0) & (i == 0))
        def _():
            dbias_ref[...] = jnp.zeros_like(dbias_ref)

        @pl.when(i == 0)
        def _():
            _att_pad(k_ref, kp_ref)
            _att_pad(v_ref, vp_ref)
            dk_acc[...] = jnp.zeros_like(dk_acc)
            dv_acc[...] = jnp.zeros_like(dv_acc)

        win = pl.ds(pl.multiple_of(i * ATT_Q, ATT_Q), ATT_WIN)
        k2, v2, q2, do2 = kp_ref[win, :], vp_ref[win, :], q_ref[...], do_ref[...]
        lo = lax.broadcasted_iota(jnp.int32, (1, LANES), 1) < 64
        dq = jnp.zeros((ATT_Q, LANES), F32)
        dk = jnp.zeros((ATT_WIN, LANES), F32)
        dv = jnp.zeros((ATT_WIN, LANES), F32)
        start = jnp.minimum(i, ATT_STARTS)
        for e in range(2):
            sel = lo if e == 0 else jnp.logical_not(lo)
            qm = _att_head(q2, sel)
            dom = jnp.where(sel, do2, jnp.zeros_like(do2))
            s_ref[e] = _dot_nt(qm, k2)
            dp_ref[e] = _dot_nt(dom, v2)
            rsum = []
            for c in range(ATT_Q // ATT_ROWS):
                rows = slice(c * ATT_ROWS, (c + 1) * ATT_ROWS)
                ex, r = _att_softmax_rows(s_ref.at[e], bias_ref.at[start, e], rows)
                dp = dp_ref[e, rows, :]
                mean = jnp.sum(dp * ex, axis=-1, keepdims=True) * r
                ds = ex * ((dp - mean) * r)
                dbias_ref[e, rows, :] += ds
                ds_ref[e, rows, :] = ds.astype(BF16)
                e_ref[e, rows, :] = ex.astype(BF16)
                rsum.append(r)
            dq = dq + _dot(ds_ref[e], jnp.where(sel, k2, jnp.zeros_like(k2)))
            dk = dk + _dot_tn(ds_ref[e], qm)
            dv = dv + _dot_tn(e_ref[e], (dom.astype(F32) * jnp.concatenate(rsum, axis=0)).astype(BF16))
        dq_ref[...] = (dq * 0.125).astype(BF16)
        dk_acc[win, :] += dk
        dv_acc[win, :] += dv

        @pl.when(i == ni - 1)
        def _():
            dk_ref[...] = dk_acc[ATT_PAD:, :].astype(BF16)
            dv_ref[...] = dv_acc[ATT_PAD:, :].astype(BF16)

        @pl.when((b == batch - 1) & (i == ni - 1))
        def _():
            n_i = lax.broadcasted_iota(jnp.int32, (ATT_Q, BIAS_LEN), 0)
            for e in range(2):
                xw = jnp.concatenate([jnp.zeros((ATT_Q, BIAS_LEN - ATT_WIN), F32), dbias_ref[e]], axis=1)
                for bit in range(8):
                    xw = jnp.where(((n_i >> bit) & 1) == 1, pltpu.roll(xw, BIAS_LEN - (1 << bit), 1), xw)
                dw_ref[e:e + 1, :] = jnp.sum(xw, axis=0, keepdims=True)

    seq_blk = pl.BlockSpec((seq, LANES), lambda hp, b, i: (b, hp))
    q_out = pl.BlockSpec((ATT_Q, LANES), lambda hp, b, i: (b * ni + i, hp))
    return _call(
        body, name="att_bwd", grid=(ATT_HEADS // 2, batch, ni),
        in_specs=[q_spec, k_spec, v_spec, b_spec, q_out],
        out_specs=[q_out, seq_blk, seq_blk, w_spec],
        out_shape=[jax.ShapeDtypeStruct((t, 512), BF16)] * 3
                  + [jax.ShapeDtypeStruct((ATT_HEADS // 2, 2, BIAS_LEN), F32)],
        scratch=[pltpu.VMEM((2, ATT_Q, ATT_WIN), F32),
                 pltpu.VMEM((seq + ATT_PAD, LANES), F32), pltpu.VMEM((seq + ATT_PAD, LANES), F32), pad, pad,
                 pltpu.VMEM((2, ATT_Q, ATT_WIN), F32), pltpu.VMEM((2, ATT_Q, ATT_WIN), F32),
                 pltpu.VMEM((2, ATT_Q, ATT_WIN), BF16), pltpu.VMEM((2, ATT_Q, ATT_WIN), BF16)],
        semantics=("arbitrary", "arbitrary", "arbitrary"), args=(proj, proj, proj, bias, dao), exchange=exchange)


def _rms_in_bwd(x2, dxn, dh1, g1):
    t = x2.shape[0]
    tm = min(512, t)

    def body(x_ref, d_ref, h_ref, g_ref, dx_ref, dg_ref):
        @pl.when(pl.program_id(0) == 0)
        def _():
            dg_ref[...] = jnp.zeros_like(dg_ref)

        dx, dg_rows = _rms_bwd(x_ref[...], g_ref[...], d_ref[...])
        dx_ref[...] = h_ref[...] + dx
        dg_ref[...] += jnp.sum(dg_rows, axis=0, keepdims=True)

    row = pl.BlockSpec((tm, D_MODEL), lambda i: (i, 0))
    vec = pl.BlockSpec((1, D_MODEL), lambda i: (0, 0))
    return pl.pallas_call(
        body, name="rms_in_bwd", grid=(t // tm,),
        in_specs=[row, row, row, vec], out_specs=[row, vec],
        out_shape=[jax.ShapeDtypeStruct((t, D_MODEL), F32), jax.ShapeDtypeStruct((1, D_MODEL), F32)],
        compiler_params=_params("arbitrary"),
    )(x2, dxn, dh1, g1)


def _pack_small(dg1, dbr, dba, dg2, dg3, dw, loss):
    def body(a_ref, b_ref, c_ref, d_ref, e_ref, w_ref, l_ref, o_ref):
        o_ref[...] = jnp.zeros_like(o_ref)
        for r, ref in enumerate((a_ref, b_ref, c_ref, d_ref, e_ref)):
            o_ref[r:r + 1, :] = ref[...]
        o_ref[5:6, 0:LANES] = l_ref[...]
        for hp in range(ATT_HEADS // 2):
            o_ref[8 + 2 * hp:10 + 2 * hp, :] = w_ref[hp]

    return pl.pallas_call(body, name="pack_small",
                          out_shape=jax.ShapeDtypeStruct((16, D_MODEL), F32))(dg1, dbr, dba, dg2, dg3, dw, loss)


def _rotary_tables(seq):
    freqs = ROPE_BASE ** (-jnp.arange(0, RET_KEY_DIM, 2, dtype=F32) / RET_KEY_DIM)
    ang = jnp.arange(seq, dtype=F32)[:, None] * freqs[None, :]
    cos, sin = jnp.cos(ang), jnp.sin(ang)
    return jnp.concatenate([cos, cos], axis=1), jnp.concatenate([-sin, sin], axis=1)


def _bias_rows(rel_bias):
    n_far = BIAS_LEN - ATT_Q - MAX_REL + 1
    n_near = BIAS_LEN - n_far - (N_REL - 2)
    w = jnp.concatenate([jnp.broadcast_to(rel_bias[:, N_REL - 1:], (ATT_HEADS, n_far)),
                         rel_bias[:, 1:N_REL - 1][:, ::-1],
                         jnp.broadcast_to(rel_bias[:, :1], (ATT_HEADS, n_near))], axis=1)
    return w.reshape(ATT_HEADS // 2, 2, BIAS_LEN)


def _bias_rows_bwd(dw):
    n_far = BIAS_LEN - ATT_Q - MAX_REL + 1
    mid = dw[:, n_far:n_far + N_REL - 2][:, ::-1]
    return jnp.concatenate([jnp.sum(dw[:, n_far + N_REL - 2:], axis=1, keepdims=True), mid,
                            jnp.sum(dw[:, :n_far], axis=1, keepdims=True)], axis=1)


def _step(x, tgt, norm_mix, b_gate, norm_ffn, norm_final, rel_bias_shard, shard):
    batch, seq, _ = x.shape
    t = batch * seq
    n_rb = rel_bias_shard.shape[-1]
    x2, tgt2 = x.reshape(t, D_MODEL), tgt.reshape(t, D_MODEL)
    g3 = norm_final.reshape(1, D_MODEL)
    cs, sn = _rotary_tables(seq)
    lg = np.log(1.0 - 2.0 ** (-5.0 - np.arange(RET_HEADS, dtype=np.float32))).astype(np.float32)
    lg_arr = jnp.asarray(np.broadcast_to(lg[:, None, None], (RET_HEADS, 1, LANES)))

    def gather(*names):
        return _ChipGather([shard[nm] for nm in names])

    def scatter(*grads):
        return _Exchange(grads, scatter=True)

    rb_pad = jnp.pad(rel_bias_shard, ((0, 0), (0, LANES - n_rb)))
    (xn,), (w_in_t, rb_full) = _rms_fwd(x2, norm_mix, _ChipGather([shard["w_in_t"], rb_pad]))
    rb_full = rb_full.reshape(N_DEV, ATT_HEADS, LANES)[:, :, :n_rb]
    bias = _att_bias_tiles(_bias_rows(jnp.transpose(rb_full, (1, 0, 2)).reshape(ATT_HEADS, N_DEV * n_rb)))
    proj, (w_ret, w_att_t, w_out, w_gate_t) = _mm(
        xn, w_in_t, tb=True, out_dtype=BF16, tm=1024, tn=1664, tk=1024, name="proj",
        exchange=gather("w_ret", "w_att_t", "w_out", "w_gate_t"))
    (gro, o_ret, qr, kr), (w_up_t,) = _ret_fwd(proj, cs, sn, lg_arr, batch, seq, gather("w_up_t"))
    (ao,), (w_down,) = _att_fwd(proj, bias, batch, seq, gather("w_down"))
    z, y_ret, y_att = _mix_fwd(gro, ao, proj, b_gate, w_ret, w_att_t)
    h1, hn = _out_fwd(z, x2, w_out, norm_ffn)
    g_act, u_act, a_act = _ffn_up(hn, w_gate_t, w_up_t)
    dh2, dh2b, loss, dg3 = _ffn_down_loss(a_act, h1, tgt2, w_down, g3)

    wg = dict(out_dtype=BF16, tn=1024, ta=True)
    slots = {}
    dw_down = _mm(a_act, dh2b, tm=1408, tk=1024, name="dw_down", **wg)
    (d_gact, d_uact), (slots["w_down"],) = _ffn_bwd_act(dh2b, w_down, g_act, u_act, scatter(dw_down))
    dw_gate = _mm(d_gact, hn, tm=1408, tk=1024, name="dw_gate", **wg)
    dw_up = _mm(d_uact, hn, tm=1408, tk=1024, name="dw_up", **wg)
    (dh1, dh1b, dg2), (slots["w_gate_t"],) = _ffn_bwd_in(d_gact, d_uact, w_gate_t, w_up_t, h1, dh2, norm_ffn,
                                                       scatter(dw_gate))
    dw_out = _mm(z, dh1b, tm=1024, tk=2048, name="dw_out", **wg)
    (dyr, dya, dglr, dgla, dbr, dba), _ = _mix_bwd(dh1b, w_out, proj, b_gate, y_ret, y_att, None)
    dgro = _mm(dyr, w_ret, tb=True, out_dtype=BF16, tm=1024, tn=1024, tk=1024, name="dgro")
    dao = _mm(dya, w_att_t, out_dtype=BF16, tm=1024, tn=512, tk=1024, name="dao")
    dw_ret = _mm(gro, dyr, tm=1024, tk=2048, name="dw_ret", **wg)
    dw_att = _mm(dya, ao, tm=1024, tk=2048, name="dw_att", **wg)
    (drq, drk, drv, drg), (slots["w_up_t"],) = _ret_bwd(
        dgro, proj, o_ret, qr, kr, cs, sn, lg_arr, batch, seq, scatter(dw_up))
    (daq, dak, dav, dw), (slots["w_out"], slots["w_ret"], slots["w_att_t"]) = _att_bwd(
        proj, bias, dao, batch, seq, scatter(dw_out, dw_ret, dw_att))
    dproj = [drq, drk, drv, drg, daq, dak, dav, dglr, dgla]
    dw_in = _mm_pieces(dproj, xn, ta=True, out_dtype=BF16, tm=512, tn=1024, tk=1024, name="dw_in")
    dxn, (slots["w_in_t"],) = _mm_pieces(dproj, w_in_t, ta=False, out_dtype=F32, tm=1024, tn=1024, tk=512, name="dxn",
                                  exchange=scatter(dw_in))
    dx, dg1 = _rms_in_bwd(x2, dxn, dh1, norm_mix)
    small = _pack_small(dg1, dbr, dba, dg2, dg3, dw, loss)
    (small_slots,) = _alone(_ChipGather([small]), "gather_small")
    return dx.reshape(batch, seq, D_MODEL), slots, small_slots.reshape(N_DEV, 16, D_MODEL)


def _sum_slots(slots, name):
    _, r, c = slots.shape
    tr = max(d for d in range(16, r + 1, 16) if r % d == 0 and (d * c <= 256 * 1024 or d == 16))

    def body(s_ref, o_ref):
        acc = s_ref[0].astype(F32)
        for s in range(1, N_DEV):
            acc = acc + s_ref[s].astype(F32)
        o_ref[...] = acc

    return pl.pallas_call(
        body, name=name, grid=(r // tr,),
        in_specs=[pl.BlockSpec((N_DEV, tr, c), lambda i: (0, i, 0))],
        out_specs=pl.BlockSpec((tr, c), lambda i: (i, 0)),
        out_shape=jax.ShapeDtypeStruct((r, c), F32),
        compiler_params=_params("parallel"),
    )(slots)


def _adamw_math(w, g, m, v):
    m = ADAM_B1 * m + (1.0 - ADAM_B1) * g
    v = ADAM_B2 * v + (1.0 - ADAM_B2) * (g * g)
    m_hat = m / (1.0 - ADAM_B1 ** ADAM_STEP)
    v_hat = v / (1.0 - ADAM_B2 ** ADAM_STEP)
    return -ADAM_LR * (m_hat / (jnp.sqrt(v_hat) + ADAM_EPS) + ADAM_WD * w), m, v


def _adamw(w, g, m, v, name):
    r, c = w.shape
    tr = r
    while tr * c > 128 * 1024 and tr % 16 == 0:
        tr //= 2

    def body(w_ref, g_ref, m_ref, v_ref, d_ref, nm_ref, nv_ref):
        d_ref[...], nm_ref[...], nv_ref[...] = _adamw_math(w_ref[...], g_ref[...], m_ref[...], v_ref[...])

    blk = pl.BlockSpec((tr, c), lambda i: (i, 0))
    return pl.pallas_call(
        body, name=name, grid=(r // tr,),
        in_specs=[blk] * 4, out_specs=[blk] * 3,
        out_shape=[jax.ShapeDtypeStruct((r, c), F32)] * 3,
        compiler_params=_params("parallel"),
    )(w, g, m, v)


def _adamw_small(ws, gs, ms, vs):
    n = len(ws)

    def body(*refs):
        for i in range(n):
            w_ref, g_ref, m_ref, v_ref = (refs[j * n + i] for j in range(4))
            d_ref, nm_ref, nv_ref = (refs[(4 + j) * n + i] for j in range(3))
            d_ref[...], nm_ref[...], nv_ref[...] = _adamw_math(w_ref[...], g_ref[...], m_ref[...], v_ref[...])

    shapes = [jax.ShapeDtypeStruct(w.shape, F32) for w in ws]
    outs = pl.pallas_call(body, name="adamw_small", out_shape=shapes * 3)(*ws, *gs, *ms, *vs)
    return outs[:n], outs[n:2 * n], outs[2 * n:]


def kernel(x, norm_mix, w_in, b_gate, rel_bias, w_ret_out, w_att_out, w_out, norm_ffn, w_ffn_gate, w_ffn_up, w_ffn_down, norm_final, loss_target, m_norm_mix, m_w_in, m_b_gate, m_rel_bias, m_w_ret_out, m_w_att_out, m_w_out, m_norm_ffn, m_w_ffn_gate, m_w_ffn_up, m_w_ffn_down, m_norm_final, v_norm_mix, v_w_in, v_b_gate, v_rel_bias, v_w_ret_out, v_w_att_out, v_w_out, v_norm_ffn, v_w_ffn_gate, v_w_ffn_up, v_w_ffn_down, v_norm_final):
    me = _index(_place())
    n_rb = rel_bias.shape[-1]

    shard = dict(w_in_t=w_in[0].T, w_gate_t=w_ffn_gate[0].T, w_up_t=w_ffn_up[0].T, w_down=w_ffn_down[0],
                 w_ret=w_ret_out[0], w_out=w_out[0], w_att_t=w_att_out[0].T)
    shard = {nm: s.astype(BF16) for nm, s in shard.items()}
    dx, slots, small_slots = _step(x, loss_target, norm_mix, b_gate, norm_ffn, norm_final, rel_bias[0], shard)
    summed = {nm: _sum_slots(s, "sum_" + nm) for nm, s in slots.items()}
    small_sum = _sum_slots(small_slots, "sum_small")
    loss = small_sum[5, 0]

    transposed = dict(w_in="w_in_t", w_ffn_gate="w_gate_t", w_ffn_up="w_up_t", w_att_out="w_att_t")
    g = dict(
        w_ffn_down=summed["w_down"], w_ret_out=summed["w_ret"], w_out=summed["w_out"],
        norm_mix=small_sum[0:1], b_gate=jnp.concatenate([small_sum[1:2], small_sum[2:3]], axis=1),
        norm_ffn=small_sum[3:4], norm_final=small_sum[4:5],
        rel_bias=lax.dynamic_slice_in_dim(_bias_rows_bwd(small_sum[8:16]), me * n_rb, n_rb, axis=1),
    )
    w = dict(norm_mix=norm_mix, w_in=w_in, b_gate=b_gate, rel_bias=rel_bias, w_ret_out=w_ret_out, w_att_out=w_att_out,
             w_out=w_out, norm_ffn=norm_ffn, w_ffn_gate=w_ffn_gate, w_ffn_up=w_ffn_up, w_ffn_down=w_ffn_down,
             norm_final=norm_final)
    m = dict(norm_mix=m_norm_mix, w_in=m_w_in, b_gate=m_b_gate, rel_bias=m_rel_bias, w_ret_out=m_w_ret_out,
             w_att_out=m_w_att_out, w_out=m_w_out, norm_ffn=m_norm_ffn, w_ffn_gate=m_w_ffn_gate, w_ffn_up=m_w_ffn_up,
             w_ffn_down=m_w_ffn_down, norm_final=m_norm_final)
    v = dict(norm_mix=v_norm_mix, w_in=v_w_in, b_gate=v_b_gate, rel_bias=v_rel_bias, w_ret_out=v_w_ret_out,
             w_att_out=v_w_att_out, w_out=v_w_out, norm_ffn=v_norm_ffn, w_ffn_gate=v_w_ffn_gate, w_ffn_up=v_w_ffn_up,
             w_ffn_down=v_w_ffn_down, norm_final=v_norm_final)
    order = ("norm_mix", "w_in", "b_gate", "rel_bias", "w_ret_out", "w_att_out", "w_out", "norm_ffn",
             "w_ffn_gate", "w_ffn_up", "w_ffn_down", "norm_final")
    small_names = ("norm_mix", "b_gate", "rel_bias", "norm_ffn", "norm_final")

    def flat(a):
        return a[0] if a.ndim == 3 else a.reshape(-1, a.shape[-1])

    grad, delta, new_m, new_v = {}, {}, {}, {}
    for nm in order:
        if nm in transposed:
            g_t = summed[transposed[nm]]
            d, nmom, nvar = _adamw(w[nm][0].T, g_t, m[nm][0].T, v[nm][0].T, "adamw_" + nm)
            grad[nm], delta[nm], new_m[nm], new_v[nm] = (a.T[None] for a in (g_t, d, nmom, nvar))
        elif nm not in small_names:
            d, nmom, nvar = _adamw(flat(w[nm]), g[nm], flat(m[nm]), flat(v[nm]), "adamw_" + nm)
            grad[nm], delta[nm], new_m[nm], new_v[nm] = (a.reshape(w[nm].shape) for a in (g[nm], d, nmom, nvar))
    ds, nms, nvs = _adamw_small([flat(w[nm]) for nm in small_names], [g[nm] for nm in small_names],
                                [flat(m[nm]) for nm in small_names], [flat(v[nm]) for nm in small_names])
    for i, nm in enumerate(small_names):
        grad[nm], delta[nm], new_m[nm], new_v[nm] = (a.reshape(w[nm].shape) for a in (g[nm], ds[i], nms[i], nvs[i]))

    return (loss, dx, *[grad[nm] for nm in order], *[delta[nm] for nm in order],
            *[new_m[nm] for nm in order], *[new_v[nm] for nm in order])
```

```python
import numpy as np
import jax
import jax.numpy as jnp
from jax import lax
from jax.experimental import pallas as pl
from jax.experimental.pallas import tpu as pltpu

F32 = jnp.float32
BF16 = jnp.bfloat16
MESH = pl.DeviceIdType.MESH

D_MODEL = 1024
CHUNK = 64
RET_HEADS = 4
RET_KEY_DIM = 128
RET_VAL_DIM = 256
ATT_HEADS = 8
BAND_CHUNKS = 8
MAX_REL = 256
N_REL = CHUNK + MAX_REL
D_FF = 2816
N_IN = 6656
ROPE_BASE = 10000.0
EPS = 1e-6
NEG_INF = -1e30
C_RQ, C_RK, C_RV, C_RG, C_AQ, C_AK, C_AV, C_GL = 0, 512, 1024, 2048, 3072, 3584, 4096, 4608

ADAM_LR = 0.001
ADAM_B1 = 0.9
ADAM_B2 = 0.999
ADAM_EPS = 1e-08
ADAM_WD = 0.01
ADAM_STEP = 10

N_DEV = 8
LANES = 128
RET_TILE = 256
ATT_Q = 256
ATT_PAD = BAND_CHUNKS * CHUNK
ATT_WIN = ATT_PAD + ATT_Q
ATT_STARTS = ATT_PAD // ATT_Q
ATT_ROWS = 32
BIAS_LEN = 1024
VMEM_LIMIT = 48 * 1024 * 1024


def _params(*sem):
    return pltpu.CompilerParams(dimension_semantics=sem, vmem_limit_bytes=VMEM_LIMIT)


def _dot(a, b):
    return lax.dot_general(a, b, (((1,), (0,)), ((), ())), preferred_element_type=F32)


def _dot_nt(a, b):
    return lax.dot_general(a, b, (((1,), (1,)), ((), ())), preferred_element_type=F32)


def _dot_tn(a, b):
    return lax.dot_general(a, b, (((0,), (0,)), ((), ())), preferred_element_type=F32)


def _sigmoid(x):
    return 1.0 / (1.0 + jnp.exp(-x))


def _rms_bwd(x, g, dy):
    r = lax.rsqrt(jnp.mean(x * x, axis=-1, keepdims=True) + EPS)
    u = dy * g
    dx = r * u - x * (r * r * r) * jnp.mean(u * x, axis=-1, keepdims=True)
    return dx, dy * x * r


def _place():
    return lax.axis_index("x"), lax.axis_index("y"), lax.axis_index("c")


def _peer(k):
    x, y, c = _place()
    return ((1 - x) if k & 4 else x, (1 - y) if k & 2 else y, (1 - c) if k & 1 else c)


def _index(place):
    return 4 * place[0] + 2 * place[1] + place[2]


def _rows(ref, block, nrows):
    align = 16 if ref.dtype == BF16 else 8
    return ref.at[pl.ds(pl.multiple_of(block * nrows, align), nrows)]


class _Exchange:
    def __init__(self, arrays, scatter):
        self.arrays, self.scatter, self.n = list(arrays), scatter, len(arrays)

    def out_shape(self):
        if self.scatter:
            return [jax.ShapeDtypeStruct((N_DEV, a.shape[0] // N_DEV) + a.shape[1:], a.dtype) for a in self.arrays]
        return [jax.ShapeDtypeStruct((N_DEV * a.shape[0],) + a.shape[1:], a.dtype) for a in self.arrays]

    def scratch(self):
        return [pltpu.SemaphoreType.DMA((self.n, N_DEV - 1)), pltpu.SemaphoreType.DMA((self.n, N_DEV - 1)),
                pltpu.SemaphoreType.DMA((self.n,))]

    def _copies(self, ins, outs, sems):
        send_sems, recv_sems, local_sems = sems
        me = _index(_place())

        def src(w, to):
            return _rows(ins[w], to, ins[w].shape[0] // N_DEV) if self.scatter else ins[w]

        def dst(w, origin):
            return outs[w].at[origin] if self.scatter else _rows(outs[w], origin, ins[w].shape[0])

        def remote(w, k, to, origin):
            return pltpu.make_async_remote_copy(src_ref=src(w, to), dst_ref=dst(w, origin),
                                                send_sem=send_sems.at[w, k - 1], recv_sem=recv_sems.at[w, k - 1],
                                                device_id=_peer(k), device_id_type=MESH)

        pairs = [(w, k) for w in range(self.n) for k in range(1, N_DEV)]
        own = lambda: [pltpu.make_async_copy(src(w, me), dst(w, me), local_sems.at[w]) for w in range(self.n)]
        sent = lambda: [remote(w, k, _index(_peer(k)), me) for w, k in pairs]
        arriving = lambda: [remote(w, k, me, _index(_peer(k))) for w, k in pairs]
        return own, sent, arriving

    def start(self, ins, outs, sems):
        own, sent, _ = self._copies(ins, outs, sems)
        for cp in own() + sent():
            cp.start()

    def wait(self, ins, outs, sems):
        own, sent, arriving = self._copies(ins, outs, sems)
        for cp in arriving():
            cp.wait_recv()
        for cp in sent():
            cp.wait_send()
        for cp in own():
            cp.wait()


class _PairSwap:
    def __init__(self, arrays):
        self.arrays, self.n = list(arrays), len(arrays)

    def out_shape(self):
        return [jax.ShapeDtypeStruct((4, a.shape[0] // N_DEV) + a.shape[1:], a.dtype) for a in self.arrays]

    def scratch(self):
        return [pltpu.SemaphoreType.DMA((self.n, 4)), pltpu.SemaphoreType.DMA((self.n, 4))]

    def _copies(self, ins, outs, sems):
        send_sems, recv_sems = sems
        x, y, c = _place()
        return [pltpu.make_async_remote_copy(
            src_ref=_rows(ins[w], 2 * q + 1 - c, ins[w].shape[0] // N_DEV), dst_ref=outs[w].at[q],
            send_sem=send_sems.at[w, q], recv_sem=recv_sems.at[w, q],
            device_id=(x, y, 1 - c), device_id_type=MESH) for w in range(self.n) for q in range(4)]

    def start(self, ins, outs, sems):
        for cp in self._copies(ins, outs, sems):
            cp.start()

    def wait(self, ins, outs, sems):
        for cp in self._copies(ins, outs, sems):
            cp.wait()


class _ChipScatter:
    def __init__(self, arrays):
        self.arrays, self.n = list(arrays), len(arrays)

    def out_shape(self):
        return [jax.ShapeDtypeStruct(a.shape, a.dtype) for a in self.arrays]

    def scratch(self):
        return [pltpu.SemaphoreType.DMA((self.n, 3)), pltpu.SemaphoreType.DMA((self.n, 3)),
                pltpu.SemaphoreType.DMA((self.n,))]

    def _copies(self, ins, outs, sems):
        send_sems, recv_sems, local_sems = sems
        x, y, c = _place()
        mine = 2 * x + y
        sent, arriving = [], []
        for w in range(self.n):
            for k in range(1, 4):
                tx, ty = (1 - x) if k & 2 else x, (1 - y) if k & 1 else y
                other = 2 * tx + ty
                sent.append(lambda w=w, k=k, tx=tx, ty=ty, other=other: pltpu.make_async_remote_copy(
                    src_ref=ins[w].at[other], dst_ref=outs[w].at[mine], send_sem=send_sems.at[w, k - 1],
                    recv_sem=recv_sems.at[w, k - 1], device_id=(tx, ty, c), device_id_type=MESH))
                arriving.append(lambda w=w, k=k, tx=tx, ty=ty, other=other: pltpu.make_async_remote_copy(
                    src_ref=ins[w].at[mine], dst_ref=outs[w].at[other], send_sem=send_sems.at[w, k - 1],
                    recv_sem=recv_sems.at[w, k - 1], device_id=(tx, ty, c), device_id_type=MESH))
        own = [lambda w=w: pltpu.make_async_copy(ins[w].at[mine], outs[w].at[mine], local_sems.at[w])
               for w in range(self.n)]
        return own, sent, arriving

    def start(self, ins, outs, sems):
        own, sent, _ = self._copies(ins, outs, sems)
        for cp in own + sent:
            cp().start()

    def wait(self, ins, outs, sems):
        own, sent, arriving = self._copies(ins, outs, sems)
        for cp in arriving:
            cp().wait_recv()
        for cp in sent:
            cp().wait_send()
        for cp in own:
            cp().wait()


class _ChipGather:
    def __init__(self, arrays):
        self.arrays, self.n = list(arrays), len(arrays)

    def out_shape(self):
        return [jax.ShapeDtypeStruct((N_DEV * a.shape[0],) + a.shape[1:], a.dtype) for a in self.arrays]

    def scratch(self):
        return [pltpu.SemaphoreType.DMA((self.n, N_DEV - 1)), pltpu.SemaphoreType.DMA((self.n, N_DEV - 1)),
                pltpu.SemaphoreType.DMA((self.n,))]

    def _parts(self, ins, outs, sems):
        send_sems, recv_sems, local_sems = sems
        x, y, c = _place()
        me, sibling = (x, y, c), (x, y, 1 - c)
        chips = [(1 - x, y), (x, 1 - y), (1 - x, 1 - y)]

        def rows(w, place):
            return _rows(outs[w], _index(place), ins[w].shape[0])

        def copy(w, k, block, to, own=False):
            return pltpu.make_async_remote_copy(src_ref=ins[w] if own else rows(w, block), dst_ref=rows(w, block),
                                                send_sem=send_sems.at[w, k], recv_sem=recv_sems.at[w, k],
                                                device_id=to, device_id_type=MESH)

        def local(w):
            return pltpu.make_async_copy(ins[w], rows(w, me), local_sems.at[w])

        return me, sibling, chips, c, copy, local

    def start(self, ins, outs, sems):
        me, sibling, chips, c, copy, local = self._parts(ins, outs, sems)
        for w in range(self.n):
            local(w).start()
            copy(w, 0, me, sibling, own=True).start()
            for j, chip in enumerate(chips):
                copy(w, 1 + j, me, (*chip, c), own=True).start()

    def wait(self, ins, outs, sems):
        me, sibling, chips, c, copy, local = self._parts(ins, outs, sems)
        for w in range(self.n):
            for j, chip in enumerate(chips):
                copy(w, 1 + j, (*chip, c), me).wait_recv()
                copy(w, 4 + j, (*chip, c), sibling).start()
        for w in range(self.n):
            copy(w, 0, sibling, me).wait_recv()
            for j, chip in enumerate(chips):
                copy(w, 4 + j, (*chip, 1 - c), me).wait_recv()
            copy(w, 0, me, sibling, own=True).wait_send()
            for j, chip in enumerate(chips):
                copy(w, 1 + j, me, (*chip, c), own=True).wait_send()
                copy(w, 4 + j, (*chip, c), sibling).wait_send()
            local(w).wait()


def _call(body, *, name, grid, in_specs, out_specs, out_shape, scratch=(), semantics, args, exchange=None):
    if exchange is None:
        return pl.pallas_call(body, name=name, grid=grid, in_specs=in_specs, out_specs=out_specs, out_shape=out_shape,
                              scratch_shapes=list(scratch), compiler_params=_params(*semantics))(*args), None
    n_in, n_out, n_scr, nx = len(in_specs), len(out_specs), len(scratch), exchange.n

    def full_body(*refs):
        ins, refs = refs[:n_in], refs[n_in:]
        x_in, refs = refs[:nx], refs[nx:]
        outs, refs = refs[:n_out], refs[n_out:]
        x_out, refs = refs[:nx], refs[nx:]
        scr, sems = refs[:n_scr], refs[n_scr:]
        first, last = True, True
        for axis, size in enumerate(grid):
            first = jnp.logical_and(first, pl.program_id(axis) == 0)
            last = jnp.logical_and(last, pl.program_id(axis) == size - 1)
        if grid:
            pl.when(first)(lambda: exchange.start(x_in, x_out, sems))
        else:
            exchange.start(x_in, x_out, sems)
        body(*ins, *outs, *scr)
        if grid:
            pl.when(last)(lambda: exchange.wait(x_in, x_out, sems))
        else:
            exchange.wait(x_in, x_out, sems)

    hbm = pl.BlockSpec(memory_space=pltpu.HBM)
    res = pl.pallas_call(
        full_body, name=name, grid=grid,
        in_specs=list(in_specs) + [hbm] * nx, out_specs=list(out_specs) + [hbm] * nx,
        out_shape=list(out_shape) + exchange.out_shape(),
        scratch_shapes=list(scratch) + exchange.scratch(),
        compiler_params=_params(*(["arbitrary"] * len(grid))),
    )(*args, *exchange.arrays)
    return res[:n_out], res[n_out:]


def _alone(exchange, name):
    return _call(lambda: None, name=name, grid=(), in_specs=[], out_specs=[], out_shape=[], semantics=(),
                 args=(), exchange=exchange)[1]


def _mm(a, b, *, ta=False, tb=False, out_dtype, tm, tn, tk, name, exchange=None):
    m, k = (a.shape[1], a.shape[0]) if ta else a.shape
    n = b.shape[0] if tb else b.shape[1]
    assert k == (b.shape[1] if tb else b.shape[0])
    tm, tn, tk = min(tm, m), min(tn, n), min(tk, k)
    assert m % tm == 0 and n % tn == 0 and k % tk == 0, (name, m, n, k)
    nk = k // tk
    dims = (((0 if ta else 1,), (1 if tb else 0,)), ((), ()))

    def body(a_ref, b_ref, o_ref, *acc):
        prod = lax.dot_general(a_ref[...].astype(BF16), b_ref[...].astype(BF16), dims, preferred_element_type=F32)
        if nk == 1:
            o_ref[...] = prod.astype(o_ref.dtype)
            return
        acc_ref, kk = acc[0], pl.program_id(2)

        @pl.when(kk == 0)
        def _():
            acc_ref[...] = prod

        @pl.when((kk > 0) & (kk < nk - 1))
        def _():
            acc_ref[...] += prod

        @pl.when(kk == nk - 1)
        def _():
            o_ref[...] = (acc_ref[...] + prod).astype(o_ref.dtype)

    a_spec = (pl.BlockSpec((tk, tm), lambda i, j, kk: (kk, i)) if ta
              else pl.BlockSpec((tm, tk), lambda i, j, kk: (i, kk)))
    b_spec = (pl.BlockSpec((tn, tk), lambda i, j, kk: (j, kk)) if tb
              else pl.BlockSpec((tk, tn), lambda i, j, kk: (kk, j)))
    (out,), moved = _call(
        body, name=name, grid=(m // tm, n // tn, nk),
        in_specs=[a_spec, b_spec],
        out_specs=[pl.BlockSpec((tm, tn), lambda i, j, kk: (i, j))],
        out_shape=[jax.ShapeDtypeStruct((m, n), out_dtype)],
        scratch=[pltpu.VMEM((tm, tn), F32)] if nk > 1 else [],
        semantics=("parallel", "parallel", "arbitrary"), args=(a, b), exchange=exchange)
    return out if exchange is None else (out, moved)


def _mm_pieces(pieces, b, *, ta, out_dtype, tm, tn, tk, name, exchange=None):
    rows, n = pieces[0].shape[0], b.shape[1]
    step = tm if ta else tk
    assert all(p.shape[0] == rows and p.shape[1] % step == 0 for p in pieces), name
    edges = [int(e) for e in np.cumsum([0] + [p.shape[1] // step for p in pieces])]
    total = edges[-1] * step
    m, k = (total, rows) if ta else (rows, total)
    assert b.shape[0] == k and m % tm == 0 and n % tn == 0 and k % tk == 0, name
    nk, npieces = k // tk, len(pieces)
    dims = (((0 if ta else 1,), (0,)), ((), ()))

    def body(*refs):
        a_refs, (b_ref, o_ref, acc_ref) = refs[:npieces], refs[npieces:]
        kk = pl.program_id(2)
        pos = pl.program_id(0) if ta else kk

        @pl.when(kk == 0)
        def _():
            acc_ref[...] = jnp.zeros_like(acc_ref)

        for p, a_ref in enumerate(a_refs):
            @pl.when((pos >= edges[p]) & (pos < edges[p + 1]))
            def _(a_ref=a_ref):
                acc_ref[...] += lax.dot_general(a_ref[...], b_ref[...], dims, preferred_element_type=F32)

        @pl.when(kk == nk - 1)
        def _():
            o_ref[...] = acc_ref[...].astype(o_ref.dtype)

    def a_spec(p):
        lo, last = edges[p], edges[p + 1] - edges[p] - 1
        if ta:
            def index(i, j, kk):
                inside = (i >= lo) & (i <= lo + last)
                return jnp.where(inside, kk, 0), jnp.clip(i - lo, 0, last)
            return pl.BlockSpec((tk, tm), index)
        return pl.BlockSpec((tm, tk), lambda i, j, kk: (i, jnp.clip(kk - lo, 0, last)))

    (out,), moved = _call(
        body, name=name, grid=(m // tm, n // tn, nk),
        in_specs=[a_spec(p) for p in range(npieces)] + [pl.BlockSpec((tk, tn), lambda i, j, kk: (kk, j))],
        out_specs=[pl.BlockSpec((tm, tn), lambda i, j, kk: (i, j))],
        out_shape=[jax.ShapeDtypeStruct((m, n), out_dtype)],
        scratch=[pltpu.VMEM((tm, tn), F32)],
        semantics=("parallel", "parallel", "arbitrary"), args=(*pieces, b), exchange=exchange)
    return out if exchange is None else (out, moved)


def _rms_fwd(x2, g, exchange):
    t = x2.shape[0]
    tm = min(512, t)

    def body(x_ref, g_ref, o_ref):
        x = x_ref[...]
        r = lax.rsqrt(jnp.mean(x * x, axis=-1, keepdims=True) + EPS)
        o_ref[...] = (x * r * g_ref[...]).astype(o_ref.dtype)

    return _call(
        body, name="rms_in_fwd", grid=(t // tm,),
        in_specs=[pl.BlockSpec((tm, D_MODEL), lambda i: (i, 0)), pl.BlockSpec((1, D_MODEL), lambda i: (0, 0))],
        out_specs=[pl.BlockSpec((tm, D_MODEL), lambda i: (i, 0))],
        out_shape=[jax.ShapeDtypeStruct((t, D_MODEL), BF16)],
        semantics=("parallel",), args=(x2, g), exchange=exchange)


def _decay(lg):
    row = lax.broadcasted_iota(jnp.int32, (RET_TILE, RET_TILE), 0)
    col = lax.broadcasted_iota(jnp.int32, (RET_TILE, RET_TILE), 1)
    within = jnp.exp(lg * jnp.abs(row - col).astype(F32))
    inside = jnp.where((col >> 6) <= (row >> 6), within, 0.0)
    pos = lax.broadcasted_iota(jnp.int32, (RET_TILE, 1), 0).astype(F32)
    q_dec = jnp.exp(lg * (pos + 1.0))
    k_dec = jnp.exp(lg * (RET_TILE - 1.0 - pos))
    tile_dec = jnp.exp(lg * float(RET_TILE))
    return inside, q_dec, k_dec, tile_dec


def _scaled(a_bf16, dec):
    return (a_bf16.astype(F32) * dec).astype(BF16)


def _ret_fwd(proj, cs, sn, lg_arr, batch, seq, exchange):
    t = batch * seq
    nt = seq // RET_TILE

    def body(q_ref, k_ref, v_ref, rg_ref, cs_ref, sn_ref, lg_ref, gro_ref, o_ref, qr_ref, kr_ref):
        lg = lg_ref[:, 0:1]
        cs_t, sn_t = cs_ref[...], sn_ref[...]
        q = q_ref[...].astype(F32)
        k = k_ref[...].astype(F32)
        qr_ref[...] = (q * cs_t + pltpu.roll(q, 64, 1) * sn_t).astype(BF16)
        kr_ref[...] = ((k * cs_t + pltpu.roll(k, 64, 1) * sn_t) * (RET_KEY_DIM ** -0.5)).astype(BF16)
        inside, q_dec, k_dec, tile_dec = _decay(lg)
        state = jnp.zeros((RET_KEY_DIM, RET_VAL_DIM), F32)
        for i in range(nt):
            rows = slice(i * RET_TILE, (i + 1) * RET_TILE)
            qi, ki, vi = qr_ref[rows, :], kr_ref[rows, :], v_ref[rows, :]
            acc = _dot((_dot_nt(qi, ki) * inside).astype(BF16), vi)
            if i > 0:
                acc = acc + _dot(_scaled(qi, q_dec), state.astype(BF16))
            if i < nt - 1:
                state = state * tile_dec + _dot_tn(_scaled(ki, k_dec), vi)
            o_ref[rows, :] = acc
            xc = acc - jnp.mean(acc, axis=-1, keepdims=True)
            nrm = xc * lax.rsqrt(jnp.mean(xc * xc, axis=-1, keepdims=True) + EPS)
            rg = rg_ref[rows, :].astype(F32)
            gro_ref[rows, :] = (rg * _sigmoid(rg) * nrm).astype(BF16)

    def col(base, width):
        return lambda b, h: (b, base // width + h)

    return _call(
        body, name="ret_fwd", grid=(batch, RET_HEADS),
        in_specs=[pl.BlockSpec((seq, RET_KEY_DIM), col(C_RQ, RET_KEY_DIM)),
                  pl.BlockSpec((seq, RET_KEY_DIM), col(C_RK, RET_KEY_DIM)),
                  pl.BlockSpec((seq, RET_VAL_DIM), col(C_RV, RET_VAL_DIM)),
                  pl.BlockSpec((seq, RET_VAL_DIM), col(C_RG, RET_VAL_DIM)),
                  pl.BlockSpec((seq, RET_KEY_DIM), lambda b, h: (0, 0)),
                  pl.BlockSpec((seq, RET_KEY_DIM), lambda b, h: (0, 0)),
                  pl.BlockSpec((None, 1, LANES), lambda b, h: (h, 0, 0))],
        out_specs=[pl.BlockSpec((seq, RET_VAL_DIM), lambda b, h: (b, h)),
                   pl.BlockSpec((seq, RET_VAL_DIM), lambda b, h: (b, h)),
                   pl.BlockSpec((seq, RET_KEY_DIM), lambda b, h: (b, h)),
                   pl.BlockSpec((seq, RET_KEY_DIM), lambda b, h: (b, h))],
        out_shape=[jax.ShapeDtypeStruct((t, RET_HEADS * RET_VAL_DIM), BF16),
                   jax.ShapeDtypeStruct((t, RET_HEADS * RET_VAL_DIM), F32),
                   jax.ShapeDtypeStruct((t, RET_HEADS * RET_KEY_DIM), BF16),
                   jax.ShapeDtypeStruct((t, RET_HEADS * RET_KEY_DIM), BF16)],
        semantics=("parallel", "parallel"), args=(proj, proj, proj, proj, cs, sn, lg_arr), exchange=exchange)


def _att_bias(w_ref, bias_ref):
    n_i = lax.broadcasted_iota(jnp.int32, (ATT_Q, BIAS_LEN), 0)
    qc = lax.broadcasted_iota(jnp.int32, (ATT_Q, ATT_WIN), 0) >> 6
    kc = lax.broadcasted_iota(jnp.int32, (ATT_Q, ATT_WIN), 1) >> 6
    dc = qc + BAND_CHUNKS - kc
    band = (dc >= 0) & (dc <= BAND_CHUNKS)
    key = lax.broadcasted_iota(jnp.int32, (ATT_Q, ATT_WIN), 1)
    for e in range(2):
        xw = jnp.broadcast_to(w_ref[e:e + 1, :], (ATT_Q, BIAS_LEN))
        for bit in range(8):
            xw = jnp.where(((n_i >> bit) & 1) == 1, pltpu.roll(xw, 1 << bit, 1), xw)
        bias = jnp.where(band, xw[:, BIAS_LEN - ATT_WIN:], NEG_INF)
        for first in range(ATT_STARTS):
            bias_ref[first, e] = jnp.where(key + (first * ATT_Q - ATT_PAD) >= 0, bias, NEG_INF)
        bias_ref[ATT_STARTS, e] = bias


def _att_specs(batch, seq):
    ni = seq // ATT_Q
    q_spec = pl.BlockSpec((ATT_Q, LANES), lambda hp, b, i: (b * ni + i, C_AQ // LANES + hp))
    k_spec = pl.BlockSpec((seq, LANES), lambda hp, b, i: (b, C_AK // LANES + hp))
    v_spec = pl.BlockSpec((seq, LANES), lambda hp, b, i: (b, C_AV // LANES + hp))
    w_spec = pl.BlockSpec((None, 2, BIAS_LEN), lambda hp, b, i: (hp, 0, 0))
    b_spec = pl.BlockSpec((None, ATT_STARTS + 1, 2, ATT_Q, ATT_WIN), lambda hp, b, i: (hp, 0, 0, 0, 0))
    pad = pltpu.VMEM((seq + ATT_PAD, LANES), BF16)
    return ni, q_spec, k_spec, v_spec, w_spec, b_spec, pad


def _att_bias_tiles(wvec):
    return pl.pallas_call(
        _att_bias, name="att_bias", grid=(ATT_HEADS // 2,),
        in_specs=[pl.BlockSpec((None, 2, BIAS_LEN), lambda hp: (hp, 0, 0))],
        out_specs=pl.BlockSpec((None, ATT_STARTS + 1, 2, ATT_Q, ATT_WIN), lambda hp: (hp, 0, 0, 0, 0)),
        out_shape=jax.ShapeDtypeStruct((ATT_HEADS // 2, ATT_STARTS + 1, 2, ATT_Q, ATT_WIN), F32),
        compiler_params=_params("parallel"),
    )(wvec)


def _att_pad(src_ref, pad_ref):
    pad_ref[:ATT_PAD, :] = jnp.zeros((ATT_PAD, LANES), BF16)
    pad_ref[ATT_PAD:, :] = src_ref[...]


def _att_head(q2, sel):
    return jnp.where(sel, q2, jnp.zeros_like(q2)) * 0.125


def _att_softmax_rows(s_ref, bias_ref, rows):
    s = s_ref[rows, :] + bias_ref[rows, :]
    ex = jnp.exp(s - jnp.max(s, axis=-1, keepdims=True))
    return ex, 1.0 / jnp.sum(ex, axis=-1, keepdims=True)


def _att_fwd(proj, bias, batch, seq, exchange):
    ni, q_spec, k_spec, v_spec, _, b_spec, pad = _att_specs(batch, seq)

    def body(q_ref, k_ref, v_ref, bias_ref, o_ref, kp_ref, vp_ref, s_ref, e_ref):
        i = pl.program_id(2)

        @pl.when(i == 0)
        def _():
            _att_pad(k_ref, kp_ref)
            _att_pad(v_ref, vp_ref)

        win = pl.ds(pl.multiple_of(i * ATT_Q, ATT_Q), ATT_WIN)
        k2, v2, q2 = kp_ref[win, :], vp_ref[win, :], q_ref[...]
        lo = lax.broadcasted_iota(jnp.int32, (1, LANES), 1) < 64
        start = jnp.minimum(i, ATT_STARTS)
        out = jnp.zeros((ATT_Q, LANES), F32)
        for e in range(2):
            sel = lo if e == 0 else jnp.logical_not(lo)
            s_ref[e] = _dot_nt(_att_head(q2, sel), k2)
            rsum = []
            for c in range(ATT_Q // ATT_ROWS):
                rows = slice(c * ATT_ROWS, (c + 1) * ATT_ROWS)
                ex, r = _att_softmax_rows(s_ref.at[e], bias_ref.at[start, e], rows)
                e_ref[e, rows, :] = ex.astype(BF16)
                rsum.append(r)
            out = out + _dot(e_ref[e], jnp.where(sel, v2, jnp.zeros_like(v2))) * jnp.concatenate(rsum, axis=0)
        o_ref[...] = out.astype(BF16)

    return _call(
        body, name="att_fwd", grid=(ATT_HEADS // 2, batch, ni),
        in_specs=[q_spec, k_spec, v_spec, b_spec],
        out_specs=[pl.BlockSpec((ATT_Q, LANES), lambda hp, b, i: (b * ni + i, hp))],
        out_shape=[jax.ShapeDtypeStruct((batch * seq, ATT_HEADS * 64), BF16)],
        scratch=[pad, pad, pltpu.VMEM((2, ATT_Q, ATT_WIN), F32), pltpu.VMEM((2, ATT_Q, ATT_WIN), BF16)],
        semantics=("arbitrary", "arbitrary", "arbitrary"), args=(proj, proj, proj, bias), exchange=exchange)


def _mix_fwd(gro, ao, proj, b_gate, w_ret, w_att_t):
    t = gro.shape[0]
    tm, tn = min(256, t), 512

    def body(gro_ref, ao_ref, glr_ref, gla_ref, br_ref, ba_ref, wr_ref, wa_ref, z_ref, yr_ref, ya_ref):
        yr = _dot(gro_ref[...], wr_ref[...])
        ya = _dot_nt(ao_ref[...], wa_ref[...])
        gr = _sigmoid(glr_ref[...].astype(F32) + br_ref[...])
        ga = _sigmoid(gla_ref[...].astype(F32) + ba_ref[...])
        z_ref[...] = (gr * yr + ga * ya).astype(BF16)
        yr_ref[...] = yr.astype(BF16)
        ya_ref[...] = ya.astype(BF16)

    nb = D_MODEL // tn
    out = pl.BlockSpec((tm, tn), lambda i, j: (i, j))
    return pl.pallas_call(
        body, name="mix_fwd", grid=(t // tm, nb),
        in_specs=[pl.BlockSpec((tm, D_MODEL), lambda i, j: (i, 0)),
                  pl.BlockSpec((tm, 512), lambda i, j: (i, 0)),
                  pl.BlockSpec((tm, tn), lambda i, j: (i, C_GL // tn + j)),
                  pl.BlockSpec((tm, tn), lambda i, j: (i, C_GL // tn + nb + j)),
                  pl.BlockSpec((1, tn), lambda i, j: (0, j)),
                  pl.BlockSpec((1, tn), lambda i, j: (0, nb + j)),
                  pl.BlockSpec((D_MODEL, tn), lambda i, j: (0, j)),
                  pl.BlockSpec((tn, 512), lambda i, j: (j, 0))],
        out_specs=[out, out, out],
        out_shape=[jax.ShapeDtypeStruct((t, D_MODEL), BF16)] * 3,
        compiler_params=_params("parallel", "parallel"),
    )(gro, ao, proj, proj, b_gate, b_gate, w_ret, w_att_t)


def _out_fwd(z, x2, w_out, g2):
    t = z.shape[0]
    tm = min(256, t)

    def body(z_ref, x_ref, w_ref, g_ref, h_ref, hn_ref):
        h = x_ref[...] + _dot(z_ref[...], w_ref[...])
        h_ref[...] = h
        r = lax.rsqrt(jnp.mean(h * h, axis=-1, keepdims=True) + EPS)
        hn_ref[...] = (h * r * g_ref[...]).astype(BF16)

    row = pl.BlockSpec((tm, D_MODEL), lambda i: (i, 0))
    return pl.pallas_call(
        body, name="out_fwd", grid=(t // tm,),
        in_specs=[row, row, pl.BlockSpec((D_MODEL, D_MODEL), lambda i: (0, 0)),
                  pl.BlockSpec((1, D_MODEL), lambda i: (0, 0))],
        out_specs=[row, row],
        out_shape=[jax.ShapeDtypeStruct((t, D_MODEL), F32), jax.ShapeDtypeStruct((t, D_MODEL), BF16)],
        compiler_params=_params("parallel"),
    )(z, x2, w_out, g2)


def _ffn_up(hn, wg_t, wu_t):
    t = hn.shape[0]
    tm, tn = min(512, t), D_FF // 2

    def body(h_ref, wg_ref, wu_ref, g_ref, u_ref, a_ref):
        g = _dot_nt(h_ref[...], wg_ref[...])
        u = _dot_nt(h_ref[...], wu_ref[...])
        g_ref[...] = g.astype(BF16)
        u_ref[...] = u.astype(BF16)
        a_ref[...] = (g * _sigmoid(g) * u).astype(BF16)

    w_spec = pl.BlockSpec((tn, D_MODEL), lambda j, i: (j, 0))
    out = pl.BlockSpec((tm, tn), lambda j, i: (i, j))
    return pl.pallas_call(
        body, name="ffn_up", grid=(D_FF // tn, t // tm),
        in_specs=[pl.BlockSpec((tm, D_MODEL), lambda j, i: (i, 0)), w_spec, w_spec],
        out_specs=[out, out, out],
        out_shape=[jax.ShapeDtypeStruct((t, D_FF), BF16)] * 3,
        compiler_params=_params("parallel", "parallel"),
    )(hn, wg_t, wu_t)


def _ffn_down_loss(a, h1, tgt, w_down, g3):
    t = a.shape[0]
    tm = min(512, t)

    def body(a_ref, h_ref, t_ref, w_ref, g_ref, dh_ref, dhb_ref, loss_ref, dg_ref):
        @pl.when(pl.program_id(0) == 0)
        def _():
            loss_ref[...] = jnp.zeros_like(loss_ref)
            dg_ref[...] = jnp.zeros_like(dg_ref)

        g = g_ref[...]
        h2 = h_ref[...] + _dot(a_ref[...], w_ref[...])
        r = lax.rsqrt(jnp.mean(h2 * h2, axis=-1, keepdims=True) + EPS)
        err = h2 * r * g - t_ref[...]
        loss_ref[...] += jnp.sum(err * err) * (0.5 / D_MODEL)
        dy = err * (1.0 / D_MODEL)
        dh, dg_rows = _rms_bwd(h2, g, dy)
        dg_ref[...] += jnp.sum(dg_rows, axis=0, keepdims=True)
        dh_ref[...] = dh
        dhb_ref[...] = dh.astype(BF16)

    row = pl.BlockSpec((tm, D_MODEL), lambda i: (i, 0))
    vec = pl.BlockSpec((1, D_MODEL), lambda i: (0, 0))
    return pl.pallas_call(
        body, name="ffn_down_loss", grid=(t // tm,),
        in_specs=[pl.BlockSpec((tm, D_FF), lambda i: (i, 0)), row, row,
                  pl.BlockSpec((D_FF, D_MODEL), lambda i: (0, 0)), vec],
        out_specs=[row, row, pl.BlockSpec((1, LANES), lambda i: (0, 0)), vec],
        out_shape=[jax.ShapeDtypeStruct((t, D_MODEL), F32), jax.ShapeDtypeStruct((t, D_MODEL), BF16),
                   jax.ShapeDtypeStruct((1, LANES), F32), jax.ShapeDtypeStruct((1, D_MODEL), F32)],
        compiler_params=_params("arbitrary"),
    )(a, h1, tgt, w_down, g3)


def _ffn_bwd_act(dh2b, w_down, g_act, u_act, exchange):
    t = dh2b.shape[0]
    tm, tn = min(512, t), D_FF // 2

    def body(d_ref, w_ref, g_ref, u_ref, dg_ref, du_ref):
        da = _dot_nt(d_ref[...], w_ref[...])
        g = g_ref[...].astype(F32)
        u = u_ref[...].astype(F32)
        sg = _sigmoid(g)
        dg_ref[...] = (da * u * sg * (1.0 + g * (1.0 - sg))).astype(BF16)
        du_ref[...] = (da * g * sg).astype(BF16)

    blk = pl.BlockSpec((tm, tn), lambda j, i: (i, j))
    return _call(
        body, name="ffn_bwd_act", grid=(D_FF // tn, t // tm),
        in_specs=[pl.BlockSpec((tm, D_MODEL), lambda j, i: (i, 0)),
                  pl.BlockSpec((tn, D_MODEL), lambda j, i: (j, 0)), blk, blk],
        out_specs=[blk, blk],
        out_shape=[jax.ShapeDtypeStruct((t, D_FF), BF16)] * 2,
        semantics=("parallel", "parallel"), args=(dh2b, w_down, g_act, u_act), exchange=exchange)


def _ffn_bwd_in(dg, du, wg_t, wu_t, h1, dh2, g2, exchange):
    t = dg.shape[0]
    tm, tk = min(512, t), D_FF // 2
    nk = D_FF // tk

    def body(dg_ref, du_ref, wg_ref, wu_ref, h_ref, d2_ref, g_ref, dh_ref, dhb_ref, gn_ref, acc_ref):
        i, kk = pl.program_id(0), pl.program_id(1)

        @pl.when((i == 0) & (kk == 0))
        def _():
            gn_ref[...] = jnp.zeros_like(gn_ref)

        @pl.when(kk == 0)
        def _():
            acc_ref[...] = jnp.zeros_like(acc_ref)

        acc_ref[...] += _dot(dg_ref[...], wg_ref[...]) + _dot(du_ref[...], wu_ref[...])

        @pl.when(kk == nk - 1)
        def _():
            dx, dg_rows = _rms_bwd(h_ref[...], g_ref[...], acc_ref[...])
            dh = d2_ref[...] + dx
            dh_ref[...] = dh
            dhb_ref[...] = dh.astype(BF16)
            gn_ref[...] += jnp.sum(dg_rows, axis=0, keepdims=True)

    act = pl.BlockSpec((tm, tk), lambda i, kk: (i, kk))
    wsp = pl.BlockSpec((tk, D_MODEL), lambda i, kk: (kk, 0))
    row = pl.BlockSpec((tm, D_MODEL), lambda i, kk: (i, 0))
    vec = pl.BlockSpec((1, D_MODEL), lambda i, kk: (0, 0))
    return _call(
        body, name="ffn_bwd_in", grid=(t // tm, nk),
        in_specs=[act, act, wsp, wsp, row, row, vec],
        out_specs=[row, row, vec],
        out_shape=[jax.ShapeDtypeStruct((t, D_MODEL), F32), jax.ShapeDtypeStruct((t, D_MODEL), BF16),
                   jax.ShapeDtypeStruct((1, D_MODEL), F32)],
        scratch=[pltpu.VMEM((tm, D_MODEL), F32)],
        semantics=("arbitrary", "arbitrary"), args=(dg, du, wg_t, wu_t, h1, dh2, g2), exchange=exchange)


def _mix_bwd(dh1b, w_out, proj, b_gate, y_ret, y_att, exchange):
    t = dh1b.shape[0]
    tm, tn = min(256, t), 512
    nb = D_MODEL // tn

    def body(d_ref, w_ref, glr_ref, gla_ref, br_ref, ba_ref, yr_ref, ya_ref,
             dyr_ref, dya_ref, dglr_ref, dgla_ref, dbr_ref, dba_ref):
        @pl.when(pl.program_id(1) == 0)
        def _():
            dbr_ref[...] = jnp.zeros_like(dbr_ref)
            dba_ref[...] = jnp.zeros_like(dba_ref)

        dz = _dot_nt(d_ref[...], w_ref[...])
        gr = _sigmoid(glr_ref[...].astype(F32) + br_ref[...])
        ga = _sigmoid(gla_ref[...].astype(F32) + ba_ref[...])
        dyr_ref[...] = (dz * gr).astype(BF16)
        dya_ref[...] = (dz * ga).astype(BF16)
        dglr = dz * yr_ref[...].astype(F32) * gr * (1.0 - gr)
        dgla = dz * ya_ref[...].astype(F32) * ga * (1.0 - ga)
        dglr_ref[...] = dglr.astype(BF16)
        dgla_ref[...] = dgla.astype(BF16)
        dbr_ref[...] += jnp.sum(dglr, axis=0, keepdims=True)
        dba_ref[...] += jnp.sum(dgla, axis=0, keepdims=True)

    blk = pl.BlockSpec((tm, tn), lambda j, i: (i, j))
    vec = pl.BlockSpec((1, tn), lambda j, i: (0, j))
    return _call(
        body, name="mix_bwd", grid=(nb, t // tm),
        in_specs=[pl.BlockSpec((tm, D_MODEL), lambda j, i: (i, 0)),
                  pl.BlockSpec((tn, D_MODEL), lambda j, i: (j, 0)),
                  pl.BlockSpec((tm, tn), lambda j, i: (i, C_GL // tn + j)),
                  pl.BlockSpec((tm, tn), lambda j, i: (i, C_GL // tn + nb + j)),
                  vec, pl.BlockSpec((1, tn), lambda j, i: (0, nb + j)), blk, blk],
        out_specs=[blk, blk, blk, blk, vec, vec],
        out_shape=[jax.ShapeDtypeStruct((t, D_MODEL), BF16)] * 4 + [jax.ShapeDtypeStruct((1, D_MODEL), F32)] * 2,
        semantics=("arbitrary", "arbitrary"), args=(dh1b, w_out, proj, proj, b_gate, b_gate, y_ret, y_att),
        exchange=exchange)


def _ret_bwd(dgro, proj, o_ret, qr, kr, cs, sn, lg_arr, batch, seq, exchange):
    t = batch * seq
    nt = seq // RET_TILE

    def body(dgro_ref, rg_ref, o_ref, qr_ref, kr_ref, v_ref, cs_ref, sn_ref, lg_ref,
             dq_ref, dk_ref, dv_ref, drg_ref, do_ref, st_ref):
        lg = lg_ref[:, 0:1]
        inside, q_dec, k_dec, tile_dec = _decay(lg)

        state = jnp.zeros((RET_KEY_DIM, RET_VAL_DIM), F32)
        for i in range(nt - 1):
            rows = slice(i * RET_TILE, (i + 1) * RET_TILE)
            state = state * tile_dec + _dot_tn(_scaled(kr_ref[rows, :], k_dec), v_ref[rows, :])
            st_ref[i + 1] = state.astype(BF16)

        for i in range(nt):
            rows = slice(i * RET_TILE, (i + 1) * RET_TILE)
            o = o_ref[rows, :]
            xc = o - jnp.mean(o, axis=-1, keepdims=True)
            rs = lax.rsqrt(jnp.mean(xc * xc, axis=-1, keepdims=True) + EPS)
            nrm = xc * rs
            rg = rg_ref[rows, :].astype(F32)
            sg = _sigmoid(rg)
            dg = dgro_ref[rows, :].astype(F32)
            drg_ref[rows, :] = (dg * nrm * sg * (1.0 + rg * (1.0 - sg))).astype(BF16)
            dn = dg * rg * sg
            do = rs * (dn - jnp.mean(dn, axis=-1, keepdims=True)
                       - nrm * jnp.mean(dn * nrm, axis=-1, keepdims=True))
            do_ref[rows, :] = do.astype(BF16)

        dstate = jnp.zeros((RET_KEY_DIM, RET_VAL_DIM), F32)
        for i in reversed(range(nt)):
            rows = slice(i * RET_TILE, (i + 1) * RET_TILE)
            qi, ki, vi, doi = qr_ref[rows, :], kr_ref[rows, :], v_ref[rows, :], do_ref[rows, :]
            p = (_dot_nt(qi, ki) * inside).astype(BF16)
            dp = (_dot_nt(doi, vi) * inside).astype(BF16)
            dq = _dot(dp, ki)
            dk = _dot_tn(dp, qi)
            dv = _dot_tn(p, doi)
            if i > 0:
                dq = dq + _dot_nt(doi, st_ref[i]) * q_dec
            if i < nt - 1:
                dsb = dstate.astype(BF16)
                dk = dk + _dot_nt(vi, dsb) * k_dec
                dv = dv + _dot(_scaled(ki, k_dec), dsb)
            if i > 0:
                dstate = dstate * tile_dec + _dot_tn(_scaled(qi, q_dec), doi)
            dq_ref[rows, :] = (dq * cs_ref[rows, :] - pltpu.roll(dq, 64, 1) * sn_ref[rows, :]).astype(BF16)
            dk = (dk * cs_ref[rows, :] - pltpu.roll(dk, 64, 1) * sn_ref[rows, :]) * (RET_KEY_DIM ** -0.5)
            dk_ref[rows, :] = dk.astype(BF16)
            dv_ref[rows, :] = dv.astype(BF16)

    key = pl.BlockSpec((seq, RET_KEY_DIM), lambda b, h: (b, h))
    val = pl.BlockSpec((seq, RET_VAL_DIM), lambda b, h: (b, h))
    tab = pl.BlockSpec((seq, RET_KEY_DIM), lambda b, h: (0, 0))
    return _call(
        body, name="ret_bwd", grid=(batch, RET_HEADS),
        in_specs=[val, pl.BlockSpec((seq, RET_VAL_DIM), lambda b, h: (b, C_RG // RET_VAL_DIM + h)), val, key, key,
                  pl.BlockSpec((seq, RET_VAL_DIM), lambda b, h: (b, C_RV // RET_VAL_DIM + h)), tab, tab,
                  pl.BlockSpec((None, 1, LANES), lambda b, h: (h, 0, 0))],
        out_specs=[key, key, val, val],
        out_shape=[jax.ShapeDtypeStruct((t, RET_HEADS * RET_KEY_DIM), BF16)] * 2
                  + [jax.ShapeDtypeStruct((t, RET_HEADS * RET_VAL_DIM), BF16)] * 2,
        scratch=[pltpu.VMEM((seq, RET_VAL_DIM), BF16), pltpu.VMEM((nt, RET_KEY_DIM, RET_VAL_DIM), BF16)],
        semantics=("parallel", "parallel"), args=(dgro, proj, o_ret, qr, kr, proj, cs, sn, lg_arr),
        exchange=exchange)


def _att_bwd(proj, bias, dao, batch, seq, exchange):
    ni, q_spec, k_spec, v_spec, w_spec, b_spec, pad = _att_specs(batch, seq)
    t = batch * seq

    def body(q_ref, k_ref, v_ref, bias_ref, do_ref, dq_ref, dk_ref, dv_ref, dw_ref,
             dbias_ref, dk_acc, dv_acc, kp_ref, vp_ref, s_ref, dp_ref, e_ref, ds_ref):
        b, i = pl.program_id(1), pl.program_id(2)

        @pl.when((b == 0) & (i == 0))
        def _():
            dbias_ref[...] = jnp.zeros_like(dbias_ref)

        @pl.when(i == 0)
        def _():
            _att_pad(k_ref, kp_ref)
            _att_pad(v_ref, vp_ref)
            dk_acc[...] = jnp.zeros_like(dk_acc)
            dv_acc[...] = jnp.zeros_like(dv_acc)

        win = pl.ds(pl.multiple_of(i * ATT_Q, ATT_Q), ATT_WIN)
        k2, v2, q2, do2 = kp_ref[win, :], vp_ref[win, :], q_ref[...], do_ref[...]
        lo = lax.broadcasted_iota(jnp.int32, (1, LANES), 1) < 64
        dq = jnp.zeros((ATT_Q, LANES), F32)
        dk = jnp.zeros((ATT_WIN, LANES), F32)
        dv = jnp.zeros((ATT_WIN, LANES), F32)
        start = jnp.minimum(i, ATT_STARTS)
        for e in range(2):
            sel = lo if e == 0 else jnp.logical_not(lo)
            qm = _att_head(q2, sel)
            dom = jnp.where(sel, do2, jnp.zeros_like(do2))
            s_ref[e] = _dot_nt(qm, k2)
            dp_ref[e] = _dot_nt(dom, v2)
            rsum = []
            for c in range(ATT_Q // ATT_ROWS):
                rows = slice(c * ATT_ROWS, (c + 1) * ATT_ROWS)
                ex, r = _att_softmax_rows(s_ref.at[e], bias_ref.at[start, e], rows)
                dp = dp_ref[e, rows, :]
                mean = jnp.sum(dp * ex, axis=-1, keepdims=True) * r
                ds = ex * ((dp - mean) * r)
                dbias_ref[e, rows, :] += ds
                ds_ref[e, rows, :] = ds.astype(BF16)
                e_ref[e, rows, :] = ex.astype(BF16)
                rsum.append(r)
            dq = dq + _dot(ds_ref[e], jnp.where(sel, k2, jnp.zeros_like(k2)))
            dk = dk + _dot_tn(ds_ref[e], qm)
            dv = dv + _dot_tn(e_ref[e], (dom.astype(F32) * jnp.concatenate(rsum, axis=0)).astype(BF16))
        dq_ref[...] = (dq * 0.125).astype(BF16)
        dk_acc[win, :] += dk
        dv_acc[win, :] += dv

        @pl.when(i == ni - 1)
        def _():
            dk_ref[...] = dk_acc[ATT_PAD:, :].astype(BF16)
            dv_ref[...] = dv_acc[ATT_PAD:, :].astype(BF16)

        @pl.when((b == batch - 1) & (i == ni - 1))
        def _():
            n_i = lax.broadcasted_iota(jnp.int32, (ATT_Q, BIAS_LEN), 0)
            for e in range(2):
                xw = jnp.concatenate([jnp.zeros((ATT_Q, BIAS_LEN - ATT_WIN), F32), dbias_ref[e]], axis=1)
                for bit in range(8):
                    xw = jnp.where(((n_i >> bit) & 1) == 1, pltpu.roll(xw, BIAS_LEN - (1 << bit), 1), xw)
                dw_ref[e:e + 1, :] = jnp.sum(xw, axis=0, keepdims=True)

    seq_blk = pl.BlockSpec((seq, LANES), lambda hp, b, i: (b, hp))
    q_out = pl.BlockSpec((ATT_Q, LANES), lambda hp, b, i: (b * ni + i, hp))
    return _call(
        body, name="att_bwd", grid=(ATT_HEADS // 2, batch, ni),
        in_specs=[q_spec, k_spec, v_spec, b_spec, q_out],
        out_specs=[q_out, seq_blk, seq_blk, w_spec],
        out_shape=[jax.ShapeDtypeStruct((t, 512), BF16)] * 3
                  + [jax.ShapeDtypeStruct((ATT_HEADS // 2, 2, BIAS_LEN), F32)],
        scratch=[pltpu.VMEM((2, ATT_Q, ATT_WIN), F32),
                 pltpu.VMEM((seq + ATT_PAD, LANES), F32), pltpu.VMEM((seq + ATT_PAD, LANES), F32), pad, pad,
                 pltpu.VMEM((2, ATT_Q, ATT_WIN), F32), pltpu.VMEM((2, ATT_Q, ATT_WIN), F32),
                 pltpu.VMEM((2, ATT_Q, ATT_WIN), BF16), pltpu.VMEM((2, ATT_Q, ATT_WIN), BF16)],
        semantics=("arbitrary", "arbitrary", "arbitrary"), args=(proj, proj, proj, bias, dao), exchange=exchange)


def _rms_in_bwd(x2, dxn, dh1, g1):
    t = x2.shape[0]
    tm = min(512, t)

    def body(x_ref, d_ref, h_ref, g_ref, dx_ref, dg_ref):
        @pl.when(pl.program_id(0) == 0)
        def _():
            dg_ref[...] = jnp.zeros_like(dg_ref)

        dx, dg_rows = _rms_bwd(x_ref[...], g_ref[...], d_ref[...])
        dx_ref[...] = h_ref[...] + dx
        dg_ref[...] += jnp.sum(dg_rows, axis=0, keepdims=True)

    row = pl.BlockSpec((tm, D_MODEL), lambda i: (i, 0))
    vec = pl.BlockSpec((1, D_MODEL), lambda i: (0, 0))
    return pl.pallas_call(
        body, name="rms_in_bwd", grid=(t // tm,),
        in_specs=[row, row, row, vec], out_specs=[row, vec],
        out_shape=[jax.ShapeDtypeStruct((t, D_MODEL), F32), jax.ShapeDtypeStruct((1, D_MODEL), F32)],
        compiler_params=_params("arbitrary"),
    )(x2, dxn, dh1, g1)


def _pack_small(dg1, dbr, dba, dg2, dg3, dw, loss):
    def body(a_ref, b_ref, c_ref, d_ref, e_ref, w_ref, l_ref, o_ref):
        o_ref[...] = jnp.zeros_like(o_ref)
        for r, ref in enumerate((a_ref, b_ref, c_ref, d_ref, e_ref)):
            o_ref[r:r + 1, :] = ref[...]
        o_ref[5:6, 0:LANES] = l_ref[...]
        for hp in range(ATT_HEADS // 2):
            o_ref[8 + 2 * hp:10 + 2 * hp, :] = w_ref[hp]

    return pl.pallas_call(body, name="pack_small",
                          out_shape=jax.ShapeDtypeStruct((16, D_MODEL), F32))(dg1, dbr, dba, dg2, dg3, dw, loss)


def _rotary_tables(seq):
    freqs = ROPE_BASE ** (-jnp.arange(0, RET_KEY_DIM, 2, dtype=F32) / RET_KEY_DIM)
    ang = jnp.arange(seq, dtype=F32)[:, None] * freqs[None, :]
    cos, sin = jnp.cos(ang), jnp.sin(ang)
    return jnp.concatenate([cos, cos], axis=1), jnp.concatenate([-sin, sin], axis=1)


def _bias_rows(rel_bias):
    n_far = BIAS_LEN - ATT_Q - MAX_REL + 1
    n_near = BIAS_LEN - n_far - (N_REL - 2)
    w = jnp.concatenate([jnp.broadcast_to(rel_bias[:, N_REL - 1:], (ATT_HEADS, n_far)),
                         rel_bias[:, 1:N_REL - 1][:, ::-1],
                         jnp.broadcast_to(rel_bias[:, :1], (ATT_HEADS, n_near))], axis=1)
    return w.reshape(ATT_HEADS // 2, 2, BIAS_LEN)


def _bias_rows_bwd(dw):
    n_far = BIAS_LEN - ATT_Q - MAX_REL + 1
    mid = dw[:, n_far:n_far + N_REL - 2][:, ::-1]
    return jnp.concatenate([jnp.sum(dw[:, n_far + N_REL - 2:], axis=1, keepdims=True), mid,
                            jnp.sum(dw[:, :n_far], axis=1, keepdims=True)], axis=1)


def _step(x, tgt, norm_mix, b_gate, norm_ffn, norm_final, rel_bias_shard, shard):
    batch, seq, _ = x.shape
    t = batch * seq
    n_rb = rel_bias_shard.shape[-1]
    x2, tgt2 = x.reshape(t, D_MODEL), tgt.reshape(t, D_MODEL)
    g3 = norm_final.reshape(1, D_MODEL)
    cs, sn = _rotary_tables(seq)
    lg = np.log(1.0 - 2.0 ** (-5.0 - np.arange(RET_HEADS, dtype=np.float32))).astype(np.float32)
    lg_arr = jnp.asarray(np.broadcast_to(lg[:, None, None], (RET_HEADS, 1, LANES)))

    def gather(*names):
        return _ChipGather([shard[nm] for nm in names])

    def scatter(*grads):
        return _Exchange(grads, scatter=True)

    rb_pad = jnp.pad(rel_bias_shard, ((0, 0), (0, LANES - n_rb)))
    (xn,), (w_in_t, rb_full) = _rms_fwd(x2, norm_mix, _ChipGather([shard["w_in_t"], rb_pad]))
    rb_full = rb_full.reshape(N_DEV, ATT_HEADS, LANES)[:, :, :n_rb]
    bias = _att_bias_tiles(_bias_rows(jnp.transpose(rb_full, (1, 0, 2)).reshape(ATT_HEADS, N_DEV * n_rb)))
    proj, (w_ret, w_att_t, w_out, w_gate_t) = _mm(
        xn, w_in_t, tb=True, out_dtype=BF16, tm=1024, tn=1664, tk=1024, name="proj",
        exchange=gather("w_ret", "w_att_t", "w_out", "w_gate_t"))
    (gro, o_ret, qr, kr), (w_up_t,) = _ret_fwd(proj, cs, sn, lg_arr, batch, seq, gather("w_up_t"))
    (ao,), (w_down,) = _att_fwd(proj, bias, batch, seq, gather("w_down"))
    z, y_ret, y_att = _mix_fwd(gro, ao, proj, b_gate, w_ret, w_att_t)
    h1, hn = _out_fwd(z, x2, w_out, norm_ffn)
    g_act, u_act, a_act = _ffn_up(hn, w_gate_t, w_up_t)
    dh2, dh2b, loss, dg3 = _ffn_down_loss(a_act, h1, tgt2, w_down, g3)

    wg = dict(out_dtype=BF16, tn=1024, ta=True)
    slots = {}
    dw_down = _mm(a_act, dh2b, tm=1408, tk=1024, name="dw_down", **wg)
    (d_gact, d_uact), (slots["w_down"],) = _ffn_bwd_act(dh2b, w_down, g_act, u_act, scatter(dw_down))
    dw_gate = _mm(d_gact, hn, tm=1408, tk=1024, name="dw_gate", **wg)
    dw_up = _mm(d_uact, hn, tm=1408, tk=1024, name="dw_up", **wg)
    (dh1, dh1b, dg2), (slots["w_gate_t"],) = _ffn_bwd_in(d_gact, d_uact, w_gate_t, w_up_t, h1, dh2, norm_ffn,
                                                       scatter(dw_gate))
    dw_out = _mm(z, dh1b, tm=1024, tk=2048, name="dw_out", **wg)
    (dyr, dya, dglr, dgla, dbr, dba), _ = _mix_bwd(dh1b, w_out, proj, b_gate, y_ret, y_att, None)
    dgro = _mm(dyr, w_ret, tb=True, out_dtype=BF16, tm=1024, tn=1024, tk=1024, name="dgro")
    dao = _mm(dya, w_att_t, out_dtype=BF16, tm=1024, tn=512, tk=1024, name="dao")
    dw_ret = _mm(gro, dyr, tm=1024, tk=2048, name="dw_ret", **wg)
    dw_att = _mm(dya, ao, tm=1024, tk=2048, name="dw_att", **wg)
    (drq, drk, drv, drg), (slots["w_up_t"],) = _ret_bwd(
        dgro, proj, o_ret, qr, kr, cs, sn, lg_arr, batch, seq, scatter(dw_up))
    (daq, dak, dav, dw), (slots["w_out"], slots["w_ret"], slots["w_att_t"]) = _att_bwd(
        proj, bias, dao, batch, seq, scatter(dw_out, dw_ret, dw_att))
    dproj = [drq, drk, drv, drg, daq, dak, dav, dglr, dgla]
    dw_in = _mm_pieces(dproj, xn, ta=True, out_dtype=BF16, tm=512, tn=1024, tk=1024, name="dw_in")
    (dw_in_sibling,) = _alone(_PairSwap([dw_in]), "swap_w_in")
    dw_in_pairs = _pair_add(dw_in, dw_in_sibling, "pair_w_in")
    dxn, (slots["w_in_t"],) = _mm_pieces(dproj, w_in_t, ta=False, out_dtype=F32, tm=1024, tn=1024, tk=512, name="dxn",
                                  exchange=_ChipScatter([dw_in_pairs]))
    dx, dg1 = _rms_in_bwd(x2, dxn, dh1, norm_mix)
    small = _pack_small(dg1, dbr, dba, dg2, dg3, dw, loss)
    (small_slots,) = _alone(_ChipGather([small]), "gather_small")
    return dx.reshape(batch, seq, D_MODEL), slots, small_slots.reshape(N_DEV, 16, D_MODEL)


def _row_tile(r, c):
    return max(d for d in range(16, r + 1, 16) if r % d == 0 and (d * c <= 256 * 1024 or d == 16))


def _pair_add(grad, got, name):
    _, r, c = got.shape
    tr = _row_tile(r, c)
    core = lax.axis_index("c").astype(jnp.int32).reshape(1)

    def body(core_ref, g_ref, a_ref, o_ref):
        o_ref[...] = (g_ref[...].astype(F32) + a_ref[...].astype(F32)).astype(o_ref.dtype)

    blk = pl.BlockSpec((None, tr, c), lambda q, i, core_ref: (q, i, 0))
    return pl.pallas_call(
        body, name=name,
        grid_spec=pltpu.PrefetchScalarGridSpec(
            num_scalar_prefetch=1, grid=(4, r // tr),
            in_specs=[pl.BlockSpec((None, None, tr, c), lambda q, i, core_ref: (q, core_ref[0], i, 0)), blk],
            out_specs=blk),
        out_shape=jax.ShapeDtypeStruct(got.shape, got.dtype),
        compiler_params=_params("parallel", "parallel"),
    )(core, grad.reshape(4, 2, r, c), got)


def _sum_slots(slots, name):
    n, r, c = slots.shape
    tr = _row_tile(r, c)

    def body(s_ref, o_ref):
        acc = s_ref[0].astype(F32)
        for s in range(1, n):
            acc = acc + s_ref[s].astype(F32)
        o_ref[...] = acc

    return pl.pallas_call(
        body, name=name, grid=(r // tr,),
        in_specs=[pl.BlockSpec((n, tr, c), lambda i: (0, i, 0))],
        out_specs=pl.BlockSpec((tr, c), lambda i: (i, 0)),
        out_shape=jax.ShapeDtypeStruct((r, c), F32),
        compiler_params=_params("parallel"),
    )(slots)


def _adamw_math(w, g, m, v):
    m = ADAM_B1 * m + (1.0 - ADAM_B1) * g
    v = ADAM_B2 * v + (1.0 - ADAM_B2) * (g * g)
    m_hat = m / (1.0 - ADAM_B1 ** ADAM_STEP)
    v_hat = v / (1.0 - ADAM_B2 ** ADAM_STEP)
    return -ADAM_LR * (m_hat / (jnp.sqrt(v_hat) + ADAM_EPS) + ADAM_WD * w), m, v


def _adamw(w, g, m, v, name):
    r, c = w.shape
    tr = r
    while tr * c > 128 * 1024 and tr % 16 == 0:
        tr //= 2

    def body(w_ref, g_ref, m_ref, v_ref, d_ref, nm_ref, nv_ref):
        d_ref[...], nm_ref[...], nv_ref[...] = _adamw_math(w_ref[...], g_ref[...], m_ref[...], v_ref[...])

    blk = pl.BlockSpec((tr, c), lambda i: (i, 0))
    return pl.pallas_call(
        body, name=name, grid=(r // tr,),
        in_specs=[blk] * 4, out_specs=[blk] * 3,
        out_shape=[jax.ShapeDtypeStruct((r, c), F32)] * 3,
        compiler_params=_params("parallel"),
    )(w, g, m, v)


def _adamw_small(ws, gs, ms, vs):
    n = len(ws)

    def body(*refs):
        for i in range(n):
            w_ref, g_ref, m_ref, v_ref = (refs[j * n + i] for j in range(4))
            d_ref, nm_ref, nv_ref = (refs[(4 + j) * n + i] for j in range(3))
            d_ref[...], nm_ref[...], nv_ref[...] = _adamw_math(w_ref[...], g_ref[...], m_ref[...], v_ref[...])

    shapes = [jax.ShapeDtypeStruct(w.shape, F32) for w in ws]
    outs = pl.pallas_call(body, name="adamw_small", out_shape=shapes * 3)(*ws, *gs, *ms, *vs)
    return outs[:n], outs[n:2 * n], outs[2 * n:]


def kernel(x, norm_mix, w_in, b_gate, rel_bias, w_ret_out, w_att_out, w_out, norm_ffn, w_ffn_gate, w_ffn_up, w_ffn_down, norm_final, loss_target, m_norm_mix, m_w_in, m_b_gate, m_rel_bias, m_w_ret_out, m_w_att_out, m_w_out, m_norm_ffn, m_w_ffn_gate, m_w_ffn_up, m_w_ffn_down, m_norm_final, v_norm_mix, v_w_in, v_b_gate, v_rel_bias, v_w_ret_out, v_w_att_out, v_w_out, v_norm_ffn, v_w_ffn_gate, v_w_ffn_up, v_w_ffn_down, v_norm_final):
    me = _index(_place())
    n_rb = rel_bias.shape[-1]

    shard = dict(w_in_t=w_in[0].T, w_gate_t=w_ffn_gate[0].T, w_up_t=w_ffn_up[0].T, w_down=w_ffn_down[0],
                 w_ret=w_ret_out[0], w_out=w_out[0], w_att_t=w_att_out[0].T)
    shard = {nm: s.astype(BF16) for nm, s in shard.items()}
    dx, slots, small_slots = _step(x, loss_target, norm_mix, b_gate, norm_ffn, norm_final, rel_bias[0], shard)
    summed = {nm: _sum_slots(s, "sum_" + nm) for nm, s in slots.items()}
    small_sum = _sum_slots(small_slots, "sum_small")
    loss = small_sum[5, 0]

    transposed = dict(w_in="w_in_t", w_ffn_gate="w_gate_t", w_ffn_up="w_up_t", w_att_out="w_att_t")
    g = dict(
        w_ffn_down=summed["w_down"], w_ret_out=summed["w_ret"], w_out=summed["w_out"],
        norm_mix=small_sum[0:1], b_gate=jnp.concatenate([small_sum[1:2], small_sum[2:3]], axis=1),
        norm_ffn=small_sum[3:4], norm_final=small_sum[4:5],
        rel_bias=lax.dynamic_slice_in_dim(_bias_rows_bwd(small_sum[8:16]), me * n_rb, n_rb, axis=1),
    )
    w = dict(norm_mix=norm_mix, w_in=w_in, b_gate=b_gate, rel_bias=rel_bias, w_ret_out=w_ret_out, w_att_out=w_att_out,
             w_out=w_out, norm_ffn=norm_ffn, w_ffn_gate=w_ffn_gate, w_ffn_up=w_ffn_up, w_ffn_down=w_ffn_down,
             norm_final=norm_final)
    m = dict(norm_mix=m_norm_mix, w_in=m_w_in, b_gate=m_b_gate, rel_bias=m_rel_bias, w_ret_out=m_w_ret_out,
             w_att_out=m_w_att_out, w_out=m_w_out, norm_ffn=m_norm_ffn, w_ffn_gate=m_w_ffn_gate, w_ffn_up=m_w_ffn_up,
             w_ffn_down=m_w_ffn_down, norm_final=m_norm_final)
    v = dict(norm_mix=v_norm_mix, w_in=v_w_in, b_gate=v_b_gate, rel_bias=v_rel_bias, w_ret_out=v_w_ret_out,
             w_att_out=v_w_att_out, w_out=v_w_out, norm_ffn=v_norm_ffn, w_ffn_gate=v_w_ffn_gate, w_ffn_up=v_w_ffn_up,
             w_ffn_down=v_w_ffn_down, norm_final=v_norm_final)
    order = ("norm_mix", "w_in", "b_gate", "rel_bias", "w_ret_out", "w_att_out", "w_out", "norm_ffn",
             "w_ffn_gate", "w_ffn_up", "w_ffn_down", "norm_final")
    small_names = ("norm_mix", "b_gate", "rel_bias", "norm_ffn", "norm_final")

    def flat(a):
        return a[0] if a.ndim == 3 else a.reshape(-1, a.shape[-1])

    grad, delta, new_m, new_v = {}, {}, {}, {}
    for nm in order:
        if nm in transposed:
            g_t = summed[transposed[nm]]
            d, nmom, nvar = _adamw(w[nm][0].T, g_t, m[nm][0].T, v[nm][0].T, "adamw_" + nm)
            grad[nm], delta[nm], new_m[nm], new_v[nm] = (a.T[None] for a in (g_t, d, nmom, nvar))
        elif nm not in small_names:
            d, nmom, nvar = _adamw(flat(w[nm]), g[nm], flat(m[nm]), flat(v[nm]), "adamw_" + nm)
            grad[nm], delta[nm], new_m[nm], new_v[nm] = (a.reshape(w[nm].shape) for a in (g[nm], d, nmom, nvar))
    ds, nms, nvs = _adamw_small([flat(w[nm]) for nm in small_names], [g[nm] for nm in small_names],
                                [flat(m[nm]) for nm in small_names], [flat(v[nm]) for nm in small_names])
    for i, nm in enumerate(small_names):
        grad[nm], delta[nm], new_m[nm], new_v[nm] = (a.reshape(w[nm].shape) for a in (g[nm], ds[i], nms[i], nvs[i]))

    return (loss, dx, *[grad[nm] for nm in order], *[delta[nm] for nm in order],
            *[new_m[nm] for nm in order], *[new_v[nm] for nm in order])
```

```python
import numpy as np
import jax
import jax.numpy as jnp
from jax import lax
from jax.experimental import pallas as pl
from jax.experimental.pallas import tpu as pltpu

F32 = jnp.float32
BF16 = jnp.bfloat16
MESH = pl.DeviceIdType.MESH

D_MODEL = 1024
CHUNK = 64
RET_HEADS = 4
RET_KEY_DIM = 128
RET_VAL_DIM = 256
ATT_HEADS = 8
BAND_CHUNKS = 8
MAX_REL = 256
N_REL = CHUNK + MAX_REL
D_FF = 2816
N_IN = 6656
ROPE_BASE = 10000.0
EPS = 1e-6
NEG_INF = -1e30
C_RQ, C_RK, C_RV, C_RG, C_AQ, C_AK, C_AV, C_GL = 0, 512, 1024, 2048, 3072, 3584, 4096, 4608

ADAM_LR = 0.001
ADAM_B1 = 0.9
ADAM_B2 = 0.999
ADAM_EPS = 1e-08
ADAM_WD = 0.01
ADAM_STEP = 10

N_DEV = 8
LANES = 128
RET_TILE = 256
ATT_Q = 256
ATT_PAD = BAND_CHUNKS * CHUNK
ATT_WIN = ATT_PAD + ATT_Q
ATT_STARTS = ATT_PAD // ATT_Q
ATT_ROWS = 32
BIAS_LEN = 1024
VMEM_LIMIT = 48 * 1024 * 1024


def _params(*sem):
    return pltpu.CompilerParams(dimension_semantics=sem, vmem_limit_bytes=VMEM_LIMIT)


def _dot(a, b):
    return lax.dot_general(a, b, (((1,), (0,)), ((), ())), preferred_element_type=F32)


def _dot_nt(a, b):
    return lax.dot_general(a, b, (((1,), (1,)), ((), ())), preferred_element_type=F32)


def _dot_tn(a, b):
    return lax.dot_general(a, b, (((0,), (0,)), ((), ())), preferred_element_type=F32)


def _sigmoid(x):
    return 1.0 / (1.0 + jnp.exp(-x))


def _rms_bwd(x, g, dy):
    r = lax.rsqrt(jnp.mean(x * x, axis=-1, keepdims=True) + EPS)
    u = dy * g
    dx = r * u - x * (r * r * r) * jnp.mean(u * x, axis=-1, keepdims=True)
    return dx, dy * x * r


def _place():
    return lax.axis_index("x"), lax.axis_index("y"), lax.axis_index("c")


def _peer(k):
    x, y, c = _place()
    return ((1 - x) if k & 4 else x, (1 - y) if k & 2 else y, (1 - c) if k & 1 else c)


def _index(place):
    return 4 * place[0] + 2 * place[1] + place[2]


def _rows(ref, block, nrows):
    align = 16 if ref.dtype == BF16 else 8
    return ref.at[pl.ds(pl.multiple_of(block * nrows, align), nrows)]


class _Exchange:
    def __init__(self, arrays, scatter):
        self.arrays, self.scatter, self.n = list(arrays), scatter, len(arrays)

    def out_shape(self):
        if self.scatter:
            return [jax.ShapeDtypeStruct((N_DEV, a.shape[0] // N_DEV) + a.shape[1:], a.dtype) for a in self.arrays]
        return [jax.ShapeDtypeStruct((N_DEV * a.shape[0],) + a.shape[1:], a.dtype) for a in self.arrays]

    def scratch(self):
        return [pltpu.SemaphoreType.DMA((self.n, N_DEV - 1)), pltpu.SemaphoreType.DMA((self.n, N_DEV - 1)),
                pltpu.SemaphoreType.DMA((self.n,))]

    def _copies(self, ins, outs, sems):
        send_sems, recv_sems, local_sems = sems
        me = _index(_place())

        def src(w, to):
            return _rows(ins[w], to, ins[w].shape[0] // N_DEV) if self.scatter else ins[w]

        def dst(w, origin):
            return outs[w].at[origin] if self.scatter else _rows(outs[w], origin, ins[w].shape[0])

        def remote(w, k, to, origin):
            return pltpu.make_async_remote_copy(src_ref=src(w, to), dst_ref=dst(w, origin),
                                                send_sem=send_sems.at[w, k - 1], recv_sem=recv_sems.at[w, k - 1],
                                                device_id=_peer(k), device_id_type=MESH)

        pairs = [(w, k) for w in range(self.n) for k in range(1, N_DEV)]
        own = lambda: [pltpu.make_async_copy(src(w, me), dst(w, me), local_sems.at[w]) for w in range(self.n)]
        sent = lambda: [remote(w, k, _index(_peer(k)), me) for w, k in pairs]
        arriving = lambda: [remote(w, k, me, _index(_peer(k))) for w, k in pairs]
        return own, sent, arriving

    def start(self, ins, outs, sems):
        own, sent, _ = self._copies(ins, outs, sems)
        for cp in own() + sent():
            cp.start()

    def wait(self, ins, outs, sems):
        own, sent, arriving = self._copies(ins, outs, sems)
        for cp in arriving():
            cp.wait_recv()
        for cp in sent():
            cp.wait_send()
        for cp in own():
            cp.wait()


class _PairSwap:
    def __init__(self, arrays):
        self.arrays, self.n = list(arrays), len(arrays)

    def out_shape(self):
        return [jax.ShapeDtypeStruct((4, a.shape[0] // N_DEV) + a.shape[1:], a.dtype) for a in self.arrays]

    def scratch(self):
        return [pltpu.SemaphoreType.DMA((self.n, 4)), pltpu.SemaphoreType.DMA((self.n, 4))]

    def _copies(self, ins, outs, sems):
        send_sems, recv_sems = sems
        x, y, c = _place()
        return [pltpu.make_async_remote_copy(
            src_ref=_rows(ins[w], 2 * q + 1 - c, ins[w].shape[0] // N_DEV), dst_ref=outs[w].at[q],
            send_sem=send_sems.at[w, q], recv_sem=recv_sems.at[w, q],
            device_id=(x, y, 1 - c), device_id_type=MESH) for w in range(self.n) for q in range(4)]

    def start(self, ins, outs, sems):
        for cp in self._copies(ins, outs, sems):
            cp.start()

    def wait(self, ins, outs, sems):
        for cp in self._copies(ins, outs, sems):
            cp.wait()


class _ChipScatter:
    def __init__(self, arrays):
        self.arrays, self.n = list(arrays), len(arrays)

    def out_shape(self):
        return [jax.ShapeDtypeStruct(a.shape, a.dtype) for a in self.arrays]

    def scratch(self):
        return [pltpu.SemaphoreType.DMA((self.n, 3)), pltpu.SemaphoreType.DMA((self.n, 3)),
                pltpu.SemaphoreType.DMA((self.n,))]

    def _copies(self, ins, outs, sems):
        send_sems, recv_sems, local_sems = sems
        x, y, c = _place()
        mine = 2 * x + y
        sent, arriving = [], []
        for w in range(self.n):
            for k in range(1, 4):
                tx, ty = (1 - x) if k & 2 else x, (1 - y) if k & 1 else y
                other = 2 * tx + ty
                sent.append(lambda w=w, k=k, tx=tx, ty=ty, other=other: pltpu.make_async_remote_copy(
                    src_ref=ins[w].at[other], dst_ref=outs[w].at[mine], send_sem=send_sems.at[w, k - 1],
                    recv_sem=recv_sems.at[w, k - 1], device_id=(tx, ty, c), device_id_type=MESH))
                arriving.append(lambda w=w, k=k, tx=tx, ty=ty, other=other: pltpu.make_async_remote_copy(
                    src_ref=ins[w].at[mine], dst_ref=outs[w].at[other], send_sem=send_sems.at[w, k - 1],
                    recv_sem=recv_sems.at[w, k - 1], device_id=(tx, ty, c), device_id_type=MESH))
        own = [lambda w=w: pltpu.make_async_copy(ins[w].at[mine], outs[w].at[mine], local_sems.at[w])
               for w in range(self.n)]
        return own, sent, arriving

    def start(self, ins, outs, sems):
        own, sent, _ = self._copies(ins, outs, sems)
        for cp in own + sent:
            cp().start()

    def wait(self, ins, outs, sems):
        own, sent, arriving = self._copies(ins, outs, sems)
        for cp in arriving:
            cp().wait_recv()
        for cp in sent:
            cp().wait_send()
        for cp in own:
            cp().wait()


class _ChipGather:
    def __init__(self, arrays):
        self.arrays, self.n = list(arrays), len(arrays)

    def out_shape(self):
        return [jax.ShapeDtypeStruct((N_DEV * a.shape[0],) + a.shape[1:], a.dtype) for a in self.arrays]

    def scratch(self):
        return [pltpu.SemaphoreType.DMA((self.n, N_DEV - 1)), pltpu.SemaphoreType.DMA((self.n, N_DEV - 1)),
                pltpu.SemaphoreType.DMA((self.n,))]

    def _parts(self, ins, outs, sems):
        send_sems, recv_sems, local_sems = sems
        x, y, c = _place()
        me, sibling = (x, y, c), (x, y, 1 - c)
        chips = [(1 - x, y), (x, 1 - y), (1 - x, 1 - y)]

        def rows(w, place):
            return _rows(outs[w], _index(place), ins[w].shape[0])

        def copy(w, k, block, to, own=False):
            return pltpu.make_async_remote_copy(src_ref=ins[w] if own else rows(w, block), dst_ref=rows(w, block),
                                                send_sem=send_sems.at[w, k], recv_sem=recv_sems.at[w, k],
                                                device_id=to, device_id_type=MESH)

        def local(w):
            return pltpu.make_async_copy(ins[w], rows(w, me), local_sems.at[w])

        return me, sibling, chips, c, copy, local

    def start(self, ins, outs, sems):
        me, sibling, chips, c, copy, local = self._parts(ins, outs, sems)
        for w in range(self.n):
            local(w).start()
            copy(w, 0, me, sibling, own=True).start()
            for j, chip in enumerate(chips):
                copy(w, 1 + j, me, (*chip, c), own=True).start()

    def wait(self, ins, outs, sems):
        me, sibling, chips, c, copy, local = self._parts(ins, outs, sems)
        for w in range(self.n):
            for j, chip in enumerate(chips):
                copy(w, 1 + j, (*chip, c), me).wait_recv()
                copy(w, 4 + j, (*chip, c), sibling).start()
        for w in range(self.n):
            copy(w, 0, sibling, me).wait_recv()
            for j, chip in enumerate(chips):
                copy(w, 4 + j, (*chip, 1 - c), me).wait_recv()
            copy(w, 0, me, sibling, own=True).wait_send()
            for j, chip in enumerate(chips):
                copy(w, 1 + j, me, (*chip, c), own=True).wait_send()
                copy(w, 4 + j, (*chip, c), sibling).wait_send()
            local(w).wait()


def _call(body, *, name, grid, in_specs, out_specs, out_shape, scratch=(), semantics, args, exchange=None):
    if exchange is None:
        return pl.pallas_call(body, name=name, grid=grid, in_specs=in_specs, out_specs=out_specs, out_shape=out_shape,
                              scratch_shapes=list(scratch), compiler_params=_params(*semantics))(*args), None
    n_in, n_out, n_scr, nx = len(in_specs), len(out_specs), len(scratch), exchange.n

    def full_body(*refs):
        ins, refs = refs[:n_in], refs[n_in:]
        x_in, refs = refs[:nx], refs[nx:]
        outs, refs = refs[:n_out], refs[n_out:]
        x_out, refs = refs[:nx], refs[nx:]
        scr, sems = refs[:n_scr], refs[n_scr:]
        first, last = True, True
        for axis, size in enumerate(grid):
            first = jnp.logical_and(first, pl.program_id(axis) == 0)
            last = jnp.logical_and(last, pl.program_id(axis) == size - 1)
        if grid:
            pl.when(first)(lambda: exchange.start(x_in, x_out, sems))
        else:
            exchange.start(x_in, x_out, sems)
        body(*ins, *outs, *scr)
        if grid:
            pl.when(last)(lambda: exchange.wait(x_in, x_out, sems))
        else:
            exchange.wait(x_in, x_out, sems)

    hbm = pl.BlockSpec(memory_space=pltpu.HBM)
    res = pl.pallas_call(
        full_body, name=name, grid=grid,
        in_specs=list(in_specs) + [hbm] * nx, out_specs=list(out_specs) + [hbm] * nx,
        out_shape=list(out_shape) + exchange.out_shape(),
        scratch_shapes=list(scratch) + exchange.scratch(),
        compiler_params=_params(*(["arbitrary"] * len(grid))),
    )(*args, *exchange.arrays)
    return res[:n_out], res[n_out:]


def _alone(exchange, name):
    return _call(lambda: None, name=name, grid=(), in_specs=[], out_specs=[], out_shape=[], semantics=(),
                 args=(), exchange=exchange)[1]


def _mm(a, b, *, ta=False, tb=False, out_dtype, tm, tn, tk, name, exchange=None):
    m, k = (a.shape[1], a.shape[0]) if ta else a.shape
    n = b.shape[0] if tb else b.shape[1]
    assert k == (b.shape[1] if tb else b.shape[0])
    tm, tn, tk = min(tm, m), min(tn, n), min(tk, k)
    assert m % tm == 0 and n % tn == 0 and k % tk == 0, (name, m, n, k)
    nk = k // tk
    dims = (((0 if ta else 1,), (1 if tb else 0,)), ((), ()))

    def body(a_ref, b_ref, o_ref, *acc):
        prod = lax.dot_general(a_ref[...].astype(BF16), b_ref[...].astype(BF16), dims, preferred_element_type=F32)
        if nk == 1:
            o_ref[...] = prod.astype(o_ref.dtype)
            return
        acc_ref, kk = acc[0], pl.program_id(2)

        @pl.when(kk == 0)
        def _():
            acc_ref[...] = prod

        @pl.when((kk > 0) & (kk < nk - 1))
        def _():
            acc_ref[...] += prod

        @pl.when(kk == nk - 1)
        def _():
            o_ref[...] = (acc_ref[...] + prod).astype(o_ref.dtype)

    a_spec = (pl.BlockSpec((tk, tm), lambda i, j, kk: (kk, i)) if ta
              else pl.BlockSpec((tm, tk), lambda i, j, kk: (i, kk)))
    b_spec = (pl.BlockSpec((tn, tk), lambda i, j, kk: (j, kk)) if tb
              else pl.BlockSpec((tk, tn), lambda i, j, kk: (kk, j)))
    (out,), moved = _call(
        body, name=name, grid=(m // tm, n // tn, nk),
        in_specs=[a_spec, b_spec],
        out_specs=[pl.BlockSpec((tm, tn), lambda i, j, kk: (i, j))],
        out_shape=[jax.ShapeDtypeStruct((m, n), out_dtype)],
        scratch=[pltpu.VMEM((tm, tn), F32)] if nk > 1 else [],
        semantics=("parallel", "parallel", "arbitrary"), args=(a, b), exchange=exchange)
    return out if exchange is None else (out, moved)


def _mm_pieces(pieces, b, *, ta, out_dtype, tm, tn, tk, name, exchange=None):
    rows, n = pieces[0].shape[0], b.shape[1]
    step = tm if ta else tk
    assert all(p.shape[0] == rows and p.shape[1] % step == 0 for p in pieces), name
    edges = [int(e) for e in np.cumsum([0] + [p.shape[1] // step for p in pieces])]
    total = edges[-1] * step
    m, k = (total, rows) if ta else (rows, total)
    assert b.shape[0] == k and m % tm == 0 and n % tn == 0 and k % tk == 0, name
    nk, npieces = k // tk, len(pieces)
    dims = (((0 if ta else 1,), (0,)), ((), ()))

    def body(*refs):
        a_refs, (b_ref, o_ref, acc_ref) = refs[:npieces], refs[npieces:]
        kk = pl.program_id(2)
        pos = pl.program_id(0) if ta else kk

        @pl.when(kk == 0)
        def _():
            acc_ref[...] = jnp.zeros_like(acc_ref)

        for p, a_ref in enumerate(a_refs):
            @pl.when((pos >= edges[p]) & (pos < edges[p + 1]))
            def _(a_ref=a_ref):
                acc_ref[...] += lax.dot_general(a_ref[...], b_ref[...], dims, preferred_element_type=F32)

        @pl.when(kk == nk - 1)
        def _():
            o_ref[...] = acc_ref[...].astype(o_ref.dtype)

    def a_spec(p):
        lo, last = edges[p], edges[p + 1] - edges[p] - 1
        if ta:
            def index(i, j, kk):
                inside = (i >= lo) & (i <= lo + last)
                return jnp.where(inside, kk, 0), jnp.clip(i - lo, 0, last)
            return pl.BlockSpec((tk, tm), index)
        return pl.BlockSpec((tm, tk), lambda i, j, kk: (i, jnp.clip(kk - lo, 0, last)))

    (out,), moved = _call(
        body, name=name, grid=(m // tm, n // tn, nk),
        in_specs=[a_spec(p) for p in range(npieces)] + [pl.BlockSpec((tk, tn), lambda i, j, kk: (kk, j))],
        out_specs=[pl.BlockSpec((tm, tn), lambda i, j, kk: (i, j))],
        out_shape=[jax.ShapeDtypeStruct((m, n), out_dtype)],
        scratch=[pltpu.VMEM((tm, tn), F32)],
        semantics=("parallel", "parallel", "arbitrary"), args=(*pieces, b), exchange=exchange)
    return out if exchange is None else (out, moved)


def _rms_fwd(x2, g, exchange):
    t = x2.shape[0]
    tm = min(512, t)

    def body(x_ref, g_ref, o_ref):
        x = x_ref[...]
        r = lax.rsqrt(jnp.mean(x * x, axis=-1, keepdims=True) + EPS)
        o_ref[...] = (x * r * g_ref[...]).astype(o_ref.dtype)

    return _call(
        body, name="rms_in_fwd", grid=(t // tm,),
        in_specs=[pl.BlockSpec((tm, D_MODEL), lambda i: (i, 0)), pl.BlockSpec((1, D_MODEL), lambda i: (0, 0))],
        out_specs=[pl.BlockSpec((tm, D_MODEL), lambda i: (i, 0))],
        out_shape=[jax.ShapeDtypeStruct((t, D_MODEL), BF16)],
        semantics=("parallel",), args=(x2, g), exchange=exchange)


def _decay(lg):
    row = lax.broadcasted_iota(jnp.int32, (RET_TILE, RET_TILE), 0)
    col = lax.broadcasted_iota(jnp.int32, (RET_TILE, RET_TILE), 1)
    within = jnp.exp(lg * jnp.abs(row - col).astype(F32))
    inside = jnp.where((col >> 6) <= (row >> 6), within, 0.0)
    pos = lax.broadcasted_iota(jnp.int32, (RET_TILE, 1), 0).astype(F32)
    q_dec = jnp.exp(lg * (pos + 1.0))
    k_dec = jnp.exp(lg * (RET_TILE - 1.0 - pos))
    tile_dec = jnp.exp(lg * float(RET_TILE))
    return inside, q_dec, k_dec, tile_dec


def _scaled(a_bf16, dec):
    return (a_bf16.astype(F32) * dec).astype(BF16)


def _ret_fwd(proj, cs, sn, lg_arr, batch, seq, exchange):
    t = batch * seq
    nt = seq // RET_TILE

    def body(q_ref, k_ref, v_ref, rg_ref, cs_ref, sn_ref, lg_ref, gro_ref, o_ref, qr_ref, kr_ref):
        lg = lg_ref[:, 0:1]
        cs_t, sn_t = cs_ref[...], sn_ref[...]
        q = q_ref[...].astype(F32)
        k = k_ref[...].astype(F32)
        qr_ref[...] = (q * cs_t + pltpu.roll(q, 64, 1) * sn_t).astype(BF16)
        kr_ref[...] = ((k * cs_t + pltpu.roll(k, 64, 1) * sn_t) * (RET_KEY_DIM ** -0.5)).astype(BF16)
        inside, q_dec, k_dec, tile_dec = _decay(lg)
        state = jnp.zeros((RET_KEY_DIM, RET_VAL_DIM), F32)
        for i in range(nt):
            rows = slice(i * RET_TILE, (i + 1) * RET_TILE)
            qi, ki, vi = qr_ref[rows, :], kr_ref[rows, :], v_ref[rows, :]
            acc = _dot((_dot_nt(qi, ki) * inside).astype(BF16), vi)
            if i > 0:
                acc = acc + _dot(_scaled(qi, q_dec), state.astype(BF16))
            if i < nt - 1:
                state = state * tile_dec + _dot_tn(_scaled(ki, k_dec), vi)
            o_ref[rows, :] = acc
            xc = acc - jnp.mean(acc, axis=-1, keepdims=True)
            nrm = xc * lax.rsqrt(jnp.mean(xc * xc, axis=-1, keepdims=True) + EPS)
            rg = rg_ref[rows, :].astype(F32)
            gro_ref[rows, :] = (rg * _sigmoid(rg) * nrm).astype(BF16)

    def col(base, width):
        return lambda b, h: (b, base // width + h)

    return _call(
        body, name="ret_fwd", grid=(batch, RET_HEADS),
        in_specs=[pl.BlockSpec((seq, RET_KEY_DIM), col(C_RQ, RET_KEY_DIM)),
                  pl.BlockSpec((seq, RET_KEY_DIM), col(C_RK, RET_KEY_DIM)),
                  pl.BlockSpec((seq, RET_VAL_DIM), col(C_RV, RET_VAL_DIM)),
                  pl.BlockSpec((seq, RET_VAL_DIM), col(C_RG, RET_VAL_DIM)),
                  pl.BlockSpec((seq, RET_KEY_DIM), lambda b, h: (0, 0)),
                  pl.BlockSpec((seq, RET_KEY_DIM), lambda b, h: (0, 0)),
                  pl.BlockSpec((None, 1, LANES), lambda b, h: (h, 0, 0))],
        out_specs=[pl.BlockSpec((seq, RET_VAL_DIM), lambda b, h: (b, h)),
                   pl.BlockSpec((seq, RET_VAL_DIM), lambda b, h: (b, h)),
                   pl.BlockSpec((seq, RET_KEY_DIM), lambda b, h: (b, h)),
                   pl.BlockSpec((seq, RET_KEY_DIM), lambda b, h: (b, h))],
        out_shape=[jax.ShapeDtypeStruct((t, RET_HEADS * RET_VAL_DIM), BF16),
                   jax.ShapeDtypeStruct((t, RET_HEADS * RET_VAL_DIM), F32),
                   jax.ShapeDtypeStruct((t, RET_HEADS * RET_KEY_DIM), BF16),
                   jax.ShapeDtypeStruct((t, RET_HEADS * RET_KEY_DIM), BF16)],
        semantics=("parallel", "parallel"), args=(proj, proj, proj, proj, cs, sn, lg_arr), exchange=exchange)


def _att_bias(w_ref, bias_ref):
    n_i = lax.broadcasted_iota(jnp.int32, (ATT_Q, BIAS_LEN), 0)
    qc = lax.broadcasted_iota(jnp.int32, (ATT_Q, ATT_WIN), 0) >> 6
    kc = lax.broadcasted_iota(jnp.int32, (ATT_Q, ATT_WIN), 1) >> 6
    dc = qc + BAND_CHUNKS - kc
    band = (dc >= 0) & (dc <= BAND_CHUNKS)
    key = lax.broadcasted_iota(jnp.int32, (ATT_Q, ATT_WIN), 1)
    for e in range(2):
        xw = jnp.broadcast_to(w_ref[e:e + 1, :], (ATT_Q, BIAS_LEN))
        for bit in range(8):
            xw = jnp.where(((n_i >> bit) & 1) == 1, pltpu.roll(xw, 1 << bit, 1), xw)
        bias = jnp.where(band, xw[:, BIAS_LEN - ATT_WIN:], NEG_INF)
        for first in range(ATT_STARTS):
            bias_ref[first, e] = jnp.where(key + (first * ATT_Q - ATT_PAD) >= 0, bias, NEG_INF)
        bias_ref[ATT_STARTS, e] = bias


def _att_specs(batch, seq):
    ni = seq // ATT_Q
    q_spec = pl.BlockSpec((ATT_Q, LANES), lambda hp, b, i: (b * ni + i, C_AQ // LANES + hp))
    k_spec = pl.BlockSpec((seq, LANES), lambda hp, b, i: (b, C_AK // LANES + hp))
    v_spec = pl.BlockSpec((seq, LANES), lambda hp, b, i: (b, C_AV // LANES + hp))
    w_spec = pl.BlockSpec((None, 2, BIAS_LEN), lambda hp, b, i: (hp, 0, 0))
    b_spec = pl.BlockSpec((None, ATT_STARTS + 1, 2, ATT_Q, ATT_WIN), lambda hp, b, i: (hp, 0, 0, 0, 0))
    pad = pltpu.VMEM((seq + ATT_PAD, LANES), BF16)
    return ni, q_spec, k_spec, v_spec, w_spec, b_spec, pad


def _att_bias_tiles(wvec):
    return pl.pallas_call(
        _att_bias, name="att_bias", grid=(ATT_HEADS // 2,),
        in_specs=[pl.BlockSpec((None, 2, BIAS_LEN), lambda hp: (hp, 0, 0))],
        out_specs=pl.BlockSpec((None, ATT_STARTS + 1, 2, ATT_Q, ATT_WIN), lambda hp: (hp, 0, 0, 0, 0)),
        out_shape=jax.ShapeDtypeStruct((ATT_HEADS // 2, ATT_STARTS + 1, 2, ATT_Q, ATT_WIN), F32),
        compiler_params=_params("parallel"),
    )(wvec)


def _att_pad(src_ref, pad_ref):
    pad_ref[:ATT_PAD, :] = jnp.zeros((ATT_PAD, LANES), BF16)
    pad_ref[ATT_PAD:, :] = src_ref[...]


def _att_head(q2, sel):
    return jnp.where(sel, q2, jnp.zeros_like(q2)) * 0.125


def _att_softmax_rows(s_ref, bias_ref, rows):
    s = s_ref[rows, :] + bias_ref[rows, :]
    ex = jnp.exp(s - jnp.max(s, axis=-1, keepdims=True))
    return ex, 1.0 / jnp.sum(ex, axis=-1, keepdims=True)


def _att_fwd(proj, bias, batch, seq, exchange):
    ni, q_spec, k_spec, v_spec, _, b_spec, pad = _att_specs(batch, seq)

    def body(q_ref, k_ref, v_ref, bias_ref, o_ref, kp_ref, vp_ref, s_ref, e_ref):
        i = pl.program_id(2)

        @pl.when(i == 0)
        def _():
            _att_pad(k_ref, kp_ref)
            _att_pad(v_ref, vp_ref)

        win = pl.ds(pl.multiple_of(i * ATT_Q, ATT_Q), ATT_WIN)
        k2, v2, q2 = kp_ref[win, :], vp_ref[win, :], q_ref[...]
        lo = lax.broadcasted_iota(jnp.int32, (1, LANES), 1) < 64
        start = jnp.minimum(i, ATT_STARTS)
        out = jnp.zeros((ATT_Q, LANES), F32)
        for e in range(2):
            sel = lo if e == 0 else jnp.logical_not(lo)
            s_ref[e] = _dot_nt(_att_head(q2, sel), k2)
            rsum = []
            for c in range(ATT_Q // ATT_ROWS):
                rows = slice(c * ATT_ROWS, (c + 1) * ATT_ROWS)
                ex, r = _att_softmax_rows(s_ref.at[e], bias_ref.at[start, e], rows)
                e_ref[e, rows, :] = ex.astype(BF16)
                rsum.append(r)
            out = out + _dot(e_ref[e], jnp.where(sel, v2, jnp.zeros_like(v2))) * jnp.concatenate(rsum, axis=0)
        o_ref[...] = out.astype(BF16)

    return _call(
        body, name="att_fwd", grid=(ATT_HEADS // 2, batch, ni),
        in_specs=[q_spec, k_spec, v_spec, b_spec],
        out_specs=[pl.BlockSpec((ATT_Q, LANES), lambda hp, b, i: (b * ni + i, hp))],
        out_shape=[jax.ShapeDtypeStruct((batch * seq, ATT_HEADS * 64), BF16)],
        scratch=[pad, pad, pltpu.VMEM((2, ATT_Q, ATT_WIN), F32), pltpu.VMEM((2, ATT_Q, ATT_WIN), BF16)],
        semantics=("arbitrary", "arbitrary", "arbitrary"), args=(proj, proj, proj, bias), exchange=exchange)


def _mix_fwd(gro, ao, proj, b_gate, w_ret, w_att_t):
    t = gro.shape[0]
    tm, tn = min(256, t), 512

    def body(gro_ref, ao_ref, glr_ref, gla_ref, br_ref, ba_ref, wr_ref, wa_ref, z_ref, yr_ref, ya_ref):
        yr = _dot(gro_ref[...], wr_ref[...])
        ya = _dot_nt(ao_ref[...], wa_ref[...])
        gr = _sigmoid(glr_ref[...].astype(F32) + br_ref[...])
        ga = _sigmoid(gla_ref[...].astype(F32) + ba_ref[...])
        z_ref[...] = (gr * yr + ga * ya).astype(BF16)
        yr_ref[...] = yr.astype(BF16)
        ya_ref[...] = ya.astype(BF16)

    nb = D_MODEL // tn
    out = pl.BlockSpec((tm, tn), lambda i, j: (i, j))
    return pl.pallas_call(
        body, name="mix_fwd", grid=(t // tm, nb),
        in_specs=[pl.BlockSpec((tm, D_MODEL), lambda i, j: (i, 0)),
                  pl.BlockSpec((tm, 512), lambda i, j: (i, 0)),
                  pl.BlockSpec((tm, tn), lambda i, j: (i, C_GL // tn + j)),
                  pl.BlockSpec((tm, tn), lambda i, j: (i, C_GL // tn + nb + j)),
                  pl.BlockSpec((1, tn), lambda i, j: (0, j)),
                  pl.BlockSpec((1, tn), lambda i, j: (0, nb + j)),
                  pl.BlockSpec((D_MODEL, tn), lambda i, j: (0, j)),
                  pl.BlockSpec((tn, 512), lambda i, j: (j, 0))],
        out_specs=[out, out, out],
        out_shape=[jax.ShapeDtypeStruct((t, D_MODEL), BF16)] * 3,
        compiler_params=_params("parallel", "parallel"),
    )(gro, ao, proj, proj, b_gate, b_gate, w_ret, w_att_t)


def _out_fwd(z, x2, w_out, g2):
    t = z.shape[0]
    tm = min(256, t)

    def body(z_ref, x_ref, w_ref, g_ref, h_ref, hn_ref):
        h = x_ref[...] + _dot(z_ref[...], w_ref[...])
        h_ref[...] = h
        r = lax.rsqrt(jnp.mean(h * h, axis=-1, keepdims=True) + EPS)
        hn_ref[...] = (h * r * g_ref[...]).astype(BF16)

    row = pl.BlockSpec((tm, D_MODEL), lambda i: (i, 0))
    return pl.pallas_call(
        body, name="out_fwd", grid=(t // tm,),
        in_specs=[row, row, pl.BlockSpec((D_MODEL, D_MODEL), lambda i: (0, 0)),
                  pl.BlockSpec((1, D_MODEL), lambda i: (0, 0))],
        out_specs=[row, row],
        out_shape=[jax.ShapeDtypeStruct((t, D_MODEL), F32), jax.ShapeDtypeStruct((t, D_MODEL), BF16)],
        compiler_params=_params("parallel"),
    )(z, x2, w_out, g2)


def _ffn_up(hn, wg_t, wu_t):
    t = hn.shape[0]
    tm, tn = min(512, t), D_FF // 2

    def body(h_ref, wg_ref, wu_ref, g_ref, u_ref, a_ref):
        g = _dot_nt(h_ref[...], wg_ref[...])
        u = _dot_nt(h_ref[...], wu_ref[...])
        g_ref[...] = g.astype(BF16)
        u_ref[...] = u.astype(BF16)
        a_ref[...] = (g * _sigmoid(g) * u).astype(BF16)

    w_spec = pl.BlockSpec((tn, D_MODEL), lambda j, i: (j, 0))
    out = pl.BlockSpec((tm, tn), lambda j, i: (i, j))
    return pl.pallas_call(
        body, name="ffn_up", grid=(D_FF // tn, t // tm),
        in_specs=[pl.BlockSpec((tm, D_MODEL), lambda j, i: (i, 0)), w_spec, w_spec],
        out_specs=[out, out, out],
        out_shape=[jax.ShapeDtypeStruct((t, D_FF), BF16)] * 3,
        compiler_params=_params("parallel", "parallel"),
    )(hn, wg_t, wu_t)


def _ffn_down_loss(a, h1, tgt, w_down, g3):
    t = a.shape[0]
    tm = min(512, t)

    def body(a_ref, h_ref, t_ref, w_ref, g_ref, dh_ref, dhb_ref, loss_ref, dg_ref):
        @pl.when(pl.program_id(0) == 0)
        def _():
            loss_ref[...] = jnp.zeros_like(loss_ref)
            dg_ref[...] = jnp.zeros_like(dg_ref)

        g = g_ref[...]
        h2 = h_ref[...] + _dot(a_ref[...], w_ref[...])
        r = lax.rsqrt(jnp.mean(h2 * h2, axis=-1, keepdims=True) + EPS)
        err = h2 * r * g - t_ref[...]
        loss_ref[...] += jnp.sum(err * err) * (0.5 / D_MODEL)
        dy = err * (1.0 / D_MODEL)
        dh, dg_rows = _rms_bwd(h2, g, dy)
        dg_ref[...] += jnp.sum(dg_rows, axis=0, keepdims=True)
        dh_ref[...] = dh
        dhb_ref[...] = dh.astype(BF16)

    row = pl.BlockSpec((tm, D_MODEL), lambda i: (i, 0))
    vec = pl.BlockSpec((1, D_MODEL), lambda i: (0, 0))
    return pl.pallas_call(
        body, name="ffn_down_loss", grid=(t // tm,),
        in_specs=[pl.BlockSpec((tm, D_FF), lambda i: (i, 0)), row, row,
                  pl.BlockSpec((D_FF, D_MODEL), lambda i: (0, 0)), vec],
        out_specs=[row, row, pl.BlockSpec((1, LANES), lambda i: (0, 0)), vec],
        out_shape=[jax.ShapeDtypeStruct((t, D_MODEL), F32), jax.ShapeDtypeStruct((t, D_MODEL), BF16),
                   jax.ShapeDtypeStruct((1, LANES), F32), jax.ShapeDtypeStruct((1, D_MODEL), F32)],
        compiler_params=_params("arbitrary"),
    )(a, h1, tgt, w_down, g3)


def _ffn_bwd_act(dh2b, w_down, g_act, u_act, exchange):
    t = dh2b.shape[0]
    tm, tn = min(512, t), D_FF // 2

    def body(d_ref, w_ref, g_ref, u_ref, dg_ref, du_ref):
        da = _dot_nt(d_ref[...], w_ref[...])
        g = g_ref[...].astype(F32)
        u = u_ref[...].astype(F32)
        sg = _sigmoid(g)
        dg_ref[...] = (da * u * sg * (1.0 + g * (1.0 - sg))).astype(BF16)
        du_ref[...] = (da * g * sg).astype(BF16)

    blk = pl.BlockSpec((tm, tn), lambda j, i: (i, j))
    return _call(
        body, name="ffn_bwd_act", grid=(D_FF // tn, t // tm),
        in_specs=[pl.BlockSpec((tm, D_MODEL), lambda j, i: (i, 0)),
                  pl.BlockSpec((tn, D_MODEL), lambda j, i: (j, 0)), blk, blk],
        out_specs=[blk, blk],
        out_shape=[jax.ShapeDtypeStruct((t, D_FF), BF16)] * 2,
        semantics=("parallel", "parallel"), args=(dh2b, w_down, g_act, u_act), exchange=exchange)


def _ffn_bwd_in(dg, du, wg_t, wu_t, h1, dh2, g2, exchange):
    t = dg.shape[0]
    tm, tk = min(512, t), D_FF // 2
    nk = D_FF // tk

    def body(dg_ref, du_ref, wg_ref, wu_ref, h_ref, d2_ref, g_ref, dh_ref, dhb_ref, gn_ref, acc_ref):
        i, kk = pl.program_id(0), pl.program_id(1)

        @pl.when((i == 0) & (kk == 0))
        def _():
            gn_ref[...] = jnp.zeros_like(gn_ref)

        @pl.when(kk == 0)
        def _():
            acc_ref[...] = jnp.zeros_like(acc_ref)

        acc_ref[...] += _dot(dg_ref[...], wg_ref[...]) + _dot(du_ref[...], wu_ref[...])

        @pl.when(kk == nk - 1)
        def _():
            dx, dg_rows = _rms_bwd(h_ref[...], g_ref[...], acc_ref[...])
            dh = d2_ref[...] + dx
            dh_ref[...] = dh
            dhb_ref[...] = dh.astype(BF16)
            gn_ref[...] += jnp.sum(dg_rows, axis=0, keepdims=True)

    act = pl.BlockSpec((tm, tk), lambda i, kk: (i, kk))
    wsp = pl.BlockSpec((tk, D_MODEL), lambda i, kk: (kk, 0))
    row = pl.BlockSpec((tm, D_MODEL), lambda i, kk: (i, 0))
    vec = pl.BlockSpec((1, D_MODEL), lambda i, kk: (0, 0))
    return _call(
        body, name="ffn_bwd_in", grid=(t // tm, nk),
        in_specs=[act, act, wsp, wsp, row, row, vec],
        out_specs=[row, row, vec],
        out_shape=[jax.ShapeDtypeStruct((t, D_MODEL), F32), jax.ShapeDtypeStruct((t, D_MODEL), BF16),
                   jax.ShapeDtypeStruct((1, D_MODEL), F32)],
        scratch=[pltpu.VMEM((tm, D_MODEL), F32)],
        semantics=("arbitrary", "arbitrary"), args=(dg, du, wg_t, wu_t, h1, dh2, g2), exchange=exchange)


def _mix_bwd(dh1b, w_out, proj, b_gate, y_ret, y_att, exchange):
    t = dh1b.shape[0]
    tm, tn = min(256, t), 512
    nb = D_MODEL // tn

    def body(d_ref, w_ref, glr_ref, gla_ref, br_ref, ba_ref, yr_ref, ya_ref,
             dyr_ref, dya_ref, dglr_ref, dgla_ref, dbr_ref, dba_ref):
        @pl.when(pl.program_id(1) == 0)
        def _():
            dbr_ref[...] = jnp.zeros_like(dbr_ref)
            dba_ref[...] = jnp.zeros_like(dba_ref)

        dz = _dot_nt(d_ref[...], w_ref[...])
        gr = _sigmoid(glr_ref[...].astype(F32) + br_ref[...])
        ga = _sigmoid(gla_ref[...].astype(F32) + ba_ref[...])
        dyr_ref[...] = (dz * gr).astype(BF16)
        dya_ref[...] = (dz * ga).astype(BF16)
        dglr = dz * yr_ref[...].astype(F32) * gr * (1.0 - gr)
        dgla = dz * ya_ref[...].astype(F32) * ga * (1.0 - ga)
        dglr_ref[...] = dglr.astype(BF16)
        dgla_ref[...] = dgla.astype(BF16)
        dbr_ref[...] += jnp.sum(dglr, axis=0, keepdims=True)
        dba_ref[...] += jnp.sum(dgla, axis=0, keepdims=True)

    blk = pl.BlockSpec((tm, tn), lambda j, i: (i, j))
    vec = pl.BlockSpec((1, tn), lambda j, i: (0, j))
    return _call(
        body, name="mix_bwd", grid=(nb, t // tm),
        in_specs=[pl.BlockSpec((tm, D_MODEL), lambda j, i: (i, 0)),
                  pl.BlockSpec((tn, D_MODEL), lambda j, i: (j, 0)),
                  pl.BlockSpec((tm, tn), lambda j, i: (i, C_GL // tn + j)),
                  pl.BlockSpec((tm, tn), lambda j, i: (i, C_GL // tn + nb + j)),
                  vec, pl.BlockSpec((1, tn), lambda j, i: (0, nb + j)), blk, blk],
        out_specs=[blk, blk, blk, blk, vec, vec],
        out_shape=[jax.ShapeDtypeStruct((t, D_MODEL), BF16)] * 4 + [jax.ShapeDtypeStruct((1, D_MODEL), F32)] * 2,
        semantics=("arbitrary", "arbitrary"), args=(dh1b, w_out, proj, proj, b_gate, b_gate, y_ret, y_att),
        exchange=exchange)


def _ret_bwd(dgro, proj, o_ret, qr, kr, cs, sn, lg_arr, batch, seq, exchange):
    t = batch * seq
    nt = seq // RET_TILE

    def body(dgro_ref, rg_ref, o_ref, qr_ref, kr_ref, v_ref, cs_ref, sn_ref, lg_ref,
             dq_ref, dk_ref, dv_ref, drg_ref, do_ref, st_ref):
        lg = lg_ref[:, 0:1]
        inside, q_dec, k_dec, tile_dec = _decay(lg)

        state = jnp.zeros((RET_KEY_DIM, RET_VAL_DIM), F32)
        for i in range(nt - 1):
            rows = slice(i * RET_TILE, (i + 1) * RET_TILE)
            state = state * tile_dec + _dot_tn(_scaled(kr_ref[rows, :], k_dec), v_ref[rows, :])
            st_ref[i + 1] = state.astype(BF16)

        for i in range(nt):
            rows = slice(i * RET_TILE, (i + 1) * RET_TILE)
            o = o_ref[rows, :]
            xc = o - jnp.mean(o, axis=-1, keepdims=True)
            rs = lax.rsqrt(jnp.mean(xc * xc, axis=-1, keepdims=True) + EPS)
            nrm = xc * rs
            rg = rg_ref[rows, :].astype(F32)
            sg = _sigmoid(rg)
            dg = dgro_ref[rows, :].astype(F32)
            drg_ref[rows, :] = (dg * nrm * sg * (1.0 + rg * (1.0 - sg))).astype(BF16)
            dn = dg * rg * sg
            do = rs * (dn - jnp.mean(dn, axis=-1, keepdims=True)
                       - nrm * jnp.mean(dn * nrm, axis=-1, keepdims=True))
            do_ref[rows, :] = do.astype(BF16)

        dstate = jnp.zeros((RET_KEY_DIM, RET_VAL_DIM), F32)
        for i in reversed(range(nt)):
            rows = slice(i * RET_TILE, (i + 1) * RET_TILE)
            qi, ki, vi, doi = qr_ref[rows, :], kr_ref[rows, :], v_ref[rows, :], do_ref[rows, :]
            p = (_dot_nt(qi, ki) * inside).astype(BF16)
            dp = (_dot_nt(doi, vi) * inside).astype(BF16)
            dq = _dot(dp, ki)
            dk = _dot_tn(dp, qi)
            dv = _dot_tn(p, doi)
            if i > 0:
                dq = dq + _dot_nt(doi, st_ref[i]) * q_dec
            if i < nt - 1:
                dsb = dstate.astype(BF16)
                dk = dk + _dot_nt(vi, dsb) * k_dec
                dv = dv + _dot(_scaled(ki, k_dec), dsb)
            if i > 0:
                dstate = dstate * tile_dec + _dot_tn(_scaled(qi, q_dec), doi)
            dq_ref[rows, :] = (dq * cs_ref[rows, :] - pltpu.roll(dq, 64, 1) * sn_ref[rows, :]).astype(BF16)
            dk = (dk * cs_ref[rows, :] - pltpu.roll(dk, 64, 1) * sn_ref[rows, :]) * (RET_KEY_DIM ** -0.5)
            dk_ref[rows, :] = dk.astype(BF16)
            dv_ref[rows, :] = dv.astype(BF16)

    key = pl.BlockSpec((seq, RET_KEY_DIM), lambda b, h: (b, h))
    val = pl.BlockSpec((seq, RET_VAL_DIM), lambda b, h: (b, h))
    tab = pl.BlockSpec((seq, RET_KEY_DIM), lambda b, h: (0, 0))
    return _call(
        body, name="ret_bwd", grid=(batch, RET_HEADS),
        in_specs=[val, pl.BlockSpec((seq, RET_VAL_DIM), lambda b, h: (b, C_RG // RET_VAL_DIM + h)), val, key, key,
                  pl.BlockSpec((seq, RET_VAL_DIM), lambda b, h: (b, C_RV // RET_VAL_DIM + h)), tab, tab,
                  pl.BlockSpec((None, 1, LANES), lambda b, h: (h, 0, 0))],
        out_specs=[key, key, val, val],
        out_shape=[jax.ShapeDtypeStruct((t, RET_HEADS * RET_KEY_DIM), BF16)] * 2
                  + [jax.ShapeDtypeStruct((t, RET_HEADS * RET_VAL_DIM), BF16)] * 2,
        scratch=[pltpu.VMEM((seq, RET_VAL_DIM), BF16), pltpu.VMEM((nt, RET_KEY_DIM, RET_VAL_DIM), BF16)],
        semantics=("parallel", "parallel"), args=(dgro, proj, o_ret, qr, kr, proj, cs, sn, lg_arr),
        exchange=exchange)


def _att_bwd(proj, bias, dao, batch, seq, exchange):
    ni, q_spec, k_spec, v_spec, w_spec, b_spec, pad = _att_specs(batch, seq)
    t = batch * seq

    def body(q_ref, k_ref, v_ref, bias_ref, do_ref, dq_ref, dk_ref, dv_ref, dw_ref,
             dbias_ref, dk_acc, dv_acc, kp_ref, vp_ref, s_ref, dp_ref, e_ref, ds_ref):
        b, i = pl.program_id(1), pl.program_id(2)

        @pl.when((b == 0) & (i == 0))
        def _():
            dbias_ref[...] = jnp.zeros_like(dbias_ref)

        @pl.when(i == 0)
        def _():
            _att_pad(k_ref, kp_ref)
            _att_pad(v_ref, vp_ref)
            dk_acc[...] = jnp.zeros_like(dk_acc)
            dv_acc[...] = jnp.zeros_like(dv_acc)

        win = pl.ds(pl.multiple_of(i * ATT_Q, ATT_Q), ATT_WIN)
        k2, v2, q2, do2 = kp_ref[win, :], vp_ref[win, :], q_ref[...], do_ref[...]
        lo = lax.broadcasted_iota(jnp.int32, (1, LANES), 1) < 64
        dq = jnp.zeros((ATT_Q, LANES), F32)
        dk = jnp.zeros((LANES, ATT_WIN), F32)
        dv = jnp.zeros((LANES, ATT_WIN), F32)
        start = jnp.minimum(i, ATT_STARTS)
        for e in range(2):
            sel = lo if e == 0 else jnp.logical_not(lo)
            qm = _att_head(q2, sel)
            dom = jnp.where(sel, do2, jnp.zeros_like(do2))
            s_ref[e] = _dot_nt(qm, k2)
            dp_ref[e] = _dot_nt(dom, v2)
            rsum = []
            for c in range(ATT_Q // ATT_ROWS):
                rows = slice(c * ATT_ROWS, (c + 1) * ATT_ROWS)
                ex, r = _att_softmax_rows(s_ref.at[e], bias_ref.at[start, e], rows)
                dp = dp_ref[e, rows, :]
                mean = jnp.sum(dp * ex, axis=-1, keepdims=True) * r
                ds = ex * ((dp - mean) * r)
                dbias_ref[e, rows, :] += ds
                ds_ref[e, rows, :] = ds.astype(BF16)
                e_ref[e, rows, :] = ex.astype(BF16)
                rsum.append(r)
            dq = dq + _dot(ds_ref[e], jnp.where(sel, k2, jnp.zeros_like(k2)))
            dk = dk + _dot_tn(qm, ds_ref[e])
            dv = dv + _dot_tn((dom.astype(F32) * jnp.concatenate(rsum, axis=0)).astype(BF16), e_ref[e])
        dq_ref[...] = (dq * 0.125).astype(BF16)
        dk_acc[:, win] += dk
        dv_acc[:, win] += dv

        @pl.when(i == ni - 1)
        def _():
            dk_ref[...] = dk_acc[:, ATT_PAD:].T.astype(BF16)
            dv_ref[...] = dv_acc[:, ATT_PAD:].T.astype(BF16)

        @pl.when((b == batch - 1) & (i == ni - 1))
        def _():
            n_i = lax.broadcasted_iota(jnp.int32, (ATT_Q, BIAS_LEN), 0)
            for e in range(2):
                xw = jnp.concatenate([jnp.zeros((ATT_Q, BIAS_LEN - ATT_WIN), F32), dbias_ref[e]], axis=1)
                for bit in range(8):
                    xw = jnp.where(((n_i >> bit) & 1) == 1, pltpu.roll(xw, BIAS_LEN - (1 << bit), 1), xw)
                dw_ref[e:e + 1, :] = jnp.sum(xw, axis=0, keepdims=True)

    seq_blk = pl.BlockSpec((seq, LANES), lambda hp, b, i: (b, hp))
    q_out = pl.BlockSpec((ATT_Q, LANES), lambda hp, b, i: (b * ni + i, hp))
    return _call(
        body, name="att_bwd", grid=(ATT_HEADS // 2, batch, ni),
        in_specs=[q_spec, k_spec, v_spec, b_spec, q_out],
        out_specs=[q_out, seq_blk, seq_blk, w_spec],
        out_shape=[jax.ShapeDtypeStruct((t, 512), BF16)] * 3
                  + [jax.ShapeDtypeStruct((ATT_HEADS // 2, 2, BIAS_LEN), F32)],
        scratch=[pltpu.VMEM((2, ATT_Q, ATT_WIN), F32),
                 pltpu.VMEM((LANES, seq + ATT_PAD), F32), pltpu.VMEM((LANES, seq + ATT_PAD), F32), pad, pad,
                 pltpu.VMEM((2, ATT_Q, ATT_WIN), F32), pltpu.VMEM((2, ATT_Q, ATT_WIN), F32),
                 pltpu.VMEM((2, ATT_Q, ATT_WIN), BF16), pltpu.VMEM((2, ATT_Q, ATT_WIN), BF16)],
        semantics=("arbitrary", "arbitrary", "arbitrary"), args=(proj, proj, proj, bias, dao), exchange=exchange)


def _rms_in_bwd(x2, dxn, dh1, g1):
    t = x2.shape[0]
    tm = min(512, t)

    def body(x_ref, d_ref, h_ref, g_ref, dx_ref, dg_ref):
        @pl.when(pl.program_id(0) == 0)
        def _():
            dg_ref[...] = jnp.zeros_like(dg_ref)

        dx, dg_rows = _rms_bwd(x_ref[...], g_ref[...], d_ref[...])
        dx_ref[...] = h_ref[...] + dx
        dg_ref[...] += jnp.sum(dg_rows, axis=0, keepdims=True)

    row = pl.BlockSpec((tm, D_MODEL), lambda i: (i, 0))
    vec = pl.BlockSpec((1, D_MODEL), lambda i: (0, 0))
    return pl.pallas_call(
        body, name="rms_in_bwd", grid=(t // tm,),
        in_specs=[row, row, row, vec], out_specs=[row, vec],
        out_shape=[jax.ShapeDtypeStruct((t, D_MODEL), F32), jax.ShapeDtypeStruct((1, D_MODEL), F32)],
        compiler_params=_params("arbitrary"),
    )(x2, dxn, dh1, g1)


def _pack_small(dg1, dbr, dba, dg2, dg3, dw, loss):
    def body(a_ref, b_ref, c_ref, d_ref, e_ref, w_ref, l_ref, o_ref):
        o_ref[...] = jnp.zeros_like(o_ref)
        for r, ref in enumerate((a_ref, b_ref, c_ref, d_ref, e_ref)):
            o_ref[r:r + 1, :] = ref[...]
        o_ref[5:6, 0:LANES] = l_ref[...]
        for hp in range(ATT_HEADS // 2):
            o_ref[8 + 2 * hp:10 + 2 * hp, :] = w_ref[hp]

    return pl.pallas_call(body, name="pack_small",
                          out_shape=jax.ShapeDtypeStruct((16, D_MODEL), F32))(dg1, dbr, dba, dg2, dg3, dw, loss)


def _rotary_tables(seq):
    freqs = ROPE_BASE ** (-jnp.arange(0, RET_KEY_DIM, 2, dtype=F32) / RET_KEY_DIM)
    ang = jnp.arange(seq, dtype=F32)[:, None] * freqs[None, :]
    cos, sin = jnp.cos(ang), jnp.sin(ang)
    return jnp.concatenate([cos, cos], axis=1), jnp.concatenate([-sin, sin], axis=1)


def _bias_rows(rel_bias):
    n_far = BIAS_LEN - ATT_Q - MAX_REL + 1
    n_near = BIAS_LEN - n_far - (N_REL - 2)
    w = jnp.concatenate([jnp.broadcast_to(rel_bias[:, N_REL - 1:], (ATT_HEADS, n_far)),
                         rel_bias[:, 1:N_REL - 1][:, ::-1],
                         jnp.broadcast_to(rel_bias[:, :1], (ATT_HEADS, n_near))], axis=1)
    return w.reshape(ATT_HEADS // 2, 2, BIAS_LEN)


def _bias_rows_bwd(dw):
    n_far = BIAS_LEN - ATT_Q - MAX_REL + 1
    mid = dw[:, n_far:n_far + N_REL - 2][:, ::-1]
    return jnp.concatenate([jnp.sum(dw[:, n_far + N_REL - 2:], axis=1, keepdims=True), mid,
                            jnp.sum(dw[:, :n_far], axis=1, keepdims=True)], axis=1)


def _step(x, tgt, norm_mix, b_gate, norm_ffn, norm_final, rel_bias_shard, shard):
    batch, seq, _ = x.shape
    t = batch * seq
    n_rb = rel_bias_shard.shape[-1]
    x2, tgt2 = x.reshape(t, D_MODEL), tgt.reshape(t, D_MODEL)
    g3 = norm_final.reshape(1, D_MODEL)
    cs, sn = _rotary_tables(seq)
    lg = np.log(1.0 - 2.0 ** (-5.0 - np.arange(RET_HEADS, dtype=np.float32))).astype(np.float32)
    lg_arr = jnp.asarray(np.broadcast_to(lg[:, None, None], (RET_HEADS, 1, LANES)))

    def gather(*names):
        return _ChipGather([shard[nm] for nm in names])

    def scatter(*grads):
        return _Exchange(grads, scatter=True)

    rb_pad = jnp.pad(rel_bias_shard, ((0, 0), (0, LANES - n_rb)))
    (xn,), (w_in_t, rb_full) = _rms_fwd(x2, norm_mix, _ChipGather([shard["w_in_t"], rb_pad]))
    rb_full = rb_full.reshape(N_DEV, ATT_HEADS, LANES)[:, :, :n_rb]
    bias = _att_bias_tiles(_bias_rows(jnp.transpose(rb_full, (1, 0, 2)).reshape(ATT_HEADS, N_DEV * n_rb)))
    proj, (w_ret, w_att_t, w_out, w_gate_t) = _mm(
        xn, w_in_t, tb=True, out_dtype=BF16, tm=1024, tn=1664, tk=1024, name="proj",
        exchange=gather("w_ret", "w_att_t", "w_out", "w_gate_t"))
    (gro, o_ret, qr, kr), (w_up_t,) = _ret_fwd(proj, cs, sn, lg_arr, batch, seq, gather("w_up_t"))
    (ao,), (w_down,) = _att_fwd(proj, bias, batch, seq, gather("w_down"))
    z, y_ret, y_att = _mix_fwd(gro, ao, proj, b_gate, w_ret, w_att_t)
    h1, hn = _out_fwd(z, x2, w_out, norm_ffn)
    g_act, u_act, a_act = _ffn_up(hn, w_gate_t, w_up_t)
    dh2, dh2b, loss, dg3 = _ffn_down_loss(a_act, h1, tgt2, w_down, g3)

    wg = dict(out_dtype=BF16, tn=1024, ta=True)
    slots = {}
    dw_down = _mm(a_act, dh2b, tm=1408, tk=1024, name="dw_down", **wg)
    (d_gact, d_uact), (slots["w_down"],) = _ffn_bwd_act(dh2b, w_down, g_act, u_act, scatter(dw_down))
    dw_gate = _mm(d_gact, hn, tm=1408, tk=1024, name="dw_gate", **wg)
    dw_up = _mm(d_uact, hn, tm=1408, tk=1024, name="dw_up", **wg)
    (dh1, dh1b, dg2), (slots["w_gate_t"],) = _ffn_bwd_in(d_gact, d_uact, w_gate_t, w_up_t, h1, dh2, norm_ffn,
                                                       scatter(dw_gate))
    dw_out = _mm(z, dh1b, tm=1024, tk=2048, name="dw_out", **wg)
    (dyr, dya, dglr, dgla, dbr, dba), _ = _mix_bwd(dh1b, w_out, proj, b_gate, y_ret, y_att, None)
    dgro = _mm(dyr, w_ret, tb=True, out_dtype=BF16, tm=1024, tn=1024, tk=1024, name="dgro")
    dao = _mm(dya, w_att_t, out_dtype=BF16, tm=1024, tn=512, tk=1024, name="dao")
    dw_ret = _mm(gro, dyr, tm=1024, tk=2048, name="dw_ret", **wg)
    dw_att = _mm(dya, ao, tm=1024, tk=2048, name="dw_att", **wg)
    (drq, drk, drv, drg), (slots["w_up_t"],) = _ret_bwd(
        dgro, proj, o_ret, qr, kr, cs, sn, lg_arr, batch, seq, scatter(dw_up))
    (daq, dak, dav, dw), (slots["w_out"], slots["w_ret"], slots["w_att_t"]) = _att_bwd(
        proj, bias, dao, batch, seq, scatter(dw_out, dw_ret, dw_att))
    dproj = [drq, drk, drv, drg, daq, dak, dav, dglr, dgla]
    dw_in = _mm_pieces(dproj, xn, ta=True, out_dtype=BF16, tm=512, tn=1024, tk=1024, name="dw_in")
    (dw_in_sibling,) = _alone(_PairSwap([dw_in]), "swap_w_in")
    dw_in_pairs = _pair_add(dw_in, dw_in_sibling, "pair_w_in")
    dxn, (slots["w_in_t"],) = _mm_pieces(dproj, w_in_t, ta=False, out_dtype=F32, tm=1024, tn=1024, tk=512, name="dxn",
                                  exchange=_ChipScatter([dw_in_pairs]))
    dx, dg1 = _rms_in_bwd(x2, dxn, dh1, norm_mix)
    small = _pack_small(dg1, dbr, dba, dg2, dg3, dw, loss)
    (small_slots,) = _alone(_ChipGather([small]), "gather_small")
    return dx.reshape(batch, seq, D_MODEL), slots, small_slots.reshape(N_DEV, 16, D_MODEL)


def _row_tile(r, c):
    return max(d for d in range(16, r + 1, 16) if r % d == 0 and (d * c <= 256 * 1024 or d == 16))


def _pair_add(grad, got, name):
    _, r, c = got.shape
    tr = _row_tile(r, c)
    core = lax.axis_index("c").astype(jnp.int32).reshape(1)

    def body(core_ref, g_ref, a_ref, o_ref):
        o_ref[...] = (g_ref[...].astype(F32) + a_ref[...].astype(F32)).astype(o_ref.dtype)

    blk = pl.BlockSpec((None, tr, c), lambda q, i, core_ref: (q, i, 0))
    return pl.pallas_call(
        body, name=name,
        grid_spec=pltpu.PrefetchScalarGridSpec(
            num_scalar_prefetch=1, grid=(4, r // tr),
            in_specs=[pl.BlockSpec((None, None, tr, c), lambda q, i, core_ref: (q, core_ref[0], i, 0)), blk],
            out_specs=blk),
        out_shape=jax.ShapeDtypeStruct(got.shape, got.dtype),
        compiler_params=_params("parallel", "parallel"),
    )(core, grad.reshape(4, 2, r, c), got)


def _sum_slots(slots, name):
    n, r, c = slots.shape
    tr = _row_tile(r, c)

    def body(s_ref, o_ref):
        acc = s_ref[0].astype(F32)
        for s in range(1, n):
            acc = acc + s_ref[s].astype(F32)
        o_ref[...] = acc

    return pl.pallas_call(
        body, name=name, grid=(r // tr,),
        in_specs=[pl.BlockSpec((n, tr, c), lambda i: (0, i, 0))],
        out_specs=pl.BlockSpec((tr, c), lambda i: (i, 0)),
        out_shape=jax.ShapeDtypeStruct((r, c), F32),
        compiler_params=_params("parallel"),
    )(slots)


def _adamw_math(w, g, m, v):
    m = ADAM_B1 * m + (1.0 - ADAM_B1) * g
    v = ADAM_B2 * v + (1.0 - ADAM_B2) * (g * g)
    m_hat = m / (1.0 - ADAM_B1 ** ADAM_STEP)
    v_hat = v / (1.0 - ADAM_B2 ** ADAM_STEP)
    return -ADAM_LR * (m_hat / (jnp.sqrt(v_hat) + ADAM_EPS) + ADAM_WD * w), m, v


def _adamw(w, g, m, v, name):
    r, c = w.shape
    tr = r
    while tr * c > 128 * 1024 and tr % 16 == 0:
        tr //= 2

    def body(w_ref, g_ref, m_ref, v_ref, d_ref, nm_ref, nv_ref):
        d_ref[...], nm_ref[...], nv_ref[...] = _adamw_math(w_ref[...], g_ref[...], m_ref[...], v_ref[...])

    blk = pl.BlockSpec((tr, c), lambda i: (i, 0))
    return pl.pallas_call(
        body, name=name, grid=(r // tr,),
        in_specs=[blk] * 4, out_specs=[blk] * 3,
        out_shape=[jax.ShapeDtypeStruct((r, c), F32)] * 3,
        compiler_params=_params("parallel"),
    )(w, g, m, v)


def _adamw_small(ws, gs, ms, vs):
    n = len(ws)

    def body(*refs):
        for i in range(n):
            w_ref, g_ref, m_ref, v_ref = (refs[j * n + i] for j in range(4))
            d_ref, nm_ref, nv_ref = (refs[(4 + j) * n + i] for j in range(3))
            d_ref[...], nm_ref[...], nv_ref[...] = _adamw_math(w_ref[...], g_ref[...], m_ref[...], v_ref[...])

    shapes = [jax.ShapeDtypeStruct(w.shape, F32) for w in ws]
    outs = pl.pallas_call(body, name="adamw_small", out_shape=shapes * 3)(*ws, *gs, *ms, *vs)
    return outs[:n], outs[n:2 * n], outs[2 * n:]


def kernel(x, norm_mix, w_in, b_gate, rel_bias, w_ret_out, w_att_out, w_out, norm_ffn, w_ffn_gate, w_ffn_up, w_ffn_down, norm_final, loss_target, m_norm_mix, m_w_in, m_b_gate, m_rel_bias, m_w_ret_out, m_w_att_out, m_w_out, m_norm_ffn, m_w_ffn_gate, m_w_ffn_up, m_w_ffn_down, m_norm_final, v_norm_mix, v_w_in, v_b_gate, v_rel_bias, v_w_ret_out, v_w_att_out, v_w_out, v_norm_ffn, v_w_ffn_gate, v_w_ffn_up, v_w_ffn_down, v_norm_final):
    me = _index(_place())
    n_rb = rel_bias.shape[-1]

    shard = dict(w_in_t=w_in[0].T, w_gate_t=w_ffn_gate[0].T, w_up_t=w_ffn_up[0].T, w_down=w_ffn_down[0],
                 w_ret=w_ret_out[0], w_out=w_out[0], w_att_t=w_att_out[0].T)
    shard = {nm: s.astype(BF16) for nm, s in shard.items()}
    dx, slots, small_slots = _step(x, loss_target, norm_mix, b_gate, norm_ffn, norm_final, rel_bias[0], shard)
    summed = {nm: _sum_slots(s, "sum_" + nm) for nm, s in slots.items()}
    small_sum = _sum_slots(small_slots, "sum_small")
    loss = small_sum[5, 0]

    transposed = dict(w_in="w_in_t", w_ffn_gate="w_gate_t", w_ffn_up="w_up_t", w_att_out="w_att_t")
    g = dict(
        w_ffn_down=summed["w_down"], w_ret_out=summed["w_ret"], w_out=summed["w_out"],
        norm_mix=small_sum[0:1], b_gate=jnp.concatenate([small_sum[1:2], small_sum[2:3]], axis=1),
        norm_ffn=small_sum[3:4], norm_final=small_sum[4:5],
        rel_bias=lax.dynamic_slice_in_dim(_bias_rows_bwd(small_sum[8:16]), me * n_rb, n_rb, axis=1),
    )
    w = dict(norm_mix=norm_mix, w_in=w_in, b_gate=b_gate, rel_bias=rel_bias, w_ret_out=w_ret_out, w_att_out=w_att_out,
             w_out=w_out, norm_ffn=norm_ffn, w_ffn_gate=w_ffn_gate, w_ffn_up=w_ffn_up, w_ffn_down=w_ffn_down,
             norm_final=norm_final)
    m = dict(norm_mix=m_norm_mix, w_in=m_w_in, b_gate=m_b_gate, rel_bias=m_rel_bias, w_ret_out=m_w_ret_out,
             w_att_out=m_w_att_out, w_out=m_w_out, norm_ffn=m_norm_ffn, w_ffn_gate=m_w_ffn_gate, w_ffn_up=m_w_ffn_up,
             w_ffn_down=m_w_ffn_down, norm_final=m_norm_final)
    v = dict(norm_mix=v_norm_mix, w_in=v_w_in, b_gate=v_b_gate, rel_bias=v_rel_bias, w_ret_out=v_w_ret_out,
             w_att_out=v_w_att_out, w_out=v_w_out, norm_ffn=v_norm_ffn, w_ffn_gate=v_w_ffn_gate, w_ffn_up=v_w_ffn_up,
             w_ffn_down=v_w_ffn_down, norm_final=v_norm_final)
    order = ("norm_mix", "w_in", "b_gate", "rel_bias", "w_ret_out", "w_att_out", "w_out", "norm_ffn",
             "w_ffn_gate", "w_ffn_up", "w_ffn_down", "norm_final")
    small_names = ("norm_mix", "b_gate", "rel_bias", "norm_ffn", "norm_final")

    def flat(a):
        return a[0] if a.ndim == 3 else a.reshape(-1, a.shape[-1])

    grad, delta, new_m, new_v = {}, {}, {}, {}
    for nm in order:
        if nm in transposed:
            g_t = summed[transposed[nm]]
            d, nmom, nvar = _adamw(w[nm][0].T, g_t, m[nm][0].T, v[nm][0].T, "adamw_" + nm)
            grad[nm], delta[nm], new_m[nm], new_v[nm] = (a.T[None] for a in (g_t, d, nmom, nvar))
        elif nm not in small_names:
            d, nmom, nvar = _adamw(flat(w[nm]), g[nm], flat(m[nm]), flat(v[nm]), "adamw_" + nm)
            grad[nm], delta[nm], new_m[nm], new_v[nm] = (a.reshape(w[nm].shape) for a in (g[nm], d, nmom, nvar))
    ds, nms, nvs = _adamw_small([flat(w[nm]) for nm in small_names], [g[nm] for nm in small_names],
                                [flat(m[nm]) for nm in small_names], [flat(v[nm]) for nm in small_names])
    for i, nm in enumerate(small_names):
        grad[nm], delta[nm], new_m[nm], new_v[nm] = (a.reshape(w[nm].shape) for a in (g[nm], ds[i], nms[i], nvs[i]))

    return (loss, dx, *[grad[nm] for nm in order], *[delta[nm] for nm in order],
            *[new_m[nm] for nm in order], *[new_v[nm] for nm in order])
```

```python
import numpy as np
import jax
import jax.numpy as jnp
from jax import lax
from jax.experimental import pallas as pl
from jax.experimental.pallas import tpu as pltpu

F32 = jnp.float32
BF16 = jnp.bfloat16
MESH = pl.DeviceIdType.MESH

D_MODEL = 1024
CHUNK = 64
RET_HEADS = 4
RET_KEY_DIM = 128
RET_VAL_DIM = 256
ATT_HEADS = 8
BAND_CHUNKS = 8
MAX_REL = 256
N_REL = CHUNK + MAX_REL
D_FF = 2816
N_IN = 6656
ROPE_BASE = 10000.0
EPS = 1e-6
NEG_INF = -1e30
C_RQ, C_RK, C_RV, C_RG, C_AQ, C_AK, C_AV, C_GL = 0, 512, 1024, 2048, 3072, 3584, 4096, 4608

ADAM_LR = 0.001
ADAM_B1 = 0.9
ADAM_B2 = 0.999
ADAM_EPS = 1e-08
ADAM_WD = 0.01
ADAM_STEP = 10

N_DEV = 8
LANES = 128
RET_TILE = 256
ATT_Q = 256
ATT_PAD = BAND_CHUNKS * CHUNK
ATT_WIN = ATT_PAD + ATT_Q
ATT_STARTS = ATT_PAD // ATT_Q
ATT_ROWS = 32
BIAS_LEN = 1024
VMEM_LIMIT = 48 * 1024 * 1024


def _params(*sem):
    return pltpu.CompilerParams(dimension_semantics=sem, vmem_limit_bytes=VMEM_LIMIT)


def _dot(a, b):
    return lax.dot_general(a, b, (((1,), (0,)), ((), ())), preferred_element_type=F32)


def _dot_nt(a, b):
    return lax.dot_general(a, b, (((1,), (1,)), ((), ())), preferred_element_type=F32)


def _dot_tn(a, b):
    return lax.dot_general(a, b, (((0,), (0,)), ((), ())), preferred_element_type=F32)


def _sigmoid(x):
    return 1.0 / (1.0 + jnp.exp(-x))


def _rms_bwd(x, g, dy):
    r = lax.rsqrt(jnp.mean(x * x, axis=-1, keepdims=True) + EPS)
    u = dy * g
    dx = r * u - x * (r * r * r) * jnp.mean(u * x, axis=-1, keepdims=True)
    return dx, dy * x * r


def _place():
    return lax.axis_index("x"), lax.axis_index("y"), lax.axis_index("c")


def _peer(k):
    x, y, c = _place()
    return ((1 - x) if k & 4 else x, (1 - y) if k & 2 else y, (1 - c) if k & 1 else c)


def _index(place):
    return 4 * place[0] + 2 * place[1] + place[2]


def _rows(ref, block, nrows):
    align = 16 if ref.dtype == BF16 else 8
    return ref.at[pl.ds(pl.multiple_of(block * nrows, align), nrows)]


class _Exchange:
    def __init__(self, arrays, scatter):
        self.arrays, self.scatter, self.n = list(arrays), scatter, len(arrays)

    def out_shape(self):
        if self.scatter:
            return [jax.ShapeDtypeStruct((N_DEV, a.shape[0] // N_DEV) + a.shape[1:], a.dtype) for a in self.arrays]
        return [jax.ShapeDtypeStruct((N_DEV * a.shape[0],) + a.shape[1:], a.dtype) for a in self.arrays]

    def scratch(self):
        return [pltpu.SemaphoreType.DMA((self.n, N_DEV - 1)), pltpu.SemaphoreType.DMA((self.n, N_DEV - 1)),
                pltpu.SemaphoreType.DMA((self.n,))]

    def _copies(self, ins, outs, sems):
        send_sems, recv_sems, local_sems = sems
        me = _index(_place())

        def src(w, to):
            return _rows(ins[w], to, ins[w].shape[0] // N_DEV) if self.scatter else ins[w]

        def dst(w, origin):
            return outs[w].at[origin] if self.scatter else _rows(outs[w], origin, ins[w].shape[0])

        def remote(w, k, to, origin):
            return pltpu.make_async_remote_copy(src_ref=src(w, to), dst_ref=dst(w, origin),
                                                send_sem=send_sems.at[w, k - 1], recv_sem=recv_sems.at[w, k - 1],
                                                device_id=_peer(k), device_id_type=MESH)

        pairs = [(w, k) for w in range(self.n) for k in range(1, N_DEV)]
        own = lambda: [pltpu.make_async_copy(src(w, me), dst(w, me), local_sems.at[w]) for w in range(self.n)]
        sent = lambda: [remote(w, k, _index(_peer(k)), me) for w, k in pairs]
        arriving = lambda: [remote(w, k, me, _index(_peer(k))) for w, k in pairs]
        return own, sent, arriving

    def start(self, ins, outs, sems):
        own, sent, _ = self._copies(ins, outs, sems)
        for cp in own() + sent():
            cp.start()

    def wait(self, ins, outs, sems):
        own, sent, arriving = self._copies(ins, outs, sems)
        for cp in arriving():
            cp.wait_recv()
        for cp in sent():
            cp.wait_send()
        for cp in own():
            cp.wait()


class _PairSwap:
    def __init__(self, arrays):
        self.arrays, self.n = list(arrays), len(arrays)

    def out_shape(self):
        return [jax.ShapeDtypeStruct((4, a.shape[0] // N_DEV) + a.shape[1:], a.dtype) for a in self.arrays]

    def scratch(self):
        return [pltpu.SemaphoreType.DMA((self.n, 4)), pltpu.SemaphoreType.DMA((self.n, 4))]

    def _copies(self, ins, outs, sems):
        send_sems, recv_sems = sems
        x, y, c = _place()
        return [pltpu.make_async_remote_copy(
            src_ref=_rows(ins[w], 2 * q + 1 - c, ins[w].shape[0] // N_DEV), dst_ref=outs[w].at[q],
            send_sem=send_sems.at[w, q], recv_sem=recv_sems.at[w, q],
            device_id=(x, y, 1 - c), device_id_type=MESH) for w in range(self.n) for q in range(4)]

    def start(self, ins, outs, sems):
        for cp in self._copies(ins, outs, sems):
            cp.start()

    def wait(self, ins, outs, sems):
        for cp in self._copies(ins, outs, sems):
            cp.wait()


class _ChipScatter:
    def __init__(self, arrays):
        self.arrays, self.n = list(arrays), len(arrays)

    def out_shape(self):
        return [jax.ShapeDtypeStruct(a.shape, a.dtype) for a in self.arrays]

    def scratch(self):
        return [pltpu.SemaphoreType.DMA((self.n, 3)), pltpu.SemaphoreType.DMA((self.n, 3)),
                pltpu.SemaphoreType.DMA((self.n,))]

    def _copies(self, ins, outs, sems):
        send_sems, recv_sems, local_sems = sems
        x, y, c = _place()
        mine = 2 * x + y
        sent, arriving = [], []
        for w in range(self.n):
            for k in range(1, 4):
                tx, ty = (1 - x) if k & 2 else x, (1 - y) if k & 1 else y
                other = 2 * tx + ty
                sent.append(lambda w=w, k=k, tx=tx, ty=ty, other=other: pltpu.make_async_remote_copy(
                    src_ref=ins[w].at[other], dst_ref=outs[w].at[mine], send_sem=send_sems.at[w, k - 1],
                    recv_sem=recv_sems.at[w, k - 1], device_id=(tx, ty, c), device_id_type=MESH))
                arriving.append(lambda w=w, k=k, tx=tx, ty=ty, other=other: pltpu.make_async_remote_copy(
                    src_ref=ins[w].at[mine], dst_ref=outs[w].at[other], send_sem=send_sems.at[w, k - 1],
                    recv_sem=recv_sems.at[w, k - 1], device_id=(tx, ty, c), device_id_type=MESH))
        own = [lambda w=w: pltpu.make_async_copy(ins[w].at[mine], outs[w].at[mine], local_sems.at[w])
               for w in range(self.n)]
        return own, sent, arriving

    def start(self, ins, outs, sems):
        own, sent, _ = self._copies(ins, outs, sems)
        for cp in own + sent:
            cp().start()

    def wait(self, ins, outs, sems):
        own, sent, arriving = self._copies(ins, outs, sems)
        for cp in arriving:
            cp().wait_recv()
        for cp in sent:
            cp().wait_send()
        for cp in own:
            cp().wait()


class _ChipGather:
    def __init__(self, arrays):
        self.arrays, self.n = list(arrays), len(arrays)

    def out_shape(self):
        return [jax.ShapeDtypeStruct((N_DEV * a.shape[0],) + a.shape[1:], a.dtype) for a in self.arrays]

    def scratch(self):
        return [pltpu.SemaphoreType.DMA((self.n, N_DEV - 1)), pltpu.SemaphoreType.DMA((self.n, N_DEV - 1)),
                pltpu.SemaphoreType.DMA((self.n,))]

    def _parts(self, ins, outs, sems):
        send_sems, recv_sems, local_sems = sems
        x, y, c = _place()
        me, sibling = (x, y, c), (x, y, 1 - c)
        chips = [(1 - x, y), (x, 1 - y), (1 - x, 1 - y)]

        def rows(w, place):
            return _rows(outs[w], _index(place), ins[w].shape[0])

        def copy(w, k, block, to, own=False):
            return pltpu.make_async_remote_copy(src_ref=ins[w] if own else rows(w, block), dst_ref=rows(w, block),
                                                send_sem=send_sems.at[w, k], recv_sem=recv_sems.at[w, k],
                                                device_id=to, device_id_type=MESH)

        def local(w):
            return pltpu.make_async_copy(ins[w], rows(w, me), local_sems.at[w])

        return me, sibling, chips, c, copy, local

    def start(self, ins, outs, sems):
        me, sibling, chips, c, copy, local = self._parts(ins, outs, sems)
        for w in range(self.n):
            local(w).start()
            copy(w, 0, me, sibling, own=True).start()
            for j, chip in enumerate(chips):
                copy(w, 1 + j, me, (*chip, c), own=True).start()

    def wait(self, ins, outs, sems):
        me, sibling, chips, c, copy, local = self._parts(ins, outs, sems)
        for w in range(self.n):
            for j, chip in enumerate(chips):
                copy(w, 1 + j, (*chip, c), me).wait_recv()
                copy(w, 4 + j, (*chip, c), sibling).start()
        for w in range(self.n):
            copy(w, 0, sibling, me).wait_recv()
            for j, chip in enumerate(chips):
                copy(w, 4 + j, (*chip, 1 - c), me).wait_recv()
            copy(w, 0, me, sibling, own=True).wait_send()
            for j, chip in enumerate(chips):
                copy(w, 1 + j, me, (*chip, c), own=True).wait_send()
                copy(w, 4 + j, (*chip, c), sibling).wait_send()
            local(w).wait()


def _call(body, *, name, grid, in_specs, out_specs, out_shape, scratch=(), semantics, args, exchange=None):
    if exchange is None:
        return pl.pallas_call(body, name=name, grid=grid, in_specs=in_specs, out_specs=out_specs, out_shape=out_shape,
                              scratch_shapes=list(scratch), compiler_params=_params(*semantics))(*args), None
    n_in, n_out, n_scr, nx = len(in_specs), len(out_specs), len(scratch), exchange.n

    def full_body(*refs):
        ins, refs = refs[:n_in], refs[n_in:]
        x_in, refs = refs[:nx], refs[nx:]
        outs, refs = refs[:n_out], refs[n_out:]
        x_out, refs = refs[:nx], refs[nx:]
        scr, sems = refs[:n_scr], refs[n_scr:]
        first, last = True, True
        for axis, size in enumerate(grid):
            first = jnp.logical_and(first, pl.program_id(axis) == 0)
            last = jnp.logical_and(last, pl.program_id(axis) == size - 1)
        if grid:
            pl.when(first)(lambda: exchange.start(x_in, x_out, sems))
        else:
            exchange.start(x_in, x_out, sems)
        body(*ins, *outs, *scr)
        if grid:
            pl.when(last)(lambda: exchange.wait(x_in, x_out, sems))
        else:
            exchange.wait(x_in, x_out, sems)

    hbm = pl.BlockSpec(memory_space=pltpu.HBM)
    res = pl.pallas_call(
        full_body, name=name, grid=grid,
        in_specs=list(in_specs) + [hbm] * nx, out_specs=list(out_specs) + [hbm] * nx,
        out_shape=list(out_shape) + exchange.out_shape(),
        scratch_shapes=list(scratch) + exchange.scratch(),
        compiler_params=_params(*(["arbitrary"] * len(grid))),
    )(*args, *exchange.arrays)
    return res[:n_out], res[n_out:]


def _alone(exchange, name):
    return _call(lambda: None, name=name, grid=(), in_specs=[], out_specs=[], out_shape=[], semantics=(),
                 args=(), exchange=exchange)[1]


def _mm(a, b, *, ta=False, tb=False, out_dtype, tm, tn, tk, name, exchange=None):
    m, k = (a.shape[1], a.shape[0]) if ta else a.shape
    n = b.shape[0] if tb else b.shape[1]
    assert k == (b.shape[1] if tb else b.shape[0])
    tm, tn, tk = min(tm, m), min(tn, n), min(tk, k)
    assert m % tm == 0 and n % tn == 0 and k % tk == 0, (name, m, n, k)
    nk = k // tk
    dims = (((0 if ta else 1,), (1 if tb else 0,)), ((), ()))

    def body(a_ref, b_ref, o_ref, *acc):
        prod = lax.dot_general(a_ref[...].astype(BF16), b_ref[...].astype(BF16), dims, preferred_element_type=F32)
        if nk == 1:
            o_ref[...] = prod.astype(o_ref.dtype)
            return
        acc_ref, kk = acc[0], pl.program_id(2)

        @pl.when(kk == 0)
        def _():
            acc_ref[...] = prod

        @pl.when((kk > 0) & (kk < nk - 1))
        def _():
            acc_ref[...] += prod

        @pl.when(kk == nk - 1)
        def _():
            o_ref[...] = (acc_ref[...] + prod).astype(o_ref.dtype)

    a_spec = (pl.BlockSpec((tk, tm), lambda i, j, kk: (kk, i)) if ta
              else pl.BlockSpec((tm, tk), lambda i, j, kk: (i, kk)))
    b_spec = (pl.BlockSpec((tn, tk), lambda i, j, kk: (j, kk)) if tb
              else pl.BlockSpec((tk, tn), lambda i, j, kk: (kk, j)))
    (out,), moved = _call(
        body, name=name, grid=(m // tm, n // tn, nk),
        in_specs=[a_spec, b_spec],
        out_specs=[pl.BlockSpec((tm, tn), lambda i, j, kk: (i, j))],
        out_shape=[jax.ShapeDtypeStruct((m, n), out_dtype)],
        scratch=[pltpu.VMEM((tm, tn), F32)] if nk > 1 else [],
        semantics=("parallel", "parallel", "arbitrary"), args=(a, b), exchange=exchange)
    return out if exchange is None else (out, moved)


def _mm_pieces(pieces, b, *, ta, out_dtype, tm, tn, tk, name, exchange=None):
    rows, n = pieces[0].shape[0], b.shape[1]
    step = tm if ta else tk
    assert all(p.shape[0] == rows and p.shape[1] % step == 0 for p in pieces), name
    edges = [int(e) for e in np.cumsum([0] + [p.shape[1] // step for p in pieces])]
    total = edges[-1] * step
    m, k = (total, rows) if ta else (rows, total)
    assert b.shape[0] == k and m % tm == 0 and n % tn == 0 and k % tk == 0, name
    nk, npieces = k // tk, len(pieces)
    dims = (((0 if ta else 1,), (0,)), ((), ()))

    def body(*refs):
        a_refs, (b_ref, o_ref, acc_ref) = refs[:npieces], refs[npieces:]
        kk = pl.program_id(2)
        pos = pl.program_id(0) if ta else kk

        @pl.when(kk == 0)
        def _():
            acc_ref[...] = jnp.zeros_like(acc_ref)

        for p, a_ref in enumerate(a_refs):
            @pl.when((pos >= edges[p]) & (pos < edges[p + 1]))
            def _(a_ref=a_ref):
                acc_ref[...] += lax.dot_general(a_ref[...], b_ref[...], dims, preferred_element_type=F32)

        @pl.when(kk == nk - 1)
        def _():
            o_ref[...] = acc_ref[...].astype(o_ref.dtype)

    def a_spec(p):
        lo, last = edges[p], edges[p + 1] - edges[p] - 1
        if ta:
            def index(i, j, kk):
                inside = (i >= lo) & (i <= lo + last)
                return jnp.where(inside, kk, 0), jnp.clip(i - lo, 0, last)
            return pl.BlockSpec((tk, tm), index)
        return pl.BlockSpec((tm, tk), lambda i, j, kk: (i, jnp.clip(kk - lo, 0, last)))

    (out,), moved = _call(
        body, name=name, grid=(m // tm, n // tn, nk),
        in_specs=[a_spec(p) for p in range(npieces)] + [pl.BlockSpec((tk, tn), lambda i, j, kk: (kk, j))],
        out_specs=[pl.BlockSpec((tm, tn), lambda i, j, kk: (i, j))],
        out_shape=[jax.ShapeDtypeStruct((m, n), out_dtype)],
        scratch=[pltpu.VMEM((tm, tn), F32)],
        semantics=("parallel", "parallel", "arbitrary"), args=(*pieces, b), exchange=exchange)
    return out if exchange is None else (out, moved)


def _rms_fwd(x2, g, exchange):
    t = x2.shape[0]
    tm = min(512, t)

    def body(x_ref, g_ref, o_ref):
        x = x_ref[...]
        r = lax.rsqrt(jnp.mean(x * x, axis=-1, keepdims=True) + EPS)
        o_ref[...] = (x * r * g_ref[...]).astype(o_ref.dtype)

    return _call(
        body, name="rms_in_fwd", grid=(t // tm,),
        in_specs=[pl.BlockSpec((tm, D_MODEL), lambda i: (i, 0)), pl.BlockSpec((1, D_MODEL), lambda i: (0, 0))],
        out_specs=[pl.BlockSpec((tm, D_MODEL), lambda i: (i, 0))],
        out_shape=[jax.ShapeDtypeStruct((t, D_MODEL), BF16)],
        semantics=("parallel",), args=(x2, g), exchange=exchange)


def _decay(lg):
    row = lax.broadcasted_iota(jnp.int32, (RET_TILE, RET_TILE), 0)
    col = lax.broadcasted_iota(jnp.int32, (RET_TILE, RET_TILE), 1)
    within = jnp.exp(lg * jnp.abs(row - col).astype(F32))
    inside = jnp.where((col >> 6) <= (row >> 6), within, 0.0)
    pos = lax.broadcasted_iota(jnp.int32, (RET_TILE, 1), 0).astype(F32)
    q_dec = jnp.exp(lg * (pos + 1.0))
    k_dec = jnp.exp(lg * (RET_TILE - 1.0 - pos))
    tile_dec = jnp.exp(lg * float(RET_TILE))
    return inside, q_dec, k_dec, tile_dec


def _scaled(a_bf16, dec):
    return (a_bf16.astype(F32) * dec).astype(BF16)


def _ret_fwd(proj, cs, sn, lg_arr, batch, seq, exchange):
    t = batch * seq
    nt = seq // RET_TILE

    def body(q_ref, k_ref, v_ref, rg_ref, cs_ref, sn_ref, lg_ref, gro_ref, o_ref, qr_ref, kr_ref):
        lg = lg_ref[:, 0:1]
        cs_t, sn_t = cs_ref[...], sn_ref[...]
        q = q_ref[...].astype(F32)
        k = k_ref[...].astype(F32)
        qr_ref[...] = (q * cs_t + pltpu.roll(q, 64, 1) * sn_t).astype(BF16)
        kr_ref[...] = ((k * cs_t + pltpu.roll(k, 64, 1) * sn_t) * (RET_KEY_DIM ** -0.5)).astype(BF16)
        inside, q_dec, k_dec, tile_dec = _decay(lg)
        state = jnp.zeros((RET_KEY_DIM, RET_VAL_DIM), F32)
        for i in range(nt):
            rows = slice(i * RET_TILE, (i + 1) * RET_TILE)
            qi, ki, vi = qr_ref[rows, :], kr_ref[rows, :], v_ref[rows, :]
            acc = _dot((_dot_nt(qi, ki) * inside).astype(BF16), vi)
            if i > 0:
                acc = acc + _dot(_scaled(qi, q_dec), state.astype(BF16))
            if i < nt - 1:
                state = state * tile_dec + _dot_tn(_scaled(ki, k_dec), vi)
            o_ref[rows, :] = acc
            xc = acc - jnp.mean(acc, axis=-1, keepdims=True)
            nrm = xc * lax.rsqrt(jnp.mean(xc * xc, axis=-1, keepdims=True) + EPS)
            rg = rg_ref[rows, :].astype(F32)
            gro_ref[rows, :] = (rg * _sigmoid(rg) * nrm).astype(BF16)

    def col(base, width):
        return lambda b, h: (b, base // width + h)

    return _call(
        body, name="ret_fwd", grid=(batch, RET_HEADS),
        in_specs=[pl.BlockSpec((seq, RET_KEY_DIM), col(C_RQ, RET_KEY_DIM)),
                  pl.BlockSpec((seq, RET_KEY_DIM), col(C_RK, RET_KEY_DIM)),
                  pl.BlockSpec((seq, RET_VAL_DIM), col(C_RV, RET_VAL_DIM)),
                  pl.BlockSpec((seq, RET_VAL_DIM), col(C_RG, RET_VAL_DIM)),
                  pl.BlockSpec((seq, RET_KEY_DIM), lambda b, h: (0, 0)),
                  pl.BlockSpec((seq, RET_KEY_DIM), lambda b, h: (0, 0)),
                  pl.BlockSpec((None, 1, LANES), lambda b, h: (h, 0, 0))],
        out_specs=[pl.BlockSpec((seq, RET_VAL_DIM), lambda b, h: (b, h)),
                   pl.BlockSpec((seq, RET_VAL_DIM), lambda b, h: (b, h)),
                   pl.BlockSpec((seq, RET_KEY_DIM), lambda b, h: (b, h)),
                   pl.BlockSpec((seq, RET_KEY_DIM), lambda b, h: (b, h))],
        out_shape=[jax.ShapeDtypeStruct((t, RET_HEADS * RET_VAL_DIM), BF16),
                   jax.ShapeDtypeStruct((t, RET_HEADS * RET_VAL_DIM), F32),
                   jax.ShapeDtypeStruct((t, RET_HEADS * RET_KEY_DIM), BF16),
                   jax.ShapeDtypeStruct((t, RET_HEADS * RET_KEY_DIM), BF16)],
        semantics=("parallel", "parallel"), args=(proj, proj, proj, proj, cs, sn, lg_arr), exchange=exchange)


def _att_bias(w_ref, bias_ref):
    n_i = lax.broadcasted_iota(jnp.int32, (ATT_Q, BIAS_LEN), 0)
    qc = lax.broadcasted_iota(jnp.int32, (ATT_Q, ATT_WIN), 0) >> 6
    kc = lax.broadcasted_iota(jnp.int32, (ATT_Q, ATT_WIN), 1) >> 6
    dc = qc + BAND_CHUNKS - kc
    band = (dc >= 0) & (dc <= BAND_CHUNKS)
    key = lax.broadcasted_iota(jnp.int32, (ATT_Q, ATT_WIN), 1)
    for e in range(2):
        xw = jnp.broadcast_to(w_ref[e:e + 1, :], (ATT_Q, BIAS_LEN))
        for bit in range(8):
            xw = jnp.where(((n_i >> bit) & 1) == 1, pltpu.roll(xw, 1 << bit, 1), xw)
        bias = jnp.where(band, xw[:, BIAS_LEN - ATT_WIN:], NEG_INF)
        for first in range(ATT_STARTS):
            bias_ref[first, e] = jnp.where(key + (first * ATT_Q - ATT_PAD) >= 0, bias, NEG_INF)
        bias_ref[ATT_STARTS, e] = bias


def _att_specs(batch, seq):
    ni = seq // ATT_Q
    q_spec = pl.BlockSpec((ATT_Q, LANES), lambda hp, b, i: (b * ni + i, C_AQ // LANES + hp))
    k_spec = pl.BlockSpec((seq, LANES), lambda hp, b, i: (b, C_AK // LANES + hp))
    v_spec = pl.BlockSpec((seq, LANES), lambda hp, b, i: (b, C_AV // LANES + hp))
    w_spec = pl.BlockSpec((None, 2, BIAS_LEN), lambda hp, b, i: (hp, 0, 0))
    b_spec = pl.BlockSpec((None, ATT_STARTS + 1, 2, ATT_Q, ATT_WIN), lambda hp, b, i: (hp, 0, 0, 0, 0))
    pad = pltpu.VMEM((seq + ATT_PAD, LANES), BF16)
    return ni, q_spec, k_spec, v_spec, w_spec, b_spec, pad


def _att_bias_tiles(wvec):
    return pl.pallas_call(
        _att_bias, name="att_bias", grid=(ATT_HEADS // 2,),
        in_specs=[pl.BlockSpec((None, 2, BIAS_LEN), lambda hp: (hp, 0, 0))],
        out_specs=pl.BlockSpec((None, ATT_STARTS + 1, 2, ATT_Q, ATT_WIN), lambda hp: (hp, 0, 0, 0, 0)),
        out_shape=jax.ShapeDtypeStruct((ATT_HEADS // 2, ATT_STARTS + 1, 2, ATT_Q, ATT_WIN), F32),
        compiler_params=_params("parallel"),
    )(wvec)


def _att_pad(src_ref, pad_ref):
    pad_ref[:ATT_PAD, :] = jnp.zeros((ATT_PAD, LANES), BF16)
    pad_ref[ATT_PAD:, :] = src_ref[...]


def _att_head(q2, sel):
    return jnp.where(sel, q2, jnp.zeros_like(q2)) * 0.125


def _att_softmax_rows(s_ref, bias_ref, rows):
    s = s_ref[rows, :] + bias_ref[rows, :]
    ex = jnp.exp(s - jnp.max(s, axis=-1, keepdims=True))
    return ex, 1.0 / jnp.sum(ex, axis=-1, keepdims=True)


def _att_fwd(proj, bias, batch, seq, exchange):
    ni, q_spec, k_spec, v_spec, _, b_spec, pad = _att_specs(batch, seq)

    def body(q_ref, k_ref, v_ref, bias_ref, o_ref, kp_ref, vp_ref, s_ref, e_ref):
        i = pl.program_id(2)

        @pl.when(i == 0)
        def _():
            _att_pad(k_ref, kp_ref)
            _att_pad(v_ref, vp_ref)

        win = pl.ds(pl.multiple_of(i * ATT_Q, ATT_Q), ATT_WIN)
        k2, v2, q2 = kp_ref[win, :], vp_ref[win, :], q_ref[...]
        lo = lax.broadcasted_iota(jnp.int32, (1, LANES), 1) < 64
        start = jnp.minimum(i, ATT_STARTS)
        out = jnp.zeros((ATT_Q, LANES), F32)
        for e in range(2):
            sel = lo if e == 0 else jnp.logical_not(lo)
            s_ref[e] = _dot_nt(_att_head(q2, sel), k2)
            rsum = []
            for c in range(ATT_Q // ATT_ROWS):
                rows = slice(c * ATT_ROWS, (c + 1) * ATT_ROWS)
                ex, r = _att_softmax_rows(s_ref.at[e], bias_ref.at[start, e], rows)
                e_ref[e, rows, :] = ex.astype(BF16)
                rsum.append(r)
            out = out + _dot(e_ref[e], jnp.where(sel, v2, jnp.zeros_like(v2))) * jnp.concatenate(rsum, axis=0)
        o_ref[...] = out.astype(BF16)

    return _call(
        body, name="att_fwd", grid=(ATT_HEADS // 2, batch, ni),
        in_specs=[q_spec, k_spec, v_spec, b_spec],
        out_specs=[pl.BlockSpec((ATT_Q, LANES), lambda hp, b, i: (b * ni + i, hp))],
        out_shape=[jax.ShapeDtypeStruct((batch * seq, ATT_HEADS * 64), BF16)],
        scratch=[pad, pad, pltpu.VMEM((2, ATT_Q, ATT_WIN), F32), pltpu.VMEM((2, ATT_Q, ATT_WIN), BF16)],
        semantics=("arbitrary", "arbitrary", "arbitrary"), args=(proj, proj, proj, bias), exchange=exchange)


def _mix_fwd(gro, ao, proj, b_gate, w_ret, w_att_t):
    t = gro.shape[0]
    tm, tn = min(256, t), 512

    def body(gro_ref, ao_ref, glr_ref, gla_ref, br_ref, ba_ref, wr_ref, wa_ref, z_ref, yr_ref, ya_ref):
        yr = _dot(gro_ref[...], wr_ref[...])
        ya = _dot_nt(ao_ref[...], wa_ref[...])
        gr = _sigmoid(glr_ref[...].astype(F32) + br_ref[...])
        ga = _sigmoid(gla_ref[...].astype(F32) + ba_ref[...])
        z_ref[...] = (gr * yr + ga * ya).astype(BF16)
        yr_ref[...] = yr.astype(BF16)
        ya_ref[...] = ya.astype(BF16)

    nb = D_MODEL // tn
    out = pl.BlockSpec((tm, tn), lambda i, j: (i, j))
    return pl.pallas_call(
        body, name="mix_fwd", grid=(t // tm, nb),
        in_specs=[pl.BlockSpec((tm, D_MODEL), lambda i, j: (i, 0)),
                  pl.BlockSpec((tm, 512), lambda i, j: (i, 0)),
                  pl.BlockSpec((tm, tn), lambda i, j: (i, C_GL // tn + j)),
                  pl.BlockSpec((tm, tn), lambda i, j: (i, C_GL // tn + nb + j)),
                  pl.BlockSpec((1, tn), lambda i, j: (0, j)),
                  pl.BlockSpec((1, tn), lambda i, j: (0, nb + j)),
                  pl.BlockSpec((D_MODEL, tn), lambda i, j: (0, j)),
                  pl.BlockSpec((tn, 512), lambda i, j: (j, 0))],
        out_specs=[out, out, out],
        out_shape=[jax.ShapeDtypeStruct((t, D_MODEL), BF16)] * 3,
        compiler_params=_params("parallel", "parallel"),
    )(gro, ao, proj, proj, b_gate, b_gate, w_ret, w_att_t)


def _out_fwd(z, x2, w_out, g2):
    t = z.shape[0]
    tm = min(256, t)

    def body(z_ref, x_ref, w_ref, g_ref, h_ref, hn_ref):
        h = x_ref[...] + _dot(z_ref[...], w_ref[...])
        h_ref[...] = h
        r = lax.rsqrt(jnp.mean(h * h, axis=-1, keepdims=True) + EPS)
        hn_ref[...] = (h * r * g_ref[...]).astype(BF16)

    row = pl.BlockSpec((tm, D_MODEL), lambda i: (i, 0))
    return pl.pallas_call(
        body, name="out_fwd", grid=(t // tm,),
        in_specs=[row, row, pl.BlockSpec((D_MODEL, D_MODEL), lambda i: (0, 0)),
                  pl.BlockSpec((1, D_MODEL), lambda i: (0, 0))],
        out_specs=[row, row],
        out_shape=[jax.ShapeDtypeStruct((t, D_MODEL), F32), jax.ShapeDtypeStruct((t, D_MODEL), BF16)],
        compiler_params=_params("parallel"),
    )(z, x2, w_out, g2)


def _ffn_up(hn, wg_t, wu_t):
    t = hn.shape[0]
    tm, tn = min(512, t), D_FF // 2

    def body(h_ref, wg_ref, wu_ref, g_ref, u_ref, a_ref):
        g = _dot_nt(h_ref[...], wg_ref[...])
        u = _dot_nt(h_ref[...], wu_ref[...])
        g_ref[...] = g.astype(BF16)
        u_ref[...] = u.astype(BF16)
        a_ref[...] = (g * _sigmoid(g) * u).astype(BF16)

    w_spec = pl.BlockSpec((tn, D_MODEL), lambda j, i: (j, 0))
    out = pl.BlockSpec((tm, tn), lambda j, i: (i, j))
    return pl.pallas_call(
        body, name="ffn_up", grid=(D_FF // tn, t // tm),
        in_specs=[pl.BlockSpec((tm, D_MODEL), lambda j, i: (i, 0)), w_spec, w_spec],
        out_specs=[out, out, out],
        out_shape=[jax.ShapeDtypeStruct((t, D_FF), BF16)] * 3,
        compiler_params=_params("parallel", "parallel"),
    )(hn, wg_t, wu_t)


def _ffn_down_loss(a, h1, tgt, w_down, g3):
    t = a.shape[0]
    tm = min(512, t)

    def body(a_ref, h_ref, t_ref, w_ref, g_ref, dh_ref, dhb_ref, loss_ref, dg_ref):
        @pl.when(pl.program_id(0) == 0)
        def _():
            loss_ref[...] = jnp.zeros_like(loss_ref)
            dg_ref[...] = jnp.zeros_like(dg_ref)

        g = g_ref[...]
        h2 = h_ref[...] + _dot(a_ref[...], w_ref[...])
        r = lax.rsqrt(jnp.mean(h2 * h2, axis=-1, keepdims=True) + EPS)
        err = h2 * r * g - t_ref[...]
        loss_ref[...] += jnp.sum(err * err) * (0.5 / D_MODEL)
        dy = err * (1.0 / D_MODEL)
        dh, dg_rows = _rms_bwd(h2, g, dy)
        dg_ref[...] += jnp.sum(dg_rows, axis=0, keepdims=True)
        dh_ref[...] = dh
        dhb_ref[...] = dh.astype(BF16)

    row = pl.BlockSpec((tm, D_MODEL), lambda i: (i, 0))
    vec = pl.BlockSpec((1, D_MODEL), lambda i: (0, 0))
    return pl.pallas_call(
        body, name="ffn_down_loss", grid=(t // tm,),
        in_specs=[pl.BlockSpec((tm, D_FF), lambda i: (i, 0)), row, row,
                  pl.BlockSpec((D_FF, D_MODEL), lambda i: (0, 0)), vec],
        out_specs=[row, row, pl.BlockSpec((1, LANES), lambda i: (0, 0)), vec],
        out_shape=[jax.ShapeDtypeStruct((t, D_MODEL), F32), jax.ShapeDtypeStruct((t, D_MODEL), BF16),
                   jax.ShapeDtypeStruct((1, LANES), F32), jax.ShapeDtypeStruct((1, D_MODEL), F32)],
        compiler_params=_params("arbitrary"),
    )(a, h1, tgt, w_down, g3)


def _ffn_bwd_act(dh2b, w_down, g_act, u_act, exchange):
    t = dh2b.shape[0]
    tm, tn = min(512, t), D_FF // 2

    def body(d_ref, w_ref, g_ref, u_ref, dg_ref, du_ref):
        da = _dot_nt(d_ref[...], w_ref[...])
        g = g_ref[...].astype(F32)
        u = u_ref[...].astype(F32)
        sg = _sigmoid(g)
        dg_ref[...] = (da * u * sg * (1.0 + g * (1.0 - sg))).astype(BF16)
        du_ref[...] = (da * g * sg).astype(BF16)

    blk = pl.BlockSpec((tm, tn), lambda j, i: (i, j))
    return _call(
        body, name="ffn_bwd_act", grid=(D_FF // tn, t // tm),
        in_specs=[pl.BlockSpec((tm, D_MODEL), lambda j, i: (i, 0)),
                  pl.BlockSpec((tn, D_MODEL), lambda j, i: (j, 0)), blk, blk],
        out_specs=[blk, blk],
        out_shape=[jax.ShapeDtypeStruct((t, D_FF), BF16)] * 2,
        semantics=("parallel", "parallel"), args=(dh2b, w_down, g_act, u_act), exchange=exchange)


def _ffn_bwd_in(dg, du, wg_t, wu_t, h1, dh2, g2, exchange):
    t = dg.shape[0]
    tm, tk = min(512, t), D_FF // 2
    nk = D_FF // tk

    def body(dg_ref, du_ref, wg_ref, wu_ref, h_ref, d2_ref, g_ref, dh_ref, dhb_ref, gn_ref, acc_ref):
        i, kk = pl.program_id(0), pl.program_id(1)

        @pl.when((i == 0) & (kk == 0))
        def _():
            gn_ref[...] = jnp.zeros_like(gn_ref)

        @pl.when(kk == 0)
        def _():
            acc_ref[...] = jnp.zeros_like(acc_ref)

        acc_ref[...] += _dot(dg_ref[...], wg_ref[...]) + _dot(du_ref[...], wu_ref[...])

        @pl.when(kk == nk - 1)
        def _():
            dx, dg_rows = _rms_bwd(h_ref[...], g_ref[...], acc_ref[...])
            dh = d2_ref[...] + dx
            dh_ref[...] = dh
            dhb_ref[...] = dh.astype(BF16)
            gn_ref[...] += jnp.sum(dg_rows, axis=0, keepdims=True)

    act = pl.BlockSpec((tm, tk), lambda i, kk: (i, kk))
    wsp = pl.BlockSpec((tk, D_MODEL), lambda i, kk: (kk, 0))
    row = pl.BlockSpec((tm, D_MODEL), lambda i, kk: (i, 0))
    vec = pl.BlockSpec((1, D_MODEL), lambda i, kk: (0, 0))
    return _call(
        body, name="ffn_bwd_in", grid=(t // tm, nk),
        in_specs=[act, act, wsp, wsp, row, row, vec],
        out_specs=[row, row, vec],
        out_shape=[jax.ShapeDtypeStruct((t, D_MODEL), F32), jax.ShapeDtypeStruct((t, D_MODEL), BF16),
                   jax.ShapeDtypeStruct((1, D_MODEL), F32)],
        scratch=[pltpu.VMEM((tm, D_MODEL), F32)],
        semantics=("arbitrary", "arbitrary"), args=(dg, du, wg_t, wu_t, h1, dh2, g2), exchange=exchange)


def _mix_bwd(dh1b, w_out, proj, b_gate, y_ret, y_att, exchange):
    t = dh1b.shape[0]
    tm, tn = min(256, t), 512
    nb = D_MODEL // tn

    def body(d_ref, w_ref, glr_ref, gla_ref, br_ref, ba_ref, yr_ref, ya_ref,
             dyr_ref, dya_ref, dglr_ref, dgla_ref, dbr_ref, dba_ref):
        @pl.when(pl.program_id(1) == 0)
        def _():
            dbr_ref[...] = jnp.zeros_like(dbr_ref)
            dba_ref[...] = jnp.zeros_like(dba_ref)

        dz = _dot_nt(d_ref[...], w_ref[...])
        gr = _sigmoid(glr_ref[...].astype(F32) + br_ref[...])
        ga = _sigmoid(gla_ref[...].astype(F32) + ba_ref[...])
        dyr_ref[...] = (dz * gr).astype(BF16)
        dya_ref[...] = (dz * ga).astype(BF16)
        dglr = dz * yr_ref[...].astype(F32) * gr * (1.0 - gr)
        dgla = dz * ya_ref[...].astype(F32) * ga * (1.0 - ga)
        dglr_ref[...] = dglr.astype(BF16)
        dgla_ref[...] = dgla.astype(BF16)
        dbr_ref[...] += jnp.sum(dglr, axis=0, keepdims=True)
        dba_ref[...] += jnp.sum(dgla, axis=0, keepdims=True)

    blk = pl.BlockSpec((tm, tn), lambda j, i: (i, j))
    vec = pl.BlockSpec((1, tn), lambda j, i: (0, j))
    return _call(
        body, name="mix_bwd", grid=(nb, t // tm),
        in_specs=[pl.BlockSpec((tm, D_MODEL), lambda j, i: (i, 0)),
                  pl.BlockSpec((tn, D_MODEL), lambda j, i: (j, 0)),
                  pl.BlockSpec((tm, tn), lambda j, i: (i, C_GL // tn + j)),
                  pl.BlockSpec((tm, tn), lambda j, i: (i, C_GL // tn + nb + j)),
                  vec, pl.BlockSpec((1, tn), lambda j, i: (0, nb + j)), blk, blk],
        out_specs=[blk, blk, blk, blk, vec, vec],
        out_shape=[jax.ShapeDtypeStruct((t, D_MODEL), BF16)] * 4 + [jax.ShapeDtypeStruct((1, D_MODEL), F32)] * 2,
        semantics=("arbitrary", "arbitrary"), args=(dh1b, w_out, proj, proj, b_gate, b_gate, y_ret, y_att),
        exchange=exchange)


def _ret_bwd(dgro, proj, o_ret, qr, kr, cs, sn, lg_arr, batch, seq, exchange):
    t = batch * seq
    nt = seq // RET_TILE

    def body(dgro_ref, rg_ref, o_ref, qr_ref, kr_ref, v_ref, cs_ref, sn_ref, lg_ref,
             dq_ref, dk_ref, dv_ref, drg_ref, do_ref, st_ref):
        lg = lg_ref[:, 0:1]
        inside, q_dec, k_dec, tile_dec = _decay(lg)

        state = jnp.zeros((RET_KEY_DIM, RET_VAL_DIM), F32)
        for i in range(nt - 1):
            rows = slice(i * RET_TILE, (i + 1) * RET_TILE)
            state = state * tile_dec + _dot_tn(_scaled(kr_ref[rows, :], k_dec), v_ref[rows, :])
            st_ref[i + 1] = state.astype(BF16)

        for i in range(nt):
            rows = slice(i * RET_TILE, (i + 1) * RET_TILE)
            o = o_ref[rows, :]
            xc = o - jnp.mean(o, axis=-1, keepdims=True)
            rs = lax.rsqrt(jnp.mean(xc * xc, axis=-1, keepdims=True) + EPS)
            nrm = xc * rs
            rg = rg_ref[rows, :].astype(F32)
            sg = _sigmoid(rg)
            dg = dgro_ref[rows, :].astype(F32)
            drg_ref[rows, :] = (dg * nrm * sg * (1.0 + rg * (1.0 - sg))).astype(BF16)
            dn = dg * rg * sg
            do = rs * (dn - jnp.mean(dn, axis=-1, keepdims=True)
                       - nrm * jnp.mean(dn * nrm, axis=-1, keepdims=True))
            do_ref[rows, :] = do.astype(BF16)

        dstate = jnp.zeros((RET_KEY_DIM, RET_VAL_DIM), F32)
        for i in reversed(range(nt)):
            rows = slice(i * RET_TILE, (i + 1) * RET_TILE)
            qi, ki, vi, doi = qr_ref[rows, :], kr_ref[rows, :], v_ref[rows, :], do_ref[rows, :]
            p = (_dot_nt(qi, ki) * inside).astype(BF16)
            dp = (_dot_nt(doi, vi) * inside).astype(BF16)
            dq = _dot(dp, ki)
            dk = _dot_tn(dp, qi)
            dv = _dot_tn(p, doi)
            if i > 0:
                dq = dq + _dot_nt(doi, st_ref[i]) * q_dec
            if i < nt - 1:
                dsb = dstate.astype(BF16)
                dk = dk + _dot_nt(vi, dsb) * k_dec
                dv = dv + _dot(_scaled(ki, k_dec), dsb)
            if i > 0:
                dstate = dstate * tile_dec + _dot_tn(_scaled(qi, q_dec), doi)
            dq_ref[rows, :] = (dq * cs_ref[rows, :] - pltpu.roll(dq, 64, 1) * sn_ref[rows, :]).astype(BF16)
            dk = (dk * cs_ref[rows, :] - pltpu.roll(dk, 64, 1) * sn_ref[rows, :]) * (RET_KEY_DIM ** -0.5)
            dk_ref[rows, :] = dk.astype(BF16)
            dv_ref[rows, :] = dv.astype(BF16)

    key = pl.BlockSpec((seq, RET_KEY_DIM), lambda b, h: (b, h))
    val = pl.BlockSpec((seq, RET_VAL_DIM), lambda b, h: (b, h))
    tab = pl.BlockSpec((seq, RET_KEY_DIM), lambda b, h: (0, 0))
    return _call(
        body, name="ret_bwd", grid=(batch, RET_HEADS),
        in_specs=[val, pl.BlockSpec((seq, RET_VAL_DIM), lambda b, h: (b, C_RG // RET_VAL_DIM + h)), val, key, key,
                  pl.BlockSpec((seq, RET_VAL_DIM), lambda b, h: (b, C_RV // RET_VAL_DIM + h)), tab, tab,
                  pl.BlockSpec((None, 1, LANES), lambda b, h: (h, 0, 0))],
        out_specs=[key, key, val, val],
        out_shape=[jax.ShapeDtypeStruct((t, RET_HEADS * RET_KEY_DIM), BF16)] * 2
                  + [jax.ShapeDtypeStruct((t, RET_HEADS * RET_VAL_DIM), BF16)] * 2,
        scratch=[pltpu.VMEM((seq, RET_VAL_DIM), BF16), pltpu.VMEM((nt, RET_KEY_DIM, RET_VAL_DIM), BF16)],
        semantics=("parallel", "parallel"), args=(dgro, proj, o_ret, qr, kr, proj, cs, sn, lg_arr),
        exchange=exchange)


def _att_bwd(proj, bias, dao, batch, seq, exchange):
    ni, q_spec, k_spec, v_spec, w_spec, b_spec, pad = _att_specs(batch, seq)
    t = batch * seq

    def body(q_ref, k_ref, v_ref, bias_ref, do_ref, dq_ref, dk_ref, dv_ref, dw_ref,
             dbias_ref, dk_acc, dv_acc, kp_ref, vp_ref, s_ref, dp_ref, e_ref, ds_ref):
        b, i = pl.program_id(1), pl.program_id(2)

        @pl.when((b == 0) & (i == 0))
        def _():
            dbias_ref[...] = jnp.zeros_like(dbias_ref)

        @pl.when(i == 0)
        def _():
            _att_pad(k_ref, kp_ref)
            _att_pad(v_ref, vp_ref)
            dk_acc[...] = jnp.zeros_like(dk_acc)
            dv_acc[...] = jnp.zeros_like(dv_acc)

        win = pl.ds(pl.multiple_of(i * ATT_Q, ATT_Q), ATT_WIN)
        k2, v2, q2, do2 = kp_ref[win, :], vp_ref[win, :], q_ref[...], do_ref[...]
        lo = lax.broadcasted_iota(jnp.int32, (1, LANES), 1) < 64
        dq = jnp.zeros((ATT_Q, LANES), F32)
        dk = jnp.zeros((LANES, ATT_WIN), F32)
        dv = jnp.zeros((LANES, ATT_WIN), F32)
        start = jnp.minimum(i, ATT_STARTS)
        for e in range(2):
            sel = lo if e == 0 else jnp.logical_not(lo)
            qm = _att_head(q2, sel)
            dom = jnp.where(sel, do2, jnp.zeros_like(do2))
            s_ref[e] = _dot_nt(qm, k2)
            dp_ref[e] = _dot_nt(dom, v2)
            rsum = []
            for c in range(ATT_Q // ATT_ROWS):
                rows = slice(c * ATT_ROWS, (c + 1) * ATT_ROWS)
                ex, r = _att_softmax_rows(s_ref.at[e], bias_ref.at[start, e], rows)
                dp = dp_ref[e, rows, :]
                mean = jnp.sum(dp * ex, axis=-1, keepdims=True) * r
                ds = ex * ((dp - mean) * r)
                dbias_ref[e, rows, :] += ds
                ds_ref[e, rows, :] = ds.astype(BF16)
                e_ref[e, rows, :] = ex.astype(BF16)
                rsum.append(r)
            dq = dq + _dot(ds_ref[e], jnp.where(sel, k2, jnp.zeros_like(k2)))
            dk = dk + _dot_tn(qm, ds_ref[e])
            dv = dv + _dot_tn((dom.astype(F32) * jnp.concatenate(rsum, axis=0)).astype(BF16), e_ref[e])
        dq_ref[...] = (dq * 0.125).astype(BF16)
        dk_acc[:, win] += dk
        dv_acc[:, win] += dv

        @pl.when(i == ni - 1)
        def _():
            dk_ref[...] = dk_acc[:, ATT_PAD:].T.astype(BF16)
            dv_ref[...] = dv_acc[:, ATT_PAD:].T.astype(BF16)

        @pl.when((b == batch - 1) & (i == ni - 1))
        def _():
            n_i = lax.broadcasted_iota(jnp.int32, (ATT_Q, BIAS_LEN), 0)
            for e in range(2):
                xw = jnp.concatenate([jnp.zeros((ATT_Q, BIAS_LEN - ATT_WIN), F32), dbias_ref[e]], axis=1)
                for bit in range(8):
                    xw = jnp.where(((n_i >> bit) & 1) == 1, pltpu.roll(xw, BIAS_LEN - (1 << bit), 1), xw)
                dw_ref[e:e + 1, :] = jnp.sum(xw, axis=0, keepdims=True)

    seq_blk = pl.BlockSpec((seq, LANES), lambda hp, b, i: (b, hp))
    q_out = pl.BlockSpec((ATT_Q, LANES), lambda hp, b, i: (b * ni + i, hp))
    return _call(
        body, name="att_bwd", grid=(ATT_HEADS // 2, batch, ni),
        in_specs=[q_spec, k_spec, v_spec, b_spec, q_out],
        out_specs=[q_out, seq_blk, seq_blk, w_spec],
        out_shape=[jax.ShapeDtypeStruct((t, 512), BF16)] * 3
                  + [jax.ShapeDtypeStruct((ATT_HEADS // 2, 2, BIAS_LEN), F32)],
        scratch=[pltpu.VMEM((2, ATT_Q, ATT_WIN), F32),
                 pltpu.VMEM((LANES, seq + ATT_PAD), F32), pltpu.VMEM((LANES, seq + ATT_PAD), F32), pad, pad,
                 pltpu.VMEM((2, ATT_Q, ATT_WIN), F32), pltpu.VMEM((2, ATT_Q, ATT_WIN), F32),
                 pltpu.VMEM((2, ATT_Q, ATT_WIN), BF16), pltpu.VMEM((2, ATT_Q, ATT_WIN), BF16)],
        semantics=("arbitrary", "arbitrary", "arbitrary"), args=(proj, proj, proj, bias, dao), exchange=exchange)


def _rms_in_bwd(x2, dxn, dh1, g1):
    t = x2.shape[0]
    tm = min(512, t)

    def body(x_ref, d_ref, h_ref, g_ref, dx_ref, dg_ref):
        @pl.when(pl.program_id(0) == 0)
        def _():
            dg_ref[...] = jnp.zeros_like(dg_ref)

        dx, dg_rows = _rms_bwd(x_ref[...], g_ref[...], d_ref[...])
        dx_ref[...] = h_ref[...] + dx
        dg_ref[...] += jnp.sum(dg_rows, axis=0, keepdims=True)

    row = pl.BlockSpec((tm, D_MODEL), lambda i: (i, 0))
    vec = pl.BlockSpec((1, D_MODEL), lambda i: (0, 0))
    return pl.pallas_call(
        body, name="rms_in_bwd", grid=(t // tm,),
        in_specs=[row, row, row, vec], out_specs=[row, vec],
        out_shape=[jax.ShapeDtypeStruct((t, D_MODEL), F32), jax.ShapeDtypeStruct((1, D_MODEL), F32)],
        compiler_params=_params("arbitrary"),
    )(x2, dxn, dh1, g1)


def _pack_small(dg1, dbr, dba, dg2, dg3, dw, loss):
    def body(a_ref, b_ref, c_ref, d_ref, e_ref, w_ref, l_ref, o_ref):
        o_ref[...] = jnp.zeros_like(o_ref)
        for r, ref in enumerate((a_ref, b_ref, c_ref, d_ref, e_ref)):
            o_ref[r:r + 1, :] = ref[...]
        o_ref[5:6, 0:LANES] = l_ref[...]
        for hp in range(ATT_HEADS // 2):
            o_ref[8 + 2 * hp:10 + 2 * hp, :] = w_ref[hp]

    return pl.pallas_call(body, name="pack_small",
                          out_shape=jax.ShapeDtypeStruct((16, D_MODEL), F32))(dg1, dbr, dba, dg2, dg3, dw, loss)


def _rotary_tables(seq):
    freqs = ROPE_BASE ** (-jnp.arange(0, RET_KEY_DIM, 2, dtype=F32) / RET_KEY_DIM)
    ang = jnp.arange(seq, dtype=F32)[:, None] * freqs[None, :]
    cos, sin = jnp.cos(ang), jnp.sin(ang)
    return jnp.concatenate([cos, cos], axis=1), jnp.concatenate([-sin, sin], axis=1)


def _bias_rows(rel_bias):
    n_far = BIAS_LEN - ATT_Q - MAX_REL + 1
    n_near = BIAS_LEN - n_far - (N_REL - 2)
    w = jnp.concatenate([jnp.broadcast_to(rel_bias[:, N_REL - 1:], (ATT_HEADS, n_far)),
                         rel_bias[:, 1:N_REL - 1][:, ::-1],
                         jnp.broadcast_to(rel_bias[:, :1], (ATT_HEADS, n_near))], axis=1)
    return w.reshape(ATT_HEADS // 2, 2, BIAS_LEN)


def _bias_rows_bwd(dw):
    n_far = BIAS_LEN - ATT_Q - MAX_REL + 1
    mid = dw[:, n_far:n_far + N_REL - 2][:, ::-1]
    return jnp.concatenate([jnp.sum(dw[:, n_far + N_REL - 2:], axis=1, keepdims=True), mid,
                            jnp.sum(dw[:, :n_far], axis=1, keepdims=True)], axis=1)


def _step(x, tgt, norm_mix, b_gate, norm_ffn, norm_final, rel_bias_shard, shard):
    batch, seq, _ = x.shape
    t = batch * seq
    n_rb = rel_bias_shard.shape[-1]
    x2, tgt2 = x.reshape(t, D_MODEL), tgt.reshape(t, D_MODEL)
    g3 = norm_final.reshape(1, D_MODEL)
    cs, sn = _rotary_tables(seq)
    lg = np.log(1.0 - 2.0 ** (-5.0 - np.arange(RET_HEADS, dtype=np.float32))).astype(np.float32)
    lg_arr = jnp.asarray(np.broadcast_to(lg[:, None, None], (RET_HEADS, 1, LANES)))

    def gather(*names):
        return _ChipGather([shard[nm] for nm in names])

    def scatter(*grads):
        return _Exchange(grads, scatter=True)

    rb_pad = jnp.pad(rel_bias_shard, ((0, 0), (0, LANES - n_rb)))
    (xn,), (w_in_t, rb_full) = _rms_fwd(x2, norm_mix, _ChipGather([shard["w_in_t"], rb_pad]))
    rb_full = rb_full.reshape(N_DEV, ATT_HEADS, LANES)[:, :, :n_rb]
    bias = _att_bias_tiles(_bias_rows(jnp.transpose(rb_full, (1, 0, 2)).reshape(ATT_HEADS, N_DEV * n_rb)))
    proj, (w_ret, w_att_t, w_out, w_gate_t) = _mm(
        xn, w_in_t, tb=True, out_dtype=BF16, tm=1024, tn=1664, tk=1024, name="proj",
        exchange=gather("w_ret", "w_att_t", "w_out", "w_gate_t"))
    (gro, o_ret, qr, kr), (w_up_t,) = _ret_fwd(proj, cs, sn, lg_arr, batch, seq, gather("w_up_t"))
    (ao,), (w_down,) = _att_fwd(proj, bias, batch, seq, gather("w_down"))
    z, y_ret, y_att = _mix_fwd(gro, ao, proj, b_gate, w_ret, w_att_t)
    h1, hn = _out_fwd(z, x2, w_out, norm_ffn)
    g_act, u_act, a_act = _ffn_up(hn, w_gate_t, w_up_t)
    dh2, dh2b, loss, dg3 = _ffn_down_loss(a_act, h1, tgt2, w_down, g3)

    wg = dict(out_dtype=BF16, tn=1024, ta=True)
    slots = {}
    dw_down = _mm(a_act, dh2b, tm=1408, tk=1024, name="dw_down", **wg)
    (d_gact, d_uact), (slots["w_down"],) = _ffn_bwd_act(dh2b, w_down, g_act, u_act, scatter(dw_down))
    dw_gate = _mm(d_gact, hn, tm=1408, tk=1024, name="dw_gate", **wg)
    dw_up = _mm(d_uact, hn, tm=1408, tk=1024, name="dw_up", **wg)
    (dh1, dh1b, dg2), (slots["w_gate_t"],) = _ffn_bwd_in(d_gact, d_uact, w_gate_t, w_up_t, h1, dh2, norm_ffn,
                                                       scatter(dw_gate))
    dw_out = _mm(z, dh1b, tm=1024, tk=2048, name="dw_out", **wg)
    (dyr, dya, dglr, dgla, dbr, dba), _ = _mix_bwd(dh1b, w_out, proj, b_gate, y_ret, y_att, None)
    dgro = _mm(dyr, w_ret, tb=True, out_dtype=BF16, tm=1024, tn=1024, tk=1024, name="dgro")
    dao = _mm(dya, w_att_t, out_dtype=BF16, tm=1024, tn=512, tk=1024, name="dao")
    dw_ret = _mm(gro, dyr, tm=1024, tk=2048, name="dw_ret", **wg)
    dw_att = _mm(dya, ao, tm=1024, tk=2048, name="dw_att", **wg)
    (drq, drk, drv, drg), (slots["w_up_t"],) = _ret_bwd(
        dgro, proj, o_ret, qr, kr, cs, sn, lg_arr, batch, seq, scatter(dw_up))
    (daq, dak, dav, dw), (slots["w_out"], slots["w_ret"], slots["w_att_t"]) = _att_bwd(
        proj, bias, dao, batch, seq, scatter(dw_out, dw_ret, dw_att))
    dproj = [drq, drk, drv, drg, daq, dak, dav, dglr, dgla]
    dw_in = _mm_pieces(dproj, xn, ta=True, out_dtype=BF16, tm=512, tn=1024, tk=1024, name="dw_in")
    (dw_in_sibling,) = _alone(_PairSwap([dw_in]), "swap_w_in")
    dw_in_pairs = _pair_add(dw_in, dw_in_sibling, "pair_w_in")
    dxn, (slots["w_in_t"],) = _mm_pieces(dproj, w_in_t, ta=False, out_dtype=F32, tm=1024, tn=1024, tk=512, name="dxn",
                                  exchange=_ChipScatter([dw_in_pairs]))
    dx, dg1 = _rms_in_bwd(x2, dxn, dh1, norm_mix)
    small = _pack_small(dg1, dbr, dba, dg2, dg3, dw, loss)
    (small_slots,) = _alone(_ChipGather([small]), "gather_small")
    return dx.reshape(batch, seq, D_MODEL), slots, small_slots.reshape(N_DEV, 16, D_MODEL)


def _row_tile(r, c):
    return max(d for d in range(16, r + 1, 16) if r % d == 0 and (d * c <= 256 * 1024 or d == 16))


def _pair_add(grad, got, name):
    _, r, c = got.shape
    tr = _row_tile(r, c)
    core = lax.axis_index("c").astype(jnp.int32).reshape(1)

    def body(core_ref, g_ref, a_ref, o_ref):
        o_ref[...] = (g_ref[...].astype(F32) + a_ref[...].astype(F32)).astype(o_ref.dtype)

    blk = pl.BlockSpec((None, tr, c), lambda q, i, core_ref: (q, i, 0))
    return pl.pallas_call(
        body, name=name,
        grid_spec=pltpu.PrefetchScalarGridSpec(
            num_scalar_prefetch=1, grid=(4, r // tr),
            in_specs=[pl.BlockSpec((None, None, tr, c), lambda q, i, core_ref: (q, core_ref[0], i, 0)), blk],
            out_specs=blk),
        out_shape=jax.ShapeDtypeStruct(got.shape, got.dtype),
        compiler_params=_params("parallel", "parallel"),
    )(core, grad.reshape(4, 2, r, c), got)


def _sum_slots(slots, name):
    n, r, c = slots.shape
    tr = _row_tile(r, c)

    def body(s_ref, o_ref):
        acc = s_ref[0].astype(F32)
        for s in range(1, n):
            acc = acc + s_ref[s].astype(F32)
        o_ref[...] = acc

    return pl.pallas_call(
        body, name=name, grid=(r // tr,),
        in_specs=[pl.BlockSpec((n, tr, c), lambda i: (0, i, 0))],
        out_specs=pl.BlockSpec((tr, c), lambda i: (i, 0)),
        out_shape=jax.ShapeDtypeStruct((r, c), F32),
        compiler_params=_params("parallel"),
    )(slots)


def _adamw_math(w, g, m, v):
    m = ADAM_B1 * m + (1.0 - ADAM_B1) * g
    v = ADAM_B2 * v + (1.0 - ADAM_B2) * (g * g)
    m_hat = m / (1.0 - ADAM_B1 ** ADAM_STEP)
    v_hat = v / (1.0 - ADAM_B2 ** ADAM_STEP)
    return -ADAM_LR * (m_hat / (jnp.sqrt(v_hat) + ADAM_EPS) + ADAM_WD * w), m, v


def _adamw(w, slots, m, v, name):
    n, r, c = slots.shape
    tr = _row_tile(r, c)

    def body(w_ref, s_ref, m_ref, v_ref, g_ref, d_ref, nm_ref, nv_ref):
        g = s_ref[0].astype(F32)
        for s in range(1, n):
            g = g + s_ref[s].astype(F32)
        g_ref[...] = g
        d_ref[...], nm_ref[...], nv_ref[...] = _adamw_math(w_ref[...], g, m_ref[...], v_ref[...])

    blk = pl.BlockSpec((tr, c), lambda i: (i, 0))
    return pl.pallas_call(
        body, name=name, grid=(r // tr,),
        in_specs=[blk, pl.BlockSpec((n, tr, c), lambda i: (0, i, 0)), blk, blk], out_specs=[blk] * 4,
        out_shape=[jax.ShapeDtypeStruct((r, c), F32)] * 4,
        compiler_params=_params("parallel"),
    )(w, slots, m, v)


def _adamw_small(ws, gs, ms, vs):
    n = len(ws)

    def body(*refs):
        for i in range(n):
            w_ref, g_ref, m_ref, v_ref = (refs[j * n + i] for j in range(4))
            d_ref, nm_ref, nv_ref = (refs[(4 + j) * n + i] for j in range(3))
            d_ref[...], nm_ref[...], nv_ref[...] = _adamw_math(w_ref[...], g_ref[...], m_ref[...], v_ref[...])

    shapes = [jax.ShapeDtypeStruct(w.shape, F32) for w in ws]
    outs = pl.pallas_call(body, name="adamw_small", out_shape=shapes * 3)(*ws, *gs, *ms, *vs)
    return outs[:n], outs[n:2 * n], outs[2 * n:]


def kernel(x, norm_mix, w_in, b_gate, rel_bias, w_ret_out, w_att_out, w_out, norm_ffn, w_ffn_gate, w_ffn_up, w_ffn_down, norm_final, loss_target, m_norm_mix, m_w_in, m_b_gate, m_rel_bias, m_w_ret_out, m_w_att_out, m_w_out, m_norm_ffn, m_w_ffn_gate, m_w_ffn_up, m_w_ffn_down, m_norm_final, v_norm_mix, v_w_in, v_b_gate, v_rel_bias, v_w_ret_out, v_w_att_out, v_w_out, v_norm_ffn, v_w_ffn_gate, v_w_ffn_up, v_w_ffn_down, v_norm_final):
    me = _index(_place())
    n_rb = rel_bias.shape[-1]

    shard = dict(w_in_t=w_in[0].T, w_gate_t=w_ffn_gate[0].T, w_up_t=w_ffn_up[0].T, w_down=w_ffn_down[0],
                 w_ret=w_ret_out[0], w_out=w_out[0], w_att_t=w_att_out[0].T)
    shard = {nm: s.astype(BF16) for nm, s in shard.items()}
    dx, slots, small_slots = _step(x, loss_target, norm_mix, b_gate, norm_ffn, norm_final, rel_bias[0], shard)
    small_sum = _sum_slots(small_slots, "sum_small")
    loss = small_sum[5, 0]

    transposed = dict(w_in="w_in_t", w_ffn_gate="w_gate_t", w_ffn_up="w_up_t", w_att_out="w_att_t")
    plain = dict(w_ffn_down="w_down", w_ret_out="w_ret", w_out="w_out")
    g = dict(
        norm_mix=small_sum[0:1], b_gate=jnp.concatenate([small_sum[1:2], small_sum[2:3]], axis=1),
        norm_ffn=small_sum[3:4], norm_final=small_sum[4:5],
        rel_bias=lax.dynamic_slice_in_dim(_bias_rows_bwd(small_sum[8:16]), me * n_rb, n_rb, axis=1),
    )
    w = dict(norm_mix=norm_mix, w_in=w_in, b_gate=b_gate, rel_bias=rel_bias, w_ret_out=w_ret_out, w_att_out=w_att_out,
             w_out=w_out, norm_ffn=norm_ffn, w_ffn_gate=w_ffn_gate, w_ffn_up=w_ffn_up, w_ffn_down=w_ffn_down,
             norm_final=norm_final)
    m = dict(norm_mix=m_norm_mix, w_in=m_w_in, b_gate=m_b_gate, rel_bias=m_rel_bias, w_ret_out=m_w_ret_out,
             w_att_out=m_w_att_out, w_out=m_w_out, norm_ffn=m_norm_ffn, w_ffn_gate=m_w_ffn_gate, w_ffn_up=m_w_ffn_up,
             w_ffn_down=m_w_ffn_down, norm_final=m_norm_final)
    v = dict(norm_mix=v_norm_mix, w_in=v_w_in, b_gate=v_b_gate, rel_bias=v_rel_bias, w_ret_out=v_w_ret_out,
             w_att_out=v_w_att_out, w_out=v_w_out, norm_ffn=v_norm_ffn, w_ffn_gate=v_w_ffn_gate, w_ffn_up=v_w_ffn_up,
             w_ffn_down=v_w_ffn_down, norm_final=v_norm_final)
    order = ("norm_mix", "w_in", "b_gate", "rel_bias", "w_ret_out", "w_att_out", "w_out", "norm_ffn",
             "w_ffn_gate", "w_ffn_up", "w_ffn_down", "norm_final")
    small_names = ("norm_mix", "b_gate", "rel_bias", "norm_ffn", "norm_final")

    def flat(a):
        return a[0] if a.ndim == 3 else a.reshape(-1, a.shape[-1])

    grad, delta, new_m, new_v = {}, {}, {}, {}
    for nm in order:
        if nm in transposed:
            res = _adamw(w[nm][0].T, slots[transposed[nm]], m[nm][0].T, v[nm][0].T, "adamw_" + nm)
            grad[nm], delta[nm], new_m[nm], new_v[nm] = (a.T[None] for a in res)
        elif nm in plain:
            res = _adamw(flat(w[nm]), slots[plain[nm]], flat(m[nm]), flat(v[nm]), "adamw_" + nm)
            grad[nm], delta[nm], new_m[nm], new_v[nm] = (a.reshape(w[nm].shape) for a in res)
    ds, nms, nvs = _adamw_small([flat(w[nm]) for nm in small_names], [g[nm] for nm in small_names],
                                [flat(m[nm]) for nm in small_names], [flat(v[nm]) for nm in small_names])
    for i, nm in enumerate(small_names):
        grad[nm], delta[nm], new_m[nm], new_v[nm] = (a.reshape(w[nm].shape) for a in (g[nm], ds[i], nms[i], nvs[i]))

    return (loss, dx, *[grad[nm] for nm in order], *[delta[nm] for nm in order],
            *[new_m[nm] for nm in order], *[new_v[nm] for nm in order])
```

```python
import numpy as np
import jax
import jax.numpy as jnp
from jax import lax
from jax.experimental import pallas as pl
from jax.experimental.pallas import tpu as pltpu

F32 = jnp.float32
BF16 = jnp.bfloat16
MESH = pl.DeviceIdType.MESH

D_MODEL = 1024
CHUNK = 64
RET_HEADS = 4
RET_KEY_DIM = 128
RET_VAL_DIM = 256
ATT_HEADS = 8
BAND_CHUNKS = 8
MAX_REL = 256
N_REL = CHUNK + MAX_REL
D_FF = 2816
N_IN = 6656
ROPE_BASE = 10000.0
EPS = 1e-6
NEG_INF = -1e30
C_RQ, C_RK, C_RV, C_RG, C_AQ, C_AK, C_AV, C_GL = 0, 512, 1024, 2048, 3072, 3584, 4096, 4608

ADAM_LR = 0.001
ADAM_B1 = 0.9
ADAM_B2 = 0.999
ADAM_EPS = 1e-08
ADAM_WD = 0.01
ADAM_STEP = 10

N_DEV = 8
LANES = 128
RET_TILE = 256
ATT_Q = 256
ATT_PAD = BAND_CHUNKS * CHUNK
ATT_WIN = ATT_PAD + ATT_Q
ATT_STARTS = ATT_PAD // ATT_Q
ATT_ROWS = 32
BIAS_LEN = 1024
VMEM_LIMIT = 48 * 1024 * 1024


def _params(*sem):
    return pltpu.CompilerParams(dimension_semantics=sem, vmem_limit_bytes=VMEM_LIMIT)


def _dot(a, b):
    return lax.dot_general(a, b, (((1,), (0,)), ((), ())), preferred_element_type=F32)


def _dot_nt(a, b):
    return lax.dot_general(a, b, (((1,), (1,)), ((), ())), preferred_element_type=F32)


def _dot_tn(a, b):
    return lax.dot_general(a, b, (((0,), (0,)), ((), ())), preferred_element_type=F32)


def _sigmoid(x):
    return 1.0 / (1.0 + jnp.exp(-x))


def _rms_bwd(x, g, dy):
    r = lax.rsqrt(jnp.mean(x * x, axis=-1, keepdims=True) + EPS)
    u = dy * g
    dx = r * u - x * (r * r * r) * jnp.mean(u * x, axis=-1, keepdims=True)
    return dx, dy * x * r


def _place():
    return lax.axis_index("x"), lax.axis_index("y"), lax.axis_index("c")


def _peer(k):
    x, y, c = _place()
    return ((1 - x) if k & 4 else x, (1 - y) if k & 2 else y, (1 - c) if k & 1 else c)


def _index(place):
    return 4 * place[0] + 2 * place[1] + place[2]


def _rows(ref, block, nrows):
    align = 16 if ref.dtype == BF16 else 8
    return ref.at[pl.ds(pl.multiple_of(block * nrows, align), nrows)]


class _Exchange:
    def __init__(self, arrays, scatter):
        self.arrays, self.scatter, self.n = list(arrays), scatter, len(arrays)

    def out_shape(self):
        if self.scatter:
            return [jax.ShapeDtypeStruct((N_DEV, a.shape[0] // N_DEV) + a.shape[1:], a.dtype) for a in self.arrays]
        return [jax.ShapeDtypeStruct((N_DEV * a.shape[0],) + a.shape[1:], a.dtype) for a in self.arrays]

    def scratch(self):
        return [pltpu.SemaphoreType.DMA((self.n, N_DEV - 1)), pltpu.SemaphoreType.DMA((self.n, N_DEV - 1)),
                pltpu.SemaphoreType.DMA((self.n,))]

    def _copies(self, ins, outs, sems):
        send_sems, recv_sems, local_sems = sems
        me = _index(_place())

        def src(w, to):
            return _rows(ins[w], to, ins[w].shape[0] // N_DEV) if self.scatter else ins[w]

        def dst(w, origin):
            return outs[w].at[origin] if self.scatter else _rows(outs[w], origin, ins[w].shape[0])

        def remote(w, k, to, origin):
            return pltpu.make_async_remote_copy(src_ref=src(w, to), dst_ref=dst(w, origin),
                                                send_sem=send_sems.at[w, k - 1], recv_sem=recv_sems.at[w, k - 1],
                                                device_id=_peer(k), device_id_type=MESH)

        pairs = [(w, k) for w in range(self.n) for k in range(1, N_DEV)]
        own = lambda: [pltpu.make_async_copy(src(w, me), dst(w, me), local_sems.at[w]) for w in range(self.n)]
        sent = lambda: [remote(w, k, _index(_peer(k)), me) for w, k in pairs]
        arriving = lambda: [remote(w, k, me, _index(_peer(k))) for w, k in pairs]
        return own, sent, arriving

    def start(self, ins, outs, sems):
        own, sent, _ = self._copies(ins, outs, sems)
        for cp in own() + sent():
            cp.start()

    def wait(self, ins, outs, sems):
        own, sent, arriving = self._copies(ins, outs, sems)
        for cp in arriving():
            cp.wait_recv()
        for cp in sent():
            cp.wait_send()
        for cp in own():
            cp.wait()


class _PairSwap:
    def __init__(self, arrays):
        self.arrays, self.n = list(arrays), len(arrays)

    def out_shape(self):
        return [jax.ShapeDtypeStruct((4, a.shape[0] // N_DEV) + a.shape[1:], a.dtype) for a in self.arrays]

    def scratch(self):
        return [pltpu.SemaphoreType.DMA((self.n, 4)), pltpu.SemaphoreType.DMA((self.n, 4))]

    def _copies(self, ins, outs, sems):
        send_sems, recv_sems = sems
        x, y, c = _place()
        return [pltpu.make_async_remote_copy(
            src_ref=_rows(ins[w], 2 * q + 1 - c, ins[w].shape[0] // N_DEV), dst_ref=outs[w].at[q],
            send_sem=send_sems.at[w, q], recv_sem=recv_sems.at[w, q],
            device_id=(x, y, 1 - c), device_id_type=MESH) for w in range(self.n) for q in range(4)]

    def start(self, ins, outs, sems):
        for cp in self._copies(ins, outs, sems):
            cp.start()

    def wait(self, ins, outs, sems):
        for cp in self._copies(ins, outs, sems):
            cp.wait()


class _ChipScatter:
    def __init__(self, arrays):
        self.arrays, self.n = list(arrays), len(arrays)

    def out_shape(self):
        return [jax.ShapeDtypeStruct(a.shape, a.dtype) for a in self.arrays]

    def scratch(self):
        return [pltpu.SemaphoreType.DMA((self.n, 3)), pltpu.SemaphoreType.DMA((self.n, 3)),
                pltpu.SemaphoreType.DMA((self.n,))]

    def _copies(self, ins, outs, sems):
        send_sems, recv_sems, local_sems = sems
        x, y, c = _place()
        mine = 2 * x + y
        sent, arriving = [], []
        for w in range(self.n):
            for k in range(1, 4):
                tx, ty = (1 - x) if k & 2 else x, (1 - y) if k & 1 else y
                other = 2 * tx + ty
                sent.append(lambda w=w, k=k, tx=tx, ty=ty, other=other: pltpu.make_async_remote_copy(
                    src_ref=ins[w].at[other], dst_ref=outs[w].at[mine], send_sem=send_sems.at[w, k - 1],
                    recv_sem=recv_sems.at[w, k - 1], device_id=(tx, ty, c), device_id_type=MESH))
                arriving.append(lambda w=w, k=k, tx=tx, ty=ty, other=other: pltpu.make_async_remote_copy(
                    src_ref=ins[w].at[mine], dst_ref=outs[w].at[other], send_sem=send_sems.at[w, k - 1],
                    recv_sem=recv_sems.at[w, k - 1], device_id=(tx, ty, c), device_id_type=MESH))
        own = [lambda w=w: pltpu.make_async_copy(ins[w].at[mine], outs[w].at[mine], local_sems.at[w])
               for w in range(self.n)]
        return own, sent, arriving

    def start(self, ins, outs, sems):
        own, sent, _ = self._copies(ins, outs, sems)
        for cp in own + sent:
            cp().start()

    def wait(self, ins, outs, sems):
        own, sent, arriving = self._copies(ins, outs, sems)
        for cp in arriving:
            cp().wait_recv()
        for cp in sent:
            cp().wait_send()
        for cp in own:
            cp().wait()


class _ChipGather:
    def __init__(self, arrays):
        self.arrays, self.n = list(arrays), len(arrays)

    def out_shape(self):
        return [jax.ShapeDtypeStruct((N_DEV * a.shape[0],) + a.shape[1:], a.dtype) for a in self.arrays]

    def scratch(self):
        return [pltpu.SemaphoreType.DMA((self.n, N_DEV - 1)), pltpu.SemaphoreType.DMA((self.n, N_DEV - 1)),
                pltpu.SemaphoreType.DMA((self.n,))]

    def _parts(self, ins, outs, sems):
        send_sems, recv_sems, local_sems = sems
        x, y, c = _place()
        me, sibling = (x, y, c), (x, y, 1 - c)
        chips = [(1 - x, y), (x, 1 - y), (1 - x, 1 - y)]

        def rows(w, place):
            return _rows(outs[w], _index(place), ins[w].shape[0])

        def copy(w, k, block, to, own=False):
            return pltpu.make_async_remote_copy(src_ref=ins[w] if own else rows(w, block), dst_ref=rows(w, block),
                                                send_sem=send_sems.at[w, k], recv_sem=recv_sems.at[w, k],
                                                device_id=to, device_id_type=MESH)

        def local(w):
            return pltpu.make_async_copy(ins[w], rows(w, me), local_sems.at[w])

        return me, sibling, chips, c, copy, local

    def start(self, ins, outs, sems):
        me, sibling, chips, c, copy, local = self._parts(ins, outs, sems)
        for w in range(self.n):
            local(w).start()
            copy(w, 0, me, sibling, own=True).start()
            for j, chip in enumerate(chips):
                copy(w, 1 + j, me, (*chip, c), own=True).start()

    def wait(self, ins, outs, sems):
        me, sibling, chips, c, copy, local = self._parts(ins, outs, sems)
        for w in range(self.n):
            for j, chip in enumerate(chips):
                copy(w, 1 + j, (*chip, c), me).wait_recv()
                copy(w, 4 + j, (*chip, c), sibling).start()
        for w in range(self.n):
            copy(w, 0, sibling, me).wait_recv()
            for j, chip in enumerate(chips):
                copy(w, 4 + j, (*chip, 1 - c), me).wait_recv()
            copy(w, 0, me, sibling, own=True).wait_send()
            for j, chip in enumerate(chips):
                copy(w, 1 + j, me, (*chip, c), own=True).wait_send()
                copy(w, 4 + j, (*chip, c), sibling).wait_send()
            local(w).wait()


def _call(body, *, name, grid, in_specs, out_specs, out_shape, scratch=(), semantics, args, exchange=None):
    if exchange is None:
        return pl.pallas_call(body, name=name, grid=grid, in_specs=in_specs, out_specs=out_specs, out_shape=out_shape,
                              scratch_shapes=list(scratch), compiler_params=_params(*semantics))(*args), None
    n_in, n_out, n_scr, nx = len(in_specs), len(out_specs), len(scratch), exchange.n

    def full_body(*refs):
        ins, refs = refs[:n_in], refs[n_in:]
        x_in, refs = refs[:nx], refs[nx:]
        outs, refs = refs[:n_out], refs[n_out:]
        x_out, refs = refs[:nx], refs[nx:]
        scr, sems = refs[:n_scr], refs[n_scr:]
        first, last = True, True
        for axis, size in enumerate(grid):
            first = jnp.logical_and(first, pl.program_id(axis) == 0)
            last = jnp.logical_and(last, pl.program_id(axis) == size - 1)
        if grid:
            pl.when(first)(lambda: exchange.start(x_in, x_out, sems))
        else:
            exchange.start(x_in, x_out, sems)
        body(*ins, *outs, *scr)
        if grid:
            pl.when(last)(lambda: exchange.wait(x_in, x_out, sems))
        else:
            exchange.wait(x_in, x_out, sems)

    hbm = pl.BlockSpec(memory_space=pltpu.HBM)
    res = pl.pallas_call(
        full_body, name=name, grid=grid,
        in_specs=list(in_specs) + [hbm] * nx, out_specs=list(out_specs) + [hbm] * nx,
        out_shape=list(out_shape) + exchange.out_shape(),
        scratch_shapes=list(scratch) + exchange.scratch(),
        compiler_params=_params(*(["arbitrary"] * len(grid))),
    )(*args, *exchange.arrays)
    return res[:n_out], res[n_out:]


def _alone(exchange, name):
    return _call(lambda: None, name=name, grid=(), in_specs=[], out_specs=[], out_shape=[], semantics=(),
                 args=(), exchange=exchange)[1]


def _mm(a, b, *, ta=False, tb=False, out_dtype, tm, tn, tk, name, exchange=None):
    m, k = (a.shape[1], a.shape[0]) if ta else a.shape
    n = b.shape[0] if tb else b.shape[1]
    assert k == (b.shape[1] if tb else b.shape[0])
    tm, tn, tk = min(tm, m), min(tn, n), min(tk, k)
    assert m % tm == 0 and n % tn == 0 and k % tk == 0, (name, m, n, k)
    nk = k // tk
    dims = (((0 if ta else 1,), (1 if tb else 0,)), ((), ()))

    def body(a_ref, b_ref, o_ref, *acc):
        prod = lax.dot_general(a_ref[...].astype(BF16), b_ref[...].astype(BF16), dims, preferred_element_type=F32)
        if nk == 1:
            o_ref[...] = prod.astype(o_ref.dtype)
            return
        acc_ref, kk = acc[0], pl.program_id(2)

        @pl.when(kk == 0)
        def _():
            acc_ref[...] = prod

        @pl.when((kk > 0) & (kk < nk - 1))
        def _():
            acc_ref[...] += prod

        @pl.when(kk == nk - 1)
        def _():
            o_ref[...] = (acc_ref[...] + prod).astype(o_ref.dtype)

    a_spec = (pl.BlockSpec((tk, tm), lambda i, j, kk: (kk, i)) if ta
              else pl.BlockSpec((tm, tk), lambda i, j, kk: (i, kk)))
    b_spec = (pl.BlockSpec((tn, tk), lambda i, j, kk: (j, kk)) if tb
              else pl.BlockSpec((tk, tn), lambda i, j, kk: (kk, j)))
    (out,), moved = _call(
        body, name=name, grid=(m // tm, n // tn, nk),
        in_specs=[a_spec, b_spec],
        out_specs=[pl.BlockSpec((tm, tn), lambda i, j, kk: (i, j))],
        out_shape=[jax.ShapeDtypeStruct((m, n), out_dtype)],
        scratch=[pltpu.VMEM((tm, tn), F32)] if nk > 1 else [],
        semantics=("parallel", "parallel", "arbitrary"), args=(a, b), exchange=exchange)
    return out if exchange is None else (out, moved)


def _mm_pieces(pieces, b, *, ta, out_dtype, tm, tn, tk, name, exchange=None):
    rows, n = pieces[0].shape[0], b.shape[1]
    step = tm if ta else tk
    assert all(p.shape[0] == rows and p.shape[1] % step == 0 for p in pieces), name
    edges = [int(e) for e in np.cumsum([0] + [p.shape[1] // step for p in pieces])]
    total = edges[-1] * step
    m, k = (total, rows) if ta else (rows, total)
    assert b.shape[0] == k and m % tm == 0 and n % tn == 0 and k % tk == 0, name
    nk, npieces = k // tk, len(pieces)
    dims = (((0 if ta else 1,), (0,)), ((), ()))

    def body(*refs):
        a_refs, (b_ref, o_ref, acc_ref) = refs[:npieces], refs[npieces:]
        kk = pl.program_id(2)
        pos = pl.program_id(0) if ta else kk

        @pl.when(kk == 0)
        def _():
            acc_ref[...] = jnp.zeros_like(acc_ref)

        for p, a_ref in enumerate(a_refs):
            @pl.when((pos >= edges[p]) & (pos < edges[p + 1]))
            def _(a_ref=a_ref):
                acc_ref[...] += lax.dot_general(a_ref[...], b_ref[...], dims, preferred_element_type=F32)

        @pl.when(kk == nk - 1)
        def _():
            o_ref[...] = acc_ref[...].astype(o_ref.dtype)

    def a_spec(p):
        lo, last = edges[p], edges[p + 1] - edges[p] - 1
        if ta:
            def index(i, j, kk):
                inside = (i >= lo) & (i <= lo + last)
                return jnp.where(inside, kk, 0), jnp.clip(i - lo, 0, last)
            return pl.BlockSpec((tk, tm), index)
        return pl.BlockSpec((tm, tk), lambda i, j, kk: (i, jnp.clip(kk - lo, 0, last)))

    (out,), moved = _call(
        body, name=name, grid=(m // tm, n // tn, nk),
        in_specs=[a_spec(p) for p in range(npieces)] + [pl.BlockSpec((tk, tn), lambda i, j, kk: (kk, j))],
        out_specs=[pl.BlockSpec((tm, tn), lambda i, j, kk: (i, j))],
        out_shape=[jax.ShapeDtypeStruct((m, n), out_dtype)],
        scratch=[pltpu.VMEM((tm, tn), F32)],
        semantics=("parallel", "parallel", "arbitrary"), args=(*pieces, b), exchange=exchange)
    return out if exchange is None else (out, moved)


def _rms_fwd(x2, g, exchange):
    t = x2.shape[0]
    tm = min(512, t)

    def body(x_ref, g_ref, o_ref):
        x = x_ref[...]
        r = lax.rsqrt(jnp.mean(x * x, axis=-1, keepdims=True) + EPS)
        o_ref[...] = (x * r * g_ref[...]).astype(o_ref.dtype)

    return _call(
        body, name="rms_in_fwd", grid=(t // tm,),
        in_specs=[pl.BlockSpec((tm, D_MODEL), lambda i: (i, 0)), pl.BlockSpec((1, D_MODEL), lambda i: (0, 0))],
        out_specs=[pl.BlockSpec((tm, D_MODEL), lambda i: (i, 0))],
        out_shape=[jax.ShapeDtypeStruct((t, D_MODEL), BF16)],
        semantics=("parallel",), args=(x2, g), exchange=exchange)


def _decay(lg):
    row = lax.broadcasted_iota(jnp.int32, (RET_TILE, RET_TILE), 0)
    col = lax.broadcasted_iota(jnp.int32, (RET_TILE, RET_TILE), 1)
    within = jnp.exp(lg * jnp.abs(row - col).astype(F32))
    inside = jnp.where((col >> 6) <= (row >> 6), within, 0.0)
    pos = lax.broadcasted_iota(jnp.int32, (RET_TILE, 1), 0).astype(F32)
    q_dec = jnp.exp(lg * (pos + 1.0))
    k_dec = jnp.exp(lg * (RET_TILE - 1.0 - pos))
    tile_dec = jnp.exp(lg * float(RET_TILE))
    return inside, q_dec, k_dec, tile_dec


def _scaled(a_bf16, dec):
    return (a_bf16.astype(F32) * dec).astype(BF16)


def _ret_fwd(proj, cs, sn, lg_arr, batch, seq, exchange):
    t = batch * seq
    nt = seq // RET_TILE

    def body(q_ref, k_ref, v_ref, rg_ref, cs_ref, sn_ref, lg_ref, gro_ref, o_ref, qr_ref, kr_ref):
        lg = lg_ref[:, 0:1]
        cs_t, sn_t = cs_ref[...], sn_ref[...]
        q = q_ref[...].astype(F32)
        k = k_ref[...].astype(F32)
        qr_ref[...] = (q * cs_t + pltpu.roll(q, 64, 1) * sn_t).astype(BF16)
        kr_ref[...] = ((k * cs_t + pltpu.roll(k, 64, 1) * sn_t) * (RET_KEY_DIM ** -0.5)).astype(BF16)
        inside, q_dec, k_dec, tile_dec = _decay(lg)
        state = jnp.zeros((RET_KEY_DIM, RET_VAL_DIM), F32)
        for i in range(nt):
            rows = slice(i * RET_TILE, (i + 1) * RET_TILE)
            qi, ki, vi = qr_ref[rows, :], kr_ref[rows, :], v_ref[rows, :]
            acc = _dot((_dot_nt(qi, ki) * inside).astype(BF16), vi)
            if i > 0:
                acc = acc + _dot(_scaled(qi, q_dec), state.astype(BF16))
            if i < nt - 1:
                state = state * tile_dec + _dot_tn(_scaled(ki, k_dec), vi)
            o_ref[rows, :] = acc
            xc = acc - jnp.mean(acc, axis=-1, keepdims=True)
            nrm = xc * lax.rsqrt(jnp.mean(xc * xc, axis=-1, keepdims=True) + EPS)
            rg = rg_ref[rows, :].astype(F32)
            gro_ref[rows, :] = (rg * _sigmoid(rg) * nrm).astype(BF16)

    def col(base, width):
        return lambda b, h: (b, base // width + h)

    return _call(
        body, name="ret_fwd", grid=(batch, RET_HEADS),
        in_specs=[pl.BlockSpec((seq, RET_KEY_DIM), col(C_RQ, RET_KEY_DIM)),
                  pl.BlockSpec((seq, RET_KEY_DIM), col(C_RK, RET_KEY_DIM)),
                  pl.BlockSpec((seq, RET_VAL_DIM), col(C_RV, RET_VAL_DIM)),
                  pl.BlockSpec((seq, RET_VAL_DIM), col(C_RG, RET_VAL_DIM)),
                  pl.BlockSpec((seq, RET_KEY_DIM), lambda b, h: (0, 0)),
                  pl.BlockSpec((seq, RET_KEY_DIM), lambda b, h: (0, 0)),
                  pl.BlockSpec((None, 1, LANES), lambda b, h: (h, 0, 0))],
        out_specs=[pl.BlockSpec((seq, RET_VAL_DIM), lambda b, h: (b, h)),
                   pl.BlockSpec((seq, RET_VAL_DIM), lambda b, h: (b, h)),
                   pl.BlockSpec((seq, RET_KEY_DIM), lambda b, h: (b, h)),
                   pl.BlockSpec((seq, RET_KEY_DIM), lambda b, h: (b, h))],
        out_shape=[jax.ShapeDtypeStruct((t, RET_HEADS * RET_VAL_DIM), BF16),
                   jax.ShapeDtypeStruct((t, RET_HEADS * RET_VAL_DIM), F32),
                   jax.ShapeDtypeStruct((t, RET_HEADS * RET_KEY_DIM), BF16),
                   jax.ShapeDtypeStruct((t, RET_HEADS * RET_KEY_DIM), BF16)],
        semantics=("parallel", "parallel"), args=(proj, proj, proj, proj, cs, sn, lg_arr), exchange=exchange)


def _att_bias(w_ref, bias_ref):
    n_i = lax.broadcasted_iota(jnp.int32, (ATT_Q, BIAS_LEN), 0)
    qc = lax.broadcasted_iota(jnp.int32, (ATT_Q, ATT_WIN), 0) >> 6
    kc = lax.broadcasted_iota(jnp.int32, (ATT_Q, ATT_WIN), 1) >> 6
    dc = qc + BAND_CHUNKS - kc
    band = (dc >= 0) & (dc <= BAND_CHUNKS)
    key = lax.broadcasted_iota(jnp.int32, (ATT_Q, ATT_WIN), 1)
    for e in range(2):
        xw = jnp.broadcast_to(w_ref[e:e + 1, :], (ATT_Q, BIAS_LEN))
        for bit in range(8):
            xw = jnp.where(((n_i >> bit) & 1) == 1, pltpu.roll(xw, 1 << bit, 1), xw)
        bias = jnp.where(band, xw[:, BIAS_LEN - ATT_WIN:], NEG_INF)
        for first in range(ATT_STARTS):
            bias_ref[first, e] = jnp.where(key + (first * ATT_Q - ATT_PAD) >= 0, bias, NEG_INF)
        bias_ref[ATT_STARTS, e] = bias


def _att_specs(batch, seq):
    ni = seq // ATT_Q
    q_spec = pl.BlockSpec((ATT_Q, LANES), lambda hp, b, i: (b * ni + i, C_AQ // LANES + hp))
    k_spec = pl.BlockSpec((seq, LANES), lambda hp, b, i: (b, C_AK // LANES + hp))
    v_spec = pl.BlockSpec((seq, LANES), lambda hp, b, i: (b, C_AV // LANES + hp))
    w_spec = pl.BlockSpec((None, 2, BIAS_LEN), lambda hp, b, i: (hp, 0, 0))
    b_spec = pl.BlockSpec((None, ATT_STARTS + 1, 2, ATT_Q, ATT_WIN), lambda hp, b, i: (hp, 0, 0, 0, 0))
    pad = pltpu.VMEM((seq + ATT_PAD, LANES), BF16)
    return ni, q_spec, k_spec, v_spec, w_spec, b_spec, pad


def _att_bias_tiles(wvec):
    return pl.pallas_call(
        _att_bias, name="att_bias", grid=(ATT_HEADS // 2,),
        in_specs=[pl.BlockSpec((None, 2, BIAS_LEN), lambda hp: (hp, 0, 0))],
        out_specs=pl.BlockSpec((None, ATT_STARTS + 1, 2, ATT_Q, ATT_WIN), lambda hp: (hp, 0, 0, 0, 0)),
        out_shape=jax.ShapeDtypeStruct((ATT_HEADS // 2, ATT_STARTS + 1, 2, ATT_Q, ATT_WIN), F32),
        compiler_params=_params("parallel"),
    )(wvec)


def _att_pad(src_ref, pad_ref):
    pad_ref[:ATT_PAD, :] = jnp.zeros((ATT_PAD, LANES), BF16)
    pad_ref[ATT_PAD:, :] = src_ref[...]


def _att_head(q2, sel):
    return jnp.where(sel, q2, jnp.zeros_like(q2)) * 0.125


def _att_softmax_rows(s_ref, bias_ref, rows):
    s = s_ref[rows, :] + bias_ref[rows, :]
    ex = jnp.exp(s - jnp.max(s, axis=-1, keepdims=True))
    return ex, 1.0 / jnp.sum(ex, axis=-1, keepdims=True)


def _att_fwd(proj, bias, batch, seq, exchange):
    ni, q_spec, k_spec, v_spec, _, b_spec, pad = _att_specs(batch, seq)

    def body(q_ref, k_ref, v_ref, bias_ref, o_ref, kp_ref, vp_ref, s_ref, e_ref):
        i = pl.program_id(2)

        @pl.when(i == 0)
        def _():
            _att_pad(k_ref, kp_ref)
            _att_pad(v_ref, vp_ref)

        win = pl.ds(pl.multiple_of(i * ATT_Q, ATT_Q), ATT_WIN)
        k2, v2, q2 = kp_ref[win, :], vp_ref[win, :], q_ref[...]
        lo = lax.broadcasted_iota(jnp.int32, (1, LANES), 1) < 64
        start = jnp.minimum(i, ATT_STARTS)
        out = jnp.zeros((ATT_Q, LANES), F32)
        for e in range(2):
            sel = lo if e == 0 else jnp.logical_not(lo)
            s_ref[e] = _dot_nt(_att_head(q2, sel), k2)
            rsum = []
            for c in range(ATT_Q // ATT_ROWS):
                rows = slice(c * ATT_ROWS, (c + 1) * ATT_ROWS)
                ex, r = _att_softmax_rows(s_ref.at[e], bias_ref.at[start, e], rows)
                e_ref[e, rows, :] = ex.astype(BF16)
                rsum.append(r)
            out = out + _dot(e_ref[e], jnp.where(sel, v2, jnp.zeros_like(v2))) * jnp.concatenate(rsum, axis=0)
        o_ref[...] = out.astype(BF16)

    return _call(
        body, name="att_fwd", grid=(ATT_HEADS // 2, batch, ni),
        in_specs=[q_spec, k_spec, v_spec, b_spec],
        out_specs=[pl.BlockSpec((ATT_Q, LANES), lambda hp, b, i: (b * ni + i, hp))],
        out_shape=[jax.ShapeDtypeStruct((batch * seq, ATT_HEADS * 64), BF16)],
        scratch=[pad, pad, pltpu.VMEM((2, ATT_Q, ATT_WIN), F32), pltpu.VMEM((2, ATT_Q, ATT_WIN), BF16)],
        semantics=("arbitrary", "arbitrary", "arbitrary"), args=(proj, proj, proj, bias), exchange=exchange)


def _mix_fwd(gro, ao, proj, b_gate, w_ret, w_att_t):
    t = gro.shape[0]
    tm, tn = min(256, t), 512

    def body(gro_ref, ao_ref, glr_ref, gla_ref, br_ref, ba_ref, wr_ref, wa_ref, z_ref, yr_ref, ya_ref):
        yr = _dot(gro_ref[...], wr_ref[...])
        ya = _dot_nt(ao_ref[...], wa_ref[...])
        gr = _sigmoid(glr_ref[...].astype(F32) + br_ref[...])
        ga = _sigmoid(gla_ref[...].astype(F32) + ba_ref[...])
        z_ref[...] = (gr * yr + ga * ya).astype(BF16)
        yr_ref[...] = yr.astype(BF16)
        ya_ref[...] = ya.astype(BF16)

    nb = D_MODEL // tn
    out = pl.BlockSpec((tm, tn), lambda i, j: (i, j))
    return pl.pallas_call(
        body, name="mix_fwd", grid=(t // tm, nb),
        in_specs=[pl.BlockSpec((tm, D_MODEL), lambda i, j: (i, 0)),
                  pl.BlockSpec((tm, 512), lambda i, j: (i, 0)),
                  pl.BlockSpec((tm, tn), lambda i, j: (i, C_GL // tn + j)),
                  pl.BlockSpec((tm, tn), lambda i, j: (i, C_GL // tn + nb + j)),
                  pl.BlockSpec((1, tn), lambda i, j: (0, j)),
                  pl.BlockSpec((1, tn), lambda i, j: (0, nb + j)),
                  pl.BlockSpec((D_MODEL, tn), lambda i, j: (0, j)),
                  pl.BlockSpec((tn, 512), lambda i, j: (j, 0))],
        out_specs=[out, out, out],
        out_shape=[jax.ShapeDtypeStruct((t, D_MODEL), BF16)] * 3,
        compiler_params=_params("parallel", "parallel"),
    )(gro, ao, proj, proj, b_gate, b_gate, w_ret, w_att_t)


def _out_fwd(z, x2, w_out, g2):
    t = z.shape[0]
    tm = min(256, t)

    def body(z_ref, x_ref, w_ref, g_ref, h_ref, hn_ref):
        h = x_ref[...] + _dot(z_ref[...], w_ref[...])
        h_ref[...] = h
        r = lax.rsqrt(jnp.mean(h * h, axis=-1, keepdims=True) + EPS)
        hn_ref[...] = (h * r * g_ref[...]).astype(BF16)

    row = pl.BlockSpec((tm, D_MODEL), lambda i: (i, 0))
    return pl.pallas_call(
        body, name="out_fwd", grid=(t // tm,),
        in_specs=[row, row, pl.BlockSpec((D_MODEL, D_MODEL), lambda i: (0, 0)),
                  pl.BlockSpec((1, D_MODEL), lambda i: (0, 0))],
        out_specs=[row, row],
        out_shape=[jax.ShapeDtypeStruct((t, D_MODEL), F32), jax.ShapeDtypeStruct((t, D_MODEL), BF16)],
        compiler_params=_params("parallel"),
    )(z, x2, w_out, g2)


def _ffn_up(hn, wg_t, wu_t):
    t = hn.shape[0]
    tm, tn = min(512, t), D_FF // 2

    def body(h_ref, wg_ref, wu_ref, g_ref, u_ref, a_ref):
        g = _dot_nt(h_ref[...], wg_ref[...])
        u = _dot_nt(h_ref[...], wu_ref[...])
        g_ref[...] = g.astype(BF16)
        u_ref[...] = u.astype(BF16)
        a_ref[...] = (g * _sigmoid(g) * u).astype(BF16)

    w_spec = pl.BlockSpec((tn, D_MODEL), lambda j, i: (j, 0))
    out = pl.BlockSpec((tm, tn), lambda j, i: (i, j))
    return pl.pallas_call(
        body, name="ffn_up", grid=(D_FF // tn, t // tm),
        in_specs=[pl.BlockSpec((tm, D_MODEL), lambda j, i: (i, 0)), w_spec, w_spec],
        out_specs=[out, out, out],
        out_shape=[jax.ShapeDtypeStruct((t, D_FF), BF16)] * 3,
        compiler_params=_params("parallel", "parallel"),
    )(hn, wg_t, wu_t)


def _ffn_down_loss(a, h1, tgt, w_down, g3):
    t = a.shape[0]
    tm = min(512, t)

    def body(a_ref, h_ref, t_ref, w_ref, g_ref, dh_ref, dhb_ref, loss_ref, dg_ref):
        @pl.when(pl.program_id(0) == 0)
        def _():
            loss_ref[...] = jnp.zeros_like(loss_ref)
            dg_ref[...] = jnp.zeros_like(dg_ref)

        g = g_ref[...]
        h2 = h_ref[...] + _dot(a_ref[...], w_ref[...])
        r = lax.rsqrt(jnp.mean(h2 * h2, axis=-1, keepdims=True) + EPS)
        err = h2 * r * g - t_ref[...]
        loss_ref[...] += jnp.sum(err * err) * (0.5 / D_MODEL)
        dy = err * (1.0 / D_MODEL)
        dh, dg_rows = _rms_bwd(h2, g, dy)
        dg_ref[...] += jnp.sum(dg_rows, axis=0, keepdims=True)
        dh_ref[...] = dh
        dhb_ref[...] = dh.astype(BF16)

    row = pl.BlockSpec((tm, D_MODEL), lambda i: (i, 0))
    vec = pl.BlockSpec((1, D_MODEL), lambda i: (0, 0))
    return pl.pallas_call(
        body, name="ffn_down_loss", grid=(t // tm,),
        in_specs=[pl.BlockSpec((tm, D_FF), lambda i: (i, 0)), row, row,
                  pl.BlockSpec((D_FF, D_MODEL), lambda i: (0, 0)), vec],
        out_specs=[row, row, pl.BlockSpec((1, LANES), lambda i: (0, 0)), vec],
        out_shape=[jax.ShapeDtypeStruct((t, D_MODEL), F32), jax.ShapeDtypeStruct((t, D_MODEL), BF16),
                   jax.ShapeDtypeStruct((1, LANES), F32), jax.ShapeDtypeStruct((1, D_MODEL), F32)],
        compiler_params=_params("arbitrary"),
    )(a, h1, tgt, w_down, g3)


def _ffn_bwd_act(dh2b, w_down, g_act, u_act, exchange):
    t = dh2b.shape[0]
    tm, tn = min(512, t), D_FF // 2

    def body(d_ref, w_ref, g_ref, u_ref, dg_ref, du_ref):
        da = _dot_nt(d_ref[...], w_ref[...])
        g = g_ref[...].astype(F32)
        u = u_ref[...].astype(F32)
        sg = _sigmoid(g)
        dg_ref[...] = (da * u * sg * (1.0 + g * (1.0 - sg))).astype(BF16)
        du_ref[...] = (da * g * sg).astype(BF16)

    blk = pl.BlockSpec((tm, tn), lambda j, i: (i, j))
    return _call(
        body, name="ffn_bwd_act", grid=(D_FF // tn, t // tm),
        in_specs=[pl.BlockSpec((tm, D_MODEL), lambda j, i: (i, 0)),
                  pl.BlockSpec((tn, D_MODEL), lambda j, i: (j, 0)), blk, blk],
        out_specs=[blk, blk],
        out_shape=[jax.ShapeDtypeStruct((t, D_FF), BF16)] * 2,
        semantics=("parallel", "parallel"), args=(dh2b, w_down, g_act, u_act), exchange=exchange)


def _ffn_bwd_in(dg, du, wg_t, wu_t, h1, dh2, g2, exchange):
    t = dg.shape[0]
    tm, tk = min(512, t), D_FF // 2
    nk = D_FF // tk

    def body(dg_ref, du_ref, wg_ref, wu_ref, h_ref, d2_ref, g_ref, dh_ref, dhb_ref, gn_ref, acc_ref):
        i, kk = pl.program_id(0), pl.program_id(1)

        @pl.when((i == 0) & (kk == 0))
        def _():
            gn_ref[...] = jnp.zeros_like(gn_ref)

        @pl.when(kk == 0)
        def _():
            acc_ref[...] = jnp.zeros_like(acc_ref)

        acc_ref[...] += _dot(dg_ref[...], wg_ref[...]) + _dot(du_ref[...], wu_ref[...])

        @pl.when(kk == nk - 1)
        def _():
            dx, dg_rows = _rms_bwd(h_ref[...], g_ref[...], acc_ref[...])
            dh = d2_ref[...] + dx
            dh_ref[...] = dh
            dhb_ref[...] = dh.astype(BF16)
            gn_ref[...] += jnp.sum(dg_rows, axis=0, keepdims=True)

    act = pl.BlockSpec((tm, tk), lambda i, kk: (i, kk))
    wsp = pl.BlockSpec((tk, D_MODEL), lambda i, kk: (kk, 0))
    row = pl.BlockSpec((tm, D_MODEL), lambda i, kk: (i, 0))
    vec = pl.BlockSpec((1, D_MODEL), lambda i, kk: (0, 0))
    return _call(
        body, name="ffn_bwd_in", grid=(t // tm, nk),
        in_specs=[act, act, wsp, wsp, row, row, vec],
        out_specs=[row, row, vec],
        out_shape=[jax.ShapeDtypeStruct((t, D_MODEL), F32), jax.ShapeDtypeStruct((t, D_MODEL), BF16),
                   jax.ShapeDtypeStruct((1, D_MODEL), F32)],
        scratch=[pltpu.VMEM((tm, D_MODEL), F32)],
        semantics=("arbitrary", "arbitrary"), args=(dg, du, wg_t, wu_t, h1, dh2, g2), exchange=exchange)


def _mix_bwd(dh1b, w_out, proj, b_gate, y_ret, y_att, exchange):
    t = dh1b.shape[0]
    tm, tn = min(256, t), 512
    nb = D_MODEL // tn

    def body(d_ref, w_ref, glr_ref, gla_ref, br_ref, ba_ref, yr_ref, ya_ref,
             dyr_ref, dya_ref, dglr_ref, dgla_ref, dbr_ref, dba_ref):
        @pl.when(pl.program_id(1) == 0)
        def _():
            dbr_ref[...] = jnp.zeros_like(dbr_ref)
            dba_ref[...] = jnp.zeros_like(dba_ref)

        dz = _dot_nt(d_ref[...], w_ref[...])
        gr = _sigmoid(glr_ref[...].astype(F32) + br_ref[...])
        ga = _sigmoid(gla_ref[...].astype(F32) + ba_ref[...])
        dyr_ref[...] = (dz * gr).astype(BF16)
        dya_ref[...] = (dz * ga).astype(BF16)
        dglr = dz * yr_ref[...].astype(F32) * gr * (1.0 - gr)
        dgla = dz * ya_ref[...].astype(F32) * ga * (1.0 - ga)
        dglr_ref[...] = dglr.astype(BF16)
        dgla_ref[...] = dgla.astype(BF16)
        dbr_ref[...] += jnp.sum(dglr, axis=0, keepdims=True)
        dba_ref[...] += jnp.sum(dgla, axis=0, keepdims=True)

    blk = pl.BlockSpec((tm, tn), lambda j, i: (i, j))
    vec = pl.BlockSpec((1, tn), lambda j, i: (0, j))
    return _call(
        body, name="mix_bwd", grid=(nb, t // tm),
        in_specs=[pl.BlockSpec((tm, D_MODEL), lambda j, i: (i, 0)),
                  pl.BlockSpec((tn, D_MODEL), lambda j, i: (j, 0)),
                  pl.BlockSpec((tm, tn), lambda j, i: (i, C_GL // tn + j)),
                  pl.BlockSpec((tm, tn), lambda j, i: (i, C_GL // tn + nb + j)),
                  vec, pl.BlockSpec((1, tn), lambda j, i: (0, nb + j)), blk, blk],
        out_specs=[blk, blk, blk, blk, vec, vec],
        out_shape=[jax.ShapeDtypeStruct((t, D_MODEL), BF16)] * 4 + [jax.ShapeDtypeStruct((1, D_MODEL), F32)] * 2,
        semantics=("arbitrary", "arbitrary"), args=(dh1b, w_out, proj, proj, b_gate, b_gate, y_ret, y_att),
        exchange=exchange)


def _ret_bwd(dgro, proj, o_ret, qr, kr, cs, sn, lg_arr, batch, seq, exchange):
    t = batch * seq
    nt = seq // RET_TILE

    def body(dgro_ref, rg_ref, o_ref, qr_ref, kr_ref, v_ref, cs_ref, sn_ref, lg_ref,
             dq_ref, dk_ref, dv_ref, drg_ref, do_ref, st_ref):
        lg = lg_ref[:, 0:1]
        inside, q_dec, k_dec, tile_dec = _decay(lg)

        state = jnp.zeros((RET_KEY_DIM, RET_VAL_DIM), F32)
        for i in range(nt - 1):
            rows = slice(i * RET_TILE, (i + 1) * RET_TILE)
            state = state * tile_dec + _dot_tn(_scaled(kr_ref[rows, :], k_dec), v_ref[rows, :])
            st_ref[i + 1] = state.astype(BF16)

        for i in range(nt):
            rows = slice(i * RET_TILE, (i + 1) * RET_TILE)
            o = o_ref[rows, :]
            xc = o - jnp.mean(o, axis=-1, keepdims=True)
            rs = lax.rsqrt(jnp.mean(xc * xc, axis=-1, keepdims=True) + EPS)
            nrm = xc * rs
            rg = rg_ref[rows, :].astype(F32)
            sg = _sigmoid(rg)
            dg = dgro_ref[rows, :].astype(F32)
            drg_ref[rows, :] = (dg * nrm * sg * (1.0 + rg * (1.0 - sg))).astype(BF16)
            dn = dg * rg * sg
            do = rs * (dn - jnp.mean(dn, axis=-1, keepdims=True)
                       - nrm * jnp.mean(dn * nrm, axis=-1, keepdims=True))
            do_ref[rows, :] = do.astype(BF16)

        dstate = jnp.zeros((RET_KEY_DIM, RET_VAL_DIM), F32)
        for i in reversed(range(nt)):
            rows = slice(i * RET_TILE, (i + 1) * RET_TILE)
            qi, ki, vi, doi = qr_ref[rows, :], kr_ref[rows, :], v_ref[rows, :], do_ref[rows, :]
            p = (_dot_nt(qi, ki) * inside).astype(BF16)
            dp = (_dot_nt(doi, vi) * inside).astype(BF16)
            dq = _dot(dp, ki)
            dk = _dot_tn(dp, qi)
            dv = _dot_tn(p, doi)
            if i > 0:
                dq = dq + _dot_nt(doi, st_ref[i]) * q_dec
            if i < nt - 1:
                dsb = dstate.astype(BF16)
                dk = dk + _dot_nt(vi, dsb) * k_dec
                dv = dv + _dot(_scaled(ki, k_dec), dsb)
            if i > 0:
                dstate = dstate * tile_dec + _dot_tn(_scaled(qi, q_dec), doi)
            dq_ref[rows, :] = (dq * cs_ref[rows, :] - pltpu.roll(dq, 64, 1) * sn_ref[rows, :]).astype(BF16)
            dk = (dk * cs_ref[rows, :] - pltpu.roll(dk, 64, 1) * sn_ref[rows, :]) * (RET_KEY_DIM ** -0.5)
            dk_ref[rows, :] = dk.astype(BF16)
            dv_ref[rows, :] = dv.astype(BF16)

    key = pl.BlockSpec((seq, RET_KEY_DIM), lambda b, h: (b, h))
    val = pl.BlockSpec((seq, RET_VAL_DIM), lambda b, h: (b, h))
    tab = pl.BlockSpec((seq, RET_KEY_DIM), lambda b, h: (0, 0))
    return _call(
        body, name="ret_bwd", grid=(batch, RET_HEADS),
        in_specs=[val, pl.BlockSpec((seq, RET_VAL_DIM), lambda b, h: (b, C_RG // RET_VAL_DIM + h)), val, key, key,
                  pl.BlockSpec((seq, RET_VAL_DIM), lambda b, h: (b, C_RV // RET_VAL_DIM + h)), tab, tab,
                  pl.BlockSpec((None, 1, LANES), lambda b, h: (h, 0, 0))],
        out_specs=[key, key, val, val],
        out_shape=[jax.ShapeDtypeStruct((t, RET_HEADS * RET_KEY_DIM), BF16)] * 2
                  + [jax.ShapeDtypeStruct((t, RET_HEADS * RET_VAL_DIM), BF16)] * 2,
        scratch=[pltpu.VMEM((seq, RET_VAL_DIM), BF16), pltpu.VMEM((nt, RET_KEY_DIM, RET_VAL_DIM), BF16)],
        semantics=("parallel", "parallel"), args=(dgro, proj, o_ret, qr, kr, proj, cs, sn, lg_arr),
        exchange=exchange)


def _att_bwd(proj, bias, dao, batch, seq, exchange):
    ni, q_spec, k_spec, v_spec, w_spec, b_spec, pad = _att_specs(batch, seq)
    t = batch * seq

    def body(q_ref, k_ref, v_ref, bias_ref, do_ref, dq_ref, dk_ref, dv_ref, dw_ref,
             dbias_ref, dk_acc, dv_acc, kp_ref, vp_ref, s_ref, dp_ref, e_ref, ds_ref):
        b, i = pl.program_id(1), pl.program_id(2)

        @pl.when((b == 0) & (i == 0))
        def _():
            dbias_ref[...] = jnp.zeros_like(dbias_ref)

        @pl.when(i == 0)
        def _():
            _att_pad(k_ref, kp_ref)
            _att_pad(v_ref, vp_ref)
            dk_acc[...] = jnp.zeros_like(dk_acc)
            dv_acc[...] = jnp.zeros_like(dv_acc)

        win = pl.ds(pl.multiple_of(i * ATT_Q, ATT_Q), ATT_WIN)
        k2, v2, q2, do2 = kp_ref[win, :], vp_ref[win, :], q_ref[...], do_ref[...]
        lo = lax.broadcasted_iota(jnp.int32, (1, LANES), 1) < 64
        dq = jnp.zeros((ATT_Q, LANES), F32)
        dk = jnp.zeros((LANES, ATT_WIN), F32)
        dv = jnp.zeros((LANES, ATT_WIN), F32)
        start = jnp.minimum(i, ATT_STARTS)
        for e in range(2):
            sel = lo if e == 0 else jnp.logical_not(lo)
            qm = _att_head(q2, sel)
            dom = jnp.where(sel, do2, jnp.zeros_like(do2))
            s_ref[e] = _dot_nt(qm, k2)
            dp_ref[e] = _dot_nt(dom, v2)
            rsum = []
            for c in range(ATT_Q // ATT_ROWS):
                rows = slice(c * ATT_ROWS, (c + 1) * ATT_ROWS)
                ex, r = _att_softmax_rows(s_ref.at[e], bias_ref.at[start, e], rows)
                dp = dp_ref[e, rows, :]
                mean = jnp.sum(dp * ex, axis=-1, keepdims=True) * r
                ds = ex * ((dp - mean) * r)
                dbias_ref[e, rows, :] += ds
                ds_ref[e, rows, :] = ds.astype(BF16)
                e_ref[e, rows, :] = ex.astype(BF16)
                rsum.append(r)
            dq = dq + _dot(ds_ref[e], jnp.where(sel, k2, jnp.zeros_like(k2)))
            dk = dk + _dot_tn(qm, ds_ref[e])
            dv = dv + _dot_tn((dom.astype(F32) * jnp.concatenate(rsum, axis=0)).astype(BF16), e_ref[e])
        dq_ref[...] = (dq * 0.125).astype(BF16)
        dk_acc[:, win] += dk
        dv_acc[:, win] += dv

        @pl.when(i == ni - 1)
        def _():
            dk_ref[...] = dk_acc[:, ATT_PAD:].T.astype(BF16)
            dv_ref[...] = dv_acc[:, ATT_PAD:].T.astype(BF16)

        @pl.when((b == batch - 1) & (i == ni - 1))
        def _():
            n_i = lax.broadcasted_iota(jnp.int32, (ATT_Q, BIAS_LEN), 0)
            for e in range(2):
                xw = jnp.concatenate([jnp.zeros((ATT_Q, BIAS_LEN - ATT_WIN), F32), dbias_ref[e]], axis=1)
                for bit in range(8):
                    xw = jnp.where(((n_i >> bit) & 1) == 1, pltpu.roll(xw, BIAS_LEN - (1 << bit), 1), xw)
                dw_ref[e:e + 1, :] = jnp.sum(xw, axis=0, keepdims=True)

    seq_blk = pl.BlockSpec((seq, LANES), lambda hp, b, i: (b, hp))
    q_out = pl.BlockSpec((ATT_Q, LANES), lambda hp, b, i: (b * ni + i, hp))
    return _call(
        body, name="att_bwd", grid=(ATT_HEADS // 2, batch, ni),
        in_specs=[q_spec, k_spec, v_spec, b_spec, q_out],
        out_specs=[q_out, seq_blk, seq_blk, w_spec],
        out_shape=[jax.ShapeDtypeStruct((t, 512), BF16)] * 3
                  + [jax.ShapeDtypeStruct((ATT_HEADS // 2, 2, BIAS_LEN), F32)],
        scratch=[pltpu.VMEM((2, ATT_Q, ATT_WIN), F32),
                 pltpu.VMEM((LANES, seq + ATT_PAD), F32), pltpu.VMEM((LANES, seq + ATT_PAD), F32), pad, pad,
                 pltpu.VMEM((2, ATT_Q, ATT_WIN), F32), pltpu.VMEM((2, ATT_Q, ATT_WIN), F32),
                 pltpu.VMEM((2, ATT_Q, ATT_WIN), BF16), pltpu.VMEM((2, ATT_Q, ATT_WIN), BF16)],
        semantics=("arbitrary", "arbitrary", "arbitrary"), args=(proj, proj, proj, bias, dao), exchange=exchange)


def _rms_in_bwd(x2, dxn, dh1, g1):
    t = x2.shape[0]
    tm = min(512, t)

    def body(x_ref, d_ref, h_ref, g_ref, dx_ref, dg_ref):
        @pl.when(pl.program_id(0) == 0)
        def _():
            dg_ref[...] = jnp.zeros_like(dg_ref)

        dx, dg_rows = _rms_bwd(x_ref[...], g_ref[...], d_ref[...])
        dx_ref[...] = h_ref[...] + dx
        dg_ref[...] += jnp.sum(dg_rows, axis=0, keepdims=True)

    row = pl.BlockSpec((tm, D_MODEL), lambda i: (i, 0))
    vec = pl.BlockSpec((1, D_MODEL), lambda i: (0, 0))
    return pl.pallas_call(
        body, name="rms_in_bwd", grid=(t // tm,),
        in_specs=[row, row, row, vec], out_specs=[row, vec],
        out_shape=[jax.ShapeDtypeStruct((t, D_MODEL), F32), jax.ShapeDtypeStruct((1, D_MODEL), F32)],
        compiler_params=_params("arbitrary"),
    )(x2, dxn, dh1, g1)


def _pack_small(dg1, dbr, dba, dg2, dg3, dw, loss):
    def body(a_ref, b_ref, c_ref, d_ref, e_ref, w_ref, l_ref, o_ref):
        o_ref[...] = jnp.zeros_like(o_ref)
        for r, ref in enumerate((a_ref, b_ref, c_ref, d_ref, e_ref)):
            o_ref[r:r + 1, :] = ref[...]
        o_ref[5:6, 0:LANES] = l_ref[...]
        for hp in range(ATT_HEADS // 2):
            o_ref[8 + 2 * hp:10 + 2 * hp, :] = w_ref[hp]

    return pl.pallas_call(body, name="pack_small",
                          out_shape=jax.ShapeDtypeStruct((16, D_MODEL), F32))(dg1, dbr, dba, dg2, dg3, dw, loss)


def _rotary_tables(seq):
    freqs = ROPE_BASE ** (-jnp.arange(0, RET_KEY_DIM, 2, dtype=F32) / RET_KEY_DIM)
    ang = jnp.arange(seq, dtype=F32)[:, None] * freqs[None, :]
    cos, sin = jnp.cos(ang), jnp.sin(ang)
    return jnp.concatenate([cos, cos], axis=1), jnp.concatenate([-sin, sin], axis=1)


def _bias_rows(rel_bias):
    n_far = BIAS_LEN - ATT_Q - MAX_REL + 1
    n_near = BIAS_LEN - n_far - (N_REL - 2)
    w = jnp.concatenate([jnp.broadcast_to(rel_bias[:, N_REL - 1:], (ATT_HEADS, n_far)),
                         rel_bias[:, 1:N_REL - 1][:, ::-1],
                         jnp.broadcast_to(rel_bias[:, :1], (ATT_HEADS, n_near))], axis=1)
    return w.reshape(ATT_HEADS // 2, 2, BIAS_LEN)


def _bias_rows_bwd(dw):
    n_far = BIAS_LEN - ATT_Q - MAX_REL + 1
    mid = dw[:, n_far:n_far + N_REL - 2][:, ::-1]
    return jnp.concatenate([jnp.sum(dw[:, n_far + N_REL - 2:], axis=1, keepdims=True), mid,
                            jnp.sum(dw[:, :n_far], axis=1, keepdims=True)], axis=1)


def _step(x, tgt, norm_mix, b_gate, norm_ffn, norm_final, rel_bias_shard, shard):
    batch, seq, _ = x.shape
    t = batch * seq
    n_rb = rel_bias_shard.shape[-1]
    x2, tgt2 = x.reshape(t, D_MODEL), tgt.reshape(t, D_MODEL)
    g3 = norm_final.reshape(1, D_MODEL)
    cs, sn = _rotary_tables(seq)
    lg = np.log(1.0 - 2.0 ** (-5.0 - np.arange(RET_HEADS, dtype=np.float32))).astype(np.float32)
    lg_arr = jnp.asarray(np.broadcast_to(lg[:, None, None], (RET_HEADS, 1, LANES)))

    def gather(*names):
        return _ChipGather([shard[nm] for nm in names])

    def scatter(*grads):
        return _Exchange(grads, scatter=True)

    rb_pad = jnp.pad(rel_bias_shard, ((0, 0), (0, LANES - n_rb)))
    (xn,), (w_in_t, rb_full) = _rms_fwd(x2, norm_mix, _ChipGather([shard["w_in_t"], rb_pad]))
    rb_full = rb_full.reshape(N_DEV, ATT_HEADS, LANES)[:, :, :n_rb]
    bias = _att_bias_tiles(_bias_rows(jnp.transpose(rb_full, (1, 0, 2)).reshape(ATT_HEADS, N_DEV * n_rb)))
    proj, (w_ret, w_att_t, w_out, w_gate_t) = _mm(
        xn, w_in_t, tb=True, out_dtype=BF16, tm=1024, tn=1664, tk=1024, name="proj",
        exchange=gather("w_ret", "w_att_t", "w_out", "w_gate_t"))
    (gro, o_ret, qr, kr), _ = _ret_fwd(proj, cs, sn, lg_arr, batch, seq, None)
    (ao,), (w_up_t, w_down) = _att_fwd(proj, bias, batch, seq, gather("w_up_t", "w_down"))
    z, y_ret, y_att = _mix_fwd(gro, ao, proj, b_gate, w_ret, w_att_t)
    h1, hn = _out_fwd(z, x2, w_out, norm_ffn)
    g_act, u_act, a_act = _ffn_up(hn, w_gate_t, w_up_t)
    dh2, dh2b, loss, dg3 = _ffn_down_loss(a_act, h1, tgt2, w_down, g3)

    wg = dict(out_dtype=BF16, tn=1024, ta=True)
    slots = {}
    dw_down = _mm(a_act, dh2b, tm=1408, tk=1024, name="dw_down", **wg)
    (d_gact, d_uact), _ = _ffn_bwd_act(dh2b, w_down, g_act, u_act, None)
    dw_gate = _mm(d_gact, hn, tm=1408, tk=1024, name="dw_gate", **wg)
    dw_up = _mm(d_uact, hn, tm=1408, tk=1024, name="dw_up", **wg)
    (dh1, dh1b, dg2), (slots["w_down"],) = _ffn_bwd_in(d_gact, d_uact, w_gate_t, w_up_t, h1, dh2, norm_ffn,
                                                     scatter(dw_down))
    dw_out = _mm(z, dh1b, tm=1024, tk=2048, name="dw_out", **wg)
    (dyr, dya, dglr, dgla, dbr, dba), _ = _mix_bwd(dh1b, w_out, proj, b_gate, y_ret, y_att, None)
    dgro = _mm(dyr, w_ret, tb=True, out_dtype=BF16, tm=1024, tn=1024, tk=1024, name="dgro")
    dao = _mm(dya, w_att_t, out_dtype=BF16, tm=1024, tn=512, tk=1024, name="dao")
    dw_ret = _mm(gro, dyr, tm=1024, tk=2048, name="dw_ret", **wg)
    dw_att = _mm(dya, ao, tm=1024, tk=2048, name="dw_att", **wg)
    (drq, drk, drv, drg), _ = _ret_bwd(dgro, proj, o_ret, qr, kr, cs, sn, lg_arr, batch, seq, None)
    (daq, dak, dav, dw), (slots["w_gate_t"], slots["w_out"], slots["w_ret"], slots["w_att_t"]) = _att_bwd(
        proj, bias, dao, batch, seq, scatter(dw_gate, dw_out, dw_ret, dw_att))
    dproj = [drq, drk, drv, drg, daq, dak, dav, dglr, dgla]
    dw_in, (slots["w_up_t"],) = _mm_pieces(dproj, xn, ta=True, out_dtype=BF16, tm=512, tn=1024, tk=1024,
                                           name="dw_in", exchange=scatter(dw_up))
    (dw_in_sibling,) = _alone(_PairSwap([dw_in]), "swap_w_in")
    dw_in_pairs = _pair_add(dw_in, dw_in_sibling, "pair_w_in")
    dxn, (slots["w_in_t"],) = _mm_pieces(dproj, w_in_t, ta=False, out_dtype=F32, tm=1024, tn=1024, tk=512, name="dxn",
                                  exchange=_ChipScatter([dw_in_pairs]))
    dx, dg1 = _rms_in_bwd(x2, dxn, dh1, norm_mix)
    small = _pack_small(dg1, dbr, dba, dg2, dg3, dw, loss)
    (small_slots,) = _alone(_ChipGather([small]), "gather_small")
    return dx.reshape(batch, seq, D_MODEL), slots, small_slots.reshape(N_DEV, 16, D_MODEL)


def _row_tile(r, c):
    return max(d for d in range(16, r + 1, 16) if r % d == 0 and (d * c <= 256 * 1024 or d == 16))


def _pair_add(grad, got, name):
    _, r, c = got.shape
    tr = _row_tile(r, c)
    core = lax.axis_index("c").astype(jnp.int32).reshape(1)

    def body(core_ref, g_ref, a_ref, o_ref):
        o_ref[...] = (g_ref[...].astype(F32) + a_ref[...].astype(F32)).astype(o_ref.dtype)

    blk = pl.BlockSpec((None, tr, c), lambda q, i, core_ref: (q, i, 0))
    return pl.pallas_call(
        body, name=name,
        grid_spec=pltpu.PrefetchScalarGridSpec(
            num_scalar_prefetch=1, grid=(4, r // tr),
            in_specs=[pl.BlockSpec((None, None, tr, c), lambda q, i, core_ref: (q, core_ref[0], i, 0)), blk],
            out_specs=blk),
        out_shape=jax.ShapeDtypeStruct(got.shape, got.dtype),
        compiler_params=_params("parallel", "parallel"),
    )(core, grad.reshape(4, 2, r, c), got)


def _sum_slots(slots, name):
    n, r, c = slots.shape
    tr = _row_tile(r, c)

    def body(s_ref, o_ref):
        acc = s_ref[0].astype(F32)
        for s in range(1, n):
            acc = acc + s_ref[s].astype(F32)
        o_ref[...] = acc

    return pl.pallas_call(
        body, name=name, grid=(r // tr,),
        in_specs=[pl.BlockSpec((n, tr, c), lambda i: (0, i, 0))],
        out_specs=pl.BlockSpec((tr, c), lambda i: (i, 0)),
        out_shape=jax.ShapeDtypeStruct((r, c), F32),
        compiler_params=_params("parallel"),
    )(slots)


def _adamw_math(w, g, m, v):
    m = ADAM_B1 * m + (1.0 - ADAM_B1) * g
    v = ADAM_B2 * v + (1.0 - ADAM_B2) * (g * g)
    m_hat = m / (1.0 - ADAM_B1 ** ADAM_STEP)
    v_hat = v / (1.0 - ADAM_B2 ** ADAM_STEP)
    return -ADAM_LR * (m_hat / (jnp.sqrt(v_hat) + ADAM_EPS) + ADAM_WD * w), m, v


def _adamw(w, slots, m, v, name):
    n, r, c = slots.shape
    tr = _row_tile(r, c)

    def body(w_ref, s_ref, m_ref, v_ref, g_ref, d_ref, nm_ref, nv_ref):
        g = s_ref[0].astype(F32)
        for s in range(1, n):
            g = g + s_ref[s].astype(F32)
        g_ref[...] = g
        d_ref[...], nm_ref[...], nv_ref[...] = _adamw_math(w_ref[...], g, m_ref[...], v_ref[...])

    blk = pl.BlockSpec((tr, c), lambda i: (i, 0))
    return pl.pallas_call(
        body, name=name, grid=(r // tr,),
        in_specs=[blk, pl.BlockSpec((n, tr, c), lambda i: (0, i, 0)), blk, blk], out_specs=[blk] * 4,
        out_shape=[jax.ShapeDtypeStruct((r, c), F32)] * 4,
        compiler_params=_params("parallel"),
    )(w, slots, m, v)


def _adamw_small(ws, gs, ms, vs):
    n = len(ws)

    def body(*refs):
        for i in range(n):
            w_ref, g_ref, m_ref, v_ref = (refs[j * n + i] for j in range(4))
            d_ref, nm_ref, nv_ref = (refs[(4 + j) * n + i] for j in range(3))
            d_ref[...], nm_ref[...], nv_ref[...] = _adamw_math(w_ref[...], g_ref[...], m_ref[...], v_ref[...])

    shapes = [jax.ShapeDtypeStruct(w.shape, F32) for w in ws]
    outs = pl.pallas_call(body, name="adamw_small", out_shape=shapes * 3)(*ws, *gs, *ms, *vs)
    return outs[:n], outs[n:2 * n], outs[2 * n:]


def kernel(x, norm_mix, w_in, b_gate, rel_bias, w_ret_out, w_att_out, w_out, norm_ffn, w_ffn_gate, w_ffn_up, w_ffn_down, norm_final, loss_target, m_norm_mix, m_w_in, m_b_gate, m_rel_bias, m_w_ret_out, m_w_att_out, m_w_out, m_norm_ffn, m_w_ffn_gate, m_w_ffn_up, m_w_ffn_down, m_norm_final, v_norm_mix, v_w_in, v_b_gate, v_rel_bias, v_w_ret_out, v_w_att_out, v_w_out, v_norm_ffn, v_w_ffn_gate, v_w_ffn_up, v_w_ffn_down, v_norm_final):
    me = _index(_place())
    n_rb = rel_bias.shape[-1]

    shard = dict(w_in_t=w_in[0].T, w_gate_t=w_ffn_gate[0].T, w_up_t=w_ffn_up[0].T, w_down=w_ffn_down[0],
                 w_ret=w_ret_out[0], w_out=w_out[0], w_att_t=w_att_out[0].T)
    shard = {nm: s.astype(BF16) for nm, s in shard.items()}
    dx, slots, small_slots = _step(x, loss_target, norm_mix, b_gate, norm_ffn, norm_final, rel_bias[0], shard)
    small_sum = _sum_slots(small_slots, "sum_small")
    loss = small_sum[5, 0]

    transposed = dict(w_in="w_in_t", w_ffn_gate="w_gate_t", w_ffn_up="w_up_t", w_att_out="w_att_t")
    plain = dict(w_ffn_down="w_down", w_ret_out="w_ret", w_out="w_out")
    g = dict(
        norm_mix=small_sum[0:1], b_gate=jnp.concatenate([small_sum[1:2], small_sum[2:3]], axis=1),
        norm_ffn=small_sum[3:4], norm_final=small_sum[4:5],
        rel_bias=lax.dynamic_slice_in_dim(_bias_rows_bwd(small_sum[8:16]), me * n_rb, n_rb, axis=1),
    )
    w = dict(norm_mix=norm_mix, w_in=w_in, b_gate=b_gate, rel_bias=rel_bias, w_ret_out=w_ret_out, w_att_out=w_att_out,
             w_out=w_out, norm_ffn=norm_ffn, w_ffn_gate=w_ffn_gate, w_ffn_up=w_ffn_up, w_ffn_down=w_ffn_down,
             norm_final=norm_final)
    m = dict(norm_mix=m_norm_mix, w_in=m_w_in, b_gate=m_b_gate, rel_bias=m_rel_bias, w_ret_out=m_w_ret_out,
             w_att_out=m_w_att_out, w_out=m_w_out, norm_ffn=m_norm_ffn, w_ffn_gate=m_w_ffn_gate, w_ffn_up=m_w_ffn_up,
             w_ffn_down=m_w_ffn_down, norm_final=m_norm_final)
    v = dict(norm_mix=v_norm_mix, w_in=v_w_in, b_gate=v_b_gate, rel_bias=v_rel_bias, w_ret_out=v_w_ret_out,
             w_att_out=v_w_att_out, w_out=v_w_out, norm_ffn=v_norm_ffn, w_ffn_gate=v_w_ffn_gate, w_ffn_up=v_w_ffn_up,
             w_ffn_down=v_w_ffn_down, norm_final=v_norm_final)
    order = ("norm_mix", "w_in", "b_gate", "rel_bias", "w_ret_out", "w_att_out", "w_out", "norm_ffn",
             "w_ffn_gate", "w_ffn_up", "w_ffn_down", "norm_final")
    small_names = ("norm_mix", "b_gate", "rel_bias", "norm_ffn", "norm_final")

    def flat(a):
        return a[0] if a.ndim == 3 else a.reshape(-1, a.shape[-1])

    grad, delta, new_m, new_v = {}, {}, {}, {}
    for nm in order:
        if nm in transposed:
            res = _adamw(w[nm][0].T, slots[transposed[nm]], m[nm][0].T, v[nm][0].T, "adamw_" + nm)
            grad[nm], delta[nm], new_m[nm], new_v[nm] = (a.T[None] for a in res)
        elif nm in plain:
            res = _adamw(flat(w[nm]), slots[plain[nm]], flat(m[nm]), flat(v[nm]), "adamw_" + nm)
            grad[nm], delta[nm], new_m[nm], new_v[nm] = (a.reshape(w[nm].shape) for a in res)
    ds, nms, nvs = _adamw_small([flat(w[nm]) for nm in small_names], [g[nm] for nm in small_names],
                                [flat(m[nm]) for nm in small_names], [flat(v[nm]) for nm in small_names])
    for i, nm in enumerate(small_names):
        grad[nm], delta[nm], new_m[nm], new_v[nm] = (a.reshape(w[nm].shape) for a in (g[nm], ds[i], nms[i], nvs[i]))

    return (loss, dx, *[grad[nm] for nm in order], *[delta[nm] for nm in order],
            *[new_m[nm] for nm in order], *[new_v[nm] for nm in order])
```

```python
import numpy as np
import jax
import jax.numpy as jnp
from jax import lax
from jax.experimental import pallas as pl
from jax.experimental.pallas import tpu as pltpu

F32 = jnp.float32
BF16 = jnp.bfloat16
MESH = pl.DeviceIdType.MESH

D_MODEL = 1024
CHUNK = 64
RET_HEADS = 4
RET_KEY_DIM = 128
RET_VAL_DIM = 256
ATT_HEADS = 8
BAND_CHUNKS = 8
MAX_REL = 256
N_REL = CHUNK + MAX_REL
D_FF = 2816
N_IN = 6656
ROPE_BASE = 10000.0
EPS = 1e-6
NEG_INF = -1e30
C_RQ, C_RK, C_RV, C_RG, C_AQ, C_AK, C_AV, C_GL = 0, 512, 1024, 2048, 3072, 3584, 4096, 4608

ADAM_LR = 0.001
ADAM_B1 = 0.9
ADAM_B2 = 0.999
ADAM_EPS = 1e-08
ADAM_WD = 0.01
ADAM_STEP = 10

N_DEV = 8
LANES = 128
RET_TILE = 256
ATT_Q = 256
ATT_PAD = BAND_CHUNKS * CHUNK
ATT_WIN = ATT_PAD + ATT_Q
ATT_STARTS = ATT_PAD // ATT_Q
ATT_ROWS = 32
BIAS_LEN = 1024
VMEM_LIMIT = 48 * 1024 * 1024


def _params(*sem):
    return pltpu.CompilerParams(dimension_semantics=sem, vmem_limit_bytes=VMEM_LIMIT)


def _dot(a, b):
    return lax.dot_general(a, b, (((1,), (0,)), ((), ())), preferred_element_type=F32)


def _dot_nt(a, b):
    return lax.dot_general(a, b, (((1,), (1,)), ((), ())), preferred_element_type=F32)


def _dot_tn(a, b):
    return lax.dot_general(a, b, (((0,), (0,)), ((), ())), preferred_element_type=F32)


def _sigmoid(x):
    return 1.0 / (1.0 + jnp.exp(-x))


def _rms_bwd(x, g, dy):
    r = lax.rsqrt(jnp.mean(x * x, axis=-1, keepdims=True) + EPS)
    u = dy * g
    dx = r * u - x * (r * r * r) * jnp.mean(u * x, axis=-1, keepdims=True)
    return dx, dy * x * r


def _place():
    return lax.axis_index("x"), lax.axis_index("y"), lax.axis_index("c")


def _peer(k):
    x, y, c = _place()
    return ((1 - x) if k & 4 else x, (1 - y) if k & 2 else y, (1 - c) if k & 1 else c)


def _index(place):
    return 4 * place[0] + 2 * place[1] + place[2]


def _rows(ref, block, nrows):
    align = 16 if ref.dtype == BF16 else 8
    return ref.at[pl.ds(pl.multiple_of(block * nrows, align), nrows)]


class _Exchange:
    def __init__(self, arrays, scatter):
        self.arrays, self.scatter, self.n = list(arrays), scatter, len(arrays)

    def out_shape(self):
        if self.scatter:
            return [jax.ShapeDtypeStruct((N_DEV, a.shape[0] // N_DEV) + a.shape[1:], a.dtype) for a in self.arrays]
        return [jax.ShapeDtypeStruct((N_DEV * a.shape[0],) + a.shape[1:], a.dtype) for a in self.arrays]

    def scratch(self):
        return [pltpu.SemaphoreType.DMA((self.n, N_DEV - 1)), pltpu.SemaphoreType.DMA((self.n, N_DEV - 1)),
                pltpu.SemaphoreType.DMA((self.n,))]

    def _copies(self, ins, outs, sems):
        send_sems, recv_sems, local_sems = sems
        me = _index(_place())

        def src(w, to):
            return _rows(ins[w], to, ins[w].shape[0] // N_DEV) if self.scatter else ins[w]

        def dst(w, origin):
            return outs[w].at[origin] if self.scatter else _rows(outs[w], origin, ins[w].shape[0])

        def remote(w, k, to, origin):
            return pltpu.make_async_remote_copy(src_ref=src(w, to), dst_ref=dst(w, origin),
                                                send_sem=send_sems.at[w, k - 1], recv_sem=recv_sems.at[w, k - 1],
                                                device_id=_peer(k), device_id_type=MESH)

        pairs = [(w, k) for w in range(self.n) for k in range(1, N_DEV)]
        own = lambda: [pltpu.make_async_copy(src(w, me), dst(w, me), local_sems.at[w]) for w in range(self.n)]
        sent = lambda: [remote(w, k, _index(_peer(k)), me) for w, k in pairs]
        arriving = lambda: [remote(w, k, me, _index(_peer(k))) for w, k in pairs]
        return own, sent, arriving

    def start(self, ins, outs, sems):
        own, sent, _ = self._copies(ins, outs, sems)
        for cp in own() + sent():
            cp.start()

    def wait(self, ins, outs, sems):
        own, sent, arriving = self._copies(ins, outs, sems)
        for cp in arriving():
            cp.wait_recv()
        for cp in sent():
            cp.wait_send()
        for cp in own():
            cp.wait()


class _PairSwap:
    def __init__(self, arrays):
        self.arrays, self.n = list(arrays), len(arrays)

    def out_shape(self):
        return [jax.ShapeDtypeStruct((4, a.shape[0] // N_DEV) + a.shape[1:], a.dtype) for a in self.arrays]

    def scratch(self):
        return [pltpu.SemaphoreType.DMA((self.n, 4)), pltpu.SemaphoreType.DMA((self.n, 4))]

    def _copies(self, ins, outs, sems):
        send_sems, recv_sems = sems
        x, y, c = _place()
        return [pltpu.make_async_remote_copy(
            src_ref=_rows(ins[w], 2 * q + 1 - c, ins[w].shape[0] // N_DEV), dst_ref=outs[w].at[q],
            send_sem=send_sems.at[w, q], recv_sem=recv_sems.at[w, q],
            device_id=(x, y, 1 - c), device_id_type=MESH) for w in range(self.n) for q in range(4)]

    def start(self, ins, outs, sems):
        for cp in self._copies(ins, outs, sems):
            cp.start()

    def wait(self, ins, outs, sems):
        for cp in self._copies(ins, outs, sems):
            cp.wait()


class _ChipScatter:
    def __init__(self, arrays):
        self.arrays, self.n = list(arrays), len(arrays)

    def out_shape(self):
        return [jax.ShapeDtypeStruct(a.shape, a.dtype) for a in self.arrays]

    def scratch(self):
        return [pltpu.SemaphoreType.DMA((self.n, 3)), pltpu.SemaphoreType.DMA((self.n, 3)),
                pltpu.SemaphoreType.DMA((self.n,))]

    def _copies(self, ins, outs, sems):
        send_sems, recv_sems, local_sems = sems
        x, y, c = _place()
        mine = 2 * x + y
        sent, arriving = [], []
        for w in range(self.n):
            for k in range(1, 4):
                tx, ty = (1 - x) if k & 2 else x, (1 - y) if k & 1 else y
                other = 2 * tx + ty
                sent.append(lambda w=w, k=k, tx=tx, ty=ty, other=other: pltpu.make_async_remote_copy(
                    src_ref=ins[w].at[other], dst_ref=outs[w].at[mine], send_sem=send_sems.at[w, k - 1],
                    recv_sem=recv_sems.at[w, k - 1], device_id=(tx, ty, c), device_id_type=MESH))
                arriving.append(lambda w=w, k=k, tx=tx, ty=ty, other=other: pltpu.make_async_remote_copy(
                    src_ref=ins[w].at[mine], dst_ref=outs[w].at[other], send_sem=send_sems.at[w, k - 1],
                    recv_sem=recv_sems.at[w, k - 1], device_id=(tx, ty, c), device_id_type=MESH))
        own = [lambda w=w: pltpu.make_async_copy(ins[w].at[mine], outs[w].at[mine], local_sems.at[w])
               for w in range(self.n)]
        return own, sent, arriving

    def start(self, ins, outs, sems):
        own, sent, _ = self._copies(ins, outs, sems)
        for cp in own + sent:
            cp().start()

    def wait(self, ins, outs, sems):
        own, sent, arriving = self._copies(ins, outs, sems)
        for cp in arriving:
            cp().wait_recv()
        for cp in sent:
            cp().wait_send()
        for cp in own:
            cp().wait()


class _ChipGather:
    def __init__(self, arrays):
        self.arrays, self.n = list(arrays), len(arrays)

    def out_shape(self):
        return [jax.ShapeDtypeStruct((N_DEV * a.shape[0],) + a.shape[1:], a.dtype) for a in self.arrays]

    def scratch(self):
        return [pltpu.SemaphoreType.DMA((self.n, N_DEV - 1)), pltpu.SemaphoreType.DMA((self.n, N_DEV - 1)),
                pltpu.SemaphoreType.DMA((self.n,))]

    def _parts(self, ins, outs, sems):
        send_sems, recv_sems, local_sems = sems
        x, y, c = _place()
        me, sibling = (x, y, c), (x, y, 1 - c)
        chips = [(1 - x, y), (x, 1 - y), (1 - x, 1 - y)]

        def rows(w, place):
            return _rows(outs[w], _index(place), ins[w].shape[0])

        def copy(w, k, block, to, own=False):
            return pltpu.make_async_remote_copy(src_ref=ins[w] if own else rows(w, block), dst_ref=rows(w, block),
                                                send_sem=send_sems.at[w, k], recv_sem=recv_sems.at[w, k],
                                                device_id=to, device_id_type=MESH)

        def local(w):
            return pltpu.make_async_copy(ins[w], rows(w, me), local_sems.at[w])

        return me, sibling, chips, c, copy, local

    def start(self, ins, outs, sems):
        me, sibling, chips, c, copy, local = self._parts(ins, outs, sems)
        for w in range(self.n):
            local(w).start()
            copy(w, 0, me, sibling, own=True).start()
            for j, chip in enumerate(chips):
                copy(w, 1 + j, me, (*chip, c), own=True).start()

    def wait(self, ins, outs, sems):
        me, sibling, chips, c, copy, local = self._parts(ins, outs, sems)
        for w in range(self.n):
            for j, chip in enumerate(chips):
                copy(w, 1 + j, (*chip, c), me).wait_recv()
                copy(w, 4 + j, (*chip, c), sibling).start()
        for w in range(self.n):
            copy(w, 0, sibling, me).wait_recv()
            for j, chip in enumerate(chips):
                copy(w, 4 + j, (*chip, 1 - c), me).wait_recv()
            copy(w, 0, me, sibling, own=True).wait_send()
            for j, chip in enumerate(chips):
                copy(w, 1 + j, me, (*chip, c), own=True).wait_send()
                copy(w, 4 + j, (*chip, c), sibling).wait_send()
            local(w).wait()


def _call(body, *, name, grid, in_specs, out_specs, out_shape, scratch=(), semantics, args, exchange=None):
    if exchange is None:
        return pl.pallas_call(body, name=name, grid=grid, in_specs=in_specs, out_specs=out_specs, out_shape=out_shape,
                              scratch_shapes=list(scratch), compiler_params=_params(*semantics))(*args), None
    n_in, n_out, n_scr, nx = len(in_specs), len(out_specs), len(scratch), exchange.n

    def full_body(*refs):
        ins, refs = refs[:n_in], refs[n_in:]
        x_in, refs = refs[:nx], refs[nx:]
        outs, refs = refs[:n_out], refs[n_out:]
        x_out, refs = refs[:nx], refs[nx:]
        scr, sems = refs[:n_scr], refs[n_scr:]
        first, last = True, True
        for axis, size in enumerate(grid):
            first = jnp.logical_and(first, pl.program_id(axis) == 0)
            last = jnp.logical_and(last, pl.program_id(axis) == size - 1)
        if grid:
            pl.when(first)(lambda: exchange.start(x_in, x_out, sems))
        else:
            exchange.start(x_in, x_out, sems)
        body(*ins, *outs, *scr)
        if grid:
            pl.when(last)(lambda: exchange.wait(x_in, x_out, sems))
        else:
            exchange.wait(x_in, x_out, sems)

    hbm = pl.BlockSpec(memory_space=pltpu.HBM)
    res = pl.pallas_call(
        full_body, name=name, grid=grid,
        in_specs=list(in_specs) + [hbm] * nx, out_specs=list(out_specs) + [hbm] * nx,
        out_shape=list(out_shape) + exchange.out_shape(),
        scratch_shapes=list(scratch) + exchange.scratch(),
        compiler_params=_params(*(["arbitrary"] * len(grid))),
    )(*args, *exchange.arrays)
    return res[:n_out], res[n_out:]


def _alone(exchange, name):
    return _call(lambda: None, name=name, grid=(), in_specs=[], out_specs=[], out_shape=[], semantics=(),
                 args=(), exchange=exchange)[1]


def _mm(a, b, *, ta=False, tb=False, out_dtype, tm, tn, tk, name, exchange=None):
    m, k = (a.shape[1], a.shape[0]) if ta else a.shape
    n = b.shape[0] if tb else b.shape[1]
    assert k == (b.shape[1] if tb else b.shape[0])
    tm, tn, tk = min(tm, m), min(tn, n), min(tk, k)
    assert m % tm == 0 and n % tn == 0 and k % tk == 0, (name, m, n, k)
    nk = k // tk
    dims = (((0 if ta else 1,), (1 if tb else 0,)), ((), ()))

    def body(a_ref, b_ref, o_ref, *acc):
        prod = lax.dot_general(a_ref[...].astype(BF16), b_ref[...].astype(BF16), dims, preferred_element_type=F32)
        if nk == 1:
            o_ref[...] = prod.astype(o_ref.dtype)
            return
        acc_ref, kk = acc[0], pl.program_id(2)

        @pl.when(kk == 0)
        def _():
            acc_ref[...] = prod

        @pl.when((kk > 0) & (kk < nk - 1))
        def _():
            acc_ref[...] += prod

        @pl.when(kk == nk - 1)
        def _():
            o_ref[...] = (acc_ref[...] + prod).astype(o_ref.dtype)

    a_spec = (pl.BlockSpec((tk, tm), lambda i, j, kk: (kk, i)) if ta
              else pl.BlockSpec((tm, tk), lambda i, j, kk: (i, kk)))
    b_spec = (pl.BlockSpec((tn, tk), lambda i, j, kk: (j, kk)) if tb
              else pl.BlockSpec((tk, tn), lambda i, j, kk: (kk, j)))
    (out,), moved = _call(
        body, name=name, grid=(m // tm, n // tn, nk),
        in_specs=[a_spec, b_spec],
        out_specs=[pl.BlockSpec((tm, tn), lambda i, j, kk: (i, j))],
        out_shape=[jax.ShapeDtypeStruct((m, n), out_dtype)],
        scratch=[pltpu.VMEM((tm, tn), F32)] if nk > 1 else [],
        semantics=("parallel", "parallel", "arbitrary"), args=(a, b), exchange=exchange)
    return out if exchange is None else (out, moved)


def _mm_pieces(pieces, b, *, ta, out_dtype, tm, tn, tk, name, exchange=None):
    rows, n = pieces[0].shape[0], b.shape[1]
    step = tm if ta else tk
    assert all(p.shape[0] == rows and p.shape[1] % step == 0 for p in pieces), name
    edges = [int(e) for e in np.cumsum([0] + [p.shape[1] // step for p in pieces])]
    total = edges[-1] * step
    m, k = (total, rows) if ta else (rows, total)
    assert b.shape[0] == k and m % tm == 0 and n % tn == 0 and k % tk == 0, name
    nk, npieces = k // tk, len(pieces)
    dims = (((0 if ta else 1,), (0,)), ((), ()))

    def body(*refs):
        a_refs, (b_ref, o_ref, acc_ref) = refs[:npieces], refs[npieces:]
        kk = pl.program_id(2)
        pos = pl.program_id(0) if ta else kk

        @pl.when(kk == 0)
        def _():
            acc_ref[...] = jnp.zeros_like(acc_ref)

        for p, a_ref in enumerate(a_refs):
            @pl.when((pos >= edges[p]) & (pos < edges[p + 1]))
            def _(a_ref=a_ref):
                acc_ref[...] += lax.dot_general(a_ref[...], b_ref[...], dims, preferred_element_type=F32)

        @pl.when(kk == nk - 1)
        def _():
            o_ref[...] = acc_ref[...].astype(o_ref.dtype)

    def a_spec(p):
        lo, last = edges[p], edges[p + 1] - edges[p] - 1
        if ta:
            def index(i, j, kk):
                inside = (i >= lo) & (i <= lo + last)
                return jnp.where(inside, kk, 0), jnp.clip(i - lo, 0, last)
            return pl.BlockSpec((tk, tm), index)
        return pl.BlockSpec((tm, tk), lambda i, j, kk: (i, jnp.clip(kk - lo, 0, last)))

    (out,), moved = _call(
        body, name=name, grid=(m // tm, n // tn, nk),
        in_specs=[a_spec(p) for p in range(npieces)] + [pl.BlockSpec((tk, tn), lambda i, j, kk: (kk, j))],
        out_specs=[pl.BlockSpec((tm, tn), lambda i, j, kk: (i, j))],
        out_shape=[jax.ShapeDtypeStruct((m, n), out_dtype)],
        scratch=[pltpu.VMEM((tm, tn), F32)],
        semantics=("parallel", "parallel", "arbitrary"), args=(*pieces, b), exchange=exchange)
    return out if exchange is None else (out, moved)


def _rms_fwd(x2, g, exchange):
    t = x2.shape[0]
    tm = min(512, t)

    def body(x_ref, g_ref, o_ref):
        x = x_ref[...]
        r = lax.rsqrt(jnp.mean(x * x, axis=-1, keepdims=True) + EPS)
        o_ref[...] = (x * r * g_ref[...]).astype(o_ref.dtype)

    return _call(
        body, name="rms_in_fwd", grid=(t // tm,),
        in_specs=[pl.BlockSpec((tm, D_MODEL), lambda i: (i, 0)), pl.BlockSpec((1, D_MODEL), lambda i: (0, 0))],
        out_specs=[pl.BlockSpec((tm, D_MODEL), lambda i: (i, 0))],
        out_shape=[jax.ShapeDtypeStruct((t, D_MODEL), BF16)],
        semantics=("parallel",), args=(x2, g), exchange=exchange)


def _decay(lg):
    row = lax.broadcasted_iota(jnp.int32, (RET_TILE, RET_TILE), 0)
    col = lax.broadcasted_iota(jnp.int32, (RET_TILE, RET_TILE), 1)
    within = jnp.exp(lg * jnp.abs(row - col).astype(F32))
    inside = jnp.where((col >> 6) <= (row >> 6), within, 0.0)
    pos = lax.broadcasted_iota(jnp.int32, (RET_TILE, 1), 0).astype(F32)
    q_dec = jnp.exp(lg * (pos + 1.0))
    k_dec = jnp.exp(lg * (RET_TILE - 1.0 - pos))
    tile_dec = jnp.exp(lg * float(RET_TILE))
    return inside, q_dec, k_dec, tile_dec


def _scaled(a_bf16, dec):
    return (a_bf16.astype(F32) * dec).astype(BF16)


def _ret_fwd(proj, cs, sn, lg_arr, batch, seq, exchange):
    t = batch * seq
    nt = seq // RET_TILE

    def body(q_ref, k_ref, v_ref, rg_ref, cs_ref, sn_ref, lg_ref, gro_ref, o_ref, qr_ref, kr_ref):
        lg = lg_ref[:, 0:1]
        cs_t, sn_t = cs_ref[...], sn_ref[...]
        q = q_ref[...].astype(F32)
        k = k_ref[...].astype(F32)
        qr_ref[...] = (q * cs_t + pltpu.roll(q, 64, 1) * sn_t).astype(BF16)
        kr_ref[...] = ((k * cs_t + pltpu.roll(k, 64, 1) * sn_t) * (RET_KEY_DIM ** -0.5)).astype(BF16)
        inside, q_dec, k_dec, tile_dec = _decay(lg)
        state = jnp.zeros((RET_KEY_DIM, RET_VAL_DIM), F32)
        for i in range(nt):
            rows = slice(i * RET_TILE, (i + 1) * RET_TILE)
            qi, ki, vi = qr_ref[rows, :], kr_ref[rows, :], v_ref[rows, :]
            acc = _dot((_dot_nt(qi, ki) * inside).astype(BF16), vi)
            if i > 0:
                acc = acc + _dot(_scaled(qi, q_dec), state.astype(BF16))
            if i < nt - 1:
                state = state * tile_dec + _dot_tn(_scaled(ki, k_dec), vi)
            o_ref[rows, :] = acc
            xc = acc - jnp.mean(acc, axis=-1, keepdims=True)
            nrm = xc * lax.rsqrt(jnp.mean(xc * xc, axis=-1, keepdims=True) + EPS)
            rg = rg_ref[rows, :].astype(F32)
            gro_ref[rows, :] = (rg * _sigmoid(rg) * nrm).astype(BF16)

    def col(base, width):
        return lambda b, h: (b, base // width + h)

    return _call(
        body, name="ret_fwd", grid=(batch, RET_HEADS),
        in_specs=[pl.BlockSpec((seq, RET_KEY_DIM), col(C_RQ, RET_KEY_DIM)),
                  pl.BlockSpec((seq, RET_KEY_DIM), col(C_RK, RET_KEY_DIM)),
                  pl.BlockSpec((seq, RET_VAL_DIM), col(C_RV, RET_VAL_DIM)),
                  pl.BlockSpec((seq, RET_VAL_DIM), col(C_RG, RET_VAL_DIM)),
                  pl.BlockSpec((seq, RET_KEY_DIM), lambda b, h: (0, 0)),
                  pl.BlockSpec((seq, RET_KEY_DIM), lambda b, h: (0, 0)),
                  pl.BlockSpec((None, 1, LANES), lambda b, h: (h, 0, 0))],
        out_specs=[pl.BlockSpec((seq, RET_VAL_DIM), lambda b, h: (b, h)),
                   pl.BlockSpec((seq, RET_VAL_DIM), lambda b, h: (b, h)),
                   pl.BlockSpec((seq, RET_KEY_DIM), lambda b, h: (b, h)),
                   pl.BlockSpec((seq, RET_KEY_DIM), lambda b, h: (b, h))],
        out_shape=[jax.ShapeDtypeStruct((t, RET_HEADS * RET_VAL_DIM), BF16),
                   jax.ShapeDtypeStruct((t, RET_HEADS * RET_VAL_DIM), F32),
                   jax.ShapeDtypeStruct((t, RET_HEADS * RET_KEY_DIM), BF16),
                   jax.ShapeDtypeStruct((t, RET_HEADS * RET_KEY_DIM), BF16)],
        semantics=("parallel", "parallel"), args=(proj, proj, proj, proj, cs, sn, lg_arr), exchange=exchange)


def _att_bias(w_ref, bias_ref):
    n_i = lax.broadcasted_iota(jnp.int32, (ATT_Q, BIAS_LEN), 0)
    qc = lax.broadcasted_iota(jnp.int32, (ATT_Q, ATT_WIN), 0) >> 6
    kc = lax.broadcasted_iota(jnp.int32, (ATT_Q, ATT_WIN), 1) >> 6
    dc = qc + BAND_CHUNKS - kc
    band = (dc >= 0) & (dc <= BAND_CHUNKS)
    key = lax.broadcasted_iota(jnp.int32, (ATT_Q, ATT_WIN), 1)
    for e in range(2):
        xw = jnp.broadcast_to(w_ref[e:e + 1, :], (ATT_Q, BIAS_LEN))
        for bit in range(8):
            xw = jnp.where(((n_i >> bit) & 1) == 1, pltpu.roll(xw, 1 << bit, 1), xw)
        bias = jnp.where(band, xw[:, BIAS_LEN - ATT_WIN:], NEG_INF)
        for first in range(ATT_STARTS):
            bias_ref[first, e] = jnp.where(key + (first * ATT_Q - ATT_PAD) >= 0, bias, NEG_INF)
        bias_ref[ATT_STARTS, e] = bias


def _att_specs(batch, seq):
    ni = seq // ATT_Q
    q_spec = pl.BlockSpec((ATT_Q, LANES), lambda hp, b, i: (b * ni + i, C_AQ // LANES + hp))
    k_spec = pl.BlockSpec((seq, LANES), lambda hp, b, i: (b, C_AK // LANES + hp))
    v_spec = pl.BlockSpec((seq, LANES), lambda hp, b, i: (b, C_AV // LANES + hp))
    w_spec = pl.BlockSpec((None, 2, BIAS_LEN), lambda hp, b, i: (hp, 0, 0))
    b_spec = pl.BlockSpec((None, ATT_STARTS + 1, 2, ATT_Q, ATT_WIN), lambda hp, b, i: (hp, 0, 0, 0, 0))
    pad = pltpu.VMEM((seq + ATT_PAD, LANES), BF16)
    return ni, q_spec, k_spec, v_spec, w_spec, b_spec, pad


def _att_bias_tiles(wvec):
    return pl.pallas_call(
        _att_bias, name="att_bias", grid=(ATT_HEADS // 2,),
        in_specs=[pl.BlockSpec((None, 2, BIAS_LEN), lambda hp: (hp, 0, 0))],
        out_specs=pl.BlockSpec((None, ATT_STARTS + 1, 2, ATT_Q, ATT_WIN), lambda hp: (hp, 0, 0, 0, 0)),
        out_shape=jax.ShapeDtypeStruct((ATT_HEADS // 2, ATT_STARTS + 1, 2, ATT_Q, ATT_WIN), F32),
        compiler_params=_params("parallel"),
    )(wvec)


def _att_pad(src_ref, pad_ref):
    pad_ref[:ATT_PAD, :] = jnp.zeros((ATT_PAD, LANES), BF16)
    pad_ref[ATT_PAD:, :] = src_ref[...]


def _att_head(q2, sel):
    return jnp.where(sel, q2, jnp.zeros_like(q2)) * 0.125


def _att_softmax_rows(s_ref, bias_ref, rows):
    s = s_ref[rows, :] + bias_ref[rows, :]
    ex = jnp.exp(s - jnp.max(s, axis=-1, keepdims=True))
    return ex, 1.0 / jnp.sum(ex, axis=-1, keepdims=True)


def _att_fwd(proj, bias, batch, seq, exchange):
    ni, q_spec, k_spec, v_spec, _, b_spec, pad = _att_specs(batch, seq)

    def body(q_ref, k_ref, v_ref, bias_ref, o_ref, kp_ref, vp_ref, s_ref, e_ref):
        i = pl.program_id(2)

        @pl.when(i == 0)
        def _():
            _att_pad(k_ref, kp_ref)
            _att_pad(v_ref, vp_ref)

        win = pl.ds(pl.multiple_of(i * ATT_Q, ATT_Q), ATT_WIN)
        k2, v2, q2 = kp_ref[win, :], vp_ref[win, :], q_ref[...]
        lo = lax.broadcasted_iota(jnp.int32, (1, LANES), 1) < 64
        start = jnp.minimum(i, ATT_STARTS)
        out = jnp.zeros((ATT_Q, LANES), F32)
        for e in range(2):
            sel = lo if e == 0 else jnp.logical_not(lo)
            s_ref[e] = _dot_nt(_att_head(q2, sel), k2)
            rsum = []
            for c in range(ATT_Q // ATT_ROWS):
                rows = slice(c * ATT_ROWS, (c + 1) * ATT_ROWS)
                ex, r = _att_softmax_rows(s_ref.at[e], bias_ref.at[start, e], rows)
                e_ref[e, rows, :] = ex.astype(BF16)
                rsum.append(r)
            out = out + _dot(e_ref[e], jnp.where(sel, v2, jnp.zeros_like(v2))) * jnp.concatenate(rsum, axis=0)
        o_ref[...] = out.astype(BF16)

    return _call(
        body, name="att_fwd", grid=(ATT_HEADS // 2, batch, ni),
        in_specs=[q_spec, k_spec, v_spec, b_spec],
        out_specs=[pl.BlockSpec((ATT_Q, LANES), lambda hp, b, i: (b * ni + i, hp))],
        out_shape=[jax.ShapeDtypeStruct((batch * seq, ATT_HEADS * 64), BF16)],
        scratch=[pad, pad, pltpu.VMEM((2, ATT_Q, ATT_WIN), F32), pltpu.VMEM((2, ATT_Q, ATT_WIN), BF16)],
        semantics=("arbitrary", "arbitrary", "arbitrary"), args=(proj, proj, proj, bias), exchange=exchange)


GL_HALF = 512


def _gl_specs(tm):
    return [pl.BlockSpec((tm, GL_HALF), lambda i, c=C_GL // GL_HALF + j: (i, c)) for j in range(4)]


def _gates(gl_refs, b_ref):
    logits = [ref[...].astype(F32) for ref in gl_refs]
    gr = _sigmoid(jnp.concatenate(logits[:2], axis=1) + b_ref[:, :D_MODEL])
    ga = _sigmoid(jnp.concatenate(logits[2:], axis=1) + b_ref[:, D_MODEL:])
    return gr, ga


def _whole(a):
    return pl.BlockSpec(a.shape, lambda i: (0,) * a.ndim)


def _mix_out_fwd(gro, ao, proj, b_gate, w_ret, w_att_t, x2, w_out, g2):
    t = gro.shape[0]
    tm = min(256, t)

    def body(gro_ref, ao_ref, gl0, gl1, gl2, gl3, b_ref, wr_ref, wa_ref, x_ref, wo_ref, g_ref,
             z_ref, yr_ref, ya_ref, h_ref, hn_ref):
        yr = _dot(gro_ref[...], wr_ref[...])
        ya = _dot_nt(ao_ref[...], wa_ref[...])
        yr_ref[...] = yr.astype(BF16)
        ya_ref[...] = ya.astype(BF16)
        gr, ga = _gates((gl0, gl1, gl2, gl3), b_ref)
        z = (gr * yr + ga * ya).astype(BF16)
        z_ref[...] = z
        h = x_ref[...] + _dot(z, wo_ref[...])
        h_ref[...] = h
        r = lax.rsqrt(jnp.mean(h * h, axis=-1, keepdims=True) + EPS)
        hn_ref[...] = (h * r * g_ref[...]).astype(BF16)

    row = pl.BlockSpec((tm, D_MODEL), lambda i: (i, 0))
    return pl.pallas_call(
        body, name="mix_out_fwd", grid=(t // tm,),
        in_specs=[row, pl.BlockSpec((tm, 512), lambda i: (i, 0)), *_gl_specs(tm),
                  _whole(b_gate), _whole(w_ret), _whole(w_att_t), row, _whole(w_out), _whole(g2)],
        out_specs=[row] * 5,
        out_shape=[jax.ShapeDtypeStruct((t, D_MODEL), BF16)] * 3
                  + [jax.ShapeDtypeStruct((t, D_MODEL), F32), jax.ShapeDtypeStruct((t, D_MODEL), BF16)],
        compiler_params=_params("parallel"),
    )(gro, ao, proj, proj, proj, proj, b_gate, w_ret, w_att_t, x2, w_out, g2)


def _ffn_up(hn, wg_t, wu_t):
    t = hn.shape[0]
    tm, tn = min(512, t), D_FF // 2

    def body(h_ref, wg_ref, wu_ref, g_ref, u_ref, a_ref):
        g = _dot_nt(h_ref[...], wg_ref[...])
        u = _dot_nt(h_ref[...], wu_ref[...])
        g_ref[...] = g.astype(BF16)
        u_ref[...] = u.astype(BF16)
        a_ref[...] = (g * _sigmoid(g) * u).astype(BF16)

    w_spec = pl.BlockSpec((tn, D_MODEL), lambda j, i: (j, 0))
    out = pl.BlockSpec((tm, tn), lambda j, i: (i, j))
    return pl.pallas_call(
        body, name="ffn_up", grid=(D_FF // tn, t // tm),
        in_specs=[pl.BlockSpec((tm, D_MODEL), lambda j, i: (i, 0)), w_spec, w_spec],
        out_specs=[out, out, out],
        out_shape=[jax.ShapeDtypeStruct((t, D_FF), BF16)] * 3,
        compiler_params=_params("parallel", "parallel"),
    )(hn, wg_t, wu_t)


def _ffn_down_loss(a, h1, tgt, w_down, g3):
    t = a.shape[0]
    tm = min(512, t)

    def body(a_ref, h_ref, t_ref, w_ref, g_ref, dh_ref, dhb_ref, loss_ref, dg_ref):
        @pl.when(pl.program_id(0) == 0)
        def _():
            loss_ref[...] = jnp.zeros_like(loss_ref)
            dg_ref[...] = jnp.zeros_like(dg_ref)

        g = g_ref[...]
        h2 = h_ref[...] + _dot(a_ref[...], w_ref[...])
        r = lax.rsqrt(jnp.mean(h2 * h2, axis=-1, keepdims=True) + EPS)
        err = h2 * r * g - t_ref[...]
        loss_ref[...] += jnp.sum(err * err) * (0.5 / D_MODEL)
        dy = err * (1.0 / D_MODEL)
        dh, dg_rows = _rms_bwd(h2, g, dy)
        dg_ref[...] += jnp.sum(dg_rows, axis=0, keepdims=True)
        dh_ref[...] = dh
        dhb_ref[...] = dh.astype(BF16)

    row = pl.BlockSpec((tm, D_MODEL), lambda i: (i, 0))
    vec = pl.BlockSpec((1, D_MODEL), lambda i: (0, 0))
    return pl.pallas_call(
        body, name="ffn_down_loss", grid=(t // tm,),
        in_specs=[pl.BlockSpec((tm, D_FF), lambda i: (i, 0)), row, row,
                  pl.BlockSpec((D_FF, D_MODEL), lambda i: (0, 0)), vec],
        out_specs=[row, row, pl.BlockSpec((1, LANES), lambda i: (0, 0)), vec],
        out_shape=[jax.ShapeDtypeStruct((t, D_MODEL), F32), jax.ShapeDtypeStruct((t, D_MODEL), BF16),
                   jax.ShapeDtypeStruct((1, LANES), F32), jax.ShapeDtypeStruct((1, D_MODEL), F32)],
        compiler_params=_params("arbitrary"),
    )(a, h1, tgt, w_down, g3)


def _ffn_bwd_act(dh2b, w_down, g_act, u_act, exchange):
    t = dh2b.shape[0]
    tm, tn = min(512, t), D_FF // 2

    def body(d_ref, w_ref, g_ref, u_ref, dg_ref, du_ref):
        da = _dot_nt(d_ref[...], w_ref[...])
        g = g_ref[...].astype(F32)
        u = u_ref[...].astype(F32)
        sg = _sigmoid(g)
        dg_ref[...] = (da * u * sg * (1.0 + g * (1.0 - sg))).astype(BF16)
        du_ref[...] = (da * g * sg).astype(BF16)

    blk = pl.BlockSpec((tm, tn), lambda j, i: (i, j))
    return _call(
        body, name="ffn_bwd_act", grid=(D_FF // tn, t // tm),
        in_specs=[pl.BlockSpec((tm, D_MODEL), lambda j, i: (i, 0)),
                  pl.BlockSpec((tn, D_MODEL), lambda j, i: (j, 0)), blk, blk],
        out_specs=[blk, blk],
        out_shape=[jax.ShapeDtypeStruct((t, D_FF), BF16)] * 2,
        semantics=("parallel", "parallel"), args=(dh2b, w_down, g_act, u_act), exchange=exchange)


def _ffn_bwd_in(dg, du, wg_t, wu_t, h1, dh2, g2, exchange):
    t = dg.shape[0]
    tm, tk = min(512, t), D_FF // 2
    nk = D_FF // tk

    def body(dg_ref, du_ref, wg_ref, wu_ref, h_ref, d2_ref, g_ref, dh_ref, dhb_ref, gn_ref, acc_ref):
        i, kk = pl.program_id(0), pl.program_id(1)

        @pl.when((i == 0) & (kk == 0))
        def _():
            gn_ref[...] = jnp.zeros_like(gn_ref)

        @pl.when(kk == 0)
        def _():
            acc_ref[...] = jnp.zeros_like(acc_ref)

        acc_ref[...] += _dot(dg_ref[...], wg_ref[...]) + _dot(du_ref[...], wu_ref[...])

        @pl.when(kk == nk - 1)
        def _():
            dx, dg_rows = _rms_bwd(h_ref[...], g_ref[...], acc_ref[...])
            dh = d2_ref[...] + dx
            dh_ref[...] = dh
            dhb_ref[...] = dh.astype(BF16)
            gn_ref[...] += jnp.sum(dg_rows, axis=0, keepdims=True)

    act = pl.BlockSpec((tm, tk), lambda i, kk: (i, kk))
    wsp = pl.BlockSpec((tk, D_MODEL), lambda i, kk: (kk, 0))
    row = pl.BlockSpec((tm, D_MODEL), lambda i, kk: (i, 0))
    vec = pl.BlockSpec((1, D_MODEL), lambda i, kk: (0, 0))
    return _call(
        body, name="ffn_bwd_in", grid=(t // tm, nk),
        in_specs=[act, act, wsp, wsp, row, row, vec],
        out_specs=[row, row, vec],
        out_shape=[jax.ShapeDtypeStruct((t, D_MODEL), F32), jax.ShapeDtypeStruct((t, D_MODEL), BF16),
                   jax.ShapeDtypeStruct((1, D_MODEL), F32)],
        scratch=[pltpu.VMEM((tm, D_MODEL), F32)],
        semantics=("arbitrary", "arbitrary"), args=(dg, du, wg_t, wu_t, h1, dh2, g2), exchange=exchange)


def _mix_bwd(dh1b, w_out, proj, b_gate, y_ret, y_att, w_ret, w_att_t):
    t = dh1b.shape[0]
    tm = min(256, t)

    def body(d_ref, wo_ref, gl0, gl1, gl2, gl3, b_ref, yr_ref, ya_ref, wr_ref, wa_ref,
             dyr_ref, dya_ref, dglr_ref, dgla_ref, dgro_ref, dao_ref, db_ref):
        @pl.when(pl.program_id(0) == 0)
        def _():
            db_ref[...] = jnp.zeros_like(db_ref)

        dz = _dot_nt(d_ref[...], wo_ref[...])
        gr, ga = _gates((gl0, gl1, gl2, gl3), b_ref)
        dyr = (dz * gr).astype(BF16)
        dya = (dz * ga).astype(BF16)
        dyr_ref[...] = dyr
        dya_ref[...] = dya
        dglr = dz * yr_ref[...].astype(F32) * gr * (1.0 - gr)
        dgla = dz * ya_ref[...].astype(F32) * ga * (1.0 - ga)
        dglr_ref[...] = dglr.astype(BF16)
        dgla_ref[...] = dgla.astype(BF16)
        db_ref[:, :D_MODEL] += jnp.sum(dglr, axis=0, keepdims=True)
        db_ref[:, D_MODEL:] += jnp.sum(dgla, axis=0, keepdims=True)
        dgro_ref[...] = _dot_nt(dyr, wr_ref[...]).astype(BF16)
        dao_ref[...] = _dot(dya, wa_ref[...]).astype(BF16)

    row = pl.BlockSpec((tm, D_MODEL), lambda i: (i, 0))
    half = pl.BlockSpec((tm, 512), lambda i: (i, 0))
    return pl.pallas_call(
        body, name="mix_bwd", grid=(t // tm,),
        in_specs=[row, _whole(w_out), *_gl_specs(tm), _whole(b_gate), row, row, _whole(w_ret), _whole(w_att_t)],
        out_specs=[row, row, row, row, row, half, _whole(b_gate)],
        out_shape=[jax.ShapeDtypeStruct((t, D_MODEL), BF16)] * 5
                  + [jax.ShapeDtypeStruct((t, 512), BF16), jax.ShapeDtypeStruct(b_gate.shape, F32)],
        compiler_params=_params("arbitrary"),
    )(dh1b, w_out, proj, proj, proj, proj, b_gate, y_ret, y_att, w_ret, w_att_t)


def _ret_bwd(dgro, proj, o_ret, qr, kr, cs, sn, lg_arr, batch, seq, exchange):
    t = batch * seq
    nt = seq // RET_TILE

    def body(dgro_ref, rg_ref, o_ref, qr_ref, kr_ref, v_ref, cs_ref, sn_ref, lg_ref,
             dq_ref, dk_ref, dv_ref, drg_ref, do_ref, st_ref):
        lg = lg_ref[:, 0:1]
        inside, q_dec, k_dec, tile_dec = _decay(lg)

        state = jnp.zeros((RET_KEY_DIM, RET_VAL_DIM), F32)
        for i in range(nt - 1):
            rows = slice(i * RET_TILE, (i + 1) * RET_TILE)
            state = state * tile_dec + _dot_tn(_scaled(kr_ref[rows, :], k_dec), v_ref[rows, :])
            st_ref[i + 1] = state.astype(BF16)

        for i in range(nt):
            rows = slice(i * RET_TILE, (i + 1) * RET_TILE)
            o = o_ref[rows, :]
            xc = o - jnp.mean(o, axis=-1, keepdims=True)
            rs = lax.rsqrt(jnp.mean(xc * xc, axis=-1, keepdims=True) + EPS)
            nrm = xc * rs
            rg = rg_ref[rows, :].astype(F32)
            sg = _sigmoid(rg)
            dg = dgro_ref[rows, :].astype(F32)
            drg_ref[rows, :] = (dg * nrm * sg * (1.0 + rg * (1.0 - sg))).astype(BF16)
            dn = dg * rg * sg
            do = rs * (dn - jnp.mean(dn, axis=-1, keepdims=True)
                       - nrm * jnp.mean(dn * nrm, axis=-1, keepdims=True))
            do_ref[rows, :] = do.astype(BF16)

        dstate = jnp.zeros((RET_KEY_DIM, RET_VAL_DIM), F32)
        for i in reversed(range(nt)):
            rows = slice(i * RET_TILE, (i + 1) * RET_TILE)
            qi, ki, vi, doi = qr_ref[rows, :], kr_ref[rows, :], v_ref[rows, :], do_ref[rows, :]
            p = (_dot_nt(qi, ki) * inside).astype(BF16)
            dp = (_dot_nt(doi, vi) * inside).astype(BF16)
            dq = _dot(dp, ki)
            dk = _dot_tn(dp, qi)
            dv = _dot_tn(p, doi)
            if i > 0:
                dq = dq + _dot_nt(doi, st_ref[i]) * q_dec
            if i < nt - 1:
                dsb = dstate.astype(BF16)
                dk = dk + _dot_nt(vi, dsb) * k_dec
                dv = dv + _dot(_scaled(ki, k_dec), dsb)
            if i > 0:
                dstate = dstate * tile_dec + _dot_tn(_scaled(qi, q_dec), doi)
            dq_ref[rows, :] = (dq * cs_ref[rows, :] - pltpu.roll(dq, 64, 1) * sn_ref[rows, :]).astype(BF16)
            dk = (dk * cs_ref[rows, :] - pltpu.roll(dk, 64, 1) * sn_ref[rows, :]) * (RET_KEY_DIM ** -0.5)
            dk_ref[rows, :] = dk.astype(BF16)
            dv_ref[rows, :] = dv.astype(BF16)

    key = pl.BlockSpec((seq, RET_KEY_DIM), lambda b, h: (b, h))
    val = pl.BlockSpec((seq, RET_VAL_DIM), lambda b, h: (b, h))
    tab = pl.BlockSpec((seq, RET_KEY_DIM), lambda b, h: (0, 0))
    return _call(
        body, name="ret_bwd", grid=(batch, RET_HEADS),
        in_specs=[val, pl.BlockSpec((seq, RET_VAL_DIM), lambda b, h: (b, C_RG // RET_VAL_DIM + h)), val, key, key,
                  pl.BlockSpec((seq, RET_VAL_DIM), lambda b, h: (b, C_RV // RET_VAL_DIM + h)), tab, tab,
                  pl.BlockSpec((None, 1, LANES), lambda b, h: (h, 0, 0))],
        out_specs=[key, key, val, val],
        out_shape=[jax.ShapeDtypeStruct((t, RET_HEADS * RET_KEY_DIM), BF16)] * 2
                  + [jax.ShapeDtypeStruct((t, RET_HEADS * RET_VAL_DIM), BF16)] * 2,
        scratch=[pltpu.VMEM((seq, RET_VAL_DIM), BF16), pltpu.VMEM((nt, RET_KEY_DIM, RET_VAL_DIM), BF16)],
        semantics=("parallel", "parallel"), args=(dgro, proj, o_ret, qr, kr, proj, cs, sn, lg_arr),
        exchange=exchange)


def _att_bwd(proj, bias, dao, batch, seq, exchange):
    ni, q_spec, k_spec, v_spec, w_spec, b_spec, pad = _att_specs(batch, seq)
    t = batch * seq

    def body(q_ref, k_ref, v_ref, bias_ref, do_ref, dq_ref, dk_ref, dv_ref, dw_ref,
             dbias_ref, dk_acc, dv_acc, kp_ref, vp_ref, s_ref, dp_ref, e_ref, ds_ref):
        b, i = pl.program_id(1), pl.program_id(2)

        @pl.when((b == 0) & (i == 0))
        def _():
            dbias_ref[...] = jnp.zeros_like(dbias_ref)

        @pl.when(i == 0)
        def _():
            _att_pad(k_ref, kp_ref)
            _att_pad(v_ref, vp_ref)
            dk_acc[...] = jnp.zeros_like(dk_acc)
            dv_acc[...] = jnp.zeros_like(dv_acc)

        win = pl.ds(pl.multiple_of(i * ATT_Q, ATT_Q), ATT_WIN)
        k2, v2, q2, do2 = kp_ref[win, :], vp_ref[win, :], q_ref[...], do_ref[...]
        lo = lax.broadcasted_iota(jnp.int32, (1, LANES), 1) < 64
        dq = jnp.zeros((ATT_Q, LANES), F32)
        dk = jnp.zeros((LANES, ATT_WIN), F32)
        dv = jnp.zeros((LANES, ATT_WIN), F32)
        start = jnp.minimum(i, ATT_STARTS)
        for e in range(2):
            sel = lo if e == 0 else jnp.logical_not(lo)
            qm = _att_head(q2, sel)
            dom = jnp.where(sel, do2, jnp.zeros_like(do2))
            s_ref[e] = _dot_nt(qm, k2)
            dp_ref[e] = _dot_nt(dom, v2)
            rsum = []
            for c in range(ATT_Q // ATT_ROWS):
                rows = slice(c * ATT_ROWS, (c + 1) * ATT_ROWS)
                ex, r = _att_softmax_rows(s_ref.at[e], bias_ref.at[start, e], rows)
                dp = dp_ref[e, rows, :]
                mean = jnp.sum(dp * ex, axis=-1, keepdims=True) * r
                ds = ex * ((dp - mean) * r)
                dbias_ref[e, rows, :] += ds
                ds_ref[e, rows, :] = ds.astype(BF16)
                e_ref[e, rows, :] = ex.astype(BF16)
                rsum.append(r)
            dq = dq + _dot(ds_ref[e], jnp.where(sel, k2, jnp.zeros_like(k2)))
            dk = dk + _dot_tn(qm, ds_ref[e])
            dv = dv + _dot_tn((dom.astype(F32) * jnp.concatenate(rsum, axis=0)).astype(BF16), e_ref[e])
        dq_ref[...] = (dq * 0.125).astype(BF16)
        dk_acc[:, win] += dk
        dv_acc[:, win] += dv

        @pl.when(i == ni - 1)
        def _():
            dk_ref[...] = dk_acc[:, ATT_PAD:].T.astype(BF16)
            dv_ref[...] = dv_acc[:, ATT_PAD:].T.astype(BF16)

        @pl.when((b == batch - 1) & (i == ni - 1))
        def _():
            n_i = lax.broadcasted_iota(jnp.int32, (ATT_Q, BIAS_LEN), 0)
            for e in range(2):
                xw = jnp.concatenate([jnp.zeros((ATT_Q, BIAS_LEN - ATT_WIN), F32), dbias_ref[e]], axis=1)
                for bit in range(8):
                    xw = jnp.where(((n_i >> bit) & 1) == 1, pltpu.roll(xw, BIAS_LEN - (1 << bit), 1), xw)
                dw_ref[e:e + 1, :] = jnp.sum(xw, axis=0, keepdims=True)

    seq_blk = pl.BlockSpec((seq, LANES), lambda hp, b, i: (b, hp))
    q_out = pl.BlockSpec((ATT_Q, LANES), lambda hp, b, i: (b * ni + i, hp))
    return _call(
        body, name="att_bwd", grid=(ATT_HEADS // 2, batch, ni),
        in_specs=[q_spec, k_spec, v_spec, b_spec, q_out],
        out_specs=[q_out, seq_blk, seq_blk, w_spec],
        out_shape=[jax.ShapeDtypeStruct((t, 512), BF16)] * 3
                  + [jax.ShapeDtypeStruct((ATT_HEADS // 2, 2, BIAS_LEN), F32)],
        scratch=[pltpu.VMEM((2, ATT_Q, ATT_WIN), F32),
                 pltpu.VMEM((LANES, seq + ATT_PAD), F32), pltpu.VMEM((LANES, seq + ATT_PAD), F32), pad, pad,
                 pltpu.VMEM((2, ATT_Q, ATT_WIN), F32), pltpu.VMEM((2, ATT_Q, ATT_WIN), F32),
                 pltpu.VMEM((2, ATT_Q, ATT_WIN), BF16), pltpu.VMEM((2, ATT_Q, ATT_WIN), BF16)],
        semantics=("arbitrary", "arbitrary", "arbitrary"), args=(proj, proj, proj, bias, dao), exchange=exchange)


def _rms_in_bwd(x2, dxn, dh1, g1):
    t = x2.shape[0]
    tm = min(512, t)

    def body(x_ref, d_ref, h_ref, g_ref, dx_ref, dg_ref):
        @pl.when(pl.program_id(0) == 0)
        def _():
            dg_ref[...] = jnp.zeros_like(dg_ref)

        dx, dg_rows = _rms_bwd(x_ref[...], g_ref[...], d_ref[...])
        dx_ref[...] = h_ref[...] + dx
        dg_ref[...] += jnp.sum(dg_rows, axis=0, keepdims=True)

    row = pl.BlockSpec((tm, D_MODEL), lambda i: (i, 0))
    vec = pl.BlockSpec((1, D_MODEL), lambda i: (0, 0))
    return pl.pallas_call(
        body, name="rms_in_bwd", grid=(t // tm,),
        in_specs=[row, row, row, vec], out_specs=[row, vec],
        out_shape=[jax.ShapeDtypeStruct((t, D_MODEL), F32), jax.ShapeDtypeStruct((1, D_MODEL), F32)],
        compiler_params=_params("arbitrary"),
    )(x2, dxn, dh1, g1)


def _pack_small(dg1, dbr, dba, dg2, dg3, dw, loss):
    def body(a_ref, b_ref, c_ref, d_ref, e_ref, w_ref, l_ref, o_ref):
        o_ref[...] = jnp.zeros_like(o_ref)
        for r, ref in enumerate((a_ref, b_ref, c_ref, d_ref, e_ref)):
            o_ref[r:r + 1, :] = ref[...]
        o_ref[5:6, 0:LANES] = l_ref[...]
        for hp in range(ATT_HEADS // 2):
            o_ref[8 + 2 * hp:10 + 2 * hp, :] = w_ref[hp]

    return pl.pallas_call(body, name="pack_small",
                          out_shape=jax.ShapeDtypeStruct((16, D_MODEL), F32))(dg1, dbr, dba, dg2, dg3, dw, loss)


def _rotary_tables(seq):
    freqs = ROPE_BASE ** (-jnp.arange(0, RET_KEY_DIM, 2, dtype=F32) / RET_KEY_DIM)
    ang = jnp.arange(seq, dtype=F32)[:, None] * freqs[None, :]
    cos, sin = jnp.cos(ang), jnp.sin(ang)
    return jnp.concatenate([cos, cos], axis=1), jnp.concatenate([-sin, sin], axis=1)


def _bias_rows(rel_bias):
    n_far = BIAS_LEN - ATT_Q - MAX_REL + 1
    n_near = BIAS_LEN - n_far - (N_REL - 2)
    w = jnp.concatenate([jnp.broadcast_to(rel_bias[:, N_REL - 1:], (ATT_HEADS, n_far)),
                         rel_bias[:, 1:N_REL - 1][:, ::-1],
                         jnp.broadcast_to(rel_bias[:, :1], (ATT_HEADS, n_near))], axis=1)
    return w.reshape(ATT_HEADS // 2, 2, BIAS_LEN)


def _bias_rows_bwd(dw):
    n_far = BIAS_LEN - ATT_Q - MAX_REL + 1
    mid = dw[:, n_far:n_far + N_REL - 2][:, ::-1]
    return jnp.concatenate([jnp.sum(dw[:, n_far + N_REL - 2:], axis=1, keepdims=True), mid,
                            jnp.sum(dw[:, :n_far], axis=1, keepdims=True)], axis=1)


def _step(x, tgt, norm_mix, b_gate, norm_ffn, norm_final, rel_bias_shard, shard):
    batch, seq, _ = x.shape
    t = batch * seq
    n_rb = rel_bias_shard.shape[-1]
    x2, tgt2 = x.reshape(t, D_MODEL), tgt.reshape(t, D_MODEL)
    g3 = norm_final.reshape(1, D_MODEL)
    cs, sn = _rotary_tables(seq)
    lg = np.log(1.0 - 2.0 ** (-5.0 - np.arange(RET_HEADS, dtype=np.float32))).astype(np.float32)
    lg_arr = jnp.asarray(np.broadcast_to(lg[:, None, None], (RET_HEADS, 1, LANES)))

    def gather(*names):
        return _ChipGather([shard[nm] for nm in names])

    def scatter(*grads):
        return _Exchange(grads, scatter=True)

    rb_pad = jnp.pad(rel_bias_shard, ((0, 0), (0, LANES - n_rb)))
    (xn,), (w_in_t, rb_full) = _rms_fwd(x2, norm_mix, _ChipGather([shard["w_in_t"], rb_pad]))
    rb_full = rb_full.reshape(N_DEV, ATT_HEADS, LANES)[:, :, :n_rb]
    bias = _att_bias_tiles(_bias_rows(jnp.transpose(rb_full, (1, 0, 2)).reshape(ATT_HEADS, N_DEV * n_rb)))
    proj, (w_ret, w_att_t, w_out, w_gate_t) = _mm(
        xn, w_in_t, tb=True, out_dtype=BF16, tm=1024, tn=1664, tk=1024, name="proj",
        exchange=gather("w_ret", "w_att_t", "w_out", "w_gate_t"))
    (gro, o_ret, qr, kr), _ = _ret_fwd(proj, cs, sn, lg_arr, batch, seq, None)
    (ao,), (w_up_t, w_down) = _att_fwd(proj, bias, batch, seq, gather("w_up_t", "w_down"))
    z, y_ret, y_att, h1, hn = _mix_out_fwd(gro, ao, proj, b_gate, w_ret, w_att_t, x2, w_out, norm_ffn)
    g_act, u_act, a_act = _ffn_up(hn, w_gate_t, w_up_t)
    dh2, dh2b, loss, dg3 = _ffn_down_loss(a_act, h1, tgt2, w_down, g3)

    wg = dict(out_dtype=BF16, tn=1024, ta=True)
    slots = {}
    dw_down = _mm(a_act, dh2b, tm=1408, tk=1024, name="dw_down", **wg)
    (d_gact, d_uact), _ = _ffn_bwd_act(dh2b, w_down, g_act, u_act, None)
    dw_gate = _mm(d_gact, hn, tm=1408, tk=1024, name="dw_gate", **wg)
    dw_up = _mm(d_uact, hn, tm=1408, tk=1024, name="dw_up", **wg)
    (dh1, dh1b, dg2), (slots["w_down"],) = _ffn_bwd_in(d_gact, d_uact, w_gate_t, w_up_t, h1, dh2, norm_ffn,
                                                     scatter(dw_down))
    dw_out = _mm(z, dh1b, tm=1024, tk=2048, name="dw_out", **wg)
    dyr, dya, dglr, dgla, dgro, dao, db = _mix_bwd(dh1b, w_out, proj, b_gate, y_ret, y_att, w_ret, w_att_t)
    dw_ret = _mm(gro, dyr, tm=1024, tk=2048, name="dw_ret", **wg)
    dw_att = _mm(dya, ao, tm=1024, tk=2048, name="dw_att", **wg)
    (drq, drk, drv, drg), _ = _ret_bwd(dgro, proj, o_ret, qr, kr, cs, sn, lg_arr, batch, seq, None)
    (daq, dak, dav, dw), (slots["w_gate_t"], slots["w_out"], slots["w_ret"], slots["w_att_t"]) = _att_bwd(
        proj, bias, dao, batch, seq, scatter(dw_gate, dw_out, dw_ret, dw_att))
    dproj = [drq, drk, drv, drg, daq, dak, dav, dglr, dgla]
    dw_in, (slots["w_up_t"],) = _mm_pieces(dproj, xn, ta=True, out_dtype=BF16, tm=512, tn=1024, tk=1024,
                                           name="dw_in", exchange=scatter(dw_up))
    (dw_in_sibling,) = _alone(_PairSwap([dw_in]), "swap_w_in")
    dw_in_pairs = _pair_add(dw_in, dw_in_sibling, "pair_w_in")
    dxn, (slots["w_in_t"],) = _mm_pieces(dproj, w_in_t, ta=False, out_dtype=F32, tm=1024, tn=1024, tk=512, name="dxn",
                                  exchange=_ChipScatter([dw_in_pairs]))
    dx, dg1 = _rms_in_bwd(x2, dxn, dh1, norm_mix)
    small = _pack_small(dg1, db[:, :D_MODEL], db[:, D_MODEL:], dg2, dg3, dw, loss)
    (small_slots,) = _alone(_ChipGather([small]), "gather_small")
    return dx.reshape(batch, seq, D_MODEL), slots, small_slots.reshape(N_DEV, 16, D_MODEL)


def _row_tile(r, c):
    return max(d for d in range(16, r + 1, 16) if r % d == 0 and (d * c <= 256 * 1024 or d == 16))


def _pair_add(grad, got, name):
    _, r, c = got.shape
    tr = r
    core = lax.axis_index("c").astype(jnp.int32).reshape(1)

    def body(core_ref, g_ref, a_ref, o_ref):
        o_ref[...] = (g_ref[...].astype(F32) + a_ref[...].astype(F32)).astype(o_ref.dtype)

    blk = pl.BlockSpec((None, tr, c), lambda q, i, core_ref: (q, i, 0))
    return pl.pallas_call(
        body, name=name,
        grid_spec=pltpu.PrefetchScalarGridSpec(
            num_scalar_prefetch=1, grid=(4, r // tr),
            in_specs=[pl.BlockSpec((None, None, tr, c), lambda q, i, core_ref: (q, core_ref[0], i, 0)), blk],
            out_specs=blk),
        out_shape=jax.ShapeDtypeStruct(got.shape, got.dtype),
        compiler_params=_params("parallel", "parallel"),
    )(core, grad.reshape(4, 2, r, c), got)


def _sum_slots(slots, name):
    n, r, c = slots.shape
    tr = _row_tile(r, c)

    def body(s_ref, o_ref):
        acc = s_ref[0].astype(F32)
        for s in range(1, n):
            acc = acc + s_ref[s].astype(F32)
        o_ref[...] = acc

    return pl.pallas_call(
        body, name=name, grid=(r // tr,),
        in_specs=[pl.BlockSpec((n, tr, c), lambda i: (0, i, 0))],
        out_specs=pl.BlockSpec((tr, c), lambda i: (i, 0)),
        out_shape=jax.ShapeDtypeStruct((r, c), F32),
        compiler_params=_params("parallel"),
    )(slots)


def _adamw_math(w, g, m, v):
    m = ADAM_B1 * m + (1.0 - ADAM_B1) * g
    v = ADAM_B2 * v + (1.0 - ADAM_B2) * (g * g)
    m_hat = m / (1.0 - ADAM_B1 ** ADAM_STEP)
    v_hat = v / (1.0 - ADAM_B2 ** ADAM_STEP)
    return -ADAM_LR * (m_hat / (jnp.sqrt(v_hat) + ADAM_EPS) + ADAM_WD * w), m, v


def _adamw(w, slots, m, v, name):
    n, r, c = slots.shape
    tr = _row_tile(r, c)

    def body(w_ref, s_ref, m_ref, v_ref, g_ref, d_ref, nm_ref, nv_ref):
        g = s_ref[0].astype(F32)
        for s in range(1, n):
            g = g + s_ref[s].astype(F32)
        g_ref[...] = g
        d_ref[...], nm_ref[...], nv_ref[...] = _adamw_math(w_ref[...], g, m_ref[...], v_ref[...])

    blk = pl.BlockSpec((tr, c), lambda i: (i, 0))
    return pl.pallas_call(
        body, name=name, grid=(r // tr,),
        in_specs=[blk, pl.BlockSpec((n, tr, c), lambda i: (0, i, 0)), blk, blk], out_specs=[blk] * 4,
        out_shape=[jax.ShapeDtypeStruct((r, c), F32)] * 4,
        compiler_params=_params("parallel"),
    )(w, slots, m, v)


def _adamw_small(ws, gs, ms, vs):
    n = len(ws)

    def body(*refs):
        for i in range(n):
            w_ref, g_ref, m_ref, v_ref = (refs[j * n + i] for j in range(4))
            d_ref, nm_ref, nv_ref = (refs[(4 + j) * n + i] for j in range(3))
            d_ref[...], nm_ref[...], nv_ref[...] = _adamw_math(w_ref[...], g_ref[...], m_ref[...], v_ref[...])

    shapes = [jax.ShapeDtypeStruct(w.shape, F32) for w in ws]
    outs = pl.pallas_call(body, name="adamw_small", out_shape=shapes * 3)(*ws, *gs, *ms, *vs)
    return outs[:n], outs[n:2 * n], outs[2 * n:]


def kernel(x, norm_mix, w_in, b_gate, rel_bias, w_ret_out, w_att_out, w_out, norm_ffn, w_ffn_gate, w_ffn_up, w_ffn_down, norm_final, loss_target, m_norm_mix, m_w_in, m_b_gate, m_rel_bias, m_w_ret_out, m_w_att_out, m_w_out, m_norm_ffn, m_w_ffn_gate, m_w_ffn_up, m_w_ffn_down, m_norm_final, v_norm_mix, v_w_in, v_b_gate, v_rel_bias, v_w_ret_out, v_w_att_out, v_w_out, v_norm_ffn, v_w_ffn_gate, v_w_ffn_up, v_w_ffn_down, v_norm_final):
    me = _index(_place())
    n_rb = rel_bias.shape[-1]

    shard = dict(w_in_t=w_in[0].T, w_gate_t=w_ffn_gate[0].T, w_up_t=w_ffn_up[0].T, w_down=w_ffn_down[0],
                 w_ret=w_ret_out[0], w_out=w_out[0], w_att_t=w_att_out[0].T)
    shard = {nm: s.astype(BF16) for nm, s in shard.items()}
    dx, slots, small_slots = _step(x, loss_target, norm_mix, b_gate, norm_ffn, norm_final, rel_bias[0], shard)
    small_sum = _sum_slots(small_slots, "sum_small")
    loss = small_sum[5, 0]

    transposed = dict(w_in="w_in_t", w_ffn_gate="w_gate_t", w_ffn_up="w_up_t", w_att_out="w_att_t")
    plain = dict(w_ffn_down="w_down", w_ret_out="w_ret", w_out="w_out")
    g = dict(
        norm_mix=small_sum[0:1], b_gate=jnp.concatenate([small_sum[1:2], small_sum[2:3]], axis=1),
        norm_ffn=small_sum[3:4], norm_final=small_sum[4:5],
        rel_bias=lax.dynamic_slice_in_dim(_bias_rows_bwd(small_sum[8:16]), me * n_rb, n_rb, axis=1),
    )
    w = dict(norm_mix=norm_mix, w_in=w_in, b_gate=b_gate, rel_bias=rel_bias, w_ret_out=w_ret_out, w_att_out=w_att_out,
             w_out=w_out, norm_ffn=norm_ffn, w_ffn_gate=w_ffn_gate, w_ffn_up=w_ffn_up, w_ffn_down=w_ffn_down,
             norm_final=norm_final)
    m = dict(norm_mix=m_norm_mix, w_in=m_w_in, b_gate=m_b_gate, rel_bias=m_rel_bias, w_ret_out=m_w_ret_out,
             w_att_out=m_w_att_out, w_out=m_w_out, norm_ffn=m_norm_ffn, w_ffn_gate=m_w_ffn_gate, w_ffn_up=m_w_ffn_up,
             w_ffn_down=m_w_ffn_down, norm_final=m_norm_final)
    v = dict(norm_mix=v_norm_mix, w_in=v_w_in, b_gate=v_b_gate, rel_bias=v_rel_bias, w_ret_out=v_w_ret_out,
             w_att_out=v_w_att_out, w_out=v_w_out, norm_ffn=v_norm_ffn, w_ffn_gate=v_w_ffn_gate, w_ffn_up=v_w_ffn_up,
             w_ffn_down=v_w_ffn_down, norm_final=v_norm_final)
    order = ("norm_mix", "w_in", "b_gate", "rel_bias", "w_ret_out", "w_att_out", "w_out", "norm_ffn",
             "w_ffn_gate", "w_ffn_up", "w_ffn_down", "norm_final")
    small_names = ("norm_mix", "b_gate", "rel_bias", "norm_ffn", "norm_final")

    def flat(a):
        return a[0] if a.ndim == 3 else a.reshape(-1, a.shape[-1])

    grad, delta, new_m, new_v = {}, {}, {}, {}
    for nm in order:
        if nm in transposed:
            res = _adamw(w[nm][0].T, slots[transposed[nm]], m[nm][0].T, v[nm][0].T, "adamw_" + nm)
            grad[nm], delta[nm], new_m[nm], new_v[nm] = (a.T[None] for a in res)
        elif nm in plain:
            res = _adamw(flat(w[nm]), slots[plain[nm]], flat(m[nm]), flat(v[nm]), "adamw_" + nm)
            grad[nm], delta[nm], new_m[nm], new_v[nm] = (a.reshape(w[nm].shape) for a in res)
    ds, nms, nvs = _adamw_small([flat(w[nm]) for nm in small_names], [g[nm] for nm in small_names],
                                [flat(m[nm]) for nm in small_names], [flat(v[nm]) for nm in small_names])
    for i, nm in enumerate(small_names):
        grad[nm], delta[nm], new_m[nm], new_v[nm] = (a.reshape(w[nm].shape) for a in (g[nm], ds[i], nms[i], nvs[i]))

    return (loss, dx, *[grad[nm] for nm in order], *[delta[nm] for nm in order],
            *[new_m[nm] for nm in order], *[new_v[nm] for nm in order])
```

```python
import numpy as np
import jax
import jax.numpy as jnp
from jax import lax
from jax.experimental import pallas as pl
from jax.experimental.pallas import tpu as pltpu

F32 = jnp.float32
BF16 = jnp.bfloat16
MESH = pl.DeviceIdType.MESH

D_MODEL = 1024
CHUNK = 64
RET_HEADS = 4
RET_KEY_DIM = 128
RET_VAL_DIM = 256
ATT_HEADS = 8
BAND_CHUNKS = 8
MAX_REL = 256
N_REL = CHUNK + MAX_REL
D_FF = 2816
N_IN = 6656
ROPE_BASE = 10000.0
EPS = 1e-6
NEG_INF = -1e30
C_RQ, C_RK, C_RV, C_RG, C_AQ, C_AK, C_AV, C_GL = 0, 512, 1024, 2048, 3072, 3584, 4096, 4608

ADAM_LR = 0.001
ADAM_B1 = 0.9
ADAM_B2 = 0.999
ADAM_EPS = 1e-08
ADAM_WD = 0.01
ADAM_STEP = 10

N_DEV = 8
LANES = 128
RET_TILE = 256
ATT_Q = 256
ATT_PAD = BAND_CHUNKS * CHUNK
ATT_WIN = ATT_PAD + ATT_Q
ATT_STARTS = ATT_PAD // ATT_Q
ATT_ROWS = 32
BIAS_LEN = 1024
VMEM_LIMIT = 48 * 1024 * 1024


def _params(*sem):
    return pltpu.CompilerParams(dimension_semantics=sem, vmem_limit_bytes=VMEM_LIMIT)


def _dot(a, b):
    return lax.dot_general(a, b, (((1,), (0,)), ((), ())), preferred_element_type=F32)


def _dot_nt(a, b):
    return lax.dot_general(a, b, (((1,), (1,)), ((), ())), preferred_element_type=F32)


def _dot_tn(a, b):
    return lax.dot_general(a, b, (((0,), (0,)), ((), ())), preferred_element_type=F32)


def _sigmoid(x):
    return 1.0 / (1.0 + jnp.exp(-x))


def _rms_bwd(x, g, dy):
    r = lax.rsqrt(jnp.mean(x * x, axis=-1, keepdims=True) + EPS)
    u = dy * g
    dx = r * u - x * (r * r * r) * jnp.mean(u * x, axis=-1, keepdims=True)
    return dx, dy * x * r


def _place():
    return lax.axis_index("x"), lax.axis_index("y"), lax.axis_index("c")


def _peer(k):
    x, y, c = _place()
    return ((1 - x) if k & 4 else x, (1 - y) if k & 2 else y, (1 - c) if k & 1 else c)


def _index(place):
    return 4 * place[0] + 2 * place[1] + place[2]


def _rows(ref, block, nrows):
    align = 16 if ref.dtype == BF16 else 8
    return ref.at[pl.ds(pl.multiple_of(block * nrows, align), nrows)]


class _Exchange:
    def __init__(self, arrays, scatter):
        self.arrays, self.scatter, self.n = list(arrays), scatter, len(arrays)

    def out_shape(self):
        if self.scatter:
            return [jax.ShapeDtypeStruct((N_DEV, a.shape[0] // N_DEV) + a.shape[1:], a.dtype) for a in self.arrays]
        return [jax.ShapeDtypeStruct((N_DEV * a.shape[0],) + a.shape[1:], a.dtype) for a in self.arrays]

    def scratch(self):
        return [pltpu.SemaphoreType.DMA((self.n, N_DEV - 1)), pltpu.SemaphoreType.DMA((self.n, N_DEV - 1)),
                pltpu.SemaphoreType.DMA((self.n,))]

    def _copies(self, ins, outs, sems):
        send_sems, recv_sems, local_sems = sems
        me = _index(_place())

        def src(w, to):
            return _rows(ins[w], to, ins[w].shape[0] // N_DEV) if self.scatter else ins[w]

        def dst(w, origin):
            return outs[w].at[origin] if self.scatter else _rows(outs[w], origin, ins[w].shape[0])

        def remote(w, k, to, origin):
            return pltpu.make_async_remote_copy(src_ref=src(w, to), dst_ref=dst(w, origin),
                                                send_sem=send_sems.at[w, k - 1], recv_sem=recv_sems.at[w, k - 1],
                                                device_id=_peer(k), device_id_type=MESH)

        pairs = [(w, k) for w in range(self.n) for k in range(1, N_DEV)]
        own = lambda: [pltpu.make_async_copy(src(w, me), dst(w, me), local_sems.at[w]) for w in range(self.n)]
        sent = lambda: [remote(w, k, _index(_peer(k)), me) for w, k in pairs]
        arriving = lambda: [remote(w, k, me, _index(_peer(k))) for w, k in pairs]
        return own, sent, arriving

    def start(self, ins, outs, sems):
        own, sent, _ = self._copies(ins, outs, sems)
        for cp in own() + sent():
            cp.start()

    def wait(self, ins, outs, sems):
        own, sent, arriving = self._copies(ins, outs, sems)
        for cp in arriving():
            cp.wait_recv()
        for cp in sent():
            cp.wait_send()
        for cp in own():
            cp.wait()


class _PairSwap:
    def __init__(self, arrays):
        self.arrays, self.n = list(arrays), len(arrays)

    def out_shape(self):
        return [jax.ShapeDtypeStruct((4, a.shape[0] // N_DEV) + a.shape[1:], a.dtype) for a in self.arrays]

    def scratch(self):
        return [pltpu.SemaphoreType.DMA((self.n, 4)), pltpu.SemaphoreType.DMA((self.n, 4))]

    def _copies(self, ins, outs, sems):
        send_sems, recv_sems = sems
        x, y, c = _place()
        return [pltpu.make_async_remote_copy(
            src_ref=_rows(ins[w], 2 * q + 1 - c, ins[w].shape[0] // N_DEV), dst_ref=outs[w].at[q],
            send_sem=send_sems.at[w, q], recv_sem=recv_sems.at[w, q],
            device_id=(x, y, 1 - c), device_id_type=MESH) for w in range(self.n) for q in range(4)]

    def start(self, ins, outs, sems):
        for cp in self._copies(ins, outs, sems):
            cp.start()

    def wait(self, ins, outs, sems):
        for cp in self._copies(ins, outs, sems):
            cp.wait()


class _ChipScatter:
    def __init__(self, arrays):
        self.arrays, self.n = list(arrays), len(arrays)

    def out_shape(self):
        return [jax.ShapeDtypeStruct(a.shape, a.dtype) for a in self.arrays]

    def scratch(self):
        return [pltpu.SemaphoreType.DMA((self.n, 3)), pltpu.SemaphoreType.DMA((self.n, 3)),
                pltpu.SemaphoreType.DMA((self.n,))]

    def _copies(self, ins, outs, sems):
        send_sems, recv_sems, local_sems = sems
        x, y, c = _place()
        mine = 2 * x + y
        sent, arriving = [], []
        for w in range(self.n):
            for k in range(1, 4):
                tx, ty = (1 - x) if k & 2 else x, (1 - y) if k & 1 else y
                other = 2 * tx + ty
                sent.append(lambda w=w, k=k, tx=tx, ty=ty, other=other: pltpu.make_async_remote_copy(
                    src_ref=ins[w].at[other], dst_ref=outs[w].at[mine], send_sem=send_sems.at[w, k - 1],
                    recv_sem=recv_sems.at[w, k - 1], device_id=(tx, ty, c), device_id_type=MESH))
                arriving.append(lambda w=w, k=k, tx=tx, ty=ty, other=other: pltpu.make_async_remote_copy(
                    src_ref=ins[w].at[mine], dst_ref=outs[w].at[other], send_sem=send_sems.at[w, k - 1],
                    recv_sem=recv_sems.at[w, k - 1], device_id=(tx, ty, c), device_id_type=MESH))
        own = [lambda w=w: pltpu.make_async_copy(ins[w].at[mine], outs[w].at[mine], local_sems.at[w])
               for w in range(self.n)]
        return own, sent, arriving

    def start(self, ins, outs, sems):
        own, sent, _ = self._copies(ins, outs, sems)
        for cp in own + sent:
            cp().start()

    def wait(self, ins, outs, sems):
        own, sent, arriving = self._copies(ins, outs, sems)
        for cp in arriving:
            cp().wait_recv()
        for cp in sent:
            cp().wait_send()
        for cp in own:
            cp().wait()


class _ChipGather:
    def __init__(self, arrays, parts=None, into=None):
        self.arrays, self.n, self.into = list(arrays), len(arrays), into
        self.parts = parts or [(0, 1)] * self.n

    def out_shape(self):
        return [jax.ShapeDtypeStruct((N_DEV * a.shape[0],) + a.shape[1:], a.dtype) for a in self.arrays]

    def scratch(self):
        return [pltpu.SemaphoreType.DMA((self.n, N_DEV - 1)), pltpu.SemaphoreType.DMA((self.n, N_DEV - 1)),
                pltpu.SemaphoreType.DMA((self.n,))]

    def _parts(self, ins, outs, sems):
        send_sems, recv_sems, local_sems = sems
        x, y, c = _place()
        me, sibling = (x, y, c), (x, y, 1 - c)
        chips = [(1 - x, y), (x, 1 - y), (1 - x, 1 - y)]

        def rows(w, place, whole):
            (index, count), r = self.parts[w], ins[w].shape[0]
            lo, size = (0, r) if whole else (index * (r // count), r // count)
            align = 16 if ins[w].dtype == BF16 else 8
            return outs[w].at[pl.ds(pl.multiple_of(_index(place) * r + lo, align), size)]

        def mine(w, whole):
            (index, count), r = self.parts[w], ins[w].shape[0]
            return ins[w] if whole or count == 1 else ins[w].at[pl.ds(index * (r // count), r // count)]

        def copy(w, k, block, to, own=False):
            whole = k == 0
            return pltpu.make_async_remote_copy(src_ref=mine(w, whole) if own else rows(w, block, whole),
                                                dst_ref=rows(w, block, whole),
                                                send_sem=send_sems.at[w, k], recv_sem=recv_sems.at[w, k],
                                                device_id=to, device_id_type=MESH)

        def local(w):
            return pltpu.make_async_copy(ins[w], rows(w, me, True), local_sems.at[w])

        return me, sibling, chips, c, copy, local, [index == 0 for index, _ in self.parts]

    def start(self, ins, outs, sems):
        me, sibling, chips, c, copy, local, places_own = self._parts(ins, outs, sems)
        for w in range(self.n):
            if places_own[w]:
                local(w).start()
                copy(w, 0, me, sibling, own=True).start()
            for j, chip in enumerate(chips):
                copy(w, 1 + j, me, (*chip, c), own=True).start()

    def wait(self, ins, outs, sems):
        me, sibling, chips, c, copy, local, places_own = self._parts(ins, outs, sems)
        for w in range(self.n):
            for j, chip in enumerate(chips):
                copy(w, 1 + j, (*chip, c), me).wait_recv()
                copy(w, 4 + j, (*chip, c), sibling).start()
        for w in range(self.n):
            if places_own[w]:
                copy(w, 0, sibling, me).wait_recv()
                copy(w, 0, me, sibling, own=True).wait_send()
                local(w).wait()
            for j, chip in enumerate(chips):
                copy(w, 4 + j, (*chip, 1 - c), me).wait_recv()
                copy(w, 1 + j, me, (*chip, c), own=True).wait_send()
                copy(w, 4 + j, (*chip, c), sibling).wait_send()


def _call(body, *, name, grid, in_specs, out_specs, out_shape, scratch=(), semantics, args, exchange=None):
    if exchange is None:
        return pl.pallas_call(body, name=name, grid=grid, in_specs=in_specs, out_specs=out_specs, out_shape=out_shape,
                              scratch_shapes=list(scratch), compiler_params=_params(*semantics))(*args), None
    n_in, n_out, n_scr, nx = len(in_specs), len(out_specs), len(scratch), exchange.n
    into = list(getattr(exchange, "into", None) or [])

    def full_body(*refs):
        ins, refs = refs[:n_in], refs[n_in:]
        x_in, refs = refs[:nx], refs[nx + len(into):]
        outs, refs = refs[:n_out], refs[n_out:]
        x_out, refs = refs[:nx], refs[nx:]
        scr, sems = refs[:n_scr], refs[n_scr:]
        first, last = True, True
        for axis, size in enumerate(grid):
            first = jnp.logical_and(first, pl.program_id(axis) == 0)
            last = jnp.logical_and(last, pl.program_id(axis) == size - 1)
        if grid:
            pl.when(first)(lambda: exchange.start(x_in, x_out, sems))
        else:
            exchange.start(x_in, x_out, sems)
        body(*ins, *outs, *scr)
        if grid:
            pl.when(last)(lambda: exchange.wait(x_in, x_out, sems))
        else:
            exchange.wait(x_in, x_out, sems)

    hbm = pl.BlockSpec(memory_space=pltpu.HBM)
    res = pl.pallas_call(
        full_body, name=name, grid=grid,
        in_specs=list(in_specs) + [hbm] * (nx + len(into)), out_specs=list(out_specs) + [hbm] * nx,
        out_shape=list(out_shape) + exchange.out_shape(),
        scratch_shapes=list(scratch) + exchange.scratch(),
        input_output_aliases={n_in + nx + w: n_out + w for w in range(len(into))},
        compiler_params=_params(*(["arbitrary"] * len(grid))),
    )(*args, *exchange.arrays, *into)
    return res[:n_out], res[n_out:]


def _alone(exchange, name):
    return _call(lambda: None, name=name, grid=(), in_specs=[], out_specs=[], out_shape=[], semantics=(),
                 args=(), exchange=exchange)[1]


def _mm(a, b, *, ta=False, tb=False, out_dtype, tm, tn, tk, name, exchange=None):
    m, k = (a.shape[1], a.shape[0]) if ta else a.shape
    n = b.shape[0] if tb else b.shape[1]
    assert k == (b.shape[1] if tb else b.shape[0])
    tm, tn, tk = min(tm, m), min(tn, n), min(tk, k)
    assert m % tm == 0 and n % tn == 0 and k % tk == 0, (name, m, n, k)
    nk = k // tk
    dims = (((0 if ta else 1,), (1 if tb else 0,)), ((), ()))

    def body(a_ref, b_ref, o_ref, *acc):
        prod = lax.dot_general(a_ref[...].astype(BF16), b_ref[...].astype(BF16), dims, preferred_element_type=F32)
        if nk == 1:
            o_ref[...] = prod.astype(o_ref.dtype)
            return
        acc_ref, kk = acc[0], pl.program_id(2)

        @pl.when(kk == 0)
        def _():
            acc_ref[...] = prod

        @pl.when((kk > 0) & (kk < nk - 1))
        def _():
            acc_ref[...] += prod

        @pl.when(kk == nk - 1)
        def _():
            o_ref[...] = (acc_ref[...] + prod).astype(o_ref.dtype)

    a_spec = (pl.BlockSpec((tk, tm), lambda i, j, kk: (kk, i)) if ta
              else pl.BlockSpec((tm, tk), lambda i, j, kk: (i, kk)))
    b_spec = (pl.BlockSpec((tn, tk), lambda i, j, kk: (j, kk)) if tb
              else pl.BlockSpec((tk, tn), lambda i, j, kk: (kk, j)))
    (out,), moved = _call(
        body, name=name, grid=(m // tm, n // tn, nk),
        in_specs=[a_spec, b_spec],
        out_specs=[pl.BlockSpec((tm, tn), lambda i, j, kk: (i, j))],
        out_shape=[jax.ShapeDtypeStruct((m, n), out_dtype)],
        scratch=[pltpu.VMEM((tm, tn), F32)] if nk > 1 else [],
        semantics=("parallel", "parallel", "arbitrary"), args=(a, b), exchange=exchange)
    return out if exchange is None else (out, moved)


def _mm_pieces(pieces, b, *, ta, out_dtype, tm, tn, tk, name, exchange=None):
    rows, n = pieces[0].shape[0], b.shape[1]
    step = tm if ta else tk
    assert all(p.shape[0] == rows and p.shape[1] % step == 0 for p in pieces), name
    edges = [int(e) for e in np.cumsum([0] + [p.shape[1] // step for p in pieces])]
    total = edges[-1] * step
    m, k = (total, rows) if ta else (rows, total)
    assert b.shape[0] == k and m % tm == 0 and n % tn == 0 and k % tk == 0, name
    nk, npieces = k // tk, len(pieces)
    dims = (((0 if ta else 1,), (0,)), ((), ()))

    def body(*refs):
        a_refs, (b_ref, o_ref, acc_ref) = refs[:npieces], refs[npieces:]
        kk = pl.program_id(2)
        pos = pl.program_id(0) if ta else kk

        @pl.when(kk == 0)
        def _():
            acc_ref[...] = jnp.zeros_like(acc_ref)

        for p, a_ref in enumerate(a_refs):
            @pl.when((pos >= edges[p]) & (pos < edges[p + 1]))
            def _(a_ref=a_ref):
                acc_ref[...] += lax.dot_general(a_ref[...], b_ref[...], dims, preferred_element_type=F32)

        @pl.when(kk == nk - 1)
        def _():
            o_ref[...] = acc_ref[...].astype(o_ref.dtype)

    def a_spec(p):
        lo, last = edges[p], edges[p + 1] - edges[p] - 1
        if ta:
            def index(i, j, kk):
                inside = (i >= lo) & (i <= lo + last)
                return jnp.where(inside, kk, 0), jnp.clip(i - lo, 0, last)
            return pl.BlockSpec((tk, tm), index)
        return pl.BlockSpec((tm, tk), lambda i, j, kk: (i, jnp.clip(kk - lo, 0, last)))

    (out,), moved = _call(
        body, name=name, grid=(m // tm, n // tn, nk),
        in_specs=[a_spec(p) for p in range(npieces)] + [pl.BlockSpec((tk, tn), lambda i, j, kk: (kk, j))],
        out_specs=[pl.BlockSpec((tm, tn), lambda i, j, kk: (i, j))],
        out_shape=[jax.ShapeDtypeStruct((m, n), out_dtype)],
        scratch=[pltpu.VMEM((tm, tn), F32)],
        semantics=("parallel", "parallel", "arbitrary"), args=(*pieces, b), exchange=exchange)
    return out if exchange is None else (out, moved)


def _rms_fwd(x2, g, exchange):
    t = x2.shape[0]
    tm = min(512, t)

    def body(x_ref, g_ref, o_ref):
        x = x_ref[...]
        r = lax.rsqrt(jnp.mean(x * x, axis=-1, keepdims=True) + EPS)
        o_ref[...] = (x * r * g_ref[...]).astype(o_ref.dtype)

    return _call(
        body, name="rms_in_fwd", grid=(t // tm,),
        in_specs=[pl.BlockSpec((tm, D_MODEL), lambda i: (i, 0)), pl.BlockSpec((1, D_MODEL), lambda i: (0, 0))],
        out_specs=[pl.BlockSpec((tm, D_MODEL), lambda i: (i, 0))],
        out_shape=[jax.ShapeDtypeStruct((t, D_MODEL), BF16)],
        semantics=("parallel",), args=(x2, g), exchange=exchange)


def _decay(lg):
    row = lax.broadcasted_iota(jnp.int32, (RET_TILE, RET_TILE), 0)
    col = lax.broadcasted_iota(jnp.int32, (RET_TILE, RET_TILE), 1)
    within = jnp.exp(lg * jnp.abs(row - col).astype(F32))
    inside = jnp.where((col >> 6) <= (row >> 6), within, 0.0)
    pos = lax.broadcasted_iota(jnp.int32, (RET_TILE, 1), 0).astype(F32)
    q_dec = jnp.exp(lg * (pos + 1.0))
    k_dec = jnp.exp(lg * (RET_TILE - 1.0 - pos))
    tile_dec = jnp.exp(lg * float(RET_TILE))
    return inside, q_dec, k_dec, tile_dec


def _scaled(a_bf16, dec):
    return (a_bf16.astype(F32) * dec).astype(BF16)


def _ret_fwd(proj, cs, sn, lg_arr, batch, seq, exchange):
    t = batch * seq
    nt = seq // RET_TILE

    def body(q_ref, k_ref, v_ref, rg_ref, cs_ref, sn_ref, lg_ref, gro_ref, o_ref, qr_ref, kr_ref):
        lg = lg_ref[:, 0:1]
        cs_t, sn_t = cs_ref[...], sn_ref[...]
        q = q_ref[...].astype(F32)
        k = k_ref[...].astype(F32)
        qr_ref[...] = (q * cs_t + pltpu.roll(q, 64, 1) * sn_t).astype(BF16)
        kr_ref[...] = ((k * cs_t + pltpu.roll(k, 64, 1) * sn_t) * (RET_KEY_DIM ** -0.5)).astype(BF16)
        inside, q_dec, k_dec, tile_dec = _decay(lg)
        state = jnp.zeros((RET_KEY_DIM, RET_VAL_DIM), F32)
        for i in range(nt):
            rows = slice(i * RET_TILE, (i + 1) * RET_TILE)
            qi, ki, vi = qr_ref[rows, :], kr_ref[rows, :], v_ref[rows, :]
            acc = _dot((_dot_nt(qi, ki) * inside).astype(BF16), vi)
            if i > 0:
                acc = acc + _dot(_scaled(qi, q_dec), state.astype(BF16))
            if i < nt - 1:
                state = state * tile_dec + _dot_tn(_scaled(ki, k_dec), vi)
            o_ref[rows, :] = acc
            xc = acc - jnp.mean(acc, axis=-1, keepdims=True)
            nrm = xc * lax.rsqrt(jnp.mean(xc * xc, axis=-1, keepdims=True) + EPS)
            rg = rg_ref[rows, :].astype(F32)
            gro_ref[rows, :] = (rg * _sigmoid(rg) * nrm).astype(BF16)

    def col(base, width):
        return lambda b, h: (b, base // width + h)

    return _call(
        body, name="ret_fwd", grid=(batch, RET_HEADS),
        in_specs=[pl.BlockSpec((seq, RET_KEY_DIM), col(C_RQ, RET_KEY_DIM)),
                  pl.BlockSpec((seq, RET_KEY_DIM), col(C_RK, RET_KEY_DIM)),
                  pl.BlockSpec((seq, RET_VAL_DIM), col(C_RV, RET_VAL_DIM)),
                  pl.BlockSpec((seq, RET_VAL_DIM), col(C_RG, RET_VAL_DIM)),
                  pl.BlockSpec((seq, RET_KEY_DIM), lambda b, h: (0, 0)),
                  pl.BlockSpec((seq, RET_KEY_DIM), lambda b, h: (0, 0)),
                  pl.BlockSpec((None, 1, LANES), lambda b, h: (h, 0, 0))],
        out_specs=[pl.BlockSpec((seq, RET_VAL_DIM), lambda b, h: (b, h)),
                   pl.BlockSpec((seq, RET_VAL_DIM), lambda b, h: (b, h)),
                   pl.BlockSpec((seq, RET_KEY_DIM), lambda b, h: (b, h)),
                   pl.BlockSpec((seq, RET_KEY_DIM), lambda b, h: (b, h))],
        out_shape=[jax.ShapeDtypeStruct((t, RET_HEADS * RET_VAL_DIM), BF16),
                   jax.ShapeDtypeStruct((t, RET_HEADS * RET_VAL_DIM), F32),
                   jax.ShapeDtypeStruct((t, RET_HEADS * RET_KEY_DIM), BF16),
                   jax.ShapeDtypeStruct((t, RET_HEADS * RET_KEY_DIM), BF16)],
        semantics=("parallel", "parallel"), args=(proj, proj, proj, proj, cs, sn, lg_arr), exchange=exchange)


def _att_bias(w_ref, bias_ref):
    n_i = lax.broadcasted_iota(jnp.int32, (ATT_Q, BIAS_LEN), 0)
    qc = lax.broadcasted_iota(jnp.int32, (ATT_Q, ATT_WIN), 0) >> 6
    kc = lax.broadcasted_iota(jnp.int32, (ATT_Q, ATT_WIN), 1) >> 6
    dc = qc + BAND_CHUNKS - kc
    band = (dc >= 0) & (dc <= BAND_CHUNKS)
    key = lax.broadcasted_iota(jnp.int32, (ATT_Q, ATT_WIN), 1)
    for e in range(2):
        xw = jnp.broadcast_to(w_ref[e:e + 1, :], (ATT_Q, BIAS_LEN))
        for bit in range(8):
            xw = jnp.where(((n_i >> bit) & 1) == 1, pltpu.roll(xw, 1 << bit, 1), xw)
        bias = jnp.where(band, xw[:, BIAS_LEN - ATT_WIN:], NEG_INF)
        for first in range(ATT_STARTS):
            bias_ref[first, e] = jnp.where(key + (first * ATT_Q - ATT_PAD) >= 0, bias, NEG_INF)
        bias_ref[ATT_STARTS, e] = bias


def _att_specs(batch, seq):
    ni = seq // ATT_Q
    q_spec = pl.BlockSpec((ATT_Q, LANES), lambda hp, b, i: (b * ni + i, C_AQ // LANES + hp))
    k_spec = pl.BlockSpec((seq, LANES), lambda hp, b, i: (b, C_AK // LANES + hp))
    v_spec = pl.BlockSpec((seq, LANES), lambda hp, b, i: (b, C_AV // LANES + hp))
    w_spec = pl.BlockSpec((None, 2, BIAS_LEN), lambda hp, b, i: (hp, 0, 0))
    b_spec = pl.BlockSpec((None, ATT_STARTS + 1, 2, ATT_Q, ATT_WIN), lambda hp, b, i: (hp, 0, 0, 0, 0))
    pad = pltpu.VMEM((seq + ATT_PAD, LANES), BF16)
    return ni, q_spec, k_spec, v_spec, w_spec, b_spec, pad


def _att_bias_tiles(wvec, exchange):
    (tiles,), moved = _call(
        lambda w_ref, o_ref: _att_bias(w_ref, o_ref), name="att_bias", grid=(ATT_HEADS // 2,),
        in_specs=[pl.BlockSpec((None, 2, BIAS_LEN), lambda hp: (hp, 0, 0))],
        out_specs=[pl.BlockSpec((None, ATT_STARTS + 1, 2, ATT_Q, ATT_WIN), lambda hp: (hp, 0, 0, 0, 0))],
        out_shape=[jax.ShapeDtypeStruct((ATT_HEADS // 2, ATT_STARTS + 1, 2, ATT_Q, ATT_WIN), F32)],
        semantics=("parallel",), args=(wvec,), exchange=exchange)
    return tiles, moved


def _att_pad(src_ref, pad_ref):
    pad_ref[:ATT_PAD, :] = jnp.zeros((ATT_PAD, LANES), BF16)
    pad_ref[ATT_PAD:, :] = src_ref[...]


def _att_head(q2, sel):
    return jnp.where(sel, q2, jnp.zeros_like(q2)) * 0.125


def _att_softmax_rows(s_ref, bias_ref, rows):
    s = s_ref[rows, :] + bias_ref[rows, :]
    ex = jnp.exp(s - jnp.max(s, axis=-1, keepdims=True))
    return ex, 1.0 / jnp.sum(ex, axis=-1, keepdims=True)


def _att_fwd(proj, bias, batch, seq, exchange):
    ni, q_spec, k_spec, v_spec, _, b_spec, pad = _att_specs(batch, seq)

    def body(q_ref, k_ref, v_ref, bias_ref, o_ref, kp_ref, vp_ref, s_ref, e_ref):
        i = pl.program_id(2)

        @pl.when(i == 0)
        def _():
            _att_pad(k_ref, kp_ref)
            _att_pad(v_ref, vp_ref)

        win = pl.ds(pl.multiple_of(i * ATT_Q, ATT_Q), ATT_WIN)
        k2, v2, q2 = kp_ref[win, :], vp_ref[win, :], q_ref[...]
        lo = lax.broadcasted_iota(jnp.int32, (1, LANES), 1) < 64
        start = jnp.minimum(i, ATT_STARTS)
        out = jnp.zeros((ATT_Q, LANES), F32)
        for e in range(2):
            sel = lo if e == 0 else jnp.logical_not(lo)
            s_ref[e] = _dot_nt(_att_head(q2, sel), k2)
            rsum = []
            for c in range(ATT_Q // ATT_ROWS):
                rows = slice(c * ATT_ROWS, (c + 1) * ATT_ROWS)
                ex, r = _att_softmax_rows(s_ref.at[e], bias_ref.at[start, e], rows)
                e_ref[e, rows, :] = ex.astype(BF16)
                rsum.append(r)
            out = out + _dot(e_ref[e], jnp.where(sel, v2, jnp.zeros_like(v2))) * jnp.concatenate(rsum, axis=0)
        o_ref[...] = out.astype(BF16)

    return _call(
        body, name="att_fwd", grid=(ATT_HEADS // 2, batch, ni),
        in_specs=[q_spec, k_spec, v_spec, b_spec],
        out_specs=[pl.BlockSpec((ATT_Q, LANES), lambda hp, b, i: (b * ni + i, hp))],
        out_shape=[jax.ShapeDtypeStruct((batch * seq, ATT_HEADS * 64), BF16)],
        scratch=[pad, pad, pltpu.VMEM((2, ATT_Q, ATT_WIN), F32), pltpu.VMEM((2, ATT_Q, ATT_WIN), BF16)],
        semantics=("arbitrary", "arbitrary", "arbitrary"), args=(proj, proj, proj, bias), exchange=exchange)


GL_HALF = 512


def _gl_specs(tm):
    return [pl.BlockSpec((tm, GL_HALF), lambda i, c=C_GL // GL_HALF + j: (i, c)) for j in range(4)]


def _gates(gl_refs, b_ref):
    logits = [ref[...].astype(F32) for ref in gl_refs]
    gr = _sigmoid(jnp.concatenate(logits[:2], axis=1) + b_ref[:, :D_MODEL])
    ga = _sigmoid(jnp.concatenate(logits[2:], axis=1) + b_ref[:, D_MODEL:])
    return gr, ga


def _whole(a):
    return pl.BlockSpec(a.shape, lambda i: (0,) * a.ndim)


def _mix_out_fwd(gro, ao, proj, b_gate, w_ret, w_att_t, x2, w_out, g2):
    t = gro.shape[0]
    tm = min(256, t)

    def body(gro_ref, ao_ref, gl0, gl1, gl2, gl3, b_ref, wr_ref, wa_ref, x_ref, wo_ref, g_ref,
             z_ref, yr_ref, ya_ref, h_ref, hn_ref):
        yr = _dot(gro_ref[...], wr_ref[...])
        ya = _dot_nt(ao_ref[...], wa_ref[...])
        yr_ref[...] = yr.astype(BF16)
        ya_ref[...] = ya.astype(BF16)
        gr, ga = _gates((gl0, gl1, gl2, gl3), b_ref)
        z = (gr * yr + ga * ya).astype(BF16)
        z_ref[...] = z
        h = x_ref[...] + _dot(z, wo_ref[...])
        h_ref[...] = h
        r = lax.rsqrt(jnp.mean(h * h, axis=-1, keepdims=True) + EPS)
        hn_ref[...] = (h * r * g_ref[...]).astype(BF16)

    row = pl.BlockSpec((tm, D_MODEL), lambda i: (i, 0))
    return pl.pallas_call(
        body, name="mix_out_fwd", grid=(t // tm,),
        in_specs=[row, pl.BlockSpec((tm, 512), lambda i: (i, 0)), *_gl_specs(tm),
                  _whole(b_gate), _whole(w_ret), _whole(w_att_t), row, _whole(w_out), _whole(g2)],
        out_specs=[row] * 5,
        out_shape=[jax.ShapeDtypeStruct((t, D_MODEL), BF16)] * 3
                  + [jax.ShapeDtypeStruct((t, D_MODEL), F32), jax.ShapeDtypeStruct((t, D_MODEL), BF16)],
        compiler_params=_params("parallel"),
    )(gro, ao, proj, proj, proj, proj, b_gate, w_ret, w_att_t, x2, w_out, g2)


def _ffn_up(hn, wg_t, wu_t):
    t = hn.shape[0]
    tm, tn = min(512, t), D_FF // 2

    def body(h_ref, wg_ref, wu_ref, g_ref, u_ref, a_ref):
        g = _dot_nt(h_ref[...], wg_ref[...])
        u = _dot_nt(h_ref[...], wu_ref[...])
        g_ref[...] = g.astype(BF16)
        u_ref[...] = u.astype(BF16)
        a_ref[...] = (g * _sigmoid(g) * u).astype(BF16)

    w_spec = pl.BlockSpec((tn, D_MODEL), lambda j, i: (j, 0))
    out = pl.BlockSpec((tm, tn), lambda j, i: (i, j))
    return pl.pallas_call(
        body, name="ffn_up", grid=(D_FF // tn, t // tm),
        in_specs=[pl.BlockSpec((tm, D_MODEL), lambda j, i: (i, 0)), w_spec, w_spec],
        out_specs=[out, out, out],
        out_shape=[jax.ShapeDtypeStruct((t, D_FF), BF16)] * 3,
        compiler_params=_params("parallel", "parallel"),
    )(hn, wg_t, wu_t)


def _ffn_down_loss(a, h1, tgt, w_down, g3):
    t = a.shape[0]
    tm = min(512, t)

    def body(a_ref, h_ref, t_ref, w_ref, g_ref, dh_ref, dhb_ref, loss_ref, dg_ref):
        @pl.when(pl.program_id(0) == 0)
        def _():
            loss_ref[...] = jnp.zeros_like(loss_ref)
            dg_ref[...] = jnp.zeros_like(dg_ref)

        g = g_ref[...]
        h2 = h_ref[...] + _dot(a_ref[...], w_ref[...])
        r = lax.rsqrt(jnp.mean(h2 * h2, axis=-1, keepdims=True) + EPS)
        err = h2 * r * g - t_ref[...]
        loss_ref[...] += jnp.sum(err * err) * (0.5 / D_MODEL)
        dy = err * (1.0 / D_MODEL)
        dh, dg_rows = _rms_bwd(h2, g, dy)
        dg_ref[...] += jnp.sum(dg_rows, axis=0, keepdims=True)
        dh_ref[...] = dh
        dhb_ref[...] = dh.astype(BF16)

    row = pl.BlockSpec((tm, D_MODEL), lambda i: (i, 0))
    vec = pl.BlockSpec((1, D_MODEL), lambda i: (0, 0))
    return pl.pallas_call(
        body, name="ffn_down_loss", grid=(t // tm,),
        in_specs=[pl.BlockSpec((tm, D_FF), lambda i: (i, 0)), row, row,
                  pl.BlockSpec((D_FF, D_MODEL), lambda i: (0, 0)), vec],
        out_specs=[row, row, pl.BlockSpec((1, LANES), lambda i: (0, 0)), vec],
        out_shape=[jax.ShapeDtypeStruct((t, D_MODEL), F32), jax.ShapeDtypeStruct((t, D_MODEL), BF16),
                   jax.ShapeDtypeStruct((1, LANES), F32), jax.ShapeDtypeStruct((1, D_MODEL), F32)],
        compiler_params=_params("arbitrary"),
    )(a, h1, tgt, w_down, g3)


def _ffn_bwd_act(dh2b, w_down, g_act, u_act, exchange):
    t = dh2b.shape[0]
    tm, tn = min(512, t), D_FF // 2

    def body(d_ref, w_ref, g_ref, u_ref, dg_ref, du_ref):
        da = _dot_nt(d_ref[...], w_ref[...])
        g = g_ref[...].astype(F32)
        u = u_ref[...].astype(F32)
        sg = _sigmoid(g)
        dg_ref[...] = (da * u * sg * (1.0 + g * (1.0 - sg))).astype(BF16)
        du_ref[...] = (da * g * sg).astype(BF16)

    blk = pl.BlockSpec((tm, tn), lambda j, i: (i, j))
    return _call(
        body, name="ffn_bwd_act", grid=(D_FF // tn, t // tm),
        in_specs=[pl.BlockSpec((tm, D_MODEL), lambda j, i: (i, 0)),
                  pl.BlockSpec((tn, D_MODEL), lambda j, i: (j, 0)), blk, blk],
        out_specs=[blk, blk],
        out_shape=[jax.ShapeDtypeStruct((t, D_FF), BF16)] * 2,
        semantics=("parallel", "parallel"), args=(dh2b, w_down, g_act, u_act), exchange=exchange)


def _ffn_bwd_in(dg, du, wg_t, wu_t, h1, dh2, g2, exchange):
    t = dg.shape[0]
    tm, tk = min(512, t), D_FF // 2
    nk = D_FF // tk

    def body(dg_ref, du_ref, wg_ref, wu_ref, h_ref, d2_ref, g_ref, dh_ref, dhb_ref, gn_ref, acc_ref):
        i, kk = pl.program_id(0), pl.program_id(1)

        @pl.when((i == 0) & (kk == 0))
        def _():
            gn_ref[...] = jnp.zeros_like(gn_ref)

        @pl.when(kk == 0)
        def _():
            acc_ref[...] = jnp.zeros_like(acc_ref)

        acc_ref[...] += _dot(dg_ref[...], wg_ref[...]) + _dot(du_ref[...], wu_ref[...])

        @pl.when(kk == nk - 1)
        def _():
            dx, dg_rows = _rms_bwd(h_ref[...], g_ref[...], acc_ref[...])
            dh = d2_ref[...] + dx
            dh_ref[...] = dh
            dhb_ref[...] = dh.astype(BF16)
            gn_ref[...] += jnp.sum(dg_rows, axis=0, keepdims=True)

    act = pl.BlockSpec((tm, tk), lambda i, kk: (i, kk))
    wsp = pl.BlockSpec((tk, D_MODEL), lambda i, kk: (kk, 0))
    row = pl.BlockSpec((tm, D_MODEL), lambda i, kk: (i, 0))
    vec = pl.BlockSpec((1, D_MODEL), lambda i, kk: (0, 0))
    return _call(
        body, name="ffn_bwd_in", grid=(t // tm, nk),
        in_specs=[act, act, wsp, wsp, row, row, vec],
        out_specs=[row, row, vec],
        out_shape=[jax.ShapeDtypeStruct((t, D_MODEL), F32), jax.ShapeDtypeStruct((t, D_MODEL), BF16),
                   jax.ShapeDtypeStruct((1, D_MODEL), F32)],
        scratch=[pltpu.VMEM((tm, D_MODEL), F32)],
        semantics=("arbitrary", "arbitrary"), args=(dg, du, wg_t, wu_t, h1, dh2, g2), exchange=exchange)


def _mix_bwd(dh1b, w_out, proj, b_gate, y_ret, y_att, w_ret, w_att_t):
    t = dh1b.shape[0]
    tm = min(256, t)

    def body(d_ref, wo_ref, gl0, gl1, gl2, gl3, b_ref, yr_ref, ya_ref, wr_ref, wa_ref,
             dyr_ref, dya_ref, dglr_ref, dgla_ref, dgro_ref, dao_ref, db_ref):
        @pl.when(pl.program_id(0) == 0)
        def _():
            db_ref[...] = jnp.zeros_like(db_ref)

        dz = _dot_nt(d_ref[...], wo_ref[...])
        gr, ga = _gates((gl0, gl1, gl2, gl3), b_ref)
        dyr = (dz * gr).astype(BF16)
        dya = (dz * ga).astype(BF16)
        dyr_ref[...] = dyr
        dya_ref[...] = dya
        dglr = dz * yr_ref[...].astype(F32) * gr * (1.0 - gr)
        dgla = dz * ya_ref[...].astype(F32) * ga * (1.0 - ga)
        dglr_ref[...] = dglr.astype(BF16)
        dgla_ref[...] = dgla.astype(BF16)
        db_ref[:, :D_MODEL] += jnp.sum(dglr, axis=0, keepdims=True)
        db_ref[:, D_MODEL:] += jnp.sum(dgla, axis=0, keepdims=True)
        dgro_ref[...] = _dot_nt(dyr, wr_ref[...]).astype(BF16)
        dao_ref[...] = _dot(dya, wa_ref[...]).astype(BF16)

    row = pl.BlockSpec((tm, D_MODEL), lambda i: (i, 0))
    half = pl.BlockSpec((tm, 512), lambda i: (i, 0))
    return pl.pallas_call(
        body, name="mix_bwd", grid=(t // tm,),
        in_specs=[row, _whole(w_out), *_gl_specs(tm), _whole(b_gate), row, row, _whole(w_ret), _whole(w_att_t)],
        out_specs=[row, row, row, row, row, half, _whole(b_gate)],
        out_shape=[jax.ShapeDtypeStruct((t, D_MODEL), BF16)] * 5
                  + [jax.ShapeDtypeStruct((t, 512), BF16), jax.ShapeDtypeStruct(b_gate.shape, F32)],
        compiler_params=_params("arbitrary"),
    )(dh1b, w_out, proj, proj, proj, proj, b_gate, y_ret, y_att, w_ret, w_att_t)


def _ret_bwd(dgro, proj, o_ret, qr, kr, cs, sn, lg_arr, batch, seq, exchange):
    t = batch * seq
    nt = seq // RET_TILE

    def body(dgro_ref, rg_ref, o_ref, qr_ref, kr_ref, v_ref, cs_ref, sn_ref, lg_ref,
             dq_ref, dk_ref, dv_ref, drg_ref, do_ref, st_ref):
        lg = lg_ref[:, 0:1]
        inside, q_dec, k_dec, tile_dec = _decay(lg)

        state = jnp.zeros((RET_KEY_DIM, RET_VAL_DIM), F32)
        for i in range(nt - 1):
            rows = slice(i * RET_TILE, (i + 1) * RET_TILE)
            state = state * tile_dec + _dot_tn(_scaled(kr_ref[rows, :], k_dec), v_ref[rows, :])
            st_ref[i + 1] = state.astype(BF16)

        for i in range(nt):
            rows = slice(i * RET_TILE, (i + 1) * RET_TILE)
            o = o_ref[rows, :]
            xc = o - jnp.mean(o, axis=-1, keepdims=True)
            rs = lax.rsqrt(jnp.mean(xc * xc, axis=-1, keepdims=True) + EPS)
            nrm = xc * rs
            rg = rg_ref[rows, :].astype(F32)
            sg = _sigmoid(rg)
            dg = dgro_ref[rows, :].astype(F32)
            drg_ref[rows, :] = (dg * nrm * sg * (1.0 + rg * (1.0 - sg))).astype(BF16)
            dn = dg * rg * sg
            do = rs * (dn - jnp.mean(dn, axis=-1, keepdims=True)
                       - nrm * jnp.mean(dn * nrm, axis=-1, keepdims=True))
            do_ref[rows, :] = do.astype(BF16)

        dstate = jnp.zeros((RET_KEY_DIM, RET_VAL_DIM), F32)
        for i in reversed(range(nt)):
            rows = slice(i * RET_TILE, (i + 1) * RET_TILE)
            qi, ki, vi, doi = qr_ref[rows, :], kr_ref[rows, :], v_ref[rows, :], do_ref[rows, :]
            p = (_dot_nt(qi, ki) * inside).astype(BF16)
            dp = (_dot_nt(doi, vi) * inside).astype(BF16)
            dq = _dot(dp, ki)
            dk = _dot_tn(dp, qi)
            dv = _dot_tn(p, doi)
            if i > 0:
                dq = dq + _dot_nt(doi, st_ref[i]) * q_dec
            if i < nt - 1:
                dsb = dstate.astype(BF16)
                dk = dk + _dot_nt(vi, dsb) * k_dec
                dv = dv + _dot(_scaled(ki, k_dec), dsb)
            if i > 0:
                dstate = dstate * tile_dec + _dot_tn(_scaled(qi, q_dec), doi)
            dq_ref[rows, :] = (dq * cs_ref[rows, :] - pltpu.roll(dq, 64, 1) * sn_ref[rows, :]).astype(BF16)
            dk = (dk * cs_ref[rows, :] - pltpu.roll(dk, 64, 1) * sn_ref[rows, :]) * (RET_KEY_DIM ** -0.5)
            dk_ref[rows, :] = dk.astype(BF16)
            dv_ref[rows, :] = dv.astype(BF16)

    key = pl.BlockSpec((seq, RET_KEY_DIM), lambda b, h: (b, h))
    val = pl.BlockSpec((seq, RET_VAL_DIM), lambda b, h: (b, h))
    tab = pl.BlockSpec((seq, RET_KEY_DIM), lambda b, h: (0, 0))
    return _call(
        body, name="ret_bwd", grid=(batch, RET_HEADS),
        in_specs=[val, pl.BlockSpec((seq, RET_VAL_DIM), lambda b, h: (b, C_RG // RET_VAL_DIM + h)), val, key, key,
                  pl.BlockSpec((seq, RET_VAL_DIM), lambda b, h: (b, C_RV // RET_VAL_DIM + h)), tab, tab,
                  pl.BlockSpec((None, 1, LANES), lambda b, h: (h, 0, 0))],
        out_specs=[key, key, val, val],
        out_shape=[jax.ShapeDtypeStruct((t, RET_HEADS * RET_KEY_DIM), BF16)] * 2
                  + [jax.ShapeDtypeStruct((t, RET_HEADS * RET_VAL_DIM), BF16)] * 2,
        scratch=[pltpu.VMEM((seq, RET_VAL_DIM), BF16), pltpu.VMEM((nt, RET_KEY_DIM, RET_VAL_DIM), BF16)],
        semantics=("parallel", "parallel"), args=(dgro, proj, o_ret, qr, kr, proj, cs, sn, lg_arr),
        exchange=exchange)


def _att_bwd(proj, bias, dao, batch, seq, exchange):
    ni, q_spec, k_spec, v_spec, w_spec, b_spec, pad = _att_specs(batch, seq)
    t = batch * seq

    def body(q_ref, k_ref, v_ref, bias_ref, do_ref, dq_ref, dk_ref, dv_ref, dw_ref,
             dbias_ref, dk_acc, dv_acc, kp_ref, vp_ref, s_ref, dp_ref, e_ref, ds_ref):
        b, i = pl.program_id(1), pl.program_id(2)

        @pl.when((b == 0) & (i == 0))
        def _():
            dbias_ref[...] = jnp.zeros_like(dbias_ref)

        @pl.when(i == 0)
        def _():
            _att_pad(k_ref, kp_ref)
            _att_pad(v_ref, vp_ref)
            dk_acc[...] = jnp.zeros_like(dk_acc)
            dv_acc[...] = jnp.zeros_like(dv_acc)

        win = pl.ds(pl.multiple_of(i * ATT_Q, ATT_Q), ATT_WIN)
        k2, v2, q2, do2 = kp_ref[win, :], vp_ref[win, :], q_ref[...], do_ref[...]
        lo = lax.broadcasted_iota(jnp.int32, (1, LANES), 1) < 64
        dq = jnp.zeros((ATT_Q, LANES), F32)
        dk = jnp.zeros((LANES, ATT_WIN), F32)
        dv = jnp.zeros((LANES, ATT_WIN), F32)
        start = jnp.minimum(i, ATT_STARTS)
        for e in range(2):
            sel = lo if e == 0 else jnp.logical_not(lo)
            qm = _att_head(q2, sel)
            dom = jnp.where(sel, do2, jnp.zeros_like(do2))
            s_ref[e] = _dot_nt(qm, k2)
            dp_ref[e] = _dot_nt(dom, v2)
            rsum = []
            for c in range(ATT_Q // ATT_ROWS):
                rows = slice(c * ATT_ROWS, (c + 1) * ATT_ROWS)
                ex, r = _att_softmax_rows(s_ref.at[e], bias_ref.at[start, e], rows)
                dp = dp_ref[e, rows, :]
                mean = jnp.sum(dp * ex, axis=-1, keepdims=True) * r
                ds = ex * ((dp - mean) * r)
                dbias_ref[e, rows, :] += ds
                ds_ref[e, rows, :] = ds.astype(BF16)
                e_ref[e, rows, :] = ex.astype(BF16)
                rsum.append(r)
            dq = dq + _dot(ds_ref[e], jnp.where(sel, k2, jnp.zeros_like(k2)))
            dk = dk + _dot_tn(qm, ds_ref[e])
            dv = dv + _dot_tn((dom.astype(F32) * jnp.concatenate(rsum, axis=0)).astype(BF16), e_ref[e])
        dq_ref[...] = (dq * 0.125).astype(BF16)
        dk_acc[:, win] += dk
        dv_acc[:, win] += dv

        @pl.when(i == ni - 1)
        def _():
            dk_ref[...] = dk_acc[:, ATT_PAD:].T.astype(BF16)
            dv_ref[...] = dv_acc[:, ATT_PAD:].T.astype(BF16)

        @pl.when((b == batch - 1) & (i == ni - 1))
        def _():
            n_i = lax.broadcasted_iota(jnp.int32, (ATT_Q, BIAS_LEN), 0)
            for e in range(2):
                xw = jnp.concatenate([jnp.zeros((ATT_Q, BIAS_LEN - ATT_WIN), F32), dbias_ref[e]], axis=1)
                for bit in range(8):
                    xw = jnp.where(((n_i >> bit) & 1) == 1, pltpu.roll(xw, BIAS_LEN - (1 << bit), 1), xw)
                dw_ref[e:e + 1, :] = jnp.sum(xw, axis=0, keepdims=True)

    seq_blk = pl.BlockSpec((seq, LANES), lambda hp, b, i: (b, hp))
    q_out = pl.BlockSpec((ATT_Q, LANES), lambda hp, b, i: (b * ni + i, hp))
    return _call(
        body, name="att_bwd", grid=(ATT_HEADS // 2, batch, ni),
        in_specs=[q_spec, k_spec, v_spec, b_spec, q_out],
        out_specs=[q_out, seq_blk, seq_blk, w_spec],
        out_shape=[jax.ShapeDtypeStruct((t, 512), BF16)] * 3
                  + [jax.ShapeDtypeStruct((ATT_HEADS // 2, 2, BIAS_LEN), F32)],
        scratch=[pltpu.VMEM((2, ATT_Q, ATT_WIN), F32),
                 pltpu.VMEM((LANES, seq + ATT_PAD), F32), pltpu.VMEM((LANES, seq + ATT_PAD), F32), pad, pad,
                 pltpu.VMEM((2, ATT_Q, ATT_WIN), F32), pltpu.VMEM((2, ATT_Q, ATT_WIN), F32),
                 pltpu.VMEM((2, ATT_Q, ATT_WIN), BF16), pltpu.VMEM((2, ATT_Q, ATT_WIN), BF16)],
        semantics=("arbitrary", "arbitrary", "arbitrary"), args=(proj, proj, proj, bias, dao), exchange=exchange)


def _rms_in_bwd(x2, dxn, dh1, g1):
    t = x2.shape[0]
    tm = min(512, t)

    def body(x_ref, d_ref, h_ref, g_ref, dx_ref, dg_ref):
        @pl.when(pl.program_id(0) == 0)
        def _():
            dg_ref[...] = jnp.zeros_like(dg_ref)

        dx, dg_rows = _rms_bwd(x_ref[...], g_ref[...], d_ref[...])
        dx_ref[...] = h_ref[...] + dx
        dg_ref[...] += jnp.sum(dg_rows, axis=0, keepdims=True)

    row = pl.BlockSpec((tm, D_MODEL), lambda i: (i, 0))
    vec = pl.BlockSpec((1, D_MODEL), lambda i: (0, 0))
    return pl.pallas_call(
        body, name="rms_in_bwd", grid=(t // tm,),
        in_specs=[row, row, row, vec], out_specs=[row, vec],
        out_shape=[jax.ShapeDtypeStruct((t, D_MODEL), F32), jax.ShapeDtypeStruct((1, D_MODEL), F32)],
        compiler_params=_params("arbitrary"),
    )(x2, dxn, dh1, g1)


def _pack_small(dg1, dbr, dba, dg2, dg3, dw, loss):
    def body(a_ref, b_ref, c_ref, d_ref, e_ref, w_ref, l_ref, o_ref):
        o_ref[...] = jnp.zeros_like(o_ref)
        for r, ref in enumerate((a_ref, b_ref, c_ref, d_ref, e_ref)):
            o_ref[r:r + 1, :] = ref[...]
        o_ref[5:6, 0:LANES] = l_ref[...]
        for hp in range(ATT_HEADS // 2):
            o_ref[8 + 2 * hp:10 + 2 * hp, :] = w_ref[hp]

    return pl.pallas_call(body, name="pack_small",
                          out_shape=jax.ShapeDtypeStruct((16, D_MODEL), F32))(dg1, dbr, dba, dg2, dg3, dw, loss)


def _rotary_tables(seq):
    freqs = ROPE_BASE ** (-jnp.arange(0, RET_KEY_DIM, 2, dtype=F32) / RET_KEY_DIM)
    ang = jnp.arange(seq, dtype=F32)[:, None] * freqs[None, :]
    cos, sin = jnp.cos(ang), jnp.sin(ang)
    return jnp.concatenate([cos, cos], axis=1), jnp.concatenate([-sin, sin], axis=1)


def _bias_rows(rel_bias):
    n_far = BIAS_LEN - ATT_Q - MAX_REL + 1
    n_near = BIAS_LEN - n_far - (N_REL - 2)
    w = jnp.concatenate([jnp.broadcast_to(rel_bias[:, N_REL - 1:], (ATT_HEADS, n_far)),
                         rel_bias[:, 1:N_REL - 1][:, ::-1],
                         jnp.broadcast_to(rel_bias[:, :1], (ATT_HEADS, n_near))], axis=1)
    return w.reshape(ATT_HEADS // 2, 2, BIAS_LEN)


def _bias_rows_bwd(dw):
    n_far = BIAS_LEN - ATT_Q - MAX_REL + 1
    mid = dw[:, n_far:n_far + N_REL - 2][:, ::-1]
    return jnp.concatenate([jnp.sum(dw[:, n_far + N_REL - 2:], axis=1, keepdims=True), mid,
                            jnp.sum(dw[:, :n_far], axis=1, keepdims=True)], axis=1)


def _step(x, tgt, norm_mix, b_gate, norm_ffn, norm_final, rel_bias_shard, shard):
    batch, seq, _ = x.shape
    t = batch * seq
    n_rb = rel_bias_shard.shape[-1]
    x2, tgt2 = x.reshape(t, D_MODEL), tgt.reshape(t, D_MODEL)
    g3 = norm_final.reshape(1, D_MODEL)
    cs, sn = _rotary_tables(seq)
    lg = np.log(1.0 - 2.0 ** (-5.0 - np.arange(RET_HEADS, dtype=np.float32))).astype(np.float32)
    lg_arr = jnp.asarray(np.broadcast_to(lg[:, None, None], (RET_HEADS, 1, LANES)))

    def gather(*names):
        return _ChipGather([shard[nm] for nm in names])

    def scatter(*grads):
        return _Exchange(grads, scatter=True)

    rb_pad = jnp.pad(rel_bias_shard, ((0, 0), (0, LANES - n_rb)))
    (xn,), (w_in_half, rb_full) = _rms_fwd(x2, norm_mix,
                                           _ChipGather([shard["w_in_t"], rb_pad], parts=[(0, 2), (0, 1)]))
    rb_full = rb_full.reshape(N_DEV, ATT_HEADS, LANES)[:, :, :n_rb]
    bias, (w_in_t,) = _att_bias_tiles(
        _bias_rows(jnp.transpose(rb_full, (1, 0, 2)).reshape(ATT_HEADS, N_DEV * n_rb)),
        _ChipGather([shard["w_in_t"]], parts=[(1, 2)], into=[w_in_half]))
    proj, (w_ret, w_att_t, w_out, w_gate_t) = _mm(
        xn, w_in_t, tb=True, out_dtype=BF16, tm=1024, tn=1664, tk=1024, name="proj",
        exchange=gather("w_ret", "w_att_t", "w_out", "w_gate_t"))
    (gro, o_ret, qr, kr), _ = _ret_fwd(proj, cs, sn, lg_arr, batch, seq, None)
    (ao,), (w_up_t, w_down) = _att_fwd(proj, bias, batch, seq, gather("w_up_t", "w_down"))
    z, y_ret, y_att, h1, hn = _mix_out_fwd(gro, ao, proj, b_gate, w_ret, w_att_t, x2, w_out, norm_ffn)
    g_act, u_act, a_act = _ffn_up(hn, w_gate_t, w_up_t)
    dh2, dh2b, loss, dg3 = _ffn_down_loss(a_act, h1, tgt2, w_down, g3)

    wg = dict(out_dtype=BF16, tn=1024, ta=True)
    slots = {}
    dw_down = _mm(a_act, dh2b, tm=1408, tk=1024, name="dw_down", **wg)
    (d_gact, d_uact), _ = _ffn_bwd_act(dh2b, w_down, g_act, u_act, None)
    dw_gate = _mm(d_gact, hn, tm=1408, tk=1024, name="dw_gate", **wg)
    dw_up = _mm(d_uact, hn, tm=1408, tk=1024, name="dw_up", **wg)
    (dh1, dh1b, dg2), (slots["w_down"],) = _ffn_bwd_in(d_gact, d_uact, w_gate_t, w_up_t, h1, dh2, norm_ffn,
                                                     scatter(dw_down))
    dw_out = _mm(z, dh1b, tm=1024, tk=2048, name="dw_out", **wg)
    dyr, dya, dglr, dgla, dgro, dao, db = _mix_bwd(dh1b, w_out, proj, b_gate, y_ret, y_att, w_ret, w_att_t)
    dw_ret = _mm(gro, dyr, tm=1024, tk=2048, name="dw_ret", **wg)
    dw_att = _mm(dya, ao, tm=1024, tk=2048, name="dw_att", **wg)
    (drq, drk, drv, drg), _ = _ret_bwd(dgro, proj, o_ret, qr, kr, cs, sn, lg_arr, batch, seq, None)
    (daq, dak, dav, dw), (slots["w_gate_t"], slots["w_out"], slots["w_ret"], slots["w_att_t"]) = _att_bwd(
        proj, bias, dao, batch, seq, scatter(dw_gate, dw_out, dw_ret, dw_att))
    dproj = [drq, drk, drv, drg, daq, dak, dav, dglr, dgla]
    dw_in, (slots["w_up_t"],) = _mm_pieces(dproj, xn, ta=True, out_dtype=BF16, tm=512, tn=1024, tk=1024,
                                           name="dw_in", exchange=scatter(dw_up))
    (dw_in_sibling,) = _alone(_PairSwap([dw_in]), "swap_w_in")
    dw_in_pairs = _pair_add(dw_in, dw_in_sibling, "pair_w_in")
    dxn, (slots["w_in_t"],) = _mm_pieces(dproj, w_in_t, ta=False, out_dtype=F32, tm=1024, tn=1024, tk=512, name="dxn",
                                  exchange=_ChipScatter([dw_in_pairs]))
    dx, dg1 = _rms_in_bwd(x2, dxn, dh1, norm_mix)
    small = _pack_small(dg1, db[:, :D_MODEL], db[:, D_MODEL:], dg2, dg3, dw, loss)
    (small_slots,) = _alone(_ChipGather([small]), "gather_small")
    return dx.reshape(batch, seq, D_MODEL), slots, small_slots.reshape(N_DEV, 16, D_MODEL)


def _row_tile(r, c):
    return max(d for d in range(16, r + 1, 16) if r % d == 0 and (d * c <= 256 * 1024 or d == 16))


def _pair_add(grad, got, name):
    _, r, c = got.shape
    tr = r
    core = lax.axis_index("c").astype(jnp.int32).reshape(1)

    def body(core_ref, g_ref, a_ref, o_ref):
        o_ref[...] = (g_ref[...].astype(F32) + a_ref[...].astype(F32)).astype(o_ref.dtype)

    blk = pl.BlockSpec((None, tr, c), lambda q, i, core_ref: (q, i, 0))
    return pl.pallas_call(
        body, name=name,
        grid_spec=pltpu.PrefetchScalarGridSpec(
            num_scalar_prefetch=1, grid=(4, r // tr),
            in_specs=[pl.BlockSpec((None, None, tr, c), lambda q, i, core_ref: (q, core_ref[0], i, 0)), blk],
            out_specs=blk),
        out_shape=jax.ShapeDtypeStruct(got.shape, got.dtype),
        compiler_params=_params("parallel", "parallel"),
    )(core, grad.reshape(4, 2, r, c), got)


def _sum_slots(slots, name):
    n, r, c = slots.shape
    tr = _row_tile(r, c)

    def body(s_ref, o_ref):
        acc = s_ref[0].astype(F32)
        for s in range(1, n):
            acc = acc + s_ref[s].astype(F32)
        o_ref[...] = acc

    return pl.pallas_call(
        body, name=name, grid=(r // tr,),
        in_specs=[pl.BlockSpec((n, tr, c), lambda i: (0, i, 0))],
        out_specs=pl.BlockSpec((tr, c), lambda i: (i, 0)),
        out_shape=jax.ShapeDtypeStruct((r, c), F32),
        compiler_params=_params("parallel"),
    )(slots)


def _adamw_math(w, g, m, v):
    m = ADAM_B1 * m + (1.0 - ADAM_B1) * g
    v = ADAM_B2 * v + (1.0 - ADAM_B2) * (g * g)
    m_hat = m / (1.0 - ADAM_B1 ** ADAM_STEP)
    v_hat = v / (1.0 - ADAM_B2 ** ADAM_STEP)
    return -ADAM_LR * (m_hat / (jnp.sqrt(v_hat) + ADAM_EPS) + ADAM_WD * w), m, v


def _adamw(w, slots, m, v, name):
    n, r, c = slots.shape
    tr = _row_tile(r, c)

    def body(w_ref, s_ref, m_ref, v_ref, g_ref, d_ref, nm_ref, nv_ref):
        g = s_ref[0].astype(F32)
        for s in range(1, n):
            g = g + s_ref[s].astype(F32)
        g_ref[...] = g
        d_ref[...], nm_ref[...], nv_ref[...] = _adamw_math(w_ref[...], g, m_ref[...], v_ref[...])

    blk = pl.BlockSpec((tr, c), lambda i: (i, 0))
    return pl.pallas_call(
        body, name=name, grid=(r // tr,),
        in_specs=[blk, pl.BlockSpec((n, tr, c), lambda i: (0, i, 0)), blk, blk], out_specs=[blk] * 4,
        out_shape=[jax.ShapeDtypeStruct((r, c), F32)] * 4,
        compiler_params=_params("parallel"),
    )(w, slots, m, v)


def _adamw_small(ws, gs, ms, vs):
    n = len(ws)

    def body(*refs):
        for i in range(n):
            w_ref, g_ref, m_ref, v_ref = (refs[j * n + i] for j in range(4))
            d_ref, nm_ref, nv_ref = (refs[(4 + j) * n + i] for j in range(3))
            d_ref[...], nm_ref[...], nv_ref[...] = _adamw_math(w_ref[...], g_ref[...], m_ref[...], v_ref[...])

    shapes = [jax.ShapeDtypeStruct(w.shape, F32) for w in ws]
    outs = pl.pallas_call(body, name="adamw_small", out_shape=shapes * 3)(*ws, *gs, *ms, *vs)
    return outs[:n], outs[n:2 * n], outs[2 * n:]


def kernel(x, norm_mix, w_in, b_gate, rel_bias, w_ret_out, w_att_out, w_out, norm_ffn, w_ffn_gate, w_ffn_up, w_ffn_down, norm_final, loss_target, m_norm_mix, m_w_in, m_b_gate, m_rel_bias, m_w_ret_out, m_w_att_out, m_w_out, m_norm_ffn, m_w_ffn_gate, m_w_ffn_up, m_w_ffn_down, m_norm_final, v_norm_mix, v_w_in, v_b_gate, v_rel_bias, v_w_ret_out, v_w_att_out, v_w_out, v_norm_ffn, v_w_ffn_gate, v_w_ffn_up, v_w_ffn_down, v_norm_final):
    me = _index(_place())
    n_rb = rel_bias.shape[-1]

    shard = dict(w_in_t=w_in[0].T, w_gate_t=w_ffn_gate[0].T, w_up_t=w_ffn_up[0].T, w_down=w_ffn_down[0],
                 w_ret=w_ret_out[0], w_out=w_out[0], w_att_t=w_att_out[0].T)
    shard = {nm: s.astype(BF16) for nm, s in shard.items()}
    dx, slots, small_slots = _step(x, loss_target, norm_mix, b_gate, norm_ffn, norm_final, rel_bias[0], shard)
    small_sum = _sum_slots(small_slots, "sum_small")
    loss = small_sum[5, 0]

    transposed = dict(w_in="w_in_t", w_ffn_gate="w_gate_t", w_ffn_up="w_up_t", w_att_out="w_att_t")
    plain = dict(w_ffn_down="w_down", w_ret_out="w_ret", w_out="w_out")
    g = dict(
        norm_mix=small_sum[0:1], b_gate=jnp.concatenate([small_sum[1:2], small_sum[2:3]], axis=1),
        norm_ffn=small_sum[3:4], norm_final=small_sum[4:5],
        rel_bias=lax.dynamic_slice_in_dim(_bias_rows_bwd(small_sum[8:16]), me * n_rb, n_rb, axis=1),
    )
    w = dict(norm_mix=norm_mix, w_in=w_in, b_gate=b_gate, rel_bias=rel_bias, w_ret_out=w_ret_out, w_att_out=w_att_out,
             w_out=w_out, norm_ffn=norm_ffn, w_ffn_gate=w_ffn_gate, w_ffn_up=w_ffn_up, w_ffn_down=w_ffn_down,
             norm_final=norm_final)
    m = dict(norm_mix=m_norm_mix, w_in=m_w_in, b_gate=m_b_gate, rel_bias=m_rel_bias, w_ret_out=m_w_ret_out,
             w_att_out=m_w_att_out, w_out=m_w_out, norm_ffn=m_norm_ffn, w_ffn_gate=m_w_ffn_gate, w_ffn_up=m_w_ffn_up,
             w_ffn_down=m_w_ffn_down, norm_final=m_norm_final)
    v = dict(norm_mix=v_norm_mix, w_in=v_w_in, b_gate=v_b_gate, rel_bias=v_rel_bias, w_ret_out=v_w_ret_out,
             w_att_out=v_w_att_out, w_out=v_w_out, norm_ffn=v_norm_ffn, w_ffn_gate=v_w_ffn_gate, w_ffn_up=v_w_ffn_up,
             w_ffn_down=v_w_ffn_down, norm_final=v_norm_final)
    order = ("norm_mix", "w_in", "b_gate", "rel_bias", "w_ret_out", "w_att_out", "w_out", "norm_ffn",
             "w_ffn_gate", "w_ffn_up", "w_ffn_down", "norm_final")
    small_names = ("norm_mix", "b_gate", "rel_bias", "norm_ffn", "norm_final")

    def flat(a):
        return a[0] if a.ndim == 3 else a.reshape(-1, a.shape[-1])

    grad, delta, new_m, new_v = {}, {}, {}, {}
    for nm in order:
        if nm in transposed:
            res = _adamw(w[nm][0].T, slots[transposed[nm]], m[nm][0].T, v[nm][0].T, "adamw_" + nm)
            grad[nm], delta[nm], new_m[nm], new_v[nm] = (a.T[None] for a in res)
        elif nm in plain:
            res = _adamw(flat(w[nm]), slots[plain[nm]], flat(m[nm]), flat(v[nm]), "adamw_" + nm)
            grad[nm], delta[nm], new_m[nm], new_v[nm] = (a.reshape(w[nm].shape) for a in res)
    ds, nms, nvs = _adamw_small([flat(w[nm]) for nm in small_names], [g[nm] for nm in small_names],
                                [flat(m[nm]) for nm in small_names], [flat(v[nm]) for nm in small_names])
    for i, nm in enumerate(small_names):
        grad[nm], delta[nm], new_m[nm], new_v[nm] = (a.reshape(w[nm].shape) for a in (g[nm], ds[i], nms[i], nvs[i]))

    return (loss, dx, *[grad[nm] for nm in order], *[delta[nm] for nm in order],
            *[new_m[nm] for nm in order], *[new_v[nm] for nm in order])
```

```python
import numpy as np
import jax
import jax.numpy as jnp
from jax import lax
from jax.experimental import pallas as pl
from jax.experimental.pallas import tpu as pltpu

F32 = jnp.float32
BF16 = jnp.bfloat16
MESH = pl.DeviceIdType.MESH

D_MODEL = 1024
CHUNK = 64
RET_HEADS = 4
RET_KEY_DIM = 128
RET_VAL_DIM = 256
ATT_HEADS = 8
BAND_CHUNKS = 8
MAX_REL = 256
N_REL = CHUNK + MAX_REL
D_FF = 2816
N_IN = 6656
ROPE_BASE = 10000.0
EPS = 1e-6
NEG_INF = -1e30
C_RQ, C_RK, C_RV, C_RG, C_AQ, C_AK, C_AV, C_GL = 0, 512, 1024, 2048, 3072, 3584, 4096, 4608

ADAM_LR = 0.001
ADAM_B1 = 0.9
ADAM_B2 = 0.999
ADAM_EPS = 1e-08
ADAM_WD = 0.01
ADAM_STEP = 10

N_DEV = 8
LANES = 128
RET_TILE = 256
ATT_Q = 256
ATT_PAD = BAND_CHUNKS * CHUNK
ATT_WIN = ATT_PAD + ATT_Q
ATT_STARTS = ATT_PAD // ATT_Q
ATT_ROWS = 32
BIAS_LEN = 1024
VMEM_LIMIT = 48 * 1024 * 1024


def _params(*sem):
    return pltpu.CompilerParams(dimension_semantics=sem, vmem_limit_bytes=VMEM_LIMIT)


def _dot(a, b):
    return lax.dot_general(a, b, (((1,), (0,)), ((), ())), preferred_element_type=F32)


def _dot_nt(a, b):
    return lax.dot_general(a, b, (((1,), (1,)), ((), ())), preferred_element_type=F32)


def _dot_tn(a, b):
    return lax.dot_general(a, b, (((0,), (0,)), ((), ())), preferred_element_type=F32)


def _sigmoid(x):
    return 1.0 / (1.0 + jnp.exp(-x))


def _rms_bwd(x, g, dy):
    r = lax.rsqrt(jnp.mean(x * x, axis=-1, keepdims=True) + EPS)
    u = dy * g
    dx = r * u - x * (r * r * r) * jnp.mean(u * x, axis=-1, keepdims=True)
    return dx, dy * x * r


def _place():
    return lax.axis_index("x"), lax.axis_index("y"), lax.axis_index("c")


def _peer(k):
    x, y, c = _place()
    return ((1 - x) if k & 4 else x, (1 - y) if k & 2 else y, (1 - c) if k & 1 else c)


def _index(place):
    return 4 * place[0] + 2 * place[1] + place[2]


def _rows(ref, block, nrows):
    align = 16 if ref.dtype == BF16 else 8
    return ref.at[pl.ds(pl.multiple_of(block * nrows, align), nrows)]


class _Exchange:
    def __init__(self, arrays, scatter):
        self.arrays, self.scatter, self.n = list(arrays), scatter, len(arrays)

    def out_shape(self):
        if self.scatter:
            return [jax.ShapeDtypeStruct((N_DEV, a.shape[0] // N_DEV) + a.shape[1:], a.dtype) for a in self.arrays]
        return [jax.ShapeDtypeStruct((N_DEV * a.shape[0],) + a.shape[1:], a.dtype) for a in self.arrays]

    def scratch(self):
        return [pltpu.SemaphoreType.DMA((self.n, N_DEV - 1)), pltpu.SemaphoreType.DMA((self.n, N_DEV - 1)),
                pltpu.SemaphoreType.DMA((self.n,))]

    def _copies(self, ins, outs, sems):
        send_sems, recv_sems, local_sems = sems
        me = _index(_place())

        def src(w, to):
            return _rows(ins[w], to, ins[w].shape[0] // N_DEV) if self.scatter else ins[w]

        def dst(w, origin):
            return outs[w].at[origin] if self.scatter else _rows(outs[w], origin, ins[w].shape[0])

        def remote(w, k, to, origin):
            return pltpu.make_async_remote_copy(src_ref=src(w, to), dst_ref=dst(w, origin),
                                                send_sem=send_sems.at[w, k - 1], recv_sem=recv_sems.at[w, k - 1],
                                                device_id=_peer(k), device_id_type=MESH)

        pairs = [(w, k) for w in range(self.n) for k in range(1, N_DEV)]
        own = lambda: [pltpu.make_async_copy(src(w, me), dst(w, me), local_sems.at[w]) for w in range(self.n)]
        sent = lambda: [remote(w, k, _index(_peer(k)), me) for w, k in pairs]
        arriving = lambda: [remote(w, k, me, _index(_peer(k))) for w, k in pairs]
        return own, sent, arriving

    def start(self, ins, outs, sems):
        own, sent, _ = self._copies(ins, outs, sems)
        for cp in own() + sent():
            cp.start()

    def wait(self, ins, outs, sems):
        own, sent, arriving = self._copies(ins, outs, sems)
        for cp in arriving():
            cp.wait_recv()
        for cp in sent():
            cp.wait_send()
        for cp in own():
            cp.wait()


class _PairSwap:
    def __init__(self, arrays):
        self.arrays, self.n = list(arrays), len(arrays)

    def out_shape(self):
        return [jax.ShapeDtypeStruct((4, a.shape[0] // N_DEV) + a.shape[1:], a.dtype) for a in self.arrays]

    def scratch(self):
        return [pltpu.SemaphoreType.DMA((self.n, 4)), pltpu.SemaphoreType.DMA((self.n, 4))]

    def _copies(self, ins, outs, sems):
        send_sems, recv_sems = sems
        x, y, c = _place()
        return [pltpu.make_async_remote_copy(
            src_ref=_rows(ins[w], 2 * q + 1 - c, ins[w].shape[0] // N_DEV), dst_ref=outs[w].at[q],
            send_sem=send_sems.at[w, q], recv_sem=recv_sems.at[w, q],
            device_id=(x, y, 1 - c), device_id_type=MESH) for w in range(self.n) for q in range(4)]

    def start(self, ins, outs, sems):
        for cp in self._copies(ins, outs, sems):
            cp.start()

    def wait(self, ins, outs, sems):
        for cp in self._copies(ins, outs, sems):
            cp.wait()


class _ChipScatter:
    def __init__(self, arrays):
        self.arrays, self.n = list(arrays), len(arrays)

    def out_shape(self):
        return [jax.ShapeDtypeStruct(a.shape, a.dtype) for a in self.arrays]

    def scratch(self):
        return [pltpu.SemaphoreType.DMA((self.n, 3)), pltpu.SemaphoreType.DMA((self.n, 3)),
                pltpu.SemaphoreType.DMA((self.n,))]

    def _copies(self, ins, outs, sems):
        send_sems, recv_sems, local_sems = sems
        x, y, c = _place()
        mine = 2 * x + y
        sent, arriving = [], []
        for w in range(self.n):
            for k in range(1, 4):
                tx, ty = (1 - x) if k & 2 else x, (1 - y) if k & 1 else y
                other = 2 * tx + ty
                sent.append(lambda w=w, k=k, tx=tx, ty=ty, other=other: pltpu.make_async_remote_copy(
                    src_ref=ins[w].at[other], dst_ref=outs[w].at[mine], send_sem=send_sems.at[w, k - 1],
                    recv_sem=recv_sems.at[w, k - 1], device_id=(tx, ty, c), device_id_type=MESH))
                arriving.append(lambda w=w, k=k, tx=tx, ty=ty, other=other: pltpu.make_async_remote_copy(
                    src_ref=ins[w].at[mine], dst_ref=outs[w].at[other], send_sem=send_sems.at[w, k - 1],
                    recv_sem=recv_sems.at[w, k - 1], device_id=(tx, ty, c), device_id_type=MESH))
        own = [lambda w=w: pltpu.make_async_copy(ins[w].at[mine], outs[w].at[mine], local_sems.at[w])
               for w in range(self.n)]
        return own, sent, arriving

    def start(self, ins, outs, sems):
        own, sent, _ = self._copies(ins, outs, sems)
        for cp in own + sent:
            cp().start()

    def wait(self, ins, outs, sems):
        own, sent, arriving = self._copies(ins, outs, sems)
        for cp in arriving:
            cp().wait_recv()
        for cp in sent:
            cp().wait_send()
        for cp in own:
            cp().wait()


class _ChipGather:
    def __init__(self, arrays, parts=None, into=None):
        self.arrays, self.n, self.into = list(arrays), len(arrays), into
        self.parts = parts or [(0, 1)] * self.n

    def out_shape(self):
        return [jax.ShapeDtypeStruct((N_DEV * a.shape[0],) + a.shape[1:], a.dtype) for a in self.arrays]

    def scratch(self):
        return [pltpu.SemaphoreType.DMA((self.n, N_DEV - 1)), pltpu.SemaphoreType.DMA((self.n, N_DEV - 1)),
                pltpu.SemaphoreType.DMA((self.n,))]

    def _parts(self, ins, outs, sems):
        send_sems, recv_sems, local_sems = sems
        x, y, c = _place()
        me, sibling = (x, y, c), (x, y, 1 - c)
        chips = [(1 - x, y), (x, 1 - y), (1 - x, 1 - y)]

        def rows(w, place, whole):
            (index, count), r = self.parts[w], ins[w].shape[0]
            lo, size = (0, r) if whole else (index * (r // count), r // count)
            align = 16 if ins[w].dtype == BF16 else 8
            return outs[w].at[pl.ds(pl.multiple_of(_index(place) * r + lo, align), size)]

        def mine(w, whole):
            (index, count), r = self.parts[w], ins[w].shape[0]
            return ins[w] if whole or count == 1 else ins[w].at[pl.ds(index * (r // count), r // count)]

        def copy(w, k, block, to, own=False):
            whole = k == 0
            return pltpu.make_async_remote_copy(src_ref=mine(w, whole) if own else rows(w, block, whole),
                                                dst_ref=rows(w, block, whole),
                                                send_sem=send_sems.at[w, k], recv_sem=recv_sems.at[w, k],
                                                device_id=to, device_id_type=MESH)

        def local(w):
            return pltpu.make_async_copy(ins[w], rows(w, me, True), local_sems.at[w])

        return me, sibling, chips, c, copy, local, [index == 0 for index, _ in self.parts]

    def start(self, ins, outs, sems):
        me, sibling, chips, c, copy, local, places_own = self._parts(ins, outs, sems)
        for w in range(self.n):
            if places_own[w]:
                local(w).start()
                copy(w, 0, me, sibling, own=True).start()
            for j, chip in enumerate(chips):
                copy(w, 1 + j, me, (*chip, c), own=True).start()

    def wait(self, ins, outs, sems):
        me, sibling, chips, c, copy, local, places_own = self._parts(ins, outs, sems)
        for w in range(self.n):
            for j, chip in enumerate(chips):
                copy(w, 1 + j, (*chip, c), me).wait_recv()
                copy(w, 4 + j, (*chip, c), sibling).start()
        for w in range(self.n):
            if places_own[w]:
                copy(w, 0, sibling, me).wait_recv()
                copy(w, 0, me, sibling, own=True).wait_send()
                local(w).wait()
            for j, chip in enumerate(chips):
                copy(w, 4 + j, (*chip, 1 - c), me).wait_recv()
                copy(w, 1 + j, me, (*chip, c), own=True).wait_send()
                copy(w, 4 + j, (*chip, c), sibling).wait_send()


def _call(body, *, name, grid, in_specs, out_specs, out_shape, scratch=(), semantics, args, exchange=None):
    if exchange is None:
        return pl.pallas_call(body, name=name, grid=grid, in_specs=in_specs, out_specs=out_specs, out_shape=out_shape,
                              scratch_shapes=list(scratch), compiler_params=_params(*semantics))(*args), None
    n_in, n_out, n_scr, nx = len(in_specs), len(out_specs), len(scratch), exchange.n
    into = list(getattr(exchange, "into", None) or [])

    def full_body(*refs):
        ins, refs = refs[:n_in], refs[n_in:]
        x_in, refs = refs[:nx], refs[nx + len(into):]
        outs, refs = refs[:n_out], refs[n_out:]
        x_out, refs = refs[:nx], refs[nx:]
        scr, sems = refs[:n_scr], refs[n_scr:]
        first, last = True, True
        for axis, size in enumerate(grid):
            first = jnp.logical_and(first, pl.program_id(axis) == 0)
            last = jnp.logical_and(last, pl.program_id(axis) == size - 1)
        if grid:
            pl.when(first)(lambda: exchange.start(x_in, x_out, sems))
        else:
            exchange.start(x_in, x_out, sems)
        body(*ins, *outs, *scr)
        if grid:
            pl.when(last)(lambda: exchange.wait(x_in, x_out, sems))
        else:
            exchange.wait(x_in, x_out, sems)

    hbm = pl.BlockSpec(memory_space=pltpu.HBM)
    res = pl.pallas_call(
        full_body, name=name, grid=grid,
        in_specs=list(in_specs) + [hbm] * (nx + len(into)), out_specs=list(out_specs) + [hbm] * nx,
        out_shape=list(out_shape) + exchange.out_shape(),
        scratch_shapes=list(scratch) + exchange.scratch(),
        input_output_aliases={n_in + nx + w: n_out + w for w in range(len(into))},
        compiler_params=_params(*(["arbitrary"] * len(grid))),
    )(*args, *exchange.arrays, *into)
    return res[:n_out], res[n_out:]


def _alone(exchange, name):
    return _call(lambda: None, name=name, grid=(), in_specs=[], out_specs=[], out_shape=[], semantics=(),
                 args=(), exchange=exchange)[1]


def _mm(a, b, *, ta=False, tb=False, out_dtype, tm, tn, tk, name, exchange=None):
    m, k = (a.shape[1], a.shape[0]) if ta else a.shape
    n = b.shape[0] if tb else b.shape[1]
    assert k == (b.shape[1] if tb else b.shape[0])
    tm, tn, tk = min(tm, m), min(tn, n), min(tk, k)
    assert m % tm == 0 and n % tn == 0 and k % tk == 0, (name, m, n, k)
    nk = k // tk
    dims = (((0 if ta else 1,), (1 if tb else 0,)), ((), ()))

    def body(a_ref, b_ref, o_ref, *acc):
        prod = lax.dot_general(a_ref[...].astype(BF16), b_ref[...].astype(BF16), dims, preferred_element_type=F32)
        if nk == 1:
            o_ref[...] = prod.astype(o_ref.dtype)
            return
        acc_ref, kk = acc[0], pl.program_id(2)

        @pl.when(kk == 0)
        def _():
            acc_ref[...] = prod

        @pl.when((kk > 0) & (kk < nk - 1))
        def _():
            acc_ref[...] += prod

        @pl.when(kk == nk - 1)
        def _():
            o_ref[...] = (acc_ref[...] + prod).astype(o_ref.dtype)

    a_spec = (pl.BlockSpec((tk, tm), lambda i, j, kk: (kk, i)) if ta
              else pl.BlockSpec((tm, tk), lambda i, j, kk: (i, kk)))
    b_spec = (pl.BlockSpec((tn, tk), lambda i, j, kk: (j, kk)) if tb
              else pl.BlockSpec((tk, tn), lambda i, j, kk: (kk, j)))
    (out,), moved = _call(
        body, name=name, grid=(m // tm, n // tn, nk),
        in_specs=[a_spec, b_spec],
        out_specs=[pl.BlockSpec((tm, tn), lambda i, j, kk: (i, j))],
        out_shape=[pltpu.HBM((m, n), out_dtype)],
        scratch=[pltpu.VMEM((tm, tn), F32)] if nk > 1 else [],
        semantics=("parallel", "parallel", "arbitrary"), args=(a, b), exchange=exchange)
    return out if exchange is None else (out, moved)


def _mm_pieces(pieces, b, *, ta, out_dtype, tm, tn, tk, name, exchange=None):
    rows, n = pieces[0].shape[0], b.shape[1]
    step = tm if ta else tk
    assert all(p.shape[0] == rows and p.shape[1] % step == 0 for p in pieces), name
    edges = [int(e) for e in np.cumsum([0] + [p.shape[1] // step for p in pieces])]
    total = edges[-1] * step
    m, k = (total, rows) if ta else (rows, total)
    assert b.shape[0] == k and m % tm == 0 and n % tn == 0 and k % tk == 0, name
    nk, npieces = k // tk, len(pieces)
    dims = (((0 if ta else 1,), (0,)), ((), ()))

    def body(*refs):
        a_refs, (b_ref, o_ref, acc_ref) = refs[:npieces], refs[npieces:]
        kk = pl.program_id(2)
        pos = pl.program_id(0) if ta else kk

        @pl.when(kk == 0)
        def _():
            acc_ref[...] = jnp.zeros_like(acc_ref)

        for p, a_ref in enumerate(a_refs):
            @pl.when((pos >= edges[p]) & (pos < edges[p + 1]))
            def _(a_ref=a_ref):
                acc_ref[...] += lax.dot_general(a_ref[...], b_ref[...], dims, preferred_element_type=F32)

        @pl.when(kk == nk - 1)
        def _():
            o_ref[...] = acc_ref[...].astype(o_ref.dtype)

    def a_spec(p):
        lo, last = edges[p], edges[p + 1] - edges[p] - 1
        if ta:
            def index(i, j, kk):
                inside = (i >= lo) & (i <= lo + last)
                return jnp.where(inside, kk, 0), jnp.clip(i - lo, 0, last)
            return pl.BlockSpec((tk, tm), index)
        return pl.BlockSpec((tm, tk), lambda i, j, kk: (i, jnp.clip(kk - lo, 0, last)))

    (out,), moved = _call(
        body, name=name, grid=(m // tm, n // tn, nk),
        in_specs=[a_spec(p) for p in range(npieces)] + [pl.BlockSpec((tk, tn), lambda i, j, kk: (kk, j))],
        out_specs=[pl.BlockSpec((tm, tn), lambda i, j, kk: (i, j))],
        out_shape=[pltpu.HBM((m, n), out_dtype)],
        scratch=[pltpu.VMEM((tm, tn), F32)],
        semantics=("parallel", "parallel", "arbitrary"), args=(*pieces, b), exchange=exchange)
    return out if exchange is None else (out, moved)


def _rms_fwd(x2, g, exchange):
    t = x2.shape[0]
    tm = min(512, t)

    def body(x_ref, g_ref, o_ref):
        x = x_ref[...]
        r = lax.rsqrt(jnp.mean(x * x, axis=-1, keepdims=True) + EPS)
        o_ref[...] = (x * r * g_ref[...]).astype(o_ref.dtype)

    return _call(
        body, name="rms_in_fwd", grid=(t // tm,),
        in_specs=[pl.BlockSpec((tm, D_MODEL), lambda i: (i, 0)), pl.BlockSpec((1, D_MODEL), lambda i: (0, 0))],
        out_specs=[pl.BlockSpec((tm, D_MODEL), lambda i: (i, 0))],
        out_shape=[jax.ShapeDtypeStruct((t, D_MODEL), BF16)],
        semantics=("parallel",), args=(x2, g), exchange=exchange)


def _decay(lg):
    row = lax.broadcasted_iota(jnp.int32, (RET_TILE, RET_TILE), 0)
    col = lax.broadcasted_iota(jnp.int32, (RET_TILE, RET_TILE), 1)
    within = jnp.exp(lg * jnp.abs(row - col).astype(F32))
    inside = jnp.where((col >> 6) <= (row >> 6), within, 0.0)
    pos = lax.broadcasted_iota(jnp.int32, (RET_TILE, 1), 0).astype(F32)
    q_dec = jnp.exp(lg * (pos + 1.0))
    k_dec = jnp.exp(lg * (RET_TILE - 1.0 - pos))
    tile_dec = jnp.exp(lg * float(RET_TILE))
    return inside, q_dec, k_dec, tile_dec


def _scaled(a_bf16, dec):
    return (a_bf16.astype(F32) * dec).astype(BF16)


def _ret_fwd(proj, cs, sn, lg_arr, batch, seq, exchange):
    t = batch * seq
    nt = seq // RET_TILE

    def body(q_ref, k_ref, v_ref, rg_ref, cs_ref, sn_ref, lg_ref, gro_ref, o_ref, qr_ref, kr_ref):
        lg = lg_ref[:, 0:1]
        cs_t, sn_t = cs_ref[...], sn_ref[...]
        q = q_ref[...].astype(F32)
        k = k_ref[...].astype(F32)
        qr_ref[...] = (q * cs_t + pltpu.roll(q, 64, 1) * sn_t).astype(BF16)
        kr_ref[...] = ((k * cs_t + pltpu.roll(k, 64, 1) * sn_t) * (RET_KEY_DIM ** -0.5)).astype(BF16)
        inside, q_dec, k_dec, tile_dec = _decay(lg)
        state = jnp.zeros((RET_KEY_DIM, RET_VAL_DIM), F32)
        for i in range(nt):
            rows = slice(i * RET_TILE, (i + 1) * RET_TILE)
            qi, ki, vi = qr_ref[rows, :], kr_ref[rows, :], v_ref[rows, :]
            acc = _dot((_dot_nt(qi, ki) * inside).astype(BF16), vi)
            if i > 0:
                acc = acc + _dot(_scaled(qi, q_dec), state.astype(BF16))
            if i < nt - 1:
                state = state * tile_dec + _dot_tn(_scaled(ki, k_dec), vi)
            o_ref[rows, :] = acc
            xc = acc - jnp.mean(acc, axis=-1, keepdims=True)
            nrm = xc * lax.rsqrt(jnp.mean(xc * xc, axis=-1, keepdims=True) + EPS)
            rg = rg_ref[rows, :].astype(F32)
            gro_ref[rows, :] = (rg * _sigmoid(rg) * nrm).astype(BF16)

    def col(base, width):
        return lambda b, h: (b, base // width + h)

    return _call(
        body, name="ret_fwd", grid=(batch, RET_HEADS),
        in_specs=[pl.BlockSpec((seq, RET_KEY_DIM), col(C_RQ, RET_KEY_DIM)),
                  pl.BlockSpec((seq, RET_KEY_DIM), col(C_RK, RET_KEY_DIM)),
                  pl.BlockSpec((seq, RET_VAL_DIM), col(C_RV, RET_VAL_DIM)),
                  pl.BlockSpec((seq, RET_VAL_DIM), col(C_RG, RET_VAL_DIM)),
                  pl.BlockSpec((seq, RET_KEY_DIM), lambda b, h: (0, 0)),
                  pl.BlockSpec((seq, RET_KEY_DIM), lambda b, h: (0, 0)),
                  pl.BlockSpec((None, 1, LANES), lambda b, h: (h, 0, 0))],
        out_specs=[pl.BlockSpec((seq, RET_VAL_DIM), lambda b, h: (b, h)),
                   pl.BlockSpec((seq, RET_VAL_DIM), lambda b, h: (b, h)),
                   pl.BlockSpec((seq, RET_KEY_DIM), lambda b, h: (b, h)),
                   pl.BlockSpec((seq, RET_KEY_DIM), lambda b, h: (b, h))],
        out_shape=[jax.ShapeDtypeStruct((t, RET_HEADS * RET_VAL_DIM), BF16),
                   jax.ShapeDtypeStruct((t, RET_HEADS * RET_VAL_DIM), F32),
                   jax.ShapeDtypeStruct((t, RET_HEADS * RET_KEY_DIM), BF16),
                   jax.ShapeDtypeStruct((t, RET_HEADS * RET_KEY_DIM), BF16)],
        semantics=("parallel", "parallel"), args=(proj, proj, proj, proj, cs, sn, lg_arr), exchange=exchange)


def _att_bias(w_ref, bias_ref):
    n_i = lax.broadcasted_iota(jnp.int32, (ATT_Q, BIAS_LEN), 0)
    qc = lax.broadcasted_iota(jnp.int32, (ATT_Q, ATT_WIN), 0) >> 6
    kc = lax.broadcasted_iota(jnp.int32, (ATT_Q, ATT_WIN), 1) >> 6
    dc = qc + BAND_CHUNKS - kc
    band = (dc >= 0) & (dc <= BAND_CHUNKS)
    key = lax.broadcasted_iota(jnp.int32, (ATT_Q, ATT_WIN), 1)
    for e in range(2):
        xw = jnp.broadcast_to(w_ref[e:e + 1, :], (ATT_Q, BIAS_LEN))
        for bit in range(8):
            xw = jnp.where(((n_i >> bit) & 1) == 1, pltpu.roll(xw, 1 << bit, 1), xw)
        bias = jnp.where(band, xw[:, BIAS_LEN - ATT_WIN:], NEG_INF)
        for first in range(ATT_STARTS):
            bias_ref[first, e] = jnp.where(key + (first * ATT_Q - ATT_PAD) >= 0, bias, NEG_INF)
        bias_ref[ATT_STARTS, e] = bias


def _att_specs(batch, seq):
    ni = seq // ATT_Q
    q_spec = pl.BlockSpec((ATT_Q, LANES), lambda hp, b, i: (b * ni + i, C_AQ // LANES + hp))
    k_spec = pl.BlockSpec((seq, LANES), lambda hp, b, i: (b, C_AK // LANES + hp))
    v_spec = pl.BlockSpec((seq, LANES), lambda hp, b, i: (b, C_AV // LANES + hp))
    w_spec = pl.BlockSpec((None, 2, BIAS_LEN), lambda hp, b, i: (hp, 0, 0))
    b_spec = pl.BlockSpec((None, ATT_STARTS + 1, 2, ATT_Q, ATT_WIN), lambda hp, b, i: (hp, 0, 0, 0, 0))
    pad = pltpu.VMEM((seq + ATT_PAD, LANES), BF16)
    return ni, q_spec, k_spec, v_spec, w_spec, b_spec, pad


def _att_bias_tiles(wvec, exchange):
    (tiles,), moved = _call(
        lambda w_ref, o_ref: _att_bias(w_ref, o_ref), name="att_bias", grid=(ATT_HEADS // 2,),
        in_specs=[pl.BlockSpec((None, 2, BIAS_LEN), lambda hp: (hp, 0, 0))],
        out_specs=[pl.BlockSpec((None, ATT_STARTS + 1, 2, ATT_Q, ATT_WIN), lambda hp: (hp, 0, 0, 0, 0))],
        out_shape=[jax.ShapeDtypeStruct((ATT_HEADS // 2, ATT_STARTS + 1, 2, ATT_Q, ATT_WIN), F32)],
        semantics=("parallel",), args=(wvec,), exchange=exchange)
    return tiles, moved


def _att_pad(src_ref, pad_ref):
    pad_ref[:ATT_PAD, :] = jnp.zeros((ATT_PAD, LANES), BF16)
    pad_ref[ATT_PAD:, :] = src_ref[...]


def _att_head(q2, sel):
    return jnp.where(sel, q2, jnp.zeros_like(q2)) * 0.125


def _att_softmax_rows(s_ref, bias_ref, rows):
    s = s_ref[rows, :] + bias_ref[rows, :]
    ex = jnp.exp(s - jnp.max(s, axis=-1, keepdims=True))
    return ex, 1.0 / jnp.sum(ex, axis=-1, keepdims=True)


def _att_fwd(proj, bias, batch, seq, exchange):
    ni, q_spec, k_spec, v_spec, _, b_spec, pad = _att_specs(batch, seq)

    def body(q_ref, k_ref, v_ref, bias_ref, o_ref, kp_ref, vp_ref, s_ref, e_ref):
        i = pl.program_id(2)

        @pl.when(i == 0)
        def _():
            _att_pad(k_ref, kp_ref)
            _att_pad(v_ref, vp_ref)

        win = pl.ds(pl.multiple_of(i * ATT_Q, ATT_Q), ATT_WIN)
        k2, v2, q2 = kp_ref[win, :], vp_ref[win, :], q_ref[...]
        lo = lax.broadcasted_iota(jnp.int32, (1, LANES), 1) < 64
        start = jnp.minimum(i, ATT_STARTS)
        out = jnp.zeros((ATT_Q, LANES), F32)
        for e in range(2):
            sel = lo if e == 0 else jnp.logical_not(lo)
            s_ref[e] = _dot_nt(_att_head(q2, sel), k2)
            rsum = []
            for c in range(ATT_Q // ATT_ROWS):
                rows = slice(c * ATT_ROWS, (c + 1) * ATT_ROWS)
                ex, r = _att_softmax_rows(s_ref.at[e], bias_ref.at[start, e], rows)
                e_ref[e, rows, :] = ex.astype(BF16)
                rsum.append(r)
            out = out + _dot(e_ref[e], jnp.where(sel, v2, jnp.zeros_like(v2))) * jnp.concatenate(rsum, axis=0)
        o_ref[...] = out.astype(BF16)

    return _call(
        body, name="att_fwd", grid=(ATT_HEADS // 2, batch, ni),
        in_specs=[q_spec, k_spec, v_spec, b_spec],
        out_specs=[pl.BlockSpec((ATT_Q, LANES), lambda hp, b, i: (b * ni + i, hp))],
        out_shape=[jax.ShapeDtypeStruct((batch * seq, ATT_HEADS * 64), BF16)],
        scratch=[pad, pad, pltpu.VMEM((2, ATT_Q, ATT_WIN), F32), pltpu.VMEM((2, ATT_Q, ATT_WIN), BF16)],
        semantics=("arbitrary", "arbitrary", "arbitrary"), args=(proj, proj, proj, bias), exchange=exchange)


GL_HALF = 512


def _gl_specs(tm):
    return [pl.BlockSpec((tm, GL_HALF), lambda i, c=C_GL // GL_HALF + j: (i, c)) for j in range(4)]


def _gates(gl_refs, b_ref):
    logits = [ref[...].astype(F32) for ref in gl_refs]
    gr = _sigmoid(jnp.concatenate(logits[:2], axis=1) + b_ref[:, :D_MODEL])
    ga = _sigmoid(jnp.concatenate(logits[2:], axis=1) + b_ref[:, D_MODEL:])
    return gr, ga


def _whole(a):
    return pl.BlockSpec(a.shape, lambda i: (0,) * a.ndim)


def _mix_out_fwd(gro, ao, proj, b_gate, w_ret, w_att_t, x2, w_out, g2):
    t = gro.shape[0]
    tm = min(256, t)

    def body(gro_ref, ao_ref, gl0, gl1, gl2, gl3, b_ref, wr_ref, wa_ref, x_ref, wo_ref, g_ref,
             z_ref, yr_ref, ya_ref, h_ref, hn_ref):
        yr = _dot(gro_ref[...], wr_ref[...])
        ya = _dot_nt(ao_ref[...], wa_ref[...])
        yr_ref[...] = yr.astype(BF16)
        ya_ref[...] = ya.astype(BF16)
        gr, ga = _gates((gl0, gl1, gl2, gl3), b_ref)
        z = (gr * yr + ga * ya).astype(BF16)
        z_ref[...] = z
        h = x_ref[...] + _dot(z, wo_ref[...])
        h_ref[...] = h
        r = lax.rsqrt(jnp.mean(h * h, axis=-1, keepdims=True) + EPS)
        hn_ref[...] = (h * r * g_ref[...]).astype(BF16)

    row = pl.BlockSpec((tm, D_MODEL), lambda i: (i, 0))
    return pl.pallas_call(
        body, name="mix_out_fwd", grid=(t // tm,),
        in_specs=[row, pl.BlockSpec((tm, 512), lambda i: (i, 0)), *_gl_specs(tm),
                  _whole(b_gate), _whole(w_ret), _whole(w_att_t), row, _whole(w_out), _whole(g2)],
        out_specs=[row] * 5,
        out_shape=[jax.ShapeDtypeStruct((t, D_MODEL), BF16)] * 3
                  + [jax.ShapeDtypeStruct((t, D_MODEL), F32), jax.ShapeDtypeStruct((t, D_MODEL), BF16)],
        compiler_params=_params("parallel"),
    )(gro, ao, proj, proj, proj, proj, b_gate, w_ret, w_att_t, x2, w_out, g2)


def _ffn_up(hn, wg_t, wu_t):
    t = hn.shape[0]
    tm, tn = min(512, t), D_FF // 2

    def body(h_ref, wg_ref, wu_ref, g_ref, u_ref, a_ref):
        g = _dot_nt(h_ref[...], wg_ref[...])
        u = _dot_nt(h_ref[...], wu_ref[...])
        g_ref[...] = g.astype(BF16)
        u_ref[...] = u.astype(BF16)
        a_ref[...] = (g * _sigmoid(g) * u).astype(BF16)

    w_spec = pl.BlockSpec((tn, D_MODEL), lambda j, i: (j, 0))
    out = pl.BlockSpec((tm, tn), lambda j, i: (i, j))
    return pl.pallas_call(
        body, name="ffn_up", grid=(D_FF // tn, t // tm),
        in_specs=[pl.BlockSpec((tm, D_MODEL), lambda j, i: (i, 0)), w_spec, w_spec],
        out_specs=[out, out, out],
        out_shape=[jax.ShapeDtypeStruct((t, D_FF), BF16)] * 3,
        compiler_params=_params("parallel", "parallel"),
    )(hn, wg_t, wu_t)


def _ffn_down_loss(a, h1, tgt, w_down, g3):
    t = a.shape[0]
    tm = min(512, t)

    def body(a_ref, h_ref, t_ref, w_ref, g_ref, dh_ref, dhb_ref, loss_ref, dg_ref):
        @pl.when(pl.program_id(0) == 0)
        def _():
            loss_ref[...] = jnp.zeros_like(loss_ref)
            dg_ref[...] = jnp.zeros_like(dg_ref)

        g = g_ref[...]
        h2 = h_ref[...] + _dot(a_ref[...], w_ref[...])
        r = lax.rsqrt(jnp.mean(h2 * h2, axis=-1, keepdims=True) + EPS)
        err = h2 * r * g - t_ref[...]
        loss_ref[...] += jnp.sum(err * err) * (0.5 / D_MODEL)
        dy = err * (1.0 / D_MODEL)
        dh, dg_rows = _rms_bwd(h2, g, dy)
        dg_ref[...] += jnp.sum(dg_rows, axis=0, keepdims=True)
        dh_ref[...] = dh
        dhb_ref[...] = dh.astype(BF16)

    row = pl.BlockSpec((tm, D_MODEL), lambda i: (i, 0))
    vec = pl.BlockSpec((1, D_MODEL), lambda i: (0, 0))
    return pl.pallas_call(
        body, name="ffn_down_loss", grid=(t // tm,),
        in_specs=[pl.BlockSpec((tm, D_FF), lambda i: (i, 0)), row, row,
                  pl.BlockSpec((D_FF, D_MODEL), lambda i: (0, 0)), vec],
        out_specs=[row, row, pl.BlockSpec((1, LANES), lambda i: (0, 0)), vec],
        out_shape=[jax.ShapeDtypeStruct((t, D_MODEL), F32), jax.ShapeDtypeStruct((t, D_MODEL), BF16),
                   jax.ShapeDtypeStruct((1, LANES), F32), jax.ShapeDtypeStruct((1, D_MODEL), F32)],
        compiler_params=_params("arbitrary"),
    )(a, h1, tgt, w_down, g3)


def _ffn_bwd_act(dh2b, w_down, g_act, u_act, exchange):
    t = dh2b.shape[0]
    tm, tn = min(512, t), D_FF // 2

    def body(d_ref, w_ref, g_ref, u_ref, dg_ref, du_ref):
        da = _dot_nt(d_ref[...], w_ref[...])
        g = g_ref[...].astype(F32)
        u = u_ref[...].astype(F32)
        sg = _sigmoid(g)
        dg_ref[...] = (da * u * sg * (1.0 + g * (1.0 - sg))).astype(BF16)
        du_ref[...] = (da * g * sg).astype(BF16)

    blk = pl.BlockSpec((tm, tn), lambda j, i: (i, j))
    return _call(
        body, name="ffn_bwd_act", grid=(D_FF // tn, t // tm),
        in_specs=[pl.BlockSpec((tm, D_MODEL), lambda j, i: (i, 0)),
                  pl.BlockSpec((tn, D_MODEL), lambda j, i: (j, 0)), blk, blk],
        out_specs=[blk, blk],
        out_shape=[jax.ShapeDtypeStruct((t, D_FF), BF16)] * 2,
        semantics=("parallel", "parallel"), args=(dh2b, w_down, g_act, u_act), exchange=exchange)


def _ffn_bwd_in(dg, du, wg_t, wu_t, h1, dh2, g2, exchange):
    t = dg.shape[0]
    tm, tk = min(512, t), D_FF // 2
    nk = D_FF // tk

    def body(dg_ref, du_ref, wg_ref, wu_ref, h_ref, d2_ref, g_ref, dh_ref, dhb_ref, gn_ref, acc_ref):
        i, kk = pl.program_id(0), pl.program_id(1)

        @pl.when((i == 0) & (kk == 0))
        def _():
            gn_ref[...] = jnp.zeros_like(gn_ref)

        @pl.when(kk == 0)
        def _():
            acc_ref[...] = jnp.zeros_like(acc_ref)

        acc_ref[...] += _dot(dg_ref[...], wg_ref[...]) + _dot(du_ref[...], wu_ref[...])

        @pl.when(kk == nk - 1)
        def _():
            dx, dg_rows = _rms_bwd(h_ref[...], g_ref[...], acc_ref[...])
            dh = d2_ref[...] + dx
            dh_ref[...] = dh
            dhb_ref[...] = dh.astype(BF16)
            gn_ref[...] += jnp.sum(dg_rows, axis=0, keepdims=True)

    act = pl.BlockSpec((tm, tk), lambda i, kk: (i, kk))
    wsp = pl.BlockSpec((tk, D_MODEL), lambda i, kk: (kk, 0))
    row = pl.BlockSpec((tm, D_MODEL), lambda i, kk: (i, 0))
    vec = pl.BlockSpec((1, D_MODEL), lambda i, kk: (0, 0))
    return _call(
        body, name="ffn_bwd_in", grid=(t // tm, nk),
        in_specs=[act, act, wsp, wsp, row, row, vec],
        out_specs=[row, row, vec],
        out_shape=[jax.ShapeDtypeStruct((t, D_MODEL), F32), jax.ShapeDtypeStruct((t, D_MODEL), BF16),
                   jax.ShapeDtypeStruct((1, D_MODEL), F32)],
        scratch=[pltpu.VMEM((tm, D_MODEL), F32)],
        semantics=("arbitrary", "arbitrary"), args=(dg, du, wg_t, wu_t, h1, dh2, g2), exchange=exchange)


def _mix_bwd(dh1b, w_out, proj, b_gate, y_ret, y_att, w_ret, w_att_t):
    t = dh1b.shape[0]
    tm = min(256, t)

    def body(d_ref, wo_ref, gl0, gl1, gl2, gl3, b_ref, yr_ref, ya_ref, wr_ref, wa_ref,
             dyr_ref, dya_ref, dglr_ref, dgla_ref, dgro_ref, dao_ref, db_ref):
        @pl.when(pl.program_id(0) == 0)
        def _():
            db_ref[...] = jnp.zeros_like(db_ref)

        dz = _dot_nt(d_ref[...], wo_ref[...])
        gr, ga = _gates((gl0, gl1, gl2, gl3), b_ref)
        dyr = (dz * gr).astype(BF16)
        dya = (dz * ga).astype(BF16)
        dyr_ref[...] = dyr
        dya_ref[...] = dya
        dglr = dz * yr_ref[...].astype(F32) * gr * (1.0 - gr)
        dgla = dz * ya_ref[...].astype(F32) * ga * (1.0 - ga)
        dglr_ref[...] = dglr.astype(BF16)
        dgla_ref[...] = dgla.astype(BF16)
        db_ref[:, :D_MODEL] += jnp.sum(dglr, axis=0, keepdims=True)
        db_ref[:, D_MODEL:] += jnp.sum(dgla, axis=0, keepdims=True)
        dgro_ref[...] = _dot_nt(dyr, wr_ref[...]).astype(BF16)
        dao_ref[...] = _dot(dya, wa_ref[...]).astype(BF16)

    row = pl.BlockSpec((tm, D_MODEL), lambda i: (i, 0))
    half = pl.BlockSpec((tm, 512), lambda i: (i, 0))
    return pl.pallas_call(
        body, name="mix_bwd", grid=(t // tm,),
        in_specs=[row, _whole(w_out), *_gl_specs(tm), _whole(b_gate), row, row, _whole(w_ret), _whole(w_att_t)],
        out_specs=[row, row, row, row, row, half, _whole(b_gate)],
        out_shape=[jax.ShapeDtypeStruct((t, D_MODEL), BF16)] * 5
                  + [jax.ShapeDtypeStruct((t, 512), BF16), jax.ShapeDtypeStruct(b_gate.shape, F32)],
        compiler_params=_params("arbitrary"),
    )(dh1b, w_out, proj, proj, proj, proj, b_gate, y_ret, y_att, w_ret, w_att_t)


def _ret_bwd(dgro, proj, o_ret, qr, kr, cs, sn, lg_arr, batch, seq, exchange):
    t = batch * seq
    nt = seq // RET_TILE

    def body(dgro_ref, rg_ref, o_ref, qr_ref, kr_ref, v_ref, cs_ref, sn_ref, lg_ref,
             dq_ref, dk_ref, dv_ref, drg_ref, do_ref, st_ref):
        lg = lg_ref[:, 0:1]
        inside, q_dec, k_dec, tile_dec = _decay(lg)

        state = jnp.zeros((RET_KEY_DIM, RET_VAL_DIM), F32)
        for i in range(nt - 1):
            rows = slice(i * RET_TILE, (i + 1) * RET_TILE)
            state = state * tile_dec + _dot_tn(_scaled(kr_ref[rows, :], k_dec), v_ref[rows, :])
            st_ref[i + 1] = state.astype(BF16)

        for i in range(nt):
            rows = slice(i * RET_TILE, (i + 1) * RET_TILE)
            o = o_ref[rows, :]
            xc = o - jnp.mean(o, axis=-1, keepdims=True)
            rs = lax.rsqrt(jnp.mean(xc * xc, axis=-1, keepdims=True) + EPS)
            nrm = xc * rs
            rg = rg_ref[rows, :].astype(F32)
            sg = _sigmoid(rg)
            dg = dgro_ref[rows, :].astype(F32)
            drg_ref[rows, :] = (dg * nrm * sg * (1.0 + rg * (1.0 - sg))).astype(BF16)
            dn = dg * rg * sg
            do = rs * (dn - jnp.mean(dn, axis=-1, keepdims=True)
                       - nrm * jnp.mean(dn * nrm, axis=-1, keepdims=True))
            do_ref[rows, :] = do.astype(BF16)

        dstate = jnp.zeros((RET_KEY_DIM, RET_VAL_DIM), F32)
        for i in reversed(range(nt)):
            rows = slice(i * RET_TILE, (i + 1) * RET_TILE)
            qi, ki, vi, doi = qr_ref[rows, :], kr_ref[rows, :], v_ref[rows, :], do_ref[rows, :]
            p = (_dot_nt(qi, ki) * inside).astype(BF16)
            dp = (_dot_nt(doi, vi) * inside).astype(BF16)
            dq = _dot(dp, ki)
            dk = _dot_tn(dp, qi)
            dv = _dot_tn(p, doi)
            if i > 0:
                dq = dq + _dot_nt(doi, st_ref[i]) * q_dec
            if i < nt - 1:
                dsb = dstate.astype(BF16)
                dk = dk + _dot_nt(vi, dsb) * k_dec
                dv = dv + _dot(_scaled(ki, k_dec), dsb)
            if i > 0:
                dstate = dstate * tile_dec + _dot_tn(_scaled(qi, q_dec), doi)
            dq_ref[rows, :] = (dq * cs_ref[rows, :] - pltpu.roll(dq, 64, 1) * sn_ref[rows, :]).astype(BF16)
            dk = (dk * cs_ref[rows, :] - pltpu.roll(dk, 64, 1) * sn_ref[rows, :]) * (RET_KEY_DIM ** -0.5)
            dk_ref[rows, :] = dk.astype(BF16)
            dv_ref[rows, :] = dv.astype(BF16)

    key = pl.BlockSpec((seq, RET_KEY_DIM), lambda b, h: (b, h))
    val = pl.BlockSpec((seq, RET_VAL_DIM), lambda b, h: (b, h))
    tab = pl.BlockSpec((seq, RET_KEY_DIM), lambda b, h: (0, 0))
    return _call(
        body, name="ret_bwd", grid=(batch, RET_HEADS),
        in_specs=[val, pl.BlockSpec((seq, RET_VAL_DIM), lambda b, h: (b, C_RG // RET_VAL_DIM + h)), val, key, key,
                  pl.BlockSpec((seq, RET_VAL_DIM), lambda b, h: (b, C_RV // RET_VAL_DIM + h)), tab, tab,
                  pl.BlockSpec((None, 1, LANES), lambda b, h: (h, 0, 0))],
        out_specs=[key, key, val, val],
        out_shape=[jax.ShapeDtypeStruct((t, RET_HEADS * RET_KEY_DIM), BF16)] * 2
                  + [jax.ShapeDtypeStruct((t, RET_HEADS * RET_VAL_DIM), BF16)] * 2,
        scratch=[pltpu.VMEM((seq, RET_VAL_DIM), BF16), pltpu.VMEM((nt, RET_KEY_DIM, RET_VAL_DIM), BF16)],
        semantics=("parallel", "parallel"), args=(dgro, proj, o_ret, qr, kr, proj, cs, sn, lg_arr),
        exchange=exchange)


def _att_bwd(proj, bias, dao, batch, seq, exchange):
    ni, q_spec, k_spec, v_spec, w_spec, b_spec, pad = _att_specs(batch, seq)
    t = batch * seq

    def body(q_ref, k_ref, v_ref, bias_ref, do_ref, dq_ref, dk_ref, dv_ref, dw_ref,
             dbias_ref, dk_acc, dv_acc, kp_ref, vp_ref, s_ref, dp_ref, e_ref, ds_ref):
        b, i = pl.program_id(1), pl.program_id(2)

        @pl.when((b == 0) & (i == 0))
        def _():
            dbias_ref[...] = jnp.zeros_like(dbias_ref)

        @pl.when(i == 0)
        def _():
            _att_pad(k_ref, kp_ref)
            _att_pad(v_ref, vp_ref)
            dk_acc[...] = jnp.zeros_like(dk_acc)
            dv_acc[...] = jnp.zeros_like(dv_acc)

        win = pl.ds(pl.multiple_of(i * ATT_Q, ATT_Q), ATT_WIN)
        k2, v2, q2, do2 = kp_ref[win, :], vp_ref[win, :], q_ref[...], do_ref[...]
        lo = lax.broadcasted_iota(jnp.int32, (1, LANES), 1) < 64
        dq = jnp.zeros((ATT_Q, LANES), F32)
        dk = jnp.zeros((LANES, ATT_WIN), F32)
        dv = jnp.zeros((LANES, ATT_WIN), F32)
        start = jnp.minimum(i, ATT_STARTS)
        for e in range(2):
            sel = lo if e == 0 else jnp.logical_not(lo)
            qm = _att_head(q2, sel)
            dom = jnp.where(sel, do2, jnp.zeros_like(do2))
            s_ref[e] = _dot_nt(qm, k2)
            dp_ref[e] = _dot_nt(dom, v2)
            rsum = []
            for c in range(ATT_Q // ATT_ROWS):
                rows = slice(c * ATT_ROWS, (c + 1) * ATT_ROWS)
                ex, r = _att_softmax_rows(s_ref.at[e], bias_ref.at[start, e], rows)
                dp = dp_ref[e, rows, :]
                mean = jnp.sum(dp * ex, axis=-1, keepdims=True) * r
                ds = ex * ((dp - mean) * r)
                dbias_ref[e, rows, :] += ds
                ds_ref[e, rows, :] = ds.astype(BF16)
                e_ref[e, rows, :] = ex.astype(BF16)
                rsum.append(r)
            dq = dq + _dot(ds_ref[e], jnp.where(sel, k2, jnp.zeros_like(k2)))
            dk = dk + _dot_tn(qm, ds_ref[e])
            dv = dv + _dot_tn((dom.astype(F32) * jnp.concatenate(rsum, axis=0)).astype(BF16), e_ref[e])
        dq_ref[...] = (dq * 0.125).astype(BF16)
        dk_acc[:, win] += dk
        dv_acc[:, win] += dv

        @pl.when(i == ni - 1)
        def _():
            dk_ref[...] = dk_acc[:, ATT_PAD:].T.astype(BF16)
            dv_ref[...] = dv_acc[:, ATT_PAD:].T.astype(BF16)

        @pl.when((b == batch - 1) & (i == ni - 1))
        def _():
            n_i = lax.broadcasted_iota(jnp.int32, (ATT_Q, BIAS_LEN), 0)
            for e in range(2):
                xw = jnp.concatenate([jnp.zeros((ATT_Q, BIAS_LEN - ATT_WIN), F32), dbias_ref[e]], axis=1)
                for bit in range(8):
                    xw = jnp.where(((n_i >> bit) & 1) == 1, pltpu.roll(xw, BIAS_LEN - (1 << bit), 1), xw)
                dw_ref[e:e + 1, :] = jnp.sum(xw, axis=0, keepdims=True)

    seq_blk = pl.BlockSpec((seq, LANES), lambda hp, b, i: (b, hp))
    q_out = pl.BlockSpec((ATT_Q, LANES), lambda hp, b, i: (b * ni + i, hp))
    return _call(
        body, name="att_bwd", grid=(ATT_HEADS // 2, batch, ni),
        in_specs=[q_spec, k_spec, v_spec, b_spec, q_out],
        out_specs=[q_out, seq_blk, seq_blk, w_spec],
        out_shape=[jax.ShapeDtypeStruct((t, 512), BF16)] * 3
                  + [jax.ShapeDtypeStruct((ATT_HEADS // 2, 2, BIAS_LEN), F32)],
        scratch=[pltpu.VMEM((2, ATT_Q, ATT_WIN), F32),
                 pltpu.VMEM((LANES, seq + ATT_PAD), F32), pltpu.VMEM((LANES, seq + ATT_PAD), F32), pad, pad,
                 pltpu.VMEM((2, ATT_Q, ATT_WIN), F32), pltpu.VMEM((2, ATT_Q, ATT_WIN), F32),
                 pltpu.VMEM((2, ATT_Q, ATT_WIN), BF16), pltpu.VMEM((2, ATT_Q, ATT_WIN), BF16)],
        semantics=("arbitrary", "arbitrary", "arbitrary"), args=(proj, proj, proj, bias, dao), exchange=exchange)


def _rms_in_bwd(x2, dxn, dh1, g1):
    t = x2.shape[0]
    tm = min(512, t)

    def body(x_ref, d_ref, h_ref, g_ref, dx_ref, dg_ref):
        @pl.when(pl.program_id(0) == 0)
        def _():
            dg_ref[...] = jnp.zeros_like(dg_ref)

        dx, dg_rows = _rms_bwd(x_ref[...], g_ref[...], d_ref[...])
        dx_ref[...] = h_ref[...] + dx
        dg_ref[...] += jnp.sum(dg_rows, axis=0, keepdims=True)

    row = pl.BlockSpec((tm, D_MODEL), lambda i: (i, 0))
    vec = pl.BlockSpec((1, D_MODEL), lambda i: (0, 0))
    return pl.pallas_call(
        body, name="rms_in_bwd", grid=(t // tm,),
        in_specs=[row, row, row, vec], out_specs=[row, vec],
        out_shape=[jax.ShapeDtypeStruct((t, D_MODEL), F32), jax.ShapeDtypeStruct((1, D_MODEL), F32)],
        compiler_params=_params("arbitrary"),
    )(x2, dxn, dh1, g1)


def _pack_small(dg1, dbr, dba, dg2, dg3, dw, loss):
    def body(a_ref, b_ref, c_ref, d_ref, e_ref, w_ref, l_ref, o_ref):
        o_ref[...] = jnp.zeros_like(o_ref)
        for r, ref in enumerate((a_ref, b_ref, c_ref, d_ref, e_ref)):
            o_ref[r:r + 1, :] = ref[...]
        o_ref[5:6, 0:LANES] = l_ref[...]
        for hp in range(ATT_HEADS // 2):
            o_ref[8 + 2 * hp:10 + 2 * hp, :] = w_ref[hp]

    return pl.pallas_call(body, name="pack_small",
                          out_shape=jax.ShapeDtypeStruct((16, D_MODEL), F32))(dg1, dbr, dba, dg2, dg3, dw, loss)


def _rotary_tables(seq):
    freqs = ROPE_BASE ** (-jnp.arange(0, RET_KEY_DIM, 2, dtype=F32) / RET_KEY_DIM)
    ang = jnp.arange(seq, dtype=F32)[:, None] * freqs[None, :]
    cos, sin = jnp.cos(ang), jnp.sin(ang)
    return jnp.concatenate([cos, cos], axis=1), jnp.concatenate([-sin, sin], axis=1)


def _bias_rows(rel_bias):
    n_far = BIAS_LEN - ATT_Q - MAX_REL + 1
    n_near = BIAS_LEN - n_far - (N_REL - 2)
    w = jnp.concatenate([jnp.broadcast_to(rel_bias[:, N_REL - 1:], (ATT_HEADS, n_far)),
                         rel_bias[:, 1:N_REL - 1][:, ::-1],
                         jnp.broadcast_to(rel_bias[:, :1], (ATT_HEADS, n_near))], axis=1)
    return w.reshape(ATT_HEADS // 2, 2, BIAS_LEN)


def _bias_rows_bwd(dw):
    n_far = BIAS_LEN - ATT_Q - MAX_REL + 1
    mid = dw[:, n_far:n_far + N_REL - 2][:, ::-1]
    return jnp.concatenate([jnp.sum(dw[:, n_far + N_REL - 2:], axis=1, keepdims=True), mid,
                            jnp.sum(dw[:, :n_far], axis=1, keepdims=True)], axis=1)


def _step(x, tgt, norm_mix, b_gate, norm_ffn, norm_final, rel_bias_shard, shard):
    batch, seq, _ = x.shape
    t = batch * seq
    n_rb = rel_bias_shard.shape[-1]
    x2, tgt2 = x.reshape(t, D_MODEL), tgt.reshape(t, D_MODEL)
    g3 = norm_final.reshape(1, D_MODEL)
    cs, sn = _rotary_tables(seq)
    lg = np.log(1.0 - 2.0 ** (-5.0 - np.arange(RET_HEADS, dtype=np.float32))).astype(np.float32)
    lg_arr = jnp.asarray(np.broadcast_to(lg[:, None, None], (RET_HEADS, 1, LANES)))

    def gather(*names):
        return _ChipGather([shard[nm] for nm in names])

    def scatter(*grads):
        return _Exchange(grads, scatter=True)

    rb_pad = jnp.pad(rel_bias_shard, ((0, 0), (0, LANES - n_rb)))
    (xn,), (w_in_half, rb_full) = _rms_fwd(x2, norm_mix,
                                           _ChipGather([shard["w_in_t"], rb_pad], parts=[(0, 2), (0, 1)]))
    rb_full = rb_full.reshape(N_DEV, ATT_HEADS, LANES)[:, :, :n_rb]
    bias, (w_in_t,) = _att_bias_tiles(
        _bias_rows(jnp.transpose(rb_full, (1, 0, 2)).reshape(ATT_HEADS, N_DEV * n_rb)),
        _ChipGather([shard["w_in_t"]], parts=[(1, 2)], into=[w_in_half]))
    proj, (w_ret, w_att_t, w_out, w_gate_t) = _mm(
        xn, w_in_t, tb=True, out_dtype=BF16, tm=1024, tn=1664, tk=1024, name="proj",
        exchange=gather("w_ret", "w_att_t", "w_out", "w_gate_t"))
    (gro, o_ret, qr, kr), _ = _ret_fwd(proj, cs, sn, lg_arr, batch, seq, None)
    (ao,), (w_up_t, w_down) = _att_fwd(proj, bias, batch, seq, gather("w_up_t", "w_down"))
    z, y_ret, y_att, h1, hn = _mix_out_fwd(gro, ao, proj, b_gate, w_ret, w_att_t, x2, w_out, norm_ffn)
    g_act, u_act, a_act = _ffn_up(hn, w_gate_t, w_up_t)
    dh2, dh2b, loss, dg3 = _ffn_down_loss(a_act, h1, tgt2, w_down, g3)

    wg = dict(out_dtype=BF16, tn=1024, ta=True)
    slots = {}
    dw_down = _mm(a_act, dh2b, tm=1408, tk=1024, name="dw_down", **wg)
    (d_gact, d_uact), _ = _ffn_bwd_act(dh2b, w_down, g_act, u_act, None)
    dw_gate = _mm(d_gact, hn, tm=1408, tk=1024, name="dw_gate", **wg)
    dw_up = _mm(d_uact, hn, tm=1408, tk=1024, name="dw_up", **wg)
    (dh1, dh1b, dg2), (slots["w_down"],) = _ffn_bwd_in(d_gact, d_uact, w_gate_t, w_up_t, h1, dh2, norm_ffn,
                                                     scatter(dw_down))
    dw_out = _mm(z, dh1b, tm=1024, tk=2048, name="dw_out", **wg)
    dyr, dya, dglr, dgla, dgro, dao, db = _mix_bwd(dh1b, w_out, proj, b_gate, y_ret, y_att, w_ret, w_att_t)
    dw_ret = _mm(gro, dyr, tm=1024, tk=2048, name="dw_ret", **wg)
    dw_att = _mm(dya, ao, tm=1024, tk=2048, name="dw_att", **wg)
    (drq, drk, drv, drg), _ = _ret_bwd(dgro, proj, o_ret, qr, kr, cs, sn, lg_arr, batch, seq, None)
    (daq, dak, dav, dw), (slots["w_gate_t"], slots["w_out"], slots["w_ret"], slots["w_att_t"]) = _att_bwd(
        proj, bias, dao, batch, seq, scatter(dw_gate, dw_out, dw_ret, dw_att))
    dproj = [drq, drk, drv, drg, daq, dak, dav, dglr, dgla]
    dw_in, (slots["w_up_t"],) = _mm_pieces(dproj, xn, ta=True, out_dtype=BF16, tm=512, tn=1024, tk=1024,
                                           name="dw_in", exchange=scatter(dw_up))
    (dw_in_sibling,) = _alone(_PairSwap([dw_in]), "swap_w_in")
    dw_in_pairs = _pair_add(dw_in, dw_in_sibling, "pair_w_in")
    dxn, (slots["w_in_t"],) = _mm_pieces(dproj, w_in_t, ta=False, out_dtype=F32, tm=1024, tn=1024, tk=512, name="dxn",
                                  exchange=_ChipScatter([dw_in_pairs]))
    dx, dg1 = _rms_in_bwd(x2, dxn, dh1, norm_mix)
    small = _pack_small(dg1, db[:, :D_MODEL], db[:, D_MODEL:], dg2, dg3, dw, loss)
    (small_slots,) = _alone(_ChipGather([small]), "gather_small")
    return dx.reshape(batch, seq, D_MODEL), slots, small_slots.reshape(N_DEV, 16, D_MODEL)


def _row_tile(r, c):
    return max(d for d in range(16, r + 1, 16) if r % d == 0 and (d * c <= 256 * 1024 or d == 16))


def _pair_add(grad, got, name):
    _, r, c = got.shape
    tr = r
    core = lax.axis_index("c").astype(jnp.int32).reshape(1)

    def body(core_ref, g_ref, a_ref, o_ref):
        o_ref[...] = (g_ref[...].astype(F32) + a_ref[...].astype(F32)).astype(o_ref.dtype)

    blk = pl.BlockSpec((None, tr, c), lambda q, i, core_ref: (q, i, 0))
    return pl.pallas_call(
        body, name=name,
        grid_spec=pltpu.PrefetchScalarGridSpec(
            num_scalar_prefetch=1, grid=(4, r // tr),
            in_specs=[pl.BlockSpec((None, None, tr, c), lambda q, i, core_ref: (q, core_ref[0], i, 0)), blk],
            out_specs=blk),
        out_shape=jax.ShapeDtypeStruct(got.shape, got.dtype),
        compiler_params=_params("parallel", "parallel"),
    )(core, grad.reshape(4, 2, r, c), got)


def _sum_slots(slots, name):
    n, r, c = slots.shape
    tr = _row_tile(r, c)

    def body(s_ref, o_ref):
        acc = s_ref[0].astype(F32)
        for s in range(1, n):
            acc = acc + s_ref[s].astype(F32)
        o_ref[...] = acc

    return pl.pallas_call(
        body, name=name, grid=(r // tr,),
        in_specs=[pl.BlockSpec((n, tr, c), lambda i: (0, i, 0))],
        out_specs=pl.BlockSpec((tr, c), lambda i: (i, 0)),
        out_shape=jax.ShapeDtypeStruct((r, c), F32),
        compiler_params=_params("parallel"),
    )(slots)


def _adamw_math(w, g, m, v):
    m = ADAM_B1 * m + (1.0 - ADAM_B1) * g
    v = ADAM_B2 * v + (1.0 - ADAM_B2) * (g * g)
    m_hat = m / (1.0 - ADAM_B1 ** ADAM_STEP)
    v_hat = v / (1.0 - ADAM_B2 ** ADAM_STEP)
    return -ADAM_LR * (m_hat / (jnp.sqrt(v_hat) + ADAM_EPS) + ADAM_WD * w), m, v


def _adamw(w, slots, m, v, name):
    n, r, c = slots.shape
    tr = _row_tile(r, c)

    def body(w_ref, s_ref, m_ref, v_ref, g_ref, d_ref, nm_ref, nv_ref):
        g = s_ref[0].astype(F32)
        for s in range(1, n):
            g = g + s_ref[s].astype(F32)
        g_ref[...] = g
        d_ref[...], nm_ref[...], nv_ref[...] = _adamw_math(w_ref[...], g, m_ref[...], v_ref[...])

    blk = pl.BlockSpec((tr, c), lambda i: (i, 0))
    return pl.pallas_call(
        body, name=name, grid=(r // tr,),
        in_specs=[blk, pl.BlockSpec((n, tr, c), lambda i: (0, i, 0)), blk, blk], out_specs=[blk] * 4,
        out_shape=[jax.ShapeDtypeStruct((r, c), F32)] * 4,
        compiler_params=_params("parallel"),
    )(w, slots, m, v)


def _adamw_small(ws, gs, ms, vs):
    n = len(ws)

    def body(*refs):
        for i in range(n):
            w_ref, g_ref, m_ref, v_ref = (refs[j * n + i] for j in range(4))
            d_ref, nm_ref, nv_ref = (refs[(4 + j) * n + i] for j in range(3))
            d_ref[...], nm_ref[...], nv_ref[...] = _adamw_math(w_ref[...], g_ref[...], m_ref[...], v_ref[...])

    shapes = [jax.ShapeDtypeStruct(w.shape, F32) for w in ws]
    outs = pl.pallas_call(body, name="adamw_small", out_shape=shapes * 3)(*ws, *gs, *ms, *vs)
    return outs[:n], outs[n:2 * n], outs[2 * n:]


def kernel(x, norm_mix, w_in, b_gate, rel_bias, w_ret_out, w_att_out, w_out, norm_ffn, w_ffn_gate, w_ffn_up, w_ffn_down, norm_final, loss_target, m_norm_mix, m_w_in, m_b_gate, m_rel_bias, m_w_ret_out, m_w_att_out, m_w_out, m_norm_ffn, m_w_ffn_gate, m_w_ffn_up, m_w_ffn_down, m_norm_final, v_norm_mix, v_w_in, v_b_gate, v_rel_bias, v_w_ret_out, v_w_att_out, v_w_out, v_norm_ffn, v_w_ffn_gate, v_w_ffn_up, v_w_ffn_down, v_norm_final):
    me = _index(_place())
    n_rb = rel_bias.shape[-1]

    shard = dict(w_in_t=w_in[0].T, w_gate_t=w_ffn_gate[0].T, w_up_t=w_ffn_up[0].T, w_down=w_ffn_down[0],
                 w_ret=w_ret_out[0], w_out=w_out[0], w_att_t=w_att_out[0].T)
    shard = {nm: s.astype(BF16) for nm, s in shard.items()}
    dx, slots, small_slots = _step(x, loss_target, norm_mix, b_gate, norm_ffn, norm_final, rel_bias[0], shard)
    small_sum = _sum_slots(small_slots, "sum_small")
    loss = small_sum[5, 0]

    transposed = dict(w_in="w_in_t", w_ffn_gate="w_gate_t", w_ffn_up="w_up_t", w_att_out="w_att_t")
    plain = dict(w_ffn_down="w_down", w_ret_out="w_ret", w_out="w_out")
    g = dict(
        norm_mix=small_sum[0:1], b_gate=jnp.concatenate([small_sum[1:2], small_sum[2:3]], axis=1),
        norm_ffn=small_sum[3:4], norm_final=small_sum[4:5],
        rel_bias=lax.dynamic_slice_in_dim(_bias_rows_bwd(small_sum[8:16]), me * n_rb, n_rb, axis=1),
    )
    w = dict(norm_mix=norm_mix, w_in=w_in, b_gate=b_gate, rel_bias=rel_bias, w_ret_out=w_ret_out, w_att_out=w_att_out,
             w_out=w_out, norm_ffn=norm_ffn, w_ffn_gate=w_ffn_gate, w_ffn_up=w_ffn_up, w_ffn_down=w_ffn_down,
             norm_final=norm_final)
    m = dict(norm_mix=m_norm_mix, w_in=m_w_in, b_gate=m_b_gate, rel_bias=m_rel_bias, w_ret_out=m_w_ret_out,
             w_att_out=m_w_att_out, w_out=m_w_out, norm_ffn=m_norm_ffn, w_ffn_gate=m_w_ffn_gate, w_ffn_up=m_w_ffn_up,
             w_ffn_down=m_w_ffn_down, norm_final=m_norm_final)
    v = dict(norm_mix=v_norm_mix, w_in=v_w_in, b_gate=v_b_gate, rel_bias=v_rel_bias, w_ret_out=v_w_ret_out,
             w_att_out=v_w_att_out, w_out=v_w_out, norm_ffn=v_norm_ffn, w_ffn_gate=v_w_ffn_gate, w_ffn_up=v_w_ffn_up,
             w_ffn_down=v_w_ffn_down, norm_final=v_norm_final)
    order = ("norm_mix", "w_in", "b_gate", "rel_bias", "w_ret_out", "w_att_out", "w_out", "norm_ffn",
             "w_ffn_gate", "w_ffn_up", "w_ffn_down", "norm_final")
    small_names = ("norm_mix", "b_gate", "rel_bias", "norm_ffn", "norm_final")

    def flat(a):
        return a[0] if a.ndim == 3 else a.reshape(-1, a.shape[-1])

    grad, delta, new_m, new_v = {}, {}, {}, {}
    for nm in order:
        if nm in transposed:
            res = _adamw(w[nm][0].T, slots[transposed[nm]], m[nm][0].T, v[nm][0].T, "adamw_" + nm)
            grad[nm], delta[nm], new_m[nm], new_v[nm] = (a.T[None] for a in res)
        elif nm in plain:
            res = _adamw(flat(w[nm]), slots[plain[nm]], flat(m[nm]), flat(v[nm]), "adamw_" + nm)
            grad[nm], delta[nm], new_m[nm], new_v[nm] = (a.reshape(w[nm].shape) for a in res)
    ds, nms, nvs = _adamw_small([flat(w[nm]) for nm in small_names], [g[nm] for nm in small_names],
                                [flat(m[nm]) for nm in small_names], [flat(v[nm]) for nm in small_names])
    for i, nm in enumerate(small_names):
        grad[nm], delta[nm], new_m[nm], new_v[nm] = (a.reshape(w[nm].shape) for a in (g[nm], ds[i], nms[i], nvs[i]))

    return (loss, dx, *[grad[nm] for nm in order], *[delta[nm] for nm in order],
            *[new_m[nm] for nm in order], *[new_v[nm] for nm in order])
```

```python
import numpy as np
import jax
import jax.numpy as jnp
from jax import lax
from jax.experimental import pallas as pl
from jax.experimental.pallas import tpu as pltpu

F32 = jnp.float32
BF16 = jnp.bfloat16
MESH = pl.DeviceIdType.MESH

D_MODEL = 1024
CHUNK = 64
RET_HEADS = 4
RET_KEY_DIM = 128
RET_VAL_DIM = 256
ATT_HEADS = 8
BAND_CHUNKS = 8
MAX_REL = 256
N_REL = CHUNK + MAX_REL
D_FF = 2816
N_IN = 6656
ROPE_BASE = 10000.0
EPS = 1e-6
NEG_INF = -1e30
C_RQ, C_RK, C_RV, C_RG, C_AQ, C_AK, C_AV, C_GL = 0, 512, 1024, 2048, 3072, 3584, 4096, 4608

ADAM_LR = 0.001
ADAM_B1 = 0.9
ADAM_B2 = 0.999
ADAM_EPS = 1e-08
ADAM_WD = 0.01
ADAM_STEP = 10

N_DEV = 8
LANES = 128
RET_TILE = 256
ATT_Q = 256
ATT_PAD = BAND_CHUNKS * CHUNK
ATT_WIN = ATT_PAD + ATT_Q
ATT_STARTS = ATT_PAD // ATT_Q
ATT_ROWS = 32
BIAS_LEN = 1024
VMEM_LIMIT = 56 * 1024 * 1024


def _params(*sem):
    return pltpu.CompilerParams(dimension_semantics=sem, vmem_limit_bytes=VMEM_LIMIT)


def _dot(a, b):
    return lax.dot_general(a, b, (((1,), (0,)), ((), ())), preferred_element_type=F32)


def _dot_nt(a, b):
    return lax.dot_general(a, b, (((1,), (1,)), ((), ())), preferred_element_type=F32)


def _dot_tn(a, b):
    return lax.dot_general(a, b, (((0,), (0,)), ((), ())), preferred_element_type=F32)


def _sigmoid(x):
    return 1.0 / (1.0 + jnp.exp(-x))


def _rms_bwd(x, g, dy):
    r = lax.rsqrt(jnp.mean(x * x, axis=-1, keepdims=True) + EPS)
    u = dy * g
    dx = r * u - x * (r * r * r) * jnp.mean(u * x, axis=-1, keepdims=True)
    return dx, dy * x * r


def _place():
    return lax.axis_index("x"), lax.axis_index("y"), lax.axis_index("c")


def _peer(k):
    x, y, c = _place()
    return ((1 - x) if k & 4 else x, (1 - y) if k & 2 else y, (1 - c) if k & 1 else c)


def _index(place):
    return 4 * place[0] + 2 * place[1] + place[2]


def _rows(ref, block, nrows):
    align = 16 if ref.dtype == BF16 else 8
    return ref.at[pl.ds(pl.multiple_of(block * nrows, align), nrows)]


class _Exchange:
    def __init__(self, arrays, scatter):
        self.arrays, self.scatter, self.n = list(arrays), scatter, len(arrays)

    def out_shape(self):
        if self.scatter:
            return [jax.ShapeDtypeStruct((N_DEV, a.shape[0] // N_DEV) + a.shape[1:], a.dtype) for a in self.arrays]
        return [jax.ShapeDtypeStruct((N_DEV * a.shape[0],) + a.shape[1:], a.dtype) for a in self.arrays]

    def scratch(self):
        return [pltpu.SemaphoreType.DMA((self.n, N_DEV - 1)), pltpu.SemaphoreType.DMA((self.n, N_DEV - 1)),
                pltpu.SemaphoreType.DMA((self.n,))]

    def _copies(self, ins, outs, sems):
        send_sems, recv_sems, local_sems = sems
        me = _index(_place())

        def src(w, to):
            return _rows(ins[w], to, ins[w].shape[0] // N_DEV) if self.scatter else ins[w]

        def dst(w, origin):
            return outs[w].at[origin] if self.scatter else _rows(outs[w], origin, ins[w].shape[0])

        def remote(w, k, to, origin):
            return pltpu.make_async_remote_copy(src_ref=src(w, to), dst_ref=dst(w, origin),
                                                send_sem=send_sems.at[w, k - 1], recv_sem=recv_sems.at[w, k - 1],
                                                device_id=_peer(k), device_id_type=MESH)

        pairs = [(w, k) for w in range(self.n) for k in range(1, N_DEV)]
        own = lambda: [pltpu.make_async_copy(src(w, me), dst(w, me), local_sems.at[w]) for w in range(self.n)]
        sent = lambda: [remote(w, k, _index(_peer(k)), me) for w, k in pairs]
        arriving = lambda: [remote(w, k, me, _index(_peer(k))) for w, k in pairs]
        return own, sent, arriving

    def start(self, ins, outs, sems):
        own, sent, _ = self._copies(ins, outs, sems)
        for cp in own() + sent():
            cp.start()

    def wait(self, ins, outs, sems):
        own, sent, arriving = self._copies(ins, outs, sems)
        for cp in arriving():
            cp.wait_recv()
        for cp in sent():
            cp.wait_send()
        for cp in own():
            cp.wait()


class _PairSwap:
    def __init__(self, arrays):
        self.arrays, self.n = list(arrays), len(arrays)

    def out_shape(self):
        return [jax.ShapeDtypeStruct((4, a.shape[0] // N_DEV) + a.shape[1:], a.dtype) for a in self.arrays]

    def scratch(self):
        return [pltpu.SemaphoreType.DMA((self.n, 4)), pltpu.SemaphoreType.DMA((self.n, 4))]

    def _copies(self, ins, outs, sems):
        send_sems, recv_sems = sems
        x, y, c = _place()
        return [pltpu.make_async_remote_copy(
            src_ref=_rows(ins[w], 2 * q + 1 - c, ins[w].shape[0] // N_DEV), dst_ref=outs[w].at[q],
            send_sem=send_sems.at[w, q], recv_sem=recv_sems.at[w, q],
            device_id=(x, y, 1 - c), device_id_type=MESH) for w in range(self.n) for q in range(4)]

    def start(self, ins, outs, sems):
        for cp in self._copies(ins, outs, sems):
            cp.start()

    def wait(self, ins, outs, sems):
        for cp in self._copies(ins, outs, sems):
            cp.wait()


class _ChipScatter:
    def __init__(self, arrays):
        self.arrays, self.n = list(arrays), len(arrays)

    def out_shape(self):
        return [jax.ShapeDtypeStruct(a.shape, a.dtype) for a in self.arrays]

    def scratch(self):
        return [pltpu.SemaphoreType.DMA((self.n, 3)), pltpu.SemaphoreType.DMA((self.n, 3)),
                pltpu.SemaphoreType.DMA((self.n,))]

    def _copies(self, ins, outs, sems):
        send_sems, recv_sems, local_sems = sems
        x, y, c = _place()
        mine = 2 * x + y
        sent, arriving = [], []
        for w in range(self.n):
            for k in range(1, 4):
                tx, ty = (1 - x) if k & 2 else x, (1 - y) if k & 1 else y
                other = 2 * tx + ty
                sent.append(lambda w=w, k=k, tx=tx, ty=ty, other=other: pltpu.make_async_remote_copy(
                    src_ref=ins[w].at[other], dst_ref=outs[w].at[mine], send_sem=send_sems.at[w, k - 1],
                    recv_sem=recv_sems.at[w, k - 1], device_id=(tx, ty, c), device_id_type=MESH))
                arriving.append(lambda w=w, k=k, tx=tx, ty=ty, other=other: pltpu.make_async_remote_copy(
                    src_ref=ins[w].at[mine], dst_ref=outs[w].at[other], send_sem=send_sems.at[w, k - 1],
                    recv_sem=recv_sems.at[w, k - 1], device_id=(tx, ty, c), device_id_type=MESH))
        own = [lambda w=w: pltpu.make_async_copy(ins[w].at[mine], outs[w].at[mine], local_sems.at[w])
               for w in range(self.n)]
        return own, sent, arriving

    def start(self, ins, outs, sems):
        own, sent, _ = self._copies(ins, outs, sems)
        for cp in own + sent:
            cp().start()

    def wait(self, ins, outs, sems):
        own, sent, arriving = self._copies(ins, outs, sems)
        for cp in arriving:
            cp().wait_recv()
        for cp in sent:
            cp().wait_send()
        for cp in own:
            cp().wait()


class _ChipGather:
    def __init__(self, arrays, parts=None, into=None):
        self.arrays, self.n, self.into = list(arrays), len(arrays), into
        self.parts = parts or [(0, 1)] * self.n

    def out_shape(self):
        return [jax.ShapeDtypeStruct((N_DEV * a.shape[0],) + a.shape[1:], a.dtype) for a in self.arrays]

    def scratch(self):
        return [pltpu.SemaphoreType.DMA((self.n, N_DEV - 1)), pltpu.SemaphoreType.DMA((self.n, N_DEV - 1)),
                pltpu.SemaphoreType.DMA((self.n,))]

    def _parts(self, ins, outs, sems):
        send_sems, recv_sems, local_sems = sems
        x, y, c = _place()
        me, sibling = (x, y, c), (x, y, 1 - c)
        chips = [(1 - x, y), (x, 1 - y), (1 - x, 1 - y)]

        def rows(w, place, whole):
            (index, count), r = self.parts[w], ins[w].shape[0]
            lo, size = (0, r) if whole else (index * (r // count), r // count)
            align = 16 if ins[w].dtype == BF16 else 8
            return outs[w].at[pl.ds(pl.multiple_of(_index(place) * r + lo, align), size)]

        def mine(w, whole):
            (index, count), r = self.parts[w], ins[w].shape[0]
            return ins[w] if whole or count == 1 else ins[w].at[pl.ds(index * (r // count), r // count)]

        def copy(w, k, block, to, own=False):
            whole = k == 0
            return pltpu.make_async_remote_copy(src_ref=mine(w, whole) if own else rows(w, block, whole),
                                                dst_ref=rows(w, block, whole),
                                                send_sem=send_sems.at[w, k], recv_sem=recv_sems.at[w, k],
                                                device_id=to, device_id_type=MESH)

        def local(w):
            return pltpu.make_async_copy(ins[w], rows(w, me, True), local_sems.at[w])

        return me, sibling, chips, c, copy, local, [index == 0 for index, _ in self.parts]

    def start(self, ins, outs, sems):
        me, sibling, chips, c, copy, local, places_own = self._parts(ins, outs, sems)
        for w in range(self.n):
            if places_own[w]:
                local(w).start()
                copy(w, 0, me, sibling, own=True).start()
            for j, chip in enumerate(chips):
                copy(w, 1 + j, me, (*chip, c), own=True).start()

    def wait(self, ins, outs, sems):
        me, sibling, chips, c, copy, local, places_own = self._parts(ins, outs, sems)
        for w in range(self.n):
            for j, chip in enumerate(chips):
                copy(w, 1 + j, (*chip, c), me).wait_recv()
                copy(w, 4 + j, (*chip, c), sibling).start()
        for w in range(self.n):
            if places_own[w]:
                copy(w, 0, sibling, me).wait_recv()
                copy(w, 0, me, sibling, own=True).wait_send()
                local(w).wait()
            for j, chip in enumerate(chips):
                copy(w, 4 + j, (*chip, 1 - c), me).wait_recv()
                copy(w, 1 + j, me, (*chip, c), own=True).wait_send()
                copy(w, 4 + j, (*chip, c), sibling).wait_send()


def _call(body, *, name, grid, in_specs, out_specs, out_shape, scratch=(), semantics, args, exchange=None):
    if exchange is None:
        return pl.pallas_call(body, name=name, grid=grid, in_specs=in_specs, out_specs=out_specs, out_shape=out_shape,
                              scratch_shapes=list(scratch), compiler_params=_params(*semantics))(*args), None
    n_in, n_out, n_scr, nx = len(in_specs), len(out_specs), len(scratch), exchange.n
    into = list(getattr(exchange, "into", None) or [])

    def full_body(*refs):
        ins, refs = refs[:n_in], refs[n_in:]
        x_in, refs = refs[:nx], refs[nx + len(into):]
        outs, refs = refs[:n_out], refs[n_out:]
        x_out, refs = refs[:nx], refs[nx:]
        scr, sems = refs[:n_scr], refs[n_scr:]
        first, last = True, True
        for axis, size in enumerate(grid):
            first = jnp.logical_and(first, pl.program_id(axis) == 0)
            last = jnp.logical_and(last, pl.program_id(axis) == size - 1)
        if grid:
            pl.when(first)(lambda: exchange.start(x_in, x_out, sems))
        else:
            exchange.start(x_in, x_out, sems)
        body(*ins, *outs, *scr)
        if grid:
            pl.when(last)(lambda: exchange.wait(x_in, x_out, sems))
        else:
            exchange.wait(x_in, x_out, sems)

    hbm = pl.BlockSpec(memory_space=pltpu.HBM)
    res = pl.pallas_call(
        full_body, name=name, grid=grid,
        in_specs=list(in_specs) + [hbm] * (nx + len(into)), out_specs=list(out_specs) + [hbm] * nx,
        out_shape=list(out_shape) + exchange.out_shape(),
        scratch_shapes=list(scratch) + exchange.scratch(),
        input_output_aliases={n_in + nx + w: n_out + w for w in range(len(into))},
        compiler_params=_params(*(["arbitrary"] * len(grid))),
    )(*args, *exchange.arrays, *into)
    return res[:n_out], res[n_out:]


def _alone(exchange, name):
    return _call(lambda: None, name=name, grid=(), in_specs=[], out_specs=[], out_shape=[], semantics=(),
                 args=(), exchange=exchange)[1]


def _mm(a, b, *, ta=False, tb=False, out_dtype, tm, tn, tk, name, exchange=None):
    m, k = (a.shape[1], a.shape[0]) if ta else a.shape
    n = b.shape[0] if tb else b.shape[1]
    assert k == (b.shape[1] if tb else b.shape[0])
    tm, tn, tk = min(tm, m), min(tn, n), min(tk, k)
    assert m % tm == 0 and n % tn == 0 and k % tk == 0, (name, m, n, k)
    nk = k // tk
    dims = (((0 if ta else 1,), (1 if tb else 0,)), ((), ()))

    def body(a_ref, b_ref, o_ref, *acc):
        prod = lax.dot_general(a_ref[...].astype(BF16), b_ref[...].astype(BF16), dims, preferred_element_type=F32)
        if nk == 1:
            o_ref[...] = prod.astype(o_ref.dtype)
            return
        acc_ref, kk = acc[0], pl.program_id(2)

        @pl.when(kk == 0)
        def _():
            acc_ref[...] = prod

        @pl.when((kk > 0) & (kk < nk - 1))
        def _():
            acc_ref[...] += prod

        @pl.when(kk == nk - 1)
        def _():
            o_ref[...] = (acc_ref[...] + prod).astype(o_ref.dtype)

    a_spec = (pl.BlockSpec((tk, tm), lambda i, j, kk: (kk, i)) if ta
              else pl.BlockSpec((tm, tk), lambda i, j, kk: (i, kk)))
    b_spec = (pl.BlockSpec((tn, tk), lambda i, j, kk: (j, kk)) if tb
              else pl.BlockSpec((tk, tn), lambda i, j, kk: (kk, j)))
    (out,), moved = _call(
        body, name=name, grid=(m // tm, n // tn, nk),
        in_specs=[a_spec, b_spec],
        out_specs=[pl.BlockSpec((tm, tn), lambda i, j, kk: (i, j))],
        out_shape=[pltpu.HBM((m, n), out_dtype)],
        scratch=[pltpu.VMEM((tm, tn), F32)] if nk > 1 else [],
        semantics=("parallel", "parallel", "arbitrary"), args=(a, b), exchange=exchange)
    return out if exchange is None else (out, moved)


def _mm_pieces(pieces, b, *, ta, out_dtype, tm, tn, tk, name, exchange=None):
    rows, n = pieces[0].shape[0], b.shape[1]
    step = tm if ta else tk
    assert all(p.shape[0] == rows and p.shape[1] % step == 0 for p in pieces), name
    edges = [int(e) for e in np.cumsum([0] + [p.shape[1] // step for p in pieces])]
    total = edges[-1] * step
    m, k = (total, rows) if ta else (rows, total)
    assert b.shape[0] == k and m % tm == 0 and n % tn == 0 and k % tk == 0, name
    nk, npieces = k // tk, len(pieces)
    dims = (((0 if ta else 1,), (0,)), ((), ()))

    def body(*refs):
        a_refs, (b_ref, o_ref, acc_ref) = refs[:npieces], refs[npieces:]
        kk = pl.program_id(2)
        pos = pl.program_id(0) if ta else kk

        @pl.when(kk == 0)
        def _():
            acc_ref[...] = jnp.zeros_like(acc_ref)

        for p, a_ref in enumerate(a_refs):
            @pl.when((pos >= edges[p]) & (pos < edges[p + 1]))
            def _(a_ref=a_ref):
                acc_ref[...] += lax.dot_general(a_ref[...], b_ref[...], dims, preferred_element_type=F32)

        @pl.when(kk == nk - 1)
        def _():
            o_ref[...] = acc_ref[...].astype(o_ref.dtype)

    def a_spec(p):
        lo, last = edges[p], edges[p + 1] - edges[p] - 1
        if ta:
            def index(i, j, kk):
                inside = (i >= lo) & (i <= lo + last)
                return jnp.where(inside, kk, 0), jnp.clip(i - lo, 0, last)
            return pl.BlockSpec((tk, tm), index)
        return pl.BlockSpec((tm, tk), lambda i, j, kk: (i, jnp.clip(kk - lo, 0, last)))

    (out,), moved = _call(
        body, name=name, grid=(m // tm, n // tn, nk),
        in_specs=[a_spec(p) for p in range(npieces)] + [pl.BlockSpec((tk, tn), lambda i, j, kk: (kk, j))],
        out_specs=[pl.BlockSpec((tm, tn), lambda i, j, kk: (i, j))],
        out_shape=[pltpu.HBM((m, n), out_dtype)],
        scratch=[pltpu.VMEM((tm, tn), F32)],
        semantics=("parallel", "parallel", "arbitrary"), args=(*pieces, b), exchange=exchange)
    return out if exchange is None else (out, moved)


def _rms_fwd(x2, g, exchange):
    t = x2.shape[0]
    tm = min(512, t)

    def body(x_ref, g_ref, o_ref):
        x = x_ref[...]
        r = lax.rsqrt(jnp.mean(x * x, axis=-1, keepdims=True) + EPS)
        o_ref[...] = (x * r * g_ref[...]).astype(o_ref.dtype)

    return _call(
        body, name="rms_in_fwd", grid=(t // tm,),
        in_specs=[pl.BlockSpec((tm, D_MODEL), lambda i: (i, 0)), pl.BlockSpec((1, D_MODEL), lambda i: (0, 0))],
        out_specs=[pl.BlockSpec((tm, D_MODEL), lambda i: (i, 0))],
        out_shape=[jax.ShapeDtypeStruct((t, D_MODEL), BF16)],
        semantics=("parallel",), args=(x2, g), exchange=exchange)


def _decay(lg):
    row = lax.broadcasted_iota(jnp.int32, (RET_TILE, RET_TILE), 0)
    col = lax.broadcasted_iota(jnp.int32, (RET_TILE, RET_TILE), 1)
    within = jnp.exp(lg * jnp.abs(row - col).astype(F32))
    inside = jnp.where((col >> 6) <= (row >> 6), within, 0.0)
    pos = lax.broadcasted_iota(jnp.int32, (RET_TILE, 1), 0).astype(F32)
    q_dec = jnp.exp(lg * (pos + 1.0))
    k_dec = jnp.exp(lg * (RET_TILE - 1.0 - pos))
    tile_dec = jnp.exp(lg * float(RET_TILE))
    return inside, q_dec, k_dec, tile_dec


def _scaled(a_bf16, dec):
    return (a_bf16.astype(F32) * dec).astype(BF16)


def _ret_fwd(proj, cs, sn, lg_arr, batch, seq, exchange):
    t = batch * seq
    nt = seq // RET_TILE

    def body(q_ref, k_ref, v_ref, rg_ref, cs_ref, sn_ref, lg_ref, gro_ref, o_ref, qr_ref, kr_ref):
        lg = lg_ref[:, 0:1]
        cs_t, sn_t = cs_ref[...], sn_ref[...]
        q = q_ref[...].astype(F32)
        k = k_ref[...].astype(F32)
        qr_ref[...] = (q * cs_t + pltpu.roll(q, 64, 1) * sn_t).astype(BF16)
        kr_ref[...] = ((k * cs_t + pltpu.roll(k, 64, 1) * sn_t) * (RET_KEY_DIM ** -0.5)).astype(BF16)
        inside, q_dec, k_dec, tile_dec = _decay(lg)
        state = jnp.zeros((RET_KEY_DIM, RET_VAL_DIM), F32)
        for i in range(nt):
            rows = slice(i * RET_TILE, (i + 1) * RET_TILE)
            qi, ki, vi = qr_ref[rows, :], kr_ref[rows, :], v_ref[rows, :]
            acc = _dot((_dot_nt(qi, ki) * inside).astype(BF16), vi)
            if i > 0:
                acc = acc + _dot(_scaled(qi, q_dec), state.astype(BF16))
            if i < nt - 1:
                state = state * tile_dec + _dot_tn(_scaled(ki, k_dec), vi)
            o_ref[rows, :] = acc
            xc = acc - jnp.mean(acc, axis=-1, keepdims=True)
            nrm = xc * lax.rsqrt(jnp.mean(xc * xc, axis=-1, keepdims=True) + EPS)
            rg = rg_ref[rows, :].astype(F32)
            gro_ref[rows, :] = (rg * _sigmoid(rg) * nrm).astype(BF16)

    def col(base, width):
        return lambda b, h: (b, base // width + h)

    return _call(
        body, name="ret_fwd", grid=(batch, RET_HEADS),
        in_specs=[pl.BlockSpec((seq, RET_KEY_DIM), col(C_RQ, RET_KEY_DIM)),
                  pl.BlockSpec((seq, RET_KEY_DIM), col(C_RK, RET_KEY_DIM)),
                  pl.BlockSpec((seq, RET_VAL_DIM), col(C_RV, RET_VAL_DIM)),
                  pl.BlockSpec((seq, RET_VAL_DIM), col(C_RG, RET_VAL_DIM)),
                  pl.BlockSpec((seq, RET_KEY_DIM), lambda b, h: (0, 0)),
                  pl.BlockSpec((seq, RET_KEY_DIM), lambda b, h: (0, 0)),
                  pl.BlockSpec((None, 1, LANES), lambda b, h: (h, 0, 0))],
        out_specs=[pl.BlockSpec((seq, RET_VAL_DIM), lambda b, h: (b, h)),
                   pl.BlockSpec((seq, RET_VAL_DIM), lambda b, h: (b, h)),
                   pl.BlockSpec((seq, RET_KEY_DIM), lambda b, h: (b, h)),
                   pl.BlockSpec((seq, RET_KEY_DIM), lambda b, h: (b, h))],
        out_shape=[jax.ShapeDtypeStruct((t, RET_HEADS * RET_VAL_DIM), BF16),
                   jax.ShapeDtypeStruct((t, RET_HEADS * RET_VAL_DIM), F32),
                   jax.ShapeDtypeStruct((t, RET_HEADS * RET_KEY_DIM), BF16),
                   jax.ShapeDtypeStruct((t, RET_HEADS * RET_KEY_DIM), BF16)],
        semantics=("parallel", "parallel"), args=(proj, proj, proj, proj, cs, sn, lg_arr), exchange=exchange)


def _att_bias(w_ref, bias_ref):
    n_i = lax.broadcasted_iota(jnp.int32, (ATT_Q, BIAS_LEN), 0)
    qc = lax.broadcasted_iota(jnp.int32, (ATT_Q, ATT_WIN), 0) >> 6
    kc = lax.broadcasted_iota(jnp.int32, (ATT_Q, ATT_WIN), 1) >> 6
    dc = qc + BAND_CHUNKS - kc
    band = (dc >= 0) & (dc <= BAND_CHUNKS)
    key = lax.broadcasted_iota(jnp.int32, (ATT_Q, ATT_WIN), 1)
    for e in range(2):
        xw = jnp.broadcast_to(w_ref[e:e + 1, :], (ATT_Q, BIAS_LEN))
        for bit in range(8):
            xw = jnp.where(((n_i >> bit) & 1) == 1, pltpu.roll(xw, 1 << bit, 1), xw)
        bias = jnp.where(band, xw[:, BIAS_LEN - ATT_WIN:], NEG_INF)
        for first in range(ATT_STARTS):
            bias_ref[first, e] = jnp.where(key + (first * ATT_Q - ATT_PAD) >= 0, bias, NEG_INF)
        bias_ref[ATT_STARTS, e] = bias


ATT_PAIRS = 2
ATT_COLS = ATT_PAIRS * LANES


def _att_specs(batch, seq):
    ni = seq // ATT_Q
    q_spec = pl.BlockSpec((ATT_Q, ATT_COLS), lambda g, b, i: (b * ni + i, C_AQ // ATT_COLS + g))
    k_spec = pl.BlockSpec((seq, ATT_COLS), lambda g, b, i: (b, C_AK // ATT_COLS + g))
    v_spec = pl.BlockSpec((seq, ATT_COLS), lambda g, b, i: (b, C_AV // ATT_COLS + g))
    w_spec = pl.BlockSpec((ATT_PAIRS, 2, BIAS_LEN), lambda g, b, i: (g, 0, 0))
    b_spec = pl.BlockSpec((ATT_PAIRS, ATT_STARTS + 1, 2, ATT_Q, ATT_WIN), lambda g, b, i: (g, 0, 0, 0, 0))
    pad = pltpu.VMEM((seq + ATT_PAD, ATT_COLS), BF16)
    return ni, q_spec, k_spec, v_spec, w_spec, b_spec, pad


def _att_bias_tiles(wvec, exchange):
    (tiles,), moved = _call(
        lambda w_ref, o_ref: _att_bias(w_ref, o_ref), name="att_bias", grid=(ATT_HEADS // 2,),
        in_specs=[pl.BlockSpec((None, 2, BIAS_LEN), lambda hp: (hp, 0, 0))],
        out_specs=[pl.BlockSpec((None, ATT_STARTS + 1, 2, ATT_Q, ATT_WIN), lambda hp: (hp, 0, 0, 0, 0))],
        out_shape=[jax.ShapeDtypeStruct((ATT_HEADS // 2, ATT_STARTS + 1, 2, ATT_Q, ATT_WIN), F32)],
        semantics=("parallel",), args=(wvec,), exchange=exchange)
    return tiles, moved


def _att_pad(src_ref, pad_ref):
    pad_ref[:ATT_PAD, :] = jnp.zeros((ATT_PAD, ATT_COLS), BF16)
    pad_ref[ATT_PAD:, :] = src_ref[...]


def _att_head(q2, sel):
    return jnp.where(sel, q2, jnp.zeros_like(q2)) * 0.125


def _att_softmax_rows(s_ref, bias_ref, rows):
    s = s_ref[rows, :] + bias_ref[rows, :]
    ex = jnp.exp(s - jnp.max(s, axis=-1, keepdims=True))
    return ex, 1.0 / jnp.sum(ex, axis=-1, keepdims=True)


def _att_fwd(proj, bias, batch, seq, exchange):
    ni, q_spec, k_spec, v_spec, _, b_spec, pad = _att_specs(batch, seq)

    def body(q_ref, k_ref, v_ref, bias_ref, o_ref, kp_ref, vp_ref, s_ref, e_ref):
        i = pl.program_id(2)

        @pl.when(i == 0)
        def _():
            _att_pad(k_ref, kp_ref)
            _att_pad(v_ref, vp_ref)

        win = pl.ds(pl.multiple_of(i * ATT_Q, ATT_Q), ATT_WIN)
        lo = lax.broadcasted_iota(jnp.int32, (1, LANES), 1) < 64
        start = jnp.minimum(i, ATT_STARTS)
        for pair in range(ATT_PAIRS):
            cols = slice(pair * LANES, (pair + 1) * LANES)
            k2, v2, q2 = kp_ref[win, cols], vp_ref[win, cols], q_ref[:, cols]
            out = jnp.zeros((ATT_Q, LANES), F32)
            for e in range(2):
                h = 2 * pair + e
                sel = lo if e == 0 else jnp.logical_not(lo)
                s_ref[h] = _dot_nt(_att_head(q2, sel), k2)
                rsum = []
                for c in range(ATT_Q // ATT_ROWS):
                    rows = slice(c * ATT_ROWS, (c + 1) * ATT_ROWS)
                    ex, r = _att_softmax_rows(s_ref.at[h], bias_ref.at[pair, start, e], rows)
                    e_ref[h, rows, :] = ex.astype(BF16)
                    rsum.append(r)
                out = out + _dot(e_ref[h], jnp.where(sel, v2, jnp.zeros_like(v2))) * jnp.concatenate(rsum, axis=0)
            o_ref[:, cols] = out.astype(BF16)

    heads = 2 * ATT_PAIRS
    return _call(
        body, name="att_fwd", grid=(ATT_HEADS // heads, batch, ni),
        in_specs=[q_spec, k_spec, v_spec, b_spec],
        out_specs=[pl.BlockSpec((ATT_Q, ATT_COLS), lambda g, b, i: (b * ni + i, g))],
        out_shape=[jax.ShapeDtypeStruct((batch * seq, ATT_HEADS * 64), BF16)],
        scratch=[pad, pad, pltpu.VMEM((heads, ATT_Q, ATT_WIN), F32), pltpu.VMEM((heads, ATT_Q, ATT_WIN), BF16)],
        semantics=("arbitrary", "arbitrary", "arbitrary"), args=(proj, proj, proj, bias), exchange=exchange)


GL_HALF = 512


def _gl_specs(tm):
    return [pl.BlockSpec((tm, GL_HALF), lambda i, c=C_GL // GL_HALF + j: (i, c)) for j in range(4)]


def _gates(gl_refs, b_ref):
    logits = [ref[...].astype(F32) for ref in gl_refs]
    gr = _sigmoid(jnp.concatenate(logits[:2], axis=1) + b_ref[:, :D_MODEL])
    ga = _sigmoid(jnp.concatenate(logits[2:], axis=1) + b_ref[:, D_MODEL:])
    return gr, ga


def _whole(a):
    return pl.BlockSpec(a.shape, lambda i: (0,) * a.ndim)


def _mix_out_fwd(gro, ao, proj, b_gate, w_ret, w_att_t, x2, w_out, g2):
    t = gro.shape[0]
    tm = min(256, t)

    def body(gro_ref, ao_ref, gl0, gl1, gl2, gl3, b_ref, wr_ref, wa_ref, x_ref, wo_ref, g_ref,
             z_ref, yr_ref, ya_ref, h_ref, hn_ref):
        yr = _dot(gro_ref[...], wr_ref[...])
        ya = _dot_nt(ao_ref[...], wa_ref[...])
        yr_ref[...] = yr.astype(BF16)
        ya_ref[...] = ya.astype(BF16)
        gr, ga = _gates((gl0, gl1, gl2, gl3), b_ref)
        z = (gr * yr + ga * ya).astype(BF16)
        z_ref[...] = z
        h = x_ref[...] + _dot(z, wo_ref[...])
        h_ref[...] = h
        r = lax.rsqrt(jnp.mean(h * h, axis=-1, keepdims=True) + EPS)
        hn_ref[...] = (h * r * g_ref[...]).astype(BF16)

    row = pl.BlockSpec((tm, D_MODEL), lambda i: (i, 0))
    return pl.pallas_call(
        body, name="mix_out_fwd", grid=(t // tm,),
        in_specs=[row, pl.BlockSpec((tm, 512), lambda i: (i, 0)), *_gl_specs(tm),
                  _whole(b_gate), _whole(w_ret), _whole(w_att_t), row, _whole(w_out), _whole(g2)],
        out_specs=[row] * 5,
        out_shape=[jax.ShapeDtypeStruct((t, D_MODEL), BF16)] * 3
                  + [jax.ShapeDtypeStruct((t, D_MODEL), F32), jax.ShapeDtypeStruct((t, D_MODEL), BF16)],
        compiler_params=_params("parallel"),
    )(gro, ao, proj, proj, proj, proj, b_gate, w_ret, w_att_t, x2, w_out, g2)


def _ffn_up(hn, wg_t, wu_t):
    t = hn.shape[0]
    tm, tn = min(512, t), D_FF // 2

    def body(h_ref, wg_ref, wu_ref, g_ref, u_ref, a_ref):
        g = _dot_nt(h_ref[...], wg_ref[...])
        u = _dot_nt(h_ref[...], wu_ref[...])
        g_ref[...] = g.astype(BF16)
        u_ref[...] = u.astype(BF16)
        a_ref[...] = (g * _sigmoid(g) * u).astype(BF16)

    w_spec = pl.BlockSpec((tn, D_MODEL), lambda j, i: (j, 0))
    out = pl.BlockSpec((tm, tn), lambda j, i: (i, j))
    return pl.pallas_call(
        body, name="ffn_up", grid=(D_FF // tn, t // tm),
        in_specs=[pl.BlockSpec((tm, D_MODEL), lambda j, i: (i, 0)), w_spec, w_spec],
        out_specs=[out, out, out],
        out_shape=[jax.ShapeDtypeStruct((t, D_FF), BF16)] * 3,
        compiler_params=_params("parallel", "parallel"),
    )(hn, wg_t, wu_t)


def _ffn_down_loss(a, h1, tgt, w_down, g3):
    t = a.shape[0]
    tm = min(512, t)

    def body(a_ref, h_ref, t_ref, w_ref, g_ref, dh_ref, dhb_ref, loss_ref, dg_ref):
        @pl.when(pl.program_id(0) == 0)
        def _():
            loss_ref[...] = jnp.zeros_like(loss_ref)
            dg_ref[...] = jnp.zeros_like(dg_ref)

        g = g_ref[...]
        h2 = h_ref[...] + _dot(a_ref[...], w_ref[...])
        r = lax.rsqrt(jnp.mean(h2 * h2, axis=-1, keepdims=True) + EPS)
        err = h2 * r * g - t_ref[...]
        loss_ref[...] += jnp.sum(err * err) * (0.5 / D_MODEL)
        dy = err * (1.0 / D_MODEL)
        dh, dg_rows = _rms_bwd(h2, g, dy)
        dg_ref[...] += jnp.sum(dg_rows, axis=0, keepdims=True)
        dh_ref[...] = dh
        dhb_ref[...] = dh.astype(BF16)

    row = pl.BlockSpec((tm, D_MODEL), lambda i: (i, 0))
    vec = pl.BlockSpec((1, D_MODEL), lambda i: (0, 0))
    return pl.pallas_call(
        body, name="ffn_down_loss", grid=(t // tm,),
        in_specs=[pl.BlockSpec((tm, D_FF), lambda i: (i, 0)), row, row,
                  pl.BlockSpec((D_FF, D_MODEL), lambda i: (0, 0)), vec],
        out_specs=[row, row, pl.BlockSpec((1, LANES), lambda i: (0, 0)), vec],
        out_shape=[jax.ShapeDtypeStruct((t, D_MODEL), F32), jax.ShapeDtypeStruct((t, D_MODEL), BF16),
                   jax.ShapeDtypeStruct((1, LANES), F32), jax.ShapeDtypeStruct((1, D_MODEL), F32)],
        compiler_params=_params("arbitrary"),
    )(a, h1, tgt, w_down, g3)


def _ffn_bwd_act(dh2b, w_down, g_act, u_act, exchange):
    t = dh2b.shape[0]
    tm, tn = min(512, t), D_FF // 2

    def body(d_ref, w_ref, g_ref, u_ref, dg_ref, du_ref):
        da = _dot_nt(d_ref[...], w_ref[...])
        g = g_ref[...].astype(F32)
        u = u_ref[...].astype(F32)
        sg = _sigmoid(g)
        dg_ref[...] = (da * u * sg * (1.0 + g * (1.0 - sg))).astype(BF16)
        du_ref[...] = (da * g * sg).astype(BF16)

    blk = pl.BlockSpec((tm, tn), lambda j, i: (i, j))
    return _call(
        body, name="ffn_bwd_act", grid=(D_FF // tn, t // tm),
        in_specs=[pl.BlockSpec((tm, D_MODEL), lambda j, i: (i, 0)),
                  pl.BlockSpec((tn, D_MODEL), lambda j, i: (j, 0)), blk, blk],
        out_specs=[blk, blk],
        out_shape=[jax.ShapeDtypeStruct((t, D_FF), BF16)] * 2,
        semantics=("parallel", "parallel"), args=(dh2b, w_down, g_act, u_act), exchange=exchange)


def _ffn_bwd_in(dg, du, wg_t, wu_t, h1, dh2, g2, exchange):
    t = dg.shape[0]
    tm, tk = min(512, t), D_FF // 2
    nk = D_FF // tk

    def body(dg_ref, du_ref, wg_ref, wu_ref, h_ref, d2_ref, g_ref, dh_ref, dhb_ref, gn_ref, acc_ref):
        i, kk = pl.program_id(0), pl.program_id(1)

        @pl.when((i == 0) & (kk == 0))
        def _():
            gn_ref[...] = jnp.zeros_like(gn_ref)

        @pl.when(kk == 0)
        def _():
            acc_ref[...] = jnp.zeros_like(acc_ref)

        acc_ref[...] += _dot(dg_ref[...], wg_ref[...]) + _dot(du_ref[...], wu_ref[...])

        @pl.when(kk == nk - 1)
        def _():
            dx, dg_rows = _rms_bwd(h_ref[...], g_ref[...], acc_ref[...])
            dh = d2_ref[...] + dx
            dh_ref[...] = dh
            dhb_ref[...] = dh.astype(BF16)
            gn_ref[...] += jnp.sum(dg_rows, axis=0, keepdims=True)

    act = pl.BlockSpec((tm, tk), lambda i, kk: (i, kk))
    wsp = pl.BlockSpec((tk, D_MODEL), lambda i, kk: (kk, 0))
    row = pl.BlockSpec((tm, D_MODEL), lambda i, kk: (i, 0))
    vec = pl.BlockSpec((1, D_MODEL), lambda i, kk: (0, 0))
    return _call(
        body, name="ffn_bwd_in", grid=(t // tm, nk),
        in_specs=[act, act, wsp, wsp, row, row, vec],
        out_specs=[row, row, vec],
        out_shape=[jax.ShapeDtypeStruct((t, D_MODEL), F32), jax.ShapeDtypeStruct((t, D_MODEL), BF16),
                   jax.ShapeDtypeStruct((1, D_MODEL), F32)],
        scratch=[pltpu.VMEM((tm, D_MODEL), F32)],
        semantics=("arbitrary", "arbitrary"), args=(dg, du, wg_t, wu_t, h1, dh2, g2), exchange=exchange)


def _mix_bwd(dh1b, w_out, proj, b_gate, y_ret, y_att, w_ret, w_att_t):
    t = dh1b.shape[0]
    tm = min(256, t)

    def body(d_ref, wo_ref, gl0, gl1, gl2, gl3, b_ref, yr_ref, ya_ref, wr_ref, wa_ref,
             dyr_ref, dya_ref, dglr_ref, dgla_ref, dgro_ref, dao_ref, db_ref):
        @pl.when(pl.program_id(0) == 0)
        def _():
            db_ref[...] = jnp.zeros_like(db_ref)

        dz = _dot_nt(d_ref[...], wo_ref[...])
        gr, ga = _gates((gl0, gl1, gl2, gl3), b_ref)
        dyr = (dz * gr).astype(BF16)
        dya = (dz * ga).astype(BF16)
        dyr_ref[...] = dyr
        dya_ref[...] = dya
        dglr = dz * yr_ref[...].astype(F32) * gr * (1.0 - gr)
        dgla = dz * ya_ref[...].astype(F32) * ga * (1.0 - ga)
        dglr_ref[...] = dglr.astype(BF16)
        dgla_ref[...] = dgla.astype(BF16)
        db_ref[:, :D_MODEL] += jnp.sum(dglr, axis=0, keepdims=True)
        db_ref[:, D_MODEL:] += jnp.sum(dgla, axis=0, keepdims=True)
        dgro_ref[...] = _dot_nt(dyr, wr_ref[...]).astype(BF16)
        dao_ref[...] = _dot(dya, wa_ref[...]).astype(BF16)

    row = pl.BlockSpec((tm, D_MODEL), lambda i: (i, 0))
    half = pl.BlockSpec((tm, 512), lambda i: (i, 0))
    return pl.pallas_call(
        body, name="mix_bwd", grid=(t // tm,),
        in_specs=[row, _whole(w_out), *_gl_specs(tm), _whole(b_gate), row, row, _whole(w_ret), _whole(w_att_t)],
        out_specs=[row, row, row, row, row, half, _whole(b_gate)],
        out_shape=[jax.ShapeDtypeStruct((t, D_MODEL), BF16)] * 5
                  + [jax.ShapeDtypeStruct((t, 512), BF16), jax.ShapeDtypeStruct(b_gate.shape, F32)],
        compiler_params=_params("arbitrary"),
    )(dh1b, w_out, proj, proj, proj, proj, b_gate, y_ret, y_att, w_ret, w_att_t)


def _ret_bwd(dgro, proj, o_ret, qr, kr, cs, sn, lg_arr, batch, seq, exchange):
    t = batch * seq
    nt = seq // RET_TILE

    def body(dgro_ref, rg_ref, o_ref, qr_ref, kr_ref, v_ref, cs_ref, sn_ref, lg_ref,
             dq_ref, dk_ref, dv_ref, drg_ref, do_ref, st_ref):
        lg = lg_ref[:, 0:1]
        inside, q_dec, k_dec, tile_dec = _decay(lg)

        state = jnp.zeros((RET_KEY_DIM, RET_VAL_DIM), F32)
        for i in range(nt - 1):
            rows = slice(i * RET_TILE, (i + 1) * RET_TILE)
            state = state * tile_dec + _dot_tn(_scaled(kr_ref[rows, :], k_dec), v_ref[rows, :])
            st_ref[i + 1] = state.astype(BF16)

        for i in range(nt):
            rows = slice(i * RET_TILE, (i + 1) * RET_TILE)
            o = o_ref[rows, :]
            xc = o - jnp.mean(o, axis=-1, keepdims=True)
            rs = lax.rsqrt(jnp.mean(xc * xc, axis=-1, keepdims=True) + EPS)
            nrm = xc * rs
            rg = rg_ref[rows, :].astype(F32)
            sg = _sigmoid(rg)
            dg = dgro_ref[rows, :].astype(F32)
            drg_ref[rows, :] = (dg * nrm * sg * (1.0 + rg * (1.0 - sg))).astype(BF16)
            dn = dg * rg * sg
            do = rs * (dn - jnp.mean(dn, axis=-1, keepdims=True)
                       - nrm * jnp.mean(dn * nrm, axis=-1, keepdims=True))
            do_ref[rows, :] = do.astype(BF16)

        dstate = jnp.zeros((RET_KEY_DIM, RET_VAL_DIM), F32)
        for i in reversed(range(nt)):
            rows = slice(i * RET_TILE, (i + 1) * RET_TILE)
            qi, ki, vi, doi = qr_ref[rows, :], kr_ref[rows, :], v_ref[rows, :], do_ref[rows, :]
            p = (_dot_nt(qi, ki) * inside).astype(BF16)
            dp = (_dot_nt(doi, vi) * inside).astype(BF16)
            dq = _dot(dp, ki)
            dk = _dot_tn(dp, qi)
            dv = _dot_tn(p, doi)
            if i > 0:
                dq = dq + _dot_nt(doi, st_ref[i]) * q_dec
            if i < nt - 1:
                dsb = dstate.astype(BF16)
                dk = dk + _dot_nt(vi, dsb) * k_dec
                dv = dv + _dot(_scaled(ki, k_dec), dsb)
            if i > 0:
                dstate = dstate * tile_dec + _dot_tn(_scaled(qi, q_dec), doi)
            dq_ref[rows, :] = (dq * cs_ref[rows, :] - pltpu.roll(dq, 64, 1) * sn_ref[rows, :]).astype(BF16)
            dk = (dk * cs_ref[rows, :] - pltpu.roll(dk, 64, 1) * sn_ref[rows, :]) * (RET_KEY_DIM ** -0.5)
            dk_ref[rows, :] = dk.astype(BF16)
            dv_ref[rows, :] = dv.astype(BF16)

    key = pl.BlockSpec((seq, RET_KEY_DIM), lambda b, h: (b, h))
    val = pl.BlockSpec((seq, RET_VAL_DIM), lambda b, h: (b, h))
    tab = pl.BlockSpec((seq, RET_KEY_DIM), lambda b, h: (0, 0))
    return _call(
        body, name="ret_bwd", grid=(batch, RET_HEADS),
        in_specs=[val, pl.BlockSpec((seq, RET_VAL_DIM), lambda b, h: (b, C_RG // RET_VAL_DIM + h)), val, key, key,
                  pl.BlockSpec((seq, RET_VAL_DIM), lambda b, h: (b, C_RV // RET_VAL_DIM + h)), tab, tab,
                  pl.BlockSpec((None, 1, LANES), lambda b, h: (h, 0, 0))],
        out_specs=[key, key, val, val],
        out_shape=[jax.ShapeDtypeStruct((t, RET_HEADS * RET_KEY_DIM), BF16)] * 2
                  + [jax.ShapeDtypeStruct((t, RET_HEADS * RET_VAL_DIM), BF16)] * 2,
        scratch=[pltpu.VMEM((seq, RET_VAL_DIM), BF16), pltpu.VMEM((nt, RET_KEY_DIM, RET_VAL_DIM), BF16)],
        semantics=("parallel", "parallel"), args=(dgro, proj, o_ret, qr, kr, proj, cs, sn, lg_arr),
        exchange=exchange)


def _att_bwd(proj, bias, dao, batch, seq, exchange):
    ni, q_spec, k_spec, v_spec, w_spec, b_spec, pad = _att_specs(batch, seq)
    t = batch * seq

    def body(q_ref, k_ref, v_ref, bias_ref, do_ref, dq_ref, dk_ref, dv_ref, dw_ref,
             dbias_ref, dk_acc, dv_acc, kp_ref, vp_ref, s_ref, dp_ref, e_ref, ds_ref):
        b, i = pl.program_id(1), pl.program_id(2)

        @pl.when((b == 0) & (i == 0))
        def _():
            dbias_ref[...] = jnp.zeros_like(dbias_ref)

        @pl.when(i == 0)
        def _():
            _att_pad(k_ref, kp_ref)
            _att_pad(v_ref, vp_ref)
            dk_acc[...] = jnp.zeros_like(dk_acc)
            dv_acc[...] = jnp.zeros_like(dv_acc)

        win = pl.ds(pl.multiple_of(i * ATT_Q, ATT_Q), ATT_WIN)
        lo = lax.broadcasted_iota(jnp.int32, (1, LANES), 1) < 64
        start = jnp.minimum(i, ATT_STARTS)
        for pair in range(ATT_PAIRS):
            cols = slice(pair * LANES, (pair + 1) * LANES)
            k2, v2, q2, do2 = kp_ref[win, cols], vp_ref[win, cols], q_ref[:, cols], do_ref[:, cols]
            dq = jnp.zeros((ATT_Q, LANES), F32)
            dk = jnp.zeros((LANES, ATT_WIN), F32)
            dv = jnp.zeros((LANES, ATT_WIN), F32)
            for e in range(2):
                h = 2 * pair + e
                sel = lo if e == 0 else jnp.logical_not(lo)
                qm = _att_head(q2, sel)
                dom = jnp.where(sel, do2, jnp.zeros_like(do2))
                s_ref[h] = _dot_nt(qm, k2)
                dp_ref[h] = _dot_nt(dom, v2)
                rsum = []
                for c in range(ATT_Q // ATT_ROWS):
                    rows = slice(c * ATT_ROWS, (c + 1) * ATT_ROWS)
                    ex, r = _att_softmax_rows(s_ref.at[h], bias_ref.at[pair, start, e], rows)
                    dp = dp_ref[h, rows, :]
                    mean = jnp.sum(dp * ex, axis=-1, keepdims=True) * r
                    ds = ex * ((dp - mean) * r)
                    dbias_ref[h, rows, :] += ds
                    ds_ref[h, rows, :] = ds.astype(BF16)
                    e_ref[h, rows, :] = ex.astype(BF16)
                    rsum.append(r)
                dq = dq + _dot(ds_ref[h], jnp.where(sel, k2, jnp.zeros_like(k2)))
                dk = dk + _dot_tn(qm, ds_ref[h])
                dv = dv + _dot_tn((dom.astype(F32) * jnp.concatenate(rsum, axis=0)).astype(BF16), e_ref[h])
            dq_ref[:, cols] = (dq * 0.125).astype(BF16)
            dk_acc[cols, win] += dk
            dv_acc[cols, win] += dv

        @pl.when(i == ni - 1)
        def _():
            dk_ref[...] = dk_acc[:, ATT_PAD:].T.astype(BF16)
            dv_ref[...] = dv_acc[:, ATT_PAD:].T.astype(BF16)

        @pl.when((b == batch - 1) & (i == ni - 1))
        def _():
            n_i = lax.broadcasted_iota(jnp.int32, (ATT_Q, BIAS_LEN), 0)
            for h in range(heads):
                xw = jnp.concatenate([jnp.zeros((ATT_Q, BIAS_LEN - ATT_WIN), F32), dbias_ref[h]], axis=1)
                for bit in range(8):
                    xw = jnp.where(((n_i >> bit) & 1) == 1, pltpu.roll(xw, BIAS_LEN - (1 << bit), 1), xw)
                dw_ref[h // 2, h % 2:h % 2 + 1, :] = jnp.sum(xw, axis=0, keepdims=True)

    heads = 2 * ATT_PAIRS
    seq_blk = pl.BlockSpec((seq, ATT_COLS), lambda g, b, i: (b, g))
    q_out = pl.BlockSpec((ATT_Q, ATT_COLS), lambda g, b, i: (b * ni + i, g))
    tile_f32, tile_bf16 = pltpu.VMEM((heads, ATT_Q, ATT_WIN), F32), pltpu.VMEM((heads, ATT_Q, ATT_WIN), BF16)
    acc = pltpu.VMEM((ATT_COLS, seq + ATT_PAD), F32)
    return _call(
        body, name="att_bwd", grid=(ATT_HEADS // heads, batch, ni),
        in_specs=[q_spec, k_spec, v_spec, b_spec, q_out],
        out_specs=[q_out, seq_blk, seq_blk, w_spec],
        out_shape=[jax.ShapeDtypeStruct((t, 512), BF16)] * 3
                  + [jax.ShapeDtypeStruct((ATT_HEADS // 2, 2, BIAS_LEN), F32)],
        scratch=[tile_f32, acc, acc, pad, pad, tile_f32, tile_f32, tile_bf16, tile_bf16],
        semantics=("arbitrary", "arbitrary", "arbitrary"), args=(proj, proj, proj, bias, dao), exchange=exchange)


def _rms_in_bwd(x2, dxn, dh1, g1):
    t = x2.shape[0]
    tm = min(512, t)

    def body(x_ref, d_ref, h_ref, g_ref, dx_ref, dg_ref):
        @pl.when(pl.program_id(0) == 0)
        def _():
            dg_ref[...] = jnp.zeros_like(dg_ref)

        dx, dg_rows = _rms_bwd(x_ref[...], g_ref[...], d_ref[...])
        dx_ref[...] = h_ref[...] + dx
        dg_ref[...] += jnp.sum(dg_rows, axis=0, keepdims=True)

    row = pl.BlockSpec((tm, D_MODEL), lambda i: (i, 0))
    vec = pl.BlockSpec((1, D_MODEL), lambda i: (0, 0))
    return pl.pallas_call(
        body, name="rms_in_bwd", grid=(t // tm,),
        in_specs=[row, row, row, vec], out_specs=[row, vec],
        out_shape=[jax.ShapeDtypeStruct((t, D_MODEL), F32), jax.ShapeDtypeStruct((1, D_MODEL), F32)],
        compiler_params=_params("arbitrary"),
    )(x2, dxn, dh1, g1)


def _pack_small(dg1, dbr, dba, dg2, dg3, dw, loss):
    def body(a_ref, b_ref, c_ref, d_ref, e_ref, w_ref, l_ref, o_ref):
        o_ref[...] = jnp.zeros_like(o_ref)
        for r, ref in enumerate((a_ref, b_ref, c_ref, d_ref, e_ref)):
            o_ref[r:r + 1, :] = ref[...]
        o_ref[5:6, 0:LANES] = l_ref[...]
        for hp in range(ATT_HEADS // 2):
            o_ref[8 + 2 * hp:10 + 2 * hp, :] = w_ref[hp]

    return pl.pallas_call(body, name="pack_small",
                          out_shape=jax.ShapeDtypeStruct((16, D_MODEL), F32))(dg1, dbr, dba, dg2, dg3, dw, loss)


def _rotary_tables(seq):
    freqs = ROPE_BASE ** (-jnp.arange(0, RET_KEY_DIM, 2, dtype=F32) / RET_KEY_DIM)
    ang = jnp.arange(seq, dtype=F32)[:, None] * freqs[None, :]
    cos, sin = jnp.cos(ang), jnp.sin(ang)
    return jnp.concatenate([cos, cos], axis=1), jnp.concatenate([-sin, sin], axis=1)


def _bias_rows(rel_bias):
    n_far = BIAS_LEN - ATT_Q - MAX_REL + 1
    n_near = BIAS_LEN - n_far - (N_REL - 2)
    w = jnp.concatenate([jnp.broadcast_to(rel_bias[:, N_REL - 1:], (ATT_HEADS, n_far)),
                         rel_bias[:, 1:N_REL - 1][:, ::-1],
                         jnp.broadcast_to(rel_bias[:, :1], (ATT_HEADS, n_near))], axis=1)
    return w.reshape(ATT_HEADS // 2, 2, BIAS_LEN)


def _bias_rows_bwd(dw):
    n_far = BIAS_LEN - ATT_Q - MAX_REL + 1
    mid = dw[:, n_far:n_far + N_REL - 2][:, ::-1]
    return jnp.concatenate([jnp.sum(dw[:, n_far + N_REL - 2:], axis=1, keepdims=True), mid,
                            jnp.sum(dw[:, :n_far], axis=1, keepdims=True)], axis=1)


def _step(x, tgt, norm_mix, b_gate, norm_ffn, norm_final, rel_bias_shard, shard):
    batch, seq, _ = x.shape
    t = batch * seq
    n_rb = rel_bias_shard.shape[-1]
    x2, tgt2 = x.reshape(t, D_MODEL), tgt.reshape(t, D_MODEL)
    g3 = norm_final.reshape(1, D_MODEL)
    cs, sn = _rotary_tables(seq)
    lg = np.log(1.0 - 2.0 ** (-5.0 - np.arange(RET_HEADS, dtype=np.float32))).astype(np.float32)
    lg_arr = jnp.asarray(np.broadcast_to(lg[:, None, None], (RET_HEADS, 1, LANES)))

    def gather(*names):
        return _ChipGather([shard[nm] for nm in names])

    def scatter(*grads):
        return _Exchange(grads, scatter=True)

    rb_pad = jnp.pad(rel_bias_shard, ((0, 0), (0, LANES - n_rb)))
    (xn,), (w_in_half, rb_full) = _rms_fwd(x2, norm_mix,
                                           _ChipGather([shard["w_in_t"], rb_pad], parts=[(0, 2), (0, 1)]))
    rb_full = rb_full.reshape(N_DEV, ATT_HEADS, LANES)[:, :, :n_rb]
    bias, (w_in_t,) = _att_bias_tiles(
        _bias_rows(jnp.transpose(rb_full, (1, 0, 2)).reshape(ATT_HEADS, N_DEV * n_rb)),
        _ChipGather([shard["w_in_t"]], parts=[(1, 2)], into=[w_in_half]))
    proj, (w_ret, w_att_t, w_out, w_gate_t) = _mm(
        xn, w_in_t, tb=True, out_dtype=BF16, tm=1024, tn=1664, tk=1024, name="proj",
        exchange=gather("w_ret", "w_att_t", "w_out", "w_gate_t"))
    (gro, o_ret, qr, kr), _ = _ret_fwd(proj, cs, sn, lg_arr, batch, seq, None)
    (ao,), (w_up_t, w_down) = _att_fwd(proj, bias, batch, seq, gather("w_up_t", "w_down"))
    z, y_ret, y_att, h1, hn = _mix_out_fwd(gro, ao, proj, b_gate, w_ret, w_att_t, x2, w_out, norm_ffn)
    g_act, u_act, a_act = _ffn_up(hn, w_gate_t, w_up_t)
    dh2, dh2b, loss, dg3 = _ffn_down_loss(a_act, h1, tgt2, w_down, g3)

    wg = dict(out_dtype=BF16, tn=1024, ta=True)
    slots = {}
    dw_down = _mm(a_act, dh2b, tm=1408, tk=1024, name="dw_down", **wg)
    (d_gact, d_uact), _ = _ffn_bwd_act(dh2b, w_down, g_act, u_act, None)
    dw_gate = _mm(d_gact, hn, tm=1408, tk=1024, name="dw_gate", **wg)
    dw_up = _mm(d_uact, hn, tm=1408, tk=1024, name="dw_up", **wg)
    (dh1, dh1b, dg2), (slots["w_down"],) = _ffn_bwd_in(d_gact, d_uact, w_gate_t, w_up_t, h1, dh2, norm_ffn,
                                                     scatter(dw_down))
    dw_out = _mm(z, dh1b, tm=1024, tk=2048, name="dw_out", **wg)
    dyr, dya, dglr, dgla, dgro, dao, db = _mix_bwd(dh1b, w_out, proj, b_gate, y_ret, y_att, w_ret, w_att_t)
    dw_ret = _mm(gro, dyr, tm=1024, tk=2048, name="dw_ret", **wg)
    dw_att = _mm(dya, ao, tm=1024, tk=2048, name="dw_att", **wg)
    (drq, drk, drv, drg), _ = _ret_bwd(dgro, proj, o_ret, qr, kr, cs, sn, lg_arr, batch, seq, None)
    (daq, dak, dav, dw), (slots["w_gate_t"], slots["w_out"], slots["w_ret"], slots["w_att_t"]) = _att_bwd(
        proj, bias, dao, batch, seq, scatter(dw_gate, dw_out, dw_ret, dw_att))
    dproj = [drq, drk, drv, drg, daq, dak, dav, dglr, dgla]
    dw_in, (slots["w_up_t"],) = _mm_pieces(dproj, xn, ta=True, out_dtype=BF16, tm=512, tn=1024, tk=1024,
                                           name="dw_in", exchange=scatter(dw_up))
    (dw_in_sibling,) = _alone(_PairSwap([dw_in]), "swap_w_in")
    dw_in_pairs = _pair_add(dw_in, dw_in_sibling, "pair_w_in")
    dxn, (slots["w_in_t"],) = _mm_pieces(dproj, w_in_t, ta=False, out_dtype=F32, tm=1024, tn=1024, tk=512, name="dxn",
                                  exchange=_ChipScatter([dw_in_pairs]))
    dx, dg1 = _rms_in_bwd(x2, dxn, dh1, norm_mix)
    small = _pack_small(dg1, db[:, :D_MODEL], db[:, D_MODEL:], dg2, dg3, dw, loss)
    (small_slots,) = _alone(_ChipGather([small]), "gather_small")
    return dx.reshape(batch, seq, D_MODEL), slots, small_slots.reshape(N_DEV, 16, D_MODEL)


def _row_tile(r, c):
    return max(d for d in range(16, r + 1, 16) if r % d == 0 and (d * c <= 256 * 1024 or d == 16))


def _pair_add(grad, got, name):
    _, r, c = got.shape
    tr = r
    core = lax.axis_index("c").astype(jnp.int32).reshape(1)

    def body(core_ref, g_ref, a_ref, o_ref):
        o_ref[...] = (g_ref[...].astype(F32) + a_ref[...].astype(F32)).astype(o_ref.dtype)

    blk = pl.BlockSpec((None, tr, c), lambda q, i, core_ref: (q, i, 0))
    return pl.pallas_call(
        body, name=name,
        grid_spec=pltpu.PrefetchScalarGridSpec(
            num_scalar_prefetch=1, grid=(4, r // tr),
            in_specs=[pl.BlockSpec((None, None, tr, c), lambda q, i, core_ref: (q, core_ref[0], i, 0)), blk],
            out_specs=blk),
        out_shape=jax.ShapeDtypeStruct(got.shape, got.dtype),
        compiler_params=_params("parallel", "parallel"),
    )(core, grad.reshape(4, 2, r, c), got)


def _sum_slots(slots, name):
    n, r, c = slots.shape
    tr = _row_tile(r, c)

    def body(s_ref, o_ref):
        acc = s_ref[0].astype(F32)
        for s in range(1, n):
            acc = acc + s_ref[s].astype(F32)
        o_ref[...] = acc

    return pl.pallas_call(
        body, name=name, grid=(r // tr,),
        in_specs=[pl.BlockSpec((n, tr, c), lambda i: (0, i, 0))],
        out_specs=pl.BlockSpec((tr, c), lambda i: (i, 0)),
        out_shape=jax.ShapeDtypeStruct((r, c), F32),
        compiler_params=_params("parallel"),
    )(slots)


def _adamw_math(w, g, m, v):
    m = ADAM_B1 * m + (1.0 - ADAM_B1) * g
    v = ADAM_B2 * v + (1.0 - ADAM_B2) * (g * g)
    m_hat = m / (1.0 - ADAM_B1 ** ADAM_STEP)
    v_hat = v / (1.0 - ADAM_B2 ** ADAM_STEP)
    return -ADAM_LR * (m_hat / (jnp.sqrt(v_hat) + ADAM_EPS) + ADAM_WD * w), m, v


def _adamw(w, slots, m, v, name):
    n, r, c = slots.shape
    tr = _row_tile(r, c)

    def body(w_ref, s_ref, m_ref, v_ref, g_ref, d_ref, nm_ref, nv_ref):
        g = s_ref[0].astype(F32)
        for s in range(1, n):
            g = g + s_ref[s].astype(F32)
        g_ref[...] = g
        d_ref[...], nm_ref[...], nv_ref[...] = _adamw_math(w_ref[...], g, m_ref[...], v_ref[...])

    blk = pl.BlockSpec((tr, c), lambda i: (i, 0))
    return pl.pallas_call(
        body, name=name, grid=(r // tr,),
        in_specs=[blk, pl.BlockSpec((n, tr, c), lambda i: (0, i, 0)), blk, blk], out_specs=[blk] * 4,
        out_shape=[jax.ShapeDtypeStruct((r, c), F32)] * 4,
        compiler_params=_params("parallel"),
    )(w, slots, m, v)


def _adamw_small(ws, gs, ms, vs):
    n = len(ws)

    def body(*refs):
        for i in range(n):
            w_ref, g_ref, m_ref, v_ref = (refs[j * n + i] for j in range(4))
            d_ref, nm_ref, nv_ref = (refs[(4 + j) * n + i] for j in range(3))
            d_ref[...], nm_ref[...], nv_ref[...] = _adamw_math(w_ref[...], g_ref[...], m_ref[...], v_ref[...])

    shapes = [jax.ShapeDtypeStruct(w.shape, F32) for w in ws]
    outs = pl.pallas_call(body, name="adamw_small", out_shape=shapes * 3)(*ws, *gs, *ms, *vs)
    return outs[:n], outs[n:2 * n], outs[2 * n:]


def kernel(x, norm_mix, w_in, b_gate, rel_bias, w_ret_out, w_att_out, w_out, norm_ffn, w_ffn_gate, w_ffn_up, w_ffn_down, norm_final, loss_target, m_norm_mix, m_w_in, m_b_gate, m_rel_bias, m_w_ret_out, m_w_att_out, m_w_out, m_norm_ffn, m_w_ffn_gate, m_w_ffn_up, m_w_ffn_down, m_norm_final, v_norm_mix, v_w_in, v_b_gate, v_rel_bias, v_w_ret_out, v_w_att_out, v_w_out, v_norm_ffn, v_w_ffn_gate, v_w_ffn_up, v_w_ffn_down, v_norm_final):
    me = _index(_place())
    n_rb = rel_bias.shape[-1]

    shard = dict(w_in_t=w_in[0].T, w_gate_t=w_ffn_gate[0].T, w_up_t=w_ffn_up[0].T, w_down=w_ffn_down[0],
                 w_ret=w_ret_out[0], w_out=w_out[0], w_att_t=w_att_out[0].T)
    shard = {nm: s.astype(BF16) for nm, s in shard.items()}
    dx, slots, small_slots = _step(x, loss_target, norm_mix, b_gate, norm_ffn, norm_final, rel_bias[0], shard)
    small_sum = _sum_slots(small_slots, "sum_small")
    loss = small_sum[5, 0]

    transposed = dict(w_in="w_in_t", w_ffn_gate="w_gate_t", w_ffn_up="w_up_t", w_att_out="w_att_t")
    plain = dict(w_ffn_down="w_down", w_ret_out="w_ret", w_out="w_out")
    g = dict(
        norm_mix=small_sum[0:1], b_gate=jnp.concatenate([small_sum[1:2], small_sum[2:3]], axis=1),
        norm_ffn=small_sum[3:4], norm_final=small_sum[4:5],
        rel_bias=lax.dynamic_slice_in_dim(_bias_rows_bwd(small_sum[8:16]), me * n_rb, n_rb, axis=1),
    )
    w = dict(norm_mix=norm_mix, w_in=w_in, b_gate=b_gate, rel_bias=rel_bias, w_ret_out=w_ret_out, w_att_out=w_att_out,
             w_out=w_out, norm_ffn=norm_ffn, w_ffn_gate=w_ffn_gate, w_ffn_up=w_ffn_up, w_ffn_down=w_ffn_down,
             norm_final=norm_final)
    m = dict(norm_mix=m_norm_mix, w_in=m_w_in, b_gate=m_b_gate, rel_bias=m_rel_bias, w_ret_out=m_w_ret_out,
             w_att_out=m_w_att_out, w_out=m_w_out, norm_ffn=m_norm_ffn, w_ffn_gate=m_w_ffn_gate, w_ffn_up=m_w_ffn_up,
             w_ffn_down=m_w_ffn_down, norm_final=m_norm_final)
    v = dict(norm_mix=v_norm_mix, w_in=v_w_in, b_gate=v_b_gate, rel_bias=v_rel_bias, w_ret_out=v_w_ret_out,
             w_att_out=v_w_att_out, w_out=v_w_out, norm_ffn=v_norm_ffn, w_ffn_gate=v_w_ffn_gate, w_ffn_up=v_w_ffn_up,
             w_ffn_down=v_w_ffn_down, norm_final=v_norm_final)
    order = ("norm_mix", "w_in", "b_gate", "rel_bias", "w_ret_out", "w_att_out", "w_out", "norm_ffn",
             "w_ffn_gate", "w_ffn_up", "w_ffn_down", "norm_final")
    small_names = ("norm_mix", "b_gate", "rel_bias", "norm_ffn", "norm_final")

    def flat(a):
        return a[0] if a.ndim == 3 else a.reshape(-1, a.shape[-1])

    grad, delta, new_m, new_v = {}, {}, {}, {}
    for nm in order:
        if nm in transposed:
            res = _adamw(w[nm][0].T, slots[transposed[nm]], m[nm][0].T, v[nm][0].T, "adamw_" + nm)
            grad[nm], delta[nm], new_m[nm], new_v[nm] = (a.T[None] for a in res)
        elif nm in plain:
            res = _adamw(flat(w[nm]), slots[plain[nm]], flat(m[nm]), flat(v[nm]), "adamw_" + nm)
            grad[nm], delta[nm], new_m[nm], new_v[nm] = (a.reshape(w[nm].shape) for a in res)
    ds, nms, nvs = _adamw_small([flat(w[nm]) for nm in small_names], [g[nm] for nm in small_names],
                                [flat(m[nm]) for nm in small_names], [flat(v[nm]) for nm in small_names])
    for i, nm in enumerate(small_names):
        grad[nm], delta[nm], new_m[nm], new_v[nm] = (a.reshape(w[nm].shape) for a in (g[nm], ds[i], nms[i], nvs[i]))

    return (loss, dx, *[grad[nm] for nm in order], *[delta[nm] for nm in order],
            *[new_m[nm] for nm in order], *[new_v[nm] for nm in order])
```

```python
import numpy as np
import jax
import jax.numpy as jnp
from jax import lax
from jax.experimental import pallas as pl
from jax.experimental.pallas import tpu as pltpu

F32 = jnp.float32
BF16 = jnp.bfloat16
MESH = pl.DeviceIdType.MESH

D_MODEL = 1024
CHUNK = 64
RET_HEADS = 4
RET_KEY_DIM = 128
RET_VAL_DIM = 256
ATT_HEADS = 8
BAND_CHUNKS = 8
MAX_REL = 256
N_REL = CHUNK + MAX_REL
D_FF = 2816
N_IN = 6656
ROPE_BASE = 10000.0
EPS = 1e-6
NEG_INF = -1e30
C_RQ, C_RK, C_RV, C_RG, C_AQ, C_AK, C_AV, C_GL = 0, 512, 1024, 2048, 3072, 3584, 4096, 4608

ADAM_LR = 0.001
ADAM_B1 = 0.9
ADAM_B2 = 0.999
ADAM_EPS = 1e-08
ADAM_WD = 0.01
ADAM_STEP = 10

N_DEV = 8
LANES = 128
RET_TILE = 256
ATT_Q = 256
ATT_PAD = BAND_CHUNKS * CHUNK
ATT_WIN = ATT_PAD + ATT_Q
ATT_STARTS = ATT_PAD // ATT_Q
ATT_ROWS = 32
BIAS_LEN = 1024
VMEM_LIMIT = 48 * 1024 * 1024
VMEM_LIMIT_ATT_BWD = 56 * 1024 * 1024


def _params(*sem, vmem=VMEM_LIMIT):
    return pltpu.CompilerParams(dimension_semantics=sem, vmem_limit_bytes=vmem)


def _dot(a, b):
    return lax.dot_general(a, b, (((1,), (0,)), ((), ())), preferred_element_type=F32)


def _dot_nt(a, b):
    return lax.dot_general(a, b, (((1,), (1,)), ((), ())), preferred_element_type=F32)


def _dot_tn(a, b):
    return lax.dot_general(a, b, (((0,), (0,)), ((), ())), preferred_element_type=F32)


def _sigmoid(x):
    return 1.0 / (1.0 + jnp.exp(-x))


def _rms_bwd(x, g, dy):
    r = lax.rsqrt(jnp.mean(x * x, axis=-1, keepdims=True) + EPS)
    u = dy * g
    dx = r * u - x * (r * r * r) * jnp.mean(u * x, axis=-1, keepdims=True)
    return dx, dy * x * r


def _place():
    return lax.axis_index("x"), lax.axis_index("y"), lax.axis_index("c")


def _peer(k):
    x, y, c = _place()
    return ((1 - x) if k & 4 else x, (1 - y) if k & 2 else y, (1 - c) if k & 1 else c)


def _index(place):
    return 4 * place[0] + 2 * place[1] + place[2]


def _rows(ref, block, nrows):
    align = 16 if ref.dtype == BF16 else 8
    return ref.at[pl.ds(pl.multiple_of(block * nrows, align), nrows)]


class _Exchange:
    def __init__(self, arrays, scatter):
        self.arrays, self.scatter, self.n = list(arrays), scatter, len(arrays)

    def out_shape(self):
        if self.scatter:
            return [jax.ShapeDtypeStruct((N_DEV, a.shape[0] // N_DEV) + a.shape[1:], a.dtype) for a in self.arrays]
        return [jax.ShapeDtypeStruct((N_DEV * a.shape[0],) + a.shape[1:], a.dtype) for a in self.arrays]

    def scratch(self):
        return [pltpu.SemaphoreType.DMA((self.n, N_DEV - 1)), pltpu.SemaphoreType.DMA((self.n, N_DEV - 1)),
                pltpu.SemaphoreType.DMA((self.n,))]

    def _copies(self, ins, outs, sems):
        send_sems, recv_sems, local_sems = sems
        me = _index(_place())

        def src(w, to):
            return _rows(ins[w], to, ins[w].shape[0] // N_DEV) if self.scatter else ins[w]

        def dst(w, origin):
            return outs[w].at[origin] if self.scatter else _rows(outs[w], origin, ins[w].shape[0])

        def remote(w, k, to, origin):
            return pltpu.make_async_remote_copy(src_ref=src(w, to), dst_ref=dst(w, origin),
                                                send_sem=send_sems.at[w, k - 1], recv_sem=recv_sems.at[w, k - 1],
                                                device_id=_peer(k), device_id_type=MESH)

        pairs = [(w, k) for w in range(self.n) for k in range(1, N_DEV)]
        own = lambda: [pltpu.make_async_copy(src(w, me), dst(w, me), local_sems.at[w]) for w in range(self.n)]
        sent = lambda: [remote(w, k, _index(_peer(k)), me) for w, k in pairs]
        arriving = lambda: [remote(w, k, me, _index(_peer(k))) for w, k in pairs]
        return own, sent, arriving

    def start(self, ins, outs, sems):
        own, sent, _ = self._copies(ins, outs, sems)
        for cp in own() + sent():
            cp.start()

    def wait(self, ins, outs, sems):
        own, sent, arriving = self._copies(ins, outs, sems)
        for cp in arriving():
            cp.wait_recv()
        for cp in sent():
            cp.wait_send()
        for cp in own():
            cp.wait()


class _PairSwap:
    def __init__(self, arrays):
        self.arrays, self.n = list(arrays), len(arrays)

    def out_shape(self):
        return [jax.ShapeDtypeStruct((4, a.shape[0] // N_DEV) + a.shape[1:], a.dtype) for a in self.arrays]

    def scratch(self):
        return [pltpu.SemaphoreType.DMA((self.n, 4)), pltpu.SemaphoreType.DMA((self.n, 4))]

    def _copies(self, ins, outs, sems):
        send_sems, recv_sems = sems
        x, y, c = _place()
        return [pltpu.make_async_remote_copy(
            src_ref=_rows(ins[w], 2 * q + 1 - c, ins[w].shape[0] // N_DEV), dst_ref=outs[w].at[q],
            send_sem=send_sems.at[w, q], recv_sem=recv_sems.at[w, q],
            device_id=(x, y, 1 - c), device_id_type=MESH) for w in range(self.n) for q in range(4)]

    def start(self, ins, outs, sems):
        for cp in self._copies(ins, outs, sems):
            cp.start()

    def wait(self, ins, outs, sems):
        for cp in self._copies(ins, outs, sems):
            cp.wait()


class _ChipScatter:
    def __init__(self, arrays):
        self.arrays, self.n = list(arrays), len(arrays)

    def out_shape(self):
        return [jax.ShapeDtypeStruct(a.shape, a.dtype) for a in self.arrays]

    def scratch(self):
        return [pltpu.SemaphoreType.DMA((self.n, 3)), pltpu.SemaphoreType.DMA((self.n, 3)),
                pltpu.SemaphoreType.DMA((self.n,))]

    def _copies(self, ins, outs, sems):
        send_sems, recv_sems, local_sems = sems
        x, y, c = _place()
        mine = 2 * x + y
        sent, arriving = [], []
        for w in range(self.n):
            for k in range(1, 4):
                tx, ty = (1 - x) if k & 2 else x, (1 - y) if k & 1 else y
                other = 2 * tx + ty
                sent.append(lambda w=w, k=k, tx=tx, ty=ty, other=other: pltpu.make_async_remote_copy(
                    src_ref=ins[w].at[other], dst_ref=outs[w].at[mine], send_sem=send_sems.at[w, k - 1],
                    recv_sem=recv_sems.at[w, k - 1], device_id=(tx, ty, c), device_id_type=MESH))
                arriving.append(lambda w=w, k=k, tx=tx, ty=ty, other=other: pltpu.make_async_remote_copy(
                    src_ref=ins[w].at[mine], dst_ref=outs[w].at[other], send_sem=send_sems.at[w, k - 1],
                    recv_sem=recv_sems.at[w, k - 1], device_id=(tx, ty, c), device_id_type=MESH))
        own = [lambda w=w: pltpu.make_async_copy(ins[w].at[mine], outs[w].at[mine], local_sems.at[w])
               for w in range(self.n)]
        return own, sent, arriving

    def start(self, ins, outs, sems):
        own, sent, _ = self._copies(ins, outs, sems)
        for cp in own + sent:
            cp().start()

    def wait(self, ins, outs, sems):
        own, sent, arriving = self._copies(ins, outs, sems)
        for cp in arriving:
            cp().wait_recv()
        for cp in sent:
            cp().wait_send()
        for cp in own:
            cp().wait()


class _ChipGather:
    def __init__(self, arrays, parts=None, into=None):
        self.arrays, self.n, self.into = list(arrays), len(arrays), into
        self.parts = parts or [(0, 1)] * self.n

    def out_shape(self):
        return [jax.ShapeDtypeStruct((N_DEV * a.shape[0],) + a.shape[1:], a.dtype) for a in self.arrays]

    def scratch(self):
        return [pltpu.SemaphoreType.DMA((self.n, N_DEV - 1)), pltpu.SemaphoreType.DMA((self.n, N_DEV - 1)),
                pltpu.SemaphoreType.DMA((self.n,))]

    def _parts(self, ins, outs, sems):
        send_sems, recv_sems, local_sems = sems
        x, y, c = _place()
        me, sibling = (x, y, c), (x, y, 1 - c)
        chips = [(1 - x, y), (x, 1 - y), (1 - x, 1 - y)]

        def rows(w, place, whole):
            (index, count), r = self.parts[w], ins[w].shape[0]
            lo, size = (0, r) if whole else (index * (r // count), r // count)
            align = 16 if ins[w].dtype == BF16 else 8
            return outs[w].at[pl.ds(pl.multiple_of(_index(place) * r + lo, align), size)]

        def mine(w, whole):
            (index, count), r = self.parts[w], ins[w].shape[0]
            return ins[w] if whole or count == 1 else ins[w].at[pl.ds(index * (r // count), r // count)]

        def copy(w, k, block, to, own=False):
            whole = k == 0
            return pltpu.make_async_remote_copy(src_ref=mine(w, whole) if own else rows(w, block, whole),
                                                dst_ref=rows(w, block, whole),
                                                send_sem=send_sems.at[w, k], recv_sem=recv_sems.at[w, k],
                                                device_id=to, device_id_type=MESH)

        def local(w):
            return pltpu.make_async_copy(ins[w], rows(w, me, True), local_sems.at[w])

        return me, sibling, chips, c, copy, local, [index == 0 for index, _ in self.parts]

    def start(self, ins, outs, sems):
        me, sibling, chips, c, copy, local, places_own = self._parts(ins, outs, sems)
        for w in range(self.n):
            if places_own[w]:
                local(w).start()
                copy(w, 0, me, sibling, own=True).start()
            for j, chip in enumerate(chips):
                copy(w, 1 + j, me, (*chip, c), own=True).start()

    def wait(self, ins, outs, sems):
        me, sibling, chips, c, copy, local, places_own = self._parts(ins, outs, sems)
        for w in range(self.n):
            for j, chip in enumerate(chips):
                copy(w, 1 + j, (*chip, c), me).wait_recv()
                copy(w, 4 + j, (*chip, c), sibling).start()
        for w in range(self.n):
            if places_own[w]:
                copy(w, 0, sibling, me).wait_recv()
                copy(w, 0, me, sibling, own=True).wait_send()
                local(w).wait()
            for j, chip in enumerate(chips):
                copy(w, 4 + j, (*chip, 1 - c), me).wait_recv()
                copy(w, 1 + j, me, (*chip, c), own=True).wait_send()
                copy(w, 4 + j, (*chip, c), sibling).wait_send()


def _call(body, *, name, grid, in_specs, out_specs, out_shape, scratch=(), semantics, args, exchange=None,
          vmem=VMEM_LIMIT):
    if exchange is None:
        return pl.pallas_call(body, name=name, grid=grid, in_specs=in_specs, out_specs=out_specs, out_shape=out_shape,
                              scratch_shapes=list(scratch),
                              compiler_params=_params(*semantics, vmem=vmem))(*args), None
    n_in, n_out, n_scr, nx = len(in_specs), len(out_specs), len(scratch), exchange.n
    into = list(getattr(exchange, "into", None) or [])

    def full_body(*refs):
        ins, refs = refs[:n_in], refs[n_in:]
        x_in, refs = refs[:nx], refs[nx + len(into):]
        outs, refs = refs[:n_out], refs[n_out:]
        x_out, refs = refs[:nx], refs[nx:]
        scr, sems = refs[:n_scr], refs[n_scr:]
        first, last = True, True
        for axis, size in enumerate(grid):
            first = jnp.logical_and(first, pl.program_id(axis) == 0)
            last = jnp.logical_and(last, pl.program_id(axis) == size - 1)
        if grid:
            pl.when(first)(lambda: exchange.start(x_in, x_out, sems))
        else:
            exchange.start(x_in, x_out, sems)
        body(*ins, *outs, *scr)
        if grid:
            pl.when(last)(lambda: exchange.wait(x_in, x_out, sems))
        else:
            exchange.wait(x_in, x_out, sems)

    hbm = pl.BlockSpec(memory_space=pltpu.HBM)
    res = pl.pallas_call(
        full_body, name=name, grid=grid,
        in_specs=list(in_specs) + [hbm] * (nx + len(into)), out_specs=list(out_specs) + [hbm] * nx,
        out_shape=list(out_shape) + exchange.out_shape(),
        scratch_shapes=list(scratch) + exchange.scratch(),
        input_output_aliases={n_in + nx + w: n_out + w for w in range(len(into))},
        compiler_params=_params(*(["arbitrary"] * len(grid)), vmem=vmem),
    )(*args, *exchange.arrays, *into)
    return res[:n_out], res[n_out:]


def _alone(exchange, name):
    return _call(lambda: None, name=name, grid=(), in_specs=[], out_specs=[], out_shape=[], semantics=(),
                 args=(), exchange=exchange)[1]


def _mm(a, b, *, ta=False, tb=False, out_dtype, tm, tn, tk, name, exchange=None):
    m, k = (a.shape[1], a.shape[0]) if ta else a.shape
    n = b.shape[0] if tb else b.shape[1]
    assert k == (b.shape[1] if tb else b.shape[0])
    tm, tn, tk = min(tm, m), min(tn, n), min(tk, k)
    assert m % tm == 0 and n % tn == 0 and k % tk == 0, (name, m, n, k)
    nk = k // tk
    dims = (((0 if ta else 1,), (1 if tb else 0,)), ((), ()))

    def body(a_ref, b_ref, o_ref, *acc):
        prod = lax.dot_general(a_ref[...].astype(BF16), b_ref[...].astype(BF16), dims, preferred_element_type=F32)
        if nk == 1:
            o_ref[...] = prod.astype(o_ref.dtype)
            return
        acc_ref, kk = acc[0], pl.program_id(2)

        @pl.when(kk == 0)
        def _():
            acc_ref[...] = prod

        @pl.when((kk > 0) & (kk < nk - 1))
        def _():
            acc_ref[...] += prod

        @pl.when(kk == nk - 1)
        def _():
            o_ref[...] = (acc_ref[...] + prod).astype(o_ref.dtype)

    a_spec = (pl.BlockSpec((tk, tm), lambda i, j, kk: (kk, i)) if ta
              else pl.BlockSpec((tm, tk), lambda i, j, kk: (i, kk)))
    b_spec = (pl.BlockSpec((tn, tk), lambda i, j, kk: (j, kk)) if tb
              else pl.BlockSpec((tk, tn), lambda i, j, kk: (kk, j)))
    (out,), moved = _call(
        body, name=name, grid=(m // tm, n // tn, nk),
        in_specs=[a_spec, b_spec],
        out_specs=[pl.BlockSpec((tm, tn), lambda i, j, kk: (i, j))],
        out_shape=[pltpu.HBM((m, n), out_dtype)],
        scratch=[pltpu.VMEM((tm, tn), F32)] if nk > 1 else [],
        semantics=("parallel", "parallel", "arbitrary"), args=(a, b), exchange=exchange)
    return out if exchange is None else (out, moved)


def _mm_pieces(pieces, b, *, ta, out_dtype, tm, tn, tk, name, exchange=None):
    rows, n = pieces[0].shape[0], b.shape[1]
    step = tm if ta else tk
    assert all(p.shape[0] == rows and p.shape[1] % step == 0 for p in pieces), name
    edges = [int(e) for e in np.cumsum([0] + [p.shape[1] // step for p in pieces])]
    total = edges[-1] * step
    m, k = (total, rows) if ta else (rows, total)
    assert b.shape[0] == k and m % tm == 0 and n % tn == 0 and k % tk == 0, name
    nk, npieces = k // tk, len(pieces)
    dims = (((0 if ta else 1,), (0,)), ((), ()))

    def body(*refs):
        a_refs, (b_ref, o_ref, acc_ref) = refs[:npieces], refs[npieces:]
        kk = pl.program_id(2)
        pos = pl.program_id(0) if ta else kk

        @pl.when(kk == 0)
        def _():
            acc_ref[...] = jnp.zeros_like(acc_ref)

        for p, a_ref in enumerate(a_refs):
            @pl.when((pos >= edges[p]) & (pos < edges[p + 1]))
            def _(a_ref=a_ref):
                acc_ref[...] += lax.dot_general(a_ref[...], b_ref[...], dims, preferred_element_type=F32)

        @pl.when(kk == nk - 1)
        def _():
            o_ref[...] = acc_ref[...].astype(o_ref.dtype)

    def a_spec(p):
        lo, last = edges[p], edges[p + 1] - edges[p] - 1
        if ta:
            def index(i, j, kk):
                inside = (i >= lo) & (i <= lo + last)
                return jnp.where(inside, kk, 0), jnp.clip(i - lo, 0, last)
            return pl.BlockSpec((tk, tm), index)
        return pl.BlockSpec((tm, tk), lambda i, j, kk: (i, jnp.clip(kk - lo, 0, last)))

    (out,), moved = _call(
        body, name=name, grid=(m // tm, n // tn, nk),
        in_specs=[a_spec(p) for p in range(npieces)] + [pl.BlockSpec((tk, tn), lambda i, j, kk: (kk, j))],
        out_specs=[pl.BlockSpec((tm, tn), lambda i, j, kk: (i, j))],
        out_shape=[pltpu.HBM((m, n), out_dtype)],
        scratch=[pltpu.VMEM((tm, tn), F32)],
        semantics=("parallel", "parallel", "arbitrary"), args=(*pieces, b), exchange=exchange)
    return out if exchange is None else (out, moved)


def _rms_fwd(x2, g, exchange):
    t = x2.shape[0]
    tm = min(512, t)

    def body(x_ref, g_ref, o_ref):
        x = x_ref[...]
        r = lax.rsqrt(jnp.mean(x * x, axis=-1, keepdims=True) + EPS)
        o_ref[...] = (x * r * g_ref[...]).astype(o_ref.dtype)

    return _call(
        body, name="rms_in_fwd", grid=(t // tm,),
        in_specs=[pl.BlockSpec((tm, D_MODEL), lambda i: (i, 0)), pl.BlockSpec((1, D_MODEL), lambda i: (0, 0))],
        out_specs=[pl.BlockSpec((tm, D_MODEL), lambda i: (i, 0))],
        out_shape=[jax.ShapeDtypeStruct((t, D_MODEL), BF16)],
        semantics=("parallel",), args=(x2, g), exchange=exchange)


def _decay(lg):
    row = lax.broadcasted_iota(jnp.int32, (RET_TILE, RET_TILE), 0)
    col = lax.broadcasted_iota(jnp.int32, (RET_TILE, RET_TILE), 1)
    within = jnp.exp(lg * jnp.abs(row - col).astype(F32))
    inside = jnp.where((col >> 6) <= (row >> 6), within, 0.0)
    pos = lax.broadcasted_iota(jnp.int32, (RET_TILE, 1), 0).astype(F32)
    q_dec = jnp.exp(lg * (pos + 1.0))
    k_dec = jnp.exp(lg * (RET_TILE - 1.0 - pos))
    tile_dec = jnp.exp(lg * float(RET_TILE))
    return inside, q_dec, k_dec, tile_dec


def _scaled(a_bf16, dec):
    return (a_bf16.astype(F32) * dec).astype(BF16)


def _ret_fwd(proj, cs, sn, lg_arr, batch, seq, exchange):
    t = batch * seq
    nt = seq // RET_TILE

    def body(q_ref, k_ref, v_ref, rg_ref, cs_ref, sn_ref, lg_ref, gro_ref, o_ref, qr_ref, kr_ref):
        lg = lg_ref[:, 0:1]
        cs_t, sn_t = cs_ref[...], sn_ref[...]
        q = q_ref[...].astype(F32)
        k = k_ref[...].astype(F32)
        qr_ref[...] = (q * cs_t + pltpu.roll(q, 64, 1) * sn_t).astype(BF16)
        kr_ref[...] = ((k * cs_t + pltpu.roll(k, 64, 1) * sn_t) * (RET_KEY_DIM ** -0.5)).astype(BF16)
        inside, q_dec, k_dec, tile_dec = _decay(lg)
        state = jnp.zeros((RET_KEY_DIM, RET_VAL_DIM), F32)
        for i in range(nt):
            rows = slice(i * RET_TILE, (i + 1) * RET_TILE)
            qi, ki, vi = qr_ref[rows, :], kr_ref[rows, :], v_ref[rows, :]
            acc = _dot((_dot_nt(qi, ki) * inside).astype(BF16), vi)
            if i > 0:
                acc = acc + _dot(_scaled(qi, q_dec), state.astype(BF16))
            if i < nt - 1:
                state = state * tile_dec + _dot_tn(_scaled(ki, k_dec), vi)
            o_ref[rows, :] = acc
            xc = acc - jnp.mean(acc, axis=-1, keepdims=True)
            nrm = xc * lax.rsqrt(jnp.mean(xc * xc, axis=-1, keepdims=True) + EPS)
            rg = rg_ref[rows, :].astype(F32)
            gro_ref[rows, :] = (rg * _sigmoid(rg) * nrm).astype(BF16)

    def col(base, width):
        return lambda b, h: (b, base // width + h)

    return _call(
        body, name="ret_fwd", grid=(batch, RET_HEADS),
        in_specs=[pl.BlockSpec((seq, RET_KEY_DIM), col(C_RQ, RET_KEY_DIM)),
                  pl.BlockSpec((seq, RET_KEY_DIM), col(C_RK, RET_KEY_DIM)),
                  pl.BlockSpec((seq, RET_VAL_DIM), col(C_RV, RET_VAL_DIM)),
                  pl.BlockSpec((seq, RET_VAL_DIM), col(C_RG, RET_VAL_DIM)),
                  pl.BlockSpec((seq, RET_KEY_DIM), lambda b, h: (0, 0)),
                  pl.BlockSpec((seq, RET_KEY_DIM), lambda b, h: (0, 0)),
                  pl.BlockSpec((None, 1, LANES), lambda b, h: (h, 0, 0))],
        out_specs=[pl.BlockSpec((seq, RET_VAL_DIM), lambda b, h: (b, h)),
                   pl.BlockSpec((seq, RET_VAL_DIM), lambda b, h: (b, h)),
                   pl.BlockSpec((seq, RET_KEY_DIM), lambda b, h: (b, h)),
                   pl.BlockSpec((seq, RET_KEY_DIM), lambda b, h: (b, h))],
        out_shape=[jax.ShapeDtypeStruct((t, RET_HEADS * RET_VAL_DIM), BF16),
                   jax.ShapeDtypeStruct((t, RET_HEADS * RET_VAL_DIM), F32),
                   jax.ShapeDtypeStruct((t, RET_HEADS * RET_KEY_DIM), BF16),
                   jax.ShapeDtypeStruct((t, RET_HEADS * RET_KEY_DIM), BF16)],
        semantics=("parallel", "parallel"), args=(proj, proj, proj, proj, cs, sn, lg_arr), exchange=exchange)


def _att_bias(w_ref, bias_ref):
    n_i = lax.broadcasted_iota(jnp.int32, (ATT_Q, BIAS_LEN), 0)
    qc = lax.broadcasted_iota(jnp.int32, (ATT_Q, ATT_WIN), 0) >> 6
    kc = lax.broadcasted_iota(jnp.int32, (ATT_Q, ATT_WIN), 1) >> 6
    dc = qc + BAND_CHUNKS - kc
    band = (dc >= 0) & (dc <= BAND_CHUNKS)
    key = lax.broadcasted_iota(jnp.int32, (ATT_Q, ATT_WIN), 1)
    for e in range(2):
        xw = jnp.broadcast_to(w_ref[e:e + 1, :], (ATT_Q, BIAS_LEN))
        for bit in range(8):
            xw = jnp.where(((n_i >> bit) & 1) == 1, pltpu.roll(xw, 1 << bit, 1), xw)
        bias = jnp.where(band, xw[:, BIAS_LEN - ATT_WIN:], NEG_INF)
        for first in range(ATT_STARTS):
            bias_ref[first, e] = jnp.where(key + (first * ATT_Q - ATT_PAD) >= 0, bias, NEG_INF)
        bias_ref[ATT_STARTS, e] = bias


ATT_PAIRS = 2
ATT_COLS = ATT_PAIRS * LANES


def _att_specs(batch, seq):
    ni = seq // ATT_Q
    q_spec = pl.BlockSpec((ATT_Q, ATT_COLS), lambda g, b, i: (b * ni + i, C_AQ // ATT_COLS + g))
    k_spec = pl.BlockSpec((seq, ATT_COLS), lambda g, b, i: (b, C_AK // ATT_COLS + g))
    v_spec = pl.BlockSpec((seq, ATT_COLS), lambda g, b, i: (b, C_AV // ATT_COLS + g))
    w_spec = pl.BlockSpec((ATT_PAIRS, 2, BIAS_LEN), lambda g, b, i: (g, 0, 0))
    b_spec = pl.BlockSpec((ATT_PAIRS, ATT_STARTS + 1, 2, ATT_Q, ATT_WIN), lambda g, b, i: (g, 0, 0, 0, 0))
    pad = pltpu.VMEM((seq + ATT_PAD, ATT_COLS), BF16)
    return ni, q_spec, k_spec, v_spec, w_spec, b_spec, pad


def _att_bias_tiles(wvec, exchange):
    (tiles,), moved = _call(
        lambda w_ref, o_ref: _att_bias(w_ref, o_ref), name="att_bias", grid=(ATT_HEADS // 2,),
        in_specs=[pl.BlockSpec((None, 2, BIAS_LEN), lambda hp: (hp, 0, 0))],
        out_specs=[pl.BlockSpec((None, ATT_STARTS + 1, 2, ATT_Q, ATT_WIN), lambda hp: (hp, 0, 0, 0, 0))],
        out_shape=[jax.ShapeDtypeStruct((ATT_HEADS // 2, ATT_STARTS + 1, 2, ATT_Q, ATT_WIN), F32)],
        semantics=("parallel",), args=(wvec,), exchange=exchange)
    return tiles, moved


def _att_pad(src_ref, pad_ref):
    pad_ref[:ATT_PAD, :] = jnp.zeros((ATT_PAD, ATT_COLS), BF16)
    pad_ref[ATT_PAD:, :] = src_ref[...]


def _att_head(q2, sel):
    return jnp.where(sel, q2, jnp.zeros_like(q2)) * 0.125


def _att_softmax_rows(s_ref, bias_ref, rows):
    s = s_ref[rows, :] + bias_ref[rows, :]
    ex = jnp.exp(s - jnp.max(s, axis=-1, keepdims=True))
    return ex, 1.0 / jnp.sum(ex, axis=-1, keepdims=True)


def _att_fwd(proj, bias, batch, seq, exchange):
    ni, q_spec, k_spec, v_spec, _, b_spec, pad = _att_specs(batch, seq)

    def body(q_ref, k_ref, v_ref, bias_ref, o_ref, kp_ref, vp_ref, s_ref, e_ref):
        i = pl.program_id(2)

        @pl.when(i == 0)
        def _():
            _att_pad(k_ref, kp_ref)
            _att_pad(v_ref, vp_ref)

        win = pl.ds(pl.multiple_of(i * ATT_Q, ATT_Q), ATT_WIN)
        lo = lax.broadcasted_iota(jnp.int32, (1, LANES), 1) < 64
        start = jnp.minimum(i, ATT_STARTS)
        for pair in range(ATT_PAIRS):
            cols = slice(pair * LANES, (pair + 1) * LANES)
            k2, v2, q2 = kp_ref[win, cols], vp_ref[win, cols], q_ref[:, cols]
            out = jnp.zeros((ATT_Q, LANES), F32)
            for e in range(2):
                h = 2 * pair + e
                sel = lo if e == 0 else jnp.logical_not(lo)
                s_ref[h] = _dot_nt(_att_head(q2, sel), k2)
                rsum = []
                for c in range(ATT_Q // ATT_ROWS):
                    rows = slice(c * ATT_ROWS, (c + 1) * ATT_ROWS)
                    ex, r = _att_softmax_rows(s_ref.at[h], bias_ref.at[pair, start, e], rows)
                    e_ref[h, rows, :] = ex.astype(BF16)
                    rsum.append(r)
                out = out + _dot(e_ref[h], jnp.where(sel, v2, jnp.zeros_like(v2))) * jnp.concatenate(rsum, axis=0)
            o_ref[:, cols] = out.astype(BF16)

    heads = 2 * ATT_PAIRS
    return _call(
        body, name="att_fwd", grid=(ATT_HEADS // heads, batch, ni),
        in_specs=[q_spec, k_spec, v_spec, b_spec],
        out_specs=[pl.BlockSpec((ATT_Q, ATT_COLS), lambda g, b, i: (b * ni + i, g))],
        out_shape=[jax.ShapeDtypeStruct((batch * seq, ATT_HEADS * 64), BF16)],
        scratch=[pad, pad, pltpu.VMEM((heads, ATT_Q, ATT_WIN), F32), pltpu.VMEM((heads, ATT_Q, ATT_WIN), BF16)],
        semantics=("arbitrary", "arbitrary", "arbitrary"), args=(proj, proj, proj, bias), exchange=exchange)


GL_HALF = 512


def _gl_specs(tm):
    return [pl.BlockSpec((tm, GL_HALF), lambda i, c=C_GL // GL_HALF + j: (i, c)) for j in range(4)]


def _gates(gl_refs, b_ref):
    logits = [ref[...].astype(F32) for ref in gl_refs]
    gr = _sigmoid(jnp.concatenate(logits[:2], axis=1) + b_ref[:, :D_MODEL])
    ga = _sigmoid(jnp.concatenate(logits[2:], axis=1) + b_ref[:, D_MODEL:])
    return gr, ga


def _whole(a):
    return pl.BlockSpec(a.shape, lambda i: (0,) * a.ndim)


def _mix_out_fwd(gro, ao, proj, b_gate, w_ret, w_att_t, x2, w_out, g2):
    t = gro.shape[0]
    tm = min(256, t)

    def body(gro_ref, ao_ref, gl0, gl1, gl2, gl3, b_ref, wr_ref, wa_ref, x_ref, wo_ref, g_ref,
             z_ref, yr_ref, ya_ref, h_ref, hn_ref):
        yr = _dot(gro_ref[...], wr_ref[...])
        ya = _dot_nt(ao_ref[...], wa_ref[...])
        yr_ref[...] = yr.astype(BF16)
        ya_ref[...] = ya.astype(BF16)
        gr, ga = _gates((gl0, gl1, gl2, gl3), b_ref)
        z = (gr * yr + ga * ya).astype(BF16)
        z_ref[...] = z
        h = x_ref[...] + _dot(z, wo_ref[...])
        h_ref[...] = h
        r = lax.rsqrt(jnp.mean(h * h, axis=-1, keepdims=True) + EPS)
        hn_ref[...] = (h * r * g_ref[...]).astype(BF16)

    row = pl.BlockSpec((tm, D_MODEL), lambda i: (i, 0))
    return pl.pallas_call(
        body, name="mix_out_fwd", grid=(t // tm,),
        in_specs=[row, pl.BlockSpec((tm, 512), lambda i: (i, 0)), *_gl_specs(tm),
                  _whole(b_gate), _whole(w_ret), _whole(w_att_t), row, _whole(w_out), _whole(g2)],
        out_specs=[row] * 5,
        out_shape=[jax.ShapeDtypeStruct((t, D_MODEL), BF16)] * 3
                  + [jax.ShapeDtypeStruct((t, D_MODEL), F32), jax.ShapeDtypeStruct((t, D_MODEL), BF16)],
        compiler_params=_params("parallel"),
    )(gro, ao, proj, proj, proj, proj, b_gate, w_ret, w_att_t, x2, w_out, g2)


def _ffn_up(hn, wg_t, wu_t):
    t = hn.shape[0]
    tm, tn = min(512, t), D_FF // 2

    def body(h_ref, wg_ref, wu_ref, g_ref, u_ref, a_ref):
        g = _dot_nt(h_ref[...], wg_ref[...])
        u = _dot_nt(h_ref[...], wu_ref[...])
        g_ref[...] = g.astype(BF16)
        u_ref[...] = u.astype(BF16)
        a_ref[...] = (g * _sigmoid(g) * u).astype(BF16)

    w_spec = pl.BlockSpec((tn, D_MODEL), lambda j, i: (j, 0))
    out = pl.BlockSpec((tm, tn), lambda j, i: (i, j))
    return pl.pallas_call(
        body, name="ffn_up", grid=(D_FF // tn, t // tm),
        in_specs=[pl.BlockSpec((tm, D_MODEL), lambda j, i: (i, 0)), w_spec, w_spec],
        out_specs=[out, out, out],
        out_shape=[jax.ShapeDtypeStruct((t, D_FF), BF16)] * 3,
        compiler_params=_params("parallel", "parallel"),
    )(hn, wg_t, wu_t)


def _ffn_down_loss(a, h1, tgt, w_down, g3):
    t = a.shape[0]
    tm = min(512, t)

    def body(a_ref, h_ref, t_ref, w_ref, g_ref, dh_ref, dhb_ref, loss_ref, dg_ref):
        @pl.when(pl.program_id(0) == 0)
        def _():
            loss_ref[...] = jnp.zeros_like(loss_ref)
            dg_ref[...] = jnp.zeros_like(dg_ref)

        g = g_ref[...]
        h2 = h_ref[...] + _dot(a_ref[...], w_ref[...])
        r = lax.rsqrt(jnp.mean(h2 * h2, axis=-1, keepdims=True) + EPS)
        err = h2 * r * g - t_ref[...]
        loss_ref[...] += jnp.sum(err * err) * (0.5 / D_MODEL)
        dy = err * (1.0 / D_MODEL)
        dh, dg_rows = _rms_bwd(h2, g, dy)
        dg_ref[...] += jnp.sum(dg_rows, axis=0, keepdims=True)
        dh_ref[...] = dh
        dhb_ref[...] = dh.astype(BF16)

    row = pl.BlockSpec((tm, D_MODEL), lambda i: (i, 0))
    vec = pl.BlockSpec((1, D_MODEL), lambda i: (0, 0))
    return pl.pallas_call(
        body, name="ffn_down_loss", grid=(t // tm,),
        in_specs=[pl.BlockSpec((tm, D_FF), lambda i: (i, 0)), row, row,
                  pl.BlockSpec((D_FF, D_MODEL), lambda i: (0, 0)), vec],
        out_specs=[row, row, pl.BlockSpec((1, LANES), lambda i: (0, 0)), vec],
        out_shape=[jax.ShapeDtypeStruct((t, D_MODEL), F32), jax.ShapeDtypeStruct((t, D_MODEL), BF16),
                   jax.ShapeDtypeStruct((1, LANES), F32), jax.ShapeDtypeStruct((1, D_MODEL), F32)],
        compiler_params=_params("arbitrary"),
    )(a, h1, tgt, w_down, g3)


def _ffn_bwd_act(dh2b, w_down, g_act, u_act, exchange):
    t = dh2b.shape[0]
    tm, tn = min(512, t), D_FF // 2

    def body(d_ref, w_ref, g_ref, u_ref, dg_ref, du_ref):
        da = _dot_nt(d_ref[...], w_ref[...])
        g = g_ref[...].astype(F32)
        u = u_ref[...].astype(F32)
        sg = _sigmoid(g)
        dg_ref[...] = (da * u * sg * (1.0 + g * (1.0 - sg))).astype(BF16)
        du_ref[...] = (da * g * sg).astype(BF16)

    blk = pl.BlockSpec((tm, tn), lambda j, i: (i, j))
    return _call(
        body, name="ffn_bwd_act", grid=(D_FF // tn, t // tm),
        in_specs=[pl.BlockSpec((tm, D_MODEL), lambda j, i: (i, 0)),
                  pl.BlockSpec((tn, D_MODEL), lambda j, i: (j, 0)), blk, blk],
        out_specs=[blk, blk],
        out_shape=[jax.ShapeDtypeStruct((t, D_FF), BF16)] * 2,
        semantics=("parallel", "parallel"), args=(dh2b, w_down, g_act, u_act), exchange=exchange)


def _ffn_bwd_in(dg, du, wg_t, wu_t, h1, dh2, g2, exchange):
    t = dg.shape[0]
    tm, tk = min(512, t), D_FF // 2
    nk = D_FF // tk

    def body(dg_ref, du_ref, wg_ref, wu_ref, h_ref, d2_ref, g_ref, dh_ref, dhb_ref, gn_ref, acc_ref):
        i, kk = pl.program_id(0), pl.program_id(1)

        @pl.when((i == 0) & (kk == 0))
        def _():
            gn_ref[...] = jnp.zeros_like(gn_ref)

        @pl.when(kk == 0)
        def _():
            acc_ref[...] = jnp.zeros_like(acc_ref)

        acc_ref[...] += _dot(dg_ref[...], wg_ref[...]) + _dot(du_ref[...], wu_ref[...])

        @pl.when(kk == nk - 1)
        def _():
            dx, dg_rows = _rms_bwd(h_ref[...], g_ref[...], acc_ref[...])
            dh = d2_ref[...] + dx
            dh_ref[...] = dh
            dhb_ref[...] = dh.astype(BF16)
            gn_ref[...] += jnp.sum(dg_rows, axis=0, keepdims=True)

    act = pl.BlockSpec((tm, tk), lambda i, kk: (i, kk))
    wsp = pl.BlockSpec((tk, D_MODEL), lambda i, kk: (kk, 0))
    row = pl.BlockSpec((tm, D_MODEL), lambda i, kk: (i, 0))
    vec = pl.BlockSpec((1, D_MODEL), lambda i, kk: (0, 0))
    return _call(
        body, name="ffn_bwd_in", grid=(t // tm, nk),
        in_specs=[act, act, wsp, wsp, row, row, vec],
        out_specs=[row, row, vec],
        out_shape=[jax.ShapeDtypeStruct((t, D_MODEL), F32), jax.ShapeDtypeStruct((t, D_MODEL), BF16),
                   jax.ShapeDtypeStruct((1, D_MODEL), F32)],
        scratch=[pltpu.VMEM((tm, D_MODEL), F32)],
        semantics=("arbitrary", "arbitrary"), args=(dg, du, wg_t, wu_t, h1, dh2, g2), exchange=exchange)


def _mix_bwd(dh1b, w_out, proj, b_gate, y_ret, y_att, w_ret, w_att_t):
    t = dh1b.shape[0]
    tm = min(256, t)

    def body(d_ref, wo_ref, gl0, gl1, gl2, gl3, b_ref, yr_ref, ya_ref, wr_ref, wa_ref,
             dyr_ref, dya_ref, dglr_ref, dgla_ref, dgro_ref, dao_ref, db_ref):
        @pl.when(pl.program_id(0) == 0)
        def _():
            db_ref[...] = jnp.zeros_like(db_ref)

        dz = _dot_nt(d_ref[...], wo_ref[...])
        gr, ga = _gates((gl0, gl1, gl2, gl3), b_ref)
        dyr = (dz * gr).astype(BF16)
        dya = (dz * ga).astype(BF16)
        dyr_ref[...] = dyr
        dya_ref[...] = dya
        dglr = dz * yr_ref[...].astype(F32) * gr * (1.0 - gr)
        dgla = dz * ya_ref[...].astype(F32) * ga * (1.0 - ga)
        dglr_ref[...] = dglr.astype(BF16)
        dgla_ref[...] = dgla.astype(BF16)
        db_ref[:, :D_MODEL] += jnp.sum(dglr, axis=0, keepdims=True)
        db_ref[:, D_MODEL:] += jnp.sum(dgla, axis=0, keepdims=True)
        dgro_ref[...] = _dot_nt(dyr, wr_ref[...]).astype(BF16)
        dao_ref[...] = _dot(dya, wa_ref[...]).astype(BF16)

    row = pl.BlockSpec((tm, D_MODEL), lambda i: (i, 0))
    half = pl.BlockSpec((tm, 512), lambda i: (i, 0))
    return pl.pallas_call(
        body, name="mix_bwd", grid=(t // tm,),
        in_specs=[row, _whole(w_out), *_gl_specs(tm), _whole(b_gate), row, row, _whole(w_ret), _whole(w_att_t)],
        out_specs=[row, row, row, row, row, half, _whole(b_gate)],
        out_shape=[jax.ShapeDtypeStruct((t, D_MODEL), BF16)] * 5
                  + [jax.ShapeDtypeStruct((t, 512), BF16), jax.ShapeDtypeStruct(b_gate.shape, F32)],
        compiler_params=_params("arbitrary"),
    )(dh1b, w_out, proj, proj, proj, proj, b_gate, y_ret, y_att, w_ret, w_att_t)


def _ret_bwd(dgro, proj, o_ret, qr, kr, cs, sn, lg_arr, batch, seq, exchange):
    t = batch * seq
    nt = seq // RET_TILE

    def body(dgro_ref, rg_ref, o_ref, qr_ref, kr_ref, v_ref, cs_ref, sn_ref, lg_ref,
             dq_ref, dk_ref, dv_ref, drg_ref, do_ref, st_ref):
        lg = lg_ref[:, 0:1]
        inside, q_dec, k_dec, tile_dec = _decay(lg)

        state = jnp.zeros((RET_KEY_DIM, RET_VAL_DIM), F32)
        for i in range(nt - 1):
            rows = slice(i * RET_TILE, (i + 1) * RET_TILE)
            state = state * tile_dec + _dot_tn(_scaled(kr_ref[rows, :], k_dec), v_ref[rows, :])
            st_ref[i + 1] = state.astype(BF16)

        for i in range(nt):
            rows = slice(i * RET_TILE, (i + 1) * RET_TILE)
            o = o_ref[rows, :]
            xc = o - jnp.mean(o, axis=-1, keepdims=True)
            rs = lax.rsqrt(jnp.mean(xc * xc, axis=-1, keepdims=True) + EPS)
            nrm = xc * rs
            rg = rg_ref[rows, :].astype(F32)
            sg = _sigmoid(rg)
            dg = dgro_ref[rows, :].astype(F32)
            drg_ref[rows, :] = (dg * nrm * sg * (1.0 + rg * (1.0 - sg))).astype(BF16)
            dn = dg * rg * sg
            do = rs * (dn - jnp.mean(dn, axis=-1, keepdims=True)
                       - nrm * jnp.mean(dn * nrm, axis=-1, keepdims=True))
            do_ref[rows, :] = do.astype(BF16)

        dstate = jnp.zeros((RET_KEY_DIM, RET_VAL_DIM), F32)
        for i in reversed(range(nt)):
            rows = slice(i * RET_TILE, (i + 1) * RET_TILE)
            qi, ki, vi, doi = qr_ref[rows, :], kr_ref[rows, :], v_ref[rows, :], do_ref[rows, :]
            p = (_dot_nt(qi, ki) * inside).astype(BF16)
            dp = (_dot_nt(doi, vi) * inside).astype(BF16)
            dq = _dot(dp, ki)
            dk = _dot_tn(dp, qi)
            dv = _dot_tn(p, doi)
            if i > 0:
                dq = dq + _dot_nt(doi, st_ref[i]) * q_dec
            if i < nt - 1:
                dsb = dstate.astype(BF16)
                dk = dk + _dot_nt(vi, dsb) * k_dec
                dv = dv + _dot(_scaled(ki, k_dec), dsb)
            if i > 0:
                dstate = dstate * tile_dec + _dot_tn(_scaled(qi, q_dec), doi)
            dq_ref[rows, :] = (dq * cs_ref[rows, :] - pltpu.roll(dq, 64, 1) * sn_ref[rows, :]).astype(BF16)
            dk = (dk * cs_ref[rows, :] - pltpu.roll(dk, 64, 1) * sn_ref[rows, :]) * (RET_KEY_DIM ** -0.5)
            dk_ref[rows, :] = dk.astype(BF16)
            dv_ref[rows, :] = dv.astype(BF16)

    key = pl.BlockSpec((seq, RET_KEY_DIM), lambda b, h: (b, h))
    val = pl.BlockSpec((seq, RET_VAL_DIM), lambda b, h: (b, h))
    tab = pl.BlockSpec((seq, RET_KEY_DIM), lambda b, h: (0, 0))
    return _call(
        body, name="ret_bwd", grid=(batch, RET_HEADS),
        in_specs=[val, pl.BlockSpec((seq, RET_VAL_DIM), lambda b, h: (b, C_RG // RET_VAL_DIM + h)), val, key, key,
                  pl.BlockSpec((seq, RET_VAL_DIM), lambda b, h: (b, C_RV // RET_VAL_DIM + h)), tab, tab,
                  pl.BlockSpec((None, 1, LANES), lambda b, h: (h, 0, 0))],
        out_specs=[key, key, val, val],
        out_shape=[jax.ShapeDtypeStruct((t, RET_HEADS * RET_KEY_DIM), BF16)] * 2
                  + [jax.ShapeDtypeStruct((t, RET_HEADS * RET_VAL_DIM), BF16)] * 2,
        scratch=[pltpu.VMEM((seq, RET_VAL_DIM), BF16), pltpu.VMEM((nt, RET_KEY_DIM, RET_VAL_DIM), BF16)],
        semantics=("parallel", "parallel"), args=(dgro, proj, o_ret, qr, kr, proj, cs, sn, lg_arr),
        exchange=exchange)


def _att_bwd(proj, bias, dao, batch, seq, exchange):
    ni, q_spec, k_spec, v_spec, w_spec, b_spec, pad = _att_specs(batch, seq)
    t = batch * seq

    def body(q_ref, k_ref, v_ref, bias_ref, do_ref, dq_ref, dk_ref, dv_ref, dw_ref,
             dbias_ref, dk_acc, dv_acc, kp_ref, vp_ref, s_ref, dp_ref, e_ref, ds_ref):
        b, i = pl.program_id(1), pl.program_id(2)

        @pl.when((b == 0) & (i == 0))
        def _():
            dbias_ref[...] = jnp.zeros_like(dbias_ref)

        @pl.when(i == 0)
        def _():
            _att_pad(k_ref, kp_ref)
            _att_pad(v_ref, vp_ref)
            dk_acc[...] = jnp.zeros_like(dk_acc)
            dv_acc[...] = jnp.zeros_like(dv_acc)

        win = pl.ds(pl.multiple_of(i * ATT_Q, ATT_Q), ATT_WIN)
        lo = lax.broadcasted_iota(jnp.int32, (1, LANES), 1) < 64
        start = jnp.minimum(i, ATT_STARTS)
        for pair in range(ATT_PAIRS):
            cols = slice(pair * LANES, (pair + 1) * LANES)
            k2, v2, q2, do2 = kp_ref[win, cols], vp_ref[win, cols], q_ref[:, cols], do_ref[:, cols]
            dq = jnp.zeros((ATT_Q, LANES), F32)
            dk = jnp.zeros((LANES, ATT_WIN), F32)
            dv = jnp.zeros((LANES, ATT_WIN), F32)
            for e in range(2):
                h = 2 * pair + e
                sel = lo if e == 0 else jnp.logical_not(lo)
                qm = _att_head(q2, sel)
                dom = jnp.where(sel, do2, jnp.zeros_like(do2))
                s_ref[h] = _dot_nt(qm, k2)
                dp_ref[h] = _dot_nt(dom, v2)
                rsum = []
                for c in range(ATT_Q // ATT_ROWS):
                    rows = slice(c * ATT_ROWS, (c + 1) * ATT_ROWS)
                    ex, r = _att_softmax_rows(s_ref.at[h], bias_ref.at[pair, start, e], rows)
                    dp = dp_ref[h, rows, :]
                    mean = jnp.sum(dp * ex, axis=-1, keepdims=True) * r
                    ds = ex * ((dp - mean) * r)
                    dbias_ref[h, rows, :] += ds
                    ds_ref[h, rows, :] = ds.astype(BF16)
                    e_ref[h, rows, :] = ex.astype(BF16)
                    rsum.append(r)
                dq = dq + _dot(ds_ref[h], jnp.where(sel, k2, jnp.zeros_like(k2)))
                dk = dk + _dot_tn(qm, ds_ref[h])
                dv = dv + _dot_tn((dom.astype(F32) * jnp.concatenate(rsum, axis=0)).astype(BF16), e_ref[h])
            dq_ref[:, cols] = (dq * 0.125).astype(BF16)
            dk_acc[cols, win] += dk
            dv_acc[cols, win] += dv

        @pl.when(i == ni - 1)
        def _():
            dk_ref[...] = dk_acc[:, ATT_PAD:].T.astype(BF16)
            dv_ref[...] = dv_acc[:, ATT_PAD:].T.astype(BF16)

        @pl.when((b == batch - 1) & (i == ni - 1))
        def _():
            n_i = lax.broadcasted_iota(jnp.int32, (ATT_Q, BIAS_LEN), 0)
            for h in range(heads):
                xw = jnp.concatenate([jnp.zeros((ATT_Q, BIAS_LEN - ATT_WIN), F32), dbias_ref[h]], axis=1)
                for bit in range(8):
                    xw = jnp.where(((n_i >> bit) & 1) == 1, pltpu.roll(xw, BIAS_LEN - (1 << bit), 1), xw)
                dw_ref[h // 2, h % 2:h % 2 + 1, :] = jnp.sum(xw, axis=0, keepdims=True)

    heads = 2 * ATT_PAIRS
    seq_blk = pl.BlockSpec((seq, ATT_COLS), lambda g, b, i: (b, g))
    q_out = pl.BlockSpec((ATT_Q, ATT_COLS), lambda g, b, i: (b * ni + i, g))
    tile_f32, tile_bf16 = pltpu.VMEM((heads, ATT_Q, ATT_WIN), F32), pltpu.VMEM((heads, ATT_Q, ATT_WIN), BF16)
    acc = pltpu.VMEM((ATT_COLS, seq + ATT_PAD), F32)
    return _call(
        body, name="att_bwd", grid=(ATT_HEADS // heads, batch, ni),
        in_specs=[q_spec, k_spec, v_spec, b_spec, q_out],
        out_specs=[q_out, seq_blk, seq_blk, w_spec],
        out_shape=[jax.ShapeDtypeStruct((t, 512), BF16)] * 3
                  + [jax.ShapeDtypeStruct((ATT_HEADS // 2, 2, BIAS_LEN), F32)],
        scratch=[tile_f32, acc, acc, pad, pad, tile_f32, tile_f32, tile_bf16, tile_bf16],
        semantics=("arbitrary", "arbitrary", "arbitrary"), args=(proj, proj, proj, bias, dao), exchange=exchange,
        vmem=VMEM_LIMIT_ATT_BWD)


def _rms_in_bwd(x2, dxn, dh1, g1):
    t = x2.shape[0]
    tm = min(512, t)

    def body(x_ref, d_ref, h_ref, g_ref, dx_ref, dg_ref):
        @pl.when(pl.program_id(0) == 0)
        def _():
            dg_ref[...] = jnp.zeros_like(dg_ref)

        dx, dg_rows = _rms_bwd(x_ref[...], g_ref[...], d_ref[...])
        dx_ref[...] = h_ref[...] + dx
        dg_ref[...] += jnp.sum(dg_rows, axis=0, keepdims=True)

    row = pl.BlockSpec((tm, D_MODEL), lambda i: (i, 0))
    vec = pl.BlockSpec((1, D_MODEL), lambda i: (0, 0))
    return pl.pallas_call(
        body, name="rms_in_bwd", grid=(t // tm,),
        in_specs=[row, row, row, vec], out_specs=[row, vec],
        out_shape=[jax.ShapeDtypeStruct((t, D_MODEL), F32), jax.ShapeDtypeStruct((1, D_MODEL), F32)],
        compiler_params=_params("arbitrary"),
    )(x2, dxn, dh1, g1)


def _pack_small(dg1, dbr, dba, dg2, dg3, dw, loss):
    def body(a_ref, b_ref, c_ref, d_ref, e_ref, w_ref, l_ref, o_ref):
        o_ref[...] = jnp.zeros_like(o_ref)
        for r, ref in enumerate((a_ref, b_ref, c_ref, d_ref, e_ref)):
            o_ref[r:r + 1, :] = ref[...]
        o_ref[5:6, 0:LANES] = l_ref[...]
        for hp in range(ATT_HEADS // 2):
            o_ref[8 + 2 * hp:10 + 2 * hp, :] = w_ref[hp]

    return pl.pallas_call(body, name="pack_small",
                          out_shape=jax.ShapeDtypeStruct((16, D_MODEL), F32))(dg1, dbr, dba, dg2, dg3, dw, loss)


def _rotary_tables(seq):
    freqs = ROPE_BASE ** (-jnp.arange(0, RET_KEY_DIM, 2, dtype=F32) / RET_KEY_DIM)
    ang = jnp.arange(seq, dtype=F32)[:, None] * freqs[None, :]
    cos, sin = jnp.cos(ang), jnp.sin(ang)
    return jnp.concatenate([cos, cos], axis=1), jnp.concatenate([-sin, sin], axis=1)


def _bias_rows(rel_bias):
    n_far = BIAS_LEN - ATT_Q - MAX_REL + 1
    n_near = BIAS_LEN - n_far - (N_REL - 2)
    w = jnp.concatenate([jnp.broadcast_to(rel_bias[:, N_REL - 1:], (ATT_HEADS, n_far)),
                         rel_bias[:, 1:N_REL - 1][:, ::-1],
                         jnp.broadcast_to(rel_bias[:, :1], (ATT_HEADS, n_near))], axis=1)
    return w.reshape(ATT_HEADS // 2, 2, BIAS_LEN)


def _bias_rows_bwd(dw):
    n_far = BIAS_LEN - ATT_Q - MAX_REL + 1
    mid = dw[:, n_far:n_far + N_REL - 2][:, ::-1]
    return jnp.concatenate([jnp.sum(dw[:, n_far + N_REL - 2:], axis=1, keepdims=True), mid,
                            jnp.sum(dw[:, :n_far], axis=1, keepdims=True)], axis=1)


def _step(x, tgt, norm_mix, b_gate, norm_ffn, norm_final, rel_bias_shard, shard):
    batch, seq, _ = x.shape
    t = batch * seq
    n_rb = rel_bias_shard.shape[-1]
    x2, tgt2 = x.reshape(t, D_MODEL), tgt.reshape(t, D_MODEL)
    g3 = norm_final.reshape(1, D_MODEL)
    cs, sn = _rotary_tables(seq)
    lg = np.log(1.0 - 2.0 ** (-5.0 - np.arange(RET_HEADS, dtype=np.float32))).astype(np.float32)
    lg_arr = jnp.asarray(np.broadcast_to(lg[:, None, None], (RET_HEADS, 1, LANES)))

    def gather(*names):
        return _ChipGather([shard[nm] for nm in names])

    def scatter(*grads):
        return _Exchange(grads, scatter=True)

    rb_pad = jnp.pad(rel_bias_shard, ((0, 0), (0, LANES - n_rb)))
    (xn,), (w_in_half, rb_full) = _rms_fwd(x2, norm_mix,
                                           _ChipGather([shard["w_in_t"], rb_pad], parts=[(0, 2), (0, 1)]))
    rb_full = rb_full.reshape(N_DEV, ATT_HEADS, LANES)[:, :, :n_rb]
    bias, (w_in_t,) = _att_bias_tiles(
        _bias_rows(jnp.transpose(rb_full, (1, 0, 2)).reshape(ATT_HEADS, N_DEV * n_rb)),
        _ChipGather([shard["w_in_t"]], parts=[(1, 2)], into=[w_in_half]))
    proj, (w_ret, w_att_t, w_out, w_gate_t) = _mm(
        xn, w_in_t, tb=True, out_dtype=BF16, tm=1024, tn=1664, tk=1024, name="proj",
        exchange=gather("w_ret", "w_att_t", "w_out", "w_gate_t"))
    (gro, o_ret, qr, kr), _ = _ret_fwd(proj, cs, sn, lg_arr, batch, seq, None)
    (ao,), (w_up_t, w_down) = _att_fwd(proj, bias, batch, seq, gather("w_up_t", "w_down"))
    z, y_ret, y_att, h1, hn = _mix_out_fwd(gro, ao, proj, b_gate, w_ret, w_att_t, x2, w_out, norm_ffn)
    g_act, u_act, a_act = _ffn_up(hn, w_gate_t, w_up_t)
    dh2, dh2b, loss, dg3 = _ffn_down_loss(a_act, h1, tgt2, w_down, g3)

    wg = dict(out_dtype=BF16, tn=1024, ta=True)
    slots = {}
    dw_down = _mm(a_act, dh2b, tm=1408, tk=1024, name="dw_down", **wg)
    (d_gact, d_uact), _ = _ffn_bwd_act(dh2b, w_down, g_act, u_act, None)
    dw_gate = _mm(d_gact, hn, tm=1408, tk=1024, name="dw_gate", **wg)
    dw_up = _mm(d_uact, hn, tm=1408, tk=1024, name="dw_up", **wg)
    (dh1, dh1b, dg2), (slots["w_down"],) = _ffn_bwd_in(d_gact, d_uact, w_gate_t, w_up_t, h1, dh2, norm_ffn,
                                                     scatter(dw_down))
    dw_out = _mm(z, dh1b, tm=1024, tk=2048, name="dw_out", **wg)
    dyr, dya, dglr, dgla, dgro, dao, db = _mix_bwd(dh1b, w_out, proj, b_gate, y_ret, y_att, w_ret, w_att_t)
    dw_ret = _mm(gro, dyr, tm=1024, tk=2048, name="dw_ret", **wg)
    dw_att = _mm(dya, ao, tm=1024, tk=2048, name="dw_att", **wg)
    (drq, drk, drv, drg), _ = _ret_bwd(dgro, proj, o_ret, qr, kr, cs, sn, lg_arr, batch, seq, None)
    (daq, dak, dav, dw), (slots["w_gate_t"], slots["w_out"], slots["w_ret"], slots["w_att_t"]) = _att_bwd(
        proj, bias, dao, batch, seq, scatter(dw_gate, dw_out, dw_ret, dw_att))
    dproj = [drq, drk, drv, drg, daq, dak, dav, dglr, dgla]
    dw_in, (slots["w_up_t"],) = _mm_pieces(dproj, xn, ta=True, out_dtype=BF16, tm=512, tn=1024, tk=1024,
                                           name="dw_in", exchange=scatter(dw_up))
    (dw_in_sibling,) = _alone(_PairSwap([dw_in]), "swap_w_in")
    dw_in_pairs = _pair_add(dw_in, dw_in_sibling, "pair_w_in")
    dxn, (slots["w_in_t"],) = _mm_pieces(dproj, w_in_t, ta=False, out_dtype=F32, tm=1024, tn=1024, tk=512, name="dxn",
                                  exchange=_ChipScatter([dw_in_pairs]))
    dx, dg1 = _rms_in_bwd(x2, dxn, dh1, norm_mix)
    small = _pack_small(dg1, db[:, :D_MODEL], db[:, D_MODEL:], dg2, dg3, dw, loss)
    (small_slots,) = _alone(_ChipGather([small]), "gather_small")
    return dx.reshape(batch, seq, D_MODEL), slots, small_slots.reshape(N_DEV, 16, D_MODEL)


def _row_tile(r, c):
    return max(d for d in range(16, r + 1, 16) if r % d == 0 and (d * c <= 256 * 1024 or d == 16))


def _pair_add(grad, got, name):
    _, r, c = got.shape
    tr = r
    core = lax.axis_index("c").astype(jnp.int32).reshape(1)

    def body(core_ref, g_ref, a_ref, o_ref):
        o_ref[...] = (g_ref[...].astype(F32) + a_ref[...].astype(F32)).astype(o_ref.dtype)

    blk = pl.BlockSpec((None, tr, c), lambda q, i, core_ref: (q, i, 0))
    return pl.pallas_call(
        body, name=name,
        grid_spec=pltpu.PrefetchScalarGridSpec(
            num_scalar_prefetch=1, grid=(4, r // tr),
            in_specs=[pl.BlockSpec((None, None, tr, c), lambda q, i, core_ref: (q, core_ref[0], i, 0)), blk],
            out_specs=blk),
        out_shape=jax.ShapeDtypeStruct(got.shape, got.dtype),
        compiler_params=_params("parallel", "parallel"),
    )(core, grad.reshape(4, 2, r, c), got)


def _sum_slots(slots, name):
    n, r, c = slots.shape
    tr = _row_tile(r, c)

    def body(s_ref, o_ref):
        acc = s_ref[0].astype(F32)
        for s in range(1, n):
            acc = acc + s_ref[s].astype(F32)
        o_ref[...] = acc

    return pl.pallas_call(
        body, name=name, grid=(r // tr,),
        in_specs=[pl.BlockSpec((n, tr, c), lambda i: (0, i, 0))],
        out_specs=pl.BlockSpec((tr, c), lambda i: (i, 0)),
        out_shape=jax.ShapeDtypeStruct((r, c), F32),
        compiler_params=_params("parallel"),
    )(slots)


def _adamw_math(w, g, m, v):
    m = ADAM_B1 * m + (1.0 - ADAM_B1) * g
    v = ADAM_B2 * v + (1.0 - ADAM_B2) * (g * g)
    m_hat = m / (1.0 - ADAM_B1 ** ADAM_STEP)
    v_hat = v / (1.0 - ADAM_B2 ** ADAM_STEP)
    return -ADAM_LR * (m_hat / (jnp.sqrt(v_hat) + ADAM_EPS) + ADAM_WD * w), m, v


def _adamw(w, slots, m, v, name):
    n, r, c = slots.shape
    tr = _row_tile(r, c)

    def body(w_ref, s_ref, m_ref, v_ref, g_ref, d_ref, nm_ref, nv_ref):
        g = s_ref[0].astype(F32)
        for s in range(1, n):
            g = g + s_ref[s].astype(F32)
        g_ref[...] = g
        d_ref[...], nm_ref[...], nv_ref[...] = _adamw_math(w_ref[...], g, m_ref[...], v_ref[...])

    blk = pl.BlockSpec((tr, c), lambda i: (i, 0))
    return pl.pallas_call(
        body, name=name, grid=(r // tr,),
        in_specs=[blk, pl.BlockSpec((n, tr, c), lambda i: (0, i, 0)), blk, blk], out_specs=[blk] * 4,
        out_shape=[jax.ShapeDtypeStruct((r, c), F32)] * 4,
        compiler_params=_params("parallel"),
    )(w, slots, m, v)


def _adamw_small(ws, gs, ms, vs):
    n = len(ws)

    def body(*refs):
        for i in range(n):
            w_ref, g_ref, m_ref, v_ref = (refs[j * n + i] for j in range(4))
            d_ref, nm_ref, nv_ref = (refs[(4 + j) * n + i] for j in range(3))
            d_ref[...], nm_ref[...], nv_ref[...] = _adamw_math(w_ref[...], g_ref[...], m_ref[...], v_ref[...])

    shapes = [jax.ShapeDtypeStruct(w.shape, F32) for w in ws]
    outs = pl.pallas_call(body, name="adamw_small", out_shape=shapes * 3)(*ws, *gs, *ms, *vs)
    return outs[:n], outs[n:2 * n], outs[2 * n:]


def kernel(x, norm_mix, w_in, b_gate, rel_bias, w_ret_out, w_att_out, w_out, norm_ffn, w_ffn_gate, w_ffn_up, w_ffn_down, norm_final, loss_target, m_norm_mix, m_w_in, m_b_gate, m_rel_bias, m_w_ret_out, m_w_att_out, m_w_out, m_norm_ffn, m_w_ffn_gate, m_w_ffn_up, m_w_ffn_down, m_norm_final, v_norm_mix, v_w_in, v_b_gate, v_rel_bias, v_w_ret_out, v_w_att_out, v_w_out, v_norm_ffn, v_w_ffn_gate, v_w_ffn_up, v_w_ffn_down, v_norm_final):
    me = _index(_place())
    n_rb = rel_bias.shape[-1]

    shard = dict(w_in_t=w_in[0].T, w_gate_t=w_ffn_gate[0].T, w_up_t=w_ffn_up[0].T, w_down=w_ffn_down[0],
                 w_ret=w_ret_out[0], w_out=w_out[0], w_att_t=w_att_out[0].T)
    shard = {nm: s.astype(BF16) for nm, s in shard.items()}
    dx, slots, small_slots = _step(x, loss_target, norm_mix, b_gate, norm_ffn, norm_final, rel_bias[0], shard)
    small_sum = _sum_slots(small_slots, "sum_small")
    loss = small_sum[5, 0]

    transposed = dict(w_in="w_in_t", w_ffn_gate="w_gate_t", w_ffn_up="w_up_t", w_att_out="w_att_t")
    plain = dict(w_ffn_down="w_down", w_ret_out="w_ret", w_out="w_out")
    g = dict(
        norm_mix=small_sum[0:1], b_gate=jnp.concatenate([small_sum[1:2], small_sum[2:3]], axis=1),
        norm_ffn=small_sum[3:4], norm_final=small_sum[4:5],
        rel_bias=lax.dynamic_slice_in_dim(_bias_rows_bwd(small_sum[8:16]), me * n_rb, n_rb, axis=1),
    )
    w = dict(norm_mix=norm_mix, w_in=w_in, b_gate=b_gate, rel_bias=rel_bias, w_ret_out=w_ret_out, w_att_out=w_att_out,
             w_out=w_out, norm_ffn=norm_ffn, w_ffn_gate=w_ffn_gate, w_ffn_up=w_ffn_up, w_ffn_down=w_ffn_down,
             norm_final=norm_final)
    m = dict(norm_mix=m_norm_mix, w_in=m_w_in, b_gate=m_b_gate, rel_bias=m_rel_bias, w_ret_out=m_w_ret_out,
             w_att_out=m_w_att_out, w_out=m_w_out, norm_ffn=m_norm_ffn, w_ffn_gate=m_w_ffn_gate, w_ffn_up=m_w_ffn_up,
             w_ffn_down=m_w_ffn_down, norm_final=m_norm_final)
    v = dict(norm_mix=v_norm_mix, w_in=v_w_in, b_gate=v_b_gate, rel_bias=v_rel_bias, w_ret_out=v_w_ret_out,
             w_att_out=v_w_att_out, w_out=v_w_out, norm_ffn=v_norm_ffn, w_ffn_gate=v_w_ffn_gate, w_ffn_up=v_w_ffn_up,
             w_ffn_down=v_w_ffn_down, norm_final=v_norm_final)
    order = ("norm_mix", "w_in", "b_gate", "rel_bias", "w_ret_out", "w_att_out", "w_out", "norm_ffn",
             "w_ffn_gate", "w_ffn_up", "w_ffn_down", "norm_final")
    small_names = ("norm_mix", "b_gate", "rel_bias", "norm_ffn", "norm_final")

    def flat(a):
        return a[0] if a.ndim == 3 else a.reshape(-1, a.shape[-1])

    grad, delta, new_m, new_v = {}, {}, {}, {}
    for nm in order:
        if nm in transposed:
            res = _adamw(w[nm][0].T, slots[transposed[nm]], m[nm][0].T, v[nm][0].T, "adamw_" + nm)
            grad[nm], delta[nm], new_m[nm], new_v[nm] = (a.T[None] for a in res)
        elif nm in plain:
            res = _adamw(flat(w[nm]), slots[plain[nm]], flat(m[nm]), flat(v[nm]), "adamw_" + nm)
            grad[nm], delta[nm], new_m[nm], new_v[nm] = (a.reshape(w[nm].shape) for a in res)
    ds, nms, nvs = _adamw_small([flat(w[nm]) for nm in small_names], [g[nm] for nm in small_names],
                                [flat(m[nm]) for nm in small_names], [flat(v[nm]) for nm in small_names])
    for i, nm in enumerate(small_names):
        grad[nm], delta[nm], new_m[nm], new_v[nm] = (a.reshape(w[nm].shape) for a in (g[nm], ds[i], nms[i], nvs[i]))

    return (loss, dx, *[grad[nm] for nm in order], *[delta[nm] for nm in order],
            *[new_m[nm] for nm in order], *[new_v[nm] for nm in order])
```

```python
import numpy as np
import jax
import jax.numpy as jnp
from jax import lax
from jax.experimental import pallas as pl
from jax.experimental.pallas import tpu as pltpu

F32 = jnp.float32
BF16 = jnp.bfloat16
MESH = pl.DeviceIdType.MESH

D_MODEL = 1024
CHUNK = 64
RET_HEADS = 4
RET_KEY_DIM = 128
RET_VAL_DIM = 256
ATT_HEADS = 8
BAND_CHUNKS = 8
MAX_REL = 256
N_REL = CHUNK + MAX_REL
D_FF = 2816
N_IN = 6656
ROPE_BASE = 10000.0
EPS = 1e-6
NEG_INF = -1e30
C_RQ, C_RK, C_RV, C_RG, C_AQ, C_AK, C_AV, C_GL = 0, 512, 1024, 2048, 3072, 3584, 4096, 4608

ADAM_LR = 0.001
ADAM_B1 = 0.9
ADAM_B2 = 0.999
ADAM_EPS = 1e-08
ADAM_WD = 0.01
ADAM_STEP = 10

N_DEV = 8
LANES = 128
RET_TILE = 256
ATT_Q = 256
ATT_PAD = BAND_CHUNKS * CHUNK
ATT_WIN = ATT_PAD + ATT_Q
ATT_STARTS = ATT_PAD // ATT_Q
ATT_ROWS = 32
BIAS_LEN = 1024
VMEM_LIMIT = 32 * 1024 * 1024
VMEM_LIMIT_ATT_BWD = 56 * 1024 * 1024


def _params(*sem, vmem=VMEM_LIMIT):
    return pltpu.CompilerParams(dimension_semantics=sem, vmem_limit_bytes=vmem)


def _dot(a, b):
    return lax.dot_general(a, b, (((1,), (0,)), ((), ())), preferred_element_type=F32)


def _dot_nt(a, b):
    return lax.dot_general(a, b, (((1,), (1,)), ((), ())), preferred_element_type=F32)


def _dot_tn(a, b):
    return lax.dot_general(a, b, (((0,), (0,)), ((), ())), preferred_element_type=F32)


def _sigmoid(x):
    return 1.0 / (1.0 + jnp.exp(-x))


def _rms_bwd(x, g, dy):
    r = lax.rsqrt(jnp.mean(x * x, axis=-1, keepdims=True) + EPS)
    u = dy * g
    dx = r * u - x * (r * r * r) * jnp.mean(u * x, axis=-1, keepdims=True)
    return dx, dy * x * r


def _place():
    return lax.axis_index("x"), lax.axis_index("y"), lax.axis_index("c")


def _peer(k):
    x, y, c = _place()
    return ((1 - x) if k & 4 else x, (1 - y) if k & 2 else y, (1 - c) if k & 1 else c)


def _index(place):
    return 4 * place[0] + 2 * place[1] + place[2]


def _rows(ref, block, nrows):
    align = 16 if ref.dtype == BF16 else 8
    return ref.at[pl.ds(pl.multiple_of(block * nrows, align), nrows)]


class _Exchange:
    def __init__(self, arrays, scatter):
        self.arrays, self.scatter, self.n = list(arrays), scatter, len(arrays)

    def out_shape(self):
        if self.scatter:
            return [jax.ShapeDtypeStruct((N_DEV, a.shape[0] // N_DEV) + a.shape[1:], a.dtype) for a in self.arrays]
        return [jax.ShapeDtypeStruct((N_DEV * a.shape[0],) + a.shape[1:], a.dtype) for a in self.arrays]

    def scratch(self):
        return [pltpu.SemaphoreType.DMA((self.n, N_DEV - 1)), pltpu.SemaphoreType.DMA((self.n, N_DEV - 1)),
                pltpu.SemaphoreType.DMA((self.n,))]

    def _copies(self, ins, outs, sems):
        send_sems, recv_sems, local_sems = sems
        me = _index(_place())

        def src(w, to):
            return _rows(ins[w], to, ins[w].shape[0] // N_DEV) if self.scatter else ins[w]

        def dst(w, origin):
            return outs[w].at[origin] if self.scatter else _rows(outs[w], origin, ins[w].shape[0])

        def remote(w, k, to, origin):
            return pltpu.make_async_remote_copy(src_ref=src(w, to), dst_ref=dst(w, origin),
                                                send_sem=send_sems.at[w, k - 1], recv_sem=recv_sems.at[w, k - 1],
                                                device_id=_peer(k), device_id_type=MESH)

        pairs = [(w, k) for w in range(self.n) for k in range(1, N_DEV)]
        own = lambda: [pltpu.make_async_copy(src(w, me), dst(w, me), local_sems.at[w]) for w in range(self.n)]
        sent = lambda: [remote(w, k, _index(_peer(k)), me) for w, k in pairs]
        arriving = lambda: [remote(w, k, me, _index(_peer(k))) for w, k in pairs]
        return own, sent, arriving

    def start(self, ins, outs, sems):
        own, sent, _ = self._copies(ins, outs, sems)
        for cp in own() + sent():
            cp.start()

    def wait(self, ins, outs, sems):
        own, sent, arriving = self._copies(ins, outs, sems)
        for cp in arriving():
            cp.wait_recv()
        for cp in sent():
            cp.wait_send()
        for cp in own():
            cp.wait()


class _PairSwap:
    def __init__(self, arrays):
        self.arrays, self.n = list(arrays), len(arrays)

    def out_shape(self):
        return [jax.ShapeDtypeStruct((4, a.shape[0] // N_DEV) + a.shape[1:], a.dtype) for a in self.arrays]

    def scratch(self):
        return [pltpu.SemaphoreType.DMA((self.n, 4)), pltpu.SemaphoreType.DMA((self.n, 4))]

    def _copies(self, ins, outs, sems):
        send_sems, recv_sems = sems
        x, y, c = _place()
        return [pltpu.make_async_remote_copy(
            src_ref=_rows(ins[w], 2 * q + 1 - c, ins[w].shape[0] // N_DEV), dst_ref=outs[w].at[q],
            send_sem=send_sems.at[w, q], recv_sem=recv_sems.at[w, q],
            device_id=(x, y, 1 - c), device_id_type=MESH) for w in range(self.n) for q in range(4)]

    def start(self, ins, outs, sems):
        for cp in self._copies(ins, outs, sems):
            cp.start()

    def wait(self, ins, outs, sems):
        for cp in self._copies(ins, outs, sems):
            cp.wait()


class _ChipScatter:
    def __init__(self, arrays):
        self.arrays, self.n = list(arrays), len(arrays)

    def out_shape(self):
        return [jax.ShapeDtypeStruct(a.shape, a.dtype) for a in self.arrays]

    def scratch(self):
        return [pltpu.SemaphoreType.DMA((self.n, 3)), pltpu.SemaphoreType.DMA((self.n, 3)),
                pltpu.SemaphoreType.DMA((self.n,))]

    def _copies(self, ins, outs, sems):
        send_sems, recv_sems, local_sems = sems
        x, y, c = _place()
        mine = 2 * x + y
        sent, arriving = [], []
        for w in range(self.n):
            for k in range(1, 4):
                tx, ty = (1 - x) if k & 2 else x, (1 - y) if k & 1 else y
                other = 2 * tx + ty
                sent.append(lambda w=w, k=k, tx=tx, ty=ty, other=other: pltpu.make_async_remote_copy(
                    src_ref=ins[w].at[other], dst_ref=outs[w].at[mine], send_sem=send_sems.at[w, k - 1],
                    recv_sem=recv_sems.at[w, k - 1], device_id=(tx, ty, c), device_id_type=MESH))
                arriving.append(lambda w=w, k=k, tx=tx, ty=ty, other=other: pltpu.make_async_remote_copy(
                    src_ref=ins[w].at[mine], dst_ref=outs[w].at[other], send_sem=send_sems.at[w, k - 1],
                    recv_sem=recv_sems.at[w, k - 1], device_id=(tx, ty, c), device_id_type=MESH))
        own = [lambda w=w: pltpu.make_async_copy(ins[w].at[mine], outs[w].at[mine], local_sems.at[w])
               for w in range(self.n)]
        return own, sent, arriving

    def start(self, ins, outs, sems):
        own, sent, _ = self._copies(ins, outs, sems)
        for cp in own + sent:
            cp().start()

    def wait(self, ins, outs, sems):
        own, sent, arriving = self._copies(ins, outs, sems)
        for cp in arriving:
            cp().wait_recv()
        for cp in sent:
            cp().wait_send()
        for cp in own:
            cp().wait()


class _ChipGather:
    def __init__(self, arrays, parts=None, into=None):
        self.arrays, self.n, self.into = list(arrays), len(arrays), into
        self.parts = parts or [(0, 1)] * self.n

    def out_shape(self):
        return [jax.ShapeDtypeStruct((N_DEV * a.shape[0],) + a.shape[1:], a.dtype) for a in self.arrays]

    def scratch(self):
        return [pltpu.SemaphoreType.DMA((self.n, N_DEV - 1)), pltpu.SemaphoreType.DMA((self.n, N_DEV - 1)),
                pltpu.SemaphoreType.DMA((self.n,))]

    def _parts(self, ins, outs, sems):
        send_sems, recv_sems, local_sems = sems
        x, y, c = _place()
        me, sibling = (x, y, c), (x, y, 1 - c)
        chips = [(1 - x, y), (x, 1 - y), (1 - x, 1 - y)]

        def rows(w, place, whole):
            (index, count), r = self.parts[w], ins[w].shape[0]
            lo, size = (0, r) if whole else (index * (r // count), r // count)
            align = 16 if ins[w].dtype == BF16 else 8
            return outs[w].at[pl.ds(pl.multiple_of(_index(place) * r + lo, align), size)]

        def mine(w, whole):
            (index, count), r = self.parts[w], ins[w].shape[0]
            return ins[w] if whole or count == 1 else ins[w].at[pl.ds(index * (r // count), r // count)]

        def copy(w, k, block, to, own=False):
            whole = k == 0
            return pltpu.make_async_remote_copy(src_ref=mine(w, whole) if own else rows(w, block, whole),
                                                dst_ref=rows(w, block, whole),
                                                send_sem=send_sems.at[w, k], recv_sem=recv_sems.at[w, k],
                                                device_id=to, device_id_type=MESH)

        def local(w):
            return pltpu.make_async_copy(ins[w], rows(w, me, True), local_sems.at[w])

        return me, sibling, chips, c, copy, local, [index == 0 for index, _ in self.parts]

    def start(self, ins, outs, sems):
        me, sibling, chips, c, copy, local, places_own = self._parts(ins, outs, sems)
        for w in range(self.n):
            if places_own[w]:
                local(w).start()
                copy(w, 0, me, sibling, own=True).start()
            for j, chip in enumerate(chips):
                copy(w, 1 + j, me, (*chip, c), own=True).start()

    def wait(self, ins, outs, sems):
        me, sibling, chips, c, copy, local, places_own = self._parts(ins, outs, sems)
        for w in range(self.n):
            for j, chip in enumerate(chips):
                copy(w, 1 + j, (*chip, c), me).wait_recv()
                copy(w, 4 + j, (*chip, c), sibling).start()
        for w in range(self.n):
            if places_own[w]:
                copy(w, 0, sibling, me).wait_recv()
                copy(w, 0, me, sibling, own=True).wait_send()
                local(w).wait()
            for j, chip in enumerate(chips):
                copy(w, 4 + j, (*chip, 1 - c), me).wait_recv()
                copy(w, 1 + j, me, (*chip, c), own=True).wait_send()
                copy(w, 4 + j, (*chip, c), sibling).wait_send()


def _call(body, *, name, grid, in_specs, out_specs, out_shape, scratch=(), semantics, args, exchange=None,
          vmem=VMEM_LIMIT):
    if exchange is None:
        return pl.pallas_call(body, name=name, grid=grid, in_specs=in_specs, out_specs=out_specs, out_shape=out_shape,
                              scratch_shapes=list(scratch),
                              compiler_params=_params(*semantics, vmem=vmem))(*args), None
    n_in, n_out, n_scr, nx = len(in_specs), len(out_specs), len(scratch), exchange.n
    into = list(getattr(exchange, "into", None) or [])

    def full_body(*refs):
        ins, refs = refs[:n_in], refs[n_in:]
        x_in, refs = refs[:nx], refs[nx + len(into):]
        outs, refs = refs[:n_out], refs[n_out:]
        x_out, refs = refs[:nx], refs[nx:]
        scr, sems = refs[:n_scr], refs[n_scr:]
        first, last = True, True
        for axis, size in enumerate(grid):
            first = jnp.logical_and(first, pl.program_id(axis) == 0)
            last = jnp.logical_and(last, pl.program_id(axis) == size - 1)
        if grid:
            pl.when(first)(lambda: exchange.start(x_in, x_out, sems))
        else:
            exchange.start(x_in, x_out, sems)
        body(*ins, *outs, *scr)
        if grid:
            pl.when(last)(lambda: exchange.wait(x_in, x_out, sems))
        else:
            exchange.wait(x_in, x_out, sems)

    hbm = pl.BlockSpec(memory_space=pltpu.HBM)
    res = pl.pallas_call(
        full_body, name=name, grid=grid,
        in_specs=list(in_specs) + [hbm] * (nx + len(into)), out_specs=list(out_specs) + [hbm] * nx,
        out_shape=list(out_shape) + exchange.out_shape(),
        scratch_shapes=list(scratch) + exchange.scratch(),
        input_output_aliases={n_in + nx + w: n_out + w for w in range(len(into))},
        compiler_params=_params(*(["arbitrary"] * len(grid)), vmem=vmem),
    )(*args, *exchange.arrays, *into)
    return res[:n_out], res[n_out:]


def _alone(exchange, name):
    return _call(lambda: None, name=name, grid=(), in_specs=[], out_specs=[], out_shape=[], semantics=(),
                 args=(), exchange=exchange)[1]


def _mm(a, b, *, ta=False, tb=False, out_dtype, tm, tn, tk, name, exchange=None):
    m, k = (a.shape[1], a.shape[0]) if ta else a.shape
    n = b.shape[0] if tb else b.shape[1]
    assert k == (b.shape[1] if tb else b.shape[0])
    tm, tn, tk = min(tm, m), min(tn, n), min(tk, k)
    assert m % tm == 0 and n % tn == 0 and k % tk == 0, (name, m, n, k)
    nk = k // tk
    dims = (((0 if ta else 1,), (1 if tb else 0,)), ((), ()))

    def body(a_ref, b_ref, o_ref, *acc):
        prod = lax.dot_general(a_ref[...].astype(BF16), b_ref[...].astype(BF16), dims, preferred_element_type=F32)
        if nk == 1:
            o_ref[...] = prod.astype(o_ref.dtype)
            return
        acc_ref, kk = acc[0], pl.program_id(2)

        @pl.when(kk == 0)
        def _():
            acc_ref[...] = prod

        @pl.when((kk > 0) & (kk < nk - 1))
        def _():
            acc_ref[...] += prod

        @pl.when(kk == nk - 1)
        def _():
            o_ref[...] = (acc_ref[...] + prod).astype(o_ref.dtype)

    a_spec = (pl.BlockSpec((tk, tm), lambda i, j, kk: (kk, i)) if ta
              else pl.BlockSpec((tm, tk), lambda i, j, kk: (i, kk)))
    b_spec = (pl.BlockSpec((tn, tk), lambda i, j, kk: (j, kk)) if tb
              else pl.BlockSpec((tk, tn), lambda i, j, kk: (kk, j)))
    (out,), moved = _call(
        body, name=name, grid=(m // tm, n // tn, nk),
        in_specs=[a_spec, b_spec],
        out_specs=[pl.BlockSpec((tm, tn), lambda i, j, kk: (i, j))],
        out_shape=[pltpu.HBM((m, n), out_dtype)],
        scratch=[pltpu.VMEM((tm, tn), F32)] if nk > 1 else [],
        semantics=("parallel", "parallel", "arbitrary"), args=(a, b), exchange=exchange)
    return out if exchange is None else (out, moved)


def _mm_pieces(pieces, b, *, ta, out_dtype, tm, tn, tk, name, exchange=None):
    rows, n = pieces[0].shape[0], b.shape[1]
    step = tm if ta else tk
    assert all(p.shape[0] == rows and p.shape[1] % step == 0 for p in pieces), name
    edges = [int(e) for e in np.cumsum([0] + [p.shape[1] // step for p in pieces])]
    total = edges[-1] * step
    m, k = (total, rows) if ta else (rows, total)
    assert b.shape[0] == k and m % tm == 0 and n % tn == 0 and k % tk == 0, name
    nk, npieces = k // tk, len(pieces)
    dims = (((0 if ta else 1,), (0,)), ((), ()))

    def body(*refs):
        a_refs, (b_ref, o_ref, acc_ref) = refs[:npieces], refs[npieces:]
        kk = pl.program_id(2)
        pos = pl.program_id(0) if ta else kk

        @pl.when(kk == 0)
        def _():
            acc_ref[...] = jnp.zeros_like(acc_ref)

        for p, a_ref in enumerate(a_refs):
            @pl.when((pos >= edges[p]) & (pos < edges[p + 1]))
            def _(a_ref=a_ref):
                acc_ref[...] += lax.dot_general(a_ref[...], b_ref[...], dims, preferred_element_type=F32)

        @pl.when(kk == nk - 1)
        def _():
            o_ref[...] = acc_ref[...].astype(o_ref.dtype)

    def a_spec(p):
        lo, last = edges[p], edges[p + 1] - edges[p] - 1
        if ta:
            def index(i, j, kk):
                inside = (i >= lo) & (i <= lo + last)
                return jnp.where(inside, kk, 0), jnp.clip(i - lo, 0, last)
            return pl.BlockSpec((tk, tm), index)
        return pl.BlockSpec((tm, tk), lambda i, j, kk: (i, jnp.clip(kk - lo, 0, last)))

    (out,), moved = _call(
        body, name=name, grid=(m // tm, n // tn, nk),
        in_specs=[a_spec(p) for p in range(npieces)] + [pl.BlockSpec((tk, tn), lambda i, j, kk: (kk, j))],
        out_specs=[pl.BlockSpec((tm, tn), lambda i, j, kk: (i, j))],
        out_shape=[pltpu.HBM((m, n), out_dtype)],
        scratch=[pltpu.VMEM((tm, tn), F32)],
        semantics=("parallel", "parallel", "arbitrary"), args=(*pieces, b), exchange=exchange)
    return out if exchange is None else (out, moved)


def _rms_fwd(x2, g, exchange):
    t = x2.shape[0]
    tm = min(512, t)

    def body(x_ref, g_ref, o_ref):
        x = x_ref[...]
        r = lax.rsqrt(jnp.mean(x * x, axis=-1, keepdims=True) + EPS)
        o_ref[...] = (x * r * g_ref[...]).astype(o_ref.dtype)

    return _call(
        body, name="rms_in_fwd", grid=(t // tm,),
        in_specs=[pl.BlockSpec((tm, D_MODEL), lambda i: (i, 0)), pl.BlockSpec((1, D_MODEL), lambda i: (0, 0))],
        out_specs=[pl.BlockSpec((tm, D_MODEL), lambda i: (i, 0))],
        out_shape=[jax.ShapeDtypeStruct((t, D_MODEL), BF16)],
        semantics=("parallel",), args=(x2, g), exchange=exchange)


def _decay(lg):
    row = lax.broadcasted_iota(jnp.int32, (RET_TILE, RET_TILE), 0)
    col = lax.broadcasted_iota(jnp.int32, (RET_TILE, RET_TILE), 1)
    within = jnp.exp(lg * jnp.abs(row - col).astype(F32))
    inside = jnp.where((col >> 6) <= (row >> 6), within, 0.0)
    pos = lax.broadcasted_iota(jnp.int32, (RET_TILE, 1), 0).astype(F32)
    q_dec = jnp.exp(lg * (pos + 1.0))
    k_dec = jnp.exp(lg * (RET_TILE - 1.0 - pos))
    tile_dec = jnp.exp(lg * float(RET_TILE))
    return inside, q_dec, k_dec, tile_dec


def _scaled(a_bf16, dec):
    return (a_bf16.astype(F32) * dec).astype(BF16)


def _ret_fwd(proj, cs, sn, lg_arr, batch, seq, exchange):
    t = batch * seq
    nt = seq // RET_TILE

    def body(q_ref, k_ref, v_ref, rg_ref, cs_ref, sn_ref, lg_ref, gro_ref, o_ref, qr_ref, kr_ref):
        lg = lg_ref[:, 0:1]
        cs_t, sn_t = cs_ref[...], sn_ref[...]
        q = q_ref[...].astype(F32)
        k = k_ref[...].astype(F32)
        qr_ref[...] = (q * cs_t + pltpu.roll(q, 64, 1) * sn_t).astype(BF16)
        kr_ref[...] = ((k * cs_t + pltpu.roll(k, 64, 1) * sn_t) * (RET_KEY_DIM ** -0.5)).astype(BF16)
        inside, q_dec, k_dec, tile_dec = _decay(lg)
        state = jnp.zeros((RET_KEY_DIM, RET_VAL_DIM), F32)
        for i in range(nt):
            rows = slice(i * RET_TILE, (i + 1) * RET_TILE)
            qi, ki, vi = qr_ref[rows, :], kr_ref[rows, :], v_ref[rows, :]
            acc = _dot((_dot_nt(qi, ki) * inside).astype(BF16), vi)
            if i > 0:
                acc = acc + _dot(_scaled(qi, q_dec), state.astype(BF16))
            if i < nt - 1:
                state = state * tile_dec + _dot_tn(_scaled(ki, k_dec), vi)
            o_ref[rows, :] = acc
            xc = acc - jnp.mean(acc, axis=-1, keepdims=True)
            nrm = xc * lax.rsqrt(jnp.mean(xc * xc, axis=-1, keepdims=True) + EPS)
            rg = rg_ref[rows, :].astype(F32)
            gro_ref[rows, :] = (rg * _sigmoid(rg) * nrm).astype(BF16)

    def col(base, width):
        return lambda b, h: (b, base // width + h)

    return _call(
        body, name="ret_fwd", grid=(batch, RET_HEADS),
        in_specs=[pl.BlockSpec((seq, RET_KEY_DIM), col(C_RQ, RET_KEY_DIM)),
                  pl.BlockSpec((seq, RET_KEY_DIM), col(C_RK, RET_KEY_DIM)),
                  pl.BlockSpec((seq, RET_VAL_DIM), col(C_RV, RET_VAL_DIM)),
                  pl.BlockSpec((seq, RET_VAL_DIM), col(C_RG, RET_VAL_DIM)),
                  pl.BlockSpec((seq, RET_KEY_DIM), lambda b, h: (0, 0)),
                  pl.BlockSpec((seq, RET_KEY_DIM), lambda b, h: (0, 0)),
                  pl.BlockSpec((None, 1, LANES), lambda b, h: (h, 0, 0))],
        out_specs=[pl.BlockSpec((seq, RET_VAL_DIM), lambda b, h: (b, h)),
                   pl.BlockSpec((seq, RET_VAL_DIM), lambda b, h: (b, h)),
                   pl.BlockSpec((seq, RET_KEY_DIM), lambda b, h: (b, h)),
                   pl.BlockSpec((seq, RET_KEY_DIM), lambda b, h: (b, h))],
        out_shape=[jax.ShapeDtypeStruct((t, RET_HEADS * RET_VAL_DIM), BF16),
                   jax.ShapeDtypeStruct((t, RET_HEADS * RET_VAL_DIM), F32),
                   jax.ShapeDtypeStruct((t, RET_HEADS * RET_KEY_DIM), BF16),
                   jax.ShapeDtypeStruct((t, RET_HEADS * RET_KEY_DIM), BF16)],
        semantics=("parallel", "parallel"), args=(proj, proj, proj, proj, cs, sn, lg_arr), exchange=exchange)


def _att_bias(w_ref, bias_ref):
    n_i = lax.broadcasted_iota(jnp.int32, (ATT_Q, BIAS_LEN), 0)
    qc = lax.broadcasted_iota(jnp.int32, (ATT_Q, ATT_WIN), 0) >> 6
    kc = lax.broadcasted_iota(jnp.int32, (ATT_Q, ATT_WIN), 1) >> 6
    dc = qc + BAND_CHUNKS - kc
    band = (dc >= 0) & (dc <= BAND_CHUNKS)
    key = lax.broadcasted_iota(jnp.int32, (ATT_Q, ATT_WIN), 1)
    for e in range(2):
        xw = jnp.broadcast_to(w_ref[e:e + 1, :], (ATT_Q, BIAS_LEN))
        for bit in range(8):
            xw = jnp.where(((n_i >> bit) & 1) == 1, pltpu.roll(xw, 1 << bit, 1), xw)
        bias = jnp.where(band, xw[:, BIAS_LEN - ATT_WIN:], NEG_INF)
        for first in range(ATT_STARTS):
            bias_ref[first, e] = jnp.where(key + (first * ATT_Q - ATT_PAD) >= 0, bias, NEG_INF)
        bias_ref[ATT_STARTS, e] = bias


ATT_PAIRS = 2
ATT_COLS = ATT_PAIRS * LANES


def _att_specs(batch, seq):
    ni = seq // ATT_Q
    q_spec = pl.BlockSpec((ATT_Q, ATT_COLS), lambda g, b, i: (b * ni + i, C_AQ // ATT_COLS + g))
    k_spec = pl.BlockSpec((seq, ATT_COLS), lambda g, b, i: (b, C_AK // ATT_COLS + g))
    v_spec = pl.BlockSpec((seq, ATT_COLS), lambda g, b, i: (b, C_AV // ATT_COLS + g))
    w_spec = pl.BlockSpec((ATT_PAIRS, 2, BIAS_LEN), lambda g, b, i: (g, 0, 0))
    b_spec = pl.BlockSpec((ATT_PAIRS, ATT_STARTS + 1, 2, ATT_Q, ATT_WIN), lambda g, b, i: (g, 0, 0, 0, 0))
    pad = pltpu.VMEM((seq + ATT_PAD, ATT_COLS), BF16)
    return ni, q_spec, k_spec, v_spec, w_spec, b_spec, pad


def _att_bias_tiles(wvec, exchange):
    (tiles,), moved = _call(
        lambda w_ref, o_ref: _att_bias(w_ref, o_ref), name="att_bias", grid=(ATT_HEADS // 2,),
        in_specs=[pl.BlockSpec((None, 2, BIAS_LEN), lambda hp: (hp, 0, 0))],
        out_specs=[pl.BlockSpec((None, ATT_STARTS + 1, 2, ATT_Q, ATT_WIN), lambda hp: (hp, 0, 0, 0, 0))],
        out_shape=[jax.ShapeDtypeStruct((ATT_HEADS // 2, ATT_STARTS + 1, 2, ATT_Q, ATT_WIN), F32)],
        semantics=("parallel",), args=(wvec,), exchange=exchange)
    return tiles, moved


def _att_pad(src_ref, pad_ref):
    pad_ref[:ATT_PAD, :] = jnp.zeros((ATT_PAD, ATT_COLS), BF16)
    pad_ref[ATT_PAD:, :] = src_ref[...]


def _att_head(q2, sel):
    return jnp.where(sel, q2, jnp.zeros_like(q2)) * 0.125


def _att_softmax_rows(s_ref, bias_ref, rows):
    s = s_ref[rows, :] + bias_ref[rows, :]
    ex = jnp.exp(s - jnp.max(s, axis=-1, keepdims=True))
    return ex, 1.0 / jnp.sum(ex, axis=-1, keepdims=True)


def _att_fwd(proj, bias, batch, seq, exchange):
    ni, q_spec, k_spec, v_spec, _, b_spec, pad = _att_specs(batch, seq)

    def body(q_ref, k_ref, v_ref, bias_ref, o_ref, kp_ref, vp_ref, s_ref, e_ref):
        i = pl.program_id(2)

        @pl.when(i == 0)
        def _():
            _att_pad(k_ref, kp_ref)
            _att_pad(v_ref, vp_ref)

        win = pl.ds(pl.multiple_of(i * ATT_Q, ATT_Q), ATT_WIN)
        lo = lax.broadcasted_iota(jnp.int32, (1, LANES), 1) < 64
        start = jnp.minimum(i, ATT_STARTS)
        for pair in range(ATT_PAIRS):
            cols = slice(pair * LANES, (pair + 1) * LANES)
            k2, v2, q2 = kp_ref[win, cols], vp_ref[win, cols], q_ref[:, cols]
            out = jnp.zeros((ATT_Q, LANES), F32)
            for e in range(2):
                h = 2 * pair + e
                sel = lo if e == 0 else jnp.logical_not(lo)
                s_ref[h] = _dot_nt(_att_head(q2, sel), k2)
                rsum = []
                for c in range(ATT_Q // ATT_ROWS):
                    rows = slice(c * ATT_ROWS, (c + 1) * ATT_ROWS)
                    ex, r = _att_softmax_rows(s_ref.at[h], bias_ref.at[pair, start, e], rows)
                    e_ref[h, rows, :] = ex.astype(BF16)
                    rsum.append(r)
                out = out + _dot(e_ref[h], jnp.where(sel, v2, jnp.zeros_like(v2))) * jnp.concatenate(rsum, axis=0)
            o_ref[:, cols] = out.astype(BF16)

    heads = 2 * ATT_PAIRS
    return _call(
        body, name="att_fwd", grid=(ATT_HEADS // heads, batch, ni),
        in_specs=[q_spec, k_spec, v_spec, b_spec],
        out_specs=[pl.BlockSpec((ATT_Q, ATT_COLS), lambda g, b, i: (b * ni + i, g))],
        out_shape=[jax.ShapeDtypeStruct((batch * seq, ATT_HEADS * 64), BF16)],
        scratch=[pad, pad, pltpu.VMEM((heads, ATT_Q, ATT_WIN), F32), pltpu.VMEM((heads, ATT_Q, ATT_WIN), BF16)],
        semantics=("arbitrary", "arbitrary", "arbitrary"), args=(proj, proj, proj, bias), exchange=exchange)


GL_HALF = 512


def _gl_specs(tm):
    return [pl.BlockSpec((tm, GL_HALF), lambda i, c=C_GL // GL_HALF + j: (i, c)) for j in range(4)]


def _gates(gl_refs, b_ref):
    logits = [ref[...].astype(F32) for ref in gl_refs]
    gr = _sigmoid(jnp.concatenate(logits[:2], axis=1) + b_ref[:, :D_MODEL])
    ga = _sigmoid(jnp.concatenate(logits[2:], axis=1) + b_ref[:, D_MODEL:])
    return gr, ga


def _whole(a):
    return pl.BlockSpec(a.shape, lambda i: (0,) * a.ndim)


def _mix_out_fwd(gro, ao, proj, b_gate, w_ret, w_att_t, x2, w_out, g2):
    t = gro.shape[0]
    tm = min(256, t)

    def body(gro_ref, ao_ref, gl0, gl1, gl2, gl3, b_ref, wr_ref, wa_ref, x_ref, wo_ref, g_ref,
             z_ref, yr_ref, ya_ref, h_ref, hn_ref):
        yr = _dot(gro_ref[...], wr_ref[...])
        ya = _dot_nt(ao_ref[...], wa_ref[...])
        yr_ref[...] = yr.astype(BF16)
        ya_ref[...] = ya.astype(BF16)
        gr, ga = _gates((gl0, gl1, gl2, gl3), b_ref)
        z = (gr * yr + ga * ya).astype(BF16)
        z_ref[...] = z
        h = x_ref[...] + _dot(z, wo_ref[...])
        h_ref[...] = h
        r = lax.rsqrt(jnp.mean(h * h, axis=-1, keepdims=True) + EPS)
        hn_ref[...] = (h * r * g_ref[...]).astype(BF16)

    row = pl.BlockSpec((tm, D_MODEL), lambda i: (i, 0))
    return pl.pallas_call(
        body, name="mix_out_fwd", grid=(t // tm,),
        in_specs=[row, pl.BlockSpec((tm, 512), lambda i: (i, 0)), *_gl_specs(tm),
                  _whole(b_gate), _whole(w_ret), _whole(w_att_t), row, _whole(w_out), _whole(g2)],
        out_specs=[row] * 5,
        out_shape=[jax.ShapeDtypeStruct((t, D_MODEL), BF16)] * 3
                  + [jax.ShapeDtypeStruct((t, D_MODEL), F32), jax.ShapeDtypeStruct((t, D_MODEL), BF16)],
        compiler_params=_params("parallel"),
    )(gro, ao, proj, proj, proj, proj, b_gate, w_ret, w_att_t, x2, w_out, g2)


def _ffn_up(hn, wg_t, wu_t):
    t = hn.shape[0]
    tm, tn = min(512, t), D_FF // 2

    def body(h_ref, wg_ref, wu_ref, g_ref, u_ref, a_ref):
        g = _dot_nt(h_ref[...], wg_ref[...])
        u = _dot_nt(h_ref[...], wu_ref[...])
        g_ref[...] = g.astype(BF16)
        u_ref[...] = u.astype(BF16)
        a_ref[...] = (g * _sigmoid(g) * u).astype(BF16)

    w_spec = pl.BlockSpec((tn, D_MODEL), lambda j, i: (j, 0))
    out = pl.BlockSpec((tm, tn), lambda j, i: (i, j))
    return pl.pallas_call(
        body, name="ffn_up", grid=(D_FF // tn, t // tm),
        in_specs=[pl.BlockSpec((tm, D_MODEL), lambda j, i: (i, 0)), w_spec, w_spec],
        out_specs=[out, out, out],
        out_shape=[jax.ShapeDtypeStruct((t, D_FF), BF16)] * 3,
        compiler_params=_params("parallel", "parallel"),
    )(hn, wg_t, wu_t)


def _ffn_down_loss(a, h1, tgt, w_down, g3):
    t = a.shape[0]
    tm = min(512, t)

    def body(a_ref, h_ref, t_ref, w_ref, g_ref, dh_ref, dhb_ref, loss_ref, dg_ref):
        @pl.when(pl.program_id(0) == 0)
        def _():
            loss_ref[...] = jnp.zeros_like(loss_ref)
            dg_ref[...] = jnp.zeros_like(dg_ref)

        g = g_ref[...]
        h2 = h_ref[...] + _dot(a_ref[...], w_ref[...])
        r = lax.rsqrt(jnp.mean(h2 * h2, axis=-1, keepdims=True) + EPS)
        err = h2 * r * g - t_ref[...]
        loss_ref[...] += jnp.sum(err * err) * (0.5 / D_MODEL)
        dy = err * (1.0 / D_MODEL)
        dh, dg_rows = _rms_bwd(h2, g, dy)
        dg_ref[...] += jnp.sum(dg_rows, axis=0, keepdims=True)
        dh_ref[...] = dh
        dhb_ref[...] = dh.astype(BF16)

    row = pl.BlockSpec((tm, D_MODEL), lambda i: (i, 0))
    vec = pl.BlockSpec((1, D_MODEL), lambda i: (0, 0))
    return pl.pallas_call(
        body, name="ffn_down_loss", grid=(t // tm,),
        in_specs=[pl.BlockSpec((tm, D_FF), lambda i: (i, 0)), row, row,
                  pl.BlockSpec((D_FF, D_MODEL), lambda i: (0, 0)), vec],
        out_specs=[row, row, pl.BlockSpec((1, LANES), lambda i: (0, 0)), vec],
        out_shape=[jax.ShapeDtypeStruct((t, D_MODEL), F32), jax.ShapeDtypeStruct((t, D_MODEL), BF16),
                   jax.ShapeDtypeStruct((1, LANES), F32), jax.ShapeDtypeStruct((1, D_MODEL), F32)],
        compiler_params=_params("arbitrary"),
    )(a, h1, tgt, w_down, g3)


def _ffn_bwd_act(dh2b, w_down, g_act, u_act, exchange):
    t = dh2b.shape[0]
    tm, tn = min(512, t), D_FF // 2

    def body(d_ref, w_ref, g_ref, u_ref, dg_ref, du_ref):
        da = _dot_nt(d_ref[...], w_ref[...])
        g = g_ref[...].astype(F32)
        u = u_ref[...].astype(F32)
        sg = _sigmoid(g)
        dg_ref[...] = (da * u * sg * (1.0 + g * (1.0 - sg))).astype(BF16)
        du_ref[...] = (da * g * sg).astype(BF16)

    blk = pl.BlockSpec((tm, tn), lambda j, i: (i, j))
    return _call(
        body, name="ffn_bwd_act", grid=(D_FF // tn, t // tm),
        in_specs=[pl.BlockSpec((tm, D_MODEL), lambda j, i: (i, 0)),
                  pl.BlockSpec((tn, D_MODEL), lambda j, i: (j, 0)), blk, blk],
        out_specs=[blk, blk],
        out_shape=[jax.ShapeDtypeStruct((t, D_FF), BF16)] * 2,
        semantics=("parallel", "parallel"), args=(dh2b, w_down, g_act, u_act), exchange=exchange)


def _ffn_bwd_in(dg, du, wg_t, wu_t, h1, dh2, g2, exchange):
    t = dg.shape[0]
    tm, tk = min(512, t), D_FF // 2
    nk = D_FF // tk

    def body(dg_ref, du_ref, wg_ref, wu_ref, h_ref, d2_ref, g_ref, dh_ref, dhb_ref, gn_ref, acc_ref):
        i, kk = pl.program_id(0), pl.program_id(1)

        @pl.when((i == 0) & (kk == 0))
        def _():
            gn_ref[...] = jnp.zeros_like(gn_ref)

        @pl.when(kk == 0)
        def _():
            acc_ref[...] = jnp.zeros_like(acc_ref)

        acc_ref[...] += _dot(dg_ref[...], wg_ref[...]) + _dot(du_ref[...], wu_ref[...])

        @pl.when(kk == nk - 1)
        def _():
            dx, dg_rows = _rms_bwd(h_ref[...], g_ref[...], acc_ref[...])
            dh = d2_ref[...] + dx
            dh_ref[...] = dh
            dhb_ref[...] = dh.astype(BF16)
            gn_ref[...] += jnp.sum(dg_rows, axis=0, keepdims=True)

    act = pl.BlockSpec((tm, tk), lambda i, kk: (i, kk))
    wsp = pl.BlockSpec((tk, D_MODEL), lambda i, kk: (kk, 0))
    row = pl.BlockSpec((tm, D_MODEL), lambda i, kk: (i, 0))
    vec = pl.BlockSpec((1, D_MODEL), lambda i, kk: (0, 0))
    return _call(
        body, name="ffn_bwd_in", grid=(t // tm, nk),
        in_specs=[act, act, wsp, wsp, row, row, vec],
        out_specs=[row, row, vec],
        out_shape=[jax.ShapeDtypeStruct((t, D_MODEL), F32), jax.ShapeDtypeStruct((t, D_MODEL), BF16),
                   jax.ShapeDtypeStruct((1, D_MODEL), F32)],
        scratch=[pltpu.VMEM((tm, D_MODEL), F32)],
        semantics=("arbitrary", "arbitrary"), args=(dg, du, wg_t, wu_t, h1, dh2, g2), exchange=exchange)


def _mix_bwd(dh1b, w_out, proj, b_gate, y_ret, y_att, w_ret, w_att_t):
    t = dh1b.shape[0]
    tm = min(256, t)

    def body(d_ref, wo_ref, gl0, gl1, gl2, gl3, b_ref, yr_ref, ya_ref, wr_ref, wa_ref,
             dyr_ref, dya_ref, dglr_ref, dgla_ref, dgro_ref, dao_ref, db_ref):
        @pl.when(pl.program_id(0) == 0)
        def _():
            db_ref[...] = jnp.zeros_like(db_ref)

        dz = _dot_nt(d_ref[...], wo_ref[...])
        gr, ga = _gates((gl0, gl1, gl2, gl3), b_ref)
        dyr = (dz * gr).astype(BF16)
        dya = (dz * ga).astype(BF16)
        dyr_ref[...] = dyr
        dya_ref[...] = dya
        dglr = dz * yr_ref[...].astype(F32) * gr * (1.0 - gr)
        dgla = dz * ya_ref[...].astype(F32) * ga * (1.0 - ga)
        dglr_ref[...] = dglr.astype(BF16)
        dgla_ref[...] = dgla.astype(BF16)
        db_ref[:, :D_MODEL] += jnp.sum(dglr, axis=0, keepdims=True)
        db_ref[:, D_MODEL:] += jnp.sum(dgla, axis=0, keepdims=True)
        dgro_ref[...] = _dot_nt(dyr, wr_ref[...]).astype(BF16)
        dao_ref[...] = _dot(dya, wa_ref[...]).astype(BF16)

    row = pl.BlockSpec((tm, D_MODEL), lambda i: (i, 0))
    half = pl.BlockSpec((tm, 512), lambda i: (i, 0))
    return pl.pallas_call(
        body, name="mix_bwd", grid=(t // tm,),
        in_specs=[row, _whole(w_out), *_gl_specs(tm), _whole(b_gate), row, row, _whole(w_ret), _whole(w_att_t)],
        out_specs=[row, row, row, row, row, half, _whole(b_gate)],
        out_shape=[jax.ShapeDtypeStruct((t, D_MODEL), BF16)] * 5
                  + [jax.ShapeDtypeStruct((t, 512), BF16), jax.ShapeDtypeStruct(b_gate.shape, F32)],
        compiler_params=_params("arbitrary"),
    )(dh1b, w_out, proj, proj, proj, proj, b_gate, y_ret, y_att, w_ret, w_att_t)


def _ret_bwd(dgro, proj, o_ret, qr, kr, cs, sn, lg_arr, batch, seq, exchange):
    t = batch * seq
    nt = seq // RET_TILE

    def body(dgro_ref, rg_ref, o_ref, qr_ref, kr_ref, v_ref, cs_ref, sn_ref, lg_ref,
             dq_ref, dk_ref, dv_ref, drg_ref, do_ref, st_ref):
        lg = lg_ref[:, 0:1]
        inside, q_dec, k_dec, tile_dec = _decay(lg)

        state = jnp.zeros((RET_KEY_DIM, RET_VAL_DIM), F32)
        for i in range(nt - 1):
            rows = slice(i * RET_TILE, (i + 1) * RET_TILE)
            state = state * tile_dec + _dot_tn(_scaled(kr_ref[rows, :], k_dec), v_ref[rows, :])
            st_ref[i + 1] = state.astype(BF16)

        for i in range(nt):
            rows = slice(i * RET_TILE, (i + 1) * RET_TILE)
            o = o_ref[rows, :]
            xc = o - jnp.mean(o, axis=-1, keepdims=True)
            rs = lax.rsqrt(jnp.mean(xc * xc, axis=-1, keepdims=True) + EPS)
            nrm = xc * rs
            rg = rg_ref[rows, :].astype(F32)
            sg = _sigmoid(rg)
            dg = dgro_ref[rows, :].astype(F32)
            drg_ref[rows, :] = (dg * nrm * sg * (1.0 + rg * (1.0 - sg))).astype(BF16)
            dn = dg * rg * sg
            do = rs * (dn - jnp.mean(dn, axis=-1, keepdims=True)
                       - nrm * jnp.mean(dn * nrm, axis=-1, keepdims=True))
            do_ref[rows, :] = do.astype(BF16)

        dstate = jnp.zeros((RET_KEY_DIM, RET_VAL_DIM), F32)
        for i in reversed(range(nt)):
            rows = slice(i * RET_TILE, (i + 1) * RET_TILE)
            qi, ki, vi, doi = qr_ref[rows, :], kr_ref[rows, :], v_ref[rows, :], do_ref[rows, :]
            p = (_dot_nt(qi, ki) * inside).astype(BF16)
            dp = (_dot_nt(doi, vi) * inside).astype(BF16)
            dq = _dot(dp, ki)
            dk = _dot_tn(dp, qi)
            dv = _dot_tn(p, doi)
            if i > 0:
                dq = dq + _dot_nt(doi, st_ref[i]) * q_dec
            if i < nt - 1:
                dsb = dstate.astype(BF16)
                dk = dk + _dot_nt(vi, dsb) * k_dec
                dv = dv + _dot(_scaled(ki, k_dec), dsb)
            if i > 0:
                dstate = dstate * tile_dec + _dot_tn(_scaled(qi, q_dec), doi)
            dq_ref[rows, :] = (dq * cs_ref[rows, :] - pltpu.roll(dq, 64, 1) * sn_ref[rows, :]).astype(BF16)
            dk = (dk * cs_ref[rows, :] - pltpu.roll(dk, 64, 1) * sn_ref[rows, :]) * (RET_KEY_DIM ** -0.5)
            dk_ref[rows, :] = dk.astype(BF16)
            dv_ref[rows, :] = dv.astype(BF16)

    key = pl.BlockSpec((seq, RET_KEY_DIM), lambda b, h: (b, h))
    val = pl.BlockSpec((seq, RET_VAL_DIM), lambda b, h: (b, h))
    tab = pl.BlockSpec((seq, RET_KEY_DIM), lambda b, h: (0, 0))
    return _call(
        body, name="ret_bwd", grid=(batch, RET_HEADS),
        in_specs=[val, pl.BlockSpec((seq, RET_VAL_DIM), lambda b, h: (b, C_RG // RET_VAL_DIM + h)), val, key, key,
                  pl.BlockSpec((seq, RET_VAL_DIM), lambda b, h: (b, C_RV // RET_VAL_DIM + h)), tab, tab,
                  pl.BlockSpec((None, 1, LANES), lambda b, h: (h, 0, 0))],
        out_specs=[key, key, val, val],
        out_shape=[jax.ShapeDtypeStruct((t, RET_HEADS * RET_KEY_DIM), BF16)] * 2
                  + [jax.ShapeDtypeStruct((t, RET_HEADS * RET_VAL_DIM), BF16)] * 2,
        scratch=[pltpu.VMEM((seq, RET_VAL_DIM), BF16), pltpu.VMEM((nt, RET_KEY_DIM, RET_VAL_DIM), BF16)],
        semantics=("parallel", "parallel"), args=(dgro, proj, o_ret, qr, kr, proj, cs, sn, lg_arr),
        exchange=exchange)


def _att_bwd(proj, bias, dao, batch, seq, exchange):
    ni, q_spec, k_spec, v_spec, w_spec, b_spec, pad = _att_specs(batch, seq)
    t = batch * seq

    def body(q_ref, k_ref, v_ref, bias_ref, do_ref, dq_ref, dk_ref, dv_ref, dw_ref,
             dbias_ref, dk_acc, dv_acc, kp_ref, vp_ref, s_ref, dp_ref, e_ref, ds_ref):
        b, i = pl.program_id(1), pl.program_id(2)

        @pl.when((b == 0) & (i == 0))
        def _():
            dbias_ref[...] = jnp.zeros_like(dbias_ref)

        @pl.when(i == 0)
        def _():
            _att_pad(k_ref, kp_ref)
            _att_pad(v_ref, vp_ref)
            dk_acc[...] = jnp.zeros_like(dk_acc)
            dv_acc[...] = jnp.zeros_like(dv_acc)

        win = pl.ds(pl.multiple_of(i * ATT_Q, ATT_Q), ATT_WIN)
        lo = lax.broadcasted_iota(jnp.int32, (1, LANES), 1) < 64
        start = jnp.minimum(i, ATT_STARTS)
        for pair in range(ATT_PAIRS):
            cols = slice(pair * LANES, (pair + 1) * LANES)
            k2, v2, q2, do2 = kp_ref[win, cols], vp_ref[win, cols], q_ref[:, cols], do_ref[:, cols]
            dq = jnp.zeros((ATT_Q, LANES), F32)
            dk = jnp.zeros((LANES, ATT_WIN), F32)
            dv = jnp.zeros((LANES, ATT_WIN), F32)
            for e in range(2):
                h = 2 * pair + e
                sel = lo if e == 0 else jnp.logical_not(lo)
                qm = _att_head(q2, sel)
                dom = jnp.where(sel, do2, jnp.zeros_like(do2))
                s_ref[h] = _dot_nt(qm, k2)
                dp_ref[h] = _dot_nt(dom, v2)
                rsum = []
                for c in range(ATT_Q // ATT_ROWS):
                    rows = slice(c * ATT_ROWS, (c + 1) * ATT_ROWS)
                    ex, r = _att_softmax_rows(s_ref.at[h], bias_ref.at[pair, start, e], rows)
                    dp = dp_ref[h, rows, :]
                    mean = jnp.sum(dp * ex, axis=-1, keepdims=True) * r
                    ds = ex * ((dp - mean) * r)
                    dbias_ref[h, rows, :] += ds
                    ds_ref[h, rows, :] = ds.astype(BF16)
                    e_ref[h, rows, :] = ex.astype(BF16)
                    rsum.append(r)
                dq = dq + _dot(ds_ref[h], jnp.where(sel, k2, jnp.zeros_like(k2)))
                dk = dk + _dot_tn(qm, ds_ref[h])
                dv = dv + _dot_tn((dom.astype(F32) * jnp.concatenate(rsum, axis=0)).astype(BF16), e_ref[h])
            dq_ref[:, cols] = (dq * 0.125).astype(BF16)
            dk_acc[cols, win] += dk
            dv_acc[cols, win] += dv

        @pl.when(i == ni - 1)
        def _():
            dk_ref[...] = dk_acc[:, ATT_PAD:].T.astype(BF16)
            dv_ref[...] = dv_acc[:, ATT_PAD:].T.astype(BF16)

        @pl.when((b == batch - 1) & (i == ni - 1))
        def _():
            n_i = lax.broadcasted_iota(jnp.int32, (ATT_Q, BIAS_LEN), 0)
            for h in range(heads):
                xw = jnp.concatenate([jnp.zeros((ATT_Q, BIAS_LEN - ATT_WIN), F32), dbias_ref[h]], axis=1)
                for bit in range(8):
                    xw = jnp.where(((n_i >> bit) & 1) == 1, pltpu.roll(xw, BIAS_LEN - (1 << bit), 1), xw)
                dw_ref[h // 2, h % 2:h % 2 + 1, :] = jnp.sum(xw, axis=0, keepdims=True)

    heads = 2 * ATT_PAIRS
    seq_blk = pl.BlockSpec((seq, ATT_COLS), lambda g, b, i: (b, g))
    q_out = pl.BlockSpec((ATT_Q, ATT_COLS), lambda g, b, i: (b * ni + i, g))
    tile_f32, tile_bf16 = pltpu.VMEM((heads, ATT_Q, ATT_WIN), F32), pltpu.VMEM((heads, ATT_Q, ATT_WIN), BF16)
    acc = pltpu.VMEM((ATT_COLS, seq + ATT_PAD), F32)
    return _call(
        body, name="att_bwd", grid=(ATT_HEADS // heads, batch, ni),
        in_specs=[q_spec, k_spec, v_spec, b_spec, q_out],
        out_specs=[q_out, seq_blk, seq_blk, w_spec],
        out_shape=[jax.ShapeDtypeStruct((t, 512), BF16)] * 3
                  + [jax.ShapeDtypeStruct((ATT_HEADS // 2, 2, BIAS_LEN), F32)],
        scratch=[tile_f32, acc, acc, pad, pad, tile_f32, tile_f32, tile_bf16, tile_bf16],
        semantics=("arbitrary", "arbitrary", "arbitrary"), args=(proj, proj, proj, bias, dao), exchange=exchange,
        vmem=VMEM_LIMIT_ATT_BWD)


def _rms_in_bwd(x2, dxn, dh1, g1):
    t = x2.shape[0]
    tm = min(512, t)

    def body(x_ref, d_ref, h_ref, g_ref, dx_ref, dg_ref):
        @pl.when(pl.program_id(0) == 0)
        def _():
            dg_ref[...] = jnp.zeros_like(dg_ref)

        dx, dg_rows = _rms_bwd(x_ref[...], g_ref[...], d_ref[...])
        dx_ref[...] = h_ref[...] + dx
        dg_ref[...] += jnp.sum(dg_rows, axis=0, keepdims=True)

    row = pl.BlockSpec((tm, D_MODEL), lambda i: (i, 0))
    vec = pl.BlockSpec((1, D_MODEL), lambda i: (0, 0))
    return pl.pallas_call(
        body, name="rms_in_bwd", grid=(t // tm,),
        in_specs=[row, row, row, vec], out_specs=[row, vec],
        out_shape=[jax.ShapeDtypeStruct((t, D_MODEL), F32), jax.ShapeDtypeStruct((1, D_MODEL), F32)],
        compiler_params=_params("arbitrary"),
    )(x2, dxn, dh1, g1)


def _pack_small(dg1, dbr, dba, dg2, dg3, dw, loss):
    def body(a_ref, b_ref, c_ref, d_ref, e_ref, w_ref, l_ref, o_ref):
        o_ref[...] = jnp.zeros_like(o_ref)
        for r, ref in enumerate((a_ref, b_ref, c_ref, d_ref, e_ref)):
            o_ref[r:r + 1, :] = ref[...]
        o_ref[5:6, 0:LANES] = l_ref[...]
        for hp in range(ATT_HEADS // 2):
            o_ref[8 + 2 * hp:10 + 2 * hp, :] = w_ref[hp]

    return pl.pallas_call(body, name="pack_small",
                          out_shape=jax.ShapeDtypeStruct((16, D_MODEL), F32))(dg1, dbr, dba, dg2, dg3, dw, loss)


def _rotary_tables(seq):
    freqs = ROPE_BASE ** (-jnp.arange(0, RET_KEY_DIM, 2, dtype=F32) / RET_KEY_DIM)
    ang = jnp.arange(seq, dtype=F32)[:, None] * freqs[None, :]
    cos, sin = jnp.cos(ang), jnp.sin(ang)
    return jnp.concatenate([cos, cos], axis=1), jnp.concatenate([-sin, sin], axis=1)


def _bias_rows(rel_bias):
    n_far = BIAS_LEN - ATT_Q - MAX_REL + 1
    n_near = BIAS_LEN - n_far - (N_REL - 2)
    w = jnp.concatenate([jnp.broadcast_to(rel_bias[:, N_REL - 1:], (ATT_HEADS, n_far)),
                         rel_bias[:, 1:N_REL - 1][:, ::-1],
                         jnp.broadcast_to(rel_bias[:, :1], (ATT_HEADS, n_near))], axis=1)
    return w.reshape(ATT_HEADS // 2, 2, BIAS_LEN)


def _bias_rows_bwd(dw):
    n_far = BIAS_LEN - ATT_Q - MAX_REL + 1
    mid = dw[:, n_far:n_far + N_REL - 2][:, ::-1]
    return jnp.concatenate([jnp.sum(dw[:, n_far + N_REL - 2:], axis=1, keepdims=True), mid,
                            jnp.sum(dw[:, :n_far], axis=1, keepdims=True)], axis=1)


def _step(x, tgt, norm_mix, b_gate, norm_ffn, norm_final, rel_bias_shard, shard):
    batch, seq, _ = x.shape
    t = batch * seq
    n_rb = rel_bias_shard.shape[-1]
    x2, tgt2 = x.reshape(t, D_MODEL), tgt.reshape(t, D_MODEL)
    g3 = norm_final.reshape(1, D_MODEL)
    cs, sn = _rotary_tables(seq)
    lg = np.log(1.0 - 2.0 ** (-5.0 - np.arange(RET_HEADS, dtype=np.float32))).astype(np.float32)
    lg_arr = jnp.asarray(np.broadcast_to(lg[:, None, None], (RET_HEADS, 1, LANES)))

    def gather(*names):
        return _ChipGather([shard[nm] for nm in names])

    def scatter(*grads):
        return _Exchange(grads, scatter=True)

    rb_pad = jnp.pad(rel_bias_shard, ((0, 0), (0, LANES - n_rb)))
    (xn,), (w_in_half, rb_full) = _rms_fwd(x2, norm_mix,
                                           _ChipGather([shard["w_in_t"], rb_pad], parts=[(0, 2), (0, 1)]))
    rb_full = rb_full.reshape(N_DEV, ATT_HEADS, LANES)[:, :, :n_rb]
    bias, (w_in_t,) = _att_bias_tiles(
        _bias_rows(jnp.transpose(rb_full, (1, 0, 2)).reshape(ATT_HEADS, N_DEV * n_rb)),
        _ChipGather([shard["w_in_t"]], parts=[(1, 2)], into=[w_in_half]))
    proj, (w_ret, w_att_t, w_out, w_gate_t) = _mm(
        xn, w_in_t, tb=True, out_dtype=BF16, tm=1024, tn=1664, tk=1024, name="proj",
        exchange=gather("w_ret", "w_att_t", "w_out", "w_gate_t"))
    (gro, o_ret, qr, kr), _ = _ret_fwd(proj, cs, sn, lg_arr, batch, seq, None)
    (ao,), (w_up_t, w_down) = _att_fwd(proj, bias, batch, seq, gather("w_up_t", "w_down"))
    z, y_ret, y_att, h1, hn = _mix_out_fwd(gro, ao, proj, b_gate, w_ret, w_att_t, x2, w_out, norm_ffn)
    g_act, u_act, a_act = _ffn_up(hn, w_gate_t, w_up_t)
    dh2, dh2b, loss, dg3 = _ffn_down_loss(a_act, h1, tgt2, w_down, g3)

    wg = dict(out_dtype=BF16, tn=1024, ta=True)
    slots = {}
    dw_down = _mm(a_act, dh2b, tm=1408, tk=1024, name="dw_down", **wg)
    (d_gact, d_uact), _ = _ffn_bwd_act(dh2b, w_down, g_act, u_act, None)
    dw_gate = _mm(d_gact, hn, tm=1408, tk=1024, name="dw_gate", **wg)
    dw_up = _mm(d_uact, hn, tm=1408, tk=1024, name="dw_up", **wg)
    (dh1, dh1b, dg2), (slots["w_down"],) = _ffn_bwd_in(d_gact, d_uact, w_gate_t, w_up_t, h1, dh2, norm_ffn,
                                                     scatter(dw_down))
    dw_out = _mm(z, dh1b, tm=1024, tk=2048, name="dw_out", **wg)
    dyr, dya, dglr, dgla, dgro, dao, db = _mix_bwd(dh1b, w_out, proj, b_gate, y_ret, y_att, w_ret, w_att_t)
    dw_ret = _mm(gro, dyr, tm=1024, tk=2048, name="dw_ret", **wg)
    dw_att = _mm(dya, ao, tm=1024, tk=2048, name="dw_att", **wg)
    (drq, drk, drv, drg), _ = _ret_bwd(dgro, proj, o_ret, qr, kr, cs, sn, lg_arr, batch, seq, None)
    (daq, dak, dav, dw), (slots["w_gate_t"], slots["w_out"], slots["w_ret"], slots["w_att_t"]) = _att_bwd(
        proj, bias, dao, batch, seq, scatter(dw_gate, dw_out, dw_ret, dw_att))
    dproj = [drq, drk, drv, drg, daq, dak, dav, dglr, dgla]
    dw_in, (slots["w_up_t"],) = _mm_pieces(dproj, xn, ta=True, out_dtype=BF16, tm=512, tn=1024, tk=1024,
                                           name="dw_in", exchange=scatter(dw_up))
    (dw_in_sibling,) = _alone(_PairSwap([dw_in]), "swap_w_in")
    dw_in_pairs = _pair_add(dw_in, dw_in_sibling, "pair_w_in")
    dxn, (slots["w_in_t"],) = _mm_pieces(dproj, w_in_t, ta=False, out_dtype=F32, tm=1024, tn=1024, tk=512, name="dxn",
                                  exchange=_ChipScatter([dw_in_pairs]))
    dx, dg1 = _rms_in_bwd(x2, dxn, dh1, norm_mix)
    small = _pack_small(dg1, db[:, :D_MODEL], db[:, D_MODEL:], dg2, dg3, dw, loss)
    (small_slots,) = _alone(_ChipGather([small]), "gather_small")
    return dx.reshape(batch, seq, D_MODEL), slots, small_slots.reshape(N_DEV, 16, D_MODEL)


def _row_tile(r, c):
    return max(d for d in range(16, r + 1, 16) if r % d == 0 and (d * c <= 256 * 1024 or d == 16))


def _pair_add(grad, got, name):
    _, r, c = got.shape
    tr = r
    core = lax.axis_index("c").astype(jnp.int32).reshape(1)

    def body(core_ref, g_ref, a_ref, o_ref):
        o_ref[...] = (g_ref[...].astype(F32) + a_ref[...].astype(F32)).astype(o_ref.dtype)

    blk = pl.BlockSpec((None, tr, c), lambda q, i, core_ref: (q, i, 0))
    return pl.pallas_call(
        body, name=name,
        grid_spec=pltpu.PrefetchScalarGridSpec(
            num_scalar_prefetch=1, grid=(4, r // tr),
            in_specs=[pl.BlockSpec((None, None, tr, c), lambda q, i, core_ref: (q, core_ref[0], i, 0)), blk],
            out_specs=blk),
        out_shape=jax.ShapeDtypeStruct(got.shape, got.dtype),
        compiler_params=_params("parallel", "parallel"),
    )(core, grad.reshape(4, 2, r, c), got)


def _sum_slots(slots, name):
    n, r, c = slots.shape
    tr = _row_tile(r, c)

    def body(s_ref, o_ref):
        acc = s_ref[0].astype(F32)
        for s in range(1, n):
            acc = acc + s_ref[s].astype(F32)
        o_ref[...] = acc

    return pl.pallas_call(
        body, name=name, grid=(r // tr,),
        in_specs=[pl.BlockSpec((n, tr, c), lambda i: (0, i, 0))],
        out_specs=pl.BlockSpec((tr, c), lambda i: (i, 0)),
        out_shape=jax.ShapeDtypeStruct((r, c), F32),
        compiler_params=_params("parallel"),
    )(slots)


def _adamw_math(w, g, m, v):
    m = ADAM_B1 * m + (1.0 - ADAM_B1) * g
    v = ADAM_B2 * v + (1.0 - ADAM_B2) * (g * g)
    m_hat = m / (1.0 - ADAM_B1 ** ADAM_STEP)
    v_hat = v / (1.0 - ADAM_B2 ** ADAM_STEP)
    return -ADAM_LR * (m_hat / (jnp.sqrt(v_hat) + ADAM_EPS) + ADAM_WD * w), m, v


def _adamw(w, slots, m, v, name):
    n, r, c = slots.shape
    tr = _row_tile(r, c)

    def body(w_ref, s_ref, m_ref, v_ref, g_ref, d_ref, nm_ref, nv_ref):
        g = s_ref[0].astype(F32)
        for s in range(1, n):
            g = g + s_ref[s].astype(F32)
        g_ref[...] = g
        d_ref[...], nm_ref[...], nv_ref[...] = _adamw_math(w_ref[...], g, m_ref[...], v_ref[...])

    blk = pl.BlockSpec((tr, c), lambda i: (i, 0))
    return pl.pallas_call(
        body, name=name, grid=(r // tr,),
        in_specs=[blk, pl.BlockSpec((n, tr, c), lambda i: (0, i, 0)), blk, blk], out_specs=[blk] * 4,
        out_shape=[jax.ShapeDtypeStruct((r, c), F32)] * 4,
        compiler_params=_params("parallel"),
    )(w, slots, m, v)


def _adamw_small(ws, gs, ms, vs):
    n = len(ws)

    def body(*refs):
        for i in range(n):
            w_ref, g_ref, m_ref, v_ref = (refs[j * n + i] for j in range(4))
            d_ref, nm_ref, nv_ref = (refs[(4 + j) * n + i] for j in range(3))
            d_ref[...], nm_ref[...], nv_ref[...] = _adamw_math(w_ref[...], g_ref[...], m_ref[...], v_ref[...])

    shapes = [jax.ShapeDtypeStruct(w.shape, F32) for w in ws]
    outs = pl.pallas_call(body, name="adamw_small", out_shape=shapes * 3)(*ws, *gs, *ms, *vs)
    return outs[:n], outs[n:2 * n], outs[2 * n:]


def kernel(x, norm_mix, w_in, b_gate, rel_bias, w_ret_out, w_att_out, w_out, norm_ffn, w_ffn_gate, w_ffn_up, w_ffn_down, norm_final, loss_target, m_norm_mix, m_w_in, m_b_gate, m_rel_bias, m_w_ret_out, m_w_att_out, m_w_out, m_norm_ffn, m_w_ffn_gate, m_w_ffn_up, m_w_ffn_down, m_norm_final, v_norm_mix, v_w_in, v_b_gate, v_rel_bias, v_w_ret_out, v_w_att_out, v_w_out, v_norm_ffn, v_w_ffn_gate, v_w_ffn_up, v_w_ffn_down, v_norm_final):
    me = _index(_place())
    n_rb = rel_bias.shape[-1]

    shard = dict(w_in_t=w_in[0].T, w_gate_t=w_ffn_gate[0].T, w_up_t=w_ffn_up[0].T, w_down=w_ffn_down[0],
                 w_ret=w_ret_out[0], w_out=w_out[0], w_att_t=w_att_out[0].T)
    shard = {nm: s.astype(BF16) for nm, s in shard.items()}
    dx, slots, small_slots = _step(x, loss_target, norm_mix, b_gate, norm_ffn, norm_final, rel_bias[0], shard)
    small_sum = _sum_slots(small_slots, "sum_small")
    loss = small_sum[5, 0]

    transposed = dict(w_in="w_in_t", w_ffn_gate="w_gate_t", w_ffn_up="w_up_t", w_att_out="w_att_t")
    plain = dict(w_ffn_down="w_down", w_ret_out="w_ret", w_out="w_out")
    g = dict(
        norm_mix=small_sum[0:1], b_gate=jnp.concatenate([small_sum[1:2], small_sum[2:3]], axis=1),
        norm_ffn=small_sum[3:4], norm_final=small_sum[4:5],
        rel_bias=lax.dynamic_slice_in_dim(_bias_rows_bwd(small_sum[8:16]), me * n_rb, n_rb, axis=1),
    )
    w = dict(norm_mix=norm_mix, w_in=w_in, b_gate=b_gate, rel_bias=rel_bias, w_ret_out=w_ret_out, w_att_out=w_att_out,
             w_out=w_out, norm_ffn=norm_ffn, w_ffn_gate=w_ffn_gate, w_ffn_up=w_ffn_up, w_ffn_down=w_ffn_down,
             norm_final=norm_final)
    m = dict(norm_mix=m_norm_mix, w_in=m_w_in, b_gate=m_b_gate, rel_bias=m_rel_bias, w_ret_out=m_w_ret_out,
             w_att_out=m_w_att_out, w_out=m_w_out, norm_ffn=m_norm_ffn, w_ffn_gate=m_w_ffn_gate, w_ffn_up=m_w_ffn_up,
             w_ffn_down=m_w_ffn_down, norm_final=m_norm_final)
    v = dict(norm_mix=v_norm_mix, w_in=v_w_in, b_gate=v_b_gate, rel_bias=v_rel_bias, w_ret_out=v_w_ret_out,
             w_att_out=v_w_att_out, w_out=v_w_out, norm_ffn=v_norm_ffn, w_ffn_gate=v_w_ffn_gate, w_ffn_up=v_w_ffn_up,
             w_ffn_down=v_w_ffn_down, norm_final=v_norm_final)
    order = ("norm_mix", "w_in", "b_gate", "rel_bias", "w_ret_out", "w_att_out", "w_out", "norm_ffn",
             "w_ffn_gate", "w_ffn_up", "w_ffn_down", "norm_final")
    small_names = ("norm_mix", "b_gate", "rel_bias", "norm_ffn", "norm_final")

    def flat(a):
        return a[0] if a.ndim == 3 else a.reshape(-1, a.shape[-1])

    grad, delta, new_m, new_v = {}, {}, {}, {}
    for nm in order:
        if nm in transposed:
            res = _adamw(w[nm][0].T, slots[transposed[nm]], m[nm][0].T, v[nm][0].T, "adamw_" + nm)
            grad[nm], delta[nm], new_m[nm], new_v[nm] = (a.T[None] for a in res)
        elif nm in plain:
            res = _adamw(flat(w[nm]), slots[plain[nm]], flat(m[nm]), flat(v[nm]), "adamw_" + nm)
            grad[nm], delta[nm], new_m[nm], new_v[nm] = (a.reshape(w[nm].shape) for a in res)
    ds, nms, nvs = _adamw_small([flat(w[nm]) for nm in small_names], [g[nm] for nm in small_names],
                                [flat(m[nm]) for nm in small_names], [flat(v[nm]) for nm in small_names])
    for i, nm in enumerate(small_names):
        grad[nm], delta[nm], new_m[nm], new_v[nm] = (a.reshape(w[nm].shape) for a in (g[nm], ds[i], nms[i], nvs[i]))

    return (loss, dx, *[grad[nm] for nm in order], *[delta[nm] for nm in order],
            *[new_m[nm] for nm in order], *[new_v[nm] for nm in order])
```

```python
import numpy as np
import jax
import jax.numpy as jnp
from jax import lax
from jax.experimental import pallas as pl
from jax.experimental.pallas import tpu as pltpu

F32 = jnp.float32
BF16 = jnp.bfloat16
MESH = pl.DeviceIdType.MESH

D_MODEL = 1024
CHUNK = 64
RET_HEADS = 4
RET_KEY_DIM = 128
RET_VAL_DIM = 256
ATT_HEADS = 8
BAND_CHUNKS = 8
MAX_REL = 256
N_REL = CHUNK + MAX_REL
D_FF = 2816
N_IN = 6656
ROPE_BASE = 10000.0
EPS = 1e-6
NEG_INF = -1e30
C_RQ, C_RK, C_RV, C_RG, C_AQ, C_AK, C_AV, C_GL = 0, 512, 1024, 2048, 3072, 3584, 4096, 4608

ADAM_LR = 0.001
ADAM_B1 = 0.9
ADAM_B2 = 0.999
ADAM_EPS = 1e-08
ADAM_WD = 0.01
ADAM_STEP = 10

N_DEV = 8
LANES = 128
RET_TILE = 256
ATT_Q = 256
ATT_PAD = BAND_CHUNKS * CHUNK
ATT_WIN = ATT_PAD + ATT_Q
ATT_STARTS = ATT_PAD // ATT_Q
ATT_ROWS = 32
BIAS_LEN = 1024
VMEM_LIMIT = 48 * 1024 * 1024
VMEM_LIMIT_ATT_BWD = 56 * 1024 * 1024


def _params(*sem, vmem=VMEM_LIMIT):
    return pltpu.CompilerParams(dimension_semantics=sem, vmem_limit_bytes=vmem)


def _dot(a, b):
    return lax.dot_general(a, b, (((1,), (0,)), ((), ())), preferred_element_type=F32)


def _dot_nt(a, b):
    return lax.dot_general(a, b, (((1,), (1,)), ((), ())), preferred_element_type=F32)


def _dot_tn(a, b):
    return lax.dot_general(a, b, (((0,), (0,)), ((), ())), preferred_element_type=F32)


def _sigmoid(x):
    return 1.0 / (1.0 + jnp.exp(-x))


def _rms_bwd(x, g, dy):
    r = lax.rsqrt(jnp.mean(x * x, axis=-1, keepdims=True) + EPS)
    u = dy * g
    dx = r * u - x * (r * r * r) * jnp.mean(u * x, axis=-1, keepdims=True)
    return dx, dy * x * r


def _place():
    return lax.axis_index("x"), lax.axis_index("y"), lax.axis_index("c")


def _peer(k):
    x, y, c = _place()
    return ((1 - x) if k & 4 else x, (1 - y) if k & 2 else y, (1 - c) if k & 1 else c)


def _index(place):
    return 4 * place[0] + 2 * place[1] + place[2]


def _rows(ref, block, nrows):
    align = 16 if ref.dtype == BF16 else 8
    return ref.at[pl.ds(pl.multiple_of(block * nrows, align), nrows)]


class _Exchange:
    def __init__(self, arrays, scatter):
        self.arrays, self.scatter, self.n = list(arrays), scatter, len(arrays)

    def out_shape(self):
        if self.scatter:
            return [jax.ShapeDtypeStruct((N_DEV, a.shape[0] // N_DEV) + a.shape[1:], a.dtype) for a in self.arrays]
        return [jax.ShapeDtypeStruct((N_DEV * a.shape[0],) + a.shape[1:], a.dtype) for a in self.arrays]

    def scratch(self):
        return [pltpu.SemaphoreType.DMA((self.n, N_DEV - 1)), pltpu.SemaphoreType.DMA((self.n, N_DEV - 1)),
                pltpu.SemaphoreType.DMA((self.n,))]

    def _copies(self, ins, outs, sems):
        send_sems, recv_sems, local_sems = sems
        me = _index(_place())

        def src(w, to):
            return _rows(ins[w], to, ins[w].shape[0] // N_DEV) if self.scatter else ins[w]

        def dst(w, origin):
            return outs[w].at[origin] if self.scatter else _rows(outs[w], origin, ins[w].shape[0])

        def remote(w, k, to, origin):
            return pltpu.make_async_remote_copy(src_ref=src(w, to), dst_ref=dst(w, origin),
                                                send_sem=send_sems.at[w, k - 1], recv_sem=recv_sems.at[w, k - 1],
                                                device_id=_peer(k), device_id_type=MESH)

        pairs = [(w, k) for w in range(self.n) for k in range(1, N_DEV)]
        own = lambda: [pltpu.make_async_copy(src(w, me), dst(w, me), local_sems.at[w]) for w in range(self.n)]
        sent = lambda: [remote(w, k, _index(_peer(k)), me) for w, k in pairs]
        arriving = lambda: [remote(w, k, me, _index(_peer(k))) for w, k in pairs]
        return own, sent, arriving

    def start(self, ins, outs, sems):
        own, sent, _ = self._copies(ins, outs, sems)
        for cp in own() + sent():
            cp.start()

    def wait(self, ins, outs, sems):
        own, sent, arriving = self._copies(ins, outs, sems)
        for cp in arriving():
            cp.wait_recv()
        for cp in sent():
            cp.wait_send()
        for cp in own():
            cp.wait()


class _PairSwap:
    def __init__(self, arrays):
        self.arrays, self.n = list(arrays), len(arrays)

    def out_shape(self):
        return [jax.ShapeDtypeStruct((4, a.shape[0] // N_DEV) + a.shape[1:], a.dtype) for a in self.arrays]

    def scratch(self):
        return [pltpu.SemaphoreType.DMA((self.n, 4)), pltpu.SemaphoreType.DMA((self.n, 4))]

    def _copies(self, ins, outs, sems):
        send_sems, recv_sems = sems
        x, y, c = _place()
        return [pltpu.make_async_remote_copy(
            src_ref=_rows(ins[w], 2 * q + 1 - c, ins[w].shape[0] // N_DEV), dst_ref=outs[w].at[q],
            send_sem=send_sems.at[w, q], recv_sem=recv_sems.at[w, q],
            device_id=(x, y, 1 - c), device_id_type=MESH) for w in range(self.n) for q in range(4)]

    def start(self, ins, outs, sems):
        for cp in self._copies(ins, outs, sems):
            cp.start()

    def wait(self, ins, outs, sems):
        for cp in self._copies(ins, outs, sems):
            cp.wait()


class _ChipScatter:
    def __init__(self, arrays):
        self.arrays, self.n = list(arrays), len(arrays)

    def out_shape(self):
        return [jax.ShapeDtypeStruct(a.shape, a.dtype) for a in self.arrays]

    def scratch(self):
        return [pltpu.SemaphoreType.DMA((self.n, 3)), pltpu.SemaphoreType.DMA((self.n, 3)),
                pltpu.SemaphoreType.DMA((self.n,))]

    def _copies(self, ins, outs, sems):
        send_sems, recv_sems, local_sems = sems
        x, y, c = _place()
        mine = 2 * x + y
        sent, arriving = [], []
        for w in range(self.n):
            for k in range(1, 4):
                tx, ty = (1 - x) if k & 2 else x, (1 - y) if k & 1 else y
                other = 2 * tx + ty
                sent.append(lambda w=w, k=k, tx=tx, ty=ty, other=other: pltpu.make_async_remote_copy(
                    src_ref=ins[w].at[other], dst_ref=outs[w].at[mine], send_sem=send_sems.at[w, k - 1],
                    recv_sem=recv_sems.at[w, k - 1], device_id=(tx, ty, c), device_id_type=MESH))
                arriving.append(lambda w=w, k=k, tx=tx, ty=ty, other=other: pltpu.make_async_remote_copy(
                    src_ref=ins[w].at[mine], dst_ref=outs[w].at[other], send_sem=send_sems.at[w, k - 1],
                    recv_sem=recv_sems.at[w, k - 1], device_id=(tx, ty, c), device_id_type=MESH))
        own = [lambda w=w: pltpu.make_async_copy(ins[w].at[mine], outs[w].at[mine], local_sems.at[w])
               for w in range(self.n)]
        return own, sent, arriving

    def start(self, ins, outs, sems):
        own, sent, _ = self._copies(ins, outs, sems)
        for cp in own + sent:
            cp().start()

    def wait(self, ins, outs, sems):
        own, sent, arriving = self._copies(ins, outs, sems)
        for cp in arriving:
            cp().wait_recv()
        for cp in sent:
            cp().wait_send()
        for cp in own:
            cp().wait()


class _ChipGather:
    def __init__(self, arrays, parts=None, into=None):
        self.arrays, self.n, self.into = list(arrays), len(arrays), into
        self.parts = parts or [(0, 1)] * self.n

    def out_shape(self):
        return [jax.ShapeDtypeStruct((N_DEV * a.shape[0],) + a.shape[1:], a.dtype) for a in self.arrays]

    def scratch(self):
        return [pltpu.SemaphoreType.DMA((self.n, N_DEV - 1)), pltpu.SemaphoreType.DMA((self.n, N_DEV - 1)),
                pltpu.SemaphoreType.DMA((self.n,))]

    def _parts(self, ins, outs, sems):
        send_sems, recv_sems, local_sems = sems
        x, y, c = _place()
        me, sibling = (x, y, c), (x, y, 1 - c)
        chips = [(1 - x, y), (x, 1 - y), (1 - x, 1 - y)]

        def rows(w, place, whole):
            (index, count), r = self.parts[w], ins[w].shape[0]
            lo, size = (0, r) if whole else (index * (r // count), r // count)
            align = 16 if ins[w].dtype == BF16 else 8
            return outs[w].at[pl.ds(pl.multiple_of(_index(place) * r + lo, align), size)]

        def mine(w, whole):
            (index, count), r = self.parts[w], ins[w].shape[0]
            return ins[w] if whole or count == 1 else ins[w].at[pl.ds(index * (r // count), r // count)]

        def copy(w, k, block, to, own=False):
            whole = k == 0
            return pltpu.make_async_remote_copy(src_ref=mine(w, whole) if own else rows(w, block, whole),
                                                dst_ref=rows(w, block, whole),
                                                send_sem=send_sems.at[w, k], recv_sem=recv_sems.at[w, k],
                                                device_id=to, device_id_type=MESH)

        def local(w):
            return pltpu.make_async_copy(ins[w], rows(w, me, True), local_sems.at[w])

        return me, sibling, chips, c, copy, local, [index == 0 for index, _ in self.parts]

    def start(self, ins, outs, sems):
        me, sibling, chips, c, copy, local, places_own = self._parts(ins, outs, sems)
        for w in range(self.n):
            if places_own[w]:
                local(w).start()
                copy(w, 0, me, sibling, own=True).start()
            for j, chip in enumerate(chips):
                copy(w, 1 + j, me, (*chip, c), own=True).start()

    def wait(self, ins, outs, sems):
        me, sibling, chips, c, copy, local, places_own = self._parts(ins, outs, sems)
        for w in range(self.n):
            for j, chip in enumerate(chips):
                copy(w, 1 + j, (*chip, c), me).wait_recv()
                copy(w, 4 + j, (*chip, c), sibling).start()
        for w in range(self.n):
            if places_own[w]:
                copy(w, 0, sibling, me).wait_recv()
                copy(w, 0, me, sibling, own=True).wait_send()
                local(w).wait()
            for j, chip in enumerate(chips):
                copy(w, 4 + j, (*chip, 1 - c), me).wait_recv()
                copy(w, 1 + j, me, (*chip, c), own=True).wait_send()
                copy(w, 4 + j, (*chip, c), sibling).wait_send()


def _call(body, *, name, grid, in_specs, out_specs, out_shape, scratch=(), semantics, args, exchange=None,
          vmem=VMEM_LIMIT):
    if exchange is None:
        return pl.pallas_call(body, name=name, grid=grid, in_specs=in_specs, out_specs=out_specs, out_shape=out_shape,
                              scratch_shapes=list(scratch),
                              compiler_params=_params(*semantics, vmem=vmem))(*args), None
    n_in, n_out, n_scr, nx = len(in_specs), len(out_specs), len(scratch), exchange.n
    into = list(getattr(exchange, "into", None) or [])

    def full_body(*refs):
        ins, refs = refs[:n_in], refs[n_in:]
        x_in, refs = refs[:nx], refs[nx + len(into):]
        outs, refs = refs[:n_out], refs[n_out:]
        x_out, refs = refs[:nx], refs[nx:]
        scr, sems = refs[:n_scr], refs[n_scr:]
        first, last = True, True
        for axis, size in enumerate(grid):
            first = jnp.logical_and(first, pl.program_id(axis) == 0)
            last = jnp.logical_and(last, pl.program_id(axis) == size - 1)
        if grid:
            pl.when(first)(lambda: exchange.start(x_in, x_out, sems))
        else:
            exchange.start(x_in, x_out, sems)
        body(*ins, *outs, *scr)
        if grid:
            pl.when(last)(lambda: exchange.wait(x_in, x_out, sems))
        else:
            exchange.wait(x_in, x_out, sems)

    hbm = pl.BlockSpec(memory_space=pltpu.HBM)
    res = pl.pallas_call(
        full_body, name=name, grid=grid,
        in_specs=list(in_specs) + [hbm] * (nx + len(into)), out_specs=list(out_specs) + [hbm] * nx,
        out_shape=list(out_shape) + exchange.out_shape(),
        scratch_shapes=list(scratch) + exchange.scratch(),
        input_output_aliases={n_in + nx + w: n_out + w for w in range(len(into))},
        compiler_params=_params(*(["arbitrary"] * len(grid)), vmem=vmem),
    )(*args, *exchange.arrays, *into)
    return res[:n_out], res[n_out:]


def _alone(exchange, name):
    return _call(lambda: None, name=name, grid=(), in_specs=[], out_specs=[], out_shape=[], semantics=(),
                 args=(), exchange=exchange)[1]


def _mm(a, b, *, ta=False, tb=False, out_dtype, tm, tn, tk, name, exchange=None):
    m, k = (a.shape[1], a.shape[0]) if ta else a.shape
    n = b.shape[0] if tb else b.shape[1]
    assert k == (b.shape[1] if tb else b.shape[0])
    tm, tn, tk = min(tm, m), min(tn, n), min(tk, k)
    assert m % tm == 0 and n % tn == 0 and k % tk == 0, (name, m, n, k)
    nk = k // tk
    dims = (((0 if ta else 1,), (1 if tb else 0,)), ((), ()))

    def body(a_ref, b_ref, o_ref, *acc):
        prod = lax.dot_general(a_ref[...].astype(BF16), b_ref[...].astype(BF16), dims, preferred_element_type=F32)
        if nk == 1:
            o_ref[...] = prod.astype(o_ref.dtype)
            return
        acc_ref, kk = acc[0], pl.program_id(2)

        @pl.when(kk == 0)
        def _():
            acc_ref[...] = prod

        @pl.when((kk > 0) & (kk < nk - 1))
        def _():
            acc_ref[...] += prod

        @pl.when(kk == nk - 1)
        def _():
            o_ref[...] = (acc_ref[...] + prod).astype(o_ref.dtype)

    a_spec = (pl.BlockSpec((tk, tm), lambda i, j, kk: (kk, i)) if ta
              else pl.BlockSpec((tm, tk), lambda i, j, kk: (i, kk)))
    b_spec = (pl.BlockSpec((tn, tk), lambda i, j, kk: (j, kk)) if tb
              else pl.BlockSpec((tk, tn), lambda i, j, kk: (kk, j)))
    (out,), moved = _call(
        body, name=name, grid=(m // tm, n // tn, nk),
        in_specs=[a_spec, b_spec],
        out_specs=[pl.BlockSpec((tm, tn), lambda i, j, kk: (i, j))],
        out_shape=[pltpu.HBM((m, n), out_dtype)],
        scratch=[pltpu.VMEM((tm, tn), F32)] if nk > 1 else [],
        semantics=("parallel", "parallel", "arbitrary"), args=(a, b), exchange=exchange)
    return out if exchange is None else (out, moved)


def _mm_pieces(pieces, b, *, ta, out_dtype, tm, tn, tk, name, exchange=None):
    rows, n = pieces[0].shape[0], b.shape[1]
    step = tm if ta else tk
    assert all(p.shape[0] == rows and p.shape[1] % step == 0 for p in pieces), name
    edges = [int(e) for e in np.cumsum([0] + [p.shape[1] // step for p in pieces])]
    total = edges[-1] * step
    m, k = (total, rows) if ta else (rows, total)
    assert b.shape[0] == k and m % tm == 0 and n % tn == 0 and k % tk == 0, name
    nk, npieces = k // tk, len(pieces)
    dims = (((0 if ta else 1,), (0,)), ((), ()))

    def body(*refs):
        a_refs, (b_ref, o_ref, acc_ref) = refs[:npieces], refs[npieces:]
        kk = pl.program_id(2)
        pos = pl.program_id(0) if ta else kk

        @pl.when(kk == 0)
        def _():
            acc_ref[...] = jnp.zeros_like(acc_ref)

        for p, a_ref in enumerate(a_refs):
            @pl.when((pos >= edges[p]) & (pos < edges[p + 1]))
            def _(a_ref=a_ref):
                acc_ref[...] += lax.dot_general(a_ref[...], b_ref[...], dims, preferred_element_type=F32)

        @pl.when(kk == nk - 1)
        def _():
            o_ref[...] = acc_ref[...].astype(o_ref.dtype)

    def a_spec(p):
        lo, last = edges[p], edges[p + 1] - edges[p] - 1
        if ta:
            def index(i, j, kk):
                inside = (i >= lo) & (i <= lo + last)
                return jnp.where(inside, kk, 0), jnp.clip(i - lo, 0, last)
            return pl.BlockSpec((tk, tm), index)
        return pl.BlockSpec((tm, tk), lambda i, j, kk: (i, jnp.clip(kk - lo, 0, last)))

    (out,), moved = _call(
        body, name=name, grid=(m // tm, n // tn, nk),
        in_specs=[a_spec(p) for p in range(npieces)] + [pl.BlockSpec((tk, tn), lambda i, j, kk: (kk, j))],
        out_specs=[pl.BlockSpec((tm, tn), lambda i, j, kk: (i, j))],
        out_shape=[pltpu.HBM((m, n), out_dtype)],
        scratch=[pltpu.VMEM((tm, tn), F32)],
        semantics=("parallel", "parallel", "arbitrary"), args=(*pieces, b), exchange=exchange)
    return out if exchange is None else (out, moved)


def _rms_fwd(x2, g, exchange):
    t = x2.shape[0]
    tm = min(512, t)

    def body(x_ref, g_ref, o_ref):
        x = x_ref[...]
        r = lax.rsqrt(jnp.mean(x * x, axis=-1, keepdims=True) + EPS)
        o_ref[...] = (x * r * g_ref[...]).astype(o_ref.dtype)

    return _call(
        body, name="rms_in_fwd", grid=(t // tm,),
        in_specs=[pl.BlockSpec((tm, D_MODEL), lambda i: (i, 0)), pl.BlockSpec((1, D_MODEL), lambda i: (0, 0))],
        out_specs=[pl.BlockSpec((tm, D_MODEL), lambda i: (i, 0))],
        out_shape=[jax.ShapeDtypeStruct((t, D_MODEL), BF16)],
        semantics=("parallel",), args=(x2, g), exchange=exchange)


def _decay(lg):
    row = lax.broadcasted_iota(jnp.int32, (RET_TILE, RET_TILE), 0)
    col = lax.broadcasted_iota(jnp.int32, (RET_TILE, RET_TILE), 1)
    within = jnp.exp(lg * jnp.abs(row - col).astype(F32))
    inside = jnp.where((col >> 6) <= (row >> 6), within, 0.0)
    pos = lax.broadcasted_iota(jnp.int32, (RET_TILE, 1), 0).astype(F32)
    q_dec = jnp.exp(lg * (pos + 1.0))
    k_dec = jnp.exp(lg * (RET_TILE - 1.0 - pos))
    tile_dec = jnp.exp(lg * float(RET_TILE))
    return inside, q_dec, k_dec, tile_dec


def _scaled(a_bf16, dec):
    return (a_bf16.astype(F32) * dec).astype(BF16)


def _ret_fwd(proj, cs, sn, lg_arr, batch, seq, exchange):
    t = batch * seq
    nt = seq // RET_TILE

    def body(q_ref, k_ref, v_ref, rg_ref, cs_ref, sn_ref, lg_ref, gro_ref, o_ref, qr_ref, kr_ref):
        lg = lg_ref[:, 0:1]
        cs_t, sn_t = cs_ref[...], sn_ref[...]
        q = q_ref[...].astype(F32)
        k = k_ref[...].astype(F32)
        qr_ref[...] = (q * cs_t + pltpu.roll(q, 64, 1) * sn_t).astype(BF16)
        kr_ref[...] = ((k * cs_t + pltpu.roll(k, 64, 1) * sn_t) * (RET_KEY_DIM ** -0.5)).astype(BF16)
        inside, q_dec, k_dec, tile_dec = _decay(lg)
        state = jnp.zeros((RET_KEY_DIM, RET_VAL_DIM), F32)
        for i in range(nt):
            rows = slice(i * RET_TILE, (i + 1) * RET_TILE)
            qi, ki, vi = qr_ref[rows, :], kr_ref[rows, :], v_ref[rows, :]
            acc = _dot((_dot_nt(qi, ki) * inside).astype(BF16), vi)
            if i > 0:
                acc = acc + _dot(_scaled(qi, q_dec), state.astype(BF16))
            if i < nt - 1:
                state = state * tile_dec + _dot_tn(_scaled(ki, k_dec), vi)
            o_ref[rows, :] = acc
            xc = acc - jnp.mean(acc, axis=-1, keepdims=True)
            nrm = xc * lax.rsqrt(jnp.mean(xc * xc, axis=-1, keepdims=True) + EPS)
            rg = rg_ref[rows, :].astype(F32)
            gro_ref[rows, :] = (rg * _sigmoid(rg) * nrm).astype(BF16)

    def col(base, width):
        return lambda b, h: (b, base // width + h)

    return _call(
        body, name="ret_fwd", grid=(batch, RET_HEADS),
        in_specs=[pl.BlockSpec((seq, RET_KEY_DIM), col(C_RQ, RET_KEY_DIM)),
                  pl.BlockSpec((seq, RET_KEY_DIM), col(C_RK, RET_KEY_DIM)),
                  pl.BlockSpec((seq, RET_VAL_DIM), col(C_RV, RET_VAL_DIM)),
                  pl.BlockSpec((seq, RET_VAL_DIM), col(C_RG, RET_VAL_DIM)),
                  pl.BlockSpec((seq, RET_KEY_DIM), lambda b, h: (0, 0)),
                  pl.BlockSpec((seq, RET_KEY_DIM), lambda b, h: (0, 0)),
                  pl.BlockSpec((None, 1, LANES), lambda b, h: (h, 0, 0))],
        out_specs=[pl.BlockSpec((seq, RET_VAL_DIM), lambda b, h: (b, h)),
                   pl.BlockSpec((seq, RET_VAL_DIM), lambda b, h: (b, h)),
                   pl.BlockSpec((seq, RET_KEY_DIM), lambda b, h: (b, h)),
                   pl.BlockSpec((seq, RET_KEY_DIM), lambda b, h: (b, h))],
        out_shape=[jax.ShapeDtypeStruct((t, RET_HEADS * RET_VAL_DIM), BF16),
                   jax.ShapeDtypeStruct((t, RET_HEADS * RET_VAL_DIM), F32),
                   jax.ShapeDtypeStruct((t, RET_HEADS * RET_KEY_DIM), BF16),
                   jax.ShapeDtypeStruct((t, RET_HEADS * RET_KEY_DIM), BF16)],
        semantics=("parallel", "parallel"), args=(proj, proj, proj, proj, cs, sn, lg_arr), exchange=exchange)


def _att_bias(w_ref, bias_ref):
    n_i = lax.broadcasted_iota(jnp.int32, (ATT_Q, BIAS_LEN), 0)
    qc = lax.broadcasted_iota(jnp.int32, (ATT_Q, ATT_WIN), 0) >> 6
    kc = lax.broadcasted_iota(jnp.int32, (ATT_Q, ATT_WIN), 1) >> 6
    dc = qc + BAND_CHUNKS - kc
    band = (dc >= 0) & (dc <= BAND_CHUNKS)
    key = lax.broadcasted_iota(jnp.int32, (ATT_Q, ATT_WIN), 1)
    for e in range(2):
        xw = jnp.broadcast_to(w_ref[e:e + 1, :], (ATT_Q, BIAS_LEN))
        for bit in range(8):
            xw = jnp.where(((n_i >> bit) & 1) == 1, pltpu.roll(xw, 1 << bit, 1), xw)
        bias = jnp.where(band, xw[:, BIAS_LEN - ATT_WIN:], NEG_INF)
        for first in range(ATT_STARTS):
            bias_ref[first, e] = jnp.where(key + (first * ATT_Q - ATT_PAD) >= 0, bias, NEG_INF)
        bias_ref[ATT_STARTS, e] = bias


ATT_PAIRS = 2
ATT_COLS = ATT_PAIRS * LANES


def _att_specs(batch, seq):
    ni = seq // ATT_Q
    q_spec = pl.BlockSpec((ATT_Q, ATT_COLS), lambda g, b, i: (b * ni + i, C_AQ // ATT_COLS + g))
    k_spec = pl.BlockSpec((seq, ATT_COLS), lambda g, b, i: (b, C_AK // ATT_COLS + g))
    v_spec = pl.BlockSpec((seq, ATT_COLS), lambda g, b, i: (b, C_AV // ATT_COLS + g))
    w_spec = pl.BlockSpec((ATT_PAIRS, 2, BIAS_LEN), lambda g, b, i: (g, 0, 0))
    b_spec = pl.BlockSpec((ATT_PAIRS, ATT_STARTS + 1, 2, ATT_Q, ATT_WIN), lambda g, b, i: (g, 0, 0, 0, 0))
    pad = pltpu.VMEM((seq + ATT_PAD, ATT_COLS), BF16)
    return ni, q_spec, k_spec, v_spec, w_spec, b_spec, pad


def _att_bias_tiles(wvec, exchange):
    (tiles,), moved = _call(
        lambda w_ref, o_ref: _att_bias(w_ref, o_ref), name="att_bias", grid=(ATT_HEADS // 2,),
        in_specs=[pl.BlockSpec((None, 2, BIAS_LEN), lambda hp: (hp, 0, 0))],
        out_specs=[pl.BlockSpec((None, ATT_STARTS + 1, 2, ATT_Q, ATT_WIN), lambda hp: (hp, 0, 0, 0, 0))],
        out_shape=[jax.ShapeDtypeStruct((ATT_HEADS // 2, ATT_STARTS + 1, 2, ATT_Q, ATT_WIN), F32)],
        semantics=("parallel",), args=(wvec,), exchange=exchange)
    return tiles, moved


def _att_pad(src_ref, pad_ref):
    pad_ref[:ATT_PAD, :] = jnp.zeros((ATT_PAD, ATT_COLS), BF16)
    pad_ref[ATT_PAD:, :] = src_ref[...]


def _att_head(q2, sel):
    return jnp.where(sel, q2, jnp.zeros_like(q2)) * 0.125


def _att_softmax_rows(s_ref, bias_ref, rows):
    s = s_ref[rows, :] + bias_ref[rows, :]
    ex = jnp.exp(s - jnp.max(s, axis=-1, keepdims=True))
    return ex, 1.0 / jnp.sum(ex, axis=-1, keepdims=True)


def _att_fwd(proj, bias, batch, seq, exchange):
    ni, q_spec, k_spec, v_spec, _, b_spec, pad = _att_specs(batch, seq)

    def body(q_ref, k_ref, v_ref, bias_ref, o_ref, kp_ref, vp_ref, s_ref, e_ref):
        i = pl.program_id(2)

        @pl.when(i == 0)
        def _():
            _att_pad(k_ref, kp_ref)
            _att_pad(v_ref, vp_ref)

        win = pl.ds(pl.multiple_of(i * ATT_Q, ATT_Q), ATT_WIN)
        lo = lax.broadcasted_iota(jnp.int32, (1, LANES), 1) < 64
        start = jnp.minimum(i, ATT_STARTS)
        for pair in range(ATT_PAIRS):
            cols = slice(pair * LANES, (pair + 1) * LANES)
            k2, v2, q2 = kp_ref[win, cols], vp_ref[win, cols], q_ref[:, cols]
            out = jnp.zeros((ATT_Q, LANES), F32)
            for e in range(2):
                h = 2 * pair + e
                sel = lo if e == 0 else jnp.logical_not(lo)
                s_ref[h] = _dot_nt(_att_head(q2, sel), k2)
                rsum = []
                for c in range(ATT_Q // ATT_ROWS):
                    rows = slice(c * ATT_ROWS, (c + 1) * ATT_ROWS)
                    ex, r = _att_softmax_rows(s_ref.at[h], bias_ref.at[pair, start, e], rows)
                    e_ref[h, rows, :] = ex.astype(BF16)
                    rsum.append(r)
                out = out + _dot(e_ref[h], jnp.where(sel, v2, jnp.zeros_like(v2))) * jnp.concatenate(rsum, axis=0)
            o_ref[:, cols] = out.astype(BF16)

    heads = 2 * ATT_PAIRS
    return _call(
        body, name="att_fwd", grid=(ATT_HEADS // heads, batch, ni),
        in_specs=[q_spec, k_spec, v_spec, b_spec],
        out_specs=[pl.BlockSpec((ATT_Q, ATT_COLS), lambda g, b, i: (b * ni + i, g))],
        out_shape=[jax.ShapeDtypeStruct((batch * seq, ATT_HEADS * 64), BF16)],
        scratch=[pad, pad, pltpu.VMEM((heads, ATT_Q, ATT_WIN), F32), pltpu.VMEM((heads, ATT_Q, ATT_WIN), BF16)],
        semantics=("arbitrary", "arbitrary", "arbitrary"), args=(proj, proj, proj, bias), exchange=exchange)


GL_HALF = 512


def _gl_specs(tm):
    return [pl.BlockSpec((tm, GL_HALF), lambda i, c=C_GL // GL_HALF + j: (i, c)) for j in range(4)]


def _gates(gl_refs, b_ref):
    logits = [ref[...].astype(F32) for ref in gl_refs]
    gr = _sigmoid(jnp.concatenate(logits[:2], axis=1) + b_ref[:, :D_MODEL])
    ga = _sigmoid(jnp.concatenate(logits[2:], axis=1) + b_ref[:, D_MODEL:])
    return gr, ga


def _whole(a):
    return pl.BlockSpec(a.shape, lambda i: (0,) * a.ndim)


def _mix_out_fwd(gro, ao, proj, b_gate, w_ret, w_att_t, x2, w_out, g2):
    t = gro.shape[0]
    tm = min(256, t)

    def body(gro_ref, ao_ref, gl0, gl1, gl2, gl3, b_ref, wr_ref, wa_ref, x_ref, wo_ref, g_ref,
             z_ref, yr_ref, ya_ref, h_ref, hn_ref):
        yr = _dot(gro_ref[...], wr_ref[...])
        ya = _dot_nt(ao_ref[...], wa_ref[...])
        yr_ref[...] = yr.astype(BF16)
        ya_ref[...] = ya.astype(BF16)
        gr, ga = _gates((gl0, gl1, gl2, gl3), b_ref)
        z = (gr * yr + ga * ya).astype(BF16)
        z_ref[...] = z
        h = x_ref[...] + _dot(z, wo_ref[...])
        h_ref[...] = h
        r = lax.rsqrt(jnp.mean(h * h, axis=-1, keepdims=True) + EPS)
        hn_ref[...] = (h * r * g_ref[...]).astype(BF16)

    row = pl.BlockSpec((tm, D_MODEL), lambda i: (i, 0))
    return pl.pallas_call(
        body, name="mix_out_fwd", grid=(t // tm,),
        in_specs=[row, pl.BlockSpec((tm, 512), lambda i: (i, 0)), *_gl_specs(tm),
                  _whole(b_gate), _whole(w_ret), _whole(w_att_t), row, _whole(w_out), _whole(g2)],
        out_specs=[row] * 5,
        out_shape=[jax.ShapeDtypeStruct((t, D_MODEL), BF16)] * 3
                  + [jax.ShapeDtypeStruct((t, D_MODEL), F32), jax.ShapeDtypeStruct((t, D_MODEL), BF16)],
        compiler_params=_params("parallel"),
    )(gro, ao, proj, proj, proj, proj, b_gate, w_ret, w_att_t, x2, w_out, g2)


def _col_chunks(width, chunk=384):
    return [slice(lo, min(lo + chunk, width)) for lo in range(0, width, chunk)]


def _ffn_up(hn, wg_t, wu_t):
    t = hn.shape[0]
    tm, tn = min(512, t), D_FF // 2

    def body(h_ref, wg_ref, wu_ref, g_ref, u_ref, a_ref):
        g = _dot_nt(h_ref[...], wg_ref[...])
        u = _dot_nt(h_ref[...], wu_ref[...])
        g_ref[...] = g.astype(BF16)
        u_ref[...] = u.astype(BF16)
        a_ref[...] = (g * _sigmoid(g) * u).astype(BF16)

    w_spec = pl.BlockSpec((tn, D_MODEL), lambda j, i: (j, 0))
    out = pl.BlockSpec((tm, tn), lambda j, i: (i, j))
    return pl.pallas_call(
        body, name="ffn_up", grid=(D_FF // tn, t // tm),
        in_specs=[pl.BlockSpec((tm, D_MODEL), lambda j, i: (i, 0)), w_spec, w_spec],
        out_specs=[out, out, out],
        out_shape=[jax.ShapeDtypeStruct((t, D_FF), BF16)] * 3,
        compiler_params=_params("parallel", "parallel"),
    )(hn, wg_t, wu_t)


def _ffn_down_loss(a, h1, tgt, w_down, g3):
    t = a.shape[0]
    tm = min(512, t)

    def body(a_ref, h_ref, t_ref, w_ref, g_ref, dh_ref, dhb_ref, loss_ref, dg_ref):
        @pl.when(pl.program_id(0) == 0)
        def _():
            loss_ref[...] = jnp.zeros_like(loss_ref)
            dg_ref[...] = jnp.zeros_like(dg_ref)

        g = g_ref[...]
        h2 = h_ref[...] + _dot(a_ref[...], w_ref[...])
        r = lax.rsqrt(jnp.mean(h2 * h2, axis=-1, keepdims=True) + EPS)
        err = h2 * r * g - t_ref[...]
        loss_ref[...] += jnp.sum(err * err) * (0.5 / D_MODEL)
        dy = err * (1.0 / D_MODEL)
        dh, dg_rows = _rms_bwd(h2, g, dy)
        dg_ref[...] += jnp.sum(dg_rows, axis=0, keepdims=True)
        dh_ref[...] = dh
        dhb_ref[...] = dh.astype(BF16)

    row = pl.BlockSpec((tm, D_MODEL), lambda i: (i, 0))
    vec = pl.BlockSpec((1, D_MODEL), lambda i: (0, 0))
    return pl.pallas_call(
        body, name="ffn_down_loss", grid=(t // tm,),
        in_specs=[pl.BlockSpec((tm, D_FF), lambda i: (i, 0)), row, row,
                  pl.BlockSpec((D_FF, D_MODEL), lambda i: (0, 0)), vec],
        out_specs=[row, row, pl.BlockSpec((1, LANES), lambda i: (0, 0)), vec],
        out_shape=[jax.ShapeDtypeStruct((t, D_MODEL), F32), jax.ShapeDtypeStruct((t, D_MODEL), BF16),
                   jax.ShapeDtypeStruct((1, LANES), F32), jax.ShapeDtypeStruct((1, D_MODEL), F32)],
        compiler_params=_params("arbitrary"),
    )(a, h1, tgt, w_down, g3)


def _ffn_bwd_act(dh2b, w_down, g_act, u_act, exchange):
    t = dh2b.shape[0]
    tm, tn = min(512, t), D_FF // 2

    def body(d_ref, w_ref, g_ref, u_ref, dg_ref, du_ref):
        d = d_ref[...]
        for cols in _col_chunks(tn):
            da = _dot_nt(d, w_ref[cols, :])
            g = g_ref[:, cols].astype(F32)
            u = u_ref[:, cols].astype(F32)
            sg = _sigmoid(g)
            dg_ref[:, cols] = (da * u * sg * (1.0 + g * (1.0 - sg))).astype(BF16)
            du_ref[:, cols] = (da * g * sg).astype(BF16)

    blk = pl.BlockSpec((tm, tn), lambda j, i: (i, j))
    return _call(
        body, name="ffn_bwd_act", grid=(D_FF // tn, t // tm),
        in_specs=[pl.BlockSpec((tm, D_MODEL), lambda j, i: (i, 0)),
                  pl.BlockSpec((tn, D_MODEL), lambda j, i: (j, 0)), blk, blk],
        out_specs=[blk, blk],
        out_shape=[jax.ShapeDtypeStruct((t, D_FF), BF16)] * 2,
        semantics=("parallel", "parallel"), args=(dh2b, w_down, g_act, u_act), exchange=exchange)


def _ffn_bwd_in(dg, du, wg_t, wu_t, h1, dh2, g2, exchange):
    t = dg.shape[0]
    tm, tk = min(512, t), D_FF // 2
    nk = D_FF // tk

    def body(dg_ref, du_ref, wg_ref, wu_ref, h_ref, d2_ref, g_ref, dh_ref, dhb_ref, gn_ref, acc_ref):
        i, kk = pl.program_id(0), pl.program_id(1)

        @pl.when((i == 0) & (kk == 0))
        def _():
            gn_ref[...] = jnp.zeros_like(gn_ref)

        part = _dot(dg_ref[...], wg_ref[...]) + _dot(du_ref[...], wu_ref[...])

        @pl.when(kk == 0)
        def _():
            acc_ref[...] = part

        @pl.when((kk > 0) & (kk < nk - 1))
        def _():
            acc_ref[...] += part

        @pl.when(kk == nk - 1)
        def _():
            dx, dg_rows = _rms_bwd(h_ref[...], g_ref[...], acc_ref[...] + part)
            dh = d2_ref[...] + dx
            dh_ref[...] = dh
            dhb_ref[...] = dh.astype(BF16)
            gn_ref[...] += jnp.sum(dg_rows, axis=0, keepdims=True)

    act = pl.BlockSpec((tm, tk), lambda i, kk: (i, kk))
    wsp = pl.BlockSpec((tk, D_MODEL), lambda i, kk: (kk, 0))
    row = pl.BlockSpec((tm, D_MODEL), lambda i, kk: (i, 0))
    vec = pl.BlockSpec((1, D_MODEL), lambda i, kk: (0, 0))
    return _call(
        body, name="ffn_bwd_in", grid=(t // tm, nk),
        in_specs=[act, act, wsp, wsp, row, row, vec],
        out_specs=[row, row, vec],
        out_shape=[jax.ShapeDtypeStruct((t, D_MODEL), F32), jax.ShapeDtypeStruct((t, D_MODEL), BF16),
                   jax.ShapeDtypeStruct((1, D_MODEL), F32)],
        scratch=[pltpu.VMEM((tm, D_MODEL), F32)],
        semantics=("arbitrary", "arbitrary"), args=(dg, du, wg_t, wu_t, h1, dh2, g2), exchange=exchange)


def _mix_bwd(dh1b, w_out, proj, b_gate, y_ret, y_att, w_ret, w_att_t):
    t = dh1b.shape[0]
    tm = min(256, t)

    def body(d_ref, wo_ref, gl0, gl1, gl2, gl3, b_ref, yr_ref, ya_ref, wr_ref, wa_ref,
             dyr_ref, dya_ref, dglr_ref, dgla_ref, dgro_ref, dao_ref, db_ref):
        @pl.when(pl.program_id(0) == 0)
        def _():
            db_ref[...] = jnp.zeros_like(db_ref)

        dz = _dot_nt(d_ref[...], wo_ref[...])
        gr, ga = _gates((gl0, gl1, gl2, gl3), b_ref)
        dyr = (dz * gr).astype(BF16)
        dya = (dz * ga).astype(BF16)
        dyr_ref[...] = dyr
        dya_ref[...] = dya
        dglr = dz * yr_ref[...].astype(F32) * gr * (1.0 - gr)
        dgla = dz * ya_ref[...].astype(F32) * ga * (1.0 - ga)
        dglr_ref[...] = dglr.astype(BF16)
        dgla_ref[...] = dgla.astype(BF16)
        db_ref[:, :D_MODEL] += jnp.sum(dglr, axis=0, keepdims=True)
        db_ref[:, D_MODEL:] += jnp.sum(dgla, axis=0, keepdims=True)
        dgro_ref[...] = _dot_nt(dyr, wr_ref[...]).astype(BF16)
        dao_ref[...] = _dot(dya, wa_ref[...]).astype(BF16)

    row = pl.BlockSpec((tm, D_MODEL), lambda i: (i, 0))
    half = pl.BlockSpec((tm, 512), lambda i: (i, 0))
    return pl.pallas_call(
        body, name="mix_bwd", grid=(t // tm,),
        in_specs=[row, _whole(w_out), *_gl_specs(tm), _whole(b_gate), row, row, _whole(w_ret), _whole(w_att_t)],
        out_specs=[row, row, row, row, row, half, _whole(b_gate)],
        out_shape=[jax.ShapeDtypeStruct((t, D_MODEL), BF16)] * 5
                  + [jax.ShapeDtypeStruct((t, 512), BF16), jax.ShapeDtypeStruct(b_gate.shape, F32)],
        compiler_params=_params("arbitrary"),
    )(dh1b, w_out, proj, proj, proj, proj, b_gate, y_ret, y_att, w_ret, w_att_t)


def _ret_bwd(dgro, proj, o_ret, qr, kr, cs, sn, lg_arr, batch, seq, exchange):
    t = batch * seq
    nt = seq // RET_TILE

    def body(dgro_ref, rg_ref, o_ref, qr_ref, kr_ref, v_ref, cs_ref, sn_ref, lg_ref,
             dq_ref, dk_ref, dv_ref, drg_ref, do_ref, st_ref):
        lg = lg_ref[:, 0:1]
        inside, q_dec, k_dec, tile_dec = _decay(lg)

        state = jnp.zeros((RET_KEY_DIM, RET_VAL_DIM), F32)
        for i in range(nt - 1):
            rows = slice(i * RET_TILE, (i + 1) * RET_TILE)
            state = state * tile_dec + _dot_tn(_scaled(kr_ref[rows, :], k_dec), v_ref[rows, :])
            st_ref[i + 1] = state.astype(BF16)

        for i in range(nt):
            rows = slice(i * RET_TILE, (i + 1) * RET_TILE)
            o = o_ref[rows, :]
            xc = o - jnp.mean(o, axis=-1, keepdims=True)
            rs = lax.rsqrt(jnp.mean(xc * xc, axis=-1, keepdims=True) + EPS)
            nrm = xc * rs
            rg = rg_ref[rows, :].astype(F32)
            sg = _sigmoid(rg)
            dg = dgro_ref[rows, :].astype(F32)
            drg_ref[rows, :] = (dg * nrm * sg * (1.0 + rg * (1.0 - sg))).astype(BF16)
            dn = dg * rg * sg
            do = rs * (dn - jnp.mean(dn, axis=-1, keepdims=True)
                       - nrm * jnp.mean(dn * nrm, axis=-1, keepdims=True))
            do_ref[rows, :] = do.astype(BF16)

        dstate = jnp.zeros((RET_KEY_DIM, RET_VAL_DIM), F32)
        for i in reversed(range(nt)):
            rows = slice(i * RET_TILE, (i + 1) * RET_TILE)
            qi, ki, vi, doi = qr_ref[rows, :], kr_ref[rows, :], v_ref[rows, :], do_ref[rows, :]
            p = (_dot_nt(qi, ki) * inside).astype(BF16)
            dp = (_dot_nt(doi, vi) * inside).astype(BF16)
            dq = _dot(dp, ki)
            dk = _dot_tn(dp, qi)
            dv = _dot_tn(p, doi)
            if i > 0:
                dq = dq + _dot_nt(doi, st_ref[i]) * q_dec
            if i < nt - 1:
                dsb = dstate.astype(BF16)
                dk = dk + _dot_nt(vi, dsb) * k_dec
                dv = dv + _dot(_scaled(ki, k_dec), dsb)
            if i > 0:
                dstate = dstate * tile_dec + _dot_tn(_scaled(qi, q_dec), doi)
            dq_ref[rows, :] = (dq * cs_ref[rows, :] - pltpu.roll(dq, 64, 1) * sn_ref[rows, :]).astype(BF16)
            dk = (dk * cs_ref[rows, :] - pltpu.roll(dk, 64, 1) * sn_ref[rows, :]) * (RET_KEY_DIM ** -0.5)
            dk_ref[rows, :] = dk.astype(BF16)
            dv_ref[rows, :] = dv.astype(BF16)

    key = pl.BlockSpec((seq, RET_KEY_DIM), lambda b, h: (b, h))
    val = pl.BlockSpec((seq, RET_VAL_DIM), lambda b, h: (b, h))
    tab = pl.BlockSpec((seq, RET_KEY_DIM), lambda b, h: (0, 0))
    return _call(
        body, name="ret_bwd", grid=(batch, RET_HEADS),
        in_specs=[val, pl.BlockSpec((seq, RET_VAL_DIM), lambda b, h: (b, C_RG // RET_VAL_DIM + h)), val, key, key,
                  pl.BlockSpec((seq, RET_VAL_DIM), lambda b, h: (b, C_RV // RET_VAL_DIM + h)), tab, tab,
                  pl.BlockSpec((None, 1, LANES), lambda b, h: (h, 0, 0))],
        out_specs=[key, key, val, val],
        out_shape=[jax.ShapeDtypeStruct((t, RET_HEADS * RET_KEY_DIM), BF16)] * 2
                  + [jax.ShapeDtypeStruct((t, RET_HEADS * RET_VAL_DIM), BF16)] * 2,
        scratch=[pltpu.VMEM((seq, RET_VAL_DIM), BF16), pltpu.VMEM((nt, RET_KEY_DIM, RET_VAL_DIM), BF16)],
        semantics=("parallel", "parallel"), args=(dgro, proj, o_ret, qr, kr, proj, cs, sn, lg_arr),
        exchange=exchange)


def _att_bwd(proj, bias, dao, batch, seq, exchange):
    ni, q_spec, k_spec, v_spec, w_spec, b_spec, pad = _att_specs(batch, seq)
    t = batch * seq

    def body(q_ref, k_ref, v_ref, bias_ref, do_ref, dq_ref, dk_ref, dv_ref, dw_ref,
             dbias_ref, dk_acc, dv_acc, kp_ref, vp_ref, s_ref, dp_ref, e_ref, ds_ref):
        b, i = pl.program_id(1), pl.program_id(2)

        @pl.when((b == 0) & (i == 0))
        def _():
            dbias_ref[...] = jnp.zeros_like(dbias_ref)

        @pl.when(i == 0)
        def _():
            _att_pad(k_ref, kp_ref)
            _att_pad(v_ref, vp_ref)
            dk_acc[...] = jnp.zeros_like(dk_acc)
            dv_acc[...] = jnp.zeros_like(dv_acc)

        win = pl.ds(pl.multiple_of(i * ATT_Q, ATT_Q), ATT_WIN)
        lo = lax.broadcasted_iota(jnp.int32, (1, LANES), 1) < 64
        start = jnp.minimum(i, ATT_STARTS)
        for pair in range(ATT_PAIRS):
            cols = slice(pair * LANES, (pair + 1) * LANES)
            k2, v2, q2, do2 = kp_ref[win, cols], vp_ref[win, cols], q_ref[:, cols], do_ref[:, cols]
            dq = jnp.zeros((ATT_Q, LANES), F32)
            dk = jnp.zeros((LANES, ATT_WIN), F32)
            dv = jnp.zeros((LANES, ATT_WIN), F32)
            for e in range(2):
                h = 2 * pair + e
                sel = lo if e == 0 else jnp.logical_not(lo)
                qm = _att_head(q2, sel)
                dom = jnp.where(sel, do2, jnp.zeros_like(do2))
                s_ref[h] = _dot_nt(qm, k2)
                dp_ref[h] = _dot_nt(dom, v2)
                rsum = []
                for c in range(ATT_Q // ATT_ROWS):
                    rows = slice(c * ATT_ROWS, (c + 1) * ATT_ROWS)
                    ex, r = _att_softmax_rows(s_ref.at[h], bias_ref.at[pair, start, e], rows)
                    dp = dp_ref[h, rows, :]
                    mean = jnp.sum(dp * ex, axis=-1, keepdims=True) * r
                    ds = ex * ((dp - mean) * r)
                    dbias_ref[h, rows, :] += ds
                    ds_ref[h, rows, :] = ds.astype(BF16)
                    e_ref[h, rows, :] = ex.astype(BF16)
                    rsum.append(r)
                dq = dq + _dot(ds_ref[h], jnp.where(sel, k2, jnp.zeros_like(k2)))
                dk = dk + _dot_tn(qm, ds_ref[h])
                dv = dv + _dot_tn((dom.astype(F32) * jnp.concatenate(rsum, axis=0)).astype(BF16), e_ref[h])
            dq_ref[:, cols] = (dq * 0.125).astype(BF16)
            dk_acc[cols, win] += dk
            dv_acc[cols, win] += dv

        @pl.when(i == ni - 1)
        def _():
            dk_ref[...] = dk_acc[:, ATT_PAD:].T.astype(BF16)
            dv_ref[...] = dv_acc[:, ATT_PAD:].T.astype(BF16)

        @pl.when((b == batch - 1) & (i == ni - 1))
        def _():
            n_i = lax.broadcasted_iota(jnp.int32, (ATT_Q, BIAS_LEN), 0)
            for h in range(heads):
                xw = jnp.concatenate([jnp.zeros((ATT_Q, BIAS_LEN - ATT_WIN), F32), dbias_ref[h]], axis=1)
                for bit in range(8):
                    xw = jnp.where(((n_i >> bit) & 1) == 1, pltpu.roll(xw, BIAS_LEN - (1 << bit), 1), xw)
                dw_ref[h // 2, h % 2:h % 2 + 1, :] = jnp.sum(xw, axis=0, keepdims=True)

    heads = 2 * ATT_PAIRS
    seq_blk = pl.BlockSpec((seq, ATT_COLS), lambda g, b, i: (b, g))
    q_out = pl.BlockSpec((ATT_Q, ATT_COLS), lambda g, b, i: (b * ni + i, g))
    tile_f32, tile_bf16 = pltpu.VMEM((heads, ATT_Q, ATT_WIN), F32), pltpu.VMEM((heads, ATT_Q, ATT_WIN), BF16)
    acc = pltpu.VMEM((ATT_COLS, seq + ATT_PAD), F32)
    return _call(
        body, name="att_bwd", grid=(ATT_HEADS // heads, batch, ni),
        in_specs=[q_spec, k_spec, v_spec, b_spec, q_out],
        out_specs=[q_out, seq_blk, seq_blk, w_spec],
        out_shape=[jax.ShapeDtypeStruct((t, 512), BF16)] * 3
                  + [jax.ShapeDtypeStruct((ATT_HEADS // 2, 2, BIAS_LEN), F32)],
        scratch=[tile_f32, acc, acc, pad, pad, tile_f32, tile_f32, tile_bf16, tile_bf16],
        semantics=("arbitrary", "arbitrary", "arbitrary"), args=(proj, proj, proj, bias, dao), exchange=exchange,
        vmem=VMEM_LIMIT_ATT_BWD)


def _rms_in_bwd(x2, dxn, dh1, g1):
    t = x2.shape[0]
    tm = min(512, t)

    def body(x_ref, d_ref, h_ref, g_ref, dx_ref, dg_ref):
        @pl.when(pl.program_id(0) == 0)
        def _():
            dg_ref[...] = jnp.zeros_like(dg_ref)

        dx, dg_rows = _rms_bwd(x_ref[...], g_ref[...], d_ref[...])
        dx_ref[...] = h_ref[...] + dx
        dg_ref[...] += jnp.sum(dg_rows, axis=0, keepdims=True)

    row = pl.BlockSpec((tm, D_MODEL), lambda i: (i, 0))
    vec = pl.BlockSpec((1, D_MODEL), lambda i: (0, 0))
    return pl.pallas_call(
        body, name="rms_in_bwd", grid=(t // tm,),
        in_specs=[row, row, row, vec], out_specs=[row, vec],
        out_shape=[jax.ShapeDtypeStruct((t, D_MODEL), F32), jax.ShapeDtypeStruct((1, D_MODEL), F32)],
        compiler_params=_params("arbitrary"),
    )(x2, dxn, dh1, g1)


def _pack_small(dg1, dbr, dba, dg2, dg3, dw, loss):
    def body(a_ref, b_ref, c_ref, d_ref, e_ref, w_ref, l_ref, o_ref):
        o_ref[...] = jnp.zeros_like(o_ref)
        for r, ref in enumerate((a_ref, b_ref, c_ref, d_ref, e_ref)):
            o_ref[r:r + 1, :] = ref[...]
        o_ref[5:6, 0:LANES] = l_ref[...]
        for hp in range(ATT_HEADS // 2):
            o_ref[8 + 2 * hp:10 + 2 * hp, :] = w_ref[hp]

    return pl.pallas_call(body, name="pack_small",
                          out_shape=jax.ShapeDtypeStruct((16, D_MODEL), F32))(dg1, dbr, dba, dg2, dg3, dw, loss)


def _rotary_tables(seq):
    freqs = ROPE_BASE ** (-jnp.arange(0, RET_KEY_DIM, 2, dtype=F32) / RET_KEY_DIM)
    ang = jnp.arange(seq, dtype=F32)[:, None] * freqs[None, :]
    cos, sin = jnp.cos(ang), jnp.sin(ang)
    return jnp.concatenate([cos, cos], axis=1), jnp.concatenate([-sin, sin], axis=1)


def _bias_rows(rel_bias):
    n_far = BIAS_LEN - ATT_Q - MAX_REL + 1
    n_near = BIAS_LEN - n_far - (N_REL - 2)
    w = jnp.concatenate([jnp.broadcast_to(rel_bias[:, N_REL - 1:], (ATT_HEADS, n_far)),
                         rel_bias[:, 1:N_REL - 1][:, ::-1],
                         jnp.broadcast_to(rel_bias[:, :1], (ATT_HEADS, n_near))], axis=1)
    return w.reshape(ATT_HEADS // 2, 2, BIAS_LEN)


def _bias_rows_bwd(dw):
    n_far = BIAS_LEN - ATT_Q - MAX_REL + 1
    mid = dw[:, n_far:n_far + N_REL - 2][:, ::-1]
    return jnp.concatenate([jnp.sum(dw[:, n_far + N_REL - 2:], axis=1, keepdims=True), mid,
                            jnp.sum(dw[:, :n_far], axis=1, keepdims=True)], axis=1)


def _step(x, tgt, norm_mix, b_gate, norm_ffn, norm_final, rel_bias_shard, shard):
    batch, seq, _ = x.shape
    t = batch * seq
    n_rb = rel_bias_shard.shape[-1]
    x2, tgt2 = x.reshape(t, D_MODEL), tgt.reshape(t, D_MODEL)
    g3 = norm_final.reshape(1, D_MODEL)
    cs, sn = _rotary_tables(seq)
    lg = np.log(1.0 - 2.0 ** (-5.0 - np.arange(RET_HEADS, dtype=np.float32))).astype(np.float32)
    lg_arr = jnp.asarray(np.broadcast_to(lg[:, None, None], (RET_HEADS, 1, LANES)))

    def gather(*names):
        return _ChipGather([shard[nm] for nm in names])

    def scatter(*grads):
        return _Exchange(grads, scatter=True)

    rb_pad = jnp.pad(rel_bias_shard, ((0, 0), (0, LANES - n_rb)))
    (xn,), (w_in_half, rb_full) = _rms_fwd(x2, norm_mix,
                                           _ChipGather([shard["w_in_t"], rb_pad], parts=[(0, 2), (0, 1)]))
    rb_full = rb_full.reshape(N_DEV, ATT_HEADS, LANES)[:, :, :n_rb]
    bias, (w_in_t,) = _att_bias_tiles(
        _bias_rows(jnp.transpose(rb_full, (1, 0, 2)).reshape(ATT_HEADS, N_DEV * n_rb)),
        _ChipGather([shard["w_in_t"]], parts=[(1, 2)], into=[w_in_half]))
    proj, (w_ret, w_att_t, w_out, w_gate_t) = _mm(
        xn, w_in_t, tb=True, out_dtype=BF16, tm=1024, tn=1664, tk=1024, name="proj",
        exchange=gather("w_ret", "w_att_t", "w_out", "w_gate_t"))
    (gro, o_ret, qr, kr), _ = _ret_fwd(proj, cs, sn, lg_arr, batch, seq, None)
    (ao,), (w_up_t, w_down) = _att_fwd(proj, bias, batch, seq, gather("w_up_t", "w_down"))
    z, y_ret, y_att, h1, hn = _mix_out_fwd(gro, ao, proj, b_gate, w_ret, w_att_t, x2, w_out, norm_ffn)
    g_act, u_act, a_act = _ffn_up(hn, w_gate_t, w_up_t)
    dh2, dh2b, loss, dg3 = _ffn_down_loss(a_act, h1, tgt2, w_down, g3)

    wg = dict(out_dtype=BF16, tn=1024, ta=True)
    slots = {}
    dw_down = _mm(a_act, dh2b, tm=1408, tk=1024, name="dw_down", **wg)
    (d_gact, d_uact), _ = _ffn_bwd_act(dh2b, w_down, g_act, u_act, None)
    dw_gate = _mm(d_gact, hn, tm=1408, tk=1024, name="dw_gate", **wg)
    dw_up = _mm(d_uact, hn, tm=1408, tk=1024, name="dw_up", **wg)
    (dh1, dh1b, dg2), (slots["w_down"],) = _ffn_bwd_in(d_gact, d_uact, w_gate_t, w_up_t, h1, dh2, norm_ffn,
                                                     scatter(dw_down))
    dw_out = _mm(z, dh1b, tm=1024, tk=2048, name="dw_out", **wg)
    dyr, dya, dglr, dgla, dgro, dao, db = _mix_bwd(dh1b, w_out, proj, b_gate, y_ret, y_att, w_ret, w_att_t)
    dw_ret = _mm(gro, dyr, tm=1024, tk=2048, name="dw_ret", **wg)
    dw_att = _mm(dya, ao, tm=1024, tk=2048, name="dw_att", **wg)
    (drq, drk, drv, drg), _ = _ret_bwd(dgro, proj, o_ret, qr, kr, cs, sn, lg_arr, batch, seq, None)
    (daq, dak, dav, dw), (slots["w_gate_t"], slots["w_out"], slots["w_ret"], slots["w_att_t"]) = _att_bwd(
        proj, bias, dao, batch, seq, scatter(dw_gate, dw_out, dw_ret, dw_att))
    dproj = [drq, drk, drv, drg, daq, dak, dav, dglr, dgla]
    dw_in, (slots["w_up_t"],) = _mm_pieces(dproj, xn, ta=True, out_dtype=BF16, tm=512, tn=1024, tk=1024,
                                           name="dw_in", exchange=scatter(dw_up))
    (dw_in_sibling,) = _alone(_PairSwap([dw_in]), "swap_w_in")
    dw_in_pairs = _pair_add(dw_in, dw_in_sibling, "pair_w_in")
    dxn, (slots["w_in_t"],) = _mm_pieces(dproj, w_in_t, ta=False, out_dtype=F32, tm=1024, tn=1024, tk=512, name="dxn",
                                  exchange=_ChipScatter([dw_in_pairs]))
    dx, dg1 = _rms_in_bwd(x2, dxn, dh1, norm_mix)
    small = _pack_small(dg1, db[:, :D_MODEL], db[:, D_MODEL:], dg2, dg3, dw, loss)
    (small_slots,) = _alone(_ChipGather([small]), "gather_small")
    return dx.reshape(batch, seq, D_MODEL), slots, small_slots.reshape(N_DEV, 16, D_MODEL)


def _row_tile(r, c):
    return max(d for d in range(16, r + 1, 16) if r % d == 0 and (d * c <= 256 * 1024 or d == 16))


def _pair_add(grad, got, name):
    _, r, c = got.shape
    tr = r
    core = lax.axis_index("c").astype(jnp.int32).reshape(1)

    def body(core_ref, g_ref, a_ref, o_ref):
        o_ref[...] = (g_ref[...].astype(F32) + a_ref[...].astype(F32)).astype(o_ref.dtype)

    blk = pl.BlockSpec((None, tr, c), lambda q, i, core_ref: (q, i, 0))
    return pl.pallas_call(
        body, name=name,
        grid_spec=pltpu.PrefetchScalarGridSpec(
            num_scalar_prefetch=1, grid=(4, r // tr),
            in_specs=[pl.BlockSpec((None, None, tr, c), lambda q, i, core_ref: (q, core_ref[0], i, 0)), blk],
            out_specs=blk),
        out_shape=jax.ShapeDtypeStruct(got.shape, got.dtype),
        compiler_params=_params("parallel", "parallel"),
    )(core, grad.reshape(4, 2, r, c), got)


def _sum_slots(slots, name):
    n, r, c = slots.shape
    tr = _row_tile(r, c)

    def body(s_ref, o_ref):
        acc = s_ref[0].astype(F32)
        for s in range(1, n):
            acc = acc + s_ref[s].astype(F32)
        o_ref[...] = acc

    return pl.pallas_call(
        body, name=name, grid=(r // tr,),
        in_specs=[pl.BlockSpec((n, tr, c), lambda i: (0, i, 0))],
        out_specs=pl.BlockSpec((tr, c), lambda i: (i, 0)),
        out_shape=jax.ShapeDtypeStruct((r, c), F32),
        compiler_params=_params("parallel"),
    )(slots)


def _adamw_math(w, g, m, v):
    m = ADAM_B1 * m + (1.0 - ADAM_B1) * g
    v = ADAM_B2 * v + (1.0 - ADAM_B2) * (g * g)
    m_hat = m / (1.0 - ADAM_B1 ** ADAM_STEP)
    v_hat = v / (1.0 - ADAM_B2 ** ADAM_STEP)
    return -ADAM_LR * (m_hat / (jnp.sqrt(v_hat) + ADAM_EPS) + ADAM_WD * w), m, v


def _adamw(w, slots, m, v, name):
    n, r, c = slots.shape
    tr = _row_tile(r, c)

    def body(w_ref, s_ref, m_ref, v_ref, g_ref, d_ref, nm_ref, nv_ref):
        g = s_ref[0].astype(F32)
        for s in range(1, n):
            g = g + s_ref[s].astype(F32)
        g_ref[...] = g
        d_ref[...], nm_ref[...], nv_ref[...] = _adamw_math(w_ref[...], g, m_ref[...], v_ref[...])

    blk = pl.BlockSpec((tr, c), lambda i: (i, 0))
    return pl.pallas_call(
        body, name=name, grid=(r // tr,),
        in_specs=[blk, pl.BlockSpec((n, tr, c), lambda i: (0, i, 0)), blk, blk], out_specs=[blk] * 4,
        out_shape=[jax.ShapeDtypeStruct((r, c), F32)] * 4,
        compiler_params=_params("parallel"),
    )(w, slots, m, v)


def _adamw_small(ws, gs, ms, vs):
    n = len(ws)

    def body(*refs):
        for i in range(n):
            w_ref, g_ref, m_ref, v_ref = (refs[j * n + i] for j in range(4))
            d_ref, nm_ref, nv_ref = (refs[(4 + j) * n + i] for j in range(3))
            d_ref[...], nm_ref[...], nv_ref[...] = _adamw_math(w_ref[...], g_ref[...], m_ref[...], v_ref[...])

    shapes = [jax.ShapeDtypeStruct(w.shape, F32) for w in ws]
    outs = pl.pallas_call(body, name="adamw_small", out_shape=shapes * 3)(*ws, *gs, *ms, *vs)
    return outs[:n], outs[n:2 * n], outs[2 * n:]


def kernel(x, norm_mix, w_in, b_gate, rel_bias, w_ret_out, w_att_out, w_out, norm_ffn, w_ffn_gate, w_ffn_up, w_ffn_down, norm_final, loss_target, m_norm_mix, m_w_in, m_b_gate, m_rel_bias, m_w_ret_out, m_w_att_out, m_w_out, m_norm_ffn, m_w_ffn_gate, m_w_ffn_up, m_w_ffn_down, m_norm_final, v_norm_mix, v_w_in, v_b_gate, v_rel_bias, v_w_ret_out, v_w_att_out, v_w_out, v_norm_ffn, v_w_ffn_gate, v_w_ffn_up, v_w_ffn_down, v_norm_final):
    me = _index(_place())
    n_rb = rel_bias.shape[-1]

    shard = dict(w_in_t=w_in[0].T, w_gate_t=w_ffn_gate[0].T, w_up_t=w_ffn_up[0].T, w_down=w_ffn_down[0],
                 w_ret=w_ret_out[0], w_out=w_out[0], w_att_t=w_att_out[0].T)
    shard = {nm: s.astype(BF16) for nm, s in shard.items()}
    dx, slots, small_slots = _step(x, loss_target, norm_mix, b_gate, norm_ffn, norm_final, rel_bias[0], shard)
    small_sum = _sum_slots(small_slots, "sum_small")
    loss = small_sum[5, 0]

    transposed = dict(w_in="w_in_t", w_ffn_gate="w_gate_t", w_ffn_up="w_up_t", w_att_out="w_att_t")
    plain = dict(w_ffn_down="w_down", w_ret_out="w_ret", w_out="w_out")
    g = dict(
        norm_mix=small_sum[0:1], b_gate=jnp.concatenate([small_sum[1:2], small_sum[2:3]], axis=1),
        norm_ffn=small_sum[3:4], norm_final=small_sum[4:5],
        rel_bias=lax.dynamic_slice_in_dim(_bias_rows_bwd(small_sum[8:16]), me * n_rb, n_rb, axis=1),
    )
    w = dict(norm_mix=norm_mix, w_in=w_in, b_gate=b_gate, rel_bias=rel_bias, w_ret_out=w_ret_out, w_att_out=w_att_out,
             w_out=w_out, norm_ffn=norm_ffn, w_ffn_gate=w_ffn_gate, w_ffn_up=w_ffn_up, w_ffn_down=w_ffn_down,
             norm_final=norm_final)
    m = dict(norm_mix=m_norm_mix, w_in=m_w_in, b_gate=m_b_gate, rel_bias=m_rel_bias, w_ret_out=m_w_ret_out,
             w_att_out=m_w_att_out, w_out=m_w_out, norm_ffn=m_norm_ffn, w_ffn_gate=m_w_ffn_gate, w_ffn_up=m_w_ffn_up,
             w_ffn_down=m_w_ffn_down, norm_final=m_norm_final)
    v = dict(norm_mix=v_norm_mix, w_in=v_w_in, b_gate=v_b_gate, rel_bias=v_rel_bias, w_ret_out=v_w_ret_out,
             w_att_out=v_w_att_out, w_out=v_w_out, norm_ffn=v_norm_ffn, w_ffn_gate=v_w_ffn_gate, w_ffn_up=v_w_ffn_up,
             w_ffn_down=v_w_ffn_down, norm_final=v_norm_final)
    order = ("norm_mix", "w_in", "b_gate", "rel_bias", "w_ret_out", "w_att_out", "w_out", "norm_ffn",
             "w_ffn_gate", "w_ffn_up", "w_ffn_down", "norm_final")
    small_names = ("norm_mix", "b_gate", "rel_bias", "norm_ffn", "norm_final")

    def flat(a):
        return a[0] if a.ndim == 3 else a.reshape(-1, a.shape[-1])

    grad, delta, new_m, new_v = {}, {}, {}, {}
    for nm in order:
        if nm in transposed:
            res = _adamw(w[nm][0].T, slots[transposed[nm]], m[nm][0].T, v[nm][0].T, "adamw_" + nm)
            grad[nm], delta[nm], new_m[nm], new_v[nm] = (a.T[None] for a in res)
        elif nm in plain:
            res = _adamw(flat(w[nm]), slots[plain[nm]], flat(m[nm]), flat(v[nm]), "adamw_" + nm)
            grad[nm], delta[nm], new_m[nm], new_v[nm] = (a.reshape(w[nm].shape) for a in res)
    ds, nms, nvs = _adamw_small([flat(w[nm]) for nm in small_names], [g[nm] for nm in small_names],
                                [flat(m[nm]) for nm in small_names], [flat(v[nm]) for nm in small_names])
    for i, nm in enumerate(small_names):
        grad[nm], delta[nm], new_m[nm], new_v[nm] = (a.reshape(w[nm].shape) for a in (g[nm], ds[i], nms[i], nvs[i]))

    return (loss, dx, *[grad[nm] for nm in order], *[delta[nm] for nm in order],
            *[new_m[nm] for nm in order], *[new_v[nm] for nm in order])
```

```python
import numpy as np
import jax
import jax.numpy as jnp
from jax import lax
from jax.experimental import pallas as pl
from jax.experimental.pallas import tpu as pltpu

F32 = jnp.float32
BF16 = jnp.bfloat16
MESH = pl.DeviceIdType.MESH

D_MODEL = 1024
CHUNK = 64
RET_HEADS = 4
RET_KEY_DIM = 128
RET_VAL_DIM = 256
ATT_HEADS = 8
BAND_CHUNKS = 8
MAX_REL = 256
N_REL = CHUNK + MAX_REL
D_FF = 2816
N_IN = 6656
ROPE_BASE = 10000.0
EPS = 1e-6
NEG_INF = -1e30
C_RQ, C_RK, C_RV, C_RG, C_AQ, C_AK, C_AV, C_GL = 0, 512, 1024, 2048, 3072, 3584, 4096, 4608

ADAM_LR = 0.001
ADAM_B1 = 0.9
ADAM_B2 = 0.999
ADAM_EPS = 1e-08
ADAM_WD = 0.01
ADAM_STEP = 10

N_DEV = 8
LANES = 128
RET_TILE = 256
ATT_Q = 256
ATT_PAD = BAND_CHUNKS * CHUNK
ATT_WIN = ATT_PAD + ATT_Q
ATT_STARTS = ATT_PAD // ATT_Q
ATT_ROWS = 32
BIAS_LEN = 1024
VMEM_LIMIT = 40 * 1024 * 1024
VMEM_LIMIT_ATT_BWD = 56 * 1024 * 1024


def _params(*sem, vmem=VMEM_LIMIT):
    return pltpu.CompilerParams(dimension_semantics=sem, vmem_limit_bytes=vmem)


def _dot(a, b):
    return lax.dot_general(a, b, (((1,), (0,)), ((), ())), preferred_element_type=F32)


def _dot_nt(a, b):
    return lax.dot_general(a, b, (((1,), (1,)), ((), ())), preferred_element_type=F32)


def _dot_tn(a, b):
    return lax.dot_general(a, b, (((0,), (0,)), ((), ())), preferred_element_type=F32)


def _sigmoid(x):
    return 1.0 / (1.0 + jnp.exp(-x))


def _rms_bwd(x, g, dy):
    r = lax.rsqrt(jnp.mean(x * x, axis=-1, keepdims=True) + EPS)
    u = dy * g
    dx = r * u - x * (r * r * r) * jnp.mean(u * x, axis=-1, keepdims=True)
    return dx, dy * x * r


def _place():
    return lax.axis_index("x"), lax.axis_index("y"), lax.axis_index("c")


def _peer(k):
    x, y, c = _place()
    return ((1 - x) if k & 4 else x, (1 - y) if k & 2 else y, (1 - c) if k & 1 else c)


def _index(place):
    return 4 * place[0] + 2 * place[1] + place[2]


def _rows(ref, block, nrows):
    align = 16 if ref.dtype == BF16 else 8
    return ref.at[pl.ds(pl.multiple_of(block * nrows, align), nrows)]


class _Exchange:
    def __init__(self, arrays, scatter):
        self.arrays, self.scatter, self.n = list(arrays), scatter, len(arrays)

    def out_shape(self):
        if self.scatter:
            return [jax.ShapeDtypeStruct((N_DEV, a.shape[0] // N_DEV) + a.shape[1:], a.dtype) for a in self.arrays]
        return [jax.ShapeDtypeStruct((N_DEV * a.shape[0],) + a.shape[1:], a.dtype) for a in self.arrays]

    def scratch(self):
        return [pltpu.SemaphoreType.DMA((self.n, N_DEV - 1)), pltpu.SemaphoreType.DMA((self.n, N_DEV - 1)),
                pltpu.SemaphoreType.DMA((self.n,))]

    def _copies(self, ins, outs, sems):
        send_sems, recv_sems, local_sems = sems
        me = _index(_place())

        def src(w, to):
            return _rows(ins[w], to, ins[w].shape[0] // N_DEV) if self.scatter else ins[w]

        def dst(w, origin):
            return outs[w].at[origin] if self.scatter else _rows(outs[w], origin, ins[w].shape[0])

        def remote(w, k, to, origin):
            return pltpu.make_async_remote_copy(src_ref=src(w, to), dst_ref=dst(w, origin),
                                                send_sem=send_sems.at[w, k - 1], recv_sem=recv_sems.at[w, k - 1],
                                                device_id=_peer(k), device_id_type=MESH)

        pairs = [(w, k) for w in range(self.n) for k in range(1, N_DEV)]
        own = lambda: [pltpu.make_async_copy(src(w, me), dst(w, me), local_sems.at[w]) for w in range(self.n)]
        sent = lambda: [remote(w, k, _index(_peer(k)), me) for w, k in pairs]
        arriving = lambda: [remote(w, k, me, _index(_peer(k))) for w, k in pairs]
        return own, sent, arriving

    def start(self, ins, outs, sems):
        own, sent, _ = self._copies(ins, outs, sems)
        for cp in own() + sent():
            cp.start()

    def wait(self, ins, outs, sems):
        own, sent, arriving = self._copies(ins, outs, sems)
        for cp in arriving():
            cp.wait_recv()
        for cp in sent():
            cp.wait_send()
        for cp in own():
            cp.wait()


class _PairSwap:
    def __init__(self, arrays):
        self.arrays, self.n = list(arrays), len(arrays)

    def out_shape(self):
        return [jax.ShapeDtypeStruct((4, a.shape[0] // N_DEV) + a.shape[1:], a.dtype) for a in self.arrays]

    def scratch(self):
        return [pltpu.SemaphoreType.DMA((self.n, 4)), pltpu.SemaphoreType.DMA((self.n, 4))]

    def _copies(self, ins, outs, sems):
        send_sems, recv_sems = sems
        x, y, c = _place()
        return [pltpu.make_async_remote_copy(
            src_ref=_rows(ins[w], 2 * q + 1 - c, ins[w].shape[0] // N_DEV), dst_ref=outs[w].at[q],
            send_sem=send_sems.at[w, q], recv_sem=recv_sems.at[w, q],
            device_id=(x, y, 1 - c), device_id_type=MESH) for w in range(self.n) for q in range(4)]

    def start(self, ins, outs, sems):
        for cp in self._copies(ins, outs, sems):
            cp.start()

    def wait(self, ins, outs, sems):
        for cp in self._copies(ins, outs, sems):
            cp.wait()


class _ChipScatter:
    def __init__(self, arrays):
        self.arrays, self.n = list(arrays), len(arrays)

    def out_shape(self):
        return [jax.ShapeDtypeStruct(a.shape, a.dtype) for a in self.arrays]

    def scratch(self):
        return [pltpu.SemaphoreType.DMA((self.n, 3)), pltpu.SemaphoreType.DMA((self.n, 3)),
                pltpu.SemaphoreType.DMA((self.n,))]

    def _copies(self, ins, outs, sems):
        send_sems, recv_sems, local_sems = sems
        x, y, c = _place()
        mine = 2 * x + y
        sent, arriving = [], []
        for w in range(self.n):
            for k in range(1, 4):
                tx, ty = (1 - x) if k & 2 else x, (1 - y) if k & 1 else y
                other = 2 * tx + ty
                sent.append(lambda w=w, k=k, tx=tx, ty=ty, other=other: pltpu.make_async_remote_copy(
                    src_ref=ins[w].at[other], dst_ref=outs[w].at[mine], send_sem=send_sems.at[w, k - 1],
                    recv_sem=recv_sems.at[w, k - 1], device_id=(tx, ty, c), device_id_type=MESH))
                arriving.append(lambda w=w, k=k, tx=tx, ty=ty, other=other: pltpu.make_async_remote_copy(
                    src_ref=ins[w].at[mine], dst_ref=outs[w].at[other], send_sem=send_sems.at[w, k - 1],
                    recv_sem=recv_sems.at[w, k - 1], device_id=(tx, ty, c), device_id_type=MESH))
        own = [lambda w=w: pltpu.make_async_copy(ins[w].at[mine], outs[w].at[mine], local_sems.at[w])
               for w in range(self.n)]
        return own, sent, arriving

    def start(self, ins, outs, sems):
        own, sent, _ = self._copies(ins, outs, sems)
        for cp in own + sent:
            cp().start()

    def wait(self, ins, outs, sems):
        own, sent, arriving = self._copies(ins, outs, sems)
        for cp in arriving:
            cp().wait_recv()
        for cp in sent:
            cp().wait_send()
        for cp in own:
            cp().wait()


class _ChipGather:
    def __init__(self, arrays, parts=None, into=None):
        self.arrays, self.n, self.into = list(arrays), len(arrays), into
        self.parts = parts or [(0, 1)] * self.n

    def out_shape(self):
        return [jax.ShapeDtypeStruct((N_DEV * a.shape[0],) + a.shape[1:], a.dtype) for a in self.arrays]

    def scratch(self):
        return [pltpu.SemaphoreType.DMA((self.n, N_DEV - 1)), pltpu.SemaphoreType.DMA((self.n, N_DEV - 1)),
                pltpu.SemaphoreType.DMA((self.n,))]

    def _parts(self, ins, outs, sems):
        send_sems, recv_sems, local_sems = sems
        x, y, c = _place()
        me, sibling = (x, y, c), (x, y, 1 - c)
        chips = [(1 - x, y), (x, 1 - y), (1 - x, 1 - y)]

        def rows(w, place, whole):
            (index, count), r = self.parts[w], ins[w].shape[0]
            lo, size = (0, r) if whole else (index * (r // count), r // count)
            align = 16 if ins[w].dtype == BF16 else 8
            return outs[w].at[pl.ds(pl.multiple_of(_index(place) * r + lo, align), size)]

        def mine(w, whole):
            (index, count), r = self.parts[w], ins[w].shape[0]
            return ins[w] if whole or count == 1 else ins[w].at[pl.ds(index * (r // count), r // count)]

        def copy(w, k, block, to, own=False):
            whole = k == 0
            return pltpu.make_async_remote_copy(src_ref=mine(w, whole) if own else rows(w, block, whole),
                                                dst_ref=rows(w, block, whole),
                                                send_sem=send_sems.at[w, k], recv_sem=recv_sems.at[w, k],
                                                device_id=to, device_id_type=MESH)

        def local(w):
            return pltpu.make_async_copy(ins[w], rows(w, me, True), local_sems.at[w])

        return me, sibling, chips, c, copy, local, [index == 0 for index, _ in self.parts]

    def start(self, ins, outs, sems):
        me, sibling, chips, c, copy, local, places_own = self._parts(ins, outs, sems)
        for w in range(self.n):
            if places_own[w]:
                local(w).start()
                copy(w, 0, me, sibling, own=True).start()
            for j, chip in enumerate(chips):
                copy(w, 1 + j, me, (*chip, c), own=True).start()

    def wait(self, ins, outs, sems):
        me, sibling, chips, c, copy, local, places_own = self._parts(ins, outs, sems)
        for w in range(self.n):
            for j, chip in enumerate(chips):
                copy(w, 1 + j, (*chip, c), me).wait_recv()
                copy(w, 4 + j, (*chip, c), sibling).start()
        for w in range(self.n):
            if places_own[w]:
                copy(w, 0, sibling, me).wait_recv()
                copy(w, 0, me, sibling, own=True).wait_send()
                local(w).wait()
            for j, chip in enumerate(chips):
                copy(w, 4 + j, (*chip, 1 - c), me).wait_recv()
                copy(w, 1 + j, me, (*chip, c), own=True).wait_send()
                copy(w, 4 + j, (*chip, c), sibling).wait_send()


def _call(body, *, name, grid, in_specs, out_specs, out_shape, scratch=(), semantics, args, exchange=None,
          vmem=VMEM_LIMIT):
    if exchange is None:
        return pl.pallas_call(body, name=name, grid=grid, in_specs=in_specs, out_specs=out_specs, out_shape=out_shape,
                              scratch_shapes=list(scratch),
                              compiler_params=_params(*semantics, vmem=vmem))(*args), None
    n_in, n_out, n_scr, nx = len(in_specs), len(out_specs), len(scratch), exchange.n
    into = list(getattr(exchange, "into", None) or [])

    def full_body(*refs):
        ins, refs = refs[:n_in], refs[n_in:]
        x_in, refs = refs[:nx], refs[nx + len(into):]
        outs, refs = refs[:n_out], refs[n_out:]
        x_out, refs = refs[:nx], refs[nx:]
        scr, sems = refs[:n_scr], refs[n_scr:]
        first, last = True, True
        for axis, size in enumerate(grid):
            first = jnp.logical_and(first, pl.program_id(axis) == 0)
            last = jnp.logical_and(last, pl.program_id(axis) == size - 1)
        if grid:
            pl.when(first)(lambda: exchange.start(x_in, x_out, sems))
        else:
            exchange.start(x_in, x_out, sems)
        body(*ins, *outs, *scr)
        if grid:
            pl.when(last)(lambda: exchange.wait(x_in, x_out, sems))
        else:
            exchange.wait(x_in, x_out, sems)

    hbm = pl.BlockSpec(memory_space=pltpu.HBM)
    res = pl.pallas_call(
        full_body, name=name, grid=grid,
        in_specs=list(in_specs) + [hbm] * (nx + len(into)), out_specs=list(out_specs) + [hbm] * nx,
        out_shape=list(out_shape) + exchange.out_shape(),
        scratch_shapes=list(scratch) + exchange.scratch(),
        input_output_aliases={n_in + nx + w: n_out + w for w in range(len(into))},
        compiler_params=_params(*(["arbitrary"] * len(grid)), vmem=vmem),
    )(*args, *exchange.arrays, *into)
    return res[:n_out], res[n_out:]


def _alone(exchange, name):
    return _call(lambda: None, name=name, grid=(), in_specs=[], out_specs=[], out_shape=[], semantics=(),
                 args=(), exchange=exchange)[1]


def _mm(a, b, *, ta=False, tb=False, out_dtype, tm, tn, tk, name, exchange=None):
    m, k = (a.shape[1], a.shape[0]) if ta else a.shape
    n = b.shape[0] if tb else b.shape[1]
    assert k == (b.shape[1] if tb else b.shape[0])
    tm, tn, tk = min(tm, m), min(tn, n), min(tk, k)
    assert m % tm == 0 and n % tn == 0 and k % tk == 0, (name, m, n, k)
    nk = k // tk
    dims = (((0 if ta else 1,), (1 if tb else 0,)), ((), ()))

    def body(a_ref, b_ref, o_ref, *acc):
        prod = lax.dot_general(a_ref[...].astype(BF16), b_ref[...].astype(BF16), dims, preferred_element_type=F32)
        if nk == 1:
            o_ref[...] = prod.astype(o_ref.dtype)
            return
        acc_ref, kk = acc[0], pl.program_id(2)

        @pl.when(kk == 0)
        def _():
            acc_ref[...] = prod

        @pl.when((kk > 0) & (kk < nk - 1))
        def _():
            acc_ref[...] += prod

        @pl.when(kk == nk - 1)
        def _():
            o_ref[...] = (acc_ref[...] + prod).astype(o_ref.dtype)

    a_spec = (pl.BlockSpec((tk, tm), lambda i, j, kk: (kk, i)) if ta
              else pl.BlockSpec((tm, tk), lambda i, j, kk: (i, kk)))
    b_spec = (pl.BlockSpec((tn, tk), lambda i, j, kk: (j, kk)) if tb
              else pl.BlockSpec((tk, tn), lambda i, j, kk: (kk, j)))
    (out,), moved = _call(
        body, name=name, grid=(m // tm, n // tn, nk),
        in_specs=[a_spec, b_spec],
        out_specs=[pl.BlockSpec((tm, tn), lambda i, j, kk: (i, j))],
        out_shape=[pltpu.HBM((m, n), out_dtype)],
        scratch=[pltpu.VMEM((tm, tn), F32)] if nk > 1 else [],
        semantics=("parallel", "parallel", "arbitrary"), args=(a, b), exchange=exchange)
    return out if exchange is None else (out, moved)


def _mm_pieces(pieces, b, *, ta, out_dtype, tm, tn, tk, name, exchange=None):
    rows, n = pieces[0].shape[0], b.shape[1]
    step = tm if ta else tk
    assert all(p.shape[0] == rows and p.shape[1] % step == 0 for p in pieces), name
    edges = [int(e) for e in np.cumsum([0] + [p.shape[1] // step for p in pieces])]
    total = edges[-1] * step
    m, k = (total, rows) if ta else (rows, total)
    assert b.shape[0] == k and m % tm == 0 and n % tn == 0 and k % tk == 0, name
    nk, npieces = k // tk, len(pieces)
    dims = (((0 if ta else 1,), (0,)), ((), ()))

    def body(*refs):
        a_refs, (b_ref, o_ref, acc_ref) = refs[:npieces], refs[npieces:]
        kk = pl.program_id(2)
        pos = pl.program_id(0) if ta else kk

        @pl.when(kk == 0)
        def _():
            acc_ref[...] = jnp.zeros_like(acc_ref)

        for p, a_ref in enumerate(a_refs):
            @pl.when((pos >= edges[p]) & (pos < edges[p + 1]))
            def _(a_ref=a_ref):
                acc_ref[...] += lax.dot_general(a_ref[...], b_ref[...], dims, preferred_element_type=F32)

        @pl.when(kk == nk - 1)
        def _():
            o_ref[...] = acc_ref[...].astype(o_ref.dtype)

    def a_spec(p):
        lo, last = edges[p], edges[p + 1] - edges[p] - 1
        if ta:
            def index(i, j, kk):
                inside = (i >= lo) & (i <= lo + last)
                return jnp.where(inside, kk, 0), jnp.clip(i - lo, 0, last)
            return pl.BlockSpec((tk, tm), index)
        return pl.BlockSpec((tm, tk), lambda i, j, kk: (i, jnp.clip(kk - lo, 0, last)))

    (out,), moved = _call(
        body, name=name, grid=(m // tm, n // tn, nk),
        in_specs=[a_spec(p) for p in range(npieces)] + [pl.BlockSpec((tk, tn), lambda i, j, kk: (kk, j))],
        out_specs=[pl.BlockSpec((tm, tn), lambda i, j, kk: (i, j))],
        out_shape=[pltpu.HBM((m, n), out_dtype)],
        scratch=[pltpu.VMEM((tm, tn), F32)],
        semantics=("parallel", "parallel", "arbitrary"), args=(*pieces, b), exchange=exchange)
    return out if exchange is None else (out, moved)


def _rms_fwd(x2, g, exchange):
    t = x2.shape[0]
    tm = min(512, t)

    def body(x_ref, g_ref, o_ref):
        x = x_ref[...]
        r = lax.rsqrt(jnp.mean(x * x, axis=-1, keepdims=True) + EPS)
        o_ref[...] = (x * r * g_ref[...]).astype(o_ref.dtype)

    return _call(
        body, name="rms_in_fwd", grid=(t // tm,),
        in_specs=[pl.BlockSpec((tm, D_MODEL), lambda i: (i, 0)), pl.BlockSpec((1, D_MODEL), lambda i: (0, 0))],
        out_specs=[pl.BlockSpec((tm, D_MODEL), lambda i: (i, 0))],
        out_shape=[jax.ShapeDtypeStruct((t, D_MODEL), BF16)],
        semantics=("parallel",), args=(x2, g), exchange=exchange)


def _decay(lg):
    row = lax.broadcasted_iota(jnp.int32, (RET_TILE, RET_TILE), 0)
    col = lax.broadcasted_iota(jnp.int32, (RET_TILE, RET_TILE), 1)
    within = jnp.exp(lg * jnp.abs(row - col).astype(F32))
    inside = jnp.where((col >> 6) <= (row >> 6), within, 0.0)
    pos = lax.broadcasted_iota(jnp.int32, (RET_TILE, 1), 0).astype(F32)
    q_dec = jnp.exp(lg * (pos + 1.0))
    k_dec = jnp.exp(lg * (RET_TILE - 1.0 - pos))
    tile_dec = jnp.exp(lg * float(RET_TILE))
    return inside, q_dec, k_dec, tile_dec


def _scaled(a_bf16, dec):
    return (a_bf16.astype(F32) * dec).astype(BF16)


def _ret_fwd(proj, cs, sn, lg_arr, batch, seq, exchange):
    t = batch * seq
    nt = seq // RET_TILE

    def body(q_ref, k_ref, v_ref, rg_ref, cs_ref, sn_ref, lg_ref, gro_ref, o_ref, qr_ref, kr_ref):
        lg = lg_ref[:, 0:1]
        cs_t, sn_t = cs_ref[...], sn_ref[...]
        q = q_ref[...].astype(F32)
        k = k_ref[...].astype(F32)
        qr_ref[...] = (q * cs_t + pltpu.roll(q, 64, 1) * sn_t).astype(BF16)
        kr_ref[...] = ((k * cs_t + pltpu.roll(k, 64, 1) * sn_t) * (RET_KEY_DIM ** -0.5)).astype(BF16)
        inside, q_dec, k_dec, tile_dec = _decay(lg)
        state = jnp.zeros((RET_KEY_DIM, RET_VAL_DIM), F32)
        for i in range(nt):
            rows = slice(i * RET_TILE, (i + 1) * RET_TILE)
            qi, ki, vi = qr_ref[rows, :], kr_ref[rows, :], v_ref[rows, :]
            acc = _dot((_dot_nt(qi, ki) * inside).astype(BF16), vi)
            if i > 0:
                acc = acc + _dot(_scaled(qi, q_dec), state.astype(BF16))
            if i < nt - 1:
                state = state * tile_dec + _dot_tn(_scaled(ki, k_dec), vi)
            o_ref[rows, :] = acc
            xc = acc - jnp.mean(acc, axis=-1, keepdims=True)
            nrm = xc * lax.rsqrt(jnp.mean(xc * xc, axis=-1, keepdims=True) + EPS)
            rg = rg_ref[rows, :].astype(F32)
            gro_ref[rows, :] = (rg * _sigmoid(rg) * nrm).astype(BF16)

    def col(base, width):
        return lambda b, h: (b, base // width + h)

    return _call(
        body, name="ret_fwd", grid=(batch, RET_HEADS),
        in_specs=[pl.BlockSpec((seq, RET_KEY_DIM), col(C_RQ, RET_KEY_DIM)),
                  pl.BlockSpec((seq, RET_KEY_DIM), col(C_RK, RET_KEY_DIM)),
                  pl.BlockSpec((seq, RET_VAL_DIM), col(C_RV, RET_VAL_DIM)),
                  pl.BlockSpec((seq, RET_VAL_DIM), col(C_RG, RET_VAL_DIM)),
                  pl.BlockSpec((seq, RET_KEY_DIM), lambda b, h: (0, 0)),
                  pl.BlockSpec((seq, RET_KEY_DIM), lambda b, h: (0, 0)),
                  pl.BlockSpec((None, 1, LANES), lambda b, h: (h, 0, 0))],
        out_specs=[pl.BlockSpec((seq, RET_VAL_DIM), lambda b, h: (b, h)),
                   pl.BlockSpec((seq, RET_VAL_DIM), lambda b, h: (b, h)),
                   pl.BlockSpec((seq, RET_KEY_DIM), lambda b, h: (b, h)),
                   pl.BlockSpec((seq, RET_KEY_DIM), lambda b, h: (b, h))],
        out_shape=[jax.ShapeDtypeStruct((t, RET_HEADS * RET_VAL_DIM), BF16),
                   jax.ShapeDtypeStruct((t, RET_HEADS * RET_VAL_DIM), F32),
                   jax.ShapeDtypeStruct((t, RET_HEADS * RET_KEY_DIM), BF16),
                   jax.ShapeDtypeStruct((t, RET_HEADS * RET_KEY_DIM), BF16)],
        semantics=("parallel", "parallel"), args=(proj, proj, proj, proj, cs, sn, lg_arr), exchange=exchange)


def _att_bias(w_ref, bias_ref):
    n_i = lax.broadcasted_iota(jnp.int32, (ATT_Q, BIAS_LEN), 0)
    qc = lax.broadcasted_iota(jnp.int32, (ATT_Q, ATT_WIN), 0) >> 6
    kc = lax.broadcasted_iota(jnp.int32, (ATT_Q, ATT_WIN), 1) >> 6
    dc = qc + BAND_CHUNKS - kc
    band = (dc >= 0) & (dc <= BAND_CHUNKS)
    key = lax.broadcasted_iota(jnp.int32, (ATT_Q, ATT_WIN), 1)
    for e in range(2):
        xw = jnp.broadcast_to(w_ref[e:e + 1, :], (ATT_Q, BIAS_LEN))
        for bit in range(8):
            xw = jnp.where(((n_i >> bit) & 1) == 1, pltpu.roll(xw, 1 << bit, 1), xw)
        bias = jnp.where(band, xw[:, BIAS_LEN - ATT_WIN:], NEG_INF)
        for first in range(ATT_STARTS):
            bias_ref[first, e] = jnp.where(key + (first * ATT_Q - ATT_PAD) >= 0, bias, NEG_INF)
        bias_ref[ATT_STARTS, e] = bias


ATT_PAIRS = 2
ATT_COLS = ATT_PAIRS * LANES


def _att_specs(batch, seq):
    ni = seq // ATT_Q
    q_spec = pl.BlockSpec((ATT_Q, ATT_COLS), lambda g, b, i: (b * ni + i, C_AQ // ATT_COLS + g))
    k_spec = pl.BlockSpec((seq, ATT_COLS), lambda g, b, i: (b, C_AK // ATT_COLS + g))
    v_spec = pl.BlockSpec((seq, ATT_COLS), lambda g, b, i: (b, C_AV // ATT_COLS + g))
    w_spec = pl.BlockSpec((ATT_PAIRS, 2, BIAS_LEN), lambda g, b, i: (g, 0, 0))
    b_spec = pl.BlockSpec((ATT_PAIRS, ATT_STARTS + 1, 2, ATT_Q, ATT_WIN), lambda g, b, i: (g, 0, 0, 0, 0))
    pad = pltpu.VMEM((seq + ATT_PAD, ATT_COLS), BF16)
    return ni, q_spec, k_spec, v_spec, w_spec, b_spec, pad


def _att_bias_tiles(wvec, exchange):
    (tiles,), moved = _call(
        lambda w_ref, o_ref: _att_bias(w_ref, o_ref), name="att_bias", grid=(ATT_HEADS // 2,),
        in_specs=[pl.BlockSpec((None, 2, BIAS_LEN), lambda hp: (hp, 0, 0))],
        out_specs=[pl.BlockSpec((None, ATT_STARTS + 1, 2, ATT_Q, ATT_WIN), lambda hp: (hp, 0, 0, 0, 0))],
        out_shape=[jax.ShapeDtypeStruct((ATT_HEADS // 2, ATT_STARTS + 1, 2, ATT_Q, ATT_WIN), F32)],
        semantics=("parallel",), args=(wvec,), exchange=exchange)
    return tiles, moved


def _att_pad(src_ref, pad_ref):
    pad_ref[:ATT_PAD, :] = jnp.zeros((ATT_PAD, ATT_COLS), BF16)
    pad_ref[ATT_PAD:, :] = src_ref[...]


def _att_head(q2, sel):
    return jnp.where(sel, q2, jnp.zeros_like(q2)) * 0.125


def _att_softmax_rows(s_ref, bias_ref, rows):
    s = s_ref[rows, :] + bias_ref[rows, :]
    ex = jnp.exp(s - jnp.max(s, axis=-1, keepdims=True))
    return ex, 1.0 / jnp.sum(ex, axis=-1, keepdims=True)


def _att_fwd(proj, bias, batch, seq, exchange):
    ni, q_spec, k_spec, v_spec, _, b_spec, pad = _att_specs(batch, seq)

    def body(q_ref, k_ref, v_ref, bias_ref, o_ref, kp_ref, vp_ref, s_ref, e_ref):
        i = pl.program_id(2)

        @pl.when(i == 0)
        def _():
            _att_pad(k_ref, kp_ref)
            _att_pad(v_ref, vp_ref)

        win = pl.ds(pl.multiple_of(i * ATT_Q, ATT_Q), ATT_WIN)
        lo = lax.broadcasted_iota(jnp.int32, (1, LANES), 1) < 64
        start = jnp.minimum(i, ATT_STARTS)
        for pair in range(ATT_PAIRS):
            cols = slice(pair * LANES, (pair + 1) * LANES)
            k2, v2, q2 = kp_ref[win, cols], vp_ref[win, cols], q_ref[:, cols]
            out = jnp.zeros((ATT_Q, LANES), F32)
            for e in range(2):
                h = 2 * pair + e
                sel = lo if e == 0 else jnp.logical_not(lo)
                s_ref[h] = _dot_nt(_att_head(q2, sel), k2)
                rsum = []
                for c in range(ATT_Q // ATT_ROWS):
                    rows = slice(c * ATT_ROWS, (c + 1) * ATT_ROWS)
                    ex, r = _att_softmax_rows(s_ref.at[h], bias_ref.at[pair, start, e], rows)
                    e_ref[h, rows, :] = ex.astype(BF16)
                    rsum.append(r)
                out = out + _dot(e_ref[h], jnp.where(sel, v2, jnp.zeros_like(v2))) * jnp.concatenate(rsum, axis=0)
            o_ref[:, cols] = out.astype(BF16)

    heads = 2 * ATT_PAIRS
    return _call(
        body, name="att_fwd", grid=(ATT_HEADS // heads, batch, ni),
        in_specs=[q_spec, k_spec, v_spec, b_spec],
        out_specs=[pl.BlockSpec((ATT_Q, ATT_COLS), lambda g, b, i: (b * ni + i, g))],
        out_shape=[jax.ShapeDtypeStruct((batch * seq, ATT_HEADS * 64), BF16)],
        scratch=[pad, pad, pltpu.VMEM((heads, ATT_Q, ATT_WIN), F32), pltpu.VMEM((heads, ATT_Q, ATT_WIN), BF16)],
        semantics=("arbitrary", "arbitrary", "arbitrary"), args=(proj, proj, proj, bias), exchange=exchange)


GL_HALF = 512


def _gl_specs(tm):
    return [pl.BlockSpec((tm, GL_HALF), lambda i, c=C_GL // GL_HALF + j: (i, c)) for j in range(4)]


def _gates(gl_refs, b_ref):
    logits = [ref[...].astype(F32) for ref in gl_refs]
    gr = _sigmoid(jnp.concatenate(logits[:2], axis=1) + b_ref[:, :D_MODEL])
    ga = _sigmoid(jnp.concatenate(logits[2:], axis=1) + b_ref[:, D_MODEL:])
    return gr, ga


def _whole(a):
    return pl.BlockSpec(a.shape, lambda i: (0,) * a.ndim)


def _mix_out_fwd(gro, ao, proj, b_gate, w_ret, w_att_t, x2, w_out, g2):
    t = gro.shape[0]
    tm = min(256, t)

    def body(gro_ref, ao_ref, gl0, gl1, gl2, gl3, b_ref, wr_ref, wa_ref, x_ref, wo_ref, g_ref,
             z_ref, yr_ref, ya_ref, h_ref, hn_ref):
        yr = _dot(gro_ref[...], wr_ref[...])
        ya = _dot_nt(ao_ref[...], wa_ref[...])
        yr_ref[...] = yr.astype(BF16)
        ya_ref[...] = ya.astype(BF16)
        gr, ga = _gates((gl0, gl1, gl2, gl3), b_ref)
        z = (gr * yr + ga * ya).astype(BF16)
        z_ref[...] = z
        h = x_ref[...] + _dot(z, wo_ref[...])
        h_ref[...] = h
        r = lax.rsqrt(jnp.mean(h * h, axis=-1, keepdims=True) + EPS)
        hn_ref[...] = (h * r * g_ref[...]).astype(BF16)

    row = pl.BlockSpec((tm, D_MODEL), lambda i: (i, 0))
    return pl.pallas_call(
        body, name="mix_out_fwd", grid=(t // tm,),
        in_specs=[row, pl.BlockSpec((tm, 512), lambda i: (i, 0)), *_gl_specs(tm),
                  _whole(b_gate), _whole(w_ret), _whole(w_att_t), row, _whole(w_out), _whole(g2)],
        out_specs=[row] * 5,
        out_shape=[jax.ShapeDtypeStruct((t, D_MODEL), BF16)] * 3
                  + [jax.ShapeDtypeStruct((t, D_MODEL), F32), jax.ShapeDtypeStruct((t, D_MODEL), BF16)],
        compiler_params=_params("parallel"),
    )(gro, ao, proj, proj, proj, proj, b_gate, w_ret, w_att_t, x2, w_out, g2)


def _col_chunks(width, chunk=384):
    return [slice(lo, min(lo + chunk, width)) for lo in range(0, width, chunk)]


def _ffn_up(hn, wg_t, wu_t):
    t = hn.shape[0]
    tm, tn = min(512, t), D_FF // 2

    def body(h_ref, wg_ref, wu_ref, g_ref, u_ref, a_ref):
        g = _dot_nt(h_ref[...], wg_ref[...])
        u = _dot_nt(h_ref[...], wu_ref[...])
        g_ref[...] = g.astype(BF16)
        u_ref[...] = u.astype(BF16)
        a_ref[...] = (g * _sigmoid(g) * u).astype(BF16)

    w_spec = pl.BlockSpec((tn, D_MODEL), lambda j, i: (j, 0))
    out = pl.BlockSpec((tm, tn), lambda j, i: (i, j))
    return pl.pallas_call(
        body, name="ffn_up", grid=(D_FF // tn, t // tm),
        in_specs=[pl.BlockSpec((tm, D_MODEL), lambda j, i: (i, 0)), w_spec, w_spec],
        out_specs=[out, out, out],
        out_shape=[jax.ShapeDtypeStruct((t, D_FF), BF16)] * 3,
        compiler_params=_params("parallel", "parallel"),
    )(hn, wg_t, wu_t)


def _ffn_down_loss(a, h1, tgt, w_down, g3):
    t = a.shape[0]
    tm = min(512, t)

    def body(a_ref, h_ref, t_ref, w_ref, g_ref, dh_ref, dhb_ref, loss_ref, dg_ref):
        @pl.when(pl.program_id(0) == 0)
        def _():
            loss_ref[...] = jnp.zeros_like(loss_ref)
            dg_ref[...] = jnp.zeros_like(dg_ref)

        g = g_ref[...]
        h2 = h_ref[...] + _dot(a_ref[...], w_ref[...])
        r = lax.rsqrt(jnp.mean(h2 * h2, axis=-1, keepdims=True) + EPS)
        err = h2 * r * g - t_ref[...]
        loss_ref[...] += jnp.sum(err * err) * (0.5 / D_MODEL)
        dy = err * (1.0 / D_MODEL)
        dh, dg_rows = _rms_bwd(h2, g, dy)
        dg_ref[...] += jnp.sum(dg_rows, axis=0, keepdims=True)
        dh_ref[...] = dh
        dhb_ref[...] = dh.astype(BF16)

    row = pl.BlockSpec((tm, D_MODEL), lambda i: (i, 0))
    vec = pl.BlockSpec((1, D_MODEL), lambda i: (0, 0))
    return pl.pallas_call(
        body, name="ffn_down_loss", grid=(t // tm,),
        in_specs=[pl.BlockSpec((tm, D_FF), lambda i: (i, 0)), row, row,
                  pl.BlockSpec((D_FF, D_MODEL), lambda i: (0, 0)), vec],
        out_specs=[row, row, pl.BlockSpec((1, LANES), lambda i: (0, 0)), vec],
        out_shape=[jax.ShapeDtypeStruct((t, D_MODEL), F32), jax.ShapeDtypeStruct((t, D_MODEL), BF16),
                   jax.ShapeDtypeStruct((1, LANES), F32), jax.ShapeDtypeStruct((1, D_MODEL), F32)],
        compiler_params=_params("arbitrary"),
    )(a, h1, tgt, w_down, g3)


def _ffn_bwd_act(dh2b, w_down, g_act, u_act, exchange):
    t = dh2b.shape[0]
    tm, tn = min(512, t), D_FF // 2

    def body(d_ref, w_ref, g_ref, u_ref, dg_ref, du_ref):
        d = d_ref[...]
        for cols in _col_chunks(tn):
            da = _dot_nt(d, w_ref[cols, :])
            g = g_ref[:, cols].astype(F32)
            u = u_ref[:, cols].astype(F32)
            sg = _sigmoid(g)
            dg_ref[:, cols] = (da * u * sg * (1.0 + g * (1.0 - sg))).astype(BF16)
            du_ref[:, cols] = (da * g * sg).astype(BF16)

    blk = pl.BlockSpec((tm, tn), lambda j, i: (i, j))
    return _call(
        body, name="ffn_bwd_act", grid=(D_FF // tn, t // tm),
        in_specs=[pl.BlockSpec((tm, D_MODEL), lambda j, i: (i, 0)),
                  pl.BlockSpec((tn, D_MODEL), lambda j, i: (j, 0)), blk, blk],
        out_specs=[blk, blk],
        out_shape=[jax.ShapeDtypeStruct((t, D_FF), BF16)] * 2,
        semantics=("parallel", "parallel"), args=(dh2b, w_down, g_act, u_act), exchange=exchange)


def _ffn_bwd_in(dg, du, wg_t, wu_t, h1, dh2, g2, exchange):
    t = dg.shape[0]
    tm, tk = min(512, t), D_FF // 2
    nk = D_FF // tk

    def body(dg_ref, du_ref, wg_ref, wu_ref, h_ref, d2_ref, g_ref, dh_ref, dhb_ref, gn_ref, acc_ref):
        i, kk = pl.program_id(0), pl.program_id(1)

        @pl.when((i == 0) & (kk == 0))
        def _():
            gn_ref[...] = jnp.zeros_like(gn_ref)

        part = _dot(dg_ref[...], wg_ref[...]) + _dot(du_ref[...], wu_ref[...])

        @pl.when(kk == 0)
        def _():
            acc_ref[...] = part

        @pl.when((kk > 0) & (kk < nk - 1))
        def _():
            acc_ref[...] += part

        @pl.when(kk == nk - 1)
        def _():
            dx, dg_rows = _rms_bwd(h_ref[...], g_ref[...], acc_ref[...] + part)
            dh = d2_ref[...] + dx
            dh_ref[...] = dh
            dhb_ref[...] = dh.astype(BF16)
            gn_ref[...] += jnp.sum(dg_rows, axis=0, keepdims=True)

    act = pl.BlockSpec((tm, tk), lambda i, kk: (i, kk))
    wsp = pl.BlockSpec((tk, D_MODEL), lambda i, kk: (kk, 0))
    row = pl.BlockSpec((tm, D_MODEL), lambda i, kk: (i, 0))
    vec = pl.BlockSpec((1, D_MODEL), lambda i, kk: (0, 0))
    return _call(
        body, name="ffn_bwd_in", grid=(t // tm, nk),
        in_specs=[act, act, wsp, wsp, row, row, vec],
        out_specs=[row, row, vec],
        out_shape=[jax.ShapeDtypeStruct((t, D_MODEL), F32), jax.ShapeDtypeStruct((t, D_MODEL), BF16),
                   jax.ShapeDtypeStruct((1, D_MODEL), F32)],
        scratch=[pltpu.VMEM((tm, D_MODEL), F32)],
        semantics=("arbitrary", "arbitrary"), args=(dg, du, wg_t, wu_t, h1, dh2, g2), exchange=exchange)


def _mix_bwd(dh1b, w_out, proj, b_gate, y_ret, y_att, w_ret, w_att_t):
    t = dh1b.shape[0]
    tm = min(256, t)

    def body(d_ref, wo_ref, gl0, gl1, gl2, gl3, b_ref, yr_ref, ya_ref, wr_ref, wa_ref,
             dyr_ref, dya_ref, dglr_ref, dgla_ref, dgro_ref, dao_ref, db_ref):
        @pl.when(pl.program_id(0) == 0)
        def _():
            db_ref[...] = jnp.zeros_like(db_ref)

        dz = _dot_nt(d_ref[...], wo_ref[...])
        gr, ga = _gates((gl0, gl1, gl2, gl3), b_ref)
        dyr = (dz * gr).astype(BF16)
        dya = (dz * ga).astype(BF16)
        dyr_ref[...] = dyr
        dya_ref[...] = dya
        dglr = dz * yr_ref[...].astype(F32) * gr * (1.0 - gr)
        dgla = dz * ya_ref[...].astype(F32) * ga * (1.0 - ga)
        dglr_ref[...] = dglr.astype(BF16)
        dgla_ref[...] = dgla.astype(BF16)
        db_ref[:, :D_MODEL] += jnp.sum(dglr, axis=0, keepdims=True)
        db_ref[:, D_MODEL:] += jnp.sum(dgla, axis=0, keepdims=True)
        dgro_ref[...] = _dot_nt(dyr, wr_ref[...]).astype(BF16)
        dao_ref[...] = _dot(dya, wa_ref[...]).astype(BF16)

    row = pl.BlockSpec((tm, D_MODEL), lambda i: (i, 0))
    half = pl.BlockSpec((tm, 512), lambda i: (i, 0))
    return pl.pallas_call(
        body, name="mix_bwd", grid=(t // tm,),
        in_specs=[row, _whole(w_out), *_gl_specs(tm), _whole(b_gate), row, row, _whole(w_ret), _whole(w_att_t)],
        out_specs=[row, row, row, row, row, half, _whole(b_gate)],
        out_shape=[jax.ShapeDtypeStruct((t, D_MODEL), BF16)] * 5
                  + [jax.ShapeDtypeStruct((t, 512), BF16), jax.ShapeDtypeStruct(b_gate.shape, F32)],
        compiler_params=_params("arbitrary"),
    )(dh1b, w_out, proj, proj, proj, proj, b_gate, y_ret, y_att, w_ret, w_att_t)


def _ret_bwd(dgro, proj, o_ret, qr, kr, cs, sn, lg_arr, batch, seq, exchange):
    t = batch * seq
    nt = seq // RET_TILE

    def body(dgro_ref, rg_ref, o_ref, qr_ref, kr_ref, v_ref, cs_ref, sn_ref, lg_ref,
             dq_ref, dk_ref, dv_ref, drg_ref, do_ref, st_ref):
        lg = lg_ref[:, 0:1]
        inside, q_dec, k_dec, tile_dec = _decay(lg)

        state = jnp.zeros((RET_KEY_DIM, RET_VAL_DIM), F32)
        for i in range(nt - 1):
            rows = slice(i * RET_TILE, (i + 1) * RET_TILE)
            state = state * tile_dec + _dot_tn(_scaled(kr_ref[rows, :], k_dec), v_ref[rows, :])
            st_ref[i + 1] = state.astype(BF16)

        for i in range(nt):
            rows = slice(i * RET_TILE, (i + 1) * RET_TILE)
            o = o_ref[rows, :]
            xc = o - jnp.mean(o, axis=-1, keepdims=True)
            rs = lax.rsqrt(jnp.mean(xc * xc, axis=-1, keepdims=True) + EPS)
            nrm = xc * rs
            rg = rg_ref[rows, :].astype(F32)
            sg = _sigmoid(rg)
            dg = dgro_ref[rows, :].astype(F32)
            drg_ref[rows, :] = (dg * nrm * sg * (1.0 + rg * (1.0 - sg))).astype(BF16)
            dn = dg * rg * sg
            do = rs * (dn - jnp.mean(dn, axis=-1, keepdims=True)
                       - nrm * jnp.mean(dn * nrm, axis=-1, keepdims=True))
            do_ref[rows, :] = do.astype(BF16)

        dstate = jnp.zeros((RET_KEY_DIM, RET_VAL_DIM), F32)
        for i in reversed(range(nt)):
            rows = slice(i * RET_TILE, (i + 1) * RET_TILE)
            qi, ki, vi, doi = qr_ref[rows, :], kr_ref[rows, :], v_ref[rows, :], do_ref[rows, :]
            p = (_dot_nt(qi, ki) * inside).astype(BF16)
            dp = (_dot_nt(doi, vi) * inside).astype(BF16)
            dq = _dot(dp, ki)
            dk = _dot_tn(dp, qi)
            dv = _dot_tn(p, doi)
            if i > 0:
                dq = dq + _dot_nt(doi, st_ref[i]) * q_dec
            if i < nt - 1:
                dsb = dstate.astype(BF16)
                dk = dk + _dot_nt(vi, dsb) * k_dec
                dv = dv + _dot(_scaled(ki, k_dec), dsb)
            if i > 0:
                dstate = dstate * tile_dec + _dot_tn(_scaled(qi, q_dec), doi)
            dq_ref[rows, :] = (dq * cs_ref[rows, :] - pltpu.roll(dq, 64, 1) * sn_ref[rows, :]).astype(BF16)
            dk = (dk * cs_ref[rows, :] - pltpu.roll(dk, 64, 1) * sn_ref[rows, :]) * (RET_KEY_DIM ** -0.5)
            dk_ref[rows, :] = dk.astype(BF16)
            dv_ref[rows, :] = dv.astype(BF16)

    key = pl.BlockSpec((seq, RET_KEY_DIM), lambda b, h: (b, h))
    val = pl.BlockSpec((seq, RET_VAL_DIM), lambda b, h: (b, h))
    tab = pl.BlockSpec((seq, RET_KEY_DIM), lambda b, h: (0, 0))
    return _call(
        body, name="ret_bwd", grid=(batch, RET_HEADS),
        in_specs=[val, pl.BlockSpec((seq, RET_VAL_DIM), lambda b, h: (b, C_RG // RET_VAL_DIM + h)), val, key, key,
                  pl.BlockSpec((seq, RET_VAL_DIM), lambda b, h: (b, C_RV // RET_VAL_DIM + h)), tab, tab,
                  pl.BlockSpec((None, 1, LANES), lambda b, h: (h, 0, 0))],
        out_specs=[key, key, val, val],
        out_shape=[jax.ShapeDtypeStruct((t, RET_HEADS * RET_KEY_DIM), BF16)] * 2
                  + [jax.ShapeDtypeStruct((t, RET_HEADS * RET_VAL_DIM), BF16)] * 2,
        scratch=[pltpu.VMEM((seq, RET_VAL_DIM), BF16), pltpu.VMEM((nt, RET_KEY_DIM, RET_VAL_DIM), BF16)],
        semantics=("parallel", "parallel"), args=(dgro, proj, o_ret, qr, kr, proj, cs, sn, lg_arr),
        exchange=exchange)


def _att_bwd(proj, bias, dao, batch, seq, exchange):
    ni, q_spec, k_spec, v_spec, w_spec, b_spec, pad = _att_specs(batch, seq)
    t = batch * seq

    def body(q_ref, k_ref, v_ref, bias_ref, do_ref, dq_ref, dk_ref, dv_ref, dw_ref,
             dbias_ref, dk_acc, dv_acc, kp_ref, vp_ref, s_ref, dp_ref, e_ref, ds_ref):
        b, i = pl.program_id(1), pl.program_id(2)

        @pl.when((b == 0) & (i == 0))
        def _():
            dbias_ref[...] = jnp.zeros_like(dbias_ref)

        @pl.when(i == 0)
        def _():
            _att_pad(k_ref, kp_ref)
            _att_pad(v_ref, vp_ref)
            dk_acc[...] = jnp.zeros_like(dk_acc)
            dv_acc[...] = jnp.zeros_like(dv_acc)

        win = pl.ds(pl.multiple_of(i * ATT_Q, ATT_Q), ATT_WIN)
        lo = lax.broadcasted_iota(jnp.int32, (1, LANES), 1) < 64
        start = jnp.minimum(i, ATT_STARTS)
        for pair in range(ATT_PAIRS):
            cols = slice(pair * LANES, (pair + 1) * LANES)
            k2, v2, q2, do2 = kp_ref[win, cols], vp_ref[win, cols], q_ref[:, cols], do_ref[:, cols]
            dq = jnp.zeros((ATT_Q, LANES), F32)
            dk = jnp.zeros((LANES, ATT_WIN), F32)
            dv = jnp.zeros((LANES, ATT_WIN), F32)
            for e in range(2):
                h = 2 * pair + e
                sel = lo if e == 0 else jnp.logical_not(lo)
                qm = _att_head(q2, sel)
                dom = jnp.where(sel, do2, jnp.zeros_like(do2))
                s_ref[h] = _dot_nt(qm, k2)
                dp_ref[h] = _dot_nt(dom, v2)
                rsum = []
                for c in range(ATT_Q // ATT_ROWS):
                    rows = slice(c * ATT_ROWS, (c + 1) * ATT_ROWS)
                    ex, r = _att_softmax_rows(s_ref.at[h], bias_ref.at[pair, start, e], rows)
                    dp = dp_ref[h, rows, :]
                    mean = jnp.sum(dp * ex, axis=-1, keepdims=True) * r
                    ds = ex * ((dp - mean) * r)
                    dbias_ref[h, rows, :] += ds
                    ds_ref[h, rows, :] = ds.astype(BF16)
                    e_ref[h, rows, :] = ex.astype(BF16)
                    rsum.append(r)
                dq = dq + _dot(ds_ref[h], jnp.where(sel, k2, jnp.zeros_like(k2)))
                dk = dk + _dot_tn(qm, ds_ref[h])
                dv = dv + _dot_tn((dom.astype(F32) * jnp.concatenate(rsum, axis=0)).astype(BF16), e_ref[h])
            dq_ref[:, cols] = (dq * 0.125).astype(BF16)
            dk_acc[cols, win] += dk
            dv_acc[cols, win] += dv

        @pl.when(i == ni - 1)
        def _():
            dk_ref[...] = dk_acc[:, ATT_PAD:].T.astype(BF16)
            dv_ref[...] = dv_acc[:, ATT_PAD:].T.astype(BF16)

        @pl.when((b == batch - 1) & (i == ni - 1))
        def _():
            n_i = lax.broadcasted_iota(jnp.int32, (ATT_Q, BIAS_LEN), 0)
            for h in range(heads):
                xw = jnp.concatenate([jnp.zeros((ATT_Q, BIAS_LEN - ATT_WIN), F32), dbias_ref[h]], axis=1)
                for bit in range(8):
                    xw = jnp.where(((n_i >> bit) & 1) == 1, pltpu.roll(xw, BIAS_LEN - (1 << bit), 1), xw)
                dw_ref[h // 2, h % 2:h % 2 + 1, :] = jnp.sum(xw, axis=0, keepdims=True)

    heads = 2 * ATT_PAIRS
    seq_blk = pl.BlockSpec((seq, ATT_COLS), lambda g, b, i: (b, g))
    q_out = pl.BlockSpec((ATT_Q, ATT_COLS), lambda g, b, i: (b * ni + i, g))
    tile_f32, tile_bf16 = pltpu.VMEM((heads, ATT_Q, ATT_WIN), F32), pltpu.VMEM((heads, ATT_Q, ATT_WIN), BF16)
    acc = pltpu.VMEM((ATT_COLS, seq + ATT_PAD), F32)
    return _call(
        body, name="att_bwd", grid=(ATT_HEADS // heads, batch, ni),
        in_specs=[q_spec, k_spec, v_spec, b_spec, q_out],
        out_specs=[q_out, seq_blk, seq_blk, w_spec],
        out_shape=[jax.ShapeDtypeStruct((t, 512), BF16)] * 3
                  + [jax.ShapeDtypeStruct((ATT_HEADS // 2, 2, BIAS_LEN), F32)],
        scratch=[tile_f32, acc, acc, pad, pad, tile_f32, tile_f32, tile_bf16, tile_bf16],
        semantics=("arbitrary", "arbitrary", "arbitrary"), args=(proj, proj, proj, bias, dao), exchange=exchange,
        vmem=VMEM_LIMIT_ATT_BWD)


def _rms_in_bwd(x2, dxn, dh1, g1):
    t = x2.shape[0]
    tm = min(512, t)

    def body(x_ref, d_ref, h_ref, g_ref, dx_ref, dg_ref):
        @pl.when(pl.program_id(0) == 0)
        def _():
            dg_ref[...] = jnp.zeros_like(dg_ref)

        dx, dg_rows = _rms_bwd(x_ref[...], g_ref[...], d_ref[...])
        dx_ref[...] = h_ref[...] + dx
        dg_ref[...] += jnp.sum(dg_rows, axis=0, keepdims=True)

    row = pl.BlockSpec((tm, D_MODEL), lambda i: (i, 0))
    vec = pl.BlockSpec((1, D_MODEL), lambda i: (0, 0))
    return pl.pallas_call(
        body, name="rms_in_bwd", grid=(t // tm,),
        in_specs=[row, row, row, vec], out_specs=[row, vec],
        out_shape=[jax.ShapeDtypeStruct((t, D_MODEL), F32), jax.ShapeDtypeStruct((1, D_MODEL), F32)],
        compiler_params=_params("arbitrary"),
    )(x2, dxn, dh1, g1)


def _pack_small(dg1, dbr, dba, dg2, dg3, dw, loss):
    def body(a_ref, b_ref, c_ref, d_ref, e_ref, w_ref, l_ref, o_ref):
        o_ref[...] = jnp.zeros_like(o_ref)
        for r, ref in enumerate((a_ref, b_ref, c_ref, d_ref, e_ref)):
            o_ref[r:r + 1, :] = ref[...]
        o_ref[5:6, 0:LANES] = l_ref[...]
        for hp in range(ATT_HEADS // 2):
            o_ref[8 + 2 * hp:10 + 2 * hp, :] = w_ref[hp]

    return pl.pallas_call(body, name="pack_small",
                          out_shape=jax.ShapeDtypeStruct((16, D_MODEL), F32))(dg1, dbr, dba, dg2, dg3, dw, loss)


def _rotary_tables(seq):
    freqs = ROPE_BASE ** (-jnp.arange(0, RET_KEY_DIM, 2, dtype=F32) / RET_KEY_DIM)
    ang = jnp.arange(seq, dtype=F32)[:, None] * freqs[None, :]
    cos, sin = jnp.cos(ang), jnp.sin(ang)
    return jnp.concatenate([cos, cos], axis=1), jnp.concatenate([-sin, sin], axis=1)


def _bias_rows(rel_bias):
    n_far = BIAS_LEN - ATT_Q - MAX_REL + 1
    n_near = BIAS_LEN - n_far - (N_REL - 2)
    w = jnp.concatenate([jnp.broadcast_to(rel_bias[:, N_REL - 1:], (ATT_HEADS, n_far)),
                         rel_bias[:, 1:N_REL - 1][:, ::-1],
                         jnp.broadcast_to(rel_bias[:, :1], (ATT_HEADS, n_near))], axis=1)
    return w.reshape(ATT_HEADS // 2, 2, BIAS_LEN)


def _bias_rows_bwd(dw):
    n_far = BIAS_LEN - ATT_Q - MAX_REL + 1
    mid = dw[:, n_far:n_far + N_REL - 2][:, ::-1]
    return jnp.concatenate([jnp.sum(dw[:, n_far + N_REL - 2:], axis=1, keepdims=True), mid,
                            jnp.sum(dw[:, :n_far], axis=1, keepdims=True)], axis=1)


def _step(x, tgt, norm_mix, b_gate, norm_ffn, norm_final, rel_bias_shard, shard):
    batch, seq, _ = x.shape
    t = batch * seq
    n_rb = rel_bias_shard.shape[-1]
    x2, tgt2 = x.reshape(t, D_MODEL), tgt.reshape(t, D_MODEL)
    g3 = norm_final.reshape(1, D_MODEL)
    cs, sn = _rotary_tables(seq)
    lg = np.log(1.0 - 2.0 ** (-5.0 - np.arange(RET_HEADS, dtype=np.float32))).astype(np.float32)
    lg_arr = jnp.asarray(np.broadcast_to(lg[:, None, None], (RET_HEADS, 1, LANES)))

    def gather(*names):
        return _ChipGather([shard[nm] for nm in names])

    def scatter(*grads):
        return _Exchange(grads, scatter=True)

    rb_pad = jnp.pad(rel_bias_shard, ((0, 0), (0, LANES - n_rb)))
    (xn,), (w_in_half, rb_full) = _rms_fwd(x2, norm_mix,
                                           _ChipGather([shard["w_in_t"], rb_pad], parts=[(0, 2), (0, 1)]))
    rb_full = rb_full.reshape(N_DEV, ATT_HEADS, LANES)[:, :, :n_rb]
    bias, (w_in_t,) = _att_bias_tiles(
        _bias_rows(jnp.transpose(rb_full, (1, 0, 2)).reshape(ATT_HEADS, N_DEV * n_rb)),
        _ChipGather([shard["w_in_t"]], parts=[(1, 2)], into=[w_in_half]))
    proj, (w_ret, w_att_t, w_out, w_gate_t) = _mm(
        xn, w_in_t, tb=True, out_dtype=BF16, tm=1024, tn=1664, tk=1024, name="proj",
        exchange=gather("w_ret", "w_att_t", "w_out", "w_gate_t"))
    (gro, o_ret, qr, kr), _ = _ret_fwd(proj, cs, sn, lg_arr, batch, seq, None)
    (ao,), (w_up_t, w_down) = _att_fwd(proj, bias, batch, seq, gather("w_up_t", "w_down"))
    z, y_ret, y_att, h1, hn = _mix_out_fwd(gro, ao, proj, b_gate, w_ret, w_att_t, x2, w_out, norm_ffn)
    g_act, u_act, a_act = _ffn_up(hn, w_gate_t, w_up_t)
    dh2, dh2b, loss, dg3 = _ffn_down_loss(a_act, h1, tgt2, w_down, g3)

    wg = dict(out_dtype=BF16, tn=1024, ta=True)
    slots = {}
    dw_down = _mm(a_act, dh2b, tm=1408, tk=1024, name="dw_down", **wg)
    (d_gact, d_uact), _ = _ffn_bwd_act(dh2b, w_down, g_act, u_act, None)
    dw_gate = _mm(d_gact, hn, tm=1408, tk=1024, name="dw_gate", **wg)
    dw_up = _mm(d_uact, hn, tm=1408, tk=1024, name="dw_up", **wg)
    (dh1, dh1b, dg2), (slots["w_down"],) = _ffn_bwd_in(d_gact, d_uact, w_gate_t, w_up_t, h1, dh2, norm_ffn,
                                                     scatter(dw_down))
    dw_out = _mm(z, dh1b, tm=1024, tk=2048, name="dw_out", **wg)
    dyr, dya, dglr, dgla, dgro, dao, db = _mix_bwd(dh1b, w_out, proj, b_gate, y_ret, y_att, w_ret, w_att_t)
    dw_ret = _mm(gro, dyr, tm=1024, tk=2048, name="dw_ret", **wg)
    dw_att = _mm(dya, ao, tm=1024, tk=2048, name="dw_att", **wg)
    (drq, drk, drv, drg), _ = _ret_bwd(dgro, proj, o_ret, qr, kr, cs, sn, lg_arr, batch, seq, None)
    (daq, dak, dav, dw), (slots["w_gate_t"], slots["w_out"], slots["w_ret"], slots["w_att_t"]) = _att_bwd(
        proj, bias, dao, batch, seq, scatter(dw_gate, dw_out, dw_ret, dw_att))
    dproj = [drq, drk, drv, drg, daq, dak, dav, dglr, dgla]
    dw_in, (slots["w_up_t"],) = _mm_pieces(dproj, xn, ta=True, out_dtype=BF16, tm=512, tn=1024, tk=1024,
                                           name="dw_in", exchange=scatter(dw_up))
    (dw_in_sibling,) = _alone(_PairSwap([dw_in]), "swap_w_in")
    dw_in_pairs = _pair_add(dw_in, dw_in_sibling, "pair_w_in")
    dxn, (slots["w_in_t"],) = _mm_pieces(dproj, w_in_t, ta=False, out_dtype=F32, tm=1024, tn=1024, tk=512, name="dxn",
                                  exchange=_ChipScatter([dw_in_pairs]))
    dx, dg1 = _rms_in_bwd(x2, dxn, dh1, norm_mix)
    small = _pack_small(dg1, db[:, :D_MODEL], db[:, D_MODEL:], dg2, dg3, dw, loss)
    (small_slots,) = _alone(_ChipGather([small]), "gather_small")
    return dx.reshape(batch, seq, D_MODEL), slots, small_slots.reshape(N_DEV, 16, D_MODEL)


def _row_tile(r, c):
    return max(d for d in range(16, r + 1, 16) if r % d == 0 and (d * c <= 256 * 1024 or d == 16))


def _pair_add(grad, got, name):
    _, r, c = got.shape
    tr = r
    core = lax.axis_index("c").astype(jnp.int32).reshape(1)

    def body(core_ref, g_ref, a_ref, o_ref):
        o_ref[...] = (g_ref[...].astype(F32) + a_ref[...].astype(F32)).astype(o_ref.dtype)

    blk = pl.BlockSpec((None, tr, c), lambda q, i, core_ref: (q, i, 0))
    return pl.pallas_call(
        body, name=name,
        grid_spec=pltpu.PrefetchScalarGridSpec(
            num_scalar_prefetch=1, grid=(4, r // tr),
            in_specs=[pl.BlockSpec((None, None, tr, c), lambda q, i, core_ref: (q, core_ref[0], i, 0)), blk],
            out_specs=blk),
        out_shape=jax.ShapeDtypeStruct(got.shape, got.dtype),
        compiler_params=_params("parallel", "parallel"),
    )(core, grad.reshape(4, 2, r, c), got)


def _sum_slots(slots, name):
    n, r, c = slots.shape
    tr = _row_tile(r, c)

    def body(s_ref, o_ref):
        acc = s_ref[0].astype(F32)
        for s in range(1, n):
            acc = acc + s_ref[s].astype(F32)
        o_ref[...] = acc

    return pl.pallas_call(
        body, name=name, grid=(r // tr,),
        in_specs=[pl.BlockSpec((n, tr, c), lambda i: (0, i, 0))],
        out_specs=pl.BlockSpec((tr, c), lambda i: (i, 0)),
        out_shape=jax.ShapeDtypeStruct((r, c), F32),
        compiler_params=_params("parallel"),
    )(slots)


def _adamw_math(w, g, m, v):
    m = ADAM_B1 * m + (1.0 - ADAM_B1) * g
    v = ADAM_B2 * v + (1.0 - ADAM_B2) * (g * g)
    m_hat = m / (1.0 - ADAM_B1 ** ADAM_STEP)
    v_hat = v / (1.0 - ADAM_B2 ** ADAM_STEP)
    return -ADAM_LR * (m_hat / (jnp.sqrt(v_hat) + ADAM_EPS) + ADAM_WD * w), m, v


def _adamw(w, slots, m, v, name):
    n, r, c = slots.shape
    tr = _row_tile(r, c)

    def body(w_ref, s_ref, m_ref, v_ref, g_ref, d_ref, nm_ref, nv_ref):
        g = s_ref[0].astype(F32)
        for s in range(1, n):
            g = g + s_ref[s].astype(F32)
        g_ref[...] = g
        d_ref[...], nm_ref[...], nv_ref[...] = _adamw_math(w_ref[...], g, m_ref[...], v_ref[...])

    blk = pl.BlockSpec((tr, c), lambda i: (i, 0))
    return pl.pallas_call(
        body, name=name, grid=(r // tr,),
        in_specs=[blk, pl.BlockSpec((n, tr, c), lambda i: (0, i, 0)), blk, blk], out_specs=[blk] * 4,
        out_shape=[jax.ShapeDtypeStruct((r, c), F32)] * 4,
        compiler_params=_params("parallel"),
    )(w, slots, m, v)


def _adamw_small(ws, gs, ms, vs):
    n = len(ws)

    def body(*refs):
        for i in range(n):
            w_ref, g_ref, m_ref, v_ref = (refs[j * n + i] for j in range(4))
            d_ref, nm_ref, nv_ref = (refs[(4 + j) * n + i] for j in range(3))
            d_ref[...], nm_ref[...], nv_ref[...] = _adamw_math(w_ref[...], g_ref[...], m_ref[...], v_ref[...])

    shapes = [jax.ShapeDtypeStruct(w.shape, F32) for w in ws]
    outs = pl.pallas_call(body, name="adamw_small", out_shape=shapes * 3)(*ws, *gs, *ms, *vs)
    return outs[:n], outs[n:2 * n], outs[2 * n:]


def kernel(x, norm_mix, w_in, b_gate, rel_bias, w_ret_out, w_att_out, w_out, norm_ffn, w_ffn_gate, w_ffn_up, w_ffn_down, norm_final, loss_target, m_norm_mix, m_w_in, m_b_gate, m_rel_bias, m_w_ret_out, m_w_att_out, m_w_out, m_norm_ffn, m_w_ffn_gate, m_w_ffn_up, m_w_ffn_down, m_norm_final, v_norm_mix, v_w_in, v_b_gate, v_rel_bias, v_w_ret_out, v_w_att_out, v_w_out, v_norm_ffn, v_w_ffn_gate, v_w_ffn_up, v_w_ffn_down, v_norm_final):
    me = _index(_place())
    n_rb = rel_bias.shape[-1]

    shard = dict(w_in_t=w_in[0].T, w_gate_t=w_ffn_gate[0].T, w_up_t=w_ffn_up[0].T, w_down=w_ffn_down[0],
                 w_ret=w_ret_out[0], w_out=w_out[0], w_att_t=w_att_out[0].T)
    shard = {nm: s.astype(BF16) for nm, s in shard.items()}
    dx, slots, small_slots = _step(x, loss_target, norm_mix, b_gate, norm_ffn, norm_final, rel_bias[0], shard)
    small_sum = _sum_slots(small_slots, "sum_small")
    loss = small_sum[5, 0]

    transposed = dict(w_in="w_in_t", w_ffn_gate="w_gate_t", w_ffn_up="w_up_t", w_att_out="w_att_t")
    plain = dict(w_ffn_down="w_down", w_ret_out="w_ret", w_out="w_out")
    g = dict(
        norm_mix=small_sum[0:1], b_gate=jnp.concatenate([small_sum[1:2], small_sum[2:3]], axis=1),
        norm_ffn=small_sum[3:4], norm_final=small_sum[4:5],
        rel_bias=lax.dynamic_slice_in_dim(_bias_rows_bwd(small_sum[8:16]), me * n_rb, n_rb, axis=1),
    )
    w = dict(norm_mix=norm_mix, w_in=w_in, b_gate=b_gate, rel_bias=rel_bias, w_ret_out=w_ret_out, w_att_out=w_att_out,
             w_out=w_out, norm_ffn=norm_ffn, w_ffn_gate=w_ffn_gate, w_ffn_up=w_ffn_up, w_ffn_down=w_ffn_down,
             norm_final=norm_final)
    m = dict(norm_mix=m_norm_mix, w_in=m_w_in, b_gate=m_b_gate, rel_bias=m_rel_bias, w_ret_out=m_w_ret_out,
             w_att_out=m_w_att_out, w_out=m_w_out, norm_ffn=m_norm_ffn, w_ffn_gate=m_w_ffn_gate, w_ffn_up=m_w_ffn_up,
             w_ffn_down=m_w_ffn_down, norm_final=m_norm_final)
    v = dict(norm_mix=v_norm_mix, w_in=v_w_in, b_gate=v_b_gate, rel_bias=v_rel_bias, w_ret_out=v_w_ret_out,
             w_att_out=v_w_att_out, w_out=v_w_out, norm_ffn=v_norm_ffn, w_ffn_gate=v_w_ffn_gate, w_ffn_up=v_w_ffn_up,
             w_ffn_down=v_w_ffn_down, norm_final=v_norm_final)
    order = ("norm_mix", "w_in", "b_gate", "rel_bias", "w_ret_out", "w_att_out", "w_out", "norm_ffn",
             "w_ffn_gate", "w_ffn_up", "w_ffn_down", "norm_final")
    small_names = ("norm_mix", "b_gate", "rel_bias", "norm_ffn", "norm_final")

    def flat(a):
        return a[0] if a.ndim == 3 else a.reshape(-1, a.shape[-1])

    grad, delta, new_m, new_v = {}, {}, {}, {}
    for nm in order:
        if nm in transposed:
            res = _adamw(w[nm][0].T, slots[transposed[nm]], m[nm][0].T, v[nm][0].T, "adamw_" + nm)
            grad[nm], delta[nm], new_m[nm], new_v[nm] = (a.T[None] for a in res)
        elif nm in plain:
            res = _adamw(flat(w[nm]), slots[plain[nm]], flat(m[nm]), flat(v[nm]), "adamw_" + nm)
            grad[nm], delta[nm], new_m[nm], new_v[nm] = (a.reshape(w[nm].shape) for a in res)
    ds, nms, nvs = _adamw_small([flat(w[nm]) for nm in small_names], [g[nm] for nm in small_names],
                                [flat(m[nm]) for nm in small_names], [flat(v[nm]) for nm in small_names])
    for i, nm in enumerate(small_names):
        grad[nm], delta[nm], new_m[nm], new_v[nm] = (a.reshape(w[nm].shape) for a in (g[nm], ds[i], nms[i], nvs[i]))

    return (loss, dx, *[grad[nm] for nm in order], *[delta[nm] for nm in order],
            *[new_m[nm] for nm in order], *[new_v[nm] for nm in order])
```

```python
import numpy as np
import jax
import jax.numpy as jnp
from jax import lax
from jax.experimental import pallas as pl
from jax.experimental.pallas import tpu as pltpu

F32 = jnp.float32
BF16 = jnp.bfloat16
MESH = pl.DeviceIdType.MESH

D_MODEL = 1024
CHUNK = 64
RET_HEADS = 4
RET_KEY_DIM = 128
RET_VAL_DIM = 256
ATT_HEADS = 8
BAND_CHUNKS = 8
MAX_REL = 256
N_REL = CHUNK + MAX_REL
D_FF = 2816
N_IN = 6656
ROPE_BASE = 10000.0
EPS = 1e-6
NEG_INF = -1e30
C_RQ, C_RK, C_RV, C_RG, C_AQ, C_AK, C_AV, C_GL = 0, 512, 1024, 2048, 3072, 3584, 4096, 4608

ADAM_LR = 0.001
ADAM_B1 = 0.9
ADAM_B2 = 0.999
ADAM_EPS = 1e-08
ADAM_WD = 0.01
ADAM_STEP = 10

N_DEV = 8
LANES = 128
RET_TILE = 256
ATT_Q = 256
ATT_PAD = BAND_CHUNKS * CHUNK
ATT_WIN = ATT_PAD + ATT_Q
ATT_STARTS = ATT_PAD // ATT_Q
ATT_ROWS = 32
BIAS_LEN = 1024
VMEM_LIMIT = 48 * 1024 * 1024
VMEM_LIMIT_ATT_BWD = 56 * 1024 * 1024


def _params(*sem, vmem=VMEM_LIMIT):
    return pltpu.CompilerParams(dimension_semantics=sem, vmem_limit_bytes=vmem)


def _dot(a, b):
    return lax.dot_general(a, b, (((1,), (0,)), ((), ())), preferred_element_type=F32)


def _dot_nt(a, b):
    return lax.dot_general(a, b, (((1,), (1,)), ((), ())), preferred_element_type=F32)


def _dot_tn(a, b):
    return lax.dot_general(a, b, (((0,), (0,)), ((), ())), preferred_element_type=F32)


def _sigmoid(x):
    return 1.0 / (1.0 + jnp.exp(-x))


def _rms_bwd(x, g, dy):
    r = lax.rsqrt(jnp.mean(x * x, axis=-1, keepdims=True) + EPS)
    u = dy * g
    dx = r * u - x * (r * r * r) * jnp.mean(u * x, axis=-1, keepdims=True)
    return dx, dy * x * r


def _place():
    return lax.axis_index("x"), lax.axis_index("y"), lax.axis_index("c")


def _peer(k):
    x, y, c = _place()
    return ((1 - x) if k & 4 else x, (1 - y) if k & 2 else y, (1 - c) if k & 1 else c)


def _index(place):
    return 4 * place[0] + 2 * place[1] + place[2]


def _rows(ref, block, nrows):
    align = 16 if ref.dtype == BF16 else 8
    return ref.at[pl.ds(pl.multiple_of(block * nrows, align), nrows)]


class _Exchange:
    def __init__(self, arrays, scatter):
        self.arrays, self.scatter, self.n = list(arrays), scatter, len(arrays)

    def out_shape(self):
        if self.scatter:
            return [jax.ShapeDtypeStruct((N_DEV, a.shape[0] // N_DEV) + a.shape[1:], a.dtype) for a in self.arrays]
        return [jax.ShapeDtypeStruct((N_DEV * a.shape[0],) + a.shape[1:], a.dtype) for a in self.arrays]

    def scratch(self):
        return [pltpu.SemaphoreType.DMA((self.n, N_DEV - 1)), pltpu.SemaphoreType.DMA((self.n, N_DEV - 1)),
                pltpu.SemaphoreType.DMA((self.n,))]

    def _copies(self, ins, outs, sems):
        send_sems, recv_sems, local_sems = sems
        me = _index(_place())

        def src(w, to):
            return _rows(ins[w], to, ins[w].shape[0] // N_DEV) if self.scatter else ins[w]

        def dst(w, origin):
            return outs[w].at[origin] if self.scatter else _rows(outs[w], origin, ins[w].shape[0])

        def remote(w, k, to, origin):
            return pltpu.make_async_remote_copy(src_ref=src(w, to), dst_ref=dst(w, origin),
                                                send_sem=send_sems.at[w, k - 1], recv_sem=recv_sems.at[w, k - 1],
                                                device_id=_peer(k), device_id_type=MESH)

        pairs = [(w, k) for w in range(self.n) for k in range(1, N_DEV)]
        own = lambda: [pltpu.make_async_copy(src(w, me), dst(w, me), local_sems.at[w]) for w in range(self.n)]
        sent = lambda: [remote(w, k, _index(_peer(k)), me) for w, k in pairs]
        arriving = lambda: [remote(w, k, me, _index(_peer(k))) for w, k in pairs]
        return own, sent, arriving

    def start(self, ins, outs, sems):
        own, sent, _ = self._copies(ins, outs, sems)
        for cp in own() + sent():
            cp.start()

    def wait(self, ins, outs, sems):
        own, sent, arriving = self._copies(ins, outs, sems)
        for cp in arriving():
            cp.wait_recv()
        for cp in sent():
            cp.wait_send()
        for cp in own():
            cp.wait()


class _PairSwap:
    def __init__(self, arrays):
        self.arrays, self.n = list(arrays), len(arrays)

    def out_shape(self):
        return [jax.ShapeDtypeStruct((4, a.shape[0] // N_DEV) + a.shape[1:], a.dtype) for a in self.arrays]

    def scratch(self):
        return [pltpu.SemaphoreType.DMA((self.n, 4)), pltpu.SemaphoreType.DMA((self.n, 4))]

    def _copies(self, ins, outs, sems):
        send_sems, recv_sems = sems
        x, y, c = _place()
        return [pltpu.make_async_remote_copy(
            src_ref=_rows(ins[w], 2 * q + 1 - c, ins[w].shape[0] // N_DEV), dst_ref=outs[w].at[q],
            send_sem=send_sems.at[w, q], recv_sem=recv_sems.at[w, q],
            device_id=(x, y, 1 - c), device_id_type=MESH) for w in range(self.n) for q in range(4)]

    def start(self, ins, outs, sems):
        for cp in self._copies(ins, outs, sems):
            cp.start()

    def wait(self, ins, outs, sems):
        for cp in self._copies(ins, outs, sems):
            cp.wait()


class _ChipScatter:
    def __init__(self, arrays):
        self.arrays, self.n = list(arrays), len(arrays)

    def out_shape(self):
        return [jax.ShapeDtypeStruct(a.shape, a.dtype) for a in self.arrays]

    def scratch(self):
        return [pltpu.SemaphoreType.DMA((self.n, 3)), pltpu.SemaphoreType.DMA((self.n, 3)),
                pltpu.SemaphoreType.DMA((self.n,))]

    def _copies(self, ins, outs, sems):
        send_sems, recv_sems, local_sems = sems
        x, y, c = _place()
        mine = 2 * x + y
        sent, arriving = [], []
        for w in range(self.n):
            for k in range(1, 4):
                tx, ty = (1 - x) if k & 2 else x, (1 - y) if k & 1 else y
                other = 2 * tx + ty
                sent.append(lambda w=w, k=k, tx=tx, ty=ty, other=other: pltpu.make_async_remote_copy(
                    src_ref=ins[w].at[other], dst_ref=outs[w].at[mine], send_sem=send_sems.at[w, k - 1],
                    recv_sem=recv_sems.at[w, k - 1], device_id=(tx, ty, c), device_id_type=MESH))
                arriving.append(lambda w=w, k=k, tx=tx, ty=ty, other=other: pltpu.make_async_remote_copy(
                    src_ref=ins[w].at[mine], dst_ref=outs[w].at[other], send_sem=send_sems.at[w, k - 1],
                    recv_sem=recv_sems.at[w, k - 1], device_id=(tx, ty, c), device_id_type=MESH))
        own = [lambda w=w: pltpu.make_async_copy(ins[w].at[mine], outs[w].at[mine], local_sems.at[w])
               for w in range(self.n)]
        return own, sent, arriving

    def start(self, ins, outs, sems):
        own, sent, _ = self._copies(ins, outs, sems)
        for cp in own + sent:
            cp().start()

    def wait(self, ins, outs, sems):
        own, sent, arriving = self._copies(ins, outs, sems)
        for cp in arriving:
            cp().wait_recv()
        for cp in sent:
            cp().wait_send()
        for cp in own:
            cp().wait()


class _ChipGather:
    def __init__(self, arrays, parts=None, into=None):
        self.arrays, self.n, self.into = list(arrays), len(arrays), into
        self.parts = parts or [(0, 1)] * self.n

    def out_shape(self):
        return [jax.ShapeDtypeStruct((N_DEV * a.shape[0],) + a.shape[1:], a.dtype) for a in self.arrays]

    def scratch(self):
        return [pltpu.SemaphoreType.DMA((self.n, N_DEV - 1)), pltpu.SemaphoreType.DMA((self.n, N_DEV - 1)),
                pltpu.SemaphoreType.DMA((self.n,))]

    def _parts(self, ins, outs, sems):
        send_sems, recv_sems, local_sems = sems
        x, y, c = _place()
        me, sibling = (x, y, c), (x, y, 1 - c)
        chips = [(1 - x, y), (x, 1 - y), (1 - x, 1 - y)]

        def rows(w, place, whole):
            (index, count), r = self.parts[w], ins[w].shape[0]
            lo, size = (0, r) if whole else (index * (r // count), r // count)
            align = 16 if ins[w].dtype == BF16 else 8
            return outs[w].at[pl.ds(pl.multiple_of(_index(place) * r + lo, align), size)]

        def mine(w, whole):
            (index, count), r = self.parts[w], ins[w].shape[0]
            return ins[w] if whole or count == 1 else ins[w].at[pl.ds(index * (r // count), r // count)]

        def copy(w, k, block, to, own=False):
            whole = k == 0
            return pltpu.make_async_remote_copy(src_ref=mine(w, whole) if own else rows(w, block, whole),
                                                dst_ref=rows(w, block, whole),
                                                send_sem=send_sems.at[w, k], recv_sem=recv_sems.at[w, k],
                                                device_id=to, device_id_type=MESH)

        def local(w):
            return pltpu.make_async_copy(ins[w], rows(w, me, True), local_sems.at[w])

        return me, sibling, chips, c, copy, local, [index == 0 for index, _ in self.parts]

    def start(self, ins, outs, sems):
        me, sibling, chips, c, copy, local, places_own = self._parts(ins, outs, sems)
        for w in range(self.n):
            if places_own[w]:
                local(w).start()
                copy(w, 0, me, sibling, own=True).start()
            for j, chip in enumerate(chips):
                copy(w, 1 + j, me, (*chip, c), own=True).start()

    def wait(self, ins, outs, sems):
        me, sibling, chips, c, copy, local, places_own = self._parts(ins, outs, sems)
        for w in range(self.n):
            for j, chip in enumerate(chips):
                copy(w, 1 + j, (*chip, c), me).wait_recv()
                copy(w, 4 + j, (*chip, c), sibling).start()
        for w in range(self.n):
            if places_own[w]:
                copy(w, 0, sibling, me).wait_recv()
                copy(w, 0, me, sibling, own=True).wait_send()
                local(w).wait()
            for j, chip in enumerate(chips):
                copy(w, 4 + j, (*chip, 1 - c), me).wait_recv()
                copy(w, 1 + j, me, (*chip, c), own=True).wait_send()
                copy(w, 4 + j, (*chip, c), sibling).wait_send()


def _call(body, *, name, grid, in_specs, out_specs, out_shape, scratch=(), semantics, args, exchange=None,
          vmem=VMEM_LIMIT):
    if exchange is None:
        return pl.pallas_call(body, name=name, grid=grid, in_specs=in_specs, out_specs=out_specs, out_shape=out_shape,
                              scratch_shapes=list(scratch),
                              compiler_params=_params(*semantics, vmem=vmem))(*args), None
    n_in, n_out, n_scr, nx = len(in_specs), len(out_specs), len(scratch), exchange.n
    into = list(getattr(exchange, "into", None) or [])

    def full_body(*refs):
        ins, refs = refs[:n_in], refs[n_in:]
        x_in, refs = refs[:nx], refs[nx + len(into):]
        outs, refs = refs[:n_out], refs[n_out:]
        x_out, refs = refs[:nx], refs[nx:]
        scr, sems = refs[:n_scr], refs[n_scr:]
        first, last = True, True
        for axis, size in enumerate(grid):
            first = jnp.logical_and(first, pl.program_id(axis) == 0)
            last = jnp.logical_and(last, pl.program_id(axis) == size - 1)
        if grid:
            pl.when(first)(lambda: exchange.start(x_in, x_out, sems))
        else:
            exchange.start(x_in, x_out, sems)
        body(*ins, *outs, *scr)
        if grid:
            pl.when(last)(lambda: exchange.wait(x_in, x_out, sems))
        else:
            exchange.wait(x_in, x_out, sems)

    hbm = pl.BlockSpec(memory_space=pltpu.HBM)
    res = pl.pallas_call(
        full_body, name=name, grid=grid,
        in_specs=list(in_specs) + [hbm] * (nx + len(into)), out_specs=list(out_specs) + [hbm] * nx,
        out_shape=list(out_shape) + exchange.out_shape(),
        scratch_shapes=list(scratch) + exchange.scratch(),
        input_output_aliases={n_in + nx + w: n_out + w for w in range(len(into))},
        compiler_params=_params(*(["arbitrary"] * len(grid)), vmem=vmem),
    )(*args, *exchange.arrays, *into)
    return res[:n_out], res[n_out:]


def _alone(exchange, name):
    return _call(lambda: None, name=name, grid=(), in_specs=[], out_specs=[], out_shape=[], semantics=(),
                 args=(), exchange=exchange)[1]


def _mm(a, b, *, ta=False, tb=False, out_dtype, tm, tn, tk, name, exchange=None):
    m, k = (a.shape[1], a.shape[0]) if ta else a.shape
    n = b.shape[0] if tb else b.shape[1]
    assert k == (b.shape[1] if tb else b.shape[0])
    tm, tn, tk = min(tm, m), min(tn, n), min(tk, k)
    assert m % tm == 0 and n % tn == 0 and k % tk == 0, (name, m, n, k)
    nk = k // tk
    dims = (((0 if ta else 1,), (1 if tb else 0,)), ((), ()))

    def body(a_ref, b_ref, o_ref, *acc):
        prod = lax.dot_general(a_ref[...].astype(BF16), b_ref[...].astype(BF16), dims, preferred_element_type=F32)
        if nk == 1:
            o_ref[...] = prod.astype(o_ref.dtype)
            return
        acc_ref, kk = acc[0], pl.program_id(2)

        @pl.when(kk == 0)
        def _():
            acc_ref[...] = prod

        @pl.when((kk > 0) & (kk < nk - 1))
        def _():
            acc_ref[...] += prod

        @pl.when(kk == nk - 1)
        def _():
            o_ref[...] = (acc_ref[...] + prod).astype(o_ref.dtype)

    a_spec = (pl.BlockSpec((tk, tm), lambda i, j, kk: (kk, i)) if ta
              else pl.BlockSpec((tm, tk), lambda i, j, kk: (i, kk)))
    b_spec = (pl.BlockSpec((tn, tk), lambda i, j, kk: (j, kk)) if tb
              else pl.BlockSpec((tk, tn), lambda i, j, kk: (kk, j)))
    (out,), moved = _call(
        body, name=name, grid=(m // tm, n // tn, nk),
        in_specs=[a_spec, b_spec],
        out_specs=[pl.BlockSpec((tm, tn), lambda i, j, kk: (i, j))],
        out_shape=[pltpu.HBM((m, n), out_dtype)],
        scratch=[pltpu.VMEM((tm, tn), F32)] if nk > 1 else [],
        semantics=("parallel", "parallel", "arbitrary"), args=(a, b), exchange=exchange)
    return out if exchange is None else (out, moved)


def _mm_pieces(pieces, b, *, ta, out_dtype, tm, tn, tk, name, exchange=None):
    rows, n = pieces[0].shape[0], b.shape[1]
    step = tm if ta else tk
    assert all(p.shape[0] == rows and p.shape[1] % step == 0 for p in pieces), name
    edges = [int(e) for e in np.cumsum([0] + [p.shape[1] // step for p in pieces])]
    total = edges[-1] * step
    m, k = (total, rows) if ta else (rows, total)
    assert b.shape[0] == k and m % tm == 0 and n % tn == 0 and k % tk == 0, name
    nk, npieces = k // tk, len(pieces)
    dims = (((0 if ta else 1,), (0,)), ((), ()))
    b_resident = ta and n == tn

    def body(*refs):
        a_refs, (b_ref, o_ref, acc_ref) = refs[:npieces], refs[npieces:]
        kk = pl.program_id(2)
        pos = pl.program_id(0) if ta else kk

        @pl.when(kk == 0)
        def _():
            acc_ref[...] = jnp.zeros_like(acc_ref)

        def b_tile():
            return b_ref[pl.ds(pl.multiple_of(kk * tk, tk), tk), :] if b_resident else b_ref[...]

        for p, a_ref in enumerate(a_refs):
            @pl.when((pos >= edges[p]) & (pos < edges[p + 1]))
            def _(a_ref=a_ref):
                acc_ref[...] += lax.dot_general(a_ref[...], b_tile(), dims, preferred_element_type=F32)

        @pl.when(kk == nk - 1)
        def _():
            o_ref[...] = acc_ref[...].astype(o_ref.dtype)

    def a_spec(p):
        lo, last = edges[p], edges[p + 1] - edges[p] - 1
        if ta:
            def index(i, j, kk):
                inside = (i >= lo) & (i <= lo + last)
                return jnp.where(inside, kk, 0), jnp.clip(i - lo, 0, last)
            return pl.BlockSpec((tk, tm), index)
        return pl.BlockSpec((tm, tk), lambda i, j, kk: (i, jnp.clip(kk - lo, 0, last)))

    (out,), moved = _call(
        body, name=name, grid=(m // tm, n // tn, nk),
        in_specs=[a_spec(p) for p in range(npieces)]
                 + [pl.BlockSpec(b.shape, lambda i, j, kk: (0, 0)) if b_resident
                    else pl.BlockSpec((tk, tn), lambda i, j, kk: (kk, j))],
        out_specs=[pl.BlockSpec((tm, tn), lambda i, j, kk: (i, j))],
        out_shape=[pltpu.HBM((m, n), out_dtype)],
        scratch=[pltpu.VMEM((tm, tn), F32)],
        semantics=("parallel", "parallel", "arbitrary"), args=(*pieces, b), exchange=exchange)
    return out if exchange is None else (out, moved)


def _rms_fwd(x2, g, exchange):
    t = x2.shape[0]
    tm = min(512, t)

    def body(x_ref, g_ref, o_ref):
        x = x_ref[...]
        r = lax.rsqrt(jnp.mean(x * x, axis=-1, keepdims=True) + EPS)
        o_ref[...] = (x * r * g_ref[...]).astype(o_ref.dtype)

    return _call(
        body, name="rms_in_fwd", grid=(t // tm,),
        in_specs=[pl.BlockSpec((tm, D_MODEL), lambda i: (i, 0)), pl.BlockSpec((1, D_MODEL), lambda i: (0, 0))],
        out_specs=[pl.BlockSpec((tm, D_MODEL), lambda i: (i, 0))],
        out_shape=[jax.ShapeDtypeStruct((t, D_MODEL), BF16)],
        semantics=("parallel",), args=(x2, g), exchange=exchange)


def _decay(lg):
    row = lax.broadcasted_iota(jnp.int32, (RET_TILE, RET_TILE), 0)
    col = lax.broadcasted_iota(jnp.int32, (RET_TILE, RET_TILE), 1)
    within = jnp.exp(lg * jnp.abs(row - col).astype(F32))
    inside = jnp.where((col >> 6) <= (row >> 6), within, 0.0)
    pos = lax.broadcasted_iota(jnp.int32, (RET_TILE, 1), 0).astype(F32)
    q_dec = jnp.exp(lg * (pos + 1.0))
    k_dec = jnp.exp(lg * (RET_TILE - 1.0 - pos))
    tile_dec = jnp.exp(lg * float(RET_TILE))
    return inside, q_dec, k_dec, tile_dec


def _scaled(a_bf16, dec):
    return (a_bf16.astype(F32) * dec).astype(BF16)


def _ret_fwd(proj, cs, sn, lg_arr, batch, seq, exchange):
    t = batch * seq
    nt = seq // RET_TILE

    def body(q_ref, k_ref, v_ref, rg_ref, cs_ref, sn_ref, lg_ref, gro_ref, o_ref, qr_ref, kr_ref):
        lg = lg_ref[:, 0:1]
        cs_t, sn_t = cs_ref[...], sn_ref[...]
        q = q_ref[...].astype(F32)
        k = k_ref[...].astype(F32)
        qr_ref[...] = (q * cs_t + pltpu.roll(q, 64, 1) * sn_t).astype(BF16)
        kr_ref[...] = ((k * cs_t + pltpu.roll(k, 64, 1) * sn_t) * (RET_KEY_DIM ** -0.5)).astype(BF16)
        inside, q_dec, k_dec, tile_dec = _decay(lg)
        state = jnp.zeros((RET_KEY_DIM, RET_VAL_DIM), F32)
        for i in range(nt):
            rows = slice(i * RET_TILE, (i + 1) * RET_TILE)
            qi, ki, vi = qr_ref[rows, :], kr_ref[rows, :], v_ref[rows, :]
            acc = _dot((_dot_nt(qi, ki) * inside).astype(BF16), vi)
            if i > 0:
                acc = acc + _dot(_scaled(qi, q_dec), state.astype(BF16))
            if i < nt - 1:
                state = state * tile_dec + _dot_tn(_scaled(ki, k_dec), vi)
            o_ref[rows, :] = acc
            xc = acc - jnp.mean(acc, axis=-1, keepdims=True)
            nrm = xc * lax.rsqrt(jnp.mean(xc * xc, axis=-1, keepdims=True) + EPS)
            rg = rg_ref[rows, :].astype(F32)
            gro_ref[rows, :] = (rg * _sigmoid(rg) * nrm).astype(BF16)

    def col(base, width):
        return lambda b, h: (b, base // width + h)

    return _call(
        body, name="ret_fwd", grid=(batch, RET_HEADS),
        in_specs=[pl.BlockSpec((seq, RET_KEY_DIM), col(C_RQ, RET_KEY_DIM)),
                  pl.BlockSpec((seq, RET_KEY_DIM), col(C_RK, RET_KEY_DIM)),
                  pl.BlockSpec((seq, RET_VAL_DIM), col(C_RV, RET_VAL_DIM)),
                  pl.BlockSpec((seq, RET_VAL_DIM), col(C_RG, RET_VAL_DIM)),
                  pl.BlockSpec((seq, RET_KEY_DIM), lambda b, h: (0, 0)),
                  pl.BlockSpec((seq, RET_KEY_DIM), lambda b, h: (0, 0)),
                  pl.BlockSpec((None, 1, LANES), lambda b, h: (h, 0, 0))],
        out_specs=[pl.BlockSpec((seq, RET_VAL_DIM), lambda b, h: (b, h)),
                   pl.BlockSpec((seq, RET_VAL_DIM), lambda b, h: (b, h)),
                   pl.BlockSpec((seq, RET_KEY_DIM), lambda b, h: (b, h)),
                   pl.BlockSpec((seq, RET_KEY_DIM), lambda b, h: (b, h))],
        out_shape=[jax.ShapeDtypeStruct((t, RET_HEADS * RET_VAL_DIM), BF16),
                   jax.ShapeDtypeStruct((t, RET_HEADS * RET_VAL_DIM), F32),
                   jax.ShapeDtypeStruct((t, RET_HEADS * RET_KEY_DIM), BF16),
                   jax.ShapeDtypeStruct((t, RET_HEADS * RET_KEY_DIM), BF16)],
        semantics=("parallel", "parallel"), args=(proj, proj, proj, proj, cs, sn, lg_arr), exchange=exchange)


def _att_bias(w_ref, bias_ref):
    n_i = lax.broadcasted_iota(jnp.int32, (ATT_Q, BIAS_LEN), 0)
    qc = lax.broadcasted_iota(jnp.int32, (ATT_Q, ATT_WIN), 0) >> 6
    kc = lax.broadcasted_iota(jnp.int32, (ATT_Q, ATT_WIN), 1) >> 6
    dc = qc + BAND_CHUNKS - kc
    band = (dc >= 0) & (dc <= BAND_CHUNKS)
    key = lax.broadcasted_iota(jnp.int32, (ATT_Q, ATT_WIN), 1)
    for e in range(2):
        xw = jnp.broadcast_to(w_ref[e:e + 1, :], (ATT_Q, BIAS_LEN))
        for bit in range(8):
            xw = jnp.where(((n_i >> bit) & 1) == 1, pltpu.roll(xw, 1 << bit, 1), xw)
        bias = jnp.where(band, xw[:, BIAS_LEN - ATT_WIN:], NEG_INF)
        for first in range(ATT_STARTS):
            bias_ref[first, e] = jnp.where(key + (first * ATT_Q - ATT_PAD) >= 0, bias, NEG_INF)
        bias_ref[ATT_STARTS, e] = bias


ATT_PAIRS = 2
ATT_COLS = ATT_PAIRS * LANES


def _att_specs(batch, seq):
    ni = seq // ATT_Q
    q_spec = pl.BlockSpec((ATT_Q, ATT_COLS), lambda g, b, i: (b * ni + i, C_AQ // ATT_COLS + g))
    k_spec = pl.BlockSpec((seq, ATT_COLS), lambda g, b, i: (b, C_AK // ATT_COLS + g))
    v_spec = pl.BlockSpec((seq, ATT_COLS), lambda g, b, i: (b, C_AV // ATT_COLS + g))
    w_spec = pl.BlockSpec((ATT_PAIRS, 2, BIAS_LEN), lambda g, b, i: (g, 0, 0))
    b_spec = pl.BlockSpec((ATT_PAIRS, ATT_STARTS + 1, 2, ATT_Q, ATT_WIN), lambda g, b, i: (g, 0, 0, 0, 0))
    pad = pltpu.VMEM((seq + ATT_PAD, ATT_COLS), BF16)
    return ni, q_spec, k_spec, v_spec, w_spec, b_spec, pad


def _att_bias_tiles(wvec, exchange):
    (tiles,), moved = _call(
        lambda w_ref, o_ref: _att_bias(w_ref, o_ref), name="att_bias", grid=(ATT_HEADS // 2,),
        in_specs=[pl.BlockSpec((None, 2, BIAS_LEN), lambda hp: (hp, 0, 0))],
        out_specs=[pl.BlockSpec((None, ATT_STARTS + 1, 2, ATT_Q, ATT_WIN), lambda hp: (hp, 0, 0, 0, 0))],
        out_shape=[jax.ShapeDtypeStruct((ATT_HEADS // 2, ATT_STARTS + 1, 2, ATT_Q, ATT_WIN), F32)],
        semantics=("parallel",), args=(wvec,), exchange=exchange)
    return tiles, moved


def _att_pad(src_ref, pad_ref):
    pad_ref[:ATT_PAD, :] = jnp.zeros((ATT_PAD, ATT_COLS), BF16)
    pad_ref[ATT_PAD:, :] = src_ref[...]


def _att_head(q2, sel):
    return jnp.where(sel, q2, jnp.zeros_like(q2)) * 0.125


def _att_softmax_rows(s_ref, bias_ref, rows):
    s = s_ref[rows, :] + bias_ref[rows, :]
    ex = jnp.exp(s - jnp.max(s, axis=-1, keepdims=True))
    return ex, 1.0 / jnp.sum(ex, axis=-1, keepdims=True)


def _att_fwd(proj, bias, batch, seq, exchange):
    ni, q_spec, k_spec, v_spec, _, b_spec, pad = _att_specs(batch, seq)

    def body(q_ref, k_ref, v_ref, bias_ref, o_ref, kp_ref, vp_ref, s_ref, e_ref):
        i = pl.program_id(2)

        @pl.when(i == 0)
        def _():
            _att_pad(k_ref, kp_ref)
            _att_pad(v_ref, vp_ref)

        win = pl.ds(pl.multiple_of(i * ATT_Q, ATT_Q), ATT_WIN)
        lo = lax.broadcasted_iota(jnp.int32, (1, LANES), 1) < 64
        start = jnp.minimum(i, ATT_STARTS)
        for pair in range(ATT_PAIRS):
            cols = slice(pair * LANES, (pair + 1) * LANES)
            k2, v2, q2 = kp_ref[win, cols], vp_ref[win, cols], q_ref[:, cols]
            out = jnp.zeros((ATT_Q, LANES), F32)
            for e in range(2):
                h = 2 * pair + e
                sel = lo if e == 0 else jnp.logical_not(lo)
                s_ref[h] = _dot_nt(_att_head(q2, sel), k2)
                rsum = []
                for c in range(ATT_Q // ATT_ROWS):
                    rows = slice(c * ATT_ROWS, (c + 1) * ATT_ROWS)
                    ex, r = _att_softmax_rows(s_ref.at[h], bias_ref.at[pair, start, e], rows)
                    e_ref[h, rows, :] = ex.astype(BF16)
                    rsum.append(r)
                out = out + _dot(e_ref[h], jnp.where(sel, v2, jnp.zeros_like(v2))) * jnp.concatenate(rsum, axis=0)
            o_ref[:, cols] = out.astype(BF16)

    heads = 2 * ATT_PAIRS
    return _call(
        body, name="att_fwd", grid=(ATT_HEADS // heads, batch, ni),
        in_specs=[q_spec, k_spec, v_spec, b_spec],
        out_specs=[pl.BlockSpec((ATT_Q, ATT_COLS), lambda g, b, i: (b * ni + i, g))],
        out_shape=[jax.ShapeDtypeStruct((batch * seq, ATT_HEADS * 64), BF16)],
        scratch=[pad, pad, pltpu.VMEM((heads, ATT_Q, ATT_WIN), F32), pltpu.VMEM((heads, ATT_Q, ATT_WIN), BF16)],
        semantics=("arbitrary", "arbitrary", "arbitrary"), args=(proj, proj, proj, bias), exchange=exchange)


GL_HALF = 512


def _gl_specs(tm):
    return [pl.BlockSpec((tm, GL_HALF), lambda i, c=C_GL // GL_HALF + j: (i, c)) for j in range(4)]


def _gates(gl_refs, b_ref):
    logits = [ref[...].astype(F32) for ref in gl_refs]
    gr = _sigmoid(jnp.concatenate(logits[:2], axis=1) + b_ref[:, :D_MODEL])
    ga = _sigmoid(jnp.concatenate(logits[2:], axis=1) + b_ref[:, D_MODEL:])
    return gr, ga


def _whole(a):
    return pl.BlockSpec(a.shape, lambda i: (0,) * a.ndim)


def _mix_out_fwd(gro, ao, proj, b_gate, w_ret, w_att_t, x2, w_out, g2):
    t = gro.shape[0]
    tm = min(256, t)

    def body(gro_ref, ao_ref, gl0, gl1, gl2, gl3, b_ref, wr_ref, wa_ref, x_ref, wo_ref, g_ref,
             z_ref, yr_ref, ya_ref, h_ref, hn_ref):
        yr = _dot(gro_ref[...], wr_ref[...])
        ya = _dot_nt(ao_ref[...], wa_ref[...])
        yr_ref[...] = yr.astype(BF16)
        ya_ref[...] = ya.astype(BF16)
        gr, ga = _gates((gl0, gl1, gl2, gl3), b_ref)
        z = (gr * yr + ga * ya).astype(BF16)
        z_ref[...] = z
        h = x_ref[...] + _dot(z, wo_ref[...])
        h_ref[...] = h
        r = lax.rsqrt(jnp.mean(h * h, axis=-1, keepdims=True) + EPS)
        hn_ref[...] = (h * r * g_ref[...]).astype(BF16)

    row = pl.BlockSpec((tm, D_MODEL), lambda i: (i, 0))
    return pl.pallas_call(
        body, name="mix_out_fwd", grid=(t // tm,),
        in_specs=[row, pl.BlockSpec((tm, 512), lambda i: (i, 0)), *_gl_specs(tm),
                  _whole(b_gate), _whole(w_ret), _whole(w_att_t), row, _whole(w_out), _whole(g2)],
        out_specs=[row] * 5,
        out_shape=[jax.ShapeDtypeStruct((t, D_MODEL), BF16)] * 3
                  + [jax.ShapeDtypeStruct((t, D_MODEL), F32), jax.ShapeDtypeStruct((t, D_MODEL), BF16)],
        compiler_params=_params("parallel"),
    )(gro, ao, proj, proj, proj, proj, b_gate, w_ret, w_att_t, x2, w_out, g2)


def _col_chunks(width, chunk=384):
    return [slice(lo, min(lo + chunk, width)) for lo in range(0, width, chunk)]


def _ffn_up(hn, wg_t, wu_t):
    t = hn.shape[0]
    tm, tn = min(512, t), D_FF // 2

    def body(h_ref, wg_ref, wu_ref, g_ref, u_ref, a_ref):
        g = _dot_nt(h_ref[...], wg_ref[...])
        u = _dot_nt(h_ref[...], wu_ref[...])
        g_ref[...] = g.astype(BF16)
        u_ref[...] = u.astype(BF16)
        a_ref[...] = (g * _sigmoid(g) * u).astype(BF16)

    w_spec = pl.BlockSpec((tn, D_MODEL), lambda j, i: (j, 0))
    out = pl.BlockSpec((tm, tn), lambda j, i: (i, j))
    return pl.pallas_call(
        body, name="ffn_up", grid=(D_FF // tn, t // tm),
        in_specs=[pl.BlockSpec((tm, D_MODEL), lambda j, i: (i, 0)), w_spec, w_spec],
        out_specs=[out, out, out],
        out_shape=[jax.ShapeDtypeStruct((t, D_FF), BF16)] * 3,
        compiler_params=_params("parallel", "parallel"),
    )(hn, wg_t, wu_t)


def _ffn_down_loss(a, h1, tgt, w_down, g3):
    t = a.shape[0]
    tm = min(512, t)

    def body(a_ref, h_ref, t_ref, w_ref, g_ref, dh_ref, dhb_ref, loss_ref, dg_ref):
        @pl.when(pl.program_id(0) == 0)
        def _():
            loss_ref[...] = jnp.zeros_like(loss_ref)
            dg_ref[...] = jnp.zeros_like(dg_ref)

        g = g_ref[...]
        h2 = h_ref[...] + _dot(a_ref[...], w_ref[...])
        r = lax.rsqrt(jnp.mean(h2 * h2, axis=-1, keepdims=True) + EPS)
        err = h2 * r * g - t_ref[...]
        loss_ref[...] += jnp.sum(err * err) * (0.5 / D_MODEL)
        dy = err * (1.0 / D_MODEL)
        dh, dg_rows = _rms_bwd(h2, g, dy)
        dg_ref[...] += jnp.sum(dg_rows, axis=0, keepdims=True)
        dh_ref[...] = dh
        dhb_ref[...] = dh.astype(BF16)

    row = pl.BlockSpec((tm, D_MODEL), lambda i: (i, 0))
    vec = pl.BlockSpec((1, D_MODEL), lambda i: (0, 0))
    return pl.pallas_call(
        body, name="ffn_down_loss", grid=(t // tm,),
        in_specs=[pl.BlockSpec((tm, D_FF), lambda i: (i, 0)), row, row,
                  pl.BlockSpec((D_FF, D_MODEL), lambda i: (0, 0)), vec],
        out_specs=[row, row, pl.BlockSpec((1, LANES), lambda i: (0, 0)), vec],
        out_shape=[jax.ShapeDtypeStruct((t, D_MODEL), F32), jax.ShapeDtypeStruct((t, D_MODEL), BF16),
                   jax.ShapeDtypeStruct((1, LANES), F32), jax.ShapeDtypeStruct((1, D_MODEL), F32)],
        compiler_params=_params("arbitrary"),
    )(a, h1, tgt, w_down, g3)


def _ffn_bwd_act(dh2b, w_down, g_act, u_act, exchange):
    t = dh2b.shape[0]
    tm, tn = min(512, t), D_FF // 2

    def body(d_ref, w_ref, g_ref, u_ref, dg_ref, du_ref):
        d = d_ref[...]
        for cols in _col_chunks(tn):
            da = _dot_nt(d, w_ref[cols, :])
            g = g_ref[:, cols].astype(F32)
            u = u_ref[:, cols].astype(F32)
            sg = _sigmoid(g)
            dg_ref[:, cols] = (da * u * sg * (1.0 + g * (1.0 - sg))).astype(BF16)
            du_ref[:, cols] = (da * g * sg).astype(BF16)

    blk = pl.BlockSpec((tm, tn), lambda j, i: (i, j))
    return _call(
        body, name="ffn_bwd_act", grid=(D_FF // tn, t // tm),
        in_specs=[pl.BlockSpec((tm, D_MODEL), lambda j, i: (i, 0)),
                  pl.BlockSpec((tn, D_MODEL), lambda j, i: (j, 0)), blk, blk],
        out_specs=[blk, blk],
        out_shape=[jax.ShapeDtypeStruct((t, D_FF), BF16)] * 2,
        semantics=("parallel", "parallel"), args=(dh2b, w_down, g_act, u_act), exchange=exchange)


def _ffn_bwd_in(dg, du, wg_t, wu_t, h1, dh2, g2, exchange):
    t = dg.shape[0]
    tm, tk = min(512, t), D_FF // 2
    nk = D_FF // tk

    def body(dg_ref, du_ref, wg_ref, wu_ref, h_ref, d2_ref, g_ref, dh_ref, dhb_ref, gn_ref, acc_ref):
        i, kk = pl.program_id(0), pl.program_id(1)

        @pl.when((i == 0) & (kk == 0))
        def _():
            gn_ref[...] = jnp.zeros_like(gn_ref)

        part = _dot(dg_ref[...], wg_ref[...]) + _dot(du_ref[...], wu_ref[...])

        @pl.when(kk == 0)
        def _():
            acc_ref[...] = part

        @pl.when((kk > 0) & (kk < nk - 1))
        def _():
            acc_ref[...] += part

        @pl.when(kk == nk - 1)
        def _():
            dx, dg_rows = _rms_bwd(h_ref[...], g_ref[...], acc_ref[...] + part)
            dh = d2_ref[...] + dx
            dh_ref[...] = dh
            dhb_ref[...] = dh.astype(BF16)
            gn_ref[...] += jnp.sum(dg_rows, axis=0, keepdims=True)

    act = pl.BlockSpec((tm, tk), lambda i, kk: (i, kk))
    wsp = pl.BlockSpec((tk, D_MODEL), lambda i, kk: (kk, 0))
    row = pl.BlockSpec((tm, D_MODEL), lambda i, kk: (i, 0))
    vec = pl.BlockSpec((1, D_MODEL), lambda i, kk: (0, 0))
    return _call(
        body, name="ffn_bwd_in", grid=(t // tm, nk),
        in_specs=[act, act, wsp, wsp, row, row, vec],
        out_specs=[row, row, vec],
        out_shape=[jax.ShapeDtypeStruct((t, D_MODEL), F32), jax.ShapeDtypeStruct((t, D_MODEL), BF16),
                   jax.ShapeDtypeStruct((1, D_MODEL), F32)],
        scratch=[pltpu.VMEM((tm, D_MODEL), F32)],
        semantics=("arbitrary", "arbitrary"), args=(dg, du, wg_t, wu_t, h1, dh2, g2), exchange=exchange)


def _mix_bwd(dh1b, w_out, proj, b_gate, y_ret, y_att, w_ret, w_att_t):
    t = dh1b.shape[0]
    tm = min(256, t)

    def body(d_ref, wo_ref, gl0, gl1, gl2, gl3, b_ref, yr_ref, ya_ref, wr_ref, wa_ref,
             dyr_ref, dya_ref, dglr_ref, dgla_ref, dgro_ref, dao_ref, db_ref):
        @pl.when(pl.program_id(0) == 0)
        def _():
            db_ref[...] = jnp.zeros_like(db_ref)

        dz = _dot_nt(d_ref[...], wo_ref[...])
        gr, ga = _gates((gl0, gl1, gl2, gl3), b_ref)
        dyr = (dz * gr).astype(BF16)
        dya = (dz * ga).astype(BF16)
        dyr_ref[...] = dyr
        dya_ref[...] = dya
        dglr = dz * yr_ref[...].astype(F32) * gr * (1.0 - gr)
        dgla = dz * ya_ref[...].astype(F32) * ga * (1.0 - ga)
        dglr_ref[...] = dglr.astype(BF16)
        dgla_ref[...] = dgla.astype(BF16)
        db_ref[:, :D_MODEL] += jnp.sum(dglr, axis=0, keepdims=True)
        db_ref[:, D_MODEL:] += jnp.sum(dgla, axis=0, keepdims=True)
        dgro_ref[...] = _dot_nt(dyr, wr_ref[...]).astype(BF16)
        dao_ref[...] = _dot(dya, wa_ref[...]).astype(BF16)

    row = pl.BlockSpec((tm, D_MODEL), lambda i: (i, 0))
    half = pl.BlockSpec((tm, 512), lambda i: (i, 0))
    return pl.pallas_call(
        body, name="mix_bwd", grid=(t // tm,),
        in_specs=[row, _whole(w_out), *_gl_specs(tm), _whole(b_gate), row, row, _whole(w_ret), _whole(w_att_t)],
        out_specs=[row, row, row, row, row, half, _whole(b_gate)],
        out_shape=[jax.ShapeDtypeStruct((t, D_MODEL), BF16)] * 5
                  + [jax.ShapeDtypeStruct((t, 512), BF16), jax.ShapeDtypeStruct(b_gate.shape, F32)],
        compiler_params=_params("arbitrary"),
    )(dh1b, w_out, proj, proj, proj, proj, b_gate, y_ret, y_att, w_ret, w_att_t)


def _ret_bwd(dgro, proj, o_ret, qr, kr, cs, sn, lg_arr, batch, seq, exchange):
    t = batch * seq
    nt = seq // RET_TILE

    def body(dgro_ref, rg_ref, o_ref, qr_ref, kr_ref, v_ref, cs_ref, sn_ref, lg_ref,
             dq_ref, dk_ref, dv_ref, drg_ref, do_ref, st_ref):
        lg = lg_ref[:, 0:1]
        inside, q_dec, k_dec, tile_dec = _decay(lg)

        state = jnp.zeros((RET_KEY_DIM, RET_VAL_DIM), F32)
        for i in range(nt - 1):
            rows = slice(i * RET_TILE, (i + 1) * RET_TILE)
            state = state * tile_dec + _dot_tn(_scaled(kr_ref[rows, :], k_dec), v_ref[rows, :])
            st_ref[i + 1] = state.astype(BF16)

        for i in range(nt):
            rows = slice(i * RET_TILE, (i + 1) * RET_TILE)
            o = o_ref[rows, :]
            xc = o - jnp.mean(o, axis=-1, keepdims=True)
            rs = lax.rsqrt(jnp.mean(xc * xc, axis=-1, keepdims=True) + EPS)
            nrm = xc * rs
            rg = rg_ref[rows, :].astype(F32)
            sg = _sigmoid(rg)
            dg = dgro_ref[rows, :].astype(F32)
            drg_ref[rows, :] = (dg * nrm * sg * (1.0 + rg * (1.0 - sg))).astype(BF16)
            dn = dg * rg * sg
            do = rs * (dn - jnp.mean(dn, axis=-1, keepdims=True)
                       - nrm * jnp.mean(dn * nrm, axis=-1, keepdims=True))
            do_ref[rows, :] = do.astype(BF16)

        dstate = jnp.zeros((RET_KEY_DIM, RET_VAL_DIM), F32)
        for i in reversed(range(nt)):
            rows = slice(i * RET_TILE, (i + 1) * RET_TILE)
            qi, ki, vi, doi = qr_ref[rows, :], kr_ref[rows, :], v_ref[rows, :], do_ref[rows, :]
            p = (_dot_nt(qi, ki) * inside).astype(BF16)
            dp = (_dot_nt(doi, vi) * inside).astype(BF16)
            dq = _dot(dp, ki)
            dk = _dot_tn(dp, qi)
            dv = _dot_tn(p, doi)
            if i > 0:
                dq = dq + _dot_nt(doi, st_ref[i]) * q_dec
            if i < nt - 1:
                dsb = dstate.astype(BF16)
                dk = dk + _dot_nt(vi, dsb) * k_dec
                dv = dv + _dot(_scaled(ki, k_dec), dsb)
            if i > 0:
                dstate = dstate * tile_dec + _dot_tn(_scaled(qi, q_dec), doi)
            dq_ref[rows, :] = (dq * cs_ref[rows, :] - pltpu.roll(dq, 64, 1) * sn_ref[rows, :]).astype(BF16)
            dk = (dk * cs_ref[rows, :] - pltpu.roll(dk, 64, 1) * sn_ref[rows, :]) * (RET_KEY_DIM ** -0.5)
            dk_ref[rows, :] = dk.astype(BF16)
            dv_ref[rows, :] = dv.astype(BF16)

    key = pl.BlockSpec((seq, RET_KEY_DIM), lambda b, h: (b, h))
    val = pl.BlockSpec((seq, RET_VAL_DIM), lambda b, h: (b, h))
    tab = pl.BlockSpec((seq, RET_KEY_DIM), lambda b, h: (0, 0))
    return _call(
        body, name="ret_bwd", grid=(batch, RET_HEADS),
        in_specs=[val, pl.BlockSpec((seq, RET_VAL_DIM), lambda b, h: (b, C_RG // RET_VAL_DIM + h)), val, key, key,
                  pl.BlockSpec((seq, RET_VAL_DIM), lambda b, h: (b, C_RV // RET_VAL_DIM + h)), tab, tab,
                  pl.BlockSpec((None, 1, LANES), lambda b, h: (h, 0, 0))],
        out_specs=[key, key, val, val],
        out_shape=[jax.ShapeDtypeStruct((t, RET_HEADS * RET_KEY_DIM), BF16)] * 2
                  + [jax.ShapeDtypeStruct((t, RET_HEADS * RET_VAL_DIM), BF16)] * 2,
        scratch=[pltpu.VMEM((seq, RET_VAL_DIM), BF16), pltpu.VMEM((nt, RET_KEY_DIM, RET_VAL_DIM), BF16)],
        semantics=("parallel", "parallel"), args=(dgro, proj, o_ret, qr, kr, proj, cs, sn, lg_arr),
        exchange=exchange)


def _att_bwd(proj, bias, dao, batch, seq, exchange):
    ni, q_spec, k_spec, v_spec, w_spec, b_spec, pad = _att_specs(batch, seq)
    t = batch * seq

    def body(q_ref, k_ref, v_ref, bias_ref, do_ref, dq_ref, dk_ref, dv_ref, dw_ref,
             dbias_ref, dk_acc, dv_acc, kp_ref, vp_ref, s_ref, dp_ref, e_ref, ds_ref):
        b, i = pl.program_id(1), pl.program_id(2)

        @pl.when((b == 0) & (i == 0))
        def _():
            dbias_ref[...] = jnp.zeros_like(dbias_ref)

        @pl.when(i == 0)
        def _():
            _att_pad(k_ref, kp_ref)
            _att_pad(v_ref, vp_ref)
            dk_acc[...] = jnp.zeros_like(dk_acc)
            dv_acc[...] = jnp.zeros_like(dv_acc)

        win = pl.ds(pl.multiple_of(i * ATT_Q, ATT_Q), ATT_WIN)
        lo = lax.broadcasted_iota(jnp.int32, (1, LANES), 1) < 64
        start = jnp.minimum(i, ATT_STARTS)
        for pair in range(ATT_PAIRS):
            cols = slice(pair * LANES, (pair + 1) * LANES)
            k2, v2, q2, do2 = kp_ref[win, cols], vp_ref[win, cols], q_ref[:, cols], do_ref[:, cols]
            dq = jnp.zeros((ATT_Q, LANES), F32)
            dk = jnp.zeros((LANES, ATT_WIN), F32)
            dv = jnp.zeros((LANES, ATT_WIN), F32)
            for e in range(2):
                h = 2 * pair + e
                sel = lo if e == 0 else jnp.logical_not(lo)
                qm = _att_head(q2, sel)
                dom = jnp.where(sel, do2, jnp.zeros_like(do2))
                s_ref[h] = _dot_nt(qm, k2)
                dp_ref[h] = _dot_nt(dom, v2)
                rsum = []
                for c in range(ATT_Q // ATT_ROWS):
                    rows = slice(c * ATT_ROWS, (c + 1) * ATT_ROWS)
                    ex, r = _att_softmax_rows(s_ref.at[h], bias_ref.at[pair, start, e], rows)
                    dp = dp_ref[h, rows, :]
                    mean = jnp.sum(dp * ex, axis=-1, keepdims=True) * r
                    ds = ex * ((dp - mean) * r)
                    dbias_ref[h, rows, :] += ds
                    ds_ref[h, rows, :] = ds.astype(BF16)
                    e_ref[h, rows, :] = ex.astype(BF16)
                    rsum.append(r)
                dq = dq + _dot(ds_ref[h], jnp.where(sel, k2, jnp.zeros_like(k2)))
                dk = dk + _dot_tn(qm, ds_ref[h])
                dv = dv + _dot_tn((dom.astype(F32) * jnp.concatenate(rsum, axis=0)).astype(BF16), e_ref[h])
            dq_ref[:, cols] = (dq * 0.125).astype(BF16)
            dk_acc[cols, win] += dk
            dv_acc[cols, win] += dv

        @pl.when(i == ni - 1)
        def _():
            dk_ref[...] = dk_acc[:, ATT_PAD:].T.astype(BF16)
            dv_ref[...] = dv_acc[:, ATT_PAD:].T.astype(BF16)

        @pl.when((b == batch - 1) & (i == ni - 1))
        def _():
            n_i = lax.broadcasted_iota(jnp.int32, (ATT_Q, BIAS_LEN), 0)
            for h in range(heads):
                xw = jnp.concatenate([jnp.zeros((ATT_Q, BIAS_LEN - ATT_WIN), F32), dbias_ref[h]], axis=1)
                for bit in range(8):
                    xw = jnp.where(((n_i >> bit) & 1) == 1, pltpu.roll(xw, BIAS_LEN - (1 << bit), 1), xw)
                dw_ref[h // 2, h % 2:h % 2 + 1, :] = jnp.sum(xw, axis=0, keepdims=True)

    heads = 2 * ATT_PAIRS
    seq_blk = pl.BlockSpec((seq, ATT_COLS), lambda g, b, i: (b, g))
    q_out = pl.BlockSpec((ATT_Q, ATT_COLS), lambda g, b, i: (b * ni + i, g))
    tile_f32, tile_bf16 = pltpu.VMEM((heads, ATT_Q, ATT_WIN), F32), pltpu.VMEM((heads, ATT_Q, ATT_WIN), BF16)
    acc = pltpu.VMEM((ATT_COLS, seq + ATT_PAD), F32)
    return _call(
        body, name="att_bwd", grid=(ATT_HEADS // heads, batch, ni),
        in_specs=[q_spec, k_spec, v_spec, b_spec, q_out],
        out_specs=[q_out, seq_blk, seq_blk, w_spec],
        out_shape=[jax.ShapeDtypeStruct((t, 512), BF16)] * 3
                  + [jax.ShapeDtypeStruct((ATT_HEADS // 2, 2, BIAS_LEN), F32)],
        scratch=[tile_f32, acc, acc, pad, pad, tile_f32, tile_f32, tile_bf16, tile_bf16],
        semantics=("arbitrary", "arbitrary", "arbitrary"), args=(proj, proj, proj, bias, dao), exchange=exchange,
        vmem=VMEM_LIMIT_ATT_BWD)


def _rms_in_bwd(x2, dxn, dh1, g1):
    t = x2.shape[0]
    tm = min(512, t)

    def body(x_ref, d_ref, h_ref, g_ref, dx_ref, dg_ref):
        @pl.when(pl.program_id(0) == 0)
        def _():
            dg_ref[...] = jnp.zeros_like(dg_ref)

        dx, dg_rows = _rms_bwd(x_ref[...], g_ref[...], d_ref[...])
        dx_ref[...] = h_ref[...] + dx
        dg_ref[...] += jnp.sum(dg_rows, axis=0, keepdims=True)

    row = pl.BlockSpec((tm, D_MODEL), lambda i: (i, 0))
    vec = pl.BlockSpec((1, D_MODEL), lambda i: (0, 0))
    return pl.pallas_call(
        body, name="rms_in_bwd", grid=(t // tm,),
        in_specs=[row, row, row, vec], out_specs=[row, vec],
        out_shape=[jax.ShapeDtypeStruct((t, D_MODEL), F32), jax.ShapeDtypeStruct((1, D_MODEL), F32)],
        compiler_params=_params("arbitrary"),
    )(x2, dxn, dh1, g1)


def _pack_small(dg1, dbr, dba, dg2, dg3, dw, loss):
    def body(a_ref, b_ref, c_ref, d_ref, e_ref, w_ref, l_ref, o_ref):
        o_ref[...] = jnp.zeros_like(o_ref)
        for r, ref in enumerate((a_ref, b_ref, c_ref, d_ref, e_ref)):
            o_ref[r:r + 1, :] = ref[...]
        o_ref[5:6, 0:LANES] = l_ref[...]
        for hp in range(ATT_HEADS // 2):
            o_ref[8 + 2 * hp:10 + 2 * hp, :] = w_ref[hp]

    return pl.pallas_call(body, name="pack_small",
                          out_shape=jax.ShapeDtypeStruct((16, D_MODEL), F32))(dg1, dbr, dba, dg2, dg3, dw, loss)


def _rotary_tables(seq):
    freqs = ROPE_BASE ** (-jnp.arange(0, RET_KEY_DIM, 2, dtype=F32) / RET_KEY_DIM)
    ang = jnp.arange(seq, dtype=F32)[:, None] * freqs[None, :]
    cos, sin = jnp.cos(ang), jnp.sin(ang)
    return jnp.concatenate([cos, cos], axis=1), jnp.concatenate([-sin, sin], axis=1)


def _bias_rows(rel_bias):
    n_far = BIAS_LEN - ATT_Q - MAX_REL + 1
    n_near = BIAS_LEN - n_far - (N_REL - 2)
    w = jnp.concatenate([jnp.broadcast_to(rel_bias[:, N_REL - 1:], (ATT_HEADS, n_far)),
                         rel_bias[:, 1:N_REL - 1][:, ::-1],
                         jnp.broadcast_to(rel_bias[:, :1], (ATT_HEADS, n_near))], axis=1)
    return w.reshape(ATT_HEADS // 2, 2, BIAS_LEN)


def _bias_rows_bwd(dw):
    n_far = BIAS_LEN - ATT_Q - MAX_REL + 1
    mid = dw[:, n_far:n_far + N_REL - 2][:, ::-1]
    return jnp.concatenate([jnp.sum(dw[:, n_far + N_REL - 2:], axis=1, keepdims=True), mid,
                            jnp.sum(dw[:, :n_far], axis=1, keepdims=True)], axis=1)


def _step(x, tgt, norm_mix, b_gate, norm_ffn, norm_final, rel_bias_shard, shard):
    batch, seq, _ = x.shape
    t = batch * seq
    n_rb = rel_bias_shard.shape[-1]
    x2, tgt2 = x.reshape(t, D_MODEL), tgt.reshape(t, D_MODEL)
    g3 = norm_final.reshape(1, D_MODEL)
    cs, sn = _rotary_tables(seq)
    lg = np.log(1.0 - 2.0 ** (-5.0 - np.arange(RET_HEADS, dtype=np.float32))).astype(np.float32)
    lg_arr = jnp.asarray(np.broadcast_to(lg[:, None, None], (RET_HEADS, 1, LANES)))

    def gather(*names):
        return _ChipGather([shard[nm] for nm in names])

    def scatter(*grads):
        return _Exchange(grads, scatter=True)

    rb_pad = jnp.pad(rel_bias_shard, ((0, 0), (0, LANES - n_rb)))
    (xn,), (w_in_half, rb_full) = _rms_fwd(x2, norm_mix,
                                           _ChipGather([shard["w_in_t"], rb_pad], parts=[(0, 2), (0, 1)]))
    rb_full = rb_full.reshape(N_DEV, ATT_HEADS, LANES)[:, :, :n_rb]
    bias, (w_in_t,) = _att_bias_tiles(
        _bias_rows(jnp.transpose(rb_full, (1, 0, 2)).reshape(ATT_HEADS, N_DEV * n_rb)),
        _ChipGather([shard["w_in_t"]], parts=[(1, 2)], into=[w_in_half]))
    proj, (w_ret, w_att_t, w_out, w_gate_t) = _mm(
        xn, w_in_t, tb=True, out_dtype=BF16, tm=1024, tn=1664, tk=1024, name="proj",
        exchange=gather("w_ret", "w_att_t", "w_out", "w_gate_t"))
    (gro, o_ret, qr, kr), _ = _ret_fwd(proj, cs, sn, lg_arr, batch, seq, None)
    (ao,), (w_up_t, w_down) = _att_fwd(proj, bias, batch, seq, gather("w_up_t", "w_down"))
    z, y_ret, y_att, h1, hn = _mix_out_fwd(gro, ao, proj, b_gate, w_ret, w_att_t, x2, w_out, norm_ffn)
    g_act, u_act, a_act = _ffn_up(hn, w_gate_t, w_up_t)
    dh2, dh2b, loss, dg3 = _ffn_down_loss(a_act, h1, tgt2, w_down, g3)

    wg = dict(out_dtype=BF16, tn=1024, ta=True)
    slots = {}
    dw_down = _mm(a_act, dh2b, tm=1408, tk=1024, name="dw_down", **wg)
    (d_gact, d_uact), _ = _ffn_bwd_act(dh2b, w_down, g_act, u_act, None)
    dw_gate = _mm(d_gact, hn, tm=1408, tk=1024, name="dw_gate", **wg)
    dw_up = _mm(d_uact, hn, tm=1408, tk=1024, name="dw_up", **wg)
    (dh1, dh1b, dg2), (slots["w_down"],) = _ffn_bwd_in(d_gact, d_uact, w_gate_t, w_up_t, h1, dh2, norm_ffn,
                                                     scatter(dw_down))
    dw_out = _mm(z, dh1b, tm=1024, tk=2048, name="dw_out", **wg)
    dyr, dya, dglr, dgla, dgro, dao, db = _mix_bwd(dh1b, w_out, proj, b_gate, y_ret, y_att, w_ret, w_att_t)
    dw_ret = _mm(gro, dyr, tm=1024, tk=2048, name="dw_ret", **wg)
    dw_att = _mm(dya, ao, tm=1024, tk=2048, name="dw_att", **wg)
    (drq, drk, drv, drg), _ = _ret_bwd(dgro, proj, o_ret, qr, kr, cs, sn, lg_arr, batch, seq, None)
    (daq, dak, dav, dw), (slots["w_gate_t"], slots["w_out"], slots["w_ret"], slots["w_att_t"]) = _att_bwd(
        proj, bias, dao, batch, seq, scatter(dw_gate, dw_out, dw_ret, dw_att))
    dproj = [drq, drk, drv, drg, daq, dak, dav, dglr, dgla]
    dw_in, (slots["w_up_t"],) = _mm_pieces(dproj, xn, ta=True, out_dtype=BF16, tm=512, tn=1024, tk=1024,
                                           name="dw_in", exchange=scatter(dw_up))
    (dw_in_sibling,) = _alone(_PairSwap([dw_in]), "swap_w_in")
    dw_in_pairs = _pair_add(dw_in, dw_in_sibling, "pair_w_in")
    dxn, (slots["w_in_t"],) = _mm_pieces(dproj, w_in_t, ta=False, out_dtype=F32, tm=1024, tn=1024, tk=512, name="dxn",
                                  exchange=_ChipScatter([dw_in_pairs]))
    dx, dg1 = _rms_in_bwd(x2, dxn, dh1, norm_mix)
    small = _pack_small(dg1, db[:, :D_MODEL], db[:, D_MODEL:], dg2, dg3, dw, loss)
    (small_slots,) = _alone(_ChipGather([small]), "gather_small")
    return dx.reshape(batch, seq, D_MODEL), slots, small_slots.reshape(N_DEV, 16, D_MODEL)


def _row_tile(r, c):
    return max(d for d in range(16, r + 1, 16) if r % d == 0 and (d * c <= 256 * 1024 or d == 16))


def _pair_add(grad, got, name):
    _, r, c = got.shape
    tr = r
    core = lax.axis_index("c").astype(jnp.int32).reshape(1)

    def body(core_ref, g_ref, a_ref, o_ref):
        o_ref[...] = (g_ref[...].astype(F32) + a_ref[...].astype(F32)).astype(o_ref.dtype)

    blk = pl.BlockSpec((None, tr, c), lambda q, i, core_ref: (q, i, 0))
    return pl.pallas_call(
        body, name=name,
        grid_spec=pltpu.PrefetchScalarGridSpec(
            num_scalar_prefetch=1, grid=(4, r // tr),
            in_specs=[pl.BlockSpec((None, None, tr, c), lambda q, i, core_ref: (q, core_ref[0], i, 0)), blk],
            out_specs=blk),
        out_shape=jax.ShapeDtypeStruct(got.shape, got.dtype),
        compiler_params=_params("parallel", "parallel"),
    )(core, grad.reshape(4, 2, r, c), got)


def _sum_slots(slots, name):
    n, r, c = slots.shape
    tr = _row_tile(r, c)

    def body(s_ref, o_ref):
        acc = s_ref[0].astype(F32)
        for s in range(1, n):
            acc = acc + s_ref[s].astype(F32)
        o_ref[...] = acc

    return pl.pallas_call(
        body, name=name, grid=(r // tr,),
        in_specs=[pl.BlockSpec((n, tr, c), lambda i: (0, i, 0))],
        out_specs=pl.BlockSpec((tr, c), lambda i: (i, 0)),
        out_shape=jax.ShapeDtypeStruct((r, c), F32),
        compiler_params=_params("parallel"),
    )(slots)


def _adamw_math(w, g, m, v):
    m = ADAM_B1 * m + (1.0 - ADAM_B1) * g
    v = ADAM_B2 * v + (1.0 - ADAM_B2) * (g * g)
    m_hat = m / (1.0 - ADAM_B1 ** ADAM_STEP)
    v_hat = v / (1.0 - ADAM_B2 ** ADAM_STEP)
    return -ADAM_LR * (m_hat / (jnp.sqrt(v_hat) + ADAM_EPS) + ADAM_WD * w), m, v


def _adamw(w, slots, m, v, name):
    n, r, c = slots.shape
    tr = _row_tile(r, c)

    def body(w_ref, s_ref, m_ref, v_ref, g_ref, d_ref, nm_ref, nv_ref):
        g = s_ref[0].astype(F32)
        for s in range(1, n):
            g = g + s_ref[s].astype(F32)
        g_ref[...] = g
        d_ref[...], nm_ref[...], nv_ref[...] = _adamw_math(w_ref[...], g, m_ref[...], v_ref[...])

    blk = pl.BlockSpec((tr, c), lambda i: (i, 0))
    return pl.pallas_call(
        body, name=name, grid=(r // tr,),
        in_specs=[blk, pl.BlockSpec((n, tr, c), lambda i: (0, i, 0)), blk, blk], out_specs=[blk] * 4,
        out_shape=[jax.ShapeDtypeStruct((r, c), F32)] * 4,
        compiler_params=_params("parallel"),
    )(w, slots, m, v)


def _adamw_small(ws, gs, ms, vs):
    n = len(ws)

    def body(*refs):
        for i in range(n):
            w_ref, g_ref, m_ref, v_ref = (refs[j * n + i] for j in range(4))
            d_ref, nm_ref, nv_ref = (refs[(4 + j) * n + i] for j in range(3))
            d_ref[...], nm_ref[...], nv_ref[...] = _adamw_math(w_ref[...], g_ref[...], m_ref[...], v_ref[...])

    shapes = [jax.ShapeDtypeStruct(w.shape, F32) for w in ws]
    outs = pl.pallas_call(body, name="adamw_small", out_shape=shapes * 3)(*ws, *gs, *ms, *vs)
    return outs[:n], outs[n:2 * n], outs[2 * n:]


def kernel(x, norm_mix, w_in, b_gate, rel_bias, w_ret_out, w_att_out, w_out, norm_ffn, w_ffn_gate, w_ffn_up, w_ffn_down, norm_final, loss_target, m_norm_mix, m_w_in, m_b_gate, m_rel_bias, m_w_ret_out, m_w_att_out, m_w_out, m_norm_ffn, m_w_ffn_gate, m_w_ffn_up, m_w_ffn_down, m_norm_final, v_norm_mix, v_w_in, v_b_gate, v_rel_bias, v_w_ret_out, v_w_att_out, v_w_out, v_norm_ffn, v_w_ffn_gate, v_w_ffn_up, v_w_ffn_down, v_norm_final):
    me = _index(_place())
    n_rb = rel_bias.shape[-1]

    shard = dict(w_in_t=w_in[0].T, w_gate_t=w_ffn_gate[0].T, w_up_t=w_ffn_up[0].T, w_down=w_ffn_down[0],
                 w_ret=w_ret_out[0], w_out=w_out[0], w_att_t=w_att_out[0].T)
    shard = {nm: s.astype(BF16) for nm, s in shard.items()}
    dx, slots, small_slots = _step(x, loss_target, norm_mix, b_gate, norm_ffn, norm_final, rel_bias[0], shard)
    small_sum = _sum_slots(small_slots, "sum_small")
    loss = small_sum[5, 0]

    transposed = dict(w_in="w_in_t", w_ffn_gate="w_gate_t", w_ffn_up="w_up_t", w_att_out="w_att_t")
    plain = dict(w_ffn_down="w_down", w_ret_out="w_ret", w_out="w_out")
    g = dict(
        norm_mix=small_sum[0:1], b_gate=jnp.concatenate([small_sum[1:2], small_sum[2:3]], axis=1),
        norm_ffn=small_sum[3:4], norm_final=small_sum[4:5],
        rel_bias=lax.dynamic_slice_in_dim(_bias_rows_bwd(small_sum[8:16]), me * n_rb, n_rb, axis=1),
    )
    w = dict(norm_mix=norm_mix, w_in=w_in, b_gate=b_gate, rel_bias=rel_bias, w_ret_out=w_ret_out, w_att_out=w_att_out,
             w_out=w_out, norm_ffn=norm_ffn, w_ffn_gate=w_ffn_gate, w_ffn_up=w_ffn_up, w_ffn_down=w_ffn_down,
             norm_final=norm_final)
    m = dict(norm_mix=m_norm_mix, w_in=m_w_in, b_gate=m_b_gate, rel_bias=m_rel_bias, w_ret_out=m_w_ret_out,
             w_att_out=m_w_att_out, w_out=m_w_out, norm_ffn=m_norm_ffn, w_ffn_gate=m_w_ffn_gate, w_ffn_up=m_w_ffn_up,
             w_ffn_down=m_w_ffn_down, norm_final=m_norm_final)
    v = dict(norm_mix=v_norm_mix, w_in=v_w_in, b_gate=v_b_gate, rel_bias=v_rel_bias, w_ret_out=v_w_ret_out,
             w_att_out=v_w_att_out, w_out=v_w_out, norm_ffn=v_norm_ffn, w_ffn_gate=v_w_ffn_gate, w_ffn_up=v_w_ffn_up,
             w_ffn_down=v_w_ffn_down, norm_final=v_norm_final)
    order = ("norm_mix", "w_in", "b_gate", "rel_bias", "w_ret_out", "w_att_out", "w_out", "norm_ffn",
             "w_ffn_gate", "w_ffn_up", "w_ffn_down", "norm_final")
    small_names = ("norm_mix", "b_gate", "rel_bias", "norm_ffn", "norm_final")

    def flat(a):
        return a[0] if a.ndim == 3 else a.reshape(-1, a.shape[-1])

    grad, delta, new_m, new_v = {}, {}, {}, {}
    for nm in order:
        if nm in transposed:
            res = _adamw(w[nm][0].T, slots[transposed[nm]], m[nm][0].T, v[nm][0].T, "adamw_" + nm)
            grad[nm], delta[nm], new_m[nm], new_v[nm] = (a.T[None] for a in res)
        elif nm in plain:
            res = _adamw(flat(w[nm]), slots[plain[nm]], flat(m[nm]), flat(v[nm]), "adamw_" + nm)
            grad[nm], delta[nm], new_m[nm], new_v[nm] = (a.reshape(w[nm].shape) for a in res)
    ds, nms, nvs = _adamw_small([flat(w[nm]) for nm in small_names], [g[nm] for nm in small_names],
                                [flat(m[nm]) for nm in small_names], [flat(v[nm]) for nm in small_names])
    for i, nm in enumerate(small_names):
        grad[nm], delta[nm], new_m[nm], new_v[nm] = (a.reshape(w[nm].shape) for a in (g[nm], ds[i], nms[i], nvs[i]))

    return (loss, dx, *[grad[nm] for nm in order], *[delta[nm] for nm in order],
            *[new_m[nm] for nm in order], *[new_v[nm] for nm in order])
```

```python
import numpy as np
import jax
import jax.numpy as jnp
from jax import lax
from jax.experimental import pallas as pl
from jax.experimental.pallas import tpu as pltpu

F32 = jnp.float32
BF16 = jnp.bfloat16
MESH = pl.DeviceIdType.MESH

D_MODEL = 1024
CHUNK = 64
RET_HEADS = 4
RET_KEY_DIM = 128
RET_VAL_DIM = 256
ATT_HEADS = 8
BAND_CHUNKS = 8
MAX_REL = 256
N_REL = CHUNK + MAX_REL
D_FF = 2816
N_IN = 6656
ROPE_BASE = 10000.0
EPS = 1e-6
NEG_INF = -1e30
C_RQ, C_RK, C_RV, C_RG, C_AQ, C_AK, C_AV, C_GL = 0, 512, 1024, 2048, 3072, 3584, 4096, 4608

ADAM_LR = 0.001
ADAM_B1 = 0.9
ADAM_B2 = 0.999
ADAM_EPS = 1e-08
ADAM_WD = 0.01
ADAM_STEP = 10

N_DEV = 8
LANES = 128
RET_TILE = 256
ATT_Q = 256
ATT_PAD = BAND_CHUNKS * CHUNK
ATT_WIN = ATT_PAD + ATT_Q
ATT_STARTS = ATT_PAD // ATT_Q
ATT_ROWS = 32
BIAS_LEN = 1024
VMEM_LIMIT = 48 * 1024 * 1024
VMEM_LIMIT_ATT_BWD = 56 * 1024 * 1024


def _params(*sem, vmem=VMEM_LIMIT):
    return pltpu.CompilerParams(dimension_semantics=sem, vmem_limit_bytes=vmem)


def _dot(a, b):
    return lax.dot_general(a, b, (((1,), (0,)), ((), ())), preferred_element_type=F32)


def _dot_nt(a, b):
    return lax.dot_general(a, b, (((1,), (1,)), ((), ())), preferred_element_type=F32)


def _dot_tn(a, b):
    return lax.dot_general(a, b, (((0,), (0,)), ((), ())), preferred_element_type=F32)


def _sigmoid(x):
    return 1.0 / (1.0 + jnp.exp(-x))


def _rms_bwd(x, g, dy):
    r = lax.rsqrt(jnp.mean(x * x, axis=-1, keepdims=True) + EPS)
    u = dy * g
    dx = r * u - x * (r * r * r) * jnp.mean(u * x, axis=-1, keepdims=True)
    return dx, dy * x * r


def _place():
    return lax.axis_index("x"), lax.axis_index("y"), lax.axis_index("c")


def _peer(k):
    x, y, c = _place()
    return ((1 - x) if k & 4 else x, (1 - y) if k & 2 else y, (1 - c) if k & 1 else c)


def _index(place):
    return 4 * place[0] + 2 * place[1] + place[2]


def _rows(ref, block, nrows):
    align = 16 if ref.dtype == BF16 else 8
    return ref.at[pl.ds(pl.multiple_of(block * nrows, align), nrows)]


class _Exchange:
    def __init__(self, arrays, scatter):
        self.arrays, self.scatter, self.n = list(arrays), scatter, len(arrays)

    def out_shape(self):
        if self.scatter:
            return [jax.ShapeDtypeStruct((N_DEV, a.shape[0] // N_DEV) + a.shape[1:], a.dtype) for a in self.arrays]
        return [jax.ShapeDtypeStruct((N_DEV * a.shape[0],) + a.shape[1:], a.dtype) for a in self.arrays]

    def scratch(self):
        return [pltpu.SemaphoreType.DMA((self.n, N_DEV - 1)), pltpu.SemaphoreType.DMA((self.n, N_DEV - 1)),
                pltpu.SemaphoreType.DMA((self.n,))]

    def _copies(self, ins, outs, sems):
        send_sems, recv_sems, local_sems = sems
        me = _index(_place())

        def src(w, to):
            return _rows(ins[w], to, ins[w].shape[0] // N_DEV) if self.scatter else ins[w]

        def dst(w, origin):
            return outs[w].at[origin] if self.scatter else _rows(outs[w], origin, ins[w].shape[0])

        def remote(w, k, to, origin):
            return pltpu.make_async_remote_copy(src_ref=src(w, to), dst_ref=dst(w, origin),
                                                send_sem=send_sems.at[w, k - 1], recv_sem=recv_sems.at[w, k - 1],
                                                device_id=_peer(k), device_id_type=MESH)

        pairs = [(w, k) for w in range(self.n) for k in range(1, N_DEV)]
        own = lambda: [pltpu.make_async_copy(src(w, me), dst(w, me), local_sems.at[w]) for w in range(self.n)]
        sent = lambda: [remote(w, k, _index(_peer(k)), me) for w, k in pairs]
        arriving = lambda: [remote(w, k, me, _index(_peer(k))) for w, k in pairs]
        return own, sent, arriving

    def start(self, ins, outs, sems):
        own, sent, _ = self._copies(ins, outs, sems)
        for cp in own() + sent():
            cp.start()

    def wait(self, ins, outs, sems):
        own, sent, arriving = self._copies(ins, outs, sems)
        for cp in arriving():
            cp.wait_recv()
        for cp in sent():
            cp.wait_send()
        for cp in own():
            cp.wait()


class _PairSwap:
    def __init__(self, arrays):
        self.arrays, self.n = list(arrays), len(arrays)

    def out_shape(self):
        return [jax.ShapeDtypeStruct((4, a.shape[0] // N_DEV) + a.shape[1:], a.dtype) for a in self.arrays]

    def scratch(self):
        return [pltpu.SemaphoreType.DMA((self.n, 4)), pltpu.SemaphoreType.DMA((self.n, 4))]

    def _copies(self, ins, outs, sems):
        send_sems, recv_sems = sems
        x, y, c = _place()
        return [pltpu.make_async_remote_copy(
            src_ref=_rows(ins[w], 2 * q + 1 - c, ins[w].shape[0] // N_DEV), dst_ref=outs[w].at[q],
            send_sem=send_sems.at[w, q], recv_sem=recv_sems.at[w, q],
            device_id=(x, y, 1 - c), device_id_type=MESH) for w in range(self.n) for q in range(4)]

    def start(self, ins, outs, sems):
        for cp in self._copies(ins, outs, sems):
            cp.start()

    def wait(self, ins, outs, sems):
        for cp in self._copies(ins, outs, sems):
            cp.wait()


class _ChipScatter:
    def __init__(self, arrays):
        self.arrays, self.n = list(arrays), len(arrays)

    def out_shape(self):
        return [jax.ShapeDtypeStruct(a.shape, a.dtype) for a in self.arrays]

    def scratch(self):
        return [pltpu.SemaphoreType.DMA((self.n, 3)), pltpu.SemaphoreType.DMA((self.n, 3)),
                pltpu.SemaphoreType.DMA((self.n,))]

    def _copies(self, ins, outs, sems):
        send_sems, recv_sems, local_sems = sems
        x, y, c = _place()
        mine = 2 * x + y
        sent, arriving = [], []
        for w in range(self.n):
            for k in range(1, 4):
                tx, ty = (1 - x) if k & 2 else x, (1 - y) if k & 1 else y
                other = 2 * tx + ty
                sent.append(lambda w=w, k=k, tx=tx, ty=ty, other=other: pltpu.make_async_remote_copy(
                    src_ref=ins[w].at[other], dst_ref=outs[w].at[mine], send_sem=send_sems.at[w, k - 1],
                    recv_sem=recv_sems.at[w, k - 1], device_id=(tx, ty, c), device_id_type=MESH))
                arriving.append(lambda w=w, k=k, tx=tx, ty=ty, other=other: pltpu.make_async_remote_copy(
                    src_ref=ins[w].at[mine], dst_ref=outs[w].at[other], send_sem=send_sems.at[w, k - 1],
                    recv_sem=recv_sems.at[w, k - 1], device_id=(tx, ty, c), device_id_type=MESH))
        own = [lambda w=w: pltpu.make_async_copy(ins[w].at[mine], outs[w].at[mine], local_sems.at[w])
               for w in range(self.n)]
        return own, sent, arriving

    def start(self, ins, outs, sems):
        own, sent, _ = self._copies(ins, outs, sems)
        for cp in own + sent:
            cp().start()

    def wait(self, ins, outs, sems):
        own, sent, arriving = self._copies(ins, outs, sems)
        for cp in arriving:
            cp().wait_recv()
        for cp in sent:
            cp().wait_send()
        for cp in own:
            cp().wait()


class _ChipGather:
    def __init__(self, arrays, parts=None, into=None):
        self.arrays, self.n, self.into = list(arrays), len(arrays), into
        self.parts = parts or [(0, 1)] * self.n

    def out_shape(self):
        return [jax.ShapeDtypeStruct((N_DEV * a.shape[0],) + a.shape[1:], a.dtype) for a in self.arrays]

    def scratch(self):
        return [pltpu.SemaphoreType.DMA((self.n, N_DEV - 1)), pltpu.SemaphoreType.DMA((self.n, N_DEV - 1)),
                pltpu.SemaphoreType.DMA((self.n,))]

    def _parts(self, ins, outs, sems):
        send_sems, recv_sems, local_sems = sems
        x, y, c = _place()
        me, sibling = (x, y, c), (x, y, 1 - c)
        chips = [(1 - x, y), (x, 1 - y), (1 - x, 1 - y)]

        def rows(w, place, whole):
            (index, count), r = self.parts[w], ins[w].shape[0]
            lo, size = (0, r) if whole else (index * (r // count), r // count)
            align = 16 if ins[w].dtype == BF16 else 8
            return outs[w].at[pl.ds(pl.multiple_of(_index(place) * r + lo, align), size)]

        def mine(w, whole):
            (index, count), r = self.parts[w], ins[w].shape[0]
            return ins[w] if whole or count == 1 else ins[w].at[pl.ds(index * (r // count), r // count)]

        def copy(w, k, block, to, own=False):
            whole = k == 0
            return pltpu.make_async_remote_copy(src_ref=mine(w, whole) if own else rows(w, block, whole),
                                                dst_ref=rows(w, block, whole),
                                                send_sem=send_sems.at[w, k], recv_sem=recv_sems.at[w, k],
                                                device_id=to, device_id_type=MESH)

        def local(w):
            return pltpu.make_async_copy(ins[w], rows(w, me, True), local_sems.at[w])

        return me, sibling, chips, c, copy, local, [index == 0 for index, _ in self.parts]

    def start(self, ins, outs, sems):
        me, sibling, chips, c, copy, local, places_own = self._parts(ins, outs, sems)
        for w in range(self.n):
            if places_own[w]:
                local(w).start()
                copy(w, 0, me, sibling, own=True).start()
            for j, chip in enumerate(chips):
                copy(w, 1 + j, me, (*chip, c), own=True).start()

    def wait(self, ins, outs, sems):
        me, sibling, chips, c, copy, local, places_own = self._parts(ins, outs, sems)
        for w in range(self.n):
            for j, chip in enumerate(chips):
                copy(w, 1 + j, (*chip, c), me).wait_recv()
                copy(w, 4 + j, (*chip, c), sibling).start()
        for w in range(self.n):
            if places_own[w]:
                copy(w, 0, sibling, me).wait_recv()
                copy(w, 0, me, sibling, own=True).wait_send()
                local(w).wait()
            for j, chip in enumerate(chips):
                copy(w, 4 + j, (*chip, 1 - c), me).wait_recv()
                copy(w, 1 + j, me, (*chip, c), own=True).wait_send()
                copy(w, 4 + j, (*chip, c), sibling).wait_send()


def _call(body, *, name, grid, in_specs, out_specs, out_shape, scratch=(), semantics, args, exchange=None,
          vmem=VMEM_LIMIT):
    if exchange is None:
        return pl.pallas_call(body, name=name, grid=grid, in_specs=in_specs, out_specs=out_specs, out_shape=out_shape,
                              scratch_shapes=list(scratch),
                              compiler_params=_params(*semantics, vmem=vmem))(*args), None
    n_in, n_out, n_scr, nx = len(in_specs), len(out_specs), len(scratch), exchange.n
    into = list(getattr(exchange, "into", None) or [])

    def full_body(*refs):
        ins, refs = refs[:n_in], refs[n_in:]
        x_in, refs = refs[:nx], refs[nx + len(into):]
        outs, refs = refs[:n_out], refs[n_out:]
        x_out, refs = refs[:nx], refs[nx:]
        scr, sems = refs[:n_scr], refs[n_scr:]
        first, last = True, True
        for axis, size in enumerate(grid):
            first = jnp.logical_and(first, pl.program_id(axis) == 0)
            last = jnp.logical_and(last, pl.program_id(axis) == size - 1)
        if grid:
            pl.when(first)(lambda: exchange.start(x_in, x_out, sems))
        else:
            exchange.start(x_in, x_out, sems)
        body(*ins, *outs, *scr)
        if grid:
            pl.when(last)(lambda: exchange.wait(x_in, x_out, sems))
        else:
            exchange.wait(x_in, x_out, sems)

    hbm = pl.BlockSpec(memory_space=pltpu.HBM)
    res = pl.pallas_call(
        full_body, name=name, grid=grid,
        in_specs=list(in_specs) + [hbm] * (nx + len(into)), out_specs=list(out_specs) + [hbm] * nx,
        out_shape=list(out_shape) + exchange.out_shape(),
        scratch_shapes=list(scratch) + exchange.scratch(),
        input_output_aliases={n_in + nx + w: n_out + w for w in range(len(into))},
        compiler_params=_params(*(["arbitrary"] * len(grid)), vmem=vmem),
    )(*args, *exchange.arrays, *into)
    return res[:n_out], res[n_out:]


def _alone(exchange, name):
    return _call(lambda: None, name=name, grid=(), in_specs=[], out_specs=[], out_shape=[], semantics=(),
                 args=(), exchange=exchange)[1]


def _mm(a, b, *, ta=False, tb=False, out_dtype, tm, tn, tk, name, exchange=None):
    m, k = (a.shape[1], a.shape[0]) if ta else a.shape
    n = b.shape[0] if tb else b.shape[1]
    assert k == (b.shape[1] if tb else b.shape[0])
    tm, tn, tk = min(tm, m), min(tn, n), min(tk, k)
    assert m % tm == 0 and n % tn == 0 and k % tk == 0, (name, m, n, k)
    nk = k // tk
    dims = (((0 if ta else 1,), (1 if tb else 0,)), ((), ()))

    def body(a_ref, b_ref, o_ref, *acc):
        prod = lax.dot_general(a_ref[...].astype(BF16), b_ref[...].astype(BF16), dims, preferred_element_type=F32)
        if nk == 1:
            o_ref[...] = prod.astype(o_ref.dtype)
            return
        acc_ref, kk = acc[0], pl.program_id(2)

        @pl.when(kk == 0)
        def _():
            acc_ref[...] = prod

        @pl.when((kk > 0) & (kk < nk - 1))
        def _():
            acc_ref[...] += prod

        @pl.when(kk == nk - 1)
        def _():
            o_ref[...] = (acc_ref[...] + prod).astype(o_ref.dtype)

    a_spec = (pl.BlockSpec((tk, tm), lambda i, j, kk: (kk, i)) if ta
              else pl.BlockSpec((tm, tk), lambda i, j, kk: (i, kk)))
    b_spec = (pl.BlockSpec((tn, tk), lambda i, j, kk: (j, kk)) if tb
              else pl.BlockSpec((tk, tn), lambda i, j, kk: (kk, j)))
    (out,), moved = _call(
        body, name=name, grid=(m // tm, n // tn, nk),
        in_specs=[a_spec, b_spec],
        out_specs=[pl.BlockSpec((tm, tn), lambda i, j, kk: (i, j))],
        out_shape=[pltpu.HBM((m, n), out_dtype)],
        scratch=[pltpu.VMEM((tm, tn), F32)] if nk > 1 else [],
        semantics=("parallel", "parallel", "arbitrary"), args=(a, b), exchange=exchange)
    return out if exchange is None else (out, moved)


def _mm_pieces(pieces, b, *, ta, out_dtype, tm, tn, tk, name, exchange=None):
    rows, n = pieces[0].shape[0], b.shape[1]
    step = tm if ta else tk
    assert all(p.shape[0] == rows and p.shape[1] % step == 0 for p in pieces), name
    edges = [int(e) for e in np.cumsum([0] + [p.shape[1] // step for p in pieces])]
    total = edges[-1] * step
    m, k = (total, rows) if ta else (rows, total)
    assert b.shape[0] == k and m % tm == 0 and n % tn == 0 and k % tk == 0, name
    nk, npieces = k // tk, len(pieces)
    dims = (((0 if ta else 1,), (0,)), ((), ()))
    b_resident = ta and n == tn

    def body(*refs):
        a_refs, (b_ref, o_ref, acc_ref) = refs[:npieces], refs[npieces:]
        kk = pl.program_id(2)
        pos = pl.program_id(0) if ta else kk

        @pl.when(kk == 0)
        def _():
            acc_ref[...] = jnp.zeros_like(acc_ref)

        def b_tile():
            return b_ref[pl.ds(pl.multiple_of(kk * tk, tk), tk), :] if b_resident else b_ref[...]

        for p, a_ref in enumerate(a_refs):
            @pl.when((pos >= edges[p]) & (pos < edges[p + 1]))
            def _(a_ref=a_ref):
                acc_ref[...] += lax.dot_general(a_ref[...], b_tile(), dims, preferred_element_type=F32)

        @pl.when(kk == nk - 1)
        def _():
            o_ref[...] = acc_ref[...].astype(o_ref.dtype)

    def a_spec(p):
        lo, last = edges[p], edges[p + 1] - edges[p] - 1
        if ta:
            def index(i, j, kk):
                inside = (i >= lo) & (i <= lo + last)
                return jnp.where(inside, kk, 0), jnp.clip(i - lo, 0, last)
            return pl.BlockSpec((tk, tm), index)
        return pl.BlockSpec((tm, tk), lambda i, j, kk: (i, jnp.clip(kk - lo, 0, last)))

    (out,), moved = _call(
        body, name=name, grid=(m // tm, n // tn, nk),
        in_specs=[a_spec(p) for p in range(npieces)]
                 + [pl.BlockSpec(b.shape, lambda i, j, kk: (0, 0)) if b_resident
                    else pl.BlockSpec((tk, tn), lambda i, j, kk: (kk, j))],
        out_specs=[pl.BlockSpec((tm, tn), lambda i, j, kk: (i, j))],
        out_shape=[pltpu.HBM((m, n), out_dtype)],
        scratch=[pltpu.VMEM((tm, tn), F32)],
        semantics=("parallel", "parallel", "arbitrary"), args=(*pieces, b), exchange=exchange)
    return out if exchange is None else (out, moved)


def _rms_fwd(x2, g, exchange):
    t = x2.shape[0]
    tm = min(512, t)

    def body(x_ref, g_ref, o_ref):
        x = x_ref[...]
        r = lax.rsqrt(jnp.mean(x * x, axis=-1, keepdims=True) + EPS)
        o_ref[...] = (x * r * g_ref[...]).astype(o_ref.dtype)

    return _call(
        body, name="rms_in_fwd", grid=(t // tm,),
        in_specs=[pl.BlockSpec((tm, D_MODEL), lambda i: (i, 0)), pl.BlockSpec((1, D_MODEL), lambda i: (0, 0))],
        out_specs=[pl.BlockSpec((tm, D_MODEL), lambda i: (i, 0))],
        out_shape=[jax.ShapeDtypeStruct((t, D_MODEL), BF16)],
        semantics=("parallel",), args=(x2, g), exchange=exchange)


def _decay(lg):
    row = lax.broadcasted_iota(jnp.int32, (RET_TILE, RET_TILE), 0)
    col = lax.broadcasted_iota(jnp.int32, (RET_TILE, RET_TILE), 1)
    within = jnp.exp(lg * jnp.abs(row - col).astype(F32))
    inside = jnp.where((col >> 6) <= (row >> 6), within, 0.0)
    pos = lax.broadcasted_iota(jnp.int32, (RET_TILE, 1), 0).astype(F32)
    q_dec = jnp.exp(lg * (pos + 1.0))
    k_dec = jnp.exp(lg * (RET_TILE - 1.0 - pos))
    tile_dec = jnp.exp(lg * float(RET_TILE))
    return inside, q_dec, k_dec, tile_dec


def _scaled(a_bf16, dec):
    return (a_bf16.astype(F32) * dec).astype(BF16)


RET_GROUP = 2


def _per_head(one_head, kinds):
    def body(*refs):
        for head in range(RET_GROUP):
            def cut(ref, kind):
                if kind in "kv":
                    width = RET_KEY_DIM if kind == "k" else RET_VAL_DIM
                    return ref.at[:, head * width:(head + 1) * width]
                return ref.at[head] if kind == "h" else ref
            one_head(*[cut(ref, kind) for ref, kind in zip(refs, kinds)])
    return body


def _ret_specs(seq):
    kw, vw = RET_GROUP * RET_KEY_DIM, RET_GROUP * RET_VAL_DIM
    key = lambda base: pl.BlockSpec((seq, kw), lambda b, g: (b, base // kw + g))
    val = lambda base: pl.BlockSpec((seq, vw), lambda b, g: (b, base // vw + g))
    tab = pl.BlockSpec((seq, RET_KEY_DIM), lambda b, g: (0, 0))
    lgs = pl.BlockSpec((RET_GROUP, 1, LANES), lambda b, g: (g, 0, 0))
    return key, val, tab, lgs


def _ret_fwd(proj, cs, sn, lg_arr, batch, seq, exchange):
    t = batch * seq
    nt = seq // RET_TILE

    def one_head(q_ref, k_ref, v_ref, rg_ref, cs_ref, sn_ref, lg_ref, gro_ref, o_ref, qr_ref, kr_ref):
        lg = lg_ref[:, 0:1]
        cs_t, sn_t = cs_ref[...], sn_ref[...]
        q = q_ref[...].astype(F32)
        k = k_ref[...].astype(F32)
        qr_ref[...] = (q * cs_t + pltpu.roll(q, 64, 1) * sn_t).astype(BF16)
        kr_ref[...] = ((k * cs_t + pltpu.roll(k, 64, 1) * sn_t) * (RET_KEY_DIM ** -0.5)).astype(BF16)
        inside, q_dec, k_dec, tile_dec = _decay(lg)
        state = jnp.zeros((RET_KEY_DIM, RET_VAL_DIM), F32)
        for i in range(nt):
            rows = slice(i * RET_TILE, (i + 1) * RET_TILE)
            qi, ki, vi = qr_ref[rows, :], kr_ref[rows, :], v_ref[rows, :]
            acc = _dot((_dot_nt(qi, ki) * inside).astype(BF16), vi)
            if i > 0:
                acc = acc + _dot(_scaled(qi, q_dec), state.astype(BF16))
            if i < nt - 1:
                state = state * tile_dec + _dot_tn(_scaled(ki, k_dec), vi)
            o_ref[rows, :] = acc
            xc = acc - jnp.mean(acc, axis=-1, keepdims=True)
            nrm = xc * lax.rsqrt(jnp.mean(xc * xc, axis=-1, keepdims=True) + EPS)
            rg = rg_ref[rows, :].astype(F32)
            gro_ref[rows, :] = (rg * _sigmoid(rg) * nrm).astype(BF16)

    key, val, tab, lgs = _ret_specs(seq)
    return _call(
        _per_head(one_head, "kkvvsshvvkk"), name="ret_fwd", grid=(batch, RET_HEADS // RET_GROUP),
        in_specs=[key(C_RQ), key(C_RK), val(C_RV), val(C_RG), tab, tab, lgs],
        out_specs=[val(0), val(0), key(0), key(0)],
        out_shape=[jax.ShapeDtypeStruct((t, RET_HEADS * RET_VAL_DIM), BF16),
                   jax.ShapeDtypeStruct((t, RET_HEADS * RET_VAL_DIM), F32),
                   jax.ShapeDtypeStruct((t, RET_HEADS * RET_KEY_DIM), BF16),
                   jax.ShapeDtypeStruct((t, RET_HEADS * RET_KEY_DIM), BF16)],
        semantics=("parallel", "parallel"), args=(proj, proj, proj, proj, cs, sn, lg_arr), exchange=exchange)


def _att_bias(w_ref, bias_ref):
    n_i = lax.broadcasted_iota(jnp.int32, (ATT_Q, BIAS_LEN), 0)
    qc = lax.broadcasted_iota(jnp.int32, (ATT_Q, ATT_WIN), 0) >> 6
    kc = lax.broadcasted_iota(jnp.int32, (ATT_Q, ATT_WIN), 1) >> 6
    dc = qc + BAND_CHUNKS - kc
    band = (dc >= 0) & (dc <= BAND_CHUNKS)
    key = lax.broadcasted_iota(jnp.int32, (ATT_Q, ATT_WIN), 1)
    for e in range(2):
        xw = jnp.broadcast_to(w_ref[e:e + 1, :], (ATT_Q, BIAS_LEN))
        for bit in range(8):
            xw = jnp.where(((n_i >> bit) & 1) == 1, pltpu.roll(xw, 1 << bit, 1), xw)
        bias = jnp.where(band, xw[:, BIAS_LEN - ATT_WIN:], NEG_INF)
        for first in range(ATT_STARTS):
            bias_ref[first, e] = jnp.where(key + (first * ATT_Q - ATT_PAD) >= 0, bias, NEG_INF)
        bias_ref[ATT_STARTS, e] = bias


ATT_PAIRS = 2
ATT_COLS = ATT_PAIRS * LANES


def _att_specs(batch, seq):
    ni = seq // ATT_Q
    q_spec = pl.BlockSpec((ATT_Q, ATT_COLS), lambda g, b, i: (b * ni + i, C_AQ // ATT_COLS + g))
    k_spec = pl.BlockSpec((seq, ATT_COLS), lambda g, b, i: (b, C_AK // ATT_COLS + g))
    v_spec = pl.BlockSpec((seq, ATT_COLS), lambda g, b, i: (b, C_AV // ATT_COLS + g))
    w_spec = pl.BlockSpec((ATT_PAIRS, 2, BIAS_LEN), lambda g, b, i: (g, 0, 0))
    b_spec = pl.BlockSpec((ATT_PAIRS, ATT_STARTS + 1, 2, ATT_Q, ATT_WIN), lambda g, b, i: (g, 0, 0, 0, 0))
    pad = pltpu.VMEM((seq + ATT_PAD, ATT_COLS), BF16)
    return ni, q_spec, k_spec, v_spec, w_spec, b_spec, pad


def _att_bias_tiles(wvec, exchange):
    (tiles,), moved = _call(
        lambda w_ref, o_ref: _att_bias(w_ref, o_ref), name="att_bias", grid=(ATT_HEADS // 2,),
        in_specs=[pl.BlockSpec((None, 2, BIAS_LEN), lambda hp: (hp, 0, 0))],
        out_specs=[pl.BlockSpec((None, ATT_STARTS + 1, 2, ATT_Q, ATT_WIN), lambda hp: (hp, 0, 0, 0, 0))],
        out_shape=[jax.ShapeDtypeStruct((ATT_HEADS // 2, ATT_STARTS + 1, 2, ATT_Q, ATT_WIN), F32)],
        semantics=("parallel",), args=(wvec,), exchange=exchange)
    return tiles, moved


def _att_pad(src_ref, pad_ref):
    pad_ref[:ATT_PAD, :] = jnp.zeros((ATT_PAD, ATT_COLS), BF16)
    pad_ref[ATT_PAD:, :] = src_ref[...]


def _att_head(q2, sel):
    return jnp.where(sel, q2, jnp.zeros_like(q2)) * 0.125


def _att_softmax_rows(s_ref, bias_ref, rows):
    s = s_ref[rows, :] + bias_ref[rows, :]
    ex = jnp.exp(s - jnp.max(s, axis=-1, keepdims=True))
    return ex, 1.0 / jnp.sum(ex, axis=-1, keepdims=True)


def _att_fwd(proj, bias, batch, seq, exchange):
    ni, q_spec, k_spec, v_spec, _, b_spec, pad = _att_specs(batch, seq)

    def body(q_ref, k_ref, v_ref, bias_ref, o_ref, kp_ref, vp_ref, s_ref, e_ref):
        i = pl.program_id(2)

        @pl.when(i == 0)
        def _():
            _att_pad(k_ref, kp_ref)
            _att_pad(v_ref, vp_ref)

        win = pl.ds(pl.multiple_of(i * ATT_Q, ATT_Q), ATT_WIN)
        lo = lax.broadcasted_iota(jnp.int32, (1, LANES), 1) < 64
        start = jnp.minimum(i, ATT_STARTS)
        for pair in range(ATT_PAIRS):
            cols = slice(pair * LANES, (pair + 1) * LANES)
            k2, v2, q2 = kp_ref[win, cols], vp_ref[win, cols], q_ref[:, cols]
            out = jnp.zeros((ATT_Q, LANES), F32)
            for e in range(2):
                h = 2 * pair + e
                sel = lo if e == 0 else jnp.logical_not(lo)
                s_ref[h] = _dot_nt(_att_head(q2, sel), k2)
                rsum = []
                for c in range(ATT_Q // ATT_ROWS):
                    rows = slice(c * ATT_ROWS, (c + 1) * ATT_ROWS)
                    ex, r = _att_softmax_rows(s_ref.at[h], bias_ref.at[pair, start, e], rows)
                    e_ref[h, rows, :] = ex.astype(BF16)
                    rsum.append(r)
                out = out + _dot(e_ref[h], jnp.where(sel, v2, jnp.zeros_like(v2))) * jnp.concatenate(rsum, axis=0)
            o_ref[:, cols] = out.astype(BF16)

    heads = 2 * ATT_PAIRS
    return _call(
        body, name="att_fwd", grid=(ATT_HEADS // heads, batch, ni),
        in_specs=[q_spec, k_spec, v_spec, b_spec],
        out_specs=[pl.BlockSpec((ATT_Q, ATT_COLS), lambda g, b, i: (b * ni + i, g))],
        out_shape=[jax.ShapeDtypeStruct((batch * seq, ATT_HEADS * 64), BF16)],
        scratch=[pad, pad, pltpu.VMEM((heads, ATT_Q, ATT_WIN), F32), pltpu.VMEM((heads, ATT_Q, ATT_WIN), BF16)],
        semantics=("arbitrary", "arbitrary", "arbitrary"), args=(proj, proj, proj, bias), exchange=exchange)


GL_HALF = 512


def _gl_specs(tm):
    return [pl.BlockSpec((tm, GL_HALF), lambda i, c=C_GL // GL_HALF + j: (i, c)) for j in range(4)]


def _gates(gl_refs, b_ref):
    logits = [ref[...].astype(F32) for ref in gl_refs]
    gr = _sigmoid(jnp.concatenate(logits[:2], axis=1) + b_ref[:, :D_MODEL])
    ga = _sigmoid(jnp.concatenate(logits[2:], axis=1) + b_ref[:, D_MODEL:])
    return gr, ga


def _whole(a):
    return pl.BlockSpec(a.shape, lambda i: (0,) * a.ndim)


def _mix_out_fwd(gro, ao, proj, b_gate, w_ret, w_att_t, x2, w_out, g2):
    t = gro.shape[0]
    tm = min(256, t)

    def body(gro_ref, ao_ref, gl0, gl1, gl2, gl3, b_ref, wr_ref, wa_ref, x_ref, wo_ref, g_ref,
             z_ref, yr_ref, ya_ref, h_ref, hn_ref):
        yr = _dot(gro_ref[...], wr_ref[...])
        ya = _dot_nt(ao_ref[...], wa_ref[...])
        yr_ref[...] = yr.astype(BF16)
        ya_ref[...] = ya.astype(BF16)
        gr, ga = _gates((gl0, gl1, gl2, gl3), b_ref)
        z = (gr * yr + ga * ya).astype(BF16)
        z_ref[...] = z
        h = x_ref[...] + _dot(z, wo_ref[...])
        h_ref[...] = h
        r = lax.rsqrt(jnp.mean(h * h, axis=-1, keepdims=True) + EPS)
        hn_ref[...] = (h * r * g_ref[...]).astype(BF16)

    row = pl.BlockSpec((tm, D_MODEL), lambda i: (i, 0))
    return pl.pallas_call(
        body, name="mix_out_fwd", grid=(t // tm,),
        in_specs=[row, pl.BlockSpec((tm, 512), lambda i: (i, 0)), *_gl_specs(tm),
                  _whole(b_gate), _whole(w_ret), _whole(w_att_t), row, _whole(w_out), _whole(g2)],
        out_specs=[row] * 5,
        out_shape=[jax.ShapeDtypeStruct((t, D_MODEL), BF16)] * 3
                  + [jax.ShapeDtypeStruct((t, D_MODEL), F32), jax.ShapeDtypeStruct((t, D_MODEL), BF16)],
        compiler_params=_params("parallel"),
    )(gro, ao, proj, proj, proj, proj, b_gate, w_ret, w_att_t, x2, w_out, g2)


def _col_chunks(width, chunk=384):
    return [slice(lo, min(lo + chunk, width)) for lo in range(0, width, chunk)]


def _ffn_up(hn, wg_t, wu_t):
    t = hn.shape[0]
    tm, tn = min(512, t), D_FF // 2

    def body(h_ref, wg_ref, wu_ref, g_ref, u_ref, a_ref):
        g = _dot_nt(h_ref[...], wg_ref[...])
        u = _dot_nt(h_ref[...], wu_ref[...])
        g_ref[...] = g.astype(BF16)
        u_ref[...] = u.astype(BF16)
        a_ref[...] = (g * _sigmoid(g) * u).astype(BF16)

    w_spec = pl.BlockSpec((tn, D_MODEL), lambda j, i: (j, 0))
    out = pl.BlockSpec((tm, tn), lambda j, i: (i, j))
    return pl.pallas_call(
        body, name="ffn_up", grid=(D_FF // tn, t // tm),
        in_specs=[pl.BlockSpec((tm, D_MODEL), lambda j, i: (i, 0)), w_spec, w_spec],
        out_specs=[out, out, out],
        out_shape=[jax.ShapeDtypeStruct((t, D_FF), BF16)] * 3,
        compiler_params=_params("parallel", "parallel"),
    )(hn, wg_t, wu_t)


def _ffn_down_loss(a, h1, tgt, w_down, g3):
    t = a.shape[0]
    tm = min(512, t)

    def body(a_ref, h_ref, t_ref, w_ref, g_ref, dh_ref, dhb_ref, loss_ref, dg_ref):
        @pl.when(pl.program_id(0) == 0)
        def _():
            loss_ref[...] = jnp.zeros_like(loss_ref)
            dg_ref[...] = jnp.zeros_like(dg_ref)

        g = g_ref[...]
        h2 = h_ref[...] + _dot(a_ref[...], w_ref[...])
        r = lax.rsqrt(jnp.mean(h2 * h2, axis=-1, keepdims=True) + EPS)
        err = h2 * r * g - t_ref[...]
        loss_ref[...] += jnp.sum(err * err) * (0.5 / D_MODEL)
        dy = err * (1.0 / D_MODEL)
        dh, dg_rows = _rms_bwd(h2, g, dy)
        dg_ref[...] += jnp.sum(dg_rows, axis=0, keepdims=True)
        dh_ref[...] = dh
        dhb_ref[...] = dh.astype(BF16)

    row = pl.BlockSpec((tm, D_MODEL), lambda i: (i, 0))
    vec = pl.BlockSpec((1, D_MODEL), lambda i: (0, 0))
    return pl.pallas_call(
        body, name="ffn_down_loss", grid=(t // tm,),
        in_specs=[pl.BlockSpec((tm, D_FF), lambda i: (i, 0)), row, row,
                  pl.BlockSpec((D_FF, D_MODEL), lambda i: (0, 0)), vec],
        out_specs=[row, row, pl.BlockSpec((1, LANES), lambda i: (0, 0)), vec],
        out_shape=[jax.ShapeDtypeStruct((t, D_MODEL), F32), jax.ShapeDtypeStruct((t, D_MODEL), BF16),
                   jax.ShapeDtypeStruct((1, LANES), F32), jax.ShapeDtypeStruct((1, D_MODEL), F32)],
        compiler_params=_params("arbitrary"),
    )(a, h1, tgt, w_down, g3)


def _ffn_bwd_act(dh2b, w_down, g_act, u_act, exchange):
    t = dh2b.shape[0]
    tm, tn = min(512, t), D_FF // 2

    def body(d_ref, w_ref, g_ref, u_ref, dg_ref, du_ref):
        d = d_ref[...]
        for cols in _col_chunks(tn):
            da = _dot_nt(d, w_ref[cols, :])
            g = g_ref[:, cols].astype(F32)
            u = u_ref[:, cols].astype(F32)
            sg = _sigmoid(g)
            dg_ref[:, cols] = (da * u * sg * (1.0 + g * (1.0 - sg))).astype(BF16)
            du_ref[:, cols] = (da * g * sg).astype(BF16)

    blk = pl.BlockSpec((tm, tn), lambda j, i: (i, j))
    return _call(
        body, name="ffn_bwd_act", grid=(D_FF // tn, t // tm),
        in_specs=[pl.BlockSpec((tm, D_MODEL), lambda j, i: (i, 0)),
                  pl.BlockSpec((tn, D_MODEL), lambda j, i: (j, 0)), blk, blk],
        out_specs=[blk, blk],
        out_shape=[jax.ShapeDtypeStruct((t, D_FF), BF16)] * 2,
        semantics=("parallel", "parallel"), args=(dh2b, w_down, g_act, u_act), exchange=exchange)


def _ffn_bwd_in(dg, du, wg_t, wu_t, h1, dh2, g2, exchange):
    t = dg.shape[0]
    tm, tk = min(512, t), D_FF // 2
    nk = D_FF // tk

    def body(dg_ref, du_ref, wg_ref, wu_ref, h_ref, d2_ref, g_ref, dh_ref, dhb_ref, gn_ref, acc_ref):
        i, kk = pl.program_id(0), pl.program_id(1)

        @pl.when((i == 0) & (kk == 0))
        def _():
            gn_ref[...] = jnp.zeros_like(gn_ref)

        part = _dot(dg_ref[...], wg_ref[...]) + _dot(du_ref[...], wu_ref[...])

        @pl.when(kk == 0)
        def _():
            acc_ref[...] = part

        @pl.when((kk > 0) & (kk < nk - 1))
        def _():
            acc_ref[...] += part

        @pl.when(kk == nk - 1)
        def _():
            dx, dg_rows = _rms_bwd(h_ref[...], g_ref[...], acc_ref[...] + part)
            dh = d2_ref[...] + dx
            dh_ref[...] = dh
            dhb_ref[...] = dh.astype(BF16)
            gn_ref[...] += jnp.sum(dg_rows, axis=0, keepdims=True)

    act = pl.BlockSpec((tm, tk), lambda i, kk: (i, kk))
    wsp = pl.BlockSpec((tk, D_MODEL), lambda i, kk: (kk, 0))
    row = pl.BlockSpec((tm, D_MODEL), lambda i, kk: (i, 0))
    vec = pl.BlockSpec((1, D_MODEL), lambda i, kk: (0, 0))
    return _call(
        body, name="ffn_bwd_in", grid=(t // tm, nk),
        in_specs=[act, act, wsp, wsp, row, row, vec],
        out_specs=[row, row, vec],
        out_shape=[jax.ShapeDtypeStruct((t, D_MODEL), F32), jax.ShapeDtypeStruct((t, D_MODEL), BF16),
                   jax.ShapeDtypeStruct((1, D_MODEL), F32)],
        scratch=[pltpu.VMEM((tm, D_MODEL), F32)],
        semantics=("arbitrary", "arbitrary"), args=(dg, du, wg_t, wu_t, h1, dh2, g2), exchange=exchange)


def _mix_bwd(dh1b, w_out, proj, b_gate, y_ret, y_att, w_ret, w_att_t):
    t = dh1b.shape[0]
    tm = min(256, t)

    def body(d_ref, wo_ref, gl0, gl1, gl2, gl3, b_ref, yr_ref, ya_ref, wr_ref, wa_ref,
             dyr_ref, dya_ref, dglr_ref, dgla_ref, dgro_ref, dao_ref, db_ref):
        @pl.when(pl.program_id(0) == 0)
        def _():
            db_ref[...] = jnp.zeros_like(db_ref)

        dz = _dot_nt(d_ref[...], wo_ref[...])
        gr, ga = _gates((gl0, gl1, gl2, gl3), b_ref)
        dyr = (dz * gr).astype(BF16)
        dya = (dz * ga).astype(BF16)
        dyr_ref[...] = dyr
        dya_ref[...] = dya
        dglr = dz * yr_ref[...].astype(F32) * gr * (1.0 - gr)
        dgla = dz * ya_ref[...].astype(F32) * ga * (1.0 - ga)
        dglr_ref[...] = dglr.astype(BF16)
        dgla_ref[...] = dgla.astype(BF16)
        db_ref[:, :D_MODEL] += jnp.sum(dglr, axis=0, keepdims=True)
        db_ref[:, D_MODEL:] += jnp.sum(dgla, axis=0, keepdims=True)
        dgro_ref[...] = _dot_nt(dyr, wr_ref[...]).astype(BF16)
        dao_ref[...] = _dot(dya, wa_ref[...]).astype(BF16)

    row = pl.BlockSpec((tm, D_MODEL), lambda i: (i, 0))
    half = pl.BlockSpec((tm, 512), lambda i: (i, 0))
    return pl.pallas_call(
        body, name="mix_bwd", grid=(t // tm,),
        in_specs=[row, _whole(w_out), *_gl_specs(tm), _whole(b_gate), row, row, _whole(w_ret), _whole(w_att_t)],
        out_specs=[row, row, row, row, row, half, _whole(b_gate)],
        out_shape=[jax.ShapeDtypeStruct((t, D_MODEL), BF16)] * 5
                  + [jax.ShapeDtypeStruct((t, 512), BF16), jax.ShapeDtypeStruct(b_gate.shape, F32)],
        compiler_params=_params("arbitrary"),
    )(dh1b, w_out, proj, proj, proj, proj, b_gate, y_ret, y_att, w_ret, w_att_t)


def _ret_bwd(dgro, proj, o_ret, qr, kr, cs, sn, lg_arr, batch, seq, exchange):
    t = batch * seq
    nt = seq // RET_TILE

    def one_head(dgro_ref, rg_ref, o_ref, qr_ref, kr_ref, v_ref, cs_ref, sn_ref, lg_ref,
                 dq_ref, dk_ref, dv_ref, drg_ref, do_ref, st_ref):
        lg = lg_ref[:, 0:1]
        inside, q_dec, k_dec, tile_dec = _decay(lg)

        state = jnp.zeros((RET_KEY_DIM, RET_VAL_DIM), F32)
        for i in range(nt - 1):
            rows = slice(i * RET_TILE, (i + 1) * RET_TILE)
            state = state * tile_dec + _dot_tn(_scaled(kr_ref[rows, :], k_dec), v_ref[rows, :])
            st_ref[i + 1] = state.astype(BF16)

        for i in range(nt):
            rows = slice(i * RET_TILE, (i + 1) * RET_TILE)
            o = o_ref[rows, :]
            xc = o - jnp.mean(o, axis=-1, keepdims=True)
            rs = lax.rsqrt(jnp.mean(xc * xc, axis=-1, keepdims=True) + EPS)
            nrm = xc * rs
            rg = rg_ref[rows, :].astype(F32)
            sg = _sigmoid(rg)
            dg = dgro_ref[rows, :].astype(F32)
            drg_ref[rows, :] = (dg * nrm * sg * (1.0 + rg * (1.0 - sg))).astype(BF16)
            dn = dg * rg * sg
            do = rs * (dn - jnp.mean(dn, axis=-1, keepdims=True)
                       - nrm * jnp.mean(dn * nrm, axis=-1, keepdims=True))
            do_ref[rows, :] = do.astype(BF16)

        dstate = jnp.zeros((RET_KEY_DIM, RET_VAL_DIM), F32)
        for i in reversed(range(nt)):
            rows = slice(i * RET_TILE, (i + 1) * RET_TILE)
            qi, ki, vi, doi = qr_ref[rows, :], kr_ref[rows, :], v_ref[rows, :], do_ref[rows, :]
            p = (_dot_nt(qi, ki) * inside).astype(BF16)
            dp = (_dot_nt(doi, vi) * inside).astype(BF16)
            dq = _dot(dp, ki)
            dk = _dot_tn(dp, qi)
            dv = _dot_tn(p, doi)
            if i > 0:
                dq = dq + _dot_nt(doi, st_ref[i]) * q_dec
            if i < nt - 1:
                dsb = dstate.astype(BF16)
                dk = dk + _dot_nt(vi, dsb) * k_dec
                dv = dv + _dot(_scaled(ki, k_dec), dsb)
            if i > 0:
                dstate = dstate * tile_dec + _dot_tn(_scaled(qi, q_dec), doi)
            dq_ref[rows, :] = (dq * cs_ref[rows, :] - pltpu.roll(dq, 64, 1) * sn_ref[rows, :]).astype(BF16)
            dk = (dk * cs_ref[rows, :] - pltpu.roll(dk, 64, 1) * sn_ref[rows, :]) * (RET_KEY_DIM ** -0.5)
            dk_ref[rows, :] = dk.astype(BF16)
            dv_ref[rows, :] = dv.astype(BF16)

    key, val, tab, lgs = _ret_specs(seq)
    return _call(
        _per_head(one_head, "vvvkkvsshkkvvhh"), name="ret_bwd", grid=(batch, RET_HEADS // RET_GROUP),
        in_specs=[val(0), val(C_RG), val(0), key(0), key(0), val(C_RV), tab, tab, lgs],
        out_specs=[key(0), key(0), val(0), val(0)],
        out_shape=[jax.ShapeDtypeStruct((t, RET_HEADS * RET_KEY_DIM), BF16)] * 2
                  + [jax.ShapeDtypeStruct((t, RET_HEADS * RET_VAL_DIM), BF16)] * 2,
        scratch=[pltpu.VMEM((RET_GROUP, seq, RET_VAL_DIM), BF16),
                 pltpu.VMEM((RET_GROUP, nt, RET_KEY_DIM, RET_VAL_DIM), BF16)],
        semantics=("parallel", "parallel"), args=(dgro, proj, o_ret, qr, kr, proj, cs, sn, lg_arr),
        exchange=exchange)


def _att_bwd(proj, bias, dao, batch, seq, exchange):
    ni, q_spec, k_spec, v_spec, w_spec, b_spec, pad = _att_specs(batch, seq)
    t = batch * seq

    def body(q_ref, k_ref, v_ref, bias_ref, do_ref, dq_ref, dk_ref, dv_ref, dw_ref,
             dbias_ref, dk_acc, dv_acc, kp_ref, vp_ref, s_ref, dp_ref, e_ref, ds_ref):
        b, i = pl.program_id(1), pl.program_id(2)

        @pl.when((b == 0) & (i == 0))
        def _():
            dbias_ref[...] = jnp.zeros_like(dbias_ref)

        @pl.when(i == 0)
        def _():
            _att_pad(k_ref, kp_ref)
            _att_pad(v_ref, vp_ref)
            dk_acc[...] = jnp.zeros_like(dk_acc)
            dv_acc[...] = jnp.zeros_like(dv_acc)

        win = pl.ds(pl.multiple_of(i * ATT_Q, ATT_Q), ATT_WIN)
        lo = lax.broadcasted_iota(jnp.int32, (1, LANES), 1) < 64
        start = jnp.minimum(i, ATT_STARTS)
        for pair in range(ATT_PAIRS):
            cols = slice(pair * LANES, (pair + 1) * LANES)
            k2, v2, q2, do2 = kp_ref[win, cols], vp_ref[win, cols], q_ref[:, cols], do_ref[:, cols]
            dq = jnp.zeros((ATT_Q, LANES), F32)
            dk = jnp.zeros((LANES, ATT_WIN), F32)
            dv = jnp.zeros((LANES, ATT_WIN), F32)
            for e in range(2):
                h = 2 * pair + e
                sel = lo if e == 0 else jnp.logical_not(lo)
                qm = _att_head(q2, sel)
                dom = jnp.where(sel, do2, jnp.zeros_like(do2))
                s_ref[h] = _dot_nt(qm, k2)
                dp_ref[h] = _dot_nt(dom, v2)
                rsum = []
                for c in range(ATT_Q // ATT_ROWS):
                    rows = slice(c * ATT_ROWS, (c + 1) * ATT_ROWS)
                    ex, r = _att_softmax_rows(s_ref.at[h], bias_ref.at[pair, start, e], rows)
                    dp = dp_ref[h, rows, :]
                    mean = jnp.sum(dp * ex, axis=-1, keepdims=True) * r
                    ds = ex * ((dp - mean) * r)
                    dbias_ref[h, rows, :] += ds
                    ds_ref[h, rows, :] = ds.astype(BF16)
                    e_ref[h, rows, :] = ex.astype(BF16)
                    rsum.append(r)
                dq = dq + _dot(ds_ref[h], jnp.where(sel, k2, jnp.zeros_like(k2)))
                dk = dk + _dot_tn(qm, ds_ref[h])
                dv = dv + _dot_tn((dom.astype(F32) * jnp.concatenate(rsum, axis=0)).astype(BF16), e_ref[h])
            dq_ref[:, cols] = (dq * 0.125).astype(BF16)
            dk_acc[cols, win] += dk
            dv_acc[cols, win] += dv

        @pl.when(i == ni - 1)
        def _():
            dk_ref[...] = dk_acc[:, ATT_PAD:].T.astype(BF16)
            dv_ref[...] = dv_acc[:, ATT_PAD:].T.astype(BF16)

        @pl.when((b == batch - 1) & (i == ni - 1))
        def _():
            n_i = lax.broadcasted_iota(jnp.int32, (ATT_Q, BIAS_LEN), 0)
            for h in range(heads):
                xw = jnp.concatenate([jnp.zeros((ATT_Q, BIAS_LEN - ATT_WIN), F32), dbias_ref[h]], axis=1)
                for bit in range(8):
                    xw = jnp.where(((n_i >> bit) & 1) == 1, pltpu.roll(xw, BIAS_LEN - (1 << bit), 1), xw)
                dw_ref[h // 2, h % 2:h % 2 + 1, :] = jnp.sum(xw, axis=0, keepdims=True)

    heads = 2 * ATT_PAIRS
    seq_blk = pl.BlockSpec((seq, ATT_COLS), lambda g, b, i: (b, g))
    q_out = pl.BlockSpec((ATT_Q, ATT_COLS), lambda g, b, i: (b * ni + i, g))
    tile_f32, tile_bf16 = pltpu.VMEM((heads, ATT_Q, ATT_WIN), F32), pltpu.VMEM((heads, ATT_Q, ATT_WIN), BF16)
    acc = pltpu.VMEM((ATT_COLS, seq + ATT_PAD), F32)
    return _call(
        body, name="att_bwd", grid=(ATT_HEADS // heads, batch, ni),
        in_specs=[q_spec, k_spec, v_spec, b_spec, q_out],
        out_specs=[q_out, seq_blk, seq_blk, w_spec],
        out_shape=[jax.ShapeDtypeStruct((t, 512), BF16)] * 3
                  + [jax.ShapeDtypeStruct((ATT_HEADS // 2, 2, BIAS_LEN), F32)],
        scratch=[tile_f32, acc, acc, pad, pad, tile_f32, tile_f32, tile_bf16, tile_bf16],
        semantics=("arbitrary", "arbitrary", "arbitrary"), args=(proj, proj, proj, bias, dao), exchange=exchange,
        vmem=VMEM_LIMIT_ATT_BWD)


def _rms_in_bwd(x2, dxn, dh1, g1):
    t = x2.shape[0]
    tm = min(512, t)

    def body(x_ref, d_ref, h_ref, g_ref, dx_ref, dg_ref):
        @pl.when(pl.program_id(0) == 0)
        def _():
            dg_ref[...] = jnp.zeros_like(dg_ref)

        dx, dg_rows = _rms_bwd(x_ref[...], g_ref[...], d_ref[...])
        dx_ref[...] = h_ref[...] + dx
        dg_ref[...] += jnp.sum(dg_rows, axis=0, keepdims=True)

    row = pl.BlockSpec((tm, D_MODEL), lambda i: (i, 0))
    vec = pl.BlockSpec((1, D_MODEL), lambda i: (0, 0))
    return pl.pallas_call(
        body, name="rms_in_bwd", grid=(t // tm,),
        in_specs=[row, row, row, vec], out_specs=[row, vec],
        out_shape=[jax.ShapeDtypeStruct((t, D_MODEL), F32), jax.ShapeDtypeStruct((1, D_MODEL), F32)],
        compiler_params=_params("arbitrary"),
    )(x2, dxn, dh1, g1)


def _pack_small(dg1, dbr, dba, dg2, dg3, dw, loss):
    def body(a_ref, b_ref, c_ref, d_ref, e_ref, w_ref, l_ref, o_ref):
        o_ref[...] = jnp.zeros_like(o_ref)
        for r, ref in enumerate((a_ref, b_ref, c_ref, d_ref, e_ref)):
            o_ref[r:r + 1, :] = ref[...]
        o_ref[5:6, 0:LANES] = l_ref[...]
        for hp in range(ATT_HEADS // 2):
            o_ref[8 + 2 * hp:10 + 2 * hp, :] = w_ref[hp]

    return pl.pallas_call(body, name="pack_small",
                          out_shape=jax.ShapeDtypeStruct((16, D_MODEL), F32))(dg1, dbr, dba, dg2, dg3, dw, loss)


def _rotary_tables(seq):
    freqs = ROPE_BASE ** (-jnp.arange(0, RET_KEY_DIM, 2, dtype=F32) / RET_KEY_DIM)
    ang = jnp.arange(seq, dtype=F32)[:, None] * freqs[None, :]
    cos, sin = jnp.cos(ang), jnp.sin(ang)
    return jnp.concatenate([cos, cos], axis=1), jnp.concatenate([-sin, sin], axis=1)


def _bias_rows(rel_bias):
    n_far = BIAS_LEN - ATT_Q - MAX_REL + 1
    n_near = BIAS_LEN - n_far - (N_REL - 2)
    w = jnp.concatenate([jnp.broadcast_to(rel_bias[:, N_REL - 1:], (ATT_HEADS, n_far)),
                         rel_bias[:, 1:N_REL - 1][:, ::-1],
                         jnp.broadcast_to(rel_bias[:, :1], (ATT_HEADS, n_near))], axis=1)
    return w.reshape(ATT_HEADS // 2, 2, BIAS_LEN)


def _bias_rows_bwd(dw):
    n_far = BIAS_LEN - ATT_Q - MAX_REL + 1
    mid = dw[:, n_far:n_far + N_REL - 2][:, ::-1]
    return jnp.concatenate([jnp.sum(dw[:, n_far + N_REL - 2:], axis=1, keepdims=True), mid,
                            jnp.sum(dw[:, :n_far], axis=1, keepdims=True)], axis=1)


def _step(x, tgt, norm_mix, b_gate, norm_ffn, norm_final, rel_bias_shard, shard):
    batch, seq, _ = x.shape
    t = batch * seq
    n_rb = rel_bias_shard.shape[-1]
    x2, tgt2 = x.reshape(t, D_MODEL), tgt.reshape(t, D_MODEL)
    g3 = norm_final.reshape(1, D_MODEL)
    cs, sn = _rotary_tables(seq)
    lg = np.log(1.0 - 2.0 ** (-5.0 - np.arange(RET_HEADS, dtype=np.float32))).astype(np.float32)
    lg_arr = jnp.asarray(np.broadcast_to(lg[:, None, None], (RET_HEADS, 1, LANES)))

    def gather(*names):
        return _ChipGather([shard[nm] for nm in names])

    def scatter(*grads):
        return _Exchange(grads, scatter=True)

    rb_pad = jnp.pad(rel_bias_shard, ((0, 0), (0, LANES - n_rb)))
    (xn,), (w_in_half, rb_full) = _rms_fwd(x2, norm_mix,
                                           _ChipGather([shard["w_in_t"], rb_pad], parts=[(0, 2), (0, 1)]))
    rb_full = rb_full.reshape(N_DEV, ATT_HEADS, LANES)[:, :, :n_rb]
    bias, (w_in_t,) = _att_bias_tiles(
        _bias_rows(jnp.transpose(rb_full, (1, 0, 2)).reshape(ATT_HEADS, N_DEV * n_rb)),
        _ChipGather([shard["w_in_t"]], parts=[(1, 2)], into=[w_in_half]))
    proj, (w_ret, w_att_t, w_out, w_gate_t) = _mm(
        xn, w_in_t, tb=True, out_dtype=BF16, tm=1024, tn=1664, tk=1024, name="proj",
        exchange=gather("w_ret", "w_att_t", "w_out", "w_gate_t"))
    (gro, o_ret, qr, kr), _ = _ret_fwd(proj, cs, sn, lg_arr, batch, seq, None)
    (ao,), (w_up_t, w_down) = _att_fwd(proj, bias, batch, seq, gather("w_up_t", "w_down"))
    z, y_ret, y_att, h1, hn = _mix_out_fwd(gro, ao, proj, b_gate, w_ret, w_att_t, x2, w_out, norm_ffn)
    g_act, u_act, a_act = _ffn_up(hn, w_gate_t, w_up_t)
    dh2, dh2b, loss, dg3 = _ffn_down_loss(a_act, h1, tgt2, w_down, g3)

    wg = dict(out_dtype=BF16, tn=1024, ta=True)
    slots = {}
    dw_down = _mm(a_act, dh2b, tm=1408, tk=1024, name="dw_down", **wg)
    (d_gact, d_uact), _ = _ffn_bwd_act(dh2b, w_down, g_act, u_act, None)
    dw_gate = _mm(d_gact, hn, tm=1408, tk=1024, name="dw_gate", **wg)
    dw_up = _mm(d_uact, hn, tm=1408, tk=1024, name="dw_up", **wg)
    (dh1, dh1b, dg2), (slots["w_down"],) = _ffn_bwd_in(d_gact, d_uact, w_gate_t, w_up_t, h1, dh2, norm_ffn,
                                                     scatter(dw_down))
    dw_out = _mm(z, dh1b, tm=1024, tk=2048, name="dw_out", **wg)
    dyr, dya, dglr, dgla, dgro, dao, db = _mix_bwd(dh1b, w_out, proj, b_gate, y_ret, y_att, w_ret, w_att_t)
    dw_ret = _mm(gro, dyr, tm=1024, tk=2048, name="dw_ret", **wg)
    dw_att = _mm(dya, ao, tm=1024, tk=2048, name="dw_att", **wg)
    (drq, drk, drv, drg), _ = _ret_bwd(dgro, proj, o_ret, qr, kr, cs, sn, lg_arr, batch, seq, None)
    (daq, dak, dav, dw), (slots["w_gate_t"], slots["w_out"], slots["w_ret"], slots["w_att_t"]) = _att_bwd(
        proj, bias, dao, batch, seq, scatter(dw_gate, dw_out, dw_ret, dw_att))
    dproj = [drq, drk, drv, drg, daq, dak, dav, dglr, dgla]
    dw_in, (slots["w_up_t"],) = _mm_pieces(dproj, xn, ta=True, out_dtype=BF16, tm=512, tn=1024, tk=1024,
                                           name="dw_in", exchange=scatter(dw_up))
    (dw_in_sibling,) = _alone(_PairSwap([dw_in]), "swap_w_in")
    dw_in_pairs = _pair_add(dw_in, dw_in_sibling, "pair_w_in")
    dxn, (slots["w_in_t"],) = _mm_pieces(dproj, w_in_t, ta=False, out_dtype=F32, tm=1024, tn=1024, tk=512, name="dxn",
                                  exchange=_ChipScatter([dw_in_pairs]))
    dx, dg1 = _rms_in_bwd(x2, dxn, dh1, norm_mix)
    small = _pack_small(dg1, db[:, :D_MODEL], db[:, D_MODEL:], dg2, dg3, dw, loss)
    (small_slots,) = _alone(_ChipGather([small]), "gather_small")
    return dx.reshape(batch, seq, D_MODEL), slots, small_slots.reshape(N_DEV, 16, D_MODEL)


def _row_tile(r, c):
    return max(d for d in range(16, r + 1, 16) if r % d == 0 and (d * c <= 256 * 1024 or d == 16))


def _pair_add(grad, got, name):
    _, r, c = got.shape
    tr = r
    core = lax.axis_index("c").astype(jnp.int32).reshape(1)

    def body(core_ref, g_ref, a_ref, o_ref):
        o_ref[...] = (g_ref[...].astype(F32) + a_ref[...].astype(F32)).astype(o_ref.dtype)

    blk = pl.BlockSpec((None, tr, c), lambda q, i, core_ref: (q, i, 0))
    return pl.pallas_call(
        body, name=name,
        grid_spec=pltpu.PrefetchScalarGridSpec(
            num_scalar_prefetch=1, grid=(4, r // tr),
            in_specs=[pl.BlockSpec((None, None, tr, c), lambda q, i, core_ref: (q, core_ref[0], i, 0)), blk],
            out_specs=blk),
        out_shape=jax.ShapeDtypeStruct(got.shape, got.dtype),
        compiler_params=_params("parallel", "parallel"),
    )(core, grad.reshape(4, 2, r, c), got)


def _sum_slots(slots, name):
    n, r, c = slots.shape
    tr = _row_tile(r, c)

    def body(s_ref, o_ref):
        acc = s_ref[0].astype(F32)
        for s in range(1, n):
            acc = acc + s_ref[s].astype(F32)
        o_ref[...] = acc

    return pl.pallas_call(
        body, name=name, grid=(r // tr,),
        in_specs=[pl.BlockSpec((n, tr, c), lambda i: (0, i, 0))],
        out_specs=pl.BlockSpec((tr, c), lambda i: (i, 0)),
        out_shape=jax.ShapeDtypeStruct((r, c), F32),
        compiler_params=_params("parallel"),
    )(slots)


def _adamw_math(w, g, m, v):
    m = ADAM_B1 * m + (1.0 - ADAM_B1) * g
    v = ADAM_B2 * v + (1.0 - ADAM_B2) * (g * g)
    m_hat = m / (1.0 - ADAM_B1 ** ADAM_STEP)
    v_hat = v / (1.0 - ADAM_B2 ** ADAM_STEP)
    return -ADAM_LR * (m_hat / (jnp.sqrt(v_hat) + ADAM_EPS) + ADAM_WD * w), m, v


def _adamw(w, slots, m, v, name):
    n, r, c = slots.shape
    tr = _row_tile(r, c)

    def body(w_ref, s_ref, m_ref, v_ref, g_ref, d_ref, nm_ref, nv_ref):
        g = s_ref[0].astype(F32)
        for s in range(1, n):
            g = g + s_ref[s].astype(F32)
        g_ref[...] = g
        d_ref[...], nm_ref[...], nv_ref[...] = _adamw_math(w_ref[...], g, m_ref[...], v_ref[...])

    blk = pl.BlockSpec((tr, c), lambda i: (i, 0))
    return pl.pallas_call(
        body, name=name, grid=(r // tr,),
        in_specs=[blk, pl.BlockSpec((n, tr, c), lambda i: (0, i, 0)), blk, blk], out_specs=[blk] * 4,
        out_shape=[jax.ShapeDtypeStruct((r, c), F32)] * 4,
        compiler_params=_params("parallel"),
    )(w, slots, m, v)


def _adamw_small(ws, gs, ms, vs):
    n = len(ws)

    def body(*refs):
        for i in range(n):
            w_ref, g_ref, m_ref, v_ref = (refs[j * n + i] for j in range(4))
            d_ref, nm_ref, nv_ref = (refs[(4 + j) * n + i] for j in range(3))
            d_ref[...], nm_ref[...], nv_ref[...] = _adamw_math(w_ref[...], g_ref[...], m_ref[...], v_ref[...])

    shapes = [jax.ShapeDtypeStruct(w.shape, F32) for w in ws]
    outs = pl.pallas_call(body, name="adamw_small", out_shape=shapes * 3)(*ws, *gs, *ms, *vs)
    return outs[:n], outs[n:2 * n], outs[2 * n:]


def kernel(x, norm_mix, w_in, b_gate, rel_bias, w_ret_out, w_att_out, w_out, norm_ffn, w_ffn_gate, w_ffn_up, w_ffn_down, norm_final, loss_target, m_norm_mix, m_w_in, m_b_gate, m_rel_bias, m_w_ret_out, m_w_att_out, m_w_out, m_norm_ffn, m_w_ffn_gate, m_w_ffn_up, m_w_ffn_down, m_norm_final, v_norm_mix, v_w_in, v_b_gate, v_rel_bias, v_w_ret_out, v_w_att_out, v_w_out, v_norm_ffn, v_w_ffn_gate, v_w_ffn_up, v_w_ffn_down, v_norm_final):
    me = _index(_place())
    n_rb = rel_bias.shape[-1]

    shard = dict(w_in_t=w_in[0].T, w_gate_t=w_ffn_gate[0].T, w_up_t=w_ffn_up[0].T, w_down=w_ffn_down[0],
                 w_ret=w_ret_out[0], w_out=w_out[0], w_att_t=w_att_out[0].T)
    shard = {nm: s.astype(BF16) for nm, s in shard.items()}
    dx, slots, small_slots = _step(x, loss_target, norm_mix, b_gate, norm_ffn, norm_final, rel_bias[0], shard)
    small_sum = _sum_slots(small_slots, "sum_small")
    loss = small_sum[5, 0]

    transposed = dict(w_in="w_in_t", w_ffn_gate="w_gate_t", w_ffn_up="w_up_t", w_att_out="w_att_t")
    plain = dict(w_ffn_down="w_down", w_ret_out="w_ret", w_out="w_out")
    g = dict(
        norm_mix=small_sum[0:1], b_gate=jnp.concatenate([small_sum[1:2], small_sum[2:3]], axis=1),
        norm_ffn=small_sum[3:4], norm_final=small_sum[4:5],
        rel_bias=lax.dynamic_slice_in_dim(_bias_rows_bwd(small_sum[8:16]), me * n_rb, n_rb, axis=1),
    )
    w = dict(norm_mix=norm_mix, w_in=w_in, b_gate=b_gate, rel_bias=rel_bias, w_ret_out=w_ret_out, w_att_out=w_att_out,
             w_out=w_out, norm_ffn=norm_ffn, w_ffn_gate=w_ffn_gate, w_ffn_up=w_ffn_up, w_ffn_down=w_ffn_down,
             norm_final=norm_final)
    m = dict(norm_mix=m_norm_mix, w_in=m_w_in, b_gate=m_b_gate, rel_bias=m_rel_bias, w_ret_out=m_w_ret_out,
             w_att_out=m_w_att_out, w_out=m_w_out, norm_ffn=m_norm_ffn, w_ffn_gate=m_w_ffn_gate, w_ffn_up=m_w_ffn_up,
             w_ffn_down=m_w_ffn_down, norm_final=m_norm_final)
    v = dict(norm_mix=v_norm_mix, w_in=v_w_in, b_gate=v_b_gate, rel_bias=v_rel_bias, w_ret_out=v_w_ret_out,
             w_att_out=v_w_att_out, w_out=v_w_out, norm_ffn=v_norm_ffn, w_ffn_gate=v_w_ffn_gate, w_ffn_up=v_w_ffn_up,
             w_ffn_down=v_w_ffn_down, norm_final=v_norm_final)
    order = ("norm_mix", "w_in", "b_gate", "rel_bias", "w_ret_out", "w_att_out", "w_out", "norm_ffn",
             "w_ffn_gate", "w_ffn_up", "w_ffn_down", "norm_final")
    small_names = ("norm_mix", "b_gate", "rel_bias", "norm_ffn", "norm_final")

    def flat(a):
        return a[0] if a.ndim == 3 else a.reshape(-1, a.shape[-1])

    grad, delta, new_m, new_v = {}, {}, {}, {}
    for nm in order:
        if nm in transposed:
            res = _adamw(w[nm][0].T, slots[transposed[nm]], m[nm][0].T, v[nm][0].T, "adamw_" + nm)
            grad[nm], delta[nm], new_m[nm], new_v[nm] = (a.T[None] for a in res)
        elif nm in plain:
            res = _adamw(flat(w[nm]), slots[plain[nm]], flat(m[nm]), flat(v[nm]), "adamw_" + nm)
            grad[nm], delta[nm], new_m[nm], new_v[nm] = (a.reshape(w[nm].shape) for a in res)
    ds, nms, nvs = _adamw_small([flat(w[nm]) for nm in small_names], [g[nm] for nm in small_names],
                                [flat(m[nm]) for nm in small_names], [flat(v[nm]) for nm in small_names])
    for i, nm in enumerate(small_names):
        grad[nm], delta[nm], new_m[nm], new_v[nm] = (a.reshape(w[nm].shape) for a in (g[nm], ds[i], nms[i], nvs[i]))

    return (loss, dx, *[grad[nm] for nm in order], *[delta[nm] for nm in order],
            *[new_m[nm] for nm in order], *[new_v[nm] for nm in order])
```

```python
import numpy as np
import jax
import jax.numpy as jnp
from jax import lax
from jax.experimental import pallas as pl
from jax.experimental.pallas import tpu as pltpu

F32 = jnp.float32
BF16 = jnp.bfloat16
MESH = pl.DeviceIdType.MESH

D_MODEL = 1024
CHUNK = 64
RET_HEADS = 4
RET_KEY_DIM = 128
RET_VAL_DIM = 256
ATT_HEADS = 8
BAND_CHUNKS = 8
MAX_REL = 256
N_REL = CHUNK + MAX_REL
D_FF = 2816
N_IN = 6656
ROPE_BASE = 10000.0
EPS = 1e-6
NEG_INF = -1e30
C_RQ, C_RK, C_RV, C_RG, C_AQ, C_AK, C_AV, C_GL = 0, 512, 1024, 2048, 3072, 3584, 4096, 4608

ADAM_LR = 0.001
ADAM_B1 = 0.9
ADAM_B2 = 0.999
ADAM_EPS = 1e-08
ADAM_WD = 0.01
ADAM_STEP = 10

N_DEV = 8
LANES = 128
RET_TILE = 256
ATT_Q = 256
ATT_PAD = BAND_CHUNKS * CHUNK
ATT_WIN = ATT_PAD + ATT_Q
ATT_STARTS = ATT_PAD // ATT_Q
ATT_ROWS = 32
BIAS_LEN = 1024
VMEM_LIMIT = 48 * 1024 * 1024
VMEM_LIMIT_ATT_BWD = 56 * 1024 * 1024


def _params(*sem, vmem=VMEM_LIMIT):
    return pltpu.CompilerParams(dimension_semantics=sem, vmem_limit_bytes=vmem)


def _dot(a, b):
    return lax.dot_general(a, b, (((1,), (0,)), ((), ())), preferred_element_type=F32)


def _dot_nt(a, b):
    return lax.dot_general(a, b, (((1,), (1,)), ((), ())), preferred_element_type=F32)


def _dot_tn(a, b):
    return lax.dot_general(a, b, (((0,), (0,)), ((), ())), preferred_element_type=F32)


def _sigmoid(x):
    return 1.0 / (1.0 + jnp.exp(-x))


def _rms_bwd(x, g, dy):
    r = lax.rsqrt(jnp.mean(x * x, axis=-1, keepdims=True) + EPS)
    u = dy * g
    dx = r * u - x * (r * r * r) * jnp.mean(u * x, axis=-1, keepdims=True)
    return dx, dy * x * r


def _place():
    return lax.axis_index("x"), lax.axis_index("y"), lax.axis_index("c")


def _peer(k):
    x, y, c = _place()
    return ((1 - x) if k & 4 else x, (1 - y) if k & 2 else y, (1 - c) if k & 1 else c)


def _index(place):
    return 4 * place[0] + 2 * place[1] + place[2]


def _rows(ref, block, nrows):
    align = 16 if ref.dtype == BF16 else 8
    return ref.at[pl.ds(pl.multiple_of(block * nrows, align), nrows)]


class _Exchange:
    def __init__(self, arrays, scatter):
        self.arrays, self.scatter, self.n = list(arrays), scatter, len(arrays)

    def out_shape(self):
        if self.scatter:
            return [jax.ShapeDtypeStruct((N_DEV, a.shape[0] // N_DEV) + a.shape[1:], a.dtype) for a in self.arrays]
        return [jax.ShapeDtypeStruct((N_DEV * a.shape[0],) + a.shape[1:], a.dtype) for a in self.arrays]

    def scratch(self):
        return [pltpu.SemaphoreType.DMA((self.n, N_DEV - 1)), pltpu.SemaphoreType.DMA((self.n, N_DEV - 1)),
                pltpu.SemaphoreType.DMA((self.n,))]

    def _copies(self, ins, outs, sems):
        send_sems, recv_sems, local_sems = sems
        me = _index(_place())

        def src(w, to):
            return _rows(ins[w], to, ins[w].shape[0] // N_DEV) if self.scatter else ins[w]

        def dst(w, origin):
            return outs[w].at[origin] if self.scatter else _rows(outs[w], origin, ins[w].shape[0])

        def remote(w, k, to, origin):
            return pltpu.make_async_remote_copy(src_ref=src(w, to), dst_ref=dst(w, origin),
                                                send_sem=send_sems.at[w, k - 1], recv_sem=recv_sems.at[w, k - 1],
                                                device_id=_peer(k), device_id_type=MESH)

        pairs = [(w, k) for w in range(self.n) for k in range(1, N_DEV)]
        own = lambda: [pltpu.make_async_copy(src(w, me), dst(w, me), local_sems.at[w]) for w in range(self.n)]
        sent = lambda: [remote(w, k, _index(_peer(k)), me) for w, k in pairs]
        arriving = lambda: [remote(w, k, me, _index(_peer(k))) for w, k in pairs]
        return own, sent, arriving

    def start(self, ins, outs, sems):
        own, sent, _ = self._copies(ins, outs, sems)
        for cp in own() + sent():
            cp.start()

    def wait(self, ins, outs, sems):
        own, sent, arriving = self._copies(ins, outs, sems)
        for cp in arriving():
            cp.wait_recv()
        for cp in sent():
            cp.wait_send()
        for cp in own():
            cp.wait()


class _PairSwap:
    def __init__(self, arrays):
        self.arrays, self.n = list(arrays), len(arrays)

    def out_shape(self):
        return [jax.ShapeDtypeStruct((4, a.shape[0] // N_DEV) + a.shape[1:], a.dtype) for a in self.arrays]

    def scratch(self):
        return [pltpu.SemaphoreType.DMA((self.n, 4)), pltpu.SemaphoreType.DMA((self.n, 4))]

    def _copies(self, ins, outs, sems):
        send_sems, recv_sems = sems
        x, y, c = _place()
        return [pltpu.make_async_remote_copy(
            src_ref=_rows(ins[w], 2 * q + 1 - c, ins[w].shape[0] // N_DEV), dst_ref=outs[w].at[q],
            send_sem=send_sems.at[w, q], recv_sem=recv_sems.at[w, q],
            device_id=(x, y, 1 - c), device_id_type=MESH) for w in range(self.n) for q in range(4)]

    def start(self, ins, outs, sems):
        for cp in self._copies(ins, outs, sems):
            cp.start()

    def wait(self, ins, outs, sems):
        for cp in self._copies(ins, outs, sems):
            cp.wait()


class _ChipScatter:
    def __init__(self, arrays):
        self.arrays, self.n = list(arrays), len(arrays)

    def out_shape(self):
        return [jax.ShapeDtypeStruct(a.shape, a.dtype) for a in self.arrays]

    def scratch(self):
        return [pltpu.SemaphoreType.DMA((self.n, 3)), pltpu.SemaphoreType.DMA((self.n, 3)),
                pltpu.SemaphoreType.DMA((self.n,))]

    def _copies(self, ins, outs, sems):
        send_sems, recv_sems, local_sems = sems
        x, y, c = _place()
        mine = 2 * x + y
        sent, arriving = [], []
        for w in range(self.n):
            for k in range(1, 4):
                tx, ty = (1 - x) if k & 2 else x, (1 - y) if k & 1 else y
                other = 2 * tx + ty
                sent.append(lambda w=w, k=k, tx=tx, ty=ty, other=other: pltpu.make_async_remote_copy(
                    src_ref=ins[w].at[other], dst_ref=outs[w].at[mine], send_sem=send_sems.at[w, k - 1],
                    recv_sem=recv_sems.at[w, k - 1], device_id=(tx, ty, c), device_id_type=MESH))
                arriving.append(lambda w=w, k=k, tx=tx, ty=ty, other=other: pltpu.make_async_remote_copy(
                    src_ref=ins[w].at[mine], dst_ref=outs[w].at[other], send_sem=send_sems.at[w, k - 1],
                    recv_sem=recv_sems.at[w, k - 1], device_id=(tx, ty, c), device_id_type=MESH))
        own = [lambda w=w: pltpu.make_async_copy(ins[w].at[mine], outs[w].at[mine], local_sems.at[w])
               for w in range(self.n)]
        return own, sent, arriving

    def start(self, ins, outs, sems):
        own, sent, _ = self._copies(ins, outs, sems)
        for cp in own + sent:
            cp().start()

    def wait(self, ins, outs, sems):
        own, sent, arriving = self._copies(ins, outs, sems)
        for cp in arriving:
            cp().wait_recv()
        for cp in sent:
            cp().wait_send()
        for cp in own:
            cp().wait()


class _ChipGather:
    def __init__(self, arrays, parts=None, into=None):
        self.arrays, self.n, self.into = list(arrays), len(arrays), into
        self.parts = parts or [(0, 1)] * self.n

    def out_shape(self):
        return [jax.ShapeDtypeStruct((N_DEV * a.shape[0],) + a.shape[1:], a.dtype) for a in self.arrays]

    def scratch(self):
        return [pltpu.SemaphoreType.DMA((self.n, N_DEV - 1)), pltpu.SemaphoreType.DMA((self.n, N_DEV - 1)),
                pltpu.SemaphoreType.DMA((self.n,))]

    def _parts(self, ins, outs, sems):
        send_sems, recv_sems, local_sems = sems
        x, y, c = _place()
        me, sibling = (x, y, c), (x, y, 1 - c)
        chips = [(1 - x, y), (x, 1 - y), (1 - x, 1 - y)]

        def rows(w, place, whole):
            (index, count), r = self.parts[w], ins[w].shape[0]
            lo, size = (0, r) if whole else (index * (r // count), r // count)
            align = 16 if ins[w].dtype == BF16 else 8
            return outs[w].at[pl.ds(pl.multiple_of(_index(place) * r + lo, align), size)]

        def mine(w, whole):
            (index, count), r = self.parts[w], ins[w].shape[0]
            return ins[w] if whole or count == 1 else ins[w].at[pl.ds(index * (r // count), r // count)]

        def copy(w, k, block, to, own=False):
            whole = k == 0
            return pltpu.make_async_remote_copy(src_ref=mine(w, whole) if own else rows(w, block, whole),
                                                dst_ref=rows(w, block, whole),
                                                send_sem=send_sems.at[w, k], recv_sem=recv_sems.at[w, k],
                                                device_id=to, device_id_type=MESH)

        def local(w):
            return pltpu.make_async_copy(ins[w], rows(w, me, True), local_sems.at[w])

        return me, sibling, chips, c, copy, local, [index == 0 for index, _ in self.parts]

    def start(self, ins, outs, sems):
        me, sibling, chips, c, copy, local, places_own = self._parts(ins, outs, sems)
        for w in range(self.n):
            if places_own[w]:
                local(w).start()
                copy(w, 0, me, sibling, own=True).start()
            for j, chip in enumerate(chips):
                copy(w, 1 + j, me, (*chip, c), own=True).start()

    def wait(self, ins, outs, sems):
        me, sibling, chips, c, copy, local, places_own = self._parts(ins, outs, sems)
        for w in range(self.n):
            for j, chip in enumerate(chips):
                copy(w, 1 + j, (*chip, c), me).wait_recv()
                copy(w, 4 + j, (*chip, c), sibling).start()
        for w in range(self.n):
            if places_own[w]:
                copy(w, 0, sibling, me).wait_recv()
                copy(w, 0, me, sibling, own=True).wait_send()
                local(w).wait()
            for j, chip in enumerate(chips):
                copy(w, 4 + j, (*chip, 1 - c), me).wait_recv()
                copy(w, 1 + j, me, (*chip, c), own=True).wait_send()
                copy(w, 4 + j, (*chip, c), sibling).wait_send()


def _call(body, *, name, grid, in_specs, out_specs, out_shape, scratch=(), semantics, args, exchange=None,
          vmem=VMEM_LIMIT):
    if exchange is None:
        return pl.pallas_call(body, name=name, grid=grid, in_specs=in_specs, out_specs=out_specs, out_shape=out_shape,
                              scratch_shapes=list(scratch),
                              compiler_params=_params(*semantics, vmem=vmem))(*args), None
    n_in, n_out, n_scr, nx = len(in_specs), len(out_specs), len(scratch), exchange.n
    into = list(getattr(exchange, "into", None) or [])

    def full_body(*refs):
        ins, refs = refs[:n_in], refs[n_in:]
        x_in, refs = refs[:nx], refs[nx + len(into):]
        outs, refs = refs[:n_out], refs[n_out:]
        x_out, refs = refs[:nx], refs[nx:]
        scr, sems = refs[:n_scr], refs[n_scr:]
        first, last = True, True
        for axis, size in enumerate(grid):
            first = jnp.logical_and(first, pl.program_id(axis) == 0)
            last = jnp.logical_and(last, pl.program_id(axis) == size - 1)
        if grid:
            pl.when(first)(lambda: exchange.start(x_in, x_out, sems))
        else:
            exchange.start(x_in, x_out, sems)
        body(*ins, *outs, *scr)
        if grid:
            pl.when(last)(lambda: exchange.wait(x_in, x_out, sems))
        else:
            exchange.wait(x_in, x_out, sems)

    hbm = pl.BlockSpec(memory_space=pltpu.HBM)
    res = pl.pallas_call(
        full_body, name=name, grid=grid,
        in_specs=list(in_specs) + [hbm] * (nx + len(into)), out_specs=list(out_specs) + [hbm] * nx,
        out_shape=list(out_shape) + exchange.out_shape(),
        scratch_shapes=list(scratch) + exchange.scratch(),
        input_output_aliases={n_in + nx + w: n_out + w for w in range(len(into))},
        compiler_params=_params(*(["arbitrary"] * len(grid)), vmem=vmem),
    )(*args, *exchange.arrays, *into)
    return res[:n_out], res[n_out:]


def _alone(exchange, name):
    return _call(lambda: None, name=name, grid=(), in_specs=[], out_specs=[], out_shape=[], semantics=(),
                 args=(), exchange=exchange)[1]


def _mm(a, b, *, ta=False, tb=False, out_dtype, tm, tn, tk, name, exchange=None):
    m, k = (a.shape[1], a.shape[0]) if ta else a.shape
    n = b.shape[0] if tb else b.shape[1]
    assert k == (b.shape[1] if tb else b.shape[0])
    tm, tn, tk = min(tm, m), min(tn, n), min(tk, k)
    assert m % tm == 0 and n % tn == 0 and k % tk == 0, (name, m, n, k)
    nk = k // tk
    dims = (((0 if ta else 1,), (1 if tb else 0,)), ((), ()))

    def body(a_ref, b_ref, o_ref, *acc):
        prod = lax.dot_general(a_ref[...].astype(BF16), b_ref[...].astype(BF16), dims, preferred_element_type=F32)
        if nk == 1:
            o_ref[...] = prod.astype(o_ref.dtype)
            return
        acc_ref, kk = acc[0], pl.program_id(2)

        @pl.when(kk == 0)
        def _():
            acc_ref[...] = prod

        @pl.when((kk > 0) & (kk < nk - 1))
        def _():
            acc_ref[...] += prod

        @pl.when(kk == nk - 1)
        def _():
            o_ref[...] = (acc_ref[...] + prod).astype(o_ref.dtype)

    a_spec = (pl.BlockSpec((tk, tm), lambda i, j, kk: (kk, i)) if ta
              else pl.BlockSpec((tm, tk), lambda i, j, kk: (i, kk)))
    b_spec = (pl.BlockSpec((tn, tk), lambda i, j, kk: (j, kk)) if tb
              else pl.BlockSpec((tk, tn), lambda i, j, kk: (kk, j)))
    (out,), moved = _call(
        body, name=name, grid=(m // tm, n // tn, nk),
        in_specs=[a_spec, b_spec],
        out_specs=[pl.BlockSpec((tm, tn), lambda i, j, kk: (i, j))],
        out_shape=[pltpu.HBM((m, n), out_dtype)],
        scratch=[pltpu.VMEM((tm, tn), F32)] if nk > 1 else [],
        semantics=("parallel", "parallel", "arbitrary"), args=(a, b), exchange=exchange)
    return out if exchange is None else (out, moved)


def _mm_pieces(pieces, b, *, ta, out_dtype, tm, tn, tk, name, exchange=None):
    rows, n = pieces[0].shape[0], b.shape[1]
    step = tm if ta else tk
    assert all(p.shape[0] == rows and p.shape[1] % step == 0 for p in pieces), name
    edges = [int(e) for e in np.cumsum([0] + [p.shape[1] // step for p in pieces])]
    total = edges[-1] * step
    m, k = (total, rows) if ta else (rows, total)
    assert b.shape[0] == k and m % tm == 0 and n % tn == 0 and k % tk == 0, name
    nk, npieces = k // tk, len(pieces)
    dims = (((0 if ta else 1,), (0,)), ((), ()))
    b_resident = ta and n == tn

    def body(*refs):
        a_refs, (b_ref, o_ref, acc_ref) = refs[:npieces], refs[npieces:]
        kk = pl.program_id(2)
        pos = pl.program_id(0) if ta else kk

        @pl.when(kk == 0)
        def _():
            acc_ref[...] = jnp.zeros_like(acc_ref)

        def b_tile():
            return b_ref[pl.ds(pl.multiple_of(kk * tk, tk), tk), :] if b_resident else b_ref[...]

        for p, a_ref in enumerate(a_refs):
            @pl.when((pos >= edges[p]) & (pos < edges[p + 1]))
            def _(a_ref=a_ref):
                acc_ref[...] += lax.dot_general(a_ref[...], b_tile(), dims, preferred_element_type=F32)

        @pl.when(kk == nk - 1)
        def _():
            o_ref[...] = acc_ref[...].astype(o_ref.dtype)

    def a_spec(p):
        lo, last = edges[p], edges[p + 1] - edges[p] - 1
        if ta:
            def index(i, j, kk):
                inside = (i >= lo) & (i <= lo + last)
                return jnp.where(inside, kk, 0), jnp.clip(i - lo, 0, last)
            return pl.BlockSpec((tk, tm), index)
        return pl.BlockSpec((tm, tk), lambda i, j, kk: (i, jnp.clip(kk - lo, 0, last)))

    (out,), moved = _call(
        body, name=name, grid=(m // tm, n // tn, nk),
        in_specs=[a_spec(p) for p in range(npieces)]
                 + [pl.BlockSpec(b.shape, lambda i, j, kk: (0, 0)) if b_resident
                    else pl.BlockSpec((tk, tn), lambda i, j, kk: (kk, j))],
        out_specs=[pl.BlockSpec((tm, tn), lambda i, j, kk: (i, j))],
        out_shape=[pltpu.HBM((m, n), out_dtype)],
        scratch=[pltpu.VMEM((tm, tn), F32)],
        semantics=("parallel", "parallel", "arbitrary"), args=(*pieces, b), exchange=exchange)
    return out if exchange is None else (out, moved)


def _rms_fwd(x2, g, exchange):
    t = x2.shape[0]
    tm = min(512, t)

    def body(x_ref, g_ref, o_ref):
        x = x_ref[...]
        r = lax.rsqrt(jnp.mean(x * x, axis=-1, keepdims=True) + EPS)
        o_ref[...] = (x * r * g_ref[...]).astype(o_ref.dtype)

    return _call(
        body, name="rms_in_fwd", grid=(t // tm,),
        in_specs=[pl.BlockSpec((tm, D_MODEL), lambda i: (i, 0)), pl.BlockSpec((1, D_MODEL), lambda i: (0, 0))],
        out_specs=[pl.BlockSpec((tm, D_MODEL), lambda i: (i, 0))],
        out_shape=[jax.ShapeDtypeStruct((t, D_MODEL), BF16)],
        semantics=("parallel",), args=(x2, g), exchange=exchange)


def _decay(lg):
    row = lax.broadcasted_iota(jnp.int32, (RET_TILE, RET_TILE), 0)
    col = lax.broadcasted_iota(jnp.int32, (RET_TILE, RET_TILE), 1)
    within = jnp.exp(lg * jnp.abs(row - col).astype(F32))
    inside = jnp.where((col >> 6) <= (row >> 6), within, 0.0)
    pos = lax.broadcasted_iota(jnp.int32, (RET_TILE, 1), 0).astype(F32)
    q_dec = jnp.exp(lg * (pos + 1.0))
    k_dec = jnp.exp(lg * (RET_TILE - 1.0 - pos))
    tile_dec = jnp.exp(lg * float(RET_TILE))
    return inside, q_dec, k_dec, tile_dec


def _scaled(a_bf16, dec):
    return (a_bf16.astype(F32) * dec).astype(BF16)


RET_GROUP = 1


def _per_head(one_head, kinds):
    def body(*refs):
        for head in range(RET_GROUP):
            def cut(ref, kind):
                if kind in "kv":
                    width = RET_KEY_DIM if kind == "k" else RET_VAL_DIM
                    return ref.at[:, head * width:(head + 1) * width]
                return ref.at[head] if kind == "h" else ref
            one_head(*[cut(ref, kind) for ref, kind in zip(refs, kinds)])
    return body


def _ret_specs(seq):
    kw, vw = RET_GROUP * RET_KEY_DIM, RET_GROUP * RET_VAL_DIM
    key = lambda base: pl.BlockSpec((seq, kw), lambda b, g: (b, base // kw + g))
    val = lambda base: pl.BlockSpec((seq, vw), lambda b, g: (b, base // vw + g))
    tab = pl.BlockSpec((seq, RET_KEY_DIM), lambda b, g: (0, 0))
    lgs = pl.BlockSpec((RET_GROUP, 1, LANES), lambda b, g: (g, 0, 0))
    return key, val, tab, lgs


def _ret_fwd(proj, cs, sn, lg_arr, batch, seq, exchange):
    t = batch * seq
    nt = seq // RET_TILE

    def one_head(q_ref, k_ref, v_ref, rg_ref, cs_ref, sn_ref, lg_ref, gro_ref, o_ref, qr_ref, kr_ref):
        lg = lg_ref[:, 0:1]
        cs_t, sn_t = cs_ref[...], sn_ref[...]
        q = q_ref[...].astype(F32)
        k = k_ref[...].astype(F32)
        qr_ref[...] = (q * cs_t + pltpu.roll(q, 64, 1) * sn_t).astype(BF16)
        kr_ref[...] = ((k * cs_t + pltpu.roll(k, 64, 1) * sn_t) * (RET_KEY_DIM ** -0.5)).astype(BF16)
        inside, q_dec, k_dec, tile_dec = _decay(lg)
        state = jnp.zeros((RET_KEY_DIM, RET_VAL_DIM), F32)
        for i in range(nt):
            rows = slice(i * RET_TILE, (i + 1) * RET_TILE)
            qi, ki, vi = qr_ref[rows, :], kr_ref[rows, :], v_ref[rows, :]
            acc = _dot((_dot_nt(qi, ki) * inside).astype(BF16), vi)
            if i > 0:
                acc = acc + _dot(_scaled(qi, q_dec), state.astype(BF16))
            if i < nt - 1:
                state = state * tile_dec + _dot_tn(_scaled(ki, k_dec), vi)
            o_ref[rows, :] = acc
            xc = acc - jnp.mean(acc, axis=-1, keepdims=True)
            nrm = xc * lax.rsqrt(jnp.mean(xc * xc, axis=-1, keepdims=True) + EPS)
            rg = rg_ref[rows, :].astype(F32)
            gro_ref[rows, :] = (rg * _sigmoid(rg) * nrm).astype(BF16)

    key, val, tab, lgs = _ret_specs(seq)
    return _call(
        _per_head(one_head, "kkvvsshvvkk"), name="ret_fwd", grid=(batch, RET_HEADS // RET_GROUP),
        in_specs=[key(C_RQ), key(C_RK), val(C_RV), val(C_RG), tab, tab, lgs],
        out_specs=[val(0), val(0), key(0), key(0)],
        out_shape=[jax.ShapeDtypeStruct((t, RET_HEADS * RET_VAL_DIM), BF16),
                   jax.ShapeDtypeStruct((t, RET_HEADS * RET_VAL_DIM), F32),
                   jax.ShapeDtypeStruct((t, RET_HEADS * RET_KEY_DIM), BF16),
                   jax.ShapeDtypeStruct((t, RET_HEADS * RET_KEY_DIM), BF16)],
        semantics=("parallel", "parallel"), args=(proj, proj, proj, proj, cs, sn, lg_arr), exchange=exchange)


def _att_bias(w_ref, bias_ref):
    n_i = lax.broadcasted_iota(jnp.int32, (ATT_Q, BIAS_LEN), 0)
    qc = lax.broadcasted_iota(jnp.int32, (ATT_Q, ATT_WIN), 0) >> 6
    kc = lax.broadcasted_iota(jnp.int32, (ATT_Q, ATT_WIN), 1) >> 6
    dc = qc + BAND_CHUNKS - kc
    band = (dc >= 0) & (dc <= BAND_CHUNKS)
    key = lax.broadcasted_iota(jnp.int32, (ATT_Q, ATT_WIN), 1)
    for e in range(2):
        xw = jnp.broadcast_to(w_ref[e:e + 1, :], (ATT_Q, BIAS_LEN))
        for bit in range(8):
            xw = jnp.where(((n_i >> bit) & 1) == 1, pltpu.roll(xw, 1 << bit, 1), xw)
        bias = jnp.where(band, xw[:, BIAS_LEN - ATT_WIN:], NEG_INF)
        for first in range(ATT_STARTS):
            bias_ref[first, e] = jnp.where(key + (first * ATT_Q - ATT_PAD) >= 0, bias, NEG_INF)
        bias_ref[ATT_STARTS, e] = bias


ATT_PAIRS = 2
ATT_COLS = ATT_PAIRS * LANES


def _att_specs(batch, seq):
    ni = seq // ATT_Q
    q_spec = pl.BlockSpec((ATT_Q, ATT_COLS), lambda g, b, i: (b * ni + i, C_AQ // ATT_COLS + g))
    k_spec = pl.BlockSpec((seq, ATT_COLS), lambda g, b, i: (b, C_AK // ATT_COLS + g))
    v_spec = pl.BlockSpec((seq, ATT_COLS), lambda g, b, i: (b, C_AV // ATT_COLS + g))
    w_spec = pl.BlockSpec((ATT_PAIRS, 2, BIAS_LEN), lambda g, b, i: (g, 0, 0))
    b_spec = pl.BlockSpec((ATT_PAIRS, ATT_STARTS + 1, 2, ATT_Q, ATT_WIN), lambda g, b, i: (g, 0, 0, 0, 0))
    pad = pltpu.VMEM((seq + ATT_PAD, ATT_COLS), BF16)
    return ni, q_spec, k_spec, v_spec, w_spec, b_spec, pad


def _att_bias_tiles(wvec, exchange):
    (tiles,), moved = _call(
        lambda w_ref, o_ref: _att_bias(w_ref, o_ref), name="att_bias", grid=(ATT_HEADS // 2,),
        in_specs=[pl.BlockSpec((None, 2, BIAS_LEN), lambda hp: (hp, 0, 0))],
        out_specs=[pl.BlockSpec((None, ATT_STARTS + 1, 2, ATT_Q, ATT_WIN), lambda hp: (hp, 0, 0, 0, 0))],
        out_shape=[jax.ShapeDtypeStruct((ATT_HEADS // 2, ATT_STARTS + 1, 2, ATT_Q, ATT_WIN), F32)],
        semantics=("parallel",), args=(wvec,), exchange=exchange)
    return tiles, moved


def _att_pad(src_ref, pad_ref):
    pad_ref[:ATT_PAD, :] = jnp.zeros((ATT_PAD, ATT_COLS), BF16)
    pad_ref[ATT_PAD:, :] = src_ref[...]


def _att_head(q2, sel):
    return jnp.where(sel, q2, jnp.zeros_like(q2)) * 0.125


def _att_softmax_rows(s_ref, bias_ref, rows):
    s = s_ref[rows, :] + bias_ref[rows, :]
    ex = jnp.exp(s - jnp.max(s, axis=-1, keepdims=True))
    return ex, 1.0 / jnp.sum(ex, axis=-1, keepdims=True)


def _att_fwd(proj, bias, batch, seq, exchange):
    ni, q_spec, k_spec, v_spec, _, b_spec, pad = _att_specs(batch, seq)

    def body(q_ref, k_ref, v_ref, bias_ref, o_ref, kp_ref, vp_ref, s_ref, e_ref):
        i = pl.program_id(2)

        @pl.when(i == 0)
        def _():
            _att_pad(k_ref, kp_ref)
            _att_pad(v_ref, vp_ref)

        win = pl.ds(pl.multiple_of(i * ATT_Q, ATT_Q), ATT_WIN)
        lo = lax.broadcasted_iota(jnp.int32, (1, LANES), 1) < 64
        start = jnp.minimum(i, ATT_STARTS)
        for pair in range(ATT_PAIRS):
            cols = slice(pair * LANES, (pair + 1) * LANES)
            k2, v2, q2 = kp_ref[win, cols], vp_ref[win, cols], q_ref[:, cols]
            out = jnp.zeros((ATT_Q, LANES), F32)
            for e in range(2):
                h = 2 * pair + e
                sel = lo if e == 0 else jnp.logical_not(lo)
                s_ref[h] = _dot_nt(_att_head(q2, sel), k2)
                rsum = []
                for c in range(ATT_Q // ATT_ROWS):
                    rows = slice(c * ATT_ROWS, (c + 1) * ATT_ROWS)
                    ex, r = _att_softmax_rows(s_ref.at[h], bias_ref.at[pair, start, e], rows)
                    e_ref[h, rows, :] = ex.astype(BF16)
                    rsum.append(r)
                out = out + _dot(e_ref[h], jnp.where(sel, v2, jnp.zeros_like(v2))) * jnp.concatenate(rsum, axis=0)
            o_ref[:, cols] = out.astype(BF16)

    heads = 2 * ATT_PAIRS
    return _call(
        body, name="att_fwd", grid=(ATT_HEADS // heads, batch, ni),
        in_specs=[q_spec, k_spec, v_spec, b_spec],
        out_specs=[pl.BlockSpec((ATT_Q, ATT_COLS), lambda g, b, i: (b * ni + i, g))],
        out_shape=[jax.ShapeDtypeStruct((batch * seq, ATT_HEADS * 64), BF16)],
        scratch=[pad, pad, pltpu.VMEM((heads, ATT_Q, ATT_WIN), F32), pltpu.VMEM((heads, ATT_Q, ATT_WIN), BF16)],
        semantics=("arbitrary", "arbitrary", "arbitrary"), args=(proj, proj, proj, bias), exchange=exchange)


GL_HALF = 512


def _gl_specs(tm):
    return [pl.BlockSpec((tm, GL_HALF), lambda i, c=C_GL // GL_HALF + j: (i, c)) for j in range(4)]


def _gates(gl_refs, b_ref):
    logits = [ref[...].astype(F32) for ref in gl_refs]
    gr = _sigmoid(jnp.concatenate(logits[:2], axis=1) + b_ref[:, :D_MODEL])
    ga = _sigmoid(jnp.concatenate(logits[2:], axis=1) + b_ref[:, D_MODEL:])
    return gr, ga


def _whole(a):
    return pl.BlockSpec(a.shape, lambda i: (0,) * a.ndim)


def _mix_out_fwd(gro, ao, proj, b_gate, w_ret, w_att_t, x2, w_out, g2):
    t = gro.shape[0]
    tm = min(256, t)

    def body(gro_ref, ao_ref, gl0, gl1, gl2, gl3, b_ref, wr_ref, wa_ref, x_ref, wo_ref, g_ref,
             z_ref, yr_ref, ya_ref, h_ref, hn_ref):
        yr = _dot(gro_ref[...], wr_ref[...])
        ya = _dot_nt(ao_ref[...], wa_ref[...])
        yr_ref[...] = yr.astype(BF16)
        ya_ref[...] = ya.astype(BF16)
        gr, ga = _gates((gl0, gl1, gl2, gl3), b_ref)
        z = (gr * yr + ga * ya).astype(BF16)
        z_ref[...] = z
        h = x_ref[...] + _dot(z, wo_ref[...])
        h_ref[...] = h
        r = lax.rsqrt(jnp.mean(h * h, axis=-1, keepdims=True) + EPS)
        hn_ref[...] = (h * r * g_ref[...]).astype(BF16)

    row = pl.BlockSpec((tm, D_MODEL), lambda i: (i, 0))
    return pl.pallas_call(
        body, name="mix_out_fwd", grid=(t // tm,),
        in_specs=[row, pl.BlockSpec((tm, 512), lambda i: (i, 0)), *_gl_specs(tm),
                  _whole(b_gate), _whole(w_ret), _whole(w_att_t), row, _whole(w_out), _whole(g2)],
        out_specs=[row] * 5,
        out_shape=[jax.ShapeDtypeStruct((t, D_MODEL), BF16)] * 3
                  + [jax.ShapeDtypeStruct((t, D_MODEL), F32), jax.ShapeDtypeStruct((t, D_MODEL), BF16)],
        compiler_params=_params("parallel"),
    )(gro, ao, proj, proj, proj, proj, b_gate, w_ret, w_att_t, x2, w_out, g2)


def _col_chunks(width, chunk=384):
    return [slice(lo, min(lo + chunk, width)) for lo in range(0, width, chunk)]


def _ffn_up(hn, wg_t, wu_t):
    t = hn.shape[0]
    tm, tn = min(512, t), D_FF // 2

    def body(h_ref, wg_ref, wu_ref, g_ref, u_ref, a_ref):
        g = _dot_nt(h_ref[...], wg_ref[...])
        u = _dot_nt(h_ref[...], wu_ref[...])
        g_ref[...] = g.astype(BF16)
        u_ref[...] = u.astype(BF16)
        a_ref[...] = (g * _sigmoid(g) * u).astype(BF16)

    w_spec = pl.BlockSpec((tn, D_MODEL), lambda j, i: (j, 0))
    out = pl.BlockSpec((tm, tn), lambda j, i: (i, j))
    return pl.pallas_call(
        body, name="ffn_up", grid=(D_FF // tn, t // tm),
        in_specs=[pl.BlockSpec((tm, D_MODEL), lambda j, i: (i, 0)), w_spec, w_spec],
        out_specs=[out, out, out],
        out_shape=[jax.ShapeDtypeStruct((t, D_FF), BF16)] * 3,
        compiler_params=_params("parallel", "parallel"),
    )(hn, wg_t, wu_t)


def _ffn_down_loss(a, h1, tgt, w_down, g3):
    t = a.shape[0]
    tm = min(512, t)

    def body(a_ref, h_ref, t_ref, w_ref, g_ref, dh_ref, dhb_ref, loss_ref, dg_ref):
        @pl.when(pl.program_id(0) == 0)
        def _():
            loss_ref[...] = jnp.zeros_like(loss_ref)
            dg_ref[...] = jnp.zeros_like(dg_ref)

        g = g_ref[...]
        h2 = h_ref[...] + _dot(a_ref[...], w_ref[...])
        r = lax.rsqrt(jnp.mean(h2 * h2, axis=-1, keepdims=True) + EPS)
        err = h2 * r * g - t_ref[...]
        loss_ref[...] += jnp.sum(err * err) * (0.5 / D_MODEL)
        dy = err * (1.0 / D_MODEL)
        dh, dg_rows = _rms_bwd(h2, g, dy)
        dg_ref[...] += jnp.sum(dg_rows, axis=0, keepdims=True)
        dh_ref[...] = dh
        dhb_ref[...] = dh.astype(BF16)

    row = pl.BlockSpec((tm, D_MODEL), lambda i: (i, 0))
    vec = pl.BlockSpec((1, D_MODEL), lambda i: (0, 0))
    return pl.pallas_call(
        body, name="ffn_down_loss", grid=(t // tm,),
        in_specs=[pl.BlockSpec((tm, D_FF), lambda i: (i, 0)), row, row,
                  pl.BlockSpec((D_FF, D_MODEL), lambda i: (0, 0)), vec],
        out_specs=[row, row, pl.BlockSpec((1, LANES), lambda i: (0, 0)), vec],
        out_shape=[jax.ShapeDtypeStruct((t, D_MODEL), F32), jax.ShapeDtypeStruct((t, D_MODEL), BF16),
                   jax.ShapeDtypeStruct((1, LANES), F32), jax.ShapeDtypeStruct((1, D_MODEL), F32)],
        compiler_params=_params("arbitrary"),
    )(a, h1, tgt, w_down, g3)


def _ffn_bwd_act(dh2b, w_down, g_act, u_act, exchange):
    t = dh2b.shape[0]
    tm, tn = min(512, t), D_FF // 2

    def body(d_ref, w_ref, g_ref, u_ref, dg_ref, du_ref):
        d = d_ref[...]
        for cols in _col_chunks(tn):
            da = _dot_nt(d, w_ref[cols, :])
            g = g_ref[:, cols].astype(F32)
            u = u_ref[:, cols].astype(F32)
            sg = _sigmoid(g)
            dg_ref[:, cols] = (da * u * sg * (1.0 + g * (1.0 - sg))).astype(BF16)
            du_ref[:, cols] = (da * g * sg).astype(BF16)

    blk = pl.BlockSpec((tm, tn), lambda j, i: (i, j))
    return _call(
        body, name="ffn_bwd_act", grid=(D_FF // tn, t // tm),
        in_specs=[pl.BlockSpec((tm, D_MODEL), lambda j, i: (i, 0)),
                  pl.BlockSpec((tn, D_MODEL), lambda j, i: (j, 0)), blk, blk],
        out_specs=[blk, blk],
        out_shape=[jax.ShapeDtypeStruct((t, D_FF), BF16)] * 2,
        semantics=("parallel", "parallel"), args=(dh2b, w_down, g_act, u_act), exchange=exchange)


def _ffn_bwd_in(dg, du, wg_t, wu_t, h1, dh2, g2, exchange):
    t = dg.shape[0]
    tm = min(256, t)

    def body(dg_ref, du_ref, wg_ref, wu_ref, h_ref, d2_ref, g_ref, dh_ref, dhb_ref, gn_ref):
        @pl.when(pl.program_id(0) == 0)
        def _():
            gn_ref[...] = jnp.zeros_like(gn_ref)

        dhn = _dot(dg_ref[...], wg_ref[...]) + _dot(du_ref[...], wu_ref[...])
        dx, dg_rows = _rms_bwd(h_ref[...], g_ref[...], dhn)
        dh = d2_ref[...] + dx
        dh_ref[...] = dh
        dhb_ref[...] = dh.astype(BF16)
        gn_ref[...] += jnp.sum(dg_rows, axis=0, keepdims=True)

    act = pl.BlockSpec((tm, D_FF), lambda i: (i, 0))
    row = pl.BlockSpec((tm, D_MODEL), lambda i: (i, 0))
    return _call(
        body, name="ffn_bwd_in", grid=(t // tm,),
        in_specs=[act, act, _whole(wg_t), _whole(wu_t), row, row, _whole(g2)],
        out_specs=[row, row, _whole(g2)],
        out_shape=[jax.ShapeDtypeStruct((t, D_MODEL), F32), jax.ShapeDtypeStruct((t, D_MODEL), BF16),
                   jax.ShapeDtypeStruct((1, D_MODEL), F32)],
        semantics=("arbitrary",), args=(dg, du, wg_t, wu_t, h1, dh2, g2), exchange=exchange)


def _mix_bwd(dh1b, w_out, proj, b_gate, y_ret, y_att, w_ret, w_att_t):
    t = dh1b.shape[0]
    tm = min(256, t)

    def body(d_ref, wo_ref, gl0, gl1, gl2, gl3, b_ref, yr_ref, ya_ref, wr_ref, wa_ref,
             dyr_ref, dya_ref, dglr_ref, dgla_ref, dgro_ref, dao_ref, db_ref):
        @pl.when(pl.program_id(0) == 0)
        def _():
            db_ref[...] = jnp.zeros_like(db_ref)

        dz = _dot_nt(d_ref[...], wo_ref[...])
        gr, ga = _gates((gl0, gl1, gl2, gl3), b_ref)
        dyr = (dz * gr).astype(BF16)
        dya = (dz * ga).astype(BF16)
        dyr_ref[...] = dyr
        dya_ref[...] = dya
        dglr = dz * yr_ref[...].astype(F32) * gr * (1.0 - gr)
        dgla = dz * ya_ref[...].astype(F32) * ga * (1.0 - ga)
        dglr_ref[...] = dglr.astype(BF16)
        dgla_ref[...] = dgla.astype(BF16)
        db_ref[:, :D_MODEL] += jnp.sum(dglr, axis=0, keepdims=True)
        db_ref[:, D_MODEL:] += jnp.sum(dgla, axis=0, keepdims=True)
        dgro_ref[...] = _dot_nt(dyr, wr_ref[...]).astype(BF16)
        dao_ref[...] = _dot(dya, wa_ref[...]).astype(BF16)

    row = pl.BlockSpec((tm, D_MODEL), lambda i: (i, 0))
    half = pl.BlockSpec((tm, 512), lambda i: (i, 0))
    return pl.pallas_call(
        body, name="mix_bwd", grid=(t // tm,),
        in_specs=[row, _whole(w_out), *_gl_specs(tm), _whole(b_gate), row, row, _whole(w_ret), _whole(w_att_t)],
        out_specs=[row, row, row, row, row, half, _whole(b_gate)],
        out_shape=[jax.ShapeDtypeStruct((t, D_MODEL), BF16)] * 5
                  + [jax.ShapeDtypeStruct((t, 512), BF16), jax.ShapeDtypeStruct(b_gate.shape, F32)],
        compiler_params=_params("arbitrary"),
    )(dh1b, w_out, proj, proj, proj, proj, b_gate, y_ret, y_att, w_ret, w_att_t)


def _ret_bwd(dgro, proj, o_ret, qr, kr, cs, sn, lg_arr, batch, seq, exchange):
    t = batch * seq
    nt = seq // RET_TILE

    def one_head(dgro_ref, rg_ref, o_ref, qr_ref, kr_ref, v_ref, cs_ref, sn_ref, lg_ref,
                 dq_ref, dk_ref, dv_ref, drg_ref, do_ref, st_ref):
        lg = lg_ref[:, 0:1]
        inside, q_dec, k_dec, tile_dec = _decay(lg)

        state = jnp.zeros((RET_KEY_DIM, RET_VAL_DIM), F32)
        for i in range(nt - 1):
            rows = slice(i * RET_TILE, (i + 1) * RET_TILE)
            state = state * tile_dec + _dot_tn(_scaled(kr_ref[rows, :], k_dec), v_ref[rows, :])
            st_ref[i + 1] = state.astype(BF16)

        for i in range(nt):
            rows = slice(i * RET_TILE, (i + 1) * RET_TILE)
            o = o_ref[rows, :]
            xc = o - jnp.mean(o, axis=-1, keepdims=True)
            rs = lax.rsqrt(jnp.mean(xc * xc, axis=-1, keepdims=True) + EPS)
            nrm = xc * rs
            rg = rg_ref[rows, :].astype(F32)
            sg = _sigmoid(rg)
            dg = dgro_ref[rows, :].astype(F32)
            drg_ref[rows, :] = (dg * nrm * sg * (1.0 + rg * (1.0 - sg))).astype(BF16)
            dn = dg * rg * sg
            do = rs * (dn - jnp.mean(dn, axis=-1, keepdims=True)
                       - nrm * jnp.mean(dn * nrm, axis=-1, keepdims=True))
            do_ref[rows, :] = do.astype(BF16)

        dstate = jnp.zeros((RET_KEY_DIM, RET_VAL_DIM), F32)
        for i in reversed(range(nt)):
            rows = slice(i * RET_TILE, (i + 1) * RET_TILE)
            qi, ki, vi, doi = qr_ref[rows, :], kr_ref[rows, :], v_ref[rows, :], do_ref[rows, :]
            p = (_dot_nt(qi, ki) * inside).astype(BF16)
            dp = (_dot_nt(doi, vi) * inside).astype(BF16)
            dq = _dot(dp, ki)
            dk = _dot_tn(dp, qi)
            dv = _dot_tn(p, doi)
            if i > 0:
                dq = dq + _dot_nt(doi, st_ref[i]) * q_dec
            if i < nt - 1:
                dsb = dstate.astype(BF16)
                dk = dk + _dot_nt(vi, dsb) * k_dec
                dv = dv + _dot(_scaled(ki, k_dec), dsb)
            if i > 0:
                dstate = dstate * tile_dec + _dot_tn(_scaled(qi, q_dec), doi)
            dq_ref[rows, :] = (dq * cs_ref[rows, :] - pltpu.roll(dq, 64, 1) * sn_ref[rows, :]).astype(BF16)
            dk = (dk * cs_ref[rows, :] - pltpu.roll(dk, 64, 1) * sn_ref[rows, :]) * (RET_KEY_DIM ** -0.5)
            dk_ref[rows, :] = dk.astype(BF16)
            dv_ref[rows, :] = dv.astype(BF16)

    key, val, tab, lgs = _ret_specs(seq)
    return _call(
        _per_head(one_head, "vvvkkvsshkkvvhh"), name="ret_bwd", grid=(batch, RET_HEADS // RET_GROUP),
        in_specs=[val(0), val(C_RG), val(0), key(0), key(0), val(C_RV), tab, tab, lgs],
        out_specs=[key(0), key(0), val(0), val(0)],
        out_shape=[jax.ShapeDtypeStruct((t, RET_HEADS * RET_KEY_DIM), BF16)] * 2
                  + [jax.ShapeDtypeStruct((t, RET_HEADS * RET_VAL_DIM), BF16)] * 2,
        scratch=[pltpu.VMEM((RET_GROUP, seq, RET_VAL_DIM), BF16),
                 pltpu.VMEM((RET_GROUP, nt, RET_KEY_DIM, RET_VAL_DIM), BF16)],
        semantics=("parallel", "parallel"), args=(dgro, proj, o_ret, qr, kr, proj, cs, sn, lg_arr),
        exchange=exchange)


def _att_bwd(proj, bias, dao, batch, seq, exchange):
    ni, q_spec, k_spec, v_spec, w_spec, b_spec, pad = _att_specs(batch, seq)
    t = batch * seq

    def body(q_ref, k_ref, v_ref, bias_ref, do_ref, dq_ref, dk_ref, dv_ref, dw_ref,
             dbias_ref, dk_acc, dv_acc, kp_ref, vp_ref, s_ref, dp_ref, e_ref, ds_ref):
        b, i = pl.program_id(1), pl.program_id(2)

        @pl.when((b == 0) & (i == 0))
        def _():
            dbias_ref[...] = jnp.zeros_like(dbias_ref)

        @pl.when(i == 0)
        def _():
            _att_pad(k_ref, kp_ref)
            _att_pad(v_ref, vp_ref)
            dk_acc[...] = jnp.zeros_like(dk_acc)
            dv_acc[...] = jnp.zeros_like(dv_acc)

        win = pl.ds(pl.multiple_of(i * ATT_Q, ATT_Q), ATT_WIN)
        lo = lax.broadcasted_iota(jnp.int32, (1, LANES), 1) < 64
        start = jnp.minimum(i, ATT_STARTS)
        for pair in range(ATT_PAIRS):
            cols = slice(pair * LANES, (pair + 1) * LANES)
            k2, v2, q2, do2 = kp_ref[win, cols], vp_ref[win, cols], q_ref[:, cols], do_ref[:, cols]
            dq = jnp.zeros((ATT_Q, LANES), F32)
            dk = jnp.zeros((LANES, ATT_WIN), F32)
            dv = jnp.zeros((LANES, ATT_WIN), F32)
            for e in range(2):
                h = 2 * pair + e
                sel = lo if e == 0 else jnp.logical_not(lo)
                qm = _att_head(q2, sel)
                dom = jnp.where(sel, do2, jnp.zeros_like(do2))
                s_ref[h] = _dot_nt(qm, k2)
                dp_ref[h] = _dot_nt(dom, v2)
                rsum = []
                for c in range(ATT_Q // ATT_ROWS):
                    rows = slice(c * ATT_ROWS, (c + 1) * ATT_ROWS)
                    ex, r = _att_softmax_rows(s_ref.at[h], bias_ref.at[pair, start, e], rows)
                    dp = dp_ref[h, rows, :]
                    mean = jnp.sum(dp * ex, axis=-1, keepdims=True) * r
                    ds = ex * ((dp - mean) * r)
                    dbias_ref[h, rows, :] += ds
                    ds_ref[h, rows, :] = ds.astype(BF16)
                    e_ref[h, rows, :] = ex.astype(BF16)
                    rsum.append(r)
                dq = dq + _dot(ds_ref[h], jnp.where(sel, k2, jnp.zeros_like(k2)))
                dk = dk + _dot_tn(qm, ds_ref[h])
                dv = dv + _dot_tn((dom.astype(F32) * jnp.concatenate(rsum, axis=0)).astype(BF16), e_ref[h])
            dq_ref[:, cols] = (dq * 0.125).astype(BF16)
            dk_acc[cols, win] += dk
            dv_acc[cols, win] += dv

        @pl.when(i == ni - 1)
        def _():
            dk_ref[...] = dk_acc[:, ATT_PAD:].T.astype(BF16)
            dv_ref[...] = dv_acc[:, ATT_PAD:].T.astype(BF16)

        @pl.when((b == batch - 1) & (i == ni - 1))
        def _():
            n_i = lax.broadcasted_iota(jnp.int32, (ATT_Q, BIAS_LEN), 0)
            for h in range(heads):
                xw = jnp.concatenate([jnp.zeros((ATT_Q, BIAS_LEN - ATT_WIN), F32), dbias_ref[h]], axis=1)
                for bit in range(8):
                    xw = jnp.where(((n_i >> bit) & 1) == 1, pltpu.roll(xw, BIAS_LEN - (1 << bit), 1), xw)
                dw_ref[h // 2, h % 2:h % 2 + 1, :] = jnp.sum(xw, axis=0, keepdims=True)

    heads = 2 * ATT_PAIRS
    seq_blk = pl.BlockSpec((seq, ATT_COLS), lambda g, b, i: (b, g))
    q_out = pl.BlockSpec((ATT_Q, ATT_COLS), lambda g, b, i: (b * ni + i, g))
    tile_f32, tile_bf16 = pltpu.VMEM((heads, ATT_Q, ATT_WIN), F32), pltpu.VMEM((heads, ATT_Q, ATT_WIN), BF16)
    acc = pltpu.VMEM((ATT_COLS, seq + ATT_PAD), F32)
    return _call(
        body, name="att_bwd", grid=(ATT_HEADS // heads, batch, ni),
        in_specs=[q_spec, k_spec, v_spec, b_spec, q_out],
        out_specs=[q_out, seq_blk, seq_blk, w_spec],
        out_shape=[jax.ShapeDtypeStruct((t, 512), BF16)] * 3
                  + [jax.ShapeDtypeStruct((ATT_HEADS // 2, 2, BIAS_LEN), F32)],
        scratch=[tile_f32, acc, acc, pad, pad, tile_f32, tile_f32, tile_bf16, tile_bf16],
        semantics=("arbitrary", "arbitrary", "arbitrary"), args=(proj, proj, proj, bias, dao), exchange=exchange,
        vmem=VMEM_LIMIT_ATT_BWD)


def _rms_in_bwd(x2, dxn, dh1, g1):
    t = x2.shape[0]
    tm = min(512, t)

    def body(x_ref, d_ref, h_ref, g_ref, dx_ref, dg_ref):
        @pl.when(pl.program_id(0) == 0)
        def _():
            dg_ref[...] = jnp.zeros_like(dg_ref)

        dx, dg_rows = _rms_bwd(x_ref[...], g_ref[...], d_ref[...])
        dx_ref[...] = h_ref[...] + dx
        dg_ref[...] += jnp.sum(dg_rows, axis=0, keepdims=True)

    row = pl.BlockSpec((tm, D_MODEL), lambda i: (i, 0))
    vec = pl.BlockSpec((1, D_MODEL), lambda i: (0, 0))
    return pl.pallas_call(
        body, name="rms_in_bwd", grid=(t // tm,),
        in_specs=[row, row, row, vec], out_specs=[row, vec],
        out_shape=[jax.ShapeDtypeStruct((t, D_MODEL), F32), jax.ShapeDtypeStruct((1, D_MODEL), F32)],
        compiler_params=_params("arbitrary"),
    )(x2, dxn, dh1, g1)


def _pack_small(dg1, dbr, dba, dg2, dg3, dw, loss):
    def body(a_ref, b_ref, c_ref, d_ref, e_ref, w_ref, l_ref, o_ref):
        o_ref[...] = jnp.zeros_like(o_ref)
        for r, ref in enumerate((a_ref, b_ref, c_ref, d_ref, e_ref)):
            o_ref[r:r + 1, :] = ref[...]
        o_ref[5:6, 0:LANES] = l_ref[...]
        for hp in range(ATT_HEADS // 2):
            o_ref[8 + 2 * hp:10 + 2 * hp, :] = w_ref[hp]

    return pl.pallas_call(body, name="pack_small",
                          out_shape=jax.ShapeDtypeStruct((16, D_MODEL), F32))(dg1, dbr, dba, dg2, dg3, dw, loss)


def _rotary_tables(seq):
    freqs = ROPE_BASE ** (-jnp.arange(0, RET_KEY_DIM, 2, dtype=F32) / RET_KEY_DIM)
    ang = jnp.arange(seq, dtype=F32)[:, None] * freqs[None, :]
    cos, sin = jnp.cos(ang), jnp.sin(ang)
    return jnp.concatenate([cos, cos], axis=1), jnp.concatenate([-sin, sin], axis=1)


def _bias_rows(rel_bias):
    n_far = BIAS_LEN - ATT_Q - MAX_REL + 1
    n_near = BIAS_LEN - n_far - (N_REL - 2)
    w = jnp.concatenate([jnp.broadcast_to(rel_bias[:, N_REL - 1:], (ATT_HEADS, n_far)),
                         rel_bias[:, 1:N_REL - 1][:, ::-1],
                         jnp.broadcast_to(rel_bias[:, :1], (ATT_HEADS, n_near))], axis=1)
    return w.reshape(ATT_HEADS // 2, 2, BIAS_LEN)


def _bias_rows_bwd(dw):
    n_far = BIAS_LEN - ATT_Q - MAX_REL + 1
    mid = dw[:, n_far:n_far + N_REL - 2][:, ::-1]
    return jnp.concatenate([jnp.sum(dw[:, n_far + N_REL - 2:], axis=1, keepdims=True), mid,
                            jnp.sum(dw[:, :n_far], axis=1, keepdims=True)], axis=1)


def _step(x, tgt, norm_mix, b_gate, norm_ffn, norm_final, rel_bias_shard, shard):
    batch, seq, _ = x.shape
    t = batch * seq
    n_rb = rel_bias_shard.shape[-1]
    x2, tgt2 = x.reshape(t, D_MODEL), tgt.reshape(t, D_MODEL)
    g3 = norm_final.reshape(1, D_MODEL)
    cs, sn = _rotary_tables(seq)
    lg = np.log(1.0 - 2.0 ** (-5.0 - np.arange(RET_HEADS, dtype=np.float32))).astype(np.float32)
    lg_arr = jnp.asarray(np.broadcast_to(lg[:, None, None], (RET_HEADS, 1, LANES)))

    def gather(*names):
        return _ChipGather([shard[nm] for nm in names])

    def scatter(*grads):
        return _Exchange(grads, scatter=True)

    rb_pad = jnp.pad(rel_bias_shard, ((0, 0), (0, LANES - n_rb)))
    (xn,), (w_in_half, rb_full) = _rms_fwd(x2, norm_mix,
                                           _ChipGather([shard["w_in_t"], rb_pad], parts=[(0, 2), (0, 1)]))
    rb_full = rb_full.reshape(N_DEV, ATT_HEADS, LANES)[:, :, :n_rb]
    bias, (w_in_t,) = _att_bias_tiles(
        _bias_rows(jnp.transpose(rb_full, (1, 0, 2)).reshape(ATT_HEADS, N_DEV * n_rb)),
        _ChipGather([shard["w_in_t"]], parts=[(1, 2)], into=[w_in_half]))
    proj, (w_ret, w_att_t, w_out, w_gate_t) = _mm(
        xn, w_in_t, tb=True, out_dtype=BF16, tm=1024, tn=1664, tk=1024, name="proj",
        exchange=gather("w_ret", "w_att_t", "w_out", "w_gate_t"))
    (gro, o_ret, qr, kr), _ = _ret_fwd(proj, cs, sn, lg_arr, batch, seq, None)
    (ao,), (w_up_t, w_down) = _att_fwd(proj, bias, batch, seq, gather("w_up_t", "w_down"))
    z, y_ret, y_att, h1, hn = _mix_out_fwd(gro, ao, proj, b_gate, w_ret, w_att_t, x2, w_out, norm_ffn)
    g_act, u_act, a_act = _ffn_up(hn, w_gate_t, w_up_t)
    dh2, dh2b, loss, dg3 = _ffn_down_loss(a_act, h1, tgt2, w_down, g3)

    wg = dict(out_dtype=BF16, tn=1024, ta=True)
    slots = {}
    dw_down = _mm(a_act, dh2b, tm=1408, tk=1024, name="dw_down", **wg)
    (d_gact, d_uact), _ = _ffn_bwd_act(dh2b, w_down, g_act, u_act, None)
    dw_gate = _mm(d_gact, hn, tm=1408, tk=1024, name="dw_gate", **wg)
    dw_up = _mm(d_uact, hn, tm=1408, tk=1024, name="dw_up", **wg)
    (dh1, dh1b, dg2), (slots["w_down"],) = _ffn_bwd_in(d_gact, d_uact, w_gate_t, w_up_t, h1, dh2, norm_ffn,
                                                     scatter(dw_down))
    dw_out = _mm(z, dh1b, tm=1024, tk=2048, name="dw_out", **wg)
    dyr, dya, dglr, dgla, dgro, dao, db = _mix_bwd(dh1b, w_out, proj, b_gate, y_ret, y_att, w_ret, w_att_t)
    dw_ret = _mm(gro, dyr, tm=1024, tk=2048, name="dw_ret", **wg)
    dw_att = _mm(dya, ao, tm=1024, tk=2048, name="dw_att", **wg)
    (drq, drk, drv, drg), _ = _ret_bwd(dgro, proj, o_ret, qr, kr, cs, sn, lg_arr, batch, seq, None)
    (daq, dak, dav, dw), (slots["w_gate_t"], slots["w_out"], slots["w_ret"], slots["w_att_t"]) = _att_bwd(
        proj, bias, dao, batch, seq, scatter(dw_gate, dw_out, dw_ret, dw_att))
    dproj = [drq, drk, drv, drg, daq, dak, dav, dglr, dgla]
    dw_in, (slots["w_up_t"],) = _mm_pieces(dproj, xn, ta=True, out_dtype=BF16, tm=512, tn=1024, tk=1024,
                                           name="dw_in", exchange=scatter(dw_up))
    (dw_in_sibling,) = _alone(_PairSwap([dw_in]), "swap_w_in")
    dw_in_pairs = _pair_add(dw_in, dw_in_sibling, "pair_w_in")
    dxn, (slots["w_in_t"],) = _mm_pieces(dproj, w_in_t, ta=False, out_dtype=F32, tm=1024, tn=1024, tk=512, name="dxn",
                                  exchange=_ChipScatter([dw_in_pairs]))
    dx, dg1 = _rms_in_bwd(x2, dxn, dh1, norm_mix)
    small = _pack_small(dg1, db[:, :D_MODEL], db[:, D_MODEL:], dg2, dg3, dw, loss)
    (small_slots,) = _alone(_ChipGather([small]), "gather_small")
    return dx.reshape(batch, seq, D_MODEL), slots, small_slots.reshape(N_DEV, 16, D_MODEL)


def _row_tile(r, c):
    return max(d for d in range(16, r + 1, 16) if r % d == 0 and (d * c <= 256 * 1024 or d == 16))


def _pair_add(grad, got, name):
    _, r, c = got.shape
    tr = r
    core = lax.axis_index("c").astype(jnp.int32).reshape(1)

    def body(core_ref, g_ref, a_ref, o_ref):
        o_ref[...] = (g_ref[...].astype(F32) + a_ref[...].astype(F32)).astype(o_ref.dtype)

    blk = pl.BlockSpec((None, tr, c), lambda q, i, core_ref: (q, i, 0))
    return pl.pallas_call(
        body, name=name,
        grid_spec=pltpu.PrefetchScalarGridSpec(
            num_scalar_prefetch=1, grid=(4, r // tr),
            in_specs=[pl.BlockSpec((None, None, tr, c), lambda q, i, core_ref: (q, core_ref[0], i, 0)), blk],
            out_specs=blk),
        out_shape=jax.ShapeDtypeStruct(got.shape, got.dtype),
        compiler_params=_params("parallel", "parallel"),
    )(core, grad.reshape(4, 2, r, c), got)


def _sum_slots(slots, name):
    n, r, c = slots.shape
    tr = _row_tile(r, c)

    def body(s_ref, o_ref):
        acc = s_ref[0].astype(F32)
        for s in range(1, n):
            acc = acc + s_ref[s].astype(F32)
        o_ref[...] = acc

    return pl.pallas_call(
        body, name=name, grid=(r // tr,),
        in_specs=[pl.BlockSpec((n, tr, c), lambda i: (0, i, 0))],
        out_specs=pl.BlockSpec((tr, c), lambda i: (i, 0)),
        out_shape=jax.ShapeDtypeStruct((r, c), F32),
        compiler_params=_params("parallel"),
    )(slots)


def _adamw_math(w, g, m, v):
    m = ADAM_B1 * m + (1.0 - ADAM_B1) * g
    v = ADAM_B2 * v + (1.0 - ADAM_B2) * (g * g)
    m_hat = m / (1.0 - ADAM_B1 ** ADAM_STEP)
    v_hat = v / (1.0 - ADAM_B2 ** ADAM_STEP)
    return -ADAM_LR * (m_hat / (jnp.sqrt(v_hat) + ADAM_EPS) + ADAM_WD * w), m, v


def _adamw(w, slots, m, v, name):
    n, r, c = slots.shape
    tr = _row_tile(r, c)

    def body(w_ref, s_ref, m_ref, v_ref, g_ref, d_ref, nm_ref, nv_ref):
        g = s_ref[0].astype(F32)
        for s in range(1, n):
            g = g + s_ref[s].astype(F32)
        g_ref[...] = g
        d_ref[...], nm_ref[...], nv_ref[...] = _adamw_math(w_ref[...], g, m_ref[...], v_ref[...])

    blk = pl.BlockSpec((tr, c), lambda i: (i, 0))
    return pl.pallas_call(
        body, name=name, grid=(r // tr,),
        in_specs=[blk, pl.BlockSpec((n, tr, c), lambda i: (0, i, 0)), blk, blk], out_specs=[blk] * 4,
        out_shape=[jax.ShapeDtypeStruct((r, c), F32)] * 4,
        compiler_params=_params("parallel"),
    )(w, slots, m, v)


def _adamw_small(ws, gs, ms, vs):
    n = len(ws)

    def body(*refs):
        for i in range(n):
            w_ref, g_ref, m_ref, v_ref = (refs[j * n + i] for j in range(4))
            d_ref, nm_ref, nv_ref = (refs[(4 + j) * n + i] for j in range(3))
            d_ref[...], nm_ref[...], nv_ref[...] = _adamw_math(w_ref[...], g_ref[...], m_ref[...], v_ref[...])

    shapes = [jax.ShapeDtypeStruct(w.shape, F32) for w in ws]
    outs = pl.pallas_call(body, name="adamw_small", out_shape=shapes * 3)(*ws, *gs, *ms, *vs)
    return outs[:n], outs[n:2 * n], outs[2 * n:]


def kernel(x, norm_mix, w_in, b_gate, rel_bias, w_ret_out, w_att_out, w_out, norm_ffn, w_ffn_gate, w_ffn_up, w_ffn_down, norm_final, loss_target, m_norm_mix, m_w_in, m_b_gate, m_rel_bias, m_w_ret_out, m_w_att_out, m_w_out, m_norm_ffn, m_w_ffn_gate, m_w_ffn_up, m_w_ffn_down, m_norm_final, v_norm_mix, v_w_in, v_b_gate, v_rel_bias, v_w_ret_out, v_w_att_out, v_w_out, v_norm_ffn, v_w_ffn_gate, v_w_ffn_up, v_w_ffn_down, v_norm_final):
    me = _index(_place())
    n_rb = rel_bias.shape[-1]

    shard = dict(w_in_t=w_in[0].T, w_gate_t=w_ffn_gate[0].T, w_up_t=w_ffn_up[0].T, w_down=w_ffn_down[0],
                 w_ret=w_ret_out[0], w_out=w_out[0], w_att_t=w_att_out[0].T)
    shard = {nm: s.astype(BF16) for nm, s in shard.items()}
    dx, slots, small_slots = _step(x, loss_target, norm_mix, b_gate, norm_ffn, norm_final, rel_bias[0], shard)
    small_sum = _sum_slots(small_slots, "sum_small")
    loss = small_sum[5, 0]

    transposed = dict(w_in="w_in_t", w_ffn_gate="w_gate_t", w_ffn_up="w_up_t", w_att_out="w_att_t")
    plain = dict(w_ffn_down="w_down", w_ret_out="w_ret", w_out="w_out")
    g = dict(
        norm_mix=small_sum[0:1], b_gate=jnp.concatenate([small_sum[1:2], small_sum[2:3]], axis=1),
        norm_ffn=small_sum[3:4], norm_final=small_sum[4:5],
        rel_bias=lax.dynamic_slice_in_dim(_bias_rows_bwd(small_sum[8:16]), me * n_rb, n_rb, axis=1),
    )
    w = dict(norm_mix=norm_mix, w_in=w_in, b_gate=b_gate, rel_bias=rel_bias, w_ret_out=w_ret_out, w_att_out=w_att_out,
             w_out=w_out, norm_ffn=norm_ffn, w_ffn_gate=w_ffn_gate, w_ffn_up=w_ffn_up, w_ffn_down=w_ffn_down,
             norm_final=norm_final)
    m = dict(norm_mix=m_norm_mix, w_in=m_w_in, b_gate=m_b_gate, rel_bias=m_rel_bias, w_ret_out=m_w_ret_out,
             w_att_out=m_w_att_out, w_out=m_w_out, norm_ffn=m_norm_ffn, w_ffn_gate=m_w_ffn_gate, w_ffn_up=m_w_ffn_up,
             w_ffn_down=m_w_ffn_down, norm_final=m_norm_final)
    v = dict(norm_mix=v_norm_mix, w_in=v_w_in, b_gate=v_b_gate, rel_bias=v_rel_bias, w_ret_out=v_w_ret_out,
             w_att_out=v_w_att_out, w_out=v_w_out, norm_ffn=v_norm_ffn, w_ffn_gate=v_w_ffn_gate, w_ffn_up=v_w_ffn_up,
             w_ffn_down=v_w_ffn_down, norm_final=v_norm_final)
    order = ("norm_mix", "w_in", "b_gate", "rel_bias", "w_ret_out", "w_att_out", "w_out", "norm_ffn",
             "w_ffn_gate", "w_ffn_up", "w_ffn_down", "norm_final")
    small_names = ("norm_mix", "b_gate", "rel_bias", "norm_ffn", "norm_final")

    def flat(a):
        return a[0] if a.ndim == 3 else a.reshape(-1, a.shape[-1])

    grad, delta, new_m, new_v = {}, {}, {}, {}
    for nm in order:
        if nm in transposed:
            res = _adamw(w[nm][0].T, slots[transposed[nm]], m[nm][0].T, v[nm][0].T, "adamw_" + nm)
            grad[nm], delta[nm], new_m[nm], new_v[nm] = (a.T[None] for a in res)
        elif nm in plain:
            res = _adamw(flat(w[nm]), slots[plain[nm]], flat(m[nm]), flat(v[nm]), "adamw_" + nm)
            grad[nm], delta[nm], new_m[nm], new_v[nm] = (a.reshape(w[nm].shape) for a in res)
    ds, nms, nvs = _adamw_small([flat(w[nm]) for nm in small_names], [g[nm] for nm in small_names],
                                [flat(m[nm]) for nm in small_names], [flat(v[nm]) for nm in small_names])
    for i, nm in enumerate(small_names):
        grad[nm], delta[nm], new_m[nm], new_v[nm] = (a.reshape(w[nm].shape) for a in (g[nm], ds[i], nms[i], nvs[i]))

    return (loss, dx, *[grad[nm] for nm in order], *[delta[nm] for nm in order],
            *[new_m[nm] for nm in order], *[new_v[nm] for nm in order])
```

```python
import numpy as np
import jax
import jax.numpy as jnp
from jax import lax
from jax.experimental import pallas as pl
from jax.experimental.pallas import tpu as pltpu

F32 = jnp.float32
BF16 = jnp.bfloat16
MESH = pl.DeviceIdType.MESH

D_MODEL = 1024
CHUNK = 64
RET_HEADS = 4
RET_KEY_DIM = 128
RET_VAL_DIM = 256
ATT_HEADS = 8
BAND_CHUNKS = 8
MAX_REL = 256
N_REL = CHUNK + MAX_REL
D_FF = 2816
N_IN = 6656
ROPE_BASE = 10000.0
EPS = 1e-6
NEG_INF = -1e30
C_RQ, C_RK, C_RV, C_RG, C_AQ, C_AK, C_AV, C_GL = 0, 512, 1024, 2048, 3072, 3584, 4096, 4608

ADAM_LR = 0.001
ADAM_B1 = 0.9
ADAM_B2 = 0.999
ADAM_EPS = 1e-08
ADAM_WD = 0.01
ADAM_STEP = 10

N_DEV = 8
LANES = 128
RET_TILE = 256
ATT_Q = 256
ATT_PAD = BAND_CHUNKS * CHUNK
ATT_WIN = ATT_PAD + ATT_Q
ATT_STARTS = ATT_PAD // ATT_Q
ATT_ROWS = 32
BIAS_LEN = 1024
VMEM_LIMIT = 48 * 1024 * 1024
VMEM_LIMIT_ATT_BWD = 56 * 1024 * 1024


def _params(*sem, vmem=VMEM_LIMIT):
    return pltpu.CompilerParams(dimension_semantics=sem, vmem_limit_bytes=vmem)


def _dot(a, b):
    return lax.dot_general(a, b, (((1,), (0,)), ((), ())), preferred_element_type=F32)


def _dot_nt(a, b):
    return lax.dot_general(a, b, (((1,), (1,)), ((), ())), preferred_element_type=F32)


def _dot_tn(a, b):
    return lax.dot_general(a, b, (((0,), (0,)), ((), ())), preferred_element_type=F32)


def _sigmoid(x):
    return 1.0 / (1.0 + jnp.exp(-x))


def _rms_bwd(x, g, dy):
    r = lax.rsqrt(jnp.mean(x * x, axis=-1, keepdims=True) + EPS)
    u = dy * g
    dx = r * u - x * (r * r * r) * jnp.mean(u * x, axis=-1, keepdims=True)
    return dx, dy * x * r


def _place():
    return lax.axis_index("x"), lax.axis_index("y"), lax.axis_index("c")


def _peer(k):
    x, y, c = _place()
    return ((1 - x) if k & 4 else x, (1 - y) if k & 2 else y, (1 - c) if k & 1 else c)


def _index(place):
    return 4 * place[0] + 2 * place[1] + place[2]


def _rows(ref, block, nrows):
    align = 16 if ref.dtype == BF16 else 8
    return ref.at[pl.ds(pl.multiple_of(block * nrows, align), nrows)]


class _Exchange:
    def __init__(self, arrays, scatter):
        self.arrays, self.scatter, self.n = list(arrays), scatter, len(arrays)

    def out_shape(self):
        if self.scatter:
            return [jax.ShapeDtypeStruct((N_DEV, a.shape[0] // N_DEV) + a.shape[1:], a.dtype) for a in self.arrays]
        return [jax.ShapeDtypeStruct((N_DEV * a.shape[0],) + a.shape[1:], a.dtype) for a in self.arrays]

    def scratch(self):
        return [pltpu.SemaphoreType.DMA((self.n, N_DEV - 1)), pltpu.SemaphoreType.DMA((self.n, N_DEV - 1)),
                pltpu.SemaphoreType.DMA((self.n,))]

    def _copies(self, ins, outs, sems):
        send_sems, recv_sems, local_sems = sems
        me = _index(_place())

        def src(w, to):
            return _rows(ins[w], to, ins[w].shape[0] // N_DEV) if self.scatter else ins[w]

        def dst(w, origin):
            return outs[w].at[origin] if self.scatter else _rows(outs[w], origin, ins[w].shape[0])

        def remote(w, k, to, origin):
            return pltpu.make_async_remote_copy(src_ref=src(w, to), dst_ref=dst(w, origin),
                                                send_sem=send_sems.at[w, k - 1], recv_sem=recv_sems.at[w, k - 1],
                                                device_id=_peer(k), device_id_type=MESH)

        pairs = [(w, k) for w in range(self.n) for k in range(1, N_DEV)]
        own = lambda: [pltpu.make_async_copy(src(w, me), dst(w, me), local_sems.at[w]) for w in range(self.n)]
        sent = lambda: [remote(w, k, _index(_peer(k)), me) for w, k in pairs]
        arriving = lambda: [remote(w, k, me, _index(_peer(k))) for w, k in pairs]
        return own, sent, arriving

    def start(self, ins, outs, sems):
        own, sent, _ = self._copies(ins, outs, sems)
        for cp in own() + sent():
            cp.start()

    def wait(self, ins, outs, sems):
        own, sent, arriving = self._copies(ins, outs, sems)
        for cp in arriving():
            cp.wait_recv()
        for cp in sent():
            cp.wait_send()
        for cp in own():
            cp.wait()


class _PairSwap:
    def __init__(self, arrays):
        self.arrays, self.n = list(arrays), len(arrays)

    def out_shape(self):
        return [jax.ShapeDtypeStruct((4, a.shape[0] // N_DEV) + a.shape[1:], a.dtype) for a in self.arrays]

    def scratch(self):
        return [pltpu.SemaphoreType.DMA((self.n, 4)), pltpu.SemaphoreType.DMA((self.n, 4))]

    def _copies(self, ins, outs, sems):
        send_sems, recv_sems = sems
        x, y, c = _place()
        return [pltpu.make_async_remote_copy(
            src_ref=_rows(ins[w], 2 * q + 1 - c, ins[w].shape[0] // N_DEV), dst_ref=outs[w].at[q],
            send_sem=send_sems.at[w, q], recv_sem=recv_sems.at[w, q],
            device_id=(x, y, 1 - c), device_id_type=MESH) for w in range(self.n) for q in range(4)]

    def start(self, ins, outs, sems):
        for cp in self._copies(ins, outs, sems):
            cp.start()

    def wait(self, ins, outs, sems):
        for cp in self._copies(ins, outs, sems):
            cp.wait()


class _ChipScatter:
    def __init__(self, arrays):
        self.arrays, self.n = list(arrays), len(arrays)

    def out_shape(self):
        return [jax.ShapeDtypeStruct(a.shape, a.dtype) for a in self.arrays]

    def scratch(self):
        return [pltpu.SemaphoreType.DMA((self.n, 3)), pltpu.SemaphoreType.DMA((self.n, 3)),
                pltpu.SemaphoreType.DMA((self.n,))]

    def _copies(self, ins, outs, sems):
        send_sems, recv_sems, local_sems = sems
        x, y, c = _place()
        mine = 2 * x + y
        sent, arriving = [], []
        for w in range(self.n):
            for k in range(1, 4):
                tx, ty = (1 - x) if k & 2 else x, (1 - y) if k & 1 else y
                other = 2 * tx + ty
                sent.append(lambda w=w, k=k, tx=tx, ty=ty, other=other: pltpu.make_async_remote_copy(
                    src_ref=ins[w].at[other], dst_ref=outs[w].at[mine], send_sem=send_sems.at[w, k - 1],
                    recv_sem=recv_sems.at[w, k - 1], device_id=(tx, ty, c), device_id_type=MESH))
                arriving.append(lambda w=w, k=k, tx=tx, ty=ty, other=other: pltpu.make_async_remote_copy(
                    src_ref=ins[w].at[mine], dst_ref=outs[w].at[other], send_sem=send_sems.at[w, k - 1],
                    recv_sem=recv_sems.at[w, k - 1], device_id=(tx, ty, c), device_id_type=MESH))
        own = [lambda w=w: pltpu.make_async_copy(ins[w].at[mine], outs[w].at[mine], local_sems.at[w])
               for w in range(self.n)]
        return own, sent, arriving

    def start(self, ins, outs, sems):
        own, sent, _ = self._copies(ins, outs, sems)
        for cp in own + sent:
            cp().start()

    def wait(self, ins, outs, sems):
        own, sent, arriving = self._copies(ins, outs, sems)
        for cp in arriving:
            cp().wait_recv()
        for cp in sent:
            cp().wait_send()
        for cp in own:
            cp().wait()


class _ChipGather:
    def __init__(self, arrays, parts=None, into=None):
        self.arrays, self.n, self.into = list(arrays), len(arrays), into
        self.parts = parts or [(0, 1)] * self.n

    def out_shape(self):
        return [jax.ShapeDtypeStruct((N_DEV * a.shape[0],) + a.shape[1:], a.dtype) for a in self.arrays]

    def scratch(self):
        return [pltpu.SemaphoreType.DMA((self.n, N_DEV - 1)), pltpu.SemaphoreType.DMA((self.n, N_DEV - 1)),
                pltpu.SemaphoreType.DMA((self.n,))]

    def _parts(self, ins, outs, sems):
        send_sems, recv_sems, local_sems = sems
        x, y, c = _place()
        me, sibling = (x, y, c), (x, y, 1 - c)
        chips = [(1 - x, y), (x, 1 - y), (1 - x, 1 - y)]

        def rows(w, place, whole):
            (index, count), r = self.parts[w], ins[w].shape[0]
            lo, size = (0, r) if whole else (index * (r // count), r // count)
            align = 16 if ins[w].dtype == BF16 else 8
            return outs[w].at[pl.ds(pl.multiple_of(_index(place) * r + lo, align), size)]

        def mine(w, whole):
            (index, count), r = self.parts[w], ins[w].shape[0]
            return ins[w] if whole or count == 1 else ins[w].at[pl.ds(index * (r // count), r // count)]

        def copy(w, k, block, to, own=False):
            whole = k == 0
            return pltpu.make_async_remote_copy(src_ref=mine(w, whole) if own else rows(w, block, whole),
                                                dst_ref=rows(w, block, whole),
                                                send_sem=send_sems.at[w, k], recv_sem=recv_sems.at[w, k],
                                                device_id=to, device_id_type=MESH)

        def local(w):
            return pltpu.make_async_copy(ins[w], rows(w, me, True), local_sems.at[w])

        return me, sibling, chips, c, copy, local, [index == 0 for index, _ in self.parts]

    def start(self, ins, outs, sems):
        me, sibling, chips, c, copy, local, places_own = self._parts(ins, outs, sems)
        for w in range(self.n):
            if places_own[w]:
                local(w).start()
                copy(w, 0, me, sibling, own=True).start()
            for j, chip in enumerate(chips):
                copy(w, 1 + j, me, (*chip, c), own=True).start()

    def wait(self, ins, outs, sems):
        me, sibling, chips, c, copy, local, places_own = self._parts(ins, outs, sems)
        for w in range(self.n):
            for j, chip in enumerate(chips):
                copy(w, 1 + j, (*chip, c), me).wait_recv()
                copy(w, 4 + j, (*chip, c), sibling).start()
        for w in range(self.n):
            if places_own[w]:
                copy(w, 0, sibling, me).wait_recv()
                copy(w, 0, me, sibling, own=True).wait_send()
                local(w).wait()
            for j, chip in enumerate(chips):
                copy(w, 4 + j, (*chip, 1 - c), me).wait_recv()
                copy(w, 1 + j, me, (*chip, c), own=True).wait_send()
                copy(w, 4 + j, (*chip, c), sibling).wait_send()


def _call(body, *, name, grid, in_specs, out_specs, out_shape, scratch=(), semantics, args, exchange=None,
          vmem=VMEM_LIMIT):
    if exchange is None:
        return pl.pallas_call(body, name=name, grid=grid, in_specs=in_specs, out_specs=out_specs, out_shape=out_shape,
                              scratch_shapes=list(scratch),
                              compiler_params=_params(*semantics, vmem=vmem))(*args), None
    n_in, n_out, n_scr, nx = len(in_specs), len(out_specs), len(scratch), exchange.n
    into = list(getattr(exchange, "into", None) or [])

    def full_body(*refs):
        ins, refs = refs[:n_in], refs[n_in:]
        x_in, refs = refs[:nx], refs[nx + len(into):]
        outs, refs = refs[:n_out], refs[n_out:]
        x_out, refs = refs[:nx], refs[nx:]
        scr, sems = refs[:n_scr], refs[n_scr:]
        first, last = True, True
        for axis, size in enumerate(grid):
            first = jnp.logical_and(first, pl.program_id(axis) == 0)
            last = jnp.logical_and(last, pl.program_id(axis) == size - 1)
        if grid:
            pl.when(first)(lambda: exchange.start(x_in, x_out, sems))
        else:
            exchange.start(x_in, x_out, sems)
        body(*ins, *outs, *scr)
        if grid:
            pl.when(last)(lambda: exchange.wait(x_in, x_out, sems))
        else:
            exchange.wait(x_in, x_out, sems)

    hbm = pl.BlockSpec(memory_space=pltpu.HBM)
    res = pl.pallas_call(
        full_body, name=name, grid=grid,
        in_specs=list(in_specs) + [hbm] * (nx + len(into)), out_specs=list(out_specs) + [hbm] * nx,
        out_shape=list(out_shape) + exchange.out_shape(),
        scratch_shapes=list(scratch) + exchange.scratch(),
        input_output_aliases={n_in + nx + w: n_out + w for w in range(len(into))},
        compiler_params=_params(*(["arbitrary"] * len(grid)), vmem=vmem),
    )(*args, *exchange.arrays, *into)
    return res[:n_out], res[n_out:]


def _alone(exchange, name):
    return _call(lambda: None, name=name, grid=(), in_specs=[], out_specs=[], out_shape=[], semantics=(),
                 args=(), exchange=exchange)[1]


def _mm(a, b, *, ta=False, tb=False, out_dtype, tm, tn, tk, name, exchange=None):
    m, k = (a.shape[1], a.shape[0]) if ta else a.shape
    n = b.shape[0] if tb else b.shape[1]
    assert k == (b.shape[1] if tb else b.shape[0])
    tm, tn, tk = min(tm, m), min(tn, n), min(tk, k)
    assert m % tm == 0 and n % tn == 0 and k % tk == 0, (name, m, n, k)
    nk = k // tk
    dims = (((0 if ta else 1,), (1 if tb else 0,)), ((), ()))

    def body(a_ref, b_ref, o_ref, *acc):
        prod = lax.dot_general(a_ref[...].astype(BF16), b_ref[...].astype(BF16), dims, preferred_element_type=F32)
        if nk == 1:
            o_ref[...] = prod.astype(o_ref.dtype)
            return
        acc_ref, kk = acc[0], pl.program_id(2)

        @pl.when(kk == 0)
        def _():
            acc_ref[...] = prod

        @pl.when((kk > 0) & (kk < nk - 1))
        def _():
            acc_ref[...] += prod

        @pl.when(kk == nk - 1)
        def _():
            o_ref[...] = (acc_ref[...] + prod).astype(o_ref.dtype)

    a_spec = (pl.BlockSpec((tk, tm), lambda i, j, kk: (kk, i)) if ta
              else pl.BlockSpec((tm, tk), lambda i, j, kk: (i, kk)))
    b_spec = (pl.BlockSpec((tn, tk), lambda i, j, kk: (j, kk)) if tb
              else pl.BlockSpec((tk, tn), lambda i, j, kk: (kk, j)))
    (out,), moved = _call(
        body, name=name, grid=(m // tm, n // tn, nk),
        in_specs=[a_spec, b_spec],
        out_specs=[pl.BlockSpec((tm, tn), lambda i, j, kk: (i, j))],
        out_shape=[pltpu.HBM((m, n), out_dtype)],
        scratch=[pltpu.VMEM((tm, tn), F32)] if nk > 1 else [],
        semantics=("parallel", "parallel", "arbitrary"), args=(a, b), exchange=exchange)
    return out if exchange is None else (out, moved)


def _mm_pieces(pieces, b, *, ta, out_dtype, tm, tn, tk, name, exchange=None):
    rows, n = pieces[0].shape[0], b.shape[1]
    step = tm if ta else tk
    assert all(p.shape[0] == rows and p.shape[1] % step == 0 for p in pieces), name
    edges = [int(e) for e in np.cumsum([0] + [p.shape[1] // step for p in pieces])]
    total = edges[-1] * step
    m, k = (total, rows) if ta else (rows, total)
    assert b.shape[0] == k and m % tm == 0 and n % tn == 0 and k % tk == 0, name
    nk, npieces = k // tk, len(pieces)
    dims = (((0 if ta else 1,), (0,)), ((), ()))
    b_resident = ta and n == tn

    def body(*refs):
        a_refs, (b_ref, o_ref, acc_ref) = refs[:npieces], refs[npieces:]
        kk = pl.program_id(2)
        pos = pl.program_id(0) if ta else kk

        @pl.when(kk == 0)
        def _():
            acc_ref[...] = jnp.zeros_like(acc_ref)

        def b_tile():
            return b_ref[pl.ds(pl.multiple_of(kk * tk, tk), tk), :] if b_resident else b_ref[...]

        for p, a_ref in enumerate(a_refs):
            @pl.when((pos >= edges[p]) & (pos < edges[p + 1]))
            def _(a_ref=a_ref):
                acc_ref[...] += lax.dot_general(a_ref[...], b_tile(), dims, preferred_element_type=F32)

        @pl.when(kk == nk - 1)
        def _():
            o_ref[...] = acc_ref[...].astype(o_ref.dtype)

    def a_spec(p):
        lo, last = edges[p], edges[p + 1] - edges[p] - 1
        if ta:
            def index(i, j, kk):
                inside = (i >= lo) & (i <= lo + last)
                return jnp.where(inside, kk, 0), jnp.clip(i - lo, 0, last)
            return pl.BlockSpec((tk, tm), index)
        return pl.BlockSpec((tm, tk), lambda i, j, kk: (i, jnp.clip(kk - lo, 0, last)))

    (out,), moved = _call(
        body, name=name, grid=(m // tm, n // tn, nk),
        in_specs=[a_spec(p) for p in range(npieces)]
                 + [pl.BlockSpec(b.shape, lambda i, j, kk: (0, 0)) if b_resident
                    else pl.BlockSpec((tk, tn), lambda i, j, kk: (kk, j))],
        out_specs=[pl.BlockSpec((tm, tn), lambda i, j, kk: (i, j))],
        out_shape=[pltpu.HBM((m, n), out_dtype)],
        scratch=[pltpu.VMEM((tm, tn), F32)],
        semantics=("parallel", "parallel", "arbitrary"), args=(*pieces, b), exchange=exchange)
    return out if exchange is None else (out, moved)


def _rms_fwd(x2, g, exchange):
    t = x2.shape[0]
    tm = min(512, t)

    def body(x_ref, g_ref, o_ref):
        x = x_ref[...]
        r = lax.rsqrt(jnp.mean(x * x, axis=-1, keepdims=True) + EPS)
        o_ref[...] = (x * r * g_ref[...]).astype(o_ref.dtype)

    return _call(
        body, name="rms_in_fwd", grid=(t // tm,),
        in_specs=[pl.BlockSpec((tm, D_MODEL), lambda i: (i, 0)), pl.BlockSpec((1, D_MODEL), lambda i: (0, 0))],
        out_specs=[pl.BlockSpec((tm, D_MODEL), lambda i: (i, 0))],
        out_shape=[jax.ShapeDtypeStruct((t, D_MODEL), BF16)],
        semantics=("parallel",), args=(x2, g), exchange=exchange)


def _decay(lg):
    row = lax.broadcasted_iota(jnp.int32, (RET_TILE, RET_TILE), 0)
    col = lax.broadcasted_iota(jnp.int32, (RET_TILE, RET_TILE), 1)
    within = jnp.exp(lg * jnp.abs(row - col).astype(F32))
    inside = jnp.where((col >> 6) <= (row >> 6), within, 0.0)
    pos = lax.broadcasted_iota(jnp.int32, (RET_TILE, 1), 0).astype(F32)
    q_dec = jnp.exp(lg * (pos + 1.0))
    k_dec = jnp.exp(lg * (RET_TILE - 1.0 - pos))
    tile_dec = jnp.exp(lg * float(RET_TILE))
    return inside, q_dec, k_dec, tile_dec


def _scaled(a_bf16, dec):
    return (a_bf16.astype(F32) * dec).astype(BF16)


RET_GROUP = 1


def _per_head(one_head, kinds):
    def body(*refs):
        for head in range(RET_GROUP):
            def cut(ref, kind):
                if kind in "kv":
                    width = RET_KEY_DIM if kind == "k" else RET_VAL_DIM
                    return ref.at[:, head * width:(head + 1) * width]
                return ref.at[head] if kind == "h" else ref
            one_head(*[cut(ref, kind) for ref, kind in zip(refs, kinds)])
    return body


def _ret_specs(seq):
    kw, vw = RET_GROUP * RET_KEY_DIM, RET_GROUP * RET_VAL_DIM
    key = lambda base: pl.BlockSpec((seq, kw), lambda b, g: (b, base // kw + g))
    val = lambda base: pl.BlockSpec((seq, vw), lambda b, g: (b, base // vw + g))
    tab = pl.BlockSpec((seq, RET_KEY_DIM), lambda b, g: (0, 0))
    lgs = pl.BlockSpec((RET_GROUP, 1, LANES), lambda b, g: (g, 0, 0))
    return key, val, tab, lgs


def _ret_fwd(proj, cs, sn, lg_arr, batch, seq, exchange):
    t = batch * seq
    nt = seq // RET_TILE

    def one_head(q_ref, k_ref, v_ref, rg_ref, cs_ref, sn_ref, lg_ref, gro_ref, o_ref, qr_ref, kr_ref):
        lg = lg_ref[:, 0:1]
        cs_t, sn_t = cs_ref[...], sn_ref[...]
        q = q_ref[...].astype(F32)
        k = k_ref[...].astype(F32)
        qr_ref[...] = (q * cs_t + pltpu.roll(q, 64, 1) * sn_t).astype(BF16)
        kr_ref[...] = ((k * cs_t + pltpu.roll(k, 64, 1) * sn_t) * (RET_KEY_DIM ** -0.5)).astype(BF16)
        inside, q_dec, k_dec, tile_dec = _decay(lg)
        state = jnp.zeros((RET_KEY_DIM, RET_VAL_DIM), F32)
        for i in range(nt):
            rows = slice(i * RET_TILE, (i + 1) * RET_TILE)
            qi, ki, vi = qr_ref[rows, :], kr_ref[rows, :], v_ref[rows, :]
            acc = _dot((_dot_nt(qi, ki) * inside).astype(BF16), vi)
            if i > 0:
                acc = acc + _dot(_scaled(qi, q_dec), state.astype(BF16))
            if i < nt - 1:
                state = state * tile_dec + _dot_tn(_scaled(ki, k_dec), vi)
            o_ref[rows, :] = acc
            xc = acc - jnp.mean(acc, axis=-1, keepdims=True)
            nrm = xc * lax.rsqrt(jnp.mean(xc * xc, axis=-1, keepdims=True) + EPS)
            rg = rg_ref[rows, :].astype(F32)
            gro_ref[rows, :] = (rg * _sigmoid(rg) * nrm).astype(BF16)

    key, val, tab, lgs = _ret_specs(seq)
    return _call(
        _per_head(one_head, "kkvvsshvvkk"), name="ret_fwd", grid=(batch, RET_HEADS // RET_GROUP),
        in_specs=[key(C_RQ), key(C_RK), val(C_RV), val(C_RG), tab, tab, lgs],
        out_specs=[val(0), val(0), key(0), key(0)],
        out_shape=[jax.ShapeDtypeStruct((t, RET_HEADS * RET_VAL_DIM), BF16),
                   jax.ShapeDtypeStruct((t, RET_HEADS * RET_VAL_DIM), F32),
                   jax.ShapeDtypeStruct((t, RET_HEADS * RET_KEY_DIM), BF16),
                   jax.ShapeDtypeStruct((t, RET_HEADS * RET_KEY_DIM), BF16)],
        semantics=("parallel", "parallel"), args=(proj, proj, proj, proj, cs, sn, lg_arr), exchange=exchange)


def _att_bias(w_ref, bias_ref):
    n_i = lax.broadcasted_iota(jnp.int32, (ATT_Q, BIAS_LEN), 0)
    qc = lax.broadcasted_iota(jnp.int32, (ATT_Q, ATT_WIN), 0) >> 6
    kc = lax.broadcasted_iota(jnp.int32, (ATT_Q, ATT_WIN), 1) >> 6
    dc = qc + BAND_CHUNKS - kc
    band = (dc >= 0) & (dc <= BAND_CHUNKS)
    key = lax.broadcasted_iota(jnp.int32, (ATT_Q, ATT_WIN), 1)
    for e in range(2):
        xw = jnp.broadcast_to(w_ref[e:e + 1, :], (ATT_Q, BIAS_LEN))
        for bit in range(8):
            xw = jnp.where(((n_i >> bit) & 1) == 1, pltpu.roll(xw, 1 << bit, 1), xw)
        bias = jnp.where(band, xw[:, BIAS_LEN - ATT_WIN:], NEG_INF)
        for first in range(ATT_STARTS):
            bias_ref[first, e] = jnp.where(key + (first * ATT_Q - ATT_PAD) >= 0, bias, NEG_INF)
        bias_ref[ATT_STARTS, e] = bias


ATT_PAIRS = 2
ATT_COLS = ATT_PAIRS * LANES


def _att_specs(batch, seq):
    ni = seq // ATT_Q
    q_spec = pl.BlockSpec((ATT_Q, ATT_COLS), lambda g, b, i: (b * ni + i, C_AQ // ATT_COLS + g))
    k_spec = pl.BlockSpec((seq, ATT_COLS), lambda g, b, i: (b, C_AK // ATT_COLS + g))
    v_spec = pl.BlockSpec((seq, ATT_COLS), lambda g, b, i: (b, C_AV // ATT_COLS + g))
    w_spec = pl.BlockSpec((ATT_PAIRS, 2, BIAS_LEN), lambda g, b, i: (g, 0, 0))
    b_spec = pl.BlockSpec((ATT_PAIRS, ATT_STARTS + 1, 2, ATT_Q, ATT_WIN), lambda g, b, i: (g, 0, 0, 0, 0))
    pad = pltpu.VMEM((seq + ATT_PAD, ATT_COLS), BF16)
    return ni, q_spec, k_spec, v_spec, w_spec, b_spec, pad


def _att_bias_tiles(wvec, exchange):
    (tiles,), moved = _call(
        lambda w_ref, o_ref: _att_bias(w_ref, o_ref), name="att_bias", grid=(ATT_HEADS // 2,),
        in_specs=[pl.BlockSpec((None, 2, BIAS_LEN), lambda hp: (hp, 0, 0))],
        out_specs=[pl.BlockSpec((None, ATT_STARTS + 1, 2, ATT_Q, ATT_WIN), lambda hp: (hp, 0, 0, 0, 0))],
        out_shape=[jax.ShapeDtypeStruct((ATT_HEADS // 2, ATT_STARTS + 1, 2, ATT_Q, ATT_WIN), F32)],
        semantics=("parallel",), args=(wvec,), exchange=exchange)
    return tiles, moved


def _att_pad(src_ref, pad_ref):
    pad_ref[:ATT_PAD, :] = jnp.zeros((ATT_PAD, ATT_COLS), BF16)
    pad_ref[ATT_PAD:, :] = src_ref[...]


def _att_head(q2, sel):
    return jnp.where(sel, q2, jnp.zeros_like(q2)) * 0.125


def _att_softmax_rows(s_ref, bias_ref, rows):
    s = s_ref[rows, :] + bias_ref[rows, :]
    ex = jnp.exp(s - jnp.max(s, axis=-1, keepdims=True))
    return ex, 1.0 / jnp.sum(ex, axis=-1, keepdims=True)


def _att_fwd(proj, bias, batch, seq, exchange):
    ni, q_spec, k_spec, v_spec, _, b_spec, pad = _att_specs(batch, seq)

    def body(q_ref, k_ref, v_ref, bias_ref, o_ref, kp_ref, vp_ref, s_ref, e_ref):
        i = pl.program_id(2)

        @pl.when(i == 0)
        def _():
            _att_pad(k_ref, kp_ref)
            _att_pad(v_ref, vp_ref)

        win = pl.ds(pl.multiple_of(i * ATT_Q, ATT_Q), ATT_WIN)
        lo = lax.broadcasted_iota(jnp.int32, (1, LANES), 1) < 64
        start = jnp.minimum(i, ATT_STARTS)
        for pair in range(ATT_PAIRS):
            cols = slice(pair * LANES, (pair + 1) * LANES)
            k2, v2, q2 = kp_ref[win, cols], vp_ref[win, cols], q_ref[:, cols]
            out = jnp.zeros((ATT_Q, LANES), F32)
            for e in range(2):
                h = 2 * pair + e
                sel = lo if e == 0 else jnp.logical_not(lo)
                s_ref[h] = _dot_nt(_att_head(q2, sel), k2)
                rsum = []
                for c in range(ATT_Q // ATT_ROWS):
                    rows = slice(c * ATT_ROWS, (c + 1) * ATT_ROWS)
                    ex, r = _att_softmax_rows(s_ref.at[h], bias_ref.at[pair, start, e], rows)
                    e_ref[h, rows, :] = ex.astype(BF16)
                    rsum.append(r)
                out = out + _dot(e_ref[h], jnp.where(sel, v2, jnp.zeros_like(v2))) * jnp.concatenate(rsum, axis=0)
            o_ref[:, cols] = out.astype(BF16)

    heads = 2 * ATT_PAIRS
    return _call(
        body, name="att_fwd", grid=(ATT_HEADS // heads, batch, ni),
        in_specs=[q_spec, k_spec, v_spec, b_spec],
        out_specs=[pl.BlockSpec((ATT_Q, ATT_COLS), lambda g, b, i: (b * ni + i, g))],
        out_shape=[jax.ShapeDtypeStruct((batch * seq, ATT_HEADS * 64), BF16)],
        scratch=[pad, pad, pltpu.VMEM((heads, ATT_Q, ATT_WIN), F32), pltpu.VMEM((heads, ATT_Q, ATT_WIN), BF16)],
        semantics=("arbitrary", "arbitrary", "arbitrary"), args=(proj, proj, proj, bias), exchange=exchange)


GL_HALF = 512


def _gl_specs(tm):
    return [pl.BlockSpec((tm, GL_HALF), lambda i, c=C_GL // GL_HALF + j: (i, c)) for j in range(4)]


def _gates(gl_refs, b_ref):
    logits = [ref[...].astype(F32) for ref in gl_refs]
    gr = _sigmoid(jnp.concatenate(logits[:2], axis=1) + b_ref[:, :D_MODEL])
    ga = _sigmoid(jnp.concatenate(logits[2:], axis=1) + b_ref[:, D_MODEL:])
    return gr, ga


def _whole(a):
    return pl.BlockSpec(a.shape, lambda i: (0,) * a.ndim)


def _mix_out_fwd(gro, ao, proj, b_gate, w_ret, w_att_t, x2, w_out, g2):
    t = gro.shape[0]
    tm = min(256, t)

    def body(gro_ref, ao_ref, gl0, gl1, gl2, gl3, b_ref, wr_ref, wa_ref, x_ref, wo_ref, g_ref,
             z_ref, yr_ref, ya_ref, h_ref, hn_ref):
        yr = _dot(gro_ref[...], wr_ref[...])
        ya = _dot_nt(ao_ref[...], wa_ref[...])
        yr_ref[...] = yr.astype(BF16)
        ya_ref[...] = ya.astype(BF16)
        gr, ga = _gates((gl0, gl1, gl2, gl3), b_ref)
        z = (gr * yr + ga * ya).astype(BF16)
        z_ref[...] = z
        h = x_ref[...] + _dot(z, wo_ref[...])
        h_ref[...] = h
        r = lax.rsqrt(jnp.mean(h * h, axis=-1, keepdims=True) + EPS)
        hn_ref[...] = (h * r * g_ref[...]).astype(BF16)

    row = pl.BlockSpec((tm, D_MODEL), lambda i: (i, 0))
    return pl.pallas_call(
        body, name="mix_out_fwd", grid=(t // tm,),
        in_specs=[row, pl.BlockSpec((tm, 512), lambda i: (i, 0)), *_gl_specs(tm),
                  _whole(b_gate), _whole(w_ret), _whole(w_att_t), row, _whole(w_out), _whole(g2)],
        out_specs=[row] * 5,
        out_shape=[jax.ShapeDtypeStruct((t, D_MODEL), BF16)] * 3
                  + [jax.ShapeDtypeStruct((t, D_MODEL), F32), jax.ShapeDtypeStruct((t, D_MODEL), BF16)],
        compiler_params=_params("parallel"),
    )(gro, ao, proj, proj, proj, proj, b_gate, w_ret, w_att_t, x2, w_out, g2)


def _col_chunks(width, chunk=384):
    return [slice(lo, min(lo + chunk, width)) for lo in range(0, width, chunk)]


def _ffn_up(hn, wg_t, wu_t):
    t = hn.shape[0]
    tm, tn = min(512, t), D_FF // 2

    def body(h_ref, wg_ref, wu_ref, g_ref, u_ref, a_ref):
        g = _dot_nt(h_ref[...], wg_ref[...])
        u = _dot_nt(h_ref[...], wu_ref[...])
        g_ref[...] = g.astype(BF16)
        u_ref[...] = u.astype(BF16)
        a_ref[...] = (g * _sigmoid(g) * u).astype(BF16)

    w_spec = pl.BlockSpec((tn, D_MODEL), lambda j, i: (j, 0))
    out = pl.BlockSpec((tm, tn), lambda j, i: (i, j))
    return pl.pallas_call(
        body, name="ffn_up", grid=(D_FF // tn, t // tm),
        in_specs=[pl.BlockSpec((tm, D_MODEL), lambda j, i: (i, 0)), w_spec, w_spec],
        out_specs=[out, out, out],
        out_shape=[jax.ShapeDtypeStruct((t, D_FF), BF16)] * 3,
        compiler_params=_params("parallel", "parallel"),
    )(hn, wg_t, wu_t)


def _ffn_down_loss(a, h1, tgt, w_down, g3):
    t = a.shape[0]
    tm = min(512, t)

    def body(a_ref, h_ref, t_ref, w_ref, g_ref, dh_ref, dhb_ref, loss_ref, dg_ref):
        @pl.when(pl.program_id(0) == 0)
        def _():
            loss_ref[...] = jnp.zeros_like(loss_ref)
            dg_ref[...] = jnp.zeros_like(dg_ref)

        g = g_ref[...]
        h2 = h_ref[...] + _dot(a_ref[...], w_ref[...])
        r = lax.rsqrt(jnp.mean(h2 * h2, axis=-1, keepdims=True) + EPS)
        err = h2 * r * g - t_ref[...]
        loss_ref[...] += jnp.sum(err * err) * (0.5 / D_MODEL)
        dy = err * (1.0 / D_MODEL)
        dh, dg_rows = _rms_bwd(h2, g, dy)
        dg_ref[...] += jnp.sum(dg_rows, axis=0, keepdims=True)
        dh_ref[...] = dh
        dhb_ref[...] = dh.astype(BF16)

    row = pl.BlockSpec((tm, D_MODEL), lambda i: (i, 0))
    vec = pl.BlockSpec((1, D_MODEL), lambda i: (0, 0))
    return pl.pallas_call(
        body, name="ffn_down_loss", grid=(t // tm,),
        in_specs=[pl.BlockSpec((tm, D_FF), lambda i: (i, 0)), row, row,
                  pl.BlockSpec((D_FF, D_MODEL), lambda i: (0, 0)), vec],
        out_specs=[row, row, pl.BlockSpec((1, LANES), lambda i: (0, 0)), vec],
        out_shape=[jax.ShapeDtypeStruct((t, D_MODEL), F32), jax.ShapeDtypeStruct((t, D_MODEL), BF16),
                   jax.ShapeDtypeStruct((1, LANES), F32), jax.ShapeDtypeStruct((1, D_MODEL), F32)],
        compiler_params=_params("arbitrary"),
    )(a, h1, tgt, w_down, g3)


def _ffn_bwd_act(dh2b, w_down, g_act, u_act, exchange):
    t = dh2b.shape[0]
    tm, tn = min(512, t), D_FF // 2

    def body(d_ref, w_ref, g_ref, u_ref, dg_ref, du_ref):
        d = d_ref[...]
        for cols in _col_chunks(tn):
            da = _dot_nt(d, w_ref[cols, :])
            g = g_ref[:, cols].astype(F32)
            u = u_ref[:, cols].astype(F32)
            sg = _sigmoid(g)
            dg_ref[:, cols] = (da * u * sg * (1.0 + g * (1.0 - sg))).astype(BF16)
            du_ref[:, cols] = (da * g * sg).astype(BF16)

    blk = pl.BlockSpec((tm, tn), lambda j, i: (i, j))
    return _call(
        body, name="ffn_bwd_act", grid=(D_FF // tn, t // tm),
        in_specs=[pl.BlockSpec((tm, D_MODEL), lambda j, i: (i, 0)),
                  pl.BlockSpec((tn, D_MODEL), lambda j, i: (j, 0)), blk, blk],
        out_specs=[blk, blk],
        out_shape=[jax.ShapeDtypeStruct((t, D_FF), BF16)] * 2,
        semantics=("parallel", "parallel"), args=(dh2b, w_down, g_act, u_act), exchange=exchange)


def _ffn_bwd_in(dg, du, wg_t, wu_t, h1, dh2, g2, exchange):
    t = dg.shape[0]
    tm = min(256, t)

    def body(dg_ref, du_ref, wg_ref, wu_ref, h_ref, d2_ref, g_ref, dh_ref, dhb_ref, gn_ref):
        @pl.when(pl.program_id(0) == 0)
        def _():
            gn_ref[...] = jnp.zeros_like(gn_ref)

        dhn = _dot(dg_ref[...], wg_ref[...]) + _dot(du_ref[...], wu_ref[...])
        dx, dg_rows = _rms_bwd(h_ref[...], g_ref[...], dhn)
        dh = d2_ref[...] + dx
        dh_ref[...] = dh
        dhb_ref[...] = dh.astype(BF16)
        gn_ref[...] += jnp.sum(dg_rows, axis=0, keepdims=True)

    act = pl.BlockSpec((tm, D_FF), lambda i: (i, 0))
    row = pl.BlockSpec((tm, D_MODEL), lambda i: (i, 0))
    return _call(
        body, name="ffn_bwd_in", grid=(t // tm,),
        in_specs=[act, act, _whole(wg_t), _whole(wu_t), row, row, _whole(g2)],
        out_specs=[row, row, _whole(g2)],
        out_shape=[jax.ShapeDtypeStruct((t, D_MODEL), F32), jax.ShapeDtypeStruct((t, D_MODEL), BF16),
                   jax.ShapeDtypeStruct((1, D_MODEL), F32)],
        semantics=("arbitrary",), args=(dg, du, wg_t, wu_t, h1, dh2, g2), exchange=exchange)


def _mix_bwd(dh1b, w_out, proj, b_gate, y_ret, y_att, w_ret, w_att_t):
    t = dh1b.shape[0]
    tm = min(256, t)

    def body(d_ref, wo_ref, gl0, gl1, gl2, gl3, b_ref, yr_ref, ya_ref, wr_ref, wa_ref,
             dyr_ref, dya_ref, dglr_ref, dgla_ref, dgro_ref, dao_ref, db_ref):
        @pl.when(pl.program_id(0) == 0)
        def _():
            db_ref[...] = jnp.zeros_like(db_ref)

        dz = _dot_nt(d_ref[...], wo_ref[...])
        gr, ga = _gates((gl0, gl1, gl2, gl3), b_ref)
        dyr = (dz * gr).astype(BF16)
        dya = (dz * ga).astype(BF16)
        dyr_ref[...] = dyr
        dya_ref[...] = dya
        dglr = dz * yr_ref[...].astype(F32) * gr * (1.0 - gr)
        dgla = dz * ya_ref[...].astype(F32) * ga * (1.0 - ga)
        dglr_ref[...] = dglr.astype(BF16)
        dgla_ref[...] = dgla.astype(BF16)
        db_ref[:, :D_MODEL] += jnp.sum(dglr, axis=0, keepdims=True)
        db_ref[:, D_MODEL:] += jnp.sum(dgla, axis=0, keepdims=True)
        dgro_ref[...] = _dot_nt(dyr, wr_ref[...]).astype(BF16)
        dao_ref[...] = _dot(dya, wa_ref[...]).astype(BF16)

    row = pl.BlockSpec((tm, D_MODEL), lambda i: (i, 0))
    half = pl.BlockSpec((tm, 512), lambda i: (i, 0))
    return pl.pallas_call(
        body, name="mix_bwd", grid=(t // tm,),
        in_specs=[row, _whole(w_out), *_gl_specs(tm), _whole(b_gate), row, row, _whole(w_ret), _whole(w_att_t)],
        out_specs=[row, row, row, row, row, half, _whole(b_gate)],
        out_shape=[jax.ShapeDtypeStruct((t, D_MODEL), BF16)] * 5
                  + [jax.ShapeDtypeStruct((t, 512), BF16), jax.ShapeDtypeStruct(b_gate.shape, F32)],
        compiler_params=_params("arbitrary"),
    )(dh1b, w_out, proj, proj, proj, proj, b_gate, y_ret, y_att, w_ret, w_att_t)


def _ret_bwd(dgro, proj, o_ret, qr, kr, cs, sn, lg_arr, batch, seq, exchange):
    t = batch * seq
    nt = seq // RET_TILE

    def one_head(dgro_ref, rg_ref, o_ref, qr_ref, kr_ref, v_ref, cs_ref, sn_ref, lg_ref,
                 dq_ref, dk_ref, dv_ref, drg_ref, do_ref, st_ref):
        lg = lg_ref[:, 0:1]
        inside, q_dec, k_dec, tile_dec = _decay(lg)

        state = jnp.zeros((RET_KEY_DIM, RET_VAL_DIM), F32)
        for i in range(nt - 1):
            rows = slice(i * RET_TILE, (i + 1) * RET_TILE)
            state = state * tile_dec + _dot_tn(_scaled(kr_ref[rows, :], k_dec), v_ref[rows, :])
            st_ref[i + 1] = state.astype(BF16)

        for i in range(nt):
            rows = slice(i * RET_TILE, (i + 1) * RET_TILE)
            o = o_ref[rows, :]
            xc = o - jnp.mean(o, axis=-1, keepdims=True)
            rs = lax.rsqrt(jnp.mean(xc * xc, axis=-1, keepdims=True) + EPS)
            nrm = xc * rs
            rg = rg_ref[rows, :].astype(F32)
            sg = _sigmoid(rg)
            dg = dgro_ref[rows, :].astype(F32)
            drg_ref[rows, :] = (dg * nrm * sg * (1.0 + rg * (1.0 - sg))).astype(BF16)
            dn = dg * rg * sg
            do = rs * (dn - jnp.mean(dn, axis=-1, keepdims=True)
                       - nrm * jnp.mean(dn * nrm, axis=-1, keepdims=True))
            do_ref[rows, :] = do.astype(BF16)

        dstate = jnp.zeros((RET_KEY_DIM, RET_VAL_DIM), F32)
        for i in reversed(range(nt)):
            rows = slice(i * RET_TILE, (i + 1) * RET_TILE)
            qi, ki, vi, doi = qr_ref[rows, :], kr_ref[rows, :], v_ref[rows, :], do_ref[rows, :]
            p = (_dot_nt(qi, ki) * inside).astype(BF16)
            dp = (_dot_nt(doi, vi) * inside).astype(BF16)
            dq = _dot(dp, ki)
            dk = _dot_tn(dp, qi)
            dv = _dot_tn(p, doi)
            if i > 0:
                dq = dq + _dot_nt(doi, st_ref[i]) * q_dec
            if i < nt - 1:
                dsb = dstate.astype(BF16)
                dk = dk + _dot_nt(vi, dsb) * k_dec
                dv = dv + _dot(_scaled(ki, k_dec), dsb)
            if i > 0:
                dstate = dstate * tile_dec + _dot_tn(_scaled(qi, q_dec), doi)
            dq_ref[rows, :] = (dq * cs_ref[rows, :] - pltpu.roll(dq, 64, 1) * sn_ref[rows, :]).astype(BF16)
            dk = (dk * cs_ref[rows, :] - pltpu.roll(dk, 64, 1) * sn_ref[rows, :]) * (RET_KEY_DIM ** -0.5)
            dk_ref[rows, :] = dk.astype(BF16)
            dv_ref[rows, :] = dv.astype(BF16)

    key, val, tab, lgs = _ret_specs(seq)
    return _call(
        _per_head(one_head, "vvvkkvsshkkvvhh"), name="ret_bwd", grid=(batch, RET_HEADS // RET_GROUP),
        in_specs=[val(0), val(C_RG), val(0), key(0), key(0), val(C_RV), tab, tab, lgs],
        out_specs=[key(0), key(0), val(0), val(0)],
        out_shape=[jax.ShapeDtypeStruct((t, RET_HEADS * RET_KEY_DIM), BF16)] * 2
                  + [jax.ShapeDtypeStruct((t, RET_HEADS * RET_VAL_DIM), BF16)] * 2,
        scratch=[pltpu.VMEM((RET_GROUP, seq, RET_VAL_DIM), BF16),
                 pltpu.VMEM((RET_GROUP, nt, RET_KEY_DIM, RET_VAL_DIM), BF16)],
        semantics=("parallel", "parallel"), args=(dgro, proj, o_ret, qr, kr, proj, cs, sn, lg_arr),
        exchange=exchange)


def _att_bwd(proj, bias, dao, batch, seq, exchange):
    ni, q_spec, k_spec, v_spec, w_spec, b_spec, pad = _att_specs(batch, seq)
    t = batch * seq

    def body(q_ref, k_ref, v_ref, bias_ref, do_ref, dq_ref, dk_ref, dv_ref, dw_ref,
             dbias_ref, dk_acc, dv_acc, kp_ref, vp_ref, s_ref, dp_ref, e_ref, ds_ref):
        b, i = pl.program_id(1), pl.program_id(2)

        @pl.when((b == 0) & (i == 0))
        def _():
            dbias_ref[...] = jnp.zeros_like(dbias_ref)

        @pl.when(i == 0)
        def _():
            _att_pad(k_ref, kp_ref)
            _att_pad(v_ref, vp_ref)
            dk_acc[...] = jnp.zeros_like(dk_acc)
            dv_acc[...] = jnp.zeros_like(dv_acc)

        win = pl.ds(pl.multiple_of(i * ATT_Q, ATT_Q), ATT_WIN)
        lo = lax.broadcasted_iota(jnp.int32, (1, LANES), 1) < 64
        start = jnp.minimum(i, ATT_STARTS)
        for pair in range(ATT_PAIRS):
            cols = slice(pair * LANES, (pair + 1) * LANES)
            k2, v2, q2, do2 = kp_ref[win, cols], vp_ref[win, cols], q_ref[:, cols], do_ref[:, cols]
            dq = jnp.zeros((ATT_Q, LANES), F32)
            dk = jnp.zeros((LANES, ATT_WIN), F32)
            dv = jnp.zeros((LANES, ATT_WIN), F32)
            for e in range(2):
                h = 2 * pair + e
                sel = lo if e == 0 else jnp.logical_not(lo)
                qm = _att_head(q2, sel)
                dom = jnp.where(sel, do2, jnp.zeros_like(do2))
                s_ref[h] = _dot_nt(qm, k2)
                dp_ref[h] = _dot_nt(dom, v2)
                rsum = []
                for c in range(ATT_Q // ATT_ROWS):
                    rows = slice(c * ATT_ROWS, (c + 1) * ATT_ROWS)
                    ex, r = _att_softmax_rows(s_ref.at[h], bias_ref.at[pair, start, e], rows)
                    dp = dp_ref[h, rows, :]
                    mean = jnp.sum(dp * ex, axis=-1, keepdims=True) * r
                    ds = ex * ((dp - mean) * r)
                    dbias_ref[h, rows, :] += ds
                    ds_ref[h, rows, :] = ds.astype(BF16)
                    e_ref[h, rows, :] = ex.astype(BF16)
                    rsum.append(r)
                dq = dq + _dot(ds_ref[h], jnp.where(sel, k2, jnp.zeros_like(k2)))
                dk = dk + _dot_tn(qm, ds_ref[h])
                dv = dv + _dot_tn((dom.astype(F32) * jnp.concatenate(rsum, axis=0)).astype(BF16), e_ref[h])
            dq_ref[:, cols] = (dq * 0.125).astype(BF16)
            dk_acc[cols, win] += dk
            dv_acc[cols, win] += dv

        @pl.when(i == ni - 1)
        def _():
            dk_ref[...] = dk_acc[:, ATT_PAD:].T.astype(BF16)
            dv_ref[...] = dv_acc[:, ATT_PAD:].T.astype(BF16)

        @pl.when((b == batch - 1) & (i == ni - 1))
        def _():
            n_i = lax.broadcasted_iota(jnp.int32, (ATT_Q, BIAS_LEN), 0)
            for h in range(heads):
                xw = jnp.concatenate([jnp.zeros((ATT_Q, BIAS_LEN - ATT_WIN), F32), dbias_ref[h]], axis=1)
                for bit in range(8):
                    xw = jnp.where(((n_i >> bit) & 1) == 1, pltpu.roll(xw, BIAS_LEN - (1 << bit), 1), xw)
                dw_ref[h // 2, h % 2:h % 2 + 1, :] = jnp.sum(xw, axis=0, keepdims=True)

    heads = 2 * ATT_PAIRS
    seq_blk = pl.BlockSpec((seq, ATT_COLS), lambda g, b, i: (b, g))
    q_out = pl.BlockSpec((ATT_Q, ATT_COLS), lambda g, b, i: (b * ni + i, g))
    tile_f32, tile_bf16 = pltpu.VMEM((heads, ATT_Q, ATT_WIN), F32), pltpu.VMEM((heads, ATT_Q, ATT_WIN), BF16)
    acc = pltpu.VMEM((ATT_COLS, seq + ATT_PAD), F32)
    return _call(
        body, name="att_bwd", grid=(ATT_HEADS // heads, batch, ni),
        in_specs=[q_spec, k_spec, v_spec, b_spec, q_out],
        out_specs=[q_out, seq_blk, seq_blk, w_spec],
        out_shape=[jax.ShapeDtypeStruct((t, 512), BF16)] * 3
                  + [jax.ShapeDtypeStruct((ATT_HEADS // 2, 2, BIAS_LEN), F32)],
        scratch=[tile_f32, acc, acc, pad, pad, tile_f32, tile_f32, tile_bf16, tile_bf16],
        semantics=("arbitrary", "arbitrary", "arbitrary"), args=(proj, proj, proj, bias, dao), exchange=exchange,
        vmem=VMEM_LIMIT_ATT_BWD)


def _rms_in_bwd(x2, dxn, dh1, g1):
    t = x2.shape[0]
    tm = min(512, t)

    def body(x_ref, d_ref, h_ref, g_ref, dx_ref, dg_ref):
        @pl.when(pl.program_id(0) == 0)
        def _():
            dg_ref[...] = jnp.zeros_like(dg_ref)

        dx, dg_rows = _rms_bwd(x_ref[...], g_ref[...], d_ref[...])
        dx_ref[...] = h_ref[...] + dx
        dg_ref[...] += jnp.sum(dg_rows, axis=0, keepdims=True)

    row = pl.BlockSpec((tm, D_MODEL), lambda i: (i, 0))
    vec = pl.BlockSpec((1, D_MODEL), lambda i: (0, 0))
    return pl.pallas_call(
        body, name="rms_in_bwd", grid=(t // tm,),
        in_specs=[row, row, row, vec], out_specs=[row, vec],
        out_shape=[jax.ShapeDtypeStruct((t, D_MODEL), F32), jax.ShapeDtypeStruct((1, D_MODEL), F32)],
        compiler_params=_params("arbitrary"),
    )(x2, dxn, dh1, g1)


def _pack_small(dg1, dbr, dba, dg2, dg3, dw, loss):
    def body(a_ref, b_ref, c_ref, d_ref, e_ref, w_ref, l_ref, o_ref):
        o_ref[...] = jnp.zeros_like(o_ref)
        for r, ref in enumerate((a_ref, b_ref, c_ref, d_ref, e_ref)):
            o_ref[r:r + 1, :] = ref[...]
        o_ref[5:6, 0:LANES] = l_ref[...]
        for hp in range(ATT_HEADS // 2):
            o_ref[8 + 2 * hp:10 + 2 * hp, :] = w_ref[hp]

    return pl.pallas_call(body, name="pack_small",
                          out_shape=jax.ShapeDtypeStruct((16, D_MODEL), F32))(dg1, dbr, dba, dg2, dg3, dw, loss)


def _rotary_tables(seq):
    freqs = ROPE_BASE ** (-jnp.arange(0, RET_KEY_DIM, 2, dtype=F32) / RET_KEY_DIM)
    ang = jnp.arange(seq, dtype=F32)[:, None] * freqs[None, :]
    cos, sin = jnp.cos(ang), jnp.sin(ang)
    return jnp.concatenate([cos, cos], axis=1), jnp.concatenate([-sin, sin], axis=1)


def _bias_rows(rel_bias):
    n_far = BIAS_LEN - ATT_Q - MAX_REL + 1
    n_near = BIAS_LEN - n_far - (N_REL - 2)
    w = jnp.concatenate([jnp.broadcast_to(rel_bias[:, N_REL - 1:], (ATT_HEADS, n_far)),
                         rel_bias[:, 1:N_REL - 1][:, ::-1],
                         jnp.broadcast_to(rel_bias[:, :1], (ATT_HEADS, n_near))], axis=1)
    return w.reshape(ATT_HEADS // 2, 2, BIAS_LEN)


def _bias_rows_bwd(dw):
    n_far = BIAS_LEN - ATT_Q - MAX_REL + 1
    mid = dw[:, n_far:n_far + N_REL - 2][:, ::-1]
    return jnp.concatenate([jnp.sum(dw[:, n_far + N_REL - 2:], axis=1, keepdims=True), mid,
                            jnp.sum(dw[:, :n_far], axis=1, keepdims=True)], axis=1)


def _step(x, tgt, norm_mix, b_gate, norm_ffn, norm_final, rel_bias_shard, shard):
    batch, seq, _ = x.shape
    t = batch * seq
    n_rb = rel_bias_shard.shape[-1]
    x2, tgt2 = x.reshape(t, D_MODEL), tgt.reshape(t, D_MODEL)
    g3 = norm_final.reshape(1, D_MODEL)
    cs, sn = _rotary_tables(seq)
    lg = np.log(1.0 - 2.0 ** (-5.0 - np.arange(RET_HEADS, dtype=np.float32))).astype(np.float32)
    lg_arr = jnp.asarray(np.broadcast_to(lg[:, None, None], (RET_HEADS, 1, LANES)))

    def gather(*names):
        return _ChipGather([shard[nm] for nm in names])

    def scatter(*grads):
        return _Exchange(grads, scatter=True)

    rb_pad = jnp.pad(rel_bias_shard, ((0, 0), (0, LANES - n_rb)))
    (xn,), (w_in_half, rb_full) = _rms_fwd(x2, norm_mix,
                                           _ChipGather([shard["w_in_t"], rb_pad], parts=[(0, 2), (0, 1)]))
    rb_full = rb_full.reshape(N_DEV, ATT_HEADS, LANES)[:, :, :n_rb]
    bias, (w_in_t,) = _att_bias_tiles(
        _bias_rows(jnp.transpose(rb_full, (1, 0, 2)).reshape(ATT_HEADS, N_DEV * n_rb)),
        _ChipGather([shard["w_in_t"]], parts=[(1, 2)], into=[w_in_half]))
    proj, (w_ret, w_att_t, w_out, w_gate_t) = _mm(
        xn, w_in_t, tb=True, out_dtype=BF16, tm=1024, tn=1664, tk=1024, name="proj",
        exchange=gather("w_ret", "w_att_t", "w_out", "w_gate_t"))
    (gro, o_ret, qr, kr), _ = _ret_fwd(proj, cs, sn, lg_arr, batch, seq, None)
    (ao,), (w_up_t, w_down) = _att_fwd(proj, bias, batch, seq, gather("w_up_t", "w_down"))
    z, y_ret, y_att, h1, hn = _mix_out_fwd(gro, ao, proj, b_gate, w_ret, w_att_t, x2, w_out, norm_ffn)
    g_act, u_act, a_act = _ffn_up(hn, w_gate_t, w_up_t)
    dh2, dh2b, loss, dg3 = _ffn_down_loss(a_act, h1, tgt2, w_down, g3)

    wg = dict(out_dtype=BF16, tn=1024, ta=True)
    slots = {}
    dw_down = _mm(a_act, dh2b, tm=1408, tk=2048, name="dw_down", **wg)
    (d_gact, d_uact), _ = _ffn_bwd_act(dh2b, w_down, g_act, u_act, None)
    dw_gate = _mm(d_gact, hn, tm=1408, tk=2048, name="dw_gate", **wg)
    dw_up = _mm(d_uact, hn, tm=1408, tk=2048, name="dw_up", **wg)
    (dh1, dh1b, dg2), (slots["w_down"],) = _ffn_bwd_in(d_gact, d_uact, w_gate_t, w_up_t, h1, dh2, norm_ffn,
                                                     scatter(dw_down))
    dw_out = _mm(z, dh1b, tm=1024, tk=2048, name="dw_out", **wg)
    dyr, dya, dglr, dgla, dgro, dao, db = _mix_bwd(dh1b, w_out, proj, b_gate, y_ret, y_att, w_ret, w_att_t)
    dw_ret = _mm(gro, dyr, tm=1024, tk=2048, name="dw_ret", **wg)
    dw_att = _mm(dya, ao, tm=1024, tk=2048, name="dw_att", **wg)
    (drq, drk, drv, drg), _ = _ret_bwd(dgro, proj, o_ret, qr, kr, cs, sn, lg_arr, batch, seq, None)
    (daq, dak, dav, dw), (slots["w_gate_t"], slots["w_out"], slots["w_ret"], slots["w_att_t"]) = _att_bwd(
        proj, bias, dao, batch, seq, scatter(dw_gate, dw_out, dw_ret, dw_att))
    dproj = [drq, drk, drv, drg, daq, dak, dav, dglr, dgla]
    dw_in, (slots["w_up_t"],) = _mm_pieces(dproj, xn, ta=True, out_dtype=BF16, tm=512, tn=1024, tk=1024,
                                           name="dw_in", exchange=scatter(dw_up))
    (dw_in_sibling,) = _alone(_PairSwap([dw_in]), "swap_w_in")
    dw_in_pairs = _pair_add(dw_in, dw_in_sibling, "pair_w_in")
    dxn, (slots["w_in_t"],) = _mm_pieces(dproj, w_in_t, ta=False, out_dtype=F32, tm=1024, tn=1024, tk=512, name="dxn",
                                  exchange=_ChipScatter([dw_in_pairs]))
    dx, dg1 = _rms_in_bwd(x2, dxn, dh1, norm_mix)
    small = _pack_small(dg1, db[:, :D_MODEL], db[:, D_MODEL:], dg2, dg3, dw, loss)
    (small_slots,) = _alone(_ChipGather([small]), "gather_small")
    return dx.reshape(batch, seq, D_MODEL), slots, small_slots.reshape(N_DEV, 16, D_MODEL)


def _row_tile(r, c):
    return max(d for d in range(16, r + 1, 16) if r % d == 0 and (d * c <= 256 * 1024 or d == 16))


def _pair_add(grad, got, name):
    _, r, c = got.shape
    tr = r
    core = lax.axis_index("c").astype(jnp.int32).reshape(1)

    def body(core_ref, g_ref, a_ref, o_ref):
        o_ref[...] = (g_ref[...].astype(F32) + a_ref[...].astype(F32)).astype(o_ref.dtype)

    blk = pl.BlockSpec((None, tr, c), lambda q, i, core_ref: (q, i, 0))
    return pl.pallas_call(
        body, name=name,
        grid_spec=pltpu.PrefetchScalarGridSpec(
            num_scalar_prefetch=1, grid=(4, r // tr),
            in_specs=[pl.BlockSpec((None, None, tr, c), lambda q, i, core_ref: (q, core_ref[0], i, 0)), blk],
            out_specs=blk),
        out_shape=jax.ShapeDtypeStruct(got.shape, got.dtype),
        compiler_params=_params("parallel", "parallel"),
    )(core, grad.reshape(4, 2, r, c), got)


def _sum_slots(slots, name):
    n, r, c = slots.shape
    tr = _row_tile(r, c)

    def body(s_ref, o_ref):
        acc = s_ref[0].astype(F32)
        for s in range(1, n):
            acc = acc + s_ref[s].astype(F32)
        o_ref[...] = acc

    return pl.pallas_call(
        body, name=name, grid=(r // tr,),
        in_specs=[pl.BlockSpec((n, tr, c), lambda i: (0, i, 0))],
        out_specs=pl.BlockSpec((tr, c), lambda i: (i, 0)),
        out_shape=jax.ShapeDtypeStruct((r, c), F32),
        compiler_params=_params("parallel"),
    )(slots)


def _adamw_math(w, g, m, v):
    m = ADAM_B1 * m + (1.0 - ADAM_B1) * g
    v = ADAM_B2 * v + (1.0 - ADAM_B2) * (g * g)
    m_hat = m / (1.0 - ADAM_B1 ** ADAM_STEP)
    v_hat = v / (1.0 - ADAM_B2 ** ADAM_STEP)
    return -ADAM_LR * (m_hat / (jnp.sqrt(v_hat) + ADAM_EPS) + ADAM_WD * w), m, v


def _adamw(w, slots, m, v, name):
    n, r, c = slots.shape
    tr = _row_tile(r, c)

    def body(w_ref, s_ref, m_ref, v_ref, g_ref, d_ref, nm_ref, nv_ref):
        g = s_ref[0].astype(F32)
        for s in range(1, n):
            g = g + s_ref[s].astype(F32)
        g_ref[...] = g
        d_ref[...], nm_ref[...], nv_ref[...] = _adamw_math(w_ref[...], g, m_ref[...], v_ref[...])

    blk = pl.BlockSpec((tr, c), lambda i: (i, 0))
    return pl.pallas_call(
        body, name=name, grid=(r // tr,),
        in_specs=[blk, pl.BlockSpec((n, tr, c), lambda i: (0, i, 0)), blk, blk], out_specs=[blk] * 4,
        out_shape=[jax.ShapeDtypeStruct((r, c), F32)] * 4,
        compiler_params=_params("parallel"),
    )(w, slots, m, v)


def _adamw_small(ws, gs, ms, vs):
    n = len(ws)

    def body(*refs):
        for i in range(n):
            w_ref, g_ref, m_ref, v_ref = (refs[j * n + i] for j in range(4))
            d_ref, nm_ref, nv_ref = (refs[(4 + j) * n + i] for j in range(3))
            d_ref[...], nm_ref[...], nv_ref[...] = _adamw_math(w_ref[...], g_ref[...], m_ref[...], v_ref[...])

    shapes = [jax.ShapeDtypeStruct(w.shape, F32) for w in ws]
    outs = pl.pallas_call(body, name="adamw_small", out_shape=shapes * 3)(*ws, *gs, *ms, *vs)
    return outs[:n], outs[n:2 * n], outs[2 * n:]


def kernel(x, norm_mix, w_in, b_gate, rel_bias, w_ret_out, w_att_out, w_out, norm_ffn, w_ffn_gate, w_ffn_up, w_ffn_down, norm_final, loss_target, m_norm_mix, m_w_in, m_b_gate, m_rel_bias, m_w_ret_out, m_w_att_out, m_w_out, m_norm_ffn, m_w_ffn_gate, m_w_ffn_up, m_w_ffn_down, m_norm_final, v_norm_mix, v_w_in, v_b_gate, v_rel_bias, v_w_ret_out, v_w_att_out, v_w_out, v_norm_ffn, v_w_ffn_gate, v_w_ffn_up, v_w_ffn_down, v_norm_final):
    me = _index(_place())
    n_rb = rel_bias.shape[-1]

    shard = dict(w_in_t=w_in[0].T, w_gate_t=w_ffn_gate[0].T, w_up_t=w_ffn_up[0].T, w_down=w_ffn_down[0],
                 w_ret=w_ret_out[0], w_out=w_out[0], w_att_t=w_att_out[0].T)
    shard = {nm: s.astype(BF16) for nm, s in shard.items()}
    dx, slots, small_slots = _step(x, loss_target, norm_mix, b_gate, norm_ffn, norm_final, rel_bias[0], shard)
    small_sum = _sum_slots(small_slots, "sum_small")
    loss = small_sum[5, 0]

    transposed = dict(w_in="w_in_t", w_ffn_gate="w_gate_t", w_ffn_up="w_up_t", w_att_out="w_att_t")
    plain = dict(w_ffn_down="w_down", w_ret_out="w_ret", w_out="w_out")
    g = dict(
        norm_mix=small_sum[0:1], b_gate=jnp.concatenate([small_sum[1:2], small_sum[2:3]], axis=1),
        norm_ffn=small_sum[3:4], norm_final=small_sum[4:5],
        rel_bias=lax.dynamic_slice_in_dim(_bias_rows_bwd(small_sum[8:16]), me * n_rb, n_rb, axis=1),
    )
    w = dict(norm_mix=norm_mix, w_in=w_in, b_gate=b_gate, rel_bias=rel_bias, w_ret_out=w_ret_out, w_att_out=w_att_out,
             w_out=w_out, norm_ffn=norm_ffn, w_ffn_gate=w_ffn_gate, w_ffn_up=w_ffn_up, w_ffn_down=w_ffn_down,
             norm_final=norm_final)
    m = dict(norm_mix=m_norm_mix, w_in=m_w_in, b_gate=m_b_gate, rel_bias=m_rel_bias, w_ret_out=m_w_ret_out,
             w_att_out=m_w_att_out, w_out=m_w_out, norm_ffn=m_norm_ffn, w_ffn_gate=m_w_ffn_gate, w_ffn_up=m_w_ffn_up,
             w_ffn_down=m_w_ffn_down, norm_final=m_norm_final)
    v = dict(norm_mix=v_norm_mix, w_in=v_w_in, b_gate=v_b_gate, rel_bias=v_rel_bias, w_ret_out=v_w_ret_out,
             w_att_out=v_w_att_out, w_out=v_w_out, norm_ffn=v_norm_ffn, w_ffn_gate=v_w_ffn_gate, w_ffn_up=v_w_ffn_up,
             w_ffn_down=v_w_ffn_down, norm_final=v_norm_final)
    order = ("norm_mix", "w_in", "b_gate", "rel_bias", "w_ret_out", "w_att_out", "w_out", "norm_ffn",
             "w_ffn_gate", "w_ffn_up", "w_ffn_down", "norm_final")
    small_names = ("norm_mix", "b_gate", "rel_bias", "norm_ffn", "norm_final")

    def flat(a):
        return a[0] if a.ndim == 3 else a.reshape(-1, a.shape[-1])

    grad, delta, new_m, new_v = {}, {}, {}, {}
    for nm in order:
        if nm in transposed:
            res = _adamw(w[nm][0].T, slots[transposed[nm]], m[nm][0].T, v[nm][0].T, "adamw_" + nm)
            grad[nm], delta[nm], new_m[nm], new_v[nm] = (a.T[None] for a in res)
        elif nm in plain:
            res = _adamw(flat(w[nm]), slots[plain[nm]], flat(m[nm]), flat(v[nm]), "adamw_" + nm)
            grad[nm], delta[nm], new_m[nm], new_v[nm] = (a.reshape(w[nm].shape) for a in res)
    ds, nms, nvs = _adamw_small([flat(w[nm]) for nm in small_names], [g[nm] for nm in small_names],
                                [flat(m[nm]) for nm in small_names], [flat(v[nm]) for nm in small_names])
    for i, nm in enumerate(small_names):
        grad[nm], delta[nm], new_m[nm], new_v[nm] = (a.reshape(w[nm].shape) for a in (g[nm], ds[i], nms[i], nvs[i]))

    return (loss, dx, *[grad[nm] for nm in order], *[delta[nm] for nm in order],
            *[new_m[nm] for nm in order], *[new_v[nm] for nm in order])
```

```python
import numpy as np
import jax
import jax.numpy as jnp
from jax import lax
from jax.experimental import pallas as pl
from jax.experimental.pallas import tpu as pltpu

F32 = jnp.float32
BF16 = jnp.bfloat16
MESH = pl.DeviceIdType.MESH

D_MODEL = 1024
CHUNK = 64
RET_HEADS = 4
RET_KEY_DIM = 128
RET_VAL_DIM = 256
ATT_HEADS = 8
BAND_CHUNKS = 8
MAX_REL = 256
N_REL = CHUNK + MAX_REL
D_FF = 2816
N_IN = 6656
ROPE_BASE = 10000.0
EPS = 1e-6
NEG_INF = -1e30
C_RQ, C_RK, C_RV, C_RG, C_AQ, C_AK, C_AV, C_GL = 0, 512, 1024, 2048, 3072, 3584, 4096, 4608

ADAM_LR = 0.001
ADAM_B1 = 0.9
ADAM_B2 = 0.999
ADAM_EPS = 1e-08
ADAM_WD = 0.01
ADAM_STEP = 10

N_DEV = 8
LANES = 128
RET_TILE = 256
ATT_Q = 256
ATT_PAD = BAND_CHUNKS * CHUNK
ATT_WIN = ATT_PAD + ATT_Q
ATT_STARTS = ATT_PAD // ATT_Q
ATT_ROWS = 32
BIAS_LEN = 1024
VMEM_LIMIT = 48 * 1024 * 1024
VMEM_LIMIT_ATT_BWD = 56 * 1024 * 1024


def _params(*sem, vmem=VMEM_LIMIT):
    return pltpu.CompilerParams(dimension_semantics=sem, vmem_limit_bytes=vmem)


def _dot(a, b):
    return lax.dot_general(a, b, (((1,), (0,)), ((), ())), preferred_element_type=F32)


def _dot_nt(a, b):
    return lax.dot_general(a, b, (((1,), (1,)), ((), ())), preferred_element_type=F32)


def _dot_tn(a, b):
    return lax.dot_general(a, b, (((0,), (0,)), ((), ())), preferred_element_type=F32)


def _sigmoid(x):
    return 1.0 / (1.0 + jnp.exp(-x))


def _rms_bwd(x, g, dy):
    r = lax.rsqrt(jnp.mean(x * x, axis=-1, keepdims=True) + EPS)
    u = dy * g
    dx = r * u - x * (r * r * r) * jnp.mean(u * x, axis=-1, keepdims=True)
    return dx, dy * x * r


def _place():
    return lax.axis_index("x"), lax.axis_index("y"), lax.axis_index("c")


def _peer(k):
    x, y, c = _place()
    return ((1 - x) if k & 4 else x, (1 - y) if k & 2 else y, (1 - c) if k & 1 else c)


def _index(place):
    return 4 * place[0] + 2 * place[1] + place[2]


def _rows(ref, block, nrows):
    align = 16 if ref.dtype == BF16 else 8
    return ref.at[pl.ds(pl.multiple_of(block * nrows, align), nrows)]


class _Exchange:
    def __init__(self, arrays, scatter):
        self.arrays, self.scatter, self.n = list(arrays), scatter, len(arrays)

    def out_shape(self):
        if self.scatter:
            return [jax.ShapeDtypeStruct((N_DEV, a.shape[0] // N_DEV) + a.shape[1:], a.dtype) for a in self.arrays]
        return [jax.ShapeDtypeStruct((N_DEV * a.shape[0],) + a.shape[1:], a.dtype) for a in self.arrays]

    def scratch(self):
        return [pltpu.SemaphoreType.DMA((self.n, N_DEV - 1)), pltpu.SemaphoreType.DMA((self.n, N_DEV - 1)),
                pltpu.SemaphoreType.DMA((self.n,))]

    def _copies(self, ins, outs, sems):
        send_sems, recv_sems, local_sems = sems
        me = _index(_place())

        def src(w, to):
            return _rows(ins[w], to, ins[w].shape[0] // N_DEV) if self.scatter else ins[w]

        def dst(w, origin):
            return outs[w].at[origin] if self.scatter else _rows(outs[w], origin, ins[w].shape[0])

        def remote(w, k, to, origin):
            return pltpu.make_async_remote_copy(src_ref=src(w, to), dst_ref=dst(w, origin),
                                                send_sem=send_sems.at[w, k - 1], recv_sem=recv_sems.at[w, k - 1],
                                                device_id=_peer(k), device_id_type=MESH)

        pairs = [(w, k) for w in range(self.n) for k in range(1, N_DEV)]
        own = lambda: [pltpu.make_async_copy(src(w, me), dst(w, me), local_sems.at[w]) for w in range(self.n)]
        sent = lambda: [remote(w, k, _index(_peer(k)), me) for w, k in pairs]
        arriving = lambda: [remote(w, k, me, _index(_peer(k))) for w, k in pairs]
        return own, sent, arriving

    def start(self, ins, outs, sems):
        own, sent, _ = self._copies(ins, outs, sems)
        for cp in own() + sent():
            cp.start()

    def wait(self, ins, outs, sems):
        own, sent, arriving = self._copies(ins, outs, sems)
        for cp in arriving():
            cp.wait_recv()
        for cp in sent():
            cp.wait_send()
        for cp in own():
            cp.wait()


class _PairSwap:
    def __init__(self, arrays):
        self.arrays, self.n = list(arrays), len(arrays)

    def out_shape(self):
        return [jax.ShapeDtypeStruct((4, a.shape[0] // N_DEV) + a.shape[1:], a.dtype) for a in self.arrays]

    def scratch(self):
        return [pltpu.SemaphoreType.DMA((self.n, 4)), pltpu.SemaphoreType.DMA((self.n, 4))]

    def _copies(self, ins, outs, sems):
        send_sems, recv_sems = sems
        x, y, c = _place()
        return [pltpu.make_async_remote_copy(
            src_ref=_rows(ins[w], 2 * q + 1 - c, ins[w].shape[0] // N_DEV), dst_ref=outs[w].at[q],
            send_sem=send_sems.at[w, q], recv_sem=recv_sems.at[w, q],
            device_id=(x, y, 1 - c), device_id_type=MESH) for w in range(self.n) for q in range(4)]

    def start(self, ins, outs, sems):
        for cp in self._copies(ins, outs, sems):
            cp.start()

    def wait(self, ins, outs, sems):
        for cp in self._copies(ins, outs, sems):
            cp.wait()


class _ChipScatter:
    def __init__(self, arrays):
        self.arrays, self.n = list(arrays), len(arrays)

    def out_shape(self):
        return [jax.ShapeDtypeStruct(a.shape, a.dtype) for a in self.arrays]

    def scratch(self):
        return [pltpu.SemaphoreType.DMA((self.n, 3)), pltpu.SemaphoreType.DMA((self.n, 3)),
                pltpu.SemaphoreType.DMA((self.n,))]

    def _copies(self, ins, outs, sems):
        send_sems, recv_sems, local_sems = sems
        x, y, c = _place()
        mine = 2 * x + y
        sent, arriving = [], []
        for w in range(self.n):
            for k in range(1, 4):
                tx, ty = (1 - x) if k & 2 else x, (1 - y) if k & 1 else y
                other = 2 * tx + ty
                sent.append(lambda w=w, k=k, tx=tx, ty=ty, other=other: pltpu.make_async_remote_copy(
                    src_ref=ins[w].at[other], dst_ref=outs[w].at[mine], send_sem=send_sems.at[w, k - 1],
                    recv_sem=recv_sems.at[w, k - 1], device_id=(tx, ty, c), device_id_type=MESH))
                arriving.append(lambda w=w, k=k, tx=tx, ty=ty, other=other: pltpu.make_async_remote_copy(
                    src_ref=ins[w].at[mine], dst_ref=outs[w].at[other], send_sem=send_sems.at[w, k - 1],
                    recv_sem=recv_sems.at[w, k - 1], device_id=(tx, ty, c), device_id_type=MESH))
        own = [lambda w=w: pltpu.make_async_copy(ins[w].at[mine], outs[w].at[mine], local_sems.at[w])
               for w in range(self.n)]
        return own, sent, arriving

    def start(self, ins, outs, sems):
        own, sent, _ = self._copies(ins, outs, sems)
        for cp in own + sent:
            cp().start()

    def wait(self, ins, outs, sems):
        own, sent, arriving = self._copies(ins, outs, sems)
        for cp in arriving:
            cp().wait_recv()
        for cp in sent:
            cp().wait_send()
        for cp in own:
            cp().wait()


class _ChipGather:
    def __init__(self, arrays, parts=None, into=None):
        self.arrays, self.n, self.into = list(arrays), len(arrays), into
        self.parts = parts or [(0, 1)] * self.n

    def out_shape(self):
        return [jax.ShapeDtypeStruct((N_DEV * a.shape[0],) + a.shape[1:], a.dtype) for a in self.arrays]

    def scratch(self):
        return [pltpu.SemaphoreType.DMA((self.n, N_DEV - 1)), pltpu.SemaphoreType.DMA((self.n, N_DEV - 1)),
                pltpu.SemaphoreType.DMA((self.n,))]

    def _parts(self, ins, outs, sems):
        send_sems, recv_sems, local_sems = sems
        x, y, c = _place()
        me, sibling = (x, y, c), (x, y, 1 - c)
        chips = [(1 - x, y), (x, 1 - y), (1 - x, 1 - y)]

        def rows(w, place, whole):
            (index, count), r = self.parts[w], ins[w].shape[0]
            lo, size = (0, r) if whole else (index * (r // count), r // count)
            align = 16 if ins[w].dtype == BF16 else 8
            return outs[w].at[pl.ds(pl.multiple_of(_index(place) * r + lo, align), size)]

        def mine(w, whole):
            (index, count), r = self.parts[w], ins[w].shape[0]
            return ins[w] if whole or count == 1 else ins[w].at[pl.ds(index * (r // count), r // count)]

        def copy(w, k, block, to, own=False):
            whole = k == 0
            return pltpu.make_async_remote_copy(src_ref=mine(w, whole) if own else rows(w, block, whole),
                                                dst_ref=rows(w, block, whole),
                                                send_sem=send_sems.at[w, k], recv_sem=recv_sems.at[w, k],
                                                device_id=to, device_id_type=MESH)

        def local(w):
            return pltpu.make_async_copy(ins[w], rows(w, me, True), local_sems.at[w])

        return me, sibling, chips, c, copy, local, [index == 0 for index, _ in self.parts]

    def start(self, ins, outs, sems):
        me, sibling, chips, c, copy, local, places_own = self._parts(ins, outs, sems)
        for w in range(self.n):
            if places_own[w]:
                local(w).start()
                copy(w, 0, me, sibling, own=True).start()
            for j, chip in enumerate(chips):
                copy(w, 1 + j, me, (*chip, c), own=True).start()

    def wait(self, ins, outs, sems):
        me, sibling, chips, c, copy, local, places_own = self._parts(ins, outs, sems)
        for w in range(self.n):
            for j, chip in enumerate(chips):
                copy(w, 1 + j, (*chip, c), me).wait_recv()
                copy(w, 4 + j, (*chip, c), sibling).start()
        for w in range(self.n):
            if places_own[w]:
                copy(w, 0, sibling, me).wait_recv()
                copy(w, 0, me, sibling, own=True).wait_send()
                local(w).wait()
            for j, chip in enumerate(chips):
                copy(w, 4 + j, (*chip, 1 - c), me).wait_recv()
                copy(w, 1 + j, me, (*chip, c), own=True).wait_send()
                copy(w, 4 + j, (*chip, c), sibling).wait_send()


def _call(body, *, name, grid, in_specs, out_specs, out_shape, scratch=(), semantics, args, exchange=None,
          vmem=VMEM_LIMIT):
    if exchange is None:
        return pl.pallas_call(body, name=name, grid=grid, in_specs=in_specs, out_specs=out_specs, out_shape=out_shape,
                              scratch_shapes=list(scratch),
                              compiler_params=_params(*semantics, vmem=vmem))(*args), None
    n_in, n_out, n_scr, nx = len(in_specs), len(out_specs), len(scratch), exchange.n
    into = list(getattr(exchange, "into", None) or [])

    def full_body(*refs):
        ins, refs = refs[:n_in], refs[n_in:]
        x_in, refs = refs[:nx], refs[nx + len(into):]
        outs, refs = refs[:n_out], refs[n_out:]
        x_out, refs = refs[:nx], refs[nx:]
        scr, sems = refs[:n_scr], refs[n_scr:]
        first, last = True, True
        for axis, size in enumerate(grid):
            first = jnp.logical_and(first, pl.program_id(axis) == 0)
            last = jnp.logical_and(last, pl.program_id(axis) == size - 1)
        if grid:
            pl.when(first)(lambda: exchange.start(x_in, x_out, sems))
        else:
            exchange.start(x_in, x_out, sems)
        body(*ins, *outs, *scr)
        if grid:
            pl.when(last)(lambda: exchange.wait(x_in, x_out, sems))
        else:
            exchange.wait(x_in, x_out, sems)

    hbm = pl.BlockSpec(memory_space=pltpu.HBM)
    res = pl.pallas_call(
        full_body, name=name, grid=grid,
        in_specs=list(in_specs) + [hbm] * (nx + len(into)), out_specs=list(out_specs) + [hbm] * nx,
        out_shape=list(out_shape) + exchange.out_shape(),
        scratch_shapes=list(scratch) + exchange.scratch(),
        input_output_aliases={n_in + nx + w: n_out + w for w in range(len(into))},
        compiler_params=_params(*(["arbitrary"] * len(grid)), vmem=vmem),
    )(*args, *exchange.arrays, *into)
    return res[:n_out], res[n_out:]


def _alone(exchange, name):
    return _call(lambda: None, name=name, grid=(), in_specs=[], out_specs=[], out_shape=[], semantics=(),
                 args=(), exchange=exchange)[1]


def _mm(a, b, *, ta=False, tb=False, out_dtype, tm, tn, tk, name, exchange=None):
    m, k = (a.shape[1], a.shape[0]) if ta else a.shape
    n = b.shape[0] if tb else b.shape[1]
    assert k == (b.shape[1] if tb else b.shape[0])
    tm, tn, tk = min(tm, m), min(tn, n), min(tk, k)
    assert m % tm == 0 and n % tn == 0 and k % tk == 0, (name, m, n, k)
    nk = k // tk
    dims = (((0 if ta else 1,), (1 if tb else 0,)), ((), ()))

    def body(a_ref, b_ref, o_ref, *acc):
        prod = lax.dot_general(a_ref[...].astype(BF16), b_ref[...].astype(BF16), dims, preferred_element_type=F32)
        if nk == 1:
            o_ref[...] = prod.astype(o_ref.dtype)
            return
        acc_ref, kk = acc[0], pl.program_id(2)

        @pl.when(kk == 0)
        def _():
            acc_ref[...] = prod

        @pl.when((kk > 0) & (kk < nk - 1))
        def _():
            acc_ref[...] += prod

        @pl.when(kk == nk - 1)
        def _():
            o_ref[...] = (acc_ref[...] + prod).astype(o_ref.dtype)

    a_spec = (pl.BlockSpec((tk, tm), lambda i, j, kk: (kk, i)) if ta
              else pl.BlockSpec((tm, tk), lambda i, j, kk: (i, kk)))
    b_spec = (pl.BlockSpec((tn, tk), lambda i, j, kk: (j, kk)) if tb
              else pl.BlockSpec((tk, tn), lambda i, j, kk: (kk, j)))
    (out,), moved = _call(
        body, name=name, grid=(m // tm, n // tn, nk),
        in_specs=[a_spec, b_spec],
        out_specs=[pl.BlockSpec((tm, tn), lambda i, j, kk: (i, j))],
        out_shape=[pltpu.HBM((m, n), out_dtype)],
        scratch=[pltpu.VMEM((tm, tn), F32)] if nk > 1 else [],
        semantics=("parallel", "parallel", "arbitrary"), args=(a, b), exchange=exchange)
    return out if exchange is None else (out, moved)


def _mm_pieces(pieces, b, *, ta, out_dtype, tm, tn, tk, name, exchange=None):
    rows, n = pieces[0].shape[0], b.shape[1]
    step = tm if ta else tk
    assert all(p.shape[0] == rows and p.shape[1] % step == 0 for p in pieces), name
    edges = [int(e) for e in np.cumsum([0] + [p.shape[1] // step for p in pieces])]
    total = edges[-1] * step
    m, k = (total, rows) if ta else (rows, total)
    assert b.shape[0] == k and m % tm == 0 and n % tn == 0 and k % tk == 0, name
    nk, npieces = k // tk, len(pieces)
    dims = (((0 if ta else 1,), (0,)), ((), ()))
    b_resident = ta and n == tn

    def body(*refs):
        a_refs, (b_ref, o_ref, acc_ref) = refs[:npieces], refs[npieces:]
        kk = pl.program_id(2)
        pos = pl.program_id(0) if ta else kk

        @pl.when(kk == 0)
        def _():
            acc_ref[...] = jnp.zeros_like(acc_ref)

        def b_tile():
            return b_ref[pl.ds(pl.multiple_of(kk * tk, tk), tk), :] if b_resident else b_ref[...]

        for p, a_ref in enumerate(a_refs):
            @pl.when((pos >= edges[p]) & (pos < edges[p + 1]))
            def _(a_ref=a_ref):
                acc_ref[...] += lax.dot_general(a_ref[...], b_tile(), dims, preferred_element_type=F32)

        @pl.when(kk == nk - 1)
        def _():
            o_ref[...] = acc_ref[...].astype(o_ref.dtype)

    def a_spec(p):
        lo, last = edges[p], edges[p + 1] - edges[p] - 1
        if ta:
            def index(i, j, kk):
                inside = (i >= lo) & (i <= lo + last)
                return jnp.where(inside, kk, 0), jnp.clip(i - lo, 0, last)
            return pl.BlockSpec((tk, tm), index)
        return pl.BlockSpec((tm, tk), lambda i, j, kk: (i, jnp.clip(kk - lo, 0, last)))

    (out,), moved = _call(
        body, name=name, grid=(m // tm, n // tn, nk),
        in_specs=[a_spec(p) for p in range(npieces)]
                 + [pl.BlockSpec(b.shape, lambda i, j, kk: (0, 0)) if b_resident
                    else pl.BlockSpec((tk, tn), lambda i, j, kk: (kk, j))],
        out_specs=[pl.BlockSpec((tm, tn), lambda i, j, kk: (i, j))],
        out_shape=[pltpu.HBM((m, n), out_dtype)],
        scratch=[pltpu.VMEM((tm, tn), F32)],
        semantics=("parallel", "parallel", "arbitrary"), args=(*pieces, b), exchange=exchange)
    return out if exchange is None else (out, moved)


def _rms_fwd(x2, g, exchange):
    t = x2.shape[0]
    tm = min(512, t)

    def body(x_ref, g_ref, o_ref):
        x = x_ref[...]
        r = lax.rsqrt(jnp.mean(x * x, axis=-1, keepdims=True) + EPS)
        o_ref[...] = (x * r * g_ref[...]).astype(o_ref.dtype)

    return _call(
        body, name="rms_in_fwd", grid=(t // tm,),
        in_specs=[pl.BlockSpec((tm, D_MODEL), lambda i: (i, 0)), pl.BlockSpec((1, D_MODEL), lambda i: (0, 0))],
        out_specs=[pl.BlockSpec((tm, D_MODEL), lambda i: (i, 0))],
        out_shape=[jax.ShapeDtypeStruct((t, D_MODEL), BF16)],
        semantics=("parallel",), args=(x2, g), exchange=exchange)


def _decay(lg):
    row = lax.broadcasted_iota(jnp.int32, (RET_TILE, RET_TILE), 0)
    col = lax.broadcasted_iota(jnp.int32, (RET_TILE, RET_TILE), 1)
    within = jnp.exp(lg * jnp.abs(row - col).astype(F32))
    inside = jnp.where((col >> 6) <= (row >> 6), within, 0.0)
    pos = lax.broadcasted_iota(jnp.int32, (RET_TILE, 1), 0).astype(F32)
    q_dec = jnp.exp(lg * (pos + 1.0))
    k_dec = jnp.exp(lg * (RET_TILE - 1.0 - pos))
    tile_dec = jnp.exp(lg * float(RET_TILE))
    return inside, q_dec, k_dec, tile_dec


def _scaled(a_bf16, dec):
    return (a_bf16.astype(F32) * dec).astype(BF16)


RET_GROUP = 1


def _per_head(one_head, kinds):
    def body(*refs):
        for head in range(RET_GROUP):
            def cut(ref, kind):
                if kind in "kv":
                    width = RET_KEY_DIM if kind == "k" else RET_VAL_DIM
                    return ref.at[:, head * width:(head + 1) * width]
                return ref.at[head] if kind == "h" else ref
            one_head(*[cut(ref, kind) for ref, kind in zip(refs, kinds)])
    return body


def _ret_specs(seq):
    kw, vw = RET_GROUP * RET_KEY_DIM, RET_GROUP * RET_VAL_DIM
    key = lambda base: pl.BlockSpec((seq, kw), lambda b, g: (b, base // kw + g))
    val = lambda base: pl.BlockSpec((seq, vw), lambda b, g: (b, base // vw + g))
    tab = pl.BlockSpec((seq, RET_KEY_DIM), lambda b, g: (0, 0))
    lgs = pl.BlockSpec((RET_GROUP, 1, LANES), lambda b, g: (g, 0, 0))
    return key, val, tab, lgs


def _ret_fwd(proj, cs, sn, lg_arr, batch, seq, exchange):
    t = batch * seq
    nt = seq // RET_TILE

    def one_head(q_ref, k_ref, v_ref, rg_ref, cs_ref, sn_ref, lg_ref, gro_ref, o_ref, qr_ref, kr_ref):
        lg = lg_ref[:, 0:1]
        cs_t, sn_t = cs_ref[...], sn_ref[...]
        q = q_ref[...].astype(F32)
        k = k_ref[...].astype(F32)
        qr_ref[...] = (q * cs_t + pltpu.roll(q, 64, 1) * sn_t).astype(BF16)
        kr_ref[...] = ((k * cs_t + pltpu.roll(k, 64, 1) * sn_t) * (RET_KEY_DIM ** -0.5)).astype(BF16)
        inside, q_dec, k_dec, tile_dec = _decay(lg)
        state = jnp.zeros((RET_KEY_DIM, RET_VAL_DIM), F32)
        for i in range(nt):
            rows = slice(i * RET_TILE, (i + 1) * RET_TILE)
            qi, ki, vi = qr_ref[rows, :], kr_ref[rows, :], v_ref[rows, :]
            acc = _dot((_dot_nt(qi, ki) * inside).astype(BF16), vi)
            if i > 0:
                acc = acc + _dot(_scaled(qi, q_dec), state.astype(BF16))
            if i < nt - 1:
                state = state * tile_dec + _dot_tn(_scaled(ki, k_dec), vi)
            o_ref[rows, :] = acc
            xc = acc - jnp.mean(acc, axis=-1, keepdims=True)
            nrm = xc * lax.rsqrt(jnp.mean(xc * xc, axis=-1, keepdims=True) + EPS)
            rg = rg_ref[rows, :].astype(F32)
            gro_ref[rows, :] = (rg * _sigmoid(rg) * nrm).astype(BF16)

    key, val, tab, lgs = _ret_specs(seq)
    return _call(
        _per_head(one_head, "kkvvsshvvkk"), name="ret_fwd", grid=(batch, RET_HEADS // RET_GROUP),
        in_specs=[key(C_RQ), key(C_RK), val(C_RV), val(C_RG), tab, tab, lgs],
        out_specs=[val(0), val(0), key(0), key(0)],
        out_shape=[jax.ShapeDtypeStruct((t, RET_HEADS * RET_VAL_DIM), BF16),
                   jax.ShapeDtypeStruct((t, RET_HEADS * RET_VAL_DIM), F32),
                   jax.ShapeDtypeStruct((t, RET_HEADS * RET_KEY_DIM), BF16),
                   jax.ShapeDtypeStruct((t, RET_HEADS * RET_KEY_DIM), BF16)],
        semantics=("parallel", "parallel"), args=(proj, proj, proj, proj, cs, sn, lg_arr), exchange=exchange)


def _att_bias(w_ref, bias_ref):
    n_i = lax.broadcasted_iota(jnp.int32, (ATT_Q, BIAS_LEN), 0)
    qc = lax.broadcasted_iota(jnp.int32, (ATT_Q, ATT_WIN), 0) >> 6
    kc = lax.broadcasted_iota(jnp.int32, (ATT_Q, ATT_WIN), 1) >> 6
    dc = qc + BAND_CHUNKS - kc
    band = (dc >= 0) & (dc <= BAND_CHUNKS)
    key = lax.broadcasted_iota(jnp.int32, (ATT_Q, ATT_WIN), 1)
    for e in range(2):
        xw = jnp.broadcast_to(w_ref[e:e + 1, :], (ATT_Q, BIAS_LEN))
        for bit in range(8):
            xw = jnp.where(((n_i >> bit) & 1) == 1, pltpu.roll(xw, 1 << bit, 1), xw)
        bias = jnp.where(band, xw[:, BIAS_LEN - ATT_WIN:], NEG_INF)
        for first in range(ATT_STARTS):
            bias_ref[first, e] = jnp.where(key + (first * ATT_Q - ATT_PAD) >= 0, bias, NEG_INF)
        bias_ref[ATT_STARTS, e] = bias


ATT_PAIRS = 2
ATT_COLS = ATT_PAIRS * LANES


def _att_specs(batch, seq):
    ni = seq // ATT_Q
    q_spec = pl.BlockSpec((ATT_Q, ATT_COLS), lambda g, b, i: (b * ni + i, C_AQ // ATT_COLS + g))
    k_spec = pl.BlockSpec((seq, ATT_COLS), lambda g, b, i: (b, C_AK // ATT_COLS + g))
    v_spec = pl.BlockSpec((seq, ATT_COLS), lambda g, b, i: (b, C_AV // ATT_COLS + g))
    w_spec = pl.BlockSpec((ATT_PAIRS, 2, BIAS_LEN), lambda g, b, i: (g, 0, 0))
    b_spec = pl.BlockSpec((ATT_PAIRS, ATT_STARTS + 1, 2, ATT_Q, ATT_WIN), lambda g, b, i: (g, 0, 0, 0, 0))
    pad = pltpu.VMEM((seq + ATT_PAD, ATT_COLS), BF16)
    return ni, q_spec, k_spec, v_spec, w_spec, b_spec, pad


def _att_bias_tiles(wvec, exchange):
    (tiles,), moved = _call(
        lambda w_ref, o_ref: _att_bias(w_ref, o_ref), name="att_bias", grid=(ATT_HEADS // 2,),
        in_specs=[pl.BlockSpec((None, 2, BIAS_LEN), lambda hp: (hp, 0, 0))],
        out_specs=[pl.BlockSpec((None, ATT_STARTS + 1, 2, ATT_Q, ATT_WIN), lambda hp: (hp, 0, 0, 0, 0))],
        out_shape=[jax.ShapeDtypeStruct((ATT_HEADS // 2, ATT_STARTS + 1, 2, ATT_Q, ATT_WIN), F32)],
        semantics=("parallel",), args=(wvec,), exchange=exchange)
    return tiles, moved


def _att_pad(src_ref, pad_ref):
    pad_ref[:ATT_PAD, :] = jnp.zeros((ATT_PAD, ATT_COLS), BF16)
    pad_ref[ATT_PAD:, :] = src_ref[...]


def _att_head(q2, sel):
    return jnp.where(sel, q2, jnp.zeros_like(q2)) * 0.125


def _att_softmax_rows(s_ref, bias_ref, rows):
    s = s_ref[rows, :] + bias_ref[rows, :]
    ex = jnp.exp(s - jnp.max(s, axis=-1, keepdims=True))
    return ex, 1.0 / jnp.sum(ex, axis=-1, keepdims=True)


def _att_fwd(proj, bias, batch, seq, exchange):
    ni, q_spec, k_spec, v_spec, _, b_spec, pad = _att_specs(batch, seq)

    def body(q_ref, k_ref, v_ref, bias_ref, o_ref, kp_ref, vp_ref, s_ref, e_ref):
        i = pl.program_id(2)

        @pl.when(i == 0)
        def _():
            _att_pad(k_ref, kp_ref)
            _att_pad(v_ref, vp_ref)

        win = pl.ds(pl.multiple_of(i * ATT_Q, ATT_Q), ATT_WIN)
        lo = lax.broadcasted_iota(jnp.int32, (1, LANES), 1) < 64
        start = jnp.minimum(i, ATT_STARTS)
        for pair in range(ATT_PAIRS):
            cols = slice(pair * LANES, (pair + 1) * LANES)
            k2, v2, q2 = kp_ref[win, cols], vp_ref[win, cols], q_ref[:, cols]
            out = jnp.zeros((ATT_Q, LANES), F32)
            for e in range(2):
                h = 2 * pair + e
                sel = lo if e == 0 else jnp.logical_not(lo)
                s_ref[h] = _dot_nt(_att_head(q2, sel), k2)
                rsum = []
                for c in range(ATT_Q // ATT_ROWS):
                    rows = slice(c * ATT_ROWS, (c + 1) * ATT_ROWS)
                    ex, r = _att_softmax_rows(s_ref.at[h], bias_ref.at[pair, start, e], rows)
                    e_ref[h, rows, :] = ex.astype(BF16)
                    rsum.append(r)
                out = out + _dot(e_ref[h], jnp.where(sel, v2, jnp.zeros_like(v2))) * jnp.concatenate(rsum, axis=0)
            o_ref[:, cols] = out.astype(BF16)

    heads = 2 * ATT_PAIRS
    return _call(
        body, name="att_fwd", grid=(ATT_HEADS // heads, batch, ni),
        in_specs=[q_spec, k_spec, v_spec, b_spec],
        out_specs=[pl.BlockSpec((ATT_Q, ATT_COLS), lambda g, b, i: (b * ni + i, g))],
        out_shape=[jax.ShapeDtypeStruct((batch * seq, ATT_HEADS * 64), BF16)],
        scratch=[pad, pad, pltpu.VMEM((heads, ATT_Q, ATT_WIN), F32), pltpu.VMEM((heads, ATT_Q, ATT_WIN), BF16)],
        semantics=("arbitrary", "arbitrary", "arbitrary"), args=(proj, proj, proj, bias), exchange=exchange)


GL_HALF = 512


def _gl_specs(tm):
    return [pl.BlockSpec((tm, GL_HALF), lambda i, c=C_GL // GL_HALF + j: (i, c)) for j in range(4)]


def _gates(gl_refs, b_ref):
    logits = [ref[...].astype(F32) for ref in gl_refs]
    gr = _sigmoid(jnp.concatenate(logits[:2], axis=1) + b_ref[:, :D_MODEL])
    ga = _sigmoid(jnp.concatenate(logits[2:], axis=1) + b_ref[:, D_MODEL:])
    return gr, ga


def _whole(a):
    return pl.BlockSpec(a.shape, lambda i: (0,) * a.ndim)


def _mix_out_fwd(gro, ao, proj, b_gate, w_ret, w_att_t, x2, w_out, g2, exchange):
    t = gro.shape[0]
    tm = min(256, t)

    def body(gro_ref, ao_ref, gl0, gl1, gl2, gl3, b_ref, wr_ref, wa_ref, x_ref, wo_ref, g_ref,
             z_ref, yr_ref, ya_ref, h_ref, hn_ref):
        yr = _dot(gro_ref[...], wr_ref[...])
        ya = _dot_nt(ao_ref[...], wa_ref[...])
        yr_ref[...] = yr.astype(BF16)
        ya_ref[...] = ya.astype(BF16)
        gr, ga = _gates((gl0, gl1, gl2, gl3), b_ref)
        z = (gr * yr + ga * ya).astype(BF16)
        z_ref[...] = z
        h = x_ref[...] + _dot(z, wo_ref[...])
        h_ref[...] = h
        r = lax.rsqrt(jnp.mean(h * h, axis=-1, keepdims=True) + EPS)
        hn_ref[...] = (h * r * g_ref[...]).astype(BF16)

    row = pl.BlockSpec((tm, D_MODEL), lambda i: (i, 0))
    return _call(
        body, name="mix_out_fwd", grid=(t // tm,),
        in_specs=[row, pl.BlockSpec((tm, 512), lambda i: (i, 0)), *_gl_specs(tm),
                  _whole(b_gate), _whole(w_ret), _whole(w_att_t), row, _whole(w_out), _whole(g2)],
        out_specs=[row] * 5,
        out_shape=[jax.ShapeDtypeStruct((t, D_MODEL), BF16)] * 3
                  + [jax.ShapeDtypeStruct((t, D_MODEL), F32), jax.ShapeDtypeStruct((t, D_MODEL), BF16)],
        semantics=("parallel",), args=(gro, ao, proj, proj, proj, proj, b_gate, w_ret, w_att_t, x2, w_out, g2),
        exchange=exchange)


def _col_chunks(width, chunk=384):
    return [slice(lo, min(lo + chunk, width)) for lo in range(0, width, chunk)]


def _ffn_up(hn, wg_t, wu_t, exchange):
    t = hn.shape[0]
    tm, tn = min(512, t), D_FF // 2

    def body(h_ref, wg_ref, wu_ref, g_ref, u_ref, a_ref):
        g = _dot_nt(h_ref[...], wg_ref[...])
        u = _dot_nt(h_ref[...], wu_ref[...])
        g_ref[...] = g.astype(BF16)
        u_ref[...] = u.astype(BF16)
        a_ref[...] = (g * _sigmoid(g) * u).astype(BF16)

    w_spec = pl.BlockSpec((tn, D_MODEL), lambda j, i: (j, 0))
    out = pl.BlockSpec((tm, tn), lambda j, i: (i, j))
    return _call(
        body, name="ffn_up", grid=(D_FF // tn, t // tm),
        in_specs=[pl.BlockSpec((tm, D_MODEL), lambda j, i: (i, 0)), w_spec, w_spec],
        out_specs=[out, out, out],
        out_shape=[jax.ShapeDtypeStruct((t, D_FF), BF16)] * 3,
        semantics=("parallel", "parallel"), args=(hn, wg_t, wu_t), exchange=exchange)


def _ffn_down_loss(a, h1, tgt, w_down, g3):
    t = a.shape[0]
    tm = min(512, t)

    def body(a_ref, h_ref, t_ref, w_ref, g_ref, dh_ref, dhb_ref, loss_ref, dg_ref):
        @pl.when(pl.program_id(0) == 0)
        def _():
            loss_ref[...] = jnp.zeros_like(loss_ref)
            dg_ref[...] = jnp.zeros_like(dg_ref)

        g = g_ref[...]
        h2 = h_ref[...] + _dot(a_ref[...], w_ref[...])
        r = lax.rsqrt(jnp.mean(h2 * h2, axis=-1, keepdims=True) + EPS)
        err = h2 * r * g - t_ref[...]
        loss_ref[...] += jnp.sum(err * err) * (0.5 / D_MODEL)
        dy = err * (1.0 / D_MODEL)
        dh, dg_rows = _rms_bwd(h2, g, dy)
        dg_ref[...] += jnp.sum(dg_rows, axis=0, keepdims=True)
        dh_ref[...] = dh
        dhb_ref[...] = dh.astype(BF16)

    row = pl.BlockSpec((tm, D_MODEL), lambda i: (i, 0))
    vec = pl.BlockSpec((1, D_MODEL), lambda i: (0, 0))
    return pl.pallas_call(
        body, name="ffn_down_loss", grid=(t // tm,),
        in_specs=[pl.BlockSpec((tm, D_FF), lambda i: (i, 0)), row, row,
                  pl.BlockSpec((D_FF, D_MODEL), lambda i: (0, 0)), vec],
        out_specs=[row, row, pl.BlockSpec((1, LANES), lambda i: (0, 0)), vec],
        out_shape=[jax.ShapeDtypeStruct((t, D_MODEL), F32), jax.ShapeDtypeStruct((t, D_MODEL), BF16),
                   jax.ShapeDtypeStruct((1, LANES), F32), jax.ShapeDtypeStruct((1, D_MODEL), F32)],
        compiler_params=_params("arbitrary"),
    )(a, h1, tgt, w_down, g3)


def _ffn_bwd_act(dh2b, w_down, g_act, u_act, exchange):
    t = dh2b.shape[0]
    tm, tn = min(512, t), D_FF // 2

    def body(d_ref, w_ref, g_ref, u_ref, dg_ref, du_ref):
        d = d_ref[...]
        for cols in _col_chunks(tn):
            da = _dot_nt(d, w_ref[cols, :])
            g = g_ref[:, cols].astype(F32)
            u = u_ref[:, cols].astype(F32)
            sg = _sigmoid(g)
            dg_ref[:, cols] = (da * u * sg * (1.0 + g * (1.0 - sg))).astype(BF16)
            du_ref[:, cols] = (da * g * sg).astype(BF16)

    blk = pl.BlockSpec((tm, tn), lambda j, i: (i, j))
    return _call(
        body, name="ffn_bwd_act", grid=(D_FF // tn, t // tm),
        in_specs=[pl.BlockSpec((tm, D_MODEL), lambda j, i: (i, 0)),
                  pl.BlockSpec((tn, D_MODEL), lambda j, i: (j, 0)), blk, blk],
        out_specs=[blk, blk],
        out_shape=[jax.ShapeDtypeStruct((t, D_FF), BF16)] * 2,
        semantics=("parallel", "parallel"), args=(dh2b, w_down, g_act, u_act), exchange=exchange)


def _ffn_bwd_in(dg, du, wg_t, wu_t, h1, dh2, g2, exchange):
    t = dg.shape[0]
    tm = min(256, t)

    def body(dg_ref, du_ref, wg_ref, wu_ref, h_ref, d2_ref, g_ref, dh_ref, dhb_ref, gn_ref):
        @pl.when(pl.program_id(0) == 0)
        def _():
            gn_ref[...] = jnp.zeros_like(gn_ref)

        dhn = _dot(dg_ref[...], wg_ref[...]) + _dot(du_ref[...], wu_ref[...])
        dx, dg_rows = _rms_bwd(h_ref[...], g_ref[...], dhn)
        dh = d2_ref[...] + dx
        dh_ref[...] = dh
        dhb_ref[...] = dh.astype(BF16)
        gn_ref[...] += jnp.sum(dg_rows, axis=0, keepdims=True)

    act = pl.BlockSpec((tm, D_FF), lambda i: (i, 0))
    row = pl.BlockSpec((tm, D_MODEL), lambda i: (i, 0))
    return _call(
        body, name="ffn_bwd_in", grid=(t // tm,),
        in_specs=[act, act, _whole(wg_t), _whole(wu_t), row, row, _whole(g2)],
        out_specs=[row, row, _whole(g2)],
        out_shape=[jax.ShapeDtypeStruct((t, D_MODEL), F32), jax.ShapeDtypeStruct((t, D_MODEL), BF16),
                   jax.ShapeDtypeStruct((1, D_MODEL), F32)],
        semantics=("arbitrary",), args=(dg, du, wg_t, wu_t, h1, dh2, g2), exchange=exchange)


def _mix_bwd(dh1b, w_out, proj, b_gate, y_ret, y_att, w_ret, w_att_t, exchange):
    t = dh1b.shape[0]
    tm = min(256, t)

    def body(d_ref, wo_ref, gl0, gl1, gl2, gl3, b_ref, yr_ref, ya_ref, wr_ref, wa_ref,
             dyr_ref, dya_ref, dglr_ref, dgla_ref, dgro_ref, dao_ref, db_ref):
        @pl.when(pl.program_id(0) == 0)
        def _():
            db_ref[...] = jnp.zeros_like(db_ref)

        dz = _dot_nt(d_ref[...], wo_ref[...])
        gr, ga = _gates((gl0, gl1, gl2, gl3), b_ref)
        dyr = (dz * gr).astype(BF16)
        dya = (dz * ga).astype(BF16)
        dyr_ref[...] = dyr
        dya_ref[...] = dya
        dglr = dz * yr_ref[...].astype(F32) * gr * (1.0 - gr)
        dgla = dz * ya_ref[...].astype(F32) * ga * (1.0 - ga)
        dglr_ref[...] = dglr.astype(BF16)
        dgla_ref[...] = dgla.astype(BF16)
        db_ref[:, :D_MODEL] += jnp.sum(dglr, axis=0, keepdims=True)
        db_ref[:, D_MODEL:] += jnp.sum(dgla, axis=0, keepdims=True)
        dgro_ref[...] = _dot_nt(dyr, wr_ref[...]).astype(BF16)
        dao_ref[...] = _dot(dya, wa_ref[...]).astype(BF16)

    row = pl.BlockSpec((tm, D_MODEL), lambda i: (i, 0))
    half = pl.BlockSpec((tm, 512), lambda i: (i, 0))
    return _call(
        body, name="mix_bwd", grid=(t // tm,),
        in_specs=[row, _whole(w_out), *_gl_specs(tm), _whole(b_gate), row, row, _whole(w_ret), _whole(w_att_t)],
        out_specs=[row, row, row, row, row, half, _whole(b_gate)],
        out_shape=[jax.ShapeDtypeStruct((t, D_MODEL), BF16)] * 5
                  + [jax.ShapeDtypeStruct((t, 512), BF16), jax.ShapeDtypeStruct(b_gate.shape, F32)],
        semantics=("arbitrary",),
        args=(dh1b, w_out, proj, proj, proj, proj, b_gate, y_ret, y_att, w_ret, w_att_t), exchange=exchange)


def _ret_bwd(dgro, proj, o_ret, qr, kr, cs, sn, lg_arr, batch, seq, exchange):
    t = batch * seq
    nt = seq // RET_TILE

    def one_head(dgro_ref, rg_ref, o_ref, qr_ref, kr_ref, v_ref, cs_ref, sn_ref, lg_ref,
                 dq_ref, dk_ref, dv_ref, drg_ref, do_ref, st_ref):
        lg = lg_ref[:, 0:1]
        inside, q_dec, k_dec, tile_dec = _decay(lg)

        state = jnp.zeros((RET_KEY_DIM, RET_VAL_DIM), F32)
        for i in range(nt - 1):
            rows = slice(i * RET_TILE, (i + 1) * RET_TILE)
            state = state * tile_dec + _dot_tn(_scaled(kr_ref[rows, :], k_dec), v_ref[rows, :])
            st_ref[i + 1] = state.astype(BF16)

        for i in range(nt):
            rows = slice(i * RET_TILE, (i + 1) * RET_TILE)
            o = o_ref[rows, :]
            xc = o - jnp.mean(o, axis=-1, keepdims=True)
            rs = lax.rsqrt(jnp.mean(xc * xc, axis=-1, keepdims=True) + EPS)
            nrm = xc * rs
            rg = rg_ref[rows, :].astype(F32)
            sg = _sigmoid(rg)
            dg = dgro_ref[rows, :].astype(F32)
            drg_ref[rows, :] = (dg * nrm * sg * (1.0 + rg * (1.0 - sg))).astype(BF16)
            dn = dg * rg * sg
            do = rs * (dn - jnp.mean(dn, axis=-1, keepdims=True)
                       - nrm * jnp.mean(dn * nrm, axis=-1, keepdims=True))
            do_ref[rows, :] = do.astype(BF16)

        dstate = jnp.zeros((RET_KEY_DIM, RET_VAL_DIM), F32)
        for i in reversed(range(nt)):
            rows = slice(i * RET_TILE, (i + 1) * RET_TILE)
            qi, ki, vi, doi = qr_ref[rows, :], kr_ref[rows, :], v_ref[rows, :], do_ref[rows, :]
            p = (_dot_nt(qi, ki) * inside).astype(BF16)
            dp = (_dot_nt(doi, vi) * inside).astype(BF16)
            dq = _dot(dp, ki)
            dk = _dot_tn(dp, qi)
            dv = _dot_tn(p, doi)
            if i > 0:
                dq = dq + _dot_nt(doi, st_ref[i]) * q_dec
            if i < nt - 1:
                dsb = dstate.astype(BF16)
                dk = dk + _dot_nt(vi, dsb) * k_dec
                dv = dv + _dot(_scaled(ki, k_dec), dsb)
            if i > 0:
                dstate = dstate * tile_dec + _dot_tn(_scaled(qi, q_dec), doi)
            dq_ref[rows, :] = (dq * cs_ref[rows, :] - pltpu.roll(dq, 64, 1) * sn_ref[rows, :]).astype(BF16)
            dk = (dk * cs_ref[rows, :] - pltpu.roll(dk, 64, 1) * sn_ref[rows, :]) * (RET_KEY_DIM ** -0.5)
            dk_ref[rows, :] = dk.astype(BF16)
            dv_ref[rows, :] = dv.astype(BF16)

    key, val, tab, lgs = _ret_specs(seq)
    return _call(
        _per_head(one_head, "vvvkkvsshkkvvhh"), name="ret_bwd", grid=(batch, RET_HEADS // RET_GROUP),
        in_specs=[val(0), val(C_RG), val(0), key(0), key(0), val(C_RV), tab, tab, lgs],
        out_specs=[key(0), key(0), val(0), val(0)],
        out_shape=[jax.ShapeDtypeStruct((t, RET_HEADS * RET_KEY_DIM), BF16)] * 2
                  + [jax.ShapeDtypeStruct((t, RET_HEADS * RET_VAL_DIM), BF16)] * 2,
        scratch=[pltpu.VMEM((RET_GROUP, seq, RET_VAL_DIM), BF16),
                 pltpu.VMEM((RET_GROUP, nt, RET_KEY_DIM, RET_VAL_DIM), BF16)],
        semantics=("parallel", "parallel"), args=(dgro, proj, o_ret, qr, kr, proj, cs, sn, lg_arr),
        exchange=exchange)


def _att_bwd(proj, bias, dao, batch, seq, exchange):
    ni, q_spec, k_spec, v_spec, w_spec, b_spec, pad = _att_specs(batch, seq)
    t = batch * seq

    def body(q_ref, k_ref, v_ref, bias_ref, do_ref, dq_ref, dk_ref, dv_ref, dw_ref,
             dbias_ref, dk_acc, dv_acc, kp_ref, vp_ref, s_ref, dp_ref, e_ref, ds_ref):
        b, i = pl.program_id(1), pl.program_id(2)

        @pl.when((b == 0) & (i == 0))
        def _():
            dbias_ref[...] = jnp.zeros_like(dbias_ref)

        @pl.when(i == 0)
        def _():
            _att_pad(k_ref, kp_ref)
            _att_pad(v_ref, vp_ref)
            dk_acc[...] = jnp.zeros_like(dk_acc)
            dv_acc[...] = jnp.zeros_like(dv_acc)

        win = pl.ds(pl.multiple_of(i * ATT_Q, ATT_Q), ATT_WIN)
        lo = lax.broadcasted_iota(jnp.int32, (1, LANES), 1) < 64
        start = jnp.minimum(i, ATT_STARTS)
        for pair in range(ATT_PAIRS):
            cols = slice(pair * LANES, (pair + 1) * LANES)
            k2, v2, q2, do2 = kp_ref[win, cols], vp_ref[win, cols], q_ref[:, cols], do_ref[:, cols]
            dq = jnp.zeros((ATT_Q, LANES), F32)
            dk = jnp.zeros((LANES, ATT_WIN), F32)
            dv = jnp.zeros((LANES, ATT_WIN), F32)
            for e in range(2):
                h = 2 * pair + e
                sel = lo if e == 0 else jnp.logical_not(lo)
                qm = _att_head(q2, sel)
                dom = jnp.where(sel, do2, jnp.zeros_like(do2))
                s_ref[h] = _dot_nt(qm, k2)
                dp_ref[h] = _dot_nt(dom, v2)
                rsum = []
                for c in range(ATT_Q // ATT_ROWS):
                    rows = slice(c * ATT_ROWS, (c + 1) * ATT_ROWS)
                    ex, r = _att_softmax_rows(s_ref.at[h], bias_ref.at[pair, start, e], rows)
                    dp = dp_ref[h, rows, :]
                    mean = jnp.sum(dp * ex, axis=-1, keepdims=True) * r
                    ds = ex * ((dp - mean) * r)
                    dbias_ref[h, rows, :] += ds
                    ds_ref[h, rows, :] = ds.astype(BF16)
                    e_ref[h, rows, :] = ex.astype(BF16)
                    rsum.append(r)
                dq = dq + _dot(ds_ref[h], jnp.where(sel, k2, jnp.zeros_like(k2)))
                dk = dk + _dot_tn(qm, ds_ref[h])
                dv = dv + _dot_tn((dom.astype(F32) * jnp.concatenate(rsum, axis=0)).astype(BF16), e_ref[h])
            dq_ref[:, cols] = (dq * 0.125).astype(BF16)
            dk_acc[cols, win] += dk
            dv_acc[cols, win] += dv

        @pl.when(i == ni - 1)
        def _():
            dk_ref[...] = dk_acc[:, ATT_PAD:].T.astype(BF16)
            dv_ref[...] = dv_acc[:, ATT_PAD:].T.astype(BF16)

        @pl.when((b == batch - 1) & (i == ni - 1))
        def _():
            n_i = lax.broadcasted_iota(jnp.int32, (ATT_Q, BIAS_LEN), 0)
            for h in range(heads):
                xw = jnp.concatenate([jnp.zeros((ATT_Q, BIAS_LEN - ATT_WIN), F32), dbias_ref[h]], axis=1)
                for bit in range(8):
                    xw = jnp.where(((n_i >> bit) & 1) == 1, pltpu.roll(xw, BIAS_LEN - (1 << bit), 1), xw)
                dw_ref[h // 2, h % 2:h % 2 + 1, :] = jnp.sum(xw, axis=0, keepdims=True)

    heads = 2 * ATT_PAIRS
    seq_blk = pl.BlockSpec((seq, ATT_COLS), lambda g, b, i: (b, g))
    q_out = pl.BlockSpec((ATT_Q, ATT_COLS), lambda g, b, i: (b * ni + i, g))
    tile_f32, tile_bf16 = pltpu.VMEM((heads, ATT_Q, ATT_WIN), F32), pltpu.VMEM((heads, ATT_Q, ATT_WIN), BF16)
    acc = pltpu.VMEM((ATT_COLS, seq + ATT_PAD), F32)
    return _call(
        body, name="att_bwd", grid=(ATT_HEADS // heads, batch, ni),
        in_specs=[q_spec, k_spec, v_spec, b_spec, q_out],
        out_specs=[q_out, seq_blk, seq_blk, w_spec],
        out_shape=[jax.ShapeDtypeStruct((t, 512), BF16)] * 3
                  + [jax.ShapeDtypeStruct((ATT_HEADS // 2, 2, BIAS_LEN), F32)],
        scratch=[tile_f32, acc, acc, pad, pad, tile_f32, tile_f32, tile_bf16, tile_bf16],
        semantics=("arbitrary", "arbitrary", "arbitrary"), args=(proj, proj, proj, bias, dao), exchange=exchange,
        vmem=VMEM_LIMIT_ATT_BWD)


def _rms_in_bwd(x2, dxn, dh1, g1):
    t = x2.shape[0]
    tm = min(512, t)

    def body(x_ref, d_ref, h_ref, g_ref, dx_ref, dg_ref):
        @pl.when(pl.program_id(0) == 0)
        def _():
            dg_ref[...] = jnp.zeros_like(dg_ref)

        dx, dg_rows = _rms_bwd(x_ref[...], g_ref[...], d_ref[...])
        dx_ref[...] = h_ref[...] + dx
        dg_ref[...] += jnp.sum(dg_rows, axis=0, keepdims=True)

    row = pl.BlockSpec((tm, D_MODEL), lambda i: (i, 0))
    vec = pl.BlockSpec((1, D_MODEL), lambda i: (0, 0))
    return pl.pallas_call(
        body, name="rms_in_bwd", grid=(t // tm,),
        in_specs=[row, row, row, vec], out_specs=[row, vec],
        out_shape=[jax.ShapeDtypeStruct((t, D_MODEL), F32), jax.ShapeDtypeStruct((1, D_MODEL), F32)],
        compiler_params=_params("arbitrary"),
    )(x2, dxn, dh1, g1)


def _pack_small(dg1, dbr, dba, dg2, dg3, dw, loss):
    def body(a_ref, b_ref, c_ref, d_ref, e_ref, w_ref, l_ref, o_ref):
        o_ref[...] = jnp.zeros_like(o_ref)
        for r, ref in enumerate((a_ref, b_ref, c_ref, d_ref, e_ref)):
            o_ref[r:r + 1, :] = ref[...]
        o_ref[5:6, 0:LANES] = l_ref[...]
        for hp in range(ATT_HEADS // 2):
            o_ref[8 + 2 * hp:10 + 2 * hp, :] = w_ref[hp]

    return pl.pallas_call(body, name="pack_small",
                          out_shape=jax.ShapeDtypeStruct((16, D_MODEL), F32))(dg1, dbr, dba, dg2, dg3, dw, loss)


def _rotary_tables(seq):
    freqs = ROPE_BASE ** (-jnp.arange(0, RET_KEY_DIM, 2, dtype=F32) / RET_KEY_DIM)
    ang = jnp.arange(seq, dtype=F32)[:, None] * freqs[None, :]
    cos, sin = jnp.cos(ang), jnp.sin(ang)
    return jnp.concatenate([cos, cos], axis=1), jnp.concatenate([-sin, sin], axis=1)


def _bias_rows(rel_bias):
    n_far = BIAS_LEN - ATT_Q - MAX_REL + 1
    n_near = BIAS_LEN - n_far - (N_REL - 2)
    w = jnp.concatenate([jnp.broadcast_to(rel_bias[:, N_REL - 1:], (ATT_HEADS, n_far)),
                         rel_bias[:, 1:N_REL - 1][:, ::-1],
                         jnp.broadcast_to(rel_bias[:, :1], (ATT_HEADS, n_near))], axis=1)
    return w.reshape(ATT_HEADS // 2, 2, BIAS_LEN)


def _bias_rows_bwd(dw):
    n_far = BIAS_LEN - ATT_Q - MAX_REL + 1
    mid = dw[:, n_far:n_far + N_REL - 2][:, ::-1]
    return jnp.concatenate([jnp.sum(dw[:, n_far + N_REL - 2:], axis=1, keepdims=True), mid,
                            jnp.sum(dw[:, :n_far], axis=1, keepdims=True)], axis=1)


def _step(x, tgt, norm_mix, b_gate, norm_ffn, norm_final, rel_bias_shard, shard):
    batch, seq, _ = x.shape
    t = batch * seq
    n_rb = rel_bias_shard.shape[-1]
    x2, tgt2 = x.reshape(t, D_MODEL), tgt.reshape(t, D_MODEL)
    g3 = norm_final.reshape(1, D_MODEL)
    cs, sn = _rotary_tables(seq)
    lg = np.log(1.0 - 2.0 ** (-5.0 - np.arange(RET_HEADS, dtype=np.float32))).astype(np.float32)
    lg_arr = jnp.asarray(np.broadcast_to(lg[:, None, None], (RET_HEADS, 1, LANES)))

    def gather(*names):
        return _ChipGather([shard[nm] for nm in names])

    def scatter(*grads):
        return _Exchange(grads, scatter=True)

    rb_pad = jnp.pad(rel_bias_shard, ((0, 0), (0, LANES - n_rb)))
    (xn,), (w_in_half, rb_full) = _rms_fwd(x2, norm_mix,
                                           _ChipGather([shard["w_in_t"], rb_pad], parts=[(0, 2), (0, 1)]))
    rb_full = rb_full.reshape(N_DEV, ATT_HEADS, LANES)[:, :, :n_rb]
    bias, (w_in_t,) = _att_bias_tiles(
        _bias_rows(jnp.transpose(rb_full, (1, 0, 2)).reshape(ATT_HEADS, N_DEV * n_rb)),
        _ChipGather([shard["w_in_t"]], parts=[(1, 2)], into=[w_in_half]))
    proj, (w_ret, w_att_t, w_out) = _mm(
        xn, w_in_t, tb=True, out_dtype=BF16, tm=1024, tn=1664, tk=1024, name="proj",
        exchange=gather("w_ret", "w_att_t", "w_out"))
    (gro, o_ret, qr, kr), _ = _ret_fwd(proj, cs, sn, lg_arr, batch, seq, None)
    (ao,), (w_up_t,) = _att_fwd(proj, bias, batch, seq, gather("w_up_t"))
    (z, y_ret, y_att, h1, hn), (w_gate_t,) = _mix_out_fwd(gro, ao, proj, b_gate, w_ret, w_att_t, x2, w_out, norm_ffn,
                                                         gather("w_gate_t"))
    (g_act, u_act, a_act), (w_down,) = _ffn_up(hn, w_gate_t, w_up_t, gather("w_down"))
    dh2, dh2b, loss, dg3 = _ffn_down_loss(a_act, h1, tgt2, w_down, g3)

    wg = dict(out_dtype=BF16, tn=1024, ta=True)
    slots = {}
    dw_down = _mm(a_act, dh2b, tm=1408, tk=2048, name="dw_down", **wg)
    (d_gact, d_uact), _ = _ffn_bwd_act(dh2b, w_down, g_act, u_act, None)
    dw_gate = _mm(d_gact, hn, tm=1408, tk=2048, name="dw_gate", **wg)
    dw_up = _mm(d_uact, hn, tm=1408, tk=2048, name="dw_up", **wg)
    (dh1, dh1b, dg2), (slots["w_down"],) = _ffn_bwd_in(d_gact, d_uact, w_gate_t, w_up_t, h1, dh2, norm_ffn,
                                                     scatter(dw_down))
    dw_out = _mm(z, dh1b, tm=1024, tk=2048, name="dw_out", **wg)
    (dyr, dya, dglr, dgla, dgro, dao, db), (slots["w_out"],) = _mix_bwd(
        dh1b, w_out, proj, b_gate, y_ret, y_att, w_ret, w_att_t, scatter(dw_out))
    dw_ret = _mm(gro, dyr, tm=1024, tk=2048, name="dw_ret", **wg)
    dw_att = _mm(dya, ao, tm=1024, tk=2048, name="dw_att", **wg)
    (drq, drk, drv, drg), (slots["w_ret"], slots["w_att_t"]) = _ret_bwd(
        dgro, proj, o_ret, qr, kr, cs, sn, lg_arr, batch, seq, scatter(dw_ret, dw_att))
    (daq, dak, dav, dw), (slots["w_gate_t"],) = _att_bwd(proj, bias, dao, batch, seq, scatter(dw_gate))
    dproj = [drq, drk, drv, drg, daq, dak, dav, dglr, dgla]
    dw_in, (slots["w_up_t"],) = _mm_pieces(dproj, xn, ta=True, out_dtype=BF16, tm=512, tn=1024, tk=1024,
                                           name="dw_in", exchange=scatter(dw_up))
    (dw_in_sibling,) = _alone(_PairSwap([dw_in]), "swap_w_in")
    dw_in_pairs = _pair_add(dw_in, dw_in_sibling, "pair_w_in")
    dxn, (slots["w_in_t"],) = _mm_pieces(dproj, w_in_t, ta=False, out_dtype=F32, tm=1024, tn=1024, tk=512, name="dxn",
                                  exchange=_ChipScatter([dw_in_pairs]))
    dx, dg1 = _rms_in_bwd(x2, dxn, dh1, norm_mix)
    small = _pack_small(dg1, db[:, :D_MODEL], db[:, D_MODEL:], dg2, dg3, dw, loss)
    (small_slots,) = _alone(_ChipGather([small]), "gather_small")
    return dx.reshape(batch, seq, D_MODEL), slots, small_slots.reshape(N_DEV, 16, D_MODEL)


def _row_tile(r, c):
    return max(d for d in range(16, r + 1, 16) if r % d == 0 and (d * c <= 256 * 1024 or d == 16))


def _pair_add(grad, got, name):
    _, r, c = got.shape
    tr = r
    core = lax.axis_index("c").astype(jnp.int32).reshape(1)

    def body(core_ref, g_ref, a_ref, o_ref):
        o_ref[...] = (g_ref[...].astype(F32) + a_ref[...].astype(F32)).astype(o_ref.dtype)

    blk = pl.BlockSpec((None, tr, c), lambda q, i, core_ref: (q, i, 0))
    return pl.pallas_call(
        body, name=name,
        grid_spec=pltpu.PrefetchScalarGridSpec(
            num_scalar_prefetch=1, grid=(4, r // tr),
            in_specs=[pl.BlockSpec((None, None, tr, c), lambda q, i, core_ref: (q, core_ref[0], i, 0)), blk],
            out_specs=blk),
        out_shape=jax.ShapeDtypeStruct(got.shape, got.dtype),
        compiler_params=_params("parallel", "parallel"),
    )(core, grad.reshape(4, 2, r, c), got)


def _sum_slots(slots, name):
    n, r, c = slots.shape
    tr = _row_tile(r, c)

    def body(s_ref, o_ref):
        acc = s_ref[0].astype(F32)
        for s in range(1, n):
            acc = acc + s_ref[s].astype(F32)
        o_ref[...] = acc

    return pl.pallas_call(
        body, name=name, grid=(r // tr,),
        in_specs=[pl.BlockSpec((n, tr, c), lambda i: (0, i, 0))],
        out_specs=pl.BlockSpec((tr, c), lambda i: (i, 0)),
        out_shape=jax.ShapeDtypeStruct((r, c), F32),
        compiler_params=_params("parallel"),
    )(slots)


def _adamw_math(w, g, m, v):
    m = ADAM_B1 * m + (1.0 - ADAM_B1) * g
    v = ADAM_B2 * v + (1.0 - ADAM_B2) * (g * g)
    m_hat = m / (1.0 - ADAM_B1 ** ADAM_STEP)
    v_hat = v / (1.0 - ADAM_B2 ** ADAM_STEP)
    return -ADAM_LR * (m_hat / (jnp.sqrt(v_hat) + ADAM_EPS) + ADAM_WD * w), m, v


def _adamw(w, slots, m, v, name):
    n, r, c = slots.shape
    tr = _row_tile(r, c)

    def body(w_ref, s_ref, m_ref, v_ref, g_ref, d_ref, nm_ref, nv_ref):
        g = s_ref[0].astype(F32)
        for s in range(1, n):
            g = g + s_ref[s].astype(F32)
        g_ref[...] = g
        d_ref[...], nm_ref[...], nv_ref[...] = _adamw_math(w_ref[...], g, m_ref[...], v_ref[...])

    blk = pl.BlockSpec((tr, c), lambda i: (i, 0))
    return pl.pallas_call(
        body, name=name, grid=(r // tr,),
        in_specs=[blk, pl.BlockSpec((n, tr, c), lambda i: (0, i, 0)), blk, blk], out_specs=[blk] * 4,
        out_shape=[jax.ShapeDtypeStruct((r, c), F32)] * 4,
        compiler_params=_params("parallel"),
    )(w, slots, m, v)


def _adamw_small(ws, gs, ms, vs):
    n = len(ws)

    def body(*refs):
        for i in range(n):
            w_ref, g_ref, m_ref, v_ref = (refs[j * n + i] for j in range(4))
            d_ref, nm_ref, nv_ref = (refs[(4 + j) * n + i] for j in range(3))
            d_ref[...], nm_ref[...], nv_ref[...] = _adamw_math(w_ref[...], g_ref[...], m_ref[...], v_ref[...])

    shapes = [jax.ShapeDtypeStruct(w.shape, F32) for w in ws]
    outs = pl.pallas_call(body, name="adamw_small", out_shape=shapes * 3)(*ws, *gs, *ms, *vs)
    return outs[:n], outs[n:2 * n], outs[2 * n:]


def kernel(x, norm_mix, w_in, b_gate, rel_bias, w_ret_out, w_att_out, w_out, norm_ffn, w_ffn_gate, w_ffn_up, w_ffn_down, norm_final, loss_target, m_norm_mix, m_w_in, m_b_gate, m_rel_bias, m_w_ret_out, m_w_att_out, m_w_out, m_norm_ffn, m_w_ffn_gate, m_w_ffn_up, m_w_ffn_down, m_norm_final, v_norm_mix, v_w_in, v_b_gate, v_rel_bias, v_w_ret_out, v_w_att_out, v_w_out, v_norm_ffn, v_w_ffn_gate, v_w_ffn_up, v_w_ffn_down, v_norm_final):
    me = _index(_place())
    n_rb = rel_bias.shape[-1]

    shard = dict(w_in_t=w_in[0].T, w_gate_t=w_ffn_gate[0].T, w_up_t=w_ffn_up[0].T, w_down=w_ffn_down[0],
                 w_ret=w_ret_out[0], w_out=w_out[0], w_att_t=w_att_out[0].T)
    shard = {nm: s.astype(BF16) for nm, s in shard.items()}
    dx, slots, small_slots = _step(x, loss_target, norm_mix, b_gate, norm_ffn, norm_final, rel_bias[0], shard)
    small_sum = _sum_slots(small_slots, "sum_small")
    loss = small_sum[5, 0]

    transposed = dict(w_in="w_in_t", w_ffn_gate="w_gate_t", w_ffn_up="w_up_t", w_att_out="w_att_t")
    plain = dict(w_ffn_down="w_down", w_ret_out="w_ret", w_out="w_out")
    g = dict(
        norm_mix=small_sum[0:1], b_gate=jnp.concatenate([small_sum[1:2], small_sum[2:3]], axis=1),
        norm_ffn=small_sum[3:4], norm_final=small_sum[4:5],
        rel_bias=lax.dynamic_slice_in_dim(_bias_rows_bwd(small_sum[8:16]), me * n_rb, n_rb, axis=1),
    )
    w = dict(norm_mix=norm_mix, w_in=w_in, b_gate=b_gate, rel_bias=rel_bias, w_ret_out=w_ret_out, w_att_out=w_att_out,
             w_out=w_out, norm_ffn=norm_ffn, w_ffn_gate=w_ffn_gate, w_ffn_up=w_ffn_up, w_ffn_down=w_ffn_down,
             norm_final=norm_final)
    m = dict(norm_mix=m_norm_mix, w_in=m_w_in, b_gate=m_b_gate, rel_bias=m_rel_bias, w_ret_out=m_w_ret_out,
             w_att_out=m_w_att_out, w_out=m_w_out, norm_ffn=m_norm_ffn, w_ffn_gate=m_w_ffn_gate, w_ffn_up=m_w_ffn_up,
             w_ffn_down=m_w_ffn_down, norm_final=m_norm_final)
    v = dict(norm_mix=v_norm_mix, w_in=v_w_in, b_gate=v_b_gate, rel_bias=v_rel_bias, w_ret_out=v_w_ret_out,
             w_att_out=v_w_att_out, w_out=v_w_out, norm_ffn=v_norm_ffn, w_ffn_gate=v_w_ffn_gate, w_ffn_up=v_w_ffn_up,
             w_ffn_down=v_w_ffn_down, norm_final=v_norm_final)
    order = ("norm_mix", "w_in", "b_gate", "rel_bias", "w_ret_out", "w_att_out", "w_out", "norm_ffn",
             "w_ffn_gate", "w_ffn_up", "w_ffn_down", "norm_final")
    small_names = ("norm_mix", "b_gate", "rel_bias", "norm_ffn", "norm_final")

    def flat(a):
        return a[0] if a.ndim == 3 else a.reshape(-1, a.shape[-1])

    grad, delta, new_m, new_v = {}, {}, {}, {}
    for nm in order:
        if nm in transposed:
            res = _adamw(w[nm][0].T, slots[transposed[nm]], m[nm][0].T, v[nm][0].T, "adamw_" + nm)
            grad[nm], delta[nm], new_m[nm], new_v[nm] = (a.T[None] for a in res)
        elif nm in plain:
            res = _adamw(flat(w[nm]), slots[plain[nm]], flat(m[nm]), flat(v[nm]), "adamw_" + nm)
            grad[nm], delta[nm], new_m[nm], new_v[nm] = (a.reshape(w[nm].shape) for a in res)
    ds, nms, nvs = _adamw_small([flat(w[nm]) for nm in small_names], [g[nm] for nm in small_names],
                                [flat(m[nm]) for nm in small_names], [flat(v[nm]) for nm in small_names])
    for i, nm in enumerate(small_names):
        grad[nm], delta[nm], new_m[nm], new_v[nm] = (a.reshape(w[nm].shape) for a in (g[nm], ds[i], nms[i], nvs[i]))

    return (loss, dx, *[grad[nm] for nm in order], *[delta[nm] for nm in order],
            *[new_m[nm] for nm in order], *[new_v[nm] for nm in order])
```

```python
import numpy as np
import jax
import jax.numpy as jnp
from jax import lax
from jax.experimental import pallas as pl
from jax.experimental.pallas import tpu as pltpu

F32 = jnp.float32
BF16 = jnp.bfloat16
MESH = pl.DeviceIdType.MESH

D_MODEL = 1024
CHUNK = 64
RET_HEADS = 4
RET_KEY_DIM = 128
RET_VAL_DIM = 256
ATT_HEADS = 8
BAND_CHUNKS = 8
MAX_REL = 256
N_REL = CHUNK + MAX_REL
D_FF = 2816
N_IN = 6656
ROPE_BASE = 10000.0
EPS = 1e-6
NEG_INF = -1e30
C_RQ, C_RK, C_RV, C_RG, C_AQ, C_AK, C_AV, C_GL = 0, 512, 1024, 2048, 3072, 3584, 4096, 4608

ADAM_LR = 0.001
ADAM_B1 = 0.9
ADAM_B2 = 0.999
ADAM_EPS = 1e-08
ADAM_WD = 0.01
ADAM_STEP = 10

N_DEV = 8
LANES = 128
RET_TILE = 256
ATT_Q = 256
ATT_PAD = BAND_CHUNKS * CHUNK
ATT_WIN = ATT_PAD + ATT_Q
ATT_STARTS = ATT_PAD // ATT_Q
ATT_ROWS = 32
BIAS_LEN = 1024
VMEM_LIMIT = 48 * 1024 * 1024
VMEM_LIMIT_ATT_BWD = 56 * 1024 * 1024


def _params(*sem, vmem=VMEM_LIMIT):
    return pltpu.CompilerParams(dimension_semantics=sem, vmem_limit_bytes=vmem)


def _dot(a, b):
    return lax.dot_general(a, b, (((1,), (0,)), ((), ())), preferred_element_type=F32)


def _dot_nt(a, b):
    return lax.dot_general(a, b, (((1,), (1,)), ((), ())), preferred_element_type=F32)


def _dot_tn(a, b):
    return lax.dot_general(a, b, (((0,), (0,)), ((), ())), preferred_element_type=F32)


def _sigmoid(x):
    return 1.0 / (1.0 + jnp.exp(-x))


def _rms_bwd(x, g, dy):
    r = lax.rsqrt(jnp.mean(x * x, axis=-1, keepdims=True) + EPS)
    u = dy * g
    dx = r * u - x * (r * r * r) * jnp.mean(u * x, axis=-1, keepdims=True)
    return dx, dy * x * r


def _place():
    return lax.axis_index("x"), lax.axis_index("y"), lax.axis_index("c")


def _peer(k):
    x, y, c = _place()
    return ((1 - x) if k & 4 else x, (1 - y) if k & 2 else y, (1 - c) if k & 1 else c)


def _index(place):
    return 4 * place[0] + 2 * place[1] + place[2]


def _rows(ref, block, nrows):
    align = 16 if ref.dtype == BF16 else 8
    return ref.at[pl.ds(pl.multiple_of(block * nrows, align), nrows)]


class _Exchange:
    def __init__(self, arrays, scatter):
        self.arrays, self.scatter, self.n = list(arrays), scatter, len(arrays)

    def out_shape(self):
        if self.scatter:
            return [jax.ShapeDtypeStruct((N_DEV, a.shape[0] // N_DEV) + a.shape[1:], a.dtype) for a in self.arrays]
        return [jax.ShapeDtypeStruct((N_DEV * a.shape[0],) + a.shape[1:], a.dtype) for a in self.arrays]

    def scratch(self):
        return [pltpu.SemaphoreType.DMA((self.n, N_DEV - 1)), pltpu.SemaphoreType.DMA((self.n, N_DEV - 1)),
                pltpu.SemaphoreType.DMA((self.n,))]

    def _copies(self, ins, outs, sems):
        send_sems, recv_sems, local_sems = sems
        me = _index(_place())

        def src(w, to):
            return _rows(ins[w], to, ins[w].shape[0] // N_DEV) if self.scatter else ins[w]

        def dst(w, origin):
            return outs[w].at[origin] if self.scatter else _rows(outs[w], origin, ins[w].shape[0])

        def remote(w, k, to, origin):
            return pltpu.make_async_remote_copy(src_ref=src(w, to), dst_ref=dst(w, origin),
                                                send_sem=send_sems.at[w, k - 1], recv_sem=recv_sems.at[w, k - 1],
                                                device_id=_peer(k), device_id_type=MESH)

        pairs = [(w, k) for w in range(self.n) for k in range(1, N_DEV)]
        own = lambda: [pltpu.make_async_copy(src(w, me), dst(w, me), local_sems.at[w]) for w in range(self.n)]
        sent = lambda: [remote(w, k, _index(_peer(k)), me) for w, k in pairs]
        arriving = lambda: [remote(w, k, me, _index(_peer(k))) for w, k in pairs]
        return own, sent, arriving

    def start(self, ins, outs, sems):
        own, sent, _ = self._copies(ins, outs, sems)
        for cp in own() + sent():
            cp.start()

    def wait(self, ins, outs, sems):
        own, sent, arriving = self._copies(ins, outs, sems)
        for cp in arriving():
            cp.wait_recv()
        for cp in sent():
            cp.wait_send()
        for cp in own():
            cp.wait()


class _PairSwap:
    def __init__(self, arrays):
        self.arrays, self.n = list(arrays), len(arrays)

    def out_shape(self):
        return [jax.ShapeDtypeStruct((4, a.shape[0] // N_DEV) + a.shape[1:], a.dtype) for a in self.arrays]

    def scratch(self):
        return [pltpu.SemaphoreType.DMA((self.n, 4)), pltpu.SemaphoreType.DMA((self.n, 4))]

    def _copies(self, ins, outs, sems):
        send_sems, recv_sems = sems
        x, y, c = _place()
        return [pltpu.make_async_remote_copy(
            src_ref=_rows(ins[w], 2 * q + 1 - c, ins[w].shape[0] // N_DEV), dst_ref=outs[w].at[q],
            send_sem=send_sems.at[w, q], recv_sem=recv_sems.at[w, q],
            device_id=(x, y, 1 - c), device_id_type=MESH) for w in range(self.n) for q in range(4)]

    def start(self, ins, outs, sems):
        for cp in self._copies(ins, outs, sems):
            cp.start()

    def wait(self, ins, outs, sems):
        for cp in self._copies(ins, outs, sems):
            cp.wait()


class _ChipScatter:
    def __init__(self, arrays):
        self.arrays, self.n = list(arrays), len(arrays)

    def out_shape(self):
        return [jax.ShapeDtypeStruct(a.shape, a.dtype) for a in self.arrays]

    def scratch(self):
        return [pltpu.SemaphoreType.DMA((self.n, 3)), pltpu.SemaphoreType.DMA((self.n, 3)),
                pltpu.SemaphoreType.DMA((self.n,))]

    def _copies(self, ins, outs, sems):
        send_sems, recv_sems, local_sems = sems
        x, y, c = _place()
        mine = 2 * x + y
        sent, arriving = [], []
        for w in range(self.n):
            for k in range(1, 4):
                tx, ty = (1 - x) if k & 2 else x, (1 - y) if k & 1 else y
                other = 2 * tx + ty
                sent.append(lambda w=w, k=k, tx=tx, ty=ty, other=other: pltpu.make_async_remote_copy(
                    src_ref=ins[w].at[other], dst_ref=outs[w].at[mine], send_sem=send_sems.at[w, k - 1],
                    recv_sem=recv_sems.at[w, k - 1], device_id=(tx, ty, c), device_id_type=MESH))
                arriving.append(lambda w=w, k=k, tx=tx, ty=ty, other=other: pltpu.make_async_remote_copy(
                    src_ref=ins[w].at[mine], dst_ref=outs[w].at[other], send_sem=send_sems.at[w, k - 1],
                    recv_sem=recv_sems.at[w, k - 1], device_id=(tx, ty, c), device_id_type=MESH))
        own = [lambda w=w: pltpu.make_async_copy(ins[w].at[mine], outs[w].at[mine], local_sems.at[w])
               for w in range(self.n)]
        return own, sent, arriving

    def start(self, ins, outs, sems):
        own, sent, _ = self._copies(ins, outs, sems)
        for cp in own + sent:
            cp().start()

    def wait(self, ins, outs, sems):
        own, sent, arriving = self._copies(ins, outs, sems)
        for cp in arriving:
            cp().wait_recv()
        for cp in sent:
            cp().wait_send()
        for cp in own:
            cp().wait()


class _ChipGather:
    def __init__(self, arrays, parts=None, into=None):
        self.arrays, self.n, self.into = list(arrays), len(arrays), into
        self.parts = parts or [(0, 1)] * self.n

    def out_shape(self):
        return [jax.ShapeDtypeStruct((N_DEV * a.shape[0],) + a.shape[1:], a.dtype) for a in self.arrays]

    def scratch(self):
        return [pltpu.SemaphoreType.DMA((self.n, N_DEV - 1)), pltpu.SemaphoreType.DMA((self.n, N_DEV - 1)),
                pltpu.SemaphoreType.DMA((self.n,))]

    def _parts(self, ins, outs, sems):
        send_sems, recv_sems, local_sems = sems
        x, y, c = _place()
        me, sibling = (x, y, c), (x, y, 1 - c)
        chips = [(1 - x, y), (x, 1 - y), (1 - x, 1 - y)]

        def rows(w, place, whole):
            (index, count), r = self.parts[w], ins[w].shape[0]
            lo, size = (0, r) if whole else (index * (r // count), r // count)
            align = 16 if ins[w].dtype == BF16 else 8
            return outs[w].at[pl.ds(pl.multiple_of(_index(place) * r + lo, align), size)]

        def mine(w, whole):
            (index, count), r = self.parts[w], ins[w].shape[0]
            return ins[w] if whole or count == 1 else ins[w].at[pl.ds(index * (r // count), r // count)]

        def copy(w, k, block, to, own=False):
            whole = k == 0
            return pltpu.make_async_remote_copy(src_ref=mine(w, whole) if own else rows(w, block, whole),
                                                dst_ref=rows(w, block, whole),
                                                send_sem=send_sems.at[w, k], recv_sem=recv_sems.at[w, k],
                                                device_id=to, device_id_type=MESH)

        def local(w):
            return pltpu.make_async_copy(ins[w], rows(w, me, True), local_sems.at[w])

        return me, sibling, chips, c, copy, local, [index == 0 for index, _ in self.parts]

    def start(self, ins, outs, sems):
        me, sibling, chips, c, copy, local, places_own = self._parts(ins, outs, sems)
        for w in range(self.n):
            if places_own[w]:
                local(w).start()
                copy(w, 0, me, sibling, own=True).start()
            for j, chip in enumerate(chips):
                copy(w, 1 + j, me, (*chip, c), own=True).start()

    def wait(self, ins, outs, sems):
        me, sibling, chips, c, copy, local, places_own = self._parts(ins, outs, sems)
        for w in range(self.n):
            for j, chip in enumerate(chips):
                copy(w, 1 + j, (*chip, c), me).wait_recv()
                copy(w, 4 + j, (*chip, c), sibling).start()
        for w in range(self.n):
            if places_own[w]:
                copy(w, 0, sibling, me).wait_recv()
                copy(w, 0, me, sibling, own=True).wait_send()
                local(w).wait()
            for j, chip in enumerate(chips):
                copy(w, 4 + j, (*chip, 1 - c), me).wait_recv()
                copy(w, 1 + j, me, (*chip, c), own=True).wait_send()
                copy(w, 4 + j, (*chip, c), sibling).wait_send()


def _call(body, *, name, grid, in_specs, out_specs, out_shape, scratch=(), semantics, args, exchange=None,
          vmem=VMEM_LIMIT):
    if exchange is None:
        return pl.pallas_call(body, name=name, grid=grid, in_specs=in_specs, out_specs=out_specs, out_shape=out_shape,
                              scratch_shapes=list(scratch),
                              compiler_params=_params(*semantics, vmem=vmem))(*args), None
    n_in, n_out, n_scr, nx = len(in_specs), len(out_specs), len(scratch), exchange.n
    into = list(getattr(exchange, "into", None) or [])

    def full_body(*refs):
        ins, refs = refs[:n_in], refs[n_in:]
        x_in, refs = refs[:nx], refs[nx + len(into):]
        outs, refs = refs[:n_out], refs[n_out:]
        x_out, refs = refs[:nx], refs[nx:]
        scr, sems = refs[:n_scr], refs[n_scr:]
        first, last = True, True
        for axis, size in enumerate(grid):
            first = jnp.logical_and(first, pl.program_id(axis) == 0)
            last = jnp.logical_and(last, pl.program_id(axis) == size - 1)
        if grid:
            pl.when(first)(lambda: exchange.start(x_in, x_out, sems))
        else:
            exchange.start(x_in, x_out, sems)
        body(*ins, *outs, *scr)
        if grid:
            pl.when(last)(lambda: exchange.wait(x_in, x_out, sems))
        else:
            exchange.wait(x_in, x_out, sems)

    hbm = pl.BlockSpec(memory_space=pltpu.HBM)
    res = pl.pallas_call(
        full_body, name=name, grid=grid,
        in_specs=list(in_specs) + [hbm] * (nx + len(into)), out_specs=list(out_specs) + [hbm] * nx,
        out_shape=list(out_shape) + exchange.out_shape(),
        scratch_shapes=list(scratch) + exchange.scratch(),
        input_output_aliases={n_in + nx + w: n_out + w for w in range(len(into))},
        compiler_params=_params(*(["arbitrary"] * len(grid)), vmem=vmem),
    )(*args, *exchange.arrays, *into)
    return res[:n_out], res[n_out:]


def _alone(exchange, name):
    return _call(lambda: None, name=name, grid=(), in_specs=[], out_specs=[], out_shape=[], semantics=(),
                 args=(), exchange=exchange)[1]


def _mm(a, b, *, ta=False, tb=False, out_dtype, tm, tn, tk, name, exchange=None):
    m, k = (a.shape[1], a.shape[0]) if ta else a.shape
    n = b.shape[0] if tb else b.shape[1]
    assert k == (b.shape[1] if tb else b.shape[0])
    tm, tn, tk = min(tm, m), min(tn, n), min(tk, k)
    assert m % tm == 0 and n % tn == 0 and k % tk == 0, (name, m, n, k)
    nk = k // tk
    dims = (((0 if ta else 1,), (1 if tb else 0,)), ((), ()))

    def body(a_ref, b_ref, o_ref, *acc):
        prod = lax.dot_general(a_ref[...].astype(BF16), b_ref[...].astype(BF16), dims, preferred_element_type=F32)
        if nk == 1:
            o_ref[...] = prod.astype(o_ref.dtype)
            return
        acc_ref, kk = acc[0], pl.program_id(2)

        @pl.when(kk == 0)
        def _():
            acc_ref[...] = prod

        @pl.when((kk > 0) & (kk < nk - 1))
        def _():
            acc_ref[...] += prod

        @pl.when(kk == nk - 1)
        def _():
            o_ref[...] = (acc_ref[...] + prod).astype(o_ref.dtype)

    a_spec = (pl.BlockSpec((tk, tm), lambda i, j, kk: (kk, i)) if ta
              else pl.BlockSpec((tm, tk), lambda i, j, kk: (i, kk)))
    b_spec = (pl.BlockSpec((tn, tk), lambda i, j, kk: (j, kk)) if tb
              else pl.BlockSpec((tk, tn), lambda i, j, kk: (kk, j)))
    (out,), moved = _call(
        body, name=name, grid=(m // tm, n // tn, nk),
        in_specs=[a_spec, b_spec],
        out_specs=[pl.BlockSpec((tm, tn), lambda i, j, kk: (i, j))],
        out_shape=[pltpu.HBM((m, n), out_dtype)],
        scratch=[pltpu.VMEM((tm, tn), F32)] if nk > 1 else [],
        semantics=("parallel", "parallel", "arbitrary"), args=(a, b), exchange=exchange)
    return out if exchange is None else (out, moved)


def _mm_pieces(pieces, b, *, ta, out_dtype, tm, tn, tk, name, exchange=None):
    rows, n = pieces[0].shape[0], b.shape[1]
    step = tm if ta else tk
    assert all(p.shape[0] == rows and p.shape[1] % step == 0 for p in pieces), name
    edges = [int(e) for e in np.cumsum([0] + [p.shape[1] // step for p in pieces])]
    total = edges[-1] * step
    m, k = (total, rows) if ta else (rows, total)
    assert b.shape[0] == k and m % tm == 0 and n % tn == 0 and k % tk == 0, name
    nk, npieces = k // tk, len(pieces)
    dims = (((0 if ta else 1,), (0,)), ((), ()))
    b_resident = ta and n == tn

    def body(*refs):
        a_refs, (b_ref, o_ref, acc_ref) = refs[:npieces], refs[npieces:]
        kk = pl.program_id(2)
        pos = pl.program_id(0) if ta else kk

        @pl.when(kk == 0)
        def _():
            acc_ref[...] = jnp.zeros_like(acc_ref)

        def b_tile():
            return b_ref[pl.ds(pl.multiple_of(kk * tk, tk), tk), :] if b_resident else b_ref[...]

        for p, a_ref in enumerate(a_refs):
            @pl.when((pos >= edges[p]) & (pos < edges[p + 1]))
            def _(a_ref=a_ref):
                acc_ref[...] += lax.dot_general(a_ref[...], b_tile(), dims, preferred_element_type=F32)

        @pl.when(kk == nk - 1)
        def _():
            o_ref[...] = acc_ref[...].astype(o_ref.dtype)

    def a_spec(p):
        lo, last = edges[p], edges[p + 1] - edges[p] - 1
        if ta:
            def index(i, j, kk):
                inside = (i >= lo) & (i <= lo + last)
                return jnp.where(inside, kk, 0), jnp.clip(i - lo, 0, last)
            return pl.BlockSpec((tk, tm), index)
        return pl.BlockSpec((tm, tk), lambda i, j, kk: (i, jnp.clip(kk - lo, 0, last)))

    (out,), moved = _call(
        body, name=name, grid=(m // tm, n // tn, nk),
        in_specs=[a_spec(p) for p in range(npieces)]
                 + [pl.BlockSpec(b.shape, lambda i, j, kk: (0, 0)) if b_resident
                    else pl.BlockSpec((tk, tn), lambda i, j, kk: (kk, j))],
        out_specs=[pl.BlockSpec((tm, tn), lambda i, j, kk: (i, j))],
        out_shape=[pltpu.HBM((m, n), out_dtype)],
        scratch=[pltpu.VMEM((tm, tn), F32)],
        semantics=("parallel", "parallel", "arbitrary"), args=(*pieces, b), exchange=exchange)
    return out if exchange is None else (out, moved)


def _rms_fwd(x2, g, exchange):
    t = x2.shape[0]
    tm = min(512, t)

    def body(x_ref, g_ref, o_ref):
        x = x_ref[...]
        r = lax.rsqrt(jnp.mean(x * x, axis=-1, keepdims=True) + EPS)
        o_ref[...] = (x * r * g_ref[...]).astype(o_ref.dtype)

    return _call(
        body, name="rms_in_fwd", grid=(t // tm,),
        in_specs=[pl.BlockSpec((tm, D_MODEL), lambda i: (i, 0)), pl.BlockSpec((1, D_MODEL), lambda i: (0, 0))],
        out_specs=[pl.BlockSpec((tm, D_MODEL), lambda i: (i, 0))],
        out_shape=[jax.ShapeDtypeStruct((t, D_MODEL), BF16)],
        semantics=("parallel",), args=(x2, g), exchange=exchange)


def _decay(lg):
    row = lax.broadcasted_iota(jnp.int32, (RET_TILE, RET_TILE), 0)
    col = lax.broadcasted_iota(jnp.int32, (RET_TILE, RET_TILE), 1)
    within = jnp.exp(lg * jnp.abs(row - col).astype(F32))
    inside = jnp.where((col >> 6) <= (row >> 6), within, 0.0)
    pos = lax.broadcasted_iota(jnp.int32, (RET_TILE, 1), 0).astype(F32)
    q_dec = jnp.exp(lg * (pos + 1.0))
    k_dec = jnp.exp(lg * (RET_TILE - 1.0 - pos))
    tile_dec = jnp.exp(lg * float(RET_TILE))
    return inside, q_dec, k_dec, tile_dec


def _scaled(a_bf16, dec):
    return (a_bf16.astype(F32) * dec).astype(BF16)


RET_GROUP = 1


def _per_head(one_head, kinds):
    def body(*refs):
        for head in range(RET_GROUP):
            def cut(ref, kind):
                if kind in "kv":
                    width = RET_KEY_DIM if kind == "k" else RET_VAL_DIM
                    return ref.at[:, head * width:(head + 1) * width]
                return ref.at[head] if kind == "h" else ref
            one_head(*[cut(ref, kind) for ref, kind in zip(refs, kinds)])
    return body


def _ret_specs(seq):
    kw, vw = RET_GROUP * RET_KEY_DIM, RET_GROUP * RET_VAL_DIM
    key = lambda base: pl.BlockSpec((seq, kw), lambda b, g: (b, base // kw + g))
    val = lambda base: pl.BlockSpec((seq, vw), lambda b, g: (b, base // vw + g))
    tab = pl.BlockSpec((seq, RET_KEY_DIM), lambda b, g: (0, 0))
    lgs = pl.BlockSpec((RET_GROUP, 1, LANES), lambda b, g: (g, 0, 0))
    return key, val, tab, lgs


def _ret_fwd(proj, cs, sn, lg_arr, batch, seq, exchange):
    t = batch * seq
    nt = seq // RET_TILE

    def one_head(q_ref, k_ref, v_ref, rg_ref, cs_ref, sn_ref, lg_ref, gro_ref, o_ref, qr_ref, kr_ref):
        lg = lg_ref[:, 0:1]
        cs_t, sn_t = cs_ref[...], sn_ref[...]
        q = q_ref[...].astype(F32)
        k = k_ref[...].astype(F32)
        qr_ref[...] = (q * cs_t + pltpu.roll(q, 64, 1) * sn_t).astype(BF16)
        kr_ref[...] = ((k * cs_t + pltpu.roll(k, 64, 1) * sn_t) * (RET_KEY_DIM ** -0.5)).astype(BF16)
        inside, q_dec, k_dec, tile_dec = _decay(lg)
        state = jnp.zeros((RET_KEY_DIM, RET_VAL_DIM), F32)
        for i in range(nt):
            rows = slice(i * RET_TILE, (i + 1) * RET_TILE)
            qi, ki, vi = qr_ref[rows, :], kr_ref[rows, :], v_ref[rows, :]
            acc = _dot((_dot_nt(qi, ki) * inside).astype(BF16), vi)
            if i > 0:
                acc = acc + _dot(_scaled(qi, q_dec), state.astype(BF16))
            if i < nt - 1:
                state = state * tile_dec + _dot_tn(_scaled(ki, k_dec), vi)
            o_ref[rows, :] = acc.astype(BF16)
            xc = acc - jnp.mean(acc, axis=-1, keepdims=True)
            nrm = xc * lax.rsqrt(jnp.mean(xc * xc, axis=-1, keepdims=True) + EPS)
            rg = rg_ref[rows, :].astype(F32)
            gro_ref[rows, :] = (rg * _sigmoid(rg) * nrm).astype(BF16)

    key, val, tab, lgs = _ret_specs(seq)
    return _call(
        _per_head(one_head, "kkvvsshvvkk"), name="ret_fwd", grid=(batch, RET_HEADS // RET_GROUP),
        in_specs=[key(C_RQ), key(C_RK), val(C_RV), val(C_RG), tab, tab, lgs],
        out_specs=[val(0), val(0), key(0), key(0)],
        out_shape=[jax.ShapeDtypeStruct((t, RET_HEADS * RET_VAL_DIM), BF16),
                   jax.ShapeDtypeStruct((t, RET_HEADS * RET_VAL_DIM), BF16),
                   jax.ShapeDtypeStruct((t, RET_HEADS * RET_KEY_DIM), BF16),
                   jax.ShapeDtypeStruct((t, RET_HEADS * RET_KEY_DIM), BF16)],
        semantics=("parallel", "parallel"), args=(proj, proj, proj, proj, cs, sn, lg_arr), exchange=exchange)


def _att_bias(w_ref, bias_ref):
    n_i = lax.broadcasted_iota(jnp.int32, (ATT_Q, BIAS_LEN), 0)
    qc = lax.broadcasted_iota(jnp.int32, (ATT_Q, ATT_WIN), 0) >> 6
    kc = lax.broadcasted_iota(jnp.int32, (ATT_Q, ATT_WIN), 1) >> 6
    dc = qc + BAND_CHUNKS - kc
    band = (dc >= 0) & (dc <= BAND_CHUNKS)
    key = lax.broadcasted_iota(jnp.int32, (ATT_Q, ATT_WIN), 1)
    for e in range(2):
        xw = jnp.broadcast_to(w_ref[e:e + 1, :], (ATT_Q, BIAS_LEN))
        for bit in range(8):
            xw = jnp.where(((n_i >> bit) & 1) == 1, pltpu.roll(xw, 1 << bit, 1), xw)
        bias = jnp.where(band, xw[:, BIAS_LEN - ATT_WIN:], NEG_INF)
        for first in range(ATT_STARTS):
            bias_ref[first, e] = jnp.where(key + (first * ATT_Q - ATT_PAD) >= 0, bias, NEG_INF)
        bias_ref[ATT_STARTS, e] = bias


ATT_PAIRS = 2
ATT_COLS = ATT_PAIRS * LANES


def _att_specs(batch, seq):
    ni = seq // ATT_Q
    q_spec = pl.BlockSpec((ATT_Q, ATT_COLS), lambda g, b, i: (b * ni + i, C_AQ // ATT_COLS + g))
    k_spec = pl.BlockSpec((seq, ATT_COLS), lambda g, b, i: (b, C_AK // ATT_COLS + g))
    v_spec = pl.BlockSpec((seq, ATT_COLS), lambda g, b, i: (b, C_AV // ATT_COLS + g))
    w_spec = pl.BlockSpec((ATT_PAIRS, 2, BIAS_LEN), lambda g, b, i: (g, 0, 0))
    b_spec = pl.BlockSpec((ATT_PAIRS, ATT_STARTS + 1, 2, ATT_Q, ATT_WIN), lambda g, b, i: (g, 0, 0, 0, 0))
    pad = pltpu.VMEM((seq + ATT_PAD, ATT_COLS), BF16)
    return ni, q_spec, k_spec, v_spec, w_spec, b_spec, pad


def _att_bias_tiles(wvec, exchange):
    (tiles,), moved = _call(
        lambda w_ref, o_ref: _att_bias(w_ref, o_ref), name="att_bias", grid=(ATT_HEADS // 2,),
        in_specs=[pl.BlockSpec((None, 2, BIAS_LEN), lambda hp: (hp, 0, 0))],
        out_specs=[pl.BlockSpec((None, ATT_STARTS + 1, 2, ATT_Q, ATT_WIN), lambda hp: (hp, 0, 0, 0, 0))],
        out_shape=[jax.ShapeDtypeStruct((ATT_HEADS // 2, ATT_STARTS + 1, 2, ATT_Q, ATT_WIN), F32)],
        semantics=("parallel",), args=(wvec,), exchange=exchange)
    return tiles, moved


def _att_pad(src_ref, pad_ref):
    pad_ref[:ATT_PAD, :] = jnp.zeros((ATT_PAD, ATT_COLS), BF16)
    pad_ref[ATT_PAD:, :] = src_ref[...]


def _att_head(q2, sel):
    return jnp.where(sel, q2, jnp.zeros_like(q2)) * 0.125


def _att_softmax_rows(s_ref, bias_ref, rows):
    s = s_ref[rows, :] + bias_ref[rows, :]
    ex = jnp.exp(s - jnp.max(s, axis=-1, keepdims=True))
    return ex, 1.0 / jnp.sum(ex, axis=-1, keepdims=True)


def _att_fwd(proj, bias, batch, seq, exchange):
    ni, q_spec, k_spec, v_spec, _, b_spec, pad = _att_specs(batch, seq)

    def body(q_ref, k_ref, v_ref, bias_ref, o_ref, kp_ref, vp_ref, s_ref, e_ref):
        i = pl.program_id(2)

        @pl.when(i == 0)
        def _():
            _att_pad(k_ref, kp_ref)
            _att_pad(v_ref, vp_ref)

        win = pl.ds(pl.multiple_of(i * ATT_Q, ATT_Q), ATT_WIN)
        lo = lax.broadcasted_iota(jnp.int32, (1, LANES), 1) < 64
        start = jnp.minimum(i, ATT_STARTS)
        for pair in range(ATT_PAIRS):
            cols = slice(pair * LANES, (pair + 1) * LANES)
            k2, v2, q2 = kp_ref[win, cols], vp_ref[win, cols], q_ref[:, cols]
            out = jnp.zeros((ATT_Q, LANES), F32)
            for e in range(2):
                h = 2 * pair + e
                sel = lo if e == 0 else jnp.logical_not(lo)
                s_ref[h] = _dot_nt(_att_head(q2, sel), k2)
                rsum = []
                for c in range(ATT_Q // ATT_ROWS):
                    rows = slice(c * ATT_ROWS, (c + 1) * ATT_ROWS)
                    ex, r = _att_softmax_rows(s_ref.at[h], bias_ref.at[pair, start, e], rows)
                    e_ref[h, rows, :] = ex.astype(BF16)
                    rsum.append(r)
                out = out + _dot(e_ref[h], jnp.where(sel, v2, jnp.zeros_like(v2))) * jnp.concatenate(rsum, axis=0)
            o_ref[:, cols] = out.astype(BF16)

    heads = 2 * ATT_PAIRS
    return _call(
        body, name="att_fwd", grid=(ATT_HEADS // heads, batch, ni),
        in_specs=[q_spec, k_spec, v_spec, b_spec],
        out_specs=[pl.BlockSpec((ATT_Q, ATT_COLS), lambda g, b, i: (b * ni + i, g))],
        out_shape=[jax.ShapeDtypeStruct((batch * seq, ATT_HEADS * 64), BF16)],
        scratch=[pad, pad, pltpu.VMEM((heads, ATT_Q, ATT_WIN), F32), pltpu.VMEM((heads, ATT_Q, ATT_WIN), BF16)],
        semantics=("arbitrary", "arbitrary", "arbitrary"), args=(proj, proj, proj, bias), exchange=exchange)


GL_HALF = 512


def _gl_specs(tm):
    return [pl.BlockSpec((tm, GL_HALF), lambda i, c=C_GL // GL_HALF + j: (i, c)) for j in range(4)]


def _gates(gl_refs, b_ref):
    logits = [ref[...].astype(F32) for ref in gl_refs]
    gr = _sigmoid(jnp.concatenate(logits[:2], axis=1) + b_ref[:, :D_MODEL])
    ga = _sigmoid(jnp.concatenate(logits[2:], axis=1) + b_ref[:, D_MODEL:])
    return gr, ga


def _whole(a):
    return pl.BlockSpec(a.shape, lambda i: (0,) * a.ndim)


def _mix_out_fwd(gro, ao, proj, b_gate, w_ret, w_att_t, x2, w_out, g2):
    t = gro.shape[0]
    tm = min(256, t)

    def body(gro_ref, ao_ref, gl0, gl1, gl2, gl3, b_ref, wr_ref, wa_ref, x_ref, wo_ref, g_ref,
             z_ref, yr_ref, ya_ref, h_ref, hn_ref):
        yr = _dot(gro_ref[...], wr_ref[...])
        ya = _dot_nt(ao_ref[...], wa_ref[...])
        yr_ref[...] = yr.astype(BF16)
        ya_ref[...] = ya.astype(BF16)
        gr, ga = _gates((gl0, gl1, gl2, gl3), b_ref)
        z = (gr * yr + ga * ya).astype(BF16)
        z_ref[...] = z
        h = x_ref[...] + _dot(z, wo_ref[...])
        h_ref[...] = h
        r = lax.rsqrt(jnp.mean(h * h, axis=-1, keepdims=True) + EPS)
        hn_ref[...] = (h * r * g_ref[...]).astype(BF16)

    row = pl.BlockSpec((tm, D_MODEL), lambda i: (i, 0))
    return pl.pallas_call(
        body, name="mix_out_fwd", grid=(t // tm,),
        in_specs=[row, pl.BlockSpec((tm, 512), lambda i: (i, 0)), *_gl_specs(tm),
                  _whole(b_gate), _whole(w_ret), _whole(w_att_t), row, _whole(w_out), _whole(g2)],
        out_specs=[row] * 5,
        out_shape=[jax.ShapeDtypeStruct((t, D_MODEL), BF16)] * 3
                  + [jax.ShapeDtypeStruct((t, D_MODEL), F32), jax.ShapeDtypeStruct((t, D_MODEL), BF16)],
        compiler_params=_params("parallel"),
    )(gro, ao, proj, proj, proj, proj, b_gate, w_ret, w_att_t, x2, w_out, g2)


def _col_chunks(width, chunk=384):
    return [slice(lo, min(lo + chunk, width)) for lo in range(0, width, chunk)]


def _ffn_up(hn, wg_t, wu_t):
    t = hn.shape[0]
    tm, tn = min(512, t), D_FF // 2

    def body(h_ref, wg_ref, wu_ref, g_ref, u_ref, a_ref):
        g = _dot_nt(h_ref[...], wg_ref[...])
        u = _dot_nt(h_ref[...], wu_ref[...])
        g_ref[...] = g.astype(BF16)
        u_ref[...] = u.astype(BF16)
        a_ref[...] = (g * _sigmoid(g) * u).astype(BF16)

    w_spec = pl.BlockSpec((tn, D_MODEL), lambda j, i: (j, 0))
    out = pl.BlockSpec((tm, tn), lambda j, i: (i, j))
    return pl.pallas_call(
        body, name="ffn_up", grid=(D_FF // tn, t // tm),
        in_specs=[pl.BlockSpec((tm, D_MODEL), lambda j, i: (i, 0)), w_spec, w_spec],
        out_specs=[out, out, out],
        out_shape=[jax.ShapeDtypeStruct((t, D_FF), BF16)] * 3,
        compiler_params=_params("parallel", "parallel"),
    )(hn, wg_t, wu_t)


def _ffn_down_loss(a, h1, tgt, w_down, g3):
    t = a.shape[0]
    tm = min(512, t)

    def body(a_ref, h_ref, t_ref, w_ref, g_ref, dh_ref, dhb_ref, loss_ref, dg_ref):
        @pl.when(pl.program_id(0) == 0)
        def _():
            loss_ref[...] = jnp.zeros_like(loss_ref)
            dg_ref[...] = jnp.zeros_like(dg_ref)

        g = g_ref[...]
        h2 = h_ref[...] + _dot(a_ref[...], w_ref[...])
        r = lax.rsqrt(jnp.mean(h2 * h2, axis=-1, keepdims=True) + EPS)
        err = h2 * r * g - t_ref[...]
        loss_ref[...] += jnp.sum(err * err) * (0.5 / D_MODEL)
        dy = err * (1.0 / D_MODEL)
        dh, dg_rows = _rms_bwd(h2, g, dy)
        dg_ref[...] += jnp.sum(dg_rows, axis=0, keepdims=True)
        dh_ref[...] = dh
        dhb_ref[...] = dh.astype(BF16)

    row = pl.BlockSpec((tm, D_MODEL), lambda i: (i, 0))
    vec = pl.BlockSpec((1, D_MODEL), lambda i: (0, 0))
    return pl.pallas_call(
        body, name="ffn_down_loss", grid=(t // tm,),
        in_specs=[pl.BlockSpec((tm, D_FF), lambda i: (i, 0)), row, row,
                  pl.BlockSpec((D_FF, D_MODEL), lambda i: (0, 0)), vec],
        out_specs=[row, row, pl.BlockSpec((1, LANES), lambda i: (0, 0)), vec],
        out_shape=[jax.ShapeDtypeStruct((t, D_MODEL), F32), jax.ShapeDtypeStruct((t, D_MODEL), BF16),
                   jax.ShapeDtypeStruct((1, LANES), F32), jax.ShapeDtypeStruct((1, D_MODEL), F32)],
        compiler_params=_params("arbitrary"),
    )(a, h1, tgt, w_down, g3)


def _ffn_bwd_act(dh2b, w_down, g_act, u_act, exchange):
    t = dh2b.shape[0]
    tm, tn = min(512, t), D_FF // 2

    def body(d_ref, w_ref, g_ref, u_ref, dg_ref, du_ref):
        d = d_ref[...]
        for cols in _col_chunks(tn):
            da = _dot_nt(d, w_ref[cols, :])
            g = g_ref[:, cols].astype(F32)
            u = u_ref[:, cols].astype(F32)
            sg = _sigmoid(g)
            dg_ref[:, cols] = (da * u * sg * (1.0 + g * (1.0 - sg))).astype(BF16)
            du_ref[:, cols] = (da * g * sg).astype(BF16)

    blk = pl.BlockSpec((tm, tn), lambda j, i: (i, j))
    return _call(
        body, name="ffn_bwd_act", grid=(D_FF // tn, t // tm),
        in_specs=[pl.BlockSpec((tm, D_MODEL), lambda j, i: (i, 0)),
                  pl.BlockSpec((tn, D_MODEL), lambda j, i: (j, 0)), blk, blk],
        out_specs=[blk, blk],
        out_shape=[jax.ShapeDtypeStruct((t, D_FF), BF16)] * 2,
        semantics=("parallel", "parallel"), args=(dh2b, w_down, g_act, u_act), exchange=exchange)


def _ffn_bwd_in(dg, du, wg_t, wu_t, h1, dh2, g2, exchange):
    t = dg.shape[0]
    tm = min(256, t)

    def body(dg_ref, du_ref, wg_ref, wu_ref, h_ref, d2_ref, g_ref, dh_ref, dhb_ref, gn_ref):
        @pl.when(pl.program_id(0) == 0)
        def _():
            gn_ref[...] = jnp.zeros_like(gn_ref)

        dhn = _dot(dg_ref[...], wg_ref[...]) + _dot(du_ref[...], wu_ref[...])
        dx, dg_rows = _rms_bwd(h_ref[...], g_ref[...], dhn)
        dh = d2_ref[...] + dx
        dh_ref[...] = dh
        dhb_ref[...] = dh.astype(BF16)
        gn_ref[...] += jnp.sum(dg_rows, axis=0, keepdims=True)

    act = pl.BlockSpec((tm, D_FF), lambda i: (i, 0))
    row = pl.BlockSpec((tm, D_MODEL), lambda i: (i, 0))
    return _call(
        body, name="ffn_bwd_in", grid=(t // tm,),
        in_specs=[act, act, _whole(wg_t), _whole(wu_t), row, row, _whole(g2)],
        out_specs=[row, row, _whole(g2)],
        out_shape=[jax.ShapeDtypeStruct((t, D_MODEL), F32), jax.ShapeDtypeStruct((t, D_MODEL), BF16),
                   jax.ShapeDtypeStruct((1, D_MODEL), F32)],
        semantics=("arbitrary",), args=(dg, du, wg_t, wu_t, h1, dh2, g2), exchange=exchange)


def _mix_bwd(dh1b, w_out, proj, b_gate, y_ret, y_att, w_ret, w_att_t):
    t = dh1b.shape[0]
    tm = min(256, t)

    def body(d_ref, wo_ref, gl0, gl1, gl2, gl3, b_ref, yr_ref, ya_ref, wr_ref, wa_ref,
             dyr_ref, dya_ref, dglr_ref, dgla_ref, dgro_ref, dao_ref, db_ref):
        @pl.when(pl.program_id(0) == 0)
        def _():
            db_ref[...] = jnp.zeros_like(db_ref)

        dz = _dot_nt(d_ref[...], wo_ref[...])
        gr, ga = _gates((gl0, gl1, gl2, gl3), b_ref)
        dyr = (dz * gr).astype(BF16)
        dya = (dz * ga).astype(BF16)
        dyr_ref[...] = dyr
        dya_ref[...] = dya
        dglr = dz * yr_ref[...].astype(F32) * gr * (1.0 - gr)
        dgla = dz * ya_ref[...].astype(F32) * ga * (1.0 - ga)
        dglr_ref[...] = dglr.astype(BF16)
        dgla_ref[...] = dgla.astype(BF16)
        db_ref[:, :D_MODEL] += jnp.sum(dglr, axis=0, keepdims=True)
        db_ref[:, D_MODEL:] += jnp.sum(dgla, axis=0, keepdims=True)
        dgro_ref[...] = _dot_nt(dyr, wr_ref[...]).astype(BF16)
        dao_ref[...] = _dot(dya, wa_ref[...]).astype(BF16)

    row = pl.BlockSpec((tm, D_MODEL), lambda i: (i, 0))
    half = pl.BlockSpec((tm, 512), lambda i: (i, 0))
    return pl.pallas_call(
        body, name="mix_bwd", grid=(t // tm,),
        in_specs=[row, _whole(w_out), *_gl_specs(tm), _whole(b_gate), row, row, _whole(w_ret), _whole(w_att_t)],
        out_specs=[row, row, row, row, row, half, _whole(b_gate)],
        out_shape=[jax.ShapeDtypeStruct((t, D_MODEL), BF16)] * 5
                  + [jax.ShapeDtypeStruct((t, 512), BF16), jax.ShapeDtypeStruct(b_gate.shape, F32)],
        compiler_params=_params("arbitrary"),
    )(dh1b, w_out, proj, proj, proj, proj, b_gate, y_ret, y_att, w_ret, w_att_t)


def _ret_bwd(dgro, proj, o_ret, qr, kr, cs, sn, lg_arr, batch, seq, exchange):
    t = batch * seq
    nt = seq // RET_TILE

    def one_head(dgro_ref, rg_ref, o_ref, qr_ref, kr_ref, v_ref, cs_ref, sn_ref, lg_ref,
                 dq_ref, dk_ref, dv_ref, drg_ref, do_ref, st_ref):
        lg = lg_ref[:, 0:1]
        inside, q_dec, k_dec, tile_dec = _decay(lg)

        state = jnp.zeros((RET_KEY_DIM, RET_VAL_DIM), F32)
        for i in range(nt - 1):
            rows = slice(i * RET_TILE, (i + 1) * RET_TILE)
            state = state * tile_dec + _dot_tn(_scaled(kr_ref[rows, :], k_dec), v_ref[rows, :])
            st_ref[i + 1] = state.astype(BF16)

        for i in range(nt):
            rows = slice(i * RET_TILE, (i + 1) * RET_TILE)
            o = o_ref[rows, :].astype(F32)
            xc = o - jnp.mean(o, axis=-1, keepdims=True)
            rs = lax.rsqrt(jnp.mean(xc * xc, axis=-1, keepdims=True) + EPS)
            nrm = xc * rs
            rg = rg_ref[rows, :].astype(F32)
            sg = _sigmoid(rg)
            dg = dgro_ref[rows, :].astype(F32)
            drg_ref[rows, :] = (dg * nrm * sg * (1.0 + rg * (1.0 - sg))).astype(BF16)
            dn = dg * rg * sg
            do = rs * (dn - jnp.mean(dn, axis=-1, keepdims=True)
                       - nrm * jnp.mean(dn * nrm, axis=-1, keepdims=True))
            do_ref[rows, :] = do.astype(BF16)

        dstate = jnp.zeros((RET_KEY_DIM, RET_VAL_DIM), F32)
        for i in reversed(range(nt)):
            rows = slice(i * RET_TILE, (i + 1) * RET_TILE)
            qi, ki, vi, doi = qr_ref[rows, :], kr_ref[rows, :], v_ref[rows, :], do_ref[rows, :]
            p = (_dot_nt(qi, ki) * inside).astype(BF16)
            dp = (_dot_nt(doi, vi) * inside).astype(BF16)
            dq = _dot(dp, ki)
            dk = _dot_tn(dp, qi)
            dv = _dot_tn(p, doi)
            if i > 0:
                dq = dq + _dot_nt(doi, st_ref[i]) * q_dec
            if i < nt - 1:
                dsb = dstate.astype(BF16)
                dk = dk + _dot_nt(vi, dsb) * k_dec
                dv = dv + _dot(_scaled(ki, k_dec), dsb)
            if i > 0:
                dstate = dstate * tile_dec + _dot_tn(_scaled(qi, q_dec), doi)
            dq_ref[rows, :] = (dq * cs_ref[rows, :] - pltpu.roll(dq, 64, 1) * sn_ref[rows, :]).astype(BF16)
            dk = (dk * cs_ref[rows, :] - pltpu.roll(dk, 64, 1) * sn_ref[rows, :]) * (RET_KEY_DIM ** -0.5)
            dk_ref[rows, :] = dk.astype(BF16)
            dv_ref[rows, :] = dv.astype(BF16)

    key, val, tab, lgs = _ret_specs(seq)
    return _call(
        _per_head(one_head, "vvvkkvsshkkvvhh"), name="ret_bwd", grid=(batch, RET_HEADS // RET_GROUP),
        in_specs=[val(0), val(C_RG), val(0), key(0), key(0), val(C_RV), tab, tab, lgs],
        out_specs=[key(0), key(0), val(0), val(0)],
        out_shape=[jax.ShapeDtypeStruct((t, RET_HEADS * RET_KEY_DIM), BF16)] * 2
                  + [jax.ShapeDtypeStruct((t, RET_HEADS * RET_VAL_DIM), BF16)] * 2,
        scratch=[pltpu.VMEM((RET_GROUP, seq, RET_VAL_DIM), BF16),
                 pltpu.VMEM((RET_GROUP, nt, RET_KEY_DIM, RET_VAL_DIM), BF16)],
        semantics=("parallel", "parallel"), args=(dgro, proj, o_ret, qr, kr, proj, cs, sn, lg_arr),
        exchange=exchange)


def _att_bwd(proj, bias, dao, batch, seq, exchange):
    ni, q_spec, k_spec, v_spec, w_spec, b_spec, pad = _att_specs(batch, seq)
    t = batch * seq

    def body(q_ref, k_ref, v_ref, bias_ref, do_ref, dq_ref, dk_ref, dv_ref, dw_ref,
             dbias_ref, dk_acc, dv_acc, kp_ref, vp_ref, s_ref, dp_ref, e_ref, ds_ref):
        b, i = pl.program_id(1), pl.program_id(2)

        @pl.when((b == 0) & (i == 0))
        def _():
            dbias_ref[...] = jnp.zeros_like(dbias_ref)

        @pl.when(i == 0)
        def _():
            _att_pad(k_ref, kp_ref)
            _att_pad(v_ref, vp_ref)
            dk_acc[...] = jnp.zeros_like(dk_acc)
            dv_acc[...] = jnp.zeros_like(dv_acc)

        win = pl.ds(pl.multiple_of(i * ATT_Q, ATT_Q), ATT_WIN)
        lo = lax.broadcasted_iota(jnp.int32, (1, LANES), 1) < 64
        start = jnp.minimum(i, ATT_STARTS)
        for pair in range(ATT_PAIRS):
            cols = slice(pair * LANES, (pair + 1) * LANES)
            k2, v2, q2, do2 = kp_ref[win, cols], vp_ref[win, cols], q_ref[:, cols], do_ref[:, cols]
            dq = jnp.zeros((ATT_Q, LANES), F32)
            dk = jnp.zeros((LANES, ATT_WIN), F32)
            dv = jnp.zeros((LANES, ATT_WIN), F32)
            for e in range(2):
                h = 2 * pair + e
                sel = lo if e == 0 else jnp.logical_not(lo)
                qm = _att_head(q2, sel)
                dom = jnp.where(sel, do2, jnp.zeros_like(do2))
                s_ref[h] = _dot_nt(qm, k2)
                dp_ref[h] = _dot_nt(dom, v2)
                rsum = []
                for c in range(ATT_Q // ATT_ROWS):
                    rows = slice(c * ATT_ROWS, (c + 1) * ATT_ROWS)
                    ex, r = _att_softmax_rows(s_ref.at[h], bias_ref.at[pair, start, e], rows)
                    dp = dp_ref[h, rows, :]
                    mean = jnp.sum(dp * ex, axis=-1, keepdims=True) * r
                    ds = ex * ((dp - mean) * r)
                    dbias_ref[h, rows, :] += ds
                    ds_ref[h, rows, :] = ds.astype(BF16)
                    e_ref[h, rows, :] = ex.astype(BF16)
                    rsum.append(r)
                dq = dq + _dot(ds_ref[h], jnp.where(sel, k2, jnp.zeros_like(k2)))
                dk = dk + _dot_tn(qm, ds_ref[h])
                dv = dv + _dot_tn((dom.astype(F32) * jnp.concatenate(rsum, axis=0)).astype(BF16), e_ref[h])
            dq_ref[:, cols] = (dq * 0.125).astype(BF16)
            dk_acc[cols, win] += dk
            dv_acc[cols, win] += dv

        @pl.when(i == ni - 1)
        def _():
            dk_ref[...] = dk_acc[:, ATT_PAD:].T.astype(BF16)
            dv_ref[...] = dv_acc[:, ATT_PAD:].T.astype(BF16)

        @pl.when((b == batch - 1) & (i == ni - 1))
        def _():
            n_i = lax.broadcasted_iota(jnp.int32, (ATT_Q, BIAS_LEN), 0)
            for h in range(heads):
                xw = jnp.concatenate([jnp.zeros((ATT_Q, BIAS_LEN - ATT_WIN), F32), dbias_ref[h]], axis=1)
                for bit in range(8):
                    xw = jnp.where(((n_i >> bit) & 1) == 1, pltpu.roll(xw, BIAS_LEN - (1 << bit), 1), xw)
                dw_ref[h // 2, h % 2:h % 2 + 1, :] = jnp.sum(xw, axis=0, keepdims=True)

    heads = 2 * ATT_PAIRS
    seq_blk = pl.BlockSpec((seq, ATT_COLS), lambda g, b, i: (b, g))
    q_out = pl.BlockSpec((ATT_Q, ATT_COLS), lambda g, b, i: (b * ni + i, g))
    tile_f32, tile_bf16 = pltpu.VMEM((heads, ATT_Q, ATT_WIN), F32), pltpu.VMEM((heads, ATT_Q, ATT_WIN), BF16)
    acc = pltpu.VMEM((ATT_COLS, seq + ATT_PAD), F32)
    return _call(
        body, name="att_bwd", grid=(ATT_HEADS // heads, batch, ni),
        in_specs=[q_spec, k_spec, v_spec, b_spec, q_out],
        out_specs=[q_out, seq_blk, seq_blk, w_spec],
        out_shape=[jax.ShapeDtypeStruct((t, 512), BF16)] * 3
                  + [jax.ShapeDtypeStruct((ATT_HEADS // 2, 2, BIAS_LEN), F32)],
        scratch=[tile_f32, acc, acc, pad, pad, tile_f32, tile_f32, tile_bf16, tile_bf16],
        semantics=("arbitrary", "arbitrary", "arbitrary"), args=(proj, proj, proj, bias, dao), exchange=exchange,
        vmem=VMEM_LIMIT_ATT_BWD)


def _rms_in_bwd(x2, dxn, dh1, g1):
    t = x2.shape[0]
    tm = min(512, t)

    def body(x_ref, d_ref, h_ref, g_ref, dx_ref, dg_ref):
        @pl.when(pl.program_id(0) == 0)
        def _():
            dg_ref[...] = jnp.zeros_like(dg_ref)

        dx, dg_rows = _rms_bwd(x_ref[...], g_ref[...], d_ref[...])
        dx_ref[...] = h_ref[...] + dx
        dg_ref[...] += jnp.sum(dg_rows, axis=0, keepdims=True)

    row = pl.BlockSpec((tm, D_MODEL), lambda i: (i, 0))
    vec = pl.BlockSpec((1, D_MODEL), lambda i: (0, 0))
    return pl.pallas_call(
        body, name="rms_in_bwd", grid=(t // tm,),
        in_specs=[row, row, row, vec], out_specs=[row, vec],
        out_shape=[jax.ShapeDtypeStruct((t, D_MODEL), F32), jax.ShapeDtypeStruct((1, D_MODEL), F32)],
        compiler_params=_params("arbitrary"),
    )(x2, dxn, dh1, g1)


def _pack_small(dg1, dbr, dba, dg2, dg3, dw, loss):
    def body(a_ref, b_ref, c_ref, d_ref, e_ref, w_ref, l_ref, o_ref):
        o_ref[...] = jnp.zeros_like(o_ref)
        for r, ref in enumerate((a_ref, b_ref, c_ref, d_ref, e_ref)):
            o_ref[r:r + 1, :] = ref[...]
        o_ref[5:6, 0:LANES] = l_ref[...]
        for hp in range(ATT_HEADS // 2):
            o_ref[8 + 2 * hp:10 + 2 * hp, :] = w_ref[hp]

    return pl.pallas_call(body, name="pack_small",
                          out_shape=jax.ShapeDtypeStruct((16, D_MODEL), F32))(dg1, dbr, dba, dg2, dg3, dw, loss)


def _rotary_tables(seq):
    freqs = ROPE_BASE ** (-jnp.arange(0, RET_KEY_DIM, 2, dtype=F32) / RET_KEY_DIM)
    ang = jnp.arange(seq, dtype=F32)[:, None] * freqs[None, :]
    cos, sin = jnp.cos(ang), jnp.sin(ang)
    return jnp.concatenate([cos, cos], axis=1), jnp.concatenate([-sin, sin], axis=1)


def _bias_rows(rel_bias):
    n_far = BIAS_LEN - ATT_Q - MAX_REL + 1
    n_near = BIAS_LEN - n_far - (N_REL - 2)
    w = jnp.concatenate([jnp.broadcast_to(rel_bias[:, N_REL - 1:], (ATT_HEADS, n_far)),
                         rel_bias[:, 1:N_REL - 1][:, ::-1],
                         jnp.broadcast_to(rel_bias[:, :1], (ATT_HEADS, n_near))], axis=1)
    return w.reshape(ATT_HEADS // 2, 2, BIAS_LEN)


def _bias_rows_bwd(dw):
    n_far = BIAS_LEN - ATT_Q - MAX_REL + 1
    mid = dw[:, n_far:n_far + N_REL - 2][:, ::-1]
    return jnp.concatenate([jnp.sum(dw[:, n_far + N_REL - 2:], axis=1, keepdims=True), mid,
                            jnp.sum(dw[:, :n_far], axis=1, keepdims=True)], axis=1)


def _step(x, tgt, norm_mix, b_gate, norm_ffn, norm_final, rel_bias_shard, shard):
    batch, seq, _ = x.shape
    t = batch * seq
    n_rb = rel_bias_shard.shape[-1]
    x2, tgt2 = x.reshape(t, D_MODEL), tgt.reshape(t, D_MODEL)
    g3 = norm_final.reshape(1, D_MODEL)
    cs, sn = _rotary_tables(seq)
    lg = np.log(1.0 - 2.0 ** (-5.0 - np.arange(RET_HEADS, dtype=np.float32))).astype(np.float32)
    lg_arr = jnp.asarray(np.broadcast_to(lg[:, None, None], (RET_HEADS, 1, LANES)))

    def gather(*names):
        return _ChipGather([shard[nm] for nm in names])

    def scatter(*grads):
        return _Exchange(grads, scatter=True)

    rb_pad = jnp.pad(rel_bias_shard, ((0, 0), (0, LANES - n_rb)))
    (xn,), (w_in_half, rb_full) = _rms_fwd(x2, norm_mix,
                                           _ChipGather([shard["w_in_t"], rb_pad], parts=[(0, 2), (0, 1)]))
    rb_full = rb_full.reshape(N_DEV, ATT_HEADS, LANES)[:, :, :n_rb]
    bias, (w_in_t,) = _att_bias_tiles(
        _bias_rows(jnp.transpose(rb_full, (1, 0, 2)).reshape(ATT_HEADS, N_DEV * n_rb)),
        _ChipGather([shard["w_in_t"]], parts=[(1, 2)], into=[w_in_half]))
    proj, (w_ret, w_att_t, w_out, w_gate_t) = _mm(
        xn, w_in_t, tb=True, out_dtype=BF16, tm=1024, tn=1664, tk=1024, name="proj",
        exchange=gather("w_ret", "w_att_t", "w_out", "w_gate_t"))
    (gro, o_ret, qr, kr), _ = _ret_fwd(proj, cs, sn, lg_arr, batch, seq, None)
    (ao,), (w_up_t, w_down) = _att_fwd(proj, bias, batch, seq, gather("w_up_t", "w_down"))
    z, y_ret, y_att, h1, hn = _mix_out_fwd(gro, ao, proj, b_gate, w_ret, w_att_t, x2, w_out, norm_ffn)
    g_act, u_act, a_act = _ffn_up(hn, w_gate_t, w_up_t)
    dh2, dh2b, loss, dg3 = _ffn_down_loss(a_act, h1, tgt2, w_down, g3)

    wg = dict(out_dtype=BF16, tn=1024, ta=True)
    slots = {}
    dw_down = _mm(a_act, dh2b, tm=1408, tk=2048, name="dw_down", **wg)
    (d_gact, d_uact), _ = _ffn_bwd_act(dh2b, w_down, g_act, u_act, None)
    dw_gate = _mm(d_gact, hn, tm=1408, tk=2048, name="dw_gate", **wg)
    dw_up = _mm(d_uact, hn, tm=1408, tk=2048, name="dw_up", **wg)
    (dh1, dh1b, dg2), (slots["w_down"],) = _ffn_bwd_in(d_gact, d_uact, w_gate_t, w_up_t, h1, dh2, norm_ffn,
                                                     scatter(dw_down))
    dw_out = _mm(z, dh1b, tm=1024, tk=2048, name="dw_out", **wg)
    dyr, dya, dglr, dgla, dgro, dao, db = _mix_bwd(dh1b, w_out, proj, b_gate, y_ret, y_att, w_ret, w_att_t)
    dw_ret = _mm(gro, dyr, tm=1024, tk=2048, name="dw_ret", **wg)
    dw_att = _mm(dya, ao, tm=1024, tk=2048, name="dw_att", **wg)
    (drq, drk, drv, drg), _ = _ret_bwd(dgro, proj, o_ret, qr, kr, cs, sn, lg_arr, batch, seq, None)
    (daq, dak, dav, dw), (slots["w_gate_t"], slots["w_out"], slots["w_ret"], slots["w_att_t"]) = _att_bwd(
        proj, bias, dao, batch, seq, scatter(dw_gate, dw_out, dw_ret, dw_att))
    dproj = [drq, drk, drv, drg, daq, dak, dav, dglr, dgla]
    dw_in, (slots["w_up_t"],) = _mm_pieces(dproj, xn, ta=True, out_dtype=BF16, tm=512, tn=1024, tk=1024,
                                           name="dw_in", exchange=scatter(dw_up))
    (dw_in_sibling,) = _alone(_PairSwap([dw_in]), "swap_w_in")
    dw_in_pairs = _pair_add(dw_in, dw_in_sibling, "pair_w_in")
    dxn, (slots["w_in_t"],) = _mm_pieces(dproj, w_in_t, ta=False, out_dtype=F32, tm=1024, tn=1024, tk=512, name="dxn",
                                  exchange=_ChipScatter([dw_in_pairs]))
    dx, dg1 = _rms_in_bwd(x2, dxn, dh1, norm_mix)
    small = _pack_small(dg1, db[:, :D_MODEL], db[:, D_MODEL:], dg2, dg3, dw, loss)
    (small_slots,) = _alone(_ChipGather([small]), "gather_small")
    return dx.reshape(batch, seq, D_MODEL), slots, small_slots.reshape(N_DEV, 16, D_MODEL)


def _row_tile(r, c):
    return max(d for d in range(16, r + 1, 16) if r % d == 0 and (d * c <= 256 * 1024 or d == 16))


def _pair_add(grad, got, name):
    _, r, c = got.shape
    tr = r
    core = lax.axis_index("c").astype(jnp.int32).reshape(1)

    def body(core_ref, g_ref, a_ref, o_ref):
        o_ref[...] = (g_ref[...].astype(F32) + a_ref[...].astype(F32)).astype(o_ref.dtype)

    blk = pl.BlockSpec((None, tr, c), lambda q, i, core_ref: (q, i, 0))
    return pl.pallas_call(
        body, name=name,
        grid_spec=pltpu.PrefetchScalarGridSpec(
            num_scalar_prefetch=1, grid=(4, r // tr),
            in_specs=[pl.BlockSpec((None, None, tr, c), lambda q, i, core_ref: (q, core_ref[0], i, 0)), blk],
            out_specs=blk),
        out_shape=jax.ShapeDtypeStruct(got.shape, got.dtype),
        compiler_params=_params("parallel", "parallel"),
    )(core, grad.reshape(4, 2, r, c), got)


def _sum_slots(slots, name):
    n, r, c = slots.shape
    tr = _row_tile(r, c)

    def body(s_ref, o_ref):
        acc = s_ref[0].astype(F32)
        for s in range(1, n):
            acc = acc + s_ref[s].astype(F32)
        o_ref[...] = acc

    return pl.pallas_call(
        body, name=name, grid=(r // tr,),
        in_specs=[pl.BlockSpec((n, tr, c), lambda i: (0, i, 0))],
        out_specs=pl.BlockSpec((tr, c), lambda i: (i, 0)),
        out_shape=jax.ShapeDtypeStruct((r, c), F32),
        compiler_params=_params("parallel"),
    )(slots)


def _adamw_math(w, g, m, v):
    m = ADAM_B1 * m + (1.0 - ADAM_B1) * g
    v = ADAM_B2 * v + (1.0 - ADAM_B2) * (g * g)
    m_hat = m / (1.0 - ADAM_B1 ** ADAM_STEP)
    v_hat = v / (1.0 - ADAM_B2 ** ADAM_STEP)
    return -ADAM_LR * (m_hat / (jnp.sqrt(v_hat) + ADAM_EPS) + ADAM_WD * w), m, v


def _adamw(w, slots, m, v, name):
    n, r, c = slots.shape
    tr = _row_tile(r, c)

    def body(w_ref, s_ref, m_ref, v_ref, g_ref, d_ref, nm_ref, nv_ref):
        g = s_ref[0].astype(F32)
        for s in range(1, n):
            g = g + s_ref[s].astype(F32)
        g_ref[...] = g
        d_ref[...], nm_ref[...], nv_ref[...] = _adamw_math(w_ref[...], g, m_ref[...], v_ref[...])

    blk = pl.BlockSpec((tr, c), lambda i: (i, 0))
    return pl.pallas_call(
        body, name=name, grid=(r // tr,),
        in_specs=[blk, pl.BlockSpec((n, tr, c), lambda i: (0, i, 0)), blk, blk], out_specs=[blk] * 4,
        out_shape=[jax.ShapeDtypeStruct((r, c), F32)] * 4,
        compiler_params=_params("parallel"),
    )(w, slots, m, v)


def _adamw_small(ws, gs, ms, vs):
    n = len(ws)

    def body(*refs):
        for i in range(n):
            w_ref, g_ref, m_ref, v_ref = (refs[j * n + i] for j in range(4))
            d_ref, nm_ref, nv_ref = (refs[(4 + j) * n + i] for j in range(3))
            d_ref[...], nm_ref[...], nv_ref[...] = _adamw_math(w_ref[...], g_ref[...], m_ref[...], v_ref[...])

    shapes = [jax.ShapeDtypeStruct(w.shape, F32) for w in ws]
    outs = pl.pallas_call(body, name="adamw_small", out_shape=shapes * 3)(*ws, *gs, *ms, *vs)
    return outs[:n], outs[n:2 * n], outs[2 * n:]


def kernel(x, norm_mix, w_in, b_gate, rel_bias, w_ret_out, w_att_out, w_out, norm_ffn, w_ffn_gate, w_ffn_up, w_ffn_down, norm_final, loss_target, m_norm_mix, m_w_in, m_b_gate, m_rel_bias, m_w_ret_out, m_w_att_out, m_w_out, m_norm_ffn, m_w_ffn_gate, m_w_ffn_up, m_w_ffn_down, m_norm_final, v_norm_mix, v_w_in, v_b_gate, v_rel_bias, v_w_ret_out, v_w_att_out, v_w_out, v_norm_ffn, v_w_ffn_gate, v_w_ffn_up, v_w_ffn_down, v_norm_final):
    me = _index(_place())
    n_rb = rel_bias.shape[-1]

    shard = dict(w_in_t=w_in[0].T, w_gate_t=w_ffn_gate[0].T, w_up_t=w_ffn_up[0].T, w_down=w_ffn_down[0],
                 w_ret=w_ret_out[0], w_out=w_out[0], w_att_t=w_att_out[0].T)
    shard = {nm: s.astype(BF16) for nm, s in shard.items()}
    dx, slots, small_slots = _step(x, loss_target, norm_mix, b_gate, norm_ffn, norm_final, rel_bias[0], shard)
    small_sum = _sum_slots(small_slots, "sum_small")
    loss = small_sum[5, 0]

    transposed = dict(w_in="w_in_t", w_ffn_gate="w_gate_t", w_ffn_up="w_up_t", w_att_out="w_att_t")
    plain = dict(w_ffn_down="w_down", w_ret_out="w_ret", w_out="w_out")
    g = dict(
        norm_mix=small_sum[0:1], b_gate=jnp.concatenate([small_sum[1:2], small_sum[2:3]], axis=1),
        norm_ffn=small_sum[3:4], norm_final=small_sum[4:5],
        rel_bias=lax.dynamic_slice_in_dim(_bias_rows_bwd(small_sum[8:16]), me * n_rb, n_rb, axis=1),
    )
    w = dict(norm_mix=norm_mix, w_in=w_in, b_gate=b_gate, rel_bias=rel_bias, w_ret_out=w_ret_out, w_att_out=w_att_out,
             w_out=w_out, norm_ffn=norm_ffn, w_ffn_gate=w_ffn_gate, w_ffn_up=w_ffn_up, w_ffn_down=w_ffn_down,
             norm_final=norm_final)
    m = dict(norm_mix=m_norm_mix, w_in=m_w_in, b_gate=m_b_gate, rel_bias=m_rel_bias, w_ret_out=m_w_ret_out,
             w_att_out=m_w_att_out, w_out=m_w_out, norm_ffn=m_norm_ffn, w_ffn_gate=m_w_ffn_gate, w_ffn_up=m_w_ffn_up,
             w_ffn_down=m_w_ffn_down, norm_final=m_norm_final)
    v = dict(norm_mix=v_norm_mix, w_in=v_w_in, b_gate=v_b_gate, rel_bias=v_rel_bias, w_ret_out=v_w_ret_out,
             w_att_out=v_w_att_out, w_out=v_w_out, norm_ffn=v_norm_ffn, w_ffn_gate=v_w_ffn_gate, w_ffn_up=v_w_ffn_up,
             w_ffn_down=v_w_ffn_down, norm_final=v_norm_final)
    order = ("norm_mix", "w_in", "b_gate", "rel_bias", "w_ret_out", "w_att_out", "w_out", "norm_ffn",
             "w_ffn_gate", "w_ffn_up", "w_ffn_down", "norm_final")
    small_names = ("norm_mix", "b_gate", "rel_bias", "norm_ffn", "norm_final")

    def flat(a):
        return a[0] if a.ndim == 3 else a.reshape(-1, a.shape[-1])

    grad, delta, new_m, new_v = {}, {}, {}, {}
    for nm in order:
        if nm in transposed:
            res = _adamw(w[nm][0].T, slots[transposed[nm]], m[nm][0].T, v[nm][0].T, "adamw_" + nm)
            grad[nm], delta[nm], new_m[nm], new_v[nm] = (a.T[None] for a in res)
        elif nm in plain:
            res = _adamw(flat(w[nm]), slots[plain[nm]], flat(m[nm]), flat(v[nm]), "adamw_" + nm)
            grad[nm], delta[nm], new_m[nm], new_v[nm] = (a.reshape(w[nm].shape) for a in res)
    ds, nms, nvs = _adamw_small([flat(w[nm]) for nm in small_names], [g[nm] for nm in small_names],
                                [flat(m[nm]) for nm in small_names], [flat(v[nm]) for nm in small_names])
    for i, nm in enumerate(small_names):
        grad[nm], delta[nm], new_m[nm], new_v[nm] = (a.reshape(w[nm].shape) for a in (g[nm], ds[i], nms[i], nvs[i]))

    return (loss, dx, *[grad[nm] for nm in order], *[delta[nm] for nm in order],
            *[new_m[nm] for nm in order], *[new_v[nm] for nm in order])
```

```python
import numpy as np
import jax
import jax.numpy as jnp
from jax import lax
from jax.experimental import pallas as pl
from jax.experimental.pallas import tpu as pltpu

F32 = jnp.float32
BF16 = jnp.bfloat16
MESH = pl.DeviceIdType.MESH

D_MODEL = 1024
CHUNK = 64
RET_HEADS = 4
RET_KEY_DIM = 128
RET_VAL_DIM = 256
ATT_HEADS = 8
BAND_CHUNKS = 8
MAX_REL = 256
N_REL = CHUNK + MAX_REL
D_FF = 2816
N_IN = 6656
ROPE_BASE = 10000.0
EPS = 1e-6
NEG_INF = -1e30
C_RQ, C_RK, C_RV, C_RG, C_AQ, C_AK, C_AV, C_GL = 0, 512, 1024, 2048, 3072, 3584, 4096, 4608

ADAM_LR = 0.001
ADAM_B1 = 0.9
ADAM_B2 = 0.999
ADAM_EPS = 1e-08
ADAM_WD = 0.01
ADAM_STEP = 10

N_DEV = 8
LANES = 128
RET_TILE = 256
ATT_Q = 256
ATT_PAD = BAND_CHUNKS * CHUNK
ATT_WIN = ATT_PAD + ATT_Q
ATT_STARTS = ATT_PAD // ATT_Q
ATT_ROWS = 32
BIAS_LEN = 1024
VMEM_LIMIT = 48 * 1024 * 1024
VMEM_LIMIT_ATT_BWD = 56 * 1024 * 1024


def _params(*sem, vmem=VMEM_LIMIT):
    return pltpu.CompilerParams(dimension_semantics=sem, vmem_limit_bytes=vmem)


def _dot(a, b):
    return lax.dot_general(a, b, (((1,), (0,)), ((), ())), preferred_element_type=F32)


def _dot_nt(a, b):
    return lax.dot_general(a, b, (((1,), (1,)), ((), ())), preferred_element_type=F32)


def _dot_tn(a, b):
    return lax.dot_general(a, b, (((0,), (0,)), ((), ())), preferred_element_type=F32)


def _sigmoid(x):
    return 1.0 / (1.0 + jnp.exp(-x))


def _rms_bwd(x, g, dy):
    r = lax.rsqrt(jnp.mean(x * x, axis=-1, keepdims=True) + EPS)
    u = dy * g
    dx = r * u - x * (r * r * r) * jnp.mean(u * x, axis=-1, keepdims=True)
    return dx, dy * x * r


def _place():
    return lax.axis_index("x"), lax.axis_index("y"), lax.axis_index("c")


def _peer(k):
    x, y, c = _place()
    return ((1 - x) if k & 4 else x, (1 - y) if k & 2 else y, (1 - c) if k & 1 else c)


def _index(place):
    return 4 * place[0] + 2 * place[1] + place[2]


def _rows(ref, block, nrows):
    align = 16 if ref.dtype == BF16 else 8
    return ref.at[pl.ds(pl.multiple_of(block * nrows, align), nrows)]


class _Scatter:
    def __init__(self, arrays):
        self.arrays, self.n = list(arrays), len(arrays)

    def out_shape(self):
        return [jax.ShapeDtypeStruct((N_DEV, a.shape[0] // N_DEV) + a.shape[1:], a.dtype) for a in self.arrays]

    def scratch(self):
        return [pltpu.SemaphoreType.DMA((self.n, N_DEV - 1)), pltpu.SemaphoreType.DMA((self.n, N_DEV - 1)),
                pltpu.SemaphoreType.DMA((self.n,))]

    def _copies(self, ins, outs, sems):
        send_sems, recv_sems, local_sems = sems
        me = _index(_place())

        def src(w, to):
            return _rows(ins[w], to, ins[w].shape[0] // N_DEV)

        def dst(w, origin):
            return outs[w].at[origin]

        def remote(w, k, to, origin):
            return pltpu.make_async_remote_copy(src_ref=src(w, to), dst_ref=dst(w, origin),
                                                send_sem=send_sems.at[w, k - 1], recv_sem=recv_sems.at[w, k - 1],
                                                device_id=_peer(k), device_id_type=MESH)

        pairs = [(w, k) for w in range(self.n) for k in range(1, N_DEV)]
        own = lambda: [pltpu.make_async_copy(src(w, me), dst(w, me), local_sems.at[w]) for w in range(self.n)]
        sent = lambda: [remote(w, k, _index(_peer(k)), me) for w, k in pairs]
        arriving = lambda: [remote(w, k, me, _index(_peer(k))) for w, k in pairs]
        return own, sent, arriving

    def start(self, ins, outs, sems):
        own, sent, _ = self._copies(ins, outs, sems)
        for cp in own() + sent():
            cp.start()

    def wait(self, ins, outs, sems):
        own, sent, arriving = self._copies(ins, outs, sems)
        for cp in arriving():
            cp.wait_recv()
        for cp in sent():
            cp.wait_send()
        for cp in own():
            cp.wait()


class _PairSwap:
    def __init__(self, arrays):
        self.arrays, self.n = list(arrays), len(arrays)

    def out_shape(self):
        return [jax.ShapeDtypeStruct((4, a.shape[0] // N_DEV) + a.shape[1:], a.dtype) for a in self.arrays]

    def scratch(self):
        return [pltpu.SemaphoreType.DMA((self.n, 4)), pltpu.SemaphoreType.DMA((self.n, 4))]

    def _copies(self, ins, outs, sems):
        send_sems, recv_sems = sems
        x, y, c = _place()
        return [pltpu.make_async_remote_copy(
            src_ref=_rows(ins[w], 2 * q + 1 - c, ins[w].shape[0] // N_DEV), dst_ref=outs[w].at[q],
            send_sem=send_sems.at[w, q], recv_sem=recv_sems.at[w, q],
            device_id=(x, y, 1 - c), device_id_type=MESH) for w in range(self.n) for q in range(4)]

    def start(self, ins, outs, sems):
        for cp in self._copies(ins, outs, sems):
            cp.start()

    def wait(self, ins, outs, sems):
        for cp in self._copies(ins, outs, sems):
            cp.wait()


class _ChipScatter:
    def __init__(self, arrays):
        self.arrays, self.n = list(arrays), len(arrays)

    def out_shape(self):
        return [jax.ShapeDtypeStruct(a.shape, a.dtype) for a in self.arrays]

    def scratch(self):
        return [pltpu.SemaphoreType.DMA((self.n, 3)), pltpu.SemaphoreType.DMA((self.n, 3)),
                pltpu.SemaphoreType.DMA((self.n,))]

    def _copies(self, ins, outs, sems):
        send_sems, recv_sems, local_sems = sems
        x, y, c = _place()
        mine = 2 * x + y
        sent, arriving = [], []
        for w in range(self.n):
            for k in range(1, 4):
                tx, ty = (1 - x) if k & 2 else x, (1 - y) if k & 1 else y
                other = 2 * tx + ty
                sent.append(lambda w=w, k=k, tx=tx, ty=ty, other=other: pltpu.make_async_remote_copy(
                    src_ref=ins[w].at[other], dst_ref=outs[w].at[mine], send_sem=send_sems.at[w, k - 1],
                    recv_sem=recv_sems.at[w, k - 1], device_id=(tx, ty, c), device_id_type=MESH))
                arriving.append(lambda w=w, k=k, tx=tx, ty=ty, other=other: pltpu.make_async_remote_copy(
                    src_ref=ins[w].at[mine], dst_ref=outs[w].at[other], send_sem=send_sems.at[w, k - 1],
                    recv_sem=recv_sems.at[w, k - 1], device_id=(tx, ty, c), device_id_type=MESH))
        own = [lambda w=w: pltpu.make_async_copy(ins[w].at[mine], outs[w].at[mine], local_sems.at[w])
               for w in range(self.n)]
        return own, sent, arriving

    def start(self, ins, outs, sems):
        own, sent, _ = self._copies(ins, outs, sems)
        for cp in own + sent:
            cp().start()

    def wait(self, ins, outs, sems):
        own, sent, arriving = self._copies(ins, outs, sems)
        for cp in arriving:
            cp().wait_recv()
        for cp in sent:
            cp().wait_send()
        for cp in own:
            cp().wait()


class _ChipGather:
    def __init__(self, arrays, parts=None, into=None):
        self.arrays, self.n, self.into = list(arrays), len(arrays), into
        self.parts = parts or [(0, 1)] * self.n

    def out_shape(self):
        return [jax.ShapeDtypeStruct((N_DEV * a.shape[0],) + a.shape[1:], a.dtype) for a in self.arrays]

    def scratch(self):
        return [pltpu.SemaphoreType.DMA((self.n, N_DEV - 1)), pltpu.SemaphoreType.DMA((self.n, N_DEV - 1)),
                pltpu.SemaphoreType.DMA((self.n,))]

    def _parts(self, ins, outs, sems):
        send_sems, recv_sems, local_sems = sems
        x, y, c = _place()
        me, sibling = (x, y, c), (x, y, 1 - c)
        chips = [(1 - x, y), (x, 1 - y), (1 - x, 1 - y)]

        def rows(w, place, whole):
            (index, count), r = self.parts[w], ins[w].shape[0]
            lo, size = (0, r) if whole else (index * (r // count), r // count)
            align = 16 if ins[w].dtype == BF16 else 8
            return outs[w].at[pl.ds(pl.multiple_of(_index(place) * r + lo, align), size)]

        def mine(w, whole):
            (index, count), r = self.parts[w], ins[w].shape[0]
            return ins[w] if whole or count == 1 else ins[w].at[pl.ds(index * (r // count), r // count)]

        def copy(w, k, block, to, own=False):
            whole = k == 0
            return pltpu.make_async_remote_copy(src_ref=mine(w, whole) if own else rows(w, block, whole),
                                                dst_ref=rows(w, block, whole),
                                                send_sem=send_sems.at[w, k], recv_sem=recv_sems.at[w, k],
                                                device_id=to, device_id_type=MESH)

        def local(w):
            return pltpu.make_async_copy(ins[w], rows(w, me, True), local_sems.at[w])

        return me, sibling, chips, c, copy, local, [index == 0 for index, _ in self.parts]

    def start(self, ins, outs, sems):
        me, sibling, chips, c, copy, local, places_own = self._parts(ins, outs, sems)
        for w in range(self.n):
            if places_own[w]:
                local(w).start()
                copy(w, 0, me, sibling, own=True).start()
            for j, chip in enumerate(chips):
                copy(w, 1 + j, me, (*chip, c), own=True).start()

    def wait(self, ins, outs, sems):
        me, sibling, chips, c, copy, local, places_own = self._parts(ins, outs, sems)
        for w in range(self.n):
            for j, chip in enumerate(chips):
                copy(w, 1 + j, (*chip, c), me).wait_recv()
                copy(w, 4 + j, (*chip, c), sibling).start()
        for w in range(self.n):
            if places_own[w]:
                copy(w, 0, sibling, me).wait_recv()
                copy(w, 0, me, sibling, own=True).wait_send()
                local(w).wait()
            for j, chip in enumerate(chips):
                copy(w, 4 + j, (*chip, 1 - c), me).wait_recv()
                copy(w, 1 + j, me, (*chip, c), own=True).wait_send()
                copy(w, 4 + j, (*chip, c), sibling).wait_send()


def _call(body, *, name, grid, in_specs, out_specs, out_shape, scratch=(), semantics, args, exchange=None,
          vmem=VMEM_LIMIT):
    if exchange is None:
        return pl.pallas_call(body, name=name, grid=grid, in_specs=in_specs, out_specs=out_specs, out_shape=out_shape,
                              scratch_shapes=list(scratch),
                              compiler_params=_params(*semantics, vmem=vmem))(*args), None
    n_in, n_out, n_scr, nx = len(in_specs), len(out_specs), len(scratch), exchange.n
    into = list(getattr(exchange, "into", None) or [])

    def full_body(*refs):
        ins, refs = refs[:n_in], refs[n_in:]
        x_in, refs = refs[:nx], refs[nx + len(into):]
        outs, refs = refs[:n_out], refs[n_out:]
        x_out, refs = refs[:nx], refs[nx:]
        scr, sems = refs[:n_scr], refs[n_scr:]
        first, last = True, True
        for axis, size in enumerate(grid):
            first = jnp.logical_and(first, pl.program_id(axis) == 0)
            last = jnp.logical_and(last, pl.program_id(axis) == size - 1)
        if grid:
            pl.when(first)(lambda: exchange.start(x_in, x_out, sems))
        else:
            exchange.start(x_in, x_out, sems)
        body(*ins, *outs, *scr)
        if grid:
            pl.when(last)(lambda: exchange.wait(x_in, x_out, sems))
        else:
            exchange.wait(x_in, x_out, sems)

    hbm = pl.BlockSpec(memory_space=pltpu.HBM)
    res = pl.pallas_call(
        full_body, name=name, grid=grid,
        in_specs=list(in_specs) + [hbm] * (nx + len(into)), out_specs=list(out_specs) + [hbm] * nx,
        out_shape=list(out_shape) + exchange.out_shape(),
        scratch_shapes=list(scratch) + exchange.scratch(),
        input_output_aliases={n_in + nx + w: n_out + w for w in range(len(into))},
        compiler_params=_params(*(["arbitrary"] * len(grid)), vmem=vmem),
    )(*args, *exchange.arrays, *into)
    return res[:n_out], res[n_out:]


def _alone(exchange, name):
    return _call(lambda: None, name=name, grid=(), in_specs=[], out_specs=[], out_shape=[], semantics=(),
                 args=(), exchange=exchange)[1]


def _mm(a, b, *, ta=False, tb=False, out_dtype, tm, tn, tk, name, exchange=None):
    m, k = (a.shape[1], a.shape[0]) if ta else a.shape
    n = b.shape[0] if tb else b.shape[1]
    assert k == (b.shape[1] if tb else b.shape[0])
    tm, tn, tk = min(tm, m), min(tn, n), min(tk, k)
    assert m % tm == 0 and n % tn == 0 and k % tk == 0, (name, m, n, k)
    nk = k // tk
    dims = (((0 if ta else 1,), (1 if tb else 0,)), ((), ()))

    def body(a_ref, b_ref, o_ref, *acc):
        prod = lax.dot_general(a_ref[...].astype(BF16), b_ref[...].astype(BF16), dims, preferred_element_type=F32)
        if nk == 1:
            o_ref[...] = prod.astype(o_ref.dtype)
            return
        acc_ref, kk = acc[0], pl.program_id(2)

        @pl.when(kk == 0)
        def _():
            acc_ref[...] = prod

        @pl.when((kk > 0) & (kk < nk - 1))
        def _():
            acc_ref[...] += prod

        @pl.when(kk == nk - 1)
        def _():
            o_ref[...] = (acc_ref[...] + prod).astype(o_ref.dtype)

    a_spec = (pl.BlockSpec((tk, tm), lambda i, j, kk: (kk, i)) if ta
              else pl.BlockSpec((tm, tk), lambda i, j, kk: (i, kk)))
    b_spec = (pl.BlockSpec((tn, tk), lambda i, j, kk: (j, kk)) if tb
              else pl.BlockSpec((tk, tn), lambda i, j, kk: (kk, j)))
    (out,), moved = _call(
        body, name=name, grid=(m // tm, n // tn, nk),
        in_specs=[a_spec, b_spec],
        out_specs=[pl.BlockSpec((tm, tn), lambda i, j, kk: (i, j))],
        out_shape=[pltpu.HBM((m, n), out_dtype)],
        scratch=[pltpu.VMEM((tm, tn), F32)] if nk > 1 else [],
        semantics=("parallel", "parallel", "arbitrary"), args=(a, b), exchange=exchange)
    return out if exchange is None else (out, moved)


def _mm_pieces(pieces, b, *, ta, out_dtype, tm, tn, tk, name, exchange=None):
    rows, n = pieces[0].shape[0], b.shape[1]
    step = tm if ta else tk
    assert all(p.shape[0] == rows and p.shape[1] % step == 0 for p in pieces), name
    edges = [int(e) for e in np.cumsum([0] + [p.shape[1] // step for p in pieces])]
    total = edges[-1] * step
    m, k = (total, rows) if ta else (rows, total)
    assert b.shape[0] == k and m % tm == 0 and n % tn == 0 and k % tk == 0, name
    nk, npieces = k // tk, len(pieces)
    dims = (((0 if ta else 1,), (0,)), ((), ()))
    b_resident = ta and n == tn

    def body(*refs):
        a_refs, (b_ref, o_ref, acc_ref) = refs[:npieces], refs[npieces:]
        kk = pl.program_id(2)
        pos = pl.program_id(0) if ta else kk

        @pl.when(kk == 0)
        def _():
            acc_ref[...] = jnp.zeros_like(acc_ref)

        def b_tile():
            return b_ref[pl.ds(pl.multiple_of(kk * tk, tk), tk), :] if b_resident else b_ref[...]

        for p, a_ref in enumerate(a_refs):
            @pl.when((pos >= edges[p]) & (pos < edges[p + 1]))
            def _(a_ref=a_ref):
                acc_ref[...] += lax.dot_general(a_ref[...], b_tile(), dims, preferred_element_type=F32)

        @pl.when(kk == nk - 1)
        def _():
            o_ref[...] = acc_ref[...].astype(o_ref.dtype)

    def a_spec(p):
        lo, last = edges[p], edges[p + 1] - edges[p] - 1
        if ta:
            def index(i, j, kk):
                inside = (i >= lo) & (i <= lo + last)
                return jnp.where(inside, kk, 0), jnp.clip(i - lo, 0, last)
            return pl.BlockSpec((tk, tm), index)
        return pl.BlockSpec((tm, tk), lambda i, j, kk: (i, jnp.clip(kk - lo, 0, last)))

    (out,), moved = _call(
        body, name=name, grid=(m // tm, n // tn, nk),
        in_specs=[a_spec(p) for p in range(npieces)]
                 + [pl.BlockSpec(b.shape, lambda i, j, kk: (0, 0)) if b_resident
                    else pl.BlockSpec((tk, tn), lambda i, j, kk: (kk, j))],
        out_specs=[pl.BlockSpec((tm, tn), lambda i, j, kk: (i, j))],
        out_shape=[pltpu.HBM((m, n), out_dtype)],
        scratch=[pltpu.VMEM((tm, tn), F32)],
        semantics=("parallel", "parallel", "arbitrary"), args=(*pieces, b), exchange=exchange)
    return out if exchange is None else (out, moved)


def _rms_fwd(x2, g, exchange):
    t = x2.shape[0]
    tm = min(512, t)

    def body(x_ref, g_ref, o_ref):
        x = x_ref[...]
        r = lax.rsqrt(jnp.mean(x * x, axis=-1, keepdims=True) + EPS)
        o_ref[...] = (x * r * g_ref[...]).astype(o_ref.dtype)

    return _call(
        body, name="rms_in_fwd", grid=(t // tm,),
        in_specs=[pl.BlockSpec((tm, D_MODEL), lambda i: (i, 0)), pl.BlockSpec((1, D_MODEL), lambda i: (0, 0))],
        out_specs=[pl.BlockSpec((tm, D_MODEL), lambda i: (i, 0))],
        out_shape=[jax.ShapeDtypeStruct((t, D_MODEL), BF16)],
        semantics=("parallel",), args=(x2, g), exchange=exchange)


def _decay(lg):
    row = lax.broadcasted_iota(jnp.int32, (RET_TILE, RET_TILE), 0)
    col = lax.broadcasted_iota(jnp.int32, (RET_TILE, RET_TILE), 1)
    within = jnp.exp(lg * jnp.abs(row - col).astype(F32))
    inside = jnp.where((col >> 6) <= (row >> 6), within, 0.0)
    pos = lax.broadcasted_iota(jnp.int32, (RET_TILE, 1), 0).astype(F32)
    q_dec = jnp.exp(lg * (pos + 1.0))
    k_dec = jnp.exp(lg * (RET_TILE - 1.0 - pos))
    tile_dec = jnp.exp(lg * float(RET_TILE))
    return inside, q_dec, k_dec, tile_dec


def _scaled(a_bf16, dec):
    return (a_bf16.astype(F32) * dec).astype(BF16)


def _ret_specs(seq):
    key = lambda base: pl.BlockSpec((seq, RET_KEY_DIM), lambda b, h: (b, base // RET_KEY_DIM + h))
    val = lambda base: pl.BlockSpec((seq, RET_VAL_DIM), lambda b, h: (b, base // RET_VAL_DIM + h))
    tab = pl.BlockSpec((seq, RET_KEY_DIM), lambda b, h: (0, 0))
    lgs = pl.BlockSpec((None, 1, LANES), lambda b, h: (h, 0, 0))
    return key, val, tab, lgs


def _ret_fwd(proj, cs, sn, lg_arr, batch, seq, exchange):
    t = batch * seq
    nt = seq // RET_TILE

    def body(q_ref, k_ref, v_ref, rg_ref, cs_ref, sn_ref, lg_ref, gro_ref, o_ref, qr_ref, kr_ref):
        lg = lg_ref[:, 0:1]
        cs_t, sn_t = cs_ref[...], sn_ref[...]
        q = q_ref[...].astype(F32)
        k = k_ref[...].astype(F32)
        qr_ref[...] = (q * cs_t + pltpu.roll(q, 64, 1) * sn_t).astype(BF16)
        kr_ref[...] = ((k * cs_t + pltpu.roll(k, 64, 1) * sn_t) * (RET_KEY_DIM ** -0.5)).astype(BF16)
        inside, q_dec, k_dec, tile_dec = _decay(lg)
        state = jnp.zeros((RET_KEY_DIM, RET_VAL_DIM), F32)
        for i in range(nt):
            rows = slice(i * RET_TILE, (i + 1) * RET_TILE)
            qi, ki, vi = qr_ref[rows, :], kr_ref[rows, :], v_ref[rows, :]
            acc = _dot((_dot_nt(qi, ki) * inside).astype(BF16), vi)
            if i > 0:
                acc = acc + _dot(_scaled(qi, q_dec), state.astype(BF16))
            if i < nt - 1:
                state = state * tile_dec + _dot_tn(_scaled(ki, k_dec), vi)
            o_ref[rows, :] = acc
            xc = acc - jnp.mean(acc, axis=-1, keepdims=True)
            nrm = xc * lax.rsqrt(jnp.mean(xc * xc, axis=-1, keepdims=True) + EPS)
            rg = rg_ref[rows, :].astype(F32)
            gro_ref[rows, :] = (rg * _sigmoid(rg) * nrm).astype(BF16)

    key, val, tab, lgs = _ret_specs(seq)
    return _call(
        body, name="ret_fwd", grid=(batch, RET_HEADS),
        in_specs=[key(C_RQ), key(C_RK), val(C_RV), val(C_RG), tab, tab, lgs],
        out_specs=[val(0), val(0), key(0), key(0)],
        out_shape=[jax.ShapeDtypeStruct((t, RET_HEADS * RET_VAL_DIM), BF16),
                   jax.ShapeDtypeStruct((t, RET_HEADS * RET_VAL_DIM), F32),
                   jax.ShapeDtypeStruct((t, RET_HEADS * RET_KEY_DIM), BF16),
                   jax.ShapeDtypeStruct((t, RET_HEADS * RET_KEY_DIM), BF16)],
        semantics=("parallel", "parallel"), args=(proj, proj, proj, proj, cs, sn, lg_arr), exchange=exchange)


def _att_bias(w_ref, bias_ref):
    n_i = lax.broadcasted_iota(jnp.int32, (ATT_Q, BIAS_LEN), 0)
    qc = lax.broadcasted_iota(jnp.int32, (ATT_Q, ATT_WIN), 0) >> 6
    kc = lax.broadcasted_iota(jnp.int32, (ATT_Q, ATT_WIN), 1) >> 6
    dc = qc + BAND_CHUNKS - kc
    band = (dc >= 0) & (dc <= BAND_CHUNKS)
    key = lax.broadcasted_iota(jnp.int32, (ATT_Q, ATT_WIN), 1)
    for e in range(2):
        xw = jnp.broadcast_to(w_ref[e:e + 1, :], (ATT_Q, BIAS_LEN))
        for bit in range(8):
            xw = jnp.where(((n_i >> bit) & 1) == 1, pltpu.roll(xw, 1 << bit, 1), xw)
        bias = jnp.where(band, xw[:, BIAS_LEN - ATT_WIN:], NEG_INF)
        for first in range(ATT_STARTS):
            bias_ref[first, e] = jnp.where(key + (first * ATT_Q - ATT_PAD) >= 0, bias, NEG_INF)
        bias_ref[ATT_STARTS, e] = bias


ATT_PAIRS = 2
ATT_COLS = ATT_PAIRS * LANES


def _att_specs(batch, seq):
    ni = seq // ATT_Q
    q_spec = pl.BlockSpec((ATT_Q, ATT_COLS), lambda g, b, i: (b * ni + i, C_AQ // ATT_COLS + g))
    k_spec = pl.BlockSpec((seq, ATT_COLS), lambda g, b, i: (b, C_AK // ATT_COLS + g))
    v_spec = pl.BlockSpec((seq, ATT_COLS), lambda g, b, i: (b, C_AV // ATT_COLS + g))
    w_spec = pl.BlockSpec((ATT_PAIRS, 2, BIAS_LEN), lambda g, b, i: (g, 0, 0))
    b_spec = pl.BlockSpec((ATT_PAIRS, ATT_STARTS + 1, 2, ATT_Q, ATT_WIN), lambda g, b, i: (g, 0, 0, 0, 0))
    pad = pltpu.VMEM((seq + ATT_PAD, ATT_COLS), BF16)
    return ni, q_spec, k_spec, v_spec, w_spec, b_spec, pad


def _att_bias_tiles(wvec, exchange):
    (tiles,), moved = _call(
        lambda w_ref, o_ref: _att_bias(w_ref, o_ref), name="att_bias", grid=(ATT_HEADS // 2,),
        in_specs=[pl.BlockSpec((None, 2, BIAS_LEN), lambda hp: (hp, 0, 0))],
        out_specs=[pl.BlockSpec((None, ATT_STARTS + 1, 2, ATT_Q, ATT_WIN), lambda hp: (hp, 0, 0, 0, 0))],
        out_shape=[jax.ShapeDtypeStruct((ATT_HEADS // 2, ATT_STARTS + 1, 2, ATT_Q, ATT_WIN), F32)],
        semantics=("parallel",), args=(wvec,), exchange=exchange)
    return tiles, moved


def _att_pad(src_ref, pad_ref):
    pad_ref[:ATT_PAD, :] = jnp.zeros((ATT_PAD, ATT_COLS), BF16)
    pad_ref[ATT_PAD:, :] = src_ref[...]


def _att_head(q2, sel):
    return jnp.where(sel, q2, jnp.zeros_like(q2)) * 0.125


def _att_softmax_rows(s_ref, bias_ref, rows):
    s = s_ref[rows, :] + bias_ref[rows, :]
    ex = jnp.exp(s - jnp.max(s, axis=-1, keepdims=True))
    return ex, 1.0 / jnp.sum(ex, axis=-1, keepdims=True)


def _att_fwd(proj, bias, batch, seq, exchange):
    ni, q_spec, k_spec, v_spec, _, b_spec, pad = _att_specs(batch, seq)

    def body(q_ref, k_ref, v_ref, bias_ref, o_ref, kp_ref, vp_ref, s_ref, e_ref):
        i = pl.program_id(2)

        @pl.when(i == 0)
        def _():
            _att_pad(k_ref, kp_ref)
            _att_pad(v_ref, vp_ref)

        win = pl.ds(pl.multiple_of(i * ATT_Q, ATT_Q), ATT_WIN)
        lo = lax.broadcasted_iota(jnp.int32, (1, LANES), 1) < 64
        start = jnp.minimum(i, ATT_STARTS)
        for pair in range(ATT_PAIRS):
            cols = slice(pair * LANES, (pair + 1) * LANES)
            k2, v2, q2 = kp_ref[win, cols], vp_ref[win, cols], q_ref[:, cols]
            out = jnp.zeros((ATT_Q, LANES), F32)
            for e in range(2):
                h = 2 * pair + e
                sel = lo if e == 0 else jnp.logical_not(lo)
                s_ref[h] = _dot_nt(_att_head(q2, sel), k2)
                rsum = []
                for c in range(ATT_Q // ATT_ROWS):
                    rows = slice(c * ATT_ROWS, (c + 1) * ATT_ROWS)
                    ex, r = _att_softmax_rows(s_ref.at[h], bias_ref.at[pair, start, e], rows)
                    e_ref[h, rows, :] = ex.astype(BF16)
                    rsum.append(r)
                out = out + _dot(e_ref[h], jnp.where(sel, v2, jnp.zeros_like(v2))) * jnp.concatenate(rsum, axis=0)
            o_ref[:, cols] = out.astype(BF16)

    heads = 2 * ATT_PAIRS
    return _call(
        body, name="att_fwd", grid=(ATT_HEADS // heads, batch, ni),
        in_specs=[q_spec, k_spec, v_spec, b_spec],
        out_specs=[pl.BlockSpec((ATT_Q, ATT_COLS), lambda g, b, i: (b * ni + i, g))],
        out_shape=[jax.ShapeDtypeStruct((batch * seq, ATT_HEADS * 64), BF16)],
        scratch=[pad, pad, pltpu.VMEM((heads, ATT_Q, ATT_WIN), F32), pltpu.VMEM((heads, ATT_Q, ATT_WIN), BF16)],
        semantics=("arbitrary", "arbitrary", "arbitrary"), args=(proj, proj, proj, bias), exchange=exchange)


GL_HALF = 512


def _gl_specs(tm):
    return [pl.BlockSpec((tm, GL_HALF), lambda i, c=C_GL // GL_HALF + j: (i, c)) for j in range(4)]


def _gates(gl_refs, b_ref):
    logits = [ref[...].astype(F32) for ref in gl_refs]
    gr = _sigmoid(jnp.concatenate(logits[:2], axis=1) + b_ref[:, :D_MODEL])
    ga = _sigmoid(jnp.concatenate(logits[2:], axis=1) + b_ref[:, D_MODEL:])
    return gr, ga


def _whole(a):
    return pl.BlockSpec(a.shape, lambda i: (0,) * a.ndim)


def _mix_out_fwd(gro, ao, proj, b_gate, w_ret, w_att_t, x2, w_out, g2):
    t = gro.shape[0]
    tm = min(256, t)

    def body(gro_ref, ao_ref, gl0, gl1, gl2, gl3, b_ref, wr_ref, wa_ref, x_ref, wo_ref, g_ref,
             z_ref, yr_ref, ya_ref, h_ref, hn_ref):
        yr = _dot(gro_ref[...], wr_ref[...])
        ya = _dot_nt(ao_ref[...], wa_ref[...])
        yr_ref[...] = yr.astype(BF16)
        ya_ref[...] = ya.astype(BF16)
        gr, ga = _gates((gl0, gl1, gl2, gl3), b_ref)
        z = (gr * yr + ga * ya).astype(BF16)
        z_ref[...] = z
        h = x_ref[...] + _dot(z, wo_ref[...])
        h_ref[...] = h
        r = lax.rsqrt(jnp.mean(h * h, axis=-1, keepdims=True) + EPS)
        hn_ref[...] = (h * r * g_ref[...]).astype(BF16)

    row = pl.BlockSpec((tm, D_MODEL), lambda i: (i, 0))
    return pl.pallas_call(
        body, name="mix_out_fwd", grid=(t // tm,),
        in_specs=[row, pl.BlockSpec((tm, 512), lambda i: (i, 0)), *_gl_specs(tm),
                  _whole(b_gate), _whole(w_ret), _whole(w_att_t), row, _whole(w_out), _whole(g2)],
        out_specs=[row] * 5,
        out_shape=[jax.ShapeDtypeStruct((t, D_MODEL), BF16)] * 3
                  + [jax.ShapeDtypeStruct((t, D_MODEL), F32), jax.ShapeDtypeStruct((t, D_MODEL), BF16)],
        compiler_params=_params("parallel"),
    )(gro, ao, proj, proj, proj, proj, b_gate, w_ret, w_att_t, x2, w_out, g2)


def _col_chunks(width, chunk=384):
    return [slice(lo, min(lo + chunk, width)) for lo in range(0, width, chunk)]


def _ffn_up(hn, wg_t, wu_t):
    t = hn.shape[0]
    tm, tn = min(512, t), D_FF // 2

    def body(h_ref, wg_ref, wu_ref, g_ref, u_ref, a_ref):
        g = _dot_nt(h_ref[...], wg_ref[...])
        u = _dot_nt(h_ref[...], wu_ref[...])
        g_ref[...] = g.astype(BF16)
        u_ref[...] = u.astype(BF16)
        a_ref[...] = (g * _sigmoid(g) * u).astype(BF16)

    w_spec = pl.BlockSpec((tn, D_MODEL), lambda j, i: (j, 0))
    out = pl.BlockSpec((tm, tn), lambda j, i: (i, j))
    return pl.pallas_call(
        body, name="ffn_up", grid=(D_FF // tn, t // tm),
        in_specs=[pl.BlockSpec((tm, D_MODEL), lambda j, i: (i, 0)), w_spec, w_spec],
        out_specs=[out, out, out],
        out_shape=[jax.ShapeDtypeStruct((t, D_FF), BF16)] * 3,
        compiler_params=_params("parallel", "parallel"),
    )(hn, wg_t, wu_t)


def _ffn_down_loss(a, h1, tgt, w_down, g3):
    t = a.shape[0]
    tm = min(512, t)

    def body(a_ref, h_ref, t_ref, w_ref, g_ref, dh_ref, dhb_ref, loss_ref, dg_ref):
        @pl.when(pl.program_id(0) == 0)
        def _():
            loss_ref[...] = jnp.zeros_like(loss_ref)
            dg_ref[...] = jnp.zeros_like(dg_ref)

        g = g_ref[...]
        h2 = h_ref[...] + _dot(a_ref[...], w_ref[...])
        r = lax.rsqrt(jnp.mean(h2 * h2, axis=-1, keepdims=True) + EPS)
        err = h2 * r * g - t_ref[...]
        loss_ref[...] += jnp.sum(err * err) * (0.5 / D_MODEL)
        dy = err * (1.0 / D_MODEL)
        dh, dg_rows = _rms_bwd(h2, g, dy)
        dg_ref[...] += jnp.sum(dg_rows, axis=0, keepdims=True)
        dh_ref[...] = dh
        dhb_ref[...] = dh.astype(BF16)

    row = pl.BlockSpec((tm, D_MODEL), lambda i: (i, 0))
    vec = pl.BlockSpec((1, D_MODEL), lambda i: (0, 0))
    return pl.pallas_call(
        body, name="ffn_down_loss", grid=(t // tm,),
        in_specs=[pl.BlockSpec((tm, D_FF), lambda i: (i, 0)), row, row,
                  pl.BlockSpec((D_FF, D_MODEL), lambda i: (0, 0)), vec],
        out_specs=[row, row, pl.BlockSpec((1, LANES), lambda i: (0, 0)), vec],
        out_shape=[jax.ShapeDtypeStruct((t, D_MODEL), F32), jax.ShapeDtypeStruct((t, D_MODEL), BF16),
                   jax.ShapeDtypeStruct((1, LANES), F32), jax.ShapeDtypeStruct((1, D_MODEL), F32)],
        compiler_params=_params("arbitrary"),
    )(a, h1, tgt, w_down, g3)


def _ffn_bwd_act(dh2b, w_down, g_act, u_act, exchange):
    t = dh2b.shape[0]
    tm, tn = min(512, t), D_FF // 2

    def body(d_ref, w_ref, g_ref, u_ref, dg_ref, du_ref):
        d = d_ref[...]
        for cols in _col_chunks(tn):
            da = _dot_nt(d, w_ref[cols, :])
            g = g_ref[:, cols].astype(F32)
            u = u_ref[:, cols].astype(F32)
            sg = _sigmoid(g)
            dg_ref[:, cols] = (da * u * sg * (1.0 + g * (1.0 - sg))).astype(BF16)
            du_ref[:, cols] = (da * g * sg).astype(BF16)

    blk = pl.BlockSpec((tm, tn), lambda j, i: (i, j))
    return _call(
        body, name="ffn_bwd_act", grid=(D_FF // tn, t // tm),
        in_specs=[pl.BlockSpec((tm, D_MODEL), lambda j, i: (i, 0)),
                  pl.BlockSpec((tn, D_MODEL), lambda j, i: (j, 0)), blk, blk],
        out_specs=[blk, blk],
        out_shape=[jax.ShapeDtypeStruct((t, D_FF), BF16)] * 2,
        semantics=("parallel", "parallel"), args=(dh2b, w_down, g_act, u_act), exchange=exchange)


def _ffn_bwd_in(dg, du, wg_t, wu_t, h1, dh2, g2, exchange):
    t = dg.shape[0]
    tm = min(256, t)

    def body(dg_ref, du_ref, wg_ref, wu_ref, h_ref, d2_ref, g_ref, dh_ref, dhb_ref, gn_ref):
        @pl.when(pl.program_id(0) == 0)
        def _():
            gn_ref[...] = jnp.zeros_like(gn_ref)

        dhn = _dot(dg_ref[...], wg_ref[...]) + _dot(du_ref[...], wu_ref[...])
        dx, dg_rows = _rms_bwd(h_ref[...], g_ref[...], dhn)
        dh = d2_ref[...] + dx
        dh_ref[...] = dh
        dhb_ref[...] = dh.astype(BF16)
        gn_ref[...] += jnp.sum(dg_rows, axis=0, keepdims=True)

    act = pl.BlockSpec((tm, D_FF), lambda i: (i, 0))
    row = pl.BlockSpec((tm, D_MODEL), lambda i: (i, 0))
    return _call(
        body, name="ffn_bwd_in", grid=(t // tm,),
        in_specs=[act, act, _whole(wg_t), _whole(wu_t), row, row, _whole(g2)],
        out_specs=[row, row, _whole(g2)],
        out_shape=[jax.ShapeDtypeStruct((t, D_MODEL), F32), jax.ShapeDtypeStruct((t, D_MODEL), BF16),
                   jax.ShapeDtypeStruct((1, D_MODEL), F32)],
        semantics=("arbitrary",), args=(dg, du, wg_t, wu_t, h1, dh2, g2), exchange=exchange)


def _mix_bwd(dh1b, w_out, proj, b_gate, y_ret, y_att, w_ret, w_att_t):
    t = dh1b.shape[0]
    tm = min(256, t)

    def body(d_ref, wo_ref, gl0, gl1, gl2, gl3, b_ref, yr_ref, ya_ref, wr_ref, wa_ref,
             dyr_ref, dya_ref, dglr_ref, dgla_ref, dgro_ref, dao_ref, db_ref):
        @pl.when(pl.program_id(0) == 0)
        def _():
            db_ref[...] = jnp.zeros_like(db_ref)

        dz = _dot_nt(d_ref[...], wo_ref[...])
        gr, ga = _gates((gl0, gl1, gl2, gl3), b_ref)
        dyr = (dz * gr).astype(BF16)
        dya = (dz * ga).astype(BF16)
        dyr_ref[...] = dyr
        dya_ref[...] = dya
        dglr = dz * yr_ref[...].astype(F32) * gr * (1.0 - gr)
        dgla = dz * ya_ref[...].astype(F32) * ga * (1.0 - ga)
        dglr_ref[...] = dglr.astype(BF16)
        dgla_ref[...] = dgla.astype(BF16)
        db_ref[:, :D_MODEL] += jnp.sum(dglr, axis=0, keepdims=True)
        db_ref[:, D_MODEL:] += jnp.sum(dgla, axis=0, keepdims=True)
        dgro_ref[...] = _dot_nt(dyr, wr_ref[...]).astype(BF16)
        dao_ref[...] = _dot(dya, wa_ref[...]).astype(BF16)

    row = pl.BlockSpec((tm, D_MODEL), lambda i: (i, 0))
    half = pl.BlockSpec((tm, 512), lambda i: (i, 0))
    return pl.pallas_call(
        body, name="mix_bwd", grid=(t // tm,),
        in_specs=[row, _whole(w_out), *_gl_specs(tm), _whole(b_gate), row, row, _whole(w_ret), _whole(w_att_t)],
        out_specs=[row, row, row, row, row, half, _whole(b_gate)],
        out_shape=[jax.ShapeDtypeStruct((t, D_MODEL), BF16)] * 5
                  + [jax.ShapeDtypeStruct((t, 512), BF16), jax.ShapeDtypeStruct(b_gate.shape, F32)],
        compiler_params=_params("arbitrary"),
    )(dh1b, w_out, proj, proj, proj, proj, b_gate, y_ret, y_att, w_ret, w_att_t)


def _ret_bwd(dgro, proj, o_ret, qr, kr, cs, sn, lg_arr, batch, seq, exchange):
    t = batch * seq
    nt = seq // RET_TILE

    def body(dgro_ref, rg_ref, o_ref, qr_ref, kr_ref, v_ref, cs_ref, sn_ref, lg_ref,
             dq_ref, dk_ref, dv_ref, drg_ref, do_ref, st_ref):
        lg = lg_ref[:, 0:1]
        inside, q_dec, k_dec, tile_dec = _decay(lg)

        state = jnp.zeros((RET_KEY_DIM, RET_VAL_DIM), F32)
        for i in range(nt - 1):
            rows = slice(i * RET_TILE, (i + 1) * RET_TILE)
            state = state * tile_dec + _dot_tn(_scaled(kr_ref[rows, :], k_dec), v_ref[rows, :])
            st_ref[i + 1] = state.astype(BF16)

        for i in range(nt):
            rows = slice(i * RET_TILE, (i + 1) * RET_TILE)
            o = o_ref[rows, :]
            xc = o - jnp.mean(o, axis=-1, keepdims=True)
            rs = lax.rsqrt(jnp.mean(xc * xc, axis=-1, keepdims=True) + EPS)
            nrm = xc * rs
            rg = rg_ref[rows, :].astype(F32)
            sg = _sigmoid(rg)
            dg = dgro_ref[rows, :].astype(F32)
            drg_ref[rows, :] = (dg * nrm * sg * (1.0 + rg * (1.0 - sg))).astype(BF16)
            dn = dg * rg * sg
            do = rs * (dn - jnp.mean(dn, axis=-1, keepdims=True)
                       - nrm * jnp.mean(dn * nrm, axis=-1, keepdims=True))
            do_ref[rows, :] = do.astype(BF16)

        dstate = jnp.zeros((RET_KEY_DIM, RET_VAL_DIM), F32)
        for i in reversed(range(nt)):
            rows = slice(i * RET_TILE, (i + 1) * RET_TILE)
            qi, ki, vi, doi = qr_ref[rows, :], kr_ref[rows, :], v_ref[rows, :], do_ref[rows, :]
            p = (_dot_nt(qi, ki) * inside).astype(BF16)
            dp = (_dot_nt(doi, vi) * inside).astype(BF16)
            dq = _dot(dp, ki)
            dk = _dot_tn(dp, qi)
            dv = _dot_tn(p, doi)
            if i > 0:
                dq = dq + _dot_nt(doi, st_ref[i]) * q_dec
            if i < nt - 1:
                dsb = dstate.astype(BF16)
                dk = dk + _dot_nt(vi, dsb) * k_dec
                dv = dv + _dot(_scaled(ki, k_dec), dsb)
            if i > 0:
                dstate = dstate * tile_dec + _dot_tn(_scaled(qi, q_dec), doi)
            dq_ref[rows, :] = (dq * cs_ref[rows, :] - pltpu.roll(dq, 64, 1) * sn_ref[rows, :]).astype(BF16)
            dk = (dk * cs_ref[rows, :] - pltpu.roll(dk, 64, 1) * sn_ref[rows, :]) * (RET_KEY_DIM ** -0.5)
            dk_ref[rows, :] = dk.astype(BF16)
            dv_ref[rows, :] = dv.astype(BF16)

    key, val, tab, lgs = _ret_specs(seq)
    return _call(
        body, name="ret_bwd", grid=(batch, RET_HEADS),
        in_specs=[val(0), val(C_RG), val(0), key(0), key(0), val(C_RV), tab, tab, lgs],
        out_specs=[key(0), key(0), val(0), val(0)],
        out_shape=[jax.ShapeDtypeStruct((t, RET_HEADS * RET_KEY_DIM), BF16)] * 2
                  + [jax.ShapeDtypeStruct((t, RET_HEADS * RET_VAL_DIM), BF16)] * 2,
        scratch=[pltpu.VMEM((seq, RET_VAL_DIM), BF16), pltpu.VMEM((nt, RET_KEY_DIM, RET_VAL_DIM), BF16)],
        semantics=("parallel", "parallel"), args=(dgro, proj, o_ret, qr, kr, proj, cs, sn, lg_arr),
        exchange=exchange)


def _att_bwd(proj, bias, dao, batch, seq, exchange):
    ni, q_spec, k_spec, v_spec, w_spec, b_spec, pad = _att_specs(batch, seq)
    t = batch * seq

    def body(q_ref, k_ref, v_ref, bias_ref, do_ref, dq_ref, dk_ref, dv_ref, dw_ref,
             dbias_ref, dk_acc, dv_acc, kp_ref, vp_ref, s_ref, dp_ref, e_ref, ds_ref):
        b, i = pl.program_id(1), pl.program_id(2)

        @pl.when((b == 0) & (i == 0))
        def _():
            dbias_ref[...] = jnp.zeros_like(dbias_ref)

        @pl.when(i == 0)
        def _():
            _att_pad(k_ref, kp_ref)
            _att_pad(v_ref, vp_ref)
            dk_acc[...] = jnp.zeros_like(dk_acc)
            dv_acc[...] = jnp.zeros_like(dv_acc)

        win = pl.ds(pl.multiple_of(i * ATT_Q, ATT_Q), ATT_WIN)
        lo = lax.broadcasted_iota(jnp.int32, (1, LANES), 1) < 64
        start = jnp.minimum(i, ATT_STARTS)
        for pair in range(ATT_PAIRS):
            cols = slice(pair * LANES, (pair + 1) * LANES)
            k2, v2, q2, do2 = kp_ref[win, cols], vp_ref[win, cols], q_ref[:, cols], do_ref[:, cols]
            dq = jnp.zeros((ATT_Q, LANES), F32)
            dk = jnp.zeros((LANES, ATT_WIN), F32)
            dv = jnp.zeros((LANES, ATT_WIN), F32)
            for e in range(2):
                h = 2 * pair + e
                sel = lo if e == 0 else jnp.logical_not(lo)
                qm = _att_head(q2, sel)
                dom = jnp.where(sel, do2, jnp.zeros_like(do2))
                s_ref[h] = _dot_nt(qm, k2)
                dp_ref[h] = _dot_nt(dom, v2)
                rsum = []
                for c in range(ATT_Q // ATT_ROWS):
                    rows = slice(c * ATT_ROWS, (c + 1) * ATT_ROWS)
                    ex, r = _att_softmax_rows(s_ref.at[h], bias_ref.at[pair, start, e], rows)
                    dp = dp_ref[h, rows, :]
                    mean = jnp.sum(dp * ex, axis=-1, keepdims=True) * r
                    ds = ex * ((dp - mean) * r)
                    dbias_ref[h, rows, :] += ds
                    ds_ref[h, rows, :] = ds.astype(BF16)
                    e_ref[h, rows, :] = ex.astype(BF16)
                    rsum.append(r)
                dq = dq + _dot(ds_ref[h], jnp.where(sel, k2, jnp.zeros_like(k2)))
                dk = dk + _dot_tn(qm, ds_ref[h])
                dv = dv + _dot_tn((dom.astype(F32) * jnp.concatenate(rsum, axis=0)).astype(BF16), e_ref[h])
            dq_ref[:, cols] = (dq * 0.125).astype(BF16)
            dk_acc[cols, win] += dk
            dv_acc[cols, win] += dv

        @pl.when(i == ni - 1)
        def _():
            dk_ref[...] = dk_acc[:, ATT_PAD:].T.astype(BF16)
            dv_ref[...] = dv_acc[:, ATT_PAD:].T.astype(BF16)

        @pl.when((b == batch - 1) & (i == ni - 1))
        def _():
            n_i = lax.broadcasted_iota(jnp.int32, (ATT_Q, BIAS_LEN), 0)
            for h in range(heads):
                xw = jnp.concatenate([jnp.zeros((ATT_Q, BIAS_LEN - ATT_WIN), F32), dbias_ref[h]], axis=1)
                for bit in range(8):
                    xw = jnp.where(((n_i >> bit) & 1) == 1, pltpu.roll(xw, BIAS_LEN - (1 << bit), 1), xw)
                dw_ref[h // 2, h % 2:h % 2 + 1, :] = jnp.sum(xw, axis=0, keepdims=True)

    heads = 2 * ATT_PAIRS
    seq_blk = pl.BlockSpec((seq, ATT_COLS), lambda g, b, i: (b, g))
    q_out = pl.BlockSpec((ATT_Q, ATT_COLS), lambda g, b, i: (b * ni + i, g))
    tile_f32, tile_bf16 = pltpu.VMEM((heads, ATT_Q, ATT_WIN), F32), pltpu.VMEM((heads, ATT_Q, ATT_WIN), BF16)
    acc = pltpu.VMEM((ATT_COLS, seq + ATT_PAD), F32)
    return _call(
        body, name="att_bwd", grid=(ATT_HEADS // heads, batch, ni),
        in_specs=[q_spec, k_spec, v_spec, b_spec, q_out],
        out_specs=[q_out, seq_blk, seq_blk, w_spec],
        out_shape=[jax.ShapeDtypeStruct((t, 512), BF16)] * 3
                  + [jax.ShapeDtypeStruct((ATT_HEADS // 2, 2, BIAS_LEN), F32)],
        scratch=[tile_f32, acc, acc, pad, pad, tile_f32, tile_f32, tile_bf16, tile_bf16],
        semantics=("arbitrary", "arbitrary", "arbitrary"), args=(proj, proj, proj, bias, dao), exchange=exchange,
        vmem=VMEM_LIMIT_ATT_BWD)


def _rms_in_bwd(x2, dxn, dh1, g1):
    t = x2.shape[0]
    tm = min(512, t)

    def body(x_ref, d_ref, h_ref, g_ref, dx_ref, dg_ref):
        @pl.when(pl.program_id(0) == 0)
        def _():
            dg_ref[...] = jnp.zeros_like(dg_ref)

        dx, dg_rows = _rms_bwd(x_ref[...], g_ref[...], d_ref[...])
        dx_ref[...] = h_ref[...] + dx
        dg_ref[...] += jnp.sum(dg_rows, axis=0, keepdims=True)

    row = pl.BlockSpec((tm, D_MODEL), lambda i: (i, 0))
    vec = pl.BlockSpec((1, D_MODEL), lambda i: (0, 0))
    return pl.pallas_call(
        body, name="rms_in_bwd", grid=(t // tm,),
        in_specs=[row, row, row, vec], out_specs=[row, vec],
        out_shape=[jax.ShapeDtypeStruct((t, D_MODEL), F32), jax.ShapeDtypeStruct((1, D_MODEL), F32)],
        compiler_params=_params("arbitrary"),
    )(x2, dxn, dh1, g1)


def _pack_small(dg1, dbr, dba, dg2, dg3, dw, loss):
    def body(a_ref, b_ref, c_ref, d_ref, e_ref, w_ref, l_ref, o_ref):
        o_ref[...] = jnp.zeros_like(o_ref)
        for r, ref in enumerate((a_ref, b_ref, c_ref, d_ref, e_ref)):
            o_ref[r:r + 1, :] = ref[...]
        o_ref[5:6, 0:LANES] = l_ref[...]
        for hp in range(ATT_HEADS // 2):
            o_ref[8 + 2 * hp:10 + 2 * hp, :] = w_ref[hp]

    return pl.pallas_call(body, name="pack_small",
                          out_shape=jax.ShapeDtypeStruct((16, D_MODEL), F32))(dg1, dbr, dba, dg2, dg3, dw, loss)


def _rotary_tables(seq):
    freqs = ROPE_BASE ** (-jnp.arange(0, RET_KEY_DIM, 2, dtype=F32) / RET_KEY_DIM)
    ang = jnp.arange(seq, dtype=F32)[:, None] * freqs[None, :]
    cos, sin = jnp.cos(ang), jnp.sin(ang)
    return jnp.concatenate([cos, cos], axis=1), jnp.concatenate([-sin, sin], axis=1)


def _bias_rows(rel_bias):
    n_far = BIAS_LEN - ATT_Q - MAX_REL + 1
    n_near = BIAS_LEN - n_far - (N_REL - 2)
    w = jnp.concatenate([jnp.broadcast_to(rel_bias[:, N_REL - 1:], (ATT_HEADS, n_far)),
                         rel_bias[:, 1:N_REL - 1][:, ::-1],
                         jnp.broadcast_to(rel_bias[:, :1], (ATT_HEADS, n_near))], axis=1)
    return w.reshape(ATT_HEADS // 2, 2, BIAS_LEN)


def _bias_rows_bwd(dw):
    n_far = BIAS_LEN - ATT_Q - MAX_REL + 1
    mid = dw[:, n_far:n_far + N_REL - 2][:, ::-1]
    return jnp.concatenate([jnp.sum(dw[:, n_far + N_REL - 2:], axis=1, keepdims=True), mid,
                            jnp.sum(dw[:, :n_far], axis=1, keepdims=True)], axis=1)


def _step(x, tgt, norm_mix, b_gate, norm_ffn, norm_final, rel_bias_shard, shard):
    batch, seq, _ = x.shape
    t = batch * seq
    n_rb = rel_bias_shard.shape[-1]
    x2, tgt2 = x.reshape(t, D_MODEL), tgt.reshape(t, D_MODEL)
    g3 = norm_final.reshape(1, D_MODEL)
    cs, sn = _rotary_tables(seq)
    lg = np.log(1.0 - 2.0 ** (-5.0 - np.arange(RET_HEADS, dtype=np.float32))).astype(np.float32)
    lg_arr = jnp.asarray(np.broadcast_to(lg[:, None, None], (RET_HEADS, 1, LANES)))

    def gather(*names):
        return _ChipGather([shard[nm] for nm in names])

    def scatter(*grads):
        return _Scatter(grads)

    rb_pad = jnp.pad(rel_bias_shard, ((0, 0), (0, LANES - n_rb)))
    (xn,), (w_in_half, rb_full) = _rms_fwd(x2, norm_mix,
                                           _ChipGather([shard["w_in_t"], rb_pad], parts=[(0, 2), (0, 1)]))
    rb_full = rb_full.reshape(N_DEV, ATT_HEADS, LANES)[:, :, :n_rb]
    bias, (w_in_t,) = _att_bias_tiles(
        _bias_rows(jnp.transpose(rb_full, (1, 0, 2)).reshape(ATT_HEADS, N_DEV * n_rb)),
        _ChipGather([shard["w_in_t"]], parts=[(1, 2)], into=[w_in_half]))
    proj, (w_ret, w_att_t, w_out, w_gate_t) = _mm(
        xn, w_in_t, tb=True, out_dtype=BF16, tm=1024, tn=1664, tk=1024, name="proj",
        exchange=gather("w_ret", "w_att_t", "w_out", "w_gate_t"))
    (gro, o_ret, qr, kr), _ = _ret_fwd(proj, cs, sn, lg_arr, batch, seq, None)
    (ao,), (w_up_t, w_down) = _att_fwd(proj, bias, batch, seq, gather("w_up_t", "w_down"))
    z, y_ret, y_att, h1, hn = _mix_out_fwd(gro, ao, proj, b_gate, w_ret, w_att_t, x2, w_out, norm_ffn)
    g_act, u_act, a_act = _ffn_up(hn, w_gate_t, w_up_t)
    dh2, dh2b, loss, dg3 = _ffn_down_loss(a_act, h1, tgt2, w_down, g3)

    wg = dict(out_dtype=BF16, tn=1024, ta=True)
    slots = {}
    dw_down = _mm(a_act, dh2b, tm=1408, tk=2048, name="dw_down", **wg)
    (d_gact, d_uact), _ = _ffn_bwd_act(dh2b, w_down, g_act, u_act, None)
    dw_gate = _mm(d_gact, hn, tm=1408, tk=2048, name="dw_gate", **wg)
    dw_up = _mm(d_uact, hn, tm=1408, tk=2048, name="dw_up", **wg)
    (dh1, dh1b, dg2), (slots["w_down"],) = _ffn_bwd_in(d_gact, d_uact, w_gate_t, w_up_t, h1, dh2, norm_ffn,
                                                     scatter(dw_down))
    dw_out = _mm(z, dh1b, tm=1024, tk=2048, name="dw_out", **wg)
    dyr, dya, dglr, dgla, dgro, dao, db = _mix_bwd(dh1b, w_out, proj, b_gate, y_ret, y_att, w_ret, w_att_t)
    dw_ret = _mm(gro, dyr, tm=1024, tk=2048, name="dw_ret", **wg)
    dw_att = _mm(dya, ao, tm=1024, tk=2048, name="dw_att", **wg)
    (drq, drk, drv, drg), _ = _ret_bwd(dgro, proj, o_ret, qr, kr, cs, sn, lg_arr, batch, seq, None)
    (daq, dak, dav, dw), (slots["w_gate_t"], slots["w_out"], slots["w_ret"], slots["w_att_t"]) = _att_bwd(
        proj, bias, dao, batch, seq, scatter(dw_gate, dw_out, dw_ret, dw_att))
    dproj = [drq, drk, drv, drg, daq, dak, dav, dglr, dgla]
    dw_in, (slots["w_up_t"],) = _mm_pieces(dproj, xn, ta=True, out_dtype=BF16, tm=512, tn=1024, tk=1024,
                                           name="dw_in", exchange=scatter(dw_up))
    (dw_in_sibling,) = _alone(_PairSwap([dw_in]), "swap_w_in")
    dw_in_pairs = _pair_add(dw_in, dw_in_sibling, "pair_w_in")
    dxn, (slots["w_in_t"],) = _mm_pieces(dproj, w_in_t, ta=False, out_dtype=F32, tm=1024, tn=1024, tk=512, name="dxn",
                                  exchange=_ChipScatter([dw_in_pairs]))
    dx, dg1 = _rms_in_bwd(x2, dxn, dh1, norm_mix)
    small = _pack_small(dg1, db[:, :D_MODEL], db[:, D_MODEL:], dg2, dg3, dw, loss)
    (small_slots,) = _alone(_ChipGather([small]), "gather_small")
    return dx.reshape(batch, seq, D_MODEL), slots, small_slots.reshape(N_DEV, 16, D_MODEL)


def _row_tile(r, c):
    return max(d for d in range(16, r + 1, 16) if r % d == 0 and (d * c <= 256 * 1024 or d == 16))


def _pair_add(grad, got, name):
    _, r, c = got.shape
    tr = r
    core = lax.axis_index("c").astype(jnp.int32).reshape(1)

    def body(core_ref, g_ref, a_ref, o_ref):
        o_ref[...] = (g_ref[...].astype(F32) + a_ref[...].astype(F32)).astype(o_ref.dtype)

    blk = pl.BlockSpec((None, tr, c), lambda q, i, core_ref: (q, i, 0))
    return pl.pallas_call(
        body, name=name,
        grid_spec=pltpu.PrefetchScalarGridSpec(
            num_scalar_prefetch=1, grid=(4, r // tr),
            in_specs=[pl.BlockSpec((None, None, tr, c), lambda q, i, core_ref: (q, core_ref[0], i, 0)), blk],
            out_specs=blk),
        out_shape=jax.ShapeDtypeStruct(got.shape, got.dtype),
        compiler_params=_params("parallel", "parallel"),
    )(core, grad.reshape(4, 2, r, c), got)


def _sum_slots(slots, name):
    n, r, c = slots.shape
    tr = _row_tile(r, c)

    def body(s_ref, o_ref):
        acc = s_ref[0].astype(F32)
        for s in range(1, n):
            acc = acc + s_ref[s].astype(F32)
        o_ref[...] = acc

    return pl.pallas_call(
        body, name=name, grid=(r // tr,),
        in_specs=[pl.BlockSpec((n, tr, c), lambda i: (0, i, 0))],
        out_specs=pl.BlockSpec((tr, c), lambda i: (i, 0)),
        out_shape=jax.ShapeDtypeStruct((r, c), F32),
        compiler_params=_params("parallel"),
    )(slots)


def _adamw_math(w, g, m, v):
    m = ADAM_B1 * m + (1.0 - ADAM_B1) * g
    v = ADAM_B2 * v + (1.0 - ADAM_B2) * (g * g)
    m_hat = m / (1.0 - ADAM_B1 ** ADAM_STEP)
    v_hat = v / (1.0 - ADAM_B2 ** ADAM_STEP)
    return -ADAM_LR * (m_hat / (jnp.sqrt(v_hat) + ADAM_EPS) + ADAM_WD * w), m, v


def _adamw(w, slots, m, v, name):
    n, r, c = slots.shape
    tr = _row_tile(r, c)

    def body(w_ref, s_ref, m_ref, v_ref, g_ref, d_ref, nm_ref, nv_ref):
        g = s_ref[0].astype(F32)
        for s in range(1, n):
            g = g + s_ref[s].astype(F32)
        g_ref[...] = g
        d_ref[...], nm_ref[...], nv_ref[...] = _adamw_math(w_ref[...], g, m_ref[...], v_ref[...])

    blk = pl.BlockSpec((tr, c), lambda i: (i, 0))
    return pl.pallas_call(
        body, name=name, grid=(r // tr,),
        in_specs=[blk, pl.BlockSpec((n, tr, c), lambda i: (0, i, 0)), blk, blk], out_specs=[blk] * 4,
        out_shape=[jax.ShapeDtypeStruct((r, c), F32)] * 4,
        compiler_params=_params("parallel"),
    )(w, slots, m, v)


def _adamw_small(ws, gs, ms, vs):
    n = len(ws)

    def body(*refs):
        for i in range(n):
            w_ref, g_ref, m_ref, v_ref = (refs[j * n + i] for j in range(4))
            d_ref, nm_ref, nv_ref = (refs[(4 + j) * n + i] for j in range(3))
            d_ref[...], nm_ref[...], nv_ref[...] = _adamw_math(w_ref[...], g_ref[...], m_ref[...], v_ref[...])

    shapes = [jax.ShapeDtypeStruct(w.shape, F32) for w in ws]
    outs = pl.pallas_call(body, name="adamw_small", out_shape=shapes * 3)(*ws, *gs, *ms, *vs)
    return outs[:n], outs[n:2 * n], outs[2 * n:]


def kernel(x, norm_mix, w_in, b_gate, rel_bias, w_ret_out, w_att_out, w_out, norm_ffn, w_ffn_gate, w_ffn_up, w_ffn_down, norm_final, loss_target, m_norm_mix, m_w_in, m_b_gate, m_rel_bias, m_w_ret_out, m_w_att_out, m_w_out, m_norm_ffn, m_w_ffn_gate, m_w_ffn_up, m_w_ffn_down, m_norm_final, v_norm_mix, v_w_in, v_b_gate, v_rel_bias, v_w_ret_out, v_w_att_out, v_w_out, v_norm_ffn, v_w_ffn_gate, v_w_ffn_up, v_w_ffn_down, v_norm_final):
    me = _index(_place())
    n_rb = rel_bias.shape[-1]

    shard = dict(w_in_t=w_in[0].T, w_gate_t=w_ffn_gate[0].T, w_up_t=w_ffn_up[0].T, w_down=w_ffn_down[0],
                 w_ret=w_ret_out[0], w_out=w_out[0], w_att_t=w_att_out[0].T)
    shard = {nm: s.astype(BF16) for nm, s in shard.items()}
    dx, slots, small_slots = _step(x, loss_target, norm_mix, b_gate, norm_ffn, norm_final, rel_bias[0], shard)
    small_sum = _sum_slots(small_slots, "sum_small")
    loss = small_sum[5, 0]

    transposed = dict(w_in="w_in_t", w_ffn_gate="w_gate_t", w_ffn_up="w_up_t", w_att_out="w_att_t")
    plain = dict(w_ffn_down="w_down", w_ret_out="w_ret", w_out="w_out")
    g = dict(
        norm_mix=small_sum[0:1], b_gate=jnp.concatenate([small_sum[1:2], small_sum[2:3]], axis=1),
        norm_ffn=small_sum[3:4], norm_final=small_sum[4:5],
        rel_bias=lax.dynamic_slice_in_dim(_bias_rows_bwd(small_sum[8:16]), me * n_rb, n_rb, axis=1),
    )
    w = dict(norm_mix=norm_mix, w_in=w_in, b_gate=b_gate, rel_bias=rel_bias, w_ret_out=w_ret_out, w_att_out=w_att_out,
             w_out=w_out, norm_ffn=norm_ffn, w_ffn_gate=w_ffn_gate, w_ffn_up=w_ffn_up, w_ffn_down=w_ffn_down,
             norm_final=norm_final)
    m = dict(norm_mix=m_norm_mix, w_in=m_w_in, b_gate=m_b_gate, rel_bias=m_rel_bias, w_ret_out=m_w_ret_out,
             w_att_out=m_w_att_out, w_out=m_w_out, norm_ffn=m_norm_ffn, w_ffn_gate=m_w_ffn_gate, w_ffn_up=m_w_ffn_up,
             w_ffn_down=m_w_ffn_down, norm_final=m_norm_final)
    v = dict(norm_mix=v_norm_mix, w_in=v_w_in, b_gate=v_b_gate, rel_bias=v_rel_bias, w_ret_out=v_w_ret_out,
             w_att_out=v_w_att_out, w_out=v_w_out, norm_ffn=v_norm_ffn, w_ffn_gate=v_w_ffn_gate, w_ffn_up=v_w_ffn_up,
             w_ffn_down=v_w_ffn_down, norm_final=v_norm_final)
    order = ("norm_mix", "w_in", "b_gate", "rel_bias", "w_ret_out", "w_att_out", "w_out", "norm_ffn",
             "w_ffn_gate", "w_ffn_up", "w_ffn_down", "norm_final")
    small_names = ("norm_mix", "b_gate", "rel_bias", "norm_ffn", "norm_final")

    def flat(a):
        return a[0] if a.ndim == 3 else a.reshape(-1, a.shape[-1])

    grad, delta, new_m, new_v = {}, {}, {}, {}
    for nm in order:
        if nm in transposed:
            res = _adamw(w[nm][0].T, slots[transposed[nm]], m[nm][0].T, v[nm][0].T, "adamw_" + nm)
            grad[nm], delta[nm], new_m[nm], new_v[nm] = (a.T[None] for a in res)
        elif nm in plain:
            res = _adamw(flat(w[nm]), slots[plain[nm]], flat(m[nm]), flat(v[nm]), "adamw_" + nm)
            grad[nm], delta[nm], new_m[nm], new_v[nm] = (a.reshape(w[nm].shape) for a in res)
    ds, nms, nvs = _adamw_small([flat(w[nm]) for nm in small_names], [g[nm] for nm in small_names],
                                [flat(m[nm]) for nm in small_names], [flat(v[nm]) for nm in small_names])
    for i, nm in enumerate(small_names):
        grad[nm], delta[nm], new_m[nm], new_v[nm] = (a.reshape(w[nm].shape) for a in (g[nm], ds[i], nms[i], nvs[i]))

    return (loss, dx, *[grad[nm] for nm in order], *[delta[nm] for nm in order],
            *[new_m[nm] for nm in order], *[new_v[nm] for nm in order])
```

```python
import numpy as np
import jax
import jax.numpy as jnp
from jax import lax
from jax.experimental import pallas as pl
from jax.experimental.pallas import tpu as pltpu

F32 = jnp.float32
BF16 = jnp.bfloat16
MESH = pl.DeviceIdType.MESH

D_MODEL = 1024
CHUNK = 64
RET_HEADS = 4
RET_KEY_DIM = 128
RET_VAL_DIM = 256
ATT_HEADS = 8
BAND_CHUNKS = 8
MAX_REL = 256
N_REL = CHUNK + MAX_REL
D_FF = 2816
N_IN = 6656
ROPE_BASE = 10000.0
EPS = 1e-6
NEG_INF = -1e30
C_RQ, C_RK, C_RV, C_RG, C_AQ, C_AK, C_AV, C_GL = 0, 512, 1024, 2048, 3072, 3584, 4096, 4608

ADAM_LR = 0.001
ADAM_B1 = 0.9
ADAM_B2 = 0.999
ADAM_EPS = 1e-08
ADAM_WD = 0.01
ADAM_STEP = 10

N_DEV = 8
LANES = 128
RET_TILE = 256
ATT_Q = 256
ATT_PAD = BAND_CHUNKS * CHUNK
ATT_WIN = ATT_PAD + ATT_Q
ATT_STARTS = ATT_PAD // ATT_Q
ATT_ROWS = 64
BIAS_LEN = 1024
VMEM_LIMIT = 48 * 1024 * 1024
VMEM_LIMIT_ATT_BWD = 56 * 1024 * 1024


def _params(*sem, vmem=VMEM_LIMIT):
    return pltpu.CompilerParams(dimension_semantics=sem, vmem_limit_bytes=vmem)


def _dot(a, b):
    return lax.dot_general(a, b, (((1,), (0,)), ((), ())), preferred_element_type=F32)


def _dot_nt(a, b):
    return lax.dot_general(a, b, (((1,), (1,)), ((), ())), preferred_element_type=F32)


def _dot_tn(a, b):
    return lax.dot_general(a, b, (((0,), (0,)), ((), ())), preferred_element_type=F32)


def _sigmoid(x):
    return 1.0 / (1.0 + jnp.exp(-x))


def _rms_bwd(x, g, dy):
    r = lax.rsqrt(jnp.mean(x * x, axis=-1, keepdims=True) + EPS)
    u = dy * g
    dx = r * u - x * (r * r * r) * jnp.mean(u * x, axis=-1, keepdims=True)
    return dx, dy * x * r


def _place():
    return lax.axis_index("x"), lax.axis_index("y"), lax.axis_index("c")


def _peer(k):
    x, y, c = _place()
    return ((1 - x) if k & 4 else x, (1 - y) if k & 2 else y, (1 - c) if k & 1 else c)


def _index(place):
    return 4 * place[0] + 2 * place[1] + place[2]


def _rows(ref, block, nrows):
    align = 16 if ref.dtype == BF16 else 8
    return ref.at[pl.ds(pl.multiple_of(block * nrows, align), nrows)]


class _Scatter:
    def __init__(self, arrays):
        self.arrays, self.n = list(arrays), len(arrays)

    def out_shape(self):
        return [jax.ShapeDtypeStruct((N_DEV, a.shape[0] // N_DEV) + a.shape[1:], a.dtype) for a in self.arrays]

    def scratch(self):
        return [pltpu.SemaphoreType.DMA((self.n, N_DEV - 1)), pltpu.SemaphoreType.DMA((self.n, N_DEV - 1)),
                pltpu.SemaphoreType.DMA((self.n,))]

    def _copies(self, ins, outs, sems):
        send_sems, recv_sems, local_sems = sems
        me = _index(_place())

        def src(w, to):
            return _rows(ins[w], to, ins[w].shape[0] // N_DEV)

        def dst(w, origin):
            return outs[w].at[origin]

        def remote(w, k, to, origin):
            return pltpu.make_async_remote_copy(src_ref=src(w, to), dst_ref=dst(w, origin),
                                                send_sem=send_sems.at[w, k - 1], recv_sem=recv_sems.at[w, k - 1],
                                                device_id=_peer(k), device_id_type=MESH)

        pairs = [(w, k) for w in range(self.n) for k in range(1, N_DEV)]
        own = lambda: [pltpu.make_async_copy(src(w, me), dst(w, me), local_sems.at[w]) for w in range(self.n)]
        sent = lambda: [remote(w, k, _index(_peer(k)), me) for w, k in pairs]
        arriving = lambda: [remote(w, k, me, _index(_peer(k))) for w, k in pairs]
        return own, sent, arriving

    def start(self, ins, outs, sems):
        own, sent, _ = self._copies(ins, outs, sems)
        for cp in own() + sent():
            cp.start()

    def wait(self, ins, outs, sems):
        own, sent, arriving = self._copies(ins, outs, sems)
        for cp in arriving():
            cp.wait_recv()
        for cp in sent():
            cp.wait_send()
        for cp in own():
            cp.wait()


class _PairSwap:
    def __init__(self, arrays):
        self.arrays, self.n = list(arrays), len(arrays)

    def out_shape(self):
        return [jax.ShapeDtypeStruct((4, a.shape[0] // N_DEV) + a.shape[1:], a.dtype) for a in self.arrays]

    def scratch(self):
        return [pltpu.SemaphoreType.DMA((self.n, 4)), pltpu.SemaphoreType.DMA((self.n, 4))]

    def _copies(self, ins, outs, sems):
        send_sems, recv_sems = sems
        x, y, c = _place()
        return [pltpu.make_async_remote_copy(
            src_ref=_rows(ins[w], 2 * q + 1 - c, ins[w].shape[0] // N_DEV), dst_ref=outs[w].at[q],
            send_sem=send_sems.at[w, q], recv_sem=recv_sems.at[w, q],
            device_id=(x, y, 1 - c), device_id_type=MESH) for w in range(self.n) for q in range(4)]

    def start(self, ins, outs, sems):
        for cp in self._copies(ins, outs, sems):
            cp.start()

    def wait(self, ins, outs, sems):
        for cp in self._copies(ins, outs, sems):
            cp.wait()


class _ChipScatter:
    def __init__(self, arrays):
        self.arrays, self.n = list(arrays), len(arrays)

    def out_shape(self):
        return [jax.ShapeDtypeStruct(a.shape, a.dtype) for a in self.arrays]

    def scratch(self):
        return [pltpu.SemaphoreType.DMA((self.n, 3)), pltpu.SemaphoreType.DMA((self.n, 3)),
                pltpu.SemaphoreType.DMA((self.n,))]

    def _copies(self, ins, outs, sems):
        send_sems, recv_sems, local_sems = sems
        x, y, c = _place()
        mine = 2 * x + y
        sent, arriving = [], []
        for w in range(self.n):
            for k in range(1, 4):
                tx, ty = (1 - x) if k & 2 else x, (1 - y) if k & 1 else y
                other = 2 * tx + ty
                sent.append(lambda w=w, k=k, tx=tx, ty=ty, other=other: pltpu.make_async_remote_copy(
                    src_ref=ins[w].at[other], dst_ref=outs[w].at[mine], send_sem=send_sems.at[w, k - 1],
                    recv_sem=recv_sems.at[w, k - 1], device_id=(tx, ty, c), device_id_type=MESH))
                arriving.append(lambda w=w, k=k, tx=tx, ty=ty, other=other: pltpu.make_async_remote_copy(
                    src_ref=ins[w].at[mine], dst_ref=outs[w].at[other], send_sem=send_sems.at[w, k - 1],
                    recv_sem=recv_sems.at[w, k - 1], device_id=(tx, ty, c), device_id_type=MESH))
        own = [lambda w=w: pltpu.make_async_copy(ins[w].at[mine], outs[w].at[mine], local_sems.at[w])
               for w in range(self.n)]
        return own, sent, arriving

    def start(self, ins, outs, sems):
        own, sent, _ = self._copies(ins, outs, sems)
        for cp in own + sent:
            cp().start()

    def wait(self, ins, outs, sems):
        own, sent, arriving = self._copies(ins, outs, sems)
        for cp in arriving:
            cp().wait_recv()
        for cp in sent:
            cp().wait_send()
        for cp in own:
            cp().wait()


class _ChipGather:
    def __init__(self, arrays, parts=None, into=None):
        self.arrays, self.n, self.into = list(arrays), len(arrays), into
        self.parts = parts or [(0, 1)] * self.n

    def out_shape(self):
        return [jax.ShapeDtypeStruct((N_DEV * a.shape[0],) + a.shape[1:], a.dtype) for a in self.arrays]

    def scratch(self):
        return [pltpu.SemaphoreType.DMA((self.n, N_DEV - 1)), pltpu.SemaphoreType.DMA((self.n, N_DEV - 1)),
                pltpu.SemaphoreType.DMA((self.n,))]

    def _parts(self, ins, outs, sems):
        send_sems, recv_sems, local_sems = sems
        x, y, c = _place()
        me, sibling = (x, y, c), (x, y, 1 - c)
        chips = [(1 - x, y), (x, 1 - y), (1 - x, 1 - y)]

        def rows(w, place, whole):
            (index, count), r = self.parts[w], ins[w].shape[0]
            lo, size = (0, r) if whole else (index * (r // count), r // count)
            align = 16 if ins[w].dtype == BF16 else 8
            return outs[w].at[pl.ds(pl.multiple_of(_index(place) * r + lo, align), size)]

        def mine(w, whole):
            (index, count), r = self.parts[w], ins[w].shape[0]
            return ins[w] if whole or count == 1 else ins[w].at[pl.ds(index * (r // count), r // count)]

        def copy(w, k, block, to, own=False):
            whole = k == 0
            return pltpu.make_async_remote_copy(src_ref=mine(w, whole) if own else rows(w, block, whole),
                                                dst_ref=rows(w, block, whole),
                                                send_sem=send_sems.at[w, k], recv_sem=recv_sems.at[w, k],
                                                device_id=to, device_id_type=MESH)

        def local(w):
            return pltpu.make_async_copy(ins[w], rows(w, me, True), local_sems.at[w])

        return me, sibling, chips, c, copy, local, [index == 0 for index, _ in self.parts]

    def start(self, ins, outs, sems):
        me, sibling, chips, c, copy, local, places_own = self._parts(ins, outs, sems)
        for w in range(self.n):
            if places_own[w]:
                local(w).start()
                copy(w, 0, me, sibling, own=True).start()
            for j, chip in enumerate(chips):
                copy(w, 1 + j, me, (*chip, c), own=True).start()

    def wait(self, ins, outs, sems):
        me, sibling, chips, c, copy, local, places_own = self._parts(ins, outs, sems)
        for w in range(self.n):
            for j, chip in enumerate(chips):
                copy(w, 1 + j, (*chip, c), me).wait_recv()
                copy(w, 4 + j, (*chip, c), sibling).start()
        for w in range(self.n):
            if places_own[w]:
                copy(w, 0, sibling, me).wait_recv()
                copy(w, 0, me, sibling, own=True).wait_send()
                local(w).wait()
            for j, chip in enumerate(chips):
                copy(w, 4 + j, (*chip, 1 - c), me).wait_recv()
                copy(w, 1 + j, me, (*chip, c), own=True).wait_send()
                copy(w, 4 + j, (*chip, c), sibling).wait_send()


def _call(body, *, name, grid, in_specs, out_specs, out_shape, scratch=(), semantics, args, exchange=None,
          vmem=VMEM_LIMIT):
    if exchange is None:
        return pl.pallas_call(body, name=name, grid=grid, in_specs=in_specs, out_specs=out_specs, out_shape=out_shape,
                              scratch_shapes=list(scratch),
                              compiler_params=_params(*semantics, vmem=vmem))(*args), None
    n_in, n_out, n_scr, nx = len(in_specs), len(out_specs), len(scratch), exchange.n
    into = list(getattr(exchange, "into", None) or [])

    def full_body(*refs):
        ins, refs = refs[:n_in], refs[n_in:]
        x_in, refs = refs[:nx], refs[nx + len(into):]
        outs, refs = refs[:n_out], refs[n_out:]
        x_out, refs = refs[:nx], refs[nx:]
        scr, sems = refs[:n_scr], refs[n_scr:]
        first, last = True, True
        for axis, size in enumerate(grid):
            first = jnp.logical_and(first, pl.program_id(axis) == 0)
            last = jnp.logical_and(last, pl.program_id(axis) == size - 1)
        if grid:
            pl.when(first)(lambda: exchange.start(x_in, x_out, sems))
        else:
            exchange.start(x_in, x_out, sems)
        body(*ins, *outs, *scr)
        if grid:
            pl.when(last)(lambda: exchange.wait(x_in, x_out, sems))
        else:
            exchange.wait(x_in, x_out, sems)

    hbm = pl.BlockSpec(memory_space=pltpu.HBM)
    res = pl.pallas_call(
        full_body, name=name, grid=grid,
        in_specs=list(in_specs) + [hbm] * (nx + len(into)), out_specs=list(out_specs) + [hbm] * nx,
        out_shape=list(out_shape) + exchange.out_shape(),
        scratch_shapes=list(scratch) + exchange.scratch(),
        input_output_aliases={n_in + nx + w: n_out + w for w in range(len(into))},
        compiler_params=_params(*(["arbitrary"] * len(grid)), vmem=vmem),
    )(*args, *exchange.arrays, *into)
    return res[:n_out], res[n_out:]


def _alone(exchange, name):
    return _call(lambda: None, name=name, grid=(), in_specs=[], out_specs=[], out_shape=[], semantics=(),
                 args=(), exchange=exchange)[1]


def _mm(a, b, *, ta=False, tb=False, out_dtype, tm, tn, tk, name, exchange=None):
    m, k = (a.shape[1], a.shape[0]) if ta else a.shape
    n = b.shape[0] if tb else b.shape[1]
    assert k == (b.shape[1] if tb else b.shape[0])
    tm, tn, tk = min(tm, m), min(tn, n), min(tk, k)
    assert m % tm == 0 and n % tn == 0 and k % tk == 0, (name, m, n, k)
    nk = k // tk
    dims = (((0 if ta else 1,), (1 if tb else 0,)), ((), ()))

    def body(a_ref, b_ref, o_ref, *acc):
        prod = lax.dot_general(a_ref[...].astype(BF16), b_ref[...].astype(BF16), dims, preferred_element_type=F32)
        if nk == 1:
            o_ref[...] = prod.astype(o_ref.dtype)
            return
        acc_ref, kk = acc[0], pl.program_id(2)

        @pl.when(kk == 0)
        def _():
            acc_ref[...] = prod

        @pl.when((kk > 0) & (kk < nk - 1))
        def _():
            acc_ref[...] += prod

        @pl.when(kk == nk - 1)
        def _():
            o_ref[...] = (acc_ref[...] + prod).astype(o_ref.dtype)

    a_spec = (pl.BlockSpec((tk, tm), lambda i, j, kk: (kk, i)) if ta
              else pl.BlockSpec((tm, tk), lambda i, j, kk: (i, kk)))
    b_spec = (pl.BlockSpec((tn, tk), lambda i, j, kk: (j, kk)) if tb
              else pl.BlockSpec((tk, tn), lambda i, j, kk: (kk, j)))
    (out,), moved = _call(
        body, name=name, grid=(m // tm, n // tn, nk),
        in_specs=[a_spec, b_spec],
        out_specs=[pl.BlockSpec((tm, tn), lambda i, j, kk: (i, j))],
        out_shape=[pltpu.HBM((m, n), out_dtype)],
        scratch=[pltpu.VMEM((tm, tn), F32)] if nk > 1 else [],
        semantics=("parallel", "parallel", "arbitrary"), args=(a, b), exchange=exchange)
    return out if exchange is None else (out, moved)


def _mm_pieces(pieces, b, *, ta, out_dtype, tm, tn, tk, name, exchange=None):
    rows, n = pieces[0].shape[0], b.shape[1]
    step = tm if ta else tk
    assert all(p.shape[0] == rows and p.shape[1] % step == 0 for p in pieces), name
    edges = [int(e) for e in np.cumsum([0] + [p.shape[1] // step for p in pieces])]
    total = edges[-1] * step
    m, k = (total, rows) if ta else (rows, total)
    assert b.shape[0] == k and m % tm == 0 and n % tn == 0 and k % tk == 0, name
    nk, npieces = k // tk, len(pieces)
    dims = (((0 if ta else 1,), (0,)), ((), ()))
    b_resident = ta and n == tn

    def body(*refs):
        a_refs, (b_ref, o_ref, acc_ref) = refs[:npieces], refs[npieces:]
        kk = pl.program_id(2)
        pos = pl.program_id(0) if ta else kk

        @pl.when(kk == 0)
        def _():
            acc_ref[...] = jnp.zeros_like(acc_ref)

        def b_tile():
            return b_ref[pl.ds(pl.multiple_of(kk * tk, tk), tk), :] if b_resident else b_ref[...]

        for p, a_ref in enumerate(a_refs):
            @pl.when((pos >= edges[p]) & (pos < edges[p + 1]))
            def _(a_ref=a_ref):
                acc_ref[...] += lax.dot_general(a_ref[...], b_tile(), dims, preferred_element_type=F32)

        @pl.when(kk == nk - 1)
        def _():
            o_ref[...] = acc_ref[...].astype(o_ref.dtype)

    def a_spec(p):
        lo, last = edges[p], edges[p + 1] - edges[p] - 1
        if ta:
            def index(i, j, kk):
                inside = (i >= lo) & (i <= lo + last)
                return jnp.where(inside, kk, 0), jnp.clip(i - lo, 0, last)
            return pl.BlockSpec((tk, tm), index)
        return pl.BlockSpec((tm, tk), lambda i, j, kk: (i, jnp.clip(kk - lo, 0, last)))

    (out,), moved = _call(
        body, name=name, grid=(m // tm, n // tn, nk),
        in_specs=[a_spec(p) for p in range(npieces)]
                 + [pl.BlockSpec(b.shape, lambda i, j, kk: (0, 0)) if b_resident
                    else pl.BlockSpec((tk, tn), lambda i, j, kk: (kk, j))],
        out_specs=[pl.BlockSpec((tm, tn), lambda i, j, kk: (i, j))],
        out_shape=[pltpu.HBM((m, n), out_dtype)],
        scratch=[pltpu.VMEM((tm, tn), F32)],
        semantics=("parallel", "parallel", "arbitrary"), args=(*pieces, b), exchange=exchange)
    return out if exchange is None else (out, moved)


def _rms_fwd(x2, g, exchange):
    t = x2.shape[0]
    tm = min(512, t)

    def body(x_ref, g_ref, o_ref):
        x = x_ref[...]
        r = lax.rsqrt(jnp.mean(x * x, axis=-1, keepdims=True) + EPS)
        o_ref[...] = (x * r * g_ref[...]).astype(o_ref.dtype)

    return _call(
        body, name="rms_in_fwd", grid=(t // tm,),
        in_specs=[pl.BlockSpec((tm, D_MODEL), lambda i: (i, 0)), pl.BlockSpec((1, D_MODEL), lambda i: (0, 0))],
        out_specs=[pl.BlockSpec((tm, D_MODEL), lambda i: (i, 0))],
        out_shape=[jax.ShapeDtypeStruct((t, D_MODEL), BF16)],
        semantics=("parallel",), args=(x2, g), exchange=exchange)


def _decay(lg):
    row = lax.broadcasted_iota(jnp.int32, (RET_TILE, RET_TILE), 0)
    col = lax.broadcasted_iota(jnp.int32, (RET_TILE, RET_TILE), 1)
    within = jnp.exp(lg * jnp.abs(row - col).astype(F32))
    inside = jnp.where((col >> 6) <= (row >> 6), within, 0.0)
    pos = lax.broadcasted_iota(jnp.int32, (RET_TILE, 1), 0).astype(F32)
    q_dec = jnp.exp(lg * (pos + 1.0))
    k_dec = jnp.exp(lg * (RET_TILE - 1.0 - pos))
    tile_dec = jnp.exp(lg * float(RET_TILE))
    return inside, q_dec, k_dec, tile_dec


def _scaled(a_bf16, dec):
    return (a_bf16.astype(F32) * dec).astype(BF16)


def _ret_specs(seq):
    key = lambda base: pl.BlockSpec((seq, RET_KEY_DIM), lambda b, h: (b, base // RET_KEY_DIM + h))
    val = lambda base: pl.BlockSpec((seq, RET_VAL_DIM), lambda b, h: (b, base // RET_VAL_DIM + h))
    tab = pl.BlockSpec((seq, RET_KEY_DIM), lambda b, h: (0, 0))
    lgs = pl.BlockSpec((None, 1, LANES), lambda b, h: (h, 0, 0))
    return key, val, tab, lgs


def _ret_fwd(proj, cs, sn, lg_arr, batch, seq, exchange):
    t = batch * seq
    nt = seq // RET_TILE

    def body(q_ref, k_ref, v_ref, rg_ref, cs_ref, sn_ref, lg_ref, gro_ref, o_ref, qr_ref, kr_ref):
        lg = lg_ref[:, 0:1]
        cs_t, sn_t = cs_ref[...], sn_ref[...]
        q = q_ref[...].astype(F32)
        k = k_ref[...].astype(F32)
        qr_ref[...] = (q * cs_t + pltpu.roll(q, 64, 1) * sn_t).astype(BF16)
        kr_ref[...] = ((k * cs_t + pltpu.roll(k, 64, 1) * sn_t) * (RET_KEY_DIM ** -0.5)).astype(BF16)
        inside, q_dec, k_dec, tile_dec = _decay(lg)
        state = jnp.zeros((RET_KEY_DIM, RET_VAL_DIM), F32)
        for i in range(nt):
            rows = slice(i * RET_TILE, (i + 1) * RET_TILE)
            qi, ki, vi = qr_ref[rows, :], kr_ref[rows, :], v_ref[rows, :]
            acc = _dot((_dot_nt(qi, ki) * inside).astype(BF16), vi)
            if i > 0:
                acc = acc + _dot(_scaled(qi, q_dec), state.astype(BF16))
            if i < nt - 1:
                state = state * tile_dec + _dot_tn(_scaled(ki, k_dec), vi)
            o_ref[rows, :] = acc
            xc = acc - jnp.mean(acc, axis=-1, keepdims=True)
            nrm = xc * lax.rsqrt(jnp.mean(xc * xc, axis=-1, keepdims=True) + EPS)
            rg = rg_ref[rows, :].astype(F32)
            gro_ref[rows, :] = (rg * _sigmoid(rg) * nrm).astype(BF16)

    key, val, tab, lgs = _ret_specs(seq)
    return _call(
        body, name="ret_fwd", grid=(batch, RET_HEADS),
        in_specs=[key(C_RQ), key(C_RK), val(C_RV), val(C_RG), tab, tab, lgs],
        out_specs=[val(0), val(0), key(0), key(0)],
        out_shape=[jax.ShapeDtypeStruct((t, RET_HEADS * RET_VAL_DIM), BF16),
                   jax.ShapeDtypeStruct((t, RET_HEADS * RET_VAL_DIM), F32),
                   jax.ShapeDtypeStruct((t, RET_HEADS * RET_KEY_DIM), BF16),
                   jax.ShapeDtypeStruct((t, RET_HEADS * RET_KEY_DIM), BF16)],
        semantics=("parallel", "parallel"), args=(proj, proj, proj, proj, cs, sn, lg_arr), exchange=exchange)


def _att_bias(w_ref, bias_ref):
    n_i = lax.broadcasted_iota(jnp.int32, (ATT_Q, BIAS_LEN), 0)
    qc = lax.broadcasted_iota(jnp.int32, (ATT_Q, ATT_WIN), 0) >> 6
    kc = lax.broadcasted_iota(jnp.int32, (ATT_Q, ATT_WIN), 1) >> 6
    dc = qc + BAND_CHUNKS - kc
    band = (dc >= 0) & (dc <= BAND_CHUNKS)
    key = lax.broadcasted_iota(jnp.int32, (ATT_Q, ATT_WIN), 1)
    for e in range(2):
        xw = jnp.broadcast_to(w_ref[e:e + 1, :], (ATT_Q, BIAS_LEN))
        for bit in range(8):
            xw = jnp.where(((n_i >> bit) & 1) == 1, pltpu.roll(xw, 1 << bit, 1), xw)
        bias = jnp.where(band, xw[:, BIAS_LEN - ATT_WIN:], NEG_INF)
        for first in range(ATT_STARTS):
            bias_ref[first, e] = jnp.where(key + (first * ATT_Q - ATT_PAD) >= 0, bias, NEG_INF)
        bias_ref[ATT_STARTS, e] = bias


ATT_PAIRS = 2
ATT_COLS = ATT_PAIRS * LANES


def _att_specs(batch, seq):
    ni = seq // ATT_Q
    q_spec = pl.BlockSpec((ATT_Q, ATT_COLS), lambda g, b, i: (b * ni + i, C_AQ // ATT_COLS + g))
    k_spec = pl.BlockSpec((seq, ATT_COLS), lambda g, b, i: (b, C_AK // ATT_COLS + g))
    v_spec = pl.BlockSpec((seq, ATT_COLS), lambda g, b, i: (b, C_AV // ATT_COLS + g))
    w_spec = pl.BlockSpec((ATT_PAIRS, 2, BIAS_LEN), lambda g, b, i: (g, 0, 0))
    b_spec = pl.BlockSpec((ATT_PAIRS, ATT_STARTS + 1, 2, ATT_Q, ATT_WIN), lambda g, b, i: (g, 0, 0, 0, 0))
    pad = pltpu.VMEM((seq + ATT_PAD, ATT_COLS), BF16)
    return ni, q_spec, k_spec, v_spec, w_spec, b_spec, pad


def _att_bias_tiles(wvec, exchange):
    (tiles,), moved = _call(
        lambda w_ref, o_ref: _att_bias(w_ref, o_ref), name="att_bias", grid=(ATT_HEADS // 2,),
        in_specs=[pl.BlockSpec((None, 2, BIAS_LEN), lambda hp: (hp, 0, 0))],
        out_specs=[pl.BlockSpec((None, ATT_STARTS + 1, 2, ATT_Q, ATT_WIN), lambda hp: (hp, 0, 0, 0, 0))],
        out_shape=[jax.ShapeDtypeStruct((ATT_HEADS // 2, ATT_STARTS + 1, 2, ATT_Q, ATT_WIN), F32)],
        semantics=("parallel",), args=(wvec,), exchange=exchange)
    return tiles, moved


def _att_pad(src_ref, pad_ref):
    pad_ref[:ATT_PAD, :] = jnp.zeros((ATT_PAD, ATT_COLS), BF16)
    pad_ref[ATT_PAD:, :] = src_ref[...]


def _att_head(q2, sel):
    return jnp.where(sel, q2, jnp.zeros_like(q2)) * 0.125


def _att_softmax_rows(s_ref, bias_ref, rows):
    s = s_ref[rows, :] + bias_ref[rows, :]
    ex = jnp.exp(s - jnp.max(s, axis=-1, keepdims=True))
    return ex, 1.0 / jnp.sum(ex, axis=-1, keepdims=True)


def _att_fwd(proj, bias, batch, seq, exchange):
    ni, q_spec, k_spec, v_spec, _, b_spec, pad = _att_specs(batch, seq)

    def body(q_ref, k_ref, v_ref, bias_ref, o_ref, kp_ref, vp_ref, s_ref, e_ref):
        i = pl.program_id(2)

        @pl.when(i == 0)
        def _():
            _att_pad(k_ref, kp_ref)
            _att_pad(v_ref, vp_ref)

        win = pl.ds(pl.multiple_of(i * ATT_Q, ATT_Q), ATT_WIN)
        lo = lax.broadcasted_iota(jnp.int32, (1, LANES), 1) < 64
        start = jnp.minimum(i, ATT_STARTS)
        for pair in range(ATT_PAIRS):
            cols = slice(pair * LANES, (pair + 1) * LANES)
            k2, v2, q2 = kp_ref[win, cols], vp_ref[win, cols], q_ref[:, cols]
            out = jnp.zeros((ATT_Q, LANES), F32)
            for e in range(2):
                h = 2 * pair + e
                sel = lo if e == 0 else jnp.logical_not(lo)
                s_ref[h] = _dot_nt(_att_head(q2, sel), k2)
                rsum = []
                for c in range(ATT_Q // ATT_ROWS):
                    rows = slice(c * ATT_ROWS, (c + 1) * ATT_ROWS)
                    ex, r = _att_softmax_rows(s_ref.at[h], bias_ref.at[pair, start, e], rows)
                    e_ref[h, rows, :] = ex.astype(BF16)
                    rsum.append(r)
                out = out + _dot(e_ref[h], jnp.where(sel, v2, jnp.zeros_like(v2))) * jnp.concatenate(rsum, axis=0)
            o_ref[:, cols] = out.astype(BF16)

    heads = 2 * ATT_PAIRS
    return _call(
        body, name="att_fwd", grid=(ATT_HEADS // heads, batch, ni),
        in_specs=[q_spec, k_spec, v_spec, b_spec],
        out_specs=[pl.BlockSpec((ATT_Q, ATT_COLS), lambda g, b, i: (b * ni + i, g))],
        out_shape=[jax.ShapeDtypeStruct((batch * seq, ATT_HEADS * 64), BF16)],
        scratch=[pad, pad, pltpu.VMEM((heads, ATT_Q, ATT_WIN), F32), pltpu.VMEM((heads, ATT_Q, ATT_WIN), BF16)],
        semantics=("arbitrary", "arbitrary", "arbitrary"), args=(proj, proj, proj, bias), exchange=exchange)


GL_HALF = 512


def _gl_specs(tm):
    return [pl.BlockSpec((tm, GL_HALF), lambda i, c=C_GL // GL_HALF + j: (i, c)) for j in range(4)]


def _gates(gl_refs, b_ref):
    logits = [ref[...].astype(F32) for ref in gl_refs]
    gr = _sigmoid(jnp.concatenate(logits[:2], axis=1) + b_ref[:, :D_MODEL])
    ga = _sigmoid(jnp.concatenate(logits[2:], axis=1) + b_ref[:, D_MODEL:])
    return gr, ga


def _whole(a):
    return pl.BlockSpec(a.shape, lambda i: (0,) * a.ndim)


def _mix_out_fwd(gro, ao, proj, b_gate, w_ret, w_att_t, x2, w_out, g2):
    t = gro.shape[0]
    tm = min(256, t)

    def body(gro_ref, ao_ref, gl0, gl1, gl2, gl3, b_ref, wr_ref, wa_ref, x_ref, wo_ref, g_ref,
             z_ref, yr_ref, ya_ref, h_ref, hn_ref):
        yr = _dot(gro_ref[...], wr_ref[...])
        ya = _dot_nt(ao_ref[...], wa_ref[...])
        yr_ref[...] = yr.astype(BF16)
        ya_ref[...] = ya.astype(BF16)
        gr, ga = _gates((gl0, gl1, gl2, gl3), b_ref)
        z = (gr * yr + ga * ya).astype(BF16)
        z_ref[...] = z
        h = x_ref[...] + _dot(z, wo_ref[...])
        h_ref[...] = h
        r = lax.rsqrt(jnp.mean(h * h, axis=-1, keepdims=True) + EPS)
        hn_ref[...] = (h * r * g_ref[...]).astype(BF16)

    row = pl.BlockSpec((tm, D_MODEL), lambda i: (i, 0))
    return pl.pallas_call(
        body, name="mix_out_fwd", grid=(t // tm,),
        in_specs=[row, pl.BlockSpec((tm, 512), lambda i: (i, 0)), *_gl_specs(tm),
                  _whole(b_gate), _whole(w_ret), _whole(w_att_t), row, _whole(w_out), _whole(g2)],
        out_specs=[row] * 5,
        out_shape=[jax.ShapeDtypeStruct((t, D_MODEL), BF16)] * 3
                  + [jax.ShapeDtypeStruct((t, D_MODEL), F32), jax.ShapeDtypeStruct((t, D_MODEL), BF16)],
        compiler_params=_params("parallel"),
    )(gro, ao, proj, proj, proj, proj, b_gate, w_ret, w_att_t, x2, w_out, g2)


def _col_chunks(width, chunk=384):
    return [slice(lo, min(lo + chunk, width)) for lo in range(0, width, chunk)]


def _ffn_up(hn, wg_t, wu_t):
    t = hn.shape[0]
    tm, tn = min(512, t), D_FF // 2

    def body(h_ref, wg_ref, wu_ref, g_ref, u_ref, a_ref):
        g = _dot_nt(h_ref[...], wg_ref[...])
        u = _dot_nt(h_ref[...], wu_ref[...])
        g_ref[...] = g.astype(BF16)
        u_ref[...] = u.astype(BF16)
        a_ref[...] = (g * _sigmoid(g) * u).astype(BF16)

    w_spec = pl.BlockSpec((tn, D_MODEL), lambda j, i: (j, 0))
    out = pl.BlockSpec((tm, tn), lambda j, i: (i, j))
    return pl.pallas_call(
        body, name="ffn_up", grid=(D_FF // tn, t // tm),
        in_specs=[pl.BlockSpec((tm, D_MODEL), lambda j, i: (i, 0)), w_spec, w_spec],
        out_specs=[out, out, out],
        out_shape=[jax.ShapeDtypeStruct((t, D_FF), BF16)] * 3,
        compiler_params=_params("parallel", "parallel"),
    )(hn, wg_t, wu_t)


def _ffn_down_loss(a, h1, tgt, w_down, g3):
    t = a.shape[0]
    tm = min(512, t)

    def body(a_ref, h_ref, t_ref, w_ref, g_ref, dh_ref, dhb_ref, loss_ref, dg_ref):
        @pl.when(pl.program_id(0) == 0)
        def _():
            loss_ref[...] = jnp.zeros_like(loss_ref)
            dg_ref[...] = jnp.zeros_like(dg_ref)

        g = g_ref[...]
        h2 = h_ref[...] + _dot(a_ref[...], w_ref[...])
        r = lax.rsqrt(jnp.mean(h2 * h2, axis=-1, keepdims=True) + EPS)
        err = h2 * r * g - t_ref[...]
        loss_ref[...] += jnp.sum(err * err) * (0.5 / D_MODEL)
        dy = err * (1.0 / D_MODEL)
        dh, dg_rows = _rms_bwd(h2, g, dy)
        dg_ref[...] += jnp.sum(dg_rows, axis=0, keepdims=True)
        dh_ref[...] = dh
        dhb_ref[...] = dh.astype(BF16)

    row = pl.BlockSpec((tm, D_MODEL), lambda i: (i, 0))
    vec = pl.BlockSpec((1, D_MODEL), lambda i: (0, 0))
    return pl.pallas_call(
        body, name="ffn_down_loss", grid=(t // tm,),
        in_specs=[pl.BlockSpec((tm, D_FF), lambda i: (i, 0)), row, row,
                  pl.BlockSpec((D_FF, D_MODEL), lambda i: (0, 0)), vec],
        out_specs=[row, row, pl.BlockSpec((1, LANES), lambda i: (0, 0)), vec],
        out_shape=[jax.ShapeDtypeStruct((t, D_MODEL), F32), jax.ShapeDtypeStruct((t, D_MODEL), BF16),
                   jax.ShapeDtypeStruct((1, LANES), F32), jax.ShapeDtypeStruct((1, D_MODEL), F32)],
        compiler_params=_params("arbitrary"),
    )(a, h1, tgt, w_down, g3)


def _ffn_bwd_act(dh2b, w_down, g_act, u_act, exchange):
    t = dh2b.shape[0]
    tm, tn = min(512, t), D_FF // 2

    def body(d_ref, w_ref, g_ref, u_ref, dg_ref, du_ref):
        d = d_ref[...]
        for cols in _col_chunks(tn):
            da = _dot_nt(d, w_ref[cols, :])
            g = g_ref[:, cols].astype(F32)
            u = u_ref[:, cols].astype(F32)
            sg = _sigmoid(g)
            dg_ref[:, cols] = (da * u * sg * (1.0 + g * (1.0 - sg))).astype(BF16)
            du_ref[:, cols] = (da * g * sg).astype(BF16)

    blk = pl.BlockSpec((tm, tn), lambda j, i: (i, j))
    return _call(
        body, name="ffn_bwd_act", grid=(D_FF // tn, t // tm),
        in_specs=[pl.BlockSpec((tm, D_MODEL), lambda j, i: (i, 0)),
                  pl.BlockSpec((tn, D_MODEL), lambda j, i: (j, 0)), blk, blk],
        out_specs=[blk, blk],
        out_shape=[jax.ShapeDtypeStruct((t, D_FF), BF16)] * 2,
        semantics=("parallel", "parallel"), args=(dh2b, w_down, g_act, u_act), exchange=exchange)


def _ffn_bwd_in(dg, du, wg_t, wu_t, h1, dh2, g2, exchange):
    t = dg.shape[0]
    tm = min(256, t)

    def body(dg_ref, du_ref, wg_ref, wu_ref, h_ref, d2_ref, g_ref, dh_ref, dhb_ref, gn_ref):
        @pl.when(pl.program_id(0) == 0)
        def _():
            gn_ref[...] = jnp.zeros_like(gn_ref)

        dhn = _dot(dg_ref[...], wg_ref[...]) + _dot(du_ref[...], wu_ref[...])
        dx, dg_rows = _rms_bwd(h_ref[...], g_ref[...], dhn)
        dh = d2_ref[...] + dx
        dh_ref[...] = dh
        dhb_ref[...] = dh.astype(BF16)
        gn_ref[...] += jnp.sum(dg_rows, axis=0, keepdims=True)

    act = pl.BlockSpec((tm, D_FF), lambda i: (i, 0))
    row = pl.BlockSpec((tm, D_MODEL), lambda i: (i, 0))
    return _call(
        body, name="ffn_bwd_in", grid=(t // tm,),
        in_specs=[act, act, _whole(wg_t), _whole(wu_t), row, row, _whole(g2)],
        out_specs=[row, row, _whole(g2)],
        out_shape=[jax.ShapeDtypeStruct((t, D_MODEL), F32), jax.ShapeDtypeStruct((t, D_MODEL), BF16),
                   jax.ShapeDtypeStruct((1, D_MODEL), F32)],
        semantics=("arbitrary",), args=(dg, du, wg_t, wu_t, h1, dh2, g2), exchange=exchange)


def _mix_bwd(dh1b, w_out, proj, b_gate, y_ret, y_att, w_ret, w_att_t):
    t = dh1b.shape[0]
    tm = min(256, t)

    def body(d_ref, wo_ref, gl0, gl1, gl2, gl3, b_ref, yr_ref, ya_ref, wr_ref, wa_ref,
             dyr_ref, dya_ref, dglr_ref, dgla_ref, dgro_ref, dao_ref, db_ref):
        @pl.when(pl.program_id(0) == 0)
        def _():
            db_ref[...] = jnp.zeros_like(db_ref)

        dz = _dot_nt(d_ref[...], wo_ref[...])
        gr, ga = _gates((gl0, gl1, gl2, gl3), b_ref)
        dyr = (dz * gr).astype(BF16)
        dya = (dz * ga).astype(BF16)
        dyr_ref[...] = dyr
        dya_ref[...] = dya
        dglr = dz * yr_ref[...].astype(F32) * gr * (1.0 - gr)
        dgla = dz * ya_ref[...].astype(F32) * ga * (1.0 - ga)
        dglr_ref[...] = dglr.astype(BF16)
        dgla_ref[...] = dgla.astype(BF16)
        db_ref[:, :D_MODEL] += jnp.sum(dglr, axis=0, keepdims=True)
        db_ref[:, D_MODEL:] += jnp.sum(dgla, axis=0, keepdims=True)
        dgro_ref[...] = _dot_nt(dyr, wr_ref[...]).astype(BF16)
        dao_ref[...] = _dot(dya, wa_ref[...]).astype(BF16)

    row = pl.BlockSpec((tm, D_MODEL), lambda i: (i, 0))
    half = pl.BlockSpec((tm, 512), lambda i: (i, 0))
    return pl.pallas_call(
        body, name="mix_bwd", grid=(t // tm,),
        in_specs=[row, _whole(w_out), *_gl_specs(tm), _whole(b_gate), row, row, _whole(w_ret), _whole(w_att_t)],
        out_specs=[row, row, row, row, row, half, _whole(b_gate)],
        out_shape=[jax.ShapeDtypeStruct((t, D_MODEL), BF16)] * 5
                  + [jax.ShapeDtypeStruct((t, 512), BF16), jax.ShapeDtypeStruct(b_gate.shape, F32)],
        compiler_params=_params("arbitrary"),
    )(dh1b, w_out, proj, proj, proj, proj, b_gate, y_ret, y_att, w_ret, w_att_t)


def _ret_bwd(dgro, proj, o_ret, qr, kr, cs, sn, lg_arr, batch, seq, exchange):
    t = batch * seq
    nt = seq // RET_TILE

    def body(dgro_ref, rg_ref, o_ref, qr_ref, kr_ref, v_ref, cs_ref, sn_ref, lg_ref,
             dq_ref, dk_ref, dv_ref, drg_ref, do_ref, st_ref):
        lg = lg_ref[:, 0:1]
        inside, q_dec, k_dec, tile_dec = _decay(lg)

        state = jnp.zeros((RET_KEY_DIM, RET_VAL_DIM), F32)
        for i in range(nt - 1):
            rows = slice(i * RET_TILE, (i + 1) * RET_TILE)
            state = state * tile_dec + _dot_tn(_scaled(kr_ref[rows, :], k_dec), v_ref[rows, :])
            st_ref[i + 1] = state.astype(BF16)

        for i in range(nt):
            rows = slice(i * RET_TILE, (i + 1) * RET_TILE)
            o = o_ref[rows, :]
            xc = o - jnp.mean(o, axis=-1, keepdims=True)
            rs = lax.rsqrt(jnp.mean(xc * xc, axis=-1, keepdims=True) + EPS)
            nrm = xc * rs
            rg = rg_ref[rows, :].astype(F32)
            sg = _sigmoid(rg)
            dg = dgro_ref[rows, :].astype(F32)
            drg_ref[rows, :] = (dg * nrm * sg * (1.0 + rg * (1.0 - sg))).astype(BF16)
            dn = dg * rg * sg
            do = rs * (dn - jnp.mean(dn, axis=-1, keepdims=True)
                       - nrm * jnp.mean(dn * nrm, axis=-1, keepdims=True))
            do_ref[rows, :] = do.astype(BF16)

        dstate = jnp.zeros((RET_KEY_DIM, RET_VAL_DIM), F32)
        for i in reversed(range(nt)):
            rows = slice(i * RET_TILE, (i + 1) * RET_TILE)
            qi, ki, vi, doi = qr_ref[rows, :], kr_ref[rows, :], v_ref[rows, :], do_ref[rows, :]
            p = (_dot_nt(qi, ki) * inside).astype(BF16)
            dp = (_dot_nt(doi, vi) * inside).astype(BF16)
            dq = _dot(dp, ki)
            dk = _dot_tn(dp, qi)
            dv = _dot_tn(p, doi)
            if i > 0:
                dq = dq + _dot_nt(doi, st_ref[i]) * q_dec
            if i < nt - 1:
                dsb = dstate.astype(BF16)
                dk = dk + _dot_nt(vi, dsb) * k_dec
                dv = dv + _dot(_scaled(ki, k_dec), dsb)
            if i > 0:
                dstate = dstate * tile_dec + _dot_tn(_scaled(qi, q_dec), doi)
            dq_ref[rows, :] = (dq * cs_ref[rows, :] - pltpu.roll(dq, 64, 1) * sn_ref[rows, :]).astype(BF16)
            dk = (dk * cs_ref[rows, :] - pltpu.roll(dk, 64, 1) * sn_ref[rows, :]) * (RET_KEY_DIM ** -0.5)
            dk_ref[rows, :] = dk.astype(BF16)
            dv_ref[rows, :] = dv.astype(BF16)

    key, val, tab, lgs = _ret_specs(seq)
    return _call(
        body, name="ret_bwd", grid=(batch, RET_HEADS),
        in_specs=[val(0), val(C_RG), val(0), key(0), key(0), val(C_RV), tab, tab, lgs],
        out_specs=[key(0), key(0), val(0), val(0)],
        out_shape=[jax.ShapeDtypeStruct((t, RET_HEADS * RET_KEY_DIM), BF16)] * 2
                  + [jax.ShapeDtypeStruct((t, RET_HEADS * RET_VAL_DIM), BF16)] * 2,
        scratch=[pltpu.VMEM((seq, RET_VAL_DIM), BF16), pltpu.VMEM((nt, RET_KEY_DIM, RET_VAL_DIM), BF16)],
        semantics=("parallel", "parallel"), args=(dgro, proj, o_ret, qr, kr, proj, cs, sn, lg_arr),
        exchange=exchange)


def _att_bwd(proj, bias, dao, batch, seq, exchange):
    ni, q_spec, k_spec, v_spec, w_spec, b_spec, pad = _att_specs(batch, seq)
    t = batch * seq

    def body(q_ref, k_ref, v_ref, bias_ref, do_ref, dq_ref, dk_ref, dv_ref, dw_ref,
             dbias_ref, dk_acc, dv_acc, kp_ref, vp_ref, s_ref, dp_ref, e_ref, ds_ref):
        b, i = pl.program_id(1), pl.program_id(2)

        @pl.when((b == 0) & (i == 0))
        def _():
            dbias_ref[...] = jnp.zeros_like(dbias_ref)

        @pl.when(i == 0)
        def _():
            _att_pad(k_ref, kp_ref)
            _att_pad(v_ref, vp_ref)
            dk_acc[...] = jnp.zeros_like(dk_acc)
            dv_acc[...] = jnp.zeros_like(dv_acc)

        win = pl.ds(pl.multiple_of(i * ATT_Q, ATT_Q), ATT_WIN)
        lo = lax.broadcasted_iota(jnp.int32, (1, LANES), 1) < 64
        start = jnp.minimum(i, ATT_STARTS)
        for pair in range(ATT_PAIRS):
            cols = slice(pair * LANES, (pair + 1) * LANES)
            k2, v2, q2, do2 = kp_ref[win, cols], vp_ref[win, cols], q_ref[:, cols], do_ref[:, cols]
            dq = jnp.zeros((ATT_Q, LANES), F32)
            dk = jnp.zeros((LANES, ATT_WIN), F32)
            dv = jnp.zeros((LANES, ATT_WIN), F32)
            for e in range(2):
                h = 2 * pair + e
                sel = lo if e == 0 else jnp.logical_not(lo)
                qm = _att_head(q2, sel)
                dom = jnp.where(sel, do2, jnp.zeros_like(do2))
                s_ref[h] = _dot_nt(qm, k2)
                dp_ref[h] = _dot_nt(dom, v2)
                rsum = []
                for c in range(ATT_Q // ATT_ROWS):
                    rows = slice(c * ATT_ROWS, (c + 1) * ATT_ROWS)
                    ex, r = _att_softmax_rows(s_ref.at[h], bias_ref.at[pair, start, e], rows)
                    dp = dp_ref[h, rows, :]
                    mean = jnp.sum(dp * ex, axis=-1, keepdims=True) * r
                    ds = ex * ((dp - mean) * r)
                    dbias_ref[h, rows, :] += ds
                    ds_ref[h, rows, :] = ds.astype(BF16)
                    e_ref[h, rows, :] = ex.astype(BF16)
                    rsum.append(r)
                dq = dq + _dot(ds_ref[h], jnp.where(sel, k2, jnp.zeros_like(k2)))
                dk = dk + _dot_tn(qm, ds_ref[h])
                dv = dv + _dot_tn((dom.astype(F32) * jnp.concatenate(rsum, axis=0)).astype(BF16), e_ref[h])
            dq_ref[:, cols] = (dq * 0.125).astype(BF16)
            dk_acc[cols, win] += dk
            dv_acc[cols, win] += dv

        @pl.when(i == ni - 1)
        def _():
            dk_ref[...] = dk_acc[:, ATT_PAD:].T.astype(BF16)
            dv_ref[...] = dv_acc[:, ATT_PAD:].T.astype(BF16)

        @pl.when((b == batch - 1) & (i == ni - 1))
        def _():
            n_i = lax.broadcasted_iota(jnp.int32, (ATT_Q, BIAS_LEN), 0)
            for h in range(heads):
                xw = jnp.concatenate([jnp.zeros((ATT_Q, BIAS_LEN - ATT_WIN), F32), dbias_ref[h]], axis=1)
                for bit in range(8):
                    xw = jnp.where(((n_i >> bit) & 1) == 1, pltpu.roll(xw, BIAS_LEN - (1 << bit), 1), xw)
                dw_ref[h // 2, h % 2:h % 2 + 1, :] = jnp.sum(xw, axis=0, keepdims=True)

    heads = 2 * ATT_PAIRS
    seq_blk = pl.BlockSpec((seq, ATT_COLS), lambda g, b, i: (b, g))
    q_out = pl.BlockSpec((ATT_Q, ATT_COLS), lambda g, b, i: (b * ni + i, g))
    tile_f32, tile_bf16 = pltpu.VMEM((heads, ATT_Q, ATT_WIN), F32), pltpu.VMEM((heads, ATT_Q, ATT_WIN), BF16)
    acc = pltpu.VMEM((ATT_COLS, seq + ATT_PAD), F32)
    return _call(
        body, name="att_bwd", grid=(ATT_HEADS // heads, batch, ni),
        in_specs=[q_spec, k_spec, v_spec, b_spec, q_out],
        out_specs=[q_out, seq_blk, seq_blk, w_spec],
        out_shape=[jax.ShapeDtypeStruct((t, 512), BF16)] * 3
                  + [jax.ShapeDtypeStruct((ATT_HEADS // 2, 2, BIAS_LEN), F32)],
        scratch=[tile_f32, acc, acc, pad, pad, tile_f32, tile_f32, tile_bf16, tile_bf16],
        semantics=("arbitrary", "arbitrary", "arbitrary"), args=(proj, proj, proj, bias, dao), exchange=exchange,
        vmem=VMEM_LIMIT_ATT_BWD)


def _rms_in_bwd(x2, dxn, dh1, g1):
    t = x2.shape[0]
    tm = min(512, t)

    def body(x_ref, d_ref, h_ref, g_ref, dx_ref, dg_ref):
        @pl.when(pl.program_id(0) == 0)
        def _():
            dg_ref[...] = jnp.zeros_like(dg_ref)

        dx, dg_rows = _rms_bwd(x_ref[...], g_ref[...], d_ref[...])
        dx_ref[...] = h_ref[...] + dx
        dg_ref[...] += jnp.sum(dg_rows, axis=0, keepdims=True)

    row = pl.BlockSpec((tm, D_MODEL), lambda i: (i, 0))
    vec = pl.BlockSpec((1, D_MODEL), lambda i: (0, 0))
    return pl.pallas_call(
        body, name="rms_in_bwd", grid=(t // tm,),
        in_specs=[row, row, row, vec], out_specs=[row, vec],
        out_shape=[jax.ShapeDtypeStruct((t, D_MODEL), F32), jax.ShapeDtypeStruct((1, D_MODEL), F32)],
        compiler_params=_params("arbitrary"),
    )(x2, dxn, dh1, g1)


def _pack_small(dg1, dbr, dba, dg2, dg3, dw, loss):
    def body(a_ref, b_ref, c_ref, d_ref, e_ref, w_ref, l_ref, o_ref):
        o_ref[...] = jnp.zeros_like(o_ref)
        for r, ref in enumerate((a_ref, b_ref, c_ref, d_ref, e_ref)):
            o_ref[r:r + 1, :] = ref[...]
        o_ref[5:6, 0:LANES] = l_ref[...]
        for hp in range(ATT_HEADS // 2):
            o_ref[8 + 2 * hp:10 + 2 * hp, :] = w_ref[hp]

    return pl.pallas_call(body, name="pack_small",
                          out_shape=jax.ShapeDtypeStruct((16, D_MODEL), F32))(dg1, dbr, dba, dg2, dg3, dw, loss)


def _rotary_tables(seq):
    freqs = ROPE_BASE ** (-jnp.arange(0, RET_KEY_DIM, 2, dtype=F32) / RET_KEY_DIM)
    ang = jnp.arange(seq, dtype=F32)[:, None] * freqs[None, :]
    cos, sin = jnp.cos(ang), jnp.sin(ang)
    return jnp.concatenate([cos, cos], axis=1), jnp.concatenate([-sin, sin], axis=1)


def _bias_rows(rel_bias):
    n_far = BIAS_LEN - ATT_Q - MAX_REL + 1
    n_near = BIAS_LEN - n_far - (N_REL - 2)
    w = jnp.concatenate([jnp.broadcast_to(rel_bias[:, N_REL - 1:], (ATT_HEADS, n_far)),
                         rel_bias[:, 1:N_REL - 1][:, ::-1],
                         jnp.broadcast_to(rel_bias[:, :1], (ATT_HEADS, n_near))], axis=1)
    return w.reshape(ATT_HEADS // 2, 2, BIAS_LEN)


def _bias_rows_bwd(dw):
    n_far = BIAS_LEN - ATT_Q - MAX_REL + 1
    mid = dw[:, n_far:n_far + N_REL - 2][:, ::-1]
    return jnp.concatenate([jnp.sum(dw[:, n_far + N_REL - 2:], axis=1, keepdims=True), mid,
                            jnp.sum(dw[:, :n_far], axis=1, keepdims=True)], axis=1)


def _step(x, tgt, norm_mix, b_gate, norm_ffn, norm_final, rel_bias_shard, shard):
    batch, seq, _ = x.shape
    t = batch * seq
    n_rb = rel_bias_shard.shape[-1]
    x2, tgt2 = x.reshape(t, D_MODEL), tgt.reshape(t, D_MODEL)
    g3 = norm_final.reshape(1, D_MODEL)
    cs, sn = _rotary_tables(seq)
    lg = np.log(1.0 - 2.0 ** (-5.0 - np.arange(RET_HEADS, dtype=np.float32))).astype(np.float32)
    lg_arr = jnp.asarray(np.broadcast_to(lg[:, None, None], (RET_HEADS, 1, LANES)))

    def gather(*names):
        return _ChipGather([shard[nm] for nm in names])

    def scatter(*grads):
        return _Scatter(grads)

    rb_pad = jnp.pad(rel_bias_shard, ((0, 0), (0, LANES - n_rb)))
    (xn,), (w_in_half, rb_full) = _rms_fwd(x2, norm_mix,
                                           _ChipGather([shard["w_in_t"], rb_pad], parts=[(0, 2), (0, 1)]))
    rb_full = rb_full.reshape(N_DEV, ATT_HEADS, LANES)[:, :, :n_rb]
    bias, (w_in_t,) = _att_bias_tiles(
        _bias_rows(jnp.transpose(rb_full, (1, 0, 2)).reshape(ATT_HEADS, N_DEV * n_rb)),
        _ChipGather([shard["w_in_t"]], parts=[(1, 2)], into=[w_in_half]))
    proj, (w_ret, w_att_t, w_out, w_gate_t) = _mm(
        xn, w_in_t, tb=True, out_dtype=BF16, tm=1024, tn=1664, tk=1024, name="proj",
        exchange=gather("w_ret", "w_att_t", "w_out", "w_gate_t"))
    (gro, o_ret, qr, kr), _ = _ret_fwd(proj, cs, sn, lg_arr, batch, seq, None)
    (ao,), (w_up_t, w_down) = _att_fwd(proj, bias, batch, seq, gather("w_up_t", "w_down"))
    z, y_ret, y_att, h1, hn = _mix_out_fwd(gro, ao, proj, b_gate, w_ret, w_att_t, x2, w_out, norm_ffn)
    g_act, u_act, a_act = _ffn_up(hn, w_gate_t, w_up_t)
    dh2, dh2b, loss, dg3 = _ffn_down_loss(a_act, h1, tgt2, w_down, g3)

    wg = dict(out_dtype=BF16, tn=1024, ta=True)
    slots = {}
    dw_down = _mm(a_act, dh2b, tm=1408, tk=2048, name="dw_down", **wg)
    (d_gact, d_uact), _ = _ffn_bwd_act(dh2b, w_down, g_act, u_act, None)
    dw_gate = _mm(d_gact, hn, tm=1408, tk=2048, name="dw_gate", **wg)
    dw_up = _mm(d_uact, hn, tm=1408, tk=2048, name="dw_up", **wg)
    (dh1, dh1b, dg2), (slots["w_down"],) = _ffn_bwd_in(d_gact, d_uact, w_gate_t, w_up_t, h1, dh2, norm_ffn,
                                                     scatter(dw_down))
    dw_out = _mm(z, dh1b, tm=1024, tk=2048, name="dw_out", **wg)
    dyr, dya, dglr, dgla, dgro, dao, db = _mix_bwd(dh1b, w_out, proj, b_gate, y_ret, y_att, w_ret, w_att_t)
    dw_ret = _mm(gro, dyr, tm=1024, tk=2048, name="dw_ret", **wg)
    dw_att = _mm(dya, ao, tm=1024, tk=2048, name="dw_att", **wg)
    (drq, drk, drv, drg), _ = _ret_bwd(dgro, proj, o_ret, qr, kr, cs, sn, lg_arr, batch, seq, None)
    (daq, dak, dav, dw), (slots["w_gate_t"], slots["w_out"], slots["w_ret"], slots["w_att_t"]) = _att_bwd(
        proj, bias, dao, batch, seq, scatter(dw_gate, dw_out, dw_ret, dw_att))
    dproj = [drq, drk, drv, drg, daq, dak, dav, dglr, dgla]
    dw_in, (slots["w_up_t"],) = _mm_pieces(dproj, xn, ta=True, out_dtype=BF16, tm=512, tn=1024, tk=1024,
                                           name="dw_in", exchange=scatter(dw_up))
    (dw_in_sibling,) = _alone(_PairSwap([dw_in]), "swap_w_in")
    dw_in_pairs = _pair_add(dw_in, dw_in_sibling, "pair_w_in")
    dxn, (slots["w_in_t"],) = _mm_pieces(dproj, w_in_t, ta=False, out_dtype=F32, tm=1024, tn=1024, tk=512, name="dxn",
                                  exchange=_ChipScatter([dw_in_pairs]))
    dx, dg1 = _rms_in_bwd(x2, dxn, dh1, norm_mix)
    small = _pack_small(dg1, db[:, :D_MODEL], db[:, D_MODEL:], dg2, dg3, dw, loss)
    (small_slots,) = _alone(_ChipGather([small]), "gather_small")
    return dx.reshape(batch, seq, D_MODEL), slots, small_slots.reshape(N_DEV, 16, D_MODEL)


def _row_tile(r, c):
    return max(d for d in range(16, r + 1, 16) if r % d == 0 and (d * c <= 256 * 1024 or d == 16))


def _pair_add(grad, got, name):
    _, r, c = got.shape
    tr = r
    core = lax.axis_index("c").astype(jnp.int32).reshape(1)

    def body(core_ref, g_ref, a_ref, o_ref):
        o_ref[...] = (g_ref[...].astype(F32) + a_ref[...].astype(F32)).astype(o_ref.dtype)

    blk = pl.BlockSpec((None, tr, c), lambda q, i, core_ref: (q, i, 0))
    return pl.pallas_call(
        body, name=name,
        grid_spec=pltpu.PrefetchScalarGridSpec(
            num_scalar_prefetch=1, grid=(4, r // tr),
            in_specs=[pl.BlockSpec((None, None, tr, c), lambda q, i, core_ref: (q, core_ref[0], i, 0)), blk],
            out_specs=blk),
        out_shape=jax.ShapeDtypeStruct(got.shape, got.dtype),
        compiler_params=_params("parallel", "parallel"),
    )(core, grad.reshape(4, 2, r, c), got)


def _sum_slots(slots, name):
    n, r, c = slots.shape
    tr = _row_tile(r, c)

    def body(s_ref, o_ref):
        acc = s_ref[0].astype(F32)
        for s in range(1, n):
            acc = acc + s_ref[s].astype(F32)
        o_ref[...] = acc

    return pl.pallas_call(
        body, name=name, grid=(r // tr,),
        in_specs=[pl.BlockSpec((n, tr, c), lambda i: (0, i, 0))],
        out_specs=pl.BlockSpec((tr, c), lambda i: (i, 0)),
        out_shape=jax.ShapeDtypeStruct((r, c), F32),
        compiler_params=_params("parallel"),
    )(slots)


def _adamw_math(w, g, m, v):
    m = ADAM_B1 * m + (1.0 - ADAM_B1) * g
    v = ADAM_B2 * v + (1.0 - ADAM_B2) * (g * g)
    m_hat = m / (1.0 - ADAM_B1 ** ADAM_STEP)
    v_hat = v / (1.0 - ADAM_B2 ** ADAM_STEP)
    return -ADAM_LR * (m_hat / (jnp.sqrt(v_hat) + ADAM_EPS) + ADAM_WD * w), m, v


def _adamw(w, slots, m, v, name):
    n, r, c = slots.shape
    tr = _row_tile(r, c)

    def body(w_ref, s_ref, m_ref, v_ref, g_ref, d_ref, nm_ref, nv_ref):
        g = s_ref[0].astype(F32)
        for s in range(1, n):
            g = g + s_ref[s].astype(F32)
        g_ref[...] = g
        d_ref[...], nm_ref[...], nv_ref[...] = _adamw_math(w_ref[...], g, m_ref[...], v_ref[...])

    blk = pl.BlockSpec((tr, c), lambda i: (i, 0))
    return pl.pallas_call(
        body, name=name, grid=(r // tr,),
        in_specs=[blk, pl.BlockSpec((n, tr, c), lambda i: (0, i, 0)), blk, blk], out_specs=[blk] * 4,
        out_shape=[jax.ShapeDtypeStruct((r, c), F32)] * 4,
        compiler_params=_params("parallel"),
    )(w, slots, m, v)


def _adamw_small(ws, gs, ms, vs):
    n = len(ws)

    def body(*refs):
        for i in range(n):
            w_ref, g_ref, m_ref, v_ref = (refs[j * n + i] for j in range(4))
            d_ref, nm_ref, nv_ref = (refs[(4 + j) * n + i] for j in range(3))
            d_ref[...], nm_ref[...], nv_ref[...] = _adamw_math(w_ref[...], g_ref[...], m_ref[...], v_ref[...])

    shapes = [jax.ShapeDtypeStruct(w.shape, F32) for w in ws]
    outs = pl.pallas_call(body, name="adamw_small", out_shape=shapes * 3)(*ws, *gs, *ms, *vs)
    return outs[:n], outs[n:2 * n], outs[2 * n:]


def kernel(x, norm_mix, w_in, b_gate, rel_bias, w_ret_out, w_att_out, w_out, norm_ffn, w_ffn_gate, w_ffn_up, w_ffn_down, norm_final, loss_target, m_norm_mix, m_w_in, m_b_gate, m_rel_bias, m_w_ret_out, m_w_att_out, m_w_out, m_norm_ffn, m_w_ffn_gate, m_w_ffn_up, m_w_ffn_down, m_norm_final, v_norm_mix, v_w_in, v_b_gate, v_rel_bias, v_w_ret_out, v_w_att_out, v_w_out, v_norm_ffn, v_w_ffn_gate, v_w_ffn_up, v_w_ffn_down, v_norm_final):
    me = _index(_place())
    n_rb = rel_bias.shape[-1]

    shard = dict(w_in_t=w_in[0].T, w_gate_t=w_ffn_gate[0].T, w_up_t=w_ffn_up[0].T, w_down=w_ffn_down[0],
                 w_ret=w_ret_out[0], w_out=w_out[0], w_att_t=w_att_out[0].T)
    shard = {nm: s.astype(BF16) for nm, s in shard.items()}
    dx, slots, small_slots = _step(x, loss_target, norm_mix, b_gate, norm_ffn, norm_final, rel_bias[0], shard)
    small_sum = _sum_slots(small_slots, "sum_small")
    loss = small_sum[5, 0]

    transposed = dict(w_in="w_in_t", w_ffn_gate="w_gate_t", w_ffn_up="w_up_t", w_att_out="w_att_t")
    plain = dict(w_ffn_down="w_down", w_ret_out="w_ret", w_out="w_out")
    g = dict(
        norm_mix=small_sum[0:1], b_gate=jnp.concatenate([small_sum[1:2], small_sum[2:3]], axis=1),
        norm_ffn=small_sum[3:4], norm_final=small_sum[4:5],
        rel_bias=lax.dynamic_slice_in_dim(_bias_rows_bwd(small_sum[8:16]), me * n_rb, n_rb, axis=1),
    )
    w = dict(norm_mix=norm_mix, w_in=w_in, b_gate=b_gate, rel_bias=rel_bias, w_ret_out=w_ret_out, w_att_out=w_att_out,
             w_out=w_out, norm_ffn=norm_ffn, w_ffn_gate=w_ffn_gate, w_ffn_up=w_ffn_up, w_ffn_down=w_ffn_down,
             norm_final=norm_final)
    m = dict(norm_mix=m_norm_mix, w_in=m_w_in, b_gate=m_b_gate, rel_bias=m_rel_bias, w_ret_out=m_w_ret_out,
             w_att_out=m_w_att_out, w_out=m_w_out, norm_ffn=m_norm_ffn, w_ffn_gate=m_w_ffn_gate, w_ffn_up=m_w_ffn_up,
             w_ffn_down=m_w_ffn_down, norm_final=m_norm_final)
    v = dict(norm_mix=v_norm_mix, w_in=v_w_in, b_gate=v_b_gate, rel_bias=v_rel_bias, w_ret_out=v_w_ret_out,
             w_att_out=v_w_att_out, w_out=v_w_out, norm_ffn=v_norm_ffn, w_ffn_gate=v_w_ffn_gate, w_ffn_up=v_w_ffn_up,
             w_ffn_down=v_w_ffn_down, norm_final=v_norm_final)
    order = ("norm_mix", "w_in", "b_gate", "rel_bias", "w_ret_out", "w_att_out", "w_out", "norm_ffn",
             "w_ffn_gate", "w_ffn_up", "w_ffn_down", "norm_final")
    small_names = ("norm_mix", "b_gate", "rel_bias", "norm_ffn", "norm_final")

    def flat(a):
        return a[0] if a.ndim == 3 else a.reshape(-1, a.shape[-1])

    grad, delta, new_m, new_v = {}, {}, {}, {}
    for nm in order:
        if nm in transposed:
            res = _adamw(w[nm][0].T, slots[transposed[nm]], m[nm][0].T, v[nm][0].T, "adamw_" + nm)
            grad[nm], delta[nm], new_m[nm], new_v[nm] = (a.T[None] for a in res)
        elif nm in plain:
            res = _adamw(flat(w[nm]), slots[plain[nm]], flat(m[nm]), flat(v[nm]), "adamw_" + nm)
            grad[nm], delta[nm], new_m[nm], new_v[nm] = (a.reshape(w[nm].shape) for a in res)
    ds, nms, nvs = _adamw_small([flat(w[nm]) for nm in small_names], [g[nm] for nm in small_names],
                                [flat(m[nm]) for nm in small_names], [flat(v[nm]) for nm in small_names])
    for i, nm in enumerate(small_names):
        grad[nm], delta[nm], new_m[nm], new_v[nm] = (a.reshape(w[nm].shape) for a in (g[nm], ds[i], nms[i], nvs[i]))

    return (loss, dx, *[grad[nm] for nm in order], *[delta[nm] for nm in order],
            *[new_m[nm] for nm in order], *[new_v[nm] for nm in order])
```

```python
import numpy as np
import jax
import jax.numpy as jnp
from jax import lax
from jax.experimental import pallas as pl
from jax.experimental.pallas import tpu as pltpu

F32 = jnp.float32
BF16 = jnp.bfloat16
MESH = pl.DeviceIdType.MESH

D_MODEL = 1024
CHUNK = 64
RET_HEADS = 4
RET_KEY_DIM = 128
RET_VAL_DIM = 256
ATT_HEADS = 8
BAND_CHUNKS = 8
MAX_REL = 256
N_REL = CHUNK + MAX_REL
D_FF = 2816
N_IN = 6656
ROPE_BASE = 10000.0
EPS = 1e-6
NEG_INF = -1e30
C_RQ, C_RK, C_RV, C_RG, C_AQ, C_AK, C_AV, C_GL = 0, 512, 1024, 2048, 3072, 3584, 4096, 4608

ADAM_LR = 0.001
ADAM_B1 = 0.9
ADAM_B2 = 0.999
ADAM_EPS = 1e-08
ADAM_WD = 0.01
ADAM_STEP = 10

N_DEV = 8
LANES = 128
RET_TILE = 256
ATT_Q = 256
ATT_PAD = BAND_CHUNKS * CHUNK
ATT_WIN = ATT_PAD + ATT_Q
ATT_STARTS = ATT_PAD // ATT_Q
ATT_ROWS = 32
BIAS_LEN = 1024
VMEM_LIMIT = 48 * 1024 * 1024
VMEM_LIMIT_ATT_BWD = 48 * 1024 * 1024


def _params(*sem, vmem=VMEM_LIMIT):
    return pltpu.CompilerParams(dimension_semantics=sem, vmem_limit_bytes=vmem)


def _dot(a, b):
    return lax.dot_general(a, b, (((1,), (0,)), ((), ())), preferred_element_type=F32)


def _dot_nt(a, b):
    return lax.dot_general(a, b, (((1,), (1,)), ((), ())), preferred_element_type=F32)


def _dot_tn(a, b):
    return lax.dot_general(a, b, (((0,), (0,)), ((), ())), preferred_element_type=F32)


def _sigmoid(x):
    return 1.0 / (1.0 + jnp.exp(-x))


def _rms_bwd(x, g, dy):
    r = lax.rsqrt(jnp.mean(x * x, axis=-1, keepdims=True) + EPS)
    u = dy * g
    dx = r * u - x * (r * r * r) * jnp.mean(u * x, axis=-1, keepdims=True)
    return dx, dy * x * r


def _place():
    return lax.axis_index("x"), lax.axis_index("y"), lax.axis_index("c")


def _peer(k):
    x, y, c = _place()
    return ((1 - x) if k & 4 else x, (1 - y) if k & 2 else y, (1 - c) if k & 1 else c)


def _index(place):
    return 4 * place[0] + 2 * place[1] + place[2]


def _rows(ref, block, nrows):
    align = 16 if ref.dtype == BF16 else 8
    return ref.at[pl.ds(pl.multiple_of(block * nrows, align), nrows)]


class _Scatter:
    def __init__(self, arrays):
        self.arrays, self.n = list(arrays), len(arrays)

    def out_shape(self):
        return [jax.ShapeDtypeStruct((N_DEV, a.shape[0] // N_DEV) + a.shape[1:], a.dtype) for a in self.arrays]

    def scratch(self):
        return [pltpu.SemaphoreType.DMA((self.n, N_DEV - 1)), pltpu.SemaphoreType.DMA((self.n, N_DEV - 1)),
                pltpu.SemaphoreType.DMA((self.n,))]

    def _copies(self, ins, outs, sems):
        send_sems, recv_sems, local_sems = sems
        me = _index(_place())

        def src(w, to):
            return _rows(ins[w], to, ins[w].shape[0] // N_DEV)

        def dst(w, origin):
            return outs[w].at[origin]

        def remote(w, k, to, origin):
            return pltpu.make_async_remote_copy(src_ref=src(w, to), dst_ref=dst(w, origin),
                                                send_sem=send_sems.at[w, k - 1], recv_sem=recv_sems.at[w, k - 1],
                                                device_id=_peer(k), device_id_type=MESH)

        pairs = [(w, k) for w in range(self.n) for k in range(1, N_DEV)]
        own = lambda: [pltpu.make_async_copy(src(w, me), dst(w, me), local_sems.at[w]) for w in range(self.n)]
        sent = lambda: [remote(w, k, _index(_peer(k)), me) for w, k in pairs]
        arriving = lambda: [remote(w, k, me, _index(_peer(k))) for w, k in pairs]
        return own, sent, arriving

    def start(self, ins, outs, sems):
        own, sent, _ = self._copies(ins, outs, sems)
        for cp in own() + sent():
            cp.start()

    def wait(self, ins, outs, sems):
        own, sent, arriving = self._copies(ins, outs, sems)
        for cp in arriving():
            cp.wait_recv()
        for cp in sent():
            cp.wait_send()
        for cp in own():
            cp.wait()


class _PairSwap:
    def __init__(self, arrays):
        self.arrays, self.n = list(arrays), len(arrays)

    def out_shape(self):
        return [jax.ShapeDtypeStruct((4, a.shape[0] // N_DEV) + a.shape[1:], a.dtype) for a in self.arrays]

    def scratch(self):
        return [pltpu.SemaphoreType.DMA((self.n, 4)), pltpu.SemaphoreType.DMA((self.n, 4))]

    def _copies(self, ins, outs, sems):
        send_sems, recv_sems = sems
        x, y, c = _place()
        return [pltpu.make_async_remote_copy(
            src_ref=_rows(ins[w], 2 * q + 1 - c, ins[w].shape[0] // N_DEV), dst_ref=outs[w].at[q],
            send_sem=send_sems.at[w, q], recv_sem=recv_sems.at[w, q],
            device_id=(x, y, 1 - c), device_id_type=MESH) for w in range(self.n) for q in range(4)]

    def start(self, ins, outs, sems):
        for cp in self._copies(ins, outs, sems):
            cp.start()

    def wait(self, ins, outs, sems):
        for cp in self._copies(ins, outs, sems):
            cp.wait()


class _ChipScatter:
    def __init__(self, arrays):
        self.arrays, self.n = list(arrays), len(arrays)

    def out_shape(self):
        return [jax.ShapeDtypeStruct(a.shape, a.dtype) for a in self.arrays]

    def scratch(self):
        return [pltpu.SemaphoreType.DMA((self.n, 3)), pltpu.SemaphoreType.DMA((self.n, 3)),
                pltpu.SemaphoreType.DMA((self.n,))]

    def _copies(self, ins, outs, sems):
        send_sems, recv_sems, local_sems = sems
        x, y, c = _place()
        mine = 2 * x + y
        sent, arriving = [], []
        for w in range(self.n):
            for k in range(1, 4):
                tx, ty = (1 - x) if k & 2 else x, (1 - y) if k & 1 else y
                other = 2 * tx + ty
                sent.append(lambda w=w, k=k, tx=tx, ty=ty, other=other: pltpu.make_async_remote_copy(
                    src_ref=ins[w].at[other], dst_ref=outs[w].at[mine], send_sem=send_sems.at[w, k - 1],
                    recv_sem=recv_sems.at[w, k - 1], device_id=(tx, ty, c), device_id_type=MESH))
                arriving.append(lambda w=w, k=k, tx=tx, ty=ty, other=other: pltpu.make_async_remote_copy(
                    src_ref=ins[w].at[mine], dst_ref=outs[w].at[other], send_sem=send_sems.at[w, k - 1],
                    recv_sem=recv_sems.at[w, k - 1], device_id=(tx, ty, c), device_id_type=MESH))
        own = [lambda w=w: pltpu.make_async_copy(ins[w].at[mine], outs[w].at[mine], local_sems.at[w])
               for w in range(self.n)]
        return own, sent, arriving

    def start(self, ins, outs, sems):
        own, sent, _ = self._copies(ins, outs, sems)
        for cp in own + sent:
            cp().start()

    def wait(self, ins, outs, sems):
        own, sent, arriving = self._copies(ins, outs, sems)
        for cp in arriving:
            cp().wait_recv()
        for cp in sent:
            cp().wait_send()
        for cp in own:
            cp().wait()


class _ChipGather:
    def __init__(self, arrays, parts=None, into=None):
        self.arrays, self.n, self.into = list(arrays), len(arrays), into
        self.parts = parts or [(0, 1)] * self.n

    def out_shape(self):
        return [jax.ShapeDtypeStruct((N_DEV * a.shape[0],) + a.shape[1:], a.dtype) for a in self.arrays]

    def scratch(self):
        return [pltpu.SemaphoreType.DMA((self.n, N_DEV - 1)), pltpu.SemaphoreType.DMA((self.n, N_DEV - 1)),
                pltpu.SemaphoreType.DMA((self.n,))]

    def _parts(self, ins, outs, sems):
        send_sems, recv_sems, local_sems = sems
        x, y, c = _place()
        me, sibling = (x, y, c), (x, y, 1 - c)
        chips = [(1 - x, y), (x, 1 - y), (1 - x, 1 - y)]

        def rows(w, place, whole):
            (index, count), r = self.parts[w], ins[w].shape[0]
            lo, size = (0, r) if whole else (index * (r // count), r // count)
            align = 16 if ins[w].dtype == BF16 else 8
            return outs[w].at[pl.ds(pl.multiple_of(_index(place) * r + lo, align), size)]

        def mine(w, whole):
            (index, count), r = self.parts[w], ins[w].shape[0]
            return ins[w] if whole or count == 1 else ins[w].at[pl.ds(index * (r // count), r // count)]

        def copy(w, k, block, to, own=False):
            whole = k == 0
            return pltpu.make_async_remote_copy(src_ref=mine(w, whole) if own else rows(w, block, whole),
                                                dst_ref=rows(w, block, whole),
                                                send_sem=send_sems.at[w, k], recv_sem=recv_sems.at[w, k],
                                                device_id=to, device_id_type=MESH)

        def local(w):
            return pltpu.make_async_copy(ins[w], rows(w, me, True), local_sems.at[w])

        return me, sibling, chips, c, copy, local, [index == 0 for index, _ in self.parts]

    def start(self, ins, outs, sems):
        me, sibling, chips, c, copy, local, places_own = self._parts(ins, outs, sems)
        for w in range(self.n):
            if places_own[w]:
                local(w).start()
                copy(w, 0, me, sibling, own=True).start()
            for j, chip in enumerate(chips):
                copy(w, 1 + j, me, (*chip, c), own=True).start()

    def wait(self, ins, outs, sems):
        me, sibling, chips, c, copy, local, places_own = self._parts(ins, outs, sems)
        for w in range(self.n):
            for j, chip in enumerate(chips):
                copy(w, 1 + j, (*chip, c), me).wait_recv()
                copy(w, 4 + j, (*chip, c), sibling).start()
        for w in range(self.n):
            if places_own[w]:
                copy(w, 0, sibling, me).wait_recv()
                copy(w, 0, me, sibling, own=True).wait_send()
                local(w).wait()
            for j, chip in enumerate(chips):
                copy(w, 4 + j, (*chip, 1 - c), me).wait_recv()
                copy(w, 1 + j, me, (*chip, c), own=True).wait_send()
                copy(w, 4 + j, (*chip, c), sibling).wait_send()


def _call(body, *, name, grid, in_specs, out_specs, out_shape, scratch=(), semantics, args, exchange=None,
          vmem=VMEM_LIMIT):
    if exchange is None:
        return pl.pallas_call(body, name=name, grid=grid, in_specs=in_specs, out_specs=out_specs, out_shape=out_shape,
                              scratch_shapes=list(scratch),
                              compiler_params=_params(*semantics, vmem=vmem))(*args), None
    n_in, n_out, n_scr, nx = len(in_specs), len(out_specs), len(scratch), exchange.n
    into = list(getattr(exchange, "into", None) or [])

    def full_body(*refs):
        ins, refs = refs[:n_in], refs[n_in:]
        x_in, refs = refs[:nx], refs[nx + len(into):]
        outs, refs = refs[:n_out], refs[n_out:]
        x_out, refs = refs[:nx], refs[nx:]
        scr, sems = refs[:n_scr], refs[n_scr:]
        first, last = True, True
        for axis, size in enumerate(grid):
            first = jnp.logical_and(first, pl.program_id(axis) == 0)
            last = jnp.logical_and(last, pl.program_id(axis) == size - 1)
        if grid:
            pl.when(first)(lambda: exchange.start(x_in, x_out, sems))
        else:
            exchange.start(x_in, x_out, sems)
        body(*ins, *outs, *scr)
        if grid:
            pl.when(last)(lambda: exchange.wait(x_in, x_out, sems))
        else:
            exchange.wait(x_in, x_out, sems)

    hbm = pl.BlockSpec(memory_space=pltpu.HBM)
    res = pl.pallas_call(
        full_body, name=name, grid=grid,
        in_specs=list(in_specs) + [hbm] * (nx + len(into)), out_specs=list(out_specs) + [hbm] * nx,
        out_shape=list(out_shape) + exchange.out_shape(),
        scratch_shapes=list(scratch) + exchange.scratch(),
        input_output_aliases={n_in + nx + w: n_out + w for w in range(len(into))},
        compiler_params=_params(*(["arbitrary"] * len(grid)), vmem=vmem),
    )(*args, *exchange.arrays, *into)
    return res[:n_out], res[n_out:]


def _alone(exchange, name):
    return _call(lambda: None, name=name, grid=(), in_specs=[], out_specs=[], out_shape=[], semantics=(),
                 args=(), exchange=exchange)[1]


def _mm(a, b, *, ta=False, tb=False, out_dtype, tm, tn, tk, name, exchange=None):
    m, k = (a.shape[1], a.shape[0]) if ta else a.shape
    n = b.shape[0] if tb else b.shape[1]
    assert k == (b.shape[1] if tb else b.shape[0])
    tm, tn, tk = min(tm, m), min(tn, n), min(tk, k)
    assert m % tm == 0 and n % tn == 0 and k % tk == 0, (name, m, n, k)
    nk = k // tk
    dims = (((0 if ta else 1,), (1 if tb else 0,)), ((), ()))

    def body(a_ref, b_ref, o_ref, *acc):
        prod = lax.dot_general(a_ref[...].astype(BF16), b_ref[...].astype(BF16), dims, preferred_element_type=F32)
        if nk == 1:
            o_ref[...] = prod.astype(o_ref.dtype)
            return
        acc_ref, kk = acc[0], pl.program_id(2)

        @pl.when(kk == 0)
        def _():
            acc_ref[...] = prod

        @pl.when((kk > 0) & (kk < nk - 1))
        def _():
            acc_ref[...] += prod

        @pl.when(kk == nk - 1)
        def _():
            o_ref[...] = (acc_ref[...] + prod).astype(o_ref.dtype)

    a_spec = (pl.BlockSpec((tk, tm), lambda i, j, kk: (kk, i)) if ta
              else pl.BlockSpec((tm, tk), lambda i, j, kk: (i, kk)))
    b_spec = (pl.BlockSpec((tn, tk), lambda i, j, kk: (j, kk)) if tb
              else pl.BlockSpec((tk, tn), lambda i, j, kk: (kk, j)))
    (out,), moved = _call(
        body, name=name, grid=(m // tm, n // tn, nk),
        in_specs=[a_spec, b_spec],
        out_specs=[pl.BlockSpec((tm, tn), lambda i, j, kk: (i, j))],
        out_shape=[pltpu.HBM((m, n), out_dtype)],
        scratch=[pltpu.VMEM((tm, tn), F32)] if nk > 1 else [],
        semantics=("parallel", "parallel", "arbitrary"), args=(a, b), exchange=exchange)
    return out if exchange is None else (out, moved)


def _mm_pieces(pieces, b, *, ta, out_dtype, tm, tn, tk, name, exchange=None):
    rows, n = pieces[0].shape[0], b.shape[1]
    step = tm if ta else tk
    assert all(p.shape[0] == rows and p.shape[1] % step == 0 for p in pieces), name
    edges = [int(e) for e in np.cumsum([0] + [p.shape[1] // step for p in pieces])]
    total = edges[-1] * step
    m, k = (total, rows) if ta else (rows, total)
    assert b.shape[0] == k and m % tm == 0 and n % tn == 0 and k % tk == 0, name
    nk, npieces = k // tk, len(pieces)
    dims = (((0 if ta else 1,), (0,)), ((), ()))
    b_resident = ta and n == tn

    def body(*refs):
        a_refs, (b_ref, o_ref, acc_ref) = refs[:npieces], refs[npieces:]
        kk = pl.program_id(2)
        pos = pl.program_id(0) if ta else kk

        @pl.when(kk == 0)
        def _():
            acc_ref[...] = jnp.zeros_like(acc_ref)

        def b_tile():
            return b_ref[pl.ds(pl.multiple_of(kk * tk, tk), tk), :] if b_resident else b_ref[...]

        for p, a_ref in enumerate(a_refs):
            @pl.when((pos >= edges[p]) & (pos < edges[p + 1]))
            def _(a_ref=a_ref):
                acc_ref[...] += lax.dot_general(a_ref[...], b_tile(), dims, preferred_element_type=F32)

        @pl.when(kk == nk - 1)
        def _():
            o_ref[...] = acc_ref[...].astype(o_ref.dtype)

    def a_spec(p):
        lo, last = edges[p], edges[p + 1] - edges[p] - 1
        if ta:
            def index(i, j, kk):
                inside = (i >= lo) & (i <= lo + last)
                return jnp.where(inside, kk, 0), jnp.clip(i - lo, 0, last)
            return pl.BlockSpec((tk, tm), index)
        return pl.BlockSpec((tm, tk), lambda i, j, kk: (i, jnp.clip(kk - lo, 0, last)))

    (out,), moved = _call(
        body, name=name, grid=(m // tm, n // tn, nk),
        in_specs=[a_spec(p) for p in range(npieces)]
                 + [pl.BlockSpec(b.shape, lambda i, j, kk: (0, 0)) if b_resident
                    else pl.BlockSpec((tk, tn), lambda i, j, kk: (kk, j))],
        out_specs=[pl.BlockSpec((tm, tn), lambda i, j, kk: (i, j))],
        out_shape=[pltpu.HBM((m, n), out_dtype)],
        scratch=[pltpu.VMEM((tm, tn), F32)],
        semantics=("parallel", "parallel", "arbitrary"), args=(*pieces, b), exchange=exchange)
    return out if exchange is None else (out, moved)


def _rms_fwd(x2, g, exchange):
    t = x2.shape[0]
    tm = min(512, t)

    def body(x_ref, g_ref, o_ref):
        x = x_ref[...]
        r = lax.rsqrt(jnp.mean(x * x, axis=-1, keepdims=True) + EPS)
        o_ref[...] = (x * r * g_ref[...]).astype(o_ref.dtype)

    return _call(
        body, name="rms_in_fwd", grid=(t // tm,),
        in_specs=[pl.BlockSpec((tm, D_MODEL), lambda i: (i, 0)), pl.BlockSpec((1, D_MODEL), lambda i: (0, 0))],
        out_specs=[pl.BlockSpec((tm, D_MODEL), lambda i: (i, 0))],
        out_shape=[jax.ShapeDtypeStruct((t, D_MODEL), BF16)],
        semantics=("parallel",), args=(x2, g), exchange=exchange)


def _decay(lg):
    row = lax.broadcasted_iota(jnp.int32, (RET_TILE, RET_TILE), 0)
    col = lax.broadcasted_iota(jnp.int32, (RET_TILE, RET_TILE), 1)
    within = jnp.exp(lg * jnp.abs(row - col).astype(F32))
    inside = jnp.where((col >> 6) <= (row >> 6), within, 0.0)
    pos = lax.broadcasted_iota(jnp.int32, (RET_TILE, 1), 0).astype(F32)
    q_dec = jnp.exp(lg * (pos + 1.0))
    k_dec = jnp.exp(lg * (RET_TILE - 1.0 - pos))
    tile_dec = jnp.exp(lg * float(RET_TILE))
    return inside, q_dec, k_dec, tile_dec


def _scaled(a_bf16, dec):
    return (a_bf16.astype(F32) * dec).astype(BF16)


def _ret_specs(seq):
    key = lambda base: pl.BlockSpec((seq, RET_KEY_DIM), lambda b, h: (b, base // RET_KEY_DIM + h))
    val = lambda base: pl.BlockSpec((seq, RET_VAL_DIM), lambda b, h: (b, base // RET_VAL_DIM + h))
    tab = pl.BlockSpec((seq, RET_KEY_DIM), lambda b, h: (0, 0))
    lgs = pl.BlockSpec((None, 1, LANES), lambda b, h: (h, 0, 0))
    return key, val, tab, lgs


def _ret_fwd(proj, cs, sn, lg_arr, batch, seq, exchange):
    t = batch * seq
    nt = seq // RET_TILE

    def body(q_ref, k_ref, v_ref, rg_ref, cs_ref, sn_ref, lg_ref, gro_ref, o_ref, qr_ref, kr_ref):
        lg = lg_ref[:, 0:1]
        cs_t, sn_t = cs_ref[...], sn_ref[...]
        q = q_ref[...].astype(F32)
        k = k_ref[...].astype(F32)
        qr_ref[...] = (q * cs_t + pltpu.roll(q, 64, 1) * sn_t).astype(BF16)
        kr_ref[...] = ((k * cs_t + pltpu.roll(k, 64, 1) * sn_t) * (RET_KEY_DIM ** -0.5)).astype(BF16)
        inside, q_dec, k_dec, tile_dec = _decay(lg)
        state = jnp.zeros((RET_KEY_DIM, RET_VAL_DIM), F32)
        for i in range(nt):
            rows = slice(i * RET_TILE, (i + 1) * RET_TILE)
            qi, ki, vi = qr_ref[rows, :], kr_ref[rows, :], v_ref[rows, :]
            acc = _dot((_dot_nt(qi, ki) * inside).astype(BF16), vi)
            if i > 0:
                acc = acc + _dot(_scaled(qi, q_dec), state.astype(BF16))
            if i < nt - 1:
                state = state * tile_dec + _dot_tn(_scaled(ki, k_dec), vi)
            o_ref[rows, :] = acc
            xc = acc - jnp.mean(acc, axis=-1, keepdims=True)
            nrm = xc * lax.rsqrt(jnp.mean(xc * xc, axis=-1, keepdims=True) + EPS)
            rg = rg_ref[rows, :].astype(F32)
            gro_ref[rows, :] = (rg * _sigmoid(rg) * nrm).astype(BF16)

    key, val, tab, lgs = _ret_specs(seq)
    return _call(
        body, name="ret_fwd", grid=(batch, RET_HEADS),
        in_specs=[key(C_RQ), key(C_RK), val(C_RV), val(C_RG), tab, tab, lgs],
        out_specs=[val(0), val(0), key(0), key(0)],
        out_shape=[jax.ShapeDtypeStruct((t, RET_HEADS * RET_VAL_DIM), BF16),
                   jax.ShapeDtypeStruct((t, RET_HEADS * RET_VAL_DIM), F32),
                   jax.ShapeDtypeStruct((t, RET_HEADS * RET_KEY_DIM), BF16),
                   jax.ShapeDtypeStruct((t, RET_HEADS * RET_KEY_DIM), BF16)],
        semantics=("parallel", "parallel"), args=(proj, proj, proj, proj, cs, sn, lg_arr), exchange=exchange)


def _att_bias(w_ref, bias_ref):
    n_i = lax.broadcasted_iota(jnp.int32, (ATT_Q, BIAS_LEN), 0)
    qc = lax.broadcasted_iota(jnp.int32, (ATT_Q, ATT_WIN), 0) >> 6
    kc = lax.broadcasted_iota(jnp.int32, (ATT_Q, ATT_WIN), 1) >> 6
    dc = qc + BAND_CHUNKS - kc
    band = (dc >= 0) & (dc <= BAND_CHUNKS)
    key = lax.broadcasted_iota(jnp.int32, (ATT_Q, ATT_WIN), 1)
    for e in range(2):
        xw = jnp.broadcast_to(w_ref[e:e + 1, :], (ATT_Q, BIAS_LEN))
        for bit in range(8):
            xw = jnp.where(((n_i >> bit) & 1) == 1, pltpu.roll(xw, 1 << bit, 1), xw)
        bias = jnp.where(band, xw[:, BIAS_LEN - ATT_WIN:], NEG_INF)
        for first in range(ATT_STARTS):
            bias_ref[first, e] = jnp.where(key + (first * ATT_Q - ATT_PAD) >= 0, bias, NEG_INF)
        bias_ref[ATT_STARTS, e] = bias


ATT_PAIRS = 2
ATT_COLS = ATT_PAIRS * LANES


def _att_specs(batch, seq):
    ni = seq // ATT_Q
    q_spec = pl.BlockSpec((ATT_Q, ATT_COLS), lambda g, b, i: (b * ni + i, C_AQ // ATT_COLS + g))
    k_spec = pl.BlockSpec((seq, ATT_COLS), lambda g, b, i: (b, C_AK // ATT_COLS + g))
    v_spec = pl.BlockSpec((seq, ATT_COLS), lambda g, b, i: (b, C_AV // ATT_COLS + g))
    w_spec = pl.BlockSpec((ATT_PAIRS, 2, BIAS_LEN), lambda g, b, i: (g, 0, 0))
    b_spec = pl.BlockSpec((ATT_PAIRS, ATT_STARTS + 1, 2, ATT_Q, ATT_WIN), lambda g, b, i: (g, 0, 0, 0, 0))
    pad = pltpu.VMEM((seq + ATT_PAD, ATT_COLS), BF16)
    return ni, q_spec, k_spec, v_spec, w_spec, b_spec, pad


def _att_bias_tiles(wvec, exchange):
    (tiles,), moved = _call(
        lambda w_ref, o_ref: _att_bias(w_ref, o_ref), name="att_bias", grid=(ATT_HEADS // 2,),
        in_specs=[pl.BlockSpec((None, 2, BIAS_LEN), lambda hp: (hp, 0, 0))],
        out_specs=[pl.BlockSpec((None, ATT_STARTS + 1, 2, ATT_Q, ATT_WIN), lambda hp: (hp, 0, 0, 0, 0))],
        out_shape=[jax.ShapeDtypeStruct((ATT_HEADS // 2, ATT_STARTS + 1, 2, ATT_Q, ATT_WIN), F32)],
        semantics=("parallel",), args=(wvec,), exchange=exchange)
    return tiles, moved


def _att_pad(src_ref, pad_ref):
    pad_ref[:ATT_PAD, :] = jnp.zeros((ATT_PAD, ATT_COLS), BF16)
    pad_ref[ATT_PAD:, :] = src_ref[...]


def _att_head(q2, sel):
    return jnp.where(sel, q2, jnp.zeros_like(q2)) * 0.125


def _att_softmax_rows(s_ref, bias_ref, rows):
    s = s_ref[rows, :] + bias_ref[rows, :]
    ex = jnp.exp(s - jnp.max(s, axis=-1, keepdims=True))
    return ex, 1.0 / jnp.sum(ex, axis=-1, keepdims=True)


def _att_fwd(proj, bias, batch, seq, exchange):
    ni, q_spec, k_spec, v_spec, _, b_spec, pad = _att_specs(batch, seq)

    def body(q_ref, k_ref, v_ref, bias_ref, o_ref, kp_ref, vp_ref, s_ref, e_ref):
        i = pl.program_id(2)

        @pl.when(i == 0)
        def _():
            _att_pad(k_ref, kp_ref)
            _att_pad(v_ref, vp_ref)

        win = pl.ds(pl.multiple_of(i * ATT_Q, ATT_Q), ATT_WIN)
        lo = lax.broadcasted_iota(jnp.int32, (1, LANES), 1) < 64
        start = jnp.minimum(i, ATT_STARTS)
        for pair in range(ATT_PAIRS):
            cols = slice(pair * LANES, (pair + 1) * LANES)
            k2, v2, q2 = kp_ref[win, cols], vp_ref[win, cols], q_ref[:, cols]
            out = jnp.zeros((ATT_Q, LANES), F32)
            for e in range(2):
                h = 2 * pair + e
                sel = lo if e == 0 else jnp.logical_not(lo)
                s_ref[h] = _dot_nt(_att_head(q2, sel), k2)
                rsum = []
                for c in range(ATT_Q // ATT_ROWS):
                    rows = slice(c * ATT_ROWS, (c + 1) * ATT_ROWS)
                    ex, r = _att_softmax_rows(s_ref.at[h], bias_ref.at[pair, start, e], rows)
                    e_ref[h, rows, :] = ex.astype(BF16)
                    rsum.append(r)
                out = out + _dot(e_ref[h], jnp.where(sel, v2, jnp.zeros_like(v2))) * jnp.concatenate(rsum, axis=0)
            o_ref[:, cols] = out.astype(BF16)

    heads = 2 * ATT_PAIRS
    return _call(
        body, name="att_fwd", grid=(ATT_HEADS // heads, batch, ni),
        in_specs=[q_spec, k_spec, v_spec, b_spec],
        out_specs=[pl.BlockSpec((ATT_Q, ATT_COLS), lambda g, b, i: (b * ni + i, g))],
        out_shape=[jax.ShapeDtypeStruct((batch * seq, ATT_HEADS * 64), BF16)],
        scratch=[pad, pad, pltpu.VMEM((heads, ATT_Q, ATT_WIN), F32), pltpu.VMEM((heads, ATT_Q, ATT_WIN), BF16)],
        semantics=("arbitrary", "arbitrary", "arbitrary"), args=(proj, proj, proj, bias), exchange=exchange)


GL_HALF = 512


def _gl_specs(tm):
    return [pl.BlockSpec((tm, GL_HALF), lambda i, c=C_GL // GL_HALF + j: (i, c)) for j in range(4)]


def _gates(gl_refs, b_ref):
    logits = [ref[...].astype(F32) for ref in gl_refs]
    gr = _sigmoid(jnp.concatenate(logits[:2], axis=1) + b_ref[:, :D_MODEL])
    ga = _sigmoid(jnp.concatenate(logits[2:], axis=1) + b_ref[:, D_MODEL:])
    return gr, ga


def _whole(a):
    return pl.BlockSpec(a.shape, lambda i: (0,) * a.ndim)


def _mix_out_fwd(gro, ao, proj, b_gate, w_ret, w_att_t, x2, w_out, g2):
    t = gro.shape[0]
    tm = min(256, t)

    def body(gro_ref, ao_ref, gl0, gl1, gl2, gl3, b_ref, wr_ref, wa_ref, x_ref, wo_ref, g_ref,
             z_ref, yr_ref, ya_ref, h_ref, hn_ref):
        yr = _dot(gro_ref[...], wr_ref[...])
        ya = _dot_nt(ao_ref[...], wa_ref[...])
        yr_ref[...] = yr.astype(BF16)
        ya_ref[...] = ya.astype(BF16)
        gr, ga = _gates((gl0, gl1, gl2, gl3), b_ref)
        z = (gr * yr + ga * ya).astype(BF16)
        z_ref[...] = z
        h = x_ref[...] + _dot(z, wo_ref[...])
        h_ref[...] = h
        r = lax.rsqrt(jnp.mean(h * h, axis=-1, keepdims=True) + EPS)
        hn_ref[...] = (h * r * g_ref[...]).astype(BF16)

    row = pl.BlockSpec((tm, D_MODEL), lambda i: (i, 0))
    return pl.pallas_call(
        body, name="mix_out_fwd", grid=(t // tm,),
        in_specs=[row, pl.BlockSpec((tm, 512), lambda i: (i, 0)), *_gl_specs(tm),
                  _whole(b_gate), _whole(w_ret), _whole(w_att_t), row, _whole(w_out), _whole(g2)],
        out_specs=[row] * 5,
        out_shape=[jax.ShapeDtypeStruct((t, D_MODEL), BF16)] * 3
                  + [jax.ShapeDtypeStruct((t, D_MODEL), F32), jax.ShapeDtypeStruct((t, D_MODEL), BF16)],
        compiler_params=_params("parallel"),
    )(gro, ao, proj, proj, proj, proj, b_gate, w_ret, w_att_t, x2, w_out, g2)


def _col_chunks(width, chunk=384):
    return [slice(lo, min(lo + chunk, width)) for lo in range(0, width, chunk)]


def _ffn_up(hn, wg_t, wu_t):
    t = hn.shape[0]
    tm, tn = min(512, t), D_FF // 2

    def body(h_ref, wg_ref, wu_ref, g_ref, u_ref, a_ref):
        g = _dot_nt(h_ref[...], wg_ref[...])
        u = _dot_nt(h_ref[...], wu_ref[...])
        g_ref[...] = g.astype(BF16)
        u_ref[...] = u.astype(BF16)
        a_ref[...] = (g * _sigmoid(g) * u).astype(BF16)

    w_spec = pl.BlockSpec((tn, D_MODEL), lambda j, i: (j, 0))
    out = pl.BlockSpec((tm, tn), lambda j, i: (i, j))
    return pl.pallas_call(
        body, name="ffn_up", grid=(D_FF // tn, t // tm),
        in_specs=[pl.BlockSpec((tm, D_MODEL), lambda j, i: (i, 0)), w_spec, w_spec],
        out_specs=[out, out, out],
        out_shape=[jax.ShapeDtypeStruct((t, D_FF), BF16)] * 3,
        compiler_params=_params("parallel", "parallel"),
    )(hn, wg_t, wu_t)


def _ffn_down_loss(a, h1, tgt, w_down, g3):
    t = a.shape[0]
    tm = min(512, t)

    def body(a_ref, h_ref, t_ref, w_ref, g_ref, dh_ref, dhb_ref, loss_ref, dg_ref):
        @pl.when(pl.program_id(0) == 0)
        def _():
            loss_ref[...] = jnp.zeros_like(loss_ref)
            dg_ref[...] = jnp.zeros_like(dg_ref)

        g = g_ref[...]
        h2 = h_ref[...] + _dot(a_ref[...], w_ref[...])
        r = lax.rsqrt(jnp.mean(h2 * h2, axis=-1, keepdims=True) + EPS)
        err = h2 * r * g - t_ref[...]
        loss_ref[...] += jnp.sum(err * err) * (0.5 / D_MODEL)
        dy = err * (1.0 / D_MODEL)
        dh, dg_rows = _rms_bwd(h2, g, dy)
        dg_ref[...] += jnp.sum(dg_rows, axis=0, keepdims=True)
        dh_ref[...] = dh
        dhb_ref[...] = dh.astype(BF16)

    row = pl.BlockSpec((tm, D_MODEL), lambda i: (i, 0))
    vec = pl.BlockSpec((1, D_MODEL), lambda i: (0, 0))
    return pl.pallas_call(
        body, name="ffn_down_loss", grid=(t // tm,),
        in_specs=[pl.BlockSpec((tm, D_FF), lambda i: (i, 0)), row, row,
                  pl.BlockSpec((D_FF, D_MODEL), lambda i: (0, 0)), vec],
        out_specs=[row, row, pl.BlockSpec((1, LANES), lambda i: (0, 0)), vec],
        out_shape=[jax.ShapeDtypeStruct((t, D_MODEL), F32), jax.ShapeDtypeStruct((t, D_MODEL), BF16),
                   jax.ShapeDtypeStruct((1, LANES), F32), jax.ShapeDtypeStruct((1, D_MODEL), F32)],
        compiler_params=_params("arbitrary"),
    )(a, h1, tgt, w_down, g3)


def _ffn_bwd_act(dh2b, w_down, g_act, u_act, exchange):
    t = dh2b.shape[0]
    tm, tn = min(512, t), D_FF // 2

    def body(d_ref, w_ref, g_ref, u_ref, dg_ref, du_ref):
        d = d_ref[...]
        for cols in _col_chunks(tn):
            da = _dot_nt(d, w_ref[cols, :])
            g = g_ref[:, cols].astype(F32)
            u = u_ref[:, cols].astype(F32)
            sg = _sigmoid(g)
            dg_ref[:, cols] = (da * u * sg * (1.0 + g * (1.0 - sg))).astype(BF16)
            du_ref[:, cols] = (da * g * sg).astype(BF16)

    blk = pl.BlockSpec((tm, tn), lambda j, i: (i, j))
    return _call(
        body, name="ffn_bwd_act", grid=(D_FF // tn, t // tm),
        in_specs=[pl.BlockSpec((tm, D_MODEL), lambda j, i: (i, 0)),
                  pl.BlockSpec((tn, D_MODEL), lambda j, i: (j, 0)), blk, blk],
        out_specs=[blk, blk],
        out_shape=[jax.ShapeDtypeStruct((t, D_FF), BF16)] * 2,
        semantics=("parallel", "parallel"), args=(dh2b, w_down, g_act, u_act), exchange=exchange)


def _ffn_bwd_in(dg, du, wg_t, wu_t, h1, dh2, g2, exchange):
    t = dg.shape[0]
    tm = min(256, t)

    def body(dg_ref, du_ref, wg_ref, wu_ref, h_ref, d2_ref, g_ref, dh_ref, dhb_ref, gn_ref):
        @pl.when(pl.program_id(0) == 0)
        def _():
            gn_ref[...] = jnp.zeros_like(gn_ref)

        dhn = _dot(dg_ref[...], wg_ref[...]) + _dot(du_ref[...], wu_ref[...])
        dx, dg_rows = _rms_bwd(h_ref[...], g_ref[...], dhn)
        dh = d2_ref[...] + dx
        dh_ref[...] = dh
        dhb_ref[...] = dh.astype(BF16)
        gn_ref[...] += jnp.sum(dg_rows, axis=0, keepdims=True)

    act = pl.BlockSpec((tm, D_FF), lambda i: (i, 0))
    row = pl.BlockSpec((tm, D_MODEL), lambda i: (i, 0))
    return _call(
        body, name="ffn_bwd_in", grid=(t // tm,),
        in_specs=[act, act, _whole(wg_t), _whole(wu_t), row, row, _whole(g2)],
        out_specs=[row, row, _whole(g2)],
        out_shape=[jax.ShapeDtypeStruct((t, D_MODEL), F32), jax.ShapeDtypeStruct((t, D_MODEL), BF16),
                   jax.ShapeDtypeStruct((1, D_MODEL), F32)],
        semantics=("arbitrary",), args=(dg, du, wg_t, wu_t, h1, dh2, g2), exchange=exchange)


def _mix_bwd(dh1b, w_out, proj, b_gate, y_ret, y_att, w_ret, w_att_t):
    t = dh1b.shape[0]
    tm = min(256, t)

    def body(d_ref, wo_ref, gl0, gl1, gl2, gl3, b_ref, yr_ref, ya_ref, wr_ref, wa_ref,
             dyr_ref, dya_ref, dglr_ref, dgla_ref, dgro_ref, dao_ref, db_ref):
        @pl.when(pl.program_id(0) == 0)
        def _():
            db_ref[...] = jnp.zeros_like(db_ref)

        dz = _dot_nt(d_ref[...], wo_ref[...])
        gr, ga = _gates((gl0, gl1, gl2, gl3), b_ref)
        dyr = (dz * gr).astype(BF16)
        dya = (dz * ga).astype(BF16)
        dyr_ref[...] = dyr
        dya_ref[...] = dya
        dglr = dz * yr_ref[...].astype(F32) * gr * (1.0 - gr)
        dgla = dz * ya_ref[...].astype(F32) * ga * (1.0 - ga)
        dglr_ref[...] = dglr.astype(BF16)
        dgla_ref[...] = dgla.astype(BF16)
        db_ref[:, :D_MODEL] += jnp.sum(dglr, axis=0, keepdims=True)
        db_ref[:, D_MODEL:] += jnp.sum(dgla, axis=0, keepdims=True)
        dgro_ref[...] = _dot_nt(dyr, wr_ref[...]).astype(BF16)
        dao_ref[...] = _dot(dya, wa_ref[...]).astype(BF16)

    row = pl.BlockSpec((tm, D_MODEL), lambda i: (i, 0))
    half = pl.BlockSpec((tm, 512), lambda i: (i, 0))
    return pl.pallas_call(
        body, name="mix_bwd", grid=(t // tm,),
        in_specs=[row, _whole(w_out), *_gl_specs(tm), _whole(b_gate), row, row, _whole(w_ret), _whole(w_att_t)],
        out_specs=[row, row, row, row, row, half, _whole(b_gate)],
        out_shape=[jax.ShapeDtypeStruct((t, D_MODEL), BF16)] * 5
                  + [jax.ShapeDtypeStruct((t, 512), BF16), jax.ShapeDtypeStruct(b_gate.shape, F32)],
        compiler_params=_params("arbitrary"),
    )(dh1b, w_out, proj, proj, proj, proj, b_gate, y_ret, y_att, w_ret, w_att_t)


def _ret_bwd(dgro, proj, o_ret, qr, kr, cs, sn, lg_arr, batch, seq, exchange):
    t = batch * seq
    nt = seq // RET_TILE

    def body(dgro_ref, rg_ref, o_ref, qr_ref, kr_ref, v_ref, cs_ref, sn_ref, lg_ref,
             dq_ref, dk_ref, dv_ref, drg_ref, do_ref, st_ref):
        lg = lg_ref[:, 0:1]
        inside, q_dec, k_dec, tile_dec = _decay(lg)

        state = jnp.zeros((RET_KEY_DIM, RET_VAL_DIM), F32)
        for i in range(nt - 1):
            rows = slice(i * RET_TILE, (i + 1) * RET_TILE)
            state = state * tile_dec + _dot_tn(_scaled(kr_ref[rows, :], k_dec), v_ref[rows, :])
            st_ref[i + 1] = state.astype(BF16)

        for i in range(nt):
            rows = slice(i * RET_TILE, (i + 1) * RET_TILE)
            o = o_ref[rows, :]
            xc = o - jnp.mean(o, axis=-1, keepdims=True)
            rs = lax.rsqrt(jnp.mean(xc * xc, axis=-1, keepdims=True) + EPS)
            nrm = xc * rs
            rg = rg_ref[rows, :].astype(F32)
            sg = _sigmoid(rg)
            dg = dgro_ref[rows, :].astype(F32)
            drg_ref[rows, :] = (dg * nrm * sg * (1.0 + rg * (1.0 - sg))).astype(BF16)
            dn = dg * rg * sg
            do = rs * (dn - jnp.mean(dn, axis=-1, keepdims=True)
                       - nrm * jnp.mean(dn * nrm, axis=-1, keepdims=True))
            do_ref[rows, :] = do.astype(BF16)

        dstate = jnp.zeros((RET_KEY_DIM, RET_VAL_DIM), F32)
        for i in reversed(range(nt)):
            rows = slice(i * RET_TILE, (i + 1) * RET_TILE)
            qi, ki, vi, doi = qr_ref[rows, :], kr_ref[rows, :], v_ref[rows, :], do_ref[rows, :]
            p = (_dot_nt(qi, ki) * inside).astype(BF16)
            dp = (_dot_nt(doi, vi) * inside).astype(BF16)
            dq = _dot(dp, ki)
            dk = _dot_tn(dp, qi)
            dv = _dot_tn(p, doi)
            if i > 0:
                dq = dq + _dot_nt(doi, st_ref[i]) * q_dec
            if i < nt - 1:
                dsb = dstate.astype(BF16)
                dk = dk + _dot_nt(vi, dsb) * k_dec
                dv = dv + _dot(_scaled(ki, k_dec), dsb)
            if i > 0:
                dstate = dstate * tile_dec + _dot_tn(_scaled(qi, q_dec), doi)
            dq_ref[rows, :] = (dq * cs_ref[rows, :] - pltpu.roll(dq, 64, 1) * sn_ref[rows, :]).astype(BF16)
            dk = (dk * cs_ref[rows, :] - pltpu.roll(dk, 64, 1) * sn_ref[rows, :]) * (RET_KEY_DIM ** -0.5)
            dk_ref[rows, :] = dk.astype(BF16)
            dv_ref[rows, :] = dv.astype(BF16)

    key, val, tab, lgs = _ret_specs(seq)
    return _call(
        body, name="ret_bwd", grid=(batch, RET_HEADS),
        in_specs=[val(0), val(C_RG), val(0), key(0), key(0), val(C_RV), tab, tab, lgs],
        out_specs=[key(0), key(0), val(0), val(0)],
        out_shape=[jax.ShapeDtypeStruct((t, RET_HEADS * RET_KEY_DIM), BF16)] * 2
                  + [jax.ShapeDtypeStruct((t, RET_HEADS * RET_VAL_DIM), BF16)] * 2,
        scratch=[pltpu.VMEM((seq, RET_VAL_DIM), BF16), pltpu.VMEM((nt, RET_KEY_DIM, RET_VAL_DIM), BF16)],
        semantics=("parallel", "parallel"), args=(dgro, proj, o_ret, qr, kr, proj, cs, sn, lg_arr),
        exchange=exchange)


def _att_bwd(proj, bias, dao, batch, seq, exchange):
    ni, q_spec, k_spec, v_spec, w_spec, b_spec, pad = _att_specs(batch, seq)
    t = batch * seq

    def body(q_ref, k_ref, v_ref, bias_ref, do_ref, dq_ref, dk_ref, dv_ref, dw_ref,
             dbias_ref, dk_acc, dv_acc, kp_ref, vp_ref, s_ref, dp_ref, e_ref, ds_ref):
        b, i = pl.program_id(1), pl.program_id(2)

        @pl.when((b == 0) & (i == 0))
        def _():
            dbias_ref[...] = jnp.zeros_like(dbias_ref)

        @pl.when(i == 0)
        def _():
            _att_pad(k_ref, kp_ref)
            _att_pad(v_ref, vp_ref)
            dk_acc[...] = jnp.zeros_like(dk_acc)
            dv_acc[...] = jnp.zeros_like(dv_acc)

        win = pl.ds(pl.multiple_of(i * ATT_Q, ATT_Q), ATT_WIN)
        lo = lax.broadcasted_iota(jnp.int32, (1, LANES), 1) < 64
        start = jnp.minimum(i, ATT_STARTS)
        for pair in range(ATT_PAIRS):
            cols = slice(pair * LANES, (pair + 1) * LANES)
            k2, v2, q2, do2 = kp_ref[win, cols], vp_ref[win, cols], q_ref[:, cols], do_ref[:, cols]
            dq = jnp.zeros((ATT_Q, LANES), F32)
            dk = jnp.zeros((LANES, ATT_WIN), F32)
            dv = jnp.zeros((LANES, ATT_WIN), F32)
            for e in range(2):
                h = 2 * pair + e
                sel = lo if e == 0 else jnp.logical_not(lo)
                qm = _att_head(q2, sel)
                dom = jnp.where(sel, do2, jnp.zeros_like(do2))
                s_ref[h] = _dot_nt(qm, k2)
                dp_ref[h] = _dot_nt(dom, v2)
                rsum = []
                for c in range(ATT_Q // ATT_ROWS):
                    rows = slice(c * ATT_ROWS, (c + 1) * ATT_ROWS)
                    ex, r = _att_softmax_rows(s_ref.at[h], bias_ref.at[pair, start, e], rows)
                    dp = dp_ref[h, rows, :]
                    mean = jnp.sum(dp * ex, axis=-1, keepdims=True) * r
                    ds = ex * ((dp - mean) * r)
                    dbias_ref[h, rows, :] += ds
                    ds_ref[h, rows, :] = ds.astype(BF16)
                    e_ref[h, rows, :] = ex.astype(BF16)
                    rsum.append(r)
                dq = dq + _dot(ds_ref[h], jnp.where(sel, k2, jnp.zeros_like(k2)))
                dk = dk + _dot_tn(qm, ds_ref[h])
                dv = dv + _dot_tn((dom.astype(F32) * jnp.concatenate(rsum, axis=0)).astype(BF16), e_ref[h])
            dq_ref[:, cols] = (dq * 0.125).astype(BF16)
            dk_acc[cols, win] += dk
            dv_acc[cols, win] += dv

        @pl.when(i == ni - 1)
        def _():
            dk_ref[...] = dk_acc[:, ATT_PAD:].T.astype(BF16)
            dv_ref[...] = dv_acc[:, ATT_PAD:].T.astype(BF16)

        @pl.when((b == batch - 1) & (i == ni - 1))
        def _():
            n_i = lax.broadcasted_iota(jnp.int32, (ATT_Q, BIAS_LEN), 0)
            for h in range(heads):
                xw = jnp.concatenate([jnp.zeros((ATT_Q, BIAS_LEN - ATT_WIN), F32), dbias_ref[h]], axis=1)
                for bit in range(8):
                    xw = jnp.where(((n_i >> bit) & 1) == 1, pltpu.roll(xw, BIAS_LEN - (1 << bit), 1), xw)
                dw_ref[h // 2, h % 2:h % 2 + 1, :] = jnp.sum(xw, axis=0, keepdims=True)

    heads = 2 * ATT_PAIRS
    seq_blk = pl.BlockSpec((seq, ATT_COLS), lambda g, b, i: (b, g))
    q_out = pl.BlockSpec((ATT_Q, ATT_COLS), lambda g, b, i: (b * ni + i, g))
    tile_f32, tile_bf16 = pltpu.VMEM((heads, ATT_Q, ATT_WIN), F32), pltpu.VMEM((heads, ATT_Q, ATT_WIN), BF16)
    acc = pltpu.VMEM((ATT_COLS, seq + ATT_PAD), F32)
    return _call(
        body, name="att_bwd", grid=(ATT_HEADS // heads, batch, ni),
        in_specs=[q_spec, k_spec, v_spec, b_spec, q_out],
        out_specs=[q_out, seq_blk, seq_blk, w_spec],
        out_shape=[jax.ShapeDtypeStruct((t, 512), BF16)] * 3
                  + [jax.ShapeDtypeStruct((ATT_HEADS // 2, 2, BIAS_LEN), F32)],
        scratch=[tile_f32, acc, acc, pad, pad, tile_f32, tile_f32, tile_bf16, tile_bf16],
        semantics=("arbitrary", "arbitrary", "arbitrary"), args=(proj, proj, proj, bias, dao), exchange=exchange,
        vmem=VMEM_LIMIT_ATT_BWD)


def _rms_in_bwd(x2, dxn, dh1, g1):
    t = x2.shape[0]
    tm = min(512, t)

    def body(x_ref, d_ref, h_ref, g_ref, dx_ref, dg_ref):
        @pl.when(pl.program_id(0) == 0)
        def _():
            dg_ref[...] = jnp.zeros_like(dg_ref)

        dx, dg_rows = _rms_bwd(x_ref[...], g_ref[...], d_ref[...])
        dx_ref[...] = h_ref[...] + dx
        dg_ref[...] += jnp.sum(dg_rows, axis=0, keepdims=True)

    row = pl.BlockSpec((tm, D_MODEL), lambda i: (i, 0))
    vec = pl.BlockSpec((1, D_MODEL), lambda i: (0, 0))
    return pl.pallas_call(
        body, name="rms_in_bwd", grid=(t // tm,),
        in_specs=[row, row, row, vec], out_specs=[row, vec],
        out_shape=[jax.ShapeDtypeStruct((t, D_MODEL), F32), jax.ShapeDtypeStruct((1, D_MODEL), F32)],
        compiler_params=_params("arbitrary"),
    )(x2, dxn, dh1, g1)


def _pack_small(dg1, dbr, dba, dg2, dg3, dw, loss):
    def body(a_ref, b_ref, c_ref, d_ref, e_ref, w_ref, l_ref, o_ref):
        o_ref[...] = jnp.zeros_like(o_ref)
        for r, ref in enumerate((a_ref, b_ref, c_ref, d_ref, e_ref)):
            o_ref[r:r + 1, :] = ref[...]
        o_ref[5:6, 0:LANES] = l_ref[...]
        for hp in range(ATT_HEADS // 2):
            o_ref[8 + 2 * hp:10 + 2 * hp, :] = w_ref[hp]

    return pl.pallas_call(body, name="pack_small",
                          out_shape=jax.ShapeDtypeStruct((16, D_MODEL), F32))(dg1, dbr, dba, dg2, dg3, dw, loss)


def _rotary_tables(seq):
    freqs = ROPE_BASE ** (-jnp.arange(0, RET_KEY_DIM, 2, dtype=F32) / RET_KEY_DIM)
    ang = jnp.arange(seq, dtype=F32)[:, None] * freqs[None, :]
    cos, sin = jnp.cos(ang), jnp.sin(ang)
    return jnp.concatenate([cos, cos], axis=1), jnp.concatenate([-sin, sin], axis=1)


def _bias_rows(rel_bias):
    n_far = BIAS_LEN - ATT_Q - MAX_REL + 1
    n_near = BIAS_LEN - n_far - (N_REL - 2)
    w = jnp.concatenate([jnp.broadcast_to(rel_bias[:, N_REL - 1:], (ATT_HEADS, n_far)),
                         rel_bias[:, 1:N_REL - 1][:, ::-1],
                         jnp.broadcast_to(rel_bias[:, :1], (ATT_HEADS, n_near))], axis=1)
    return w.reshape(ATT_HEADS // 2, 2, BIAS_LEN)


def _bias_rows_bwd(dw):
    n_far = BIAS_LEN - ATT_Q - MAX_REL + 1
    mid = dw[:, n_far:n_far + N_REL - 2][:, ::-1]
    return jnp.concatenate([jnp.sum(dw[:, n_far + N_REL - 2:], axis=1, keepdims=True), mid,
                            jnp.sum(dw[:, :n_far], axis=1, keepdims=True)], axis=1)


def _step(x, tgt, norm_mix, b_gate, norm_ffn, norm_final, rel_bias_shard, shard):
    batch, seq, _ = x.shape
    t = batch * seq
    n_rb = rel_bias_shard.shape[-1]
    x2, tgt2 = x.reshape(t, D_MODEL), tgt.reshape(t, D_MODEL)
    g3 = norm_final.reshape(1, D_MODEL)
    cs, sn = _rotary_tables(seq)
    lg = np.log(1.0 - 2.0 ** (-5.0 - np.arange(RET_HEADS, dtype=np.float32))).astype(np.float32)
    lg_arr = jnp.asarray(np.broadcast_to(lg[:, None, None], (RET_HEADS, 1, LANES)))

    def gather(*names):
        return _ChipGather([shard[nm] for nm in names])

    def scatter(*grads):
        return _Scatter(grads)

    rb_pad = jnp.pad(rel_bias_shard, ((0, 0), (0, LANES - n_rb)))
    (xn,), (w_in_half, rb_full) = _rms_fwd(x2, norm_mix,
                                           _ChipGather([shard["w_in_t"], rb_pad], parts=[(0, 2), (0, 1)]))
    rb_full = rb_full.reshape(N_DEV, ATT_HEADS, LANES)[:, :, :n_rb]
    bias, (w_in_t,) = _att_bias_tiles(
        _bias_rows(jnp.transpose(rb_full, (1, 0, 2)).reshape(ATT_HEADS, N_DEV * n_rb)),
        _ChipGather([shard["w_in_t"]], parts=[(1, 2)], into=[w_in_half]))
    proj, (w_ret, w_att_t, w_out, w_gate_t) = _mm(
        xn, w_in_t, tb=True, out_dtype=BF16, tm=1024, tn=1664, tk=1024, name="proj",
        exchange=gather("w_ret", "w_att_t", "w_out", "w_gate_t"))
    (gro, o_ret, qr, kr), _ = _ret_fwd(proj, cs, sn, lg_arr, batch, seq, None)
    (ao,), (w_up_t, w_down) = _att_fwd(proj, bias, batch, seq, gather("w_up_t", "w_down"))
    z, y_ret, y_att, h1, hn = _mix_out_fwd(gro, ao, proj, b_gate, w_ret, w_att_t, x2, w_out, norm_ffn)
    g_act, u_act, a_act = _ffn_up(hn, w_gate_t, w_up_t)
    dh2, dh2b, loss, dg3 = _ffn_down_loss(a_act, h1, tgt2, w_down, g3)

    wg = dict(out_dtype=BF16, tn=1024, ta=True)
    slots = {}
    dw_down = _mm(a_act, dh2b, tm=1408, tk=2048, name="dw_down", **wg)
    (d_gact, d_uact), _ = _ffn_bwd_act(dh2b, w_down, g_act, u_act, None)
    dw_gate = _mm(d_gact, hn, tm=1408, tk=2048, name="dw_gate", **wg)
    dw_up = _mm(d_uact, hn, tm=1408, tk=2048, name="dw_up", **wg)
    (dh1, dh1b, dg2), (slots["w_down"],) = _ffn_bwd_in(d_gact, d_uact, w_gate_t, w_up_t, h1, dh2, norm_ffn,
                                                     scatter(dw_down))
    dw_out = _mm(z, dh1b, tm=1024, tk=2048, name="dw_out", **wg)
    dyr, dya, dglr, dgla, dgro, dao, db = _mix_bwd(dh1b, w_out, proj, b_gate, y_ret, y_att, w_ret, w_att_t)
    dw_ret = _mm(gro, dyr, tm=1024, tk=2048, name="dw_ret", **wg)
    dw_att = _mm(dya, ao, tm=1024, tk=2048, name="dw_att", **wg)
    (drq, drk, drv, drg), _ = _ret_bwd(dgro, proj, o_ret, qr, kr, cs, sn, lg_arr, batch, seq, None)
    (daq, dak, dav, dw), (slots["w_gate_t"], slots["w_out"], slots["w_ret"], slots["w_att_t"]) = _att_bwd(
        proj, bias, dao, batch, seq, scatter(dw_gate, dw_out, dw_ret, dw_att))
    dproj = [drq, drk, drv, drg, daq, dak, dav, dglr, dgla]
    dw_in, (slots["w_up_t"],) = _mm_pieces(dproj, xn, ta=True, out_dtype=BF16, tm=512, tn=1024, tk=1024,
                                           name="dw_in", exchange=scatter(dw_up))
    (dw_in_sibling,) = _alone(_PairSwap([dw_in]), "swap_w_in")
    dw_in_pairs = _pair_add(dw_in, dw_in_sibling, "pair_w_in")
    dxn, (slots["w_in_t"],) = _mm_pieces(dproj, w_in_t, ta=False, out_dtype=F32, tm=1024, tn=1024, tk=512, name="dxn",
                                  exchange=_ChipScatter([dw_in_pairs]))
    dx, dg1 = _rms_in_bwd(x2, dxn, dh1, norm_mix)
    small = _pack_small(dg1, db[:, :D_MODEL], db[:, D_MODEL:], dg2, dg3, dw, loss)
    (small_slots,) = _alone(_ChipGather([small]), "gather_small")
    return dx.reshape(batch, seq, D_MODEL), slots, small_slots.reshape(N_DEV, 16, D_MODEL)


def _row_tile(r, c):
    return max(d for d in range(16, r + 1, 16) if r % d == 0 and (d * c <= 256 * 1024 or d == 16))


def _pair_add(grad, got, name):
    _, r, c = got.shape
    tr = r
    core = lax.axis_index("c").astype(jnp.int32).reshape(1)

    def body(core_ref, g_ref, a_ref, o_ref):
        o_ref[...] = (g_ref[...].astype(F32) + a_ref[...].astype(F32)).astype(o_ref.dtype)

    blk = pl.BlockSpec((None, tr, c), lambda q, i, core_ref: (q, i, 0))
    return pl.pallas_call(
        body, name=name,
        grid_spec=pltpu.PrefetchScalarGridSpec(
            num_scalar_prefetch=1, grid=(4, r // tr),
            in_specs=[pl.BlockSpec((None, None, tr, c), lambda q, i, core_ref: (q, core_ref[0], i, 0)), blk],
            out_specs=blk),
        out_shape=jax.ShapeDtypeStruct(got.shape, got.dtype),
        compiler_params=_params("parallel", "parallel"),
    )(core, grad.reshape(4, 2, r, c), got)


def _sum_slots(slots, name):
    n, r, c = slots.shape
    tr = _row_tile(r, c)

    def body(s_ref, o_ref):
        acc = s_ref[0].astype(F32)
        for s in range(1, n):
            acc = acc + s_ref[s].astype(F32)
        o_ref[...] = acc

    return pl.pallas_call(
        body, name=name, grid=(r // tr,),
        in_specs=[pl.BlockSpec((n, tr, c), lambda i: (0, i, 0))],
        out_specs=pl.BlockSpec((tr, c), lambda i: (i, 0)),
        out_shape=jax.ShapeDtypeStruct((r, c), F32),
        compiler_params=_params("parallel"),
    )(slots)


def _adamw_math(w, g, m, v):
    m = ADAM_B1 * m + (1.0 - ADAM_B1) * g
    v = ADAM_B2 * v + (1.0 - ADAM_B2) * (g * g)
    m_hat = m / (1.0 - ADAM_B1 ** ADAM_STEP)
    v_hat = v / (1.0 - ADAM_B2 ** ADAM_STEP)
    return -ADAM_LR * (m_hat / (jnp.sqrt(v_hat) + ADAM_EPS) + ADAM_WD * w), m, v


def _adamw(w, slots, m, v, name):
    n, r, c = slots.shape
    tr = _row_tile(r, c)

    def body(w_ref, s_ref, m_ref, v_ref, g_ref, d_ref, nm_ref, nv_ref):
        g = s_ref[0].astype(F32)
        for s in range(1, n):
            g = g + s_ref[s].astype(F32)
        g_ref[...] = g
        d_ref[...], nm_ref[...], nv_ref[...] = _adamw_math(w_ref[...], g, m_ref[...], v_ref[...])

    blk = pl.BlockSpec((tr, c), lambda i: (i, 0))
    return pl.pallas_call(
        body, name=name, grid=(r // tr,),
        in_specs=[blk, pl.BlockSpec((n, tr, c), lambda i: (0, i, 0)), blk, blk], out_specs=[blk] * 4,
        out_shape=[jax.ShapeDtypeStruct((r, c), F32)] * 4,
        compiler_params=_params("parallel"),
    )(w, slots, m, v)


def _adamw_small(ws, gs, ms, vs):
    n = len(ws)

    def body(*refs):
        for i in range(n):
            w_ref, g_ref, m_ref, v_ref = (refs[j * n + i] for j in range(4))
            d_ref, nm_ref, nv_ref = (refs[(4 + j) * n + i] for j in range(3))
            d_ref[...], nm_ref[...], nv_ref[...] = _adamw_math(w_ref[...], g_ref[...], m_ref[...], v_ref[...])

    shapes = [jax.ShapeDtypeStruct(w.shape, F32) for w in ws]
    outs = pl.pallas_call(body, name="adamw_small", out_shape=shapes * 3)(*ws, *gs, *ms, *vs)
    return outs[:n], outs[n:2 * n], outs[2 * n:]


def kernel(x, norm_mix, w_in, b_gate, rel_bias, w_ret_out, w_att_out, w_out, norm_ffn, w_ffn_gate, w_ffn_up, w_ffn_down, norm_final, loss_target, m_norm_mix, m_w_in, m_b_gate, m_rel_bias, m_w_ret_out, m_w_att_out, m_w_out, m_norm_ffn, m_w_ffn_gate, m_w_ffn_up, m_w_ffn_down, m_norm_final, v_norm_mix, v_w_in, v_b_gate, v_rel_bias, v_w_ret_out, v_w_att_out, v_w_out, v_norm_ffn, v_w_ffn_gate, v_w_ffn_up, v_w_ffn_down, v_norm_final):
    me = _index(_place())
    n_rb = rel_bias.shape[-1]

    shard = dict(w_in_t=w_in[0].T, w_gate_t=w_ffn_gate[0].T, w_up_t=w_ffn_up[0].T, w_down=w_ffn_down[0],
                 w_ret=w_ret_out[0], w_out=w_out[0], w_att_t=w_att_out[0].T)
    shard = {nm: s.astype(BF16) for nm, s in shard.items()}
    dx, slots, small_slots = _step(x, loss_target, norm_mix, b_gate, norm_ffn, norm_final, rel_bias[0], shard)
    small_sum = _sum_slots(small_slots, "sum_small")
    loss = small_sum[5, 0]

    transposed = dict(w_in="w_in_t", w_ffn_gate="w_gate_t", w_ffn_up="w_up_t", w_att_out="w_att_t")
    plain = dict(w_ffn_down="w_down", w_ret_out="w_ret", w_out="w_out")
    g = dict(
        norm_mix=small_sum[0:1], b_gate=jnp.concatenate([small_sum[1:2], small_sum[2:3]], axis=1),
        norm_ffn=small_sum[3:4], norm_final=small_sum[4:5],
        rel_bias=lax.dynamic_slice_in_dim(_bias_rows_bwd(small_sum[8:16]), me * n_rb, n_rb, axis=1),
    )
    w = dict(norm_mix=norm_mix, w_in=w_in, b_gate=b_gate, rel_bias=rel_bias, w_ret_out=w_ret_out, w_att_out=w_att_out,
             w_out=w_out, norm_ffn=norm_ffn, w_ffn_gate=w_ffn_gate, w_ffn_up=w_ffn_up, w_ffn_down=w_ffn_down,
             norm_final=norm_final)
    m = dict(norm_mix=m_norm_mix, w_in=m_w_in, b_gate=m_b_gate, rel_bias=m_rel_bias, w_ret_out=m_w_ret_out,
             w_att_out=m_w_att_out, w_out=m_w_out, norm_ffn=m_norm_ffn, w_ffn_gate=m_w_ffn_gate, w_ffn_up=m_w_ffn_up,
             w_ffn_down=m_w_ffn_down, norm_final=m_norm_final)
    v = dict(norm_mix=v_norm_mix, w_in=v_w_in, b_gate=v_b_gate, rel_bias=v_rel_bias, w_ret_out=v_w_ret_out,
             w_att_out=v_w_att_out, w_out=v_w_out, norm_ffn=v_norm_ffn, w_ffn_gate=v_w_ffn_gate, w_ffn_up=v_w_ffn_up,
             w_ffn_down=v_w_ffn_down, norm_final=v_norm_final)
    order = ("norm_mix", "w_in", "b_gate", "rel_bias", "w_ret_out", "w_att_out", "w_out", "norm_ffn",
             "w_ffn_gate", "w_ffn_up", "w_ffn_down", "norm_final")
    small_names = ("norm_mix", "b_gate", "rel_bias", "norm_ffn", "norm_final")

    def flat(a):
        return a[0] if a.ndim == 3 else a.reshape(-1, a.shape[-1])

    grad, delta, new_m, new_v = {}, {}, {}, {}
    for nm in order:
        if nm in transposed:
            res = _adamw(w[nm][0].T, slots[transposed[nm]], m[nm][0].T, v[nm][0].T, "adamw_" + nm)
            grad[nm], delta[nm], new_m[nm], new_v[nm] = (a.T[None] for a in res)
        elif nm in plain:
            res = _adamw(flat(w[nm]), slots[plain[nm]], flat(m[nm]), flat(v[nm]), "adamw_" + nm)
            grad[nm], delta[nm], new_m[nm], new_v[nm] = (a.reshape(w[nm].shape) for a in res)
    ds, nms, nvs = _adamw_small([flat(w[nm]) for nm in small_names], [g[nm] for nm in small_names],
                                [flat(m[nm]) for nm in small_names], [flat(v[nm]) for nm in small_names])
    for i, nm in enumerate(small_names):
        grad[nm], delta[nm], new_m[nm], new_v[nm] = (a.reshape(w[nm].shape) for a in (g[nm], ds[i], nms[i], nvs[i]))

    return (loss, dx, *[grad[nm] for nm in order], *[delta[nm] for nm in order],
            *[new_m[nm] for nm in order], *[new_v[nm] for nm in order])
```

```python
import numpy as np
import jax
import jax.numpy as jnp
from jax import lax
from jax.experimental import pallas as pl
from jax.experimental.pallas import tpu as pltpu

F32 = jnp.float32
BF16 = jnp.bfloat16
MESH = pl.DeviceIdType.MESH

D_MODEL = 1024
CHUNK = 64
RET_HEADS = 4
RET_KEY_DIM = 128
RET_VAL_DIM = 256
ATT_HEADS = 8
BAND_CHUNKS = 8
MAX_REL = 256
N_REL = CHUNK + MAX_REL
D_FF = 2816
N_IN = 6656
ROPE_BASE = 10000.0
EPS = 1e-6
NEG_INF = -1e30
C_RQ, C_RK, C_RV, C_RG, C_AQ, C_AK, C_AV, C_GL = 0, 512, 1024, 2048, 3072, 3584, 4096, 4608

ADAM_LR = 0.001
ADAM_B1 = 0.9
ADAM_B2 = 0.999
ADAM_EPS = 1e-08
ADAM_WD = 0.01
ADAM_STEP = 10

N_DEV = 8
LANES = 128
RET_TILE = 256
ATT_Q = 256
ATT_PAD = BAND_CHUNKS * CHUNK
ATT_WIN = ATT_PAD + ATT_Q
ATT_STARTS = ATT_PAD // ATT_Q
ATT_ROWS = 32
BIAS_LEN = 1024
VMEM_LIMIT = 48 * 1024 * 1024
VMEM_LIMIT_ATT_BWD = 56 * 1024 * 1024


def _params(*sem, vmem=VMEM_LIMIT):
    return pltpu.CompilerParams(dimension_semantics=sem, vmem_limit_bytes=vmem)


def _dot(a, b):
    return lax.dot_general(a, b, (((1,), (0,)), ((), ())), preferred_element_type=F32)


def _dot_nt(a, b):
    return lax.dot_general(a, b, (((1,), (1,)), ((), ())), preferred_element_type=F32)


def _dot_tn(a, b):
    return lax.dot_general(a, b, (((0,), (0,)), ((), ())), preferred_element_type=F32)


def _sigmoid(x):
    return 1.0 / (1.0 + jnp.exp(-x))


def _rms_bwd(x, g, dy):
    r = lax.rsqrt(jnp.mean(x * x, axis=-1, keepdims=True) + EPS)
    u = dy * g
    dx = r * u - x * (r * r * r) * jnp.mean(u * x, axis=-1, keepdims=True)
    return dx, dy * x * r


def _place():
    return lax.axis_index("x"), lax.axis_index("y"), lax.axis_index("c")


def _peer(k):
    x, y, c = _place()
    return ((1 - x) if k & 4 else x, (1 - y) if k & 2 else y, (1 - c) if k & 1 else c)


def _index(place):
    return 4 * place[0] + 2 * place[1] + place[2]


def _rows(ref, block, nrows):
    align = 16 if ref.dtype == BF16 else 8
    return ref.at[pl.ds(pl.multiple_of(block * nrows, align), nrows)]


class _Scatter:
    def __init__(self, arrays):
        self.arrays, self.n = list(arrays), len(arrays)

    def out_shape(self):
        return [jax.ShapeDtypeStruct((N_DEV, a.shape[0] // N_DEV) + a.shape[1:], a.dtype) for a in self.arrays]

    def scratch(self):
        return [pltpu.SemaphoreType.DMA((self.n, N_DEV - 1)), pltpu.SemaphoreType.DMA((self.n, N_DEV - 1)),
                pltpu.SemaphoreType.DMA((self.n,))]

    def _copies(self, ins, outs, sems):
        send_sems, recv_sems, local_sems = sems
        me = _index(_place())

        def src(w, to):
            return _rows(ins[w], to, ins[w].shape[0] // N_DEV)

        def dst(w, origin):
            return outs[w].at[origin]

        def remote(w, k, to, origin):
            return pltpu.make_async_remote_copy(src_ref=src(w, to), dst_ref=dst(w, origin),
                                                send_sem=send_sems.at[w, k - 1], recv_sem=recv_sems.at[w, k - 1],
                                                device_id=_peer(k), device_id_type=MESH)

        pairs = [(w, k) for w in range(self.n) for k in range(1, N_DEV)]
        own = lambda: [pltpu.make_async_copy(src(w, me), dst(w, me), local_sems.at[w]) for w in range(self.n)]
        sent = lambda: [remote(w, k, _index(_peer(k)), me) for w, k in pairs]
        arriving = lambda: [remote(w, k, me, _index(_peer(k))) for w, k in pairs]
        return own, sent, arriving

    def start(self, ins, outs, sems):
        own, sent, _ = self._copies(ins, outs, sems)
        for cp in own() + sent():
            cp.start()

    def wait(self, ins, outs, sems):
        own, sent, arriving = self._copies(ins, outs, sems)
        for cp in arriving():
            cp.wait_recv()
        for cp in sent():
            cp.wait_send()
        for cp in own():
            cp.wait()


class _PairSwap:
    def __init__(self, arrays):
        self.arrays, self.n = list(arrays), len(arrays)

    def out_shape(self):
        return [jax.ShapeDtypeStruct((4, a.shape[0] // N_DEV) + a.shape[1:], a.dtype) for a in self.arrays]

    def scratch(self):
        return [pltpu.SemaphoreType.DMA((self.n, 4)), pltpu.SemaphoreType.DMA((self.n, 4))]

    def _copies(self, ins, outs, sems):
        send_sems, recv_sems = sems
        x, y, c = _place()
        return [pltpu.make_async_remote_copy(
            src_ref=_rows(ins[w], 2 * q + 1 - c, ins[w].shape[0] // N_DEV), dst_ref=outs[w].at[q],
            send_sem=send_sems.at[w, q], recv_sem=recv_sems.at[w, q],
            device_id=(x, y, 1 - c), device_id_type=MESH) for w in range(self.n) for q in range(4)]

    def start(self, ins, outs, sems):
        for cp in self._copies(ins, outs, sems):
            cp.start()

    def wait(self, ins, outs, sems):
        for cp in self._copies(ins, outs, sems):
            cp.wait()


class _ChipScatter:
    def __init__(self, arrays):
        self.arrays, self.n = list(arrays), len(arrays)

    def out_shape(self):
        return [jax.ShapeDtypeStruct(a.shape, a.dtype) for a in self.arrays]

    def scratch(self):
        return [pltpu.SemaphoreType.DMA((self.n, 3)), pltpu.SemaphoreType.DMA((self.n, 3)),
                pltpu.SemaphoreType.DMA((self.n,))]

    def _copies(self, ins, outs, sems):
        send_sems, recv_sems, local_sems = sems
        x, y, c = _place()
        mine = 2 * x + y
        sent, arriving = [], []
        for w in range(self.n):
            for k in range(1, 4):
                tx, ty = (1 - x) if k & 2 else x, (1 - y) if k & 1 else y
                other = 2 * tx + ty
                sent.append(lambda w=w, k=k, tx=tx, ty=ty, other=other: pltpu.make_async_remote_copy(
                    src_ref=ins[w].at[other], dst_ref=outs[w].at[mine], send_sem=send_sems.at[w, k - 1],
                    recv_sem=recv_sems.at[w, k - 1], device_id=(tx, ty, c), device_id_type=MESH))
                arriving.append(lambda w=w, k=k, tx=tx, ty=ty, other=other: pltpu.make_async_remote_copy(
                    src_ref=ins[w].at[mine], dst_ref=outs[w].at[other], send_sem=send_sems.at[w, k - 1],
                    recv_sem=recv_sems.at[w, k - 1], device_id=(tx, ty, c), device_id_type=MESH))
        own = [lambda w=w: pltpu.make_async_copy(ins[w].at[mine], outs[w].at[mine], local_sems.at[w])
               for w in range(self.n)]
        return own, sent, arriving

    def start(self, ins, outs, sems):
        own, sent, _ = self._copies(ins, outs, sems)
        for cp in own + sent:
            cp().start()

    def wait(self, ins, outs, sems):
        own, sent, arriving = self._copies(ins, outs, sems)
        for cp in arriving:
            cp().wait_recv()
        for cp in sent:
            cp().wait_send()
        for cp in own:
            cp().wait()


class _ChipGather:
    def __init__(self, arrays, parts=None, into=None):
        self.arrays, self.n, self.into = list(arrays), len(arrays), into
        self.parts = parts or [(0, 1)] * self.n

    def out_shape(self):
        return [jax.ShapeDtypeStruct((N_DEV * a.shape[0],) + a.shape[1:], a.dtype) for a in self.arrays]

    def scratch(self):
        return [pltpu.SemaphoreType.DMA((self.n, N_DEV - 1)), pltpu.SemaphoreType.DMA((self.n, N_DEV - 1)),
                pltpu.SemaphoreType.DMA((self.n,))]

    def _parts(self, ins, outs, sems):
        send_sems, recv_sems, local_sems = sems
        x, y, c = _place()
        me, sibling = (x, y, c), (x, y, 1 - c)
        chips = [(1 - x, y), (x, 1 - y), (1 - x, 1 - y)]

        def rows(w, place, whole):
            (index, count), r = self.parts[w], ins[w].shape[0]
            lo, size = (0, r) if whole else (index * (r // count), r // count)
            align = 16 if ins[w].dtype == BF16 else 8
            return outs[w].at[pl.ds(pl.multiple_of(_index(place) * r + lo, align), size)]

        def mine(w, whole):
            (index, count), r = self.parts[w], ins[w].shape[0]
            return ins[w] if whole or count == 1 else ins[w].at[pl.ds(index * (r // count), r // count)]

        def copy(w, k, block, to, own=False):
            whole = k == 0
            return pltpu.make_async_remote_copy(src_ref=mine(w, whole) if own else rows(w, block, whole),
                                                dst_ref=rows(w, block, whole),
                                                send_sem=send_sems.at[w, k], recv_sem=recv_sems.at[w, k],
                                                device_id=to, device_id_type=MESH)

        def local(w):
            return pltpu.make_async_copy(ins[w], rows(w, me, True), local_sems.at[w])

        return me, sibling, chips, c, copy, local, [index == 0 for index, _ in self.parts]

    def start(self, ins, outs, sems):
        me, sibling, chips, c, copy, local, places_own = self._parts(ins, outs, sems)
        for w in range(self.n):
            if places_own[w]:
                local(w).start()
                copy(w, 0, me, sibling, own=True).start()
            for j, chip in enumerate(chips):
                copy(w, 1 + j, me, (*chip, c), own=True).start()

    def wait(self, ins, outs, sems):
        me, sibling, chips, c, copy, local, places_own = self._parts(ins, outs, sems)
        for w in range(self.n):
            for j, chip in enumerate(chips):
                copy(w, 1 + j, (*chip, c), me).wait_recv()
                copy(w, 4 + j, (*chip, c), sibling).start()
        for w in range(self.n):
            if places_own[w]:
                copy(w, 0, sibling, me).wait_recv()
                copy(w, 0, me, sibling, own=True).wait_send()
                local(w).wait()
            for j, chip in enumerate(chips):
                copy(w, 4 + j, (*chip, 1 - c), me).wait_recv()
                copy(w, 1 + j, me, (*chip, c), own=True).wait_send()
                copy(w, 4 + j, (*chip, c), sibling).wait_send()


def _call(body, *, name, grid, in_specs, out_specs, out_shape, scratch=(), semantics, args, exchange=None,
          vmem=VMEM_LIMIT):
    if exchange is None:
        return pl.pallas_call(body, name=name, grid=grid, in_specs=in_specs, out_specs=out_specs, out_shape=out_shape,
                              scratch_shapes=list(scratch),
                              compiler_params=_params(*semantics, vmem=vmem))(*args), None
    n_in, n_out, n_scr, nx = len(in_specs), len(out_specs), len(scratch), exchange.n
    into = list(getattr(exchange, "into", None) or [])

    def full_body(*refs):
        ins, refs = refs[:n_in], refs[n_in:]
        x_in, refs = refs[:nx], refs[nx + len(into):]
        outs, refs = refs[:n_out], refs[n_out:]
        x_out, refs = refs[:nx], refs[nx:]
        scr, sems = refs[:n_scr], refs[n_scr:]
        first, last = True, True
        for axis, size in enumerate(grid):
            first = jnp.logical_and(first, pl.program_id(axis) == 0)
            last = jnp.logical_and(last, pl.program_id(axis) == size - 1)
        if grid:
            pl.when(first)(lambda: exchange.start(x_in, x_out, sems))
        else:
            exchange.start(x_in, x_out, sems)
        body(*ins, *outs, *scr)
        if grid:
            pl.when(last)(lambda: exchange.wait(x_in, x_out, sems))
        else:
            exchange.wait(x_in, x_out, sems)

    hbm = pl.BlockSpec(memory_space=pltpu.HBM)
    res = pl.pallas_call(
        full_body, name=name, grid=grid,
        in_specs=list(in_specs) + [hbm] * (nx + len(into)), out_specs=list(out_specs) + [hbm] * nx,
        out_shape=list(out_shape) + exchange.out_shape(),
        scratch_shapes=list(scratch) + exchange.scratch(),
        input_output_aliases={n_in + nx + w: n_out + w for w in range(len(into))},
        compiler_params=_params(*(["arbitrary"] * len(grid)), vmem=vmem),
    )(*args, *exchange.arrays, *into)
    return res[:n_out], res[n_out:]


def _alone(exchange, name):
    return _call(lambda: None, name=name, grid=(), in_specs=[], out_specs=[], out_shape=[], semantics=(),
                 args=(), exchange=exchange)[1]


def _mm(a, b, *, ta=False, tb=False, out_dtype, tm, tn, tk, name, exchange=None):
    m, k = (a.shape[1], a.shape[0]) if ta else a.shape
    n = b.shape[0] if tb else b.shape[1]
    assert k == (b.shape[1] if tb else b.shape[0])
    tm, tn, tk = min(tm, m), min(tn, n), min(tk, k)
    assert m % tm == 0 and n % tn == 0 and k % tk == 0, (name, m, n, k)
    nk = k // tk
    dims = (((0 if ta else 1,), (1 if tb else 0,)), ((), ()))

    def body(a_ref, b_ref, o_ref, *acc):
        prod = lax.dot_general(a_ref[...].astype(BF16), b_ref[...].astype(BF16), dims, preferred_element_type=F32)
        if nk == 1:
            o_ref[...] = prod.astype(o_ref.dtype)
            return
        acc_ref, kk = acc[0], pl.program_id(2)

        @pl.when(kk == 0)
        def _():
            acc_ref[...] = prod

        @pl.when((kk > 0) & (kk < nk - 1))
        def _():
            acc_ref[...] += prod

        @pl.when(kk == nk - 1)
        def _():
            o_ref[...] = (acc_ref[...] + prod).astype(o_ref.dtype)

    a_spec = (pl.BlockSpec((tk, tm), lambda i, j, kk: (kk, i)) if ta
              else pl.BlockSpec((tm, tk), lambda i, j, kk: (i, kk)))
    b_spec = (pl.BlockSpec((tn, tk), lambda i, j, kk: (j, kk)) if tb
              else pl.BlockSpec((tk, tn), lambda i, j, kk: (kk, j)))
    (out,), moved = _call(
        body, name=name, grid=(m // tm, n // tn, nk),
        in_specs=[a_spec, b_spec],
        out_specs=[pl.BlockSpec((tm, tn), lambda i, j, kk: (i, j))],
        out_shape=[pltpu.HBM((m, n), out_dtype)],
        scratch=[pltpu.VMEM((tm, tn), F32)] if nk > 1 else [],
        semantics=("parallel", "parallel", "arbitrary"), args=(a, b), exchange=exchange)
    return out if exchange is None else (out, moved)


def _mm_pieces(pieces, b, *, ta, out_dtype, tm, tn, tk, name, exchange=None):
    rows, n = pieces[0].shape[0], b.shape[1]
    step = tm if ta else tk
    assert all(p.shape[0] == rows and p.shape[1] % step == 0 for p in pieces), name
    edges = [int(e) for e in np.cumsum([0] + [p.shape[1] // step for p in pieces])]
    total = edges[-1] * step
    m, k = (total, rows) if ta else (rows, total)
    assert b.shape[0] == k and m % tm == 0 and n % tn == 0 and k % tk == 0, name
    nk, npieces = k // tk, len(pieces)
    dims = (((0 if ta else 1,), (0,)), ((), ()))
    b_resident = ta and n == tn

    def body(*refs):
        a_refs, (b_ref, o_ref, acc_ref) = refs[:npieces], refs[npieces:]
        kk = pl.program_id(2)
        pos = pl.program_id(0) if ta else kk

        @pl.when(kk == 0)
        def _():
            acc_ref[...] = jnp.zeros_like(acc_ref)

        def b_tile():
            return b_ref[pl.ds(pl.multiple_of(kk * tk, tk), tk), :] if b_resident else b_ref[...]

        for p, a_ref in enumerate(a_refs):
            @pl.when((pos >= edges[p]) & (pos < edges[p + 1]))
            def _(a_ref=a_ref):
                acc_ref[...] += lax.dot_general(a_ref[...], b_tile(), dims, preferred_element_type=F32)

        @pl.when(kk == nk - 1)
        def _():
            o_ref[...] = acc_ref[...].astype(o_ref.dtype)

    def a_spec(p):
        lo, last = edges[p], edges[p + 1] - edges[p] - 1
        if ta:
            def index(i, j, kk):
                inside = (i >= lo) & (i <= lo + last)
                return jnp.where(inside, kk, 0), jnp.clip(i - lo, 0, last)
            return pl.BlockSpec((tk, tm), index)
        return pl.BlockSpec((tm, tk), lambda i, j, kk: (i, jnp.clip(kk - lo, 0, last)))

    (out,), moved = _call(
        body, name=name, grid=(m // tm, n // tn, nk),
        in_specs=[a_spec(p) for p in range(npieces)]
                 + [pl.BlockSpec(b.shape, lambda i, j, kk: (0, 0)) if b_resident
                    else pl.BlockSpec((tk, tn), lambda i, j, kk: (kk, j))],
        out_specs=[pl.BlockSpec((tm, tn), lambda i, j, kk: (i, j))],
        out_shape=[pltpu.HBM((m, n), out_dtype)],
        scratch=[pltpu.VMEM((tm, tn), F32)],
        semantics=("parallel", "parallel", "arbitrary"), args=(*pieces, b), exchange=exchange)
    return out if exchange is None else (out, moved)


def _rms_fwd(x2, g, exchange):
    t = x2.shape[0]
    tm = min(512, t)

    def body(x_ref, g_ref, o_ref):
        x = x_ref[...]
        r = lax.rsqrt(jnp.mean(x * x, axis=-1, keepdims=True) + EPS)
        o_ref[...] = (x * r * g_ref[...]).astype(o_ref.dtype)

    return _call(
        body, name="rms_in_fwd", grid=(t // tm,),
        in_specs=[pl.BlockSpec((tm, D_MODEL), lambda i: (i, 0)), pl.BlockSpec((1, D_MODEL), lambda i: (0, 0))],
        out_specs=[pl.BlockSpec((tm, D_MODEL), lambda i: (i, 0))],
        out_shape=[jax.ShapeDtypeStruct((t, D_MODEL), BF16)],
        semantics=("parallel",), args=(x2, g), exchange=exchange)


def _decay(lg):
    row = lax.broadcasted_iota(jnp.int32, (RET_TILE, RET_TILE), 0)
    col = lax.broadcasted_iota(jnp.int32, (RET_TILE, RET_TILE), 1)
    within = jnp.exp(lg * jnp.abs(row - col).astype(F32))
    inside = jnp.where((col >> 6) <= (row >> 6), within, 0.0)
    pos = lax.broadcasted_iota(jnp.int32, (RET_TILE, 1), 0).astype(F32)
    q_dec = jnp.exp(lg * (pos + 1.0))
    k_dec = jnp.exp(lg * (RET_TILE - 1.0 - pos))
    tile_dec = jnp.exp(lg * float(RET_TILE))
    return inside, q_dec, k_dec, tile_dec


def _scaled(a_bf16, dec):
    return (a_bf16.astype(F32) * dec).astype(BF16)


def _ret_specs(seq):
    key = lambda base: pl.BlockSpec((seq, RET_KEY_DIM), lambda b, h: (b, base // RET_KEY_DIM + h))
    val = lambda base: pl.BlockSpec((seq, RET_VAL_DIM), lambda b, h: (b, base // RET_VAL_DIM + h))
    tab = pl.BlockSpec((seq, RET_KEY_DIM), lambda b, h: (0, 0))
    lgs = pl.BlockSpec((None, 1, LANES), lambda b, h: (h, 0, 0))
    return key, val, tab, lgs


def _ret_fwd(proj, cs, sn, lg_arr, batch, seq, exchange):
    t = batch * seq
    nt = seq // RET_TILE

    def body(q_ref, k_ref, v_ref, rg_ref, cs_ref, sn_ref, lg_ref, gro_ref, o_ref, qr_ref, kr_ref):
        lg = lg_ref[:, 0:1]
        cs_t, sn_t = cs_ref[...], sn_ref[...]
        q = q_ref[...].astype(F32)
        k = k_ref[...].astype(F32)
        qr_ref[...] = (q * cs_t + pltpu.roll(q, 64, 1) * sn_t).astype(BF16)
        kr_ref[...] = ((k * cs_t + pltpu.roll(k, 64, 1) * sn_t) * (RET_KEY_DIM ** -0.5)).astype(BF16)
        inside, q_dec, k_dec, tile_dec = _decay(lg)
        state = jnp.zeros((RET_KEY_DIM, RET_VAL_DIM), F32)
        for i in range(nt):
            rows = slice(i * RET_TILE, (i + 1) * RET_TILE)
            qi, ki, vi = qr_ref[rows, :], kr_ref[rows, :], v_ref[rows, :]
            acc = _dot((_dot_nt(qi, ki) * inside).astype(BF16), vi)
            if i > 0:
                acc = acc + _dot(_scaled(qi, q_dec), state.astype(BF16))
            if i < nt - 1:
                state = state * tile_dec + _dot_tn(_scaled(ki, k_dec), vi)
            o_ref[rows, :] = acc
            xc = acc - jnp.mean(acc, axis=-1, keepdims=True)
            nrm = xc * lax.rsqrt(jnp.mean(xc * xc, axis=-1, keepdims=True) + EPS)
            rg = rg_ref[rows, :].astype(F32)
            gro_ref[rows, :] = (rg * _sigmoid(rg) * nrm).astype(BF16)

    key, val, tab, lgs = _ret_specs(seq)
    return _call(
        body, name="ret_fwd", grid=(batch, RET_HEADS),
        in_specs=[key(C_RQ), key(C_RK), val(C_RV), val(C_RG), tab, tab, lgs],
        out_specs=[val(0), val(0), key(0), key(0)],
        out_shape=[jax.ShapeDtypeStruct((t, RET_HEADS * RET_VAL_DIM), BF16),
                   jax.ShapeDtypeStruct((t, RET_HEADS * RET_VAL_DIM), F32),
                   jax.ShapeDtypeStruct((t, RET_HEADS * RET_KEY_DIM), BF16),
                   jax.ShapeDtypeStruct((t, RET_HEADS * RET_KEY_DIM), BF16)],
        semantics=("parallel", "parallel"), args=(proj, proj, proj, proj, cs, sn, lg_arr), exchange=exchange)


def _att_bias(w_ref, bias_ref):
    n_i = lax.broadcasted_iota(jnp.int32, (ATT_Q, BIAS_LEN), 0)
    qc = lax.broadcasted_iota(jnp.int32, (ATT_Q, ATT_WIN), 0) >> 6
    kc = lax.broadcasted_iota(jnp.int32, (ATT_Q, ATT_WIN), 1) >> 6
    dc = qc + BAND_CHUNKS - kc
    band = (dc >= 0) & (dc <= BAND_CHUNKS)
    key = lax.broadcasted_iota(jnp.int32, (ATT_Q, ATT_WIN), 1)
    for e in range(2):
        xw = jnp.broadcast_to(w_ref[e:e + 1, :], (ATT_Q, BIAS_LEN))
        for bit in range(8):
            xw = jnp.where(((n_i >> bit) & 1) == 1, pltpu.roll(xw, 1 << bit, 1), xw)
        bias = jnp.where(band, xw[:, BIAS_LEN - ATT_WIN:], NEG_INF)
        for first in range(ATT_STARTS):
            bias_ref[first, e] = jnp.where(key + (first * ATT_Q - ATT_PAD) >= 0, bias, NEG_INF)
        bias_ref[ATT_STARTS, e] = bias


ATT_PAIRS = 2
ATT_COLS = ATT_PAIRS * LANES


def _att_specs(batch, seq):
    ni = seq // ATT_Q
    q_spec = pl.BlockSpec((ATT_Q, ATT_COLS), lambda g, b, i: (b * ni + i, C_AQ // ATT_COLS + g))
    k_spec = pl.BlockSpec((seq, ATT_COLS), lambda g, b, i: (b, C_AK // ATT_COLS + g))
    v_spec = pl.BlockSpec((seq, ATT_COLS), lambda g, b, i: (b, C_AV // ATT_COLS + g))
    w_spec = pl.BlockSpec((ATT_PAIRS, 2, BIAS_LEN), lambda g, b, i: (g, 0, 0))
    b_spec = pl.BlockSpec((ATT_PAIRS, ATT_STARTS + 1, 2, ATT_Q, ATT_WIN), lambda g, b, i: (g, 0, 0, 0, 0))
    pad = pltpu.VMEM((seq + ATT_PAD, ATT_COLS), BF16)
    return ni, q_spec, k_spec, v_spec, w_spec, b_spec, pad


def _att_bias_tiles(wvec, exchange):
    (tiles,), moved = _call(
        lambda w_ref, o_ref: _att_bias(w_ref, o_ref), name="att_bias", grid=(ATT_HEADS // 2,),
        in_specs=[pl.BlockSpec((None, 2, BIAS_LEN), lambda hp: (hp, 0, 0))],
        out_specs=[pl.BlockSpec((None, ATT_STARTS + 1, 2, ATT_Q, ATT_WIN), lambda hp: (hp, 0, 0, 0, 0))],
        out_shape=[jax.ShapeDtypeStruct((ATT_HEADS // 2, ATT_STARTS + 1, 2, ATT_Q, ATT_WIN), F32)],
        semantics=("parallel",), args=(wvec,), exchange=exchange)
    return tiles, moved


def _att_pad(src_ref, pad_ref):
    pad_ref[:ATT_PAD, :] = jnp.zeros((ATT_PAD, ATT_COLS), BF16)
    pad_ref[ATT_PAD:, :] = src_ref[...]


def _att_head(q2, sel):
    return jnp.where(sel, q2, jnp.zeros_like(q2)) * 0.125


def _att_softmax_rows(s_ref, bias_ref, rows):
    s = s_ref[rows, :] + bias_ref[rows, :]
    ex = jnp.exp(s - jnp.max(s, axis=-1, keepdims=True))
    return ex, 1.0 / jnp.sum(ex, axis=-1, keepdims=True)


def _att_fwd(proj, bias, batch, seq, exchange):
    ni, q_spec, k_spec, v_spec, _, b_spec, pad = _att_specs(batch, seq)

    def body(q_ref, k_ref, v_ref, bias_ref, o_ref, kp_ref, vp_ref, s_ref, e_ref):
        i = pl.program_id(2)

        @pl.when(i == 0)
        def _():
            _att_pad(k_ref, kp_ref)
            _att_pad(v_ref, vp_ref)

        win = pl.ds(pl.multiple_of(i * ATT_Q, ATT_Q), ATT_WIN)
        lo = lax.broadcasted_iota(jnp.int32, (1, LANES), 1) < 64
        start = jnp.minimum(i, ATT_STARTS)
        for pair in range(ATT_PAIRS):
            cols = slice(pair * LANES, (pair + 1) * LANES)
            k2, v2, q2 = kp_ref[win, cols], vp_ref[win, cols], q_ref[:, cols]
            out = jnp.zeros((ATT_Q, LANES), F32)
            for e in range(2):
                h = 2 * pair + e
                sel = lo if e == 0 else jnp.logical_not(lo)
                s_ref[h] = _dot_nt(_att_head(q2, sel), k2)
                rsum = []
                for c in range(ATT_Q // ATT_ROWS):
                    rows = slice(c * ATT_ROWS, (c + 1) * ATT_ROWS)
                    ex, r = _att_softmax_rows(s_ref.at[h], bias_ref.at[pair, start, e], rows)
                    e_ref[h, rows, :] = ex.astype(BF16)
                    rsum.append(r)
                out = out + _dot(e_ref[h], jnp.where(sel, v2, jnp.zeros_like(v2))) * jnp.concatenate(rsum, axis=0)
            o_ref[:, cols] = out.astype(BF16)

    heads = 2 * ATT_PAIRS
    return _call(
        body, name="att_fwd", grid=(ATT_HEADS // heads, batch, ni),
        in_specs=[q_spec, k_spec, v_spec, b_spec],
        out_specs=[pl.BlockSpec((ATT_Q, ATT_COLS), lambda g, b, i: (b * ni + i, g))],
        out_shape=[jax.ShapeDtypeStruct((batch * seq, ATT_HEADS * 64), BF16)],
        scratch=[pad, pad, pltpu.VMEM((heads, ATT_Q, ATT_WIN), F32), pltpu.VMEM((heads, ATT_Q, ATT_WIN), BF16)],
        semantics=("arbitrary", "arbitrary", "arbitrary"), args=(proj, proj, proj, bias), exchange=exchange)


GL_HALF = 512


def _gl_specs(tm):
    return [pl.BlockSpec((tm, GL_HALF), lambda i, c=C_GL // GL_HALF + j: (i, c)) for j in range(4)]


def _gates(gl_refs, b_ref):
    logits = [ref[...].astype(F32) for ref in gl_refs]
    gr = _sigmoid(jnp.concatenate(logits[:2], axis=1) + b_ref[:, :D_MODEL])
    ga = _sigmoid(jnp.concatenate(logits[2:], axis=1) + b_ref[:, D_MODEL:])
    return gr, ga


def _whole(a):
    return pl.BlockSpec(a.shape, lambda i: (0,) * a.ndim)


def _mix_out_fwd(gro, ao, proj, b_gate, w_ret, w_att_t, x2, w_out, g2):
    t = gro.shape[0]
    tm = min(512, t)

    def body(gro_ref, ao_ref, gl0, gl1, gl2, gl3, b_ref, wr_ref, wa_ref, x_ref, wo_ref, g_ref,
             z_ref, yr_ref, ya_ref, h_ref, hn_ref):
        yr = _dot(gro_ref[...], wr_ref[...])
        ya = _dot_nt(ao_ref[...], wa_ref[...])
        yr_ref[...] = yr.astype(BF16)
        ya_ref[...] = ya.astype(BF16)
        gr, ga = _gates((gl0, gl1, gl2, gl3), b_ref)
        z = (gr * yr + ga * ya).astype(BF16)
        z_ref[...] = z
        h = x_ref[...] + _dot(z, wo_ref[...])
        h_ref[...] = h
        r = lax.rsqrt(jnp.mean(h * h, axis=-1, keepdims=True) + EPS)
        hn_ref[...] = (h * r * g_ref[...]).astype(BF16)

    row = pl.BlockSpec((tm, D_MODEL), lambda i: (i, 0))
    return pl.pallas_call(
        body, name="mix_out_fwd", grid=(t // tm,),
        in_specs=[row, pl.BlockSpec((tm, 512), lambda i: (i, 0)), *_gl_specs(tm),
                  _whole(b_gate), _whole(w_ret), _whole(w_att_t), row, _whole(w_out), _whole(g2)],
        out_specs=[row] * 5,
        out_shape=[jax.ShapeDtypeStruct((t, D_MODEL), BF16)] * 3
                  + [jax.ShapeDtypeStruct((t, D_MODEL), F32), jax.ShapeDtypeStruct((t, D_MODEL), BF16)],
        compiler_params=_params("parallel"),
    )(gro, ao, proj, proj, proj, proj, b_gate, w_ret, w_att_t, x2, w_out, g2)


def _col_chunks(width, chunk=384):
    return [slice(lo, min(lo + chunk, width)) for lo in range(0, width, chunk)]


def _ffn_up(hn, wg_t, wu_t):
    t = hn.shape[0]
    tm, tn = min(512, t), D_FF // 2

    def body(h_ref, wg_ref, wu_ref, g_ref, u_ref, a_ref):
        g = _dot_nt(h_ref[...], wg_ref[...])
        u = _dot_nt(h_ref[...], wu_ref[...])
        g_ref[...] = g.astype(BF16)
        u_ref[...] = u.astype(BF16)
        a_ref[...] = (g * _sigmoid(g) * u).astype(BF16)

    w_spec = pl.BlockSpec((tn, D_MODEL), lambda j, i: (j, 0))
    out = pl.BlockSpec((tm, tn), lambda j, i: (i, j))
    return pl.pallas_call(
        body, name="ffn_up", grid=(D_FF // tn, t // tm),
        in_specs=[pl.BlockSpec((tm, D_MODEL), lambda j, i: (i, 0)), w_spec, w_spec],
        out_specs=[out, out, out],
        out_shape=[jax.ShapeDtypeStruct((t, D_FF), BF16)] * 3,
        compiler_params=_params("parallel", "parallel"),
    )(hn, wg_t, wu_t)


def _ffn_down_loss(a, h1, tgt, w_down, g3):
    t = a.shape[0]
    tm = min(512, t)

    def body(a_ref, h_ref, t_ref, w_ref, g_ref, dh_ref, dhb_ref, loss_ref, dg_ref):
        @pl.when(pl.program_id(0) == 0)
        def _():
            loss_ref[...] = jnp.zeros_like(loss_ref)
            dg_ref[...] = jnp.zeros_like(dg_ref)

        g = g_ref[...]
        h2 = h_ref[...] + _dot(a_ref[...], w_ref[...])
        r = lax.rsqrt(jnp.mean(h2 * h2, axis=-1, keepdims=True) + EPS)
        err = h2 * r * g - t_ref[...]
        loss_ref[...] += jnp.sum(err * err) * (0.5 / D_MODEL)
        dy = err * (1.0 / D_MODEL)
        dh, dg_rows = _rms_bwd(h2, g, dy)
        dg_ref[...] += jnp.sum(dg_rows, axis=0, keepdims=True)
        dh_ref[...] = dh
        dhb_ref[...] = dh.astype(BF16)

    row = pl.BlockSpec((tm, D_MODEL), lambda i: (i, 0))
    vec = pl.BlockSpec((1, D_MODEL), lambda i: (0, 0))
    return pl.pallas_call(
        body, name="ffn_down_loss", grid=(t // tm,),
        in_specs=[pl.BlockSpec((tm, D_FF), lambda i: (i, 0)), row, row,
                  pl.BlockSpec((D_FF, D_MODEL), lambda i: (0, 0)), vec],
        out_specs=[row, row, pl.BlockSpec((1, LANES), lambda i: (0, 0)), vec],
        out_shape=[jax.ShapeDtypeStruct((t, D_MODEL), F32), jax.ShapeDtypeStruct((t, D_MODEL), BF16),
                   jax.ShapeDtypeStruct((1, LANES), F32), jax.ShapeDtypeStruct((1, D_MODEL), F32)],
        compiler_params=_params("arbitrary"),
    )(a, h1, tgt, w_down, g3)


def _ffn_bwd_act(dh2b, w_down, g_act, u_act, exchange):
    t = dh2b.shape[0]
    tm, tn = min(512, t), D_FF // 2

    def body(d_ref, w_ref, g_ref, u_ref, dg_ref, du_ref):
        d = d_ref[...]
        for cols in _col_chunks(tn):
            da = _dot_nt(d, w_ref[cols, :])
            g = g_ref[:, cols].astype(F32)
            u = u_ref[:, cols].astype(F32)
            sg = _sigmoid(g)
            dg_ref[:, cols] = (da * u * sg * (1.0 + g * (1.0 - sg))).astype(BF16)
            du_ref[:, cols] = (da * g * sg).astype(BF16)

    blk = pl.BlockSpec((tm, tn), lambda j, i: (i, j))
    return _call(
        body, name="ffn_bwd_act", grid=(D_FF // tn, t // tm),
        in_specs=[pl.BlockSpec((tm, D_MODEL), lambda j, i: (i, 0)),
                  pl.BlockSpec((tn, D_MODEL), lambda j, i: (j, 0)), blk, blk],
        out_specs=[blk, blk],
        out_shape=[jax.ShapeDtypeStruct((t, D_FF), BF16)] * 2,
        semantics=("parallel", "parallel"), args=(dh2b, w_down, g_act, u_act), exchange=exchange)


def _ffn_bwd_in(dg, du, wg_t, wu_t, h1, dh2, g2, exchange):
    t = dg.shape[0]
    tm = min(256, t)

    def body(dg_ref, du_ref, wg_ref, wu_ref, h_ref, d2_ref, g_ref, dh_ref, dhb_ref, gn_ref):
        @pl.when(pl.program_id(0) == 0)
        def _():
            gn_ref[...] = jnp.zeros_like(gn_ref)

        dhn = _dot(dg_ref[...], wg_ref[...]) + _dot(du_ref[...], wu_ref[...])
        dx, dg_rows = _rms_bwd(h_ref[...], g_ref[...], dhn)
        dh = d2_ref[...] + dx
        dh_ref[...] = dh
        dhb_ref[...] = dh.astype(BF16)
        gn_ref[...] += jnp.sum(dg_rows, axis=0, keepdims=True)

    act = pl.BlockSpec((tm, D_FF), lambda i: (i, 0))
    row = pl.BlockSpec((tm, D_MODEL), lambda i: (i, 0))
    return _call(
        body, name="ffn_bwd_in", grid=(t // tm,),
        in_specs=[act, act, _whole(wg_t), _whole(wu_t), row, row, _whole(g2)],
        out_specs=[row, row, _whole(g2)],
        out_shape=[jax.ShapeDtypeStruct((t, D_MODEL), F32), jax.ShapeDtypeStruct((t, D_MODEL), BF16),
                   jax.ShapeDtypeStruct((1, D_MODEL), F32)],
        semantics=("arbitrary",), args=(dg, du, wg_t, wu_t, h1, dh2, g2), exchange=exchange)


def _mix_bwd(dh1b, w_out, proj, b_gate, y_ret, y_att, w_ret, w_att_t):
    t = dh1b.shape[0]
    tm = min(512, t)

    def body(d_ref, wo_ref, gl0, gl1, gl2, gl3, b_ref, yr_ref, ya_ref, wr_ref, wa_ref,
             dyr_ref, dya_ref, dglr_ref, dgla_ref, dgro_ref, dao_ref, db_ref):
        @pl.when(pl.program_id(0) == 0)
        def _():
            db_ref[...] = jnp.zeros_like(db_ref)

        dz = _dot_nt(d_ref[...], wo_ref[...])
        gr, ga = _gates((gl0, gl1, gl2, gl3), b_ref)
        dyr = (dz * gr).astype(BF16)
        dya = (dz * ga).astype(BF16)
        dyr_ref[...] = dyr
        dya_ref[...] = dya
        dglr = dz * yr_ref[...].astype(F32) * gr * (1.0 - gr)
        dgla = dz * ya_ref[...].astype(F32) * ga * (1.0 - ga)
        dglr_ref[...] = dglr.astype(BF16)
        dgla_ref[...] = dgla.astype(BF16)
        db_ref[:, :D_MODEL] += jnp.sum(dglr, axis=0, keepdims=True)
        db_ref[:, D_MODEL:] += jnp.sum(dgla, axis=0, keepdims=True)
        dgro_ref[...] = _dot_nt(dyr, wr_ref[...]).astype(BF16)
        dao_ref[...] = _dot(dya, wa_ref[...]).astype(BF16)

    row = pl.BlockSpec((tm, D_MODEL), lambda i: (i, 0))
    half = pl.BlockSpec((tm, 512), lambda i: (i, 0))
    return pl.pallas_call(
        body, name="mix_bwd", grid=(t // tm,),
        in_specs=[row, _whole(w_out), *_gl_specs(tm), _whole(b_gate), row, row, _whole(w_ret), _whole(w_att_t)],
        out_specs=[row, row, row, row, row, half, _whole(b_gate)],
        out_shape=[jax.ShapeDtypeStruct((t, D_MODEL), BF16)] * 5
                  + [jax.ShapeDtypeStruct((t, 512), BF16), jax.ShapeDtypeStruct(b_gate.shape, F32)],
        compiler_params=_params("arbitrary"),
    )(dh1b, w_out, proj, proj, proj, proj, b_gate, y_ret, y_att, w_ret, w_att_t)


def _ret_bwd(dgro, proj, o_ret, qr, kr, cs, sn, lg_arr, batch, seq, exchange):
    t = batch * seq
    nt = seq // RET_TILE

    def body(dgro_ref, rg_ref, o_ref, qr_ref, kr_ref, v_ref, cs_ref, sn_ref, lg_ref,
             dq_ref, dk_ref, dv_ref, drg_ref, do_ref, st_ref):
        lg = lg_ref[:, 0:1]
        inside, q_dec, k_dec, tile_dec = _decay(lg)

        state = jnp.zeros((RET_KEY_DIM, RET_VAL_DIM), F32)
        for i in range(nt - 1):
            rows = slice(i * RET_TILE, (i + 1) * RET_TILE)
            state = state * tile_dec + _dot_tn(_scaled(kr_ref[rows, :], k_dec), v_ref[rows, :])
            st_ref[i + 1] = state.astype(BF16)

        for i in range(nt):
            rows = slice(i * RET_TILE, (i + 1) * RET_TILE)
            o = o_ref[rows, :]
            xc = o - jnp.mean(o, axis=-1, keepdims=True)
            rs = lax.rsqrt(jnp.mean(xc * xc, axis=-1, keepdims=True) + EPS)
            nrm = xc * rs
            rg = rg_ref[rows, :].astype(F32)
            sg = _sigmoid(rg)
            dg = dgro_ref[rows, :].astype(F32)
            drg_ref[rows, :] = (dg * nrm * sg * (1.0 + rg * (1.0 - sg))).astype(BF16)
            dn = dg * rg * sg
            do = rs * (dn - jnp.mean(dn, axis=-1, keepdims=True)
                       - nrm * jnp.mean(dn * nrm, axis=-1, keepdims=True))
            do_ref[rows, :] = do.astype(BF16)

        dstate = jnp.zeros((RET_KEY_DIM, RET_VAL_DIM), F32)
        for i in reversed(range(nt)):
            rows = slice(i * RET_TILE, (i + 1) * RET_TILE)
            qi, ki, vi, doi = qr_ref[rows, :], kr_ref[rows, :], v_ref[rows, :], do_ref[rows, :]
            p = (_dot_nt(qi, ki) * inside).astype(BF16)
            dp = (_dot_nt(doi, vi) * inside).astype(BF16)
            dq = _dot(dp, ki)
            dk = _dot_tn(dp, qi)
            dv = _dot_tn(p, doi)
            if i > 0:
                dq = dq + _dot_nt(doi, st_ref[i]) * q_dec
            if i < nt - 1:
                dsb = dstate.astype(BF16)
                dk = dk + _dot_nt(vi, dsb) * k_dec
                dv = dv + _dot(_scaled(ki, k_dec), dsb)
            if i > 0:
                dstate = dstate * tile_dec + _dot_tn(_scaled(qi, q_dec), doi)
            dq_ref[rows, :] = (dq * cs_ref[rows, :] - pltpu.roll(dq, 64, 1) * sn_ref[rows, :]).astype(BF16)
            dk = (dk * cs_ref[rows, :] - pltpu.roll(dk, 64, 1) * sn_ref[rows, :]) * (RET_KEY_DIM ** -0.5)
            dk_ref[rows, :] = dk.astype(BF16)
            dv_ref[rows, :] = dv.astype(BF16)

    key, val, tab, lgs = _ret_specs(seq)
    return _call(
        body, name="ret_bwd", grid=(batch, RET_HEADS),
        in_specs=[val(0), val(C_RG), val(0), key(0), key(0), val(C_RV), tab, tab, lgs],
        out_specs=[key(0), key(0), val(0), val(0)],
        out_shape=[jax.ShapeDtypeStruct((t, RET_HEADS * RET_KEY_DIM), BF16)] * 2
                  + [jax.ShapeDtypeStruct((t, RET_HEADS * RET_VAL_DIM), BF16)] * 2,
        scratch=[pltpu.VMEM((seq, RET_VAL_DIM), BF16), pltpu.VMEM((nt, RET_KEY_DIM, RET_VAL_DIM), BF16)],
        semantics=("parallel", "parallel"), args=(dgro, proj, o_ret, qr, kr, proj, cs, sn, lg_arr),
        exchange=exchange)


def _att_bwd(proj, bias, dao, batch, seq, exchange):
    ni, q_spec, k_spec, v_spec, w_spec, b_spec, pad = _att_specs(batch, seq)
    t = batch * seq

    def body(q_ref, k_ref, v_ref, bias_ref, do_ref, dq_ref, dk_ref, dv_ref, dw_ref,
             dbias_ref, dk_acc, dv_acc, kp_ref, vp_ref, s_ref, dp_ref, e_ref, ds_ref):
        b, i = pl.program_id(1), pl.program_id(2)

        @pl.when((b == 0) & (i == 0))
        def _():
            dbias_ref[...] = jnp.zeros_like(dbias_ref)

        @pl.when(i == 0)
        def _():
            _att_pad(k_ref, kp_ref)
            _att_pad(v_ref, vp_ref)
            dk_acc[...] = jnp.zeros_like(dk_acc)
            dv_acc[...] = jnp.zeros_like(dv_acc)

        win = pl.ds(pl.multiple_of(i * ATT_Q, ATT_Q), ATT_WIN)
        lo = lax.broadcasted_iota(jnp.int32, (1, LANES), 1) < 64
        start = jnp.minimum(i, ATT_STARTS)
        for pair in range(ATT_PAIRS):
            cols = slice(pair * LANES, (pair + 1) * LANES)
            k2, v2, q2, do2 = kp_ref[win, cols], vp_ref[win, cols], q_ref[:, cols], do_ref[:, cols]
            dq = jnp.zeros((ATT_Q, LANES), F32)
            dk = jnp.zeros((LANES, ATT_WIN), F32)
            dv = jnp.zeros((LANES, ATT_WIN), F32)
            for e in range(2):
                h = 2 * pair + e
                sel = lo if e == 0 else jnp.logical_not(lo)
                qm = _att_head(q2, sel)
                dom = jnp.where(sel, do2, jnp.zeros_like(do2))
                s_ref[h] = _dot_nt(qm, k2)
                dp_ref[h] = _dot_nt(dom, v2)
                rsum = []
                for c in range(ATT_Q // ATT_ROWS):
                    rows = slice(c * ATT_ROWS, (c + 1) * ATT_ROWS)
                    ex, r = _att_softmax_rows(s_ref.at[h], bias_ref.at[pair, start, e], rows)
                    dp = dp_ref[h, rows, :]
                    mean = jnp.sum(dp * ex, axis=-1, keepdims=True) * r
                    ds = ex * ((dp - mean) * r)
                    dbias_ref[h, rows, :] += ds
                    ds_ref[h, rows, :] = ds.astype(BF16)
                    e_ref[h, rows, :] = ex.astype(BF16)
                    rsum.append(r)
                dq = dq + _dot(ds_ref[h], jnp.where(sel, k2, jnp.zeros_like(k2)))
                dk = dk + _dot_tn(qm, ds_ref[h])
                dv = dv + _dot_tn((dom.astype(F32) * jnp.concatenate(rsum, axis=0)).astype(BF16), e_ref[h])
            dq_ref[:, cols] = (dq * 0.125).astype(BF16)
            dk_acc[cols, win] += dk
            dv_acc[cols, win] += dv

        @pl.when(i == ni - 1)
        def _():
            dk_ref[...] = dk_acc[:, ATT_PAD:].T.astype(BF16)
            dv_ref[...] = dv_acc[:, ATT_PAD:].T.astype(BF16)

        @pl.when((b == batch - 1) & (i == ni - 1))
        def _():
            n_i = lax.broadcasted_iota(jnp.int32, (ATT_Q, BIAS_LEN), 0)
            for h in range(heads):
                xw = jnp.concatenate([jnp.zeros((ATT_Q, BIAS_LEN - ATT_WIN), F32), dbias_ref[h]], axis=1)
                for bit in range(8):
                    xw = jnp.where(((n_i >> bit) & 1) == 1, pltpu.roll(xw, BIAS_LEN - (1 << bit), 1), xw)
                dw_ref[h // 2, h % 2:h % 2 + 1, :] = jnp.sum(xw, axis=0, keepdims=True)

    heads = 2 * ATT_PAIRS
    seq_blk = pl.BlockSpec((seq, ATT_COLS), lambda g, b, i: (b, g))
    q_out = pl.BlockSpec((ATT_Q, ATT_COLS), lambda g, b, i: (b * ni + i, g))
    tile_f32, tile_bf16 = pltpu.VMEM((heads, ATT_Q, ATT_WIN), F32), pltpu.VMEM((heads, ATT_Q, ATT_WIN), BF16)
    acc = pltpu.VMEM((ATT_COLS, seq + ATT_PAD), F32)
    return _call(
        body, name="att_bwd", grid=(ATT_HEADS // heads, batch, ni),
        in_specs=[q_spec, k_spec, v_spec, b_spec, q_out],
        out_specs=[q_out, seq_blk, seq_blk, w_spec],
        out_shape=[jax.ShapeDtypeStruct((t, 512), BF16)] * 3
                  + [jax.ShapeDtypeStruct((ATT_HEADS // 2, 2, BIAS_LEN), F32)],
        scratch=[tile_f32, acc, acc, pad, pad, tile_f32, tile_f32, tile_bf16, tile_bf16],
        semantics=("arbitrary", "arbitrary", "arbitrary"), args=(proj, proj, proj, bias, dao), exchange=exchange,
        vmem=VMEM_LIMIT_ATT_BWD)


def _rms_in_bwd(x2, dxn, dh1, g1):
    t = x2.shape[0]
    tm = min(512, t)

    def body(x_ref, d_ref, h_ref, g_ref, dx_ref, dg_ref):
        @pl.when(pl.program_id(0) == 0)
        def _():
            dg_ref[...] = jnp.zeros_like(dg_ref)

        dx, dg_rows = _rms_bwd(x_ref[...], g_ref[...], d_ref[...])
        dx_ref[...] = h_ref[...] + dx
        dg_ref[...] += jnp.sum(dg_rows, axis=0, keepdims=True)

    row = pl.BlockSpec((tm, D_MODEL), lambda i: (i, 0))
    vec = pl.BlockSpec((1, D_MODEL), lambda i: (0, 0))
    return pl.pallas_call(
        body, name="rms_in_bwd", grid=(t // tm,),
        in_specs=[row, row, row, vec], out_specs=[row, vec],
        out_shape=[jax.ShapeDtypeStruct((t, D_MODEL), F32), jax.ShapeDtypeStruct((1, D_MODEL), F32)],
        compiler_params=_params("arbitrary"),
    )(x2, dxn, dh1, g1)


def _pack_small(dg1, dbr, dba, dg2, dg3, dw, loss):
    def body(a_ref, b_ref, c_ref, d_ref, e_ref, w_ref, l_ref, o_ref):
        o_ref[...] = jnp.zeros_like(o_ref)
        for r, ref in enumerate((a_ref, b_ref, c_ref, d_ref, e_ref)):
            o_ref[r:r + 1, :] = ref[...]
        o_ref[5:6, 0:LANES] = l_ref[...]
        for hp in range(ATT_HEADS // 2):
            o_ref[8 + 2 * hp:10 + 2 * hp, :] = w_ref[hp]

    return pl.pallas_call(body, name="pack_small",
                          out_shape=jax.ShapeDtypeStruct((16, D_MODEL), F32))(dg1, dbr, dba, dg2, dg3, dw, loss)


def _rotary_tables(seq):
    freqs = ROPE_BASE ** (-jnp.arange(0, RET_KEY_DIM, 2, dtype=F32) / RET_KEY_DIM)
    ang = jnp.arange(seq, dtype=F32)[:, None] * freqs[None, :]
    cos, sin = jnp.cos(ang), jnp.sin(ang)
    return jnp.concatenate([cos, cos], axis=1), jnp.concatenate([-sin, sin], axis=1)


def _bias_rows(rel_bias):
    n_far = BIAS_LEN - ATT_Q - MAX_REL + 1
    n_near = BIAS_LEN - n_far - (N_REL - 2)
    w = jnp.concatenate([jnp.broadcast_to(rel_bias[:, N_REL - 1:], (ATT_HEADS, n_far)),
                         rel_bias[:, 1:N_REL - 1][:, ::-1],
                         jnp.broadcast_to(rel_bias[:, :1], (ATT_HEADS, n_near))], axis=1)
    return w.reshape(ATT_HEADS // 2, 2, BIAS_LEN)


def _bias_rows_bwd(dw):
    n_far = BIAS_LEN - ATT_Q - MAX_REL + 1
    mid = dw[:, n_far:n_far + N_REL - 2][:, ::-1]
    return jnp.concatenate([jnp.sum(dw[:, n_far + N_REL - 2:], axis=1, keepdims=True), mid,
                            jnp.sum(dw[:, :n_far], axis=1, keepdims=True)], axis=1)


def _step(x, tgt, norm_mix, b_gate, norm_ffn, norm_final, rel_bias_shard, shard):
    batch, seq, _ = x.shape
    t = batch * seq
    n_rb = rel_bias_shard.shape[-1]
    x2, tgt2 = x.reshape(t, D_MODEL), tgt.reshape(t, D_MODEL)
    g3 = norm_final.reshape(1, D_MODEL)
    cs, sn = _rotary_tables(seq)
    lg = np.log(1.0 - 2.0 ** (-5.0 - np.arange(RET_HEADS, dtype=np.float32))).astype(np.float32)
    lg_arr = jnp.asarray(np.broadcast_to(lg[:, None, None], (RET_HEADS, 1, LANES)))

    def gather(*names):
        return _ChipGather([shard[nm] for nm in names])

    def scatter(*grads):
        return _Scatter(grads)

    rb_pad = jnp.pad(rel_bias_shard, ((0, 0), (0, LANES - n_rb)))
    (xn,), (w_in_half, rb_full) = _rms_fwd(x2, norm_mix,
                                           _ChipGather([shard["w_in_t"], rb_pad], parts=[(0, 2), (0, 1)]))
    rb_full = rb_full.reshape(N_DEV, ATT_HEADS, LANES)[:, :, :n_rb]
    bias, (w_in_t,) = _att_bias_tiles(
        _bias_rows(jnp.transpose(rb_full, (1, 0, 2)).reshape(ATT_HEADS, N_DEV * n_rb)),
        _ChipGather([shard["w_in_t"]], parts=[(1, 2)], into=[w_in_half]))
    proj, (w_ret, w_att_t, w_out, w_gate_t) = _mm(
        xn, w_in_t, tb=True, out_dtype=BF16, tm=1024, tn=1664, tk=1024, name="proj",
        exchange=gather("w_ret", "w_att_t", "w_out", "w_gate_t"))
    (gro, o_ret, qr, kr), _ = _ret_fwd(proj, cs, sn, lg_arr, batch, seq, None)
    (ao,), (w_up_t, w_down) = _att_fwd(proj, bias, batch, seq, gather("w_up_t", "w_down"))
    z, y_ret, y_att, h1, hn = _mix_out_fwd(gro, ao, proj, b_gate, w_ret, w_att_t, x2, w_out, norm_ffn)
    g_act, u_act, a_act = _ffn_up(hn, w_gate_t, w_up_t)
    dh2, dh2b, loss, dg3 = _ffn_down_loss(a_act, h1, tgt2, w_down, g3)

    wg = dict(out_dtype=BF16, tn=1024, ta=True)
    slots = {}
    dw_down = _mm(a_act, dh2b, tm=1408, tk=2048, name="dw_down", **wg)
    (d_gact, d_uact), _ = _ffn_bwd_act(dh2b, w_down, g_act, u_act, None)
    dw_gate = _mm(d_gact, hn, tm=1408, tk=2048, name="dw_gate", **wg)
    dw_up = _mm(d_uact, hn, tm=1408, tk=2048, name="dw_up", **wg)
    (dh1, dh1b, dg2), (slots["w_down"],) = _ffn_bwd_in(d_gact, d_uact, w_gate_t, w_up_t, h1, dh2, norm_ffn,
                                                     scatter(dw_down))
    dw_out = _mm(z, dh1b, tm=1024, tk=2048, name="dw_out", **wg)
    dyr, dya, dglr, dgla, dgro, dao, db = _mix_bwd(dh1b, w_out, proj, b_gate, y_ret, y_att, w_ret, w_att_t)
    dw_ret = _mm(gro, dyr, tm=1024, tk=2048, name="dw_ret", **wg)
    dw_att = _mm(dya, ao, tm=1024, tk=2048, name="dw_att", **wg)
    (drq, drk, drv, drg), _ = _ret_bwd(dgro, proj, o_ret, qr, kr, cs, sn, lg_arr, batch, seq, None)
    (daq, dak, dav, dw), (slots["w_gate_t"], slots["w_out"], slots["w_ret"], slots["w_att_t"]) = _att_bwd(
        proj, bias, dao, batch, seq, scatter(dw_gate, dw_out, dw_ret, dw_att))
    dproj = [drq, drk, drv, drg, daq, dak, dav, dglr, dgla]
    dw_in, (slots["w_up_t"],) = _mm_pieces(dproj, xn, ta=True, out_dtype=BF16, tm=512, tn=1024, tk=1024,
                                           name="dw_in", exchange=scatter(dw_up))
    (dw_in_sibling,) = _alone(_PairSwap([dw_in]), "swap_w_in")
    dw_in_pairs = _pair_add(dw_in, dw_in_sibling, "pair_w_in")
    dxn, (slots["w_in_t"],) = _mm_pieces(dproj, w_in_t, ta=False, out_dtype=F32, tm=1024, tn=1024, tk=512, name="dxn",
                                  exchange=_ChipScatter([dw_in_pairs]))
    dx, dg1 = _rms_in_bwd(x2, dxn, dh1, norm_mix)
    small = _pack_small(dg1, db[:, :D_MODEL], db[:, D_MODEL:], dg2, dg3, dw, loss)
    (small_slots,) = _alone(_ChipGather([small]), "gather_small")
    return dx.reshape(batch, seq, D_MODEL), slots, small_slots.reshape(N_DEV, 16, D_MODEL)


def _row_tile(r, c):
    return max(d for d in range(16, r + 1, 16) if r % d == 0 and (d * c <= 256 * 1024 or d == 16))


def _pair_add(grad, got, name):
    _, r, c = got.shape
    tr = r
    core = lax.axis_index("c").astype(jnp.int32).reshape(1)

    def body(core_ref, g_ref, a_ref, o_ref):
        o_ref[...] = (g_ref[...].astype(F32) + a_ref[...].astype(F32)).astype(o_ref.dtype)

    blk = pl.BlockSpec((None, tr, c), lambda q, i, core_ref: (q, i, 0))
    return pl.pallas_call(
        body, name=name,
        grid_spec=pltpu.PrefetchScalarGridSpec(
            num_scalar_prefetch=1, grid=(4, r // tr),
            in_specs=[pl.BlockSpec((None, None, tr, c), lambda q, i, core_ref: (q, core_ref[0], i, 0)), blk],
            out_specs=blk),
        out_shape=jax.ShapeDtypeStruct(got.shape, got.dtype),
        compiler_params=_params("parallel", "parallel"),
    )(core, grad.reshape(4, 2, r, c), got)


def _sum_slots(slots, name):
    n, r, c = slots.shape
    tr = _row_tile(r, c)

    def body(s_ref, o_ref):
        acc = s_ref[0].astype(F32)
        for s in range(1, n):
            acc = acc + s_ref[s].astype(F32)
        o_ref[...] = acc

    return pl.pallas_call(
        body, name=name, grid=(r // tr,),
        in_specs=[pl.BlockSpec((n, tr, c), lambda i: (0, i, 0))],
        out_specs=pl.BlockSpec((tr, c), lambda i: (i, 0)),
        out_shape=jax.ShapeDtypeStruct((r, c), F32),
        compiler_params=_params("parallel"),
    )(slots)


def _adamw_math(w, g, m, v):
    m = ADAM_B1 * m + (1.0 - ADAM_B1) * g
    v = ADAM_B2 * v + (1.0 - ADAM_B2) * (g * g)
    m_hat = m / (1.0 - ADAM_B1 ** ADAM_STEP)
    v_hat = v / (1.0 - ADAM_B2 ** ADAM_STEP)
    return -ADAM_LR * (m_hat / (jnp.sqrt(v_hat) + ADAM_EPS) + ADAM_WD * w), m, v


def _adamw(w, slots, m, v, name):
    n, r, c = slots.shape
    tr = _row_tile(r, c)

    def body(w_ref, s_ref, m_ref, v_ref, g_ref, d_ref, nm_ref, nv_ref):
        g = s_ref[0].astype(F32)
        for s in range(1, n):
            g = g + s_ref[s].astype(F32)
        g_ref[...] = g
        d_ref[...], nm_ref[...], nv_ref[...] = _adamw_math(w_ref[...], g, m_ref[...], v_ref[...])

    blk = pl.BlockSpec((tr, c), lambda i: (i, 0))
    return pl.pallas_call(
        body, name=name, grid=(r // tr,),
        in_specs=[blk, pl.BlockSpec((n, tr, c), lambda i: (0, i, 0)), blk, blk], out_specs=[blk] * 4,
        out_shape=[jax.ShapeDtypeStruct((r, c), F32)] * 4,
        compiler_params=_params("parallel"),
    )(w, slots, m, v)


def _adamw_small(ws, gs, ms, vs):
    n = len(ws)

    def body(*refs):
        for i in range(n):
            w_ref, g_ref, m_ref, v_ref = (refs[j * n + i] for j in range(4))
            d_ref, nm_ref, nv_ref = (refs[(4 + j) * n + i] for j in range(3))
            d_ref[...], nm_ref[...], nv_ref[...] = _adamw_math(w_ref[...], g_ref[...], m_ref[...], v_ref[...])

    shapes = [jax.ShapeDtypeStruct(w.shape, F32) for w in ws]
    outs = pl.pallas_call(body, name="adamw_small", out_shape=shapes * 3)(*ws, *gs, *ms, *vs)
    return outs[:n], outs[n:2 * n], outs[2 * n:]


def kernel(x, norm_mix, w_in, b_gate, rel_bias, w_ret_out, w_att_out, w_out, norm_ffn, w_ffn_gate, w_ffn_up, w_ffn_down, norm_final, loss_target, m_norm_mix, m_w_in, m_b_gate, m_rel_bias, m_w_ret_out, m_w_att_out, m_w_out, m_norm_ffn, m_w_ffn_gate, m_w_ffn_up, m_w_ffn_down, m_norm_final, v_norm_mix, v_w_in, v_b_gate, v_rel_bias, v_w_ret_out, v_w_att_out, v_w_out, v_norm_ffn, v_w_ffn_gate, v_w_ffn_up, v_w_ffn_down, v_norm_final):
    me = _index(_place())
    n_rb = rel_bias.shape[-1]

    shard = dict(w_in_t=w_in[0].T, w_gate_t=w_ffn_gate[0].T, w_up_t=w_ffn_up[0].T, w_down=w_ffn_down[0],
                 w_ret=w_ret_out[0], w_out=w_out[0], w_att_t=w_att_out[0].T)
    shard = {nm: s.astype(BF16) for nm, s in shard.items()}
    dx, slots, small_slots = _step(x, loss_target, norm_mix, b_gate, norm_ffn, norm_final, rel_bias[0], shard)
    small_sum = _sum_slots(small_slots, "sum_small")
    loss = small_sum[5, 0]

    transposed = dict(w_in="w_in_t", w_ffn_gate="w_gate_t", w_ffn_up="w_up_t", w_att_out="w_att_t")
    plain = dict(w_ffn_down="w_down", w_ret_out="w_ret", w_out="w_out")
    g = dict(
        norm_mix=small_sum[0:1], b_gate=jnp.concatenate([small_sum[1:2], small_sum[2:3]], axis=1),
        norm_ffn=small_sum[3:4], norm_final=small_sum[4:5],
        rel_bias=lax.dynamic_slice_in_dim(_bias_rows_bwd(small_sum[8:16]), me * n_rb, n_rb, axis=1),
    )
    w = dict(norm_mix=norm_mix, w_in=w_in, b_gate=b_gate, rel_bias=rel_bias, w_ret_out=w_ret_out, w_att_out=w_att_out,
             w_out=w_out, norm_ffn=norm_ffn, w_ffn_gate=w_ffn_gate, w_ffn_up=w_ffn_up, w_ffn_down=w_ffn_down,
             norm_final=norm_final)
    m = dict(norm_mix=m_norm_mix, w_in=m_w_in, b_gate=m_b_gate, rel_bias=m_rel_bias, w_ret_out=m_w_ret_out,
             w_att_out=m_w_att_out, w_out=m_w_out, norm_ffn=m_norm_ffn, w_ffn_gate=m_w_ffn_gate, w_ffn_up=m_w_ffn_up,
             w_ffn_down=m_w_ffn_down, norm_final=m_norm_final)
    v = dict(norm_mix=v_norm_mix, w_in=v_w_in, b_gate=v_b_gate, rel_bias=v_rel_bias, w_ret_out=v_w_ret_out,
             w_att_out=v_w_att_out, w_out=v_w_out, norm_ffn=v_norm_ffn, w_ffn_gate=v_w_ffn_gate, w_ffn_up=v_w_ffn_up,
             w_ffn_down=v_w_ffn_down, norm_final=v_norm_final)
    order = ("norm_mix", "w_in", "b_gate", "rel_bias", "w_ret_out", "w_att_out", "w_out", "norm_ffn",
             "w_ffn_gate", "w_ffn_up", "w_ffn_down", "norm_final")
    small_names = ("norm_mix", "b_gate", "rel_bias", "norm_ffn", "norm_final")

    def flat(a):
        return a[0] if a.ndim == 3 else a.reshape(-1, a.shape[-1])

    grad, delta, new_m, new_v = {}, {}, {}, {}
    for nm in order:
        if nm in transposed:
            res = _adamw(w[nm][0].T, slots[transposed[nm]], m[nm][0].T, v[nm][0].T, "adamw_" + nm)
            grad[nm], delta[nm], new_m[nm], new_v[nm] = (a.T[None] for a in res)
        elif nm in plain:
            res = _adamw(flat(w[nm]), slots[plain[nm]], flat(m[nm]), flat(v[nm]), "adamw_" + nm)
            grad[nm], delta[nm], new_m[nm], new_v[nm] = (a.reshape(w[nm].shape) for a in res)
    ds, nms, nvs = _adamw_small([flat(w[nm]) for nm in small_names], [g[nm] for nm in small_names],
                                [flat(m[nm]) for nm in small_names], [flat(v[nm]) for nm in small_names])
    for i, nm in enumerate(small_names):
        grad[nm], delta[nm], new_m[nm], new_v[nm] = (a.reshape(w[nm].shape) for a in (g[nm], ds[i], nms[i], nvs[i]))

    return (loss, dx, *[grad[nm] for nm in order], *[delta[nm] for nm in order],
            *[new_m[nm] for nm in order], *[new_v[nm] for nm in order])
```

```python
import numpy as np
import jax
import jax.numpy as jnp
from jax import lax
from jax.experimental import pallas as pl
from jax.experimental.pallas import tpu as pltpu

F32 = jnp.float32
BF16 = jnp.bfloat16
MESH = pl.DeviceIdType.MESH

D_MODEL = 1024
CHUNK = 64
RET_HEADS = 4
RET_KEY_DIM = 128
RET_VAL_DIM = 256
ATT_HEADS = 8
BAND_CHUNKS = 8
MAX_REL = 256
N_REL = CHUNK + MAX_REL
D_FF = 2816
N_IN = 6656
ROPE_BASE = 10000.0
EPS = 1e-6
NEG_INF = -1e30
C_RQ, C_RK, C_RV, C_RG, C_AQ, C_AK, C_AV, C_GL = 0, 512, 1024, 2048, 3072, 3584, 4096, 4608

ADAM_LR = 0.001
ADAM_B1 = 0.9
ADAM_B2 = 0.999
ADAM_EPS = 1e-08
ADAM_WD = 0.01
ADAM_STEP = 10

N_DEV = 8
LANES = 128
RET_TILE = 256
ATT_Q = 256
ATT_PAD = BAND_CHUNKS * CHUNK
ATT_WIN = ATT_PAD + ATT_Q
ATT_STARTS = ATT_PAD // ATT_Q
ATT_ROWS = 32
BIAS_LEN = 1024
VMEM_LIMIT = 48 * 1024 * 1024
VMEM_LIMIT_ATT_BWD = 56 * 1024 * 1024


def _params(*sem, vmem=VMEM_LIMIT):
    return pltpu.CompilerParams(dimension_semantics=sem, vmem_limit_bytes=vmem)


def _dot(a, b):
    return lax.dot_general(a, b, (((1,), (0,)), ((), ())), preferred_element_type=F32)


def _dot_nt(a, b):
    return lax.dot_general(a, b, (((1,), (1,)), ((), ())), preferred_element_type=F32)


def _dot_tn(a, b):
    return lax.dot_general(a, b, (((0,), (0,)), ((), ())), preferred_element_type=F32)


def _sigmoid(x):
    return 1.0 / (1.0 + jnp.exp(-x))


def _rms_bwd(x, g, dy):
    r = lax.rsqrt(jnp.mean(x * x, axis=-1, keepdims=True) + EPS)
    u = dy * g
    dx = r * u - x * (r * r * r) * jnp.mean(u * x, axis=-1, keepdims=True)
    return dx, dy * x * r


def _place():
    return lax.axis_index("x"), lax.axis_index("y"), lax.axis_index("c")


def _peer(k):
    x, y, c = _place()
    return ((1 - x) if k & 4 else x, (1 - y) if k & 2 else y, (1 - c) if k & 1 else c)


def _index(place):
    return 4 * place[0] + 2 * place[1] + place[2]


def _rows(ref, block, nrows):
    align = 16 if ref.dtype == BF16 else 8
    return ref.at[pl.ds(pl.multiple_of(block * nrows, align), nrows)]


class _Scatter:
    def __init__(self, arrays):
        self.arrays, self.n = list(arrays), len(arrays)

    def out_shape(self):
        return [jax.ShapeDtypeStruct((N_DEV, a.shape[0] // N_DEV) + a.shape[1:], a.dtype) for a in self.arrays]

    def scratch(self):
        return [pltpu.SemaphoreType.DMA((self.n, N_DEV - 1)), pltpu.SemaphoreType.DMA((self.n, N_DEV - 1)),
                pltpu.SemaphoreType.DMA((self.n,))]

    def _copies(self, ins, outs, sems):
        send_sems, recv_sems, local_sems = sems
        me = _index(_place())

        def src(w, to):
            return _rows(ins[w], to, ins[w].shape[0] // N_DEV)

        def dst(w, origin):
            return outs[w].at[origin]

        def remote(w, k, to, origin):
            return pltpu.make_async_remote_copy(src_ref=src(w, to), dst_ref=dst(w, origin),
                                                send_sem=send_sems.at[w, k - 1], recv_sem=recv_sems.at[w, k - 1],
                                                device_id=_peer(k), device_id_type=MESH)

        pairs = [(w, k) for w in range(self.n) for k in range(1, N_DEV)]
        own = lambda: [pltpu.make_async_copy(src(w, me), dst(w, me), local_sems.at[w]) for w in range(self.n)]
        sent = lambda: [remote(w, k, _index(_peer(k)), me) for w, k in pairs]
        arriving = lambda: [remote(w, k, me, _index(_peer(k))) for w, k in pairs]
        return own, sent, arriving

    def start(self, ins, outs, sems):
        own, sent, _ = self._copies(ins, outs, sems)
        for cp in own() + sent():
            cp.start()

    def wait(self, ins, outs, sems):
        own, sent, arriving = self._copies(ins, outs, sems)
        for cp in arriving():
            cp.wait_recv()
        for cp in sent():
            cp.wait_send()
        for cp in own():
            cp.wait()


class _PairSwap:
    def __init__(self, arrays):
        self.arrays, self.n = list(arrays), len(arrays)

    def out_shape(self):
        return [jax.ShapeDtypeStruct((4, a.shape[0] // N_DEV) + a.shape[1:], a.dtype) for a in self.arrays]

    def scratch(self):
        return [pltpu.SemaphoreType.DMA((self.n, 4)), pltpu.SemaphoreType.DMA((self.n, 4))]

    def _copies(self, ins, outs, sems):
        send_sems, recv_sems = sems
        x, y, c = _place()
        return [pltpu.make_async_remote_copy(
            src_ref=_rows(ins[w], 2 * q + 1 - c, ins[w].shape[0] // N_DEV), dst_ref=outs[w].at[q],
            send_sem=send_sems.at[w, q], recv_sem=recv_sems.at[w, q],
            device_id=(x, y, 1 - c), device_id_type=MESH) for w in range(self.n) for q in range(4)]

    def start(self, ins, outs, sems):
        for cp in self._copies(ins, outs, sems):
            cp.start()

    def wait(self, ins, outs, sems):
        for cp in self._copies(ins, outs, sems):
            cp.wait()


class _ChipScatter:
    def __init__(self, arrays):
        self.arrays, self.n = list(arrays), len(arrays)

    def out_shape(self):
        return [jax.ShapeDtypeStruct(a.shape, a.dtype) for a in self.arrays]

    def scratch(self):
        return [pltpu.SemaphoreType.DMA((self.n, 3)), pltpu.SemaphoreType.DMA((self.n, 3)),
                pltpu.SemaphoreType.DMA((self.n,))]

    def _copies(self, ins, outs, sems):
        send_sems, recv_sems, local_sems = sems
        x, y, c = _place()
        mine = 2 * x + y
        sent, arriving = [], []
        for w in range(self.n):
            for k in range(1, 4):
                tx, ty = (1 - x) if k & 2 else x, (1 - y) if k & 1 else y
                other = 2 * tx + ty
                sent.append(lambda w=w, k=k, tx=tx, ty=ty, other=other: pltpu.make_async_remote_copy(
                    src_ref=ins[w].at[other], dst_ref=outs[w].at[mine], send_sem=send_sems.at[w, k - 1],
                    recv_sem=recv_sems.at[w, k - 1], device_id=(tx, ty, c), device_id_type=MESH))
                arriving.append(lambda w=w, k=k, tx=tx, ty=ty, other=other: pltpu.make_async_remote_copy(
                    src_ref=ins[w].at[mine], dst_ref=outs[w].at[other], send_sem=send_sems.at[w, k - 1],
                    recv_sem=recv_sems.at[w, k - 1], device_id=(tx, ty, c), device_id_type=MESH))
        own = [lambda w=w: pltpu.make_async_copy(ins[w].at[mine], outs[w].at[mine], local_sems.at[w])
               for w in range(self.n)]
        return own, sent, arriving

    def start(self, ins, outs, sems):
        own, sent, _ = self._copies(ins, outs, sems)
        for cp in own + sent:
            cp().start()

    def wait(self, ins, outs, sems):
        own, sent, arriving = self._copies(ins, outs, sems)
        for cp in arriving:
            cp().wait_recv()
        for cp in sent:
            cp().wait_send()
        for cp in own:
            cp().wait()


class _ChipGather:
    def __init__(self, arrays, parts=None, into=None):
        self.arrays, self.n, self.into = list(arrays), len(arrays), into
        self.parts = parts or [(0, 1)] * self.n

    def out_shape(self):
        return [jax.ShapeDtypeStruct((N_DEV * a.shape[0],) + a.shape[1:], a.dtype) for a in self.arrays]

    def scratch(self):
        return [pltpu.SemaphoreType.DMA((self.n, N_DEV - 1)), pltpu.SemaphoreType.DMA((self.n, N_DEV - 1)),
                pltpu.SemaphoreType.DMA((self.n,))]

    def _parts(self, ins, outs, sems):
        send_sems, recv_sems, local_sems = sems
        x, y, c = _place()
        me, sibling = (x, y, c), (x, y, 1 - c)
        chips = [(1 - x, y), (x, 1 - y), (1 - x, 1 - y)]

        def rows(w, place, whole):
            (index, count), r = self.parts[w], ins[w].shape[0]
            lo, size = (0, r) if whole else (index * (r // count), r // count)
            align = 16 if ins[w].dtype == BF16 else 8
            return outs[w].at[pl.ds(pl.multiple_of(_index(place) * r + lo, align), size)]

        def mine(w, whole):
            (index, count), r = self.parts[w], ins[w].shape[0]
            return ins[w] if whole or count == 1 else ins[w].at[pl.ds(index * (r // count), r // count)]

        def copy(w, k, block, to, own=False):
            whole = k == 0
            return pltpu.make_async_remote_copy(src_ref=mine(w, whole) if own else rows(w, block, whole),
                                                dst_ref=rows(w, block, whole),
                                                send_sem=send_sems.at[w, k], recv_sem=recv_sems.at[w, k],
                                                device_id=to, device_id_type=MESH)

        def local(w):
            return pltpu.make_async_copy(ins[w], rows(w, me, True), local_sems.at[w])

        return me, sibling, chips, c, copy, local, [index == 0 for index, _ in self.parts]

    def start(self, ins, outs, sems):
        me, sibling, chips, c, copy, local, places_own = self._parts(ins, outs, sems)
        for w in range(self.n):
            if places_own[w]:
                local(w).start()
                copy(w, 0, me, sibling, own=True).start()
            for j, chip in enumerate(chips):
                copy(w, 1 + j, me, (*chip, c), own=True).start()

    def wait(self, ins, outs, sems):
        me, sibling, chips, c, copy, local, places_own = self._parts(ins, outs, sems)
        for w in range(self.n):
            for j, chip in enumerate(chips):
                copy(w, 1 + j, (*chip, c), me).wait_recv()
                copy(w, 4 + j, (*chip, c), sibling).start()
        for w in range(self.n):
            if places_own[w]:
                copy(w, 0, sibling, me).wait_recv()
                copy(w, 0, me, sibling, own=True).wait_send()
                local(w).wait()
            for j, chip in enumerate(chips):
                copy(w, 4 + j, (*chip, 1 - c), me).wait_recv()
                copy(w, 1 + j, me, (*chip, c), own=True).wait_send()
                copy(w, 4 + j, (*chip, c), sibling).wait_send()


def _call(body, *, name, grid, in_specs, out_specs, out_shape, scratch=(), semantics, args, exchange=None,
          vmem=VMEM_LIMIT):
    if exchange is None:
        return pl.pallas_call(body, name=name, grid=grid, in_specs=in_specs, out_specs=out_specs, out_shape=out_shape,
                              scratch_shapes=list(scratch),
                              compiler_params=_params(*semantics, vmem=vmem))(*args), None
    n_in, n_out, n_scr, nx = len(in_specs), len(out_specs), len(scratch), exchange.n
    into = list(getattr(exchange, "into", None) or [])

    def full_body(*refs):
        ins, refs = refs[:n_in], refs[n_in:]
        x_in, refs = refs[:nx], refs[nx + len(into):]
        outs, refs = refs[:n_out], refs[n_out:]
        x_out, refs = refs[:nx], refs[nx:]
        scr, sems = refs[:n_scr], refs[n_scr:]
        first, last = True, True
        for axis, size in enumerate(grid):
            first = jnp.logical_and(first, pl.program_id(axis) == 0)
            last = jnp.logical_and(last, pl.program_id(axis) == size - 1)
        if grid:
            pl.when(first)(lambda: exchange.start(x_in, x_out, sems))
        else:
            exchange.start(x_in, x_out, sems)
        body(*ins, *outs, *scr)
        if grid:
            pl.when(last)(lambda: exchange.wait(x_in, x_out, sems))
        else:
            exchange.wait(x_in, x_out, sems)

    hbm = pl.BlockSpec(memory_space=pltpu.HBM)
    res = pl.pallas_call(
        full_body, name=name, grid=grid,
        in_specs=list(in_specs) + [hbm] * (nx + len(into)), out_specs=list(out_specs) + [hbm] * nx,
        out_shape=list(out_shape) + exchange.out_shape(),
        scratch_shapes=list(scratch) + exchange.scratch(),
        input_output_aliases={n_in + nx + w: n_out + w for w in range(len(into))},
        compiler_params=_params(*(["arbitrary"] * len(grid)), vmem=vmem),
    )(*args, *exchange.arrays, *into)
    return res[:n_out], res[n_out:]


def _alone(exchange, name):
    return _call(lambda: None, name=name, grid=(), in_specs=[], out_specs=[], out_shape=[], semantics=(),
                 args=(), exchange=exchange)[1]


def _mm(a, b, *, ta=False, tb=False, out_dtype, tm, tn, tk, name, exchange=None):
    m, k = (a.shape[1], a.shape[0]) if ta else a.shape
    n = b.shape[0] if tb else b.shape[1]
    assert k == (b.shape[1] if tb else b.shape[0])
    tm, tn, tk = min(tm, m), min(tn, n), min(tk, k)
    assert m % tm == 0 and n % tn == 0 and k % tk == 0, (name, m, n, k)
    nk = k // tk
    dims = (((0 if ta else 1,), (1 if tb else 0,)), ((), ()))

    def body(a_ref, b_ref, o_ref, *acc):
        prod = lax.dot_general(a_ref[...].astype(BF16), b_ref[...].astype(BF16), dims, preferred_element_type=F32)
        if nk == 1:
            o_ref[...] = prod.astype(o_ref.dtype)
            return
        acc_ref, kk = acc[0], pl.program_id(2)

        @pl.when(kk == 0)
        def _():
            acc_ref[...] = prod

        @pl.when((kk > 0) & (kk < nk - 1))
        def _():
            acc_ref[...] += prod

        @pl.when(kk == nk - 1)
        def _():
            o_ref[...] = (acc_ref[...] + prod).astype(o_ref.dtype)

    a_spec = (pl.BlockSpec((tk, tm), lambda i, j, kk: (kk, i)) if ta
              else pl.BlockSpec((tm, tk), lambda i, j, kk: (i, kk)))
    b_spec = (pl.BlockSpec((tn, tk), lambda i, j, kk: (j, kk)) if tb
              else pl.BlockSpec((tk, tn), lambda i, j, kk: (kk, j)))
    (out,), moved = _call(
        body, name=name, grid=(m // tm, n // tn, nk),
        in_specs=[a_spec, b_spec],
        out_specs=[pl.BlockSpec((tm, tn), lambda i, j, kk: (i, j))],
        out_shape=[pltpu.HBM((m, n), out_dtype)],
        scratch=[pltpu.VMEM((tm, tn), F32)] if nk > 1 else [],
        semantics=("parallel", "parallel", "arbitrary"), args=(a, b), exchange=exchange)
    return out if exchange is None else (out, moved)


def _mm_pieces(pieces, b, *, ta, out_dtype, tm, tn, tk, name, exchange=None):
    rows, n = pieces[0].shape[0], b.shape[1]
    step = tm if ta else tk
    assert all(p.shape[0] == rows and p.shape[1] % step == 0 for p in pieces), name
    edges = [int(e) for e in np.cumsum([0] + [p.shape[1] // step for p in pieces])]
    total = edges[-1] * step
    m, k = (total, rows) if ta else (rows, total)
    assert b.shape[0] == k and m % tm == 0 and n % tn == 0 and k % tk == 0, name
    nk, npieces = k // tk, len(pieces)
    dims = (((0 if ta else 1,), (0,)), ((), ()))
    b_resident = ta and n == tn

    def body(*refs):
        a_refs, (b_ref, o_ref, acc_ref) = refs[:npieces], refs[npieces:]
        kk = pl.program_id(2)
        pos = pl.program_id(0) if ta else kk

        @pl.when(kk == 0)
        def _():
            acc_ref[...] = jnp.zeros_like(acc_ref)

        def b_tile():
            return b_ref[pl.ds(pl.multiple_of(kk * tk, tk), tk), :] if b_resident else b_ref[...]

        for p, a_ref in enumerate(a_refs):
            @pl.when((pos >= edges[p]) & (pos < edges[p + 1]))
            def _(a_ref=a_ref):
                acc_ref[...] += lax.dot_general(a_ref[...], b_tile(), dims, preferred_element_type=F32)

        @pl.when(kk == nk - 1)
        def _():
            o_ref[...] = acc_ref[...].astype(o_ref.dtype)

    def a_spec(p):
        lo, last = edges[p], edges[p + 1] - edges[p] - 1
        if ta:
            def index(i, j, kk):
                inside = (i >= lo) & (i <= lo + last)
                return jnp.where(inside, kk, 0), jnp.clip(i - lo, 0, last)
            return pl.BlockSpec((tk, tm), index)
        return pl.BlockSpec((tm, tk), lambda i, j, kk: (i, jnp.clip(kk - lo, 0, last)))

    (out,), moved = _call(
        body, name=name, grid=(m // tm, n // tn, nk),
        in_specs=[a_spec(p) for p in range(npieces)]
                 + [pl.BlockSpec(b.shape, lambda i, j, kk: (0, 0)) if b_resident
                    else pl.BlockSpec((tk, tn), lambda i, j, kk: (kk, j))],
        out_specs=[pl.BlockSpec((tm, tn), lambda i, j, kk: (i, j))],
        out_shape=[pltpu.HBM((m, n), out_dtype)],
        scratch=[pltpu.VMEM((tm, tn), F32)],
        semantics=("parallel", "parallel", "arbitrary"), args=(*pieces, b), exchange=exchange)
    return out if exchange is None else (out, moved)


def _rms_fwd(x2, g, exchange):
    t = x2.shape[0]
    tm = min(512, t)

    def body(x_ref, g_ref, o_ref):
        x = x_ref[...]
        r = lax.rsqrt(jnp.mean(x * x, axis=-1, keepdims=True) + EPS)
        o_ref[...] = (x * r * g_ref[...]).astype(o_ref.dtype)

    return _call(
        body, name="rms_in_fwd", grid=(t // tm,),
        in_specs=[pl.BlockSpec((tm, D_MODEL), lambda i: (i, 0)), pl.BlockSpec((1, D_MODEL), lambda i: (0, 0))],
        out_specs=[pl.BlockSpec((tm, D_MODEL), lambda i: (i, 0))],
        out_shape=[jax.ShapeDtypeStruct((t, D_MODEL), BF16)],
        semantics=("parallel",), args=(x2, g), exchange=exchange)


def _decay(lg):
    row = lax.broadcasted_iota(jnp.int32, (RET_TILE, RET_TILE), 0)
    col = lax.broadcasted_iota(jnp.int32, (RET_TILE, RET_TILE), 1)
    within = jnp.exp(lg * jnp.abs(row - col).astype(F32))
    inside = jnp.where((col >> 6) <= (row >> 6), within, 0.0)
    pos = lax.broadcasted_iota(jnp.int32, (RET_TILE, 1), 0).astype(F32)
    q_dec = jnp.exp(lg * (pos + 1.0))
    k_dec = jnp.exp(lg * (RET_TILE - 1.0 - pos))
    tile_dec = jnp.exp(lg * float(RET_TILE))
    return inside, q_dec, k_dec, tile_dec


def _scaled(a_bf16, dec):
    return (a_bf16.astype(F32) * dec).astype(BF16)


def _ret_specs(seq):
    key = lambda base: pl.BlockSpec((seq, RET_KEY_DIM), lambda b, h: (b, base // RET_KEY_DIM + h))
    val = lambda base: pl.BlockSpec((seq, RET_VAL_DIM), lambda b, h: (b, base // RET_VAL_DIM + h))
    tab = pl.BlockSpec((seq, RET_KEY_DIM), lambda b, h: (0, 0))
    lgs = pl.BlockSpec((None, 1, LANES), lambda b, h: (h, 0, 0))
    return key, val, tab, lgs


def _ret_fwd(proj, cs, sn, lg_arr, batch, seq, exchange):
    t = batch * seq
    nt = seq // RET_TILE

    def body(q_ref, k_ref, v_ref, rg_ref, cs_ref, sn_ref, lg_ref, gro_ref, o_ref, qr_ref, kr_ref):
        lg = lg_ref[:, 0:1]
        cs_t, sn_t = cs_ref[...], sn_ref[...]
        q = q_ref[...].astype(F32)
        k = k_ref[...].astype(F32)
        qr_ref[...] = (q * cs_t + pltpu.roll(q, 64, 1) * sn_t).astype(BF16)
        kr_ref[...] = ((k * cs_t + pltpu.roll(k, 64, 1) * sn_t) * (RET_KEY_DIM ** -0.5)).astype(BF16)
        inside, q_dec, k_dec, tile_dec = _decay(lg)
        state = jnp.zeros((RET_KEY_DIM, RET_VAL_DIM), F32)
        for i in range(nt):
            rows = slice(i * RET_TILE, (i + 1) * RET_TILE)
            qi, ki, vi = qr_ref[rows, :], kr_ref[rows, :], v_ref[rows, :]
            acc = _dot((_dot_nt(qi, ki) * inside).astype(BF16), vi)
            if i > 0:
                acc = acc + _dot(_scaled(qi, q_dec), state.astype(BF16))
            if i < nt - 1:
                state = state * tile_dec + _dot_tn(_scaled(ki, k_dec), vi)
            o_ref[rows, :] = acc
            xc = acc - jnp.mean(acc, axis=-1, keepdims=True)
            nrm = xc * lax.rsqrt(jnp.mean(xc * xc, axis=-1, keepdims=True) + EPS)
            rg = rg_ref[rows, :].astype(F32)
            gro_ref[rows, :] = (rg * _sigmoid(rg) * nrm).astype(BF16)

    key, val, tab, lgs = _ret_specs(seq)
    return _call(
        body, name="ret_fwd", grid=(batch, RET_HEADS),
        in_specs=[key(C_RQ), key(C_RK), val(C_RV), val(C_RG), tab, tab, lgs],
        out_specs=[val(0), val(0), key(0), key(0)],
        out_shape=[jax.ShapeDtypeStruct((t, RET_HEADS * RET_VAL_DIM), BF16),
                   jax.ShapeDtypeStruct((t, RET_HEADS * RET_VAL_DIM), F32),
                   jax.ShapeDtypeStruct((t, RET_HEADS * RET_KEY_DIM), BF16),
                   jax.ShapeDtypeStruct((t, RET_HEADS * RET_KEY_DIM), BF16)],
        semantics=("parallel", "parallel"), args=(proj, proj, proj, proj, cs, sn, lg_arr), exchange=exchange)


def _att_bias(w_ref, bias_ref):
    n_i = lax.broadcasted_iota(jnp.int32, (ATT_Q, BIAS_LEN), 0)
    qc = lax.broadcasted_iota(jnp.int32, (ATT_Q, ATT_WIN), 0) >> 6
    kc = lax.broadcasted_iota(jnp.int32, (ATT_Q, ATT_WIN), 1) >> 6
    dc = qc + BAND_CHUNKS - kc
    band = (dc >= 0) & (dc <= BAND_CHUNKS)
    key = lax.broadcasted_iota(jnp.int32, (ATT_Q, ATT_WIN), 1)
    for e in range(2):
        xw = jnp.broadcast_to(w_ref[e:e + 1, :], (ATT_Q, BIAS_LEN))
        for bit in range(8):
            xw = jnp.where(((n_i >> bit) & 1) == 1, pltpu.roll(xw, 1 << bit, 1), xw)
        bias = jnp.where(band, xw[:, BIAS_LEN - ATT_WIN:], NEG_INF)
        for first in range(ATT_STARTS):
            bias_ref[first, e] = jnp.where(key + (first * ATT_Q - ATT_PAD) >= 0, bias, NEG_INF)
        bias_ref[ATT_STARTS, e] = bias


ATT_PAIRS = 2
ATT_COLS = ATT_PAIRS * LANES


def _att_specs(batch, seq):
    ni = seq // ATT_Q
    q_spec = pl.BlockSpec((ATT_Q, ATT_COLS), lambda g, b, i: (b * ni + i, C_AQ // ATT_COLS + g))
    k_spec = pl.BlockSpec((seq, ATT_COLS), lambda g, b, i: (b, C_AK // ATT_COLS + g))
    v_spec = pl.BlockSpec((seq, ATT_COLS), lambda g, b, i: (b, C_AV // ATT_COLS + g))
    w_spec = pl.BlockSpec((ATT_PAIRS, 2, BIAS_LEN), lambda g, b, i: (g, 0, 0))
    b_spec = pl.BlockSpec((ATT_PAIRS, ATT_STARTS + 1, 2, ATT_Q, ATT_WIN), lambda g, b, i: (g, 0, 0, 0, 0))
    pad = pltpu.VMEM((seq + ATT_PAD, ATT_COLS), BF16)
    return ni, q_spec, k_spec, v_spec, w_spec, b_spec, pad


def _att_bias_tiles(wvec, exchange):
    (tiles,), moved = _call(
        lambda w_ref, o_ref: _att_bias(w_ref, o_ref), name="att_bias", grid=(ATT_HEADS // 2,),
        in_specs=[pl.BlockSpec((None, 2, BIAS_LEN), lambda hp: (hp, 0, 0))],
        out_specs=[pl.BlockSpec((None, ATT_STARTS + 1, 2, ATT_Q, ATT_WIN), lambda hp: (hp, 0, 0, 0, 0))],
        out_shape=[jax.ShapeDtypeStruct((ATT_HEADS // 2, ATT_STARTS + 1, 2, ATT_Q, ATT_WIN), F32)],
        semantics=("parallel",), args=(wvec,), exchange=exchange)
    return tiles, moved


def _att_pad(src_ref, pad_ref):
    pad_ref[:ATT_PAD, :] = jnp.zeros((ATT_PAD, ATT_COLS), BF16)
    pad_ref[ATT_PAD:, :] = src_ref[...]


def _att_head(q2, sel):
    return jnp.where(sel, q2, jnp.zeros_like(q2)) * 0.125


def _att_softmax_rows(s_ref, bias_ref, rows):
    s = s_ref[rows, :] + bias_ref[rows, :]
    ex = jnp.exp(s - jnp.max(s, axis=-1, keepdims=True))
    return ex, 1.0 / jnp.sum(ex, axis=-1, keepdims=True)


def _att_fwd(proj, bias, batch, seq, exchange):
    ni, q_spec, k_spec, v_spec, _, b_spec, pad = _att_specs(batch, seq)

    def body(q_ref, k_ref, v_ref, bias_ref, o_ref, kp_ref, vp_ref, s_ref, e_ref):
        i = pl.program_id(2)

        @pl.when(i == 0)
        def _():
            _att_pad(k_ref, kp_ref)
            _att_pad(v_ref, vp_ref)

        win = pl.ds(pl.multiple_of(i * ATT_Q, ATT_Q), ATT_WIN)
        lo = lax.broadcasted_iota(jnp.int32, (1, LANES), 1) < 64
        start = jnp.minimum(i, ATT_STARTS)
        for pair in range(ATT_PAIRS):
            cols = slice(pair * LANES, (pair + 1) * LANES)
            k2, v2, q2 = kp_ref[win, cols], vp_ref[win, cols], q_ref[:, cols]
            out = jnp.zeros((ATT_Q, LANES), F32)
            for e in range(2):
                h = 2 * pair + e
                sel = lo if e == 0 else jnp.logical_not(lo)
                s_ref[h] = _dot_nt(_att_head(q2, sel), k2)
                rsum = []
                for c in range(ATT_Q // ATT_ROWS):
                    rows = slice(c * ATT_ROWS, (c + 1) * ATT_ROWS)
                    ex, r = _att_softmax_rows(s_ref.at[h], bias_ref.at[pair, start, e], rows)
                    e_ref[h, rows, :] = ex.astype(BF16)
                    rsum.append(r)
                out = out + _dot(e_ref[h], jnp.where(sel, v2, jnp.zeros_like(v2))) * jnp.concatenate(rsum, axis=0)
            o_ref[:, cols] = out.astype(BF16)

    heads = 2 * ATT_PAIRS
    return _call(
        body, name="att_fwd", grid=(ATT_HEADS // heads, batch, ni),
        in_specs=[q_spec, k_spec, v_spec, b_spec],
        out_specs=[pl.BlockSpec((ATT_Q, ATT_COLS), lambda g, b, i: (b * ni + i, g))],
        out_shape=[jax.ShapeDtypeStruct((batch * seq, ATT_HEADS * 64), BF16)],
        scratch=[pad, pad, pltpu.VMEM((heads, ATT_Q, ATT_WIN), F32), pltpu.VMEM((heads, ATT_Q, ATT_WIN), BF16)],
        semantics=("arbitrary", "arbitrary", "arbitrary"), args=(proj, proj, proj, bias), exchange=exchange)


GL_HALF = 512


def _gl_specs(tm):
    return [pl.BlockSpec((tm, GL_HALF), lambda i, c=C_GL // GL_HALF + j: (i, c)) for j in range(4)]


def _gates(gl_refs, b_ref):
    logits = [ref[...].astype(F32) for ref in gl_refs]
    gr = _sigmoid(jnp.concatenate(logits[:2], axis=1) + b_ref[:, :D_MODEL])
    ga = _sigmoid(jnp.concatenate(logits[2:], axis=1) + b_ref[:, D_MODEL:])
    return gr, ga


def _whole(a):
    return pl.BlockSpec(a.shape, lambda i: (0,) * a.ndim)


def _mix_out_fwd(gro, ao, proj, b_gate, w_ret, w_att_t, x2, w_out, g2):
    t = gro.shape[0]
    tm = min(512, t)

    def body(gro_ref, ao_ref, gl0, gl1, gl2, gl3, b_ref, wr_ref, wa_ref, x_ref, wo_ref, g_ref,
             z_ref, yr_ref, ya_ref, h_ref, hn_ref):
        yr = _dot(gro_ref[...], wr_ref[...])
        ya = _dot_nt(ao_ref[...], wa_ref[...])
        yr_ref[...] = yr.astype(BF16)
        ya_ref[...] = ya.astype(BF16)
        gr, ga = _gates((gl0, gl1, gl2, gl3), b_ref)
        z = (gr * yr + ga * ya).astype(BF16)
        z_ref[...] = z
        h = x_ref[...] + _dot(z, wo_ref[...])
        h_ref[...] = h
        r = lax.rsqrt(jnp.mean(h * h, axis=-1, keepdims=True) + EPS)
        hn_ref[...] = (h * r * g_ref[...]).astype(BF16)

    row = pl.BlockSpec((tm, D_MODEL), lambda i: (i, 0))
    return pl.pallas_call(
        body, name="mix_out_fwd", grid=(t // tm,),
        in_specs=[row, pl.BlockSpec((tm, 512), lambda i: (i, 0)), *_gl_specs(tm),
                  _whole(b_gate), _whole(w_ret), _whole(w_att_t), row, _whole(w_out), _whole(g2)],
        out_specs=[row] * 5,
        out_shape=[jax.ShapeDtypeStruct((t, D_MODEL), BF16)] * 3
                  + [jax.ShapeDtypeStruct((t, D_MODEL), F32), jax.ShapeDtypeStruct((t, D_MODEL), BF16)],
        compiler_params=_params("parallel"),
    )(gro, ao, proj, proj, proj, proj, b_gate, w_ret, w_att_t, x2, w_out, g2)


def _col_chunks(width, chunk=384):
    return [slice(lo, min(lo + chunk, width)) for lo in range(0, width, chunk)]


def _ffn_up(hn, wg_t, wu_t):
    t = hn.shape[0]
    tm, tn = min(512, t), D_FF // 2

    def body(h_ref, wg_ref, wu_ref, g_ref, u_ref, a_ref):
        g = _dot_nt(h_ref[...], wg_ref[...])
        u = _dot_nt(h_ref[...], wu_ref[...])
        g_ref[...] = g.astype(BF16)
        u_ref[...] = u.astype(BF16)
        a_ref[...] = (g * _sigmoid(g) * u).astype(BF16)

    w_spec = pl.BlockSpec((tn, D_MODEL), lambda j, i: (j, 0))
    out = pl.BlockSpec((tm, tn), lambda j, i: (i, j))
    return pl.pallas_call(
        body, name="ffn_up", grid=(D_FF // tn, t // tm),
        in_specs=[pl.BlockSpec((tm, D_MODEL), lambda j, i: (i, 0)), w_spec, w_spec],
        out_specs=[out, out, out],
        out_shape=[jax.ShapeDtypeStruct((t, D_FF), BF16)] * 3,
        compiler_params=_params("parallel", "parallel"),
    )(hn, wg_t, wu_t)


def _ffn_down_loss(a, h1, tgt, w_down, g3):
    t = a.shape[0]
    tm = min(512, t)

    def body(a_ref, h_ref, t_ref, w_ref, g_ref, dh_ref, dhb_ref, loss_ref, dg_ref):
        @pl.when(pl.program_id(0) == 0)
        def _():
            loss_ref[...] = jnp.zeros_like(loss_ref)
            dg_ref[...] = jnp.zeros_like(dg_ref)

        g = g_ref[...]
        h2 = h_ref[...] + _dot(a_ref[...], w_ref[...])
        r = lax.rsqrt(jnp.mean(h2 * h2, axis=-1, keepdims=True) + EPS)
        err = h2 * r * g - t_ref[...]
        loss_ref[...] += jnp.sum(err * err) * (0.5 / D_MODEL)
        dy = err * (1.0 / D_MODEL)
        dh, dg_rows = _rms_bwd(h2, g, dy)
        dg_ref[...] += jnp.sum(dg_rows, axis=0, keepdims=True)
        dh_ref[...] = dh
        dhb_ref[...] = dh.astype(BF16)

    row = pl.BlockSpec((tm, D_MODEL), lambda i: (i, 0))
    vec = pl.BlockSpec((1, D_MODEL), lambda i: (0, 0))
    return pl.pallas_call(
        body, name="ffn_down_loss", grid=(t // tm,),
        in_specs=[pl.BlockSpec((tm, D_FF), lambda i: (i, 0)), row, row,
                  pl.BlockSpec((D_FF, D_MODEL), lambda i: (0, 0)), vec],
        out_specs=[row, row, pl.BlockSpec((1, LANES), lambda i: (0, 0)), vec],
        out_shape=[jax.ShapeDtypeStruct((t, D_MODEL), F32), jax.ShapeDtypeStruct((t, D_MODEL), BF16),
                   jax.ShapeDtypeStruct((1, LANES), F32), jax.ShapeDtypeStruct((1, D_MODEL), F32)],
        compiler_params=_params("arbitrary"),
    )(a, h1, tgt, w_down, g3)


def _ffn_bwd_act(dh2b, w_down, g_act, u_act, exchange):
    t = dh2b.shape[0]
    tm, tn = min(1024, t), D_FF // 2

    def body(d_ref, w_ref, g_ref, u_ref, dg_ref, du_ref):
        d = d_ref[...]
        for cols in _col_chunks(tn):
            da = _dot_nt(d, w_ref[cols, :])
            g = g_ref[:, cols].astype(F32)
            u = u_ref[:, cols].astype(F32)
            sg = _sigmoid(g)
            dg_ref[:, cols] = (da * u * sg * (1.0 + g * (1.0 - sg))).astype(BF16)
            du_ref[:, cols] = (da * g * sg).astype(BF16)

    blk = pl.BlockSpec((tm, tn), lambda j, i: (i, j))
    return _call(
        body, name="ffn_bwd_act", grid=(D_FF // tn, t // tm),
        in_specs=[pl.BlockSpec((tm, D_MODEL), lambda j, i: (i, 0)),
                  pl.BlockSpec((tn, D_MODEL), lambda j, i: (j, 0)), blk, blk],
        out_specs=[blk, blk],
        out_shape=[jax.ShapeDtypeStruct((t, D_FF), BF16)] * 2,
        semantics=("parallel", "parallel"), args=(dh2b, w_down, g_act, u_act), exchange=exchange)


def _ffn_bwd_in(dg, du, wg_t, wu_t, h1, dh2, g2, exchange):
    t = dg.shape[0]
    tm = min(256, t)

    def body(dg_ref, du_ref, wg_ref, wu_ref, h_ref, d2_ref, g_ref, dh_ref, dhb_ref, gn_ref):
        @pl.when(pl.program_id(0) == 0)
        def _():
            gn_ref[...] = jnp.zeros_like(gn_ref)

        dhn = _dot(dg_ref[...], wg_ref[...]) + _dot(du_ref[...], wu_ref[...])
        dx, dg_rows = _rms_bwd(h_ref[...], g_ref[...], dhn)
        dh = d2_ref[...] + dx
        dh_ref[...] = dh
        dhb_ref[...] = dh.astype(BF16)
        gn_ref[...] += jnp.sum(dg_rows, axis=0, keepdims=True)

    act = pl.BlockSpec((tm, D_FF), lambda i: (i, 0))
    row = pl.BlockSpec((tm, D_MODEL), lambda i: (i, 0))
    return _call(
        body, name="ffn_bwd_in", grid=(t // tm,),
        in_specs=[act, act, _whole(wg_t), _whole(wu_t), row, row, _whole(g2)],
        out_specs=[row, row, _whole(g2)],
        out_shape=[jax.ShapeDtypeStruct((t, D_MODEL), F32), jax.ShapeDtypeStruct((t, D_MODEL), BF16),
                   jax.ShapeDtypeStruct((1, D_MODEL), F32)],
        semantics=("arbitrary",), args=(dg, du, wg_t, wu_t, h1, dh2, g2), exchange=exchange)


def _mix_bwd(dh1b, w_out, proj, b_gate, y_ret, y_att, w_ret, w_att_t):
    t = dh1b.shape[0]
    tm = min(512, t)

    def body(d_ref, wo_ref, gl0, gl1, gl2, gl3, b_ref, yr_ref, ya_ref, wr_ref, wa_ref,
             dyr_ref, dya_ref, dglr_ref, dgla_ref, dgro_ref, dao_ref, db_ref):
        @pl.when(pl.program_id(0) == 0)
        def _():
            db_ref[...] = jnp.zeros_like(db_ref)

        dz = _dot_nt(d_ref[...], wo_ref[...])
        gr, ga = _gates((gl0, gl1, gl2, gl3), b_ref)
        dyr = (dz * gr).astype(BF16)
        dya = (dz * ga).astype(BF16)
        dyr_ref[...] = dyr
        dya_ref[...] = dya
        dglr = dz * yr_ref[...].astype(F32) * gr * (1.0 - gr)
        dgla = dz * ya_ref[...].astype(F32) * ga * (1.0 - ga)
        dglr_ref[...] = dglr.astype(BF16)
        dgla_ref[...] = dgla.astype(BF16)
        db_ref[:, :D_MODEL] += jnp.sum(dglr, axis=0, keepdims=True)
        db_ref[:, D_MODEL:] += jnp.sum(dgla, axis=0, keepdims=True)
        dgro_ref[...] = _dot_nt(dyr, wr_ref[...]).astype(BF16)
        dao_ref[...] = _dot(dya, wa_ref[...]).astype(BF16)

    row = pl.BlockSpec((tm, D_MODEL), lambda i: (i, 0))
    half = pl.BlockSpec((tm, 512), lambda i: (i, 0))
    return pl.pallas_call(
        body, name="mix_bwd", grid=(t // tm,),
        in_specs=[row, _whole(w_out), *_gl_specs(tm), _whole(b_gate), row, row, _whole(w_ret), _whole(w_att_t)],
        out_specs=[row, row, row, row, row, half, _whole(b_gate)],
        out_shape=[jax.ShapeDtypeStruct((t, D_MODEL), BF16)] * 5
                  + [jax.ShapeDtypeStruct((t, 512), BF16), jax.ShapeDtypeStruct(b_gate.shape, F32)],
        compiler_params=_params("arbitrary"),
    )(dh1b, w_out, proj, proj, proj, proj, b_gate, y_ret, y_att, w_ret, w_att_t)


def _ret_bwd(dgro, proj, o_ret, qr, kr, cs, sn, lg_arr, batch, seq, exchange):
    t = batch * seq
    nt = seq // RET_TILE

    def body(dgro_ref, rg_ref, o_ref, qr_ref, kr_ref, v_ref, cs_ref, sn_ref, lg_ref,
             dq_ref, dk_ref, dv_ref, drg_ref, do_ref, st_ref):
        lg = lg_ref[:, 0:1]
        inside, q_dec, k_dec, tile_dec = _decay(lg)

        state = jnp.zeros((RET_KEY_DIM, RET_VAL_DIM), F32)
        for i in range(nt - 1):
            rows = slice(i * RET_TILE, (i + 1) * RET_TILE)
            state = state * tile_dec + _dot_tn(_scaled(kr_ref[rows, :], k_dec), v_ref[rows, :])
            st_ref[i + 1] = state.astype(BF16)

        for i in range(nt):
            rows = slice(i * RET_TILE, (i + 1) * RET_TILE)
            o = o_ref[rows, :]
            xc = o - jnp.mean(o, axis=-1, keepdims=True)
            rs = lax.rsqrt(jnp.mean(xc * xc, axis=-1, keepdims=True) + EPS)
            nrm = xc * rs
            rg = rg_ref[rows, :].astype(F32)
            sg = _sigmoid(rg)
            dg = dgro_ref[rows, :].astype(F32)
            drg_ref[rows, :] = (dg * nrm * sg * (1.0 + rg * (1.0 - sg))).astype(BF16)
            dn = dg * rg * sg
            do = rs * (dn - jnp.mean(dn, axis=-1, keepdims=True)
                       - nrm * jnp.mean(dn * nrm, axis=-1, keepdims=True))
            do_ref[rows, :] = do.astype(BF16)

        dstate = jnp.zeros((RET_KEY_DIM, RET_VAL_DIM), F32)
        for i in reversed(range(nt)):
            rows = slice(i * RET_TILE, (i + 1) * RET_TILE)
            qi, ki, vi, doi = qr_ref[rows, :], kr_ref[rows, :], v_ref[rows, :], do_ref[rows, :]
            p = (_dot_nt(qi, ki) * inside).astype(BF16)
            dp = (_dot_nt(doi, vi) * inside).astype(BF16)
            dq = _dot(dp, ki)
            dk = _dot_tn(dp, qi)
            dv = _dot_tn(p, doi)
            if i > 0:
                dq = dq + _dot_nt(doi, st_ref[i]) * q_dec
            if i < nt - 1:
                dsb = dstate.astype(BF16)
                dk = dk + _dot_nt(vi, dsb) * k_dec
                dv = dv + _dot(_scaled(ki, k_dec), dsb)
            if i > 0:
                dstate = dstate * tile_dec + _dot_tn(_scaled(qi, q_dec), doi)
            dq_ref[rows, :] = (dq * cs_ref[rows, :] - pltpu.roll(dq, 64, 1) * sn_ref[rows, :]).astype(BF16)
            dk = (dk * cs_ref[rows, :] - pltpu.roll(dk, 64, 1) * sn_ref[rows, :]) * (RET_KEY_DIM ** -0.5)
            dk_ref[rows, :] = dk.astype(BF16)
            dv_ref[rows, :] = dv.astype(BF16)

    key, val, tab, lgs = _ret_specs(seq)
    return _call(
        body, name="ret_bwd", grid=(batch, RET_HEADS),
        in_specs=[val(0), val(C_RG), val(0), key(0), key(0), val(C_RV), tab, tab, lgs],
        out_specs=[key(0), key(0), val(0), val(0)],
        out_shape=[jax.ShapeDtypeStruct((t, RET_HEADS * RET_KEY_DIM), BF16)] * 2
                  + [jax.ShapeDtypeStruct((t, RET_HEADS * RET_VAL_DIM), BF16)] * 2,
        scratch=[pltpu.VMEM((seq, RET_VAL_DIM), BF16), pltpu.VMEM((nt, RET_KEY_DIM, RET_VAL_DIM), BF16)],
        semantics=("parallel", "parallel"), args=(dgro, proj, o_ret, qr, kr, proj, cs, sn, lg_arr),
        exchange=exchange)


def _att_bwd(proj, bias, dao, batch, seq, exchange):
    ni, q_spec, k_spec, v_spec, w_spec, b_spec, pad = _att_specs(batch, seq)
    t = batch * seq

    def body(q_ref, k_ref, v_ref, bias_ref, do_ref, dq_ref, dk_ref, dv_ref, dw_ref,
             dbias_ref, dk_acc, dv_acc, kp_ref, vp_ref, s_ref, dp_ref, e_ref, ds_ref):
        b, i = pl.program_id(1), pl.program_id(2)

        @pl.when((b == 0) & (i == 0))
        def _():
            dbias_ref[...] = jnp.zeros_like(dbias_ref)

        @pl.when(i == 0)
        def _():
            _att_pad(k_ref, kp_ref)
            _att_pad(v_ref, vp_ref)
            dk_acc[...] = jnp.zeros_like(dk_acc)
            dv_acc[...] = jnp.zeros_like(dv_acc)

        win = pl.ds(pl.multiple_of(i * ATT_Q, ATT_Q), ATT_WIN)
        lo = lax.broadcasted_iota(jnp.int32, (1, LANES), 1) < 64
        start = jnp.minimum(i, ATT_STARTS)
        for pair in range(ATT_PAIRS):
            cols = slice(pair * LANES, (pair + 1) * LANES)
            k2, v2, q2, do2 = kp_ref[win, cols], vp_ref[win, cols], q_ref[:, cols], do_ref[:, cols]
            dq = jnp.zeros((ATT_Q, LANES), F32)
            dk = jnp.zeros((LANES, ATT_WIN), F32)
            dv = jnp.zeros((LANES, ATT_WIN), F32)
            for e in range(2):
                h = 2 * pair + e
                sel = lo if e == 0 else jnp.logical_not(lo)
                qm = _att_head(q2, sel)
                dom = jnp.where(sel, do2, jnp.zeros_like(do2))
                s_ref[h] = _dot_nt(qm, k2)
                dp_ref[h] = _dot_nt(dom, v2)
                rsum = []
                for c in range(ATT_Q // ATT_ROWS):
                    rows = slice(c * ATT_ROWS, (c + 1) * ATT_ROWS)
                    ex, r = _att_softmax_rows(s_ref.at[h], bias_ref.at[pair, start, e], rows)
                    dp = dp_ref[h, rows, :]
                    mean = jnp.sum(dp * ex, axis=-1, keepdims=True) * r
                    ds = ex * ((dp - mean) * r)
                    dbias_ref[h, rows, :] += ds
                    ds_ref[h, rows, :] = ds.astype(BF16)
                    e_ref[h, rows, :] = ex.astype(BF16)
                    rsum.append(r)
                dq = dq + _dot(ds_ref[h], jnp.where(sel, k2, jnp.zeros_like(k2)))
                dk = dk + _dot_tn(qm, ds_ref[h])
                dv = dv + _dot_tn((dom.astype(F32) * jnp.concatenate(rsum, axis=0)).astype(BF16), e_ref[h])
            dq_ref[:, cols] = (dq * 0.125).astype(BF16)
            dk_acc[cols, win] += dk
            dv_acc[cols, win] += dv

        @pl.when(i == ni - 1)
        def _():
            dk_ref[...] = dk_acc[:, ATT_PAD:].T.astype(BF16)
            dv_ref[...] = dv_acc[:, ATT_PAD:].T.astype(BF16)

        @pl.when((b == batch - 1) & (i == ni - 1))
        def _():
            n_i = lax.broadcasted_iota(jnp.int32, (ATT_Q, BIAS_LEN), 0)
            for h in range(heads):
                xw = jnp.concatenate([jnp.zeros((ATT_Q, BIAS_LEN - ATT_WIN), F32), dbias_ref[h]], axis=1)
                for bit in range(8):
                    xw = jnp.where(((n_i >> bit) & 1) == 1, pltpu.roll(xw, BIAS_LEN - (1 << bit), 1), xw)
                dw_ref[h // 2, h % 2:h % 2 + 1, :] = jnp.sum(xw, axis=0, keepdims=True)

    heads = 2 * ATT_PAIRS
    seq_blk = pl.BlockSpec((seq, ATT_COLS), lambda g, b, i: (b, g))
    q_out = pl.BlockSpec((ATT_Q, ATT_COLS), lambda g, b, i: (b * ni + i, g))
    tile_f32, tile_bf16 = pltpu.VMEM((heads, ATT_Q, ATT_WIN), F32), pltpu.VMEM((heads, ATT_Q, ATT_WIN), BF16)
    acc = pltpu.VMEM((ATT_COLS, seq + ATT_PAD), F32)
    return _call(
        body, name="att_bwd", grid=(ATT_HEADS // heads, batch, ni),
        in_specs=[q_spec, k_spec, v_spec, b_spec, q_out],
        out_specs=[q_out, seq_blk, seq_blk, w_spec],
        out_shape=[jax.ShapeDtypeStruct((t, 512), BF16)] * 3
                  + [jax.ShapeDtypeStruct((ATT_HEADS // 2, 2, BIAS_LEN), F32)],
        scratch=[tile_f32, acc, acc, pad, pad, tile_f32, tile_f32, tile_bf16, tile_bf16],
        semantics=("arbitrary", "arbitrary", "arbitrary"), args=(proj, proj, proj, bias, dao), exchange=exchange,
        vmem=VMEM_LIMIT_ATT_BWD)


def _rms_in_bwd(x2, dxn, dh1, g1):
    t = x2.shape[0]
    tm = min(1024, t)

    def body(x_ref, d_ref, h_ref, g_ref, dx_ref, dg_ref):
        @pl.when(pl.program_id(0) == 0)
        def _():
            dg_ref[...] = jnp.zeros_like(dg_ref)

        dx, dg_rows = _rms_bwd(x_ref[...], g_ref[...], d_ref[...])
        dx_ref[...] = h_ref[...] + dx
        dg_ref[...] += jnp.sum(dg_rows, axis=0, keepdims=True)

    row = pl.BlockSpec((tm, D_MODEL), lambda i: (i, 0))
    vec = pl.BlockSpec((1, D_MODEL), lambda i: (0, 0))
    return pl.pallas_call(
        body, name="rms_in_bwd", grid=(t // tm,),
        in_specs=[row, row, row, vec], out_specs=[row, vec],
        out_shape=[jax.ShapeDtypeStruct((t, D_MODEL), F32), jax.ShapeDtypeStruct((1, D_MODEL), F32)],
        compiler_params=_params("arbitrary"),
    )(x2, dxn, dh1, g1)


def _pack_small(dg1, dbr, dba, dg2, dg3, dw, loss):
    def body(a_ref, b_ref, c_ref, d_ref, e_ref, w_ref, l_ref, o_ref):
        o_ref[...] = jnp.zeros_like(o_ref)
        for r, ref in enumerate((a_ref, b_ref, c_ref, d_ref, e_ref)):
            o_ref[r:r + 1, :] = ref[...]
        o_ref[5:6, 0:LANES] = l_ref[...]
        for hp in range(ATT_HEADS // 2):
            o_ref[8 + 2 * hp:10 + 2 * hp, :] = w_ref[hp]

    return pl.pallas_call(body, name="pack_small",
                          out_shape=jax.ShapeDtypeStruct((16, D_MODEL), F32))(dg1, dbr, dba, dg2, dg3, dw, loss)


def _rotary_tables(seq):
    freqs = ROPE_BASE ** (-jnp.arange(0, RET_KEY_DIM, 2, dtype=F32) / RET_KEY_DIM)
    ang = jnp.arange(seq, dtype=F32)[:, None] * freqs[None, :]
    cos, sin = jnp.cos(ang), jnp.sin(ang)
    return jnp.concatenate([cos, cos], axis=1), jnp.concatenate([-sin, sin], axis=1)


def _bias_rows(rel_bias):
    n_far = BIAS_LEN - ATT_Q - MAX_REL + 1
    n_near = BIAS_LEN - n_far - (N_REL - 2)
    w = jnp.concatenate([jnp.broadcast_to(rel_bias[:, N_REL - 1:], (ATT_HEADS, n_far)),
                         rel_bias[:, 1:N_REL - 1][:, ::-1],
                         jnp.broadcast_to(rel_bias[:, :1], (ATT_HEADS, n_near))], axis=1)
    return w.reshape(ATT_HEADS // 2, 2, BIAS_LEN)


def _bias_rows_bwd(dw):
    n_far = BIAS_LEN - ATT_Q - MAX_REL + 1
    mid = dw[:, n_far:n_far + N_REL - 2][:, ::-1]
    return jnp.concatenate([jnp.sum(dw[:, n_far + N_REL - 2:], axis=1, keepdims=True), mid,
                            jnp.sum(dw[:, :n_far], axis=1, keepdims=True)], axis=1)


def _step(x, tgt, norm_mix, b_gate, norm_ffn, norm_final, rel_bias_shard, shard):
    batch, seq, _ = x.shape
    t = batch * seq
    n_rb = rel_bias_shard.shape[-1]
    x2, tgt2 = x.reshape(t, D_MODEL), tgt.reshape(t, D_MODEL)
    g3 = norm_final.reshape(1, D_MODEL)
    cs, sn = _rotary_tables(seq)
    lg = np.log(1.0 - 2.0 ** (-5.0 - np.arange(RET_HEADS, dtype=np.float32))).astype(np.float32)
    lg_arr = jnp.asarray(np.broadcast_to(lg[:, None, None], (RET_HEADS, 1, LANES)))

    def gather(*names):
        return _ChipGather([shard[nm] for nm in names])

    def scatter(*grads):
        return _Scatter(grads)

    rb_pad = jnp.pad(rel_bias_shard, ((0, 0), (0, LANES - n_rb)))
    (xn,), (w_in_half, rb_full) = _rms_fwd(x2, norm_mix,
                                           _ChipGather([shard["w_in_t"], rb_pad], parts=[(0, 2), (0, 1)]))
    rb_full = rb_full.reshape(N_DEV, ATT_HEADS, LANES)[:, :, :n_rb]
    bias, (w_in_t,) = _att_bias_tiles(
        _bias_rows(jnp.transpose(rb_full, (1, 0, 2)).reshape(ATT_HEADS, N_DEV * n_rb)),
        _ChipGather([shard["w_in_t"]], parts=[(1, 2)], into=[w_in_half]))
    proj, (w_ret, w_att_t, w_out, w_gate_t) = _mm(
        xn, w_in_t, tb=True, out_dtype=BF16, tm=1024, tn=1664, tk=1024, name="proj",
        exchange=gather("w_ret", "w_att_t", "w_out", "w_gate_t"))
    (gro, o_ret, qr, kr), _ = _ret_fwd(proj, cs, sn, lg_arr, batch, seq, None)
    (ao,), (w_up_t, w_down) = _att_fwd(proj, bias, batch, seq, gather("w_up_t", "w_down"))
    z, y_ret, y_att, h1, hn = _mix_out_fwd(gro, ao, proj, b_gate, w_ret, w_att_t, x2, w_out, norm_ffn)
    g_act, u_act, a_act = _ffn_up(hn, w_gate_t, w_up_t)
    dh2, dh2b, loss, dg3 = _ffn_down_loss(a_act, h1, tgt2, w_down, g3)

    wg = dict(out_dtype=BF16, tn=1024, ta=True)
    slots = {}
    dw_down = _mm(a_act, dh2b, tm=1408, tk=2048, name="dw_down", **wg)
    (d_gact, d_uact), _ = _ffn_bwd_act(dh2b, w_down, g_act, u_act, None)
    dw_gate = _mm(d_gact, hn, tm=1408, tk=2048, name="dw_gate", **wg)
    dw_up = _mm(d_uact, hn, tm=1408, tk=2048, name="dw_up", **wg)
    (dh1, dh1b, dg2), (slots["w_down"],) = _ffn_bwd_in(d_gact, d_uact, w_gate_t, w_up_t, h1, dh2, norm_ffn,
                                                     scatter(dw_down))
    dw_out = _mm(z, dh1b, tm=1024, tk=2048, name="dw_out", **wg)
    dyr, dya, dglr, dgla, dgro, dao, db = _mix_bwd(dh1b, w_out, proj, b_gate, y_ret, y_att, w_ret, w_att_t)
    dw_ret = _mm(gro, dyr, tm=1024, tk=2048, name="dw_ret", **wg)
    dw_att = _mm(dya, ao, tm=1024, tk=2048, name="dw_att", **wg)
    (drq, drk, drv, drg), _ = _ret_bwd(dgro, proj, o_ret, qr, kr, cs, sn, lg_arr, batch, seq, None)
    (daq, dak, dav, dw), (slots["w_gate_t"], slots["w_out"], slots["w_ret"], slots["w_att_t"]) = _att_bwd(
        proj, bias, dao, batch, seq, scatter(dw_gate, dw_out, dw_ret, dw_att))
    dproj = [drq, drk, drv, drg, daq, dak, dav, dglr, dgla]
    dw_in, (slots["w_up_t"],) = _mm_pieces(dproj, xn, ta=True, out_dtype=BF16, tm=512, tn=1024, tk=1024,
                                           name="dw_in", exchange=scatter(dw_up))
    (dw_in_sibling,) = _alone(_PairSwap([dw_in]), "swap_w_in")
    dw_in_pairs = _pair_add(dw_in, dw_in_sibling, "pair_w_in")
    dxn, (slots["w_in_t"],) = _mm_pieces(dproj, w_in_t, ta=False, out_dtype=F32, tm=1024, tn=1024, tk=512, name="dxn",
                                  exchange=_ChipScatter([dw_in_pairs]))
    dx, dg1 = _rms_in_bwd(x2, dxn, dh1, norm_mix)
    small = _pack_small(dg1, db[:, :D_MODEL], db[:, D_MODEL:], dg2, dg3, dw, loss)
    (small_slots,) = _alone(_ChipGather([small]), "gather_small")
    return dx.reshape(batch, seq, D_MODEL), slots, small_slots.reshape(N_DEV, 16, D_MODEL)


def _row_tile(r, c):
    return max(d for d in range(16, r + 1, 16) if r % d == 0 and (d * c <= 256 * 1024 or d == 16))


def _pair_add(grad, got, name):
    _, r, c = got.shape
    tr = r
    core = lax.axis_index("c").astype(jnp.int32).reshape(1)

    def body(core_ref, g_ref, a_ref, o_ref):
        o_ref[...] = (g_ref[...].astype(F32) + a_ref[...].astype(F32)).astype(o_ref.dtype)

    blk = pl.BlockSpec((None, tr, c), lambda q, i, core_ref: (q, i, 0))
    return pl.pallas_call(
        body, name=name,
        grid_spec=pltpu.PrefetchScalarGridSpec(
            num_scalar_prefetch=1, grid=(4, r // tr),
            in_specs=[pl.BlockSpec((None, None, tr, c), lambda q, i, core_ref: (q, core_ref[0], i, 0)), blk],
            out_specs=blk),
        out_shape=jax.ShapeDtypeStruct(got.shape, got.dtype),
        compiler_params=_params("parallel", "parallel"),
    )(core, grad.reshape(4, 2, r, c), got)


def _sum_slots(slots, name):
    n, r, c = slots.shape
    tr = _row_tile(r, c)

    def body(s_ref, o_ref):
        acc = s_ref[0].astype(F32)
        for s in range(1, n):
            acc = acc + s_ref[s].astype(F32)
        o_ref[...] = acc

    return pl.pallas_call(
        body, name=name, grid=(r // tr,),
        in_specs=[pl.BlockSpec((n, tr, c), lambda i: (0, i, 0))],
        out_specs=pl.BlockSpec((tr, c), lambda i: (i, 0)),
        out_shape=jax.ShapeDtypeStruct((r, c), F32),
        compiler_params=_params("parallel"),
    )(slots)


def _adamw_math(w, g, m, v):
    m = ADAM_B1 * m + (1.0 - ADAM_B1) * g
    v = ADAM_B2 * v + (1.0 - ADAM_B2) * (g * g)
    m_hat = m / (1.0 - ADAM_B1 ** ADAM_STEP)
    v_hat = v / (1.0 - ADAM_B2 ** ADAM_STEP)
    return -ADAM_LR * (m_hat / (jnp.sqrt(v_hat) + ADAM_EPS) + ADAM_WD * w), m, v


def _adamw(w, slots, m, v, name):
    n, r, c = slots.shape
    tr = _row_tile(r, c)

    def body(w_ref, s_ref, m_ref, v_ref, g_ref, d_ref, nm_ref, nv_ref):
        g = s_ref[0].astype(F32)
        for s in range(1, n):
            g = g + s_ref[s].astype(F32)
        g_ref[...] = g
        d_ref[...], nm_ref[...], nv_ref[...] = _adamw_math(w_ref[...], g, m_ref[...], v_ref[...])

    blk = pl.BlockSpec((tr, c), lambda i: (i, 0))
    return pl.pallas_call(
        body, name=name, grid=(r // tr,),
        in_specs=[blk, pl.BlockSpec((n, tr, c), lambda i: (0, i, 0)), blk, blk], out_specs=[blk] * 4,
        out_shape=[jax.ShapeDtypeStruct((r, c), F32)] * 4,
        compiler_params=_params("parallel"),
    )(w, slots, m, v)


def _adamw_small(ws, gs, ms, vs):
    n = len(ws)

    def body(*refs):
        for i in range(n):
            w_ref, g_ref, m_ref, v_ref = (refs[j * n + i] for j in range(4))
            d_ref, nm_ref, nv_ref = (refs[(4 + j) * n + i] for j in range(3))
            d_ref[...], nm_ref[...], nv_ref[...] = _adamw_math(w_ref[...], g_ref[...], m_ref[...], v_ref[...])

    shapes = [jax.ShapeDtypeStruct(w.shape, F32) for w in ws]
    outs = pl.pallas_call(body, name="adamw_small", out_shape=shapes * 3)(*ws, *gs, *ms, *vs)
    return outs[:n], outs[n:2 * n], outs[2 * n:]


def kernel(x, norm_mix, w_in, b_gate, rel_bias, w_ret_out, w_att_out, w_out, norm_ffn, w_ffn_gate, w_ffn_up, w_ffn_down, norm_final, loss_target, m_norm_mix, m_w_in, m_b_gate, m_rel_bias, m_w_ret_out, m_w_att_out, m_w_out, m_norm_ffn, m_w_ffn_gate, m_w_ffn_up, m_w_ffn_down, m_norm_final, v_norm_mix, v_w_in, v_b_gate, v_rel_bias, v_w_ret_out, v_w_att_out, v_w_out, v_norm_ffn, v_w_ffn_gate, v_w_ffn_up, v_w_ffn_down, v_norm_final):
    me = _index(_place())
    n_rb = rel_bias.shape[-1]

    shard = dict(w_in_t=w_in[0].T, w_gate_t=w_ffn_gate[0].T, w_up_t=w_ffn_up[0].T, w_down=w_ffn_down[0],
                 w_ret=w_ret_out[0], w_out=w_out[0], w_att_t=w_att_out[0].T)
    shard = {nm: s.astype(BF16) for nm, s in shard.items()}
    dx, slots, small_slots = _step(x, loss_target, norm_mix, b_gate, norm_ffn, norm_final, rel_bias[0], shard)
    small_sum = _sum_slots(small_slots, "sum_small")
    loss = small_sum[5, 0]

    transposed = dict(w_in="w_in_t", w_ffn_gate="w_gate_t", w_ffn_up="w_up_t", w_att_out="w_att_t")
    plain = dict(w_ffn_down="w_down", w_ret_out="w_ret", w_out="w_out")
    g = dict(
        norm_mix=small_sum[0:1], b_gate=jnp.concatenate([small_sum[1:2], small_sum[2:3]], axis=1),
        norm_ffn=small_sum[3:4], norm_final=small_sum[4:5],
        rel_bias=lax.dynamic_slice_in_dim(_bias_rows_bwd(small_sum[8:16]), me * n_rb, n_rb, axis=1),
    )
    w = dict(norm_mix=norm_mix, w_in=w_in, b_gate=b_gate, rel_bias=rel_bias, w_ret_out=w_ret_out, w_att_out=w_att_out,
             w_out=w_out, norm_ffn=norm_ffn, w_ffn_gate=w_ffn_gate, w_ffn_up=w_ffn_up, w_ffn_down=w_ffn_down,
             norm_final=norm_final)
    m = dict(norm_mix=m_norm_mix, w_in=m_w_in, b_gate=m_b_gate, rel_bias=m_rel_bias, w_ret_out=m_w_ret_out,
             w_att_out=m_w_att_out, w_out=m_w_out, norm_ffn=m_norm_ffn, w_ffn_gate=m_w_ffn_gate, w_ffn_up=m_w_ffn_up,
             w_ffn_down=m_w_ffn_down, norm_final=m_norm_final)
    v = dict(norm_mix=v_norm_mix, w_in=v_w_in, b_gate=v_b_gate, rel_bias=v_rel_bias, w_ret_out=v_w_ret_out,
             w_att_out=v_w_att_out, w_out=v_w_out, norm_ffn=v_norm_ffn, w_ffn_gate=v_w_ffn_gate, w_ffn_up=v_w_ffn_up,
             w_ffn_down=v_w_ffn_down, norm_final=v_norm_final)
    order = ("norm_mix", "w_in", "b_gate", "rel_bias", "w_ret_out", "w_att_out", "w_out", "norm_ffn",
             "w_ffn_gate", "w_ffn_up", "w_ffn_down", "norm_final")
    small_names = ("norm_mix", "b_gate", "rel_bias", "norm_ffn", "norm_final")

    def flat(a):
        return a[0] if a.ndim == 3 else a.reshape(-1, a.shape[-1])

    grad, delta, new_m, new_v = {}, {}, {}, {}
    for nm in order:
        if nm in transposed:
            res = _adamw(w[nm][0].T, slots[transposed[nm]], m[nm][0].T, v[nm][0].T, "adamw_" + nm)
            grad[nm], delta[nm], new_m[nm], new_v[nm] = (a.T[None] for a in res)
        elif nm in plain:
            res = _adamw(flat(w[nm]), slots[plain[nm]], flat(m[nm]), flat(v[nm]), "adamw_" + nm)
            grad[nm], delta[nm], new_m[nm], new_v[nm] = (a.reshape(w[nm].shape) for a in res)
    ds, nms, nvs = _adamw_small([flat(w[nm]) for nm in small_names], [g[nm] for nm in small_names],
                                [flat(m[nm]) for nm in small_names], [flat(v[nm]) for nm in small_names])
    for i, nm in enumerate(small_names):
        grad[nm], delta[nm], new_m[nm], new_v[nm] = (a.reshape(w[nm].shape) for a in (g[nm], ds[i], nms[i], nvs[i]))

    return (loss, dx, *[grad[nm] for nm in order], *[delta[nm] for nm in order],
            *[new_m[nm] for nm in order], *[new_v[nm] for nm in order])
```

```python
import numpy as np
import jax
import jax.numpy as jnp
from jax import lax
from jax.experimental import pallas as pl
from jax.experimental.pallas import tpu as pltpu

F32 = jnp.float32
BF16 = jnp.bfloat16
MESH = pl.DeviceIdType.MESH

D_MODEL = 1024
CHUNK = 64
RET_HEADS = 4
RET_KEY_DIM = 128
RET_VAL_DIM = 256
ATT_HEADS = 8
BAND_CHUNKS = 8
MAX_REL = 256
N_REL = CHUNK + MAX_REL
D_FF = 2816
N_IN = 6656
ROPE_BASE = 10000.0
EPS = 1e-6
NEG_INF = -1e30
C_RQ, C_RK, C_RV, C_RG, C_AQ, C_AK, C_AV, C_GL = 0, 512, 1024, 2048, 3072, 3584, 4096, 4608

ADAM_LR = 0.001
ADAM_B1 = 0.9
ADAM_B2 = 0.999
ADAM_EPS = 1e-08
ADAM_WD = 0.01
ADAM_STEP = 10

N_DEV = 8
LANES = 128
RET_TILE = 256
ATT_Q = 256
ATT_PAD = BAND_CHUNKS * CHUNK
ATT_WIN = ATT_PAD + ATT_Q
ATT_STARTS = ATT_PAD // ATT_Q
ATT_ROWS = 32
BIAS_LEN = 1024
VMEM_LIMIT = 48 * 1024 * 1024
VMEM_LIMIT_ATT_BWD = 56 * 1024 * 1024


def _params(*sem, vmem=VMEM_LIMIT):
    return pltpu.CompilerParams(dimension_semantics=sem, vmem_limit_bytes=vmem)


def _dot(a, b):
    return lax.dot_general(a, b, (((1,), (0,)), ((), ())), preferred_element_type=F32)


def _dot_nt(a, b):
    return lax.dot_general(a, b, (((1,), (1,)), ((), ())), preferred_element_type=F32)


def _dot_tn(a, b):
    return lax.dot_general(a, b, (((0,), (0,)), ((), ())), preferred_element_type=F32)


def _sigmoid(x):
    return 1.0 / (1.0 + jnp.exp(-x))


def _rms_bwd(x, g, dy):
    r = lax.rsqrt(jnp.mean(x * x, axis=-1, keepdims=True) + EPS)
    u = dy * g
    dx = r * u - x * (r * r * r) * jnp.mean(u * x, axis=-1, keepdims=True)
    return dx, dy * x * r


def _place():
    return lax.axis_index("x"), lax.axis_index("y"), lax.axis_index("c")


def _peer(k):
    x, y, c = _place()
    return ((1 - x) if k & 4 else x, (1 - y) if k & 2 else y, (1 - c) if k & 1 else c)


def _index(place):
    return 4 * place[0] + 2 * place[1] + place[2]


def _rows(ref, block, nrows):
    align = 16 if ref.dtype == BF16 else 8
    return ref.at[pl.ds(pl.multiple_of(block * nrows, align), nrows)]


class _Scatter:
    def __init__(self, arrays):
        self.arrays, self.n = list(arrays), len(arrays)

    def out_shape(self):
        return [jax.ShapeDtypeStruct((N_DEV, a.shape[0] // N_DEV) + a.shape[1:], a.dtype) for a in self.arrays]

    def scratch(self):
        return [pltpu.SemaphoreType.DMA((self.n, N_DEV - 1)), pltpu.SemaphoreType.DMA((self.n, N_DEV - 1)),
                pltpu.SemaphoreType.DMA((self.n,))]

    def _copies(self, ins, outs, sems):
        send_sems, recv_sems, local_sems = sems
        me = _index(_place())

        def src(w, to):
            return _rows(ins[w], to, ins[w].shape[0] // N_DEV)

        def dst(w, origin):
            return outs[w].at[origin]

        def remote(w, k, to, origin):
            return pltpu.make_async_remote_copy(src_ref=src(w, to), dst_ref=dst(w, origin),
                                                send_sem=send_sems.at[w, k - 1], recv_sem=recv_sems.at[w, k - 1],
                                                device_id=_peer(k), device_id_type=MESH)

        pairs = [(w, k) for w in range(self.n) for k in range(1, N_DEV)]
        own = lambda: [pltpu.make_async_copy(src(w, me), dst(w, me), local_sems.at[w]) for w in range(self.n)]
        sent = lambda: [remote(w, k, _index(_peer(k)), me) for w, k in pairs]
        arriving = lambda: [remote(w, k, me, _index(_peer(k))) for w, k in pairs]
        return own, sent, arriving

    def start(self, ins, outs, sems):
        own, sent, _ = self._copies(ins, outs, sems)
        for cp in own() + sent():
            cp.start()

    def wait(self, ins, outs, sems):
        own, sent, arriving = self._copies(ins, outs, sems)
        for cp in arriving():
            cp.wait_recv()
        for cp in sent():
            cp.wait_send()
        for cp in own():
            cp.wait()


class _DirectGather:
    def __init__(self, arrays):
        self.arrays, self.n = list(arrays), len(arrays)

    def out_shape(self):
        return [jax.ShapeDtypeStruct((N_DEV * a.shape[0],) + a.shape[1:], a.dtype) for a in self.arrays]

    def scratch(self):
        return [pltpu.SemaphoreType.DMA((self.n, N_DEV - 1)), pltpu.SemaphoreType.DMA((self.n, N_DEV - 1)),
                pltpu.SemaphoreType.DMA((self.n,))]

    def _copies(self, ins, outs, sems):
        send_sems, recv_sems, local_sems = sems
        me = _index(_place())

        def remote(w, k, origin):
            return pltpu.make_async_remote_copy(src_ref=ins[w], dst_ref=_rows(outs[w], origin, ins[w].shape[0]),
                                                send_sem=send_sems.at[w, k - 1], recv_sem=recv_sems.at[w, k - 1],
                                                device_id=_peer(k), device_id_type=MESH)

        pairs = [(w, k) for w in range(self.n) for k in range(1, N_DEV)]
        own = lambda: [pltpu.make_async_copy(ins[w], _rows(outs[w], me, ins[w].shape[0]), local_sems.at[w])
                       for w in range(self.n)]
        sent = lambda: [remote(w, k, me) for w, k in pairs]
        arriving = lambda: [remote(w, k, _index(_peer(k))) for w, k in pairs]
        return own, sent, arriving

    def start(self, ins, outs, sems):
        own, sent, _ = self._copies(ins, outs, sems)
        for cp in own() + sent():
            cp.start()

    def wait(self, ins, outs, sems):
        own, sent, arriving = self._copies(ins, outs, sems)
        for cp in arriving():
            cp.wait_recv()
        for cp in sent():
            cp.wait_send()
        for cp in own():
            cp.wait()


class _PairSwap:
    def __init__(self, arrays):
        self.arrays, self.n = list(arrays), len(arrays)

    def out_shape(self):
        return [jax.ShapeDtypeStruct((4, a.shape[0] // N_DEV) + a.shape[1:], a.dtype) for a in self.arrays]

    def scratch(self):
        return [pltpu.SemaphoreType.DMA((self.n, 4)), pltpu.SemaphoreType.DMA((self.n, 4))]

    def _copies(self, ins, outs, sems):
        send_sems, recv_sems = sems
        x, y, c = _place()
        return [pltpu.make_async_remote_copy(
            src_ref=_rows(ins[w], 2 * q + 1 - c, ins[w].shape[0] // N_DEV), dst_ref=outs[w].at[q],
            send_sem=send_sems.at[w, q], recv_sem=recv_sems.at[w, q],
            device_id=(x, y, 1 - c), device_id_type=MESH) for w in range(self.n) for q in range(4)]

    def start(self, ins, outs, sems):
        for cp in self._copies(ins, outs, sems):
            cp.start()

    def wait(self, ins, outs, sems):
        for cp in self._copies(ins, outs, sems):
            cp.wait()


class _ChipScatter:
    def __init__(self, arrays):
        self.arrays, self.n = list(arrays), len(arrays)

    def out_shape(self):
        return [jax.ShapeDtypeStruct(a.shape, a.dtype) for a in self.arrays]

    def scratch(self):
        return [pltpu.SemaphoreType.DMA((self.n, 3)), pltpu.SemaphoreType.DMA((self.n, 3)),
                pltpu.SemaphoreType.DMA((self.n,))]

    def _copies(self, ins, outs, sems):
        send_sems, recv_sems, local_sems = sems
        x, y, c = _place()
        mine = 2 * x + y
        sent, arriving = [], []
        for w in range(self.n):
            for k in range(1, 4):
                tx, ty = (1 - x) if k & 2 else x, (1 - y) if k & 1 else y
                other = 2 * tx + ty
                sent.append(lambda w=w, k=k, tx=tx, ty=ty, other=other: pltpu.make_async_remote_copy(
                    src_ref=ins[w].at[other], dst_ref=outs[w].at[mine], send_sem=send_sems.at[w, k - 1],
                    recv_sem=recv_sems.at[w, k - 1], device_id=(tx, ty, c), device_id_type=MESH))
                arriving.append(lambda w=w, k=k, tx=tx, ty=ty, other=other: pltpu.make_async_remote_copy(
                    src_ref=ins[w].at[mine], dst_ref=outs[w].at[other], send_sem=send_sems.at[w, k - 1],
                    recv_sem=recv_sems.at[w, k - 1], device_id=(tx, ty, c), device_id_type=MESH))
        own = [lambda w=w: pltpu.make_async_copy(ins[w].at[mine], outs[w].at[mine], local_sems.at[w])
               for w in range(self.n)]
        return own, sent, arriving

    def start(self, ins, outs, sems):
        own, sent, _ = self._copies(ins, outs, sems)
        for cp in own + sent:
            cp().start()

    def wait(self, ins, outs, sems):
        own, sent, arriving = self._copies(ins, outs, sems)
        for cp in arriving:
            cp().wait_recv()
        for cp in sent:
            cp().wait_send()
        for cp in own:
            cp().wait()


class _ChipGather:
    def __init__(self, arrays, parts=None, into=None):
        self.arrays, self.n, self.into = list(arrays), len(arrays), into
        self.parts = parts or [(0, 1)] * self.n

    def out_shape(self):
        return [jax.ShapeDtypeStruct((N_DEV * a.shape[0],) + a.shape[1:], a.dtype) for a in self.arrays]

    def scratch(self):
        return [pltpu.SemaphoreType.DMA((self.n, N_DEV - 1)), pltpu.SemaphoreType.DMA((self.n, N_DEV - 1)),
                pltpu.SemaphoreType.DMA((self.n,))]

    def _parts(self, ins, outs, sems):
        send_sems, recv_sems, local_sems = sems
        x, y, c = _place()
        me, sibling = (x, y, c), (x, y, 1 - c)
        chips = [(1 - x, y), (x, 1 - y), (1 - x, 1 - y)]

        def rows(w, place, whole):
            (index, count), r = self.parts[w], ins[w].shape[0]
            lo, size = (0, r) if whole else (index * (r // count), r // count)
            align = 16 if ins[w].dtype == BF16 else 8
            return outs[w].at[pl.ds(pl.multiple_of(_index(place) * r + lo, align), size)]

        def mine(w, whole):
            (index, count), r = self.parts[w], ins[w].shape[0]
            return ins[w] if whole or count == 1 else ins[w].at[pl.ds(index * (r // count), r // count)]

        def copy(w, k, block, to, own=False):
            whole = k == 0
            return pltpu.make_async_remote_copy(src_ref=mine(w, whole) if own else rows(w, block, whole),
                                                dst_ref=rows(w, block, whole),
                                                send_sem=send_sems.at[w, k], recv_sem=recv_sems.at[w, k],
                                                device_id=to, device_id_type=MESH)

        def local(w):
            return pltpu.make_async_copy(ins[w], rows(w, me, True), local_sems.at[w])

        return me, sibling, chips, c, copy, local, [index == 0 for index, _ in self.parts]

    def start(self, ins, outs, sems):
        me, sibling, chips, c, copy, local, places_own = self._parts(ins, outs, sems)
        for w in range(self.n):
            if places_own[w]:
                local(w).start()
                copy(w, 0, me, sibling, own=True).start()
            for j, chip in enumerate(chips):
                copy(w, 1 + j, me, (*chip, c), own=True).start()

    def wait(self, ins, outs, sems):
        me, sibling, chips, c, copy, local, places_own = self._parts(ins, outs, sems)
        for w in range(self.n):
            for j, chip in enumerate(chips):
                copy(w, 1 + j, (*chip, c), me).wait_recv()
                copy(w, 4 + j, (*chip, c), sibling).start()
        for w in range(self.n):
            if places_own[w]:
                copy(w, 0, sibling, me).wait_recv()
                copy(w, 0, me, sibling, own=True).wait_send()
                local(w).wait()
            for j, chip in enumerate(chips):
                copy(w, 4 + j, (*chip, 1 - c), me).wait_recv()
                copy(w, 1 + j, me, (*chip, c), own=True).wait_send()
                copy(w, 4 + j, (*chip, c), sibling).wait_send()


def _call(body, *, name, grid, in_specs, out_specs, out_shape, scratch=(), semantics, args, exchange=None,
          vmem=VMEM_LIMIT):
    if exchange is None:
        return pl.pallas_call(body, name=name, grid=grid, in_specs=in_specs, out_specs=out_specs, out_shape=out_shape,
                              scratch_shapes=list(scratch),
                              compiler_params=_params(*semantics, vmem=vmem))(*args), None
    n_in, n_out, n_scr, nx = len(in_specs), len(out_specs), len(scratch), exchange.n
    into = list(getattr(exchange, "into", None) or [])

    def full_body(*refs):
        ins, refs = refs[:n_in], refs[n_in:]
        x_in, refs = refs[:nx], refs[nx + len(into):]
        outs, refs = refs[:n_out], refs[n_out:]
        x_out, refs = refs[:nx], refs[nx:]
        scr, sems = refs[:n_scr], refs[n_scr:]
        first, last = True, True
        for axis, size in enumerate(grid):
            first = jnp.logical_and(first, pl.program_id(axis) == 0)
            last = jnp.logical_and(last, pl.program_id(axis) == size - 1)
        if grid:
            pl.when(first)(lambda: exchange.start(x_in, x_out, sems))
        else:
            exchange.start(x_in, x_out, sems)
        body(*ins, *outs, *scr)
        if grid:
            pl.when(last)(lambda: exchange.wait(x_in, x_out, sems))
        else:
            exchange.wait(x_in, x_out, sems)

    hbm = pl.BlockSpec(memory_space=pltpu.HBM)
    res = pl.pallas_call(
        full_body, name=name, grid=grid,
        in_specs=list(in_specs) + [hbm] * (nx + len(into)), out_specs=list(out_specs) + [hbm] * nx,
        out_shape=list(out_shape) + exchange.out_shape(),
        scratch_shapes=list(scratch) + exchange.scratch(),
        input_output_aliases={n_in + nx + w: n_out + w for w in range(len(into))},
        compiler_params=_params(*(["arbitrary"] * len(grid)), vmem=vmem),
    )(*args, *exchange.arrays, *into)
    return res[:n_out], res[n_out:]


def _alone(exchange, name):
    return _call(lambda: None, name=name, grid=(), in_specs=[], out_specs=[], out_shape=[], semantics=(),
                 args=(), exchange=exchange)[1]


def _mm(a, b, *, ta=False, tb=False, out_dtype, tm, tn, tk, name, exchange=None):
    m, k = (a.shape[1], a.shape[0]) if ta else a.shape
    n = b.shape[0] if tb else b.shape[1]
    assert k == (b.shape[1] if tb else b.shape[0])
    tm, tn, tk = min(tm, m), min(tn, n), min(tk, k)
    assert m % tm == 0 and n % tn == 0 and k % tk == 0, (name, m, n, k)
    nk = k // tk
    dims = (((0 if ta else 1,), (1 if tb else 0,)), ((), ()))

    def body(a_ref, b_ref, o_ref, *acc):
        prod = lax.dot_general(a_ref[...].astype(BF16), b_ref[...].astype(BF16), dims, preferred_element_type=F32)
        if nk == 1:
            o_ref[...] = prod.astype(o_ref.dtype)
            return
        acc_ref, kk = acc[0], pl.program_id(2)

        @pl.when(kk == 0)
        def _():
            acc_ref[...] = prod

        @pl.when((kk > 0) & (kk < nk - 1))
        def _():
            acc_ref[...] += prod

        @pl.when(kk == nk - 1)
        def _():
            o_ref[...] = (acc_ref[...] + prod).astype(o_ref.dtype)

    a_spec = (pl.BlockSpec((tk, tm), lambda i, j, kk: (kk, i)) if ta
              else pl.BlockSpec((tm, tk), lambda i, j, kk: (i, kk)))
    b_spec = (pl.BlockSpec((tn, tk), lambda i, j, kk: (j, kk)) if tb
              else pl.BlockSpec((tk, tn), lambda i, j, kk: (kk, j)))
    (out,), moved = _call(
        body, name=name, grid=(m // tm, n // tn, nk),
        in_specs=[a_spec, b_spec],
        out_specs=[pl.BlockSpec((tm, tn), lambda i, j, kk: (i, j))],
        out_shape=[pltpu.HBM((m, n), out_dtype)],
        scratch=[pltpu.VMEM((tm, tn), F32)] if nk > 1 else [],
        semantics=("parallel", "parallel", "arbitrary"), args=(a, b), exchange=exchange)
    return out if exchange is None else (out, moved)


def _mm_pieces(pieces, b, *, ta, out_dtype, tm, tn, tk, name, exchange=None):
    rows, n = pieces[0].shape[0], b.shape[1]
    step = tm if ta else tk
    assert all(p.shape[0] == rows and p.shape[1] % step == 0 for p in pieces), name
    edges = [int(e) for e in np.cumsum([0] + [p.shape[1] // step for p in pieces])]
    total = edges[-1] * step
    m, k = (total, rows) if ta else (rows, total)
    assert b.shape[0] == k and m % tm == 0 and n % tn == 0 and k % tk == 0, name
    nk, npieces = k // tk, len(pieces)
    dims = (((0 if ta else 1,), (0,)), ((), ()))
    b_resident = ta and n == tn

    def body(*refs):
        a_refs, (b_ref, o_ref, acc_ref) = refs[:npieces], refs[npieces:]
        kk = pl.program_id(2)
        pos = pl.program_id(0) if ta else kk

        @pl.when(kk == 0)
        def _():
            acc_ref[...] = jnp.zeros_like(acc_ref)

        def b_tile():
            return b_ref[pl.ds(pl.multiple_of(kk * tk, tk), tk), :] if b_resident else b_ref[...]

        for p, a_ref in enumerate(a_refs):
            @pl.when((pos >= edges[p]) & (pos < edges[p + 1]))
            def _(a_ref=a_ref):
                acc_ref[...] += lax.dot_general(a_ref[...], b_tile(), dims, preferred_element_type=F32)

        @pl.when(kk == nk - 1)
        def _():
            o_ref[...] = acc_ref[...].astype(o_ref.dtype)

    def a_spec(p):
        lo, last = edges[p], edges[p + 1] - edges[p] - 1
        if ta:
            def index(i, j, kk):
                inside = (i >= lo) & (i <= lo + last)
                return jnp.where(inside, kk, 0), jnp.clip(i - lo, 0, last)
            return pl.BlockSpec((tk, tm), index)
        return pl.BlockSpec((tm, tk), lambda i, j, kk: (i, jnp.clip(kk - lo, 0, last)))

    (out,), moved = _call(
        body, name=name, grid=(m // tm, n // tn, nk),
        in_specs=[a_spec(p) for p in range(npieces)]
                 + [pl.BlockSpec(b.shape, lambda i, j, kk: (0, 0)) if b_resident
                    else pl.BlockSpec((tk, tn), lambda i, j, kk: (kk, j))],
        out_specs=[pl.BlockSpec((tm, tn), lambda i, j, kk: (i, j))],
        out_shape=[pltpu.HBM((m, n), out_dtype)],
        scratch=[pltpu.VMEM((tm, tn), F32)],
        semantics=("parallel", "parallel", "arbitrary"), args=(*pieces, b), exchange=exchange)
    return out if exchange is None else (out, moved)


def _rms_fwd(x2, g, exchange):
    t = x2.shape[0]
    tm = min(512, t)

    def body(x_ref, g_ref, o_ref):
        x = x_ref[...]
        r = lax.rsqrt(jnp.mean(x * x, axis=-1, keepdims=True) + EPS)
        o_ref[...] = (x * r * g_ref[...]).astype(o_ref.dtype)

    return _call(
        body, name="rms_in_fwd", grid=(t // tm,),
        in_specs=[pl.BlockSpec((tm, D_MODEL), lambda i: (i, 0)), pl.BlockSpec((1, D_MODEL), lambda i: (0, 0))],
        out_specs=[pl.BlockSpec((tm, D_MODEL), lambda i: (i, 0))],
        out_shape=[jax.ShapeDtypeStruct((t, D_MODEL), BF16)],
        semantics=("parallel",), args=(x2, g), exchange=exchange)


def _decay(lg):
    row = lax.broadcasted_iota(jnp.int32, (RET_TILE, RET_TILE), 0)
    col = lax.broadcasted_iota(jnp.int32, (RET_TILE, RET_TILE), 1)
    within = jnp.exp(lg * jnp.abs(row - col).astype(F32))
    inside = jnp.where((col >> 6) <= (row >> 6), within, 0.0)
    pos = lax.broadcasted_iota(jnp.int32, (RET_TILE, 1), 0).astype(F32)
    q_dec = jnp.exp(lg * (pos + 1.0))
    k_dec = jnp.exp(lg * (RET_TILE - 1.0 - pos))
    tile_dec = jnp.exp(lg * float(RET_TILE))
    return inside, q_dec, k_dec, tile_dec


def _scaled(a_bf16, dec):
    return (a_bf16.astype(F32) * dec).astype(BF16)


def _ret_specs(seq):
    key = lambda base: pl.BlockSpec((seq, RET_KEY_DIM), lambda b, h: (b, base // RET_KEY_DIM + h))
    val = lambda base: pl.BlockSpec((seq, RET_VAL_DIM), lambda b, h: (b, base // RET_VAL_DIM + h))
    tab = pl.BlockSpec((seq, RET_KEY_DIM), lambda b, h: (0, 0))
    lgs = pl.BlockSpec((None, 1, LANES), lambda b, h: (h, 0, 0))
    return key, val, tab, lgs


def _ret_fwd(proj, cs, sn, lg_arr, batch, seq, exchange):
    t = batch * seq
    nt = seq // RET_TILE

    def body(q_ref, k_ref, v_ref, rg_ref, cs_ref, sn_ref, lg_ref, gro_ref, o_ref, qr_ref, kr_ref):
        lg = lg_ref[:, 0:1]
        cs_t, sn_t = cs_ref[...], sn_ref[...]
        q = q_ref[...].astype(F32)
        k = k_ref[...].astype(F32)
        qr_ref[...] = (q * cs_t + pltpu.roll(q, 64, 1) * sn_t).astype(BF16)
        kr_ref[...] = ((k * cs_t + pltpu.roll(k, 64, 1) * sn_t) * (RET_KEY_DIM ** -0.5)).astype(BF16)
        inside, q_dec, k_dec, tile_dec = _decay(lg)
        state = jnp.zeros((RET_KEY_DIM, RET_VAL_DIM), F32)
        for i in range(nt):
            rows = slice(i * RET_TILE, (i + 1) * RET_TILE)
            qi, ki, vi = qr_ref[rows, :], kr_ref[rows, :], v_ref[rows, :]
            acc = _dot((_dot_nt(qi, ki) * inside).astype(BF16), vi)
            if i > 0:
                acc = acc + _dot(_scaled(qi, q_dec), state.astype(BF16))
            if i < nt - 1:
                state = state * tile_dec + _dot_tn(_scaled(ki, k_dec), vi)
            o_ref[rows, :] = acc
            xc = acc - jnp.mean(acc, axis=-1, keepdims=True)
            nrm = xc * lax.rsqrt(jnp.mean(xc * xc, axis=-1, keepdims=True) + EPS)
            rg = rg_ref[rows, :].astype(F32)
            gro_ref[rows, :] = (rg * _sigmoid(rg) * nrm).astype(BF16)

    key, val, tab, lgs = _ret_specs(seq)
    return _call(
        body, name="ret_fwd", grid=(batch, RET_HEADS),
        in_specs=[key(C_RQ), key(C_RK), val(C_RV), val(C_RG), tab, tab, lgs],
        out_specs=[val(0), val(0), key(0), key(0)],
        out_shape=[jax.ShapeDtypeStruct((t, RET_HEADS * RET_VAL_DIM), BF16),
                   jax.ShapeDtypeStruct((t, RET_HEADS * RET_VAL_DIM), F32),
                   jax.ShapeDtypeStruct((t, RET_HEADS * RET_KEY_DIM), BF16),
                   jax.ShapeDtypeStruct((t, RET_HEADS * RET_KEY_DIM), BF16)],
        semantics=("parallel", "parallel"), args=(proj, proj, proj, proj, cs, sn, lg_arr), exchange=exchange)


def _att_bias(w_ref, bias_ref):
    n_i = lax.broadcasted_iota(jnp.int32, (ATT_Q, BIAS_LEN), 0)
    qc = lax.broadcasted_iota(jnp.int32, (ATT_Q, ATT_WIN), 0) >> 6
    kc = lax.broadcasted_iota(jnp.int32, (ATT_Q, ATT_WIN), 1) >> 6
    dc = qc + BAND_CHUNKS - kc
    band = (dc >= 0) & (dc <= BAND_CHUNKS)
    key = lax.broadcasted_iota(jnp.int32, (ATT_Q, ATT_WIN), 1)
    for e in range(2):
        xw = jnp.broadcast_to(w_ref[e:e + 1, :], (ATT_Q, BIAS_LEN))
        for bit in range(8):
            xw = jnp.where(((n_i >> bit) & 1) == 1, pltpu.roll(xw, 1 << bit, 1), xw)
        bias = jnp.where(band, xw[:, BIAS_LEN - ATT_WIN:], NEG_INF)
        for first in range(ATT_STARTS):
            bias_ref[first, e] = jnp.where(key + (first * ATT_Q - ATT_PAD) >= 0, bias, NEG_INF)
        bias_ref[ATT_STARTS, e] = bias


ATT_PAIRS = 2
ATT_COLS = ATT_PAIRS * LANES


def _att_specs(batch, seq):
    ni = seq // ATT_Q
    q_spec = pl.BlockSpec((ATT_Q, ATT_COLS), lambda g, b, i: (b * ni + i, C_AQ // ATT_COLS + g))
    k_spec = pl.BlockSpec((seq, ATT_COLS), lambda g, b, i: (b, C_AK // ATT_COLS + g))
    v_spec = pl.BlockSpec((seq, ATT_COLS), lambda g, b, i: (b, C_AV // ATT_COLS + g))
    w_spec = pl.BlockSpec((ATT_PAIRS, 2, BIAS_LEN), lambda g, b, i: (g, 0, 0))
    b_spec = pl.BlockSpec((ATT_PAIRS, ATT_STARTS + 1, 2, ATT_Q, ATT_WIN), lambda g, b, i: (g, 0, 0, 0, 0))
    pad = pltpu.VMEM((seq + ATT_PAD, ATT_COLS), BF16)
    return ni, q_spec, k_spec, v_spec, w_spec, b_spec, pad


def _att_bias_tiles(wvec, exchange):
    (tiles,), moved = _call(
        lambda w_ref, o_ref: _att_bias(w_ref, o_ref), name="att_bias", grid=(ATT_HEADS // 2,),
        in_specs=[pl.BlockSpec((None, 2, BIAS_LEN), lambda hp: (hp, 0, 0))],
        out_specs=[pl.BlockSpec((None, ATT_STARTS + 1, 2, ATT_Q, ATT_WIN), lambda hp: (hp, 0, 0, 0, 0))],
        out_shape=[jax.ShapeDtypeStruct((ATT_HEADS // 2, ATT_STARTS + 1, 2, ATT_Q, ATT_WIN), F32)],
        semantics=("parallel",), args=(wvec,), exchange=exchange)
    return tiles, moved


def _att_pad(src_ref, pad_ref):
    pad_ref[:ATT_PAD, :] = jnp.zeros((ATT_PAD, ATT_COLS), BF16)
    pad_ref[ATT_PAD:, :] = src_ref[...]


def _att_head(q2, sel):
    return jnp.where(sel, q2, jnp.zeros_like(q2)) * 0.125


def _att_softmax_rows(s_ref, bias_ref, rows):
    s = s_ref[rows, :] + bias_ref[rows, :]
    ex = jnp.exp(s - jnp.max(s, axis=-1, keepdims=True))
    return ex, 1.0 / jnp.sum(ex, axis=-1, keepdims=True)


def _att_fwd(proj, bias, batch, seq, exchange):
    ni, q_spec, k_spec, v_spec, _, b_spec, pad = _att_specs(batch, seq)

    def body(q_ref, k_ref, v_ref, bias_ref, o_ref, kp_ref, vp_ref, s_ref, e_ref):
        i = pl.program_id(2)

        @pl.when(i == 0)
        def _():
            _att_pad(k_ref, kp_ref)
            _att_pad(v_ref, vp_ref)

        win = pl.ds(pl.multiple_of(i * ATT_Q, ATT_Q), ATT_WIN)
        lo = lax.broadcasted_iota(jnp.int32, (1, LANES), 1) < 64
        start = jnp.minimum(i, ATT_STARTS)
        for pair in range(ATT_PAIRS):
            cols = slice(pair * LANES, (pair + 1) * LANES)
            k2, v2, q2 = kp_ref[win, cols], vp_ref[win, cols], q_ref[:, cols]
            out = jnp.zeros((ATT_Q, LANES), F32)
            for e in range(2):
                h = 2 * pair + e
                sel = lo if e == 0 else jnp.logical_not(lo)
                s_ref[h] = _dot_nt(_att_head(q2, sel), k2)
                rsum = []
                for c in range(ATT_Q // ATT_ROWS):
                    rows = slice(c * ATT_ROWS, (c + 1) * ATT_ROWS)
                    ex, r = _att_softmax_rows(s_ref.at[h], bias_ref.at[pair, start, e], rows)
                    e_ref[h, rows, :] = ex.astype(BF16)
                    rsum.append(r)
                out = out + _dot(e_ref[h], jnp.where(sel, v2, jnp.zeros_like(v2))) * jnp.concatenate(rsum, axis=0)
            o_ref[:, cols] = out.astype(BF16)

    heads = 2 * ATT_PAIRS
    return _call(
        body, name="att_fwd", grid=(ATT_HEADS // heads, batch, ni),
        in_specs=[q_spec, k_spec, v_spec, b_spec],
        out_specs=[pl.BlockSpec((ATT_Q, ATT_COLS), lambda g, b, i: (b * ni + i, g))],
        out_shape=[jax.ShapeDtypeStruct((batch * seq, ATT_HEADS * 64), BF16)],
        scratch=[pad, pad, pltpu.VMEM((heads, ATT_Q, ATT_WIN), F32), pltpu.VMEM((heads, ATT_Q, ATT_WIN), BF16)],
        semantics=("arbitrary", "arbitrary", "arbitrary"), args=(proj, proj, proj, bias), exchange=exchange)


GL_HALF = 512


def _gl_specs(tm):
    return [pl.BlockSpec((tm, GL_HALF), lambda i, c=C_GL // GL_HALF + j: (i, c)) for j in range(4)]


def _gates(gl_refs, b_ref):
    logits = [ref[...].astype(F32) for ref in gl_refs]
    gr = _sigmoid(jnp.concatenate(logits[:2], axis=1) + b_ref[:, :D_MODEL])
    ga = _sigmoid(jnp.concatenate(logits[2:], axis=1) + b_ref[:, D_MODEL:])
    return gr, ga


def _whole(a):
    return pl.BlockSpec(a.shape, lambda i: (0,) * a.ndim)


def _mix_out_fwd(gro, ao, proj, b_gate, w_ret, w_att_t, x2, w_out, g2):
    t = gro.shape[0]
    tm = min(512, t)

    def body(gro_ref, ao_ref, gl0, gl1, gl2, gl3, b_ref, wr_ref, wa_ref, x_ref, wo_ref, g_ref,
             z_ref, yr_ref, ya_ref, h_ref, hn_ref):
        yr = _dot(gro_ref[...], wr_ref[...])
        ya = _dot_nt(ao_ref[...], wa_ref[...])
        yr_ref[...] = yr.astype(BF16)
        ya_ref[...] = ya.astype(BF16)
        gr, ga = _gates((gl0, gl1, gl2, gl3), b_ref)
        z = (gr * yr + ga * ya).astype(BF16)
        z_ref[...] = z
        h = x_ref[...] + _dot(z, wo_ref[...])
        h_ref[...] = h
        r = lax.rsqrt(jnp.mean(h * h, axis=-1, keepdims=True) + EPS)
        hn_ref[...] = (h * r * g_ref[...]).astype(BF16)

    row = pl.BlockSpec((tm, D_MODEL), lambda i: (i, 0))
    return pl.pallas_call(
        body, name="mix_out_fwd", grid=(t // tm,),
        in_specs=[row, pl.BlockSpec((tm, 512), lambda i: (i, 0)), *_gl_specs(tm),
                  _whole(b_gate), _whole(w_ret), _whole(w_att_t), row, _whole(w_out), _whole(g2)],
        out_specs=[row] * 5,
        out_shape=[jax.ShapeDtypeStruct((t, D_MODEL), BF16)] * 3
                  + [jax.ShapeDtypeStruct((t, D_MODEL), F32), jax.ShapeDtypeStruct((t, D_MODEL), BF16)],
        compiler_params=_params("parallel"),
    )(gro, ao, proj, proj, proj, proj, b_gate, w_ret, w_att_t, x2, w_out, g2)


def _col_chunks(width, chunk=384):
    return [slice(lo, min(lo + chunk, width)) for lo in range(0, width, chunk)]


def _ffn_up(hn, wg_t, wu_t):
    t = hn.shape[0]
    tm, tn = min(512, t), D_FF // 2

    def body(h_ref, wg_ref, wu_ref, g_ref, u_ref, a_ref):
        g = _dot_nt(h_ref[...], wg_ref[...])
        u = _dot_nt(h_ref[...], wu_ref[...])
        g_ref[...] = g.astype(BF16)
        u_ref[...] = u.astype(BF16)
        a_ref[...] = (g * _sigmoid(g) * u).astype(BF16)

    w_spec = pl.BlockSpec((tn, D_MODEL), lambda j, i: (j, 0))
    out = pl.BlockSpec((tm, tn), lambda j, i: (i, j))
    return pl.pallas_call(
        body, name="ffn_up", grid=(D_FF // tn, t // tm),
        in_specs=[pl.BlockSpec((tm, D_MODEL), lambda j, i: (i, 0)), w_spec, w_spec],
        out_specs=[out, out, out],
        out_shape=[jax.ShapeDtypeStruct((t, D_FF), BF16)] * 3,
        compiler_params=_params("parallel", "parallel"),
    )(hn, wg_t, wu_t)


def _ffn_down_loss(a, h1, tgt, w_down, g3):
    t = a.shape[0]
    tm = min(512, t)

    def body(a_ref, h_ref, t_ref, w_ref, g_ref, dh_ref, dhb_ref, loss_ref, dg_ref):
        @pl.when(pl.program_id(0) == 0)
        def _():
            loss_ref[...] = jnp.zeros_like(loss_ref)
            dg_ref[...] = jnp.zeros_like(dg_ref)

        g = g_ref[...]
        h2 = h_ref[...] + _dot(a_ref[...], w_ref[...])
        r = lax.rsqrt(jnp.mean(h2 * h2, axis=-1, keepdims=True) + EPS)
        err = h2 * r * g - t_ref[...]
        loss_ref[...] += jnp.sum(err * err) * (0.5 / D_MODEL)
        dy = err * (1.0 / D_MODEL)
        dh, dg_rows = _rms_bwd(h2, g, dy)
        dg_ref[...] += jnp.sum(dg_rows, axis=0, keepdims=True)
        dh_ref[...] = dh
        dhb_ref[...] = dh.astype(BF16)

    row = pl.BlockSpec((tm, D_MODEL), lambda i: (i, 0))
    vec = pl.BlockSpec((1, D_MODEL), lambda i: (0, 0))
    return pl.pallas_call(
        body, name="ffn_down_loss", grid=(t // tm,),
        in_specs=[pl.BlockSpec((tm, D_FF), lambda i: (i, 0)), row, row,
                  pl.BlockSpec((D_FF, D_MODEL), lambda i: (0, 0)), vec],
        out_specs=[row, row, pl.BlockSpec((1, LANES), lambda i: (0, 0)), vec],
        out_shape=[jax.ShapeDtypeStruct((t, D_MODEL), F32), jax.ShapeDtypeStruct((t, D_MODEL), BF16),
                   jax.ShapeDtypeStruct((1, LANES), F32), jax.ShapeDtypeStruct((1, D_MODEL), F32)],
        compiler_params=_params("arbitrary"),
    )(a, h1, tgt, w_down, g3)


def _ffn_bwd_act(dh2b, w_down, g_act, u_act, exchange):
    t = dh2b.shape[0]
    tm, tn = min(1024, t), D_FF // 2

    def body(d_ref, w_ref, g_ref, u_ref, dg_ref, du_ref):
        d = d_ref[...]
        for cols in _col_chunks(tn):
            da = _dot_nt(d, w_ref[cols, :])
            g = g_ref[:, cols].astype(F32)
            u = u_ref[:, cols].astype(F32)
            sg = _sigmoid(g)
            dg_ref[:, cols] = (da * u * sg * (1.0 + g * (1.0 - sg))).astype(BF16)
            du_ref[:, cols] = (da * g * sg).astype(BF16)

    blk = pl.BlockSpec((tm, tn), lambda j, i: (i, j))
    return _call(
        body, name="ffn_bwd_act", grid=(D_FF // tn, t // tm),
        in_specs=[pl.BlockSpec((tm, D_MODEL), lambda j, i: (i, 0)),
                  pl.BlockSpec((tn, D_MODEL), lambda j, i: (j, 0)), blk, blk],
        out_specs=[blk, blk],
        out_shape=[jax.ShapeDtypeStruct((t, D_FF), BF16)] * 2,
        semantics=("parallel", "parallel"), args=(dh2b, w_down, g_act, u_act), exchange=exchange)


def _ffn_bwd_in(dg, du, wg_t, wu_t, h1, dh2, g2, exchange):
    t = dg.shape[0]
    tm = min(256, t)

    def body(dg_ref, du_ref, wg_ref, wu_ref, h_ref, d2_ref, g_ref, dh_ref, dhb_ref, gn_ref):
        @pl.when(pl.program_id(0) == 0)
        def _():
            gn_ref[...] = jnp.zeros_like(gn_ref)

        dhn = _dot(dg_ref[...], wg_ref[...]) + _dot(du_ref[...], wu_ref[...])
        dx, dg_rows = _rms_bwd(h_ref[...], g_ref[...], dhn)
        dh = d2_ref[...] + dx
        dh_ref[...] = dh
        dhb_ref[...] = dh.astype(BF16)
        gn_ref[...] += jnp.sum(dg_rows, axis=0, keepdims=True)

    act = pl.BlockSpec((tm, D_FF), lambda i: (i, 0))
    row = pl.BlockSpec((tm, D_MODEL), lambda i: (i, 0))
    return _call(
        body, name="ffn_bwd_in", grid=(t // tm,),
        in_specs=[act, act, _whole(wg_t), _whole(wu_t), row, row, _whole(g2)],
        out_specs=[row, row, _whole(g2)],
        out_shape=[jax.ShapeDtypeStruct((t, D_MODEL), F32), jax.ShapeDtypeStruct((t, D_MODEL), BF16),
                   jax.ShapeDtypeStruct((1, D_MODEL), F32)],
        semantics=("arbitrary",), args=(dg, du, wg_t, wu_t, h1, dh2, g2), exchange=exchange)


def _mix_bwd(dh1b, w_out, proj, b_gate, y_ret, y_att, w_ret, w_att_t):
    t = dh1b.shape[0]
    tm = min(512, t)

    def body(d_ref, wo_ref, gl0, gl1, gl2, gl3, b_ref, yr_ref, ya_ref, wr_ref, wa_ref,
             dyr_ref, dya_ref, dglr_ref, dgla_ref, dgro_ref, dao_ref, db_ref):
        @pl.when(pl.program_id(0) == 0)
        def _():
            db_ref[...] = jnp.zeros_like(db_ref)

        dz = _dot_nt(d_ref[...], wo_ref[...])
        gr, ga = _gates((gl0, gl1, gl2, gl3), b_ref)
        dyr = (dz * gr).astype(BF16)
        dya = (dz * ga).astype(BF16)
        dyr_ref[...] = dyr
        dya_ref[...] = dya
        dglr = dz * yr_ref[...].astype(F32) * gr * (1.0 - gr)
        dgla = dz * ya_ref[...].astype(F32) * ga * (1.0 - ga)
        dglr_ref[...] = dglr.astype(BF16)
        dgla_ref[...] = dgla.astype(BF16)
        db_ref[:, :D_MODEL] += jnp.sum(dglr, axis=0, keepdims=True)
        db_ref[:, D_MODEL:] += jnp.sum(dgla, axis=0, keepdims=True)
        dgro_ref[...] = _dot_nt(dyr, wr_ref[...]).astype(BF16)
        dao_ref[...] = _dot(dya, wa_ref[...]).astype(BF16)

    row = pl.BlockSpec((tm, D_MODEL), lambda i: (i, 0))
    half = pl.BlockSpec((tm, 512), lambda i: (i, 0))
    return pl.pallas_call(
        body, name="mix_bwd", grid=(t // tm,),
        in_specs=[row, _whole(w_out), *_gl_specs(tm), _whole(b_gate), row, row, _whole(w_ret), _whole(w_att_t)],
        out_specs=[row, row, row, row, row, half, _whole(b_gate)],
        out_shape=[jax.ShapeDtypeStruct((t, D_MODEL), BF16)] * 5
                  + [jax.ShapeDtypeStruct((t, 512), BF16), jax.ShapeDtypeStruct(b_gate.shape, F32)],
        compiler_params=_params("arbitrary"),
    )(dh1b, w_out, proj, proj, proj, proj, b_gate, y_ret, y_att, w_ret, w_att_t)


def _ret_bwd(dgro, proj, o_ret, qr, kr, cs, sn, lg_arr, batch, seq, exchange):
    t = batch * seq
    nt = seq // RET_TILE

    def body(dgro_ref, rg_ref, o_ref, qr_ref, kr_ref, v_ref, cs_ref, sn_ref, lg_ref,
             dq_ref, dk_ref, dv_ref, drg_ref, do_ref, st_ref):
        lg = lg_ref[:, 0:1]
        inside, q_dec, k_dec, tile_dec = _decay(lg)

        state = jnp.zeros((RET_KEY_DIM, RET_VAL_DIM), F32)
        for i in range(nt - 1):
            rows = slice(i * RET_TILE, (i + 1) * RET_TILE)
            state = state * tile_dec + _dot_tn(_scaled(kr_ref[rows, :], k_dec), v_ref[rows, :])
            st_ref[i + 1] = state.astype(BF16)

        for i in range(nt):
            rows = slice(i * RET_TILE, (i + 1) * RET_TILE)
            o = o_ref[rows, :]
            xc = o - jnp.mean(o, axis=-1, keepdims=True)
            rs = lax.rsqrt(jnp.mean(xc * xc, axis=-1, keepdims=True) + EPS)
            nrm = xc * rs
            rg = rg_ref[rows, :].astype(F32)
            sg = _sigmoid(rg)
            dg = dgro_ref[rows, :].astype(F32)
            drg_ref[rows, :] = (dg * nrm * sg * (1.0 + rg * (1.0 - sg))).astype(BF16)
            dn = dg * rg * sg
            do = rs * (dn - jnp.mean(dn, axis=-1, keepdims=True)
                       - nrm * jnp.mean(dn * nrm, axis=-1, keepdims=True))
            do_ref[rows, :] = do.astype(BF16)

        dstate = jnp.zeros((RET_KEY_DIM, RET_VAL_DIM), F32)
        for i in reversed(range(nt)):
            rows = slice(i * RET_TILE, (i + 1) * RET_TILE)
            qi, ki, vi, doi = qr_ref[rows, :], kr_ref[rows, :], v_ref[rows, :], do_ref[rows, :]
            p = (_dot_nt(qi, ki) * inside).astype(BF16)
            dp = (_dot_nt(doi, vi) * inside).astype(BF16)
            dq = _dot(dp, ki)
            dk = _dot_tn(dp, qi)
            dv = _dot_tn(p, doi)
            if i > 0:
                dq = dq + _dot_nt(doi, st_ref[i]) * q_dec
            if i < nt - 1:
                dsb = dstate.astype(BF16)
                dk = dk + _dot_nt(vi, dsb) * k_dec
                dv = dv + _dot(_scaled(ki, k_dec), dsb)
            if i > 0:
                dstate = dstate * tile_dec + _dot_tn(_scaled(qi, q_dec), doi)
            dq_ref[rows, :] = (dq * cs_ref[rows, :] - pltpu.roll(dq, 64, 1) * sn_ref[rows, :]).astype(BF16)
            dk = (dk * cs_ref[rows, :] - pltpu.roll(dk, 64, 1) * sn_ref[rows, :]) * (RET_KEY_DIM ** -0.5)
            dk_ref[rows, :] = dk.astype(BF16)
            dv_ref[rows, :] = dv.astype(BF16)

    key, val, tab, lgs = _ret_specs(seq)
    return _call(
        body, name="ret_bwd", grid=(batch, RET_HEADS),
        in_specs=[val(0), val(C_RG), val(0), key(0), key(0), val(C_RV), tab, tab, lgs],
        out_specs=[key(0), key(0), val(0), val(0)],
        out_shape=[jax.ShapeDtypeStruct((t, RET_HEADS * RET_KEY_DIM), BF16)] * 2
                  + [jax.ShapeDtypeStruct((t, RET_HEADS * RET_VAL_DIM), BF16)] * 2,
        scratch=[pltpu.VMEM((seq, RET_VAL_DIM), BF16), pltpu.VMEM((nt, RET_KEY_DIM, RET_VAL_DIM), BF16)],
        semantics=("parallel", "parallel"), args=(dgro, proj, o_ret, qr, kr, proj, cs, sn, lg_arr),
        exchange=exchange)


def _att_bwd(proj, bias, dao, batch, seq, exchange):
    ni, q_spec, k_spec, v_spec, w_spec, b_spec, pad = _att_specs(batch, seq)
    t = batch * seq

    def body(q_ref, k_ref, v_ref, bias_ref, do_ref, dq_ref, dk_ref, dv_ref, dw_ref,
             dbias_ref, dk_acc, dv_acc, kp_ref, vp_ref, s_ref, dp_ref, e_ref, ds_ref):
        b, i = pl.program_id(1), pl.program_id(2)

        @pl.when((b == 0) & (i == 0))
        def _():
            dbias_ref[...] = jnp.zeros_like(dbias_ref)

        @pl.when(i == 0)
        def _():
            _att_pad(k_ref, kp_ref)
            _att_pad(v_ref, vp_ref)
            dk_acc[...] = jnp.zeros_like(dk_acc)
            dv_acc[...] = jnp.zeros_like(dv_acc)

        win = pl.ds(pl.multiple_of(i * ATT_Q, ATT_Q), ATT_WIN)
        lo = lax.broadcasted_iota(jnp.int32, (1, LANES), 1) < 64
        start = jnp.minimum(i, ATT_STARTS)
        for pair in range(ATT_PAIRS):
            cols = slice(pair * LANES, (pair + 1) * LANES)
            k2, v2, q2, do2 = kp_ref[win, cols], vp_ref[win, cols], q_ref[:, cols], do_ref[:, cols]
            dq = jnp.zeros((ATT_Q, LANES), F32)
            dk = jnp.zeros((LANES, ATT_WIN), F32)
            dv = jnp.zeros((LANES, ATT_WIN), F32)
            for e in range(2):
                h = 2 * pair + e
                sel = lo if e == 0 else jnp.logical_not(lo)
                qm = _att_head(q2, sel)
                dom = jnp.where(sel, do2, jnp.zeros_like(do2))
                s_ref[h] = _dot_nt(qm, k2)
                dp_ref[h] = _dot_nt(dom, v2)
                rsum = []
                for c in range(ATT_Q // ATT_ROWS):
                    rows = slice(c * ATT_ROWS, (c + 1) * ATT_ROWS)
                    ex, r = _att_softmax_rows(s_ref.at[h], bias_ref.at[pair, start, e], rows)
                    dp = dp_ref[h, rows, :]
                    mean = jnp.sum(dp * ex, axis=-1, keepdims=True) * r
                    ds = ex * ((dp - mean) * r)
                    dbias_ref[h, rows, :] += ds
                    ds_ref[h, rows, :] = ds.astype(BF16)
                    e_ref[h, rows, :] = ex.astype(BF16)
                    rsum.append(r)
                dq = dq + _dot(ds_ref[h], jnp.where(sel, k2, jnp.zeros_like(k2)))
                dk = dk + _dot_tn(qm, ds_ref[h])
                dv = dv + _dot_tn((dom.astype(F32) * jnp.concatenate(rsum, axis=0)).astype(BF16), e_ref[h])
            dq_ref[:, cols] = (dq * 0.125).astype(BF16)
            dk_acc[cols, win] += dk
            dv_acc[cols, win] += dv

        @pl.when(i == ni - 1)
        def _():
            dk_ref[...] = dk_acc[:, ATT_PAD:].T.astype(BF16)
            dv_ref[...] = dv_acc[:, ATT_PAD:].T.astype(BF16)

        @pl.when((b == batch - 1) & (i == ni - 1))
        def _():
            n_i = lax.broadcasted_iota(jnp.int32, (ATT_Q, BIAS_LEN), 0)
            for h in range(heads):
                xw = jnp.concatenate([jnp.zeros((ATT_Q, BIAS_LEN - ATT_WIN), F32), dbias_ref[h]], axis=1)
                for bit in range(8):
                    xw = jnp.where(((n_i >> bit) & 1) == 1, pltpu.roll(xw, BIAS_LEN - (1 << bit), 1), xw)
                dw_ref[h // 2, h % 2:h % 2 + 1, :] = jnp.sum(xw, axis=0, keepdims=True)

    heads = 2 * ATT_PAIRS
    seq_blk = pl.BlockSpec((seq, ATT_COLS), lambda g, b, i: (b, g))
    q_out = pl.BlockSpec((ATT_Q, ATT_COLS), lambda g, b, i: (b * ni + i, g))
    tile_f32, tile_bf16 = pltpu.VMEM((heads, ATT_Q, ATT_WIN), F32), pltpu.VMEM((heads, ATT_Q, ATT_WIN), BF16)
    acc = pltpu.VMEM((ATT_COLS, seq + ATT_PAD), F32)
    return _call(
        body, name="att_bwd", grid=(ATT_HEADS // heads, batch, ni),
        in_specs=[q_spec, k_spec, v_spec, b_spec, q_out],
        out_specs=[q_out, seq_blk, seq_blk, w_spec],
        out_shape=[jax.ShapeDtypeStruct((t, 512), BF16)] * 3
                  + [jax.ShapeDtypeStruct((ATT_HEADS // 2, 2, BIAS_LEN), F32)],
        scratch=[tile_f32, acc, acc, pad, pad, tile_f32, tile_f32, tile_bf16, tile_bf16],
        semantics=("arbitrary", "arbitrary", "arbitrary"), args=(proj, proj, proj, bias, dao), exchange=exchange,
        vmem=VMEM_LIMIT_ATT_BWD)


def _rms_in_bwd(x2, dxn, dh1, g1):
    t = x2.shape[0]
    tm = min(1024, t)

    def body(x_ref, d_ref, h_ref, g_ref, dx_ref, dg_ref):
        @pl.when(pl.program_id(0) == 0)
        def _():
            dg_ref[...] = jnp.zeros_like(dg_ref)

        dx, dg_rows = _rms_bwd(x_ref[...], g_ref[...], d_ref[...])
        dx_ref[...] = h_ref[...] + dx
        dg_ref[...] += jnp.sum(dg_rows, axis=0, keepdims=True)

    row = pl.BlockSpec((tm, D_MODEL), lambda i: (i, 0))
    vec = pl.BlockSpec((1, D_MODEL), lambda i: (0, 0))
    return pl.pallas_call(
        body, name="rms_in_bwd", grid=(t // tm,),
        in_specs=[row, row, row, vec], out_specs=[row, vec],
        out_shape=[jax.ShapeDtypeStruct((t, D_MODEL), F32), jax.ShapeDtypeStruct((1, D_MODEL), F32)],
        compiler_params=_params("arbitrary"),
    )(x2, dxn, dh1, g1)


def _pack_small(dg1, dbr, dba, dg2, dg3, dw, loss):
    def body(a_ref, b_ref, c_ref, d_ref, e_ref, w_ref, l_ref, o_ref):
        o_ref[...] = jnp.zeros_like(o_ref)
        for r, ref in enumerate((a_ref, b_ref, c_ref, d_ref, e_ref)):
            o_ref[r:r + 1, :] = ref[...]
        o_ref[5:6, 0:LANES] = l_ref[...]
        for hp in range(ATT_HEADS // 2):
            o_ref[8 + 2 * hp:10 + 2 * hp, :] = w_ref[hp]

    return pl.pallas_call(body, name="pack_small",
                          out_shape=jax.ShapeDtypeStruct((16, D_MODEL), F32))(dg1, dbr, dba, dg2, dg3, dw, loss)


def _rotary_tables(seq):
    freqs = ROPE_BASE ** (-jnp.arange(0, RET_KEY_DIM, 2, dtype=F32) / RET_KEY_DIM)
    ang = jnp.arange(seq, dtype=F32)[:, None] * freqs[None, :]
    cos, sin = jnp.cos(ang), jnp.sin(ang)
    return jnp.concatenate([cos, cos], axis=1), jnp.concatenate([-sin, sin], axis=1)


def _bias_rows(rel_bias):
    n_far = BIAS_LEN - ATT_Q - MAX_REL + 1
    n_near = BIAS_LEN - n_far - (N_REL - 2)
    w = jnp.concatenate([jnp.broadcast_to(rel_bias[:, N_REL - 1:], (ATT_HEADS, n_far)),
                         rel_bias[:, 1:N_REL - 1][:, ::-1],
                         jnp.broadcast_to(rel_bias[:, :1], (ATT_HEADS, n_near))], axis=1)
    return w.reshape(ATT_HEADS // 2, 2, BIAS_LEN)


def _bias_rows_bwd(dw):
    n_far = BIAS_LEN - ATT_Q - MAX_REL + 1
    mid = dw[:, n_far:n_far + N_REL - 2][:, ::-1]
    return jnp.concatenate([jnp.sum(dw[:, n_far + N_REL - 2:], axis=1, keepdims=True), mid,
                            jnp.sum(dw[:, :n_far], axis=1, keepdims=True)], axis=1)


def _step(x, tgt, norm_mix, b_gate, norm_ffn, norm_final, rel_bias_shard, shard):
    batch, seq, _ = x.shape
    t = batch * seq
    n_rb = rel_bias_shard.shape[-1]
    x2, tgt2 = x.reshape(t, D_MODEL), tgt.reshape(t, D_MODEL)
    g3 = norm_final.reshape(1, D_MODEL)
    cs, sn = _rotary_tables(seq)
    lg = np.log(1.0 - 2.0 ** (-5.0 - np.arange(RET_HEADS, dtype=np.float32))).astype(np.float32)
    lg_arr = jnp.asarray(np.broadcast_to(lg[:, None, None], (RET_HEADS, 1, LANES)))

    def gather(*names):
        return _ChipGather([shard[nm] for nm in names])

    def scatter(*grads):
        return _Scatter(grads)

    rb_pad = jnp.pad(rel_bias_shard, ((0, 0), (0, LANES - n_rb)))
    (xn,), (w_in_half, rb_full) = _rms_fwd(x2, norm_mix,
                                           _ChipGather([shard["w_in_t"], rb_pad], parts=[(0, 2), (0, 1)]))
    rb_full = rb_full.reshape(N_DEV, ATT_HEADS, LANES)[:, :, :n_rb]
    bias, (w_in_t,) = _att_bias_tiles(
        _bias_rows(jnp.transpose(rb_full, (1, 0, 2)).reshape(ATT_HEADS, N_DEV * n_rb)),
        _ChipGather([shard["w_in_t"]], parts=[(1, 2)], into=[w_in_half]))
    proj, (w_ret, w_att_t, w_out, w_gate_t) = _mm(
        xn, w_in_t, tb=True, out_dtype=BF16, tm=1024, tn=1664, tk=1024, name="proj",
        exchange=gather("w_ret", "w_att_t", "w_out", "w_gate_t"))
    (gro, o_ret, qr, kr), _ = _ret_fwd(proj, cs, sn, lg_arr, batch, seq, None)
    (ao,), (w_up_t, w_down) = _att_fwd(proj, bias, batch, seq, gather("w_up_t", "w_down"))
    z, y_ret, y_att, h1, hn = _mix_out_fwd(gro, ao, proj, b_gate, w_ret, w_att_t, x2, w_out, norm_ffn)
    g_act, u_act, a_act = _ffn_up(hn, w_gate_t, w_up_t)
    dh2, dh2b, loss, dg3 = _ffn_down_loss(a_act, h1, tgt2, w_down, g3)

    wg = dict(out_dtype=BF16, tn=1024, ta=True)
    slots = {}
    dw_down = _mm(a_act, dh2b, tm=1408, tk=2048, name="dw_down", **wg)
    (d_gact, d_uact), _ = _ffn_bwd_act(dh2b, w_down, g_act, u_act, None)
    dw_gate = _mm(d_gact, hn, tm=1408, tk=2048, name="dw_gate", **wg)
    dw_up = _mm(d_uact, hn, tm=1408, tk=2048, name="dw_up", **wg)
    (dh1, dh1b, dg2), (slots["w_down"],) = _ffn_bwd_in(d_gact, d_uact, w_gate_t, w_up_t, h1, dh2, norm_ffn,
                                                     scatter(dw_down))
    dw_out = _mm(z, dh1b, tm=1024, tk=2048, name="dw_out", **wg)
    dyr, dya, dglr, dgla, dgro, dao, db = _mix_bwd(dh1b, w_out, proj, b_gate, y_ret, y_att, w_ret, w_att_t)
    dw_ret = _mm(gro, dyr, tm=1024, tk=2048, name="dw_ret", **wg)
    dw_att = _mm(dya, ao, tm=1024, tk=2048, name="dw_att", **wg)
    (drq, drk, drv, drg), _ = _ret_bwd(dgro, proj, o_ret, qr, kr, cs, sn, lg_arr, batch, seq, None)
    (daq, dak, dav, dw), (slots["w_gate_t"], slots["w_out"], slots["w_ret"], slots["w_att_t"]) = _att_bwd(
        proj, bias, dao, batch, seq, scatter(dw_gate, dw_out, dw_ret, dw_att))
    dproj = [drq, drk, drv, drg, daq, dak, dav, dglr, dgla]
    dw_in, (slots["w_up_t"],) = _mm_pieces(dproj, xn, ta=True, out_dtype=BF16, tm=512, tn=1024, tk=1024,
                                           name="dw_in", exchange=scatter(dw_up))
    (dw_in_sibling,) = _alone(_PairSwap([dw_in]), "swap_w_in")
    dw_in_pairs = _pair_add(dw_in, dw_in_sibling, "pair_w_in")
    dxn, (slots["w_in_t"],) = _mm_pieces(dproj, w_in_t, ta=False, out_dtype=F32, tm=1024, tn=1024, tk=512, name="dxn",
                                  exchange=_ChipScatter([dw_in_pairs]))
    dx, dg1 = _rms_in_bwd(x2, dxn, dh1, norm_mix)
    small = _pack_small(dg1, db[:, :D_MODEL], db[:, D_MODEL:], dg2, dg3, dw, loss)
    (small_slots,) = _alone(_DirectGather([small]), "gather_small")
    return dx.reshape(batch, seq, D_MODEL), slots, small_slots.reshape(N_DEV, 16, D_MODEL)


def _row_tile(r, c):
    return max(d for d in range(16, r + 1, 16) if r % d == 0 and (d * c <= 256 * 1024 or d == 16))


def _pair_add(grad, got, name):
    _, r, c = got.shape
    tr = r
    core = lax.axis_index("c").astype(jnp.int32).reshape(1)

    def body(core_ref, g_ref, a_ref, o_ref):
        o_ref[...] = (g_ref[...].astype(F32) + a_ref[...].astype(F32)).astype(o_ref.dtype)

    blk = pl.BlockSpec((None, tr, c), lambda q, i, core_ref: (q, i, 0))
    return pl.pallas_call(
        body, name=name,
        grid_spec=pltpu.PrefetchScalarGridSpec(
            num_scalar_prefetch=1, grid=(4, r // tr),
            in_specs=[pl.BlockSpec((None, None, tr, c), lambda q, i, core_ref: (q, core_ref[0], i, 0)), blk],
            out_specs=blk),
        out_shape=jax.ShapeDtypeStruct(got.shape, got.dtype),
        compiler_params=_params("parallel", "parallel"),
    )(core, grad.reshape(4, 2, r, c), got)


def _sum_slots(slots, name):
    n, r, c = slots.shape
    tr = _row_tile(r, c)

    def body(s_ref, o_ref):
        acc = s_ref[0].astype(F32)
        for s in range(1, n):
            acc = acc + s_ref[s].astype(F32)
        o_ref[...] = acc

    return pl.pallas_call(
        body, name=name, grid=(r // tr,),
        in_specs=[pl.BlockSpec((n, tr, c), lambda i: (0, i, 0))],
        out_specs=pl.BlockSpec((tr, c), lambda i: (i, 0)),
        out_shape=jax.ShapeDtypeStruct((r, c), F32),
        compiler_params=_params("parallel"),
    )(slots)


def _adamw_math(w, g, m, v):
    m = ADAM_B1 * m + (1.0 - ADAM_B1) * g
    v = ADAM_B2 * v + (1.0 - ADAM_B2) * (g * g)
    m_hat = m / (1.0 - ADAM_B1 ** ADAM_STEP)
    v_hat = v / (1.0 - ADAM_B2 ** ADAM_STEP)
    return -ADAM_LR * (m_hat / (jnp.sqrt(v_hat) + ADAM_EPS) + ADAM_WD * w), m, v


def _adamw(w, slots, m, v, name):
    n, r, c = slots.shape
    tr = _row_tile(r, c)

    def body(w_ref, s_ref, m_ref, v_ref, g_ref, d_ref, nm_ref, nv_ref):
        g = s_ref[0].astype(F32)
        for s in range(1, n):
            g = g + s_ref[s].astype(F32)
        g_ref[...] = g
        d_ref[...], nm_ref[...], nv_ref[...] = _adamw_math(w_ref[...], g, m_ref[...], v_ref[...])

    blk = pl.BlockSpec((tr, c), lambda i: (i, 0))
    return pl.pallas_call(
        body, name=name, grid=(r // tr,),
        in_specs=[blk, pl.BlockSpec((n, tr, c), lambda i: (0, i, 0)), blk, blk], out_specs=[blk] * 4,
        out_shape=[jax.ShapeDtypeStruct((r, c), F32)] * 4,
        compiler_params=_params("parallel"),
    )(w, slots, m, v)


def _adamw_small(ws, gs, ms, vs):
    n = len(ws)

    def body(*refs):
        for i in range(n):
            w_ref, g_ref, m_ref, v_ref = (refs[j * n + i] for j in range(4))
            d_ref, nm_ref, nv_ref = (refs[(4 + j) * n + i] for j in range(3))
            d_ref[...], nm_ref[...], nv_ref[...] = _adamw_math(w_ref[...], g_ref[...], m_ref[...], v_ref[...])

    shapes = [jax.ShapeDtypeStruct(w.shape, F32) for w in ws]
    outs = pl.pallas_call(body, name="adamw_small", out_shape=shapes * 3)(*ws, *gs, *ms, *vs)
    return outs[:n], outs[n:2 * n], outs[2 * n:]


def kernel(x, norm_mix, w_in, b_gate, rel_bias, w_ret_out, w_att_out, w_out, norm_ffn, w_ffn_gate, w_ffn_up, w_ffn_down, norm_final, loss_target, m_norm_mix, m_w_in, m_b_gate, m_rel_bias, m_w_ret_out, m_w_att_out, m_w_out, m_norm_ffn, m_w_ffn_gate, m_w_ffn_up, m_w_ffn_down, m_norm_final, v_norm_mix, v_w_in, v_b_gate, v_rel_bias, v_w_ret_out, v_w_att_out, v_w_out, v_norm_ffn, v_w_ffn_gate, v_w_ffn_up, v_w_ffn_down, v_norm_final):
    me = _index(_place())
    n_rb = rel_bias.shape[-1]

    shard = dict(w_in_t=w_in[0].T, w_gate_t=w_ffn_gate[0].T, w_up_t=w_ffn_up[0].T, w_down=w_ffn_down[0],
                 w_ret=w_ret_out[0], w_out=w_out[0], w_att_t=w_att_out[0].T)
    shard = {nm: s.astype(BF16) for nm, s in shard.items()}
    dx, slots, small_slots = _step(x, loss_target, norm_mix, b_gate, norm_ffn, norm_final, rel_bias[0], shard)
    small_sum = _sum_slots(small_slots, "sum_small")
    loss = small_sum[5, 0]

    transposed = dict(w_in="w_in_t", w_ffn_gate="w_gate_t", w_ffn_up="w_up_t", w_att_out="w_att_t")
    plain = dict(w_ffn_down="w_down", w_ret_out="w_ret", w_out="w_out")
    g = dict(
        norm_mix=small_sum[0:1], b_gate=jnp.concatenate([small_sum[1:2], small_sum[2:3]], axis=1),
        norm_ffn=small_sum[3:4], norm_final=small_sum[4:5],
        rel_bias=lax.dynamic_slice_in_dim(_bias_rows_bwd(small_sum[8:16]), me * n_rb, n_rb, axis=1),
    )
    w = dict(norm_mix=norm_mix, w_in=w_in, b_gate=b_gate, rel_bias=rel_bias, w_ret_out=w_ret_out, w_att_out=w_att_out,
             w_out=w_out, norm_ffn=norm_ffn, w_ffn_gate=w_ffn_gate, w_ffn_up=w_ffn_up, w_ffn_down=w_ffn_down,
             norm_final=norm_final)
    m = dict(norm_mix=m_norm_mix, w_in=m_w_in, b_gate=m_b_gate, rel_bias=m_rel_bias, w_ret_out=m_w_ret_out,
             w_att_out=m_w_att_out, w_out=m_w_out, norm_ffn=m_norm_ffn, w_ffn_gate=m_w_ffn_gate, w_ffn_up=m_w_ffn_up,
             w_ffn_down=m_w_ffn_down, norm_final=m_norm_final)
    v = dict(norm_mix=v_norm_mix, w_in=v_w_in, b_gate=v_b_gate, rel_bias=v_rel_bias, w_ret_out=v_w_ret_out,
             w_att_out=v_w_att_out, w_out=v_w_out, norm_ffn=v_norm_ffn, w_ffn_gate=v_w_ffn_gate, w_ffn_up=v_w_ffn_up,
             w_ffn_down=v_w_ffn_down, norm_final=v_norm_final)
    order = ("norm_mix", "w_in", "b_gate", "rel_bias", "w_ret_out", "w_att_out", "w_out", "norm_ffn",
             "w_ffn_gate", "w_ffn_up", "w_ffn_down", "norm_final")
    small_names = ("norm_mix", "b_gate", "rel_bias", "norm_ffn", "norm_final")

    def flat(a):
        return a[0] if a.ndim == 3 else a.reshape(-1, a.shape[-1])

    grad, delta, new_m, new_v = {}, {}, {}, {}
    for nm in order:
        if nm in transposed:
            res = _adamw(w[nm][0].T, slots[transposed[nm]], m[nm][0].T, v[nm][0].T, "adamw_" + nm)
            grad[nm], delta[nm], new_m[nm], new_v[nm] = (a.T[None] for a in res)
        elif nm in plain:
            res = _adamw(flat(w[nm]), slots[plain[nm]], flat(m[nm]), flat(v[nm]), "adamw_" + nm)
            grad[nm], delta[nm], new_m[nm], new_v[nm] = (a.reshape(w[nm].shape) for a in res)
    ds, nms, nvs = _adamw_small([flat(w[nm]) for nm in small_names], [g[nm] for nm in small_names],
                                [flat(m[nm]) for nm in small_names], [flat(v[nm]) for nm in small_names])
    for i, nm in enumerate(small_names):
        grad[nm], delta[nm], new_m[nm], new_v[nm] = (a.reshape(w[nm].shape) for a in (g[nm], ds[i], nms[i], nvs[i]))

    return (loss, dx, *[grad[nm] for nm in order], *[delta[nm] for nm in order],
            *[new_m[nm] for nm in order], *[new_v[nm] for nm in order])
```

```python
import numpy as np
import jax
import jax.numpy as jnp
from jax import lax
from jax.experimental import pallas as pl
from jax.experimental.pallas import tpu as pltpu

F32 = jnp.float32
BF16 = jnp.bfloat16
MESH = pl.DeviceIdType.MESH

D_MODEL = 1024
CHUNK = 64
RET_HEADS = 4
RET_KEY_DIM = 128
RET_VAL_DIM = 256
ATT_HEADS = 8
BAND_CHUNKS = 8
MAX_REL = 256
N_REL = CHUNK + MAX_REL
D_FF = 2816
N_IN = 6656
ROPE_BASE = 10000.0
EPS = 1e-6
NEG_INF = -1e30
C_RQ, C_RK, C_RV, C_RG, C_AQ, C_AK, C_AV, C_GL = 0, 512, 1024, 2048, 3072, 3584, 4096, 4608

ADAM_LR = 0.001
ADAM_B1 = 0.9
ADAM_B2 = 0.999
ADAM_EPS = 1e-08
ADAM_WD = 0.01
ADAM_STEP = 10

N_DEV = 8
LANES = 128
RET_TILE = 256
ATT_Q = 256
ATT_PAD = BAND_CHUNKS * CHUNK
ATT_WIN = ATT_PAD + ATT_Q
ATT_STARTS = ATT_PAD // ATT_Q
ATT_ROWS = 32
BIAS_LEN = 1024
VMEM_LIMIT = 48 * 1024 * 1024
VMEM_LIMIT_ATT_BWD = 56 * 1024 * 1024


def _params(*sem, vmem=VMEM_LIMIT):
    return pltpu.CompilerParams(dimension_semantics=sem, vmem_limit_bytes=vmem)


def _dot(a, b):
    return lax.dot_general(a, b, (((1,), (0,)), ((), ())), preferred_element_type=F32)


def _dot_nt(a, b):
    return lax.dot_general(a, b, (((1,), (1,)), ((), ())), preferred_element_type=F32)


def _dot_tn(a, b):
    return lax.dot_general(a, b, (((0,), (0,)), ((), ())), preferred_element_type=F32)


def _sigmoid(x):
    return 1.0 / (1.0 + jnp.exp(-x))


def _rms_bwd(x, g, dy):
    r = lax.rsqrt(jnp.mean(x * x, axis=-1, keepdims=True) + EPS)
    u = dy * g
    dx = r * u - x * (r * r * r) * jnp.mean(u * x, axis=-1, keepdims=True)
    return dx, dy * x * r


def _place():
    return lax.axis_index("x"), lax.axis_index("y"), lax.axis_index("c")


def _peer(k):
    x, y, c = _place()
    return ((1 - x) if k & 4 else x, (1 - y) if k & 2 else y, (1 - c) if k & 1 else c)


def _index(place):
    return 4 * place[0] + 2 * place[1] + place[2]


def _rows(ref, block, nrows):
    align = 16 if ref.dtype == BF16 else 8
    return ref.at[pl.ds(pl.multiple_of(block * nrows, align), nrows)]


class _Scatter:
    def __init__(self, arrays):
        self.arrays, self.n = list(arrays), len(arrays)

    def out_shape(self):
        return [jax.ShapeDtypeStruct((N_DEV, a.shape[0] // N_DEV) + a.shape[1:], a.dtype) for a in self.arrays]

    def scratch(self):
        return [pltpu.SemaphoreType.DMA((self.n, N_DEV - 1)), pltpu.SemaphoreType.DMA((self.n, N_DEV - 1)),
                pltpu.SemaphoreType.DMA((self.n,))]

    def _copies(self, ins, outs, sems):
        send_sems, recv_sems, local_sems = sems
        me = _index(_place())

        def src(w, to):
            return _rows(ins[w], to, ins[w].shape[0] // N_DEV)

        def dst(w, origin):
            return outs[w].at[origin]

        def remote(w, k, to, origin):
            return pltpu.make_async_remote_copy(src_ref=src(w, to), dst_ref=dst(w, origin),
                                                send_sem=send_sems.at[w, k - 1], recv_sem=recv_sems.at[w, k - 1],
                                                device_id=_peer(k), device_id_type=MESH)

        pairs = [(w, k) for w in range(self.n) for k in range(1, N_DEV)]
        own = lambda: [pltpu.make_async_copy(src(w, me), dst(w, me), local_sems.at[w]) for w in range(self.n)]
        sent = lambda: [remote(w, k, _index(_peer(k)), me) for w, k in pairs]
        arriving = lambda: [remote(w, k, me, _index(_peer(k))) for w, k in pairs]
        return own, sent, arriving

    def start(self, ins, outs, sems):
        own, sent, _ = self._copies(ins, outs, sems)
        for cp in own() + sent():
            cp.start()

    def wait(self, ins, outs, sems):
        own, sent, arriving = self._copies(ins, outs, sems)
        for cp in arriving():
            cp.wait_recv()
        for cp in sent():
            cp.wait_send()
        for cp in own():
            cp.wait()


class _PairSwap:
    def __init__(self, arrays):
        self.arrays, self.n = list(arrays), len(arrays)

    def out_shape(self):
        return [jax.ShapeDtypeStruct((4, a.shape[0] // N_DEV) + a.shape[1:], a.dtype) for a in self.arrays]

    def scratch(self):
        return [pltpu.SemaphoreType.DMA((self.n, 4)), pltpu.SemaphoreType.DMA((self.n, 4))]

    def _copies(self, ins, outs, sems):
        send_sems, recv_sems = sems
        x, y, c = _place()
        return [pltpu.make_async_remote_copy(
            src_ref=_rows(ins[w], 2 * q + 1 - c, ins[w].shape[0] // N_DEV), dst_ref=outs[w].at[q],
            send_sem=send_sems.at[w, q], recv_sem=recv_sems.at[w, q],
            device_id=(x, y, 1 - c), device_id_type=MESH) for w in range(self.n) for q in range(4)]

    def start(self, ins, outs, sems):
        for cp in self._copies(ins, outs, sems):
            cp.start()

    def wait(self, ins, outs, sems):
        for cp in self._copies(ins, outs, sems):
            cp.wait()


class _ChipScatter:
    def __init__(self, arrays):
        self.arrays, self.n = list(arrays), len(arrays)

    def out_shape(self):
        return [jax.ShapeDtypeStruct(a.shape, a.dtype) for a in self.arrays]

    def scratch(self):
        return [pltpu.SemaphoreType.DMA((self.n, 3)), pltpu.SemaphoreType.DMA((self.n, 3)),
                pltpu.SemaphoreType.DMA((self.n,))]

    def _copies(self, ins, outs, sems):
        send_sems, recv_sems, local_sems = sems
        x, y, c = _place()
        mine = 2 * x + y
        sent, arriving = [], []
        for w in range(self.n):
            for k in range(1, 4):
                tx, ty = (1 - x) if k & 2 else x, (1 - y) if k & 1 else y
                other = 2 * tx + ty
                sent.append(lambda w=w, k=k, tx=tx, ty=ty, other=other: pltpu.make_async_remote_copy(
                    src_ref=ins[w].at[other], dst_ref=outs[w].at[mine], send_sem=send_sems.at[w, k - 1],
                    recv_sem=recv_sems.at[w, k - 1], device_id=(tx, ty, c), device_id_type=MESH))
                arriving.append(lambda w=w, k=k, tx=tx, ty=ty, other=other: pltpu.make_async_remote_copy(
                    src_ref=ins[w].at[mine], dst_ref=outs[w].at[other], send_sem=send_sems.at[w, k - 1],
                    recv_sem=recv_sems.at[w, k - 1], device_id=(tx, ty, c), device_id_type=MESH))
        own = [lambda w=w: pltpu.make_async_copy(ins[w].at[mine], outs[w].at[mine], local_sems.at[w])
               for w in range(self.n)]
        return own, sent, arriving

    def start(self, ins, outs, sems):
        own, sent, _ = self._copies(ins, outs, sems)
        for cp in own + sent:
            cp().start()

    def wait(self, ins, outs, sems):
        own, sent, arriving = self._copies(ins, outs, sems)
        for cp in arriving:
            cp().wait_recv()
        for cp in sent:
            cp().wait_send()
        for cp in own:
            cp().wait()


class _ChipGather:
    def __init__(self, arrays, parts=None, into=None):
        self.arrays, self.n, self.into = list(arrays), len(arrays), into
        self.parts = parts or [(0, 1)] * self.n

    def out_shape(self):
        return [jax.ShapeDtypeStruct((N_DEV * a.shape[0],) + a.shape[1:], a.dtype) for a in self.arrays]

    def scratch(self):
        return [pltpu.SemaphoreType.DMA((self.n, N_DEV - 1)), pltpu.SemaphoreType.DMA((self.n, N_DEV - 1)),
                pltpu.SemaphoreType.DMA((self.n,))]

    def _parts(self, ins, outs, sems):
        send_sems, recv_sems, local_sems = sems
        x, y, c = _place()
        me, sibling = (x, y, c), (x, y, 1 - c)
        chips = [(1 - x, y), (x, 1 - y), (1 - x, 1 - y)]

        def rows(w, place, whole):
            (index, count), r = self.parts[w], ins[w].shape[0]
            lo, size = (0, r) if whole else (index * (r // count), r // count)
            align = 16 if ins[w].dtype == BF16 else 8
            return outs[w].at[pl.ds(pl.multiple_of(_index(place) * r + lo, align), size)]

        def mine(w, whole):
            (index, count), r = self.parts[w], ins[w].shape[0]
            return ins[w] if whole or count == 1 else ins[w].at[pl.ds(index * (r // count), r // count)]

        def copy(w, k, block, to, own=False):
            whole = k == 0
            return pltpu.make_async_remote_copy(src_ref=mine(w, whole) if own else rows(w, block, whole),
                                                dst_ref=rows(w, block, whole),
                                                send_sem=send_sems.at[w, k], recv_sem=recv_sems.at[w, k],
                                                device_id=to, device_id_type=MESH)

        def local(w):
            return pltpu.make_async_copy(ins[w], rows(w, me, True), local_sems.at[w])

        return me, sibling, chips, c, copy, local, [index == 0 for index, _ in self.parts]

    def start(self, ins, outs, sems):
        me, sibling, chips, c, copy, local, places_own = self._parts(ins, outs, sems)
        for w in range(self.n):
            if places_own[w]:
                local(w).start()
                copy(w, 0, me, sibling, own=True).start()
            for j, chip in enumerate(chips):
                copy(w, 1 + j, me, (*chip, c), own=True).start()

    def wait(self, ins, outs, sems):
        me, sibling, chips, c, copy, local, places_own = self._parts(ins, outs, sems)
        for w in range(self.n):
            for j, chip in enumerate(chips):
                copy(w, 1 + j, (*chip, c), me).wait_recv()
                copy(w, 4 + j, (*chip, c), sibling).start()
        for w in range(self.n):
            if places_own[w]:
                copy(w, 0, sibling, me).wait_recv()
                copy(w, 0, me, sibling, own=True).wait_send()
                local(w).wait()
            for j, chip in enumerate(chips):
                copy(w, 4 + j, (*chip, 1 - c), me).wait_recv()
                copy(w, 1 + j, me, (*chip, c), own=True).wait_send()
                copy(w, 4 + j, (*chip, c), sibling).wait_send()


def _call(body, *, name, grid, in_specs, out_specs, out_shape, scratch=(), semantics, args, exchange=None,
          vmem=VMEM_LIMIT):
    if exchange is None:
        return pl.pallas_call(body, name=name, grid=grid, in_specs=in_specs, out_specs=out_specs, out_shape=out_shape,
                              scratch_shapes=list(scratch),
                              compiler_params=_params(*semantics, vmem=vmem))(*args), None
    n_in, n_out, n_scr, nx = len(in_specs), len(out_specs), len(scratch), exchange.n
    into = list(getattr(exchange, "into", None) or [])

    def full_body(*refs):
        ins, refs = refs[:n_in], refs[n_in:]
        x_in, refs = refs[:nx], refs[nx + len(into):]
        outs, refs = refs[:n_out], refs[n_out:]
        x_out, refs = refs[:nx], refs[nx:]
        scr, sems = refs[:n_scr], refs[n_scr:]
        first, last = True, True
        for axis, size in enumerate(grid):
            first = jnp.logical_and(first, pl.program_id(axis) == 0)
            last = jnp.logical_and(last, pl.program_id(axis) == size - 1)
        if grid:
            pl.when(first)(lambda: exchange.start(x_in, x_out, sems))
        else:
            exchange.start(x_in, x_out, sems)
        body(*ins, *outs, *scr)
        if grid:
            pl.when(last)(lambda: exchange.wait(x_in, x_out, sems))
        else:
            exchange.wait(x_in, x_out, sems)

    hbm = pl.BlockSpec(memory_space=pltpu.HBM)
    res = pl.pallas_call(
        full_body, name=name, grid=grid,
        in_specs=list(in_specs) + [hbm] * (nx + len(into)), out_specs=list(out_specs) + [hbm] * nx,
        out_shape=list(out_shape) + exchange.out_shape(),
        scratch_shapes=list(scratch) + exchange.scratch(),
        input_output_aliases={n_in + nx + w: n_out + w for w in range(len(into))},
        compiler_params=_params(*(["arbitrary"] * len(grid)), vmem=vmem),
    )(*args, *exchange.arrays, *into)
    return res[:n_out], res[n_out:]


def _gather_sum(block, name):
    exchange = _ChipGather([block])
    r, c = block.shape

    def body(x_ref, sum_ref, slots_ref, buf, copy_sem, *sems):
        exchange.start([x_ref], [slots_ref], sems)
        exchange.wait([x_ref], [slots_ref], sems)
        cp = pltpu.make_async_copy(slots_ref, buf, copy_sem)
        cp.start()
        cp.wait()
        acc = buf[pl.ds(0, r)]
        for s in range(1, N_DEV):
            acc = acc + buf[pl.ds(s * r, r)]
        sum_ref[...] = acc

    hbm = pl.BlockSpec(memory_space=pltpu.HBM)
    return pl.pallas_call(
        body, name=name, in_specs=[hbm], out_specs=[pl.BlockSpec(memory_space=pltpu.VMEM), hbm],
        out_shape=[jax.ShapeDtypeStruct((r, c), F32)] + exchange.out_shape(),
        scratch_shapes=[pltpu.VMEM((N_DEV * r, c), F32), pltpu.SemaphoreType.DMA(())] + exchange.scratch(),
        compiler_params=_params(),
    )(block)[0]


def _alone(exchange, name):
    return _call(lambda: None, name=name, grid=(), in_specs=[], out_specs=[], out_shape=[], semantics=(),
                 args=(), exchange=exchange)[1]


def _mm(a, b, *, ta=False, tb=False, out_dtype, tm, tn, tk, name, exchange=None):
    m, k = (a.shape[1], a.shape[0]) if ta else a.shape
    n = b.shape[0] if tb else b.shape[1]
    assert k == (b.shape[1] if tb else b.shape[0])
    tm, tn, tk = min(tm, m), min(tn, n), min(tk, k)
    assert m % tm == 0 and n % tn == 0 and k % tk == 0, (name, m, n, k)
    nk = k // tk
    dims = (((0 if ta else 1,), (1 if tb else 0,)), ((), ()))

    def body(a_ref, b_ref, o_ref, *acc):
        prod = lax.dot_general(a_ref[...].astype(BF16), b_ref[...].astype(BF16), dims, preferred_element_type=F32)
        if nk == 1:
            o_ref[...] = prod.astype(o_ref.dtype)
            return
        acc_ref, kk = acc[0], pl.program_id(2)

        @pl.when(kk == 0)
        def _():
            acc_ref[...] = prod

        @pl.when((kk > 0) & (kk < nk - 1))
        def _():
            acc_ref[...] += prod

        @pl.when(kk == nk - 1)
        def _():
            o_ref[...] = (acc_ref[...] + prod).astype(o_ref.dtype)

    a_spec = (pl.BlockSpec((tk, tm), lambda i, j, kk: (kk, i)) if ta
              else pl.BlockSpec((tm, tk), lambda i, j, kk: (i, kk)))
    b_spec = (pl.BlockSpec((tn, tk), lambda i, j, kk: (j, kk)) if tb
              else pl.BlockSpec((tk, tn), lambda i, j, kk: (kk, j)))
    (out,), moved = _call(
        body, name=name, grid=(m // tm, n // tn, nk),
        in_specs=[a_spec, b_spec],
        out_specs=[pl.BlockSpec((tm, tn), lambda i, j, kk: (i, j))],
        out_shape=[pltpu.HBM((m, n), out_dtype)],
        scratch=[pltpu.VMEM((tm, tn), F32)] if nk > 1 else [],
        semantics=("parallel", "parallel", "arbitrary"), args=(a, b), exchange=exchange)
    return out if exchange is None else (out, moved)


def _mm_pieces(pieces, b, *, ta, out_dtype, tm, tn, tk, name, exchange=None):
    rows, n = pieces[0].shape[0], b.shape[1]
    step = tm if ta else tk
    assert all(p.shape[0] == rows and p.shape[1] % step == 0 for p in pieces), name
    edges = [int(e) for e in np.cumsum([0] + [p.shape[1] // step for p in pieces])]
    total = edges[-1] * step
    m, k = (total, rows) if ta else (rows, total)
    assert b.shape[0] == k and m % tm == 0 and n % tn == 0 and k % tk == 0, name
    nk, npieces = k // tk, len(pieces)
    dims = (((0 if ta else 1,), (0,)), ((), ()))
    b_resident = ta and n == tn

    def body(*refs):
        a_refs, (b_ref, o_ref, acc_ref) = refs[:npieces], refs[npieces:]
        kk = pl.program_id(2)
        pos = pl.program_id(0) if ta else kk

        @pl.when(kk == 0)
        def _():
            acc_ref[...] = jnp.zeros_like(acc_ref)

        def b_tile():
            return b_ref[pl.ds(pl.multiple_of(kk * tk, tk), tk), :] if b_resident else b_ref[...]

        for p, a_ref in enumerate(a_refs):
            @pl.when((pos >= edges[p]) & (pos < edges[p + 1]))
            def _(a_ref=a_ref):
                acc_ref[...] += lax.dot_general(a_ref[...], b_tile(), dims, preferred_element_type=F32)

        @pl.when(kk == nk - 1)
        def _():
            o_ref[...] = acc_ref[...].astype(o_ref.dtype)

    def a_spec(p):
        lo, last = edges[p], edges[p + 1] - edges[p] - 1
        if ta:
            def index(i, j, kk):
                inside = (i >= lo) & (i <= lo + last)
                return jnp.where(inside, kk, 0), jnp.clip(i - lo, 0, last)
            return pl.BlockSpec((tk, tm), index)
        return pl.BlockSpec((tm, tk), lambda i, j, kk: (i, jnp.clip(kk - lo, 0, last)))

    (out,), moved = _call(
        body, name=name, grid=(m // tm, n // tn, nk),
        in_specs=[a_spec(p) for p in range(npieces)]
                 + [pl.BlockSpec(b.shape, lambda i, j, kk: (0, 0)) if b_resident
                    else pl.BlockSpec((tk, tn), lambda i, j, kk: (kk, j))],
        out_specs=[pl.BlockSpec((tm, tn), lambda i, j, kk: (i, j))],
        out_shape=[pltpu.HBM((m, n), out_dtype)],
        scratch=[pltpu.VMEM((tm, tn), F32)],
        semantics=("parallel", "parallel", "arbitrary"), args=(*pieces, b), exchange=exchange)
    return out if exchange is None else (out, moved)


def _rms_fwd(x2, g, exchange):
    t = x2.shape[0]
    tm = min(512, t)

    def body(x_ref, g_ref, o_ref):
        x = x_ref[...]
        r = lax.rsqrt(jnp.mean(x * x, axis=-1, keepdims=True) + EPS)
        o_ref[...] = (x * r * g_ref[...]).astype(o_ref.dtype)

    return _call(
        body, name="rms_in_fwd", grid=(t // tm,),
        in_specs=[pl.BlockSpec((tm, D_MODEL), lambda i: (i, 0)), pl.BlockSpec((1, D_MODEL), lambda i: (0, 0))],
        out_specs=[pl.BlockSpec((tm, D_MODEL), lambda i: (i, 0))],
        out_shape=[jax.ShapeDtypeStruct((t, D_MODEL), BF16)],
        semantics=("parallel",), args=(x2, g), exchange=exchange)


def _decay(lg):
    row = lax.broadcasted_iota(jnp.int32, (RET_TILE, RET_TILE), 0)
    col = lax.broadcasted_iota(jnp.int32, (RET_TILE, RET_TILE), 1)
    within = jnp.exp(lg * jnp.abs(row - col).astype(F32))
    inside = jnp.where((col >> 6) <= (row >> 6), within, 0.0)
    pos = lax.broadcasted_iota(jnp.int32, (RET_TILE, 1), 0).astype(F32)
    q_dec = jnp.exp(lg * (pos + 1.0))
    k_dec = jnp.exp(lg * (RET_TILE - 1.0 - pos))
    tile_dec = jnp.exp(lg * float(RET_TILE))
    return inside, q_dec, k_dec, tile_dec


def _scaled(a_bf16, dec):
    return (a_bf16.astype(F32) * dec).astype(BF16)


def _ret_specs(seq):
    key = lambda base: pl.BlockSpec((seq, RET_KEY_DIM), lambda b, h: (b, base // RET_KEY_DIM + h))
    val = lambda base: pl.BlockSpec((seq, RET_VAL_DIM), lambda b, h: (b, base // RET_VAL_DIM + h))
    tab = pl.BlockSpec((seq, RET_KEY_DIM), lambda b, h: (0, 0))
    lgs = pl.BlockSpec((None, 1, LANES), lambda b, h: (h, 0, 0))
    return key, val, tab, lgs


def _ret_fwd(proj, cs, sn, lg_arr, batch, seq, exchange):
    t = batch * seq
    nt = seq // RET_TILE

    def body(q_ref, k_ref, v_ref, rg_ref, cs_ref, sn_ref, lg_ref, gro_ref, o_ref, qr_ref, kr_ref):
        lg = lg_ref[:, 0:1]
        cs_t, sn_t = cs_ref[...], sn_ref[...]
        q = q_ref[...].astype(F32)
        k = k_ref[...].astype(F32)
        qr_ref[...] = (q * cs_t + pltpu.roll(q, 64, 1) * sn_t).astype(BF16)
        kr_ref[...] = ((k * cs_t + pltpu.roll(k, 64, 1) * sn_t) * (RET_KEY_DIM ** -0.5)).astype(BF16)
        inside, q_dec, k_dec, tile_dec = _decay(lg)
        state = jnp.zeros((RET_KEY_DIM, RET_VAL_DIM), F32)
        for i in range(nt):
            rows = slice(i * RET_TILE, (i + 1) * RET_TILE)
            qi, ki, vi = qr_ref[rows, :], kr_ref[rows, :], v_ref[rows, :]
            acc = _dot((_dot_nt(qi, ki) * inside).astype(BF16), vi)
            if i > 0:
                acc = acc + _dot(_scaled(qi, q_dec), state.astype(BF16))
            if i < nt - 1:
                state = state * tile_dec + _dot_tn(_scaled(ki, k_dec), vi)
            o_ref[rows, :] = acc
            xc = acc - jnp.mean(acc, axis=-1, keepdims=True)
            nrm = xc * lax.rsqrt(jnp.mean(xc * xc, axis=-1, keepdims=True) + EPS)
            rg = rg_ref[rows, :].astype(F32)
            gro_ref[rows, :] = (rg * _sigmoid(rg) * nrm).astype(BF16)

    key, val, tab, lgs = _ret_specs(seq)
    return _call(
        body, name="ret_fwd", grid=(batch, RET_HEADS),
        in_specs=[key(C_RQ), key(C_RK), val(C_RV), val(C_RG), tab, tab, lgs],
        out_specs=[val(0), val(0), key(0), key(0)],
        out_shape=[jax.ShapeDtypeStruct((t, RET_HEADS * RET_VAL_DIM), BF16),
                   jax.ShapeDtypeStruct((t, RET_HEADS * RET_VAL_DIM), F32),
                   jax.ShapeDtypeStruct((t, RET_HEADS * RET_KEY_DIM), BF16),
                   jax.ShapeDtypeStruct((t, RET_HEADS * RET_KEY_DIM), BF16)],
        semantics=("parallel", "parallel"), args=(proj, proj, proj, proj, cs, sn, lg_arr), exchange=exchange)


def _att_bias(w_ref, bias_ref):
    n_i = lax.broadcasted_iota(jnp.int32, (ATT_Q, BIAS_LEN), 0)
    qc = lax.broadcasted_iota(jnp.int32, (ATT_Q, ATT_WIN), 0) >> 6
    kc = lax.broadcasted_iota(jnp.int32, (ATT_Q, ATT_WIN), 1) >> 6
    dc = qc + BAND_CHUNKS - kc
    band = (dc >= 0) & (dc <= BAND_CHUNKS)
    key = lax.broadcasted_iota(jnp.int32, (ATT_Q, ATT_WIN), 1)
    for e in range(2):
        xw = jnp.broadcast_to(w_ref[e:e + 1, :], (ATT_Q, BIAS_LEN))
        for bit in range(8):
            xw = jnp.where(((n_i >> bit) & 1) == 1, pltpu.roll(xw, 1 << bit, 1), xw)
        bias = jnp.where(band, xw[:, BIAS_LEN - ATT_WIN:], NEG_INF)
        for first in range(ATT_STARTS):
            bias_ref[first, e] = jnp.where(key + (first * ATT_Q - ATT_PAD) >= 0, bias, NEG_INF)
        bias_ref[ATT_STARTS, e] = bias


ATT_PAIRS = 2
ATT_COLS = ATT_PAIRS * LANES


def _att_specs(batch, seq):
    ni = seq // ATT_Q
    q_spec = pl.BlockSpec((ATT_Q, ATT_COLS), lambda g, b, i: (b * ni + i, C_AQ // ATT_COLS + g))
    k_spec = pl.BlockSpec((seq, ATT_COLS), lambda g, b, i: (b, C_AK // ATT_COLS + g))
    v_spec = pl.BlockSpec((seq, ATT_COLS), lambda g, b, i: (b, C_AV // ATT_COLS + g))
    w_spec = pl.BlockSpec((ATT_PAIRS, 2, BIAS_LEN), lambda g, b, i: (g, 0, 0))
    b_spec = pl.BlockSpec((ATT_PAIRS, ATT_STARTS + 1, 2, ATT_Q, ATT_WIN), lambda g, b, i: (g, 0, 0, 0, 0))
    pad = pltpu.VMEM((seq + ATT_PAD, ATT_COLS), BF16)
    return ni, q_spec, k_spec, v_spec, w_spec, b_spec, pad


def _att_bias_tiles(wvec, exchange):
    (tiles,), moved = _call(
        lambda w_ref, o_ref: _att_bias(w_ref, o_ref), name="att_bias", grid=(ATT_HEADS // 2,),
        in_specs=[pl.BlockSpec((None, 2, BIAS_LEN), lambda hp: (hp, 0, 0))],
        out_specs=[pl.BlockSpec((None, ATT_STARTS + 1, 2, ATT_Q, ATT_WIN), lambda hp: (hp, 0, 0, 0, 0))],
        out_shape=[jax.ShapeDtypeStruct((ATT_HEADS // 2, ATT_STARTS + 1, 2, ATT_Q, ATT_WIN), F32)],
        semantics=("parallel",), args=(wvec,), exchange=exchange)
    return tiles, moved


def _att_pad(src_ref, pad_ref):
    pad_ref[:ATT_PAD, :] = jnp.zeros((ATT_PAD, ATT_COLS), BF16)
    pad_ref[ATT_PAD:, :] = src_ref[...]


def _att_head(q2, sel):
    return jnp.where(sel, q2, jnp.zeros_like(q2)) * 0.125


def _att_softmax_rows(s_ref, bias_ref, rows):
    s = s_ref[rows, :] + bias_ref[rows, :]
    ex = jnp.exp(s - jnp.max(s, axis=-1, keepdims=True))
    return ex, 1.0 / jnp.sum(ex, axis=-1, keepdims=True)


def _att_fwd(proj, bias, batch, seq, exchange):
    ni, q_spec, k_spec, v_spec, _, b_spec, pad = _att_specs(batch, seq)

    def body(q_ref, k_ref, v_ref, bias_ref, o_ref, kp_ref, vp_ref, s_ref, e_ref):
        i = pl.program_id(2)

        @pl.when(i == 0)
        def _():
            _att_pad(k_ref, kp_ref)
            _att_pad(v_ref, vp_ref)

        win = pl.ds(pl.multiple_of(i * ATT_Q, ATT_Q), ATT_WIN)
        lo = lax.broadcasted_iota(jnp.int32, (1, LANES), 1) < 64
        start = jnp.minimum(i, ATT_STARTS)
        for pair in range(ATT_PAIRS):
            cols = slice(pair * LANES, (pair + 1) * LANES)
            k2, v2, q2 = kp_ref[win, cols], vp_ref[win, cols], q_ref[:, cols]
            out = jnp.zeros((ATT_Q, LANES), F32)
            for e in range(2):
                h = 2 * pair + e
                sel = lo if e == 0 else jnp.logical_not(lo)
                s_ref[h] = _dot_nt(_att_head(q2, sel), k2)
                rsum = []
                for c in range(ATT_Q // ATT_ROWS):
                    rows = slice(c * ATT_ROWS, (c + 1) * ATT_ROWS)
                    ex, r = _att_softmax_rows(s_ref.at[h], bias_ref.at[pair, start, e], rows)
                    e_ref[h, rows, :] = ex.astype(BF16)
                    rsum.append(r)
                out = out + _dot(e_ref[h], jnp.where(sel, v2, jnp.zeros_like(v2))) * jnp.concatenate(rsum, axis=0)
            o_ref[:, cols] = out.astype(BF16)

    heads = 2 * ATT_PAIRS
    return _call(
        body, name="att_fwd", grid=(ATT_HEADS // heads, batch, ni),
        in_specs=[q_spec, k_spec, v_spec, b_spec],
        out_specs=[pl.BlockSpec((ATT_Q, ATT_COLS), lambda g, b, i: (b * ni + i, g))],
        out_shape=[jax.ShapeDtypeStruct((batch * seq, ATT_HEADS * 64), BF16)],
        scratch=[pad, pad, pltpu.VMEM((heads, ATT_Q, ATT_WIN), F32), pltpu.VMEM((heads, ATT_Q, ATT_WIN), BF16)],
        semantics=("arbitrary", "arbitrary", "arbitrary"), args=(proj, proj, proj, bias), exchange=exchange)


GL_HALF = 512


def _gl_specs(tm):
    return [pl.BlockSpec((tm, GL_HALF), lambda i, c=C_GL // GL_HALF + j: (i, c)) for j in range(4)]


def _gates(gl_refs, b_ref):
    logits = [ref[...].astype(F32) for ref in gl_refs]
    gr = _sigmoid(jnp.concatenate(logits[:2], axis=1) + b_ref[:, :D_MODEL])
    ga = _sigmoid(jnp.concatenate(logits[2:], axis=1) + b_ref[:, D_MODEL:])
    return gr, ga


def _whole(a):
    return pl.BlockSpec(a.shape, lambda i: (0,) * a.ndim)


def _mix_out_fwd(gro, ao, proj, b_gate, w_ret, w_att_t, x2, w_out, g2):
    t = gro.shape[0]
    tm = min(512, t)

    def body(gro_ref, ao_ref, gl0, gl1, gl2, gl3, b_ref, wr_ref, wa_ref, x_ref, wo_ref, g_ref,
             z_ref, yr_ref, ya_ref, h_ref, hn_ref):
        yr = _dot(gro_ref[...], wr_ref[...])
        ya = _dot_nt(ao_ref[...], wa_ref[...])
        yr_ref[...] = yr.astype(BF16)
        ya_ref[...] = ya.astype(BF16)
        gr, ga = _gates((gl0, gl1, gl2, gl3), b_ref)
        z = (gr * yr + ga * ya).astype(BF16)
        z_ref[...] = z
        h = x_ref[...] + _dot(z, wo_ref[...])
        h_ref[...] = h
        r = lax.rsqrt(jnp.mean(h * h, axis=-1, keepdims=True) + EPS)
        hn_ref[...] = (h * r * g_ref[...]).astype(BF16)

    row = pl.BlockSpec((tm, D_MODEL), lambda i: (i, 0))
    return pl.pallas_call(
        body, name="mix_out_fwd", grid=(t // tm,),
        in_specs=[row, pl.BlockSpec((tm, 512), lambda i: (i, 0)), *_gl_specs(tm),
                  _whole(b_gate), _whole(w_ret), _whole(w_att_t), row, _whole(w_out), _whole(g2)],
        out_specs=[row] * 5,
        out_shape=[jax.ShapeDtypeStruct((t, D_MODEL), BF16)] * 3
                  + [jax.ShapeDtypeStruct((t, D_MODEL), F32), jax.ShapeDtypeStruct((t, D_MODEL), BF16)],
        compiler_params=_params("parallel"),
    )(gro, ao, proj, proj, proj, proj, b_gate, w_ret, w_att_t, x2, w_out, g2)


def _col_chunks(width, chunk=384):
    return [slice(lo, min(lo + chunk, width)) for lo in range(0, width, chunk)]


def _ffn_up(hn, wg_t, wu_t):
    t = hn.shape[0]
    tm, tn = min(512, t), D_FF // 2

    def body(h_ref, wg_ref, wu_ref, g_ref, u_ref, a_ref):
        g = _dot_nt(h_ref[...], wg_ref[...])
        u = _dot_nt(h_ref[...], wu_ref[...])
        g_ref[...] = g.astype(BF16)
        u_ref[...] = u.astype(BF16)
        a_ref[...] = (g * _sigmoid(g) * u).astype(BF16)

    w_spec = pl.BlockSpec((tn, D_MODEL), lambda j, i: (j, 0))
    out = pl.BlockSpec((tm, tn), lambda j, i: (i, j))
    return pl.pallas_call(
        body, name="ffn_up", grid=(D_FF // tn, t // tm),
        in_specs=[pl.BlockSpec((tm, D_MODEL), lambda j, i: (i, 0)), w_spec, w_spec],
        out_specs=[out, out, out],
        out_shape=[jax.ShapeDtypeStruct((t, D_FF), BF16)] * 3,
        compiler_params=_params("parallel", "parallel"),
    )(hn, wg_t, wu_t)


def _ffn_down_loss(a, h1, tgt, w_down, g3):
    t = a.shape[0]
    tm = min(512, t)

    def body(a_ref, h_ref, t_ref, w_ref, g_ref, dh_ref, dhb_ref, loss_ref, dg_ref):
        @pl.when(pl.program_id(0) == 0)
        def _():
            loss_ref[...] = jnp.zeros_like(loss_ref)
            dg_ref[...] = jnp.zeros_like(dg_ref)

        g = g_ref[...]
        h2 = h_ref[...] + _dot(a_ref[...], w_ref[...])
        r = lax.rsqrt(jnp.mean(h2 * h2, axis=-1, keepdims=True) + EPS)
        err = h2 * r * g - t_ref[...]
        loss_ref[...] += jnp.sum(err * err) * (0.5 / D_MODEL)
        dy = err * (1.0 / D_MODEL)
        dh, dg_rows = _rms_bwd(h2, g, dy)
        dg_ref[...] += jnp.sum(dg_rows, axis=0, keepdims=True)
        dh_ref[...] = dh
        dhb_ref[...] = dh.astype(BF16)

    row = pl.BlockSpec((tm, D_MODEL), lambda i: (i, 0))
    vec = pl.BlockSpec((1, D_MODEL), lambda i: (0, 0))
    return pl.pallas_call(
        body, name="ffn_down_loss", grid=(t // tm,),
        in_specs=[pl.BlockSpec((tm, D_FF), lambda i: (i, 0)), row, row,
                  pl.BlockSpec((D_FF, D_MODEL), lambda i: (0, 0)), vec],
        out_specs=[row, row, pl.BlockSpec((1, LANES), lambda i: (0, 0)), vec],
        out_shape=[jax.ShapeDtypeStruct((t, D_MODEL), F32), jax.ShapeDtypeStruct((t, D_MODEL), BF16),
                   jax.ShapeDtypeStruct((1, LANES), F32), jax.ShapeDtypeStruct((1, D_MODEL), F32)],
        compiler_params=_params("arbitrary"),
    )(a, h1, tgt, w_down, g3)


def _ffn_bwd_act(dh2b, w_down, g_act, u_act, exchange):
    t = dh2b.shape[0]
    tm, tn = min(1024, t), D_FF // 2

    def body(d_ref, w_ref, g_ref, u_ref, dg_ref, du_ref):
        d = d_ref[...]
        for cols in _col_chunks(tn):
            da = _dot_nt(d, w_ref[cols, :])
            g = g_ref[:, cols].astype(F32)
            u = u_ref[:, cols].astype(F32)
            sg = _sigmoid(g)
            dg_ref[:, cols] = (da * u * sg * (1.0 + g * (1.0 - sg))).astype(BF16)
            du_ref[:, cols] = (da * g * sg).astype(BF16)

    blk = pl.BlockSpec((tm, tn), lambda j, i: (i, j))
    return _call(
        body, name="ffn_bwd_act", grid=(D_FF // tn, t // tm),
        in_specs=[pl.BlockSpec((tm, D_MODEL), lambda j, i: (i, 0)),
                  pl.BlockSpec((tn, D_MODEL), lambda j, i: (j, 0)), blk, blk],
        out_specs=[blk, blk],
        out_shape=[jax.ShapeDtypeStruct((t, D_FF), BF16)] * 2,
        semantics=("parallel", "parallel"), args=(dh2b, w_down, g_act, u_act), exchange=exchange)


def _ffn_bwd_in(dg, du, wg_t, wu_t, h1, dh2, g2, exchange):
    t = dg.shape[0]
    tm = min(256, t)

    def body(dg_ref, du_ref, wg_ref, wu_ref, h_ref, d2_ref, g_ref, dh_ref, dhb_ref, gn_ref):
        @pl.when(pl.program_id(0) == 0)
        def _():
            gn_ref[...] = jnp.zeros_like(gn_ref)

        dhn = _dot(dg_ref[...], wg_ref[...]) + _dot(du_ref[...], wu_ref[...])
        dx, dg_rows = _rms_bwd(h_ref[...], g_ref[...], dhn)
        dh = d2_ref[...] + dx
        dh_ref[...] = dh
        dhb_ref[...] = dh.astype(BF16)
        gn_ref[...] += jnp.sum(dg_rows, axis=0, keepdims=True)

    act = pl.BlockSpec((tm, D_FF), lambda i: (i, 0))
    row = pl.BlockSpec((tm, D_MODEL), lambda i: (i, 0))
    return _call(
        body, name="ffn_bwd_in", grid=(t // tm,),
        in_specs=[act, act, _whole(wg_t), _whole(wu_t), row, row, _whole(g2)],
        out_specs=[row, row, _whole(g2)],
        out_shape=[jax.ShapeDtypeStruct((t, D_MODEL), F32), jax.ShapeDtypeStruct((t, D_MODEL), BF16),
                   jax.ShapeDtypeStruct((1, D_MODEL), F32)],
        semantics=("arbitrary",), args=(dg, du, wg_t, wu_t, h1, dh2, g2), exchange=exchange)


def _mix_bwd(dh1b, w_out, proj, b_gate, y_ret, y_att, w_ret, w_att_t):
    t = dh1b.shape[0]
    tm = min(512, t)

    def body(d_ref, wo_ref, gl0, gl1, gl2, gl3, b_ref, yr_ref, ya_ref, wr_ref, wa_ref,
             dyr_ref, dya_ref, dglr_ref, dgla_ref, dgro_ref, dao_ref, db_ref):
        @pl.when(pl.program_id(0) == 0)
        def _():
            db_ref[...] = jnp.zeros_like(db_ref)

        dz = _dot_nt(d_ref[...], wo_ref[...])
        gr, ga = _gates((gl0, gl1, gl2, gl3), b_ref)
        dyr = (dz * gr).astype(BF16)
        dya = (dz * ga).astype(BF16)
        dyr_ref[...] = dyr
        dya_ref[...] = dya
        dglr = dz * yr_ref[...].astype(F32) * gr * (1.0 - gr)
        dgla = dz * ya_ref[...].astype(F32) * ga * (1.0 - ga)
        dglr_ref[...] = dglr.astype(BF16)
        dgla_ref[...] = dgla.astype(BF16)
        db_ref[:, :D_MODEL] += jnp.sum(dglr, axis=0, keepdims=True)
        db_ref[:, D_MODEL:] += jnp.sum(dgla, axis=0, keepdims=True)
        dgro_ref[...] = _dot_nt(dyr, wr_ref[...]).astype(BF16)
        dao_ref[...] = _dot(dya, wa_ref[...]).astype(BF16)

    row = pl.BlockSpec((tm, D_MODEL), lambda i: (i, 0))
    half = pl.BlockSpec((tm, 512), lambda i: (i, 0))
    return pl.pallas_call(
        body, name="mix_bwd", grid=(t // tm,),
        in_specs=[row, _whole(w_out), *_gl_specs(tm), _whole(b_gate), row, row, _whole(w_ret), _whole(w_att_t)],
        out_specs=[row, row, row, row, row, half, _whole(b_gate)],
        out_shape=[jax.ShapeDtypeStruct((t, D_MODEL), BF16)] * 5
                  + [jax.ShapeDtypeStruct((t, 512), BF16), jax.ShapeDtypeStruct(b_gate.shape, F32)],
        compiler_params=_params("arbitrary"),
    )(dh1b, w_out, proj, proj, proj, proj, b_gate, y_ret, y_att, w_ret, w_att_t)


def _ret_bwd(dgro, proj, o_ret, qr, kr, cs, sn, lg_arr, batch, seq, exchange):
    t = batch * seq
    nt = seq // RET_TILE

    def body(dgro_ref, rg_ref, o_ref, qr_ref, kr_ref, v_ref, cs_ref, sn_ref, lg_ref,
             dq_ref, dk_ref, dv_ref, drg_ref, do_ref, st_ref):
        lg = lg_ref[:, 0:1]
        inside, q_dec, k_dec, tile_dec = _decay(lg)

        state = jnp.zeros((RET_KEY_DIM, RET_VAL_DIM), F32)
        for i in range(nt - 1):
            rows = slice(i * RET_TILE, (i + 1) * RET_TILE)
            state = state * tile_dec + _dot_tn(_scaled(kr_ref[rows, :], k_dec), v_ref[rows, :])
            st_ref[i + 1] = state.astype(BF16)

        for i in range(nt):
            rows = slice(i * RET_TILE, (i + 1) * RET_TILE)
            o = o_ref[rows, :]
            xc = o - jnp.mean(o, axis=-1, keepdims=True)
            rs = lax.rsqrt(jnp.mean(xc * xc, axis=-1, keepdims=True) + EPS)
            nrm = xc * rs
            rg = rg_ref[rows, :].astype(F32)
            sg = _sigmoid(rg)
            dg = dgro_ref[rows, :].astype(F32)
            drg_ref[rows, :] = (dg * nrm * sg * (1.0 + rg * (1.0 - sg))).astype(BF16)
            dn = dg * rg * sg
            do = rs * (dn - jnp.mean(dn, axis=-1, keepdims=True)
                       - nrm * jnp.mean(dn * nrm, axis=-1, keepdims=True))
            do_ref[rows, :] = do.astype(BF16)

        dstate = jnp.zeros((RET_KEY_DIM, RET_VAL_DIM), F32)
        for i in reversed(range(nt)):
            rows = slice(i * RET_TILE, (i + 1) * RET_TILE)
            qi, ki, vi, doi = qr_ref[rows, :], kr_ref[rows, :], v_ref[rows, :], do_ref[rows, :]
            p = (_dot_nt(qi, ki) * inside).astype(BF16)
            dp = (_dot_nt(doi, vi) * inside).astype(BF16)
            dq = _dot(dp, ki)
            dk = _dot_tn(dp, qi)
            dv = _dot_tn(p, doi)
            if i > 0:
                dq = dq + _dot_nt(doi, st_ref[i]) * q_dec
            if i < nt - 1:
                dsb = dstate.astype(BF16)
                dk = dk + _dot_nt(vi, dsb) * k_dec
                dv = dv + _dot(_scaled(ki, k_dec), dsb)
            if i > 0:
                dstate = dstate * tile_dec + _dot_tn(_scaled(qi, q_dec), doi)
            dq_ref[rows, :] = (dq * cs_ref[rows, :] - pltpu.roll(dq, 64, 1) * sn_ref[rows, :]).astype(BF16)
            dk = (dk * cs_ref[rows, :] - pltpu.roll(dk, 64, 1) * sn_ref[rows, :]) * (RET_KEY_DIM ** -0.5)
            dk_ref[rows, :] = dk.astype(BF16)
            dv_ref[rows, :] = dv.astype(BF16)

    key, val, tab, lgs = _ret_specs(seq)
    return _call(
        body, name="ret_bwd", grid=(batch, RET_HEADS),
        in_specs=[val(0), val(C_RG), val(0), key(0), key(0), val(C_RV), tab, tab, lgs],
        out_specs=[key(0), key(0), val(0), val(0)],
        out_shape=[jax.ShapeDtypeStruct((t, RET_HEADS * RET_KEY_DIM), BF16)] * 2
                  + [jax.ShapeDtypeStruct((t, RET_HEADS * RET_VAL_DIM), BF16)] * 2,
        scratch=[pltpu.VMEM((seq, RET_VAL_DIM), BF16), pltpu.VMEM((nt, RET_KEY_DIM, RET_VAL_DIM), BF16)],
        semantics=("parallel", "parallel"), args=(dgro, proj, o_ret, qr, kr, proj, cs, sn, lg_arr),
        exchange=exchange)


def _att_bwd(proj, bias, dao, batch, seq, exchange):
    ni, q_spec, k_spec, v_spec, w_spec, b_spec, pad = _att_specs(batch, seq)
    t = batch * seq

    def body(q_ref, k_ref, v_ref, bias_ref, do_ref, dq_ref, dk_ref, dv_ref, dw_ref,
             dbias_ref, dk_acc, dv_acc, kp_ref, vp_ref, s_ref, dp_ref, e_ref, ds_ref):
        b, i = pl.program_id(1), pl.program_id(2)

        @pl.when((b == 0) & (i == 0))
        def _():
            dbias_ref[...] = jnp.zeros_like(dbias_ref)

        @pl.when(i == 0)
        def _():
            _att_pad(k_ref, kp_ref)
            _att_pad(v_ref, vp_ref)
            dk_acc[...] = jnp.zeros_like(dk_acc)
            dv_acc[...] = jnp.zeros_like(dv_acc)

        win = pl.ds(pl.multiple_of(i * ATT_Q, ATT_Q), ATT_WIN)
        lo = lax.broadcasted_iota(jnp.int32, (1, LANES), 1) < 64
        start = jnp.minimum(i, ATT_STARTS)
        for pair in range(ATT_PAIRS):
            cols = slice(pair * LANES, (pair + 1) * LANES)
            k2, v2, q2, do2 = kp_ref[win, cols], vp_ref[win, cols], q_ref[:, cols], do_ref[:, cols]
            dq = jnp.zeros((ATT_Q, LANES), F32)
            dk = jnp.zeros((LANES, ATT_WIN), F32)
            dv = jnp.zeros((LANES, ATT_WIN), F32)
            for e in range(2):
                h = 2 * pair + e
                sel = lo if e == 0 else jnp.logical_not(lo)
                qm = _att_head(q2, sel)
                dom = jnp.where(sel, do2, jnp.zeros_like(do2))
                s_ref[h] = _dot_nt(qm, k2)
                dp_ref[h] = _dot_nt(dom, v2)
                rsum = []
                for c in range(ATT_Q // ATT_ROWS):
                    rows = slice(c * ATT_ROWS, (c + 1) * ATT_ROWS)
                    ex, r = _att_softmax_rows(s_ref.at[h], bias_ref.at[pair, start, e], rows)
                    dp = dp_ref[h, rows, :]
                    mean = jnp.sum(dp * ex, axis=-1, keepdims=True) * r
                    ds = ex * ((dp - mean) * r)
                    dbias_ref[h, rows, :] += ds
                    ds_ref[h, rows, :] = ds.astype(BF16)
                    e_ref[h, rows, :] = ex.astype(BF16)
                    rsum.append(r)
                dq = dq + _dot(ds_ref[h], jnp.where(sel, k2, jnp.zeros_like(k2)))
                dk = dk + _dot_tn(qm, ds_ref[h])
                dv = dv + _dot_tn((dom.astype(F32) * jnp.concatenate(rsum, axis=0)).astype(BF16), e_ref[h])
            dq_ref[:, cols] = (dq * 0.125).astype(BF16)
            dk_acc[cols, win] += dk
            dv_acc[cols, win] += dv

        @pl.when(i == ni - 1)
        def _():
            dk_ref[...] = dk_acc[:, ATT_PAD:].T.astype(BF16)
            dv_ref[...] = dv_acc[:, ATT_PAD:].T.astype(BF16)

        @pl.when((b == batch - 1) & (i == ni - 1))
        def _():
            n_i = lax.broadcasted_iota(jnp.int32, (ATT_Q, BIAS_LEN), 0)
            for h in range(heads):
                xw = jnp.concatenate([jnp.zeros((ATT_Q, BIAS_LEN - ATT_WIN), F32), dbias_ref[h]], axis=1)
                for bit in range(8):
                    xw = jnp.where(((n_i >> bit) & 1) == 1, pltpu.roll(xw, BIAS_LEN - (1 << bit), 1), xw)
                dw_ref[h // 2, h % 2:h % 2 + 1, :] = jnp.sum(xw, axis=0, keepdims=True)

    heads = 2 * ATT_PAIRS
    seq_blk = pl.BlockSpec((seq, ATT_COLS), lambda g, b, i: (b, g))
    q_out = pl.BlockSpec((ATT_Q, ATT_COLS), lambda g, b, i: (b * ni + i, g))
    tile_f32, tile_bf16 = pltpu.VMEM((heads, ATT_Q, ATT_WIN), F32), pltpu.VMEM((heads, ATT_Q, ATT_WIN), BF16)
    acc = pltpu.VMEM((ATT_COLS, seq + ATT_PAD), F32)
    return _call(
        body, name="att_bwd", grid=(ATT_HEADS // heads, batch, ni),
        in_specs=[q_spec, k_spec, v_spec, b_spec, q_out],
        out_specs=[q_out, seq_blk, seq_blk, w_spec],
        out_shape=[jax.ShapeDtypeStruct((t, 512), BF16)] * 3
                  + [jax.ShapeDtypeStruct((ATT_HEADS // 2, 2, BIAS_LEN), F32)],
        scratch=[tile_f32, acc, acc, pad, pad, tile_f32, tile_f32, tile_bf16, tile_bf16],
        semantics=("arbitrary", "arbitrary", "arbitrary"), args=(proj, proj, proj, bias, dao), exchange=exchange,
        vmem=VMEM_LIMIT_ATT_BWD)


def _rms_in_bwd(x2, dxn, dh1, g1):
    t = x2.shape[0]
    tm = min(1024, t)

    def body(x_ref, d_ref, h_ref, g_ref, dx_ref, dg_ref):
        @pl.when(pl.program_id(0) == 0)
        def _():
            dg_ref[...] = jnp.zeros_like(dg_ref)

        dx, dg_rows = _rms_bwd(x_ref[...], g_ref[...], d_ref[...])
        dx_ref[...] = h_ref[...] + dx
        dg_ref[...] += jnp.sum(dg_rows, axis=0, keepdims=True)

    row = pl.BlockSpec((tm, D_MODEL), lambda i: (i, 0))
    vec = pl.BlockSpec((1, D_MODEL), lambda i: (0, 0))
    return pl.pallas_call(
        body, name="rms_in_bwd", grid=(t // tm,),
        in_specs=[row, row, row, vec], out_specs=[row, vec],
        out_shape=[jax.ShapeDtypeStruct((t, D_MODEL), F32), jax.ShapeDtypeStruct((1, D_MODEL), F32)],
        compiler_params=_params("arbitrary"),
    )(x2, dxn, dh1, g1)


def _pack_small(dg1, dbr, dba, dg2, dg3, dw, loss):
    def body(a_ref, b_ref, c_ref, d_ref, e_ref, w_ref, l_ref, o_ref):
        o_ref[...] = jnp.zeros_like(o_ref)
        for r, ref in enumerate((a_ref, b_ref, c_ref, d_ref, e_ref)):
            o_ref[r:r + 1, :] = ref[...]
        o_ref[5:6, 0:LANES] = l_ref[...]
        for hp in range(ATT_HEADS // 2):
            o_ref[8 + 2 * hp:10 + 2 * hp, :] = w_ref[hp]

    return pl.pallas_call(body, name="pack_small",
                          out_shape=jax.ShapeDtypeStruct((16, D_MODEL), F32))(dg1, dbr, dba, dg2, dg3, dw, loss)


def _rotary_tables(seq):
    freqs = ROPE_BASE ** (-jnp.arange(0, RET_KEY_DIM, 2, dtype=F32) / RET_KEY_DIM)
    ang = jnp.arange(seq, dtype=F32)[:, None] * freqs[None, :]
    cos, sin = jnp.cos(ang), jnp.sin(ang)
    return jnp.concatenate([cos, cos], axis=1), jnp.concatenate([-sin, sin], axis=1)


def _bias_rows(rel_bias):
    n_far = BIAS_LEN - ATT_Q - MAX_REL + 1
    n_near = BIAS_LEN - n_far - (N_REL - 2)
    w = jnp.concatenate([jnp.broadcast_to(rel_bias[:, N_REL - 1:], (ATT_HEADS, n_far)),
                         rel_bias[:, 1:N_REL - 1][:, ::-1],
                         jnp.broadcast_to(rel_bias[:, :1], (ATT_HEADS, n_near))], axis=1)
    return w.reshape(ATT_HEADS // 2, 2, BIAS_LEN)


def _bias_rows_bwd(dw):
    n_far = BIAS_LEN - ATT_Q - MAX_REL + 1
    mid = dw[:, n_far:n_far + N_REL - 2][:, ::-1]
    return jnp.concatenate([jnp.sum(dw[:, n_far + N_REL - 2:], axis=1, keepdims=True), mid,
                            jnp.sum(dw[:, :n_far], axis=1, keepdims=True)], axis=1)


def _step(x, tgt, norm_mix, b_gate, norm_ffn, norm_final, rel_bias_shard, shard):
    batch, seq, _ = x.shape
    t = batch * seq
    n_rb = rel_bias_shard.shape[-1]
    x2, tgt2 = x.reshape(t, D_MODEL), tgt.reshape(t, D_MODEL)
    g3 = norm_final.reshape(1, D_MODEL)
    cs, sn = _rotary_tables(seq)
    lg = np.log(1.0 - 2.0 ** (-5.0 - np.arange(RET_HEADS, dtype=np.float32))).astype(np.float32)
    lg_arr = jnp.asarray(np.broadcast_to(lg[:, None, None], (RET_HEADS, 1, LANES)))

    def gather(*names):
        return _ChipGather([shard[nm] for nm in names])

    def scatter(*grads):
        return _Scatter(grads)

    rb_pad = jnp.pad(rel_bias_shard, ((0, 0), (0, LANES - n_rb)))
    (xn,), (w_in_half, rb_full) = _rms_fwd(x2, norm_mix,
                                           _ChipGather([shard["w_in_t"], rb_pad], parts=[(0, 2), (0, 1)]))
    rb_full = rb_full.reshape(N_DEV, ATT_HEADS, LANES)[:, :, :n_rb]
    bias, (w_in_t,) = _att_bias_tiles(
        _bias_rows(jnp.transpose(rb_full, (1, 0, 2)).reshape(ATT_HEADS, N_DEV * n_rb)),
        _ChipGather([shard["w_in_t"]], parts=[(1, 2)], into=[w_in_half]))
    proj, (w_ret, w_att_t, w_out, w_gate_t) = _mm(
        xn, w_in_t, tb=True, out_dtype=BF16, tm=1024, tn=1664, tk=1024, name="proj",
        exchange=gather("w_ret", "w_att_t", "w_out", "w_gate_t"))
    (gro, o_ret, qr, kr), _ = _ret_fwd(proj, cs, sn, lg_arr, batch, seq, None)
    (ao,), (w_up_t, w_down) = _att_fwd(proj, bias, batch, seq, gather("w_up_t", "w_down"))
    z, y_ret, y_att, h1, hn = _mix_out_fwd(gro, ao, proj, b_gate, w_ret, w_att_t, x2, w_out, norm_ffn)
    g_act, u_act, a_act = _ffn_up(hn, w_gate_t, w_up_t)
    dh2, dh2b, loss, dg3 = _ffn_down_loss(a_act, h1, tgt2, w_down, g3)

    wg = dict(out_dtype=BF16, tn=1024, ta=True)
    slots = {}
    dw_down = _mm(a_act, dh2b, tm=1408, tk=2048, name="dw_down", **wg)
    (d_gact, d_uact), _ = _ffn_bwd_act(dh2b, w_down, g_act, u_act, None)
    dw_gate = _mm(d_gact, hn, tm=1408, tk=2048, name="dw_gate", **wg)
    dw_up = _mm(d_uact, hn, tm=1408, tk=2048, name="dw_up", **wg)
    (dh1, dh1b, dg2), (slots["w_down"],) = _ffn_bwd_in(d_gact, d_uact, w_gate_t, w_up_t, h1, dh2, norm_ffn,
                                                     scatter(dw_down))
    dw_out = _mm(z, dh1b, tm=1024, tk=2048, name="dw_out", **wg)
    dyr, dya, dglr, dgla, dgro, dao, db = _mix_bwd(dh1b, w_out, proj, b_gate, y_ret, y_att, w_ret, w_att_t)
    dw_ret = _mm(gro, dyr, tm=1024, tk=2048, name="dw_ret", **wg)
    dw_att = _mm(dya, ao, tm=1024, tk=2048, name="dw_att", **wg)
    (drq, drk, drv, drg), _ = _ret_bwd(dgro, proj, o_ret, qr, kr, cs, sn, lg_arr, batch, seq, None)
    (daq, dak, dav, dw), (slots["w_gate_t"], slots["w_out"], slots["w_ret"], slots["w_att_t"]) = _att_bwd(
        proj, bias, dao, batch, seq, scatter(dw_gate, dw_out, dw_ret, dw_att))
    dproj = [drq, drk, drv, drg, daq, dak, dav, dglr, dgla]
    dw_in, (slots["w_up_t"],) = _mm_pieces(dproj, xn, ta=True, out_dtype=BF16, tm=512, tn=1024, tk=1024,
                                           name="dw_in", exchange=scatter(dw_up))
    (dw_in_sibling,) = _alone(_PairSwap([dw_in]), "swap_w_in")
    dw_in_pairs = _pair_add(dw_in, dw_in_sibling, "pair_w_in")
    dxn, (slots["w_in_t"],) = _mm_pieces(dproj, w_in_t, ta=False, out_dtype=F32, tm=1024, tn=1024, tk=512, name="dxn",
                                  exchange=_ChipScatter([dw_in_pairs]))
    dx, dg1 = _rms_in_bwd(x2, dxn, dh1, norm_mix)
    small = _pack_small(dg1, db[:, :D_MODEL], db[:, D_MODEL:], dg2, dg3, dw, loss)
    return dx.reshape(batch, seq, D_MODEL), slots, _gather_sum(small, "gather_sum_small")


def _row_tile(r, c):
    return max(d for d in range(16, r + 1, 16) if r % d == 0 and (d * c <= 256 * 1024 or d == 16))


def _pair_add(grad, got, name):
    _, r, c = got.shape
    tr = r
    core = lax.axis_index("c").astype(jnp.int32).reshape(1)

    def body(core_ref, g_ref, a_ref, o_ref):
        o_ref[...] = (g_ref[...].astype(F32) + a_ref[...].astype(F32)).astype(o_ref.dtype)

    blk = pl.BlockSpec((None, tr, c), lambda q, i, core_ref: (q, i, 0))
    return pl.pallas_call(
        body, name=name,
        grid_spec=pltpu.PrefetchScalarGridSpec(
            num_scalar_prefetch=1, grid=(4, r // tr),
            in_specs=[pl.BlockSpec((None, None, tr, c), lambda q, i, core_ref: (q, core_ref[0], i, 0)), blk],
            out_specs=blk),
        out_shape=jax.ShapeDtypeStruct(got.shape, got.dtype),
        compiler_params=_params("parallel", "parallel"),
    )(core, grad.reshape(4, 2, r, c), got)


def _sum_slots(slots, name):
    n, r, c = slots.shape
    tr = _row_tile(r, c)

    def body(s_ref, o_ref):
        acc = s_ref[0].astype(F32)
        for s in range(1, n):
            acc = acc + s_ref[s].astype(F32)
        o_ref[...] = acc

    return pl.pallas_call(
        body, name=name, grid=(r // tr,),
        in_specs=[pl.BlockSpec((n, tr, c), lambda i: (0, i, 0))],
        out_specs=pl.BlockSpec((tr, c), lambda i: (i, 0)),
        out_shape=jax.ShapeDtypeStruct((r, c), F32),
        compiler_params=_params("parallel"),
    )(slots)


def _adamw_math(w, g, m, v):
    m = ADAM_B1 * m + (1.0 - ADAM_B1) * g
    v = ADAM_B2 * v + (1.0 - ADAM_B2) * (g * g)
    m_hat = m / (1.0 - ADAM_B1 ** ADAM_STEP)
    v_hat = v / (1.0 - ADAM_B2 ** ADAM_STEP)
    return -ADAM_LR * (m_hat / (jnp.sqrt(v_hat) + ADAM_EPS) + ADAM_WD * w), m, v


def _adamw(w, slots, m, v, name):
    n, r, c = slots.shape
    tr = _row_tile(r, c)

    def body(w_ref, s_ref, m_ref, v_ref, g_ref, d_ref, nm_ref, nv_ref):
        g = s_ref[0].astype(F32)
        for s in range(1, n):
            g = g + s_ref[s].astype(F32)
        g_ref[...] = g
        d_ref[...], nm_ref[...], nv_ref[...] = _adamw_math(w_ref[...], g, m_ref[...], v_ref[...])

    blk = pl.BlockSpec((tr, c), lambda i: (i, 0))
    return pl.pallas_call(
        body, name=name, grid=(r // tr,),
        in_specs=[blk, pl.BlockSpec((n, tr, c), lambda i: (0, i, 0)), blk, blk], out_specs=[blk] * 4,
        out_shape=[jax.ShapeDtypeStruct((r, c), F32)] * 4,
        compiler_params=_params("parallel"),
    )(w, slots, m, v)


def _adamw_small(ws, gs, ms, vs):
    n = len(ws)

    def body(*refs):
        for i in range(n):
            w_ref, g_ref, m_ref, v_ref = (refs[j * n + i] for j in range(4))
            d_ref, nm_ref, nv_ref = (refs[(4 + j) * n + i] for j in range(3))
            d_ref[...], nm_ref[...], nv_ref[...] = _adamw_math(w_ref[...], g_ref[...], m_ref[...], v_ref[...])

    shapes = [jax.ShapeDtypeStruct(w.shape, F32) for w in ws]
    outs = pl.pallas_call(body, name="adamw_small", out_shape=shapes * 3)(*ws, *gs, *ms, *vs)
    return outs[:n], outs[n:2 * n], outs[2 * n:]


def kernel(x, norm_mix, w_in, b_gate, rel_bias, w_ret_out, w_att_out, w_out, norm_ffn, w_ffn_gate, w_ffn_up, w_ffn_down, norm_final, loss_target, m_norm_mix, m_w_in, m_b_gate, m_rel_bias, m_w_ret_out, m_w_att_out, m_w_out, m_norm_ffn, m_w_ffn_gate, m_w_ffn_up, m_w_ffn_down, m_norm_final, v_norm_mix, v_w_in, v_b_gate, v_rel_bias, v_w_ret_out, v_w_att_out, v_w_out, v_norm_ffn, v_w_ffn_gate, v_w_ffn_up, v_w_ffn_down, v_norm_final):
    me = _index(_place())
    n_rb = rel_bias.shape[-1]

    shard = dict(w_in_t=w_in[0].T, w_gate_t=w_ffn_gate[0].T, w_up_t=w_ffn_up[0].T, w_down=w_ffn_down[0],
                 w_ret=w_ret_out[0], w_out=w_out[0], w_att_t=w_att_out[0].T)
    shard = {nm: s.astype(BF16) for nm, s in shard.items()}
    dx, slots, small_sum = _step(x, loss_target, norm_mix, b_gate, norm_ffn, norm_final, rel_bias[0], shard)
    loss = small_sum[5, 0]

    transposed = dict(w_in="w_in_t", w_ffn_gate="w_gate_t", w_ffn_up="w_up_t", w_att_out="w_att_t")
    plain = dict(w_ffn_down="w_down", w_ret_out="w_ret", w_out="w_out")
    g = dict(
        norm_mix=small_sum[0:1], b_gate=jnp.concatenate([small_sum[1:2], small_sum[2:3]], axis=1),
        norm_ffn=small_sum[3:4], norm_final=small_sum[4:5],
        rel_bias=lax.dynamic_slice_in_dim(_bias_rows_bwd(small_sum[8:16]), me * n_rb, n_rb, axis=1),
    )
    w = dict(norm_mix=norm_mix, w_in=w_in, b_gate=b_gate, rel_bias=rel_bias, w_ret_out=w_ret_out, w_att_out=w_att_out,
             w_out=w_out, norm_ffn=norm_ffn, w_ffn_gate=w_ffn_gate, w_ffn_up=w_ffn_up, w_ffn_down=w_ffn_down,
             norm_final=norm_final)
    m = dict(norm_mix=m_norm_mix, w_in=m_w_in, b_gate=m_b_gate, rel_bias=m_rel_bias, w_ret_out=m_w_ret_out,
             w_att_out=m_w_att_out, w_out=m_w_out, norm_ffn=m_norm_ffn, w_ffn_gate=m_w_ffn_gate, w_ffn_up=m_w_ffn_up,
             w_ffn_down=m_w_ffn_down, norm_final=m_norm_final)
    v = dict(norm_mix=v_norm_mix, w_in=v_w_in, b_gate=v_b_gate, rel_bias=v_rel_bias, w_ret_out=v_w_ret_out,
             w_att_out=v_w_att_out, w_out=v_w_out, norm_ffn=v_norm_ffn, w_ffn_gate=v_w_ffn_gate, w_ffn_up=v_w_ffn_up,
             w_ffn_down=v_w_ffn_down, norm_final=v_norm_final)
    order = ("norm_mix", "w_in", "b_gate", "rel_bias", "w_ret_out", "w_att_out", "w_out", "norm_ffn",
             "w_ffn_gate", "w_ffn_up", "w_ffn_down", "norm_final")
    small_names = ("norm_mix", "b_gate", "rel_bias", "norm_ffn", "norm_final")

    def flat(a):
        return a[0] if a.ndim == 3 else a.reshape(-1, a.shape[-1])

    grad, delta, new_m, new_v = {}, {}, {}, {}
    for nm in order:
        if nm in transposed:
            res = _adamw(w[nm][0].T, slots[transposed[nm]], m[nm][0].T, v[nm][0].T, "adamw_" + nm)
            grad[nm], delta[nm], new_m[nm], new_v[nm] = (a.T[None] for a in res)
        elif nm in plain:
            res = _adamw(flat(w[nm]), slots[plain[nm]], flat(m[nm]), flat(v[nm]), "adamw_" + nm)
            grad[nm], delta[nm], new_m[nm], new_v[nm] = (a.reshape(w[nm].shape) for a in res)
    ds, nms, nvs = _adamw_small([flat(w[nm]) for nm in small_names], [g[nm] for nm in small_names],
                                [flat(m[nm]) for nm in small_names], [flat(v[nm]) for nm in small_names])
    for i, nm in enumerate(small_names):
        grad[nm], delta[nm], new_m[nm], new_v[nm] = (a.reshape(w[nm].shape) for a in (g[nm], ds[i], nms[i], nvs[i]))

    return (loss, dx, *[grad[nm] for nm in order], *[delta[nm] for nm in order],
            *[new_m[nm] for nm in order], *[new_v[nm] for nm in order])
```
